```python
import jax, jax.numpy as jnp
from jax import lax
import numpy as np

D_MODEL = 1024
BATCH = 16
SEQ = 2048
DEPTH = 1

CTX_LEN = 256
GRID_W = 64
RET_HEADS = 4
RET_DIM = 128
RET_WIDTH = RET_HEADS * RET_DIM
RET_CHUNK = 128
NA_HEADS = 8
NA_DIM = 64
NA_WIDTH = NA_HEADS * NA_DIM
NA_KH = 8
NA_KW = 16
MIX_WIDTH = RET_WIDTH + NA_WIDTH
IN_SPLITS = (RET_WIDTH, RET_WIDTH, RET_WIDTH, RET_WIDTH, NA_WIDTH, NA_WIDTH, NA_WIDTH)
IN_WIDTH = 4 * RET_WIDTH + 3 * NA_WIDTH
D_FF = 4 * D_MODEL
ROPE_BASE = 10000.0
NORM_EPS = 1e-6
N_MOD = 6
NEG_INF = -1e30

kernel_name = 'hybrid_retention_natten_dit_block'


def rmsnorm(x, g):
    xf = x.astype(jnp.float32)
    y = xf * lax.rsqrt(jnp.mean(xf * xf, axis=-1, keepdims=True) + NORM_EPS)
    return (y * g.astype(jnp.float32)).astype(x.dtype)


def modulations(cvec, w_ada, b_ada):
    return jnp.split(jax.nn.silu(cvec) @ w_ada + b_ada, N_MOD, axis=-1)


def to_heads(t, n_heads):
    b, l, _ = t.shape
    return t.reshape(b, l, n_heads, -1).transpose(0, 2, 1, 3)


def axial_rope(x, pos_r, pos_c):
    d_axis = x.shape[-1] // 2
    n_freq = d_axis // 2
    inv = ROPE_BASE ** (-jnp.arange(n_freq, dtype=jnp.float32) / n_freq)

    def rot(seg, pos):
        ang = pos[:, None] * inv[None, :]
        cos, sin = jnp.cos(ang), jnp.sin(ang)
        s1 = seg[..., :n_freq].astype(jnp.float32)
        s2 = seg[..., n_freq:].astype(jnp.float32)
        return jnp.concatenate([s1 * cos - s2 * sin, s1 * sin + s2 * cos], axis=-1)

    out = jnp.concatenate([rot(x[..., :d_axis], pos_r), rot(x[..., d_axis:], pos_c)], axis=-1)
    return out.astype(x.dtype)


def retention_scan(q, k, v, log_gamma, state0, strict):
    b, h, l, dk = q.shape
    dv = v.shape[-1]
    nc = l // RET_CHUNK
    qc = (q * dk ** -0.5).reshape(b, h, nc, RET_CHUNK, dk)
    kc = k.reshape(b, h, nc, RET_CHUNK, dk)
    vc = v.reshape(b, h, nc, RET_CHUNK, dv)
    lg = log_gamma.astype(jnp.float32)[:, None]
    i = jnp.arange(RET_CHUNK, dtype=jnp.float32)
    diff = i[:, None] - i[None, :]
    mask = (diff > 0) if strict else (diff >= 0)
    decay = jnp.where(mask, jnp.exp(lg[:, :, None] * jnp.where(mask, diff, 0.0)), 0.0)
    scores = jnp.einsum('bhnid,bhnjd->bhnij', qc, kc) * decay[:, None]
    inner = jnp.einsum('bhnij,bhnje->bhnie', scores, vc)
    k_w = kc * jnp.exp(lg * (RET_CHUNK - 1.0 - i))[:, None, :, None]
    upd = jnp.einsum('bhnjd,bhnje->nbhde', k_w, vc).astype(jnp.float32)
    chunk_decay = jnp.exp(lg * RET_CHUNK)[:, :, None]

    def step(state, u):
        return chunk_decay * state + u, state

    state_final, state_prev = lax.scan(step, state0.astype(jnp.float32), upd)
    q_w = qc * jnp.exp(lg * (i + 1.0))[:, None, :, None]
    cross = jnp.einsum('bhnid,nbhde->bhnie', q_w, state_prev)
    out = (inner + cross).reshape(b, h, l, dv).astype(v.dtype)
    return out, state_final


def head_layernorm(o, w):
    of = o.astype(jnp.float32)
    mu = jnp.mean(of, axis=-1, keepdims=True)
    var = jnp.mean(jnp.square(of - mu), axis=-1, keepdims=True)
    y = (of - mu) * lax.rsqrt(var + NORM_EPS)
    b, h, l, dv = o.shape
    y = y.transpose(0, 2, 1, 3).reshape(b, l, h * dv)
    return (y * w.astype(jnp.float32)).astype(o.dtype)


def retention_mixer(q, k, v, g, qc, kc, vc, gc, log_gammas, gn_w, with_ctx_out):
    n = q.shape[1]
    tok = jnp.arange(n)
    pos_r = (tok // GRID_W).astype(jnp.float32)
    pos_c = (tok % GRID_W).astype(jnp.float32)
    q = axial_rope(to_heads(q, RET_HEADS), pos_r, pos_c)
    k = axial_rope(to_heads(k, RET_HEADS), pos_r, pos_c)
    v = to_heads(v, RET_HEADS)
    qc, kc, vc = to_heads(qc, RET_HEADS), to_heads(kc, RET_HEADS), to_heads(vc, RET_HEADS)
    b = q.shape[0]
    zero = jnp.zeros((b, RET_HEADS, RET_DIM, RET_DIM), jnp.float32)

    def flip(t):
        return jnp.flip(t, axis=2)

    ctx_f, s_f = retention_scan(qc, kc, vc, log_gammas[0], zero, False)
    lat_f, _ = retention_scan(q, k, v, log_gammas[0], s_f, False)
    ctx_b, s_b = retention_scan(flip(qc), flip(kc), flip(vc), log_gammas[1], zero, True)
    lat_b, _ = retention_scan(flip(q), flip(k), flip(v), log_gammas[1], s_b, True)
    lat = head_layernorm(lat_f + flip(lat_b), gn_w) * jax.nn.silu(g)
    ctx_out = None
    if with_ctx_out:
        ctx_out = head_layernorm(ctx_f + flip(ctx_b), gn_w) * jax.nn.silu(gc)
    return lat, ctx_out


def neighbourhood_attention(q, k, v, kc, vc, rpb):
    b, n, _ = q.shape
    rows = n // GRID_W
    kh = min(NA_KH, rows)

    def grid(t):
        return t.reshape(b, rows, GRID_W, NA_HEADS, NA_DIM).transpose(0, 3, 1, 2, 4)

    qg = grid(q) * NA_DIM ** -0.5
    kg, vg = grid(k), grid(v)
    kc, vc = to_heads(kc, NA_HEADS), to_heads(vc, NA_HEADS)
    r = jnp.arange(rows)
    row_idx = jnp.clip(r - kh // 2, 0, rows - kh)[:, None] + jnp.arange(kh)[None, :]
    nk = kh * GRID_W
    k_band = kg[:, :, row_idx].reshape(b, NA_HEADS, rows, nk, NA_DIM)
    v_band = vg[:, :, row_idx].reshape(b, NA_HEADS, rows, nk, NA_DIM)
    col = jnp.arange(GRID_W)
    col_start = jnp.clip(col - NA_KW // 2, 0, GRID_W - NA_KW)
    key_col = jnp.tile(col, kh)
    key_row = jnp.repeat(row_idx, GRID_W, axis=1)
    valid = (key_col[None, :] >= col_start[:, None]) & (key_col[None, :] < col_start[:, None] + NA_KW)
    dr = key_row - r[:, None] + (NA_KH - 1)
    dc = jnp.clip(key_col[None, :] - col[:, None] + (NA_KW - 1), 0, 2 * NA_KW - 2)
    bias = rpb[:, dr[:, None, :], dc[None, :, :]].astype(jnp.float32)
    bias = jnp.where(valid[None, None], bias, NEG_INF)
    s_loc = jnp.einsum('bhrqd,bhrkd->bhrqk', qg, k_band).astype(jnp.float32) + bias
    s_ctx = jnp.einsum('bhrqd,bhkd->bhrqk', qg, kc).astype(jnp.float32)
    p = jax.nn.softmax(jnp.concatenate([s_loc, s_ctx], axis=-1), axis=-1).astype(v.dtype)
    out = (jnp.einsum('bhrqk,bhrkd->bhrqd', p[..., :nk], v_band)
           + jnp.einsum('bhrqk,bhkd->bhrqd', p[..., nk:], vc))
    return out.transpose(0, 2, 3, 1, 4).reshape(b, n, NA_WIDTH)


def context_attention(qc, kc, vc):
    b, l, _ = qc.shape
    q, k, v = to_heads(qc, NA_HEADS), to_heads(kc, NA_HEADS), to_heads(vc, NA_HEADS)
    s = jnp.einsum('bhqd,bhkd->bhqk', q * NA_DIM ** -0.5, k).astype(jnp.float32)
    p = jax.nn.softmax(s, axis=-1).astype(v.dtype)
    o = jnp.einsum('bhqk,bhkd->bhqd', p, v)
    return o.transpose(0, 2, 1, 3).reshape(b, l, NA_WIDTH)


def squared_relu_mlp(h, w1, w2):
    return jnp.square(jax.nn.relu(h @ w1)) @ w2


def hybrid_layer(x, ctx, c, c_ctx, w_ada, b_ada, g_pre_mix, g_post_mix, g_pre_mlp, g_post_mlp,
                 w_in, ret_decay, ret_gn, na_rpb, w_out, w_mlp1, w_mlp2, update_ctx):
    sh1, sc1, gt1, sh2, sc2, gt2 = modulations(c[:, None, :], w_ada, b_ada)
    csh1, csc1, cgt1, csh2, csc2, cgt2 = modulations(c_ctx, w_ada, b_ada)
    split_at = [int(s) for s in np.cumsum(IN_SPLITS)[:-1]]
    h = rmsnorm(x, g_pre_mix) * (1.0 + sc1) + sh1
    hc = rmsnorm(ctx, g_pre_mix) * (1.0 + csc1) + csh1
    rq, rk, rv, rg, nq, nk, nv = jnp.split(h @ w_in, split_at, axis=-1)
    crq, crk, crv, crg, cnq, cnk, cnv = jnp.split(hc @ w_in, split_at, axis=-1)
    log_gammas = jax.nn.log_sigmoid(ret_decay.astype(jnp.float32))
    ret_lat, ret_ctx = retention_mixer(rq, rk, rv, rg, crq, crk, crv, crg, log_gammas, ret_gn, update_ctx)
    na_lat = neighbourhood_attention(nq, nk, nv, cnk, cnv, na_rpb)
    mix = jnp.concatenate([ret_lat, na_lat], axis=-1) @ w_out
    x = x + gt1 * rmsnorm(mix, g_post_mix)
    h2 = rmsnorm(x, g_pre_mlp) * (1.0 + sc2) + sh2
    x = x + gt2 * rmsnorm(squared_relu_mlp(h2, w_mlp1, w_mlp2), g_post_mlp)
    if update_ctx:
        na_ctx = context_attention(cnq, cnk, cnv)
        mix_c = jnp.concatenate([ret_ctx, na_ctx], axis=-1) @ w_out
        ctx = ctx + cgt1 * rmsnorm(mix_c, g_post_mix)
        hc2 = rmsnorm(ctx, g_pre_mlp) * (1.0 + csc2) + csh2
        ctx = ctx + cgt2 * rmsnorm(squared_relu_mlp(hc2, w_mlp1, w_mlp2), g_post_mlp)
    return x, ctx


def _fwd_setup_inputs(seed: int = 0) -> dict:
    key = jax.random.key(seed)
    ks = jax.random.split(key, 17)

    def nrm(k, shape, s):
        return jax.random.normal(k, shape, jnp.float32) * s

    base_logit = jnp.log(2.0 ** (5.0 + jnp.arange(RET_HEADS, dtype=jnp.float32)) - 1.0)
    return {
        'x': nrm(ks[0], (BATCH, SEQ, D_MODEL), 1.0),
        'c': nrm(ks[1], (BATCH, D_MODEL), 1.0),
        'ctx': nrm(ks[2], (BATCH, CTX_LEN, D_MODEL), 1.0),
        'c_ctx': nrm(ks[3], (D_MODEL,), 1.0),
        'w_ada': nrm(ks[4], (DEPTH, D_MODEL, N_MOD * D_MODEL), D_MODEL ** -0.5),
        'b_ada': nrm(ks[5], (DEPTH, N_MOD * D_MODEL), 0.02),
        'g_pre_mix': 1.0 + nrm(ks[6], (DEPTH, D_MODEL), 0.02),
        'g_post_mix': 1.0 + nrm(ks[7], (DEPTH, D_MODEL), 0.02),
        'g_pre_mlp': 1.0 + nrm(ks[8], (DEPTH, D_MODEL), 0.02),
        'g_post_mlp': 1.0 + nrm(ks[9], (DEPTH, D_MODEL), 0.02),
        'w_in': nrm(ks[10], (DEPTH, D_MODEL, IN_WIDTH), D_MODEL ** -0.5),
        'ret_decay': base_logit[None, None, :] + nrm(ks[11], (DEPTH, 2, RET_HEADS), 0.1),
        'ret_gn': 1.0 + nrm(ks[12], (DEPTH, RET_WIDTH), 0.02),
        'na_rpb': nrm(ks[13], (DEPTH, NA_HEADS, 2 * NA_KH - 1, 2 * NA_KW - 1), 0.1),
        'w_out': nrm(ks[14], (DEPTH, MIX_WIDTH, D_MODEL), MIX_WIDTH ** -0.5),
        'w_mlp1': nrm(ks[15], (DEPTH, D_MODEL, D_FF), D_MODEL ** -0.5),
        'w_mlp2': nrm(ks[16], (DEPTH, D_FF, D_MODEL), D_FF ** -0.5),
    }


def _fwd_reference(x, c, ctx, c_ctx, w_ada, b_ada, g_pre_mix, g_post_mix, g_pre_mlp, g_post_mlp,
              w_in, ret_decay, ret_gn, na_rpb, w_out, w_mlp1, w_mlp2):
    for layer in range(DEPTH):
        x, ctx = hybrid_layer(x, ctx, c, c_ctx, w_ada[layer], b_ada[layer], g_pre_mix[layer],
                              g_post_mix[layer], g_pre_mlp[layer], g_post_mlp[layer], w_in[layer],
                              ret_decay[layer], ret_gn[layer], na_rpb[layer], w_out[layer],
                              w_mlp1[layer], w_mlp2[layer], update_ctx=(layer + 1 < DEPTH))
    return x


import jax as _jax
import jax.numpy as _jnp

TWIN_FORMAT = 'train_step'
FWD_PARAMS = ['x', 'c', 'ctx', 'c_ctx', 'w_ada', 'b_ada', 'g_pre_mix', 'g_post_mix', 'g_pre_mlp', 'g_post_mlp', 'w_in', 'ret_decay', 'ret_gn', 'na_rpb', 'w_out', 'w_mlp1', 'w_mlp2']
TWIN_WEIGHTS = ['c_ctx', 'w_ada', 'b_ada', 'g_pre_mix', 'g_post_mix', 'g_pre_mlp', 'g_post_mlp', 'w_in', 'ret_decay', 'ret_gn', 'na_rpb', 'w_out', 'w_mlp1', 'w_mlp2']
TWIN_DIFF_INPUT = 'x'
TWIN_INPUTS = ['x', 'c', 'ctx', 'c_ctx', 'w_ada', 'b_ada', 'g_pre_mix', 'g_post_mix', 'g_pre_mlp', 'g_post_mlp', 'w_in', 'ret_decay', 'ret_gn', 'na_rpb', 'w_out', 'w_mlp1', 'w_mlp2', 'loss_target', 'm_c_ctx', 'm_w_ada', 'm_b_ada', 'm_g_pre_mix', 'm_g_post_mix', 'm_g_pre_mlp', 'm_g_post_mlp', 'm_w_in', 'm_ret_decay', 'm_ret_gn', 'm_na_rpb', 'm_w_out', 'm_w_mlp1', 'm_w_mlp2', 'v_c_ctx', 'v_w_ada', 'v_b_ada', 'v_g_pre_mix', 'v_g_post_mix', 'v_g_pre_mlp', 'v_g_post_mlp', 'v_w_in', 'v_ret_decay', 'v_ret_gn', 'v_na_rpb', 'v_w_out', 'v_w_mlp1', 'v_w_mlp2']
TWIN_OUTPUTS = ['loss', 'grad_x', 'grad_c_ctx', 'grad_w_ada', 'grad_b_ada', 'grad_g_pre_mix', 'grad_g_post_mix', 'grad_g_pre_mlp', 'grad_g_post_mlp', 'grad_w_in', 'grad_ret_decay', 'grad_ret_gn', 'grad_na_rpb', 'grad_w_out', 'grad_w_mlp1', 'grad_w_mlp2', 'delta_c_ctx', 'delta_w_ada', 'delta_b_ada', 'delta_g_pre_mix', 'delta_g_post_mix', 'delta_g_pre_mlp', 'delta_g_post_mlp', 'delta_w_in', 'delta_ret_decay', 'delta_ret_gn', 'delta_na_rpb', 'delta_w_out', 'delta_w_mlp1', 'delta_w_mlp2', 'new_m_c_ctx', 'new_m_w_ada', 'new_m_b_ada', 'new_m_g_pre_mix', 'new_m_g_post_mix', 'new_m_g_pre_mlp', 'new_m_g_post_mlp', 'new_m_w_in', 'new_m_ret_decay', 'new_m_ret_gn', 'new_m_na_rpb', 'new_m_w_out', 'new_m_w_mlp1', 'new_m_w_mlp2', 'new_v_c_ctx', 'new_v_w_ada', 'new_v_b_ada', 'new_v_g_pre_mix', 'new_v_g_post_mix', 'new_v_g_pre_mlp', 'new_v_g_post_mlp', 'new_v_w_in', 'new_v_ret_decay', 'new_v_ret_gn', 'new_v_na_rpb', 'new_v_w_out', 'new_v_w_mlp1', 'new_v_w_mlp2']
TWIN_LEAF_KINDS = {'loss': 'loss', 'grad_x': 'grad_x', 'grad_c_ctx': 'grad_w', 'grad_w_ada': 'grad_w', 'grad_b_ada': 'grad_w', 'grad_g_pre_mix': 'grad_w', 'grad_g_post_mix': 'grad_w', 'grad_g_pre_mlp': 'grad_w', 'grad_g_post_mlp': 'grad_w', 'grad_w_in': 'grad_w', 'grad_ret_decay': 'grad_w', 'grad_ret_gn': 'grad_w', 'grad_na_rpb': 'grad_w', 'grad_w_out': 'grad_w', 'grad_w_mlp1': 'grad_w', 'grad_w_mlp2': 'grad_w', 'delta_c_ctx': 'delta_w', 'delta_w_ada': 'delta_w', 'delta_b_ada': 'delta_w', 'delta_g_pre_mix': 'delta_w', 'delta_g_post_mix': 'delta_w', 'delta_g_pre_mlp': 'delta_w', 'delta_g_post_mlp': 'delta_w', 'delta_w_in': 'delta_w', 'delta_ret_decay': 'delta_w', 'delta_ret_gn': 'delta_w', 'delta_na_rpb': 'delta_w', 'delta_w_out': 'delta_w', 'delta_w_mlp1': 'delta_w', 'delta_w_mlp2': 'delta_w', 'new_m_c_ctx': 'new_m', 'new_m_w_ada': 'new_m', 'new_m_b_ada': 'new_m', 'new_m_g_pre_mix': 'new_m', 'new_m_g_post_mix': 'new_m', 'new_m_g_pre_mlp': 'new_m', 'new_m_g_post_mlp': 'new_m', 'new_m_w_in': 'new_m', 'new_m_ret_decay': 'new_m', 'new_m_ret_gn': 'new_m', 'new_m_na_rpb': 'new_m', 'new_m_w_out': 'new_m', 'new_m_w_mlp1': 'new_m', 'new_m_w_mlp2': 'new_m', 'new_v_c_ctx': 'new_v', 'new_v_w_ada': 'new_v', 'new_v_b_ada': 'new_v', 'new_v_g_pre_mix': 'new_v', 'new_v_g_post_mix': 'new_v', 'new_v_g_pre_mlp': 'new_v', 'new_v_g_post_mlp': 'new_v', 'new_v_w_in': 'new_v', 'new_v_ret_decay': 'new_v', 'new_v_ret_gn': 'new_v', 'new_v_na_rpb': 'new_v', 'new_v_w_out': 'new_v', 'new_v_w_mlp1': 'new_v', 'new_v_w_mlp2': 'new_v'}


def _forward(args):
    return _fwd_reference(*[args[k] for k in FWD_PARAMS])


def _output_shape():
    out = _jax.eval_shape(lambda: _forward(_fwd_setup_inputs(0)))
    return out.shape, out.dtype

N_MICROBATCH = 1
ADAM_LR = 0.001
ADAM_B1 = 0.9
ADAM_B2 = 0.999
ADAM_EPS = 1e-08
ADAM_WD = 0.01
ADAM_STEP = 10
PER_EXAMPLE_BATCH_AXIS = {'x': 0, 'c': 0, 'ctx': 0, 'loss_target': 0}
SHARED_INPUTS = []
_WEIGHT_DTYPES = {'c_ctx': _jnp.float32, 'w_ada': _jnp.float32, 'b_ada': _jnp.float32, 'g_pre_mix': _jnp.float32, 'g_post_mix': _jnp.float32, 'g_pre_mlp': _jnp.float32, 'g_post_mlp': _jnp.float32, 'w_in': _jnp.float32, 'ret_decay': _jnp.float32, 'ret_gn': _jnp.float32, 'na_rpb': _jnp.float32, 'w_out': _jnp.float32, 'w_mlp1': _jnp.float32, 'w_mlp2': _jnp.float32}
MOMENT_SCALE = {'c_ctx': 1.648031e+00, 'w_ada': 4.224942e+00, 'b_ada': 7.888499e+00, 'g_pre_mix': 7.612643e-01, 'g_post_mix': 1.531820e+01, 'g_pre_mlp': 9.024067e-01, 'g_post_mlp': 1.655617e+01, 'w_in': 1.612050e+00, 'ret_decay': 1.204434e+00, 'ret_gn': 1.940852e+00, 'na_rpb': 2.410993e-02, 'w_out': 2.492879e+00, 'w_mlp1': 1.075358e+00, 'w_mlp2': 3.179633e+00}


def _to_microbatches(a, axis):
    t = _jnp.moveaxis(a, axis, 0)
    t = t.reshape((N_MICROBATCH, t.shape[0] // N_MICROBATCH) + t.shape[1:])
    return _jnp.moveaxis(t, 1, axis + 1)


def setup_inputs(seed: int = 0) -> dict:
    inp = _fwd_setup_inputs(seed)
    key = _jax.random.fold_in(_jax.random.key(seed), 7919)
    shape, _ = _output_shape()
    out = dict(inp)
    out["loss_target"] = _jax.random.normal(_jax.random.fold_in(key, 0), shape, _jnp.float32)
    for i, name in enumerate(TWIN_WEIGHTS):
        w = inp[name].astype(_jnp.float32)
        if MOMENT_SCALE is None:
            s = _jnp.sqrt(_jnp.mean(_jnp.square(w)) + 1e-30)
        else:
            s = MOMENT_SCALE[name]
        km, kv = _jax.random.split(_jax.random.fold_in(key, i + 1))
        out[name] = w
        out["m_" + name] = s * _jax.random.normal(km, w.shape, _jnp.float32)
        out["v_" + name] = (s * s) * _jax.random.uniform(kv, w.shape, _jnp.float32, 0.5, 1.5)
    if N_MICROBATCH > 1:
        for name, axis in PER_EXAMPLE_BATCH_AXIS.items():
            out[name] = _to_microbatches(out[name], axis)
    return {'x': out['x'], 'c': out['c'], 'ctx': out['ctx'], 'c_ctx': out['c_ctx'], 'w_ada': out['w_ada'], 'b_ada': out['b_ada'], 'g_pre_mix': out['g_pre_mix'], 'g_post_mix': out['g_post_mix'], 'g_pre_mlp': out['g_pre_mlp'], 'g_post_mlp': out['g_post_mlp'], 'w_in': out['w_in'], 'ret_decay': out['ret_decay'], 'ret_gn': out['ret_gn'], 'na_rpb': out['na_rpb'], 'w_out': out['w_out'], 'w_mlp1': out['w_mlp1'], 'w_mlp2': out['w_mlp2'], 'loss_target': out['loss_target'], 'm_c_ctx': out['m_c_ctx'], 'm_w_ada': out['m_w_ada'], 'm_b_ada': out['m_b_ada'], 'm_g_pre_mix': out['m_g_pre_mix'], 'm_g_post_mix': out['m_g_post_mix'], 'm_g_pre_mlp': out['m_g_pre_mlp'], 'm_g_post_mlp': out['m_g_post_mlp'], 'm_w_in': out['m_w_in'], 'm_ret_decay': out['m_ret_decay'], 'm_ret_gn': out['m_ret_gn'], 'm_na_rpb': out['m_na_rpb'], 'm_w_out': out['m_w_out'], 'm_w_mlp1': out['m_w_mlp1'], 'm_w_mlp2': out['m_w_mlp2'], 'v_c_ctx': out['v_c_ctx'], 'v_w_ada': out['v_w_ada'], 'v_b_ada': out['v_b_ada'], 'v_g_pre_mix': out['v_g_pre_mix'], 'v_g_post_mix': out['v_g_post_mix'], 'v_g_pre_mlp': out['v_g_pre_mlp'], 'v_g_post_mlp': out['v_g_post_mlp'], 'v_w_in': out['v_w_in'], 'v_ret_decay': out['v_ret_decay'], 'v_ret_gn': out['v_ret_gn'], 'v_na_rpb': out['v_na_rpb'], 'v_w_out': out['v_w_out'], 'v_w_mlp1': out['v_w_mlp1'], 'v_w_mlp2': out['v_w_mlp2']}


def _loss(weights, diff, rest, loss_target):
    with _jax.named_scope("forward"):
        args = {**rest, TWIN_DIFF_INPUT: diff, **{k: w.astype(_WEIGHT_DTYPES[k]) for k, w in weights.items()}}
        y = _forward(args)
    with _jax.named_scope("loss_head"):
        err = _jnp.square(y.astype(_jnp.float32) - loss_target)
        return 0.5 * _jnp.sum(_jnp.mean(err, axis=-1)) if err.ndim else 0.5 * err


def _adamw(w, g, m, v):
    m = ADAM_B1 * m + (1.0 - ADAM_B1) * g
    v = ADAM_B2 * v + (1.0 - ADAM_B2) * _jnp.square(g)
    m_hat = m / (1.0 - ADAM_B1 ** ADAM_STEP)
    v_hat = v / (1.0 - ADAM_B2 ** ADAM_STEP)
    delta = -ADAM_LR * (m_hat / (_jnp.sqrt(v_hat) + ADAM_EPS) + ADAM_WD * w)
    return delta, m, v


def reference(x, c, ctx, c_ctx, w_ada, b_ada, g_pre_mix, g_post_mix, g_pre_mlp, g_post_mlp, w_in, ret_decay, ret_gn, na_rpb, w_out, w_mlp1, w_mlp2, loss_target, m_c_ctx, m_w_ada, m_b_ada, m_g_pre_mix, m_g_post_mix, m_g_pre_mlp, m_g_post_mlp, m_w_in, m_ret_decay, m_ret_gn, m_na_rpb, m_w_out, m_w_mlp1, m_w_mlp2, v_c_ctx, v_w_ada, v_b_ada, v_g_pre_mix, v_g_post_mix, v_g_pre_mlp, v_g_post_mlp, v_w_in, v_ret_decay, v_ret_gn, v_na_rpb, v_w_out, v_w_mlp1, v_w_mlp2):
    given = dict(x=x, c=c, ctx=ctx, c_ctx=c_ctx, w_ada=w_ada, b_ada=b_ada, g_pre_mix=g_pre_mix, g_post_mix=g_post_mix, g_pre_mlp=g_pre_mlp, g_post_mlp=g_post_mlp, w_in=w_in, ret_decay=ret_decay, ret_gn=ret_gn, na_rpb=na_rpb, w_out=w_out, w_mlp1=w_mlp1, w_mlp2=w_mlp2, loss_target=loss_target, m_c_ctx=m_c_ctx, m_w_ada=m_w_ada, m_b_ada=m_b_ada, m_g_pre_mix=m_g_pre_mix, m_g_post_mix=m_g_post_mix, m_g_pre_mlp=m_g_pre_mlp, m_g_post_mlp=m_g_post_mlp, m_w_in=m_w_in, m_ret_decay=m_ret_decay, m_ret_gn=m_ret_gn, m_na_rpb=m_na_rpb, m_w_out=m_w_out, m_w_mlp1=m_w_mlp1, m_w_mlp2=m_w_mlp2, v_c_ctx=v_c_ctx, v_w_ada=v_w_ada, v_b_ada=v_b_ada, v_g_pre_mix=v_g_pre_mix, v_g_post_mix=v_g_post_mix, v_g_pre_mlp=v_g_pre_mlp, v_g_post_mlp=v_g_post_mlp, v_w_in=v_w_in, v_ret_decay=v_ret_decay, v_ret_gn=v_ret_gn, v_na_rpb=v_na_rpb, v_w_out=v_w_out, v_w_mlp1=v_w_mlp1, v_w_mlp2=v_w_mlp2)
    weights = {n: given[n] for n in TWIN_WEIGHTS}
    shared = {n: given[n] for n in SHARED_INPUTS}
    per_example = {n: given[n] for n in ['x', 'c', 'ctx']}
    grad_fn = _jax.value_and_grad(_loss, argnums=(0, 1))

    def one_microbatch(ex, loss_target):
        ex = dict(ex)
        diff = ex.pop(TWIN_DIFF_INPUT)
        return grad_fn(weights, diff, {**shared, **ex}, loss_target)

    if N_MICROBATCH == 1:
        loss, (grad_w, grad_x) = one_microbatch(per_example, given["loss_target"])
    else:
        def body(carry, xs):
            loss_sum, grad_sum = carry
            l_k, (gw_k, gx_k) = one_microbatch(xs[0], xs[1])
            with _jax.named_scope("update"):
                return (loss_sum + l_k, _jax.tree.map(_jnp.add, grad_sum, gw_k)), gx_k

        init = (_jnp.zeros((), _jnp.float32), _jax.tree.map(_jnp.zeros_like, weights))
        (loss, grad_w), grad_x = _jax.lax.scan(body, init, (per_example, given["loss_target"]))
    with _jax.named_scope("update"):
        delta_w, new_m, new_v = {}, {}, {}
        for n in TWIN_WEIGHTS:
            delta_w[n], new_m[n], new_v[n] = _adamw(weights[n], grad_w[n], given["m_" + n], given["v_" + n])
    return (loss, grad_x, *[grad_w[n] for n in TWIN_WEIGHTS], *[delta_w[n] for n in TWIN_WEIGHTS],
            *[new_m[n] for n in TWIN_WEIGHTS], *[new_v[n] for n in TWIN_WEIGHTS])
```

```python
import functools

import jax
import jax.numpy as jnp
from jax import lax
from jax.experimental import pallas as pl
from jax.experimental.pallas import tpu as pltpu

F32, BF16 = jnp.float32, jnp.bfloat16
D = 1024
SEQ = 2048
LC = 256
GW = 64
RH, RD, CH = 4, 128, 128
NPAIR = 4
IN_W = 3584
RET_W = 2048
DFF = 4096
EPS = 1e-6
NEG = -1e30
TN = 256
NCH = SEQ // CH
LR, B1, B2, AEPS, WD, STEP = 0.001, 0.9, 0.999, 1e-08, 0.01, 10
MESH = pl.DeviceIdType.MESH
VMEM_LIMIT = 56 * 1024 * 1024


def _cp(sem=None):
    return pltpu.CompilerParams(dimension_semantics=sem, vmem_limit_bytes=VMEM_LIMIT)


def _nn(a, b):
    return jnp.dot(a.astype(BF16), b.astype(BF16), preferred_element_type=F32)


def _nt(a, b):
    return lax.dot_general(a.astype(BF16), b.astype(BF16), (((1,), (1,)), ((), ())), preferred_element_type=F32)


def _tn(a, b):
    return lax.dot_general(a.astype(BF16), b.astype(BF16), (((0,), (0,)), ((), ())), preferred_element_type=F32)


@jax.custom_vjp
def mm_nn(a, b):
    return _nn(a, b)


@jax.custom_vjp
def mm_nt(a, b):
    return _nt(a, b)


@jax.custom_vjp
def mm_tn(a, b):
    return _tn(a, b)


mm_nn.defvjp(lambda a, b: (_nn(a, b), (a, b)), lambda r, g: (_nt(g, r[1]), _tn(r[0], g)))
mm_nt.defvjp(lambda a, b: (_nt(a, b), (a, b)), lambda r, g: (_nn(g, r[1]), _tn(g, r[0])))
mm_tn.defvjp(lambda a, b: (_tn(a, b), (a, b)), lambda r, g: (_nt(r[1], g), _nn(r[0], g)))


def _rms(x):
    return x * lax.rsqrt(jnp.mean(x * x, axis=-1, keepdims=True) + EPS)


def _rms_mod(x, g, sc, sh):
    return (_rms(x) * g) * (1.0 + sc) + sh


def _post_mix(x, mix, gt1, sc2, sh2, g_post_mix, g_pre_mlp):
    x1 = x + gt1 * (_rms(mix) * g_post_mix)
    return x1, _rms_mod(x1, g_pre_mlp, sc2, sh2)


def _head_loss(x1, m, gt2, g_post_mlp, tgt):
    err = x1 + gt2 * (_rms(m) * g_post_mlp) - tgt
    return 0.5 * jnp.sum(jnp.mean(err * err, axis=-1, keepdims=True), axis=0, keepdims=True)


def _ln_gate(o, g, w):
    mu = jnp.mean(o, axis=-1, keepdims=True)
    var = jnp.mean(jnp.square(o - mu), axis=-1, keepdims=True)
    y = (o - mu) * lax.rsqrt(var + EPS)
    return (y * w) * (g * jax.nn.sigmoid(g))


def _swap32(x):
    lane = lax.broadcasted_iota(jnp.int32, x.shape, 1)
    return jnp.where((lane & 32) == 0, pltpu.roll(x, 96, 1), pltpu.roll(x, 32, 1))


def _rope(x, cos, sin):
    return x * cos + _swap32(x) * sin


def _rope_t(g, cos, sin):
    return g * cos + _swap32(g * sin)


def _rope_tables():
    tok = jnp.arange(SEQ)
    pos_r = (tok // GW).astype(F32)
    pos_c = (tok % GW).astype(F32)
    inv = 10000.0 ** (-jnp.arange(32, dtype=F32) / 32)
    ar = pos_r[:, None] * inv[None, :]
    ac = pos_c[:, None] * inv[None, :]
    cos = jnp.concatenate([jnp.cos(ar), jnp.cos(ar), jnp.cos(ac), jnp.cos(ac)], axis=-1)
    sin = jnp.concatenate([-jnp.sin(ar), jnp.sin(ar), -jnp.sin(ac), jnp.sin(ac)], axis=-1)
    return cos, sin


def _fiota(shape, dim):
    return lax.broadcasted_iota(jnp.int32, shape, dim).astype(F32)


def _ret_state(k, v, s, lg, reverse):
    pos = _fiota((CH, 1), 0)
    b_exp = pos if reverse else (CH - 1.0 - pos)
    return jnp.exp(lg * CH) * s + mm_tn(k * jnp.exp(lg * b_exp), v)


def _ret_chunk(q, k, v, s, lg, reverse):
    i = _fiota((CH, CH), 0)
    j = _fiota((CH, CH), 1)
    diff = (j - i) if reverse else (i - j)
    mask = (diff > 0) if reverse else (diff >= 0)
    decay = jnp.where(mask, jnp.exp(lg * jnp.where(mask, diff, 0.0)), 0.0)
    pos = _fiota((CH, 1), 0)
    a_exp = (CH - pos) if reverse else (pos + 1.0)
    o = mm_nn(mm_nt(q, k) * decay, v) + mm_nn(q * jnp.exp(lg * a_exp), s)
    return o, _ret_state(k, v, s, lg, reverse)


def premix_proj(xin, mod3, g_pre, wperm, is_ctx, name):
    nb, length, _ = xin.shape
    tn = min(TN, length)

    def body(x_ref, mod_ref, g_ref, w_ref, h_ref, pret_ref, pna_ref):
        h = _rms_mod(x_ref[...], g_ref[...], mod_ref[1:2, :], mod_ref[0:1, :])
        hb = h.astype(BF16)
        h_ref[...] = hb
        pret_ref[...] = jnp.dot(hb, w_ref[:, :RET_W], preferred_element_type=F32)
        pna_ref[...] = jnp.dot(hb, w_ref[:, RET_W:], preferred_element_type=F32).astype(BF16)

    return pl.pallas_call(
        body, name=name, grid=(nb, length // tn),
        in_specs=[
            pl.BlockSpec((None, tn, D), lambda b, t: (b, t, 0)),
            pl.BlockSpec((None, 6, D), (lambda b, t: (2, 0, 0)) if is_ctx else (lambda b, t: (b, 0, 0))),
            pl.BlockSpec((1, D), lambda b, t: (0, 0)),
            pl.BlockSpec((D, IN_W), lambda b, t: (0, 0), pipeline_mode=pl.Buffered(1)),
        ],
        out_specs=[
            pl.BlockSpec((None, tn, D), lambda b, t: (b, t, 0)),
            pl.BlockSpec((None, tn, RET_W), lambda b, t: (b, t, 0)),
            pl.BlockSpec((None, tn, IN_W - RET_W), lambda b, t: (b, t, 0)),
        ],
        out_shape=[
            jax.ShapeDtypeStruct((nb, length, D), BF16),
            jax.ShapeDtypeStruct((nb, length, RET_W), F32),
            jax.ShapeDtypeStruct((nb, length, IN_W - RET_W), BF16),
        ],
        compiler_params=_cp(("arbitrary", "arbitrary")),
    )(xin, mod3, g_pre, wperm)


def premix_bwd(xin, mod3, g_pre, wperm, dproj, dx_tail, name):
    nb, length, _ = xin.shape
    tn = min(TN, length)
    is_ctx = dx_tail is None

    def body(*refs):
        if is_ctx:
            x_ref, mod_ref, g_ref, w_ref, dp_ref, dmod_ref, dg_ref = refs
        else:
            x_ref, mod_ref, g_ref, w_ref, dp_ref, dxt_ref, dx_ref, dmod_ref, dg_ref = refs
        b, t = pl.program_id(0), pl.program_id(1)
        dh = lax.dot_general(dp_ref[...], w_ref[...], (((1,), (1,)), ((), ())), preferred_element_type=F32)
        _, vjp = jax.vjp(_rms_mod, x_ref[...], g_ref[...], mod_ref[1:2, :], mod_ref[0:1, :])
        dx, dg, dsc, dsh = vjp(dh)
        if not is_ctx:
            dx_ref[...] = dx + dxt_ref[...]

        @pl.when((t == 0) & ((b == 0) if is_ctx else True))
        def _():
            dmod_ref[...] = jnp.zeros_like(dmod_ref)

        @pl.when((t == 0) & (b == 0))
        def _():
            dg_ref[...] = jnp.zeros_like(dg_ref)

        dmod_ref[0:1, :] += dsh
        dmod_ref[1:2, :] += dsc
        dg_ref[0:1, :] += dg

    tok = lambda b, t: (b, t, 0)
    in_specs = [
        pl.BlockSpec((None, tn, D), tok),
        pl.BlockSpec((None, 6, D), (lambda b, t: (2, 0, 0)) if is_ctx else (lambda b, t: (b, 0, 0))),
        pl.BlockSpec((1, D), lambda b, t: (0, 0)),
        pl.BlockSpec((D, IN_W), lambda b, t: (0, 0), pipeline_mode=pl.Buffered(1)),
        pl.BlockSpec((None, tn, IN_W), tok),
    ]
    args = [xin, mod3, g_pre, wperm, dproj]
    out_specs = [
        pl.BlockSpec((None, 6, D), (lambda b, t: (0, 0, 0)) if is_ctx else (lambda b, t: (b, 0, 0))),
        pl.BlockSpec((8, D), lambda b, t: (0, 0)),
    ]
    out_shape = [jax.ShapeDtypeStruct((1 if is_ctx else nb, 6, D), F32), jax.ShapeDtypeStruct((8, D), F32)]
    if not is_ctx:
        in_specs.append(pl.BlockSpec((None, tn, D), tok))
        args.append(dx_tail)
        out_specs.insert(0, pl.BlockSpec((None, tn, D), tok))
        out_shape.insert(0, jax.ShapeDtypeStruct((nb, length, D), F32))
    return pl.pallas_call(
        body, name=name, grid=(nb, length // tn), in_specs=in_specs, out_specs=out_specs, out_shape=out_shape,
        compiler_params=_cp(("arbitrary", "arbitrary")),
    )(*args)


def _ret_specs(order):
    def im(f):
        return lambda *g: f(*order(*g))
    return dict(
        pret=pl.BlockSpec((None, SEQ, 512), im(lambda b, h: (b, 0, h))),
        pretc=pl.BlockSpec((None, LC, 512), im(lambda b, h: (b, 0, h))),
        rd=pl.BlockSpec((None, 2, 1), im(lambda b, h: (h, 0, 0))),
        gn=pl.BlockSpec((None, 1, RD), im(lambda b, h: (h, 0, 0))),
        tab=pl.BlockSpec((SEQ, RD), im(lambda b, h: (0, 0))),
        head=pl.BlockSpec((None, SEQ, RD), im(lambda b, h: (b, 0, h))),
    )


def retention_fwd(pret, pretc, rd, gn, cos, sin):
    nb = pret.shape[0]
    sp = _ret_specs(lambda b, h: (b, h))

    def body(p_ref, pc_ref, rd_ref, gn_ref, cos_ref, sin_ref, o_ref, mix_ref, q_s, k_s, o_s):
        cos_v, sin_v = cos_ref[...], sin_ref[...]
        q_s[...] = _rope(p_ref[:, 0:128], cos_v, sin_v) * (RD ** -0.5)
        k_s[...] = _rope(p_ref[:, 128:256], cos_v, sin_v)
        for direction in (0, 1):
            rev = direction == 1
            lg = jax.nn.log_sigmoid(rd_ref[direction:direction + 1, :])
            s = jnp.zeros((RD, RD), F32)
            for n in ((1, 0) if rev else (0, 1)):
                s = _ret_state(pc_ref[n * CH:(n + 1) * CH, 128:256], pc_ref[n * CH:(n + 1) * CH, 256:384], s, lg, rev)

            def step(t, s, rev=rev, lg=lg):
                n = (NCH - 1 - t) if rev else t
                sl = pl.ds(pl.multiple_of(n * CH, CH), CH)
                o, s2 = _ret_chunk(q_s[sl, :], k_s[sl, :], p_ref[sl, 256:384], s, lg, rev)
                if rev:
                    o_s[sl, :] += o
                else:
                    o_s[sl, :] = o
                return s2

            lax.fori_loop(0, NCH, step, s)
        o = o_s[...]
        o_ref[...] = o
        mix_ref[...] = _ln_gate(o, p_ref[:, 384:512], gn_ref[...]).astype(BF16)

    return pl.pallas_call(
        body, name="retention_fwd", grid=(nb, RH),
        in_specs=[sp["pret"], sp["pretc"], sp["rd"], sp["gn"], sp["tab"], sp["tab"]],
        out_specs=[sp["head"], sp["head"]],
        out_shape=[jax.ShapeDtypeStruct((nb, SEQ, RH * RD), F32), jax.ShapeDtypeStruct((nb, SEQ, D), BF16)],
        scratch_shapes=[pltpu.VMEM((SEQ, RD), F32)] * 3,
        compiler_params=_cp(("arbitrary", "arbitrary")),
    )(pret, pretc, rd, gn, cos, sin)


def retention_bwd(pret, pretc, o_all, dmixin, rd, gn, cos, sin):
    nb = pret.shape[0]
    sp = _ret_specs(lambda h, b: (b, h))

    def body(p_ref, pc_ref, o_ref, dmix_ref, rd_ref, gn_ref, cos_ref, sin_ref,
             dp_ref, dpc_ref, drd_ref, dgn_ref, q_s, k_s, do_s, dq_s, dk_s, dv_s, st_s):
        b = pl.program_id(1)
        cos_v, sin_v = cos_ref[...], sin_ref[...]
        q_s[...] = _rope(p_ref[:, 0:128], cos_v, sin_v) * (RD ** -0.5)
        k_s[...] = _rope(p_ref[:, 128:256], cos_v, sin_v)
        _, gate_vjp = jax.vjp(_ln_gate, o_ref[...], p_ref[:, 384:512], gn_ref[...])
        do, dg, dgn = gate_vjp(dmix_ref[...].astype(F32))
        do_s[...] = do
        dp_ref[:, 384:512] = dg.astype(BF16)

        @pl.when(b == 0)
        def _():
            drd_ref[...] = jnp.zeros_like(drd_ref)
            dgn_ref[...] = jnp.zeros_like(dgn_ref)

        dgn_ref[...] += dgn
        dkc = [None, None]
        dvc = [None, None]
        for direction in (0, 1):
            rev = direction == 1
            rdv = rd_ref[direction:direction + 1, :]
            lg = jax.nn.log_sigmoid(rdv)
            order_c = (1, 0) if rev else (0, 1)
            kcs = [pc_ref[n * CH:(n + 1) * CH, 128:256] for n in (0, 1)]
            vcs = [pc_ref[n * CH:(n + 1) * CH, 256:384] for n in (0, 1)]
            s = jnp.zeros((RD, RD), F32)
            ctx_states = []
            for n in order_c:
                ctx_states.append(s)
                s = _ret_state(kcs[n], vcs[n], s, lg, rev)

            def fstep(t, s, rev=rev, lg=lg):
                n = (NCH - 1 - t) if rev else t
                sl = pl.ds(pl.multiple_of(n * CH, CH), CH)
                st_s[n] = s
                return _ret_state(k_s[sl, :], p_ref[sl, 256:384], s, lg, rev)

            lax.fori_loop(0, NCH, fstep, s)

            def bstep(t, carry, rev=rev, lg=lg, direction=direction):
                ds, dlg = carry
                n = t if rev else (NCH - 1 - t)
                sl = pl.ds(pl.multiple_of(n * CH, CH), CH)
                _, vjp = jax.vjp(functools.partial(_ret_chunk, reverse=rev),
                                 q_s[sl, :], k_s[sl, :], p_ref[sl, 256:384], st_s[n], lg)
                dq, dk, dv, ds_prev, dl = vjp((do_s[sl, :], ds))
                if direction == 0:
                    dq_s[sl, :] = dq
                    dk_s[sl, :] = dk
                    dv_s[sl, :] = dv
                else:
                    dq_s[sl, :] += dq
                    dk_s[sl, :] += dk
                    dv_s[sl, :] += dv
                return ds_prev, dlg + dl

            ds, dlg = lax.fori_loop(0, NCH, bstep, (jnp.zeros((RD, RD), F32), jnp.zeros((1, 1), F32)))
            for idx in (1, 0):
                n = order_c[idx]
                _, vjp = jax.vjp(functools.partial(_ret_state, reverse=rev), kcs[n], vcs[n], ctx_states[idx], lg)
                dk_c, dv_c, ds, dl = vjp(ds)
                dlg = dlg + dl
                dkc[n] = dk_c if dkc[n] is None else dkc[n] + dk_c
                dvc[n] = dv_c if dvc[n] is None else dvc[n] + dv_c
            drd_ref[direction:direction + 1, :] += dlg * jax.nn.sigmoid(-rdv)
        dp_ref[:, 0:128] = _rope_t(dq_s[...] * (RD ** -0.5), cos_v, sin_v).astype(BF16)
        dp_ref[:, 128:256] = _rope_t(dk_s[...], cos_v, sin_v).astype(BF16)
        dp_ref[:, 256:384] = dv_s[...].astype(BF16)
        zero = jnp.zeros((CH, RD), BF16)
        for n in (0, 1):
            rows = slice(n * CH, (n + 1) * CH)
            dpc_ref[rows, 0:128] = zero
            dpc_ref[rows, 128:256] = dkc[n].astype(BF16)
            dpc_ref[rows, 256:384] = dvc[n].astype(BF16)
            dpc_ref[rows, 384:512] = zero

    return pl.pallas_call(
        body, name="retention_bwd", grid=(RH, nb),
        in_specs=[sp["pret"], sp["pretc"], sp["head"], sp["head"], sp["rd"], sp["gn"], sp["tab"], sp["tab"]],
        out_specs=[
            pl.BlockSpec((None, SEQ, 512), lambda h, b: (b, 0, h)),
            pl.BlockSpec((None, LC, 512), lambda h, b: (b, 0, h)),
            pl.BlockSpec((None, 2, 1), lambda h, b: (h, 0, 0)),
            pl.BlockSpec((None, 1, RD), lambda h, b: (h, 0, 0)),
        ],
        out_shape=[
            jax.ShapeDtypeStruct((nb, SEQ, IN_W), BF16),
            jax.ShapeDtypeStruct((nb, LC, IN_W), BF16),
            jax.ShapeDtypeStruct((RH, 2, 1), F32),
            jax.ShapeDtypeStruct((RH, 1, RD), F32),
        ],
        scratch_shapes=[pltpu.VMEM((SEQ, RD), F32)] * 6 + [pltpu.VMEM((NCH, RD, RD), F32)],
        compiler_params=_cp(("arbitrary", "arbitrary")),
    )(pret, pretc, o_all, dmixin, rd, gn, cos, sin)


def _row_class_dr(cls, kr):
    return kr + 3 if cls == 4 else kr - cls + 7


def _bias_patterns(rpb):
    per_cls = []
    for cls in range(8):
        parts = [jnp.pad(rpb[:, _row_class_dr(cls, kr), :], ((0, 0), (0, 33))) for kr in range(8)]
        per_cls.append(jnp.concatenate(parts, axis=-1))
    return jnp.stack(per_cls, axis=1).reshape(NPAIR, 2, 8, 1, 512)


def _bias_patterns_t(dpat):
    dpat = dpat.reshape(8, 8, 512)
    rows = []
    for dr in range(15):
        acc = jnp.zeros((8, 31), F32)
        for cls in range(8):
            for kr in range(8):
                if _row_class_dr(cls, kr) == dr:
                    acc = acc + dpat[:, cls, kr * 64:kr * 64 + 31]
        rows.append(acc)
    return jnp.stack(rows, axis=1)


def _barrel(x, left):
    row = lax.broadcasted_iota(jnp.int32, x.shape, 0)
    n = x.shape[1]
    for bit in range(6):
        s = 1 << bit
        x = jnp.where(((row >> bit) & 1) == 1, pltpu.roll(x, (n - s) if left else s, 1), x)
    return x


def _bias_from_pattern(pat):
    x = _barrel(pltpu.roll(jnp.broadcast_to(pat, (GW, 512)), 512 - 15, 1), left=False)
    qc = lax.broadcasted_iota(jnp.int32, (GW, 512), 0)
    kc = lax.broadcasted_iota(jnp.int32, (GW, 512), 1) & 63
    start = jnp.clip(qc - 8, 0, GW - 16)
    return jnp.where((kc >= start) & (kc < start + 16), x, NEG)


def _pattern_grad(ds_acc):
    return jnp.sum(pltpu.roll(_barrel(ds_acc, left=True), 15, 1), axis=0, keepdims=True)


def _na_row(r):
    rs = jnp.clip(r - 4, 0, 24)
    cls = jnp.where(r < 4, r, jnp.where(r > 28, r - 24, 4))
    return pl.ds(pl.multiple_of(r * GW, GW), GW), pl.ds(pl.multiple_of(rs * GW, GW), 8 * GW), cls


def _na_probs(qm, kb, kc, bias):
    s_loc = _nt(qm, kb) * 0.125 + bias
    s_ctx = _nt(qm, kc) * 0.125
    m = jnp.maximum(jnp.max(s_loc, axis=1, keepdims=True), jnp.max(s_ctx, axis=1, keepdims=True))
    e_loc, e_ctx = jnp.exp(s_loc - m), jnp.exp(s_ctx - m)
    den = jnp.sum(e_loc, axis=1, keepdims=True) + jnp.sum(e_ctx, axis=1, keepdims=True)
    return e_loc / den, e_ctx / den


def na_fwd(pna, pnac, pat, mixin):
    nb = pna.shape[0]

    def body(p_ref, pc_ref, pat_ref, alias_ref, out_ref, bias_s):
        del alias_ref
        for hh in (0, 1):
            for cls in range(8):
                bias_s[hh, cls] = _bias_from_pattern(pat_ref[hh, cls])
        lane = lax.broadcasted_iota(jnp.int32, (GW, 128), 1)
        kc, vc = pc_ref[:, 128:256], pc_ref[:, 256:384]

        def row(r, carry):
            qsl, bsl, cls = _na_row(r)
            q, kb, vb = p_ref[qsl, 0:128], p_ref[bsl, 128:256], p_ref[bsl, 256:384]
            out = jnp.zeros((GW, 128), F32)
            for hh in (0, 1):
                hm = (lane >= 64 * hh) & (lane < 64 * hh + 64)
                p_loc, p_ctx = _na_probs(jnp.where(hm, q, jnp.zeros_like(q)), kb, kc, bias_s[hh, cls])
                out = jnp.where(hm, _nn(p_loc, vb) + _nn(p_ctx, vc), out)
            out_ref[qsl, :] = out.astype(BF16)
            return carry

        lax.fori_loop(0, SEQ // GW, row, 0)

    return pl.pallas_call(
        body, name="na_fwd", grid=(nb, NPAIR),
        in_specs=[
            pl.BlockSpec((None, SEQ, 384), lambda b, p: (b, 0, p)),
            pl.BlockSpec((None, LC, 384), lambda b, p: (b, 0, p)),
            pl.BlockSpec((None, 2, 8, 1, 512), lambda b, p: (p, 0, 0, 0, 0)),
            pl.BlockSpec(memory_space=pl.ANY),
        ],
        out_specs=pl.BlockSpec((None, SEQ, 128), lambda b, p: (b, 0, 4 + p)),
        out_shape=jax.ShapeDtypeStruct((nb, SEQ, D), BF16),
        input_output_aliases={3: 0},
        scratch_shapes=[pltpu.VMEM((2, 8, GW, 512), F32)],
        compiler_params=_cp(("arbitrary", "arbitrary")),
    )(pna, pnac, pat, mixin)


def na_bwd(pna, pnac, pat, dmixin, dproj, dprojc):
    nb = pna.shape[0]

    def body(p_ref, pc_ref, pat_ref, dmix_ref, a1_ref, a2_ref, dp_ref, dpc_ref, dpat_ref,
             bias_s, dbias_s, dk_s, dv_s, dkc_s, dvc_s, res_s, resc_s):
        del a1_ref, a2_ref
        b, part = pl.program_id(1), pl.program_id(2)

        @pl.when(part == 0)
        def _():
            for hh in (0, 1):
                for cls in range(8):
                    bias_s[hh, cls] = _bias_from_pattern(pat_ref[hh, cls])

            @pl.when(b == 0)
            def _():
                dbias_s[...] = jnp.zeros_like(dbias_s)

            dk_s[...] = jnp.zeros_like(dk_s)
            dv_s[...] = jnp.zeros_like(dv_s)
            dkc_s[...] = jnp.zeros_like(dkc_s)
            dvc_s[...] = jnp.zeros_like(dvc_s)
            lane = lax.broadcasted_iota(jnp.int32, (GW, 128), 1)
            kc, vc = pc_ref[:, 128:256], pc_ref[:, 256:384]

            def row(r, carry):
                qsl, bsl, cls = _na_row(r)
                q, kb, vb = p_ref[qsl, 0:128], p_ref[bsl, 128:256], p_ref[bsl, 256:384]
                do = dmix_ref[qsl, :]
                dq = jnp.zeros((GW, 128), F32)
                for hh in (0, 1):
                    hm = (lane >= 64 * hh) & (lane < 64 * hh + 64)
                    qm = jnp.where(hm, q, jnp.zeros_like(q))
                    dom = jnp.where(hm, do, jnp.zeros_like(do))
                    p_loc, p_ctx = _na_probs(qm, kb, kc, bias_s[hh, cls])
                    dp_loc, dp_ctx = _nt(dom, vb), _nt(dom, vc)
                    delta = (jnp.sum(p_loc * dp_loc, axis=1, keepdims=True)
                             + jnp.sum(p_ctx * dp_ctx, axis=1, keepdims=True))
                    ds_loc, ds_ctx = p_loc * (dp_loc - delta), p_ctx * (dp_ctx - delta)
                    dbias_s[hh, cls] += ds_loc
                    dq = jnp.where(hm, (_nn(ds_loc, kb) + _nn(ds_ctx, kc)) * 0.125, dq)
                    dk_s[bsl, :] += _tn(ds_loc, qm) * 0.125
                    dv_s[bsl, :] += _tn(p_loc, dom)
                    dkc_s[...] += _tn(ds_ctx, qm) * 0.125
                    dvc_s[...] += _tn(p_ctx, dom)
                res_s[0, qsl, :] = dq.astype(BF16)
                return carry

            lax.fori_loop(0, SEQ // GW, row, 0)
            res_s[1] = dk_s[...].astype(BF16)
            res_s[2] = dv_s[...].astype(BF16)
            resc_s[0] = jnp.zeros((LC, 128), BF16)
            resc_s[1] = dkc_s[...].astype(BF16)
            resc_s[2] = dvc_s[...].astype(BF16)

            @pl.when(b == nb - 1)
            def _():
                for hh in (0, 1):
                    for cls in range(8):
                        dpat_ref[hh, cls] = _pattern_grad(dbias_s[hh, cls])

        dp_ref[...] = res_s[part]
        dpc_ref[...] = resc_s[part]

    return pl.pallas_call(
        body, name="na_bwd", grid=(NPAIR, nb, 3),
        in_specs=[
            pl.BlockSpec((None, SEQ, 384), lambda p, b, s: (b, 0, p)),
            pl.BlockSpec((None, LC, 384), lambda p, b, s: (b, 0, p)),
            pl.BlockSpec((None, 2, 8, 1, 512), lambda p, b, s: (p, 0, 0, 0, 0)),
            pl.BlockSpec((None, SEQ, 128), lambda p, b, s: (b, 0, 4 + p)),
            pl.BlockSpec(memory_space=pl.ANY),
            pl.BlockSpec(memory_space=pl.ANY),
        ],
        out_specs=[
            pl.BlockSpec((None, SEQ, 128), lambda p, b, s: (b, 0, 16 + 3 * p + s)),
            pl.BlockSpec((None, LC, 128), lambda p, b, s: (b, 0, 16 + 3 * p + s)),
            pl.BlockSpec((None, 2, 8, 1, 512), lambda p, b, s: (p, 0, 0, 0, 0)),
        ],
        out_shape=[
            jax.ShapeDtypeStruct((nb, SEQ, IN_W), BF16),
            jax.ShapeDtypeStruct((nb, LC, IN_W), BF16),
            jax.ShapeDtypeStruct((NPAIR, 2, 8, 1, 512), F32),
        ],
        input_output_aliases={4: 0, 5: 1},
        scratch_shapes=[
            pltpu.VMEM((2, 8, GW, 512), F32), pltpu.VMEM((2, 8, GW, 512), F32),
            pltpu.VMEM((SEQ, 128), F32), pltpu.VMEM((SEQ, 128), F32),
            pltpu.VMEM((LC, 128), F32), pltpu.VMEM((LC, 128), F32),
            pltpu.VMEM((3, SEQ, 128), BF16), pltpu.VMEM((3, LC, 128), BF16),
        ],
        compiler_params=_cp(("arbitrary", "arbitrary", "arbitrary")),
    )(pna, pnac, pat, dmixin, dproj, dprojc)


def tail_fwd_bwd(x, mixin, tgt, mod3, g_post_mix, g_pre_mlp, g_post_mlp, wout, w1, w2):
    nb = x.shape[0]

    def body(x_ref, mi_ref, tgt_ref, mod_ref, gpm_ref, gpl_ref, gpo_ref, wo_ref, w1_ref, w2_ref,
             dx_ref, dmix_ref, h2_ref, du_ref, a_ref, dm_ref, dmi_ref, dmod_ref, dg_ref, loss_ref):
        b, t = pl.program_id(0), pl.program_id(1)
        gt1, sh2, sc2, gt2 = mod_ref[2:3, :], mod_ref[3:4, :], mod_ref[4:5, :], mod_ref[5:6, :]
        mix = jnp.dot(mi_ref[...], wo_ref[...], preferred_element_type=F32)
        (x1, h2), vjp_a = jax.vjp(_post_mix, x_ref[...], mix, gt1, sc2, sh2, gpm_ref[...], gpl_ref[...])
        h2b = h2.astype(BF16)
        h2_ref[...] = h2b
        m = jnp.zeros((TN, D), F32)
        relus = []
        for j in range(4):
            cols = slice(j * D, (j + 1) * D)
            r = jnp.maximum(jnp.dot(h2b, w1_ref[:, cols], preferred_element_type=F32), 0.0)
            ab = (r * r).astype(BF16)
            a_ref[:, cols] = ab
            m = m + jnp.dot(ab, w2_ref[cols, :], preferred_element_type=F32)
            relus.append(r)
        loss, vjp_b = jax.vjp(_head_loss, x1, m, gt2, gpo_ref[...], tgt_ref[...])
        dx1, dm, dgt2, dgpo, _ = vjp_b(jnp.ones((1, 1), F32))
        dmb = dm.astype(BF16)
        dm_ref[...] = dmb
        dh2 = jnp.zeros((TN, D), F32)
        for j in range(4):
            cols = slice(j * D, (j + 1) * D)
            da = lax.dot_general(dmb, w2_ref[cols, :], (((1,), (1,)), ((), ())), preferred_element_type=F32)
            dub = (da * (2.0 * relus[j])).astype(BF16)
            du_ref[:, cols] = dub
            dh2 = dh2 + lax.dot_general(dub, w1_ref[:, cols], (((1,), (1,)), ((), ())), preferred_element_type=F32)
        dx, dmix, dgt1, dsc2, dsh2, dgpm, dgpl = vjp_a((dx1, dh2))
        dx_ref[...] = dx
        dmixb = dmix.astype(BF16)
        dmix_ref[...] = dmixb
        dmi_ref[...] = lax.dot_general(dmixb, wo_ref[...], (((1,), (1,)), ((), ())),
                                       preferred_element_type=F32).astype(BF16)

        @pl.when(t == 0)
        def _():
            dmod_ref[...] = jnp.zeros_like(dmod_ref)

        @pl.when((t == 0) & (b == 0))
        def _():
            dg_ref[...] = jnp.zeros_like(dg_ref)
            loss_ref[...] = jnp.zeros_like(loss_ref)

        dmod_ref[2:3, :] += dgt1
        dmod_ref[3:4, :] += dsh2
        dmod_ref[4:5, :] += dsc2
        dmod_ref[5:6, :] += dgt2
        dg_ref[0:1, :] += dgpm
        dg_ref[1:2, :] += dgpl
        dg_ref[2:3, :] += dgpo
        loss_ref[...] += jnp.broadcast_to(loss, loss_ref.shape)

    tok = lambda b, t: (b, t, 0)
    const = lambda b, t: (0, 0)
    vec = pl.BlockSpec((1, D), const)
    return pl.pallas_call(
        body, name="tail_fwd_bwd", grid=(nb, SEQ // TN),
        in_specs=[
            pl.BlockSpec((None, TN, D), tok), pl.BlockSpec((None, TN, D), tok), pl.BlockSpec((None, TN, D), tok),
            pl.BlockSpec((None, 6, D), lambda b, t: (b, 0, 0)), vec, vec, vec,
            pl.BlockSpec((D, D), const, pipeline_mode=pl.Buffered(1)),
            pl.BlockSpec((D, DFF), const, pipeline_mode=pl.Buffered(1)),
            pl.BlockSpec((DFF, D), const, pipeline_mode=pl.Buffered(1)),
        ],
        out_specs=[
            pl.BlockSpec((None, TN, D), tok), pl.BlockSpec((None, TN, D), tok), pl.BlockSpec((None, TN, D), tok),
            pl.BlockSpec((None, TN, DFF), tok), pl.BlockSpec((None, TN, DFF), tok), pl.BlockSpec((None, TN, D), tok),
            pl.BlockSpec((None, TN, D), tok),
            pl.BlockSpec((None, 6, D), lambda b, t: (b, 0, 0)),
            pl.BlockSpec((8, D), const), pl.BlockSpec((8, 128), const),
        ],
        out_shape=[
            jax.ShapeDtypeStruct((nb, SEQ, D), F32), jax.ShapeDtypeStruct((nb, SEQ, D), BF16),
            jax.ShapeDtypeStruct((nb, SEQ, D), BF16), jax.ShapeDtypeStruct((nb, SEQ, DFF), BF16),
            jax.ShapeDtypeStruct((nb, SEQ, DFF), BF16), jax.ShapeDtypeStruct((nb, SEQ, D), BF16),
            jax.ShapeDtypeStruct((nb, SEQ, D), BF16),
            jax.ShapeDtypeStruct((nb, 6, D), F32), jax.ShapeDtypeStruct((8, D), F32),
            jax.ShapeDtypeStruct((8, 128), F32),
        ],
        compiler_params=_cp(("arbitrary", "arbitrary")),
    )(x, mixin, tgt, mod3, g_post_mix, g_pre_mlp, g_post_mlp, wout, w1, w2)


def weight_grad(pairs, name, tm=512, tn=512, tk=512):
    m, n = pairs[0][0].shape[1], pairs[0][1].shape[1]
    tn = min(tn, n)
    steps = [xa.shape[0] // tk for xa, _ in pairs]
    total = sum(steps)
    offs = [sum(steps[:i]) for i in range(len(pairs))]

    def body(*refs):
        out_ref, acc = refs[-2], refs[-1]
        k = pl.program_id(2)

        @pl.when(k == 0)
        def _():
            acc[...] = jnp.zeros_like(acc)

        for i in range(len(pairs)):
            @pl.when((k >= offs[i]) & (k < offs[i] + steps[i]))
            def _(i=i):
                acc[...] += lax.dot_general(refs[2 * i][...], refs[2 * i + 1][...], (((0,), (0,)), ((), ())),
                                            preferred_element_type=F32)

        @pl.when(k == total - 1)
        def _():
            out_ref[...] = acc[...]

    in_specs, args = [], []
    for i, (xa, ya) in enumerate(pairs):
        clamp = lambda k, i=i: jnp.clip(k - offs[i], 0, steps[i] - 1)
        in_specs.append(pl.BlockSpec((tk, tm), lambda a, c, k, clamp=clamp: (clamp(k), a)))
        in_specs.append(pl.BlockSpec((tk, tn), lambda a, c, k, clamp=clamp: (clamp(k), c)))
        args += [xa, ya]
    return pl.pallas_call(
        body, name=name, grid=(m // tm, n // tn, total), in_specs=in_specs,
        out_specs=pl.BlockSpec((tm, tn), lambda a, c, k: (a, c)),
        out_shape=jax.ShapeDtypeStruct((m, n), F32),
        scratch_shapes=[pltpu.VMEM((tm, tn), F32)],
        compiler_params=_cp(("arbitrary", "arbitrary", "arbitrary")),
    )(*args)


def _perm_block(t):
    u = t - 16
    return jnp.where(t < 16, 4 * (t % 4) + t // 4, 16 + 3 * (u % 4) + u // 4)


def unpack_w_in(blocks):
    def body(i_ref, o_ref):
        o_ref[...] = i_ref[...]

    return pl.pallas_call(
        body, name="unpack_w_in", grid=(28,),
        in_specs=[pl.BlockSpec((None, D, 128), lambda t: (t // 7, 0, t % 7))],
        out_specs=pl.BlockSpec((D, 128), lambda t: (0, _perm_block(t))),
        out_shape=jax.ShapeDtypeStruct((D, IN_W), BF16),
    )(blocks)


def pack_w_in(dw):
    def body(i_ref, o_ref):
        o_ref[...] = i_ref[...].astype(BF16)

    return pl.pallas_call(
        body, name="pack_w_in", grid=(28,),
        in_specs=[pl.BlockSpec((D, 128), lambda t: (0, _perm_block(t)))],
        out_specs=pl.BlockSpec((None, D, 128), lambda t: (t // 7, 0, t % 7)),
        out_shape=jax.ShapeDtypeStruct((4, D, 896), BF16),
    )(dw)


def unpack_cols(blocks, name):
    _, r, c = blocks.shape

    def body(i_ref, o_ref):
        o_ref[...] = i_ref[...]

    return pl.pallas_call(
        body, name=name, grid=(4,),
        in_specs=[pl.BlockSpec((None, r, c), lambda j: (j, 0, 0))],
        out_specs=pl.BlockSpec((r, c), lambda j: (0, j)),
        out_shape=jax.ShapeDtypeStruct((r, 4 * c), blocks.dtype),
    )(blocks)


def pack_cols(w, name):
    r, c4 = w.shape
    c = c4 // 4

    def body(i_ref, o_ref):
        o_ref[...] = i_ref[...].astype(BF16)

    return pl.pallas_call(
        body, name=name, grid=(4,),
        in_specs=[pl.BlockSpec((r, c), lambda j: (0, j))],
        out_specs=pl.BlockSpec((None, r, c), lambda j: (j, 0, 0)),
        out_shape=jax.ShapeDtypeStruct((4, r, c), BF16),
    )(w)


def cast_rows(w, name):
    r, c = w.shape

    def body(i_ref, o_ref):
        o_ref[...] = i_ref[...].astype(BF16)

    return pl.pallas_call(
        body, name=name, grid=(8,),
        in_specs=[pl.BlockSpec((r // 8, c), lambda j: (j, 0))],
        out_specs=pl.BlockSpec((r // 8, c), lambda j: (j, 0)),
        out_shape=jax.ShapeDtypeStruct((r, c), BF16),
    )(w)


def _place():
    return lax.axis_index("x"), lax.axis_index("y"), lax.axis_index("c")


def all_gather8(blocks, name):
    na = len(blocks)

    def body(*refs):
        ins, outs = refs[:na], refs[na:2 * na]
        send_sems, recv_sems, local_sem = refs[2 * na:]
        x, y, c = _place()
        me, sibling = (x, y, c), (x, y, 1 - c)
        chips = [(1 - x, y), (x, 1 - y), (1 - x, 1 - y)]

        def slot(o_ref, px, py, pc):
            return o_ref.at[4 * px + 2 * py + pc]

        def copy(a, k, block, to, src=None):
            return pltpu.make_async_remote_copy(
                src_ref=slot(outs[a], *block) if src is None else src, dst_ref=slot(outs[a], *block),
                send_sem=send_sems.at[a, k], recv_sem=recv_sems.at[a, k], device_id=to, device_id_type=MESH)

        mine = [pltpu.make_async_copy(ins[a], slot(outs[a], *me), local_sem.at[a]) for a in range(na)]
        for cp in mine:
            cp.start()
        first = []
        for a in range(na):
            first.append(copy(a, 0, me, sibling, src=ins[a]))
            first += [copy(a, 1 + j, me, (*chip, c), src=ins[a]) for j, chip in enumerate(chips)]
        for cp in first:
            cp.start()
        passed = []
        for j, chip in enumerate(chips):
            for a in range(na):
                copy(a, 1 + j, (*chip, c), me).wait_recv()
                cp = copy(a, 4 + j, (*chip, c), sibling)
                cp.start()
                passed.append(cp)
        for a in range(na):
            copy(a, 0, sibling, me).wait_recv()
            for j, chip in enumerate(chips):
                copy(a, 4 + j, (*chip, 1 - c), me).wait_recv()
        for cp in first + passed:
            cp.wait_send()
        for cp in mine:
            cp.wait()

    hbm = pl.BlockSpec(memory_space=pl.ANY)
    return pl.pallas_call(
        body, name=name, in_specs=[hbm] * na, out_specs=[hbm] * na,
        out_shape=[jax.ShapeDtypeStruct((8,) + b.shape, b.dtype) for b in blocks],
        scratch_shapes=[pltpu.SemaphoreType.DMA((na, 7)), pltpu.SemaphoreType.DMA((na, 7)),
                        pltpu.SemaphoreType.DMA((na,))],
    )(*blocks)


def sibling_exchange(arrays, name):
    na = len(arrays)

    def body(*refs):
        ins, outs = refs[:na], refs[na:2 * na]
        send_sems, recv_sems = refs[2 * na:]
        x, y, c = _place()
        cps = [pltpu.make_async_remote_copy(
            src_ref=ins[a], dst_ref=outs[a], send_sem=send_sems.at[a], recv_sem=recv_sems.at[a],
            device_id=(x, y, 1 - c), device_id_type=MESH) for a in range(na)]
        for cp in cps:
            cp.start()
        for cp in cps:
            cp.wait()

    hbm = pl.BlockSpec(memory_space=pl.ANY)
    return pl.pallas_call(
        body, name=name, in_specs=[hbm] * na, out_specs=[hbm] * na,
        out_shape=[jax.ShapeDtypeStruct(a.shape, a.dtype) for a in arrays],
        scratch_shapes=[pltpu.SemaphoreType.DMA((na,)), pltpu.SemaphoreType.DMA((na,))],
    )(*arrays)


def sibling_blocks(arrays, name):
    na = len(arrays)

    def body(*refs):
        ins, outs = refs[:na], refs[na:2 * na]
        send_sems, recv_sems = refs[2 * na:]
        x, y, c = _place()
        cps = [pltpu.make_async_remote_copy(
            src_ref=ins[a].at[2 * j + 1 - c], dst_ref=outs[a].at[j],
            send_sem=send_sems.at[a, j], recv_sem=recv_sems.at[a, j],
            device_id=(x, y, 1 - c), device_id_type=MESH) for a in range(na) for j in range(4)]
        for cp in cps:
            cp.start()
        for cp in cps:
            cp.wait()

    hbm = pl.BlockSpec(memory_space=pl.ANY)
    return pl.pallas_call(
        body, name=name, in_specs=[hbm] * na, out_specs=[hbm] * na,
        out_shape=[jax.ShapeDtypeStruct((4,) + a.shape[1:], a.dtype) for a in arrays],
        scratch_shapes=[pltpu.SemaphoreType.DMA((na, 4)), pltpu.SemaphoreType.DMA((na, 4))],
    )(*arrays)


def chip_exchange(arrays, name):
    na = len(arrays)

    def body(*refs):
        ins, outs = refs[:na], refs[na:2 * na]
        send_sems, recv_sems = refs[2 * na:]
        x, y, c = _place()
        chips = [(1 - x, y), (x, 1 - y), (1 - x, 1 - y)]
        cps = []
        for a in range(na):
            for k, (px, py) in enumerate(chips):
                cps.append(pltpu.make_async_remote_copy(
                    src_ref=ins[a].at[2 * px + py], dst_ref=outs[a].at[k],
                    send_sem=send_sems.at[a, k], recv_sem=recv_sems.at[a, k],
                    device_id=(px, py, c), device_id_type=MESH))
        for cp in cps:
            cp.start()
        for cp in cps:
            cp.wait()

    hbm = pl.BlockSpec(memory_space=pl.ANY)
    return pl.pallas_call(
        body, name=name, in_specs=[hbm] * na, out_specs=[hbm] * na,
        out_shape=[jax.ShapeDtypeStruct((3,) + a.shape[1:], a.dtype) for a in arrays],
        scratch_shapes=[pltpu.SemaphoreType.DMA((na, 3)), pltpu.SemaphoreType.DMA((na, 3))],
    )(*arrays)


def _row_tile(r):
    for cand in (512, 256, 128, 64, 32, 16, 8):
        if r % cand == 0:
            return cand
    return r


def chip_partial(mine4, landed4, name):
    _, r, ccols = mine4.shape
    tr = _row_tile(r)

    def body(g_ref, l_ref, o_ref):
        o_ref[...] = (g_ref[...].astype(F32) + l_ref[...].astype(F32)).astype(BF16)

    spec = pl.BlockSpec((None, tr, ccols), lambda j, i: (j, i, 0))
    return pl.pallas_call(
        body, name=name, grid=(4, r // tr), in_specs=[spec, spec], out_specs=spec,
        out_shape=jax.ShapeDtypeStruct((4, r, ccols), BF16),
    )(mine4, landed4)


def shard_sum(own, landed3, name):
    r, ccols = own.shape
    tr = _row_tile(r)

    def body(p_ref, l_ref, o_ref):
        acc = p_ref[...].astype(F32)
        for k in range(3):
            acc = acc + l_ref[k].astype(F32)
        o_ref[...] = acc

    return pl.pallas_call(
        body, name=name, grid=(r // tr,),
        in_specs=[pl.BlockSpec((tr, ccols), lambda i: (i, 0)), pl.BlockSpec((3, tr, ccols), lambda i: (0, i, 0))],
        out_specs=pl.BlockSpec((tr, ccols), lambda i: (i, 0)),
        out_shape=jax.ShapeDtypeStruct((r, ccols), F32),
    )(own, landed3)


def _adamw_math(w, g, m, v):
    m2 = B1 * m + (1.0 - B1) * g
    v2 = B2 * v + (1.0 - B2) * (g * g)
    m_hat = m2 / (1.0 - B1 ** STEP)
    v_hat = v2 / (1.0 - B2 ** STEP)
    return -LR * (m_hat / (jnp.sqrt(v_hat) + AEPS) + WD * w), m2, v2


def adamw_halves(w, lo, hi, m, v, name):
    r, ccols = w.shape
    hr = r // 2
    tr = _row_tile(hr)
    nt = hr // tr

    def body(w_ref, lo_ref, hi_ref, m_ref, v_ref, g_out, d_out, m_out, v_out):
        g = jnp.where(pl.program_id(0) == 0, lo_ref[...], hi_ref[...])
        d, m2, v2 = _adamw_math(w_ref[...], g, m_ref[...], v_ref[...])
        g_out[...] = g
        d_out[...] = d
        m_out[...] = m2
        v_out[...] = v2

    full = pl.BlockSpec((tr, ccols), lambda h, i: (h * nt + i, 0))
    part = pl.BlockSpec((tr, ccols), lambda h, i: (i, 0))
    return pl.pallas_call(
        body, name=name, grid=(2, nt),
        in_specs=[full, part, part, full, full], out_specs=[full] * 4,
        out_shape=[jax.ShapeDtypeStruct((r, ccols), F32)] * 4,
    )(w, lo, hi, m, v)


def adamw_plain(w, g, m, v, name):
    r, ccols = w.shape
    tr = _row_tile(r)

    def body(w_ref, g_ref, m_ref, v_ref, d_out, m_out, v_out):
        d, m2, v2 = _adamw_math(w_ref[...], g_ref[...], m_ref[...], v_ref[...])
        d_out[...] = d
        m_out[...] = m2
        v_out[...] = v2

    spec = pl.BlockSpec((tr, ccols), lambda i: (i, 0))
    return pl.pallas_call(
        body, name=name, grid=(r // tr,), in_specs=[spec] * 4, out_specs=[spec] * 3,
        out_shape=[jax.ShapeDtypeStruct((r, ccols), F32)] * 3,
    )(w, g, m, v)


def _silu(x):
    return x * jax.nn.sigmoid(x)


def mod_shard(cin, w_ada, b_shard):
    def body(c_ref, w_ref, b_ref, o_ref):
        o_ref[...] = _nn(_silu(c_ref[...]), w_ref[...]) + b_ref[...]

    return pl.pallas_call(
        body, name="mod_shard", grid=(3,),
        in_specs=[pl.BlockSpec((32, D), lambda j: (0, 0)), pl.BlockSpec((D, 512), lambda j: (0, j)),
                  pl.BlockSpec((1, 512), lambda j: (0, j))],
        out_specs=pl.BlockSpec((32, 512), lambda j: (0, j)),
        out_shape=jax.ShapeDtypeStruct((32, 1536), F32),
    )(cin, w_ada, b_shard)


def ada_grads(cin, gb, gc, w_ada):
    def body(c_ref, gb_ref, gc_ref, w_ref, gw_ref, pc_ref):
        ctx_tot = jnp.sum(gc_ref[...], axis=0, keepdims=True)
        rows = lax.broadcasted_iota(jnp.int32, (16, 512), 0)
        dm = jnp.concatenate([gb_ref[...], jnp.where(rows == 0, ctx_tot, 0.0)], axis=0)
        gw_ref[...] = _tn(_silu(c_ref[...]), dm)
        rows8 = lax.broadcasted_iota(jnp.int32, (8, 512), 0)
        part = _nt(jnp.where(rows8 == 0, ctx_tot, 0.0), w_ref[...])

        @pl.when(pl.program_id(0) == 0)
        def _():
            pc_ref[...] = jnp.zeros_like(pc_ref)

        pc_ref[...] += part

    return pl.pallas_call(
        body, name="ada_grads", grid=(3,),
        in_specs=[pl.BlockSpec((32, D), lambda j: (0, 0)), pl.BlockSpec((16, 512), lambda j: (0, j)),
                  pl.BlockSpec((8, 512), lambda j: (0, j)), pl.BlockSpec((D, 512), lambda j: (0, j))],
        out_specs=[pl.BlockSpec((D, 512), lambda j: (0, j)), pl.BlockSpec((8, D), lambda j: (0, 0))],
        out_shape=[jax.ShapeDtypeStruct((D, 1536), F32), jax.ShapeDtypeStruct((8, D), F32)],
    )(cin, gb, gc, w_ada)


SMALL_ROWS = 24


def small_update(gsm, gbf, gcf, pcg, w_pack, m_pack, v_pack):
    def body(gsm_ref, gbf_ref, gcf_ref, pcg_ref, w_ref, m_ref, v_ref, g_out, d_out, m_out, v_out, loss_out):
        g_out[...] = jnp.zeros_like(g_out)
        c_ctx = w_ref[0:1, :]
        sg = jax.nn.sigmoid(c_ctx)
        dsilu = pcg_ref[0:1, :] + pcg_ref[2:3, :] + pcg_ref[4:5, :] + pcg_ref[6:7, :]
        g_out[0:1, :] = dsilu * (sg * (1.0 + c_ctx * (1.0 - sg)))
        gb = jnp.sum(gbf_ref[...], axis=0, keepdims=True) + jnp.sum(gcf_ref[...], axis=0, keepdims=True)
        for j in range(6):
            g_out[1 + j:2 + j, :] = gb[:, j * D:(j + 1) * D]
        acc = gsm_ref[0]
        for dev in range(1, 8):
            acc = acc + gsm_ref[dev]
        g_out[7:8, :] = acc[0:1, :] + acc[1:2, :]
        g_out[8:16, :] = acc[2:10, :]
        lane = lax.broadcasted_iota(jnp.int32, (1, D), 1)
        last = acc[10:11, :]
        g_out[16:17, :] = jnp.where(lane < 8, last, 0.0)
        loss_out[...] = jnp.broadcast_to(jnp.sum(jnp.where(lane == 8, last, 0.0), axis=1, keepdims=True), (8, 128))
        g = g_out[...]
        d, m2, v2 = _adamw_math(w_ref[...], g, m_ref[...], v_ref[...])
        d_out[...] = d
        m_out[...] = m2
        v_out[...] = v2

    pack = jax.ShapeDtypeStruct((SMALL_ROWS, D), F32)
    return pl.pallas_call(
        body, name="small_update", out_shape=[pack, pack, pack, pack, jax.ShapeDtypeStruct((8, 128), F32)],
    )(gsm, gbf, gcf, pcg, w_pack, m_pack, v_pack)


def _pad_row(v, rows):
    flat = v.reshape(-1)
    return jnp.pad(flat, (0, rows * D - flat.shape[0])).reshape(rows, D)


def _pack_small(c_ctx, b_ada, g1, g2, g3, g4, ret_gn, na_rpb, ret_decay):
    parts = [_pad_row(c_ctx, 1), _pad_row(b_ada, 6), _pad_row(g1, 1), _pad_row(g2, 1), _pad_row(g3, 1),
             _pad_row(g4, 1), _pad_row(ret_gn, 1), _pad_row(na_rpb, 4), _pad_row(ret_decay, 1),
             jnp.zeros((SMALL_ROWS - 17, D), F32)]
    return jnp.concatenate(parts, axis=0)


def _unpack_small(p):
    return (p[0], p[1:7].reshape(1, 6 * D), p[7:8], p[8:9], p[9:10], p[10:11], p[11:12, :512],
            p[12:16].reshape(-1)[:8 * 15 * 31].reshape(1, 8, 15, 31), p[16, :8].reshape(1, 2, 4))


def local_step(x, ctx, tgt, mod3, g_pre_mix, g_post_mix, g_pre_mlp, g_post_mlp, ret_decay, ret_gn, na_rpb,
               wperm, wout_b, w1_b, w2_b):
    nb = x.shape[0]
    tokens = nb * SEQ
    cos, sin = _rope_tables()
    rd = ret_decay.T.reshape(RH, 2, 1)
    gn = ret_gn.reshape(RH, 1, RD)
    pat = _bias_patterns(na_rpb)
    h, pret, pna = premix_proj(x, mod3, g_pre_mix, wperm, False, "premix_proj")
    hc, pretc, pnac = premix_proj(ctx, mod3, g_pre_mix, wperm, True, "premix_proj_ctx")
    o_all, mixin = retention_fwd(pret, pretc, rd, gn, cos, sin)
    mixin = na_fwd(pna, pnac, pat, mixin)
    dx_tail, dmix, h2, du, act, dm, dmixin, dmod_t, dg_t, loss_t = tail_fwd_bwd(
        x, mixin, tgt, mod3, g_post_mix, g_pre_mlp, g_post_mlp, wout_b, w1_b, w2_b)
    dproj, dprojc, drd, dgn = retention_bwd(pret, pretc, o_all, dmixin, rd, gn, cos, sin)
    dproj, dprojc, dpat = na_bwd(pna, pnac, pat, dmixin, dproj, dprojc)
    grad_x, dmod_a, dg_a = premix_bwd(x, mod3, g_pre_mix, wperm, dproj, dx_tail, "premix_bwd")
    dmod_c, dg_c = premix_bwd(ctx, mod3, g_pre_mix, wperm, dprojc, None, "premix_bwd_ctx")
    dw_in = weight_grad([(h.reshape(tokens, D), dproj.reshape(tokens, IN_W)),
                         (hc.reshape(nb * LC, D), dprojc.reshape(nb * LC, IN_W))], "grad_w_in")
    dw_out = weight_grad([(mixin.reshape(tokens, D), dmix.reshape(tokens, D))], "grad_w_out")
    dw1 = weight_grad([(h2.reshape(tokens, D), du.reshape(tokens, DFF))], "grad_w_mlp1")
    dw2 = weight_grad([(act.reshape(tokens, DFF), dm.reshape(tokens, D))], "grad_w_mlp2")
    dmod = jnp.concatenate([jnp.concatenate([dmod_a[:, 0:2], dmod_t[:, 2:6]], axis=1), dmod_c], axis=0)
    last = jnp.pad(jnp.concatenate([drd[:, :, 0].T.reshape(8), loss_t[0, 0:1]]), (0, D - 9)).reshape(1, D)
    small = jnp.concatenate([
        dg_a[0:1], dg_c[0:1], dg_t[0:3], _pad_row(dgn, 1), _pad_row(_bias_patterns_t(dpat), 4), last], axis=0)
    return grad_x, dw_in, dw_out, dw1, dw2, dmod, small


def kernel(x, c, ctx, c_ctx, w_ada, b_ada, g_pre_mix, g_post_mix, g_pre_mlp, g_post_mlp, w_in, ret_decay, ret_gn, na_rpb, w_out, w_mlp1, w_mlp2, loss_target, m_c_ctx, m_w_ada, m_b_ada, m_g_pre_mix, m_g_post_mix, m_g_pre_mlp, m_g_post_mlp, m_w_in, m_ret_decay, m_ret_gn, m_na_rpb, m_w_out, m_w_mlp1, m_w_mlp2, v_c_ctx, v_w_ada, v_b_ada, v_g_pre_mix, v_g_post_mix, v_g_pre_mlp, v_g_post_mlp, v_w_in, v_ret_decay, v_ret_gn, v_na_rpb, v_w_out, v_w_mlp1, v_w_mlp2):
    px, py, pc = _place()
    dev = 4 * px + 2 * py + pc
    chip = 2 * px + py

    def my_half(w2d):
        rows = w2d.shape[0] // 2
        return lax.dynamic_slice_in_dim(w2d, pc * rows, rows, 0)

    halves = [my_half(w[0]).astype(BF16) for w in (w_in, w_out, w_mlp1, w_mlp2)]
    gw_in, gw_out, gw1, gw2, cg = all_gather8(halves + [jnp.pad(c, ((0, 6), (0, 0)))], "gather_weights")
    wperm = unpack_w_in(gw_in.reshape(4, D, 896))
    wout_b = gw_out.reshape(D, D)
    w1_b = unpack_cols(gw1.reshape(4, D, D), "unpack_w_mlp1")
    w2_b = gw2.reshape(DFF, D)

    cin = jnp.pad(cg[:, 0:2].reshape(16, D), ((0, 16), (0, 0))) + jnp.pad(c_ctx[None], ((16, 15), (0, 0)))
    mod_mine = mod_shard(cin, w_ada[0], lax.dynamic_slice_in_dim(b_ada, chip * 1536, 1536, 1))
    (mg,) = all_gather8([mod_mine], "gather_mod")
    mod_all = jnp.concatenate([mg[0], mg[2], mg[4], mg[6]], axis=1)
    mod3 = (jnp.pad(lax.dynamic_slice_in_dim(mod_all, 2 * dev, 2, 0), ((0, 1), (0, 0)))
            + jnp.pad(mod_all[16:17], ((2, 0), (0, 0)))).reshape(3, 6, D)

    grad_x, dw_in, dw_out, dw1, dw2, dmod, small = local_step(
        x, ctx, loss_target, mod3, g_pre_mix, g_post_mix, g_pre_mlp, g_post_mlp, ret_decay[0], ret_gn, na_rpb[0],
        wperm, wout_b, w1_b, w2_b)

    g8 = [pack_w_in(dw_in).reshape(8, 512, 896), cast_rows(dw_out, "cast_g_w_out").reshape(8, 128, D),
          pack_cols(dw1, "pack_g_w_mlp1").reshape(8, 512, D), cast_rows(dw2, "cast_g_w_mlp2").reshape(8, 512, D)]
    names = ["w_in", "w_out", "w_mlp1", "w_mlp2"]
    landed = sibling_blocks(g8, "rs_sibling")
    partial = [chip_partial(lax.dynamic_index_in_dim(g.reshape(4, 2, *g.shape[1:]), pc, 1, keepdims=False), l,
                            "rs_chip_sum_" + n) for g, l, n in zip(g8, landed, names)]
    landed3 = chip_exchange(partial, "rs_chips")
    mine = [shard_sum(lax.dynamic_index_in_dim(p, chip, 0, keepdims=False), l, "rs_shard_sum_" + n)
            for p, l, n in zip(partial, landed3, names)]
    theirs = sibling_exchange(mine, "rs_halves")
    big = []
    for a, (w, m, v) in enumerate(((w_in, m_w_in, v_w_in), (w_out, m_w_out, v_w_out),
                                   (w_mlp1, m_w_mlp1, v_w_mlp1), (w_mlp2, m_w_mlp2, v_w_mlp2))):
        lo = jnp.where(pc == 0, mine[a], theirs[a])
        hi = jnp.where(pc == 0, theirs[a], mine[a])
        big.append([r[None] for r in adamw_halves(w[0], lo, hi, m[0], v[0], "adamw_" + names[a])])

    pay = jnp.concatenate([dmod.reshape(18, D), small, jnp.zeros((3, D), F32)], axis=0)
    (gs,) = all_gather8([pay], "gather_small")
    gbf = gs[:, 0:12].reshape(16, 6 * D)
    gcf = gs[:, 12:18].reshape(8, 6 * D)
    gw_ada, pc_part = ada_grads(cin, lax.dynamic_slice_in_dim(gbf, chip * 1536, 1536, 1),
                                lax.dynamic_slice_in_dim(gcf, chip * 1536, 1536, 1), w_ada[0])
    (pcg,) = all_gather8([pc_part], "gather_c_ctx")
    d_ada, m_ada, v_ada = adamw_plain(w_ada[0], gw_ada, m_w_ada[0], v_w_ada[0], "adamw_w_ada")
    w_pack = _pack_small(c_ctx, b_ada, g_pre_mix, g_post_mix, g_pre_mlp, g_post_mlp, ret_gn, na_rpb, ret_decay)
    m_pack = _pack_small(m_c_ctx, m_b_ada, m_g_pre_mix, m_g_post_mix, m_g_pre_mlp, m_g_post_mlp, m_ret_gn, m_na_rpb,
                         m_ret_decay)
    v_pack = _pack_small(v_c_ctx, v_b_ada, v_g_pre_mix, v_g_post_mix, v_g_pre_mlp, v_g_post_mlp, v_ret_gn, v_na_rpb,
                         v_ret_decay)
    g_pack, d_pack, m2_pack, v2_pack, loss8 = small_update(gs[:, 18:29], gbf, gcf, pcg[:, 0], w_pack, m_pack, v_pack)

    def leaves(pack, ada, idx):
        s_c, s_b, s_g1, s_g2, s_g3, s_g4, s_gn, s_rpb, s_rd = _unpack_small(pack)
        return [s_c, ada[None], s_b, s_g1, s_g2, s_g3, s_g4, big[0][idx], s_rd, s_gn, s_rpb,
                big[1][idx], big[2][idx], big[3][idx]]

    return (loss8[0, 0], grad_x, *leaves(g_pack, gw_ada, 0), *leaves(d_pack, d_ada, 1),
            *leaves(m2_pack, m_ada, 2), *leaves(v2_pack, v_ada, 3))
```

```python
import functools

import jax
import jax.numpy as jnp
from jax import lax
from jax.experimental import pallas as pl
from jax.experimental.pallas import tpu as pltpu

F32, BF16 = jnp.float32, jnp.bfloat16
D = 1024
SEQ = 2048
LC = 256
GW = 64
RH, RD, CH = 4, 128, 128
NPAIR = 4
IN_W = 3584
RET_W = 2048
DFF = 4096
EPS = 1e-6
NEG = -1e30
TN = 256
NCH = SEQ // CH
LR, B1, B2, AEPS, WD, STEP = 0.001, 0.9, 0.999, 1e-08, 0.01, 10
MESH = pl.DeviceIdType.MESH
VMEM_LIMIT = 56 * 1024 * 1024


def _cp(sem=None):
    return pltpu.CompilerParams(dimension_semantics=sem, vmem_limit_bytes=VMEM_LIMIT)


def _nn(a, b):
    return jnp.dot(a.astype(BF16), b.astype(BF16), preferred_element_type=F32)


def _nt(a, b):
    return lax.dot_general(a.astype(BF16), b.astype(BF16), (((1,), (1,)), ((), ())), preferred_element_type=F32)


def _tn(a, b):
    return lax.dot_general(a.astype(BF16), b.astype(BF16), (((0,), (0,)), ((), ())), preferred_element_type=F32)


@jax.custom_vjp
def mm_nn(a, b):
    return _nn(a, b)


@jax.custom_vjp
def mm_nt(a, b):
    return _nt(a, b)


@jax.custom_vjp
def mm_tn(a, b):
    return _tn(a, b)


mm_nn.defvjp(lambda a, b: (_nn(a, b), (a, b)), lambda r, g: (_nt(g, r[1]), _tn(r[0], g)))
mm_nt.defvjp(lambda a, b: (_nt(a, b), (a, b)), lambda r, g: (_nn(g, r[1]), _tn(g, r[0])))
mm_tn.defvjp(lambda a, b: (_tn(a, b), (a, b)), lambda r, g: (_nt(r[1], g), _nn(r[0], g)))


def _rms(x):
    return x * lax.rsqrt(jnp.mean(x * x, axis=-1, keepdims=True) + EPS)


def _rms_mod(x, g, sc, sh):
    return (_rms(x) * g) * (1.0 + sc) + sh


def _post_mix(x, mix, gt1, sc2, sh2, g_post_mix, g_pre_mlp):
    x1 = x + gt1 * (_rms(mix) * g_post_mix)
    return x1, _rms_mod(x1, g_pre_mlp, sc2, sh2)


def _head_loss(x1, m, gt2, g_post_mlp, tgt):
    err = x1 + gt2 * (_rms(m) * g_post_mlp) - tgt
    return 0.5 * jnp.sum(jnp.mean(err * err, axis=-1, keepdims=True), axis=0, keepdims=True)


def _ln_gate(o, g, w):
    mu = jnp.mean(o, axis=-1, keepdims=True)
    var = jnp.mean(jnp.square(o - mu), axis=-1, keepdims=True)
    y = (o - mu) * lax.rsqrt(var + EPS)
    return (y * w) * (g * jax.nn.sigmoid(g))


def _swap32(x):
    lane = lax.broadcasted_iota(jnp.int32, x.shape, 1)
    return jnp.where((lane & 32) == 0, pltpu.roll(x, 96, 1), pltpu.roll(x, 32, 1))


def _rope(x, cos, sin):
    return x * cos + _swap32(x) * sin


def _rope_t(g, cos, sin):
    return g * cos + _swap32(g * sin)


def _rope_tables():
    tok = jnp.arange(SEQ)
    pos_r = (tok // GW).astype(F32)
    pos_c = (tok % GW).astype(F32)
    inv = 10000.0 ** (-jnp.arange(32, dtype=F32) / 32)
    ar = pos_r[:, None] * inv[None, :]
    ac = pos_c[:, None] * inv[None, :]
    cos = jnp.concatenate([jnp.cos(ar), jnp.cos(ar), jnp.cos(ac), jnp.cos(ac)], axis=-1)
    sin = jnp.concatenate([-jnp.sin(ar), jnp.sin(ar), -jnp.sin(ac), jnp.sin(ac)], axis=-1)
    return cos, sin


def _fiota(shape, dim):
    return lax.broadcasted_iota(jnp.int32, shape, dim).astype(F32)


def _ret_state(k, v, s, lg, reverse):
    pos = _fiota((CH, 1), 0)
    b_exp = pos if reverse else (CH - 1.0 - pos)
    return jnp.exp(lg * CH) * s + mm_tn(k * jnp.exp(lg * b_exp), v)


def _ret_chunk(q, k, v, s, lg, reverse):
    i = _fiota((CH, CH), 0)
    j = _fiota((CH, CH), 1)
    diff = (j - i) if reverse else (i - j)
    mask = (diff > 0) if reverse else (diff >= 0)
    decay = jnp.where(mask, jnp.exp(lg * jnp.where(mask, diff, 0.0)), 0.0)
    pos = _fiota((CH, 1), 0)
    a_exp = (CH - pos) if reverse else (pos + 1.0)
    o = mm_nn(mm_nt(q, k) * decay, v) + mm_nn(q * jnp.exp(lg * a_exp), s)
    return o, _ret_state(k, v, s, lg, reverse)


def premix_proj(xin, mod3, g_pre, wperm, is_ctx, name):
    nb, length, _ = xin.shape
    tn = min(TN, length)

    def body(x_ref, mod_ref, g_ref, w_ref, h_ref, pret_ref, pna_ref):
        h = _rms_mod(x_ref[...], g_ref[...], mod_ref[1:2, :], mod_ref[0:1, :])
        hb = h.astype(BF16)
        h_ref[...] = hb
        pret_ref[...] = jnp.dot(hb, w_ref[:, :RET_W], preferred_element_type=F32)
        pna_ref[...] = jnp.dot(hb, w_ref[:, RET_W:], preferred_element_type=F32).astype(BF16)

    return pl.pallas_call(
        body, name=name, grid=(nb, length // tn),
        in_specs=[
            pl.BlockSpec((None, tn, D), lambda b, t: (b, t, 0)),
            pl.BlockSpec((None, 6, D), (lambda b, t: (2, 0, 0)) if is_ctx else (lambda b, t: (b, 0, 0))),
            pl.BlockSpec((1, D), lambda b, t: (0, 0)),
            pl.BlockSpec((D, IN_W), lambda b, t: (0, 0), pipeline_mode=pl.Buffered(1)),
        ],
        out_specs=[
            pl.BlockSpec((None, tn, D), lambda b, t: (b, t, 0)),
            pl.BlockSpec((None, tn, RET_W), lambda b, t: (b, t, 0)),
            pl.BlockSpec((None, tn, IN_W - RET_W), lambda b, t: (b, t, 0)),
        ],
        out_shape=[
            jax.ShapeDtypeStruct((nb, length, D), BF16),
            jax.ShapeDtypeStruct((nb, length, RET_W), F32),
            jax.ShapeDtypeStruct((nb, length, IN_W - RET_W), BF16),
        ],
        compiler_params=_cp(("arbitrary", "arbitrary")),
    )(xin, mod3, g_pre, wperm)


def premix_bwd(xin, mod3, g_pre, wperm, dproj, dx_tail, name):
    nb, length, _ = xin.shape
    tn = min(TN, length)
    is_ctx = dx_tail is None

    def body(*refs):
        if is_ctx:
            x_ref, mod_ref, g_ref, w_ref, dp_ref, dmod_ref, dg_ref = refs
        else:
            x_ref, mod_ref, g_ref, w_ref, dp_ref, dxt_ref, dx_ref, dmod_ref, dg_ref = refs
        b, t = pl.program_id(0), pl.program_id(1)
        dh = lax.dot_general(dp_ref[...], w_ref[...], (((1,), (1,)), ((), ())), preferred_element_type=F32)
        _, vjp = jax.vjp(_rms_mod, x_ref[...], g_ref[...], mod_ref[1:2, :], mod_ref[0:1, :])
        dx, dg, dsc, dsh = vjp(dh)
        if not is_ctx:
            dx_ref[...] = dx + dxt_ref[...]

        @pl.when((t == 0) & ((b == 0) if is_ctx else True))
        def _():
            dmod_ref[...] = jnp.zeros_like(dmod_ref)

        @pl.when((t == 0) & (b == 0))
        def _():
            dg_ref[...] = jnp.zeros_like(dg_ref)

        dmod_ref[0:1, :] += dsh
        dmod_ref[1:2, :] += dsc
        dg_ref[0:1, :] += dg

    tok = lambda b, t: (b, t, 0)
    in_specs = [
        pl.BlockSpec((None, tn, D), tok),
        pl.BlockSpec((None, 6, D), (lambda b, t: (2, 0, 0)) if is_ctx else (lambda b, t: (b, 0, 0))),
        pl.BlockSpec((1, D), lambda b, t: (0, 0)),
        pl.BlockSpec((D, IN_W), lambda b, t: (0, 0), pipeline_mode=pl.Buffered(1)),
        pl.BlockSpec((None, tn, IN_W), tok),
    ]
    args = [xin, mod3, g_pre, wperm, dproj]
    out_specs = [
        pl.BlockSpec((None, 6, D), (lambda b, t: (0, 0, 0)) if is_ctx else (lambda b, t: (b, 0, 0))),
        pl.BlockSpec((8, D), lambda b, t: (0, 0)),
    ]
    out_shape = [jax.ShapeDtypeStruct((1 if is_ctx else nb, 6, D), F32), jax.ShapeDtypeStruct((8, D), F32)]
    if not is_ctx:
        in_specs.append(pl.BlockSpec((None, tn, D), tok))
        args.append(dx_tail)
        out_specs.insert(0, pl.BlockSpec((None, tn, D), tok))
        out_shape.insert(0, jax.ShapeDtypeStruct((nb, length, D), F32))
    return pl.pallas_call(
        body, name=name, grid=(nb, length // tn), in_specs=in_specs, out_specs=out_specs, out_shape=out_shape,
        compiler_params=_cp(("arbitrary", "arbitrary")),
    )(*args)


def _ret_specs(order):
    def im(f):
        return lambda *g: f(*order(*g))
    return dict(
        pret=pl.BlockSpec((None, SEQ, 512), im(lambda b, h: (b, 0, h))),
        pretc=pl.BlockSpec((None, LC, 512), im(lambda b, h: (b, 0, h))),
        rd=pl.BlockSpec((None, 2, 1), im(lambda b, h: (h, 0, 0))),
        gn=pl.BlockSpec((None, 1, RD), im(lambda b, h: (h, 0, 0))),
        tab=pl.BlockSpec((SEQ, RD), im(lambda b, h: (0, 0))),
        head=pl.BlockSpec((None, SEQ, RD), im(lambda b, h: (b, 0, h))),
    )


def retention_fwd(pret, pretc, rd, gn, cos, sin):
    nb = pret.shape[0]
    sp = _ret_specs(lambda b, h: (b, h))

    def body(p_ref, pc_ref, rd_ref, gn_ref, cos_ref, sin_ref, o_ref, mix_ref, q_s, k_s, o_s):
        cos_v, sin_v = cos_ref[...], sin_ref[...]
        q_s[...] = _rope(p_ref[:, 0:128], cos_v, sin_v) * (RD ** -0.5)
        k_s[...] = _rope(p_ref[:, 128:256], cos_v, sin_v)
        for direction in (0, 1):
            rev = direction == 1
            lg = jax.nn.log_sigmoid(rd_ref[direction:direction + 1, :])
            s = jnp.zeros((RD, RD), F32)
            for n in ((1, 0) if rev else (0, 1)):
                s = _ret_state(pc_ref[n * CH:(n + 1) * CH, 128:256], pc_ref[n * CH:(n + 1) * CH, 256:384], s, lg, rev)

            def step(t, s, rev=rev, lg=lg):
                n = (NCH - 1 - t) if rev else t
                sl = pl.ds(pl.multiple_of(n * CH, CH), CH)
                o, s2 = _ret_chunk(q_s[sl, :], k_s[sl, :], p_ref[sl, 256:384], s, lg, rev)
                if rev:
                    o_s[sl, :] += o
                else:
                    o_s[sl, :] = o
                return s2

            lax.fori_loop(0, NCH, step, s)
        o = o_s[...]
        o_ref[...] = o
        mix_ref[...] = _ln_gate(o, p_ref[:, 384:512], gn_ref[...]).astype(BF16)

    return pl.pallas_call(
        body, name="retention_fwd", grid=(nb, RH),
        in_specs=[sp["pret"], sp["pretc"], sp["rd"], sp["gn"], sp["tab"], sp["tab"]],
        out_specs=[sp["head"], sp["head"]],
        out_shape=[jax.ShapeDtypeStruct((nb, SEQ, RH * RD), F32), jax.ShapeDtypeStruct((nb, SEQ, D), BF16)],
        scratch_shapes=[pltpu.VMEM((SEQ, RD), F32)] * 3,
        compiler_params=_cp(("arbitrary", "arbitrary")),
    )(pret, pretc, rd, gn, cos, sin)


def retention_bwd(pret, pretc, o_all, dmixin, rd, gn, cos, sin):
    nb = pret.shape[0]
    sp = _ret_specs(lambda h, b: (b, h))

    def body(p_ref, pc_ref, o_ref, dmix_ref, rd_ref, gn_ref, cos_ref, sin_ref,
             dp_ref, dpc_ref, drd_ref, dgn_ref, q_s, k_s, do_s, dq_s, dk_s, dv_s, st_s):
        b = pl.program_id(1)
        cos_v, sin_v = cos_ref[...], sin_ref[...]
        q_s[...] = _rope(p_ref[:, 0:128], cos_v, sin_v) * (RD ** -0.5)
        k_s[...] = _rope(p_ref[:, 128:256], cos_v, sin_v)
        _, gate_vjp = jax.vjp(_ln_gate, o_ref[...], p_ref[:, 384:512], gn_ref[...])
        do, dg, dgn = gate_vjp(dmix_ref[...].astype(F32))
        do_s[...] = do
        dp_ref[:, 384:512] = dg.astype(BF16)

        @pl.when(b == 0)
        def _():
            drd_ref[...] = jnp.zeros_like(drd_ref)
            dgn_ref[...] = jnp.zeros_like(dgn_ref)

        dgn_ref[...] += dgn
        dkc = [None, None]
        dvc = [None, None]
        for direction in (0, 1):
            rev = direction == 1
            rdv = rd_ref[direction:direction + 1, :]
            lg = jax.nn.log_sigmoid(rdv)
            order_c = (1, 0) if rev else (0, 1)
            kcs = [pc_ref[n * CH:(n + 1) * CH, 128:256] for n in (0, 1)]
            vcs = [pc_ref[n * CH:(n + 1) * CH, 256:384] for n in (0, 1)]
            s = jnp.zeros((RD, RD), F32)
            ctx_states = []
            for n in order_c:
                ctx_states.append(s)
                s = _ret_state(kcs[n], vcs[n], s, lg, rev)

            def fstep(t, s, rev=rev, lg=lg):
                n = (NCH - 1 - t) if rev else t
                sl = pl.ds(pl.multiple_of(n * CH, CH), CH)
                st_s[n] = s
                return _ret_state(k_s[sl, :], p_ref[sl, 256:384], s, lg, rev)

            lax.fori_loop(0, NCH, fstep, s)

            def bstep(t, carry, rev=rev, lg=lg, direction=direction):
                ds, dlg = carry
                n = t if rev else (NCH - 1 - t)
                sl = pl.ds(pl.multiple_of(n * CH, CH), CH)
                _, vjp = jax.vjp(functools.partial(_ret_chunk, reverse=rev),
                                 q_s[sl, :], k_s[sl, :], p_ref[sl, 256:384], st_s[n], lg)
                dq, dk, dv, ds_prev, dl = vjp((do_s[sl, :], ds))
                if direction == 0:
                    dq_s[sl, :] = dq
                    dk_s[sl, :] = dk
                    dv_s[sl, :] = dv
                else:
                    dq_s[sl, :] += dq
                    dk_s[sl, :] += dk
                    dv_s[sl, :] += dv
                return ds_prev, dlg + dl

            ds, dlg = lax.fori_loop(0, NCH, bstep, (jnp.zeros((RD, RD), F32), jnp.zeros((1, 1), F32)))
            for idx in (1, 0):
                n = order_c[idx]
                _, vjp = jax.vjp(functools.partial(_ret_state, reverse=rev), kcs[n], vcs[n], ctx_states[idx], lg)
                dk_c, dv_c, ds, dl = vjp(ds)
                dlg = dlg + dl
                dkc[n] = dk_c if dkc[n] is None else dkc[n] + dk_c
                dvc[n] = dv_c if dvc[n] is None else dvc[n] + dv_c
            drd_ref[direction:direction + 1, :] += dlg * jax.nn.sigmoid(-rdv)
        dp_ref[:, 0:128] = _rope_t(dq_s[...] * (RD ** -0.5), cos_v, sin_v).astype(BF16)
        dp_ref[:, 128:256] = _rope_t(dk_s[...], cos_v, sin_v).astype(BF16)
        dp_ref[:, 256:384] = dv_s[...].astype(BF16)
        zero = jnp.zeros((CH, RD), BF16)
        for n in (0, 1):
            rows = slice(n * CH, (n + 1) * CH)
            dpc_ref[rows, 0:128] = zero
            dpc_ref[rows, 128:256] = dkc[n].astype(BF16)
            dpc_ref[rows, 256:384] = dvc[n].astype(BF16)
            dpc_ref[rows, 384:512] = zero

    return pl.pallas_call(
        body, name="retention_bwd", grid=(RH, nb),
        in_specs=[sp["pret"], sp["pretc"], sp["head"], sp["head"], sp["rd"], sp["gn"], sp["tab"], sp["tab"]],
        out_specs=[
            pl.BlockSpec((None, SEQ, 512), lambda h, b: (b, 0, h)),
            pl.BlockSpec((None, LC, 512), lambda h, b: (b, 0, h)),
            pl.BlockSpec((None, 2, 1), lambda h, b: (h, 0, 0)),
            pl.BlockSpec((None, 1, RD), lambda h, b: (h, 0, 0)),
        ],
        out_shape=[
            jax.ShapeDtypeStruct((nb, SEQ, IN_W), BF16),
            jax.ShapeDtypeStruct((nb, LC, IN_W), BF16),
            jax.ShapeDtypeStruct((RH, 2, 1), F32),
            jax.ShapeDtypeStruct((RH, 1, RD), F32),
        ],
        scratch_shapes=[pltpu.VMEM((SEQ, RD), F32)] * 6 + [pltpu.VMEM((NCH, RD, RD), F32)],
        compiler_params=_cp(("arbitrary", "arbitrary")),
    )(pret, pretc, o_all, dmixin, rd, gn, cos, sin)


def _row_class_dr(cls, kr):
    return kr + 3 if cls == 4 else kr - cls + 7


def _bias_patterns(rpb):
    per_cls = []
    for cls in range(8):
        parts = [jnp.pad(rpb[:, _row_class_dr(cls, kr), :], ((0, 0), (0, 33))) for kr in range(8)]
        per_cls.append(jnp.concatenate(parts, axis=-1))
    return jnp.stack(per_cls, axis=1).reshape(NPAIR, 2, 8, 1, 512)


def _bias_patterns_t(dpat):
    dpat = dpat.reshape(8, 8, 512)
    rows = []
    for dr in range(15):
        acc = jnp.zeros((8, 31), F32)
        for cls in range(8):
            for kr in range(8):
                if _row_class_dr(cls, kr) == dr:
                    acc = acc + dpat[:, cls, kr * 64:kr * 64 + 31]
        rows.append(acc)
    return jnp.stack(rows, axis=1)


def _barrel(x, left):
    row = lax.broadcasted_iota(jnp.int32, x.shape, 0)
    n = x.shape[1]
    for bit in range(6):
        s = 1 << bit
        x = jnp.where(((row >> bit) & 1) == 1, pltpu.roll(x, (n - s) if left else s, 1), x)
    return x


def _bias_from_pattern(pat):
    x = _barrel(pltpu.roll(jnp.broadcast_to(pat, (GW, 512)), 512 - 15, 1), left=False)
    qc = lax.broadcasted_iota(jnp.int32, (GW, 512), 0)
    kc = lax.broadcasted_iota(jnp.int32, (GW, 512), 1) & 63
    start = jnp.clip(qc - 8, 0, GW - 16)
    return jnp.where((kc >= start) & (kc < start + 16), x, NEG)


def _pattern_grad(ds_acc):
    return jnp.sum(pltpu.roll(_barrel(ds_acc, left=True), 15, 1), axis=0, keepdims=True)


def _na_row(r):
    rs = jnp.clip(r - 4, 0, 24)
    cls = jnp.where(r < 4, r, jnp.where(r > 28, r - 24, 4))
    return pl.ds(pl.multiple_of(r * GW, GW), GW), pl.ds(pl.multiple_of(rs * GW, GW), 8 * GW), cls


def _na_probs(qst, kb, kc, bias):
    s_loc = _nt(qst, kb) * 0.125 + bias
    s_ctx = _nt(qst, kc) * 0.125
    m = jnp.maximum(jnp.max(s_loc, axis=1, keepdims=True), jnp.max(s_ctx, axis=1, keepdims=True))
    e_loc, e_ctx = jnp.exp(s_loc - m), jnp.exp(s_ctx - m)
    den = jnp.sum(e_loc, axis=1, keepdims=True) + jnp.sum(e_ctx, axis=1, keepdims=True)
    return e_loc / den, e_ctx / den


def _stack_heads(t):
    lane = lax.broadcasted_iota(jnp.int32, t.shape, 1)
    zero = jnp.zeros_like(t)
    return jnp.concatenate([jnp.where(lane < 64, t, zero), jnp.where(lane >= 64, t, zero)], axis=0)


def _unstack_heads(t):
    lane = lax.broadcasted_iota(jnp.int32, (GW, 128), 1)
    return jnp.where(lane < 64, t[:GW], t[GW:])


def na_bias_table(pat):
    def body(pat_ref, out_ref):
        cls = pl.program_id(1)
        for hh in (0, 1):
            out_ref[hh * GW:(hh + 1) * GW, :] = _bias_from_pattern(pat_ref[hh, cls])

    return pl.pallas_call(
        body, name="na_bias_table", grid=(NPAIR, 8),
        in_specs=[pl.BlockSpec((None, 2, 8, 1, 512), lambda p, k: (p, 0, 0, 0, 0))],
        out_specs=pl.BlockSpec((None, None, 2 * GW, 512), lambda p, k: (p, k, 0, 0)),
        out_shape=jax.ShapeDtypeStruct((NPAIR, 8, 2 * GW, 512), F32),
    )(pat)


def na_fwd(pna, pnac, bias, mixin):
    nb = pna.shape[0]

    def body(p_ref, pc_ref, bias_ref, alias_ref, out_ref):
        del alias_ref
        kc, vc = pc_ref[:, 128:256], pc_ref[:, 256:384]

        def row(r, carry):
            qsl, bsl, cls = _na_row(r)
            kb, vb = p_ref[bsl, 128:256], p_ref[bsl, 256:384]
            p_loc, p_ctx = _na_probs(_stack_heads(p_ref[qsl, 0:128]), kb, kc, bias_ref[cls])
            out_ref[qsl, :] = _unstack_heads(_nn(p_loc, vb) + _nn(p_ctx, vc)).astype(BF16)
            return carry

        lax.fori_loop(0, SEQ // GW, row, 0, unroll=2)

    return pl.pallas_call(
        body, name="na_fwd", grid=(NPAIR, nb),
        in_specs=[
            pl.BlockSpec((None, SEQ, 384), lambda p, b: (b, 0, p)),
            pl.BlockSpec((None, LC, 384), lambda p, b: (b, 0, p)),
            pl.BlockSpec((None, 8, 2 * GW, 512), lambda p, b: (p, 0, 0, 0)),
            pl.BlockSpec(memory_space=pl.ANY),
        ],
        out_specs=pl.BlockSpec((None, SEQ, 128), lambda p, b: (b, 0, 4 + p)),
        out_shape=jax.ShapeDtypeStruct((nb, SEQ, D), BF16),
        input_output_aliases={3: 0},
        compiler_params=_cp(("arbitrary", "arbitrary")),
    )(pna, pnac, bias, mixin)


def na_bwd(pna, pnac, bias, dmixin, dproj, dprojc):
    nb = pna.shape[0]

    def body(p_ref, pc_ref, bias_ref, dmix_ref, a1_ref, a2_ref, dp_ref, dpc_ref, dpat_ref,
             dbias_s, dk_s, dv_s, dkc_s, dvc_s, res_s, resc_s):
        del a1_ref, a2_ref
        b, part = pl.program_id(1), pl.program_id(2)

        @pl.when(part == 0)
        def _():
            @pl.when(b == 0)
            def _():
                dbias_s[...] = jnp.zeros_like(dbias_s)

            dk_s[...] = jnp.zeros_like(dk_s)
            dv_s[...] = jnp.zeros_like(dv_s)
            dkc_s[...] = jnp.zeros_like(dkc_s)
            dvc_s[...] = jnp.zeros_like(dvc_s)
            kc, vc = pc_ref[:, 128:256], pc_ref[:, 256:384]

            def row(r, carry):
                qsl, bsl, cls = _na_row(r)
                kb, vb = p_ref[bsl, 128:256], p_ref[bsl, 256:384]
                qst, dost = _stack_heads(p_ref[qsl, 0:128]), _stack_heads(dmix_ref[qsl, :])
                p_loc, p_ctx = _na_probs(qst, kb, kc, bias_ref[cls])
                dp_loc, dp_ctx = _nt(dost, vb), _nt(dost, vc)
                delta = (jnp.sum(p_loc * dp_loc, axis=1, keepdims=True)
                         + jnp.sum(p_ctx * dp_ctx, axis=1, keepdims=True))
                ds_loc, ds_ctx = p_loc * (dp_loc - delta), p_ctx * (dp_ctx - delta)
                dbias_s[cls] += ds_loc
                res_s[0, qsl, :] = _unstack_heads((_nn(ds_loc, kb) + _nn(ds_ctx, kc)) * 0.125).astype(BF16)
                dk_s[bsl, :] += _tn(ds_loc, qst) * 0.125
                dv_s[bsl, :] += _tn(p_loc, dost)
                dkc_s[...] += _tn(ds_ctx, qst) * 0.125
                dvc_s[...] += _tn(p_ctx, dost)
                return carry

            lax.fori_loop(0, SEQ // GW, row, 0)
            res_s[1] = dk_s[...].astype(BF16)
            res_s[2] = dv_s[...].astype(BF16)
            resc_s[0] = jnp.zeros((LC, 128), BF16)
            resc_s[1] = dkc_s[...].astype(BF16)
            resc_s[2] = dvc_s[...].astype(BF16)

            @pl.when(b == nb - 1)
            def _():
                for hh in (0, 1):
                    for cls in range(8):
                        dpat_ref[hh, cls] = _pattern_grad(dbias_s[cls, hh * GW:(hh + 1) * GW, :])

        dp_ref[...] = res_s[part]
        dpc_ref[...] = resc_s[part]

    return pl.pallas_call(
        body, name="na_bwd", grid=(NPAIR, nb, 3),
        in_specs=[
            pl.BlockSpec((None, SEQ, 384), lambda p, b, s: (b, 0, p)),
            pl.BlockSpec((None, LC, 384), lambda p, b, s: (b, 0, p)),
            pl.BlockSpec((None, 8, 2 * GW, 512), lambda p, b, s: (p, 0, 0, 0)),
            pl.BlockSpec((None, SEQ, 128), lambda p, b, s: (b, 0, 4 + p)),
            pl.BlockSpec(memory_space=pl.ANY),
            pl.BlockSpec(memory_space=pl.ANY),
        ],
        out_specs=[
            pl.BlockSpec((None, SEQ, 128), lambda p, b, s: (b, 0, 16 + 3 * p + s)),
            pl.BlockSpec((None, LC, 128), lambda p, b, s: (b, 0, 16 + 3 * p + s)),
            pl.BlockSpec((None, 2, 8, 1, 512), lambda p, b, s: (p, 0, 0, 0, 0)),
        ],
        out_shape=[
            jax.ShapeDtypeStruct((nb, SEQ, IN_W), BF16),
            jax.ShapeDtypeStruct((nb, LC, IN_W), BF16),
            jax.ShapeDtypeStruct((NPAIR, 2, 8, 1, 512), F32),
        ],
        input_output_aliases={4: 0, 5: 1},
        scratch_shapes=[
            pltpu.VMEM((8, 2 * GW, 512), F32),
            pltpu.VMEM((SEQ, 128), F32), pltpu.VMEM((SEQ, 128), F32),
            pltpu.VMEM((LC, 128), F32), pltpu.VMEM((LC, 128), F32),
            pltpu.VMEM((3, SEQ, 128), BF16), pltpu.VMEM((3, LC, 128), BF16),
        ],
        compiler_params=_cp(("arbitrary", "arbitrary", "arbitrary")),
    )(pna, pnac, bias, dmixin, dproj, dprojc)


def tail_fwd_bwd(x, mixin, tgt, mod3, g_post_mix, g_pre_mlp, g_post_mlp, wout, w1, w2):
    nb = x.shape[0]

    def body(x_ref, mi_ref, tgt_ref, mod_ref, gpm_ref, gpl_ref, gpo_ref, wo_ref, w1_ref, w2_ref,
             dx_ref, dmix_ref, h2_ref, du_ref, a_ref, dm_ref, dmi_ref, dmod_ref, dg_ref, loss_ref):
        b, t = pl.program_id(0), pl.program_id(1)
        gt1, sh2, sc2, gt2 = mod_ref[2:3, :], mod_ref[3:4, :], mod_ref[4:5, :], mod_ref[5:6, :]
        mix = jnp.dot(mi_ref[...], wo_ref[...], preferred_element_type=F32)
        (x1, h2), vjp_a = jax.vjp(_post_mix, x_ref[...], mix, gt1, sc2, sh2, gpm_ref[...], gpl_ref[...])
        h2b = h2.astype(BF16)
        h2_ref[...] = h2b
        m = jnp.zeros((TN, D), F32)
        relus = []
        for j in range(4):
            cols = slice(j * D, (j + 1) * D)
            r = jnp.maximum(jnp.dot(h2b, w1_ref[:, cols], preferred_element_type=F32), 0.0)
            ab = (r * r).astype(BF16)
            a_ref[:, cols] = ab
            m = m + jnp.dot(ab, w2_ref[cols, :], preferred_element_type=F32)
            relus.append(r)
        loss, vjp_b = jax.vjp(_head_loss, x1, m, gt2, gpo_ref[...], tgt_ref[...])
        dx1, dm, dgt2, dgpo, _ = vjp_b(jnp.ones((1, 1), F32))
        dmb = dm.astype(BF16)
        dm_ref[...] = dmb
        dh2 = jnp.zeros((TN, D), F32)
        for j in range(4):
            cols = slice(j * D, (j + 1) * D)
            da = lax.dot_general(dmb, w2_ref[cols, :], (((1,), (1,)), ((), ())), preferred_element_type=F32)
            dub = (da * (2.0 * relus[j])).astype(BF16)
            du_ref[:, cols] = dub
            dh2 = dh2 + lax.dot_general(dub, w1_ref[:, cols], (((1,), (1,)), ((), ())), preferred_element_type=F32)
        dx, dmix, dgt1, dsc2, dsh2, dgpm, dgpl = vjp_a((dx1, dh2))
        dx_ref[...] = dx
        dmixb = dmix.astype(BF16)
        dmix_ref[...] = dmixb
        dmi_ref[...] = lax.dot_general(dmixb, wo_ref[...], (((1,), (1,)), ((), ())),
                                       preferred_element_type=F32).astype(BF16)

        @pl.when(t == 0)
        def _():
            dmod_ref[...] = jnp.zeros_like(dmod_ref)

        @pl.when((t == 0) & (b == 0))
        def _():
            dg_ref[...] = jnp.zeros_like(dg_ref)
            loss_ref[...] = jnp.zeros_like(loss_ref)

        dmod_ref[2:3, :] += dgt1
        dmod_ref[3:4, :] += dsh2
        dmod_ref[4:5, :] += dsc2
        dmod_ref[5:6, :] += dgt2
        dg_ref[0:1, :] += dgpm
        dg_ref[1:2, :] += dgpl
        dg_ref[2:3, :] += dgpo
        loss_ref[...] += jnp.broadcast_to(loss, loss_ref.shape)

    tok = lambda b, t: (b, t, 0)
    const = lambda b, t: (0, 0)
    vec = pl.BlockSpec((1, D), const)
    return pl.pallas_call(
        body, name="tail_fwd_bwd", grid=(nb, SEQ // TN),
        in_specs=[
            pl.BlockSpec((None, TN, D), tok), pl.BlockSpec((None, TN, D), tok), pl.BlockSpec((None, TN, D), tok),
            pl.BlockSpec((None, 6, D), lambda b, t: (b, 0, 0)), vec, vec, vec,
            pl.BlockSpec((D, D), const, pipeline_mode=pl.Buffered(1)),
            pl.BlockSpec((D, DFF), const, pipeline_mode=pl.Buffered(1)),
            pl.BlockSpec((DFF, D), const, pipeline_mode=pl.Buffered(1)),
        ],
        out_specs=[
            pl.BlockSpec((None, TN, D), tok), pl.BlockSpec((None, TN, D), tok), pl.BlockSpec((None, TN, D), tok),
            pl.BlockSpec((None, TN, DFF), tok), pl.BlockSpec((None, TN, DFF), tok), pl.BlockSpec((None, TN, D), tok),
            pl.BlockSpec((None, TN, D), tok),
            pl.BlockSpec((None, 6, D), lambda b, t: (b, 0, 0)),
            pl.BlockSpec((8, D), const), pl.BlockSpec((8, 128), const),
        ],
        out_shape=[
            jax.ShapeDtypeStruct((nb, SEQ, D), F32), jax.ShapeDtypeStruct((nb, SEQ, D), BF16),
            jax.ShapeDtypeStruct((nb, SEQ, D), BF16), jax.ShapeDtypeStruct((nb, SEQ, DFF), BF16),
            jax.ShapeDtypeStruct((nb, SEQ, DFF), BF16), jax.ShapeDtypeStruct((nb, SEQ, D), BF16),
            jax.ShapeDtypeStruct((nb, SEQ, D), BF16),
            jax.ShapeDtypeStruct((nb, 6, D), F32), jax.ShapeDtypeStruct((8, D), F32),
            jax.ShapeDtypeStruct((8, 128), F32),
        ],
        compiler_params=_cp(("arbitrary", "arbitrary")),
    )(x, mixin, tgt, mod3, g_post_mix, g_pre_mlp, g_post_mlp, wout, w1, w2)


def weight_grad(pairs, name, tm=1024, tn=1024, tk=512):
    m, n = pairs[0][0].shape[1], pairs[0][1].shape[1]
    tn = min(tn, n)
    steps = [xa.shape[0] // tk for xa, _ in pairs]
    total = sum(steps)
    offs = [sum(steps[:i]) for i in range(len(pairs))]

    def body(*refs):
        out_ref = refs[-1]
        k = pl.program_id(2)

        @pl.when(k == 0)
        def _():
            out_ref[...] = jnp.zeros_like(out_ref)

        for i in range(len(pairs)):
            @pl.when((k >= offs[i]) & (k < offs[i] + steps[i]))
            def _(i=i):
                out_ref[...] += lax.dot_general(refs[2 * i][...], refs[2 * i + 1][...], (((0,), (0,)), ((), ())),
                                                preferred_element_type=F32)

    in_specs, args = [], []
    for i, (xa, ya) in enumerate(pairs):
        clamp = lambda k, i=i: jnp.clip(k - offs[i], 0, steps[i] - 1)
        in_specs.append(pl.BlockSpec((tk, tm), lambda a, c, k, clamp=clamp: (clamp(k), a)))
        in_specs.append(pl.BlockSpec((tk, tn), lambda a, c, k, clamp=clamp: (clamp(k), c)))
        args += [xa, ya]
    return pl.pallas_call(
        body, name=name, grid=(m // tm, n // tn, total), in_specs=in_specs,
        out_specs=pl.BlockSpec((tm, tn), lambda a, c, k: (a, c)),
        out_shape=jax.ShapeDtypeStruct((m, n), F32),
        compiler_params=_cp(("arbitrary", "arbitrary", "arbitrary")),
    )(*args)


def _perm_block(t):
    u = t - 16
    return jnp.where(t < 16, 4 * (t % 4) + t // 4, 16 + 3 * (u % 4) + u // 4)


def unpack_w_in(blocks):
    def body(i_ref, o_ref):
        o_ref[...] = i_ref[...]

    return pl.pallas_call(
        body, name="unpack_w_in", grid=(28,),
        in_specs=[pl.BlockSpec((None, D, 128), lambda t: (t // 7, 0, t % 7))],
        out_specs=pl.BlockSpec((D, 128), lambda t: (0, _perm_block(t))),
        out_shape=jax.ShapeDtypeStruct((D, IN_W), BF16),
    )(blocks)


def pack_w_in(dw):
    def body(i_ref, o_ref):
        o_ref[...] = i_ref[...].astype(BF16)

    return pl.pallas_call(
        body, name="pack_w_in", grid=(28,),
        in_specs=[pl.BlockSpec((D, 128), lambda t: (0, _perm_block(t)))],
        out_specs=pl.BlockSpec((None, D, 128), lambda t: (t // 7, 0, t % 7)),
        out_shape=jax.ShapeDtypeStruct((4, D, 896), BF16),
    )(dw)


def unpack_cols(blocks, name):
    _, r, c = blocks.shape

    def body(i_ref, o_ref):
        o_ref[...] = i_ref[...]

    return pl.pallas_call(
        body, name=name, grid=(4,),
        in_specs=[pl.BlockSpec((None, r, c), lambda j: (j, 0, 0))],
        out_specs=pl.BlockSpec((r, c), lambda j: (0, j)),
        out_shape=jax.ShapeDtypeStruct((r, 4 * c), blocks.dtype),
    )(blocks)


def pack_cols(w, name):
    r, c4 = w.shape
    c = c4 // 4

    def body(i_ref, o_ref):
        o_ref[...] = i_ref[...].astype(BF16)

    return pl.pallas_call(
        body, name=name, grid=(4,),
        in_specs=[pl.BlockSpec((r, c), lambda j: (0, j))],
        out_specs=pl.BlockSpec((None, r, c), lambda j: (j, 0, 0)),
        out_shape=jax.ShapeDtypeStruct((4, r, c), BF16),
    )(w)


def cast_rows(w, name):
    r, c = w.shape

    def body(i_ref, o_ref):
        o_ref[...] = i_ref[...].astype(BF16)

    return pl.pallas_call(
        body, name=name, grid=(8,),
        in_specs=[pl.BlockSpec((r // 8, c), lambda j: (j, 0))],
        out_specs=pl.BlockSpec((r // 8, c), lambda j: (j, 0)),
        out_shape=jax.ShapeDtypeStruct((r, c), BF16),
    )(w)


def _place():
    return lax.axis_index("x"), lax.axis_index("y"), lax.axis_index("c")


def all_gather8(blocks, name):
    na = len(blocks)

    def body(*refs):
        ins, outs = refs[:na], refs[na:2 * na]
        send_sems, recv_sems, local_sem = refs[2 * na:]
        x, y, c = _place()
        me, sibling = (x, y, c), (x, y, 1 - c)
        chips = [(1 - x, y), (x, 1 - y), (1 - x, 1 - y)]

        def slot(o_ref, px, py, pc):
            return o_ref.at[4 * px + 2 * py + pc]

        def copy(a, k, block, to, src=None):
            return pltpu.make_async_remote_copy(
                src_ref=slot(outs[a], *block) if src is None else src, dst_ref=slot(outs[a], *block),
                send_sem=send_sems.at[a, k], recv_sem=recv_sems.at[a, k], device_id=to, device_id_type=MESH)

        mine = [pltpu.make_async_copy(ins[a], slot(outs[a], *me), local_sem.at[a]) for a in range(na)]
        for cp in mine:
            cp.start()
        first = []
        for a in range(na):
            first.append(copy(a, 0, me, sibling, src=ins[a]))
            first += [copy(a, 1 + j, me, (*chip, c), src=ins[a]) for j, chip in enumerate(chips)]
        for cp in first:
            cp.start()
        passed = []
        for j, chip in enumerate(chips):
            for a in range(na):
                copy(a, 1 + j, (*chip, c), me).wait_recv()
                cp = copy(a, 4 + j, (*chip, c), sibling)
                cp.start()
                passed.append(cp)
        for a in range(na):
            copy(a, 0, sibling, me).wait_recv()
            for j, chip in enumerate(chips):
                copy(a, 4 + j, (*chip, 1 - c), me).wait_recv()
        for cp in first + passed:
            cp.wait_send()
        for cp in mine:
            cp.wait()

    hbm = pl.BlockSpec(memory_space=pl.ANY)
    return pl.pallas_call(
        body, name=name, in_specs=[hbm] * na, out_specs=[hbm] * na,
        out_shape=[jax.ShapeDtypeStruct((8,) + b.shape, b.dtype) for b in blocks],
        scratch_shapes=[pltpu.SemaphoreType.DMA((na, 7)), pltpu.SemaphoreType.DMA((na, 7)),
                        pltpu.SemaphoreType.DMA((na,))],
    )(*blocks)


def sibling_exchange(arrays, name):
    na = len(arrays)

    def body(*refs):
        ins, outs = refs[:na], refs[na:2 * na]
        send_sems, recv_sems = refs[2 * na:]
        x, y, c = _place()
        cps = [pltpu.make_async_remote_copy(
            src_ref=ins[a], dst_ref=outs[a], send_sem=send_sems.at[a], recv_sem=recv_sems.at[a],
            device_id=(x, y, 1 - c), device_id_type=MESH) for a in range(na)]
        for cp in cps:
            cp.start()
        for cp in cps:
            cp.wait()

    hbm = pl.BlockSpec(memory_space=pl.ANY)
    return pl.pallas_call(
        body, name=name, in_specs=[hbm] * na, out_specs=[hbm] * na,
        out_shape=[jax.ShapeDtypeStruct(a.shape, a.dtype) for a in arrays],
        scratch_shapes=[pltpu.SemaphoreType.DMA((na,)), pltpu.SemaphoreType.DMA((na,))],
    )(*arrays)


def sibling_blocks(arrays, name):
    na = len(arrays)

    def body(*refs):
        ins, outs = refs[:na], refs[na:2 * na]
        send_sems, recv_sems = refs[2 * na:]
        x, y, c = _place()
        cps = [pltpu.make_async_remote_copy(
            src_ref=ins[a].at[2 * j + 1 - c], dst_ref=outs[a].at[j],
            send_sem=send_sems.at[a, j], recv_sem=recv_sems.at[a, j],
            device_id=(x, y, 1 - c), device_id_type=MESH) for a in range(na) for j in range(4)]
        for cp in cps:
            cp.start()
        for cp in cps:
            cp.wait()

    hbm = pl.BlockSpec(memory_space=pl.ANY)
    return pl.pallas_call(
        body, name=name, in_specs=[hbm] * na, out_specs=[hbm] * na,
        out_shape=[jax.ShapeDtypeStruct((4,) + a.shape[1:], a.dtype) for a in arrays],
        scratch_shapes=[pltpu.SemaphoreType.DMA((na, 4)), pltpu.SemaphoreType.DMA((na, 4))],
    )(*arrays)


def chip_exchange(arrays, name):
    na = len(arrays)

    def body(*refs):
        ins, outs = refs[:na], refs[na:2 * na]
        send_sems, recv_sems = refs[2 * na:]
        x, y, c = _place()
        chips = [(1 - x, y), (x, 1 - y), (1 - x, 1 - y)]
        cps = []
        for a in range(na):
            for k, (px, py) in enumerate(chips):
                cps.append(pltpu.make_async_remote_copy(
                    src_ref=ins[a].at[2 * px + py], dst_ref=outs[a].at[k],
                    send_sem=send_sems.at[a, k], recv_sem=recv_sems.at[a, k],
                    device_id=(px, py, c), device_id_type=MESH))
        for cp in cps:
            cp.start()
        for cp in cps:
            cp.wait()

    hbm = pl.BlockSpec(memory_space=pl.ANY)
    return pl.pallas_call(
        body, name=name, in_specs=[hbm] * na, out_specs=[hbm] * na,
        out_shape=[jax.ShapeDtypeStruct((3,) + a.shape[1:], a.dtype) for a in arrays],
        scratch_shapes=[pltpu.SemaphoreType.DMA((na, 3)), pltpu.SemaphoreType.DMA((na, 3))],
    )(*arrays)


def _row_tile(r):
    for cand in (512, 256, 128, 64, 32, 16, 8):
        if r % cand == 0:
            return cand
    return r


def chip_partial(mine4, landed4, name):
    _, r, ccols = mine4.shape
    tr = _row_tile(r)

    def body(g_ref, l_ref, o_ref):
        o_ref[...] = (g_ref[...].astype(F32) + l_ref[...].astype(F32)).astype(BF16)

    spec = pl.BlockSpec((None, tr, ccols), lambda j, i: (j, i, 0))
    return pl.pallas_call(
        body, name=name, grid=(4, r // tr), in_specs=[spec, spec], out_specs=spec,
        out_shape=jax.ShapeDtypeStruct((4, r, ccols), BF16),
    )(mine4, landed4)


def shard_sum(own, landed3, name):
    r, ccols = own.shape
    tr = _row_tile(r)

    def body(p_ref, l_ref, o_ref):
        acc = p_ref[...].astype(F32)
        for k in range(3):
            acc = acc + l_ref[k].astype(F32)
        o_ref[...] = acc

    return pl.pallas_call(
        body, name=name, grid=(r // tr,),
        in_specs=[pl.BlockSpec((tr, ccols), lambda i: (i, 0)), pl.BlockSpec((3, tr, ccols), lambda i: (0, i, 0))],
        out_specs=pl.BlockSpec((tr, ccols), lambda i: (i, 0)),
        out_shape=jax.ShapeDtypeStruct((r, ccols), F32),
    )(own, landed3)


def _adamw_math(w, g, m, v):
    m2 = B1 * m + (1.0 - B1) * g
    v2 = B2 * v + (1.0 - B2) * (g * g)
    m_hat = m2 / (1.0 - B1 ** STEP)
    v_hat = v2 / (1.0 - B2 ** STEP)
    return -LR * (m_hat / (jnp.sqrt(v_hat) + AEPS) + WD * w), m2, v2


def adamw_halves(w, lo, hi, m, v, name):
    r, ccols = w.shape
    hr = r // 2
    tr = _row_tile(hr)
    nt = hr // tr

    def body(w_ref, lo_ref, hi_ref, m_ref, v_ref, g_out, d_out, m_out, v_out):
        g = jnp.where(pl.program_id(0) == 0, lo_ref[...], hi_ref[...])
        d, m2, v2 = _adamw_math(w_ref[...], g, m_ref[...], v_ref[...])
        g_out[...] = g
        d_out[...] = d
        m_out[...] = m2
        v_out[...] = v2

    full = pl.BlockSpec((tr, ccols), lambda h, i: (h * nt + i, 0))
    part = pl.BlockSpec((tr, ccols), lambda h, i: (i, 0))
    return pl.pallas_call(
        body, name=name, grid=(2, nt),
        in_specs=[full, part, part, full, full], out_specs=[full] * 4,
        out_shape=[jax.ShapeDtypeStruct((r, ccols), F32)] * 4,
    )(w, lo, hi, m, v)


def adamw_plain(w, g, m, v, name):
    r, ccols = w.shape
    tr = _row_tile(r)

    def body(w_ref, g_ref, m_ref, v_ref, d_out, m_out, v_out):
        d, m2, v2 = _adamw_math(w_ref[...], g_ref[...], m_ref[...], v_ref[...])
        d_out[...] = d
        m_out[...] = m2
        v_out[...] = v2

    spec = pl.BlockSpec((tr, ccols), lambda i: (i, 0))
    return pl.pallas_call(
        body, name=name, grid=(r // tr,), in_specs=[spec] * 4, out_specs=[spec] * 3,
        out_shape=[jax.ShapeDtypeStruct((r, ccols), F32)] * 3,
    )(w, g, m, v)


def _silu(x):
    return x * jax.nn.sigmoid(x)


def mod_shard(cin, w_ada, b_shard):
    def body(c_ref, w_ref, b_ref, o_ref):
        o_ref[...] = _nn(_silu(c_ref[...]), w_ref[...]) + b_ref[...]

    return pl.pallas_call(
        body, name="mod_shard", grid=(3,),
        in_specs=[pl.BlockSpec((32, D), lambda j: (0, 0)), pl.BlockSpec((D, 512), lambda j: (0, j)),
                  pl.BlockSpec((1, 512), lambda j: (0, j))],
        out_specs=pl.BlockSpec((32, 512), lambda j: (0, j)),
        out_shape=jax.ShapeDtypeStruct((32, 1536), F32),
    )(cin, w_ada, b_shard)


def ada_grads(cin, gb, gc, w_ada):
    def body(c_ref, gb_ref, gc_ref, w_ref, gw_ref, pc_ref):
        ctx_tot = jnp.sum(gc_ref[...], axis=0, keepdims=True)
        rows = lax.broadcasted_iota(jnp.int32, (16, 512), 0)
        dm = jnp.concatenate([gb_ref[...], jnp.where(rows == 0, ctx_tot, 0.0)], axis=0)
        gw_ref[...] = _tn(_silu(c_ref[...]), dm)
        rows8 = lax.broadcasted_iota(jnp.int32, (8, 512), 0)
        part = _nt(jnp.where(rows8 == 0, ctx_tot, 0.0), w_ref[...])

        @pl.when(pl.program_id(0) == 0)
        def _():
            pc_ref[...] = jnp.zeros_like(pc_ref)

        pc_ref[...] += part

    return pl.pallas_call(
        body, name="ada_grads", grid=(3,),
        in_specs=[pl.BlockSpec((32, D), lambda j: (0, 0)), pl.BlockSpec((16, 512), lambda j: (0, j)),
                  pl.BlockSpec((8, 512), lambda j: (0, j)), pl.BlockSpec((D, 512), lambda j: (0, j))],
        out_specs=[pl.BlockSpec((D, 512), lambda j: (0, j)), pl.BlockSpec((8, D), lambda j: (0, 0))],
        out_shape=[jax.ShapeDtypeStruct((D, 1536), F32), jax.ShapeDtypeStruct((8, D), F32)],
    )(cin, gb, gc, w_ada)


SMALL_ROWS = 24


def small_update(gsm, gbf, gcf, pcg, w_pack, m_pack, v_pack):
    def body(gsm_ref, gbf_ref, gcf_ref, pcg_ref, w_ref, m_ref, v_ref, g_out, d_out, m_out, v_out, loss_out):
        g_out[...] = jnp.zeros_like(g_out)
        c_ctx = w_ref[0:1, :]
        sg = jax.nn.sigmoid(c_ctx)
        dsilu = pcg_ref[0:1, :] + pcg_ref[2:3, :] + pcg_ref[4:5, :] + pcg_ref[6:7, :]
        g_out[0:1, :] = dsilu * (sg * (1.0 + c_ctx * (1.0 - sg)))
        gb = jnp.sum(gbf_ref[...], axis=0, keepdims=True) + jnp.sum(gcf_ref[...], axis=0, keepdims=True)
        for j in range(6):
            g_out[1 + j:2 + j, :] = gb[:, j * D:(j + 1) * D]
        acc = gsm_ref[0]
        for dev in range(1, 8):
            acc = acc + gsm_ref[dev]
        g_out[7:8, :] = acc[0:1, :] + acc[1:2, :]
        g_out[8:16, :] = acc[2:10, :]
        lane = lax.broadcasted_iota(jnp.int32, (1, D), 1)
        last = acc[10:11, :]
        g_out[16:17, :] = jnp.where(lane < 8, last, 0.0)
        loss_out[...] = jnp.broadcast_to(jnp.sum(jnp.where(lane == 8, last, 0.0), axis=1, keepdims=True), (8, 128))
        g = g_out[...]
        d, m2, v2 = _adamw_math(w_ref[...], g, m_ref[...], v_ref[...])
        d_out[...] = d
        m_out[...] = m2
        v_out[...] = v2

    pack = jax.ShapeDtypeStruct((SMALL_ROWS, D), F32)
    return pl.pallas_call(
        body, name="small_update", out_shape=[pack, pack, pack, pack, jax.ShapeDtypeStruct((8, 128), F32)],
    )(gsm, gbf, gcf, pcg, w_pack, m_pack, v_pack)


def _pad_row(v, rows):
    flat = v.reshape(-1)
    return jnp.pad(flat, (0, rows * D - flat.shape[0])).reshape(rows, D)


def _pack_small(c_ctx, b_ada, g1, g2, g3, g4, ret_gn, na_rpb, ret_decay):
    parts = [_pad_row(c_ctx, 1), _pad_row(b_ada, 6), _pad_row(g1, 1), _pad_row(g2, 1), _pad_row(g3, 1),
             _pad_row(g4, 1), _pad_row(ret_gn, 1), _pad_row(na_rpb, 4), _pad_row(ret_decay, 1),
             jnp.zeros((SMALL_ROWS - 17, D), F32)]
    return jnp.concatenate(parts, axis=0)


def _unpack_small(p):
    return (p[0], p[1:7].reshape(1, 6 * D), p[7:8], p[8:9], p[9:10], p[10:11], p[11:12, :512],
            p[12:16].reshape(-1)[:8 * 15 * 31].reshape(1, 8, 15, 31), p[16, :8].reshape(1, 2, 4))


def local_step(x, ctx, tgt, mod3, g_pre_mix, g_post_mix, g_pre_mlp, g_post_mlp, ret_decay, ret_gn, na_rpb,
               wperm, wout_b, w1_b, w2_b):
    nb = x.shape[0]
    tokens = nb * SEQ
    cos, sin = _rope_tables()
    rd = ret_decay.T.reshape(RH, 2, 1)
    gn = ret_gn.reshape(RH, 1, RD)
    bias = na_bias_table(_bias_patterns(na_rpb))
    h, pret, pna = premix_proj(x, mod3, g_pre_mix, wperm, False, "premix_proj")
    hc, pretc, pnac = premix_proj(ctx, mod3, g_pre_mix, wperm, True, "premix_proj_ctx")
    o_all, mixin = retention_fwd(pret, pretc, rd, gn, cos, sin)
    mixin = na_fwd(pna, pnac, bias, mixin)
    dx_tail, dmix, h2, du, act, dm, dmixin, dmod_t, dg_t, loss_t = tail_fwd_bwd(
        x, mixin, tgt, mod3, g_post_mix, g_pre_mlp, g_post_mlp, wout_b, w1_b, w2_b)
    dproj, dprojc, drd, dgn = retention_bwd(pret, pretc, o_all, dmixin, rd, gn, cos, sin)
    dproj, dprojc, dpat = na_bwd(pna, pnac, bias, dmixin, dproj, dprojc)
    grad_x, dmod_a, dg_a = premix_bwd(x, mod3, g_pre_mix, wperm, dproj, dx_tail, "premix_bwd")
    dmod_c, dg_c = premix_bwd(ctx, mod3, g_pre_mix, wperm, dprojc, None, "premix_bwd_ctx")
    dw_in = weight_grad([(h.reshape(tokens, D), dproj.reshape(tokens, IN_W)),
                         (hc.reshape(nb * LC, D), dprojc.reshape(nb * LC, IN_W))], "grad_w_in", tn=512)
    dw_out = weight_grad([(mixin.reshape(tokens, D), dmix.reshape(tokens, D))], "grad_w_out")
    dw1 = weight_grad([(h2.reshape(tokens, D), du.reshape(tokens, DFF))], "grad_w_mlp1")
    dw2 = weight_grad([(act.reshape(tokens, DFF), dm.reshape(tokens, D))], "grad_w_mlp2")
    dmod = jnp.concatenate([jnp.concatenate([dmod_a[:, 0:2], dmod_t[:, 2:6]], axis=1), dmod_c], axis=0)
    last = jnp.pad(jnp.concatenate([drd[:, :, 0].T.reshape(8), loss_t[0, 0:1]]), (0, D - 9)).reshape(1, D)
    small = jnp.concatenate([
        dg_a[0:1], dg_c[0:1], dg_t[0:3], _pad_row(dgn, 1), _pad_row(_bias_patterns_t(dpat), 4), last], axis=0)
    return grad_x, dw_in, dw_out, dw1, dw2, dmod, small


def kernel(x, c, ctx, c_ctx, w_ada, b_ada, g_pre_mix, g_post_mix, g_pre_mlp, g_post_mlp, w_in, ret_decay, ret_gn, na_rpb, w_out, w_mlp1, w_mlp2, loss_target, m_c_ctx, m_w_ada, m_b_ada, m_g_pre_mix, m_g_post_mix, m_g_pre_mlp, m_g_post_mlp, m_w_in, m_ret_decay, m_ret_gn, m_na_rpb, m_w_out, m_w_mlp1, m_w_mlp2, v_c_ctx, v_w_ada, v_b_ada, v_g_pre_mix, v_g_post_mix, v_g_pre_mlp, v_g_post_mlp, v_w_in, v_ret_decay, v_ret_gn, v_na_rpb, v_w_out, v_w_mlp1, v_w_mlp2):
    px, py, pc = _place()
    dev = 4 * px + 2 * py + pc
    chip = 2 * px + py

    def my_half(w2d):
        rows = w2d.shape[0] // 2
        return lax.dynamic_slice_in_dim(w2d, pc * rows, rows, 0)

    halves = [my_half(w[0]).astype(BF16) for w in (w_in, w_out, w_mlp1, w_mlp2)]
    gw_in, gw_out, gw1, gw2, cg = all_gather8(halves + [jnp.pad(c, ((0, 6), (0, 0)))], "gather_weights")
    wperm = unpack_w_in(gw_in.reshape(4, D, 896))
    wout_b = gw_out.reshape(D, D)
    w1_b = unpack_cols(gw1.reshape(4, D, D), "unpack_w_mlp1")
    w2_b = gw2.reshape(DFF, D)

    cin = jnp.pad(cg[:, 0:2].reshape(16, D), ((0, 16), (0, 0))) + jnp.pad(c_ctx[None], ((16, 15), (0, 0)))
    mod_mine = mod_shard(cin, w_ada[0], lax.dynamic_slice_in_dim(b_ada, chip * 1536, 1536, 1))
    (mg,) = all_gather8([mod_mine], "gather_mod")
    mod_all = jnp.concatenate([mg[0], mg[2], mg[4], mg[6]], axis=1)
    mod3 = (jnp.pad(lax.dynamic_slice_in_dim(mod_all, 2 * dev, 2, 0), ((0, 1), (0, 0)))
            + jnp.pad(mod_all[16:17], ((2, 0), (0, 0)))).reshape(3, 6, D)

    grad_x, dw_in, dw_out, dw1, dw2, dmod, small = local_step(
        x, ctx, loss_target, mod3, g_pre_mix, g_post_mix, g_pre_mlp, g_post_mlp, ret_decay[0], ret_gn, na_rpb[0],
        wperm, wout_b, w1_b, w2_b)

    g8 = [pack_w_in(dw_in).reshape(8, 512, 896), cast_rows(dw_out, "cast_g_w_out").reshape(8, 128, D),
          pack_cols(dw1, "pack_g_w_mlp1").reshape(8, 512, D), cast_rows(dw2, "cast_g_w_mlp2").reshape(8, 512, D)]
    names = ["w_in", "w_out", "w_mlp1", "w_mlp2"]
    landed = sibling_blocks(g8, "rs_sibling")
    partial = [chip_partial(lax.dynamic_index_in_dim(g.reshape(4, 2, *g.shape[1:]), pc, 1, keepdims=False), l,
                            "rs_chip_sum_" + n) for g, l, n in zip(g8, landed, names)]
    landed3 = chip_exchange(partial, "rs_chips")
    mine = [shard_sum(lax.dynamic_index_in_dim(p, chip, 0, keepdims=False), l, "rs_shard_sum_" + n)
            for p, l, n in zip(partial, landed3, names)]
    theirs = sibling_exchange(mine, "rs_halves")
    big = []
    for a, (w, m, v) in enumerate(((w_in, m_w_in, v_w_in), (w_out, m_w_out, v_w_out),
                                   (w_mlp1, m_w_mlp1, v_w_mlp1), (w_mlp2, m_w_mlp2, v_w_mlp2))):
        lo = jnp.where(pc == 0, mine[a], theirs[a])
        hi = jnp.where(pc == 0, theirs[a], mine[a])
        big.append([r[None] for r in adamw_halves(w[0], lo, hi, m[0], v[0], "adamw_" + names[a])])

    pay = jnp.concatenate([dmod.reshape(18, D), small, jnp.zeros((3, D), F32)], axis=0)
    (gs,) = all_gather8([pay], "gather_small")
    gbf = gs[:, 0:12].reshape(16, 6 * D)
    gcf = gs[:, 12:18].reshape(8, 6 * D)
    gw_ada, pc_part = ada_grads(cin, lax.dynamic_slice_in_dim(gbf, chip * 1536, 1536, 1),
                                lax.dynamic_slice_in_dim(gcf, chip * 1536, 1536, 1), w_ada[0])
    (pcg,) = all_gather8([pc_part], "gather_c_ctx")
    d_ada, m_ada, v_ada = adamw_plain(w_ada[0], gw_ada, m_w_ada[0], v_w_ada[0], "adamw_w_ada")
    w_pack = _pack_small(c_ctx, b_ada, g_pre_mix, g_post_mix, g_pre_mlp, g_post_mlp, ret_gn, na_rpb, ret_decay)
    m_pack = _pack_small(m_c_ctx, m_b_ada, m_g_pre_mix, m_g_post_mix, m_g_pre_mlp, m_g_post_mlp, m_ret_gn, m_na_rpb,
                         m_ret_decay)
    v_pack = _pack_small(v_c_ctx, v_b_ada, v_g_pre_mix, v_g_post_mix, v_g_pre_mlp, v_g_post_mlp, v_ret_gn, v_na_rpb,
                         v_ret_decay)
    g_pack, d_pack, m2_pack, v2_pack, loss8 = small_update(gs[:, 18:29], gbf, gcf, pcg[:, 0], w_pack, m_pack, v_pack)

    def leaves(pack, ada, idx):
        s_c, s_b, s_g1, s_g2, s_g3, s_g4, s_gn, s_rpb, s_rd = _unpack_small(pack)
        return [s_c, ada[None], s_b, s_g1, s_g2, s_g3, s_g4, big[0][idx], s_rd, s_gn, s_rpb,
                big[1][idx], big[2][idx], big[3][idx]]

    return (loss8[0, 0], grad_x, *leaves(g_pack, gw_ada, 0), *leaves(d_pack, d_ada, 1),
            *leaves(m2_pack, m_ada, 2), *leaves(v2_pack, v_ada, 3))
```

```python
import functools

import jax
import jax.numpy as jnp
from jax import lax
from jax.experimental import pallas as pl
from jax.experimental.pallas import tpu as pltpu

F32, BF16 = jnp.float32, jnp.bfloat16
D = 1024
SEQ = 2048
LC = 256
GW = 64
RH, RD, CH = 4, 128, 128
NPAIR = 4
IN_W = 3584
RET_W = 2048
DFF = 4096
EPS = 1e-6
NEG = -1e30
TN = 256
NCH = SEQ // CH
LR, B1, B2, AEPS, WD, STEP = 0.001, 0.9, 0.999, 1e-08, 0.01, 10
MESH = pl.DeviceIdType.MESH
VMEM_LIMIT = 56 * 1024 * 1024


def _cp(sem=None):
    return pltpu.CompilerParams(dimension_semantics=sem, vmem_limit_bytes=VMEM_LIMIT)


def _nn(a, b):
    return jnp.dot(a.astype(BF16), b.astype(BF16), preferred_element_type=F32)


def _nt(a, b):
    return lax.dot_general(a.astype(BF16), b.astype(BF16), (((1,), (1,)), ((), ())), preferred_element_type=F32)


def _tn(a, b):
    return lax.dot_general(a.astype(BF16), b.astype(BF16), (((0,), (0,)), ((), ())), preferred_element_type=F32)


@jax.custom_vjp
def mm_nn(a, b):
    return _nn(a, b)


@jax.custom_vjp
def mm_nt(a, b):
    return _nt(a, b)


@jax.custom_vjp
def mm_tn(a, b):
    return _tn(a, b)


mm_nn.defvjp(lambda a, b: (_nn(a, b), (a, b)), lambda r, g: (_nt(g, r[1]), _tn(r[0], g)))
mm_nt.defvjp(lambda a, b: (_nt(a, b), (a, b)), lambda r, g: (_nn(g, r[1]), _tn(g, r[0])))
mm_tn.defvjp(lambda a, b: (_tn(a, b), (a, b)), lambda r, g: (_nt(r[1], g), _nn(r[0], g)))


def _rms(x):
    return x * lax.rsqrt(jnp.mean(x * x, axis=-1, keepdims=True) + EPS)


def _rms_mod(x, g, sc, sh):
    return (_rms(x) * g) * (1.0 + sc) + sh


def _post_mix(x, mix, gt1, sc2, sh2, g_post_mix, g_pre_mlp):
    x1 = x + gt1 * (_rms(mix) * g_post_mix)
    return x1, _rms_mod(x1, g_pre_mlp, sc2, sh2)


def _head_loss(x1, m, gt2, g_post_mlp, tgt):
    err = x1 + gt2 * (_rms(m) * g_post_mlp) - tgt
    return 0.5 * jnp.sum(jnp.mean(err * err, axis=-1, keepdims=True), axis=0, keepdims=True)


def _ln_gate(o, g, w):
    mu = jnp.mean(o, axis=-1, keepdims=True)
    var = jnp.mean(jnp.square(o - mu), axis=-1, keepdims=True)
    y = (o - mu) * lax.rsqrt(var + EPS)
    return (y * w) * (g * jax.nn.sigmoid(g))


def _swap32(x):
    lane = lax.broadcasted_iota(jnp.int32, x.shape, 1)
    return jnp.where((lane & 32) == 0, pltpu.roll(x, 96, 1), pltpu.roll(x, 32, 1))


def _rope(x, cos, sin):
    return x * cos + _swap32(x) * sin


def _rope_t(g, cos, sin):
    return g * cos + _swap32(g * sin)


def _rope_tables():
    tok = jnp.arange(SEQ)
    pos_r = (tok // GW).astype(F32)
    pos_c = (tok % GW).astype(F32)
    inv = 10000.0 ** (-jnp.arange(32, dtype=F32) / 32)
    ar = pos_r[:, None] * inv[None, :]
    ac = pos_c[:, None] * inv[None, :]
    cos = jnp.concatenate([jnp.cos(ar), jnp.cos(ar), jnp.cos(ac), jnp.cos(ac)], axis=-1)
    sin = jnp.concatenate([-jnp.sin(ar), jnp.sin(ar), -jnp.sin(ac), jnp.sin(ac)], axis=-1)
    return cos, sin


def _fiota(shape, dim):
    return lax.broadcasted_iota(jnp.int32, shape, dim).astype(F32)


def _ret_state(k, v, s, lg, reverse):
    pos = _fiota((CH, 1), 0)
    b_exp = pos if reverse else (CH - 1.0 - pos)
    return jnp.exp(lg * CH) * s + mm_tn(k * jnp.exp(lg * b_exp), v)


def _ret_chunk(q, k, v, s, lg, reverse):
    i = _fiota((CH, CH), 0)
    j = _fiota((CH, CH), 1)
    diff = (j - i) if reverse else (i - j)
    mask = (diff > 0) if reverse else (diff >= 0)
    decay = jnp.where(mask, jnp.exp(lg * jnp.where(mask, diff, 0.0)), 0.0)
    pos = _fiota((CH, 1), 0)
    a_exp = (CH - pos) if reverse else (pos + 1.0)
    o = mm_nn(mm_nt(q, k) * decay, v) + mm_nn(q * jnp.exp(lg * a_exp), s)
    return o, _ret_state(k, v, s, lg, reverse)


def premix_proj(xin, mod3, g_pre, wperm, is_ctx, name):
    nb, length, _ = xin.shape
    tn = min(TN, length)

    def body(x_ref, mod_ref, g_ref, w_ref, h_ref, pret_ref, pna_ref):
        h = _rms_mod(x_ref[...], g_ref[...], mod_ref[1:2, :], mod_ref[0:1, :])
        hb = h.astype(BF16)
        h_ref[...] = hb
        pret_ref[...] = jnp.dot(hb, w_ref[:, :RET_W], preferred_element_type=F32)
        pna_ref[...] = jnp.dot(hb, w_ref[:, RET_W:], preferred_element_type=F32).astype(BF16)

    return pl.pallas_call(
        body, name=name, grid=(nb, length // tn),
        in_specs=[
            pl.BlockSpec((None, tn, D), lambda b, t: (b, t, 0)),
            pl.BlockSpec((None, 6, D), (lambda b, t: (2, 0, 0)) if is_ctx else (lambda b, t: (b, 0, 0))),
            pl.BlockSpec((1, D), lambda b, t: (0, 0)),
            pl.BlockSpec((D, IN_W), lambda b, t: (0, 0), pipeline_mode=pl.Buffered(1)),
        ],
        out_specs=[
            pl.BlockSpec((None, tn, D), lambda b, t: (b, t, 0)),
            pl.BlockSpec((None, tn, RET_W), lambda b, t: (b, t, 0)),
            pl.BlockSpec((None, tn, IN_W - RET_W), lambda b, t: (b, t, 0)),
        ],
        out_shape=[
            jax.ShapeDtypeStruct((nb, length, D), BF16),
            jax.ShapeDtypeStruct((nb, length, RET_W), F32),
            jax.ShapeDtypeStruct((nb, length, IN_W - RET_W), BF16),
        ],
        compiler_params=_cp(("arbitrary", "arbitrary")),
    )(xin, mod3, g_pre, wperm)


def premix_bwd(xin, mod3, g_pre, wperm, dproj, dx_tail, name):
    nb, length, _ = xin.shape
    tn = min(TN, length)
    is_ctx = dx_tail is None

    def body(*refs):
        if is_ctx:
            x_ref, mod_ref, g_ref, w_ref, dp_ref, dmod_ref, dg_ref = refs
        else:
            x_ref, mod_ref, g_ref, w_ref, dp_ref, dxt_ref, dx_ref, dmod_ref, dg_ref = refs
        b, t = pl.program_id(0), pl.program_id(1)
        dh = lax.dot_general(dp_ref[...], w_ref[...], (((1,), (1,)), ((), ())), preferred_element_type=F32)
        _, vjp = jax.vjp(_rms_mod, x_ref[...], g_ref[...], mod_ref[1:2, :], mod_ref[0:1, :])
        dx, dg, dsc, dsh = vjp(dh)
        if not is_ctx:
            dx_ref[...] = dx + dxt_ref[...]

        @pl.when((t == 0) & ((b == 0) if is_ctx else True))
        def _():
            dmod_ref[...] = jnp.zeros_like(dmod_ref)

        @pl.when((t == 0) & (b == 0))
        def _():
            dg_ref[...] = jnp.zeros_like(dg_ref)

        dmod_ref[0:1, :] += dsh
        dmod_ref[1:2, :] += dsc
        dg_ref[0:1, :] += dg

    tok = lambda b, t: (b, t, 0)
    in_specs = [
        pl.BlockSpec((None, tn, D), tok),
        pl.BlockSpec((None, 6, D), (lambda b, t: (2, 0, 0)) if is_ctx else (lambda b, t: (b, 0, 0))),
        pl.BlockSpec((1, D), lambda b, t: (0, 0)),
        pl.BlockSpec((D, IN_W), lambda b, t: (0, 0), pipeline_mode=pl.Buffered(1)),
        pl.BlockSpec((None, tn, IN_W), tok),
    ]
    args = [xin, mod3, g_pre, wperm, dproj]
    out_specs = [
        pl.BlockSpec((None, 6, D), (lambda b, t: (0, 0, 0)) if is_ctx else (lambda b, t: (b, 0, 0))),
        pl.BlockSpec((8, D), lambda b, t: (0, 0)),
    ]
    out_shape = [jax.ShapeDtypeStruct((1 if is_ctx else nb, 6, D), F32), jax.ShapeDtypeStruct((8, D), F32)]
    if not is_ctx:
        in_specs.append(pl.BlockSpec((None, tn, D), tok))
        args.append(dx_tail)
        out_specs.insert(0, pl.BlockSpec((None, tn, D), tok))
        out_shape.insert(0, jax.ShapeDtypeStruct((nb, length, D), F32))
    return pl.pallas_call(
        body, name=name, grid=(nb, length // tn), in_specs=in_specs, out_specs=out_specs, out_shape=out_shape,
        compiler_params=_cp(("arbitrary", "arbitrary")),
    )(*args)


def _ret_specs(order):
    def im(f):
        return lambda *g: f(*order(*g))
    return dict(
        pret=pl.BlockSpec((None, SEQ, 512), im(lambda b, h: (b, 0, h))),
        pretc=pl.BlockSpec((None, LC, 512), im(lambda b, h: (b, 0, h))),
        rd=pl.BlockSpec((None, 2, 1), im(lambda b, h: (h, 0, 0))),
        gn=pl.BlockSpec((None, 1, RD), im(lambda b, h: (h, 0, 0))),
        tab=pl.BlockSpec((SEQ, RD), im(lambda b, h: (0, 0))),
        head=pl.BlockSpec((None, SEQ, RD), im(lambda b, h: (b, 0, h))),
    )


def retention_fwd(pret, pretc, rd, gn, cos, sin, hosted):
    nb = pret.shape[0]
    sp = _ret_specs(lambda b, h: (b, h))

    def body(*refs):
        own_in, h_in, own_out, h_out, own_scr, h_sems = hosted.split(refs, 6, 2)
        p_ref, pc_ref, rd_ref, gn_ref, cos_ref, sin_ref = own_in
        (o_ref, mix_ref), (q_s, k_s, o_s) = own_out, own_scr
        grid_step = pl.program_id(0) * RH + pl.program_id(1)

        @pl.when(grid_step == 0)
        def _():
            hosted.start(h_in, h_out, h_sems)

        cos_v, sin_v = cos_ref[...], sin_ref[...]
        q_s[...] = _rope(p_ref[:, 0:128], cos_v, sin_v) * (RD ** -0.5)
        k_s[...] = _rope(p_ref[:, 128:256], cos_v, sin_v)
        for direction in (0, 1):
            rev = direction == 1
            lg = jax.nn.log_sigmoid(rd_ref[direction:direction + 1, :])
            s = jnp.zeros((RD, RD), F32)
            for n in ((1, 0) if rev else (0, 1)):
                s = _ret_state(pc_ref[n * CH:(n + 1) * CH, 128:256], pc_ref[n * CH:(n + 1) * CH, 256:384], s, lg, rev)

            def step(t, s, rev=rev, lg=lg):
                n = (NCH - 1 - t) if rev else t
                sl = pl.ds(pl.multiple_of(n * CH, CH), CH)
                o, s2 = _ret_chunk(q_s[sl, :], k_s[sl, :], p_ref[sl, 256:384], s, lg, rev)
                if rev:
                    o_s[sl, :] += o
                else:
                    o_s[sl, :] = o
                return s2

            lax.fori_loop(0, NCH, step, s)
        o = o_s[...]
        o_ref[...] = o
        mix_ref[...] = _ln_gate(o, p_ref[:, 384:512], gn_ref[...]).astype(BF16)

        @pl.when(grid_step == nb * RH - 1)
        def _():
            hosted.finish(h_in, h_out, h_sems)

    h_in_specs, h_out_specs = hosted.specs()
    return pl.pallas_call(
        body, name="retention_fwd", grid=(nb, RH),
        in_specs=[sp["pret"], sp["pretc"], sp["rd"], sp["gn"], sp["tab"], sp["tab"]] + h_in_specs,
        out_specs=[sp["head"], sp["head"]] + h_out_specs,
        out_shape=[jax.ShapeDtypeStruct((nb, SEQ, RH * RD), F32), jax.ShapeDtypeStruct((nb, SEQ, D), BF16)]
        + hosted.out_shape,
        scratch_shapes=[pltpu.VMEM((SEQ, RD), F32)] * 3 + hosted.scratch,
        compiler_params=_cp(("arbitrary", "arbitrary")),
    )(pret, pretc, rd, gn, cos, sin, *hosted.args)


def retention_bwd(pret, pretc, o_all, dmixin, rd, gn, cos, sin):
    nb = pret.shape[0]
    sp = _ret_specs(lambda h, b: (b, h))

    def body(p_ref, pc_ref, o_ref, dmix_ref, rd_ref, gn_ref, cos_ref, sin_ref,
             dp_ref, dpc_ref, drd_ref, dgn_ref, q_s, k_s, do_s, dq_s, dk_s, dv_s, st_s):
        b = pl.program_id(1)
        cos_v, sin_v = cos_ref[...], sin_ref[...]
        q_s[...] = _rope(p_ref[:, 0:128], cos_v, sin_v) * (RD ** -0.5)
        k_s[...] = _rope(p_ref[:, 128:256], cos_v, sin_v)
        _, gate_vjp = jax.vjp(_ln_gate, o_ref[...], p_ref[:, 384:512], gn_ref[...])
        do, dg, dgn = gate_vjp(dmix_ref[...].astype(F32))
        do_s[...] = do
        dp_ref[:, 384:512] = dg.astype(BF16)

        @pl.when(b == 0)
        def _():
            drd_ref[...] = jnp.zeros_like(drd_ref)
            dgn_ref[...] = jnp.zeros_like(dgn_ref)

        dgn_ref[...] += dgn
        dkc = [None, None]
        dvc = [None, None]
        for direction in (0, 1):
            rev = direction == 1
            rdv = rd_ref[direction:direction + 1, :]
            lg = jax.nn.log_sigmoid(rdv)
            order_c = (1, 0) if rev else (0, 1)
            kcs = [pc_ref[n * CH:(n + 1) * CH, 128:256] for n in (0, 1)]
            vcs = [pc_ref[n * CH:(n + 1) * CH, 256:384] for n in (0, 1)]
            s = jnp.zeros((RD, RD), F32)
            ctx_states = []
            for n in order_c:
                ctx_states.append(s)
                s = _ret_state(kcs[n], vcs[n], s, lg, rev)

            def fstep(t, s, rev=rev, lg=lg):
                n = (NCH - 1 - t) if rev else t
                sl = pl.ds(pl.multiple_of(n * CH, CH), CH)
                st_s[n] = s
                return _ret_state(k_s[sl, :], p_ref[sl, 256:384], s, lg, rev)

            lax.fori_loop(0, NCH, fstep, s)

            def bstep(t, carry, rev=rev, lg=lg, direction=direction):
                ds, dlg = carry
                n = t if rev else (NCH - 1 - t)
                sl = pl.ds(pl.multiple_of(n * CH, CH), CH)
                _, vjp = jax.vjp(functools.partial(_ret_chunk, reverse=rev),
                                 q_s[sl, :], k_s[sl, :], p_ref[sl, 256:384], st_s[n], lg)
                dq, dk, dv, ds_prev, dl = vjp((do_s[sl, :], ds))
                if direction == 0:
                    dq_s[sl, :] = dq
                    dk_s[sl, :] = dk
                    dv_s[sl, :] = dv
                else:
                    dq_s[sl, :] += dq
                    dk_s[sl, :] += dk
                    dv_s[sl, :] += dv
                return ds_prev, dlg + dl

            ds, dlg = lax.fori_loop(0, NCH, bstep, (jnp.zeros((RD, RD), F32), jnp.zeros((1, 1), F32)))
            for idx in (1, 0):
                n = order_c[idx]
                _, vjp = jax.vjp(functools.partial(_ret_state, reverse=rev), kcs[n], vcs[n], ctx_states[idx], lg)
                dk_c, dv_c, ds, dl = vjp(ds)
                dlg = dlg + dl
                dkc[n] = dk_c if dkc[n] is None else dkc[n] + dk_c
                dvc[n] = dv_c if dvc[n] is None else dvc[n] + dv_c
            drd_ref[direction:direction + 1, :] += dlg * jax.nn.sigmoid(-rdv)
        dp_ref[:, 0:128] = _rope_t(dq_s[...] * (RD ** -0.5), cos_v, sin_v).astype(BF16)
        dp_ref[:, 128:256] = _rope_t(dk_s[...], cos_v, sin_v).astype(BF16)
        dp_ref[:, 256:384] = dv_s[...].astype(BF16)
        zero = jnp.zeros((CH, RD), BF16)
        for n in (0, 1):
            rows = slice(n * CH, (n + 1) * CH)
            dpc_ref[rows, 0:128] = zero
            dpc_ref[rows, 128:256] = dkc[n].astype(BF16)
            dpc_ref[rows, 256:384] = dvc[n].astype(BF16)
            dpc_ref[rows, 384:512] = zero

    return pl.pallas_call(
        body, name="retention_bwd", grid=(RH, nb),
        in_specs=[sp["pret"], sp["pretc"], sp["head"], sp["head"], sp["rd"], sp["gn"], sp["tab"], sp["tab"]],
        out_specs=[
            pl.BlockSpec((None, SEQ, 512), lambda h, b: (b, 0, h)),
            pl.BlockSpec((None, LC, 512), lambda h, b: (b, 0, h)),
            pl.BlockSpec((None, 2, 1), lambda h, b: (h, 0, 0)),
            pl.BlockSpec((None, 1, RD), lambda h, b: (h, 0, 0)),
        ],
        out_shape=[
            jax.ShapeDtypeStruct((nb, SEQ, IN_W), BF16),
            jax.ShapeDtypeStruct((nb, LC, IN_W), BF16),
            jax.ShapeDtypeStruct((RH, 2, 1), F32),
            jax.ShapeDtypeStruct((RH, 1, RD), F32),
        ],
        scratch_shapes=[pltpu.VMEM((SEQ, RD), F32)] * 6 + [pltpu.VMEM((NCH, RD, RD), F32)],
        compiler_params=_cp(("arbitrary", "arbitrary")),
    )(pret, pretc, o_all, dmixin, rd, gn, cos, sin)


def _row_class_dr(cls, kr):
    return kr + 3 if cls == 4 else kr - cls + 7


def _bias_patterns(rpb):
    per_cls = []
    for cls in range(8):
        parts = [jnp.pad(rpb[:, _row_class_dr(cls, kr), :], ((0, 0), (0, 33))) for kr in range(8)]
        per_cls.append(jnp.concatenate(parts, axis=-1))
    return jnp.stack(per_cls, axis=1).reshape(NPAIR, 2, 8, 1, 512)


def _bias_patterns_t(dpat):
    dpat = dpat.reshape(8, 8, 512)
    rows = []
    for dr in range(15):
        acc = jnp.zeros((8, 31), F32)
        for cls in range(8):
            for kr in range(8):
                if _row_class_dr(cls, kr) == dr:
                    acc = acc + dpat[:, cls, kr * 64:kr * 64 + 31]
        rows.append(acc)
    return jnp.stack(rows, axis=1)


def _barrel(x, left):
    row = lax.broadcasted_iota(jnp.int32, x.shape, 0)
    n = x.shape[1]
    for bit in range(6):
        s = 1 << bit
        x = jnp.where(((row >> bit) & 1) == 1, pltpu.roll(x, (n - s) if left else s, 1), x)
    return x


def _bias_from_pattern(pat):
    x = _barrel(pltpu.roll(jnp.broadcast_to(pat, (GW, 512)), 512 - 15, 1), left=False)
    qc = lax.broadcasted_iota(jnp.int32, (GW, 512), 0)
    kc = lax.broadcasted_iota(jnp.int32, (GW, 512), 1) & 63
    start = jnp.clip(qc - 8, 0, GW - 16)
    return jnp.where((kc >= start) & (kc < start + 16), x, NEG)


def _pattern_grad(ds_acc):
    return jnp.sum(pltpu.roll(_barrel(ds_acc, left=True), 15, 1), axis=0, keepdims=True)


def _na_row(r):
    rs = jnp.clip(r - 4, 0, 24)
    cls = jnp.where(r < 4, r, jnp.where(r > 28, r - 24, 4))
    return pl.ds(pl.multiple_of(r * GW, GW), GW), pl.ds(pl.multiple_of(rs * GW, GW), 8 * GW), cls


def _na_probs(qst, kb, kc, bias):
    s_loc = _nt(qst, kb) * 0.125 + bias
    s_ctx = _nt(qst, kc) * 0.125
    m = jnp.maximum(jnp.max(s_loc, axis=1, keepdims=True), jnp.max(s_ctx, axis=1, keepdims=True))
    e_loc, e_ctx = jnp.exp(s_loc - m), jnp.exp(s_ctx - m)
    den = jnp.sum(e_loc, axis=1, keepdims=True) + jnp.sum(e_ctx, axis=1, keepdims=True)
    return e_loc / den, e_ctx / den


def _stack_heads(t):
    lane = lax.broadcasted_iota(jnp.int32, t.shape, 1)
    zero = jnp.zeros_like(t)
    return jnp.concatenate([jnp.where(lane < 64, t, zero), jnp.where(lane >= 64, t, zero)], axis=0)


def _unstack_heads(t):
    lane = lax.broadcasted_iota(jnp.int32, (GW, 128), 1)
    return jnp.where(lane < 64, t[:GW], t[GW:])


def na_bias_table(pat):
    def body(pat_ref, out_ref):
        cls = pl.program_id(1)
        for hh in (0, 1):
            out_ref[hh * GW:(hh + 1) * GW, :] = _bias_from_pattern(pat_ref[hh, cls])

    return pl.pallas_call(
        body, name="na_bias_table", grid=(NPAIR, 8),
        in_specs=[pl.BlockSpec((None, 2, 8, 1, 512), lambda p, k: (p, 0, 0, 0, 0))],
        out_specs=pl.BlockSpec((None, None, 2 * GW, 512), lambda p, k: (p, k, 0, 0)),
        out_shape=jax.ShapeDtypeStruct((NPAIR, 8, 2 * GW, 512), F32),
    )(pat)


def na_fwd(pna, pnac, bias, mixin, hosted):
    nb = pna.shape[0]

    def body(*refs):
        (p_ref, pc_ref, bias_ref, _), h_in, (out_ref,), h_out, _, h_sems = hosted.split(refs, 4, 1)
        grid_step = pl.program_id(0) * nb + pl.program_id(1)

        @pl.when(grid_step == 0)
        def _():
            hosted.start(h_in, h_out, h_sems)

        kc, vc = pc_ref[:, 128:256], pc_ref[:, 256:384]

        def row(r, carry):
            qsl, bsl, cls = _na_row(r)
            kb, vb = p_ref[bsl, 128:256], p_ref[bsl, 256:384]
            p_loc, p_ctx = _na_probs(_stack_heads(p_ref[qsl, 0:128]), kb, kc, bias_ref[cls])
            out_ref[qsl, :] = _unstack_heads(_nn(p_loc, vb) + _nn(p_ctx, vc)).astype(BF16)
            return carry

        lax.fori_loop(0, SEQ // GW, row, 0, unroll=2)

        @pl.when(grid_step == NPAIR * nb - 1)
        def _():
            hosted.finish(h_in, h_out, h_sems)

    h_in_specs, h_out_specs = hosted.specs()
    return pl.pallas_call(
        body, name="na_fwd", grid=(NPAIR, nb),
        in_specs=[
            pl.BlockSpec((None, SEQ, 384), lambda p, b: (b, 0, p)),
            pl.BlockSpec((None, LC, 384), lambda p, b: (b, 0, p)),
            pl.BlockSpec((None, 8, 2 * GW, 512), lambda p, b: (p, 0, 0, 0)),
            pl.BlockSpec(memory_space=pl.ANY),
        ] + h_in_specs,
        out_specs=[pl.BlockSpec((None, SEQ, 128), lambda p, b: (b, 0, 4 + p))] + h_out_specs,
        out_shape=[jax.ShapeDtypeStruct((nb, SEQ, D), BF16)] + hosted.out_shape,
        input_output_aliases={3: 0},
        scratch_shapes=hosted.scratch,
        compiler_params=_cp(("arbitrary", "arbitrary")),
    )(pna, pnac, bias, mixin, *hosted.args)


def na_bwd(pna, pnac, bias, dmixin, dproj, dprojc, hosted):
    nb = pna.shape[0]

    def body(*refs):
        own_in, h_in, own_out, h_out, own_scr, h_sems = hosted.split(refs, 6, 3)
        p_ref, pc_ref, bias_ref, dmix_ref = own_in[:4]
        dp_ref, dpc_ref, dpat_ref = own_out
        dbias_s, dk_s, dv_s, dkc_s, dvc_s, res_s, resc_s = own_scr
        b, part = pl.program_id(1), pl.program_id(2)
        grid_step = (pl.program_id(0) * nb + b) * 3 + part

        @pl.when(grid_step == 0)
        def _():
            hosted.start(h_in, h_out, h_sems)

        @pl.when(grid_step == NPAIR * nb * 3 - 1)
        def _():
            hosted.finish(h_in, h_out, h_sems)

        @pl.when(part == 0)
        def _():
            @pl.when(b == 0)
            def _():
                dbias_s[...] = jnp.zeros_like(dbias_s)

            dk_s[...] = jnp.zeros_like(dk_s)
            dv_s[...] = jnp.zeros_like(dv_s)
            dkc_s[...] = jnp.zeros_like(dkc_s)
            dvc_s[...] = jnp.zeros_like(dvc_s)
            kc, vc = pc_ref[:, 128:256], pc_ref[:, 256:384]

            def row(r, carry):
                qsl, bsl, cls = _na_row(r)
                kb, vb = p_ref[bsl, 128:256], p_ref[bsl, 256:384]
                qst, dost = _stack_heads(p_ref[qsl, 0:128]), _stack_heads(dmix_ref[qsl, :])
                p_loc, p_ctx = _na_probs(qst, kb, kc, bias_ref[cls])
                dp_loc, dp_ctx = _nt(dost, vb), _nt(dost, vc)
                delta = (jnp.sum(p_loc * dp_loc, axis=1, keepdims=True)
                         + jnp.sum(p_ctx * dp_ctx, axis=1, keepdims=True))
                ds_loc, ds_ctx = p_loc * (dp_loc - delta), p_ctx * (dp_ctx - delta)
                dbias_s[cls] += ds_loc
                res_s[0, qsl, :] = _unstack_heads((_nn(ds_loc, kb) + _nn(ds_ctx, kc)) * 0.125).astype(BF16)
                dk_s[bsl, :] += _tn(ds_loc, qst) * 0.125
                dv_s[bsl, :] += _tn(p_loc, dost)
                dkc_s[...] += _tn(ds_ctx, qst) * 0.125
                dvc_s[...] += _tn(p_ctx, dost)
                return carry

            lax.fori_loop(0, SEQ // GW, row, 0)
            res_s[1] = dk_s[...].astype(BF16)
            res_s[2] = dv_s[...].astype(BF16)
            resc_s[0] = jnp.zeros((LC, 128), BF16)
            resc_s[1] = dkc_s[...].astype(BF16)
            resc_s[2] = dvc_s[...].astype(BF16)

            @pl.when(b == nb - 1)
            def _():
                for hh in (0, 1):
                    for cls in range(8):
                        dpat_ref[hh, cls] = _pattern_grad(dbias_s[cls, hh * GW:(hh + 1) * GW, :])

        dp_ref[...] = res_s[part]
        dpc_ref[...] = resc_s[part]

    h_in_specs, h_out_specs = hosted.specs()
    return pl.pallas_call(
        body, name="na_bwd", grid=(NPAIR, nb, 3),
        in_specs=[
            pl.BlockSpec((None, SEQ, 384), lambda p, b, s: (b, 0, p)),
            pl.BlockSpec((None, LC, 384), lambda p, b, s: (b, 0, p)),
            pl.BlockSpec((None, 8, 2 * GW, 512), lambda p, b, s: (p, 0, 0, 0)),
            pl.BlockSpec((None, SEQ, 128), lambda p, b, s: (b, 0, 4 + p)),
            pl.BlockSpec(memory_space=pl.ANY),
            pl.BlockSpec(memory_space=pl.ANY),
        ] + h_in_specs,
        out_specs=[
            pl.BlockSpec((None, SEQ, 128), lambda p, b, s: (b, 0, 16 + 3 * p + s)),
            pl.BlockSpec((None, LC, 128), lambda p, b, s: (b, 0, 16 + 3 * p + s)),
            pl.BlockSpec((None, 2, 8, 1, 512), lambda p, b, s: (p, 0, 0, 0, 0)),
        ] + h_out_specs,
        out_shape=[
            jax.ShapeDtypeStruct((nb, SEQ, IN_W), BF16),
            jax.ShapeDtypeStruct((nb, LC, IN_W), BF16),
            jax.ShapeDtypeStruct((NPAIR, 2, 8, 1, 512), F32),
        ] + hosted.out_shape,
        input_output_aliases={4: 0, 5: 1},
        scratch_shapes=[
            pltpu.VMEM((8, 2 * GW, 512), F32),
            pltpu.VMEM((SEQ, 128), F32), pltpu.VMEM((SEQ, 128), F32),
            pltpu.VMEM((LC, 128), F32), pltpu.VMEM((LC, 128), F32),
            pltpu.VMEM((3, SEQ, 128), BF16), pltpu.VMEM((3, LC, 128), BF16),
        ] + hosted.scratch,
        compiler_params=_cp(("arbitrary", "arbitrary", "arbitrary")),
    )(pna, pnac, bias, dmixin, dproj, dprojc, *hosted.args)


def tail_fwd_bwd(x, mixin, tgt, mod3, g_post_mix, g_pre_mlp, g_post_mlp, wout, w1, w2):
    nb = x.shape[0]

    def body(x_ref, mi_ref, tgt_ref, mod_ref, gpm_ref, gpl_ref, gpo_ref, wo_ref, w1_ref, w2_ref,
             dx_ref, dmix_ref, h2_ref, du_ref, a_ref, dm_ref, dmi_ref, dmod_ref, dg_ref, loss_ref):
        b, t = pl.program_id(0), pl.program_id(1)
        gt1, sh2, sc2, gt2 = mod_ref[2:3, :], mod_ref[3:4, :], mod_ref[4:5, :], mod_ref[5:6, :]
        mix = jnp.dot(mi_ref[...], wo_ref[...], preferred_element_type=F32)
        (x1, h2), vjp_a = jax.vjp(_post_mix, x_ref[...], mix, gt1, sc2, sh2, gpm_ref[...], gpl_ref[...])
        h2b = h2.astype(BF16)
        h2_ref[...] = h2b
        m = jnp.zeros((TN, D), F32)
        relus = []
        for j in range(4):
            cols = slice(j * D, (j + 1) * D)
            r = jnp.maximum(jnp.dot(h2b, w1_ref[:, cols], preferred_element_type=F32), 0.0)
            ab = (r * r).astype(BF16)
            a_ref[:, cols] = ab
            m = m + jnp.dot(ab, w2_ref[cols, :], preferred_element_type=F32)
            relus.append(r)
        loss, vjp_b = jax.vjp(_head_loss, x1, m, gt2, gpo_ref[...], tgt_ref[...])
        dx1, dm, dgt2, dgpo, _ = vjp_b(jnp.ones((1, 1), F32))
        dmb = dm.astype(BF16)
        dm_ref[...] = dmb
        dh2 = jnp.zeros((TN, D), F32)
        for j in range(4):
            cols = slice(j * D, (j + 1) * D)
            da = lax.dot_general(dmb, w2_ref[cols, :], (((1,), (1,)), ((), ())), preferred_element_type=F32)
            dub = (da * (2.0 * relus[j])).astype(BF16)
            du_ref[:, cols] = dub
            dh2 = dh2 + lax.dot_general(dub, w1_ref[:, cols], (((1,), (1,)), ((), ())), preferred_element_type=F32)
        dx, dmix, dgt1, dsc2, dsh2, dgpm, dgpl = vjp_a((dx1, dh2))
        dx_ref[...] = dx
        dmixb = dmix.astype(BF16)
        dmix_ref[...] = dmixb
        dmi_ref[...] = lax.dot_general(dmixb, wo_ref[...], (((1,), (1,)), ((), ())),
                                       preferred_element_type=F32).astype(BF16)

        @pl.when(t == 0)
        def _():
            dmod_ref[...] = jnp.zeros_like(dmod_ref)

        @pl.when((t == 0) & (b == 0))
        def _():
            dg_ref[...] = jnp.zeros_like(dg_ref)
            loss_ref[...] = jnp.zeros_like(loss_ref)

        dmod_ref[2:3, :] += dgt1
        dmod_ref[3:4, :] += dsh2
        dmod_ref[4:5, :] += dsc2
        dmod_ref[5:6, :] += dgt2
        dg_ref[0:1, :] += dgpm
        dg_ref[1:2, :] += dgpl
        dg_ref[2:3, :] += dgpo
        loss_ref[...] += jnp.broadcast_to(loss, loss_ref.shape)

    tok = lambda b, t: (b, t, 0)
    const = lambda b, t: (0, 0)
    vec = pl.BlockSpec((1, D), const)
    return pl.pallas_call(
        body, name="tail_fwd_bwd", grid=(nb, SEQ // TN),
        in_specs=[
            pl.BlockSpec((None, TN, D), tok), pl.BlockSpec((None, TN, D), tok), pl.BlockSpec((None, TN, D), tok),
            pl.BlockSpec((None, 6, D), lambda b, t: (b, 0, 0)), vec, vec, vec,
            pl.BlockSpec((D, D), const, pipeline_mode=pl.Buffered(1)),
            pl.BlockSpec((D, DFF), const, pipeline_mode=pl.Buffered(1)),
            pl.BlockSpec((DFF, D), const, pipeline_mode=pl.Buffered(1)),
        ],
        out_specs=[
            pl.BlockSpec((None, TN, D), tok), pl.BlockSpec((None, TN, D), tok), pl.BlockSpec((None, TN, D), tok),
            pl.BlockSpec((None, TN, DFF), tok), pl.BlockSpec((None, TN, DFF), tok), pl.BlockSpec((None, TN, D), tok),
            pl.BlockSpec((None, TN, D), tok),
            pl.BlockSpec((None, 6, D), lambda b, t: (b, 0, 0)),
            pl.BlockSpec((8, D), const), pl.BlockSpec((8, 128), const),
        ],
        out_shape=[
            jax.ShapeDtypeStruct((nb, SEQ, D), F32), jax.ShapeDtypeStruct((nb, SEQ, D), BF16),
            jax.ShapeDtypeStruct((nb, SEQ, D), BF16), jax.ShapeDtypeStruct((nb, SEQ, DFF), BF16),
            jax.ShapeDtypeStruct((nb, SEQ, DFF), BF16), jax.ShapeDtypeStruct((nb, SEQ, D), BF16),
            jax.ShapeDtypeStruct((nb, SEQ, D), BF16),
            jax.ShapeDtypeStruct((nb, 6, D), F32), jax.ShapeDtypeStruct((8, D), F32),
            jax.ShapeDtypeStruct((8, 128), F32),
        ],
        compiler_params=_cp(("arbitrary", "arbitrary")),
    )(x, mixin, tgt, mod3, g_post_mix, g_pre_mlp, g_post_mlp, wout, w1, w2)


def weight_grad(pairs, name, tm=1024, tn=1024, tk=512):
    m, n = pairs[0][0].shape[1], pairs[0][1].shape[1]
    tn = min(tn, n)
    steps = [xa.shape[0] // tk for xa, _ in pairs]
    total = sum(steps)
    offs = [sum(steps[:i]) for i in range(len(pairs))]

    def body(*refs):
        out_ref = refs[-1]
        k = pl.program_id(2)

        @pl.when(k == 0)
        def _():
            out_ref[...] = jnp.zeros_like(out_ref)

        for i in range(len(pairs)):
            @pl.when((k >= offs[i]) & (k < offs[i] + steps[i]))
            def _(i=i):
                out_ref[...] += lax.dot_general(refs[2 * i][...], refs[2 * i + 1][...], (((0,), (0,)), ((), ())),
                                                preferred_element_type=F32)

    in_specs, args = [], []
    for i, (xa, ya) in enumerate(pairs):
        clamp = lambda k, i=i: jnp.clip(k - offs[i], 0, steps[i] - 1)
        in_specs.append(pl.BlockSpec((tk, tm), lambda a, c, k, clamp=clamp: (clamp(k), a)))
        in_specs.append(pl.BlockSpec((tk, tn), lambda a, c, k, clamp=clamp: (clamp(k), c)))
        args += [xa, ya]
    return pl.pallas_call(
        body, name=name, grid=(m // tm, n // tn, total), in_specs=in_specs,
        out_specs=pl.BlockSpec((tm, tn), lambda a, c, k: (a, c)),
        out_shape=jax.ShapeDtypeStruct((m, n), F32),
        compiler_params=_cp(("arbitrary", "arbitrary", "arbitrary")),
    )(*args)


def _perm_block(t):
    u = t - 16
    return jnp.where(t < 16, 4 * (t % 4) + t // 4, 16 + 3 * (u % 4) + u // 4)


def unpack_w_in(blocks):
    def body(i_ref, o_ref):
        o_ref[...] = i_ref[...]

    return pl.pallas_call(
        body, name="unpack_w_in", grid=(28,),
        in_specs=[pl.BlockSpec((None, D, 128), lambda t: (t // 7, 0, t % 7))],
        out_specs=pl.BlockSpec((D, 128), lambda t: (0, _perm_block(t))),
        out_shape=jax.ShapeDtypeStruct((D, IN_W), BF16),
    )(blocks)


def pack_w_in(dw):
    def body(i_ref, o_ref):
        o_ref[...] = i_ref[...].astype(BF16)

    return pl.pallas_call(
        body, name="pack_w_in", grid=(28,),
        in_specs=[pl.BlockSpec((D, 128), lambda t: (0, _perm_block(t)))],
        out_specs=pl.BlockSpec((None, D, 128), lambda t: (t // 7, 0, t % 7)),
        out_shape=jax.ShapeDtypeStruct((4, D, 896), BF16),
    )(dw)


def unpack_cols(blocks, name):
    _, r, c = blocks.shape

    def body(i_ref, o_ref):
        o_ref[...] = i_ref[...]

    return pl.pallas_call(
        body, name=name, grid=(4,),
        in_specs=[pl.BlockSpec((None, r, c), lambda j: (j, 0, 0))],
        out_specs=pl.BlockSpec((r, c), lambda j: (0, j)),
        out_shape=jax.ShapeDtypeStruct((r, 4 * c), blocks.dtype),
    )(blocks)


def pack_cols(w, name):
    r, c4 = w.shape
    c = c4 // 4

    def body(i_ref, o_ref):
        o_ref[...] = i_ref[...].astype(BF16)

    return pl.pallas_call(
        body, name=name, grid=(4,),
        in_specs=[pl.BlockSpec((r, c), lambda j: (0, j))],
        out_specs=pl.BlockSpec((None, r, c), lambda j: (j, 0, 0)),
        out_shape=jax.ShapeDtypeStruct((4, r, c), BF16),
    )(w)


def cast_rows(w, name):
    r, c = w.shape

    def body(i_ref, o_ref):
        o_ref[...] = i_ref[...].astype(BF16)

    return pl.pallas_call(
        body, name=name, grid=(8,),
        in_specs=[pl.BlockSpec((r // 8, c), lambda j: (j, 0))],
        out_specs=pl.BlockSpec((r // 8, c), lambda j: (j, 0)),
        out_shape=jax.ShapeDtypeStruct((r, c), BF16),
    )(w)


def _place():
    return lax.axis_index("x"), lax.axis_index("y"), lax.axis_index("c")


class Hosted:
    def __init__(self, args, out_shape, scratch, start, finish):
        self.args, self.out_shape, self.scratch, self.start, self.finish = args, out_shape, scratch, start, finish

    def specs(self):
        hbm = pl.BlockSpec(memory_space=pl.ANY)
        return [hbm] * len(self.args), [hbm] * len(self.out_shape)

    def split(self, refs, n_in, n_out):
        a, b = len(self.args), len(self.out_shape)
        cuts = [n_in, n_in + a, n_in + a + n_out, n_in + a + n_out + b, len(refs) - len(self.scratch)]
        parts = [refs[i:j] for i, j in zip([0] + cuts, cuts + [len(refs)])]
        return parts[0], parts[1], parts[2], parts[3], parts[4], parts[5]


def run_hosted(hosted, name):
    def body(*refs):
        _, ins, _, outs, _, sems = hosted.split(refs, 0, 0)
        hosted.start(ins, outs, sems)
        hosted.finish(ins, outs, sems)

    in_specs, out_specs = hosted.specs()
    return pl.pallas_call(body, name=name, in_specs=in_specs, out_specs=out_specs, out_shape=hosted.out_shape,
                          scratch_shapes=hosted.scratch)(*hosted.args)


def gather8(blocks):
    na = len(blocks)

    def copies(ins, outs, sems):
        send_sems, recv_sems, local_sem = sems
        x, y, c = _place()
        me, sibling = (x, y, c), (x, y, 1 - c)
        chips = [(1 - x, y), (x, 1 - y), (1 - x, 1 - y)]

        def slot(o_ref, px, py, pc):
            return o_ref.at[4 * px + 2 * py + pc]

        def copy(a, k, block, to, src=None):
            return pltpu.make_async_remote_copy(
                src_ref=slot(outs[a], *block) if src is None else src, dst_ref=slot(outs[a], *block),
                send_sem=send_sems.at[a, k], recv_sem=recv_sems.at[a, k], device_id=to, device_id_type=MESH)

        mine = [pltpu.make_async_copy(ins[a], slot(outs[a], *me), local_sem.at[a]) for a in range(na)]
        first = []
        for a in range(na):
            first.append(copy(a, 0, me, sibling, src=ins[a]))
            first += [copy(a, 1 + j, me, (*chip, c), src=ins[a]) for j, chip in enumerate(chips)]
        return copy, mine, first, me, sibling, chips, c

    def start(ins, outs, sems):
        _, mine, first, *_ = copies(ins, outs, sems)
        for cp in mine + first:
            cp.start()

    def finish(ins, outs, sems):
        copy, mine, first, me, sibling, chips, c = copies(ins, outs, sems)
        passed = []
        for j, chip in enumerate(chips):
            for a in range(na):
                copy(a, 1 + j, (*chip, c), me).wait_recv()
                cp = copy(a, 4 + j, (*chip, c), sibling)
                cp.start()
                passed.append(cp)
        for a in range(na):
            copy(a, 0, sibling, me).wait_recv()
            for j, chip in enumerate(chips):
                copy(a, 4 + j, (*chip, 1 - c), me).wait_recv()
        for cp in first + passed:
            cp.wait_send()
        for cp in mine:
            cp.wait()

    return Hosted(list(blocks), [jax.ShapeDtypeStruct((8,) + b.shape, b.dtype) for b in blocks],
                  [pltpu.SemaphoreType.DMA((na, 7)), pltpu.SemaphoreType.DMA((na, 7)), pltpu.SemaphoreType.DMA((na,))],
                  start, finish)


def all_gather8(blocks, name):
    return run_hosted(gather8(blocks), name)


def chips3(arrays):
    na = len(arrays)

    def copies(ins, outs, sems):
        send_sems, recv_sems = sems
        x, y, c = _place()
        return [pltpu.make_async_remote_copy(
            src_ref=ins[a].at[2 * px + py], dst_ref=outs[a].at[k], send_sem=send_sems.at[a, k],
            recv_sem=recv_sems.at[a, k], device_id=(px, py, c), device_id_type=MESH)
            for a in range(na) for k, (px, py) in enumerate([(1 - x, y), (x, 1 - y), (1 - x, 1 - y)])]

    def start(ins, outs, sems):
        for cp in copies(ins, outs, sems):
            cp.start()

    def finish(ins, outs, sems):
        for cp in copies(ins, outs, sems):
            cp.wait()

    return Hosted(list(arrays), [jax.ShapeDtypeStruct((3,) + a.shape[1:], a.dtype) for a in arrays],
                  [pltpu.SemaphoreType.DMA((na, 3)), pltpu.SemaphoreType.DMA((na, 3))], start, finish)


def sibling_exchange(arrays, name):
    na = len(arrays)

    def body(*refs):
        ins, outs = refs[:na], refs[na:2 * na]
        send_sems, recv_sems = refs[2 * na:]
        x, y, c = _place()
        cps = [pltpu.make_async_remote_copy(
            src_ref=ins[a], dst_ref=outs[a], send_sem=send_sems.at[a], recv_sem=recv_sems.at[a],
            device_id=(x, y, 1 - c), device_id_type=MESH) for a in range(na)]
        for cp in cps:
            cp.start()
        for cp in cps:
            cp.wait()

    hbm = pl.BlockSpec(memory_space=pl.ANY)
    return pl.pallas_call(
        body, name=name, in_specs=[hbm] * na, out_specs=[hbm] * na,
        out_shape=[jax.ShapeDtypeStruct(a.shape, a.dtype) for a in arrays],
        scratch_shapes=[pltpu.SemaphoreType.DMA((na,)), pltpu.SemaphoreType.DMA((na,))],
    )(*arrays)


def sibling_blocks(arrays, name):
    na = len(arrays)

    def body(*refs):
        ins, outs = refs[:na], refs[na:2 * na]
        send_sems, recv_sems = refs[2 * na:]
        x, y, c = _place()
        cps = [pltpu.make_async_remote_copy(
            src_ref=ins[a].at[2 * j + 1 - c], dst_ref=outs[a].at[j],
            send_sem=send_sems.at[a, j], recv_sem=recv_sems.at[a, j],
            device_id=(x, y, 1 - c), device_id_type=MESH) for a in range(na) for j in range(4)]
        for cp in cps:
            cp.start()
        for cp in cps:
            cp.wait()

    hbm = pl.BlockSpec(memory_space=pl.ANY)
    return pl.pallas_call(
        body, name=name, in_specs=[hbm] * na, out_specs=[hbm] * na,
        out_shape=[jax.ShapeDtypeStruct((4,) + a.shape[1:], a.dtype) for a in arrays],
        scratch_shapes=[pltpu.SemaphoreType.DMA((na, 4)), pltpu.SemaphoreType.DMA((na, 4))],
    )(*arrays)


def chip_exchange(arrays, name):
    return run_hosted(chips3(arrays), name)


def _row_tile(r):
    for cand in (512, 256, 128, 64, 32, 16, 8):
        if r % cand == 0:
            return cand
    return r


def chip_partial(mine4, landed4, name):
    _, r, ccols = mine4.shape
    tr = _row_tile(r)

    def body(g_ref, l_ref, o_ref):
        o_ref[...] = (g_ref[...].astype(F32) + l_ref[...].astype(F32)).astype(BF16)

    spec = pl.BlockSpec((None, tr, ccols), lambda j, i: (j, i, 0))
    return pl.pallas_call(
        body, name=name, grid=(4, r // tr), in_specs=[spec, spec], out_specs=spec,
        out_shape=jax.ShapeDtypeStruct((4, r, ccols), BF16),
    )(mine4, landed4)


def shard_sum(own, landed3, name):
    r, ccols = own.shape
    tr = _row_tile(r)

    def body(p_ref, l_ref, o_ref):
        acc = p_ref[...].astype(F32)
        for k in range(3):
            acc = acc + l_ref[k].astype(F32)
        o_ref[...] = acc

    return pl.pallas_call(
        body, name=name, grid=(r // tr,),
        in_specs=[pl.BlockSpec((tr, ccols), lambda i: (i, 0)), pl.BlockSpec((3, tr, ccols), lambda i: (0, i, 0))],
        out_specs=pl.BlockSpec((tr, ccols), lambda i: (i, 0)),
        out_shape=jax.ShapeDtypeStruct((r, ccols), F32),
    )(own, landed3)


def _adamw_math(w, g, m, v):
    m2 = B1 * m + (1.0 - B1) * g
    v2 = B2 * v + (1.0 - B2) * (g * g)
    m_hat = m2 / (1.0 - B1 ** STEP)
    v_hat = v2 / (1.0 - B2 ** STEP)
    return -LR * (m_hat / (jnp.sqrt(v_hat) + AEPS) + WD * w), m2, v2


def adamw_halves(w, lo, hi, m, v, name):
    r, ccols = w.shape
    hr = r // 2
    tr = _row_tile(hr)
    nt = hr // tr

    def body(w_ref, lo_ref, hi_ref, m_ref, v_ref, g_out, d_out, m_out, v_out):
        g = jnp.where(pl.program_id(0) == 0, lo_ref[...], hi_ref[...])
        d, m2, v2 = _adamw_math(w_ref[...], g, m_ref[...], v_ref[...])
        g_out[...] = g
        d_out[...] = d
        m_out[...] = m2
        v_out[...] = v2

    full = pl.BlockSpec((tr, ccols), lambda h, i: (h * nt + i, 0))
    part = pl.BlockSpec((tr, ccols), lambda h, i: (i, 0))
    return pl.pallas_call(
        body, name=name, grid=(2, nt),
        in_specs=[full, part, part, full, full], out_specs=[full] * 4,
        out_shape=[jax.ShapeDtypeStruct((r, ccols), F32)] * 4,
    )(w, lo, hi, m, v)


def adamw_plain(w, g, m, v, name):
    r, ccols = w.shape
    tr = _row_tile(r)

    def body(w_ref, g_ref, m_ref, v_ref, d_out, m_out, v_out):
        d, m2, v2 = _adamw_math(w_ref[...], g_ref[...], m_ref[...], v_ref[...])
        d_out[...] = d
        m_out[...] = m2
        v_out[...] = v2

    spec = pl.BlockSpec((tr, ccols), lambda i: (i, 0))
    return pl.pallas_call(
        body, name=name, grid=(r // tr,), in_specs=[spec] * 4, out_specs=[spec] * 3,
        out_shape=[jax.ShapeDtypeStruct((r, ccols), F32)] * 3,
    )(w, g, m, v)


def _silu(x):
    return x * jax.nn.sigmoid(x)


def mod_shard(cin, w_ada, b_shard):
    def body(c_ref, w_ref, b_ref, o_ref):
        o_ref[...] = _nn(_silu(c_ref[...]), w_ref[...]) + b_ref[...]

    return pl.pallas_call(
        body, name="mod_shard", grid=(3,),
        in_specs=[pl.BlockSpec((32, D), lambda j: (0, 0)), pl.BlockSpec((D, 512), lambda j: (0, j)),
                  pl.BlockSpec((1, 512), lambda j: (0, j))],
        out_specs=pl.BlockSpec((32, 512), lambda j: (0, j)),
        out_shape=jax.ShapeDtypeStruct((32, 1536), F32),
    )(cin, w_ada, b_shard)


def ada_grads(cin, gb, gc, w_ada):
    def body(c_ref, gb_ref, gc_ref, w_ref, gw_ref, pc_ref):
        ctx_tot = jnp.sum(gc_ref[...], axis=0, keepdims=True)
        rows = lax.broadcasted_iota(jnp.int32, (16, 512), 0)
        dm = jnp.concatenate([gb_ref[...], jnp.where(rows == 0, ctx_tot, 0.0)], axis=0)
        gw_ref[...] = _tn(_silu(c_ref[...]), dm)
        rows8 = lax.broadcasted_iota(jnp.int32, (8, 512), 0)
        part = _nt(jnp.where(rows8 == 0, ctx_tot, 0.0), w_ref[...])

        @pl.when(pl.program_id(0) == 0)
        def _():
            pc_ref[...] = jnp.zeros_like(pc_ref)

        pc_ref[...] += part

    return pl.pallas_call(
        body, name="ada_grads", grid=(3,),
        in_specs=[pl.BlockSpec((32, D), lambda j: (0, 0)), pl.BlockSpec((16, 512), lambda j: (0, j)),
                  pl.BlockSpec((8, 512), lambda j: (0, j)), pl.BlockSpec((D, 512), lambda j: (0, j))],
        out_specs=[pl.BlockSpec((D, 512), lambda j: (0, j)), pl.BlockSpec((8, D), lambda j: (0, 0))],
        out_shape=[jax.ShapeDtypeStruct((D, 1536), F32), jax.ShapeDtypeStruct((8, D), F32)],
    )(cin, gb, gc, w_ada)


SMALL_ROWS = 24


def small_update(gsm, gbf, gcf, pcg, w_pack, m_pack, v_pack):
    def body(gsm_ref, gbf_ref, gcf_ref, pcg_ref, w_ref, m_ref, v_ref, g_out, d_out, m_out, v_out, loss_out):
        g_out[...] = jnp.zeros_like(g_out)
        c_ctx = w_ref[0:1, :]
        sg = jax.nn.sigmoid(c_ctx)
        dsilu = pcg_ref[0:1, :] + pcg_ref[2:3, :] + pcg_ref[4:5, :] + pcg_ref[6:7, :]
        g_out[0:1, :] = dsilu * (sg * (1.0 + c_ctx * (1.0 - sg)))
        gb = jnp.sum(gbf_ref[...], axis=0, keepdims=True) + jnp.sum(gcf_ref[...], axis=0, keepdims=True)
        for j in range(6):
            g_out[1 + j:2 + j, :] = gb[:, j * D:(j + 1) * D]
        acc = gsm_ref[0]
        for dev in range(1, 8):
            acc = acc + gsm_ref[dev]
        g_out[7:8, :] = acc[0:1, :] + acc[1:2, :]
        g_out[8:16, :] = acc[2:10, :]
        lane = lax.broadcasted_iota(jnp.int32, (1, D), 1)
        last = acc[10:11, :]
        g_out[16:17, :] = jnp.where(lane < 8, last, 0.0)
        loss_out[...] = jnp.broadcast_to(jnp.sum(jnp.where(lane == 8, last, 0.0), axis=1, keepdims=True), (8, 128))
        g = g_out[...]
        d, m2, v2 = _adamw_math(w_ref[...], g, m_ref[...], v_ref[...])
        d_out[...] = d
        m_out[...] = m2
        v_out[...] = v2

    pack = jax.ShapeDtypeStruct((SMALL_ROWS, D), F32)
    return pl.pallas_call(
        body, name="small_update", out_shape=[pack, pack, pack, pack, jax.ShapeDtypeStruct((8, 128), F32)],
    )(gsm, gbf, gcf, pcg, w_pack, m_pack, v_pack)


def _pad_row(v, rows):
    flat = v.reshape(-1)
    return jnp.pad(flat, (0, rows * D - flat.shape[0])).reshape(rows, D)


def _pack_small(c_ctx, b_ada, g1, g2, g3, g4, ret_gn, na_rpb, ret_decay):
    parts = [_pad_row(c_ctx, 1), _pad_row(b_ada, 6), _pad_row(g1, 1), _pad_row(g2, 1), _pad_row(g3, 1),
             _pad_row(g4, 1), _pad_row(ret_gn, 1), _pad_row(na_rpb, 4), _pad_row(ret_decay, 1),
             jnp.zeros((SMALL_ROWS - 17, D), F32)]
    return jnp.concatenate(parts, axis=0)


def _unpack_small(p):
    return (p[0], p[1:7].reshape(1, 6 * D), p[7:8], p[8:9], p[9:10], p[10:11], p[11:12, :512],
            p[12:16].reshape(-1)[:8 * 15 * 31].reshape(1, 8, 15, 31), p[16, :8].reshape(1, 2, 4))


def local_step(x, ctx, tgt, mod3, g_pre_mix, g_post_mix, g_pre_mlp, g_post_mlp, ret_decay, ret_gn, na_rpb,
               wperm, late_weights, early_grads):
    nb = x.shape[0]
    tokens = nb * SEQ
    cos, sin = _rope_tables()
    rd = ret_decay.T.reshape(RH, 2, 1)
    gn = ret_gn.reshape(RH, 1, RD)
    bias = na_bias_table(_bias_patterns(na_rpb))
    h, pret, pna = premix_proj(x, mod3, g_pre_mix, wperm, False, "premix_proj")
    hc, pretc, pnac = premix_proj(ctx, mod3, g_pre_mix, wperm, True, "premix_proj_ctx")
    o_all, mixin, gw_out = retention_fwd(pret, pretc, rd, gn, cos, sin, late_weights(0))
    mixin, gw1, gw2 = na_fwd(pna, pnac, bias, mixin, late_weights(1))
    w1_b = unpack_cols(gw1.reshape(4, D, D), "unpack_w_mlp1")
    dx_tail, dmix, h2, du, act, dm, dmixin, dmod_t, dg_t, loss_t = tail_fwd_bwd(
        x, mixin, tgt, mod3, g_post_mix, g_pre_mlp, g_post_mlp, gw_out.reshape(D, D), w1_b, gw2.reshape(DFF, D))
    dw_out = weight_grad([(mixin.reshape(tokens, D), dmix.reshape(tokens, D))], "grad_w_out")
    dw1 = weight_grad([(h2.reshape(tokens, D), du.reshape(tokens, DFF))], "grad_w_mlp1")
    dw2 = weight_grad([(act.reshape(tokens, DFF), dm.reshape(tokens, D))], "grad_w_mlp2")
    dproj, dprojc, drd, dgn = retention_bwd(pret, pretc, o_all, dmixin, rd, gn, cos, sin)
    dproj, dprojc, dpat, *early = na_bwd(pna, pnac, bias, dmixin, dproj, dprojc, early_grads(dw_out, dw1, dw2))
    grad_x, dmod_a, dg_a = premix_bwd(x, mod3, g_pre_mix, wperm, dproj, dx_tail, "premix_bwd")
    dmod_c, dg_c = premix_bwd(ctx, mod3, g_pre_mix, wperm, dprojc, None, "premix_bwd_ctx")
    dw_in = weight_grad([(h.reshape(tokens, D), dproj.reshape(tokens, IN_W)),
                         (hc.reshape(nb * LC, D), dprojc.reshape(nb * LC, IN_W))], "grad_w_in", tn=512)
    dmod = jnp.concatenate([jnp.concatenate([dmod_a[:, 0:2], dmod_t[:, 2:6]], axis=1), dmod_c], axis=0)
    last = jnp.pad(jnp.concatenate([drd[:, :, 0].T.reshape(8), loss_t[0, 0:1]]), (0, D - 9)).reshape(1, D)
    small = jnp.concatenate([
        dg_a[0:1], dg_c[0:1], dg_t[0:3], _pad_row(dgn, 1), _pad_row(_bias_patterns_t(dpat), 4), last], axis=0)
    return grad_x, dw_in, early, dmod, small


def kernel(x, c, ctx, c_ctx, w_ada, b_ada, g_pre_mix, g_post_mix, g_pre_mlp, g_post_mlp, w_in, ret_decay, ret_gn, na_rpb, w_out, w_mlp1, w_mlp2, loss_target, m_c_ctx, m_w_ada, m_b_ada, m_g_pre_mix, m_g_post_mix, m_g_pre_mlp, m_g_post_mlp, m_w_in, m_ret_decay, m_ret_gn, m_na_rpb, m_w_out, m_w_mlp1, m_w_mlp2, v_c_ctx, v_w_ada, v_b_ada, v_g_pre_mix, v_g_post_mix, v_g_pre_mlp, v_g_post_mlp, v_w_in, v_ret_decay, v_ret_gn, v_na_rpb, v_w_out, v_w_mlp1, v_w_mlp2):
    px, py, pc = _place()
    dev = 4 * px + 2 * py + pc
    chip = 2 * px + py

    def my_half(w2d):
        rows = w2d.shape[0] // 2
        return lax.dynamic_slice_in_dim(w2d, pc * rows, rows, 0)

    halves = [my_half(w[0]).astype(BF16) for w in (w_in, w_out, w_mlp1, w_mlp2)]
    gw_in, cg = all_gather8([halves[0], jnp.pad(c, ((0, 6), (0, 0)))], "gather_w_in")
    wperm = unpack_w_in(gw_in.reshape(4, D, 896))

    cin = jnp.pad(cg[:, 0:2].reshape(16, D), ((0, 16), (0, 0))) + jnp.pad(c_ctx[None], ((16, 15), (0, 0)))
    mod_mine = mod_shard(cin, w_ada[0], lax.dynamic_slice_in_dim(b_ada, chip * 1536, 1536, 1))
    (mg,) = all_gather8([mod_mine], "gather_mod")
    mod_all = jnp.concatenate([mg[0], mg[2], mg[4], mg[6]], axis=1)
    mod3 = (jnp.pad(lax.dynamic_slice_in_dim(mod_all, 2 * dev, 2, 0), ((0, 1), (0, 0)))
            + jnp.pad(mod_all[16:17], ((2, 0), (0, 0)))).reshape(3, 6, D)

    def chip_partials(g8, names, tag):
        landed = sibling_blocks(g8, "rs_sibling_" + tag)
        return [chip_partial(lax.dynamic_index_in_dim(g.reshape(4, 2, *g.shape[1:]), pc, 1, keepdims=False), l,
                             "rs_chip_sum_" + n) for g, l, n in zip(g8, landed, names)]

    partial = [None] * 4

    def early_grads(dw_out, dw1, dw2):
        g8 = [cast_rows(dw_out, "cast_g_w_out").reshape(8, 128, D), pack_cols(dw1, "pack_g_w_mlp1").reshape(8, 512, D),
              cast_rows(dw2, "cast_g_w_mlp2").reshape(8, 512, D)]
        partial[1:] = chip_partials(g8, ["w_out", "w_mlp1", "w_mlp2"], "early")
        return chips3(partial[1:])

    grad_x, dw_in, landed3, dmod, small = local_step(
        x, ctx, loss_target, mod3, g_pre_mix, g_post_mix, g_pre_mlp, g_post_mlp, ret_decay[0], ret_gn, na_rpb[0],
        wperm, lambda k: gather8(halves[1:2] if k == 0 else halves[2:4]), early_grads)

    names = ["w_in", "w_out", "w_mlp1", "w_mlp2"]
    partial[0:1] = chip_partials([pack_w_in(dw_in).reshape(8, 512, 896)], ["w_in"], "w_in")
    landed3 = list(chip_exchange(partial[0:1], "rs_chips_w_in")) + list(landed3)
    mine = [shard_sum(lax.dynamic_index_in_dim(p, chip, 0, keepdims=False), l, "rs_shard_sum_" + n)
            for p, l, n in zip(partial, landed3, names)]
    theirs = sibling_exchange(mine, "rs_halves")
    big = []
    for a, (w, m, v) in enumerate(((w_in, m_w_in, v_w_in), (w_out, m_w_out, v_w_out),
                                   (w_mlp1, m_w_mlp1, v_w_mlp1), (w_mlp2, m_w_mlp2, v_w_mlp2))):
        lo = jnp.where(pc == 0, mine[a], theirs[a])
        hi = jnp.where(pc == 0, theirs[a], mine[a])
        big.append([r[None] for r in adamw_halves(w[0], lo, hi, m[0], v[0], "adamw_" + names[a])])

    pay = jnp.concatenate([dmod.reshape(18, D), small, jnp.zeros((3, D), F32)], axis=0)
    (gs,) = all_gather8([pay], "gather_small")
    gbf = gs[:, 0:12].reshape(16, 6 * D)
    gcf = gs[:, 12:18].reshape(8, 6 * D)
    gw_ada, pc_part = ada_grads(cin, lax.dynamic_slice_in_dim(gbf, chip * 1536, 1536, 1),
                                lax.dynamic_slice_in_dim(gcf, chip * 1536, 1536, 1), w_ada[0])
    (pcg,) = all_gather8([pc_part], "gather_c_ctx")
    d_ada, m_ada, v_ada = adamw_plain(w_ada[0], gw_ada, m_w_ada[0], v_w_ada[0], "adamw_w_ada")
    w_pack = _pack_small(c_ctx, b_ada, g_pre_mix, g_post_mix, g_pre_mlp, g_post_mlp, ret_gn, na_rpb, ret_decay)
    m_pack = _pack_small(m_c_ctx, m_b_ada, m_g_pre_mix, m_g_post_mix, m_g_pre_mlp, m_g_post_mlp, m_ret_gn, m_na_rpb,
                         m_ret_decay)
    v_pack = _pack_small(v_c_ctx, v_b_ada, v_g_pre_mix, v_g_post_mix, v_g_pre_mlp, v_g_post_mlp, v_ret_gn, v_na_rpb,
                         v_ret_decay)
    g_pack, d_pack, m2_pack, v2_pack, loss8 = small_update(gs[:, 18:29], gbf, gcf, pcg[:, 0], w_pack, m_pack, v_pack)

    def leaves(pack, ada, idx):
        s_c, s_b, s_g1, s_g2, s_g3, s_g4, s_gn, s_rpb, s_rd = _unpack_small(pack)
        return [s_c, ada[None], s_b, s_g1, s_g2, s_g3, s_g4, big[0][idx], s_rd, s_gn, s_rpb,
                big[1][idx], big[2][idx], big[3][idx]]

    return (loss8[0, 0], grad_x, *leaves(g_pack, gw_ada, 0), *leaves(d_pack, d_ada, 1),
            *leaves(m2_pack, m_ada, 2), *leaves(v2_pack, v_ada, 3))
```

```python
import functools

import jax
import jax.numpy as jnp
from jax import lax
from jax.experimental import pallas as pl
from jax.experimental.pallas import tpu as pltpu

F32, BF16 = jnp.float32, jnp.bfloat16
D = 1024
SEQ = 2048
LC = 256
GW = 64
RH, RD, CH = 4, 128, 128
NPAIR = 4
IN_W = 3584
RET_W = 2048
DFF = 4096
EPS = 1e-6
NEG = -1e30
TN = 256
NCH = SEQ // CH
LR, B1, B2, AEPS, WD, STEP = 0.001, 0.9, 0.999, 1e-08, 0.01, 10
MESH = pl.DeviceIdType.MESH
VMEM_LIMIT = 56 * 1024 * 1024


def _cp(sem=None):
    return pltpu.CompilerParams(dimension_semantics=sem, vmem_limit_bytes=VMEM_LIMIT)


def _nn(a, b):
    return jnp.dot(a.astype(BF16), b.astype(BF16), preferred_element_type=F32)


def _nt(a, b):
    return lax.dot_general(a.astype(BF16), b.astype(BF16), (((1,), (1,)), ((), ())), preferred_element_type=F32)


def _tn(a, b):
    return lax.dot_general(a.astype(BF16), b.astype(BF16), (((0,), (0,)), ((), ())), preferred_element_type=F32)


@jax.custom_vjp
def mm_nn(a, b):
    return _nn(a, b)


@jax.custom_vjp
def mm_nt(a, b):
    return _nt(a, b)


@jax.custom_vjp
def mm_tn(a, b):
    return _tn(a, b)


mm_nn.defvjp(lambda a, b: (_nn(a, b), (a, b)), lambda r, g: (_nt(g, r[1]), _tn(r[0], g)))
mm_nt.defvjp(lambda a, b: (_nt(a, b), (a, b)), lambda r, g: (_nn(g, r[1]), _tn(g, r[0])))
mm_tn.defvjp(lambda a, b: (_tn(a, b), (a, b)), lambda r, g: (_nt(r[1], g), _nn(r[0], g)))


def _rms(x):
    return x * lax.rsqrt(jnp.mean(x * x, axis=-1, keepdims=True) + EPS)


def _rms_mod(x, g, sc, sh):
    return (_rms(x) * g) * (1.0 + sc) + sh


def _post_mix(x, mix, gt1, sc2, sh2, g_post_mix, g_pre_mlp):
    x1 = x + gt1 * (_rms(mix) * g_post_mix)
    return x1, _rms_mod(x1, g_pre_mlp, sc2, sh2)


def _head_loss(x1, m, gt2, g_post_mlp, tgt):
    err = x1 + gt2 * (_rms(m) * g_post_mlp) - tgt
    return 0.5 * jnp.sum(jnp.mean(err * err, axis=-1, keepdims=True), axis=0, keepdims=True)


def _ln_gate(o, g, w):
    mu = jnp.mean(o, axis=-1, keepdims=True)
    var = jnp.mean(jnp.square(o - mu), axis=-1, keepdims=True)
    y = (o - mu) * lax.rsqrt(var + EPS)
    return (y * w) * (g * jax.nn.sigmoid(g))


def _swap32(x):
    lane = lax.broadcasted_iota(jnp.int32, x.shape, 1)
    return jnp.where((lane & 32) == 0, pltpu.roll(x, 96, 1), pltpu.roll(x, 32, 1))


def _rope(x, cos, sin):
    return x * cos + _swap32(x) * sin


def _rope_t(g, cos, sin):
    return g * cos + _swap32(g * sin)


def _rope_tables():
    tok = jnp.arange(SEQ)
    pos_r = (tok // GW).astype(F32)
    pos_c = (tok % GW).astype(F32)
    inv = 10000.0 ** (-jnp.arange(32, dtype=F32) / 32)
    ar = pos_r[:, None] * inv[None, :]
    ac = pos_c[:, None] * inv[None, :]
    cos = jnp.concatenate([jnp.cos(ar), jnp.cos(ar), jnp.cos(ac), jnp.cos(ac)], axis=-1)
    sin = jnp.concatenate([-jnp.sin(ar), jnp.sin(ar), -jnp.sin(ac), jnp.sin(ac)], axis=-1)
    return cos, sin


def _fiota(shape, dim):
    return lax.broadcasted_iota(jnp.int32, shape, dim).astype(F32)


def _ret_state(k, v, s, lg, reverse):
    pos = _fiota((CH, 1), 0)
    b_exp = pos if reverse else (CH - 1.0 - pos)
    return jnp.exp(lg * CH) * s + mm_tn(k * jnp.exp(lg * b_exp), v)


def _ret_chunk(q, k, v, s, lg, reverse):
    i = _fiota((CH, CH), 0)
    j = _fiota((CH, CH), 1)
    diff = (j - i) if reverse else (i - j)
    mask = (diff > 0) if reverse else (diff >= 0)
    decay = jnp.where(mask, jnp.exp(lg * jnp.where(mask, diff, 0.0)), 0.0)
    pos = _fiota((CH, 1), 0)
    a_exp = (CH - pos) if reverse else (pos + 1.0)
    o = mm_nn(mm_nt(q, k) * decay, v) + mm_nn(q * jnp.exp(lg * a_exp), s)
    return o, _ret_state(k, v, s, lg, reverse)


def premix_proj(xin, mod3, g_pre, wperm, is_ctx, name):
    nb, length, _ = xin.shape
    tn = min(TN, length)

    def body(x_ref, mod_ref, g_ref, w_ref, h_ref, pret_ref, pna_ref):
        h = _rms_mod(x_ref[...], g_ref[...], mod_ref[1:2, :], mod_ref[0:1, :])
        hb = h.astype(BF16)
        h_ref[...] = hb
        pret_ref[...] = jnp.dot(hb, w_ref[:, :RET_W], preferred_element_type=F32)
        pna_ref[...] = jnp.dot(hb, w_ref[:, RET_W:], preferred_element_type=F32).astype(BF16)

    return pl.pallas_call(
        body, name=name, grid=(nb, length // tn),
        in_specs=[
            pl.BlockSpec((None, tn, D), lambda b, t: (b, t, 0)),
            pl.BlockSpec((None, 6, D), (lambda b, t: (2, 0, 0)) if is_ctx else (lambda b, t: (b, 0, 0))),
            pl.BlockSpec((1, D), lambda b, t: (0, 0)),
            pl.BlockSpec((D, IN_W), lambda b, t: (0, 0), pipeline_mode=pl.Buffered(1)),
        ],
        out_specs=[
            pl.BlockSpec((None, tn, D), lambda b, t: (b, t, 0)),
            pl.BlockSpec((None, tn, RET_W), lambda b, t: (b, t, 0)),
            pl.BlockSpec((None, tn, IN_W - RET_W), lambda b, t: (b, t, 0)),
        ],
        out_shape=[
            jax.ShapeDtypeStruct((nb, length, D), BF16),
            jax.ShapeDtypeStruct((nb, length, RET_W), F32),
            jax.ShapeDtypeStruct((nb, length, IN_W - RET_W), BF16),
        ],
        compiler_params=_cp(("arbitrary", "arbitrary")),
    )(xin, mod3, g_pre, wperm)


def premix_bwd(xin, mod3, g_pre, wperm, dproj, dx_tail, name):
    nb, length, _ = xin.shape
    tn = min(TN, length)
    is_ctx = dx_tail is None

    def body(*refs):
        if is_ctx:
            x_ref, mod_ref, g_ref, w_ref, dp_ref, dmod_ref, dg_ref = refs
        else:
            x_ref, mod_ref, g_ref, w_ref, dp_ref, dxt_ref, dx_ref, dmod_ref, dg_ref = refs
        b, t = pl.program_id(0), pl.program_id(1)
        dh = lax.dot_general(dp_ref[...], w_ref[...], (((1,), (1,)), ((), ())), preferred_element_type=F32)
        _, vjp = jax.vjp(_rms_mod, x_ref[...], g_ref[...], mod_ref[1:2, :], mod_ref[0:1, :])
        dx, dg, dsc, dsh = vjp(dh)
        if not is_ctx:
            dx_ref[...] = dx + dxt_ref[...]

        @pl.when((t == 0) & ((b == 0) if is_ctx else True))
        def _():
            dmod_ref[...] = jnp.zeros_like(dmod_ref)

        @pl.when((t == 0) & (b == 0))
        def _():
            dg_ref[...] = jnp.zeros_like(dg_ref)

        dmod_ref[0:1, :] += dsh
        dmod_ref[1:2, :] += dsc
        dg_ref[0:1, :] += dg

    tok = lambda b, t: (b, t, 0)
    in_specs = [
        pl.BlockSpec((None, tn, D), tok),
        pl.BlockSpec((None, 6, D), (lambda b, t: (2, 0, 0)) if is_ctx else (lambda b, t: (b, 0, 0))),
        pl.BlockSpec((1, D), lambda b, t: (0, 0)),
        pl.BlockSpec((D, IN_W), lambda b, t: (0, 0), pipeline_mode=pl.Buffered(1)),
        pl.BlockSpec((None, tn, IN_W), tok),
    ]
    args = [xin, mod3, g_pre, wperm, dproj]
    out_specs = [
        pl.BlockSpec((None, 6, D), (lambda b, t: (0, 0, 0)) if is_ctx else (lambda b, t: (b, 0, 0))),
        pl.BlockSpec((8, D), lambda b, t: (0, 0)),
    ]
    out_shape = [jax.ShapeDtypeStruct((1 if is_ctx else nb, 6, D), F32), jax.ShapeDtypeStruct((8, D), F32)]
    if not is_ctx:
        in_specs.append(pl.BlockSpec((None, tn, D), tok))
        args.append(dx_tail)
        out_specs.insert(0, pl.BlockSpec((None, tn, D), tok))
        out_shape.insert(0, jax.ShapeDtypeStruct((nb, length, D), F32))
    return pl.pallas_call(
        body, name=name, grid=(nb, length // tn), in_specs=in_specs, out_specs=out_specs, out_shape=out_shape,
        compiler_params=_cp(("arbitrary", "arbitrary")),
    )(*args)


def _ret_specs(order):
    def im(f):
        return lambda *g: f(*order(*g))
    return dict(
        pret=pl.BlockSpec((None, SEQ, 512), im(lambda b, h: (b, 0, h))),
        pretc=pl.BlockSpec((None, LC, 512), im(lambda b, h: (b, 0, h))),
        rd=pl.BlockSpec((None, 2, 1), im(lambda b, h: (h, 0, 0))),
        gn=pl.BlockSpec((None, 1, RD), im(lambda b, h: (h, 0, 0))),
        tab=pl.BlockSpec((SEQ, RD), im(lambda b, h: (0, 0))),
        head=pl.BlockSpec((None, SEQ, RD), im(lambda b, h: (b, 0, h))),
    )


def retention_fwd(pret, pretc, rd, gn, cos, sin, hosted):
    nb = pret.shape[0]
    sp = _ret_specs(lambda b, h: (b, h))

    def body(*refs):
        own_in, h_in, own_out, h_out, own_scr, h_sems = hosted.split(refs, 6, 2)
        p_ref, pc_ref, rd_ref, gn_ref, cos_ref, sin_ref = own_in
        (o_ref, mix_ref), (q_s, k_s, of_s, ob_s) = own_out, own_scr
        grid_step = pl.program_id(0) * RH + pl.program_id(1)

        @pl.when(grid_step == 0)
        def _():
            hosted.start(h_in, h_out, h_sems)

        cos_v, sin_v = cos_ref[...], sin_ref[...]
        q_s[...] = _rope(p_ref[:, 0:128], cos_v, sin_v) * (RD ** -0.5)
        k_s[...] = _rope(p_ref[:, 128:256], cos_v, sin_v)
        lgs, init = [], []
        for rev in (False, True):
            lg = jax.nn.log_sigmoid(rd_ref[int(rev):int(rev) + 1, :])
            s = jnp.zeros((RD, RD), F32)
            for n in ((1, 0) if rev else (0, 1)):
                s = _ret_state(pc_ref[n * CH:(n + 1) * CH, 128:256], pc_ref[n * CH:(n + 1) * CH, 256:384], s, lg, rev)
            lgs.append(lg)
            init.append(s)

        def step(t, carry):
            out = []
            for rev, o_s, s in ((False, of_s, carry[0]), (True, ob_s, carry[1])):
                n = (NCH - 1 - t) if rev else t
                sl = pl.ds(pl.multiple_of(n * CH, CH), CH)
                o, s2 = _ret_chunk(q_s[sl, :], k_s[sl, :], p_ref[sl, 256:384], s, lgs[int(rev)], rev)
                o_s[sl, :] = o
                out.append(s2)
            return tuple(out)

        lax.fori_loop(0, NCH, step, tuple(init))
        o = of_s[...] + ob_s[...]
        o_ref[...] = o
        mix_ref[...] = _ln_gate(o, p_ref[:, 384:512], gn_ref[...]).astype(BF16)

        @pl.when(grid_step == nb * RH - 1)
        def _():
            hosted.finish(h_in, h_out, h_sems)

    h_in_specs, h_out_specs = hosted.specs()
    return pl.pallas_call(
        body, name="retention_fwd", grid=(nb, RH),
        in_specs=[sp["pret"], sp["pretc"], sp["rd"], sp["gn"], sp["tab"], sp["tab"]] + h_in_specs,
        out_specs=[sp["head"], sp["head"]] + h_out_specs,
        out_shape=[jax.ShapeDtypeStruct((nb, SEQ, RH * RD), F32), jax.ShapeDtypeStruct((nb, SEQ, D), BF16)]
        + hosted.out_shape,
        scratch_shapes=[pltpu.VMEM((SEQ, RD), F32)] * 4 + hosted.scratch,
        compiler_params=_cp(("arbitrary", "arbitrary")),
    )(pret, pretc, rd, gn, cos, sin, *hosted.args)


def retention_bwd(pret, pretc, o_all, dmixin, rd, gn, cos, sin):
    nb = pret.shape[0]
    sp = _ret_specs(lambda h, b: (b, h))

    def body(p_ref, pc_ref, o_ref, dmix_ref, rd_ref, gn_ref, cos_ref, sin_ref,
             dp_ref, dpc_ref, drd_ref, dgn_ref, q_s, k_s, do_s, dqf_s, dkf_s, dvf_s, dqb_s, dkb_s, dvb_s, stf_s, stb_s):
        b = pl.program_id(1)
        cos_v, sin_v = cos_ref[...], sin_ref[...]
        q_s[...] = _rope(p_ref[:, 0:128], cos_v, sin_v) * (RD ** -0.5)
        k_s[...] = _rope(p_ref[:, 128:256], cos_v, sin_v)
        _, gate_vjp = jax.vjp(_ln_gate, o_ref[...], p_ref[:, 384:512], gn_ref[...])
        do, dg, dgn = gate_vjp(dmix_ref[...].astype(F32))
        do_s[...] = do
        dp_ref[:, 384:512] = dg.astype(BF16)

        @pl.when(b == 0)
        def _():
            drd_ref[...] = jnp.zeros_like(drd_ref)
            dgn_ref[...] = jnp.zeros_like(dgn_ref)

        dgn_ref[...] += dgn
        kcs = [pc_ref[n * CH:(n + 1) * CH, 128:256] for n in (0, 1)]
        vcs = [pc_ref[n * CH:(n + 1) * CH, 256:384] for n in (0, 1)]
        dirs = []
        init = []
        for rev in (False, True):
            rdv = rd_ref[int(rev):int(rev) + 1, :]
            lg = jax.nn.log_sigmoid(rdv)
            order_c = (1, 0) if rev else (0, 1)
            s = jnp.zeros((RD, RD), F32)
            ctx_states = []
            for n in order_c:
                ctx_states.append(s)
                s = _ret_state(kcs[n], vcs[n], s, lg, rev)
            dirs.append((rev, order_c, lg, rdv, ctx_states))
            init.append(s)
        acc = ((dqf_s, dkf_s, dvf_s, stf_s), (dqb_s, dkb_s, dvb_s, stb_s))

        def fstep(t, carry):
            out = []
            for (rev, _, lg, _, _), (_, _, _, st_s), s in zip(dirs, acc, carry):
                n = (NCH - 1 - t) if rev else t
                sl = pl.ds(pl.multiple_of(n * CH, CH), CH)
                st_s[n] = s
                out.append(_ret_state(k_s[sl, :], p_ref[sl, 256:384], s, lg, rev))
            return tuple(out)

        lax.fori_loop(0, NCH, fstep, tuple(init))

        def bstep(t, carry):
            out = []
            for (rev, _, lg, _, _), (dq_s, dk_s, dv_s, st_s), (ds, dlg) in zip(dirs, acc, carry):
                n = t if rev else (NCH - 1 - t)
                sl = pl.ds(pl.multiple_of(n * CH, CH), CH)
                _, vjp = jax.vjp(functools.partial(_ret_chunk, reverse=rev),
                                 q_s[sl, :], k_s[sl, :], p_ref[sl, 256:384], st_s[n], lg)
                dq, dk, dv, ds_prev, dl = vjp((do_s[sl, :], ds))
                dq_s[sl, :] = dq
                dk_s[sl, :] = dk
                dv_s[sl, :] = dv
                out.append((ds_prev, dlg + dl))
            return tuple(out)

        zero_c = (jnp.zeros((RD, RD), F32), jnp.zeros((1, 1), F32))
        res = lax.fori_loop(0, NCH, bstep, (zero_c, zero_c))
        dkc = [None, None]
        dvc = [None, None]
        for (rev, order_c, lg, rdv, ctx_states), (ds, dlg) in zip(dirs, res):
            for idx in (1, 0):
                n = order_c[idx]
                _, vjp = jax.vjp(functools.partial(_ret_state, reverse=rev), kcs[n], vcs[n], ctx_states[idx], lg)
                dk_c, dv_c, ds, dl = vjp(ds)
                dlg = dlg + dl
                dkc[n] = dk_c if dkc[n] is None else dkc[n] + dk_c
                dvc[n] = dv_c if dvc[n] is None else dvc[n] + dv_c
            drd_ref[int(rev):int(rev) + 1, :] += dlg * jax.nn.sigmoid(-rdv)
        dp_ref[:, 0:128] = _rope_t((dqf_s[...] + dqb_s[...]) * (RD ** -0.5), cos_v, sin_v).astype(BF16)
        dp_ref[:, 128:256] = _rope_t(dkf_s[...] + dkb_s[...], cos_v, sin_v).astype(BF16)
        dp_ref[:, 256:384] = (dvf_s[...] + dvb_s[...]).astype(BF16)
        zero = jnp.zeros((CH, RD), BF16)
        for n in (0, 1):
            rows = slice(n * CH, (n + 1) * CH)
            dpc_ref[rows, 0:128] = zero
            dpc_ref[rows, 128:256] = dkc[n].astype(BF16)
            dpc_ref[rows, 256:384] = dvc[n].astype(BF16)
            dpc_ref[rows, 384:512] = zero

    return pl.pallas_call(
        body, name="retention_bwd", grid=(RH, nb),
        in_specs=[sp["pret"], sp["pretc"], sp["head"], sp["head"], sp["rd"], sp["gn"], sp["tab"], sp["tab"]],
        out_specs=[
            pl.BlockSpec((None, SEQ, 512), lambda h, b: (b, 0, h)),
            pl.BlockSpec((None, LC, 512), lambda h, b: (b, 0, h)),
            pl.BlockSpec((None, 2, 1), lambda h, b: (h, 0, 0)),
            pl.BlockSpec((None, 1, RD), lambda h, b: (h, 0, 0)),
        ],
        out_shape=[
            jax.ShapeDtypeStruct((nb, SEQ, IN_W), BF16),
            jax.ShapeDtypeStruct((nb, LC, IN_W), BF16),
            jax.ShapeDtypeStruct((RH, 2, 1), F32),
            jax.ShapeDtypeStruct((RH, 1, RD), F32),
        ],
        scratch_shapes=[pltpu.VMEM((SEQ, RD), F32)] * 9 + [pltpu.VMEM((NCH, RD, RD), F32)] * 2,
        compiler_params=_cp(("arbitrary", "arbitrary")),
    )(pret, pretc, o_all, dmixin, rd, gn, cos, sin)


def _rpb_flat(rpb):
    return jnp.pad(rpb, ((0, 0), (0, 1), (0, 33))).reshape(NPAIR, 2, 1, 1024)


def _rpb_flat_t(dflat):
    return dflat.reshape(8, 16, 64)[:, :15, :31]


def _barrel(x, left):
    row = lax.broadcasted_iota(jnp.int32, x.shape, 0)
    n = x.shape[1]
    for bit in range(6):
        s = 1 << bit
        x = jnp.where(((row >> bit) & 1) == 1, pltpu.roll(x, (n - s) if left else s, 1), x)
    return x


def _bias_from_pattern(pat):
    x = _barrel(pltpu.roll(jnp.broadcast_to(pat, (GW, 512)), 512 - 15, 1), left=False)
    qc = lax.broadcasted_iota(jnp.int32, (GW, 512), 0)
    kc = lax.broadcasted_iota(jnp.int32, (GW, 512), 1) & 63
    start = jnp.clip(qc - 8, 0, GW - 16)
    return jnp.where((kc >= start) & (kc < start + 16), x, NEG)


def _pattern_grad(ds_acc):
    return jnp.sum(pltpu.roll(_barrel(ds_acc, left=True), 15, 1), axis=0, keepdims=True)


def _na_row(r):
    rs = jnp.clip(r - 4, 0, 24)
    cls = jnp.where(r < 4, r, jnp.where(r > 28, r - 24, 4))
    return pl.ds(pl.multiple_of(r * GW, GW), GW), pl.ds(pl.multiple_of(rs * GW, GW), 8 * GW), cls


def _na_probs(qst, kb, kc, bias):
    s_loc = _nt(qst, kb) * 0.125 + bias
    s_ctx = _nt(qst, kc) * 0.125
    m = jnp.maximum(jnp.max(s_loc, axis=1, keepdims=True), jnp.max(s_ctx, axis=1, keepdims=True))
    e_loc, e_ctx = jnp.exp(s_loc - m), jnp.exp(s_ctx - m)
    den = jnp.sum(e_loc, axis=1, keepdims=True) + jnp.sum(e_ctx, axis=1, keepdims=True)
    return e_loc / den, e_ctx / den


def _stack_heads(t):
    lane = lax.broadcasted_iota(jnp.int32, t.shape, 1)
    zero = jnp.zeros_like(t)
    return jnp.concatenate([jnp.where(lane < 64, t, zero), jnp.where(lane >= 64, t, zero)], axis=0)


def _unstack_heads(t):
    lane = lax.broadcasted_iota(jnp.int32, (GW, 128), 1)
    return jnp.where(lane < 64, t[:GW], t[GW:])


def na_bias_table(flat):
    def body(flat_ref, out_ref):
        shift = ((pl.program_id(1) + 9) * 64) & 1023
        for hh in (0, 1):
            pat = pltpu.roll(jnp.broadcast_to(flat_ref[hh], (8, 1024)), shift, 1)[0:1, 0:512]
            out_ref[hh * GW:(hh + 1) * GW, :] = _bias_from_pattern(pat)

    return pl.pallas_call(
        body, name="na_bias_table", grid=(NPAIR, 8),
        in_specs=[pl.BlockSpec((None, 2, 1, 1024), lambda p, k: (p, 0, 0, 0))],
        out_specs=pl.BlockSpec((None, None, 2 * GW, 512), lambda p, k: (p, k, 0, 0)),
        out_shape=jax.ShapeDtypeStruct((NPAIR, 8, 2 * GW, 512), F32),
    )(flat)


def na_fwd(pna, pnac, bias, mixin, hosted):
    nb = pna.shape[0]

    def body(*refs):
        (p_ref, pc_ref, bias_ref, _), h_in, (out_ref,), h_out, _, h_sems = hosted.split(refs, 4, 1)
        grid_step = pl.program_id(0) * nb + pl.program_id(1)

        @pl.when(grid_step == 0)
        def _():
            hosted.start(h_in, h_out, h_sems)

        kc, vc = pc_ref[:, 128:256], pc_ref[:, 256:384]

        def row(r, carry):
            qsl, bsl, cls = _na_row(r)
            kb, vb = p_ref[bsl, 128:256], p_ref[bsl, 256:384]
            p_loc, p_ctx = _na_probs(_stack_heads(p_ref[qsl, 0:128]), kb, kc, bias_ref[cls])
            out_ref[qsl, :] = _unstack_heads(_nn(p_loc, vb) + _nn(p_ctx, vc)).astype(BF16)
            return carry

        lax.fori_loop(0, SEQ // GW, row, 0, unroll=2)

        @pl.when(grid_step == NPAIR * nb - 1)
        def _():
            hosted.finish(h_in, h_out, h_sems)

    h_in_specs, h_out_specs = hosted.specs()
    return pl.pallas_call(
        body, name="na_fwd", grid=(NPAIR, nb),
        in_specs=[
            pl.BlockSpec((None, SEQ, 384), lambda p, b: (b, 0, p)),
            pl.BlockSpec((None, LC, 384), lambda p, b: (b, 0, p)),
            pl.BlockSpec((None, 8, 2 * GW, 512), lambda p, b: (p, 0, 0, 0)),
            pl.BlockSpec(memory_space=pl.ANY),
        ] + h_in_specs,
        out_specs=[pl.BlockSpec((None, SEQ, 128), lambda p, b: (b, 0, 4 + p))] + h_out_specs,
        out_shape=[jax.ShapeDtypeStruct((nb, SEQ, D), BF16)] + hosted.out_shape,
        input_output_aliases={3: 0},
        scratch_shapes=hosted.scratch,
        compiler_params=_cp(("arbitrary", "arbitrary")),
    )(pna, pnac, bias, mixin, *hosted.args)


def na_bwd(pna, pnac, bias, dmixin, dproj, dprojc, hosted):
    nb = pna.shape[0]

    def body(*refs):
        own_in, h_in, own_out, h_out, own_scr, h_sems = hosted.split(refs, 6, 3)
        p_ref, pc_ref, bias_ref, dmix_ref = own_in[:4]
        dp_ref, dpc_ref, dpat_ref = own_out
        dbias_s, dk_s, dv_s, dkc_s, dvc_s, res_s, resc_s = own_scr
        b, part = pl.program_id(1), pl.program_id(2)
        grid_step = (pl.program_id(0) * nb + b) * 3 + part

        @pl.when(grid_step == 0)
        def _():
            hosted.start(h_in, h_out, h_sems)

        @pl.when(grid_step == NPAIR * nb * 3 - 1)
        def _():
            hosted.finish(h_in, h_out, h_sems)

        @pl.when(part == 0)
        def _():
            @pl.when(b == 0)
            def _():
                dbias_s[...] = jnp.zeros_like(dbias_s)

            dk_s[...] = jnp.zeros_like(dk_s)
            dv_s[...] = jnp.zeros_like(dv_s)
            dkc_s[...] = jnp.zeros_like(dkc_s)
            dvc_s[...] = jnp.zeros_like(dvc_s)
            kc, vc = pc_ref[:, 128:256], pc_ref[:, 256:384]

            def row(r, carry):
                qsl, bsl, cls = _na_row(r)
                kb, vb = p_ref[bsl, 128:256], p_ref[bsl, 256:384]
                qst, dost = _stack_heads(p_ref[qsl, 0:128]), _stack_heads(dmix_ref[qsl, :])
                p_loc, p_ctx = _na_probs(qst, kb, kc, bias_ref[cls])
                dp_loc, dp_ctx = _nt(dost, vb), _nt(dost, vc)
                delta = (jnp.sum(p_loc * dp_loc, axis=1, keepdims=True)
                         + jnp.sum(p_ctx * dp_ctx, axis=1, keepdims=True))
                ds_loc, ds_ctx = p_loc * (dp_loc - delta), p_ctx * (dp_ctx - delta)
                dbias_s[cls] += ds_loc
                res_s[0, qsl, :] = _unstack_heads((_nn(ds_loc, kb) + _nn(ds_ctx, kc)) * 0.125).astype(BF16)
                dk_s[bsl, :] += _tn(ds_loc, qst) * 0.125
                dv_s[bsl, :] += _tn(p_loc, dost)
                dkc_s[...] += _tn(ds_ctx, qst) * 0.125
                dvc_s[...] += _tn(p_ctx, dost)
                return carry

            lax.fori_loop(0, SEQ // GW, row, 0, unroll=2)
            res_s[1] = dk_s[...].astype(BF16)
            res_s[2] = dv_s[...].astype(BF16)
            resc_s[0] = jnp.zeros((LC, 128), BF16)
            resc_s[1] = dkc_s[...].astype(BF16)
            resc_s[2] = dvc_s[...].astype(BF16)

            @pl.when(b == nb - 1)
            def _():
                for hh in (0, 1):
                    flat = jnp.zeros((1, 1024), F32)
                    for cls in range(8):
                        g = _pattern_grad(dbias_s[cls, hh * GW:(hh + 1) * GW, :])
                        g = jnp.concatenate([g, jnp.zeros((1, 512), F32)], axis=1)
                        flat = flat + (pltpu.roll(g, (7 - cls) * 64, 1) if cls < 7 else g)
                    dpat_ref[hh] = flat

        dp_ref[...] = res_s[part]
        dpc_ref[...] = resc_s[part]

    h_in_specs, h_out_specs = hosted.specs()
    return pl.pallas_call(
        body, name="na_bwd", grid=(NPAIR, nb, 3),
        in_specs=[
            pl.BlockSpec((None, SEQ, 384), lambda p, b, s: (b, 0, p)),
            pl.BlockSpec((None, LC, 384), lambda p, b, s: (b, 0, p)),
            pl.BlockSpec((None, 8, 2 * GW, 512), lambda p, b, s: (p, 0, 0, 0)),
            pl.BlockSpec((None, SEQ, 128), lambda p, b, s: (b, 0, 4 + p)),
            pl.BlockSpec(memory_space=pl.ANY),
            pl.BlockSpec(memory_space=pl.ANY),
        ] + h_in_specs,
        out_specs=[
            pl.BlockSpec((None, SEQ, 128), lambda p, b, s: (b, 0, 16 + 3 * p + s)),
            pl.BlockSpec((None, LC, 128), lambda p, b, s: (b, 0, 16 + 3 * p + s)),
            pl.BlockSpec((None, 2, 1, 1024), lambda p, b, s: (p, 0, 0, 0)),
        ] + h_out_specs,
        out_shape=[
            jax.ShapeDtypeStruct((nb, SEQ, IN_W), BF16),
            jax.ShapeDtypeStruct((nb, LC, IN_W), BF16),
            jax.ShapeDtypeStruct((NPAIR, 2, 1, 1024), F32),
        ] + hosted.out_shape,
        input_output_aliases={4: 0, 5: 1},
        scratch_shapes=[
            pltpu.VMEM((8, 2 * GW, 512), F32),
            pltpu.VMEM((SEQ, 128), F32), pltpu.VMEM((SEQ, 128), F32),
            pltpu.VMEM((LC, 128), F32), pltpu.VMEM((LC, 128), F32),
            pltpu.VMEM((3, SEQ, 128), BF16), pltpu.VMEM((3, LC, 128), BF16),
        ] + hosted.scratch,
        compiler_params=_cp(("arbitrary", "arbitrary", "arbitrary")),
    )(pna, pnac, bias, dmixin, dproj, dprojc, *hosted.args)


def tail_fwd_bwd(x, mixin, tgt, mod3, g_post_mix, g_pre_mlp, g_post_mlp, wout, w1, w2):
    nb = x.shape[0]

    def body(x_ref, mi_ref, tgt_ref, mod_ref, gpm_ref, gpl_ref, gpo_ref, wo_ref, w1_ref, w2_ref,
             dx_ref, dmix_ref, h2_ref, du_ref, a_ref, dm_ref, dmi_ref, dmod_ref, dg_ref, loss_ref):
        b, t = pl.program_id(0), pl.program_id(1)
        gt1, sh2, sc2, gt2 = mod_ref[2:3, :], mod_ref[3:4, :], mod_ref[4:5, :], mod_ref[5:6, :]
        mix = jnp.dot(mi_ref[...], wo_ref[...], preferred_element_type=F32)
        (x1, h2), vjp_a = jax.vjp(_post_mix, x_ref[...], mix, gt1, sc2, sh2, gpm_ref[...], gpl_ref[...])
        h2b = h2.astype(BF16)
        h2_ref[...] = h2b
        m = jnp.zeros((TN, D), F32)
        relus = []
        for j in range(4):
            cols = slice(j * D, (j + 1) * D)
            r = jnp.maximum(jnp.dot(h2b, w1_ref[:, cols], preferred_element_type=F32), 0.0)
            ab = (r * r).astype(BF16)
            a_ref[:, cols] = ab
            m = m + jnp.dot(ab, w2_ref[cols, :], preferred_element_type=F32)
            relus.append(r)
        loss, vjp_b = jax.vjp(_head_loss, x1, m, gt2, gpo_ref[...], tgt_ref[...])
        dx1, dm, dgt2, dgpo, _ = vjp_b(jnp.ones((1, 1), F32))
        dmb = dm.astype(BF16)
        dm_ref[...] = dmb
        dh2 = jnp.zeros((TN, D), F32)
        for j in range(4):
            cols = slice(j * D, (j + 1) * D)
            da = lax.dot_general(dmb, w2_ref[cols, :], (((1,), (1,)), ((), ())), preferred_element_type=F32)
            dub = (da * (2.0 * relus[j])).astype(BF16)
            du_ref[:, cols] = dub
            dh2 = dh2 + lax.dot_general(dub, w1_ref[:, cols], (((1,), (1,)), ((), ())), preferred_element_type=F32)
        dx, dmix, dgt1, dsc2, dsh2, dgpm, dgpl = vjp_a((dx1, dh2))
        dx_ref[...] = dx
        dmixb = dmix.astype(BF16)
        dmix_ref[...] = dmixb
        dmi_ref[...] = lax.dot_general(dmixb, wo_ref[...], (((1,), (1,)), ((), ())),
                                       preferred_element_type=F32).astype(BF16)

        @pl.when(t == 0)
        def _():
            dmod_ref[...] = jnp.zeros_like(dmod_ref)

        @pl.when((t == 0) & (b == 0))
        def _():
            dg_ref[...] = jnp.zeros_like(dg_ref)
            loss_ref[...] = jnp.zeros_like(loss_ref)

        dmod_ref[2:3, :] += dgt1
        dmod_ref[3:4, :] += dsh2
        dmod_ref[4:5, :] += dsc2
        dmod_ref[5:6, :] += dgt2
        dg_ref[0:1, :] += dgpm
        dg_ref[1:2, :] += dgpl
        dg_ref[2:3, :] += dgpo
        loss_ref[...] += jnp.broadcast_to(loss, loss_ref.shape)

    tok = lambda b, t: (b, t, 0)
    const = lambda b, t: (0, 0)
    vec = pl.BlockSpec((1, D), const)
    return pl.pallas_call(
        body, name="tail_fwd_bwd", grid=(nb, SEQ // TN),
        in_specs=[
            pl.BlockSpec((None, TN, D), tok), pl.BlockSpec((None, TN, D), tok), pl.BlockSpec((None, TN, D), tok),
            pl.BlockSpec((None, 6, D), lambda b, t: (b, 0, 0)), vec, vec, vec,
            pl.BlockSpec((D, D), const, pipeline_mode=pl.Buffered(1)),
            pl.BlockSpec((D, DFF), const, pipeline_mode=pl.Buffered(1)),
            pl.BlockSpec((DFF, D), const, pipeline_mode=pl.Buffered(1)),
        ],
        out_specs=[
            pl.BlockSpec((None, TN, D), tok), pl.BlockSpec((None, TN, D), tok), pl.BlockSpec((None, TN, D), tok),
            pl.BlockSpec((None, TN, DFF), tok), pl.BlockSpec((None, TN, DFF), tok), pl.BlockSpec((None, TN, D), tok),
            pl.BlockSpec((None, TN, D), tok),
            pl.BlockSpec((None, 6, D), lambda b, t: (b, 0, 0)),
            pl.BlockSpec((8, D), const), pl.BlockSpec((8, 128), const),
        ],
        out_shape=[
            jax.ShapeDtypeStruct((nb, SEQ, D), F32), jax.ShapeDtypeStruct((nb, SEQ, D), BF16),
            jax.ShapeDtypeStruct((nb, SEQ, D), BF16), jax.ShapeDtypeStruct((nb, SEQ, DFF), BF16),
            jax.ShapeDtypeStruct((nb, SEQ, DFF), BF16), jax.ShapeDtypeStruct((nb, SEQ, D), BF16),
            jax.ShapeDtypeStruct((nb, SEQ, D), BF16),
            jax.ShapeDtypeStruct((nb, 6, D), F32), jax.ShapeDtypeStruct((8, D), F32),
            jax.ShapeDtypeStruct((8, 128), F32),
        ],
        compiler_params=_cp(("arbitrary", "arbitrary")),
    )(x, mixin, tgt, mod3, g_post_mix, g_pre_mlp, g_post_mlp, wout, w1, w2)


def weight_grad(pairs, name, tm=1024, tn=1024, tk=512):
    m, n = pairs[0][0].shape[1], pairs[0][1].shape[1]
    tn = min(tn, n)
    steps = [xa.shape[0] // tk for xa, _ in pairs]
    total = sum(steps)
    offs = [sum(steps[:i]) for i in range(len(pairs))]

    def body(*refs):
        out_ref = refs[-1]
        k = pl.program_id(2)

        @pl.when(k == 0)
        def _():
            out_ref[...] = jnp.zeros_like(out_ref)

        for i in range(len(pairs)):
            @pl.when((k >= offs[i]) & (k < offs[i] + steps[i]))
            def _(i=i):
                out_ref[...] += lax.dot_general(refs[2 * i][...], refs[2 * i + 1][...], (((0,), (0,)), ((), ())),
                                                preferred_element_type=F32)

    in_specs, args = [], []
    for i, (xa, ya) in enumerate(pairs):
        clamp = lambda k, i=i: jnp.clip(k - offs[i], 0, steps[i] - 1)
        in_specs.append(pl.BlockSpec((tk, tm), lambda a, c, k, clamp=clamp: (clamp(k), a)))
        in_specs.append(pl.BlockSpec((tk, tn), lambda a, c, k, clamp=clamp: (clamp(k), c)))
        args += [xa, ya]
    return pl.pallas_call(
        body, name=name, grid=(m // tm, n // tn, total), in_specs=in_specs,
        out_specs=pl.BlockSpec((tm, tn), lambda a, c, k: (a, c)),
        out_shape=jax.ShapeDtypeStruct((m, n), F32),
        compiler_params=_cp(("arbitrary", "arbitrary", "arbitrary")),
    )(*args)


def _perm_block(t):
    u = t - 16
    return jnp.where(t < 16, 4 * (t % 4) + t // 4, 16 + 3 * (u % 4) + u // 4)


def unpack_w_in(blocks):
    def body(i_ref, o_ref):
        o_ref[...] = i_ref[...]

    return pl.pallas_call(
        body, name="unpack_w_in", grid=(28,),
        in_specs=[pl.BlockSpec((None, D, 128), lambda t: (t // 7, 0, t % 7))],
        out_specs=pl.BlockSpec((D, 128), lambda t: (0, _perm_block(t))),
        out_shape=jax.ShapeDtypeStruct((D, IN_W), BF16),
    )(blocks)


def pack_w_in(dw):
    def body(i_ref, o_ref):
        o_ref[...] = i_ref[...].astype(BF16)

    return pl.pallas_call(
        body, name="pack_w_in", grid=(28,),
        in_specs=[pl.BlockSpec((D, 128), lambda t: (0, _perm_block(t)))],
        out_specs=pl.BlockSpec((None, D, 128), lambda t: (t // 7, 0, t % 7)),
        out_shape=jax.ShapeDtypeStruct((4, D, 896), BF16),
    )(dw)


def unpack_cols(blocks, name):
    _, r, c = blocks.shape

    def body(i_ref, o_ref):
        o_ref[...] = i_ref[...]

    return pl.pallas_call(
        body, name=name, grid=(4,),
        in_specs=[pl.BlockSpec((None, r, c), lambda j: (j, 0, 0))],
        out_specs=pl.BlockSpec((r, c), lambda j: (0, j)),
        out_shape=jax.ShapeDtypeStruct((r, 4 * c), blocks.dtype),
    )(blocks)


def pack_cols(w, name):
    r, c4 = w.shape
    c = c4 // 4

    def body(i_ref, o_ref):
        o_ref[...] = i_ref[...].astype(BF16)

    return pl.pallas_call(
        body, name=name, grid=(4,),
        in_specs=[pl.BlockSpec((r, c), lambda j: (0, j))],
        out_specs=pl.BlockSpec((None, r, c), lambda j: (j, 0, 0)),
        out_shape=jax.ShapeDtypeStruct((4, r, c), BF16),
    )(w)


def cast_rows(w, name):
    r, c = w.shape

    def body(i_ref, o_ref):
        o_ref[...] = i_ref[...].astype(BF16)

    return pl.pallas_call(
        body, name=name, grid=(8,),
        in_specs=[pl.BlockSpec((r // 8, c), lambda j: (j, 0))],
        out_specs=pl.BlockSpec((r // 8, c), lambda j: (j, 0)),
        out_shape=jax.ShapeDtypeStruct((r, c), BF16),
    )(w)


def _place():
    return lax.axis_index("x"), lax.axis_index("y"), lax.axis_index("c")


class Hosted:
    def __init__(self, args, out_shape, scratch, start, finish):
        self.args, self.out_shape, self.scratch, self.start, self.finish = args, out_shape, scratch, start, finish

    def specs(self):
        hbm = pl.BlockSpec(memory_space=pl.ANY)
        return [hbm] * len(self.args), [hbm] * len(self.out_shape)

    def split(self, refs, n_in, n_out):
        a, b = len(self.args), len(self.out_shape)
        cuts = [n_in, n_in + a, n_in + a + n_out, n_in + a + n_out + b, len(refs) - len(self.scratch)]
        parts = [refs[i:j] for i, j in zip([0] + cuts, cuts + [len(refs)])]
        return parts[0], parts[1], parts[2], parts[3], parts[4], parts[5]


def run_hosted(hosted, name):
    def body(*refs):
        _, ins, _, outs, _, sems = hosted.split(refs, 0, 0)
        hosted.start(ins, outs, sems)
        hosted.finish(ins, outs, sems)

    in_specs, out_specs = hosted.specs()
    return pl.pallas_call(body, name=name, in_specs=in_specs, out_specs=out_specs, out_shape=hosted.out_shape,
                          scratch_shapes=hosted.scratch)(*hosted.args)


def gather8(blocks):
    na = len(blocks)

    def copies(ins, outs, sems):
        send_sems, recv_sems, local_sem = sems
        x, y, c = _place()
        me, sibling = (x, y, c), (x, y, 1 - c)
        chips = [(1 - x, y), (x, 1 - y), (1 - x, 1 - y)]

        def slot(o_ref, px, py, pc):
            return o_ref.at[4 * px + 2 * py + pc]

        def copy(a, k, block, to, src=None):
            return pltpu.make_async_remote_copy(
                src_ref=slot(outs[a], *block) if src is None else src, dst_ref=slot(outs[a], *block),
                send_sem=send_sems.at[a, k], recv_sem=recv_sems.at[a, k], device_id=to, device_id_type=MESH)

        mine = [pltpu.make_async_copy(ins[a], slot(outs[a], *me), local_sem.at[a]) for a in range(na)]
        first = []
        for a in range(na):
            first.append(copy(a, 0, me, sibling, src=ins[a]))
            first += [copy(a, 1 + j, me, (*chip, c), src=ins[a]) for j, chip in enumerate(chips)]
        return copy, mine, first, me, sibling, chips, c

    def start(ins, outs, sems):
        _, mine, first, *_ = copies(ins, outs, sems)
        for cp in mine + first:
            cp.start()

    def finish(ins, outs, sems):
        copy, mine, first, me, sibling, chips, c = copies(ins, outs, sems)
        passed = []
        for j, chip in enumerate(chips):
            for a in range(na):
                copy(a, 1 + j, (*chip, c), me).wait_recv()
                cp = copy(a, 4 + j, (*chip, c), sibling)
                cp.start()
                passed.append(cp)
        for a in range(na):
            copy(a, 0, sibling, me).wait_recv()
            for j, chip in enumerate(chips):
                copy(a, 4 + j, (*chip, 1 - c), me).wait_recv()
        for cp in first + passed:
            cp.wait_send()
        for cp in mine:
            cp.wait()

    return Hosted(list(blocks), [jax.ShapeDtypeStruct((8,) + b.shape, b.dtype) for b in blocks],
                  [pltpu.SemaphoreType.DMA((na, 7)), pltpu.SemaphoreType.DMA((na, 7)), pltpu.SemaphoreType.DMA((na,))],
                  start, finish)


def all_gather8(blocks, name):
    return run_hosted(gather8(blocks), name)


def chips3(arrays):
    na = len(arrays)

    def copies(ins, outs, sems):
        send_sems, recv_sems = sems
        x, y, c = _place()
        return [pltpu.make_async_remote_copy(
            src_ref=ins[a].at[2 * px + py], dst_ref=outs[a].at[k], send_sem=send_sems.at[a, k],
            recv_sem=recv_sems.at[a, k], device_id=(px, py, c), device_id_type=MESH)
            for a in range(na) for k, (px, py) in enumerate([(1 - x, y), (x, 1 - y), (1 - x, 1 - y)])]

    def start(ins, outs, sems):
        for cp in copies(ins, outs, sems):
            cp.start()

    def finish(ins, outs, sems):
        for cp in copies(ins, outs, sems):
            cp.wait()

    return Hosted(list(arrays), [jax.ShapeDtypeStruct((3,) + a.shape[1:], a.dtype) for a in arrays],
                  [pltpu.SemaphoreType.DMA((na, 3)), pltpu.SemaphoreType.DMA((na, 3))], start, finish)


def sibling_exchange(arrays, name):
    na = len(arrays)

    def body(*refs):
        ins, outs = refs[:na], refs[na:2 * na]
        send_sems, recv_sems = refs[2 * na:]
        x, y, c = _place()
        cps = [pltpu.make_async_remote_copy(
            src_ref=ins[a], dst_ref=outs[a], send_sem=send_sems.at[a], recv_sem=recv_sems.at[a],
            device_id=(x, y, 1 - c), device_id_type=MESH) for a in range(na)]
        for cp in cps:
            cp.start()
        for cp in cps:
            cp.wait()

    hbm = pl.BlockSpec(memory_space=pl.ANY)
    return pl.pallas_call(
        body, name=name, in_specs=[hbm] * na, out_specs=[hbm] * na,
        out_shape=[jax.ShapeDtypeStruct(a.shape, a.dtype) for a in arrays],
        scratch_shapes=[pltpu.SemaphoreType.DMA((na,)), pltpu.SemaphoreType.DMA((na,))],
    )(*arrays)


def sibling_blocks(arrays, name):
    na = len(arrays)

    def body(*refs):
        ins, outs = refs[:na], refs[na:2 * na]
        send_sems, recv_sems = refs[2 * na:]
        x, y, c = _place()
        cps = [pltpu.make_async_remote_copy(
            src_ref=ins[a].at[2 * j + 1 - c], dst_ref=outs[a].at[j],
            send_sem=send_sems.at[a, j], recv_sem=recv_sems.at[a, j],
            device_id=(x, y, 1 - c), device_id_type=MESH) for a in range(na) for j in range(4)]
        for cp in cps:
            cp.start()
        for cp in cps:
            cp.wait()

    hbm = pl.BlockSpec(memory_space=pl.ANY)
    return pl.pallas_call(
        body, name=name, in_specs=[hbm] * na, out_specs=[hbm] * na,
        out_shape=[jax.ShapeDtypeStruct((4,) + a.shape[1:], a.dtype) for a in arrays],
        scratch_shapes=[pltpu.SemaphoreType.DMA((na, 4)), pltpu.SemaphoreType.DMA((na, 4))],
    )(*arrays)


def chip_exchange(arrays, name):
    return run_hosted(chips3(arrays), name)


def _row_tile(r):
    for cand in (512, 256, 128, 64, 32, 16, 8):
        if r % cand == 0:
            return cand
    return r


def chip_partial(place, g8, landed4, name):
    _, r, ccols = g8.shape
    tr = _row_tile(r)

    def body(place_ref, g_ref, l_ref, o_ref):
        del place_ref
        o_ref[...] = (g_ref[...].astype(F32) + l_ref[...].astype(F32)).astype(BF16)

    spec = pl.BlockSpec((None, tr, ccols), lambda j, i, s: (j, i, 0))
    return pl.pallas_call(
        body, name=name,
        grid_spec=pltpu.PrefetchScalarGridSpec(
            num_scalar_prefetch=1, grid=(4, r // tr),
            in_specs=[pl.BlockSpec((None, tr, ccols), lambda j, i, s: (2 * j + s[0], i, 0)), spec], out_specs=spec),
        out_shape=jax.ShapeDtypeStruct((4, r, ccols), BF16),
    )(place, g8, landed4)


def shard_sum(place, partial4, landed3, name):
    _, r, ccols = partial4.shape
    tr = _row_tile(r)

    def body(place_ref, p_ref, l_ref, o_ref):
        del place_ref
        acc = p_ref[...].astype(F32)
        for k in range(3):
            acc = acc + l_ref[k].astype(F32)
        o_ref[...] = acc

    return pl.pallas_call(
        body, name=name,
        grid_spec=pltpu.PrefetchScalarGridSpec(
            num_scalar_prefetch=1, grid=(r // tr,),
            in_specs=[pl.BlockSpec((None, tr, ccols), lambda i, s: (s[1], i, 0)),
                      pl.BlockSpec((3, tr, ccols), lambda i, s: (0, i, 0))],
            out_specs=pl.BlockSpec((tr, ccols), lambda i, s: (i, 0))),
        out_shape=jax.ShapeDtypeStruct((r, ccols), F32),
    )(place, partial4, landed3)


def _adamw_math(w, g, m, v):
    m2 = B1 * m + (1.0 - B1) * g
    v2 = B2 * v + (1.0 - B2) * (g * g)
    m_hat = m2 / (1.0 - B1 ** STEP)
    v_hat = v2 / (1.0 - B2 ** STEP)
    return -LR * (m_hat / (jnp.sqrt(v_hat) + AEPS) + WD * w), m2, v2


def adamw_halves(place, w, mine, theirs, m, v, name):
    r, ccols = w.shape
    hr = r // 2
    tr = _row_tile(hr)
    nt = hr // tr

    def body(place_ref, w_ref, a_ref, b_ref, m_ref, v_ref, g_out, d_out, m_out, v_out):
        g = jnp.where(pl.program_id(0) == place_ref[0], a_ref[...], b_ref[...])
        d, m2, v2 = _adamw_math(w_ref[...], g, m_ref[...], v_ref[...])
        g_out[...] = g
        d_out[...] = d
        m_out[...] = m2
        v_out[...] = v2

    full = pl.BlockSpec((tr, ccols), lambda h, i, s: (h * nt + i, 0))
    part = pl.BlockSpec((tr, ccols), lambda h, i, s: (i, 0))
    return pl.pallas_call(
        body, name=name,
        grid_spec=pltpu.PrefetchScalarGridSpec(
            num_scalar_prefetch=1, grid=(2, nt), in_specs=[full, part, part, full, full], out_specs=[full] * 4),
        out_shape=[jax.ShapeDtypeStruct((r, ccols), F32)] * 4,
    )(place, w, mine, theirs, m, v)


def adamw_plain(w, g, m, v, name):
    r, ccols = w.shape
    tr = _row_tile(r)

    def body(w_ref, g_ref, m_ref, v_ref, d_out, m_out, v_out):
        d, m2, v2 = _adamw_math(w_ref[...], g_ref[...], m_ref[...], v_ref[...])
        d_out[...] = d
        m_out[...] = m2
        v_out[...] = v2

    spec = pl.BlockSpec((tr, ccols), lambda i: (i, 0))
    return pl.pallas_call(
        body, name=name, grid=(r // tr,), in_specs=[spec] * 4, out_specs=[spec] * 3,
        out_shape=[jax.ShapeDtypeStruct((r, ccols), F32)] * 3,
    )(w, g, m, v)


def _silu(x):
    return x * jax.nn.sigmoid(x)


def mod_shard(cin, w_ada, b_shard):
    def body(c_ref, w_ref, b_ref, o_ref):
        o_ref[...] = _nn(_silu(c_ref[...]), w_ref[...]) + b_ref[...]

    return pl.pallas_call(
        body, name="mod_shard", grid=(3,),
        in_specs=[pl.BlockSpec((32, D), lambda j: (0, 0)), pl.BlockSpec((D, 512), lambda j: (0, j)),
                  pl.BlockSpec((1, 512), lambda j: (0, j))],
        out_specs=pl.BlockSpec((32, 512), lambda j: (0, j)),
        out_shape=jax.ShapeDtypeStruct((32, 1536), F32),
    )(cin, w_ada, b_shard)


def ada_grads(cin, gb, gc, w_ada):
    def body(c_ref, gb_ref, gc_ref, w_ref, gw_ref, pc_ref):
        ctx_tot = jnp.sum(gc_ref[...], axis=0, keepdims=True)
        rows = lax.broadcasted_iota(jnp.int32, (16, 512), 0)
        dm = jnp.concatenate([gb_ref[...], jnp.where(rows == 0, ctx_tot, 0.0)], axis=0)
        gw_ref[...] = _tn(_silu(c_ref[...]), dm)
        rows8 = lax.broadcasted_iota(jnp.int32, (8, 512), 0)
        part = _nt(jnp.where(rows8 == 0, ctx_tot, 0.0), w_ref[...])

        @pl.when(pl.program_id(0) == 0)
        def _():
            pc_ref[...] = jnp.zeros_like(pc_ref)

        pc_ref[...] += part

    return pl.pallas_call(
        body, name="ada_grads", grid=(3,),
        in_specs=[pl.BlockSpec((32, D), lambda j: (0, 0)), pl.BlockSpec((16, 512), lambda j: (0, j)),
                  pl.BlockSpec((8, 512), lambda j: (0, j)), pl.BlockSpec((D, 512), lambda j: (0, j))],
        out_specs=[pl.BlockSpec((D, 512), lambda j: (0, j)), pl.BlockSpec((8, D), lambda j: (0, 0))],
        out_shape=[jax.ShapeDtypeStruct((D, 1536), F32), jax.ShapeDtypeStruct((8, D), F32)],
    )(cin, gb, gc, w_ada)


SMALL_SUM_ROWS = 15


def small_update(gsm, gbf, gcf, pcg, params):
    n = len(params)

    def body(*refs):
        gsm_ref, gbf_ref, gcf_ref, pcg_ref = refs[:4]
        wmv, outs, loss_out = refs[4:4 + 3 * n], refs[4 + 3 * n:4 + 7 * n], refs[-1]
        acc = gsm_ref[0]
        for dev in range(1, 8):
            acc = acc + gsm_ref[dev]
        c_ctx = wmv[0][...]
        sg = jax.nn.sigmoid(c_ctx)
        dsilu = pcg_ref[0:1, :] + pcg_ref[2:3, :] + pcg_ref[4:5, :] + pcg_ref[6:7, :]
        lane = lax.broadcasted_iota(jnp.int32, (1, D), 1)
        last = acc[14:15, :]
        grads = [
            dsilu * (sg * (1.0 + c_ctx * (1.0 - sg))),
            jnp.sum(gbf_ref[...], axis=0, keepdims=True) + jnp.sum(gcf_ref[...], axis=0, keepdims=True),
            acc[0:1, :] + acc[1:2, :], acc[2:3, :], acc[3:4, :], acc[4:5, :],
            acc[5:6, 0:512], acc[6:14, :], jnp.where(lane < 8, last, 0.0),
        ]
        loss_out[...] = jnp.broadcast_to(jnp.sum(jnp.where(lane == 8, last, 0.0), axis=1, keepdims=True), (8, 128))
        for i, g in enumerate(grads):
            d, m2, v2 = _adamw_math(wmv[3 * i][...], g, wmv[3 * i + 1][...], wmv[3 * i + 2][...])
            outs[4 * i][...] = g
            outs[4 * i + 1][...] = d
            outs[4 * i + 2][...] = m2
            outs[4 * i + 3][...] = v2

    flat = [a for wmv in params for a in wmv]
    out_shape = [jax.ShapeDtypeStruct(w.shape, F32) for w, _, _ in params for _ in range(4)]
    return pl.pallas_call(
        body, name="small_update", out_shape=out_shape + [jax.ShapeDtypeStruct((8, 128), F32)],
    )(gsm, gbf, gcf, pcg, *flat)


def _pad_row(v, rows):
    flat = v.reshape(-1)
    return jnp.pad(flat, (0, rows * D - flat.shape[0])).reshape(rows, D)


def local_step(x, ctx, tgt, mod3, g_pre_mix, g_post_mix, g_pre_mlp, g_post_mlp, ret_decay, ret_gn, na_rpb,
               wperm, late_weights, early_grads):
    nb = x.shape[0]
    tokens = nb * SEQ
    cos, sin = _rope_tables()
    rd = ret_decay.T.reshape(RH, 2, 1)
    gn = ret_gn.reshape(RH, 1, RD)
    bias = na_bias_table(_rpb_flat(na_rpb))
    h, pret, pna = premix_proj(x, mod3, g_pre_mix, wperm, False, "premix_proj")
    hc, pretc, pnac = premix_proj(ctx, mod3, g_pre_mix, wperm, True, "premix_proj_ctx")
    o_all, mixin, gw_out = retention_fwd(pret, pretc, rd, gn, cos, sin, late_weights(0))
    mixin, gw1, gw2 = na_fwd(pna, pnac, bias, mixin, late_weights(1))
    w1_b = unpack_cols(gw1.reshape(4, D, D), "unpack_w_mlp1")
    dx_tail, dmix, h2, du, act, dm, dmixin, dmod_t, dg_t, loss_t = tail_fwd_bwd(
        x, mixin, tgt, mod3, g_post_mix, g_pre_mlp, g_post_mlp, gw_out.reshape(D, D), w1_b, gw2.reshape(DFF, D))
    dw_out = weight_grad([(mixin.reshape(tokens, D), dmix.reshape(tokens, D))], "grad_w_out")
    dw1 = weight_grad([(h2.reshape(tokens, D), du.reshape(tokens, DFF))], "grad_w_mlp1")
    dw2 = weight_grad([(act.reshape(tokens, DFF), dm.reshape(tokens, D))], "grad_w_mlp2")
    dproj, dprojc, drd, dgn = retention_bwd(pret, pretc, o_all, dmixin, rd, gn, cos, sin)
    dproj, dprojc, dpat, *early = na_bwd(pna, pnac, bias, dmixin, dproj, dprojc, early_grads(dw_out, dw1, dw2))
    grad_x, dmod_a, dg_a = premix_bwd(x, mod3, g_pre_mix, wperm, dproj, dx_tail, "premix_bwd")
    dmod_c, dg_c = premix_bwd(ctx, mod3, g_pre_mix, wperm, dprojc, None, "premix_bwd_ctx")
    dw_in = weight_grad([(h.reshape(tokens, D), dproj.reshape(tokens, IN_W)),
                         (hc.reshape(nb * LC, D), dprojc.reshape(nb * LC, IN_W))], "grad_w_in", tn=512)
    dmod = jnp.concatenate([jnp.concatenate([dmod_a[:, 0:2], dmod_t[:, 2:6]], axis=1), dmod_c], axis=0)
    last = jnp.pad(jnp.concatenate([drd[:, :, 0].T.reshape(8), loss_t[0, 0:1]]), (0, D - 9)).reshape(1, D)
    small = jnp.concatenate([dg_a[0:1], dg_c[0:1], dg_t[0:3], _pad_row(dgn, 1), dpat.reshape(8, D), last], axis=0)
    return grad_x, dw_in, early, dmod, small


def kernel(x, c, ctx, c_ctx, w_ada, b_ada, g_pre_mix, g_post_mix, g_pre_mlp, g_post_mlp, w_in, ret_decay, ret_gn, na_rpb, w_out, w_mlp1, w_mlp2, loss_target, m_c_ctx, m_w_ada, m_b_ada, m_g_pre_mix, m_g_post_mix, m_g_pre_mlp, m_g_post_mlp, m_w_in, m_ret_decay, m_ret_gn, m_na_rpb, m_w_out, m_w_mlp1, m_w_mlp2, v_c_ctx, v_w_ada, v_b_ada, v_g_pre_mix, v_g_post_mix, v_g_pre_mlp, v_g_post_mlp, v_w_in, v_ret_decay, v_ret_gn, v_na_rpb, v_w_out, v_w_mlp1, v_w_mlp2):
    px, py, pc = _place()
    dev = 4 * px + 2 * py + pc
    chip = 2 * px + py

    def my_half(w2d):
        rows = w2d.shape[0] // 2
        return lax.dynamic_slice_in_dim(w2d, pc * rows, rows, 0)

    halves = [my_half(w[0]).astype(BF16) for w in (w_in, w_out, w_mlp1, w_mlp2)]
    gw_in, cg = all_gather8([halves[0], jnp.pad(c, ((0, 6), (0, 0)))], "gather_w_in")
    wperm = unpack_w_in(gw_in.reshape(4, D, 896))

    cin = jnp.pad(cg[:, 0:2].reshape(16, D), ((0, 16), (0, 0))) + jnp.pad(c_ctx[None], ((16, 15), (0, 0)))
    mod_mine = mod_shard(cin, w_ada[0], lax.dynamic_slice_in_dim(b_ada, chip * 1536, 1536, 1))
    (mg,) = all_gather8([mod_mine], "gather_mod")
    mod_all = jnp.concatenate([mg[0], mg[2], mg[4], mg[6]], axis=1)
    mod3 = (jnp.pad(lax.dynamic_slice_in_dim(mod_all, 2 * dev, 2, 0), ((0, 1), (0, 0)))
            + jnp.pad(mod_all[16:17], ((2, 0), (0, 0)))).reshape(3, 6, D)

    place = jnp.stack([pc, chip]).astype(jnp.int32)

    def chip_partials(g8, names, tag):
        landed = sibling_blocks(g8, "rs_sibling_" + tag)
        return [chip_partial(place, g, l, "rs_chip_sum_" + n) for g, l, n in zip(g8, landed, names)]

    partial = [None] * 4

    def early_grads(dw_out, dw1, dw2):
        g8 = [cast_rows(dw_out, "cast_g_w_out").reshape(8, 128, D), pack_cols(dw1, "pack_g_w_mlp1").reshape(8, 512, D),
              cast_rows(dw2, "cast_g_w_mlp2").reshape(8, 512, D)]
        partial[1:] = chip_partials(g8, ["w_out", "w_mlp1", "w_mlp2"], "early")
        return chips3(partial[1:])

    grad_x, dw_in, landed3, dmod, small = local_step(
        x, ctx, loss_target, mod3, g_pre_mix, g_post_mix, g_pre_mlp, g_post_mlp, ret_decay[0], ret_gn, na_rpb[0],
        wperm, lambda k: gather8(halves[1:2] if k == 0 else halves[2:4]), early_grads)

    names = ["w_in", "w_out", "w_mlp1", "w_mlp2"]
    partial[0:1] = chip_partials([pack_w_in(dw_in).reshape(8, 512, 896)], ["w_in"], "w_in")
    landed3 = list(chip_exchange(partial[0:1], "rs_chips_w_in")) + list(landed3)
    mine = [shard_sum(place, p, l, "rs_shard_sum_" + n) for p, l, n in zip(partial, landed3, names)]
    theirs = sibling_exchange(mine, "rs_halves")
    big = []
    for a, (w, m, v) in enumerate(((w_in, m_w_in, v_w_in), (w_out, m_w_out, v_w_out),
                                   (w_mlp1, m_w_mlp1, v_w_mlp1), (w_mlp2, m_w_mlp2, v_w_mlp2))):
        big.append([r[None] for r in adamw_halves(place, w[0], mine[a], theirs[a], m[0], v[0], "adamw_" + names[a])])

    pay = jnp.concatenate([dmod.reshape(18, D), small, jnp.zeros((40 - 18 - SMALL_SUM_ROWS, D), F32)], axis=0)
    (gs,) = all_gather8([pay], "gather_small")
    gbf = gs[:, 0:12].reshape(16, 6 * D)
    gcf = gs[:, 12:18].reshape(8, 6 * D)
    gw_ada, pc_part = ada_grads(cin, lax.dynamic_slice_in_dim(gbf, chip * 1536, 1536, 1),
                                lax.dynamic_slice_in_dim(gcf, chip * 1536, 1536, 1), w_ada[0])
    (pcg,) = all_gather8([pc_part], "gather_c_ctx")
    d_ada, m_ada, v_ada = adamw_plain(w_ada[0], gw_ada, m_w_ada[0], v_w_ada[0], "adamw_w_ada")

    def rpb_rows(t):
        return _rpb_flat(t[0]).reshape(8, D)

    def decay_row(t):
        return jnp.pad(t.reshape(1, 8), ((0, 0), (0, D - 8)))

    views = [lambda t: t.reshape(1, D), lambda t: t, lambda t: t, lambda t: t, lambda t: t, lambda t: t, lambda t: t,
             rpb_rows, decay_row]
    back = [lambda t: t.reshape(D), lambda t: t, lambda t: t, lambda t: t, lambda t: t, lambda t: t, lambda t: t,
            lambda t: _rpb_flat_t(t)[None], lambda t: t[:, 0:8].reshape(1, 2, 4)]
    small_w = (c_ctx, b_ada, g_pre_mix, g_post_mix, g_pre_mlp, g_post_mlp, ret_gn, na_rpb, ret_decay)
    small_m = (m_c_ctx, m_b_ada, m_g_pre_mix, m_g_post_mix, m_g_pre_mlp, m_g_post_mlp, m_ret_gn, m_na_rpb, m_ret_decay)
    small_v = (v_c_ctx, v_b_ada, v_g_pre_mix, v_g_post_mix, v_g_pre_mlp, v_g_post_mlp, v_ret_gn, v_na_rpb, v_ret_decay)
    *res, loss8 = small_update(gs[:, 18:18 + SMALL_SUM_ROWS], gbf, gcf, pcg[:, 0],
                               [(f(w), f(m), f(v)) for f, w, m, v in zip(views, small_w, small_m, small_v)])

    def leaves(ada, idx):
        s_c, s_b, s_g1, s_g2, s_g3, s_g4, s_gn, s_rpb, s_rd = [back[i](res[4 * i + idx]) for i in range(9)]
        return [s_c, ada[None], s_b, s_g1, s_g2, s_g3, s_g4, big[0][idx], s_rd, s_gn, s_rpb,
                big[1][idx], big[2][idx], big[3][idx]]

    return (loss8[0, 0], grad_x, *leaves(gw_ada, 0), *leaves(d_ada, 1), *leaves(m_ada, 2), *leaves(v_ada, 3))
```

```python
import functools

import jax
import jax.numpy as jnp
from jax import lax
from jax.experimental import pallas as pl
from jax.experimental.pallas import tpu as pltpu

F32, BF16 = jnp.float32, jnp.bfloat16
D = 1024
SEQ = 2048
LC = 256
GW = 64
RH, RD, CH = 4, 128, 128
NPAIR = 4
IN_W = 3584
RET_W = 2048
DFF = 4096
EPS = 1e-6
NEG = -1e30
TN = 256
NCH = SEQ // CH
LR, B1, B2, AEPS, WD, STEP = 0.001, 0.9, 0.999, 1e-08, 0.01, 10
MESH = pl.DeviceIdType.MESH
VMEM_LIMIT = 56 * 1024 * 1024


def _cp(sem=None):
    return pltpu.CompilerParams(dimension_semantics=sem, vmem_limit_bytes=VMEM_LIMIT)


def _nn(a, b):
    return jnp.dot(a.astype(BF16), b.astype(BF16), preferred_element_type=F32)


def _nt(a, b):
    return lax.dot_general(a.astype(BF16), b.astype(BF16), (((1,), (1,)), ((), ())), preferred_element_type=F32)


def _tn(a, b):
    return lax.dot_general(a.astype(BF16), b.astype(BF16), (((0,), (0,)), ((), ())), preferred_element_type=F32)


@jax.custom_vjp
def mm_nn(a, b):
    return _nn(a, b)


@jax.custom_vjp
def mm_nt(a, b):
    return _nt(a, b)


@jax.custom_vjp
def mm_tn(a, b):
    return _tn(a, b)


mm_nn.defvjp(lambda a, b: (_nn(a, b), (a, b)), lambda r, g: (_nt(g, r[1]), _tn(r[0], g)))
mm_nt.defvjp(lambda a, b: (_nt(a, b), (a, b)), lambda r, g: (_nn(g, r[1]), _tn(g, r[0])))
mm_tn.defvjp(lambda a, b: (_tn(a, b), (a, b)), lambda r, g: (_nt(r[1], g), _nn(r[0], g)))


def _rms(x):
    return x * lax.rsqrt(jnp.mean(x * x, axis=-1, keepdims=True) + EPS)


def _rms_mod(x, g, sc, sh):
    return (_rms(x) * g) * (1.0 + sc) + sh


def _post_mix(x, mix, gt1, sc2, sh2, g_post_mix, g_pre_mlp):
    x1 = x + gt1 * (_rms(mix) * g_post_mix)
    return x1, _rms_mod(x1, g_pre_mlp, sc2, sh2)


def _head_loss(x1, m, gt2, g_post_mlp, tgt):
    err = x1 + gt2 * (_rms(m) * g_post_mlp) - tgt
    return 0.5 * jnp.sum(jnp.mean(err * err, axis=-1, keepdims=True), axis=0, keepdims=True)


def _ln_gate(o, g, w):
    mu = jnp.mean(o, axis=-1, keepdims=True)
    var = jnp.mean(jnp.square(o - mu), axis=-1, keepdims=True)
    y = (o - mu) * lax.rsqrt(var + EPS)
    return (y * w) * (g * jax.nn.sigmoid(g))


def _swap32(x):
    lane = lax.broadcasted_iota(jnp.int32, x.shape, 1)
    return jnp.where((lane & 32) == 0, pltpu.roll(x, 96, 1), pltpu.roll(x, 32, 1))


def _rope(x, cos, sin):
    return x * cos + _swap32(x) * sin


def _rope_t(g, cos, sin):
    return g * cos + _swap32(g * sin)


def _rope_tables():
    tok = jnp.arange(SEQ)
    pos_r = (tok // GW).astype(F32)
    pos_c = (tok % GW).astype(F32)
    inv = 10000.0 ** (-jnp.arange(32, dtype=F32) / 32)
    ar = pos_r[:, None] * inv[None, :]
    ac = pos_c[:, None] * inv[None, :]
    cos = jnp.concatenate([jnp.cos(ar), jnp.cos(ar), jnp.cos(ac), jnp.cos(ac)], axis=-1)
    sin = jnp.concatenate([-jnp.sin(ar), jnp.sin(ar), -jnp.sin(ac), jnp.sin(ac)], axis=-1)
    return cos, sin


def _fiota(shape, dim):
    return lax.broadcasted_iota(jnp.int32, shape, dim).astype(F32)


def _ret_state(k, v, s, lg, reverse):
    pos = _fiota((CH, 1), 0)
    b_exp = pos if reverse else (CH - 1.0 - pos)
    return jnp.exp(lg * CH) * s + mm_tn(k * jnp.exp(lg * b_exp), v)


def _ret_chunk(q, k, v, s, lg, reverse):
    i = _fiota((CH, CH), 0)
    j = _fiota((CH, CH), 1)
    diff = (j - i) if reverse else (i - j)
    mask = (diff > 0) if reverse else (diff >= 0)
    decay = jnp.where(mask, jnp.exp(lg * jnp.where(mask, diff, 0.0)), 0.0)
    pos = _fiota((CH, 1), 0)
    a_exp = (CH - pos) if reverse else (pos + 1.0)
    o = mm_nn(mm_nt(q, k) * decay, v) + mm_nn(q * jnp.exp(lg * a_exp), s)
    return o, _ret_state(k, v, s, lg, reverse)


def premix_proj(xin, mod3, g_pre, wperm, is_ctx, name):
    nb, length, _ = xin.shape
    tn = min(TN, length)

    def body(x_ref, mod_ref, g_ref, w_ref, h_ref, pret_ref, pna_ref):
        h = _rms_mod(x_ref[...], g_ref[...], mod_ref[1:2, :], mod_ref[0:1, :])
        hb = h.astype(BF16)
        h_ref[...] = hb
        pret_ref[...] = jnp.dot(hb, w_ref[:, :RET_W], preferred_element_type=F32)
        pna_ref[...] = jnp.dot(hb, w_ref[:, RET_W:], preferred_element_type=F32).astype(BF16)

    return pl.pallas_call(
        body, name=name, grid=(nb, length // tn),
        in_specs=[
            pl.BlockSpec((None, tn, D), lambda b, t: (b, t, 0)),
            pl.BlockSpec((None, 6, D), (lambda b, t: (2, 0, 0)) if is_ctx else (lambda b, t: (b, 0, 0))),
            pl.BlockSpec((1, D), lambda b, t: (0, 0)),
            pl.BlockSpec((D, IN_W), lambda b, t: (0, 0), pipeline_mode=pl.Buffered(1)),
        ],
        out_specs=[
            pl.BlockSpec((None, tn, D), lambda b, t: (b, t, 0)),
            pl.BlockSpec((None, tn, RET_W), lambda b, t: (b, t, 0)),
            pl.BlockSpec((None, tn, IN_W - RET_W), lambda b, t: (b, t, 0)),
        ],
        out_shape=[
            jax.ShapeDtypeStruct((nb, length, D), BF16),
            jax.ShapeDtypeStruct((nb, length, RET_W), F32),
            jax.ShapeDtypeStruct((nb, length, IN_W - RET_W), BF16),
        ],
        compiler_params=_cp(("arbitrary", "arbitrary")),
    )(xin, mod3, g_pre, wperm)


def premix_bwd(xin, mod3, g_pre, wperm, dproj, dx_tail, name):
    nb, length, _ = xin.shape
    tn = min(TN, length)
    is_ctx = dx_tail is None

    def body(*refs):
        if is_ctx:
            x_ref, mod_ref, g_ref, w_ref, dp_ref, dmod_ref, dg_ref = refs
        else:
            x_ref, mod_ref, g_ref, w_ref, dp_ref, dxt_ref, dx_ref, dmod_ref, dg_ref = refs
        b, t = pl.program_id(0), pl.program_id(1)
        dh = lax.dot_general(dp_ref[...], w_ref[...], (((1,), (1,)), ((), ())), preferred_element_type=F32)
        _, vjp = jax.vjp(_rms_mod, x_ref[...], g_ref[...], mod_ref[1:2, :], mod_ref[0:1, :])
        dx, dg, dsc, dsh = vjp(dh)
        if not is_ctx:
            dx_ref[...] = dx + dxt_ref[...]

        @pl.when((t == 0) & ((b == 0) if is_ctx else True))
        def _():
            dmod_ref[...] = jnp.zeros_like(dmod_ref)

        @pl.when((t == 0) & (b == 0))
        def _():
            dg_ref[...] = jnp.zeros_like(dg_ref)

        dmod_ref[0:1, :] += dsh
        dmod_ref[1:2, :] += dsc
        dg_ref[0:1, :] += dg

    tok = lambda b, t: (b, t, 0)
    in_specs = [
        pl.BlockSpec((None, tn, D), tok),
        pl.BlockSpec((None, 6, D), (lambda b, t: (2, 0, 0)) if is_ctx else (lambda b, t: (b, 0, 0))),
        pl.BlockSpec((1, D), lambda b, t: (0, 0)),
        pl.BlockSpec((D, IN_W), lambda b, t: (0, 0), pipeline_mode=pl.Buffered(1)),
        pl.BlockSpec((None, tn, IN_W), tok),
    ]
    args = [xin, mod3, g_pre, wperm, dproj]
    out_specs = [
        pl.BlockSpec((None, 6, D), (lambda b, t: (0, 0, 0)) if is_ctx else (lambda b, t: (b, 0, 0))),
        pl.BlockSpec((8, D), lambda b, t: (0, 0)),
    ]
    out_shape = [jax.ShapeDtypeStruct((1 if is_ctx else nb, 6, D), F32), jax.ShapeDtypeStruct((8, D), F32)]
    if not is_ctx:
        in_specs.append(pl.BlockSpec((None, tn, D), tok))
        args.append(dx_tail)
        out_specs.insert(0, pl.BlockSpec((None, tn, D), tok))
        out_shape.insert(0, jax.ShapeDtypeStruct((nb, length, D), F32))
    return pl.pallas_call(
        body, name=name, grid=(nb, length // tn), in_specs=in_specs, out_specs=out_specs, out_shape=out_shape,
        compiler_params=_cp(("arbitrary", "arbitrary")),
    )(*args)


def _ret_specs(order):
    def im(f):
        return lambda *g: f(*order(*g))
    return dict(
        pret=pl.BlockSpec((None, SEQ, 512), im(lambda b, h: (b, 0, h))),
        pretc=pl.BlockSpec((None, LC, 512), im(lambda b, h: (b, 0, h))),
        rd=pl.BlockSpec((None, 2, 1), im(lambda b, h: (h, 0, 0))),
        gn=pl.BlockSpec((None, 1, RD), im(lambda b, h: (h, 0, 0))),
        tab=pl.BlockSpec((SEQ, RD), im(lambda b, h: (0, 0))),
        head=pl.BlockSpec((None, SEQ, RD), im(lambda b, h: (b, 0, h))),
    )


def retention_fwd(pret, pretc, rd, gn, cos, sin, hosted):
    nb = pret.shape[0]
    sp = _ret_specs(lambda b, h: (b, h))

    def body(*refs):
        own_in, h_in, own_out, h_out, own_scr, h_sems = hosted.split(refs, 6, 2)
        p_ref, pc_ref, rd_ref, gn_ref, cos_ref, sin_ref = own_in
        (o_ref, mix_ref), (q_s, k_s, of_s, ob_s) = own_out, own_scr
        grid_step = pl.program_id(0) * RH + pl.program_id(1)

        @pl.when(grid_step == 0)
        def _():
            hosted.start(h_in, h_out, h_sems)

        cos_v, sin_v = cos_ref[...], sin_ref[...]
        q_s[...] = _rope(p_ref[:, 0:128], cos_v, sin_v) * (RD ** -0.5)
        k_s[...] = _rope(p_ref[:, 128:256], cos_v, sin_v)
        lgs, init = [], []
        for rev in (False, True):
            lg = jax.nn.log_sigmoid(rd_ref[int(rev):int(rev) + 1, :])
            s = jnp.zeros((RD, RD), F32)
            for n in ((1, 0) if rev else (0, 1)):
                s = _ret_state(pc_ref[n * CH:(n + 1) * CH, 128:256], pc_ref[n * CH:(n + 1) * CH, 256:384], s, lg, rev)
            lgs.append(lg)
            init.append(s)

        def step(t, carry):
            out = []
            for rev, o_s, s in ((False, of_s, carry[0]), (True, ob_s, carry[1])):
                n = (NCH - 1 - t) if rev else t
                sl = pl.ds(pl.multiple_of(n * CH, CH), CH)
                o, s2 = _ret_chunk(q_s[sl, :], k_s[sl, :], p_ref[sl, 256:384], s, lgs[int(rev)], rev)
                o_s[sl, :] = o
                out.append(s2)
            return tuple(out)

        lax.fori_loop(0, NCH, step, tuple(init))
        o = of_s[...] + ob_s[...]
        o_ref[...] = o
        mix_ref[...] = _ln_gate(o, p_ref[:, 384:512], gn_ref[...]).astype(BF16)

        @pl.when(grid_step == nb * RH - 1)
        def _():
            hosted.finish(h_in, h_out, h_sems)

    h_in_specs, h_out_specs = hosted.specs()
    return pl.pallas_call(
        body, name="retention_fwd", grid=(nb, RH),
        in_specs=[sp["pret"], sp["pretc"], sp["rd"], sp["gn"], sp["tab"], sp["tab"]] + h_in_specs,
        out_specs=[sp["head"], sp["head"]] + h_out_specs,
        out_shape=[jax.ShapeDtypeStruct((nb, SEQ, RH * RD), F32), jax.ShapeDtypeStruct((nb, SEQ, D), BF16)]
        + hosted.out_shape,
        scratch_shapes=[pltpu.VMEM((SEQ, RD), F32)] * 4 + hosted.scratch,
        compiler_params=_cp(("arbitrary", "arbitrary")),
    )(pret, pretc, rd, gn, cos, sin, *hosted.args)


def retention_bwd(pret, pretc, o_all, dmixin, rd, gn, cos, sin, hosted):
    nb = pret.shape[0]
    sp = _ret_specs(lambda h, b: (b, h))

    def body(*refs):
        own_in, h_in, own_out, h_out, own_scr, h_sems = hosted.split(refs, 8, 4)
        p_ref, pc_ref, o_ref, dmix_ref, rd_ref, gn_ref, cos_ref, sin_ref = own_in
        dp_ref, dpc_ref, drd_ref, dgn_ref = own_out
        q_s, k_s, do_s, dqf_s, dkf_s, dvf_s, dqb_s, dkb_s, dvb_s, stf_s, stb_s = own_scr
        b = pl.program_id(1)
        grid_step = pl.program_id(0) * nb + b

        @pl.when(grid_step == 0)
        def _():
            hosted.start(h_in, h_out, h_sems)

        cos_v, sin_v = cos_ref[...], sin_ref[...]
        q_s[...] = _rope(p_ref[:, 0:128], cos_v, sin_v) * (RD ** -0.5)
        k_s[...] = _rope(p_ref[:, 128:256], cos_v, sin_v)
        _, gate_vjp = jax.vjp(_ln_gate, o_ref[...], p_ref[:, 384:512], gn_ref[...])
        do, dg, dgn = gate_vjp(dmix_ref[...].astype(F32))
        do_s[...] = do
        dp_ref[:, 384:512] = dg.astype(BF16)

        @pl.when(b == 0)
        def _():
            drd_ref[...] = jnp.zeros_like(drd_ref)
            dgn_ref[...] = jnp.zeros_like(dgn_ref)

        dgn_ref[...] += dgn
        kcs = [pc_ref[n * CH:(n + 1) * CH, 128:256] for n in (0, 1)]
        vcs = [pc_ref[n * CH:(n + 1) * CH, 256:384] for n in (0, 1)]
        dirs = []
        init = []
        for rev in (False, True):
            rdv = rd_ref[int(rev):int(rev) + 1, :]
            lg = jax.nn.log_sigmoid(rdv)
            order_c = (1, 0) if rev else (0, 1)
            s = jnp.zeros((RD, RD), F32)
            ctx_states = []
            for n in order_c:
                ctx_states.append(s)
                s = _ret_state(kcs[n], vcs[n], s, lg, rev)
            dirs.append((rev, order_c, lg, rdv, ctx_states))
            init.append(s)
        acc = ((dqf_s, dkf_s, dvf_s, stf_s), (dqb_s, dkb_s, dvb_s, stb_s))

        def fstep(t, carry):
            out = []
            for (rev, _, lg, _, _), (_, _, _, st_s), s in zip(dirs, acc, carry):
                n = (NCH - 1 - t) if rev else t
                sl = pl.ds(pl.multiple_of(n * CH, CH), CH)
                st_s[n] = s
                out.append(_ret_state(k_s[sl, :], p_ref[sl, 256:384], s, lg, rev))
            return tuple(out)

        lax.fori_loop(0, NCH, fstep, tuple(init))

        def bstep(t, carry):
            out = []
            for (rev, _, lg, _, _), (dq_s, dk_s, dv_s, st_s), (ds, dlg) in zip(dirs, acc, carry):
                n = t if rev else (NCH - 1 - t)
                sl = pl.ds(pl.multiple_of(n * CH, CH), CH)
                _, vjp = jax.vjp(functools.partial(_ret_chunk, reverse=rev),
                                 q_s[sl, :], k_s[sl, :], p_ref[sl, 256:384], st_s[n], lg)
                dq, dk, dv, ds_prev, dl = vjp((do_s[sl, :], ds))
                dq_s[sl, :] = dq
                dk_s[sl, :] = dk
                dv_s[sl, :] = dv
                out.append((ds_prev, dlg + dl))
            return tuple(out)

        zero_c = (jnp.zeros((RD, RD), F32), jnp.zeros((1, 1), F32))
        res = lax.fori_loop(0, NCH, bstep, (zero_c, zero_c))
        dkc = [None, None]
        dvc = [None, None]
        for (rev, order_c, lg, rdv, ctx_states), (ds, dlg) in zip(dirs, res):
            for idx in (1, 0):
                n = order_c[idx]
                _, vjp = jax.vjp(functools.partial(_ret_state, reverse=rev), kcs[n], vcs[n], ctx_states[idx], lg)
                dk_c, dv_c, ds, dl = vjp(ds)
                dlg = dlg + dl
                dkc[n] = dk_c if dkc[n] is None else dkc[n] + dk_c
                dvc[n] = dv_c if dvc[n] is None else dvc[n] + dv_c
            drd_ref[int(rev):int(rev) + 1, :] += dlg * jax.nn.sigmoid(-rdv)
        dp_ref[:, 0:128] = _rope_t((dqf_s[...] + dqb_s[...]) * (RD ** -0.5), cos_v, sin_v).astype(BF16)
        dp_ref[:, 128:256] = _rope_t(dkf_s[...] + dkb_s[...], cos_v, sin_v).astype(BF16)
        dp_ref[:, 256:384] = (dvf_s[...] + dvb_s[...]).astype(BF16)
        zero = jnp.zeros((CH, RD), BF16)
        for n in (0, 1):
            rows = slice(n * CH, (n + 1) * CH)
            dpc_ref[rows, 0:128] = zero
            dpc_ref[rows, 128:256] = dkc[n].astype(BF16)
            dpc_ref[rows, 256:384] = dvc[n].astype(BF16)
            dpc_ref[rows, 384:512] = zero

        @pl.when(grid_step == RH * nb - 1)
        def _():
            hosted.finish(h_in, h_out, h_sems)

    h_in_specs, h_out_specs = hosted.specs()
    return pl.pallas_call(
        body, name="retention_bwd", grid=(RH, nb),
        in_specs=[sp["pret"], sp["pretc"], sp["head"], sp["head"], sp["rd"], sp["gn"], sp["tab"], sp["tab"]]
        + h_in_specs,
        out_specs=[
            pl.BlockSpec((None, SEQ, 512), lambda h, b: (b, 0, h)),
            pl.BlockSpec((None, LC, 512), lambda h, b: (b, 0, h)),
            pl.BlockSpec((None, 2, 1), lambda h, b: (h, 0, 0)),
            pl.BlockSpec((None, 1, RD), lambda h, b: (h, 0, 0)),
        ] + h_out_specs,
        out_shape=[
            jax.ShapeDtypeStruct((nb, SEQ, IN_W), BF16),
            jax.ShapeDtypeStruct((nb, LC, IN_W), BF16),
            jax.ShapeDtypeStruct((RH, 2, 1), F32),
            jax.ShapeDtypeStruct((RH, 1, RD), F32),
        ] + hosted.out_shape,
        scratch_shapes=[pltpu.VMEM((SEQ, RD), F32)] * 9 + [pltpu.VMEM((NCH, RD, RD), F32)] * 2 + hosted.scratch,
        compiler_params=_cp(("arbitrary", "arbitrary")),
    )(pret, pretc, o_all, dmixin, rd, gn, cos, sin, *hosted.args)


def _rpb_flat(rpb):
    return jnp.pad(rpb, ((0, 0), (0, 1), (0, 33))).reshape(NPAIR, 2, 1, 1024)


def _rpb_flat_t(dflat):
    return dflat.reshape(8, 16, 64)[:, :15, :31]


def _barrel(x, left):
    row = lax.broadcasted_iota(jnp.int32, x.shape, 0)
    n = x.shape[1]
    for bit in range(6):
        s = 1 << bit
        x = jnp.where(((row >> bit) & 1) == 1, pltpu.roll(x, (n - s) if left else s, 1), x)
    return x


def _na_row(r):
    rs = jnp.clip(r - 4, 0, 24)
    cls = jnp.where(r < 4, r, jnp.where(r > 28, r - 24, 4))
    return pl.ds(pl.multiple_of(r * GW, GW), GW), pl.ds(pl.multiple_of(rs * GW, GW), 8 * GW), cls


def _na_probs(qst, kb, kc, bias):
    s_loc = _nt(qst, kb) * 0.125 + bias
    s_ctx = _nt(qst, kc) * 0.125
    m = jnp.maximum(jnp.max(s_loc, axis=1, keepdims=True), jnp.max(s_ctx, axis=1, keepdims=True))
    e_loc, e_ctx = jnp.exp(s_loc - m), jnp.exp(s_ctx - m)
    den = jnp.sum(e_loc, axis=1, keepdims=True) + jnp.sum(e_ctx, axis=1, keepdims=True)
    return e_loc / den, e_ctx / den


def _stack_heads(t):
    lane = lax.broadcasted_iota(jnp.int32, t.shape, 1)
    zero = jnp.zeros_like(t)
    return jnp.concatenate([jnp.where(lane < 64, t, zero), jnp.where(lane >= 64, t, zero)], axis=0)


def _unstack_heads(t):
    lane = lax.broadcasted_iota(jnp.int32, (GW, 128), 1)
    return jnp.where(lane < 64, t[:GW], t[GW:])


def na_bias_table(flat):
    def body(flat_ref, out_ref):
        qc = lax.broadcasted_iota(jnp.int32, (GW, 512), 0)
        kc = lax.broadcasted_iota(jnp.int32, (GW, 512), 1) & 63
        start = jnp.clip(qc - 8, 0, GW - 16)
        window = (kc >= start) & (kc < start + 16)
        for hh in (0, 1):
            skew = _barrel(pltpu.roll(jnp.broadcast_to(flat_ref[hh], (GW, 1024)), 1024 - 15, 1), left=False)
            for cls in range(8):
                w = skew if cls == 7 else pltpu.roll(skew, (9 + cls) * 64, 1)
                out_ref[cls, hh * GW:(hh + 1) * GW, :] = jnp.where(window, w[:, 0:512], NEG)

    return pl.pallas_call(
        body, name="na_bias_table", grid=(NPAIR,),
        in_specs=[pl.BlockSpec((None, 2, 1, 1024), lambda p: (p, 0, 0, 0))],
        out_specs=pl.BlockSpec((None, 8, 2 * GW, 512), lambda p: (p, 0, 0, 0)),
        out_shape=jax.ShapeDtypeStruct((NPAIR, 8, 2 * GW, 512), F32),
    )(flat)


def na_fwd(pna, pnac, bias, mixin, hosted):
    nb = pna.shape[0]

    def body(*refs):
        (p_ref, pc_ref, bias_ref, _), h_in, (out_ref,), h_out, _, h_sems = hosted.split(refs, 4, 1)
        grid_step = pl.program_id(0) * nb + pl.program_id(1)

        @pl.when(grid_step == 0)
        def _():
            hosted.start(h_in, h_out, h_sems)

        kc, vc = pc_ref[:, 128:256], pc_ref[:, 256:384]

        def row(r, carry):
            qsl, bsl, cls = _na_row(r)
            kb, vb = p_ref[bsl, 128:256], p_ref[bsl, 256:384]
            p_loc, p_ctx = _na_probs(_stack_heads(p_ref[qsl, 0:128]), kb, kc, bias_ref[cls])
            out_ref[qsl, :] = _unstack_heads(_nn(p_loc, vb) + _nn(p_ctx, vc)).astype(BF16)
            return carry

        lax.fori_loop(0, SEQ // GW, row, 0, unroll=2)

        @pl.when(grid_step == NPAIR * nb - 1)
        def _():
            hosted.finish(h_in, h_out, h_sems)

    h_in_specs, h_out_specs = hosted.specs()
    return pl.pallas_call(
        body, name="na_fwd", grid=(NPAIR, nb),
        in_specs=[
            pl.BlockSpec((None, SEQ, 384), lambda p, b: (b, 0, p)),
            pl.BlockSpec((None, LC, 384), lambda p, b: (b, 0, p)),
            pl.BlockSpec((None, 8, 2 * GW, 512), lambda p, b: (p, 0, 0, 0)),
            pl.BlockSpec(memory_space=pl.ANY),
        ] + h_in_specs,
        out_specs=[pl.BlockSpec((None, SEQ, 128), lambda p, b: (b, 0, 4 + p))] + h_out_specs,
        out_shape=[jax.ShapeDtypeStruct((nb, SEQ, D), BF16)] + hosted.out_shape,
        input_output_aliases={3: 0},
        scratch_shapes=hosted.scratch,
        compiler_params=_cp(("arbitrary", "arbitrary")),
    )(pna, pnac, bias, mixin, *hosted.args)


def na_bwd(pna, pnac, bias, dmixin, dproj, dprojc, hosted):
    nb = pna.shape[0]

    def body(*refs):
        own_in, h_in, own_out, h_out, own_scr, h_sems = hosted.split(refs, 6, 3)
        p_ref, pc_ref, bias_ref, dmix_ref = own_in[:4]
        dp_ref, dpc_ref, dpat_ref = own_out
        dbias_s, dk_s, dv_s, dkc_s, dvc_s, res_s, resc_s = own_scr
        b, part = pl.program_id(1), pl.program_id(2)
        grid_step = (pl.program_id(0) * nb + b) * 3 + part

        @pl.when(grid_step == 0)
        def _():
            hosted.start(h_in, h_out, h_sems)

        @pl.when(grid_step == NPAIR * nb * 3 - 1)
        def _():
            hosted.finish(h_in, h_out, h_sems)

        @pl.when(part == 0)
        def _():
            @pl.when(b == 0)
            def _():
                dbias_s[...] = jnp.zeros_like(dbias_s)

            dk_s[...] = jnp.zeros_like(dk_s)
            dv_s[...] = jnp.zeros_like(dv_s)
            dkc_s[...] = jnp.zeros_like(dkc_s)
            dvc_s[...] = jnp.zeros_like(dvc_s)
            kc, vc = pc_ref[:, 128:256], pc_ref[:, 256:384]

            def row(r, carry):
                qsl, bsl, cls = _na_row(r)
                kb, vb = p_ref[bsl, 128:256], p_ref[bsl, 256:384]
                qst, dost = _stack_heads(p_ref[qsl, 0:128]), _stack_heads(dmix_ref[qsl, :])
                p_loc, p_ctx = _na_probs(qst, kb, kc, bias_ref[cls])
                dp_loc, dp_ctx = _nt(dost, vb), _nt(dost, vc)
                delta = (jnp.sum(p_loc * dp_loc, axis=1, keepdims=True)
                         + jnp.sum(p_ctx * dp_ctx, axis=1, keepdims=True))
                ds_loc, ds_ctx = p_loc * (dp_loc - delta), p_ctx * (dp_ctx - delta)
                dbias_s[cls] += ds_loc
                res_s[0, qsl, :] = _unstack_heads((_nn(ds_loc, kb) + _nn(ds_ctx, kc)) * 0.125).astype(BF16)
                dk_s[bsl, :] += _tn(ds_loc, qst) * 0.125
                dv_s[bsl, :] += _tn(p_loc, dost)
                dkc_s[...] += _tn(ds_ctx, qst) * 0.125
                dvc_s[...] += _tn(p_ctx, dost)
                return carry

            lax.fori_loop(0, SEQ // GW, row, 0, unroll=2)
            res_s[1] = dk_s[...].astype(BF16)
            res_s[2] = dv_s[...].astype(BF16)
            resc_s[0] = jnp.zeros((LC, 128), BF16)
            resc_s[1] = dkc_s[...].astype(BF16)
            resc_s[2] = dvc_s[...].astype(BF16)

            @pl.when(b == nb - 1)
            def _():
                for hh in (0, 1):
                    skew = jnp.zeros((GW, 1024), F32)
                    for cls in range(8):
                        w = jnp.concatenate([dbias_s[cls, hh * GW:(hh + 1) * GW, :], jnp.zeros((GW, 512), F32)], axis=1)
                        skew = skew + (w if cls == 7 else pltpu.roll(w, (7 - cls) * 64, 1))
                    dpat_ref[hh] = jnp.sum(pltpu.roll(_barrel(skew, left=True), 15, 1), axis=0, keepdims=True)

        dp_ref[...] = res_s[part]
        dpc_ref[...] = resc_s[part]

    h_in_specs, h_out_specs = hosted.specs()
    return pl.pallas_call(
        body, name="na_bwd", grid=(NPAIR, nb, 3),
        in_specs=[
            pl.BlockSpec((None, SEQ, 384), lambda p, b, s: (b, 0, p)),
            pl.BlockSpec((None, LC, 384), lambda p, b, s: (b, 0, p)),
            pl.BlockSpec((None, 8, 2 * GW, 512), lambda p, b, s: (p, 0, 0, 0)),
            pl.BlockSpec((None, SEQ, 128), lambda p, b, s: (b, 0, 4 + p)),
            pl.BlockSpec(memory_space=pl.ANY),
            pl.BlockSpec(memory_space=pl.ANY),
        ] + h_in_specs,
        out_specs=[
            pl.BlockSpec((None, SEQ, 128), lambda p, b, s: (b, 0, 16 + 3 * p + s)),
            pl.BlockSpec((None, LC, 128), lambda p, b, s: (b, 0, 16 + 3 * p + s)),
            pl.BlockSpec((None, 2, 1, 1024), lambda p, b, s: (p, 0, 0, 0)),
        ] + h_out_specs,
        out_shape=[
            jax.ShapeDtypeStruct((nb, SEQ, IN_W), BF16),
            jax.ShapeDtypeStruct((nb, LC, IN_W), BF16),
            jax.ShapeDtypeStruct((NPAIR, 2, 1, 1024), F32),
        ] + hosted.out_shape,
        input_output_aliases={4: 0, 5: 1},
        scratch_shapes=[
            pltpu.VMEM((8, 2 * GW, 512), F32),
            pltpu.VMEM((SEQ, 128), F32), pltpu.VMEM((SEQ, 128), F32),
            pltpu.VMEM((LC, 128), F32), pltpu.VMEM((LC, 128), F32),
            pltpu.VMEM((3, SEQ, 128), BF16), pltpu.VMEM((3, LC, 128), BF16),
        ] + hosted.scratch,
        compiler_params=_cp(("arbitrary", "arbitrary", "arbitrary")),
    )(pna, pnac, bias, dmixin, dproj, dprojc, *hosted.args)


def tail_fwd_bwd(x, mixin, tgt, mod3, g_post_mix, g_pre_mlp, g_post_mlp, wout, w1, w2):
    nb = x.shape[0]

    def body(x_ref, mi_ref, tgt_ref, mod_ref, gpm_ref, gpl_ref, gpo_ref, wo_ref, w1_ref, w2_ref,
             dx_ref, dmix_ref, h2_ref, du_ref, a_ref, dm_ref, dmi_ref, dmod_ref, dg_ref, loss_ref):
        b, t = pl.program_id(0), pl.program_id(1)
        gt1, sh2, sc2, gt2 = mod_ref[2:3, :], mod_ref[3:4, :], mod_ref[4:5, :], mod_ref[5:6, :]
        mix = jnp.dot(mi_ref[...], wo_ref[...], preferred_element_type=F32)
        (x1, h2), vjp_a = jax.vjp(_post_mix, x_ref[...], mix, gt1, sc2, sh2, gpm_ref[...], gpl_ref[...])
        h2b = h2.astype(BF16)
        h2_ref[...] = h2b
        m = jnp.zeros((TN, D), F32)
        relus = []
        for j in range(4):
            cols = slice(j * D, (j + 1) * D)
            r = jnp.maximum(jnp.dot(h2b, w1_ref[j], preferred_element_type=F32), 0.0)
            ab = (r * r).astype(BF16)
            a_ref[:, cols] = ab
            m = m + jnp.dot(ab, w2_ref[cols, :], preferred_element_type=F32)
            relus.append(r)
        loss, vjp_b = jax.vjp(_head_loss, x1, m, gt2, gpo_ref[...], tgt_ref[...])
        dx1, dm, dgt2, dgpo, _ = vjp_b(jnp.ones((1, 1), F32))
        dmb = dm.astype(BF16)
        dm_ref[...] = dmb
        dh2 = jnp.zeros((TN, D), F32)
        for j in range(4):
            cols = slice(j * D, (j + 1) * D)
            da = lax.dot_general(dmb, w2_ref[cols, :], (((1,), (1,)), ((), ())), preferred_element_type=F32)
            dub = (da * (2.0 * relus[j])).astype(BF16)
            du_ref[:, cols] = dub
            dh2 = dh2 + lax.dot_general(dub, w1_ref[j], (((1,), (1,)), ((), ())), preferred_element_type=F32)
        dx, dmix, dgt1, dsc2, dsh2, dgpm, dgpl = vjp_a((dx1, dh2))
        dx_ref[...] = dx
        dmixb = dmix.astype(BF16)
        dmix_ref[...] = dmixb
        dmi_ref[...] = lax.dot_general(dmixb, wo_ref[...], (((1,), (1,)), ((), ())),
                                       preferred_element_type=F32).astype(BF16)

        @pl.when(t == 0)
        def _():
            dmod_ref[...] = jnp.zeros_like(dmod_ref)

        @pl.when((t == 0) & (b == 0))
        def _():
            dg_ref[...] = jnp.zeros_like(dg_ref)
            loss_ref[...] = jnp.zeros_like(loss_ref)

        dmod_ref[2:3, :] += dgt1
        dmod_ref[3:4, :] += dsh2
        dmod_ref[4:5, :] += dsc2
        dmod_ref[5:6, :] += dgt2
        dg_ref[0:1, :] += dgpm
        dg_ref[1:2, :] += dgpl
        dg_ref[2:3, :] += dgpo
        loss_ref[...] += jnp.broadcast_to(loss, loss_ref.shape)

    tok = lambda b, t: (b, t, 0)
    const = lambda b, t: (0, 0)
    vec = pl.BlockSpec((1, D), const)
    return pl.pallas_call(
        body, name="tail_fwd_bwd", grid=(nb, SEQ // TN),
        in_specs=[
            pl.BlockSpec((None, TN, D), tok), pl.BlockSpec((None, TN, D), tok), pl.BlockSpec((None, TN, D), tok),
            pl.BlockSpec((None, 6, D), lambda b, t: (b, 0, 0)), vec, vec, vec,
            pl.BlockSpec((D, D), const, pipeline_mode=pl.Buffered(1)),
            pl.BlockSpec((4, D, D), lambda b, t: (0, 0, 0), pipeline_mode=pl.Buffered(1)),
            pl.BlockSpec((DFF, D), const, pipeline_mode=pl.Buffered(1)),
        ],
        out_specs=[
            pl.BlockSpec((None, TN, D), tok), pl.BlockSpec((None, TN, D), tok), pl.BlockSpec((None, TN, D), tok),
            pl.BlockSpec((None, TN, DFF), tok), pl.BlockSpec((None, TN, DFF), tok), pl.BlockSpec((None, TN, D), tok),
            pl.BlockSpec((None, TN, D), tok),
            pl.BlockSpec((None, 6, D), lambda b, t: (b, 0, 0)),
            pl.BlockSpec((8, D), const), pl.BlockSpec((8, 128), const),
        ],
        out_shape=[
            jax.ShapeDtypeStruct((nb, SEQ, D), F32), jax.ShapeDtypeStruct((nb, SEQ, D), BF16),
            jax.ShapeDtypeStruct((nb, SEQ, D), BF16), jax.ShapeDtypeStruct((nb, SEQ, DFF), BF16),
            jax.ShapeDtypeStruct((nb, SEQ, DFF), BF16), jax.ShapeDtypeStruct((nb, SEQ, D), BF16),
            jax.ShapeDtypeStruct((nb, SEQ, D), BF16),
            jax.ShapeDtypeStruct((nb, 6, D), F32), jax.ShapeDtypeStruct((8, D), F32),
            jax.ShapeDtypeStruct((8, 128), F32),
        ],
        compiler_params=_cp(("arbitrary", "arbitrary")),
    )(x, mixin, tgt, mod3, g_post_mix, g_pre_mlp, g_post_mlp, wout, w1, w2)


def weight_grad(pairs, name, out_dtype=F32, col_blocks=False, tm=1024, tn=1024, tk=2048):
    m, n = pairs[0][0].shape[1], pairs[0][1].shape[1]
    tn = min(tn, n)
    tks = [min(tk, xa.shape[0]) for xa, _ in pairs]
    steps = [xa.shape[0] // t for (xa, _), t in zip(pairs, tks)]
    total = sum(steps)
    offs = [sum(steps[:i]) for i in range(len(pairs))]

    def body(*refs):
        out_ref, acc = refs[2 * len(pairs)], refs[-1]
        k = pl.program_id(2)

        @pl.when(k == 0)
        def _():
            acc[...] = jnp.zeros_like(acc)

        for i in range(len(pairs)):
            @pl.when((k >= offs[i]) & (k < offs[i] + steps[i]))
            def _(i=i):
                acc[...] += lax.dot_general(refs[2 * i][...], refs[2 * i + 1][...], (((0,), (0,)), ((), ())),
                                            preferred_element_type=F32)

        if out_dtype != F32:
            @pl.when(k == total - 1)
            def _():
                out_ref[...] = acc[...].astype(out_dtype)

    in_specs, args = [], []
    for i, (xa, ya) in enumerate(pairs):
        clamp = lambda k, i=i: jnp.clip(k - offs[i], 0, steps[i] - 1)
        in_specs.append(pl.BlockSpec((tks[i], tm), lambda a, c, k, clamp=clamp: (clamp(k), a)))
        in_specs.append(pl.BlockSpec((tks[i], tn), lambda a, c, k, clamp=clamp: (clamp(k), c)))
        args += [xa, ya]
    if col_blocks:
        out_spec = pl.BlockSpec((None, tm, tn), lambda a, c, k: (c, a, 0))
        out_shape = jax.ShapeDtypeStruct((n // tn, m, tn), out_dtype)
    else:
        out_spec = pl.BlockSpec((tm, tn), lambda a, c, k: (a, c))
        out_shape = jax.ShapeDtypeStruct((m, n), out_dtype)
    return pl.pallas_call(
        body, name=name, grid=(m // tm, n // tn, total), in_specs=in_specs, out_specs=out_spec, out_shape=out_shape,
        scratch_shapes=[] if out_dtype == F32 else [pltpu.VMEM((tm, tn), F32)],
        compiler_params=_cp(("arbitrary", "arbitrary", "arbitrary")),
    )(*args)


def _perm_block(t):
    return 4 * (t % 4) + t // 4 if t < 16 else 16 + 3 * ((t - 16) % 4) + (t - 16) // 4


def unpack_w_in(blocks):
    def body(i_ref, o_ref):
        for t in range(28):
            p = _perm_block(t)
            o_ref[:, p * 128:(p + 1) * 128] = i_ref[t // 7, :, (t % 7) * 128:(t % 7 + 1) * 128]

    return pl.pallas_call(
        body, name="unpack_w_in", grid=(2,),
        in_specs=[pl.BlockSpec((4, D // 2, 896), lambda i: (0, i, 0))],
        out_specs=pl.BlockSpec((D // 2, IN_W), lambda i: (i, 0)),
        out_shape=jax.ShapeDtypeStruct((D, IN_W), BF16),
    )(blocks)


def pack_w_in(dw):
    def body(i_ref, o_ref):
        for t in range(28):
            p = _perm_block(t)
            o_ref[t // 7, :, (t % 7) * 128:(t % 7 + 1) * 128] = i_ref[:, p * 128:(p + 1) * 128].astype(BF16)

    return pl.pallas_call(
        body, name="pack_w_in", grid=(4,),
        in_specs=[pl.BlockSpec((D // 4, IN_W), lambda i: (i, 0))],
        out_specs=pl.BlockSpec((4, D // 4, 896), lambda i: (0, i, 0)),
        out_shape=jax.ShapeDtypeStruct((4, D, 896), BF16),
    )(dw)


def _place():
    return lax.axis_index("x"), lax.axis_index("y"), lax.axis_index("c")


class Hosted:
    def __init__(self, args, out_shape, scratch, start, finish):
        self.args, self.out_shape, self.scratch, self.start, self.finish = args, out_shape, scratch, start, finish

    def specs(self):
        hbm = pl.BlockSpec(memory_space=pl.ANY)
        return [hbm] * len(self.args), [hbm] * len(self.out_shape)

    def split(self, refs, n_in, n_out):
        a, b = len(self.args), len(self.out_shape)
        cuts = [n_in, n_in + a, n_in + a + n_out, n_in + a + n_out + b, len(refs) - len(self.scratch)]
        parts = [refs[i:j] for i, j in zip([0] + cuts, cuts + [len(refs)])]
        return parts[0], parts[1], parts[2], parts[3], parts[4], parts[5]


def run_hosted(hosted, name):
    def body(*refs):
        _, ins, _, outs, _, sems = hosted.split(refs, 0, 0)
        hosted.start(ins, outs, sems)
        hosted.finish(ins, outs, sems)

    in_specs, out_specs = hosted.specs()
    return pl.pallas_call(body, name=name, in_specs=in_specs, out_specs=out_specs, out_shape=hosted.out_shape,
                          scratch_shapes=hosted.scratch)(*hosted.args)


def gather8(blocks):
    na = len(blocks)

    def copies(ins, outs, sems):
        send_sems, recv_sems, local_sem = sems
        x, y, c = _place()
        me, sibling = (x, y, c), (x, y, 1 - c)
        chips = [(1 - x, y), (x, 1 - y), (1 - x, 1 - y)]

        def slot(o_ref, px, py, pc):
            return o_ref.at[4 * px + 2 * py + pc]

        def copy(a, k, block, to, src=None):
            return pltpu.make_async_remote_copy(
                src_ref=slot(outs[a], *block) if src is None else src, dst_ref=slot(outs[a], *block),
                send_sem=send_sems.at[a, k], recv_sem=recv_sems.at[a, k], device_id=to, device_id_type=MESH)

        mine = [pltpu.make_async_copy(ins[a], slot(outs[a], *me), local_sem.at[a]) for a in range(na)]
        first = []
        for a in range(na):
            first.append(copy(a, 0, me, sibling, src=ins[a]))
            first += [copy(a, 1 + j, me, (*chip, c), src=ins[a]) for j, chip in enumerate(chips)]
        return copy, mine, first, me, sibling, chips, c

    def start(ins, outs, sems):
        _, mine, first, *_ = copies(ins, outs, sems)
        for cp in mine + first:
            cp.start()

    def finish(ins, outs, sems):
        copy, mine, first, me, sibling, chips, c = copies(ins, outs, sems)
        passed = []
        for j, chip in enumerate(chips):
            for a in range(na):
                copy(a, 1 + j, (*chip, c), me).wait_recv()
                cp = copy(a, 4 + j, (*chip, c), sibling)
                cp.start()
                passed.append(cp)
        for a in range(na):
            copy(a, 0, sibling, me).wait_recv()
            for j, chip in enumerate(chips):
                copy(a, 4 + j, (*chip, 1 - c), me).wait_recv()
        for cp in first + passed:
            cp.wait_send()
        for cp in mine:
            cp.wait()

    return Hosted(list(blocks), [jax.ShapeDtypeStruct((8,) + b.shape, b.dtype) for b in blocks],
                  [pltpu.SemaphoreType.DMA((na, 7)), pltpu.SemaphoreType.DMA((na, 7)), pltpu.SemaphoreType.DMA((na,))],
                  start, finish)


def all_gather8(blocks, name):
    return run_hosted(gather8(blocks), name)


def chips3(arrays):
    na = len(arrays)

    def copies(ins, outs, sems):
        send_sems, recv_sems = sems
        x, y, c = _place()
        return [pltpu.make_async_remote_copy(
            src_ref=ins[a].at[2 * px + py], dst_ref=outs[a].at[k], send_sem=send_sems.at[a, k],
            recv_sem=recv_sems.at[a, k], device_id=(px, py, c), device_id_type=MESH)
            for a in range(na) for k, (px, py) in enumerate([(1 - x, y), (x, 1 - y), (1 - x, 1 - y)])]

    def start(ins, outs, sems):
        for cp in copies(ins, outs, sems):
            cp.start()

    def finish(ins, outs, sems):
        for cp in copies(ins, outs, sems):
            cp.wait()

    return Hosted(list(arrays), [jax.ShapeDtypeStruct((3,) + a.shape[1:], a.dtype) for a in arrays],
                  [pltpu.SemaphoreType.DMA((na, 3)), pltpu.SemaphoreType.DMA((na, 3))], start, finish)


def sibling_exchange(arrays, name):
    na = len(arrays)

    def body(*refs):
        ins, outs = refs[:na], refs[na:2 * na]
        send_sems, recv_sems = refs[2 * na:]
        x, y, c = _place()
        cps = [pltpu.make_async_remote_copy(
            src_ref=ins[a], dst_ref=outs[a], send_sem=send_sems.at[a], recv_sem=recv_sems.at[a],
            device_id=(x, y, 1 - c), device_id_type=MESH) for a in range(na)]
        for cp in cps:
            cp.start()
        for cp in cps:
            cp.wait()

    hbm = pl.BlockSpec(memory_space=pl.ANY)
    return pl.pallas_call(
        body, name=name, in_specs=[hbm] * na, out_specs=[hbm] * na,
        out_shape=[jax.ShapeDtypeStruct(a.shape, a.dtype) for a in arrays],
        scratch_shapes=[pltpu.SemaphoreType.DMA((na,)), pltpu.SemaphoreType.DMA((na,))],
    )(*arrays)


def siblings4(arrays):
    na = len(arrays)

    def copies(ins, outs, sems):
        send_sems, recv_sems = sems
        x, y, c = _place()
        return [pltpu.make_async_remote_copy(
            src_ref=ins[a].at[2 * j + 1 - c], dst_ref=outs[a].at[j],
            send_sem=send_sems.at[a, j], recv_sem=recv_sems.at[a, j],
            device_id=(x, y, 1 - c), device_id_type=MESH) for a in range(na) for j in range(4)]

    def start(ins, outs, sems):
        for cp in copies(ins, outs, sems):
            cp.start()

    def finish(ins, outs, sems):
        for cp in copies(ins, outs, sems):
            cp.wait()

    return Hosted(list(arrays), [jax.ShapeDtypeStruct((4,) + a.shape[1:], a.dtype) for a in arrays],
                  [pltpu.SemaphoreType.DMA((na, 4)), pltpu.SemaphoreType.DMA((na, 4))], start, finish)


def sibling_blocks(arrays, name):
    return run_hosted(siblings4(arrays), name)


def _row_tile(r):
    for cand in (512, 256, 128, 64, 32, 16, 8):
        if r % cand == 0:
            return cand
    return r


def chip_partial(place, g8, landed4, name):
    _, r, ccols = g8.shape
    tr = _row_tile(r)

    def body(place_ref, g_ref, l_ref, o_ref):
        del place_ref
        o_ref[...] = (g_ref[...].astype(F32) + l_ref[...].astype(F32)).astype(BF16)

    spec = pl.BlockSpec((None, tr, ccols), lambda j, i, s: (j, i, 0))
    return pl.pallas_call(
        body, name=name,
        grid_spec=pltpu.PrefetchScalarGridSpec(
            num_scalar_prefetch=1, grid=(4, r // tr),
            in_specs=[pl.BlockSpec((None, tr, ccols), lambda j, i, s: (2 * j + s[0], i, 0)), spec], out_specs=spec),
        out_shape=jax.ShapeDtypeStruct((4, r, ccols), BF16),
    )(place, g8, landed4)


def shard_sum(place, partial4, landed3, name):
    _, r, ccols = partial4.shape
    tr = _row_tile(r)

    def body(place_ref, p_ref, l_ref, o_ref):
        del place_ref
        acc = p_ref[...].astype(F32)
        for k in range(3):
            acc = acc + l_ref[k].astype(F32)
        o_ref[...] = acc

    return pl.pallas_call(
        body, name=name,
        grid_spec=pltpu.PrefetchScalarGridSpec(
            num_scalar_prefetch=1, grid=(r // tr,),
            in_specs=[pl.BlockSpec((None, tr, ccols), lambda i, s: (s[1], i, 0)),
                      pl.BlockSpec((3, tr, ccols), lambda i, s: (0, i, 0))],
            out_specs=pl.BlockSpec((tr, ccols), lambda i, s: (i, 0))),
        out_shape=jax.ShapeDtypeStruct((r, ccols), F32),
    )(place, partial4, landed3)


def _adamw_math(w, g, m, v):
    m2 = B1 * m + (1.0 - B1) * g
    v2 = B2 * v + (1.0 - B2) * (g * g)
    m_hat = m2 / (1.0 - B1 ** STEP)
    v_hat = v2 / (1.0 - B2 ** STEP)
    return -LR * (m_hat / (jnp.sqrt(v_hat) + AEPS) + WD * w), m2, v2


def adamw_halves(place, w, mine, theirs, m, v, name):
    r, ccols = w.shape
    hr = r // 2
    tr = _row_tile(hr)
    nt = hr // tr

    def body(place_ref, w_ref, a_ref, b_ref, m_ref, v_ref, g_out, d_out, m_out, v_out):
        g = jnp.where(pl.program_id(0) == place_ref[0], a_ref[...], b_ref[...])
        d, m2, v2 = _adamw_math(w_ref[...], g, m_ref[...], v_ref[...])
        g_out[...] = g
        d_out[...] = d
        m_out[...] = m2
        v_out[...] = v2

    full = pl.BlockSpec((tr, ccols), lambda h, i, s: (h * nt + i, 0))
    part = pl.BlockSpec((tr, ccols), lambda h, i, s: (i, 0))
    return pl.pallas_call(
        body, name=name,
        grid_spec=pltpu.PrefetchScalarGridSpec(
            num_scalar_prefetch=1, grid=(2, nt), in_specs=[full, part, part, full, full], out_specs=[full] * 4),
        out_shape=[jax.ShapeDtypeStruct((r, ccols), F32)] * 4,
    )(place, w, mine, theirs, m, v)


def adamw_group(place, halved, plain, hosted, name):
    rows = halved[0][0].shape[0]
    tr = 64
    nt = rows // 2 // tr
    nh, npl = len(halved), len(plain)

    def body(place_ref, *refs):
        own_in, h_in, own_out, h_out, _, h_sems = hosted.split(refs, 5 * nh + 4 * npl, 4 * nh + 3 * npl)
        half = pl.program_id(0)
        grid_step = half * nt + pl.program_id(1)

        @pl.when(grid_step == 0)
        def _():
            hosted.start(h_in, h_out, h_sems)

        for i in range(nh):
            w_ref, a_ref, b_ref, m_ref, v_ref = own_in[5 * i:5 * i + 5]
            g = jnp.where(half == place_ref[0], a_ref[...], b_ref[...])
            res = (g,) + _adamw_math(w_ref[...], g, m_ref[...], v_ref[...])
            for o_ref, r in zip(own_out[4 * i:4 * i + 4], res):
                o_ref[...] = r
        for i in range(npl):
            w_ref, g_ref, m_ref, v_ref = own_in[5 * nh + 4 * i:5 * nh + 4 * i + 4]
            res = _adamw_math(w_ref[...], g_ref[...], m_ref[...], v_ref[...])
            for o_ref, r in zip(own_out[4 * nh + 3 * i:4 * nh + 3 * i + 3], res):
                o_ref[...] = r

        @pl.when(grid_step == 2 * nt - 1)
        def _():
            hosted.finish(h_in, h_out, h_sems)

    def full(cols):
        return pl.BlockSpec((tr, cols), lambda h, i, s: (h * nt + i, 0))

    def part(cols):
        return pl.BlockSpec((tr, cols), lambda h, i, s: (i, 0))

    in_specs, out_specs, out_shape, args = [], [], [], []
    for w, a, b, m, v in halved:
        cols = w.shape[1]
        in_specs += [full(cols), part(cols), part(cols), full(cols), full(cols)]
        out_specs += [full(cols)] * 4
        out_shape += [jax.ShapeDtypeStruct(w.shape, F32)] * 4
        args += [w, a, b, m, v]
    for w, g, m, v in plain:
        cols = w.shape[1]
        in_specs += [full(cols)] * 4
        out_specs += [full(cols)] * 3
        out_shape += [jax.ShapeDtypeStruct(w.shape, F32)] * 3
        args += [w, g, m, v]
    h_in_specs, h_out_specs = hosted.specs()
    return pl.pallas_call(
        body, name=name,
        grid_spec=pltpu.PrefetchScalarGridSpec(
            num_scalar_prefetch=1, grid=(2, nt), in_specs=in_specs + h_in_specs, out_specs=out_specs + h_out_specs,
            scratch_shapes=hosted.scratch),
        out_shape=out_shape + hosted.out_shape,
        compiler_params=_cp(("arbitrary", "arbitrary")),
    )(place, *args, *hosted.args)


def _silu(x):
    return x * jax.nn.sigmoid(x)


def mod_shard(cin, w_ada, b_shard):
    def body(c_ref, w_ref, b_ref, o_ref):
        o_ref[...] = _nn(_silu(c_ref[...]), w_ref[...]) + b_ref[...]

    return pl.pallas_call(
        body, name="mod_shard", grid=(3,),
        in_specs=[pl.BlockSpec((32, D), lambda j: (0, 0)), pl.BlockSpec((D, 512), lambda j: (0, j)),
                  pl.BlockSpec((1, 512), lambda j: (0, j))],
        out_specs=pl.BlockSpec((32, 512), lambda j: (0, j)),
        out_shape=jax.ShapeDtypeStruct((32, 1536), F32),
    )(cin, w_ada, b_shard)


def ada_grads(cin, gb, gc, w_ada):
    def body(c_ref, gb_ref, gc_ref, w_ref, gw_ref, pc_ref):
        ctx_tot = jnp.sum(gc_ref[...], axis=0, keepdims=True)
        rows = lax.broadcasted_iota(jnp.int32, (16, 512), 0)
        dm = jnp.concatenate([gb_ref[...], jnp.where(rows == 0, ctx_tot, 0.0)], axis=0)
        gw_ref[...] = _tn(_silu(c_ref[...]), dm)
        rows8 = lax.broadcasted_iota(jnp.int32, (8, 512), 0)
        part = _nt(jnp.where(rows8 == 0, ctx_tot, 0.0), w_ref[...])

        @pl.when(pl.program_id(0) == 0)
        def _():
            pc_ref[...] = jnp.zeros_like(pc_ref)

        pc_ref[...] += part

    return pl.pallas_call(
        body, name="ada_grads", grid=(3,),
        in_specs=[pl.BlockSpec((32, D), lambda j: (0, 0)), pl.BlockSpec((16, 512), lambda j: (0, j)),
                  pl.BlockSpec((8, 512), lambda j: (0, j)), pl.BlockSpec((D, 512), lambda j: (0, j))],
        out_specs=[pl.BlockSpec((D, 512), lambda j: (0, j)), pl.BlockSpec((8, D), lambda j: (0, 0))],
        out_shape=[jax.ShapeDtypeStruct((D, 1536), F32), jax.ShapeDtypeStruct((8, D), F32)],
    )(cin, gb, gc, w_ada)


SMALL_SUM_ROWS = 15


def small_update(gsm, gbf, gcf, pcg, params):
    n = len(params)

    def body(*refs):
        gsm_ref, gbf_ref, gcf_ref, pcg_ref = refs[:4]
        wmv, outs, loss_out = refs[4:4 + 3 * n], refs[4 + 3 * n:4 + 7 * n], refs[-1]
        acc = gsm_ref[0]
        for dev in range(1, 8):
            acc = acc + gsm_ref[dev]
        c_ctx = wmv[0][...]
        sg = jax.nn.sigmoid(c_ctx)
        dsilu = pcg_ref[0:1, :] + pcg_ref[2:3, :] + pcg_ref[4:5, :] + pcg_ref[6:7, :]
        lane = lax.broadcasted_iota(jnp.int32, (1, D), 1)
        last = acc[14:15, :]
        grads = [
            dsilu * (sg * (1.0 + c_ctx * (1.0 - sg))),
            jnp.sum(gbf_ref[...], axis=0, keepdims=True) + jnp.sum(gcf_ref[...], axis=0, keepdims=True),
            acc[0:1, :] + acc[1:2, :], acc[2:3, :], acc[3:4, :], acc[4:5, :],
            acc[5:6, 0:512], acc[6:14, :], jnp.where(lane < 8, last, 0.0),
        ]
        loss_out[...] = jnp.broadcast_to(jnp.sum(jnp.where(lane == 8, last, 0.0), axis=1, keepdims=True), (8, 128))
        for i, g in enumerate(grads):
            d, m2, v2 = _adamw_math(wmv[3 * i][...], g, wmv[3 * i + 1][...], wmv[3 * i + 2][...])
            outs[4 * i][...] = g
            outs[4 * i + 1][...] = d
            outs[4 * i + 2][...] = m2
            outs[4 * i + 3][...] = v2

    flat = [a for wmv in params for a in wmv]
    out_shape = [jax.ShapeDtypeStruct(w.shape, F32) for w, _, _ in params for _ in range(4)]
    return pl.pallas_call(
        body, name="small_update", out_shape=out_shape + [jax.ShapeDtypeStruct((8, 128), F32)],
    )(gsm, gbf, gcf, pcg, *flat)


def _pad_row(v, rows):
    flat = v.reshape(-1)
    return jnp.pad(flat, (0, rows * D - flat.shape[0])).reshape(rows, D)


def local_step(x, ctx, tgt, mod3, g_pre_mix, g_post_mix, g_pre_mlp, g_post_mlp, ret_decay, ret_gn, na_rpb,
               wperm, late_weights, early_grads):
    nb = x.shape[0]
    tokens = nb * SEQ
    cos, sin = _rope_tables()
    rd = ret_decay.T.reshape(RH, 2, 1)
    gn = ret_gn.reshape(RH, 1, RD)
    bias = na_bias_table(_rpb_flat(na_rpb))
    h, pret, pna = premix_proj(x, mod3, g_pre_mix, wperm, False, "premix_proj")
    hc, pretc, pnac = premix_proj(ctx, mod3, g_pre_mix, wperm, True, "premix_proj_ctx")
    o_all, mixin, gw_out = retention_fwd(pret, pretc, rd, gn, cos, sin, late_weights(0))
    mixin, gw1, gw2 = na_fwd(pna, pnac, bias, mixin, late_weights(1))
    dx_tail, dmix, h2, du, act, dm, dmixin, dmod_t, dg_t, loss_t = tail_fwd_bwd(
        x, mixin, tgt, mod3, g_post_mix, g_pre_mlp, g_post_mlp, gw_out.reshape(D, D), gw1.reshape(4, D, D),
        gw2.reshape(DFF, D))
    dw_out = weight_grad([(mixin.reshape(tokens, D), dmix.reshape(tokens, D))], "grad_w_out", BF16)
    dw1 = weight_grad([(h2.reshape(tokens, D), du.reshape(tokens, DFF))], "grad_w_mlp1", BF16, col_blocks=True)
    dw2 = weight_grad([(act.reshape(tokens, DFF), dm.reshape(tokens, D))], "grad_w_mlp2", BF16)
    dproj, dprojc, drd, dgn, *landed = retention_bwd(pret, pretc, o_all, dmixin, rd, gn, cos, sin,
                                                     early_grads[0](dw_out, dw1, dw2))
    dproj, dprojc, dpat, *early = na_bwd(pna, pnac, bias, dmixin, dproj, dprojc, early_grads[1](landed))
    grad_x, dmod_a, dg_a = premix_bwd(x, mod3, g_pre_mix, wperm, dproj, dx_tail, "premix_bwd")
    dmod_c, dg_c = premix_bwd(ctx, mod3, g_pre_mix, wperm, dprojc, None, "premix_bwd_ctx")
    dw_in = weight_grad([(h.reshape(tokens, D), dproj.reshape(tokens, IN_W)),
                         (hc.reshape(nb * LC, D), dprojc.reshape(nb * LC, IN_W))], "grad_w_in", tn=512)
    dmod = jnp.concatenate([jnp.concatenate([dmod_a[:, 0:2], dmod_t[:, 2:6]], axis=1), dmod_c], axis=0)
    last = jnp.pad(jnp.concatenate([drd[:, :, 0].T.reshape(8), loss_t[0, 0:1]]), (0, D - 9)).reshape(1, D)
    small = jnp.concatenate([dg_a[0:1], dg_c[0:1], dg_t[0:3], _pad_row(dgn, 1), dpat.reshape(8, D), last], axis=0)
    return grad_x, dw_in, early, dmod, small


def kernel(x, c, ctx, c_ctx, w_ada, b_ada, g_pre_mix, g_post_mix, g_pre_mlp, g_post_mlp, w_in, ret_decay, ret_gn, na_rpb, w_out, w_mlp1, w_mlp2, loss_target, m_c_ctx, m_w_ada, m_b_ada, m_g_pre_mix, m_g_post_mix, m_g_pre_mlp, m_g_post_mlp, m_w_in, m_ret_decay, m_ret_gn, m_na_rpb, m_w_out, m_w_mlp1, m_w_mlp2, v_c_ctx, v_w_ada, v_b_ada, v_g_pre_mix, v_g_post_mix, v_g_pre_mlp, v_g_post_mlp, v_w_in, v_ret_decay, v_ret_gn, v_na_rpb, v_w_out, v_w_mlp1, v_w_mlp2):
    px, py, pc = _place()
    dev = 4 * px + 2 * py + pc
    chip = 2 * px + py

    def my_half(w2d):
        rows = w2d.shape[0] // 2
        return lax.dynamic_slice_in_dim(w2d, pc * rows, rows, 0)

    halves = [my_half(w[0]).astype(BF16) for w in (w_in, w_out, w_mlp1, w_mlp2)]
    gw_in, cg = all_gather8([halves[0], jnp.pad(c, ((0, 6), (0, 0)))], "gather_w_in")
    wperm = unpack_w_in(gw_in.reshape(4, D, 896))

    cin = jnp.pad(cg[:, 0:2].reshape(16, D), ((0, 16), (0, 0))) + jnp.pad(c_ctx[None], ((16, 15), (0, 0)))
    mod_mine = mod_shard(cin, w_ada[0], lax.dynamic_slice_in_dim(b_ada, chip * 1536, 1536, 1))
    (mg,) = all_gather8([mod_mine], "gather_mod")
    mod_all = jnp.concatenate([mg[0], mg[2], mg[4], mg[6]], axis=1)
    mod3 = (jnp.pad(lax.dynamic_slice_in_dim(mod_all, 2 * dev, 2, 0), ((0, 1), (0, 0)))
            + jnp.pad(mod_all[16:17], ((2, 0), (0, 0)))).reshape(3, 6, D)

    place = jnp.stack([pc, chip]).astype(jnp.int32)

    early_names = ["w_out", "w_mlp1", "w_mlp2"]
    early_g8, early_partial = [], []

    def early_a(dw_out, dw1, dw2):
        early_g8[:] = [dw_out.reshape(8, 128, D), dw1.reshape(8, 512, D), dw2.reshape(8, 512, D)]
        return siblings4(early_g8)

    def early_b(landed):
        early_partial[:] = [chip_partial(place, g, l, "rs_chip_sum_" + n)
                            for g, l, n in zip(early_g8, landed, early_names)]
        return chips3(early_partial)

    grad_x, dw_in, early_landed, dmod, small = local_step(
        x, ctx, loss_target, mod3, g_pre_mix, g_post_mix, g_pre_mlp, g_post_mlp, ret_decay[0], ret_gn, na_rpb[0],
        wperm, lambda k: gather8(halves[1:2] if k == 0 else halves[2:4]), (early_a, early_b))
    early_mine = [shard_sum(place, p, l, "rs_shard_sum_" + n)
                  for p, l, n in zip(early_partial, early_landed, early_names)]
    early_theirs = sibling_exchange(early_mine, "rs_halves_early")

    pay = jnp.concatenate([dmod.reshape(18, D), small, jnp.zeros((40 - 18 - SMALL_SUM_ROWS, D), F32)], axis=0)
    (gs,) = all_gather8([pay], "gather_small")
    gbf = gs[:, 0:12].reshape(16, 6 * D)
    gcf = gs[:, 12:18].reshape(8, 6 * D)
    gw_ada, pc_part = ada_grads(cin, lax.dynamic_slice_in_dim(gbf, chip * 1536, 1536, 1),
                                lax.dynamic_slice_in_dim(gcf, chip * 1536, 1536, 1), w_ada[0])
    (pcg,) = all_gather8([pc_part], "gather_c_ctx")

    g8_in = pack_w_in(dw_in).reshape(8, 512, 896)
    (landed_in,) = sibling_blocks([g8_in], "rs_sibling_w_in")
    partial_in = chip_partial(place, g8_in, landed_in, "rs_chip_sum_w_in")
    *grouped, landed3_in = adamw_group(
        place,
        [(w_mlp1[0], early_mine[1], early_theirs[1], m_w_mlp1[0], v_w_mlp1[0]),
         (w_mlp2[0], early_mine[2], early_theirs[2], m_w_mlp2[0], v_w_mlp2[0])],
        [(w_ada[0], gw_ada, m_w_ada[0], v_w_ada[0])], chips3([partial_in]), "adamw_group")
    d_ada, m_ada, v_ada = grouped[8:11]
    mine_in = shard_sum(place, partial_in, landed3_in, "rs_shard_sum_w_in")
    (theirs_in,) = sibling_exchange([mine_in], "rs_halves_w_in")
    big = [
        [r[None] for r in adamw_halves(place, w_in[0], mine_in, theirs_in, m_w_in[0], v_w_in[0], "adamw_w_in")],
        [r[None] for r in adamw_halves(place, w_out[0], early_mine[0], early_theirs[0], m_w_out[0], v_w_out[0],
                                       "adamw_w_out")],
        [r[None] for r in grouped[0:4]], [r[None] for r in grouped[4:8]],
    ]

    def rpb_rows(t):
        return _rpb_flat(t[0]).reshape(8, D)

    def decay_row(t):
        return jnp.pad(t.reshape(1, 8), ((0, 0), (0, D - 8)))

    views = [lambda t: t.reshape(1, D), lambda t: t, lambda t: t, lambda t: t, lambda t: t, lambda t: t, lambda t: t,
             rpb_rows, decay_row]
    back = [lambda t: t.reshape(D), lambda t: t, lambda t: t, lambda t: t, lambda t: t, lambda t: t, lambda t: t,
            lambda t: _rpb_flat_t(t)[None], lambda t: t[:, 0:8].reshape(1, 2, 4)]
    small_w = (c_ctx, b_ada, g_pre_mix, g_post_mix, g_pre_mlp, g_post_mlp, ret_gn, na_rpb, ret_decay)
    small_m = (m_c_ctx, m_b_ada, m_g_pre_mix, m_g_post_mix, m_g_pre_mlp, m_g_post_mlp, m_ret_gn, m_na_rpb, m_ret_decay)
    small_v = (v_c_ctx, v_b_ada, v_g_pre_mix, v_g_post_mix, v_g_pre_mlp, v_g_post_mlp, v_ret_gn, v_na_rpb, v_ret_decay)
    *res, loss8 = small_update(gs[:, 18:18 + SMALL_SUM_ROWS], gbf, gcf, pcg[:, 0],
                               [(f(w), f(m), f(v)) for f, w, m, v in zip(views, small_w, small_m, small_v)])

    def leaves(ada, idx):
        s_c, s_b, s_g1, s_g2, s_g3, s_g4, s_gn, s_rpb, s_rd = [back[i](res[4 * i + idx]) for i in range(9)]
        return [s_c, ada[None], s_b, s_g1, s_g2, s_g3, s_g4, big[0][idx], s_rd, s_gn, s_rpb,
                big[1][idx], big[2][idx], big[3][idx]]

    return (loss8[0, 0], grad_x, *leaves(gw_ada, 0), *leaves(d_ada, 1), *leaves(m_ada, 2), *leaves(v_ada, 3))
```

```python
import functools

import jax
import jax.numpy as jnp
from jax import lax
from jax.experimental import pallas as pl
from jax.experimental.pallas import tpu as pltpu

F32, BF16 = jnp.float32, jnp.bfloat16
D = 1024
SEQ = 2048
LC = 256
GW = 64
RH, RD, CH = 4, 128, 128
NPAIR = 4
IN_W = 3584
RET_W = 2048
DFF = 4096
EPS = 1e-6
NEG = -1e30
TN = 256
NCH = SEQ // CH
LR, B1, B2, AEPS, WD, STEP = 0.001, 0.9, 0.999, 1e-08, 0.01, 10
MESH = pl.DeviceIdType.MESH
VMEM_LIMIT = 56 * 1024 * 1024


def _cp(sem=None):
    return pltpu.CompilerParams(dimension_semantics=sem, vmem_limit_bytes=VMEM_LIMIT)


def _nn(a, b):
    return jnp.dot(a.astype(BF16), b.astype(BF16), preferred_element_type=F32)


def _nt(a, b):
    return lax.dot_general(a.astype(BF16), b.astype(BF16), (((1,), (1,)), ((), ())), preferred_element_type=F32)


def _tn(a, b):
    return lax.dot_general(a.astype(BF16), b.astype(BF16), (((0,), (0,)), ((), ())), preferred_element_type=F32)


@jax.custom_vjp
def mm_nn(a, b):
    return _nn(a, b)


@jax.custom_vjp
def mm_nt(a, b):
    return _nt(a, b)


@jax.custom_vjp
def mm_tn(a, b):
    return _tn(a, b)


mm_nn.defvjp(lambda a, b: (_nn(a, b), (a, b)), lambda r, g: (_nt(g, r[1]), _tn(r[0], g)))
mm_nt.defvjp(lambda a, b: (_nt(a, b), (a, b)), lambda r, g: (_nn(g, r[1]), _tn(g, r[0])))
mm_tn.defvjp(lambda a, b: (_tn(a, b), (a, b)), lambda r, g: (_nt(r[1], g), _nn(r[0], g)))


def _rms(x):
    return x * lax.rsqrt(jnp.mean(x * x, axis=-1, keepdims=True) + EPS)


def _rms_mod(x, g, sc, sh):
    return (_rms(x) * g) * (1.0 + sc) + sh


def _post_mix(x, mix, gt1, sc2, sh2, g_post_mix, g_pre_mlp):
    x1 = x + gt1 * (_rms(mix) * g_post_mix)
    return x1, _rms_mod(x1, g_pre_mlp, sc2, sh2)


def _head_loss(x1, m, gt2, g_post_mlp, tgt):
    err = x1 + gt2 * (_rms(m) * g_post_mlp) - tgt
    return 0.5 * jnp.sum(jnp.mean(err * err, axis=-1, keepdims=True), axis=0, keepdims=True)


def _ln_gate(o, g, w):
    mu = jnp.mean(o, axis=-1, keepdims=True)
    var = jnp.mean(jnp.square(o - mu), axis=-1, keepdims=True)
    y = (o - mu) * lax.rsqrt(var + EPS)
    return (y * w) * (g * jax.nn.sigmoid(g))


def _swap32(x):
    lane = lax.broadcasted_iota(jnp.int32, x.shape, 1)
    return jnp.where((lane & 32) == 0, pltpu.roll(x, 96, 1), pltpu.roll(x, 32, 1))


def _rope(x, cos, sin):
    return x * cos + _swap32(x) * sin


def _rope_t(g, cos, sin):
    return g * cos + _swap32(g * sin)


def _rope_tables():
    tok = jnp.arange(SEQ)
    pos_r = (tok // GW).astype(F32)
    pos_c = (tok % GW).astype(F32)
    inv = 10000.0 ** (-jnp.arange(32, dtype=F32) / 32)
    ar = pos_r[:, None] * inv[None, :]
    ac = pos_c[:, None] * inv[None, :]
    cos = jnp.concatenate([jnp.cos(ar), jnp.cos(ar), jnp.cos(ac), jnp.cos(ac)], axis=-1)
    sin = jnp.concatenate([-jnp.sin(ar), jnp.sin(ar), -jnp.sin(ac), jnp.sin(ac)], axis=-1)
    return cos, sin


def _fiota(shape, dim):
    return lax.broadcasted_iota(jnp.int32, shape, dim).astype(F32)


def _ret_state(k, v, s, lg, reverse):
    pos = _fiota((CH, 1), 0)
    b_exp = pos if reverse else (CH - 1.0 - pos)
    return jnp.exp(lg * CH) * s + mm_tn(k * jnp.exp(lg * b_exp), v)


def _ret_chunk(q, k, v, s, lg, reverse):
    i = _fiota((CH, CH), 0)
    j = _fiota((CH, CH), 1)
    diff = (j - i) if reverse else (i - j)
    mask = (diff > 0) if reverse else (diff >= 0)
    decay = jnp.where(mask, jnp.exp(lg * jnp.where(mask, diff, 0.0)), 0.0)
    pos = _fiota((CH, 1), 0)
    a_exp = (CH - pos) if reverse else (pos + 1.0)
    o = mm_nn(mm_nt(q, k) * decay, v) + mm_nn(q * jnp.exp(lg * a_exp), s)
    return o, _ret_state(k, v, s, lg, reverse)


def premix_proj(xin, mod3, g_pre, wperm, is_ctx, name):
    nb, length, _ = xin.shape
    tn = min(TN, length)

    def body(x_ref, mod_ref, g_ref, w_ref, h_ref, pret_ref, pna_ref):
        h = _rms_mod(x_ref[...], g_ref[...], mod_ref[1:2, :], mod_ref[0:1, :])
        hb = h.astype(BF16)
        h_ref[...] = hb
        pret_ref[...] = jnp.dot(hb, w_ref[:, :RET_W], preferred_element_type=F32)
        pna_ref[...] = jnp.dot(hb, w_ref[:, RET_W:], preferred_element_type=F32).astype(BF16)

    return pl.pallas_call(
        body, name=name, grid=(nb, length // tn),
        in_specs=[
            pl.BlockSpec((None, tn, D), lambda b, t: (b, t, 0)),
            pl.BlockSpec((None, 6, D), (lambda b, t: (2, 0, 0)) if is_ctx else (lambda b, t: (b, 0, 0))),
            pl.BlockSpec((1, D), lambda b, t: (0, 0)),
            pl.BlockSpec((D, IN_W), lambda b, t: (0, 0), pipeline_mode=pl.Buffered(1)),
        ],
        out_specs=[
            pl.BlockSpec((None, tn, D), lambda b, t: (b, t, 0)),
            pl.BlockSpec((None, tn, RET_W), lambda b, t: (b, t, 0)),
            pl.BlockSpec((None, tn, IN_W - RET_W), lambda b, t: (b, t, 0)),
        ],
        out_shape=[
            jax.ShapeDtypeStruct((nb, length, D), BF16),
            jax.ShapeDtypeStruct((nb, length, RET_W), F32),
            jax.ShapeDtypeStruct((nb, length, IN_W - RET_W), BF16),
        ],
        compiler_params=_cp(("arbitrary", "arbitrary")),
    )(xin, mod3, g_pre, wperm)


def premix_bwd(xin, mod3, g_pre, wperm, dproj, dx_tail, name):
    nb, length, _ = xin.shape
    tn = min(TN, length)
    is_ctx = dx_tail is None

    def body(*refs):
        if is_ctx:
            x_ref, mod_ref, g_ref, w_ref, dp_ref, dmod_ref, dg_ref = refs
        else:
            x_ref, mod_ref, g_ref, w_ref, dp_ref, dxt_ref, dx_ref, dmod_ref, dg_ref = refs
        b, t = pl.program_id(0), pl.program_id(1)
        dh = lax.dot_general(dp_ref[...], w_ref[...], (((1,), (1,)), ((), ())), preferred_element_type=F32)
        _, vjp = jax.vjp(_rms_mod, x_ref[...], g_ref[...], mod_ref[1:2, :], mod_ref[0:1, :])
        dx, dg, dsc, dsh = vjp(dh)
        if not is_ctx:
            dx_ref[...] = dx + dxt_ref[...]

        @pl.when((t == 0) & ((b == 0) if is_ctx else True))
        def _():
            dmod_ref[...] = jnp.zeros_like(dmod_ref)

        @pl.when((t == 0) & (b == 0))
        def _():
            dg_ref[...] = jnp.zeros_like(dg_ref)

        dmod_ref[0:1, :] += dsh
        dmod_ref[1:2, :] += dsc
        dg_ref[0:1, :] += dg

    tok = lambda b, t: (b, t, 0)
    in_specs = [
        pl.BlockSpec((None, tn, D), tok),
        pl.BlockSpec((None, 6, D), (lambda b, t: (2, 0, 0)) if is_ctx else (lambda b, t: (b, 0, 0))),
        pl.BlockSpec((1, D), lambda b, t: (0, 0)),
        pl.BlockSpec((D, IN_W), lambda b, t: (0, 0), pipeline_mode=pl.Buffered(1)),
        pl.BlockSpec((None, tn, IN_W), tok),
    ]
    args = [xin, mod3, g_pre, wperm, dproj]
    out_specs = [
        pl.BlockSpec((None, 6, D), (lambda b, t: (0, 0, 0)) if is_ctx else (lambda b, t: (b, 0, 0))),
        pl.BlockSpec((8, D), lambda b, t: (0, 0)),
    ]
    out_shape = [jax.ShapeDtypeStruct((1 if is_ctx else nb, 6, D), F32), jax.ShapeDtypeStruct((8, D), F32)]
    if not is_ctx:
        in_specs.append(pl.BlockSpec((None, tn, D), tok))
        args.append(dx_tail)
        out_specs.insert(0, pl.BlockSpec((None, tn, D), tok))
        out_shape.insert(0, jax.ShapeDtypeStruct((nb, length, D), F32))
    return pl.pallas_call(
        body, name=name, grid=(nb, length // tn), in_specs=in_specs, out_specs=out_specs, out_shape=out_shape,
        compiler_params=_cp(("arbitrary", "arbitrary")),
    )(*args)


def _ret_specs(order):
    def im(f):
        return lambda *g: f(*order(*g))
    return dict(
        pret=pl.BlockSpec((None, SEQ, 512), im(lambda b, h: (b, 0, h))),
        pretc=pl.BlockSpec((None, LC, 512), im(lambda b, h: (b, 0, h))),
        rd=pl.BlockSpec((None, 2, 1), im(lambda b, h: (h, 0, 0))),
        gn=pl.BlockSpec((None, 1, RD), im(lambda b, h: (h, 0, 0))),
        tab=pl.BlockSpec((SEQ, RD), im(lambda b, h: (0, 0))),
        head=pl.BlockSpec((None, SEQ, RD), im(lambda b, h: (b, 0, h))),
    )


def retention_fwd(pret, pretc, rd, gn, cos, sin, hosted):
    nb = pret.shape[0]
    sp = _ret_specs(lambda b, h: (b, h))

    def body(*refs):
        own_in, h_in, own_out, h_out, own_scr, h_sems = hosted.split(refs, 6, 2)
        p_ref, pc_ref, rd_ref, gn_ref, cos_ref, sin_ref = own_in
        (o_ref, mix_ref), (q_s, k_s, of_s, ob_s) = own_out, own_scr
        grid_step = pl.program_id(0) * RH + pl.program_id(1)

        @pl.when(grid_step == 0)
        def _():
            hosted.start(h_in, h_out, h_sems)

        cos_v, sin_v = cos_ref[...], sin_ref[...]
        q_s[...] = _rope(p_ref[:, 0:128], cos_v, sin_v) * (RD ** -0.5)
        k_s[...] = _rope(p_ref[:, 128:256], cos_v, sin_v)
        lgs, init = [], []
        for rev in (False, True):
            lg = jax.nn.log_sigmoid(rd_ref[int(rev):int(rev) + 1, :])
            s = jnp.zeros((RD, RD), F32)
            for n in ((1, 0) if rev else (0, 1)):
                s = _ret_state(pc_ref[n * CH:(n + 1) * CH, 128:256], pc_ref[n * CH:(n + 1) * CH, 256:384], s, lg, rev)
            lgs.append(lg)
            init.append(s)

        def step(t, carry):
            out = []
            for rev, o_s, s in ((False, of_s, carry[0]), (True, ob_s, carry[1])):
                n = (NCH - 1 - t) if rev else t
                sl = pl.ds(pl.multiple_of(n * CH, CH), CH)
                o, s2 = _ret_chunk(q_s[sl, :], k_s[sl, :], p_ref[sl, 256:384], s, lgs[int(rev)], rev)
                o_s[sl, :] = o
                out.append(s2)
            return tuple(out)

        lax.fori_loop(0, NCH, step, tuple(init))
        o = of_s[...] + ob_s[...]
        o_ref[...] = o
        mix_ref[...] = _ln_gate(o, p_ref[:, 384:512], gn_ref[...]).astype(BF16)

        @pl.when(grid_step == nb * RH - 1)
        def _():
            hosted.finish(h_in, h_out, h_sems)

    h_in_specs, h_out_specs = hosted.specs()
    return pl.pallas_call(
        body, name="retention_fwd", grid=(nb, RH),
        in_specs=[sp["pret"], sp["pretc"], sp["rd"], sp["gn"], sp["tab"], sp["tab"]] + h_in_specs,
        out_specs=[sp["head"], sp["head"]] + h_out_specs,
        out_shape=[jax.ShapeDtypeStruct((nb, SEQ, RH * RD), F32), jax.ShapeDtypeStruct((nb, SEQ, D), BF16)]
        + hosted.out_shape,
        scratch_shapes=[pltpu.VMEM((SEQ, RD), F32)] * 4 + hosted.scratch,
        compiler_params=_cp(("arbitrary", "arbitrary")),
    )(pret, pretc, rd, gn, cos, sin, *hosted.args)


def retention_bwd(pret, pretc, o_all, dmixin, rd, gn, cos, sin, hosted):
    nb = pret.shape[0]
    sp = _ret_specs(lambda h, b: (b, h))

    def body(*refs):
        own_in, h_in, own_out, h_out, own_scr, h_sems = hosted.split(refs, 8, 4)
        p_ref, pc_ref, o_ref, dmix_ref, rd_ref, gn_ref, cos_ref, sin_ref = own_in
        dp_ref, dpc_ref, drd_ref, dgn_ref = own_out
        q_s, k_s, do_s, dqf_s, dkf_s, dvf_s, dqb_s, dkb_s, dvb_s, stf_s, stb_s = own_scr
        b = pl.program_id(1)
        grid_step = pl.program_id(0) * nb + b

        @pl.when(grid_step == 0)
        def _():
            hosted.start(h_in, h_out, h_sems)

        cos_v, sin_v = cos_ref[...], sin_ref[...]
        q_s[...] = _rope(p_ref[:, 0:128], cos_v, sin_v) * (RD ** -0.5)
        k_s[...] = _rope(p_ref[:, 128:256], cos_v, sin_v)
        _, gate_vjp = jax.vjp(_ln_gate, o_ref[...], p_ref[:, 384:512], gn_ref[...])
        do, dg, dgn = gate_vjp(dmix_ref[...].astype(F32))
        do_s[...] = do
        dp_ref[:, 384:512] = dg.astype(BF16)

        @pl.when(b == 0)
        def _():
            drd_ref[...] = jnp.zeros_like(drd_ref)
            dgn_ref[...] = jnp.zeros_like(dgn_ref)

        dgn_ref[...] += dgn
        kcs = [pc_ref[n * CH:(n + 1) * CH, 128:256] for n in (0, 1)]
        vcs = [pc_ref[n * CH:(n + 1) * CH, 256:384] for n in (0, 1)]
        dirs = []
        init = []
        for rev in (False, True):
            rdv = rd_ref[int(rev):int(rev) + 1, :]
            lg = jax.nn.log_sigmoid(rdv)
            order_c = (1, 0) if rev else (0, 1)
            s = jnp.zeros((RD, RD), F32)
            ctx_states = []
            for n in order_c:
                ctx_states.append(s)
                s = _ret_state(kcs[n], vcs[n], s, lg, rev)
            dirs.append((rev, order_c, lg, rdv, ctx_states))
            init.append(s)
        acc = ((dqf_s, dkf_s, dvf_s, stf_s), (dqb_s, dkb_s, dvb_s, stb_s))

        def fstep(t, carry):
            out = []
            for (rev, _, lg, _, _), (_, _, _, st_s), s in zip(dirs, acc, carry):
                n = (NCH - 1 - t) if rev else t
                sl = pl.ds(pl.multiple_of(n * CH, CH), CH)
                st_s[n] = s
                out.append(_ret_state(k_s[sl, :], p_ref[sl, 256:384], s, lg, rev))
            return tuple(out)

        lax.fori_loop(0, NCH, fstep, tuple(init))

        def bstep(t, carry):
            out = []
            for (rev, _, lg, _, _), (dq_s, dk_s, dv_s, st_s), (ds, dlg) in zip(dirs, acc, carry):
                n = t if rev else (NCH - 1 - t)
                sl = pl.ds(pl.multiple_of(n * CH, CH), CH)
                _, vjp = jax.vjp(functools.partial(_ret_chunk, reverse=rev),
                                 q_s[sl, :], k_s[sl, :], p_ref[sl, 256:384], st_s[n], lg)
                dq, dk, dv, ds_prev, dl = vjp((do_s[sl, :], ds))
                dq_s[sl, :] = dq
                dk_s[sl, :] = dk
                dv_s[sl, :] = dv
                out.append((ds_prev, dlg + dl))
            return tuple(out)

        zero_c = (jnp.zeros((RD, RD), F32), jnp.zeros((1, 1), F32))
        res = lax.fori_loop(0, NCH, bstep, (zero_c, zero_c))
        dkc = [None, None]
        dvc = [None, None]
        for (rev, order_c, lg, rdv, ctx_states), (ds, dlg) in zip(dirs, res):
            for idx in (1, 0):
                n = order_c[idx]
                _, vjp = jax.vjp(functools.partial(_ret_state, reverse=rev), kcs[n], vcs[n], ctx_states[idx], lg)
                dk_c, dv_c, ds, dl = vjp(ds)
                dlg = dlg + dl
                dkc[n] = dk_c if dkc[n] is None else dkc[n] + dk_c
                dvc[n] = dv_c if dvc[n] is None else dvc[n] + dv_c
            drd_ref[int(rev):int(rev) + 1, :] += dlg * jax.nn.sigmoid(-rdv)
        dp_ref[:, 0:128] = _rope_t((dqf_s[...] + dqb_s[...]) * (RD ** -0.5), cos_v, sin_v).astype(BF16)
        dp_ref[:, 128:256] = _rope_t(dkf_s[...] + dkb_s[...], cos_v, sin_v).astype(BF16)
        dp_ref[:, 256:384] = (dvf_s[...] + dvb_s[...]).astype(BF16)
        zero = jnp.zeros((CH, RD), BF16)
        for n in (0, 1):
            rows = slice(n * CH, (n + 1) * CH)
            dpc_ref[rows, 0:128] = zero
            dpc_ref[rows, 128:256] = dkc[n].astype(BF16)
            dpc_ref[rows, 256:384] = dvc[n].astype(BF16)
            dpc_ref[rows, 384:512] = zero

        @pl.when(grid_step == RH * nb - 1)
        def _():
            hosted.finish(h_in, h_out, h_sems)

    h_in_specs, h_out_specs = hosted.specs()
    return pl.pallas_call(
        body, name="retention_bwd", grid=(RH, nb),
        in_specs=[sp["pret"], sp["pretc"], sp["head"], sp["head"], sp["rd"], sp["gn"], sp["tab"], sp["tab"]]
        + h_in_specs,
        out_specs=[
            pl.BlockSpec((None, SEQ, 512), lambda h, b: (b, 0, h)),
            pl.BlockSpec((None, LC, 512), lambda h, b: (b, 0, h)),
            pl.BlockSpec((None, 2, 1), lambda h, b: (h, 0, 0)),
            pl.BlockSpec((None, 1, RD), lambda h, b: (h, 0, 0)),
        ] + h_out_specs,
        out_shape=[
            jax.ShapeDtypeStruct((nb, SEQ, IN_W), BF16),
            jax.ShapeDtypeStruct((nb, LC, IN_W), BF16),
            jax.ShapeDtypeStruct((RH, 2, 1), F32),
            jax.ShapeDtypeStruct((RH, 1, RD), F32),
        ] + hosted.out_shape,
        scratch_shapes=[pltpu.VMEM((SEQ, RD), F32)] * 9 + [pltpu.VMEM((NCH, RD, RD), F32)] * 2 + hosted.scratch,
        compiler_params=_cp(("arbitrary", "arbitrary")),
    )(pret, pretc, o_all, dmixin, rd, gn, cos, sin, *hosted.args)


def _rpb_flat(rpb):
    return jnp.pad(rpb, ((0, 0), (0, 1), (0, 33))).reshape(NPAIR, 2, 1, 1024)


def _rpb_flat_t(dflat):
    return dflat.reshape(8, 16, 64)[:, :15, :31]


def _barrel(x, left):
    row = lax.broadcasted_iota(jnp.int32, x.shape, 0)
    n = x.shape[1]
    for bit in range(6):
        s = 1 << bit
        x = jnp.where(((row >> bit) & 1) == 1, pltpu.roll(x, (n - s) if left else s, 1), x)
    return x


NA_TILE_ROWS, NA_BAND_ROWS = 4, 12
NA_Q, NA_K = NA_TILE_ROWS * GW, NA_BAND_ROWS * GW
NA_TILES = SEQ // NA_Q


def _tile_rows(cls):
    if cls == 0:
        return [(qr, 0) for qr in range(4)]
    if cls == 1:
        return [(4, qr) for qr in range(4)]
    return [(4, 4), (5, 4), (6, 4), (7, 4)]


def _na_tile(t):
    start = jnp.clip(4 * t - 4, 0, 32 - NA_BAND_ROWS)
    cls = jnp.where(t == 0, 0, jnp.where(t == NA_TILES - 1, 2, 1))
    return pl.ds(pl.multiple_of(t * NA_Q, NA_Q), NA_Q), pl.ds(pl.multiple_of(start * GW, NA_Q), NA_K), cls


def _na_probs(qst, kb, kc, bias):
    s_loc = _nt(qst, kb) * 0.125 + bias
    s_ctx = _nt(qst, kc) * 0.125
    m = jnp.maximum(jnp.max(s_loc, axis=1, keepdims=True), jnp.max(s_ctx, axis=1, keepdims=True))
    e_loc, e_ctx = jnp.exp(s_loc - m), jnp.exp(s_ctx - m)
    den = jnp.sum(e_loc, axis=1, keepdims=True) + jnp.sum(e_ctx, axis=1, keepdims=True)
    return e_loc / den, e_ctx / den


def _stack_heads(t):
    lane = lax.broadcasted_iota(jnp.int32, t.shape, 1)
    zero = jnp.zeros_like(t)
    return jnp.concatenate([jnp.where(lane < 64, t, zero), jnp.where(lane >= 64, t, zero)], axis=0)


def _unstack_heads(t):
    n = t.shape[0] // 2
    lane = lax.broadcasted_iota(jnp.int32, (n, 128), 1)
    return jnp.where(lane < 64, t[:n], t[n:])


def na_bias_table(flat):
    def body(flat_ref, out_ref):
        qc = lax.broadcasted_iota(jnp.int32, (GW, 512), 0)
        kc = lax.broadcasted_iota(jnp.int32, (GW, 512), 1) & 63
        start = jnp.clip(qc - 8, 0, GW - 16)
        window = (kc >= start) & (kc < start + 16)
        fill = jnp.full((GW, NA_K - 512), NEG, F32)
        for hh in (0, 1):
            skew = _barrel(pltpu.roll(jnp.broadcast_to(flat_ref[hh], (GW, 1024)), 1024 - 15, 1), left=False)
            by_class = [jnp.where(window, (skew if rc == 7 else pltpu.roll(skew, (9 + rc) * 64, 1))[:, 0:512], NEG)
                        for rc in range(8)]
            for cls in range(3):
                for qr, (rc, off) in enumerate(_tile_rows(cls)):
                    w = jnp.concatenate([by_class[rc], fill], axis=1)
                    rows = slice(hh * NA_Q + qr * GW, hh * NA_Q + (qr + 1) * GW)
                    out_ref[cls, rows, :] = pltpu.roll(w, off * GW, 1) if off else w

    return pl.pallas_call(
        body, name="na_bias_table", grid=(NPAIR,),
        in_specs=[pl.BlockSpec((None, 2, 1, 1024), lambda p: (p, 0, 0, 0))],
        out_specs=pl.BlockSpec((None, 3, 2 * NA_Q, NA_K), lambda p: (p, 0, 0, 0)),
        out_shape=jax.ShapeDtypeStruct((NPAIR, 3, 2 * NA_Q, NA_K), F32),
    )(flat)


def na_fwd(pna, pnac, bias, mixin, hosted):
    nb = pna.shape[0]

    def body(*refs):
        (p_ref, pc_ref, bias_ref, _), h_in, (out_ref,), h_out, _, h_sems = hosted.split(refs, 4, 1)
        grid_step = pl.program_id(0) * nb + pl.program_id(1)

        @pl.when(grid_step == 0)
        def _():
            hosted.start(h_in, h_out, h_sems)

        kc, vc = pc_ref[:, 128:256], pc_ref[:, 256:384]

        def tile(t, carry):
            qsl, bsl, cls = _na_tile(t)
            kb, vb = p_ref[bsl, 128:256], p_ref[bsl, 256:384]
            p_loc, p_ctx = _na_probs(_stack_heads(p_ref[qsl, 0:128]), kb, kc, bias_ref[cls])
            out_ref[qsl, :] = _unstack_heads(_nn(p_loc, vb) + _nn(p_ctx, vc)).astype(BF16)
            return carry

        lax.fori_loop(0, NA_TILES, tile, 0, unroll=2)

        @pl.when(grid_step == NPAIR * nb - 1)
        def _():
            hosted.finish(h_in, h_out, h_sems)

    h_in_specs, h_out_specs = hosted.specs()
    return pl.pallas_call(
        body, name="na_fwd", grid=(NPAIR, nb),
        in_specs=[
            pl.BlockSpec((None, SEQ, 384), lambda p, b: (b, 0, p)),
            pl.BlockSpec((None, LC, 384), lambda p, b: (b, 0, p)),
            pl.BlockSpec((None, 3, 2 * NA_Q, NA_K), lambda p, b: (p, 0, 0, 0)),
            pl.BlockSpec(memory_space=pl.ANY),
        ] + h_in_specs,
        out_specs=[pl.BlockSpec((None, SEQ, 128), lambda p, b: (b, 0, 4 + p))] + h_out_specs,
        out_shape=[jax.ShapeDtypeStruct((nb, SEQ, D), BF16)] + hosted.out_shape,
        input_output_aliases={3: 0},
        scratch_shapes=hosted.scratch,
        compiler_params=_cp(("arbitrary", "arbitrary")),
    )(pna, pnac, bias, mixin, *hosted.args)


def na_bwd(pna, pnac, bias, dmixin, dproj, dprojc, hosted):
    nb = pna.shape[0]

    def body(*refs):
        own_in, h_in, own_out, h_out, own_scr, h_sems = hosted.split(refs, 6, 3)
        p_ref, pc_ref, bias_ref, dmix_ref = own_in[:4]
        dp_ref, dpc_ref, dpat_ref = own_out
        dbias_s, dk_s, dv_s, dkc_s, dvc_s, res_s, resc_s = own_scr
        b, part = pl.program_id(1), pl.program_id(2)
        grid_step = (pl.program_id(0) * nb + b) * 3 + part

        @pl.when(grid_step == 0)
        def _():
            hosted.start(h_in, h_out, h_sems)

        @pl.when(grid_step == NPAIR * nb * 3 - 1)
        def _():
            hosted.finish(h_in, h_out, h_sems)

        @pl.when(part == 0)
        def _():
            @pl.when(b == 0)
            def _():
                dbias_s[...] = jnp.zeros_like(dbias_s)

            dk_s[...] = jnp.zeros_like(dk_s)
            dv_s[...] = jnp.zeros_like(dv_s)
            dkc_s[...] = jnp.zeros_like(dkc_s)
            dvc_s[...] = jnp.zeros_like(dvc_s)
            kc, vc = pc_ref[:, 128:256], pc_ref[:, 256:384]

            def tile(t, carry):
                qsl, bsl, cls = _na_tile(t)
                kb, vb = p_ref[bsl, 128:256], p_ref[bsl, 256:384]
                qst, dost = _stack_heads(p_ref[qsl, 0:128]), _stack_heads(dmix_ref[qsl, :])
                p_loc, p_ctx = _na_probs(qst, kb, kc, bias_ref[cls])
                dp_loc, dp_ctx = _nt(dost, vb), _nt(dost, vc)
                delta = (jnp.sum(p_loc * dp_loc, axis=1, keepdims=True)
                         + jnp.sum(p_ctx * dp_ctx, axis=1, keepdims=True))
                ds_loc, ds_ctx = p_loc * (dp_loc - delta), p_ctx * (dp_ctx - delta)
                dbias_s[cls] += ds_loc
                res_s[0, qsl, :] = _unstack_heads((_nn(ds_loc, kb) + _nn(ds_ctx, kc)) * 0.125).astype(BF16)
                dk_s[bsl, :] += _tn(ds_loc, qst) * 0.125
                dv_s[bsl, :] += _tn(p_loc, dost)
                dkc_s[...] += _tn(ds_ctx, qst) * 0.125
                dvc_s[...] += _tn(p_ctx, dost)
                return carry

            lax.fori_loop(0, NA_TILES, tile, 0, unroll=2)
            res_s[1] = dk_s[...].astype(BF16)
            res_s[2] = dv_s[...].astype(BF16)
            resc_s[0] = jnp.zeros((LC, 128), BF16)
            resc_s[1] = dkc_s[...].astype(BF16)
            resc_s[2] = dvc_s[...].astype(BF16)

            @pl.when(b == nb - 1)
            def _():
                for hh in (0, 1):
                    by_class = [None] * 8
                    for cls in range(3):
                        for qr, (rc, off) in enumerate(_tile_rows(cls)):
                            w = dbias_s[cls, hh * NA_Q + qr * GW:hh * NA_Q + (qr + 1) * GW, :]
                            w = (pltpu.roll(w, NA_K - off * GW, 1) if off else w)[:, 0:512]
                            by_class[rc] = w if by_class[rc] is None else by_class[rc] + w
                    skew = jnp.zeros((GW, 1024), F32)
                    for rc in range(8):
                        w = jnp.concatenate([by_class[rc], jnp.zeros((GW, 512), F32)], axis=1)
                        skew = skew + (w if rc == 7 else pltpu.roll(w, (7 - rc) * 64, 1))
                    dpat_ref[hh] = jnp.sum(pltpu.roll(_barrel(skew, left=True), 15, 1), axis=0, keepdims=True)

        dp_ref[...] = res_s[part]
        dpc_ref[...] = resc_s[part]

    h_in_specs, h_out_specs = hosted.specs()
    return pl.pallas_call(
        body, name="na_bwd", grid=(NPAIR, nb, 3),
        in_specs=[
            pl.BlockSpec((None, SEQ, 384), lambda p, b, s: (b, 0, p)),
            pl.BlockSpec((None, LC, 384), lambda p, b, s: (b, 0, p)),
            pl.BlockSpec((None, 3, 2 * NA_Q, NA_K), lambda p, b, s: (p, 0, 0, 0)),
            pl.BlockSpec((None, SEQ, 128), lambda p, b, s: (b, 0, 4 + p)),
            pl.BlockSpec(memory_space=pl.ANY),
            pl.BlockSpec(memory_space=pl.ANY),
        ] + h_in_specs,
        out_specs=[
            pl.BlockSpec((None, SEQ, 128), lambda p, b, s: (b, 0, 16 + 3 * p + s)),
            pl.BlockSpec((None, LC, 128), lambda p, b, s: (b, 0, 16 + 3 * p + s)),
            pl.BlockSpec((None, 2, 1, 1024), lambda p, b, s: (p, 0, 0, 0)),
        ] + h_out_specs,
        out_shape=[
            jax.ShapeDtypeStruct((nb, SEQ, IN_W), BF16),
            jax.ShapeDtypeStruct((nb, LC, IN_W), BF16),
            jax.ShapeDtypeStruct((NPAIR, 2, 1, 1024), F32),
        ] + hosted.out_shape,
        input_output_aliases={4: 0, 5: 1},
        scratch_shapes=[
            pltpu.VMEM((3, 2 * NA_Q, NA_K), F32),
            pltpu.VMEM((SEQ, 128), F32), pltpu.VMEM((SEQ, 128), F32),
            pltpu.VMEM((LC, 128), F32), pltpu.VMEM((LC, 128), F32),
            pltpu.VMEM((3, SEQ, 128), BF16), pltpu.VMEM((3, LC, 128), BF16),
        ] + hosted.scratch,
        compiler_params=_cp(("arbitrary", "arbitrary", "arbitrary")),
    )(pna, pnac, bias, dmixin, dproj, dprojc, *hosted.args)


def tail_fwd_bwd(x, mixin, tgt, mod3, g_post_mix, g_pre_mlp, g_post_mlp, wout, w1, w2):
    nb = x.shape[0]

    def body(x_ref, mi_ref, tgt_ref, mod_ref, gpm_ref, gpl_ref, gpo_ref, wo_ref, w1_ref, w2_ref,
             dx_ref, dmix_ref, h2_ref, du_ref, a_ref, dm_ref, dmi_ref, dmod_ref, dg_ref, loss_ref):
        b, t = pl.program_id(0), pl.program_id(1)
        gt1, sh2, sc2, gt2 = mod_ref[2:3, :], mod_ref[3:4, :], mod_ref[4:5, :], mod_ref[5:6, :]
        mix = jnp.dot(mi_ref[...], wo_ref[...], preferred_element_type=F32)
        (x1, h2), vjp_a = jax.vjp(_post_mix, x_ref[...], mix, gt1, sc2, sh2, gpm_ref[...], gpl_ref[...])
        h2b = h2.astype(BF16)
        h2_ref[...] = h2b
        m = jnp.zeros((TN, D), F32)
        relus = []
        for j in range(4):
            cols = slice(j * D, (j + 1) * D)
            r = jnp.maximum(jnp.dot(h2b, w1_ref[j], preferred_element_type=F32), 0.0)
            ab = (r * r).astype(BF16)
            a_ref[:, cols] = ab
            m = m + jnp.dot(ab, w2_ref[cols, :], preferred_element_type=F32)
            relus.append(r)
        loss, vjp_b = jax.vjp(_head_loss, x1, m, gt2, gpo_ref[...], tgt_ref[...])
        dx1, dm, dgt2, dgpo, _ = vjp_b(jnp.ones((1, 1), F32))
        dmb = dm.astype(BF16)
        dm_ref[...] = dmb
        dh2 = jnp.zeros((TN, D), F32)
        for j in range(4):
            cols = slice(j * D, (j + 1) * D)
            da = lax.dot_general(dmb, w2_ref[cols, :], (((1,), (1,)), ((), ())), preferred_element_type=F32)
            dub = (da * (2.0 * relus[j])).astype(BF16)
            du_ref[:, cols] = dub
            dh2 = dh2 + lax.dot_general(dub, w1_ref[j], (((1,), (1,)), ((), ())), preferred_element_type=F32)
        dx, dmix, dgt1, dsc2, dsh2, dgpm, dgpl = vjp_a((dx1, dh2))
        dx_ref[...] = dx
        dmixb = dmix.astype(BF16)
        dmix_ref[...] = dmixb
        dmi_ref[...] = lax.dot_general(dmixb, wo_ref[...], (((1,), (1,)), ((), ())),
                                       preferred_element_type=F32).astype(BF16)

        @pl.when(t == 0)
        def _():
            dmod_ref[...] = jnp.zeros_like(dmod_ref)

        @pl.when((t == 0) & (b == 0))
        def _():
            dg_ref[...] = jnp.zeros_like(dg_ref)
            loss_ref[...] = jnp.zeros_like(loss_ref)

        dmod_ref[2:3, :] += dgt1
        dmod_ref[3:4, :] += dsh2
        dmod_ref[4:5, :] += dsc2
        dmod_ref[5:6, :] += dgt2
        dg_ref[0:1, :] += dgpm
        dg_ref[1:2, :] += dgpl
        dg_ref[2:3, :] += dgpo
        loss_ref[...] += jnp.broadcast_to(loss, loss_ref.shape)

    tok = lambda b, t: (b, t, 0)
    const = lambda b, t: (0, 0)
    vec = pl.BlockSpec((1, D), const)
    return pl.pallas_call(
        body, name="tail_fwd_bwd", grid=(nb, SEQ // TN),
        in_specs=[
            pl.BlockSpec((None, TN, D), tok), pl.BlockSpec((None, TN, D), tok), pl.BlockSpec((None, TN, D), tok),
            pl.BlockSpec((None, 6, D), lambda b, t: (b, 0, 0)), vec, vec, vec,
            pl.BlockSpec((D, D), const, pipeline_mode=pl.Buffered(1)),
            pl.BlockSpec((4, D, D), lambda b, t: (0, 0, 0), pipeline_mode=pl.Buffered(1)),
            pl.BlockSpec((DFF, D), const, pipeline_mode=pl.Buffered(1)),
        ],
        out_specs=[
            pl.BlockSpec((None, TN, D), tok), pl.BlockSpec((None, TN, D), tok), pl.BlockSpec((None, TN, D), tok),
            pl.BlockSpec((None, TN, DFF), tok), pl.BlockSpec((None, TN, DFF), tok), pl.BlockSpec((None, TN, D), tok),
            pl.BlockSpec((None, TN, D), tok),
            pl.BlockSpec((None, 6, D), lambda b, t: (b, 0, 0)),
            pl.BlockSpec((8, D), const), pl.BlockSpec((8, 128), const),
        ],
        out_shape=[
            jax.ShapeDtypeStruct((nb, SEQ, D), F32), jax.ShapeDtypeStruct((nb, SEQ, D), BF16),
            jax.ShapeDtypeStruct((nb, SEQ, D), BF16), jax.ShapeDtypeStruct((nb, SEQ, DFF), BF16),
            jax.ShapeDtypeStruct((nb, SEQ, DFF), BF16), jax.ShapeDtypeStruct((nb, SEQ, D), BF16),
            jax.ShapeDtypeStruct((nb, SEQ, D), BF16),
            jax.ShapeDtypeStruct((nb, 6, D), F32), jax.ShapeDtypeStruct((8, D), F32),
            jax.ShapeDtypeStruct((8, 128), F32),
        ],
        compiler_params=_cp(("arbitrary", "arbitrary")),
    )(x, mixin, tgt, mod3, g_post_mix, g_pre_mlp, g_post_mlp, wout, w1, w2)


def weight_grad(pairs, name, out_dtype=F32, col_blocks=False, tm=1024, tn=1024, tk=2048):
    m, n = pairs[0][0].shape[1], pairs[0][1].shape[1]
    tn = min(tn, n)
    tks = [min(tk, xa.shape[0]) for xa, _ in pairs]
    steps = [xa.shape[0] // t for (xa, _), t in zip(pairs, tks)]
    total = sum(steps)
    offs = [sum(steps[:i]) for i in range(len(pairs))]

    def body(*refs):
        out_ref, acc = refs[2 * len(pairs)], refs[-1]
        k = pl.program_id(2)

        @pl.when(k == 0)
        def _():
            acc[...] = jnp.zeros_like(acc)

        for i in range(len(pairs)):
            @pl.when((k >= offs[i]) & (k < offs[i] + steps[i]))
            def _(i=i):
                acc[...] += lax.dot_general(refs[2 * i][...], refs[2 * i + 1][...], (((0,), (0,)), ((), ())),
                                            preferred_element_type=F32)

        if out_dtype != F32:
            @pl.when(k == total - 1)
            def _():
                out_ref[...] = acc[...].astype(out_dtype)

    in_specs, args = [], []
    for i, (xa, ya) in enumerate(pairs):
        clamp = lambda k, i=i: jnp.clip(k - offs[i], 0, steps[i] - 1)
        in_specs.append(pl.BlockSpec((tks[i], tm), lambda a, c, k, clamp=clamp: (clamp(k), a)))
        in_specs.append(pl.BlockSpec((tks[i], tn), lambda a, c, k, clamp=clamp: (clamp(k), c)))
        args += [xa, ya]
    if col_blocks:
        out_spec = pl.BlockSpec((None, tm, tn), lambda a, c, k: (c, a, 0))
        out_shape = jax.ShapeDtypeStruct((n // tn, m, tn), out_dtype)
    else:
        out_spec = pl.BlockSpec((tm, tn), lambda a, c, k: (a, c))
        out_shape = jax.ShapeDtypeStruct((m, n), out_dtype)
    return pl.pallas_call(
        body, name=name, grid=(m // tm, n // tn, total), in_specs=in_specs, out_specs=out_spec, out_shape=out_shape,
        scratch_shapes=[] if out_dtype == F32 else [pltpu.VMEM((tm, tn), F32)],
        compiler_params=_cp(("arbitrary", "arbitrary", "arbitrary")),
    )(*args)


def _perm_block(t):
    return 4 * (t % 4) + t // 4 if t < 16 else 16 + 3 * ((t - 16) % 4) + (t - 16) // 4


def unpack_w_in(blocks):
    def body(i_ref, o_ref):
        for t in range(28):
            p = _perm_block(t)
            o_ref[:, p * 128:(p + 1) * 128] = i_ref[t // 7, :, (t % 7) * 128:(t % 7 + 1) * 128]

    return pl.pallas_call(
        body, name="unpack_w_in", grid=(2,),
        in_specs=[pl.BlockSpec((4, D // 2, 896), lambda i: (0, i, 0))],
        out_specs=pl.BlockSpec((D // 2, IN_W), lambda i: (i, 0)),
        out_shape=jax.ShapeDtypeStruct((D, IN_W), BF16),
    )(blocks)


def pack_w_in(dw):
    def body(i_ref, o_ref):
        for t in range(28):
            p = _perm_block(t)
            o_ref[t // 7, :, (t % 7) * 128:(t % 7 + 1) * 128] = i_ref[:, p * 128:(p + 1) * 128].astype(BF16)

    return pl.pallas_call(
        body, name="pack_w_in", grid=(4,),
        in_specs=[pl.BlockSpec((D // 4, IN_W), lambda i: (i, 0))],
        out_specs=pl.BlockSpec((4, D // 4, 896), lambda i: (0, i, 0)),
        out_shape=jax.ShapeDtypeStruct((4, D, 896), BF16),
    )(dw)


def _place():
    return lax.axis_index("x"), lax.axis_index("y"), lax.axis_index("c")


class Hosted:
    def __init__(self, args, out_shape, scratch, start, finish):
        self.args, self.out_shape, self.scratch, self.start, self.finish = args, out_shape, scratch, start, finish

    def specs(self):
        hbm = pl.BlockSpec(memory_space=pl.ANY)
        return [hbm] * len(self.args), [hbm] * len(self.out_shape)

    def split(self, refs, n_in, n_out):
        a, b = len(self.args), len(self.out_shape)
        cuts = [n_in, n_in + a, n_in + a + n_out, n_in + a + n_out + b, len(refs) - len(self.scratch)]
        parts = [refs[i:j] for i, j in zip([0] + cuts, cuts + [len(refs)])]
        return parts[0], parts[1], parts[2], parts[3], parts[4], parts[5]


def run_hosted(hosted, name):
    def body(*refs):
        _, ins, _, outs, _, sems = hosted.split(refs, 0, 0)
        hosted.start(ins, outs, sems)
        hosted.finish(ins, outs, sems)

    in_specs, out_specs = hosted.specs()
    return pl.pallas_call(body, name=name, in_specs=in_specs, out_specs=out_specs, out_shape=hosted.out_shape,
                          scratch_shapes=hosted.scratch)(*hosted.args)


def gather8(blocks):
    na = len(blocks)

    def copies(ins, outs, sems):
        send_sems, recv_sems, local_sem = sems
        x, y, c = _place()
        me, sibling = (x, y, c), (x, y, 1 - c)
        chips = [(1 - x, y), (x, 1 - y), (1 - x, 1 - y)]

        def slot(o_ref, px, py, pc):
            return o_ref.at[4 * px + 2 * py + pc]

        def copy(a, k, block, to, src=None):
            return pltpu.make_async_remote_copy(
                src_ref=slot(outs[a], *block) if src is None else src, dst_ref=slot(outs[a], *block),
                send_sem=send_sems.at[a, k], recv_sem=recv_sems.at[a, k], device_id=to, device_id_type=MESH)

        mine = [pltpu.make_async_copy(ins[a], slot(outs[a], *me), local_sem.at[a]) for a in range(na)]
        first = []
        for a in range(na):
            first.append(copy(a, 0, me, sibling, src=ins[a]))
            first += [copy(a, 1 + j, me, (*chip, c), src=ins[a]) for j, chip in enumerate(chips)]
        return copy, mine, first, me, sibling, chips, c

    def start(ins, outs, sems):
        _, mine, first, *_ = copies(ins, outs, sems)
        for cp in mine + first:
            cp.start()

    def finish(ins, outs, sems):
        copy, mine, first, me, sibling, chips, c = copies(ins, outs, sems)
        passed = []
        for j, chip in enumerate(chips):
            for a in range(na):
                copy(a, 1 + j, (*chip, c), me).wait_recv()
                cp = copy(a, 4 + j, (*chip, c), sibling)
                cp.start()
                passed.append(cp)
        for a in range(na):
            copy(a, 0, sibling, me).wait_recv()
            for j, chip in enumerate(chips):
                copy(a, 4 + j, (*chip, 1 - c), me).wait_recv()
        for cp in first + passed:
            cp.wait_send()
        for cp in mine:
            cp.wait()

    return Hosted(list(blocks), [jax.ShapeDtypeStruct((8,) + b.shape, b.dtype) for b in blocks],
                  [pltpu.SemaphoreType.DMA((na, 7)), pltpu.SemaphoreType.DMA((na, 7)), pltpu.SemaphoreType.DMA((na,))],
                  start, finish)


def all_gather8(blocks, name):
    return run_hosted(gather8(blocks), name)


def chips3(arrays):
    na = len(arrays)

    def copies(ins, outs, sems):
        send_sems, recv_sems = sems
        x, y, c = _place()
        return [pltpu.make_async_remote_copy(
            src_ref=ins[a].at[2 * px + py], dst_ref=outs[a].at[k], send_sem=send_sems.at[a, k],
            recv_sem=recv_sems.at[a, k], device_id=(px, py, c), device_id_type=MESH)
            for a in range(na) for k, (px, py) in enumerate([(1 - x, y), (x, 1 - y), (1 - x, 1 - y)])]

    def start(ins, outs, sems):
        for cp in copies(ins, outs, sems):
            cp.start()

    def finish(ins, outs, sems):
        for cp in copies(ins, outs, sems):
            cp.wait()

    return Hosted(list(arrays), [jax.ShapeDtypeStruct((3,) + a.shape[1:], a.dtype) for a in arrays],
                  [pltpu.SemaphoreType.DMA((na, 3)), pltpu.SemaphoreType.DMA((na, 3))], start, finish)


def sibling_exchange(arrays, name):
    na = len(arrays)

    def body(*refs):
        ins, outs = refs[:na], refs[na:2 * na]
        send_sems, recv_sems = refs[2 * na:]
        x, y, c = _place()
        cps = [pltpu.make_async_remote_copy(
            src_ref=ins[a], dst_ref=outs[a], send_sem=send_sems.at[a], recv_sem=recv_sems.at[a],
            device_id=(x, y, 1 - c), device_id_type=MESH) for a in range(na)]
        for cp in cps:
            cp.start()
        for cp in cps:
            cp.wait()

    hbm = pl.BlockSpec(memory_space=pl.ANY)
    return pl.pallas_call(
        body, name=name, in_specs=[hbm] * na, out_specs=[hbm] * na,
        out_shape=[jax.ShapeDtypeStruct(a.shape, a.dtype) for a in arrays],
        scratch_shapes=[pltpu.SemaphoreType.DMA((na,)), pltpu.SemaphoreType.DMA((na,))],
    )(*arrays)


def siblings4(arrays):
    na = len(arrays)

    def copies(ins, outs, sems):
        send_sems, recv_sems = sems
        x, y, c = _place()
        return [pltpu.make_async_remote_copy(
            src_ref=ins[a].at[2 * j + 1 - c], dst_ref=outs[a].at[j],
            send_sem=send_sems.at[a, j], recv_sem=recv_sems.at[a, j],
            device_id=(x, y, 1 - c), device_id_type=MESH) for a in range(na) for j in range(4)]

    def start(ins, outs, sems):
        for cp in copies(ins, outs, sems):
            cp.start()

    def finish(ins, outs, sems):
        for cp in copies(ins, outs, sems):
            cp.wait()

    return Hosted(list(arrays), [jax.ShapeDtypeStruct((4,) + a.shape[1:], a.dtype) for a in arrays],
                  [pltpu.SemaphoreType.DMA((na, 4)), pltpu.SemaphoreType.DMA((na, 4))], start, finish)


def sibling_blocks(arrays, name):
    return run_hosted(siblings4(arrays), name)


def _row_tile(r):
    for cand in (512, 256, 128, 64, 32, 16, 8):
        if r % cand == 0:
            return cand
    return r


def chip_partial(place, g8, landed4, name):
    _, r, ccols = g8.shape
    tr = _row_tile(r)

    def body(place_ref, g_ref, l_ref, o_ref):
        del place_ref
        o_ref[...] = (g_ref[...].astype(F32) + l_ref[...].astype(F32)).astype(BF16)

    spec = pl.BlockSpec((None, tr, ccols), lambda j, i, s: (j, i, 0))
    return pl.pallas_call(
        body, name=name,
        grid_spec=pltpu.PrefetchScalarGridSpec(
            num_scalar_prefetch=1, grid=(4, r // tr),
            in_specs=[pl.BlockSpec((None, tr, ccols), lambda j, i, s: (2 * j + s[0], i, 0)), spec], out_specs=spec),
        out_shape=jax.ShapeDtypeStruct((4, r, ccols), BF16),
    )(place, g8, landed4)


def shard_sum(place, partial4, landed3, name):
    _, r, ccols = partial4.shape
    tr = _row_tile(r)

    def body(place_ref, p_ref, l_ref, o_ref):
        del place_ref
        acc = p_ref[...].astype(F32)
        for k in range(3):
            acc = acc + l_ref[k].astype(F32)
        o_ref[...] = acc

    return pl.pallas_call(
        body, name=name,
        grid_spec=pltpu.PrefetchScalarGridSpec(
            num_scalar_prefetch=1, grid=(r // tr,),
            in_specs=[pl.BlockSpec((None, tr, ccols), lambda i, s: (s[1], i, 0)),
                      pl.BlockSpec((3, tr, ccols), lambda i, s: (0, i, 0))],
            out_specs=pl.BlockSpec((tr, ccols), lambda i, s: (i, 0))),
        out_shape=jax.ShapeDtypeStruct((r, ccols), F32),
    )(place, partial4, landed3)


def _adamw_math(w, g, m, v):
    m2 = B1 * m + (1.0 - B1) * g
    v2 = B2 * v + (1.0 - B2) * (g * g)
    m_hat = m2 / (1.0 - B1 ** STEP)
    v_hat = v2 / (1.0 - B2 ** STEP)
    return -LR * (m_hat / (jnp.sqrt(v_hat) + AEPS) + WD * w), m2, v2


def adamw_halves(place, w, mine, theirs, m, v, name):
    r, ccols = w.shape
    hr = r // 2
    tr = _row_tile(hr)
    nt = hr // tr

    def body(place_ref, w_ref, a_ref, b_ref, m_ref, v_ref, g_out, d_out, m_out, v_out):
        g = jnp.where(pl.program_id(0) == place_ref[0], a_ref[...], b_ref[...])
        d, m2, v2 = _adamw_math(w_ref[...], g, m_ref[...], v_ref[...])
        g_out[...] = g
        d_out[...] = d
        m_out[...] = m2
        v_out[...] = v2

    full = pl.BlockSpec((tr, ccols), lambda h, i, s: (h * nt + i, 0))
    part = pl.BlockSpec((tr, ccols), lambda h, i, s: (i, 0))
    return pl.pallas_call(
        body, name=name,
        grid_spec=pltpu.PrefetchScalarGridSpec(
            num_scalar_prefetch=1, grid=(2, nt), in_specs=[full, part, part, full, full], out_specs=[full] * 4),
        out_shape=[jax.ShapeDtypeStruct((r, ccols), F32)] * 4,
    )(place, w, mine, theirs, m, v)


def adamw_group(place, halved, plain, hosted, name):
    rows = halved[0][0].shape[0]
    tr = 64
    nt = rows // 2 // tr
    nh, npl = len(halved), len(plain)

    def body(place_ref, *refs):
        own_in, h_in, own_out, h_out, _, h_sems = hosted.split(refs, 5 * nh + 4 * npl, 4 * nh + 3 * npl)
        half = pl.program_id(0)
        grid_step = half * nt + pl.program_id(1)

        @pl.when(grid_step == 0)
        def _():
            hosted.start(h_in, h_out, h_sems)

        for i in range(nh):
            w_ref, a_ref, b_ref, m_ref, v_ref = own_in[5 * i:5 * i + 5]
            g = jnp.where(half == place_ref[0], a_ref[...], b_ref[...])
            res = (g,) + _adamw_math(w_ref[...], g, m_ref[...], v_ref[...])
            for o_ref, r in zip(own_out[4 * i:4 * i + 4], res):
                o_ref[...] = r
        for i in range(npl):
            w_ref, g_ref, m_ref, v_ref = own_in[5 * nh + 4 * i:5 * nh + 4 * i + 4]
            res = _adamw_math(w_ref[...], g_ref[...], m_ref[...], v_ref[...])
            for o_ref, r in zip(own_out[4 * nh + 3 * i:4 * nh + 3 * i + 3], res):
                o_ref[...] = r

        @pl.when(grid_step == 2 * nt - 1)
        def _():
            hosted.finish(h_in, h_out, h_sems)

    def full(cols):
        return pl.BlockSpec((tr, cols), lambda h, i, s: (h * nt + i, 0))

    def part(cols):
        return pl.BlockSpec((tr, cols), lambda h, i, s: (i, 0))

    in_specs, out_specs, out_shape, args = [], [], [], []
    for w, a, b, m, v in halved:
        cols = w.shape[1]
        in_specs += [full(cols), part(cols), part(cols), full(cols), full(cols)]
        out_specs += [full(cols)] * 4
        out_shape += [jax.ShapeDtypeStruct(w.shape, F32)] * 4
        args += [w, a, b, m, v]
    for w, g, m, v in plain:
        cols = w.shape[1]
        in_specs += [full(cols)] * 4
        out_specs += [full(cols)] * 3
        out_shape += [jax.ShapeDtypeStruct(w.shape, F32)] * 3
        args += [w, g, m, v]
    h_in_specs, h_out_specs = hosted.specs()
    return pl.pallas_call(
        body, name=name,
        grid_spec=pltpu.PrefetchScalarGridSpec(
            num_scalar_prefetch=1, grid=(2, nt), in_specs=in_specs + h_in_specs, out_specs=out_specs + h_out_specs,
            scratch_shapes=hosted.scratch),
        out_shape=out_shape + hosted.out_shape,
        compiler_params=_cp(("arbitrary", "arbitrary")),
    )(place, *args, *hosted.args)


def _silu(x):
    return x * jax.nn.sigmoid(x)


def mod_shard(cin, w_ada, b_shard):
    def body(c_ref, w_ref, b_ref, o_ref):
        o_ref[...] = _nn(_silu(c_ref[...]), w_ref[...]) + b_ref[...]

    return pl.pallas_call(
        body, name="mod_shard", grid=(3,),
        in_specs=[pl.BlockSpec((32, D), lambda j: (0, 0)), pl.BlockSpec((D, 512), lambda j: (0, j)),
                  pl.BlockSpec((1, 512), lambda j: (0, j))],
        out_specs=pl.BlockSpec((32, 512), lambda j: (0, j)),
        out_shape=jax.ShapeDtypeStruct((32, 1536), F32),
    )(cin, w_ada, b_shard)


def ada_grads(cin, gb, gc, w_ada):
    def body(c_ref, gb_ref, gc_ref, w_ref, gw_ref, pc_ref):
        ctx_tot = jnp.sum(gc_ref[...], axis=0, keepdims=True)
        rows = lax.broadcasted_iota(jnp.int32, (16, 512), 0)
        dm = jnp.concatenate([gb_ref[...], jnp.where(rows == 0, ctx_tot, 0.0)], axis=0)
        gw_ref[...] = _tn(_silu(c_ref[...]), dm)
        rows8 = lax.broadcasted_iota(jnp.int32, (8, 512), 0)
        part = _nt(jnp.where(rows8 == 0, ctx_tot, 0.0), w_ref[...])

        @pl.when(pl.program_id(0) == 0)
        def _():
            pc_ref[...] = jnp.zeros_like(pc_ref)

        pc_ref[...] += part

    return pl.pallas_call(
        body, name="ada_grads", grid=(3,),
        in_specs=[pl.BlockSpec((32, D), lambda j: (0, 0)), pl.BlockSpec((16, 512), lambda j: (0, j)),
                  pl.BlockSpec((8, 512), lambda j: (0, j)), pl.BlockSpec((D, 512), lambda j: (0, j))],
        out_specs=[pl.BlockSpec((D, 512), lambda j: (0, j)), pl.BlockSpec((8, D), lambda j: (0, 0))],
        out_shape=[jax.ShapeDtypeStruct((D, 1536), F32), jax.ShapeDtypeStruct((8, D), F32)],
    )(cin, gb, gc, w_ada)


SMALL_SUM_ROWS = 15


def small_update(gsm, gbf, gcf, pcg, params):
    n = len(params)

    def body(*refs):
        gsm_ref, gbf_ref, gcf_ref, pcg_ref = refs[:4]
        wmv, outs, loss_out = refs[4:4 + 3 * n], refs[4 + 3 * n:4 + 7 * n], refs[-1]
        acc = gsm_ref[0]
        for dev in range(1, 8):
            acc = acc + gsm_ref[dev]
        c_ctx = wmv[0][...]
        sg = jax.nn.sigmoid(c_ctx)
        dsilu = pcg_ref[0:1, :] + pcg_ref[2:3, :] + pcg_ref[4:5, :] + pcg_ref[6:7, :]
        lane = lax.broadcasted_iota(jnp.int32, (1, D), 1)
        last = acc[14:15, :]
        grads = [
            dsilu * (sg * (1.0 + c_ctx * (1.0 - sg))),
            jnp.sum(gbf_ref[...], axis=0, keepdims=True) + jnp.sum(gcf_ref[...], axis=0, keepdims=True),
            acc[0:1, :] + acc[1:2, :], acc[2:3, :], acc[3:4, :], acc[4:5, :],
            acc[5:6, 0:512], acc[6:14, :], jnp.where(lane < 8, last, 0.0),
        ]
        loss_out[...] = jnp.broadcast_to(jnp.sum(jnp.where(lane == 8, last, 0.0), axis=1, keepdims=True), (8, 128))
        for i, g in enumerate(grads):
            d, m2, v2 = _adamw_math(wmv[3 * i][...], g, wmv[3 * i + 1][...], wmv[3 * i + 2][...])
            outs[4 * i][...] = g
            outs[4 * i + 1][...] = d
            outs[4 * i + 2][...] = m2
            outs[4 * i + 3][...] = v2

    flat = [a for wmv in params for a in wmv]
    out_shape = [jax.ShapeDtypeStruct(w.shape, F32) for w, _, _ in params for _ in range(4)]
    return pl.pallas_call(
        body, name="small_update", out_shape=out_shape + [jax.ShapeDtypeStruct((8, 128), F32)],
    )(gsm, gbf, gcf, pcg, *flat)


def _pad_row(v, rows):
    flat = v.reshape(-1)
    return jnp.pad(flat, (0, rows * D - flat.shape[0])).reshape(rows, D)


def local_step(x, ctx, tgt, mod3, g_pre_mix, g_post_mix, g_pre_mlp, g_post_mlp, ret_decay, ret_gn, na_rpb,
               wperm, late_weights, early_grads):
    nb = x.shape[0]
    tokens = nb * SEQ
    cos, sin = _rope_tables()
    rd = ret_decay.T.reshape(RH, 2, 1)
    gn = ret_gn.reshape(RH, 1, RD)
    bias = na_bias_table(_rpb_flat(na_rpb))
    h, pret, pna = premix_proj(x, mod3, g_pre_mix, wperm, False, "premix_proj")
    hc, pretc, pnac = premix_proj(ctx, mod3, g_pre_mix, wperm, True, "premix_proj_ctx")
    o_all, mixin, gw_out = retention_fwd(pret, pretc, rd, gn, cos, sin, late_weights(0))
    mixin, gw1, gw2 = na_fwd(pna, pnac, bias, mixin, late_weights(1))
    dx_tail, dmix, h2, du, act, dm, dmixin, dmod_t, dg_t, loss_t = tail_fwd_bwd(
        x, mixin, tgt, mod3, g_post_mix, g_pre_mlp, g_post_mlp, gw_out.reshape(D, D), gw1.reshape(4, D, D),
        gw2.reshape(DFF, D))
    dw_out = weight_grad([(mixin.reshape(tokens, D), dmix.reshape(tokens, D))], "grad_w_out", BF16)
    dw1 = weight_grad([(h2.reshape(tokens, D), du.reshape(tokens, DFF))], "grad_w_mlp1", BF16, col_blocks=True)
    dw2 = weight_grad([(act.reshape(tokens, DFF), dm.reshape(tokens, D))], "grad_w_mlp2", BF16)
    dproj, dprojc, drd, dgn, *landed = retention_bwd(pret, pretc, o_all, dmixin, rd, gn, cos, sin,
                                                     early_grads[0](dw_out, dw1, dw2))
    dproj, dprojc, dpat, *early = na_bwd(pna, pnac, bias, dmixin, dproj, dprojc, early_grads[1](landed))
    grad_x, dmod_a, dg_a = premix_bwd(x, mod3, g_pre_mix, wperm, dproj, dx_tail, "premix_bwd")
    dmod_c, dg_c = premix_bwd(ctx, mod3, g_pre_mix, wperm, dprojc, None, "premix_bwd_ctx")
    dw_in = weight_grad([(h.reshape(tokens, D), dproj.reshape(tokens, IN_W)),
                         (hc.reshape(nb * LC, D), dprojc.reshape(nb * LC, IN_W))], "grad_w_in", tn=512)
    dmod = jnp.concatenate([jnp.concatenate([dmod_a[:, 0:2], dmod_t[:, 2:6]], axis=1), dmod_c], axis=0)
    last = jnp.pad(jnp.concatenate([drd[:, :, 0].T.reshape(8), loss_t[0, 0:1]]), (0, D - 9)).reshape(1, D)
    small = jnp.concatenate([dg_a[0:1], dg_c[0:1], dg_t[0:3], _pad_row(dgn, 1), dpat.reshape(8, D), last], axis=0)
    return grad_x, dw_in, early, dmod, small


def kernel(x, c, ctx, c_ctx, w_ada, b_ada, g_pre_mix, g_post_mix, g_pre_mlp, g_post_mlp, w_in, ret_decay, ret_gn, na_rpb, w_out, w_mlp1, w_mlp2, loss_target, m_c_ctx, m_w_ada, m_b_ada, m_g_pre_mix, m_g_post_mix, m_g_pre_mlp, m_g_post_mlp, m_w_in, m_ret_decay, m_ret_gn, m_na_rpb, m_w_out, m_w_mlp1, m_w_mlp2, v_c_ctx, v_w_ada, v_b_ada, v_g_pre_mix, v_g_post_mix, v_g_pre_mlp, v_g_post_mlp, v_w_in, v_ret_decay, v_ret_gn, v_na_rpb, v_w_out, v_w_mlp1, v_w_mlp2):
    px, py, pc = _place()
    dev = 4 * px + 2 * py + pc
    chip = 2 * px + py

    def my_half(w2d):
        rows = w2d.shape[0] // 2
        return lax.dynamic_slice_in_dim(w2d, pc * rows, rows, 0)

    halves = [my_half(w[0]).astype(BF16) for w in (w_in, w_out, w_mlp1, w_mlp2)]
    gw_in, cg = all_gather8([halves[0], jnp.pad(c, ((0, 6), (0, 0)))], "gather_w_in")
    wperm = unpack_w_in(gw_in.reshape(4, D, 896))

    cin = jnp.pad(cg[:, 0:2].reshape(16, D), ((0, 16), (0, 0))) + jnp.pad(c_ctx[None], ((16, 15), (0, 0)))
    mod_mine = mod_shard(cin, w_ada[0], lax.dynamic_slice_in_dim(b_ada, chip * 1536, 1536, 1))
    (mg,) = all_gather8([mod_mine], "gather_mod")
    mod_all = jnp.concatenate([mg[0], mg[2], mg[4], mg[6]], axis=1)
    mod3 = (jnp.pad(lax.dynamic_slice_in_dim(mod_all, 2 * dev, 2, 0), ((0, 1), (0, 0)))
            + jnp.pad(mod_all[16:17], ((2, 0), (0, 0)))).reshape(3, 6, D)

    place = jnp.stack([pc, chip]).astype(jnp.int32)

    early_names = ["w_out", "w_mlp1", "w_mlp2"]
    early_g8, early_partial = [], []

    def early_a(dw_out, dw1, dw2):
        early_g8[:] = [dw_out.reshape(8, 128, D), dw1.reshape(8, 512, D), dw2.reshape(8, 512, D)]
        return siblings4(early_g8)

    def early_b(landed):
        early_partial[:] = [chip_partial(place, g, l, "rs_chip_sum_" + n)
                            for g, l, n in zip(early_g8, landed, early_names)]
        return chips3(early_partial)

    grad_x, dw_in, early_landed, dmod, small = local_step(
        x, ctx, loss_target, mod3, g_pre_mix, g_post_mix, g_pre_mlp, g_post_mlp, ret_decay[0], ret_gn, na_rpb[0],
        wperm, lambda k: gather8(halves[1:2] if k == 0 else halves[2:4]), (early_a, early_b))
    early_mine = [shard_sum(place, p, l, "rs_shard_sum_" + n)
                  for p, l, n in zip(early_partial, early_landed, early_names)]
    early_theirs = sibling_exchange(early_mine, "rs_halves_early")

    pay = jnp.concatenate([dmod.reshape(18, D), small, jnp.zeros((40 - 18 - SMALL_SUM_ROWS, D), F32)], axis=0)
    (gs,) = all_gather8([pay], "gather_small")
    gbf = gs[:, 0:12].reshape(16, 6 * D)
    gcf = gs[:, 12:18].reshape(8, 6 * D)
    gw_ada, pc_part = ada_grads(cin, lax.dynamic_slice_in_dim(gbf, chip * 1536, 1536, 1),
                                lax.dynamic_slice_in_dim(gcf, chip * 1536, 1536, 1), w_ada[0])
    (pcg,) = all_gather8([pc_part], "gather_c_ctx")

    g8_in = pack_w_in(dw_in).reshape(8, 512, 896)
    (landed_in,) = sibling_blocks([g8_in], "rs_sibling_w_in")
    partial_in = chip_partial(place, g8_in, landed_in, "rs_chip_sum_w_in")
    *grouped, landed3_in = adamw_group(
        place,
        [(w_mlp1[0], early_mine[1], early_theirs[1], m_w_mlp1[0], v_w_mlp1[0]),
         (w_mlp2[0], early_mine[2], early_theirs[2], m_w_mlp2[0], v_w_mlp2[0])],
        [(w_ada[0], gw_ada, m_w_ada[0], v_w_ada[0])], chips3([partial_in]), "adamw_group")
    d_ada, m_ada, v_ada = grouped[8:11]
    mine_in = shard_sum(place, partial_in, landed3_in, "rs_shard_sum_w_in")
    (theirs_in,) = sibling_exchange([mine_in], "rs_halves_w_in")
    big = [
        [r[None] for r in adamw_halves(place, w_in[0], mine_in, theirs_in, m_w_in[0], v_w_in[0], "adamw_w_in")],
        [r[None] for r in adamw_halves(place, w_out[0], early_mine[0], early_theirs[0], m_w_out[0], v_w_out[0],
                                       "adamw_w_out")],
        [r[None] for r in grouped[0:4]], [r[None] for r in grouped[4:8]],
    ]

    def rpb_rows(t):
        return _rpb_flat(t[0]).reshape(8, D)

    def decay_row(t):
        return jnp.pad(t.reshape(1, 8), ((0, 0), (0, D - 8)))

    views = [lambda t: t.reshape(1, D), lambda t: t, lambda t: t, lambda t: t, lambda t: t, lambda t: t, lambda t: t,
             rpb_rows, decay_row]
    back = [lambda t: t.reshape(D), lambda t: t, lambda t: t, lambda t: t, lambda t: t, lambda t: t, lambda t: t,
            lambda t: _rpb_flat_t(t)[None], lambda t: t[:, 0:8].reshape(1, 2, 4)]
    small_w = (c_ctx, b_ada, g_pre_mix, g_post_mix, g_pre_mlp, g_post_mlp, ret_gn, na_rpb, ret_decay)
    small_m = (m_c_ctx, m_b_ada, m_g_pre_mix, m_g_post_mix, m_g_pre_mlp, m_g_post_mlp, m_ret_gn, m_na_rpb, m_ret_decay)
    small_v = (v_c_ctx, v_b_ada, v_g_pre_mix, v_g_post_mix, v_g_pre_mlp, v_g_post_mlp, v_ret_gn, v_na_rpb, v_ret_decay)
    *res, loss8 = small_update(gs[:, 18:18 + SMALL_SUM_ROWS], gbf, gcf, pcg[:, 0],
                               [(f(w), f(m), f(v)) for f, w, m, v in zip(views, small_w, small_m, small_v)])

    def leaves(ada, idx):
        s_c, s_b, s_g1, s_g2, s_g3, s_g4, s_gn, s_rpb, s_rd = [back[i](res[4 * i + idx]) for i in range(9)]
        return [s_c, ada[None], s_b, s_g1, s_g2, s_g3, s_g4, big[0][idx], s_rd, s_gn, s_rpb,
                big[1][idx], big[2][idx], big[3][idx]]

    return (loss8[0, 0], grad_x, *leaves(gw_ada, 0), *leaves(d_ada, 1), *leaves(m_ada, 2), *leaves(v_ada, 3))
```

```python
import functools

import jax
import jax.numpy as jnp
from jax import lax
from jax.experimental import pallas as pl
from jax.experimental.pallas import tpu as pltpu

F32, BF16 = jnp.float32, jnp.bfloat16
D = 1024
SEQ = 2048
LC = 256
GW = 64
RH, RD, CH = 4, 128, 128
NPAIR = 4
IN_W = 3584
RET_W = 2048
DFF = 4096
EPS = 1e-6
NEG = -1e30
TN = 256
NCH = SEQ // CH
LR, B1, B2, AEPS, WD, STEP = 0.001, 0.9, 0.999, 1e-08, 0.01, 10
MESH = pl.DeviceIdType.MESH
VMEM_LIMIT = 56 * 1024 * 1024


def _cp(sem=None):
    return pltpu.CompilerParams(dimension_semantics=sem, vmem_limit_bytes=VMEM_LIMIT)


def _nn(a, b):
    return jnp.dot(a.astype(BF16), b.astype(BF16), preferred_element_type=F32)


def _nt(a, b):
    return lax.dot_general(a.astype(BF16), b.astype(BF16), (((1,), (1,)), ((), ())), preferred_element_type=F32)


def _tn(a, b):
    return lax.dot_general(a.astype(BF16), b.astype(BF16), (((0,), (0,)), ((), ())), preferred_element_type=F32)


@jax.custom_vjp
def mm_nn(a, b):
    return _nn(a, b)


@jax.custom_vjp
def mm_nt(a, b):
    return _nt(a, b)


@jax.custom_vjp
def mm_tn(a, b):
    return _tn(a, b)


mm_nn.defvjp(lambda a, b: (_nn(a, b), (a, b)), lambda r, g: (_nt(g, r[1]), _tn(r[0], g)))
mm_nt.defvjp(lambda a, b: (_nt(a, b), (a, b)), lambda r, g: (_nn(g, r[1]), _tn(g, r[0])))
mm_tn.defvjp(lambda a, b: (_tn(a, b), (a, b)), lambda r, g: (_nt(r[1], g), _nn(r[0], g)))


def _rms(x):
    return x * lax.rsqrt(jnp.mean(x * x, axis=-1, keepdims=True) + EPS)


def _rms_mod(x, g, sc, sh):
    return (_rms(x) * g) * (1.0 + sc) + sh


def _post_mix(x, mix, gt1, sc2, sh2, g_post_mix, g_pre_mlp):
    x1 = x + gt1 * (_rms(mix) * g_post_mix)
    return x1, _rms_mod(x1, g_pre_mlp, sc2, sh2)


def _head_loss(x1, m, gt2, g_post_mlp, tgt):
    err = x1 + gt2 * (_rms(m) * g_post_mlp) - tgt
    return 0.5 * jnp.sum(jnp.mean(err * err, axis=-1, keepdims=True), axis=0, keepdims=True)


def _ln_gate(o, g, w):
    mu = jnp.mean(o, axis=-1, keepdims=True)
    var = jnp.mean(jnp.square(o - mu), axis=-1, keepdims=True)
    y = (o - mu) * lax.rsqrt(var + EPS)
    return (y * w) * (g * jax.nn.sigmoid(g))


def _swap32(x):
    lane = lax.broadcasted_iota(jnp.int32, x.shape, 1)
    return jnp.where((lane & 32) == 0, pltpu.roll(x, 96, 1), pltpu.roll(x, 32, 1))


def _rope(x, cos, sin):
    return x * cos + _swap32(x) * sin


def _rope_t(g, cos, sin):
    return g * cos + _swap32(g * sin)


def _rope_tables():
    tok = jnp.arange(SEQ)
    pos_r = (tok // GW).astype(F32)
    pos_c = (tok % GW).astype(F32)
    inv = 10000.0 ** (-jnp.arange(32, dtype=F32) / 32)
    ar = pos_r[:, None] * inv[None, :]
    ac = pos_c[:, None] * inv[None, :]
    cos = jnp.concatenate([jnp.cos(ar), jnp.cos(ar), jnp.cos(ac), jnp.cos(ac)], axis=-1)
    sin = jnp.concatenate([-jnp.sin(ar), jnp.sin(ar), -jnp.sin(ac), jnp.sin(ac)], axis=-1)
    return cos, sin


def _fiota(shape, dim):
    return lax.broadcasted_iota(jnp.int32, shape, dim).astype(F32)


def _ret_state(k, v, s, lg, reverse):
    pos = _fiota((CH, 1), 0)
    b_exp = pos if reverse else (CH - 1.0 - pos)
    return jnp.exp(lg * CH) * s + mm_tn(k * jnp.exp(lg * b_exp), v)


def _ret_chunk(q, k, v, s, lg, reverse):
    i = _fiota((CH, CH), 0)
    j = _fiota((CH, CH), 1)
    diff = (j - i) if reverse else (i - j)
    mask = (diff > 0) if reverse else (diff >= 0)
    decay = jnp.where(mask, jnp.exp(lg * jnp.where(mask, diff, 0.0)), 0.0)
    pos = _fiota((CH, 1), 0)
    a_exp = (CH - pos) if reverse else (pos + 1.0)
    o = mm_nn(mm_nt(q, k) * decay, v) + mm_nn(q * jnp.exp(lg * a_exp), s)
    return o, _ret_state(k, v, s, lg, reverse)


def premix_proj(xin, mod3, g_pre, wperm, is_ctx, name):
    nb, length, _ = xin.shape
    tn = min(TN, length)

    def body(x_ref, mod_ref, g_ref, w_ref, h_ref, pret_ref, pna_ref):
        h = _rms_mod(x_ref[...], g_ref[...], mod_ref[1:2, :], mod_ref[0:1, :])
        hb = h.astype(BF16)
        h_ref[...] = hb
        pret_ref[...] = jnp.dot(hb, w_ref[:, :RET_W], preferred_element_type=F32)
        pna_ref[...] = jnp.dot(hb, w_ref[:, RET_W:], preferred_element_type=F32).astype(BF16)

    return pl.pallas_call(
        body, name=name, grid=(nb, length // tn),
        in_specs=[
            pl.BlockSpec((None, tn, D), lambda b, t: (b, t, 0)),
            pl.BlockSpec((None, 6, D), (lambda b, t: (2, 0, 0)) if is_ctx else (lambda b, t: (b, 0, 0))),
            pl.BlockSpec((1, D), lambda b, t: (0, 0)),
            pl.BlockSpec((D, IN_W), lambda b, t: (0, 0), pipeline_mode=pl.Buffered(1)),
        ],
        out_specs=[
            pl.BlockSpec((None, tn, D), lambda b, t: (b, t, 0)),
            pl.BlockSpec((None, tn, RET_W), lambda b, t: (b, t, 0)),
            pl.BlockSpec((None, tn, IN_W - RET_W), lambda b, t: (b, t, 0)),
        ],
        out_shape=[
            jax.ShapeDtypeStruct((nb, length, D), BF16),
            jax.ShapeDtypeStruct((nb, length, RET_W), F32),
            jax.ShapeDtypeStruct((nb, length, IN_W - RET_W), BF16),
        ],
        compiler_params=_cp(("arbitrary", "arbitrary")),
    )(xin, mod3, g_pre, wperm)


def premix_bwd(xin, mod3, g_pre, wperm, dproj, dx_tail, hosted, name):
    nb, length, _ = xin.shape
    tn = min(TN, length)
    is_ctx = dx_tail is None

    def body(*refs):
        own_in, h_in, own_out, h_out, _, h_sems = hosted.split(refs, 5 if is_ctx else 6, 2 if is_ctx else 3)
        if is_ctx:
            (x_ref, mod_ref, g_ref, w_ref, dp_ref), (dmod_ref, dg_ref) = own_in, own_out
        else:
            (x_ref, mod_ref, g_ref, w_ref, dp_ref, dxt_ref), (dx_ref, dmod_ref, dg_ref) = own_in, own_out
        b, t = pl.program_id(0), pl.program_id(1)
        grid_step = b * (length // tn) + t

        @pl.when(grid_step == 0)
        def _():
            hosted.start(h_in, h_out, h_sems)

        @pl.when(grid_step == nb * (length // tn) - 1)
        def _():
            hosted.finish(h_in, h_out, h_sems)

        dh = lax.dot_general(dp_ref[...], w_ref[...], (((1,), (1,)), ((), ())), preferred_element_type=F32)
        _, vjp = jax.vjp(_rms_mod, x_ref[...], g_ref[...], mod_ref[1:2, :], mod_ref[0:1, :])
        dx, dg, dsc, dsh = vjp(dh)
        if not is_ctx:
            dx_ref[...] = dx + dxt_ref[...]

        @pl.when((t == 0) & ((b == 0) if is_ctx else True))
        def _():
            dmod_ref[...] = jnp.zeros_like(dmod_ref)

        @pl.when((t == 0) & (b == 0))
        def _():
            dg_ref[...] = jnp.zeros_like(dg_ref)

        dmod_ref[0:1, :] += dsh
        dmod_ref[1:2, :] += dsc
        dg_ref[0:1, :] += dg

    tok = lambda b, t: (b, t, 0)
    in_specs = [
        pl.BlockSpec((None, tn, D), tok),
        pl.BlockSpec((None, 6, D), (lambda b, t: (2, 0, 0)) if is_ctx else (lambda b, t: (b, 0, 0))),
        pl.BlockSpec((1, D), lambda b, t: (0, 0)),
        pl.BlockSpec((D, IN_W), lambda b, t: (0, 0), pipeline_mode=pl.Buffered(1)),
        pl.BlockSpec((None, tn, IN_W), tok),
    ]
    args = [xin, mod3, g_pre, wperm, dproj]
    out_specs = [
        pl.BlockSpec((None, 6, D), (lambda b, t: (0, 0, 0)) if is_ctx else (lambda b, t: (b, 0, 0))),
        pl.BlockSpec((8, D), lambda b, t: (0, 0)),
    ]
    out_shape = [jax.ShapeDtypeStruct((1 if is_ctx else nb, 6, D), F32), jax.ShapeDtypeStruct((8, D), F32)]
    if not is_ctx:
        in_specs.append(pl.BlockSpec((None, tn, D), tok))
        args.append(dx_tail)
        out_specs.insert(0, pl.BlockSpec((None, tn, D), tok))
        out_shape.insert(0, jax.ShapeDtypeStruct((nb, length, D), F32))
    h_in_specs, h_out_specs = hosted.specs()
    return pl.pallas_call(
        body, name=name, grid=(nb, length // tn), in_specs=in_specs + h_in_specs, out_specs=out_specs + h_out_specs,
        out_shape=out_shape + hosted.out_shape, scratch_shapes=hosted.scratch,
        compiler_params=_cp(("arbitrary", "arbitrary")),
    )(*args, *hosted.args)


def _ret_specs(order):
    def im(f):
        return lambda *g: f(*order(*g))
    return dict(
        pret=pl.BlockSpec((None, SEQ, 512), im(lambda b, h: (b, 0, h))),
        pretc=pl.BlockSpec((None, LC, 512), im(lambda b, h: (b, 0, h))),
        rd=pl.BlockSpec((None, 2, 1), im(lambda b, h: (h, 0, 0))),
        gn=pl.BlockSpec((None, 1, RD), im(lambda b, h: (h, 0, 0))),
        tab=pl.BlockSpec((SEQ, RD), im(lambda b, h: (0, 0))),
        head=pl.BlockSpec((None, SEQ, RD), im(lambda b, h: (b, 0, h))),
    )


def retention_fwd(pret, pretc, rd, gn, cos, sin, hosted):
    nb = pret.shape[0]
    sp = _ret_specs(lambda b, h: (b, h))

    def body(*refs):
        own_in, h_in, own_out, h_out, own_scr, h_sems = hosted.split(refs, 6, 2)
        p_ref, pc_ref, rd_ref, gn_ref, cos_ref, sin_ref = own_in
        (o_ref, mix_ref), (q_s, k_s, of_s, ob_s) = own_out, own_scr
        grid_step = pl.program_id(0) * RH + pl.program_id(1)

        @pl.when(grid_step == 0)
        def _():
            hosted.start(h_in, h_out, h_sems)

        cos_v, sin_v = cos_ref[...], sin_ref[...]
        q_s[...] = _rope(p_ref[:, 0:128], cos_v, sin_v) * (RD ** -0.5)
        k_s[...] = _rope(p_ref[:, 128:256], cos_v, sin_v)
        lgs, init = [], []
        for rev in (False, True):
            lg = jax.nn.log_sigmoid(rd_ref[int(rev):int(rev) + 1, :])
            s = jnp.zeros((RD, RD), F32)
            for n in ((1, 0) if rev else (0, 1)):
                s = _ret_state(pc_ref[n * CH:(n + 1) * CH, 128:256], pc_ref[n * CH:(n + 1) * CH, 256:384], s, lg, rev)
            lgs.append(lg)
            init.append(s)

        def step(t, carry):
            out = []
            for rev, o_s, s in ((False, of_s, carry[0]), (True, ob_s, carry[1])):
                n = (NCH - 1 - t) if rev else t
                sl = pl.ds(pl.multiple_of(n * CH, CH), CH)
                o, s2 = _ret_chunk(q_s[sl, :], k_s[sl, :], p_ref[sl, 256:384], s, lgs[int(rev)], rev)
                o_s[sl, :] = o
                out.append(s2)
            return tuple(out)

        lax.fori_loop(0, NCH, step, tuple(init))
        o = of_s[...] + ob_s[...]
        o_ref[...] = o
        mix_ref[...] = _ln_gate(o, p_ref[:, 384:512], gn_ref[...]).astype(BF16)

        @pl.when(grid_step == nb * RH - 1)
        def _():
            hosted.finish(h_in, h_out, h_sems)

    h_in_specs, h_out_specs = hosted.specs()
    return pl.pallas_call(
        body, name="retention_fwd", grid=(nb, RH),
        in_specs=[sp["pret"], sp["pretc"], sp["rd"], sp["gn"], sp["tab"], sp["tab"]] + h_in_specs,
        out_specs=[sp["head"], sp["head"]] + h_out_specs,
        out_shape=[jax.ShapeDtypeStruct((nb, SEQ, RH * RD), F32), jax.ShapeDtypeStruct((nb, SEQ, D), BF16)]
        + hosted.out_shape,
        scratch_shapes=[pltpu.VMEM((SEQ, RD), F32)] * 4 + hosted.scratch,
        compiler_params=_cp(("arbitrary", "arbitrary")),
    )(pret, pretc, rd, gn, cos, sin, *hosted.args)


def retention_bwd(pret, pretc, o_all, dmixin, rd, gn, cos, sin, hosted):
    nb = pret.shape[0]
    sp = _ret_specs(lambda h, b: (b, h))

    def body(*refs):
        own_in, h_in, own_out, h_out, own_scr, h_sems = hosted.split(refs, 8, 4)
        p_ref, pc_ref, o_ref, dmix_ref, rd_ref, gn_ref, cos_ref, sin_ref = own_in
        dp_ref, dpc_ref, drd_ref, dgn_ref = own_out
        q_s, k_s, do_s, dqf_s, dkf_s, dvf_s, dqb_s, dkb_s, dvb_s, stf_s, stb_s = own_scr
        b = pl.program_id(1)
        grid_step = pl.program_id(0) * nb + b

        @pl.when(grid_step == 0)
        def _():
            hosted.start(h_in, h_out, h_sems)

        cos_v, sin_v = cos_ref[...], sin_ref[...]
        q_s[...] = _rope(p_ref[:, 0:128], cos_v, sin_v) * (RD ** -0.5)
        k_s[...] = _rope(p_ref[:, 128:256], cos_v, sin_v)
        _, gate_vjp = jax.vjp(_ln_gate, o_ref[...], p_ref[:, 384:512], gn_ref[...])
        do, dg, dgn = gate_vjp(dmix_ref[...].astype(F32))
        do_s[...] = do
        dp_ref[:, 384:512] = dg.astype(BF16)

        @pl.when(b == 0)
        def _():
            drd_ref[...] = jnp.zeros_like(drd_ref)
            dgn_ref[...] = jnp.zeros_like(dgn_ref)

        dgn_ref[...] += dgn
        kcs = [pc_ref[n * CH:(n + 1) * CH, 128:256] for n in (0, 1)]
        vcs = [pc_ref[n * CH:(n + 1) * CH, 256:384] for n in (0, 1)]
        dirs = []
        init = []
        for rev in (False, True):
            rdv = rd_ref[int(rev):int(rev) + 1, :]
            lg = jax.nn.log_sigmoid(rdv)
            order_c = (1, 0) if rev else (0, 1)
            s = jnp.zeros((RD, RD), F32)
            ctx_states = []
            for n in order_c:
                ctx_states.append(s)
                s = _ret_state(kcs[n], vcs[n], s, lg, rev)
            dirs.append((rev, order_c, lg, rdv, ctx_states))
            init.append(s)
        acc = ((dqf_s, dkf_s, dvf_s, stf_s), (dqb_s, dkb_s, dvb_s, stb_s))

        def fstep(t, carry):
            out = []
            for (rev, _, lg, _, _), (_, _, _, st_s), s in zip(dirs, acc, carry):
                n = (NCH - 1 - t) if rev else t
                sl = pl.ds(pl.multiple_of(n * CH, CH), CH)
                st_s[n] = s
                out.append(_ret_state(k_s[sl, :], p_ref[sl, 256:384], s, lg, rev))
            return tuple(out)

        lax.fori_loop(0, NCH, fstep, tuple(init))

        def bstep(t, carry):
            out = []
            for (rev, _, lg, _, _), (dq_s, dk_s, dv_s, st_s), (ds, dlg) in zip(dirs, acc, carry):
                n = t if rev else (NCH - 1 - t)
                sl = pl.ds(pl.multiple_of(n * CH, CH), CH)
                _, vjp = jax.vjp(functools.partial(_ret_chunk, reverse=rev),
                                 q_s[sl, :], k_s[sl, :], p_ref[sl, 256:384], st_s[n], lg)
                dq, dk, dv, ds_prev, dl = vjp((do_s[sl, :], ds))
                dq_s[sl, :] = dq
                dk_s[sl, :] = dk
                dv_s[sl, :] = dv
                out.append((ds_prev, dlg + dl))
            return tuple(out)

        zero_c = (jnp.zeros((RD, RD), F32), jnp.zeros((1, 1), F32))
        res = lax.fori_loop(0, NCH, bstep, (zero_c, zero_c))
        dkc = [None, None]
        dvc = [None, None]
        for (rev, order_c, lg, rdv, ctx_states), (ds, dlg) in zip(dirs, res):
            for idx in (1, 0):
                n = order_c[idx]
                _, vjp = jax.vjp(functools.partial(_ret_state, reverse=rev), kcs[n], vcs[n], ctx_states[idx], lg)
                dk_c, dv_c, ds, dl = vjp(ds)
                dlg = dlg + dl
                dkc[n] = dk_c if dkc[n] is None else dkc[n] + dk_c
                dvc[n] = dv_c if dvc[n] is None else dvc[n] + dv_c
            drd_ref[int(rev):int(rev) + 1, :] += dlg * jax.nn.sigmoid(-rdv)
        dp_ref[:, 0:128] = _rope_t((dqf_s[...] + dqb_s[...]) * (RD ** -0.5), cos_v, sin_v).astype(BF16)
        dp_ref[:, 128:256] = _rope_t(dkf_s[...] + dkb_s[...], cos_v, sin_v).astype(BF16)
        dp_ref[:, 256:384] = (dvf_s[...] + dvb_s[...]).astype(BF16)
        zero = jnp.zeros((CH, RD), BF16)
        for n in (0, 1):
            rows = slice(n * CH, (n + 1) * CH)
            dpc_ref[rows, 0:128] = zero
            dpc_ref[rows, 128:256] = dkc[n].astype(BF16)
            dpc_ref[rows, 256:384] = dvc[n].astype(BF16)
            dpc_ref[rows, 384:512] = zero

        @pl.when(grid_step == RH * nb - 1)
        def _():
            hosted.finish(h_in, h_out, h_sems)

    h_in_specs, h_out_specs = hosted.specs()
    return pl.pallas_call(
        body, name="retention_bwd", grid=(RH, nb),
        in_specs=[sp["pret"], sp["pretc"], sp["head"], sp["head"], sp["rd"], sp["gn"], sp["tab"], sp["tab"]]
        + h_in_specs,
        out_specs=[
            pl.BlockSpec((None, SEQ, 512), lambda h, b: (b, 0, h)),
            pl.BlockSpec((None, LC, 512), lambda h, b: (b, 0, h)),
            pl.BlockSpec((None, 2, 1), lambda h, b: (h, 0, 0)),
            pl.BlockSpec((None, 1, RD), lambda h, b: (h, 0, 0)),
        ] + h_out_specs,
        out_shape=[
            jax.ShapeDtypeStruct((nb, SEQ, IN_W), BF16),
            jax.ShapeDtypeStruct((nb, LC, IN_W), BF16),
            jax.ShapeDtypeStruct((RH, 2, 1), F32),
            jax.ShapeDtypeStruct((RH, 1, RD), F32),
        ] + hosted.out_shape,
        scratch_shapes=[pltpu.VMEM((SEQ, RD), F32)] * 9 + [pltpu.VMEM((NCH, RD, RD), F32)] * 2 + hosted.scratch,
        compiler_params=_cp(("arbitrary", "arbitrary")),
    )(pret, pretc, o_all, dmixin, rd, gn, cos, sin, *hosted.args)


def _rpb_flat(rpb):
    return jnp.pad(rpb, ((0, 0), (0, 1), (0, 33))).reshape(NPAIR, 2, 1, 1024)


def _rpb_flat_t(dflat):
    return dflat.reshape(8, 16, 64)[:, :15, :31]


def _barrel(x, left):
    row = lax.broadcasted_iota(jnp.int32, x.shape, 0)
    n = x.shape[1]
    for bit in range(6):
        s = 1 << bit
        x = jnp.where(((row >> bit) & 1) == 1, pltpu.roll(x, (n - s) if left else s, 1), x)
    return x


NA_TILE_ROWS, NA_BAND_ROWS = 4, 12
NA_Q, NA_K = NA_TILE_ROWS * GW, NA_BAND_ROWS * GW
NA_TILES = SEQ // NA_Q


def _tile_rows(cls):
    if cls == 0:
        return [(qr, 0) for qr in range(4)]
    if cls == 1:
        return [(4, qr) for qr in range(4)]
    return [(4, 4), (5, 4), (6, 4), (7, 4)]


def _na_tile(t):
    start = jnp.clip(4 * t - 4, 0, 32 - NA_BAND_ROWS)
    cls = jnp.where(t == 0, 0, jnp.where(t == NA_TILES - 1, 2, 1))
    return pl.ds(pl.multiple_of(t * NA_Q, NA_Q), NA_Q), pl.ds(pl.multiple_of(start * GW, NA_Q), NA_K), cls


def _na_probs(qst, kb, kc, bias):
    s_loc = _nt(qst, kb) * 0.125 + bias
    s_ctx = _nt(qst, kc) * 0.125
    m = jnp.maximum(jnp.max(s_loc, axis=1, keepdims=True), jnp.max(s_ctx, axis=1, keepdims=True))
    e_loc, e_ctx = jnp.exp(s_loc - m), jnp.exp(s_ctx - m)
    den = jnp.sum(e_loc, axis=1, keepdims=True) + jnp.sum(e_ctx, axis=1, keepdims=True)
    return e_loc / den, e_ctx / den


def _stack_heads(t):
    lane = lax.broadcasted_iota(jnp.int32, t.shape, 1)
    zero = jnp.zeros_like(t)
    return jnp.concatenate([jnp.where(lane < 64, t, zero), jnp.where(lane >= 64, t, zero)], axis=0)


def _unstack_heads(t):
    n = t.shape[0] // 2
    lane = lax.broadcasted_iota(jnp.int32, (n, 128), 1)
    return jnp.where(lane < 64, t[:n], t[n:])


def na_bias_table(flat):
    def body(flat_ref, out_ref):
        qc = lax.broadcasted_iota(jnp.int32, (GW, 512), 0)
        kc = lax.broadcasted_iota(jnp.int32, (GW, 512), 1) & 63
        start = jnp.clip(qc - 8, 0, GW - 16)
        window = (kc >= start) & (kc < start + 16)
        fill = jnp.full((GW, NA_K - 512), NEG, F32)
        for hh in (0, 1):
            skew = _barrel(pltpu.roll(jnp.broadcast_to(flat_ref[hh], (GW, 1024)), 1024 - 15, 1), left=False)
            by_class = [jnp.where(window, (skew if rc == 7 else pltpu.roll(skew, (9 + rc) * 64, 1))[:, 0:512], NEG)
                        for rc in range(8)]
            for cls in range(3):
                for qr, (rc, off) in enumerate(_tile_rows(cls)):
                    w = jnp.concatenate([by_class[rc], fill], axis=1)
                    rows = slice(hh * NA_Q + qr * GW, hh * NA_Q + (qr + 1) * GW)
                    out_ref[cls, rows, :] = pltpu.roll(w, off * GW, 1) if off else w

    return pl.pallas_call(
        body, name="na_bias_table", grid=(NPAIR,),
        in_specs=[pl.BlockSpec((None, 2, 1, 1024), lambda p: (p, 0, 0, 0))],
        out_specs=pl.BlockSpec((None, 3, 2 * NA_Q, NA_K), lambda p: (p, 0, 0, 0)),
        out_shape=jax.ShapeDtypeStruct((NPAIR, 3, 2 * NA_Q, NA_K), F32),
    )(flat)


def na_fwd(pna, pnac, bias, mixin, hosted):
    nb = pna.shape[0]

    def body(*refs):
        (p_ref, pc_ref, bias_ref, _), h_in, (out_ref,), h_out, _, h_sems = hosted.split(refs, 4, 1)
        grid_step = pl.program_id(0) * nb + pl.program_id(1)

        @pl.when(grid_step == 0)
        def _():
            hosted.start(h_in, h_out, h_sems)

        kc, vc = pc_ref[:, 128:256], pc_ref[:, 256:384]

        def tile(t, carry):
            qsl, bsl, cls = _na_tile(t)
            kb, vb = p_ref[bsl, 128:256], p_ref[bsl, 256:384]
            p_loc, p_ctx = _na_probs(_stack_heads(p_ref[qsl, 0:128]), kb, kc, bias_ref[cls])
            out_ref[qsl, :] = _unstack_heads(_nn(p_loc, vb) + _nn(p_ctx, vc)).astype(BF16)
            return carry

        lax.fori_loop(0, NA_TILES, tile, 0, unroll=2)

        @pl.when(grid_step == NPAIR * nb - 1)
        def _():
            hosted.finish(h_in, h_out, h_sems)

    h_in_specs, h_out_specs = hosted.specs()
    return pl.pallas_call(
        body, name="na_fwd", grid=(NPAIR, nb),
        in_specs=[
            pl.BlockSpec((None, SEQ, 384), lambda p, b: (b, 0, p)),
            pl.BlockSpec((None, LC, 384), lambda p, b: (b, 0, p)),
            pl.BlockSpec((None, 3, 2 * NA_Q, NA_K), lambda p, b: (p, 0, 0, 0)),
            pl.BlockSpec(memory_space=pl.ANY),
        ] + h_in_specs,
        out_specs=[pl.BlockSpec((None, SEQ, 128), lambda p, b: (b, 0, 4 + p))] + h_out_specs,
        out_shape=[jax.ShapeDtypeStruct((nb, SEQ, D), BF16)] + hosted.out_shape,
        input_output_aliases={3: 0},
        scratch_shapes=hosted.scratch,
        compiler_params=_cp(("arbitrary", "arbitrary")),
    )(pna, pnac, bias, mixin, *hosted.args)


def na_bwd(pna, pnac, bias, dmixin, dproj, dprojc, hosted):
    nb = pna.shape[0]

    def body(*refs):
        own_in, h_in, own_out, h_out, own_scr, h_sems = hosted.split(refs, 6, 3)
        p_ref, pc_ref, bias_ref, dmix_ref = own_in[:4]
        dp_ref, dpc_ref, dpat_ref = own_out
        dbias_s, dk_s, dv_s, dkc_s, dvc_s, res_s, resc_s = own_scr
        b, part = pl.program_id(1), pl.program_id(2)
        grid_step = (pl.program_id(0) * nb + b) * 3 + part

        @pl.when(grid_step == 0)
        def _():
            hosted.start(h_in, h_out, h_sems)

        @pl.when(grid_step == NPAIR * nb * 3 - 1)
        def _():
            hosted.finish(h_in, h_out, h_sems)

        @pl.when(part == 0)
        def _():
            @pl.when(b == 0)
            def _():
                dbias_s[...] = jnp.zeros_like(dbias_s)

            dk_s[...] = jnp.zeros_like(dk_s)
            dv_s[...] = jnp.zeros_like(dv_s)
            dkc_s[...] = jnp.zeros_like(dkc_s)
            dvc_s[...] = jnp.zeros_like(dvc_s)
            kc, vc = pc_ref[:, 128:256], pc_ref[:, 256:384]

            def tile(t, carry):
                qsl, bsl, cls = _na_tile(t)
                kb, vb = p_ref[bsl, 128:256], p_ref[bsl, 256:384]
                qst, dost = _stack_heads(p_ref[qsl, 0:128]), _stack_heads(dmix_ref[qsl, :])
                p_loc, p_ctx = _na_probs(qst, kb, kc, bias_ref[cls])
                dp_loc, dp_ctx = _nt(dost, vb), _nt(dost, vc)
                delta = (jnp.sum(p_loc * dp_loc, axis=1, keepdims=True)
                         + jnp.sum(p_ctx * dp_ctx, axis=1, keepdims=True))
                ds_loc, ds_ctx = p_loc * (dp_loc - delta), p_ctx * (dp_ctx - delta)
                dbias_s[cls] += ds_loc
                res_s[0, qsl, :] = _unstack_heads((_nn(ds_loc, kb) + _nn(ds_ctx, kc)) * 0.125).astype(BF16)
                dk_s[bsl, :] += _tn(ds_loc, qst) * 0.125
                dv_s[bsl, :] += _tn(p_loc, dost)
                dkc_s[...] += _tn(ds_ctx, qst) * 0.125
                dvc_s[...] += _tn(p_ctx, dost)
                return carry

            lax.fori_loop(0, NA_TILES, tile, 0, unroll=2)
            res_s[1] = dk_s[...].astype(BF16)
            res_s[2] = dv_s[...].astype(BF16)
            resc_s[0] = jnp.zeros((LC, 128), BF16)
            resc_s[1] = dkc_s[...].astype(BF16)
            resc_s[2] = dvc_s[...].astype(BF16)

            @pl.when(b == nb - 1)
            def _():
                for hh in (0, 1):
                    by_class = [None] * 8
                    for cls in range(3):
                        for qr, (rc, off) in enumerate(_tile_rows(cls)):
                            w = dbias_s[cls, hh * NA_Q + qr * GW:hh * NA_Q + (qr + 1) * GW, :]
                            w = (pltpu.roll(w, NA_K - off * GW, 1) if off else w)[:, 0:512]
                            by_class[rc] = w if by_class[rc] is None else by_class[rc] + w
                    skew = jnp.zeros((GW, 1024), F32)
                    for rc in range(8):
                        w = jnp.concatenate([by_class[rc], jnp.zeros((GW, 512), F32)], axis=1)
                        skew = skew + (w if rc == 7 else pltpu.roll(w, (7 - rc) * 64, 1))
                    dpat_ref[hh] = jnp.sum(pltpu.roll(_barrel(skew, left=True), 15, 1), axis=0, keepdims=True)

        dp_ref[...] = res_s[part]
        dpc_ref[...] = resc_s[part]

    h_in_specs, h_out_specs = hosted.specs()
    return pl.pallas_call(
        body, name="na_bwd", grid=(NPAIR, nb, 3),
        in_specs=[
            pl.BlockSpec((None, SEQ, 384), lambda p, b, s: (b, 0, p)),
            pl.BlockSpec((None, LC, 384), lambda p, b, s: (b, 0, p)),
            pl.BlockSpec((None, 3, 2 * NA_Q, NA_K), lambda p, b, s: (p, 0, 0, 0)),
            pl.BlockSpec((None, SEQ, 128), lambda p, b, s: (b, 0, 4 + p)),
            pl.BlockSpec(memory_space=pl.ANY),
            pl.BlockSpec(memory_space=pl.ANY),
        ] + h_in_specs,
        out_specs=[
            pl.BlockSpec((None, SEQ, 128), lambda p, b, s: (b, 0, 16 + 3 * p + s)),
            pl.BlockSpec((None, LC, 128), lambda p, b, s: (b, 0, 16 + 3 * p + s)),
            pl.BlockSpec((None, 2, 1, 1024), lambda p, b, s: (p, 0, 0, 0)),
        ] + h_out_specs,
        out_shape=[
            jax.ShapeDtypeStruct((nb, SEQ, IN_W), BF16),
            jax.ShapeDtypeStruct((nb, LC, IN_W), BF16),
            jax.ShapeDtypeStruct((NPAIR, 2, 1, 1024), F32),
        ] + hosted.out_shape,
        input_output_aliases={4: 0, 5: 1},
        scratch_shapes=[
            pltpu.VMEM((3, 2 * NA_Q, NA_K), F32),
            pltpu.VMEM((SEQ, 128), F32), pltpu.VMEM((SEQ, 128), F32),
            pltpu.VMEM((LC, 128), F32), pltpu.VMEM((LC, 128), F32),
            pltpu.VMEM((3, SEQ, 128), BF16), pltpu.VMEM((3, LC, 128), BF16),
        ] + hosted.scratch,
        compiler_params=_cp(("arbitrary", "arbitrary", "arbitrary")),
    )(pna, pnac, bias, dmixin, dproj, dprojc, *hosted.args)


def tail_fwd_bwd(x, mixin, tgt, mod3, g_post_mix, g_pre_mlp, g_post_mlp, wout, w1, w2):
    nb = x.shape[0]

    def body(x_ref, mi_ref, tgt_ref, mod_ref, gpm_ref, gpl_ref, gpo_ref, wo_ref, w1_ref, w2_ref,
             dx_ref, dmix_ref, h2_ref, du_ref, a_ref, dm_ref, dmi_ref, dmod_ref, dg_ref, loss_ref):
        b, t = pl.program_id(0), pl.program_id(1)
        gt1, sh2, sc2, gt2 = mod_ref[2:3, :], mod_ref[3:4, :], mod_ref[4:5, :], mod_ref[5:6, :]
        mix = jnp.dot(mi_ref[...], wo_ref[...], preferred_element_type=F32)
        (x1, h2), vjp_a = jax.vjp(_post_mix, x_ref[...], mix, gt1, sc2, sh2, gpm_ref[...], gpl_ref[...])
        h2b = h2.astype(BF16)
        h2_ref[...] = h2b
        m = jnp.zeros((TN, D), F32)
        relus = []
        for j in range(4):
            cols = slice(j * D, (j + 1) * D)
            r = jnp.maximum(jnp.dot(h2b, w1_ref[j], preferred_element_type=F32), 0.0)
            ab = (r * r).astype(BF16)
            a_ref[:, cols] = ab
            m = m + jnp.dot(ab, w2_ref[cols, :], preferred_element_type=F32)
            relus.append(r)
        loss, vjp_b = jax.vjp(_head_loss, x1, m, gt2, gpo_ref[...], tgt_ref[...])
        dx1, dm, dgt2, dgpo, _ = vjp_b(jnp.ones((1, 1), F32))
        dmb = dm.astype(BF16)
        dm_ref[...] = dmb
        dh2 = jnp.zeros((TN, D), F32)
        for j in range(4):
            cols = slice(j * D, (j + 1) * D)
            da = lax.dot_general(dmb, w2_ref[cols, :], (((1,), (1,)), ((), ())), preferred_element_type=F32)
            dub = (da * (2.0 * relus[j])).astype(BF16)
            du_ref[:, cols] = dub
            dh2 = dh2 + lax.dot_general(dub, w1_ref[j], (((1,), (1,)), ((), ())), preferred_element_type=F32)
        dx, dmix, dgt1, dsc2, dsh2, dgpm, dgpl = vjp_a((dx1, dh2))
        dx_ref[...] = dx
        dmixb = dmix.astype(BF16)
        dmix_ref[...] = dmixb
        dmi_ref[...] = lax.dot_general(dmixb, wo_ref[...], (((1,), (1,)), ((), ())),
                                       preferred_element_type=F32).astype(BF16)

        @pl.when(t == 0)
        def _():
            dmod_ref[...] = jnp.zeros_like(dmod_ref)

        @pl.when((t == 0) & (b == 0))
        def _():
            dg_ref[...] = jnp.zeros_like(dg_ref)
            loss_ref[...] = jnp.zeros_like(loss_ref)

        dmod_ref[2:3, :] += dgt1
        dmod_ref[3:4, :] += dsh2
        dmod_ref[4:5, :] += dsc2
        dmod_ref[5:6, :] += dgt2
        dg_ref[0:1, :] += dgpm
        dg_ref[1:2, :] += dgpl
        dg_ref[2:3, :] += dgpo
        loss_ref[...] += jnp.broadcast_to(loss, loss_ref.shape)

    tok = lambda b, t: (b, t, 0)
    const = lambda b, t: (0, 0)
    vec = pl.BlockSpec((1, D), const)
    return pl.pallas_call(
        body, name="tail_fwd_bwd", grid=(nb, SEQ // TN),
        in_specs=[
            pl.BlockSpec((None, TN, D), tok), pl.BlockSpec((None, TN, D), tok), pl.BlockSpec((None, TN, D), tok),
            pl.BlockSpec((None, 6, D), lambda b, t: (b, 0, 0)), vec, vec, vec,
            pl.BlockSpec((D, D), const, pipeline_mode=pl.Buffered(1)),
            pl.BlockSpec((4, D, D), lambda b, t: (0, 0, 0), pipeline_mode=pl.Buffered(1)),
            pl.BlockSpec((DFF, D), const, pipeline_mode=pl.Buffered(1)),
        ],
        out_specs=[
            pl.BlockSpec((None, TN, D), tok), pl.BlockSpec((None, TN, D), tok), pl.BlockSpec((None, TN, D), tok),
            pl.BlockSpec((None, TN, DFF), tok), pl.BlockSpec((None, TN, DFF), tok), pl.BlockSpec((None, TN, D), tok),
            pl.BlockSpec((None, TN, D), tok),
            pl.BlockSpec((None, 6, D), lambda b, t: (b, 0, 0)),
            pl.BlockSpec((8, D), const), pl.BlockSpec((8, 128), const),
        ],
        out_shape=[
            jax.ShapeDtypeStruct((nb, SEQ, D), F32), jax.ShapeDtypeStruct((nb, SEQ, D), BF16),
            jax.ShapeDtypeStruct((nb, SEQ, D), BF16), jax.ShapeDtypeStruct((nb, SEQ, DFF), BF16),
            jax.ShapeDtypeStruct((nb, SEQ, DFF), BF16), jax.ShapeDtypeStruct((nb, SEQ, D), BF16),
            jax.ShapeDtypeStruct((nb, SEQ, D), BF16),
            jax.ShapeDtypeStruct((nb, 6, D), F32), jax.ShapeDtypeStruct((8, D), F32),
            jax.ShapeDtypeStruct((8, 128), F32),
        ],
        compiler_params=_cp(("arbitrary", "arbitrary")),
    )(x, mixin, tgt, mod3, g_post_mix, g_pre_mlp, g_post_mlp, wout, w1, w2)


def weight_grad(pairs, name, out_dtype=F32, col_blocks=False, tm=1024, tn=1024, tk=2048):
    m, n = pairs[0][0].shape[1], pairs[0][1].shape[1]
    tn = min(tn, n)
    tks = [min(tk, xa.shape[0]) for xa, _ in pairs]
    steps = [xa.shape[0] // t for (xa, _), t in zip(pairs, tks)]
    total = sum(steps)
    offs = [sum(steps[:i]) for i in range(len(pairs))]

    def body(*refs):
        out_ref, acc = refs[2 * len(pairs)], refs[-1]
        k = pl.program_id(2)

        @pl.when(k == 0)
        def _():
            acc[...] = jnp.zeros_like(acc)

        for i in range(len(pairs)):
            @pl.when((k >= offs[i]) & (k < offs[i] + steps[i]))
            def _(i=i):
                acc[...] += lax.dot_general(refs[2 * i][...], refs[2 * i + 1][...], (((0,), (0,)), ((), ())),
                                            preferred_element_type=F32)

        if out_dtype != F32:
            @pl.when(k == total - 1)
            def _():
                out_ref[...] = acc[...].astype(out_dtype)

    in_specs, args = [], []
    for i, (xa, ya) in enumerate(pairs):
        clamp = lambda k, i=i: jnp.clip(k - offs[i], 0, steps[i] - 1)
        in_specs.append(pl.BlockSpec((tks[i], tm), lambda a, c, k, clamp=clamp: (clamp(k), a)))
        in_specs.append(pl.BlockSpec((tks[i], tn), lambda a, c, k, clamp=clamp: (clamp(k), c)))
        args += [xa, ya]
    if col_blocks:
        out_spec = pl.BlockSpec((None, tm, tn), lambda a, c, k: (c, a, 0))
        out_shape = jax.ShapeDtypeStruct((n // tn, m, tn), out_dtype)
    else:
        out_spec = pl.BlockSpec((tm, tn), lambda a, c, k: (a, c))
        out_shape = jax.ShapeDtypeStruct((m, n), out_dtype)
    return pl.pallas_call(
        body, name=name, grid=(m // tm, n // tn, total), in_specs=in_specs, out_specs=out_spec, out_shape=out_shape,
        scratch_shapes=[] if out_dtype == F32 else [pltpu.VMEM((tm, tn), F32)],
        compiler_params=_cp(("arbitrary", "arbitrary", "arbitrary")),
    )(*args)


def _perm_block(t):
    return 4 * (t % 4) + t // 4 if t < 16 else 16 + 3 * ((t - 16) % 4) + (t - 16) // 4


def unpack_w_in(blocks):
    def body(i_ref, o_ref):
        for t in range(28):
            p = _perm_block(t)
            o_ref[:, p * 128:(p + 1) * 128] = i_ref[t // 7, :, (t % 7) * 128:(t % 7 + 1) * 128]

    return pl.pallas_call(
        body, name="unpack_w_in", grid=(2,),
        in_specs=[pl.BlockSpec((4, D // 2, 896), lambda i: (0, i, 0))],
        out_specs=pl.BlockSpec((D // 2, IN_W), lambda i: (i, 0)),
        out_shape=jax.ShapeDtypeStruct((D, IN_W), BF16),
    )(blocks)


def pack_w_in(dw):
    def body(i_ref, o_ref):
        for t in range(28):
            p = _perm_block(t)
            o_ref[t // 7, :, (t % 7) * 128:(t % 7 + 1) * 128] = i_ref[:, p * 128:(p + 1) * 128].astype(BF16)

    return pl.pallas_call(
        body, name="pack_w_in", grid=(4,),
        in_specs=[pl.BlockSpec((D // 4, IN_W), lambda i: (i, 0))],
        out_specs=pl.BlockSpec((4, D // 4, 896), lambda i: (0, i, 0)),
        out_shape=jax.ShapeDtypeStruct((4, D, 896), BF16),
    )(dw)


def _place():
    return lax.axis_index("x"), lax.axis_index("y"), lax.axis_index("c")


class Hosted:
    def __init__(self, args, out_shape, scratch, start, finish):
        self.args, self.out_shape, self.scratch, self.start, self.finish = args, out_shape, scratch, start, finish

    def specs(self):
        hbm = pl.BlockSpec(memory_space=pl.ANY)
        return [hbm] * len(self.args), [hbm] * len(self.out_shape)

    def split(self, refs, n_in, n_out):
        a, b = len(self.args), len(self.out_shape)
        cuts = [n_in, n_in + a, n_in + a + n_out, n_in + a + n_out + b, len(refs) - len(self.scratch)]
        parts = [refs[i:j] for i, j in zip([0] + cuts, cuts + [len(refs)])]
        return parts[0], parts[1], parts[2], parts[3], parts[4], parts[5]


def no_exchange():
    return Hosted([], [], [], lambda *a: None, lambda *a: None)


def run_hosted(hosted, name):
    def body(*refs):
        _, ins, _, outs, _, sems = hosted.split(refs, 0, 0)
        hosted.start(ins, outs, sems)
        hosted.finish(ins, outs, sems)

    in_specs, out_specs = hosted.specs()
    return pl.pallas_call(body, name=name, in_specs=in_specs, out_specs=out_specs, out_shape=hosted.out_shape,
                          scratch_shapes=hosted.scratch)(*hosted.args)


def gather8(blocks):
    na = len(blocks)

    def copies(ins, outs, sems):
        send_sems, recv_sems, local_sem = sems
        x, y, c = _place()
        me, sibling = (x, y, c), (x, y, 1 - c)
        chips = [(1 - x, y), (x, 1 - y), (1 - x, 1 - y)]

        def slot(o_ref, px, py, pc):
            return o_ref.at[4 * px + 2 * py + pc]

        def copy(a, k, block, to, src=None):
            return pltpu.make_async_remote_copy(
                src_ref=slot(outs[a], *block) if src is None else src, dst_ref=slot(outs[a], *block),
                send_sem=send_sems.at[a, k], recv_sem=recv_sems.at[a, k], device_id=to, device_id_type=MESH)

        mine = [pltpu.make_async_copy(ins[a], slot(outs[a], *me), local_sem.at[a]) for a in range(na)]
        first = []
        for a in range(na):
            first.append(copy(a, 0, me, sibling, src=ins[a]))
            first += [copy(a, 1 + j, me, (*chip, c), src=ins[a]) for j, chip in enumerate(chips)]
        return copy, mine, first, me, sibling, chips, c

    def start(ins, outs, sems):
        _, mine, first, *_ = copies(ins, outs, sems)
        for cp in mine + first:
            cp.start()

    def finish(ins, outs, sems):
        copy, mine, first, me, sibling, chips, c = copies(ins, outs, sems)
        passed = []
        for j, chip in enumerate(chips):
            for a in range(na):
                copy(a, 1 + j, (*chip, c), me).wait_recv()
                cp = copy(a, 4 + j, (*chip, c), sibling)
                cp.start()
                passed.append(cp)
        for a in range(na):
            copy(a, 0, sibling, me).wait_recv()
            for j, chip in enumerate(chips):
                copy(a, 4 + j, (*chip, 1 - c), me).wait_recv()
        for cp in first + passed:
            cp.wait_send()
        for cp in mine:
            cp.wait()

    return Hosted(list(blocks), [jax.ShapeDtypeStruct((8,) + b.shape, b.dtype) for b in blocks],
                  [pltpu.SemaphoreType.DMA((na, 7)), pltpu.SemaphoreType.DMA((na, 7)), pltpu.SemaphoreType.DMA((na,))],
                  start, finish)


def all_gather8(blocks, name):
    return run_hosted(gather8(blocks), name)


def chips3(arrays):
    na = len(arrays)

    def copies(ins, outs, sems):
        send_sems, recv_sems = sems
        x, y, c = _place()
        return [pltpu.make_async_remote_copy(
            src_ref=ins[a].at[2 * px + py], dst_ref=outs[a].at[k], send_sem=send_sems.at[a, k],
            recv_sem=recv_sems.at[a, k], device_id=(px, py, c), device_id_type=MESH)
            for a in range(na) for k, (px, py) in enumerate([(1 - x, y), (x, 1 - y), (1 - x, 1 - y)])]

    def start(ins, outs, sems):
        for cp in copies(ins, outs, sems):
            cp.start()

    def finish(ins, outs, sems):
        for cp in copies(ins, outs, sems):
            cp.wait()

    return Hosted(list(arrays), [jax.ShapeDtypeStruct((3,) + a.shape[1:], a.dtype) for a in arrays],
                  [pltpu.SemaphoreType.DMA((na, 3)), pltpu.SemaphoreType.DMA((na, 3))], start, finish)


def sibling_exchange(arrays, name):
    na = len(arrays)

    def body(*refs):
        ins, outs = refs[:na], refs[na:2 * na]
        send_sems, recv_sems = refs[2 * na:]
        x, y, c = _place()
        cps = [pltpu.make_async_remote_copy(
            src_ref=ins[a], dst_ref=outs[a], send_sem=send_sems.at[a], recv_sem=recv_sems.at[a],
            device_id=(x, y, 1 - c), device_id_type=MESH) for a in range(na)]
        for cp in cps:
            cp.start()
        for cp in cps:
            cp.wait()

    hbm = pl.BlockSpec(memory_space=pl.ANY)
    return pl.pallas_call(
        body, name=name, in_specs=[hbm] * na, out_specs=[hbm] * na,
        out_shape=[jax.ShapeDtypeStruct(a.shape, a.dtype) for a in arrays],
        scratch_shapes=[pltpu.SemaphoreType.DMA((na,)), pltpu.SemaphoreType.DMA((na,))],
    )(*arrays)


def siblings4(arrays):
    na = len(arrays)

    def copies(ins, outs, sems):
        send_sems, recv_sems = sems
        x, y, c = _place()
        return [pltpu.make_async_remote_copy(
            src_ref=ins[a].at[2 * j + 1 - c], dst_ref=outs[a].at[j],
            send_sem=send_sems.at[a, j], recv_sem=recv_sems.at[a, j],
            device_id=(x, y, 1 - c), device_id_type=MESH) for a in range(na) for j in range(4)]

    def start(ins, outs, sems):
        for cp in copies(ins, outs, sems):
            cp.start()

    def finish(ins, outs, sems):
        for cp in copies(ins, outs, sems):
            cp.wait()

    return Hosted(list(arrays), [jax.ShapeDtypeStruct((4,) + a.shape[1:], a.dtype) for a in arrays],
                  [pltpu.SemaphoreType.DMA((na, 4)), pltpu.SemaphoreType.DMA((na, 4))], start, finish)


def sibling_blocks(arrays, name):
    return run_hosted(siblings4(arrays), name)


def _row_tile(r):
    for cand in (512, 256, 128, 64, 32, 16, 8):
        if r % cand == 0:
            return cand
    return r


def chip_partial(place, g8, landed4, name):
    _, r, ccols = g8.shape
    tr = _row_tile(r)

    def body(place_ref, g_ref, l_ref, o_ref):
        del place_ref
        o_ref[...] = (g_ref[...].astype(F32) + l_ref[...].astype(F32)).astype(BF16)

    spec = pl.BlockSpec((None, tr, ccols), lambda j, i, s: (j, i, 0))
    return pl.pallas_call(
        body, name=name,
        grid_spec=pltpu.PrefetchScalarGridSpec(
            num_scalar_prefetch=1, grid=(4, r // tr),
            in_specs=[pl.BlockSpec((None, tr, ccols), lambda j, i, s: (2 * j + s[0], i, 0)), spec], out_specs=spec),
        out_shape=jax.ShapeDtypeStruct((4, r, ccols), BF16),
    )(place, g8, landed4)


def shard_sum(place, partial4, landed3, name):
    _, r, ccols = partial4.shape
    tr = _row_tile(r)

    def body(place_ref, p_ref, l_ref, o_ref):
        del place_ref
        acc = p_ref[...].astype(F32)
        for k in range(3):
            acc = acc + l_ref[k].astype(F32)
        o_ref[...] = acc

    return pl.pallas_call(
        body, name=name,
        grid_spec=pltpu.PrefetchScalarGridSpec(
            num_scalar_prefetch=1, grid=(r // tr,),
            in_specs=[pl.BlockSpec((None, tr, ccols), lambda i, s: (s[1], i, 0)),
                      pl.BlockSpec((3, tr, ccols), lambda i, s: (0, i, 0))],
            out_specs=pl.BlockSpec((tr, ccols), lambda i, s: (i, 0))),
        out_shape=jax.ShapeDtypeStruct((r, ccols), F32),
    )(place, partial4, landed3)


def _adamw_math(w, g, m, v):
    m2 = B1 * m + (1.0 - B1) * g
    v2 = B2 * v + (1.0 - B2) * (g * g)
    m_hat = m2 / (1.0 - B1 ** STEP)
    v_hat = v2 / (1.0 - B2 ** STEP)
    return -LR * (m_hat / (jnp.sqrt(v_hat) + AEPS) + WD * w), m2, v2


def adamw_halves(place, w, mine, theirs, m, v, name):
    r, ccols = w.shape
    hr = r // 2
    tr = _row_tile(hr)
    nt = hr // tr

    def body(place_ref, w_ref, a_ref, b_ref, m_ref, v_ref, g_out, d_out, m_out, v_out):
        g = jnp.where(pl.program_id(0) == place_ref[0], a_ref[...], b_ref[...])
        d, m2, v2 = _adamw_math(w_ref[...], g, m_ref[...], v_ref[...])
        g_out[...] = g
        d_out[...] = d
        m_out[...] = m2
        v_out[...] = v2

    full = pl.BlockSpec((tr, ccols), lambda h, i, s: (h * nt + i, 0))
    part = pl.BlockSpec((tr, ccols), lambda h, i, s: (i, 0))
    return pl.pallas_call(
        body, name=name,
        grid_spec=pltpu.PrefetchScalarGridSpec(
            num_scalar_prefetch=1, grid=(2, nt), in_specs=[full, part, part, full, full], out_specs=[full] * 4),
        out_shape=[jax.ShapeDtypeStruct((r, ccols), F32)] * 4,
    )(place, w, mine, theirs, m, v)


def adamw_group(place, halved, plain, hosted, name):
    rows = halved[0][0].shape[0]
    tr = 64
    nt = rows // 2 // tr
    nh, npl = len(halved), len(plain)

    def body(place_ref, *refs):
        own_in, h_in, own_out, h_out, _, h_sems = hosted.split(refs, 5 * nh + 4 * npl, 4 * nh + 3 * npl)
        half = pl.program_id(0)
        grid_step = half * nt + pl.program_id(1)

        @pl.when(grid_step == 0)
        def _():
            hosted.start(h_in, h_out, h_sems)

        for i in range(nh):
            w_ref, a_ref, b_ref, m_ref, v_ref = own_in[5 * i:5 * i + 5]
            g = jnp.where(half == place_ref[0], a_ref[...], b_ref[...])
            res = (g,) + _adamw_math(w_ref[...], g, m_ref[...], v_ref[...])
            for o_ref, r in zip(own_out[4 * i:4 * i + 4], res):
                o_ref[...] = r
        for i in range(npl):
            w_ref, g_ref, m_ref, v_ref = own_in[5 * nh + 4 * i:5 * nh + 4 * i + 4]
            res = _adamw_math(w_ref[...], g_ref[...], m_ref[...], v_ref[...])
            for o_ref, r in zip(own_out[4 * nh + 3 * i:4 * nh + 3 * i + 3], res):
                o_ref[...] = r

        @pl.when(grid_step == 2 * nt - 1)
        def _():
            hosted.finish(h_in, h_out, h_sems)

    def full(cols):
        return pl.BlockSpec((tr, cols), lambda h, i, s: (h * nt + i, 0))

    def part(cols):
        return pl.BlockSpec((tr, cols), lambda h, i, s: (i, 0))

    in_specs, out_specs, out_shape, args = [], [], [], []
    for w, a, b, m, v in halved:
        cols = w.shape[1]
        in_specs += [full(cols), part(cols), part(cols), full(cols), full(cols)]
        out_specs += [full(cols)] * 4
        out_shape += [jax.ShapeDtypeStruct(w.shape, F32)] * 4
        args += [w, a, b, m, v]
    for w, g, m, v in plain:
        cols = w.shape[1]
        in_specs += [full(cols)] * 4
        out_specs += [full(cols)] * 3
        out_shape += [jax.ShapeDtypeStruct(w.shape, F32)] * 3
        args += [w, g, m, v]
    h_in_specs, h_out_specs = hosted.specs()
    return pl.pallas_call(
        body, name=name,
        grid_spec=pltpu.PrefetchScalarGridSpec(
            num_scalar_prefetch=1, grid=(2, nt), in_specs=in_specs + h_in_specs, out_specs=out_specs + h_out_specs,
            scratch_shapes=hosted.scratch),
        out_shape=out_shape + hosted.out_shape,
        compiler_params=_cp(("arbitrary", "arbitrary")),
    )(place, *args, *hosted.args)


def _silu(x):
    return x * jax.nn.sigmoid(x)


def mod_shard(cin, w_ada, b_shard):
    def body(c_ref, w_ref, b_ref, o_ref):
        o_ref[...] = _nn(_silu(c_ref[...]), w_ref[...]) + b_ref[...]

    return pl.pallas_call(
        body, name="mod_shard", grid=(3,),
        in_specs=[pl.BlockSpec((32, D), lambda j: (0, 0)), pl.BlockSpec((D, 512), lambda j: (0, j)),
                  pl.BlockSpec((1, 512), lambda j: (0, j))],
        out_specs=pl.BlockSpec((32, 512), lambda j: (0, j)),
        out_shape=jax.ShapeDtypeStruct((32, 1536), F32),
    )(cin, w_ada, b_shard)


def ada_grads(cin, gb, gc, w_ada):
    def body(c_ref, gb_ref, gc_ref, w_ref, gw_ref, pc_ref):
        ctx_tot = jnp.sum(gc_ref[...], axis=0, keepdims=True)
        rows = lax.broadcasted_iota(jnp.int32, (16, 512), 0)
        dm = jnp.concatenate([gb_ref[...], jnp.where(rows == 0, ctx_tot, 0.0)], axis=0)
        gw_ref[...] = _tn(_silu(c_ref[...]), dm)
        rows8 = lax.broadcasted_iota(jnp.int32, (8, 512), 0)
        part = _nt(jnp.where(rows8 == 0, ctx_tot, 0.0), w_ref[...])

        @pl.when(pl.program_id(0) == 0)
        def _():
            pc_ref[...] = jnp.zeros_like(pc_ref)

        pc_ref[...] += part

    return pl.pallas_call(
        body, name="ada_grads", grid=(3,),
        in_specs=[pl.BlockSpec((32, D), lambda j: (0, 0)), pl.BlockSpec((16, 512), lambda j: (0, j)),
                  pl.BlockSpec((8, 512), lambda j: (0, j)), pl.BlockSpec((D, 512), lambda j: (0, j))],
        out_specs=[pl.BlockSpec((D, 512), lambda j: (0, j)), pl.BlockSpec((8, D), lambda j: (0, 0))],
        out_shape=[jax.ShapeDtypeStruct((D, 1536), F32), jax.ShapeDtypeStruct((8, D), F32)],
    )(cin, gb, gc, w_ada)


SMALL_SUM_ROWS = 15


def small_update(gsm, gbf, gcf, pcg, params):
    n = len(params)

    def body(*refs):
        gsm_ref, gbf_ref, gcf_ref, pcg_ref = refs[:4]
        wmv, outs, loss_out = refs[4:4 + 3 * n], refs[4 + 3 * n:4 + 7 * n], refs[-1]
        acc = gsm_ref[0]
        for dev in range(1, 8):
            acc = acc + gsm_ref[dev]
        c_ctx = wmv[0][...]
        sg = jax.nn.sigmoid(c_ctx)
        dsilu = pcg_ref[0:1, :] + pcg_ref[2:3, :] + pcg_ref[4:5, :] + pcg_ref[6:7, :]
        lane = lax.broadcasted_iota(jnp.int32, (1, D), 1)
        last = acc[14:15, :]
        grads = [
            dsilu * (sg * (1.0 + c_ctx * (1.0 - sg))),
            jnp.sum(gbf_ref[...], axis=0, keepdims=True) + jnp.sum(gcf_ref[...], axis=0, keepdims=True),
            acc[0:1, :] + acc[1:2, :], acc[2:3, :], acc[3:4, :], acc[4:5, :],
            acc[5:6, 0:512], acc[6:14, :], jnp.where(lane < 8, last, 0.0),
        ]
        loss_out[...] = jnp.broadcast_to(jnp.sum(jnp.where(lane == 8, last, 0.0), axis=1, keepdims=True), (8, 128))
        for i, g in enumerate(grads):
            d, m2, v2 = _adamw_math(wmv[3 * i][...], g, wmv[3 * i + 1][...], wmv[3 * i + 2][...])
            outs[4 * i][...] = g
            outs[4 * i + 1][...] = d
            outs[4 * i + 2][...] = m2
            outs[4 * i + 3][...] = v2

    flat = [a for wmv in params for a in wmv]
    out_shape = [jax.ShapeDtypeStruct(w.shape, F32) for w, _, _ in params for _ in range(4)]
    return pl.pallas_call(
        body, name="small_update", out_shape=out_shape + [jax.ShapeDtypeStruct((8, 128), F32)],
    )(gsm, gbf, gcf, pcg, *flat)


def _pad_row(v, rows):
    flat = v.reshape(-1)
    return jnp.pad(flat, (0, rows * D - flat.shape[0])).reshape(rows, D)


def local_step(x, ctx, tgt, mod3, g_pre_mix, g_post_mix, g_pre_mlp, g_post_mlp, ret_decay, ret_gn, na_rpb,
               wperm, late_weights, early_grads):
    nb = x.shape[0]
    tokens = nb * SEQ
    cos, sin = _rope_tables()
    rd = ret_decay.T.reshape(RH, 2, 1)
    gn = ret_gn.reshape(RH, 1, RD)
    bias = na_bias_table(_rpb_flat(na_rpb))
    h, pret, pna = premix_proj(x, mod3, g_pre_mix, wperm, False, "premix_proj")
    hc, pretc, pnac = premix_proj(ctx, mod3, g_pre_mix, wperm, True, "premix_proj_ctx")
    o_all, mixin, gw_out = retention_fwd(pret, pretc, rd, gn, cos, sin, late_weights(0))
    mixin, gw1, gw2 = na_fwd(pna, pnac, bias, mixin, late_weights(1))
    dx_tail, dmix, h2, du, act, dm, dmixin, dmod_t, dg_t, loss_t = tail_fwd_bwd(
        x, mixin, tgt, mod3, g_post_mix, g_pre_mlp, g_post_mlp, gw_out.reshape(D, D), gw1.reshape(4, D, D),
        gw2.reshape(DFF, D))
    dw_out = weight_grad([(mixin.reshape(tokens, D), dmix.reshape(tokens, D))], "grad_w_out", BF16)
    dw1 = weight_grad([(h2.reshape(tokens, D), du.reshape(tokens, DFF))], "grad_w_mlp1", BF16, col_blocks=True)
    dw2 = weight_grad([(act.reshape(tokens, DFF), dm.reshape(tokens, D))], "grad_w_mlp2", BF16)
    dproj, dprojc, drd, dgn, *landed = retention_bwd(pret, pretc, o_all, dmixin, rd, gn, cos, sin,
                                                     early_grads[0](dw_out, dw1, dw2))
    dproj, dprojc, dpat, *early = na_bwd(pna, pnac, bias, dmixin, dproj, dprojc, early_grads[1](landed))
    dw_in = weight_grad([(h.reshape(tokens, D), dproj.reshape(tokens, IN_W)),
                         (hc.reshape(nb * LC, D), dprojc.reshape(nb * LC, IN_W))], "grad_w_in", tn=IN_W // 2, tk=1024)
    grad_x, dmod_a, dg_a, *late = premix_bwd(x, mod3, g_pre_mix, wperm, dproj, dx_tail, early_grads[2](dw_in),
                                             "premix_bwd")
    dmod_c, dg_c = premix_bwd(ctx, mod3, g_pre_mix, wperm, dprojc, None, no_exchange(), "premix_bwd_ctx")
    dmod = jnp.concatenate([jnp.concatenate([dmod_a[:, 0:2], dmod_t[:, 2:6]], axis=1), dmod_c], axis=0)
    last = jnp.pad(jnp.concatenate([drd[:, :, 0].T.reshape(8), loss_t[0, 0:1]]), (0, D - 9)).reshape(1, D)
    small = jnp.concatenate([dg_a[0:1], dg_c[0:1], dg_t[0:3], _pad_row(dgn, 1), dpat.reshape(8, D), last], axis=0)
    return grad_x, late, early, dmod, small


def kernel(x, c, ctx, c_ctx, w_ada, b_ada, g_pre_mix, g_post_mix, g_pre_mlp, g_post_mlp, w_in, ret_decay, ret_gn, na_rpb, w_out, w_mlp1, w_mlp2, loss_target, m_c_ctx, m_w_ada, m_b_ada, m_g_pre_mix, m_g_post_mix, m_g_pre_mlp, m_g_post_mlp, m_w_in, m_ret_decay, m_ret_gn, m_na_rpb, m_w_out, m_w_mlp1, m_w_mlp2, v_c_ctx, v_w_ada, v_b_ada, v_g_pre_mix, v_g_post_mix, v_g_pre_mlp, v_g_post_mlp, v_w_in, v_ret_decay, v_ret_gn, v_na_rpb, v_w_out, v_w_mlp1, v_w_mlp2):
    px, py, pc = _place()
    dev = 4 * px + 2 * py + pc
    chip = 2 * px + py

    def my_half(w2d):
        rows = w2d.shape[0] // 2
        return lax.dynamic_slice_in_dim(w2d, pc * rows, rows, 0)

    halves = [my_half(w[0]).astype(BF16) for w in (w_in, w_out, w_mlp1, w_mlp2)]
    gw_in, cg = all_gather8([halves[0], jnp.pad(c, ((0, 6), (0, 0)))], "gather_w_in")
    wperm = unpack_w_in(gw_in.reshape(4, D, 896))

    cin = jnp.pad(cg[:, 0:2].reshape(16, D), ((0, 16), (0, 0))) + jnp.pad(c_ctx[None], ((16, 15), (0, 0)))
    mod_mine = mod_shard(cin, w_ada[0], lax.dynamic_slice_in_dim(b_ada, chip * 1536, 1536, 1))
    (mg,) = all_gather8([mod_mine], "gather_mod")
    mod_all = jnp.concatenate([mg[0], mg[2], mg[4], mg[6]], axis=1)
    mod3 = (jnp.pad(lax.dynamic_slice_in_dim(mod_all, 2 * dev, 2, 0), ((0, 1), (0, 0)))
            + jnp.pad(mod_all[16:17], ((2, 0), (0, 0)))).reshape(3, 6, D)

    place = jnp.stack([pc, chip]).astype(jnp.int32)

    early_names = ["w_out", "w_mlp1", "w_mlp2"]
    early_g8, early_partial = [], []

    def early_a(dw_out, dw1, dw2):
        early_g8[:] = [dw_out.reshape(8, 128, D), dw1.reshape(8, 512, D), dw2.reshape(8, 512, D)]
        return siblings4(early_g8)

    def early_b(landed):
        early_partial[:] = [chip_partial(place, g, l, "rs_chip_sum_" + n)
                            for g, l, n in zip(early_g8, landed, early_names)]
        return chips3(early_partial)

    late_partial = []

    def late_c(dw_in):
        g8_in = pack_w_in(dw_in).reshape(8, 512, 896)
        (landed_in,) = sibling_blocks([g8_in], "rs_sibling_w_in")
        late_partial[:] = [chip_partial(place, g8_in, landed_in, "rs_chip_sum_w_in")]
        return chips3(late_partial)

    grad_x, (landed3_in,), early_landed, dmod, small = local_step(
        x, ctx, loss_target, mod3, g_pre_mix, g_post_mix, g_pre_mlp, g_post_mlp, ret_decay[0], ret_gn, na_rpb[0],
        wperm, lambda k: gather8(halves[1:2] if k == 0 else halves[2:4]), (early_a, early_b, late_c))
    early_mine = [shard_sum(place, p, l, "rs_shard_sum_" + n)
                  for p, l, n in zip(early_partial, early_landed, early_names)]
    early_theirs = sibling_exchange(early_mine, "rs_halves_early")

    pay = jnp.concatenate([dmod.reshape(18, D), small, jnp.zeros((40 - 18 - SMALL_SUM_ROWS, D), F32)], axis=0)
    (gs,) = all_gather8([pay], "gather_small")
    gbf = gs[:, 0:12].reshape(16, 6 * D)
    gcf = gs[:, 12:18].reshape(8, 6 * D)
    gw_ada, pc_part = ada_grads(cin, lax.dynamic_slice_in_dim(gbf, chip * 1536, 1536, 1),
                                lax.dynamic_slice_in_dim(gcf, chip * 1536, 1536, 1), w_ada[0])
    (pcg,) = all_gather8([pc_part], "gather_c_ctx")

    grouped = adamw_group(
        place,
        [(w_mlp1[0], early_mine[1], early_theirs[1], m_w_mlp1[0], v_w_mlp1[0]),
         (w_mlp2[0], early_mine[2], early_theirs[2], m_w_mlp2[0], v_w_mlp2[0])],
        [(w_ada[0], gw_ada, m_w_ada[0], v_w_ada[0])], no_exchange(), "adamw_group")
    d_ada, m_ada, v_ada = grouped[8:11]
    mine_in = shard_sum(place, late_partial[0], landed3_in, "rs_shard_sum_w_in")
    (theirs_in,) = sibling_exchange([mine_in], "rs_halves_w_in")
    big = [
        [r[None] for r in adamw_halves(place, w_in[0], mine_in, theirs_in, m_w_in[0], v_w_in[0], "adamw_w_in")],
        [r[None] for r in adamw_halves(place, w_out[0], early_mine[0], early_theirs[0], m_w_out[0], v_w_out[0],
                                       "adamw_w_out")],
        [r[None] for r in grouped[0:4]], [r[None] for r in grouped[4:8]],
    ]

    def rpb_rows(t):
        return _rpb_flat(t[0]).reshape(8, D)

    def decay_row(t):
        return jnp.pad(t.reshape(1, 8), ((0, 0), (0, D - 8)))

    views = [lambda t: t.reshape(1, D), lambda t: t, lambda t: t, lambda t: t, lambda t: t, lambda t: t, lambda t: t,
             rpb_rows, decay_row]
    back = [lambda t: t.reshape(D), lambda t: t, lambda t: t, lambda t: t, lambda t: t, lambda t: t, lambda t: t,
            lambda t: _rpb_flat_t(t)[None], lambda t: t[:, 0:8].reshape(1, 2, 4)]
    small_w = (c_ctx, b_ada, g_pre_mix, g_post_mix, g_pre_mlp, g_post_mlp, ret_gn, na_rpb, ret_decay)
    small_m = (m_c_ctx, m_b_ada, m_g_pre_mix, m_g_post_mix, m_g_pre_mlp, m_g_post_mlp, m_ret_gn, m_na_rpb, m_ret_decay)
    small_v = (v_c_ctx, v_b_ada, v_g_pre_mix, v_g_post_mix, v_g_pre_mlp, v_g_post_mlp, v_ret_gn, v_na_rpb, v_ret_decay)
    *res, loss8 = small_update(gs[:, 18:18 + SMALL_SUM_ROWS], gbf, gcf, pcg[:, 0],
                               [(f(w), f(m), f(v)) for f, w, m, v in zip(views, small_w, small_m, small_v)])

    def leaves(ada, idx):
        s_c, s_b, s_g1, s_g2, s_g3, s_g4, s_gn, s_rpb, s_rd = [back[i](res[4 * i + idx]) for i in range(9)]
        return [s_c, ada[None], s_b, s_g1, s_g2, s_g3, s_g4, big[0][idx], s_rd, s_gn, s_rpb,
                big[1][idx], big[2][idx], big[3][idx]]

    return (loss8[0, 0], grad_x, *leaves(gw_ada, 0), *leaves(d_ada, 1), *leaves(m_ada, 2), *leaves(v_ada, 3))
```

```python
import functools

import jax
import jax.numpy as jnp
from jax import lax
from jax.experimental import pallas as pl
from jax.experimental.pallas import tpu as pltpu

F32, BF16 = jnp.float32, jnp.bfloat16
D = 1024
SEQ = 2048
LC = 256
GW = 64
RH, RD, CH = 4, 128, 128
NPAIR = 4
IN_W = 3584
RET_W = 2048
DFF = 4096
EPS = 1e-6
NEG = -1e30
TN = 256
NCH = SEQ // CH
LR, B1, B2, AEPS, WD, STEP = 0.001, 0.9, 0.999, 1e-08, 0.01, 10
MESH = pl.DeviceIdType.MESH
VMEM_LIMIT = 56 * 1024 * 1024


def _cp(sem=None):
    return pltpu.CompilerParams(dimension_semantics=sem, vmem_limit_bytes=VMEM_LIMIT)


def _nn(a, b):
    return jnp.dot(a.astype(BF16), b.astype(BF16), preferred_element_type=F32)


def _nt(a, b):
    return lax.dot_general(a.astype(BF16), b.astype(BF16), (((1,), (1,)), ((), ())), preferred_element_type=F32)


def _tn(a, b):
    return lax.dot_general(a.astype(BF16), b.astype(BF16), (((0,), (0,)), ((), ())), preferred_element_type=F32)


@jax.custom_vjp
def mm_nn(a, b):
    return _nn(a, b)


@jax.custom_vjp
def mm_nt(a, b):
    return _nt(a, b)


@jax.custom_vjp
def mm_tn(a, b):
    return _tn(a, b)


mm_nn.defvjp(lambda a, b: (_nn(a, b), (a, b)), lambda r, g: (_nt(g, r[1]), _tn(r[0], g)))
mm_nt.defvjp(lambda a, b: (_nt(a, b), (a, b)), lambda r, g: (_nn(g, r[1]), _tn(g, r[0])))
mm_tn.defvjp(lambda a, b: (_tn(a, b), (a, b)), lambda r, g: (_nt(r[1], g), _nn(r[0], g)))


def _rms(x):
    return x * lax.rsqrt(jnp.mean(x * x, axis=-1, keepdims=True) + EPS)


def _rms_mod(x, g, sc, sh):
    return (_rms(x) * g) * (1.0 + sc) + sh


def _post_mix(x, mix, gt1, sc2, sh2, g_post_mix, g_pre_mlp):
    x1 = x + gt1 * (_rms(mix) * g_post_mix)
    return x1, _rms_mod(x1, g_pre_mlp, sc2, sh2)


def _head_loss(x1, m, gt2, g_post_mlp, tgt):
    err = x1 + gt2 * (_rms(m) * g_post_mlp) - tgt
    return 0.5 * jnp.sum(jnp.mean(err * err, axis=-1, keepdims=True), axis=0, keepdims=True)


def _ln_gate(o, g, w):
    mu = jnp.mean(o, axis=-1, keepdims=True)
    var = jnp.mean(jnp.square(o - mu), axis=-1, keepdims=True)
    y = (o - mu) * lax.rsqrt(var + EPS)
    return (y * w) * (g * jax.nn.sigmoid(g))


def _swap32(x):
    lane = lax.broadcasted_iota(jnp.int32, x.shape, 1)
    return jnp.where((lane & 32) == 0, pltpu.roll(x, 96, 1), pltpu.roll(x, 32, 1))


def _rope(x, cos, sin):
    return x * cos + _swap32(x) * sin


def _rope_t(g, cos, sin):
    return g * cos + _swap32(g * sin)


def _rope_tables():
    tok = jnp.arange(SEQ)
    pos_r = (tok // GW).astype(F32)
    pos_c = (tok % GW).astype(F32)
    inv = 10000.0 ** (-jnp.arange(32, dtype=F32) / 32)
    ar = pos_r[:, None] * inv[None, :]
    ac = pos_c[:, None] * inv[None, :]
    cos = jnp.concatenate([jnp.cos(ar), jnp.cos(ar), jnp.cos(ac), jnp.cos(ac)], axis=-1)
    sin = jnp.concatenate([-jnp.sin(ar), jnp.sin(ar), -jnp.sin(ac), jnp.sin(ac)], axis=-1)
    return cos, sin


def _chunk_loop(n, body, init, k=4):
    def several(t, carry):
        for i in range(k):
            carry = body(k * t + i, carry)
        return carry

    return lax.fori_loop(0, n // k, several, init)


def _fiota(shape, dim):
    return lax.broadcasted_iota(jnp.int32, shape, dim).astype(F32)


def _ret_state(k, v, s, lg, reverse):
    pos = _fiota((CH, 1), 0)
    b_exp = pos if reverse else (CH - 1.0 - pos)
    return jnp.exp(lg * CH) * s + mm_tn(k * jnp.exp(lg * b_exp), v)


def _ret_chunk(q, k, v, s, lg, reverse):
    i = _fiota((CH, CH), 0)
    j = _fiota((CH, CH), 1)
    diff = (j - i) if reverse else (i - j)
    mask = (diff > 0) if reverse else (diff >= 0)
    decay = jnp.where(mask, jnp.exp(lg * jnp.where(mask, diff, 0.0)), 0.0)
    pos = _fiota((CH, 1), 0)
    a_exp = (CH - pos) if reverse else (pos + 1.0)
    o = mm_nn(mm_nt(q, k) * decay, v) + mm_nn(q * jnp.exp(lg * a_exp), s)
    return o, _ret_state(k, v, s, lg, reverse)


def premix_proj(xin, mod3, g_pre, wperm, is_ctx, name):
    nb, length, _ = xin.shape
    tn = min(TN, length)

    def body(x_ref, mod_ref, g_ref, w_ref, h_ref, pret_ref, pna_ref):
        h = _rms_mod(x_ref[...], g_ref[...], mod_ref[1:2, :], mod_ref[0:1, :])
        hb = h.astype(BF16)
        h_ref[...] = hb
        pret_ref[...] = jnp.dot(hb, w_ref[:, :RET_W], preferred_element_type=F32)
        pna_ref[...] = jnp.dot(hb, w_ref[:, RET_W:], preferred_element_type=F32).astype(BF16)

    return pl.pallas_call(
        body, name=name, grid=(nb, length // tn),
        in_specs=[
            pl.BlockSpec((None, tn, D), lambda b, t: (b, t, 0)),
            pl.BlockSpec((None, 6, D), (lambda b, t: (2, 0, 0)) if is_ctx else (lambda b, t: (b, 0, 0))),
            pl.BlockSpec((1, D), lambda b, t: (0, 0)),
            pl.BlockSpec((D, IN_W), lambda b, t: (0, 0), pipeline_mode=pl.Buffered(1)),
        ],
        out_specs=[
            pl.BlockSpec((None, tn, D), lambda b, t: (b, t, 0)),
            pl.BlockSpec((None, tn, RET_W), lambda b, t: (b, t, 0)),
            pl.BlockSpec((None, tn, IN_W - RET_W), lambda b, t: (b, t, 0)),
        ],
        out_shape=[
            jax.ShapeDtypeStruct((nb, length, D), BF16),
            jax.ShapeDtypeStruct((nb, length, RET_W), F32),
            jax.ShapeDtypeStruct((nb, length, IN_W - RET_W), BF16),
        ],
        compiler_params=_cp(("arbitrary", "arbitrary")),
    )(xin, mod3, g_pre, wperm)


def premix_bwd(xin, mod3, g_pre, wperm, dproj, dx_tail, hosted, name):
    nb, length, _ = xin.shape
    tn = min(TN, length)
    is_ctx = dx_tail is None

    def body(*refs):
        own_in, h_in, own_out, h_out, _, h_sems = hosted.split(refs, 5 if is_ctx else 6, 2 if is_ctx else 3)
        if is_ctx:
            (x_ref, mod_ref, g_ref, w_ref, dp_ref), (dmod_ref, dg_ref) = own_in, own_out
        else:
            (x_ref, mod_ref, g_ref, w_ref, dp_ref, dxt_ref), (dx_ref, dmod_ref, dg_ref) = own_in, own_out
        b, t = pl.program_id(0), pl.program_id(1)
        grid_step = b * (length // tn) + t

        @pl.when(grid_step == 0)
        def _():
            hosted.start(h_in, h_out, h_sems)

        @pl.when(grid_step == nb * (length // tn) - 1)
        def _():
            hosted.finish(h_in, h_out, h_sems)

        dh = lax.dot_general(dp_ref[...], w_ref[...], (((1,), (1,)), ((), ())), preferred_element_type=F32)
        _, vjp = jax.vjp(_rms_mod, x_ref[...], g_ref[...], mod_ref[1:2, :], mod_ref[0:1, :])
        dx, dg, dsc, dsh = vjp(dh)
        if not is_ctx:
            dx_ref[...] = dx + dxt_ref[...]

        @pl.when((t == 0) & ((b == 0) if is_ctx else True))
        def _():
            dmod_ref[...] = jnp.zeros_like(dmod_ref)

        @pl.when((t == 0) & (b == 0))
        def _():
            dg_ref[...] = jnp.zeros_like(dg_ref)

        dmod_ref[0:1, :] += dsh
        dmod_ref[1:2, :] += dsc
        dg_ref[0:1, :] += dg

    tok = lambda b, t: (b, t, 0)
    in_specs = [
        pl.BlockSpec((None, tn, D), tok),
        pl.BlockSpec((None, 6, D), (lambda b, t: (2, 0, 0)) if is_ctx else (lambda b, t: (b, 0, 0))),
        pl.BlockSpec((1, D), lambda b, t: (0, 0)),
        pl.BlockSpec((D, IN_W), lambda b, t: (0, 0), pipeline_mode=pl.Buffered(1)),
        pl.BlockSpec((None, tn, IN_W), tok),
    ]
    args = [xin, mod3, g_pre, wperm, dproj]
    out_specs = [
        pl.BlockSpec((None, 6, D), (lambda b, t: (0, 0, 0)) if is_ctx else (lambda b, t: (b, 0, 0))),
        pl.BlockSpec((8, D), lambda b, t: (0, 0)),
    ]
    out_shape = [jax.ShapeDtypeStruct((1 if is_ctx else nb, 6, D), F32), jax.ShapeDtypeStruct((8, D), F32)]
    if not is_ctx:
        in_specs.append(pl.BlockSpec((None, tn, D), tok))
        args.append(dx_tail)
        out_specs.insert(0, pl.BlockSpec((None, tn, D), tok))
        out_shape.insert(0, jax.ShapeDtypeStruct((nb, length, D), F32))
    h_in_specs, h_out_specs = hosted.specs()
    return pl.pallas_call(
        body, name=name, grid=(nb, length // tn), in_specs=in_specs + h_in_specs, out_specs=out_specs + h_out_specs,
        out_shape=out_shape + hosted.out_shape, scratch_shapes=hosted.scratch,
        compiler_params=_cp(("arbitrary", "arbitrary")),
    )(*args, *hosted.args)


def _ret_specs(order):
    def im(f):
        return lambda *g: f(*order(*g))
    return dict(
        pret=pl.BlockSpec((None, SEQ, 512), im(lambda b, h: (b, 0, h))),
        pretc=pl.BlockSpec((None, LC, 512), im(lambda b, h: (b, 0, h))),
        rd=pl.BlockSpec((None, 2, 1), im(lambda b, h: (h, 0, 0))),
        gn=pl.BlockSpec((None, 1, RD), im(lambda b, h: (h, 0, 0))),
        tab=pl.BlockSpec((SEQ, RD), im(lambda b, h: (0, 0))),
        head=pl.BlockSpec((None, SEQ, RD), im(lambda b, h: (b, 0, h))),
    )


def retention_fwd(pret, pretc, rd, gn, cos, sin, hosted):
    nb = pret.shape[0]
    sp = _ret_specs(lambda b, h: (b, h))

    def body(*refs):
        own_in, h_in, own_out, h_out, own_scr, h_sems = hosted.split(refs, 6, 2)
        p_ref, pc_ref, rd_ref, gn_ref, cos_ref, sin_ref = own_in
        (o_ref, mix_ref), (q_s, k_s, of_s, ob_s) = own_out, own_scr
        grid_step = pl.program_id(0) * RH + pl.program_id(1)

        @pl.when(grid_step == 0)
        def _():
            hosted.start(h_in, h_out, h_sems)

        cos_v, sin_v = cos_ref[...], sin_ref[...]
        q_s[...] = _rope(p_ref[:, 0:128], cos_v, sin_v) * (RD ** -0.5)
        k_s[...] = _rope(p_ref[:, 128:256], cos_v, sin_v)
        lgs, init = [], []
        for rev in (False, True):
            lg = jax.nn.log_sigmoid(rd_ref[int(rev):int(rev) + 1, :])
            s = jnp.zeros((RD, RD), F32)
            for n in ((1, 0) if rev else (0, 1)):
                s = _ret_state(pc_ref[n * CH:(n + 1) * CH, 128:256], pc_ref[n * CH:(n + 1) * CH, 256:384], s, lg, rev)
            lgs.append(lg)
            init.append(s)

        def step(t, carry):
            out = []
            for rev, o_s, s in ((False, of_s, carry[0]), (True, ob_s, carry[1])):
                n = (NCH - 1 - t) if rev else t
                sl = pl.ds(pl.multiple_of(n * CH, CH), CH)
                o, s2 = _ret_chunk(q_s[sl, :], k_s[sl, :], p_ref[sl, 256:384], s, lgs[int(rev)], rev)
                o_s[sl, :] = o
                out.append(s2)
            return tuple(out)

        _chunk_loop(NCH, step, tuple(init))
        o = of_s[...] + ob_s[...]
        o_ref[...] = o
        mix_ref[...] = _ln_gate(o, p_ref[:, 384:512], gn_ref[...]).astype(BF16)

        @pl.when(grid_step == nb * RH - 1)
        def _():
            hosted.finish(h_in, h_out, h_sems)

    h_in_specs, h_out_specs = hosted.specs()
    return pl.pallas_call(
        body, name="retention_fwd", grid=(nb, RH),
        in_specs=[sp["pret"], sp["pretc"], sp["rd"], sp["gn"], sp["tab"], sp["tab"]] + h_in_specs,
        out_specs=[sp["head"], sp["head"]] + h_out_specs,
        out_shape=[jax.ShapeDtypeStruct((nb, SEQ, RH * RD), F32), jax.ShapeDtypeStruct((nb, SEQ, D), BF16)]
        + hosted.out_shape,
        scratch_shapes=[pltpu.VMEM((SEQ, RD), F32)] * 4 + hosted.scratch,
        compiler_params=_cp(("arbitrary", "arbitrary")),
    )(pret, pretc, rd, gn, cos, sin, *hosted.args)


def retention_bwd(pret, pretc, o_all, dmixin, rd, gn, cos, sin, hosted):
    nb = pret.shape[0]
    sp = _ret_specs(lambda h, b: (b, h))

    def body(*refs):
        own_in, h_in, own_out, h_out, own_scr, h_sems = hosted.split(refs, 8, 4)
        p_ref, pc_ref, o_ref, dmix_ref, rd_ref, gn_ref, cos_ref, sin_ref = own_in
        dp_ref, dpc_ref, drd_ref, dgn_ref = own_out
        q_s, k_s, do_s, dqf_s, dkf_s, dvf_s, dqb_s, dkb_s, dvb_s, stf_s, stb_s = own_scr
        b = pl.program_id(1)
        grid_step = pl.program_id(0) * nb + b

        @pl.when(grid_step == 0)
        def _():
            hosted.start(h_in, h_out, h_sems)

        cos_v, sin_v = cos_ref[...], sin_ref[...]
        q_s[...] = _rope(p_ref[:, 0:128], cos_v, sin_v) * (RD ** -0.5)
        k_s[...] = _rope(p_ref[:, 128:256], cos_v, sin_v)
        _, gate_vjp = jax.vjp(_ln_gate, o_ref[...], p_ref[:, 384:512], gn_ref[...])
        do, dg, dgn = gate_vjp(dmix_ref[...].astype(F32))
        do_s[...] = do
        dp_ref[:, 384:512] = dg.astype(BF16)

        @pl.when(b == 0)
        def _():
            drd_ref[...] = jnp.zeros_like(drd_ref)
            dgn_ref[...] = jnp.zeros_like(dgn_ref)

        dgn_ref[...] += dgn
        kcs = [pc_ref[n * CH:(n + 1) * CH, 128:256] for n in (0, 1)]
        vcs = [pc_ref[n * CH:(n + 1) * CH, 256:384] for n in (0, 1)]
        dirs = []
        init = []
        for rev in (False, True):
            rdv = rd_ref[int(rev):int(rev) + 1, :]
            lg = jax.nn.log_sigmoid(rdv)
            order_c = (1, 0) if rev else (0, 1)
            s = jnp.zeros((RD, RD), F32)
            ctx_states = []
            for n in order_c:
                ctx_states.append(s)
                s = _ret_state(kcs[n], vcs[n], s, lg, rev)
            dirs.append((rev, order_c, lg, rdv, ctx_states))
            init.append(s)
        acc = ((dqf_s, dkf_s, dvf_s, stf_s), (dqb_s, dkb_s, dvb_s, stb_s))

        def fstep(t, carry):
            out = []
            for (rev, _, lg, _, _), (_, _, _, st_s), s in zip(dirs, acc, carry):
                n = (NCH - 1 - t) if rev else t
                sl = pl.ds(pl.multiple_of(n * CH, CH), CH)
                st_s[n] = s
                out.append(_ret_state(k_s[sl, :], p_ref[sl, 256:384], s, lg, rev))
            return tuple(out)

        _chunk_loop(NCH, fstep, tuple(init))

        def bstep(t, carry):
            out = []
            for (rev, _, lg, _, _), (dq_s, dk_s, dv_s, st_s), (ds, dlg) in zip(dirs, acc, carry):
                n = t if rev else (NCH - 1 - t)
                sl = pl.ds(pl.multiple_of(n * CH, CH), CH)
                _, vjp = jax.vjp(functools.partial(_ret_chunk, reverse=rev),
                                 q_s[sl, :], k_s[sl, :], p_ref[sl, 256:384], st_s[n], lg)
                dq, dk, dv, ds_prev, dl = vjp((do_s[sl, :], ds))
                dq_s[sl, :] = dq
                dk_s[sl, :] = dk
                dv_s[sl, :] = dv
                out.append((ds_prev, dlg + dl))
            return tuple(out)

        zero_c = (jnp.zeros((RD, RD), F32), jnp.zeros((1, 1), F32))
        res = _chunk_loop(NCH, bstep, (zero_c, zero_c))
        dkc = [None, None]
        dvc = [None, None]
        for (rev, order_c, lg, rdv, ctx_states), (ds, dlg) in zip(dirs, res):
            for idx in (1, 0):
                n = order_c[idx]
                _, vjp = jax.vjp(functools.partial(_ret_state, reverse=rev), kcs[n], vcs[n], ctx_states[idx], lg)
                dk_c, dv_c, ds, dl = vjp(ds)
                dlg = dlg + dl
                dkc[n] = dk_c if dkc[n] is None else dkc[n] + dk_c
                dvc[n] = dv_c if dvc[n] is None else dvc[n] + dv_c
            drd_ref[int(rev):int(rev) + 1, :] += dlg * jax.nn.sigmoid(-rdv)
        dp_ref[:, 0:128] = _rope_t((dqf_s[...] + dqb_s[...]) * (RD ** -0.5), cos_v, sin_v).astype(BF16)
        dp_ref[:, 128:256] = _rope_t(dkf_s[...] + dkb_s[...], cos_v, sin_v).astype(BF16)
        dp_ref[:, 256:384] = (dvf_s[...] + dvb_s[...]).astype(BF16)
        zero = jnp.zeros((CH, RD), BF16)
        for n in (0, 1):
            rows = slice(n * CH, (n + 1) * CH)
            dpc_ref[rows, 0:128] = zero
            dpc_ref[rows, 128:256] = dkc[n].astype(BF16)
            dpc_ref[rows, 256:384] = dvc[n].astype(BF16)
            dpc_ref[rows, 384:512] = zero

        @pl.when(grid_step == RH * nb - 1)
        def _():
            hosted.finish(h_in, h_out, h_sems)

    h_in_specs, h_out_specs = hosted.specs()
    return pl.pallas_call(
        body, name="retention_bwd", grid=(RH, nb),
        in_specs=[sp["pret"], sp["pretc"], sp["head"], sp["head"], sp["rd"], sp["gn"], sp["tab"], sp["tab"]]
        + h_in_specs,
        out_specs=[
            pl.BlockSpec((None, SEQ, 512), lambda h, b: (b, 0, h)),
            pl.BlockSpec((None, LC, 512), lambda h, b: (b, 0, h)),
            pl.BlockSpec((None, 2, 1), lambda h, b: (h, 0, 0)),
            pl.BlockSpec((None, 1, RD), lambda h, b: (h, 0, 0)),
        ] + h_out_specs,
        out_shape=[
            jax.ShapeDtypeStruct((nb, SEQ, IN_W), BF16),
            jax.ShapeDtypeStruct((nb, LC, IN_W), BF16),
            jax.ShapeDtypeStruct((RH, 2, 1), F32),
            jax.ShapeDtypeStruct((RH, 1, RD), F32),
        ] + hosted.out_shape,
        scratch_shapes=[pltpu.VMEM((SEQ, RD), F32)] * 9 + [pltpu.VMEM((NCH, RD, RD), F32)] * 2 + hosted.scratch,
        compiler_params=_cp(("arbitrary", "arbitrary")),
    )(pret, pretc, o_all, dmixin, rd, gn, cos, sin, *hosted.args)


def _rpb_flat(rpb):
    return jnp.pad(rpb, ((0, 0), (0, 1), (0, 33))).reshape(NPAIR, 2, 1, 1024)


def _rpb_flat_t(dflat):
    return dflat.reshape(8, 16, 64)[:, :15, :31]


def _barrel(x, left):
    row = lax.broadcasted_iota(jnp.int32, x.shape, 0)
    n = x.shape[1]
    for bit in range(6):
        s = 1 << bit
        x = jnp.where(((row >> bit) & 1) == 1, pltpu.roll(x, (n - s) if left else s, 1), x)
    return x


NA_TILE_ROWS, NA_BAND_ROWS = 4, 12
NA_Q, NA_K = NA_TILE_ROWS * GW, NA_BAND_ROWS * GW
NA_TILES = SEQ // NA_Q


def _tile_rows(cls):
    if cls == 0:
        return [(qr, 0) for qr in range(4)]
    if cls == 1:
        return [(4, qr) for qr in range(4)]
    return [(4, 4), (5, 4), (6, 4), (7, 4)]


def _na_tile(t):
    start = jnp.clip(4 * t - 4, 0, 32 - NA_BAND_ROWS)
    cls = jnp.where(t == 0, 0, jnp.where(t == NA_TILES - 1, 2, 1))
    return pl.ds(pl.multiple_of(t * NA_Q, NA_Q), NA_Q), pl.ds(pl.multiple_of(start * GW, NA_Q), NA_K), cls


def _na_probs(qst, kb, kc, bias):
    s_loc = _nt(qst, kb) * 0.125 + bias
    s_ctx = _nt(qst, kc) * 0.125
    m = jnp.maximum(jnp.max(s_loc, axis=1, keepdims=True), jnp.max(s_ctx, axis=1, keepdims=True))
    e_loc, e_ctx = jnp.exp(s_loc - m), jnp.exp(s_ctx - m)
    den = jnp.sum(e_loc, axis=1, keepdims=True) + jnp.sum(e_ctx, axis=1, keepdims=True)
    return e_loc / den, e_ctx / den


def _stack_heads(t):
    lane = lax.broadcasted_iota(jnp.int32, t.shape, 1)
    zero = jnp.zeros_like(t)
    return jnp.concatenate([jnp.where(lane < 64, t, zero), jnp.where(lane >= 64, t, zero)], axis=0)


def _unstack_heads(t):
    n = t.shape[0] // 2
    lane = lax.broadcasted_iota(jnp.int32, (n, 128), 1)
    return jnp.where(lane < 64, t[:n], t[n:])


def na_bias_table(flat):
    def body(flat_ref, out_ref):
        qc = lax.broadcasted_iota(jnp.int32, (GW, 512), 0)
        kc = lax.broadcasted_iota(jnp.int32, (GW, 512), 1) & 63
        start = jnp.clip(qc - 8, 0, GW - 16)
        window = (kc >= start) & (kc < start + 16)
        fill = jnp.full((GW, NA_K - 512), NEG, F32)
        for hh in (0, 1):
            skew = _barrel(pltpu.roll(jnp.broadcast_to(flat_ref[hh], (GW, 1024)), 1024 - 15, 1), left=False)
            by_class = [jnp.where(window, (skew if rc == 7 else pltpu.roll(skew, (9 + rc) * 64, 1))[:, 0:512], NEG)
                        for rc in range(8)]
            for cls in range(3):
                for qr, (rc, off) in enumerate(_tile_rows(cls)):
                    w = jnp.concatenate([by_class[rc], fill], axis=1)
                    rows = slice(hh * NA_Q + qr * GW, hh * NA_Q + (qr + 1) * GW)
                    out_ref[cls, rows, :] = pltpu.roll(w, off * GW, 1) if off else w

    return pl.pallas_call(
        body, name="na_bias_table", grid=(NPAIR,),
        in_specs=[pl.BlockSpec((None, 2, 1, 1024), lambda p: (p, 0, 0, 0))],
        out_specs=pl.BlockSpec((None, 3, 2 * NA_Q, NA_K), lambda p: (p, 0, 0, 0)),
        out_shape=jax.ShapeDtypeStruct((NPAIR, 3, 2 * NA_Q, NA_K), F32),
    )(flat)


def na_fwd(pna, pnac, bias, mixin, hosted):
    nb = pna.shape[0]

    def body(*refs):
        (p_ref, pc_ref, bias_ref, _), h_in, (out_ref,), h_out, _, h_sems = hosted.split(refs, 4, 1)
        grid_step = pl.program_id(0) * nb + pl.program_id(1)

        @pl.when(grid_step == 0)
        def _():
            hosted.start(h_in, h_out, h_sems)

        kc, vc = pc_ref[:, 128:256], pc_ref[:, 256:384]

        def tile(t, carry):
            qsl, bsl, cls = _na_tile(t)
            kb, vb = p_ref[bsl, 128:256], p_ref[bsl, 256:384]
            p_loc, p_ctx = _na_probs(_stack_heads(p_ref[qsl, 0:128]), kb, kc, bias_ref[cls])
            out_ref[qsl, :] = _unstack_heads(_nn(p_loc, vb) + _nn(p_ctx, vc)).astype(BF16)
            return carry

        lax.fori_loop(0, NA_TILES, tile, 0, unroll=2)

        @pl.when(grid_step == NPAIR * nb - 1)
        def _():
            hosted.finish(h_in, h_out, h_sems)

    h_in_specs, h_out_specs = hosted.specs()
    return pl.pallas_call(
        body, name="na_fwd", grid=(NPAIR, nb),
        in_specs=[
            pl.BlockSpec((None, SEQ, 384), lambda p, b: (b, 0, p)),
            pl.BlockSpec((None, LC, 384), lambda p, b: (b, 0, p)),
            pl.BlockSpec((None, 3, 2 * NA_Q, NA_K), lambda p, b: (p, 0, 0, 0)),
            pl.BlockSpec(memory_space=pl.ANY),
        ] + h_in_specs,
        out_specs=[pl.BlockSpec((None, SEQ, 128), lambda p, b: (b, 0, 4 + p))] + h_out_specs,
        out_shape=[jax.ShapeDtypeStruct((nb, SEQ, D), BF16)] + hosted.out_shape,
        input_output_aliases={3: 0},
        scratch_shapes=hosted.scratch,
        compiler_params=_cp(("arbitrary", "arbitrary")),
    )(pna, pnac, bias, mixin, *hosted.args)


def na_bwd(pna, pnac, bias, dmixin, dproj, dprojc, hosted):
    nb = pna.shape[0]

    def body(*refs):
        own_in, h_in, own_out, h_out, own_scr, h_sems = hosted.split(refs, 6, 3)
        p_ref, pc_ref, bias_ref, dmix_ref = own_in[:4]
        dp_ref, dpc_ref, dpat_ref = own_out
        dbias_s, dk_s, dv_s, dkc_s, dvc_s, res_s, resc_s = own_scr
        b, part = pl.program_id(1), pl.program_id(2)
        grid_step = (pl.program_id(0) * nb + b) * 3 + part

        @pl.when(grid_step == 0)
        def _():
            hosted.start(h_in, h_out, h_sems)

        @pl.when(grid_step == NPAIR * nb * 3 - 1)
        def _():
            hosted.finish(h_in, h_out, h_sems)

        @pl.when(part == 0)
        def _():
            @pl.when(b == 0)
            def _():
                dbias_s[...] = jnp.zeros_like(dbias_s)

            dk_s[...] = jnp.zeros_like(dk_s)
            dv_s[...] = jnp.zeros_like(dv_s)
            dkc_s[...] = jnp.zeros_like(dkc_s)
            dvc_s[...] = jnp.zeros_like(dvc_s)
            kc, vc = pc_ref[:, 128:256], pc_ref[:, 256:384]

            def tile(t, carry):
                qsl, bsl, cls = _na_tile(t)
                kb, vb = p_ref[bsl, 128:256], p_ref[bsl, 256:384]
                qst, dost = _stack_heads(p_ref[qsl, 0:128]), _stack_heads(dmix_ref[qsl, :])
                p_loc, p_ctx = _na_probs(qst, kb, kc, bias_ref[cls])
                dp_loc, dp_ctx = _nt(dost, vb), _nt(dost, vc)
                delta = (jnp.sum(p_loc * dp_loc, axis=1, keepdims=True)
                         + jnp.sum(p_ctx * dp_ctx, axis=1, keepdims=True))
                ds_loc, ds_ctx = p_loc * (dp_loc - delta), p_ctx * (dp_ctx - delta)
                dbias_s[cls] += ds_loc
                res_s[0, qsl, :] = _unstack_heads((_nn(ds_loc, kb) + _nn(ds_ctx, kc)) * 0.125).astype(BF16)
                dk_s[bsl, :] += _tn(ds_loc, qst) * 0.125
                dv_s[bsl, :] += _tn(p_loc, dost)
                dkc_s[...] += _tn(ds_ctx, qst) * 0.125
                dvc_s[...] += _tn(p_ctx, dost)
                return carry

            lax.fori_loop(0, NA_TILES, tile, 0, unroll=2)
            res_s[1] = dk_s[...].astype(BF16)
            res_s[2] = dv_s[...].astype(BF16)
            resc_s[0] = jnp.zeros((LC, 128), BF16)
            resc_s[1] = dkc_s[...].astype(BF16)
            resc_s[2] = dvc_s[...].astype(BF16)

            @pl.when(b == nb - 1)
            def _():
                for hh in (0, 1):
                    by_class = [None] * 8
                    for cls in range(3):
                        for qr, (rc, off) in enumerate(_tile_rows(cls)):
                            w = dbias_s[cls, hh * NA_Q + qr * GW:hh * NA_Q + (qr + 1) * GW, :]
                            w = (pltpu.roll(w, NA_K - off * GW, 1) if off else w)[:, 0:512]
                            by_class[rc] = w if by_class[rc] is None else by_class[rc] + w
                    skew = jnp.zeros((GW, 1024), F32)
                    for rc in range(8):
                        w = jnp.concatenate([by_class[rc], jnp.zeros((GW, 512), F32)], axis=1)
                        skew = skew + (w if rc == 7 else pltpu.roll(w, (7 - rc) * 64, 1))
                    dpat_ref[hh] = jnp.sum(pltpu.roll(_barrel(skew, left=True), 15, 1), axis=0, keepdims=True)

        dp_ref[...] = res_s[part]
        dpc_ref[...] = resc_s[part]

    h_in_specs, h_out_specs = hosted.specs()
    return pl.pallas_call(
        body, name="na_bwd", grid=(NPAIR, nb, 3),
        in_specs=[
            pl.BlockSpec((None, SEQ, 384), lambda p, b, s: (b, 0, p)),
            pl.BlockSpec((None, LC, 384), lambda p, b, s: (b, 0, p)),
            pl.BlockSpec((None, 3, 2 * NA_Q, NA_K), lambda p, b, s: (p, 0, 0, 0)),
            pl.BlockSpec((None, SEQ, 128), lambda p, b, s: (b, 0, 4 + p)),
            pl.BlockSpec(memory_space=pl.ANY),
            pl.BlockSpec(memory_space=pl.ANY),
        ] + h_in_specs,
        out_specs=[
            pl.BlockSpec((None, SEQ, 128), lambda p, b, s: (b, 0, 16 + 3 * p + s)),
            pl.BlockSpec((None, LC, 128), lambda p, b, s: (b, 0, 16 + 3 * p + s)),
            pl.BlockSpec((None, 2, 1, 1024), lambda p, b, s: (p, 0, 0, 0)),
        ] + h_out_specs,
        out_shape=[
            jax.ShapeDtypeStruct((nb, SEQ, IN_W), BF16),
            jax.ShapeDtypeStruct((nb, LC, IN_W), BF16),
            jax.ShapeDtypeStruct((NPAIR, 2, 1, 1024), F32),
        ] + hosted.out_shape,
        input_output_aliases={4: 0, 5: 1},
        scratch_shapes=[
            pltpu.VMEM((3, 2 * NA_Q, NA_K), F32),
            pltpu.VMEM((SEQ, 128), F32), pltpu.VMEM((SEQ, 128), F32),
            pltpu.VMEM((LC, 128), F32), pltpu.VMEM((LC, 128), F32),
            pltpu.VMEM((3, SEQ, 128), BF16), pltpu.VMEM((3, LC, 128), BF16),
        ] + hosted.scratch,
        compiler_params=_cp(("arbitrary", "arbitrary", "arbitrary")),
    )(pna, pnac, bias, dmixin, dproj, dprojc, *hosted.args)


def tail_fwd_bwd(x, mixin, tgt, mod3, g_post_mix, g_pre_mlp, g_post_mlp, wout, w1, w2):
    nb = x.shape[0]

    def body(x_ref, mi_ref, tgt_ref, mod_ref, gpm_ref, gpl_ref, gpo_ref, wo_ref, w1_ref, w2_ref,
             dx_ref, dmix_ref, h2_ref, du_ref, a_ref, dm_ref, dmi_ref, dmod_ref, dg_ref, loss_ref):
        b, t = pl.program_id(0), pl.program_id(1)
        gt1, sh2, sc2, gt2 = mod_ref[2:3, :], mod_ref[3:4, :], mod_ref[4:5, :], mod_ref[5:6, :]
        mix = jnp.dot(mi_ref[...], wo_ref[...], preferred_element_type=F32)
        (x1, h2), vjp_a = jax.vjp(_post_mix, x_ref[...], mix, gt1, sc2, sh2, gpm_ref[...], gpl_ref[...])
        h2b = h2.astype(BF16)
        h2_ref[...] = h2b
        m = jnp.zeros((TN, D), F32)
        relus = []
        for j in range(4):
            cols = slice(j * D, (j + 1) * D)
            r = jnp.maximum(jnp.dot(h2b, w1_ref[j], preferred_element_type=F32), 0.0)
            ab = (r * r).astype(BF16)
            a_ref[:, cols] = ab
            m = m + jnp.dot(ab, w2_ref[cols, :], preferred_element_type=F32)
            relus.append(r)
        loss, vjp_b = jax.vjp(_head_loss, x1, m, gt2, gpo_ref[...], tgt_ref[...])
        dx1, dm, dgt2, dgpo, _ = vjp_b(jnp.ones((1, 1), F32))
        dmb = dm.astype(BF16)
        dm_ref[...] = dmb
        dh2 = jnp.zeros((TN, D), F32)
        for j in range(4):
            cols = slice(j * D, (j + 1) * D)
            da = lax.dot_general(dmb, w2_ref[cols, :], (((1,), (1,)), ((), ())), preferred_element_type=F32)
            dub = (da * (2.0 * relus[j])).astype(BF16)
            du_ref[:, cols] = dub
            dh2 = dh2 + lax.dot_general(dub, w1_ref[j], (((1,), (1,)), ((), ())), preferred_element_type=F32)
        dx, dmix, dgt1, dsc2, dsh2, dgpm, dgpl = vjp_a((dx1, dh2))
        dx_ref[...] = dx
        dmixb = dmix.astype(BF16)
        dmix_ref[...] = dmixb
        dmi_ref[...] = lax.dot_general(dmixb, wo_ref[...], (((1,), (1,)), ((), ())),
                                       preferred_element_type=F32).astype(BF16)

        @pl.when(t == 0)
        def _():
            dmod_ref[...] = jnp.zeros_like(dmod_ref)

        @pl.when((t == 0) & (b == 0))
        def _():
            dg_ref[...] = jnp.zeros_like(dg_ref)
            loss_ref[...] = jnp.zeros_like(loss_ref)

        dmod_ref[2:3, :] += dgt1
        dmod_ref[3:4, :] += dsh2
        dmod_ref[4:5, :] += dsc2
        dmod_ref[5:6, :] += dgt2
        dg_ref[0:1, :] += dgpm
        dg_ref[1:2, :] += dgpl
        dg_ref[2:3, :] += dgpo
        loss_ref[...] += jnp.broadcast_to(loss, loss_ref.shape)

    tok = lambda b, t: (b, t, 0)
    const = lambda b, t: (0, 0)
    vec = pl.BlockSpec((1, D), const)
    return pl.pallas_call(
        body, name="tail_fwd_bwd", grid=(nb, SEQ // TN),
        in_specs=[
            pl.BlockSpec((None, TN, D), tok), pl.BlockSpec((None, TN, D), tok), pl.BlockSpec((None, TN, D), tok),
            pl.BlockSpec((None, 6, D), lambda b, t: (b, 0, 0)), vec, vec, vec,
            pl.BlockSpec((D, D), const, pipeline_mode=pl.Buffered(1)),
            pl.BlockSpec((4, D, D), lambda b, t: (0, 0, 0), pipeline_mode=pl.Buffered(1)),
            pl.BlockSpec((DFF, D), const, pipeline_mode=pl.Buffered(1)),
        ],
        out_specs=[
            pl.BlockSpec((None, TN, D), tok), pl.BlockSpec((None, TN, D), tok), pl.BlockSpec((None, TN, D), tok),
            pl.BlockSpec((None, TN, DFF), tok), pl.BlockSpec((None, TN, DFF), tok), pl.BlockSpec((None, TN, D), tok),
            pl.BlockSpec((None, TN, D), tok),
            pl.BlockSpec((None, 6, D), lambda b, t: (b, 0, 0)),
            pl.BlockSpec((8, D), const), pl.BlockSpec((8, 128), const),
        ],
        out_shape=[
            jax.ShapeDtypeStruct((nb, SEQ, D), F32), jax.ShapeDtypeStruct((nb, SEQ, D), BF16),
            jax.ShapeDtypeStruct((nb, SEQ, D), BF16), jax.ShapeDtypeStruct((nb, SEQ, DFF), BF16),
            jax.ShapeDtypeStruct((nb, SEQ, DFF), BF16), jax.ShapeDtypeStruct((nb, SEQ, D), BF16),
            jax.ShapeDtypeStruct((nb, SEQ, D), BF16),
            jax.ShapeDtypeStruct((nb, 6, D), F32), jax.ShapeDtypeStruct((8, D), F32),
            jax.ShapeDtypeStruct((8, 128), F32),
        ],
        compiler_params=_cp(("arbitrary", "arbitrary")),
    )(x, mixin, tgt, mod3, g_post_mix, g_pre_mlp, g_post_mlp, wout, w1, w2)


def weight_grad(pairs, name, out_dtype=F32, col_blocks=False, tm=1024, tn=1024, tk=2048):
    m, n = pairs[0][0].shape[1], pairs[0][1].shape[1]
    tn = min(tn, n)
    tks = [min(tk, xa.shape[0]) for xa, _ in pairs]
    steps = [xa.shape[0] // t for (xa, _), t in zip(pairs, tks)]
    total = sum(steps)
    offs = [sum(steps[:i]) for i in range(len(pairs))]

    def body(*refs):
        out_ref, acc = refs[2 * len(pairs)], refs[-1]
        k = pl.program_id(2)

        @pl.when(k == 0)
        def _():
            acc[...] = jnp.zeros_like(acc)

        for i in range(len(pairs)):
            @pl.when((k >= offs[i]) & (k < offs[i] + steps[i]))
            def _(i=i):
                acc[...] += lax.dot_general(refs[2 * i][...], refs[2 * i + 1][...], (((0,), (0,)), ((), ())),
                                            preferred_element_type=F32)

        if out_dtype != F32:
            @pl.when(k == total - 1)
            def _():
                out_ref[...] = acc[...].astype(out_dtype)

    in_specs, args = [], []
    for i, (xa, ya) in enumerate(pairs):
        clamp = lambda k, i=i: jnp.clip(k - offs[i], 0, steps[i] - 1)
        in_specs.append(pl.BlockSpec((tks[i], tm), lambda a, c, k, clamp=clamp: (clamp(k), a)))
        in_specs.append(pl.BlockSpec((tks[i], tn), lambda a, c, k, clamp=clamp: (clamp(k), c)))
        args += [xa, ya]
    if col_blocks:
        out_spec = pl.BlockSpec((None, tm, tn), lambda a, c, k: (c, a, 0))
        out_shape = jax.ShapeDtypeStruct((n // tn, m, tn), out_dtype)
    else:
        out_spec = pl.BlockSpec((tm, tn), lambda a, c, k: (a, c))
        out_shape = jax.ShapeDtypeStruct((m, n), out_dtype)
    return pl.pallas_call(
        body, name=name, grid=(m // tm, n // tn, total), in_specs=in_specs, out_specs=out_spec, out_shape=out_shape,
        scratch_shapes=[] if out_dtype == F32 else [pltpu.VMEM((tm, tn), F32)],
        compiler_params=_cp(("arbitrary", "arbitrary", "arbitrary")),
    )(*args)


def _perm_block(t):
    return 4 * (t % 4) + t // 4 if t < 16 else 16 + 3 * ((t - 16) % 4) + (t - 16) // 4


def unpack_w_in(blocks):
    def body(i_ref, o_ref):
        for t in range(28):
            p = _perm_block(t)
            o_ref[:, p * 128:(p + 1) * 128] = i_ref[t // 7, :, (t % 7) * 128:(t % 7 + 1) * 128]

    return pl.pallas_call(
        body, name="unpack_w_in", grid=(2,),
        in_specs=[pl.BlockSpec((4, D // 2, 896), lambda i: (0, i, 0))],
        out_specs=pl.BlockSpec((D // 2, IN_W), lambda i: (i, 0)),
        out_shape=jax.ShapeDtypeStruct((D, IN_W), BF16),
    )(blocks)


def pack_w_in(dw):
    def body(i_ref, o_ref):
        for t in range(28):
            p = _perm_block(t)
            o_ref[t // 7, :, (t % 7) * 128:(t % 7 + 1) * 128] = i_ref[:, p * 128:(p + 1) * 128].astype(BF16)

    return pl.pallas_call(
        body, name="pack_w_in", grid=(4,),
        in_specs=[pl.BlockSpec((D // 4, IN_W), lambda i: (i, 0))],
        out_specs=pl.BlockSpec((4, D // 4, 896), lambda i: (0, i, 0)),
        out_shape=jax.ShapeDtypeStruct((4, D, 896), BF16),
    )(dw)


def _place():
    return lax.axis_index("x"), lax.axis_index("y"), lax.axis_index("c")


class Hosted:
    def __init__(self, args, out_shape, scratch, start, finish):
        self.args, self.out_shape, self.scratch, self.start, self.finish = args, out_shape, scratch, start, finish

    def specs(self):
        hbm = pl.BlockSpec(memory_space=pl.ANY)
        return [hbm] * len(self.args), [hbm] * len(self.out_shape)

    def split(self, refs, n_in, n_out):
        a, b = len(self.args), len(self.out_shape)
        cuts = [n_in, n_in + a, n_in + a + n_out, n_in + a + n_out + b, len(refs) - len(self.scratch)]
        parts = [refs[i:j] for i, j in zip([0] + cuts, cuts + [len(refs)])]
        return parts[0], parts[1], parts[2], parts[3], parts[4], parts[5]


def no_exchange():
    return Hosted([], [], [], lambda *a: None, lambda *a: None)


def run_hosted(hosted, name):
    def body(*refs):
        _, ins, _, outs, _, sems = hosted.split(refs, 0, 0)
        hosted.start(ins, outs, sems)
        hosted.finish(ins, outs, sems)

    in_specs, out_specs = hosted.specs()
    return pl.pallas_call(body, name=name, in_specs=in_specs, out_specs=out_specs, out_shape=hosted.out_shape,
                          scratch_shapes=hosted.scratch)(*hosted.args)


def gather8(blocks):
    na = len(blocks)

    def copies(ins, outs, sems):
        send_sems, recv_sems, local_sem = sems
        x, y, c = _place()
        me, sibling = (x, y, c), (x, y, 1 - c)
        chips = [(1 - x, y), (x, 1 - y), (1 - x, 1 - y)]

        def slot(o_ref, px, py, pc):
            return o_ref.at[4 * px + 2 * py + pc]

        def copy(a, k, block, to, src=None):
            return pltpu.make_async_remote_copy(
                src_ref=slot(outs[a], *block) if src is None else src, dst_ref=slot(outs[a], *block),
                send_sem=send_sems.at[a, k], recv_sem=recv_sems.at[a, k], device_id=to, device_id_type=MESH)

        mine = [pltpu.make_async_copy(ins[a], slot(outs[a], *me), local_sem.at[a]) for a in range(na)]
        first = []
        for a in range(na):
            first.append(copy(a, 0, me, sibling, src=ins[a]))
            first += [copy(a, 1 + j, me, (*chip, c), src=ins[a]) for j, chip in enumerate(chips)]
        return copy, mine, first, me, sibling, chips, c

    def start(ins, outs, sems):
        _, mine, first, *_ = copies(ins, outs, sems)
        for cp in mine + first:
            cp.start()

    def finish(ins, outs, sems):
        copy, mine, first, me, sibling, chips, c = copies(ins, outs, sems)
        passed = []
        for j, chip in enumerate(chips):
            for a in range(na):
                copy(a, 1 + j, (*chip, c), me).wait_recv()
                cp = copy(a, 4 + j, (*chip, c), sibling)
                cp.start()
                passed.append(cp)
        for a in range(na):
            copy(a, 0, sibling, me).wait_recv()
            for j, chip in enumerate(chips):
                copy(a, 4 + j, (*chip, 1 - c), me).wait_recv()
        for cp in first + passed:
            cp.wait_send()
        for cp in mine:
            cp.wait()

    return Hosted(list(blocks), [jax.ShapeDtypeStruct((8,) + b.shape, b.dtype) for b in blocks],
                  [pltpu.SemaphoreType.DMA((na, 7)), pltpu.SemaphoreType.DMA((na, 7)), pltpu.SemaphoreType.DMA((na,))],
                  start, finish)


def all_gather8(blocks, name):
    return run_hosted(gather8(blocks), name)


def chips3(arrays):
    na = len(arrays)

    def copies(ins, outs, sems):
        send_sems, recv_sems = sems
        x, y, c = _place()
        return [pltpu.make_async_remote_copy(
            src_ref=ins[a].at[2 * px + py], dst_ref=outs[a].at[k], send_sem=send_sems.at[a, k],
            recv_sem=recv_sems.at[a, k], device_id=(px, py, c), device_id_type=MESH)
            for a in range(na) for k, (px, py) in enumerate([(1 - x, y), (x, 1 - y), (1 - x, 1 - y)])]

    def start(ins, outs, sems):
        for cp in copies(ins, outs, sems):
            cp.start()

    def finish(ins, outs, sems):
        for cp in copies(ins, outs, sems):
            cp.wait()

    return Hosted(list(arrays), [jax.ShapeDtypeStruct((3,) + a.shape[1:], a.dtype) for a in arrays],
                  [pltpu.SemaphoreType.DMA((na, 3)), pltpu.SemaphoreType.DMA((na, 3))], start, finish)


def sibling_exchange(arrays, name):
    na = len(arrays)

    def body(*refs):
        ins, outs = refs[:na], refs[na:2 * na]
        send_sems, recv_sems = refs[2 * na:]
        x, y, c = _place()
        cps = [pltpu.make_async_remote_copy(
            src_ref=ins[a], dst_ref=outs[a], send_sem=send_sems.at[a], recv_sem=recv_sems.at[a],
            device_id=(x, y, 1 - c), device_id_type=MESH) for a in range(na)]
        for cp in cps:
            cp.start()
        for cp in cps:
            cp.wait()

    hbm = pl.BlockSpec(memory_space=pl.ANY)
    return pl.pallas_call(
        body, name=name, in_specs=[hbm] * na, out_specs=[hbm] * na,
        out_shape=[jax.ShapeDtypeStruct(a.shape, a.dtype) for a in arrays],
        scratch_shapes=[pltpu.SemaphoreType.DMA((na,)), pltpu.SemaphoreType.DMA((na,))],
    )(*arrays)


def siblings4(arrays):
    na = len(arrays)

    def copies(ins, outs, sems):
        send_sems, recv_sems = sems
        x, y, c = _place()
        return [pltpu.make_async_remote_copy(
            src_ref=ins[a].at[2 * j + 1 - c], dst_ref=outs[a].at[j],
            send_sem=send_sems.at[a, j], recv_sem=recv_sems.at[a, j],
            device_id=(x, y, 1 - c), device_id_type=MESH) for a in range(na) for j in range(4)]

    def start(ins, outs, sems):
        for cp in copies(ins, outs, sems):
            cp.start()

    def finish(ins, outs, sems):
        for cp in copies(ins, outs, sems):
            cp.wait()

    return Hosted(list(arrays), [jax.ShapeDtypeStruct((4,) + a.shape[1:], a.dtype) for a in arrays],
                  [pltpu.SemaphoreType.DMA((na, 4)), pltpu.SemaphoreType.DMA((na, 4))], start, finish)


def sibling_blocks(arrays, name):
    return run_hosted(siblings4(arrays), name)


def _row_tile(r):
    for cand in (512, 256, 128, 64, 32, 16, 8):
        if r % cand == 0:
            return cand
    return r


def chip_partial(place, g8, landed4, name):
    _, r, ccols = g8.shape
    tr = _row_tile(r)

    def body(place_ref, g_ref, l_ref, o_ref):
        del place_ref
        o_ref[...] = (g_ref[...].astype(F32) + l_ref[...].astype(F32)).astype(BF16)

    spec = pl.BlockSpec((None, tr, ccols), lambda j, i, s: (j, i, 0))
    return pl.pallas_call(
        body, name=name,
        grid_spec=pltpu.PrefetchScalarGridSpec(
            num_scalar_prefetch=1, grid=(4, r // tr),
            in_specs=[pl.BlockSpec((None, tr, ccols), lambda j, i, s: (2 * j + s[0], i, 0)), spec], out_specs=spec),
        out_shape=jax.ShapeDtypeStruct((4, r, ccols), BF16),
    )(place, g8, landed4)


def shard_sum(place, partial4, landed3, name):
    _, r, ccols = partial4.shape
    tr = _row_tile(r)

    def body(place_ref, p_ref, l_ref, o_ref):
        del place_ref
        acc = p_ref[...].astype(F32)
        for k in range(3):
            acc = acc + l_ref[k].astype(F32)
        o_ref[...] = acc

    return pl.pallas_call(
        body, name=name,
        grid_spec=pltpu.PrefetchScalarGridSpec(
            num_scalar_prefetch=1, grid=(r // tr,),
            in_specs=[pl.BlockSpec((None, tr, ccols), lambda i, s: (s[1], i, 0)),
                      pl.BlockSpec((3, tr, ccols), lambda i, s: (0, i, 0))],
            out_specs=pl.BlockSpec((tr, ccols), lambda i, s: (i, 0))),
        out_shape=jax.ShapeDtypeStruct((r, ccols), F32),
    )(place, partial4, landed3)


def _adamw_math(w, g, m, v):
    m2 = B1 * m + (1.0 - B1) * g
    v2 = B2 * v + (1.0 - B2) * (g * g)
    m_hat = m2 / (1.0 - B1 ** STEP)
    v_hat = v2 / (1.0 - B2 ** STEP)
    return -LR * (m_hat / (jnp.sqrt(v_hat) + AEPS) + WD * w), m2, v2


def adamw_halves(place, w, mine, theirs, m, v, name):
    r, ccols = w.shape
    hr = r // 2
    tr = _row_tile(hr)
    nt = hr // tr

    def body(place_ref, w_ref, a_ref, b_ref, m_ref, v_ref, g_out, d_out, m_out, v_out):
        g = jnp.where(pl.program_id(0) == place_ref[0], a_ref[...], b_ref[...])
        d, m2, v2 = _adamw_math(w_ref[...], g, m_ref[...], v_ref[...])
        g_out[...] = g
        d_out[...] = d
        m_out[...] = m2
        v_out[...] = v2

    full = pl.BlockSpec((tr, ccols), lambda h, i, s: (h * nt + i, 0))
    part = pl.BlockSpec((tr, ccols), lambda h, i, s: (i, 0))
    return pl.pallas_call(
        body, name=name,
        grid_spec=pltpu.PrefetchScalarGridSpec(
            num_scalar_prefetch=1, grid=(2, nt), in_specs=[full, part, part, full, full], out_specs=[full] * 4),
        out_shape=[jax.ShapeDtypeStruct((r, ccols), F32)] * 4,
    )(place, w, mine, theirs, m, v)


def adamw_group(place, halved, plain, hosted, name):
    rows = halved[0][0].shape[0]
    tr = 64
    nt = rows // 2 // tr
    nh, npl = len(halved), len(plain)

    def body(place_ref, *refs):
        own_in, h_in, own_out, h_out, _, h_sems = hosted.split(refs, 5 * nh + 4 * npl, 4 * nh + 3 * npl)
        half = pl.program_id(0)
        grid_step = half * nt + pl.program_id(1)

        @pl.when(grid_step == 0)
        def _():
            hosted.start(h_in, h_out, h_sems)

        for i in range(nh):
            w_ref, a_ref, b_ref, m_ref, v_ref = own_in[5 * i:5 * i + 5]
            g = jnp.where(half == place_ref[0], a_ref[...], b_ref[...])
            res = (g,) + _adamw_math(w_ref[...], g, m_ref[...], v_ref[...])
            for o_ref, r in zip(own_out[4 * i:4 * i + 4], res):
                o_ref[...] = r
        for i in range(npl):
            w_ref, g_ref, m_ref, v_ref = own_in[5 * nh + 4 * i:5 * nh + 4 * i + 4]
            res = _adamw_math(w_ref[...], g_ref[...], m_ref[...], v_ref[...])
            for o_ref, r in zip(own_out[4 * nh + 3 * i:4 * nh + 3 * i + 3], res):
                o_ref[...] = r

        @pl.when(grid_step == 2 * nt - 1)
        def _():
            hosted.finish(h_in, h_out, h_sems)

    def full(cols):
        return pl.BlockSpec((tr, cols), lambda h, i, s: (h * nt + i, 0))

    def part(cols):
        return pl.BlockSpec((tr, cols), lambda h, i, s: (i, 0))

    in_specs, out_specs, out_shape, args = [], [], [], []
    for w, a, b, m, v in halved:
        cols = w.shape[1]
        in_specs += [full(cols), part(cols), part(cols), full(cols), full(cols)]
        out_specs += [full(cols)] * 4
        out_shape += [jax.ShapeDtypeStruct(w.shape, F32)] * 4
        args += [w, a, b, m, v]
    for w, g, m, v in plain:
        cols = w.shape[1]
        in_specs += [full(cols)] * 4
        out_specs += [full(cols)] * 3
        out_shape += [jax.ShapeDtypeStruct(w.shape, F32)] * 3
        args += [w, g, m, v]
    h_in_specs, h_out_specs = hosted.specs()
    return pl.pallas_call(
        body, name=name,
        grid_spec=pltpu.PrefetchScalarGridSpec(
            num_scalar_prefetch=1, grid=(2, nt), in_specs=in_specs + h_in_specs, out_specs=out_specs + h_out_specs,
            scratch_shapes=hosted.scratch),
        out_shape=out_shape + hosted.out_shape,
        compiler_params=_cp(("arbitrary", "arbitrary")),
    )(place, *args, *hosted.args)


def _silu(x):
    return x * jax.nn.sigmoid(x)


def mod_shard(cin, w_ada, b_shard):
    def body(c_ref, w_ref, b_ref, o_ref):
        o_ref[...] = _nn(_silu(c_ref[...]), w_ref[...]) + b_ref[...]

    return pl.pallas_call(
        body, name="mod_shard", grid=(3,),
        in_specs=[pl.BlockSpec((32, D), lambda j: (0, 0)), pl.BlockSpec((D, 512), lambda j: (0, j)),
                  pl.BlockSpec((1, 512), lambda j: (0, j))],
        out_specs=pl.BlockSpec((32, 512), lambda j: (0, j)),
        out_shape=jax.ShapeDtypeStruct((32, 1536), F32),
    )(cin, w_ada, b_shard)


def ada_grads(cin, gb, gc, w_ada):
    def body(c_ref, gb_ref, gc_ref, w_ref, gw_ref, pc_ref):
        ctx_tot = jnp.sum(gc_ref[...], axis=0, keepdims=True)
        rows = lax.broadcasted_iota(jnp.int32, (16, 512), 0)
        dm = jnp.concatenate([gb_ref[...], jnp.where(rows == 0, ctx_tot, 0.0)], axis=0)
        gw_ref[...] = _tn(_silu(c_ref[...]), dm)
        rows8 = lax.broadcasted_iota(jnp.int32, (8, 512), 0)
        part = _nt(jnp.where(rows8 == 0, ctx_tot, 0.0), w_ref[...])

        @pl.when(pl.program_id(0) == 0)
        def _():
            pc_ref[...] = jnp.zeros_like(pc_ref)

        pc_ref[...] += part

    return pl.pallas_call(
        body, name="ada_grads", grid=(3,),
        in_specs=[pl.BlockSpec((32, D), lambda j: (0, 0)), pl.BlockSpec((16, 512), lambda j: (0, j)),
                  pl.BlockSpec((8, 512), lambda j: (0, j)), pl.BlockSpec((D, 512), lambda j: (0, j))],
        out_specs=[pl.BlockSpec((D, 512), lambda j: (0, j)), pl.BlockSpec((8, D), lambda j: (0, 0))],
        out_shape=[jax.ShapeDtypeStruct((D, 1536), F32), jax.ShapeDtypeStruct((8, D), F32)],
    )(cin, gb, gc, w_ada)


SMALL_SUM_ROWS = 15


def small_update(gsm, gbf, gcf, pcg, params):
    n = len(params)

    def body(*refs):
        gsm_ref, gbf_ref, gcf_ref, pcg_ref = refs[:4]
        wmv, outs, loss_out = refs[4:4 + 3 * n], refs[4 + 3 * n:4 + 7 * n], refs[-1]
        acc = gsm_ref[0]
        for dev in range(1, 8):
            acc = acc + gsm_ref[dev]
        c_ctx = wmv[0][...]
        sg = jax.nn.sigmoid(c_ctx)
        dsilu = pcg_ref[0:1, :] + pcg_ref[2:3, :] + pcg_ref[4:5, :] + pcg_ref[6:7, :]
        lane = lax.broadcasted_iota(jnp.int32, (1, D), 1)
        last = acc[14:15, :]
        grads = [
            dsilu * (sg * (1.0 + c_ctx * (1.0 - sg))),
            jnp.sum(gbf_ref[...], axis=0, keepdims=True) + jnp.sum(gcf_ref[...], axis=0, keepdims=True),
            acc[0:1, :] + acc[1:2, :], acc[2:3, :], acc[3:4, :], acc[4:5, :],
            acc[5:6, 0:512], acc[6:14, :], jnp.where(lane < 8, last, 0.0),
        ]
        loss_out[...] = jnp.broadcast_to(jnp.sum(jnp.where(lane == 8, last, 0.0), axis=1, keepdims=True), (8, 128))
        for i, g in enumerate(grads):
            d, m2, v2 = _adamw_math(wmv[3 * i][...], g, wmv[3 * i + 1][...], wmv[3 * i + 2][...])
            outs[4 * i][...] = g
            outs[4 * i + 1][...] = d
            outs[4 * i + 2][...] = m2
            outs[4 * i + 3][...] = v2

    flat = [a for wmv in params for a in wmv]
    out_shape = [jax.ShapeDtypeStruct(w.shape, F32) for w, _, _ in params for _ in range(4)]
    return pl.pallas_call(
        body, name="small_update", out_shape=out_shape + [jax.ShapeDtypeStruct((8, 128), F32)],
    )(gsm, gbf, gcf, pcg, *flat)


def _pad_row(v, rows):
    flat = v.reshape(-1)
    return jnp.pad(flat, (0, rows * D - flat.shape[0])).reshape(rows, D)


def local_step(x, ctx, tgt, mod3, g_pre_mix, g_post_mix, g_pre_mlp, g_post_mlp, ret_decay, ret_gn, na_rpb,
               wperm, late_weights, early_grads):
    nb = x.shape[0]
    tokens = nb * SEQ
    cos, sin = _rope_tables()
    rd = ret_decay.T.reshape(RH, 2, 1)
    gn = ret_gn.reshape(RH, 1, RD)
    bias = na_bias_table(_rpb_flat(na_rpb))
    h, pret, pna = premix_proj(x, mod3, g_pre_mix, wperm, False, "premix_proj")
    hc, pretc, pnac = premix_proj(ctx, mod3, g_pre_mix, wperm, True, "premix_proj_ctx")
    o_all, mixin, gw_out = retention_fwd(pret, pretc, rd, gn, cos, sin, late_weights(0))
    mixin, gw1, gw2 = na_fwd(pna, pnac, bias, mixin, late_weights(1))
    dx_tail, dmix, h2, du, act, dm, dmixin, dmod_t, dg_t, loss_t = tail_fwd_bwd(
        x, mixin, tgt, mod3, g_post_mix, g_pre_mlp, g_post_mlp, gw_out.reshape(D, D), gw1.reshape(4, D, D),
        gw2.reshape(DFF, D))
    dw_out = weight_grad([(mixin.reshape(tokens, D), dmix.reshape(tokens, D))], "grad_w_out", BF16)
    dw1 = weight_grad([(h2.reshape(tokens, D), du.reshape(tokens, DFF))], "grad_w_mlp1", BF16, col_blocks=True)
    dw2 = weight_grad([(act.reshape(tokens, DFF), dm.reshape(tokens, D))], "grad_w_mlp2", BF16)
    dproj, dprojc, drd, dgn, *landed = retention_bwd(pret, pretc, o_all, dmixin, rd, gn, cos, sin,
                                                     early_grads[0](dw_out, dw1, dw2))
    dproj, dprojc, dpat, *early = na_bwd(pna, pnac, bias, dmixin, dproj, dprojc, early_grads[1](landed))
    dw_in = weight_grad([(h.reshape(tokens, D), dproj.reshape(tokens, IN_W)),
                         (hc.reshape(nb * LC, D), dprojc.reshape(nb * LC, IN_W))], "grad_w_in", tn=IN_W // 2, tk=1024)
    grad_x, dmod_a, dg_a, *late = premix_bwd(x, mod3, g_pre_mix, wperm, dproj, dx_tail, early_grads[2](dw_in),
                                             "premix_bwd")
    dmod_c, dg_c = premix_bwd(ctx, mod3, g_pre_mix, wperm, dprojc, None, no_exchange(), "premix_bwd_ctx")
    dmod = jnp.concatenate([jnp.concatenate([dmod_a[:, 0:2], dmod_t[:, 2:6]], axis=1), dmod_c], axis=0)
    last = jnp.pad(jnp.concatenate([drd[:, :, 0].T.reshape(8), loss_t[0, 0:1]]), (0, D - 9)).reshape(1, D)
    small = jnp.concatenate([dg_a[0:1], dg_c[0:1], dg_t[0:3], _pad_row(dgn, 1), dpat.reshape(8, D), last], axis=0)
    return grad_x, late, early, dmod, small


def kernel(x, c, ctx, c_ctx, w_ada, b_ada, g_pre_mix, g_post_mix, g_pre_mlp, g_post_mlp, w_in, ret_decay, ret_gn, na_rpb, w_out, w_mlp1, w_mlp2, loss_target, m_c_ctx, m_w_ada, m_b_ada, m_g_pre_mix, m_g_post_mix, m_g_pre_mlp, m_g_post_mlp, m_w_in, m_ret_decay, m_ret_gn, m_na_rpb, m_w_out, m_w_mlp1, m_w_mlp2, v_c_ctx, v_w_ada, v_b_ada, v_g_pre_mix, v_g_post_mix, v_g_pre_mlp, v_g_post_mlp, v_w_in, v_ret_decay, v_ret_gn, v_na_rpb, v_w_out, v_w_mlp1, v_w_mlp2):
    px, py, pc = _place()
    dev = 4 * px + 2 * py + pc
    chip = 2 * px + py

    def my_half(w2d):
        rows = w2d.shape[0] // 2
        return lax.dynamic_slice_in_dim(w2d, pc * rows, rows, 0)

    halves = [my_half(w[0]).astype(BF16) for w in (w_in, w_out, w_mlp1, w_mlp2)]
    gw_in, cg = all_gather8([halves[0], jnp.pad(c, ((0, 6), (0, 0)))], "gather_w_in")
    wperm = unpack_w_in(gw_in.reshape(4, D, 896))

    cin = jnp.pad(cg[:, 0:2].reshape(16, D), ((0, 16), (0, 0))) + jnp.pad(c_ctx[None], ((16, 15), (0, 0)))
    mod_mine = mod_shard(cin, w_ada[0], lax.dynamic_slice_in_dim(b_ada, chip * 1536, 1536, 1))
    (mg,) = all_gather8([mod_mine], "gather_mod")
    mod_all = jnp.concatenate([mg[0], mg[2], mg[4], mg[6]], axis=1)
    mod3 = (jnp.pad(lax.dynamic_slice_in_dim(mod_all, 2 * dev, 2, 0), ((0, 1), (0, 0)))
            + jnp.pad(mod_all[16:17], ((2, 0), (0, 0)))).reshape(3, 6, D)

    place = jnp.stack([pc, chip]).astype(jnp.int32)

    early_names = ["w_out", "w_mlp1", "w_mlp2"]
    early_g8, early_partial = [], []

    def early_a(dw_out, dw1, dw2):
        early_g8[:] = [dw_out.reshape(8, 128, D), dw1.reshape(8, 512, D), dw2.reshape(8, 512, D)]
        return siblings4(early_g8)

    def early_b(landed):
        early_partial[:] = [chip_partial(place, g, l, "rs_chip_sum_" + n)
                            for g, l, n in zip(early_g8, landed, early_names)]
        return chips3(early_partial)

    late_partial = []

    def late_c(dw_in):
        g8_in = pack_w_in(dw_in).reshape(8, 512, 896)
        (landed_in,) = sibling_blocks([g8_in], "rs_sibling_w_in")
        late_partial[:] = [chip_partial(place, g8_in, landed_in, "rs_chip_sum_w_in")]
        return chips3(late_partial)

    grad_x, (landed3_in,), early_landed, dmod, small = local_step(
        x, ctx, loss_target, mod3, g_pre_mix, g_post_mix, g_pre_mlp, g_post_mlp, ret_decay[0], ret_gn, na_rpb[0],
        wperm, lambda k: gather8(halves[1:2] if k == 0 else halves[2:4]), (early_a, early_b, late_c))
    early_mine = [shard_sum(place, p, l, "rs_shard_sum_" + n)
                  for p, l, n in zip(early_partial, early_landed, early_names)]
    early_theirs = sibling_exchange(early_mine, "rs_halves_early")

    pay = jnp.concatenate([dmod.reshape(18, D), small, jnp.zeros((40 - 18 - SMALL_SUM_ROWS, D), F32)], axis=0)
    (gs,) = all_gather8([pay], "gather_small")
    gbf = gs[:, 0:12].reshape(16, 6 * D)
    gcf = gs[:, 12:18].reshape(8, 6 * D)
    gw_ada, pc_part = ada_grads(cin, lax.dynamic_slice_in_dim(gbf, chip * 1536, 1536, 1),
                                lax.dynamic_slice_in_dim(gcf, chip * 1536, 1536, 1), w_ada[0])
    (pcg,) = all_gather8([pc_part], "gather_c_ctx")

    grouped = adamw_group(
        place,
        [(w_mlp1[0], early_mine[1], early_theirs[1], m_w_mlp1[0], v_w_mlp1[0]),
         (w_mlp2[0], early_mine[2], early_theirs[2], m_w_mlp2[0], v_w_mlp2[0])],
        [(w_ada[0], gw_ada, m_w_ada[0], v_w_ada[0])], no_exchange(), "adamw_group")
    d_ada, m_ada, v_ada = grouped[8:11]
    mine_in = shard_sum(place, late_partial[0], landed3_in, "rs_shard_sum_w_in")
    (theirs_in,) = sibling_exchange([mine_in], "rs_halves_w_in")
    big = [
        [r[None] for r in adamw_halves(place, w_in[0], mine_in, theirs_in, m_w_in[0], v_w_in[0], "adamw_w_in")],
        [r[None] for r in adamw_halves(place, w_out[0], early_mine[0], early_theirs[0], m_w_out[0], v_w_out[0],
                                       "adamw_w_out")],
        [r[None] for r in grouped[0:4]], [r[None] for r in grouped[4:8]],
    ]

    def rpb_rows(t):
        return _rpb_flat(t[0]).reshape(8, D)

    def decay_row(t):
        return jnp.pad(t.reshape(1, 8), ((0, 0), (0, D - 8)))

    views = [lambda t: t.reshape(1, D), lambda t: t, lambda t: t, lambda t: t, lambda t: t, lambda t: t, lambda t: t,
             rpb_rows, decay_row]
    back = [lambda t: t.reshape(D), lambda t: t, lambda t: t, lambda t: t, lambda t: t, lambda t: t, lambda t: t,
            lambda t: _rpb_flat_t(t)[None], lambda t: t[:, 0:8].reshape(1, 2, 4)]
    small_w = (c_ctx, b_ada, g_pre_mix, g_post_mix, g_pre_mlp, g_post_mlp, ret_gn, na_rpb, ret_decay)
    small_m = (m_c_ctx, m_b_ada, m_g_pre_mix, m_g_post_mix, m_g_pre_mlp, m_g_post_mlp, m_ret_gn, m_na_rpb, m_ret_decay)
    small_v = (v_c_ctx, v_b_ada, v_g_pre_mix, v_g_post_mix, v_g_pre_mlp, v_g_post_mlp, v_ret_gn, v_na_rpb, v_ret_decay)
    *res, loss8 = small_update(gs[:, 18:18 + SMALL_SUM_ROWS], gbf, gcf, pcg[:, 0],
                               [(f(w), f(m), f(v)) for f, w, m, v in zip(views, small_w, small_m, small_v)])

    def leaves(ada, idx):
        s_c, s_b, s_g1, s_g2, s_g3, s_g4, s_gn, s_rpb, s_rd = [back[i](res[4 * i + idx]) for i in range(9)]
        return [s_c, ada[None], s_b, s_g1, s_g2, s_g3, s_g4, big[0][idx], s_rd, s_gn, s_rpb,
                big[1][idx], big[2][idx], big[3][idx]]

    return (loss8[0, 0], grad_x, *leaves(gw_ada, 0), *leaves(d_ada, 1), *leaves(m_ada, 2), *leaves(v_ada, 3))
```

```python
import functools

import jax
import jax.numpy as jnp
from jax import lax
from jax.experimental import pallas as pl
from jax.experimental.pallas import tpu as pltpu

F32, BF16 = jnp.float32, jnp.bfloat16
D = 1024
SEQ = 2048
LC = 256
GW = 64
RH, RD, CH = 4, 128, 128
NPAIR = 4
IN_W = 3584
RET_W = 2048
DFF = 4096
EPS = 1e-6
NEG = -1e30
TN = 256
NCH = SEQ // CH
LR, B1, B2, AEPS, WD, STEP = 0.001, 0.9, 0.999, 1e-08, 0.01, 10
MESH = pl.DeviceIdType.MESH
VMEM_LIMIT = 56 * 1024 * 1024


def _cp(sem=None):
    return pltpu.CompilerParams(dimension_semantics=sem, vmem_limit_bytes=VMEM_LIMIT)


def _nn(a, b):
    return jnp.dot(a.astype(BF16), b.astype(BF16), preferred_element_type=F32)


def _nt(a, b):
    return lax.dot_general(a.astype(BF16), b.astype(BF16), (((1,), (1,)), ((), ())), preferred_element_type=F32)


def _tn(a, b):
    return lax.dot_general(a.astype(BF16), b.astype(BF16), (((0,), (0,)), ((), ())), preferred_element_type=F32)


@jax.custom_vjp
def mm_nn(a, b):
    return _nn(a, b)


@jax.custom_vjp
def mm_nt(a, b):
    return _nt(a, b)


@jax.custom_vjp
def mm_tn(a, b):
    return _tn(a, b)


mm_nn.defvjp(lambda a, b: (_nn(a, b), (a, b)), lambda r, g: (_nt(g, r[1]), _tn(r[0], g)))
mm_nt.defvjp(lambda a, b: (_nt(a, b), (a, b)), lambda r, g: (_nn(g, r[1]), _tn(g, r[0])))
mm_tn.defvjp(lambda a, b: (_tn(a, b), (a, b)), lambda r, g: (_nt(r[1], g), _nn(r[0], g)))


def _rms(x):
    return x * lax.rsqrt(jnp.mean(x * x, axis=-1, keepdims=True) + EPS)


def _rms_mod(x, g, sc, sh):
    return (_rms(x) * g) * (1.0 + sc) + sh


def _post_mix(x, mix, gt1, sc2, sh2, g_post_mix, g_pre_mlp):
    x1 = x + gt1 * (_rms(mix) * g_post_mix)
    return x1, _rms_mod(x1, g_pre_mlp, sc2, sh2)


def _head_loss(x1, m, gt2, g_post_mlp, tgt):
    err = x1 + gt2 * (_rms(m) * g_post_mlp) - tgt
    return 0.5 * jnp.sum(jnp.mean(err * err, axis=-1, keepdims=True), axis=0, keepdims=True)


def _ln_gate(o, g, w):
    mu = jnp.mean(o, axis=-1, keepdims=True)
    var = jnp.mean(jnp.square(o - mu), axis=-1, keepdims=True)
    y = (o - mu) * lax.rsqrt(var + EPS)
    return (y * w) * (g * jax.nn.sigmoid(g))


def _swap32(x):
    lane = lax.broadcasted_iota(jnp.int32, x.shape, 1)
    return jnp.where((lane & 32) == 0, pltpu.roll(x, 96, 1), pltpu.roll(x, 32, 1))


def _rope(x, cos, sin):
    return x * cos + _swap32(x) * sin


def _rope_t(g, cos, sin):
    return g * cos + _swap32(g * sin)


def _rope_tables():
    tok = jnp.arange(SEQ)
    pos_r = (tok // GW).astype(F32)
    pos_c = (tok % GW).astype(F32)
    inv = 10000.0 ** (-jnp.arange(32, dtype=F32) / 32)
    ar = pos_r[:, None] * inv[None, :]
    ac = pos_c[:, None] * inv[None, :]
    cos = jnp.concatenate([jnp.cos(ar), jnp.cos(ar), jnp.cos(ac), jnp.cos(ac)], axis=-1)
    sin = jnp.concatenate([-jnp.sin(ar), jnp.sin(ar), -jnp.sin(ac), jnp.sin(ac)], axis=-1)
    return cos, sin


def _chunk_loop(n, body, init, k=4):
    def several(t, carry):
        for i in range(k):
            carry = body(k * t + i, carry)
        return carry

    return lax.fori_loop(0, n // k, several, init)


def _fiota(shape, dim):
    return lax.broadcasted_iota(jnp.int32, shape, dim).astype(F32)


def _ret_state(k, v, s, lg, reverse):
    pos = _fiota((CH, 1), 0)
    b_exp = pos if reverse else (CH - 1.0 - pos)
    return jnp.exp(lg * CH) * s + mm_tn(k * jnp.exp(lg * b_exp), v)


def _ret_chunk(q, k, v, s, lg, reverse):
    i = _fiota((CH, CH), 0)
    j = _fiota((CH, CH), 1)
    diff = (j - i) if reverse else (i - j)
    mask = (diff > 0) if reverse else (diff >= 0)
    decay = jnp.where(mask, jnp.exp(lg * jnp.where(mask, diff, 0.0)), 0.0)
    pos = _fiota((CH, 1), 0)
    a_exp = (CH - pos) if reverse else (pos + 1.0)
    o = mm_nn(mm_nt(q, k) * decay, v) + mm_nn(q * jnp.exp(lg * a_exp), s)
    return o, _ret_state(k, v, s, lg, reverse)


def premix_proj(xin, mod3, g_pre, wperm, is_ctx, name):
    nb, length, _ = xin.shape
    tn = min(TN, length)

    def body(x_ref, mod_ref, g_ref, w_ref, h_ref, pret_ref, pna_ref):
        h = _rms_mod(x_ref[...], g_ref[...], mod_ref[1:2, :], mod_ref[0:1, :])
        hb = h.astype(BF16)
        h_ref[...] = hb
        pret_ref[...] = jnp.dot(hb, w_ref[:, :RET_W], preferred_element_type=F32)
        pna_ref[...] = jnp.dot(hb, w_ref[:, RET_W:], preferred_element_type=F32).astype(BF16)

    return pl.pallas_call(
        body, name=name, grid=(nb, length // tn),
        in_specs=[
            pl.BlockSpec((None, tn, D), lambda b, t: (b, t, 0)),
            pl.BlockSpec((None, 6, D), (lambda b, t: (2, 0, 0)) if is_ctx else (lambda b, t: (b, 0, 0))),
            pl.BlockSpec((1, D), lambda b, t: (0, 0)),
            pl.BlockSpec((D, IN_W), lambda b, t: (0, 0), pipeline_mode=pl.Buffered(1)),
        ],
        out_specs=[
            pl.BlockSpec((None, tn, D), lambda b, t: (b, t, 0)),
            pl.BlockSpec((None, tn, RET_W), lambda b, t: (b, t, 0)),
            pl.BlockSpec((None, tn, IN_W - RET_W), lambda b, t: (b, t, 0)),
        ],
        out_shape=[
            jax.ShapeDtypeStruct((nb, length, D), BF16),
            jax.ShapeDtypeStruct((nb, length, RET_W), F32),
            jax.ShapeDtypeStruct((nb, length, IN_W - RET_W), BF16),
        ],
        compiler_params=_cp(("arbitrary", "arbitrary")),
    )(xin, mod3, g_pre, wperm)


def premix_bwd(xin, mod3, g_pre, wperm, dproj, dx_tail, hosted, name):
    nb, length, _ = xin.shape
    tn = min(TN, length)
    is_ctx = dx_tail is None

    def body(*refs):
        own_in, h_in, own_out, h_out, _, h_sems = hosted.split(refs, 5 if is_ctx else 6, 2 if is_ctx else 3)
        if is_ctx:
            (x_ref, mod_ref, g_ref, w_ref, dp_ref), (dmod_ref, dg_ref) = own_in, own_out
        else:
            (x_ref, mod_ref, g_ref, w_ref, dp_ref, dxt_ref), (dx_ref, dmod_ref, dg_ref) = own_in, own_out
        b, t = pl.program_id(0), pl.program_id(1)
        grid_step = b * (length // tn) + t

        @pl.when(grid_step == 0)
        def _():
            hosted.start(h_in, h_out, h_sems)

        @pl.when(grid_step == nb * (length // tn) - 1)
        def _():
            hosted.finish(h_in, h_out, h_sems)

        dh = lax.dot_general(dp_ref[...], w_ref[...], (((1,), (1,)), ((), ())), preferred_element_type=F32)
        _, vjp = jax.vjp(_rms_mod, x_ref[...], g_ref[...], mod_ref[1:2, :], mod_ref[0:1, :])
        dx, dg, dsc, dsh = vjp(dh)
        if not is_ctx:
            dx_ref[...] = dx + dxt_ref[...]

        @pl.when((t == 0) & ((b == 0) if is_ctx else True))
        def _():
            dmod_ref[...] = jnp.zeros_like(dmod_ref)

        @pl.when((t == 0) & (b == 0))
        def _():
            dg_ref[...] = jnp.zeros_like(dg_ref)

        dmod_ref[0:1, :] += dsh
        dmod_ref[1:2, :] += dsc
        dg_ref[0:1, :] += dg

    tok = lambda b, t: (b, t, 0)
    in_specs = [
        pl.BlockSpec((None, tn, D), tok),
        pl.BlockSpec((None, 6, D), (lambda b, t: (2, 0, 0)) if is_ctx else (lambda b, t: (b, 0, 0))),
        pl.BlockSpec((1, D), lambda b, t: (0, 0)),
        pl.BlockSpec((D, IN_W), lambda b, t: (0, 0), pipeline_mode=pl.Buffered(1)),
        pl.BlockSpec((None, tn, IN_W), tok),
    ]
    args = [xin, mod3, g_pre, wperm, dproj]
    out_specs = [
        pl.BlockSpec((None, 6, D), (lambda b, t: (0, 0, 0)) if is_ctx else (lambda b, t: (b, 0, 0))),
        pl.BlockSpec((8, D), lambda b, t: (0, 0)),
    ]
    out_shape = [jax.ShapeDtypeStruct((1 if is_ctx else nb, 6, D), F32), jax.ShapeDtypeStruct((8, D), F32)]
    if not is_ctx:
        in_specs.append(pl.BlockSpec((None, tn, D), tok))
        args.append(dx_tail)
        out_specs.insert(0, pl.BlockSpec((None, tn, D), tok))
        out_shape.insert(0, jax.ShapeDtypeStruct((nb, length, D), F32))
    h_in_specs, h_out_specs = hosted.specs()
    return pl.pallas_call(
        body, name=name, grid=(nb, length // tn), in_specs=in_specs + h_in_specs, out_specs=out_specs + h_out_specs,
        out_shape=out_shape + hosted.out_shape, scratch_shapes=hosted.scratch,
        compiler_params=_cp(("arbitrary", "arbitrary")),
    )(*args, *hosted.args)


def _ret_specs(order):
    def im(f):
        return lambda *g: f(*order(*g))
    return dict(
        pret=pl.BlockSpec((None, SEQ, 512), im(lambda b, h: (b, 0, h))),
        pretc=pl.BlockSpec((None, LC, 512), im(lambda b, h: (b, 0, h))),
        rd=pl.BlockSpec((None, 2, 1), im(lambda b, h: (h, 0, 0))),
        gn=pl.BlockSpec((None, 1, RD), im(lambda b, h: (h, 0, 0))),
        tab=pl.BlockSpec((SEQ, RD), im(lambda b, h: (0, 0))),
        head=pl.BlockSpec((None, SEQ, RD), im(lambda b, h: (b, 0, h))),
    )


def retention_fwd(pret, pretc, rd, gn, cos, sin, hosted):
    nb = pret.shape[0]
    sp = _ret_specs(lambda b, h: (b, h))

    def body(*refs):
        own_in, h_in, own_out, h_out, own_scr, h_sems = hosted.split(refs, 6, 2)
        p_ref, pc_ref, rd_ref, gn_ref, cos_ref, sin_ref = own_in
        (o_ref, mix_ref), (q_s, k_s, of_s, ob_s) = own_out, own_scr
        grid_step = pl.program_id(0) * RH + pl.program_id(1)

        @pl.when(grid_step == 0)
        def _():
            hosted.start(h_in, h_out, h_sems)

        cos_v, sin_v = cos_ref[...], sin_ref[...]
        q_s[...] = _rope(p_ref[:, 0:128], cos_v, sin_v) * (RD ** -0.5)
        k_s[...] = _rope(p_ref[:, 128:256], cos_v, sin_v)
        lgs, init = [], []
        for rev in (False, True):
            lg = jax.nn.log_sigmoid(rd_ref[int(rev):int(rev) + 1, :])
            s = jnp.zeros((RD, RD), F32)
            for n in ((1, 0) if rev else (0, 1)):
                s = _ret_state(pc_ref[n * CH:(n + 1) * CH, 128:256], pc_ref[n * CH:(n + 1) * CH, 256:384], s, lg, rev)
            lgs.append(lg)
            init.append(s)

        def step(t, carry):
            out = []
            for rev, o_s, s in ((False, of_s, carry[0]), (True, ob_s, carry[1])):
                n = (NCH - 1 - t) if rev else t
                sl = pl.ds(pl.multiple_of(n * CH, CH), CH)
                o, s2 = _ret_chunk(q_s[sl, :], k_s[sl, :], p_ref[sl, 256:384], s, lgs[int(rev)], rev)
                o_s[sl, :] = o
                out.append(s2)
            return tuple(out)

        _chunk_loop(NCH, step, tuple(init))
        o = of_s[...] + ob_s[...]
        o_ref[...] = o
        mix_ref[...] = _ln_gate(o, p_ref[:, 384:512], gn_ref[...]).astype(BF16)

        @pl.when(grid_step == nb * RH - 1)
        def _():
            hosted.finish(h_in, h_out, h_sems)

    h_in_specs, h_out_specs = hosted.specs()
    return pl.pallas_call(
        body, name="retention_fwd", grid=(nb, RH),
        in_specs=[sp["pret"], sp["pretc"], sp["rd"], sp["gn"], sp["tab"], sp["tab"]] + h_in_specs,
        out_specs=[sp["head"], sp["head"]] + h_out_specs,
        out_shape=[jax.ShapeDtypeStruct((nb, SEQ, RH * RD), F32), jax.ShapeDtypeStruct((nb, SEQ, D), BF16)]
        + hosted.out_shape,
        scratch_shapes=[pltpu.VMEM((SEQ, RD), F32)] * 4 + hosted.scratch,
        compiler_params=_cp(("arbitrary", "arbitrary")),
    )(pret, pretc, rd, gn, cos, sin, *hosted.args)


def retention_bwd(pret, pretc, o_all, dmixin, rd, gn, cos, sin, hosted):
    nb = pret.shape[0]
    sp = _ret_specs(lambda h, b: (b, h))

    def body(*refs):
        own_in, h_in, own_out, h_out, own_scr, h_sems = hosted.split(refs, 8, 4)
        p_ref, pc_ref, o_ref, dmix_ref, rd_ref, gn_ref, cos_ref, sin_ref = own_in
        dp_ref, dpc_ref, drd_ref, dgn_ref = own_out
        q_s, k_s, do_s, dqf_s, dkf_s, dvf_s, dqb_s, dkb_s, dvb_s, stf_s, stb_s = own_scr
        b = pl.program_id(1)
        grid_step = pl.program_id(0) * nb + b

        @pl.when(grid_step == 0)
        def _():
            hosted.start(h_in, h_out, h_sems)

        cos_v, sin_v = cos_ref[...], sin_ref[...]
        q_s[...] = _rope(p_ref[:, 0:128], cos_v, sin_v) * (RD ** -0.5)
        k_s[...] = _rope(p_ref[:, 128:256], cos_v, sin_v)
        _, gate_vjp = jax.vjp(_ln_gate, o_ref[...], p_ref[:, 384:512], gn_ref[...])
        do, dg, dgn = gate_vjp(dmix_ref[...].astype(F32))
        do_s[...] = do
        dp_ref[:, 384:512] = dg.astype(BF16)

        @pl.when(b == 0)
        def _():
            drd_ref[...] = jnp.zeros_like(drd_ref)
            dgn_ref[...] = jnp.zeros_like(dgn_ref)

        dgn_ref[...] += dgn
        kcs = [pc_ref[n * CH:(n + 1) * CH, 128:256] for n in (0, 1)]
        vcs = [pc_ref[n * CH:(n + 1) * CH, 256:384] for n in (0, 1)]
        dirs = []
        init = []
        for rev in (False, True):
            rdv = rd_ref[int(rev):int(rev) + 1, :]
            lg = jax.nn.log_sigmoid(rdv)
            order_c = (1, 0) if rev else (0, 1)
            s = jnp.zeros((RD, RD), F32)
            ctx_states = []
            for n in order_c:
                ctx_states.append(s)
                s = _ret_state(kcs[n], vcs[n], s, lg, rev)
            dirs.append((rev, order_c, lg, rdv, ctx_states))
            init.append(s)
        acc = ((dqf_s, dkf_s, dvf_s, stf_s), (dqb_s, dkb_s, dvb_s, stb_s))

        def fstep(t, carry):
            out = []
            for (rev, _, lg, _, _), (_, _, _, st_s), s in zip(dirs, acc, carry):
                n = (NCH - 1 - t) if rev else t
                sl = pl.ds(pl.multiple_of(n * CH, CH), CH)
                st_s[n] = s
                out.append(_ret_state(k_s[sl, :], p_ref[sl, 256:384], s, lg, rev))
            return tuple(out)

        _chunk_loop(NCH, fstep, tuple(init))

        def bstep(t, carry):
            out = []
            for (rev, _, lg, _, _), (dq_s, dk_s, dv_s, st_s), (ds, dlg) in zip(dirs, acc, carry):
                n = t if rev else (NCH - 1 - t)
                sl = pl.ds(pl.multiple_of(n * CH, CH), CH)
                _, vjp = jax.vjp(functools.partial(_ret_chunk, reverse=rev),
                                 q_s[sl, :], k_s[sl, :], p_ref[sl, 256:384], st_s[n], lg)
                dq, dk, dv, ds_prev, dl = vjp((do_s[sl, :], ds))
                dq_s[sl, :] = dq
                dk_s[sl, :] = dk
                dv_s[sl, :] = dv
                out.append((ds_prev, dlg + dl))
            return tuple(out)

        zero_c = (jnp.zeros((RD, RD), F32), jnp.zeros((1, 1), F32))
        res = _chunk_loop(NCH, bstep, (zero_c, zero_c))
        dkc = [None, None]
        dvc = [None, None]
        for (rev, order_c, lg, rdv, ctx_states), (ds, dlg) in zip(dirs, res):
            for idx in (1, 0):
                n = order_c[idx]
                _, vjp = jax.vjp(functools.partial(_ret_state, reverse=rev), kcs[n], vcs[n], ctx_states[idx], lg)
                dk_c, dv_c, ds, dl = vjp(ds)
                dlg = dlg + dl
                dkc[n] = dk_c if dkc[n] is None else dkc[n] + dk_c
                dvc[n] = dv_c if dvc[n] is None else dvc[n] + dv_c
            drd_ref[int(rev):int(rev) + 1, :] += dlg * jax.nn.sigmoid(-rdv)
        dp_ref[:, 0:128] = _rope_t((dqf_s[...] + dqb_s[...]) * (RD ** -0.5), cos_v, sin_v).astype(BF16)
        dp_ref[:, 128:256] = _rope_t(dkf_s[...] + dkb_s[...], cos_v, sin_v).astype(BF16)
        dp_ref[:, 256:384] = (dvf_s[...] + dvb_s[...]).astype(BF16)
        zero = jnp.zeros((CH, RD), BF16)
        for n in (0, 1):
            rows = slice(n * CH, (n + 1) * CH)
            dpc_ref[rows, 0:128] = zero
            dpc_ref[rows, 128:256] = dkc[n].astype(BF16)
            dpc_ref[rows, 256:384] = dvc[n].astype(BF16)
            dpc_ref[rows, 384:512] = zero

        @pl.when(grid_step == RH * nb - 1)
        def _():
            hosted.finish(h_in, h_out, h_sems)

    h_in_specs, h_out_specs = hosted.specs()
    return pl.pallas_call(
        body, name="retention_bwd", grid=(RH, nb),
        in_specs=[sp["pret"], sp["pretc"], sp["head"], sp["head"], sp["rd"], sp["gn"], sp["tab"], sp["tab"]]
        + h_in_specs,
        out_specs=[
            pl.BlockSpec((None, SEQ, 512), lambda h, b: (b, 0, h)),
            pl.BlockSpec((None, LC, 512), lambda h, b: (b, 0, h)),
            pl.BlockSpec((None, 2, 1), lambda h, b: (h, 0, 0)),
            pl.BlockSpec((None, 1, RD), lambda h, b: (h, 0, 0)),
        ] + h_out_specs,
        out_shape=[
            jax.ShapeDtypeStruct((nb, SEQ, IN_W), BF16),
            jax.ShapeDtypeStruct((nb, LC, IN_W), BF16),
            jax.ShapeDtypeStruct((RH, 2, 1), F32),
            jax.ShapeDtypeStruct((RH, 1, RD), F32),
        ] + hosted.out_shape,
        scratch_shapes=[pltpu.VMEM((SEQ, RD), F32)] * 9 + [pltpu.VMEM((NCH, RD, RD), F32)] * 2 + hosted.scratch,
        compiler_params=_cp(("arbitrary", "arbitrary")),
    )(pret, pretc, o_all, dmixin, rd, gn, cos, sin, *hosted.args)


def _rpb_flat(rpb):
    return jnp.pad(rpb, ((0, 0), (0, 1), (0, 33))).reshape(NPAIR, 2, 1, 1024)


def _rpb_flat_t(dflat):
    return dflat.reshape(8, 16, 64)[:, :15, :31]


def _barrel(x, left):
    row = lax.broadcasted_iota(jnp.int32, x.shape, 0)
    n = x.shape[1]
    for bit in range(6):
        s = 1 << bit
        x = jnp.where(((row >> bit) & 1) == 1, pltpu.roll(x, (n - s) if left else s, 1), x)
    return x


NA_TILE_ROWS, NA_BAND_ROWS = 4, 12
NA_Q, NA_K = NA_TILE_ROWS * GW, NA_BAND_ROWS * GW
NA_TILES = SEQ // NA_Q


def _tile_rows(cls):
    if cls == 0:
        return [(qr, 0) for qr in range(4)]
    if cls == 1:
        return [(4, qr) for qr in range(4)]
    return [(4, 4), (5, 4), (6, 4), (7, 4)]


def _na_tile(t):
    start = jnp.clip(4 * t - 4, 0, 32 - NA_BAND_ROWS)
    cls = jnp.where(t == 0, 0, jnp.where(t == NA_TILES - 1, 2, 1))
    return pl.ds(pl.multiple_of(t * NA_Q, NA_Q), NA_Q), pl.ds(pl.multiple_of(start * GW, NA_Q), NA_K), cls


def _na_probs(qst, kb, kc, bias):
    s_loc = _nt(qst, kb) * 0.125 + bias
    s_ctx = _nt(qst, kc) * 0.125
    m = jnp.maximum(jnp.max(s_loc, axis=1, keepdims=True), jnp.max(s_ctx, axis=1, keepdims=True))
    e_loc, e_ctx = jnp.exp(s_loc - m), jnp.exp(s_ctx - m)
    den = jnp.sum(e_loc, axis=1, keepdims=True) + jnp.sum(e_ctx, axis=1, keepdims=True)
    return e_loc / den, e_ctx / den


def _stack_heads(t):
    lane = lax.broadcasted_iota(jnp.int32, t.shape, 1)
    zero = jnp.zeros_like(t)
    return jnp.concatenate([jnp.where(lane < 64, t, zero), jnp.where(lane >= 64, t, zero)], axis=0)


def _unstack_heads(t):
    n = t.shape[0] // 2
    lane = lax.broadcasted_iota(jnp.int32, (n, 128), 1)
    return jnp.where(lane < 64, t[:n], t[n:])


def na_bias_table(flat):
    def body(flat_ref, out_ref):
        qc = lax.broadcasted_iota(jnp.int32, (GW, 512), 0)
        kc = lax.broadcasted_iota(jnp.int32, (GW, 512), 1) & 63
        start = jnp.clip(qc - 8, 0, GW - 16)
        window = (kc >= start) & (kc < start + 16)
        fill = jnp.full((GW, NA_K - 512), NEG, F32)
        for hh in (0, 1):
            skew = _barrel(pltpu.roll(jnp.broadcast_to(flat_ref[hh], (GW, 1024)), 1024 - 15, 1), left=False)
            by_class = [jnp.where(window, (skew if rc == 7 else pltpu.roll(skew, (9 + rc) * 64, 1))[:, 0:512], NEG)
                        for rc in range(8)]
            for cls in range(3):
                for qr, (rc, off) in enumerate(_tile_rows(cls)):
                    w = jnp.concatenate([by_class[rc], fill], axis=1)
                    rows = slice(hh * NA_Q + qr * GW, hh * NA_Q + (qr + 1) * GW)
                    out_ref[cls, rows, :] = pltpu.roll(w, off * GW, 1) if off else w

    return pl.pallas_call(
        body, name="na_bias_table", grid=(NPAIR,),
        in_specs=[pl.BlockSpec((None, 2, 1, 1024), lambda p: (p, 0, 0, 0))],
        out_specs=pl.BlockSpec((None, 3, 2 * NA_Q, NA_K), lambda p: (p, 0, 0, 0)),
        out_shape=jax.ShapeDtypeStruct((NPAIR, 3, 2 * NA_Q, NA_K), F32),
    )(flat)


def na_fwd(pna, pnac, bias, mixin, hosted):
    nb = pna.shape[0]

    def body(*refs):
        (p_ref, pc_ref, bias_ref, _), h_in, (out_ref,), h_out, _, h_sems = hosted.split(refs, 4, 1)
        grid_step = pl.program_id(0) * nb + pl.program_id(1)

        @pl.when(grid_step == 0)
        def _():
            hosted.start(h_in, h_out, h_sems)

        kc, vc = pc_ref[:, 128:256], pc_ref[:, 256:384]

        def tile(t, carry):
            qsl, bsl, cls = _na_tile(t)
            kb, vb = p_ref[bsl, 128:256], p_ref[bsl, 256:384]
            p_loc, p_ctx = _na_probs(_stack_heads(p_ref[qsl, 0:128]), kb, kc, bias_ref[cls])
            out_ref[qsl, :] = _unstack_heads(_nn(p_loc, vb) + _nn(p_ctx, vc)).astype(BF16)
            return carry

        lax.fori_loop(0, NA_TILES, tile, 0, unroll=4)

        @pl.when(grid_step == NPAIR * nb - 1)
        def _():
            hosted.finish(h_in, h_out, h_sems)

    h_in_specs, h_out_specs = hosted.specs()
    return pl.pallas_call(
        body, name="na_fwd", grid=(NPAIR, nb),
        in_specs=[
            pl.BlockSpec((None, SEQ, 384), lambda p, b: (b, 0, p)),
            pl.BlockSpec((None, LC, 384), lambda p, b: (b, 0, p)),
            pl.BlockSpec((None, 3, 2 * NA_Q, NA_K), lambda p, b: (p, 0, 0, 0)),
            pl.BlockSpec(memory_space=pl.ANY),
        ] + h_in_specs,
        out_specs=[pl.BlockSpec((None, SEQ, 128), lambda p, b: (b, 0, 4 + p))] + h_out_specs,
        out_shape=[jax.ShapeDtypeStruct((nb, SEQ, D), BF16)] + hosted.out_shape,
        input_output_aliases={3: 0},
        scratch_shapes=hosted.scratch,
        compiler_params=_cp(("arbitrary", "arbitrary")),
    )(pna, pnac, bias, mixin, *hosted.args)


def na_bwd(pna, pnac, bias, dmixin, dproj, dprojc, hosted):
    nb = pna.shape[0]

    def body(*refs):
        own_in, h_in, own_out, h_out, own_scr, h_sems = hosted.split(refs, 6, 3)
        p_ref, pc_ref, bias_ref, dmix_ref = own_in[:4]
        dp_ref, dpc_ref, dpat_ref = own_out
        dbias_s, dk_s, dv_s, dkc_s, dvc_s, res_s, resc_s = own_scr
        b, part = pl.program_id(1), pl.program_id(2)
        grid_step = (pl.program_id(0) * nb + b) * 3 + part

        @pl.when(grid_step == 0)
        def _():
            hosted.start(h_in, h_out, h_sems)

        @pl.when(grid_step == NPAIR * nb * 3 - 1)
        def _():
            hosted.finish(h_in, h_out, h_sems)

        @pl.when(part == 0)
        def _():
            @pl.when(b == 0)
            def _():
                dbias_s[...] = jnp.zeros_like(dbias_s)

            dk_s[...] = jnp.zeros_like(dk_s)
            dv_s[...] = jnp.zeros_like(dv_s)
            dkc_s[...] = jnp.zeros_like(dkc_s)
            dvc_s[...] = jnp.zeros_like(dvc_s)
            kc, vc = pc_ref[:, 128:256], pc_ref[:, 256:384]

            def tile(t, carry):
                qsl, bsl, cls = _na_tile(t)
                kb, vb = p_ref[bsl, 128:256], p_ref[bsl, 256:384]
                qst, dost = _stack_heads(p_ref[qsl, 0:128]), _stack_heads(dmix_ref[qsl, :])
                p_loc, p_ctx = _na_probs(qst, kb, kc, bias_ref[cls])
                dp_loc, dp_ctx = _nt(dost, vb), _nt(dost, vc)
                delta = (jnp.sum(p_loc * dp_loc, axis=1, keepdims=True)
                         + jnp.sum(p_ctx * dp_ctx, axis=1, keepdims=True))
                ds_loc, ds_ctx = p_loc * (dp_loc - delta), p_ctx * (dp_ctx - delta)
                dbias_s[cls] += ds_loc
                res_s[0, qsl, :] = _unstack_heads((_nn(ds_loc, kb) + _nn(ds_ctx, kc)) * 0.125).astype(BF16)
                dk_s[bsl, :] += _tn(ds_loc, qst) * 0.125
                dv_s[bsl, :] += _tn(p_loc, dost)
                dkc_s[...] += _tn(ds_ctx, qst) * 0.125
                dvc_s[...] += _tn(p_ctx, dost)
                return carry

            lax.fori_loop(0, NA_TILES, tile, 0, unroll=2)
            res_s[1] = dk_s[...].astype(BF16)
            res_s[2] = dv_s[...].astype(BF16)
            resc_s[0] = jnp.zeros((LC, 128), BF16)
            resc_s[1] = dkc_s[...].astype(BF16)
            resc_s[2] = dvc_s[...].astype(BF16)

            @pl.when(b == nb - 1)
            def _():
                for hh in (0, 1):
                    by_class = [None] * 8
                    for cls in range(3):
                        for qr, (rc, off) in enumerate(_tile_rows(cls)):
                            w = dbias_s[cls, hh * NA_Q + qr * GW:hh * NA_Q + (qr + 1) * GW, :]
                            w = (pltpu.roll(w, NA_K - off * GW, 1) if off else w)[:, 0:512]
                            by_class[rc] = w if by_class[rc] is None else by_class[rc] + w
                    skew = jnp.zeros((GW, 1024), F32)
                    for rc in range(8):
                        w = jnp.concatenate([by_class[rc], jnp.zeros((GW, 512), F32)], axis=1)
                        skew = skew + (w if rc == 7 else pltpu.roll(w, (7 - rc) * 64, 1))
                    dpat_ref[hh] = jnp.sum(pltpu.roll(_barrel(skew, left=True), 15, 1), axis=0, keepdims=True)

        dp_ref[...] = res_s[part]
        dpc_ref[...] = resc_s[part]

    h_in_specs, h_out_specs = hosted.specs()
    return pl.pallas_call(
        body, name="na_bwd", grid=(NPAIR, nb, 3),
        in_specs=[
            pl.BlockSpec((None, SEQ, 384), lambda p, b, s: (b, 0, p)),
            pl.BlockSpec((None, LC, 384), lambda p, b, s: (b, 0, p)),
            pl.BlockSpec((None, 3, 2 * NA_Q, NA_K), lambda p, b, s: (p, 0, 0, 0)),
            pl.BlockSpec((None, SEQ, 128), lambda p, b, s: (b, 0, 4 + p)),
            pl.BlockSpec(memory_space=pl.ANY),
            pl.BlockSpec(memory_space=pl.ANY),
        ] + h_in_specs,
        out_specs=[
            pl.BlockSpec((None, SEQ, 128), lambda p, b, s: (b, 0, 16 + 3 * p + s)),
            pl.BlockSpec((None, LC, 128), lambda p, b, s: (b, 0, 16 + 3 * p + s)),
            pl.BlockSpec((None, 2, 1, 1024), lambda p, b, s: (p, 0, 0, 0)),
        ] + h_out_specs,
        out_shape=[
            jax.ShapeDtypeStruct((nb, SEQ, IN_W), BF16),
            jax.ShapeDtypeStruct((nb, LC, IN_W), BF16),
            jax.ShapeDtypeStruct((NPAIR, 2, 1, 1024), F32),
        ] + hosted.out_shape,
        input_output_aliases={4: 0, 5: 1},
        scratch_shapes=[
            pltpu.VMEM((3, 2 * NA_Q, NA_K), F32),
            pltpu.VMEM((SEQ, 128), F32), pltpu.VMEM((SEQ, 128), F32),
            pltpu.VMEM((LC, 128), F32), pltpu.VMEM((LC, 128), F32),
            pltpu.VMEM((3, SEQ, 128), BF16), pltpu.VMEM((3, LC, 128), BF16),
        ] + hosted.scratch,
        compiler_params=_cp(("arbitrary", "arbitrary", "arbitrary")),
    )(pna, pnac, bias, dmixin, dproj, dprojc, *hosted.args)


def tail_fwd_bwd(x, mixin, tgt, mod3, g_post_mix, g_pre_mlp, g_post_mlp, wout, w1, w2):
    nb = x.shape[0]

    def body(x_ref, mi_ref, tgt_ref, mod_ref, gpm_ref, gpl_ref, gpo_ref, wo_ref, w1_ref, w2_ref,
             dx_ref, dmix_ref, h2_ref, du_ref, a_ref, dm_ref, dmi_ref, dmod_ref, dg_ref, loss_ref):
        b, t = pl.program_id(0), pl.program_id(1)
        gt1, sh2, sc2, gt2 = mod_ref[2:3, :], mod_ref[3:4, :], mod_ref[4:5, :], mod_ref[5:6, :]
        mix = jnp.dot(mi_ref[...], wo_ref[...], preferred_element_type=F32)
        (x1, h2), vjp_a = jax.vjp(_post_mix, x_ref[...], mix, gt1, sc2, sh2, gpm_ref[...], gpl_ref[...])
        h2b = h2.astype(BF16)
        h2_ref[...] = h2b
        m = jnp.zeros((TN, D), F32)
        relus = []
        for j in range(4):
            cols = slice(j * D, (j + 1) * D)
            r = jnp.maximum(jnp.dot(h2b, w1_ref[j], preferred_element_type=F32), 0.0)
            ab = (r * r).astype(BF16)
            a_ref[:, cols] = ab
            m = m + jnp.dot(ab, w2_ref[cols, :], preferred_element_type=F32)
            relus.append(r)
        loss, vjp_b = jax.vjp(_head_loss, x1, m, gt2, gpo_ref[...], tgt_ref[...])
        dx1, dm, dgt2, dgpo, _ = vjp_b(jnp.ones((1, 1), F32))
        dmb = dm.astype(BF16)
        dm_ref[...] = dmb
        dh2 = jnp.zeros((TN, D), F32)
        for j in range(4):
            cols = slice(j * D, (j + 1) * D)
            da = lax.dot_general(dmb, w2_ref[cols, :], (((1,), (1,)), ((), ())), preferred_element_type=F32)
            dub = (da * (2.0 * relus[j])).astype(BF16)
            du_ref[:, cols] = dub
            dh2 = dh2 + lax.dot_general(dub, w1_ref[j], (((1,), (1,)), ((), ())), preferred_element_type=F32)
        dx, dmix, dgt1, dsc2, dsh2, dgpm, dgpl = vjp_a((dx1, dh2))
        dx_ref[...] = dx
        dmixb = dmix.astype(BF16)
        dmix_ref[...] = dmixb
        dmi_ref[...] = lax.dot_general(dmixb, wo_ref[...], (((1,), (1,)), ((), ())),
                                       preferred_element_type=F32).astype(BF16)

        @pl.when(t == 0)
        def _():
            dmod_ref[...] = jnp.zeros_like(dmod_ref)

        @pl.when((t == 0) & (b == 0))
        def _():
            dg_ref[...] = jnp.zeros_like(dg_ref)
            loss_ref[...] = jnp.zeros_like(loss_ref)

        dmod_ref[2:3, :] += dgt1
        dmod_ref[3:4, :] += dsh2
        dmod_ref[4:5, :] += dsc2
        dmod_ref[5:6, :] += dgt2
        dg_ref[0:1, :] += dgpm
        dg_ref[1:2, :] += dgpl
        dg_ref[2:3, :] += dgpo
        loss_ref[...] += jnp.broadcast_to(loss, loss_ref.shape)

    tok = lambda b, t: (b, t, 0)
    const = lambda b, t: (0, 0)
    vec = pl.BlockSpec((1, D), const)
    return pl.pallas_call(
        body, name="tail_fwd_bwd", grid=(nb, SEQ // TN),
        in_specs=[
            pl.BlockSpec((None, TN, D), tok), pl.BlockSpec((None, TN, D), tok), pl.BlockSpec((None, TN, D), tok),
            pl.BlockSpec((None, 6, D), lambda b, t: (b, 0, 0)), vec, vec, vec,
            pl.BlockSpec((D, D), const, pipeline_mode=pl.Buffered(1)),
            pl.BlockSpec((4, D, D), lambda b, t: (0, 0, 0), pipeline_mode=pl.Buffered(1)),
            pl.BlockSpec((DFF, D), const, pipeline_mode=pl.Buffered(1)),
        ],
        out_specs=[
            pl.BlockSpec((None, TN, D), tok), pl.BlockSpec((None, TN, D), tok), pl.BlockSpec((None, TN, D), tok),
            pl.BlockSpec((None, TN, DFF), tok), pl.BlockSpec((None, TN, DFF), tok), pl.BlockSpec((None, TN, D), tok),
            pl.BlockSpec((None, TN, D), tok),
            pl.BlockSpec((None, 6, D), lambda b, t: (b, 0, 0)),
            pl.BlockSpec((8, D), const), pl.BlockSpec((8, 128), const),
        ],
        out_shape=[
            jax.ShapeDtypeStruct((nb, SEQ, D), F32), jax.ShapeDtypeStruct((nb, SEQ, D), BF16),
            jax.ShapeDtypeStruct((nb, SEQ, D), BF16), jax.ShapeDtypeStruct((nb, SEQ, DFF), BF16),
            jax.ShapeDtypeStruct((nb, SEQ, DFF), BF16), jax.ShapeDtypeStruct((nb, SEQ, D), BF16),
            jax.ShapeDtypeStruct((nb, SEQ, D), BF16),
            jax.ShapeDtypeStruct((nb, 6, D), F32), jax.ShapeDtypeStruct((8, D), F32),
            jax.ShapeDtypeStruct((8, 128), F32),
        ],
        compiler_params=_cp(("arbitrary", "arbitrary")),
    )(x, mixin, tgt, mod3, g_post_mix, g_pre_mlp, g_post_mlp, wout, w1, w2)


def weight_grad(pairs, name, out_dtype=F32, col_blocks=False, tm=1024, tn=1024, tk=2048):
    m, n = pairs[0][0].shape[1], pairs[0][1].shape[1]
    tn = min(tn, n)
    tks = [min(tk, xa.shape[0]) for xa, _ in pairs]
    steps = [xa.shape[0] // t for (xa, _), t in zip(pairs, tks)]
    total = sum(steps)
    offs = [sum(steps[:i]) for i in range(len(pairs))]

    def body(*refs):
        out_ref, acc = refs[2 * len(pairs)], refs[-1]
        k = pl.program_id(2)

        @pl.when(k == 0)
        def _():
            acc[...] = jnp.zeros_like(acc)

        for i in range(len(pairs)):
            @pl.when((k >= offs[i]) & (k < offs[i] + steps[i]))
            def _(i=i):
                acc[...] += lax.dot_general(refs[2 * i][...], refs[2 * i + 1][...], (((0,), (0,)), ((), ())),
                                            preferred_element_type=F32)

        if out_dtype != F32:
            @pl.when(k == total - 1)
            def _():
                out_ref[...] = acc[...].astype(out_dtype)

    in_specs, args = [], []
    for i, (xa, ya) in enumerate(pairs):
        clamp = lambda k, i=i: jnp.clip(k - offs[i], 0, steps[i] - 1)
        in_specs.append(pl.BlockSpec((tks[i], tm), lambda a, c, k, clamp=clamp: (clamp(k), a)))
        in_specs.append(pl.BlockSpec((tks[i], tn), lambda a, c, k, clamp=clamp: (clamp(k), c)))
        args += [xa, ya]
    if col_blocks:
        out_spec = pl.BlockSpec((None, tm, tn), lambda a, c, k: (c, a, 0))
        out_shape = jax.ShapeDtypeStruct((n // tn, m, tn), out_dtype)
    else:
        out_spec = pl.BlockSpec((tm, tn), lambda a, c, k: (a, c))
        out_shape = jax.ShapeDtypeStruct((m, n), out_dtype)
    return pl.pallas_call(
        body, name=name, grid=(m // tm, n // tn, total), in_specs=in_specs, out_specs=out_spec, out_shape=out_shape,
        scratch_shapes=[] if out_dtype == F32 else [pltpu.VMEM((tm, tn), F32)],
        compiler_params=_cp(("arbitrary", "arbitrary", "arbitrary")),
    )(*args)


def _perm_block(t):
    return 4 * (t % 4) + t // 4 if t < 16 else 16 + 3 * ((t - 16) % 4) + (t - 16) // 4


def unpack_w_in(blocks):
    def body(i_ref, o_ref):
        for t in range(28):
            p = _perm_block(t)
            o_ref[:, p * 128:(p + 1) * 128] = i_ref[t // 7, :, (t % 7) * 128:(t % 7 + 1) * 128]

    return pl.pallas_call(
        body, name="unpack_w_in", grid=(2,),
        in_specs=[pl.BlockSpec((4, D // 2, 896), lambda i: (0, i, 0))],
        out_specs=pl.BlockSpec((D // 2, IN_W), lambda i: (i, 0)),
        out_shape=jax.ShapeDtypeStruct((D, IN_W), BF16),
    )(blocks)


def pack_w_in(dw):
    def body(i_ref, o_ref):
        for t in range(28):
            p = _perm_block(t)
            o_ref[t // 7, :, (t % 7) * 128:(t % 7 + 1) * 128] = i_ref[:, p * 128:(p + 1) * 128].astype(BF16)

    return pl.pallas_call(
        body, name="pack_w_in", grid=(4,),
        in_specs=[pl.BlockSpec((D // 4, IN_W), lambda i: (i, 0))],
        out_specs=pl.BlockSpec((4, D // 4, 896), lambda i: (0, i, 0)),
        out_shape=jax.ShapeDtypeStruct((4, D, 896), BF16),
    )(dw)


def _place():
    return lax.axis_index("x"), lax.axis_index("y"), lax.axis_index("c")


class Hosted:
    def __init__(self, args, out_shape, scratch, start, finish):
        self.args, self.out_shape, self.scratch, self.start, self.finish = args, out_shape, scratch, start, finish

    def specs(self):
        hbm = pl.BlockSpec(memory_space=pl.ANY)
        return [hbm] * len(self.args), [hbm] * len(self.out_shape)

    def split(self, refs, n_in, n_out):
        a, b = len(self.args), len(self.out_shape)
        cuts = [n_in, n_in + a, n_in + a + n_out, n_in + a + n_out + b, len(refs) - len(self.scratch)]
        parts = [refs[i:j] for i, j in zip([0] + cuts, cuts + [len(refs)])]
        return parts[0], parts[1], parts[2], parts[3], parts[4], parts[5]


def no_exchange():
    return Hosted([], [], [], lambda *a: None, lambda *a: None)


def run_hosted(hosted, name):
    def body(*refs):
        _, ins, _, outs, _, sems = hosted.split(refs, 0, 0)
        hosted.start(ins, outs, sems)
        hosted.finish(ins, outs, sems)

    in_specs, out_specs = hosted.specs()
    return pl.pallas_call(body, name=name, in_specs=in_specs, out_specs=out_specs, out_shape=hosted.out_shape,
                          scratch_shapes=hosted.scratch)(*hosted.args)


def gather8(blocks):
    na = len(blocks)

    def copies(ins, outs, sems):
        send_sems, recv_sems, local_sem = sems
        x, y, c = _place()
        me, sibling = (x, y, c), (x, y, 1 - c)
        chips = [(1 - x, y), (x, 1 - y), (1 - x, 1 - y)]

        def slot(o_ref, px, py, pc):
            return o_ref.at[4 * px + 2 * py + pc]

        def copy(a, k, block, to, src=None):
            return pltpu.make_async_remote_copy(
                src_ref=slot(outs[a], *block) if src is None else src, dst_ref=slot(outs[a], *block),
                send_sem=send_sems.at[a, k], recv_sem=recv_sems.at[a, k], device_id=to, device_id_type=MESH)

        mine = [pltpu.make_async_copy(ins[a], slot(outs[a], *me), local_sem.at[a]) for a in range(na)]
        first = []
        for a in range(na):
            first.append(copy(a, 0, me, sibling, src=ins[a]))
            first += [copy(a, 1 + j, me, (*chip, c), src=ins[a]) for j, chip in enumerate(chips)]
        return copy, mine, first, me, sibling, chips, c

    def start(ins, outs, sems):
        _, mine, first, *_ = copies(ins, outs, sems)
        for cp in mine + first:
            cp.start()

    def finish(ins, outs, sems):
        copy, mine, first, me, sibling, chips, c = copies(ins, outs, sems)
        passed = []
        for j, chip in enumerate(chips):
            for a in range(na):
                copy(a, 1 + j, (*chip, c), me).wait_recv()
                cp = copy(a, 4 + j, (*chip, c), sibling)
                cp.start()
                passed.append(cp)
        for a in range(na):
            copy(a, 0, sibling, me).wait_recv()
            for j, chip in enumerate(chips):
                copy(a, 4 + j, (*chip, 1 - c), me).wait_recv()
        for cp in first + passed:
            cp.wait_send()
        for cp in mine:
            cp.wait()

    return Hosted(list(blocks), [jax.ShapeDtypeStruct((8,) + b.shape, b.dtype) for b in blocks],
                  [pltpu.SemaphoreType.DMA((na, 7)), pltpu.SemaphoreType.DMA((na, 7)), pltpu.SemaphoreType.DMA((na,))],
                  start, finish)


def all_gather8(blocks, name):
    return run_hosted(gather8(blocks), name)


def chips3(arrays):
    na = len(arrays)

    def copies(ins, outs, sems):
        send_sems, recv_sems = sems
        x, y, c = _place()
        return [pltpu.make_async_remote_copy(
            src_ref=ins[a].at[2 * px + py], dst_ref=outs[a].at[k], send_sem=send_sems.at[a, k],
            recv_sem=recv_sems.at[a, k], device_id=(px, py, c), device_id_type=MESH)
            for a in range(na) for k, (px, py) in enumerate([(1 - x, y), (x, 1 - y), (1 - x, 1 - y)])]

    def start(ins, outs, sems):
        for cp in copies(ins, outs, sems):
            cp.start()

    def finish(ins, outs, sems):
        for cp in copies(ins, outs, sems):
            cp.wait()

    return Hosted(list(arrays), [jax.ShapeDtypeStruct((3,) + a.shape[1:], a.dtype) for a in arrays],
                  [pltpu.SemaphoreType.DMA((na, 3)), pltpu.SemaphoreType.DMA((na, 3))], start, finish)


def siblings(arrays):
    na = len(arrays)

    def copies(ins, outs, sems):
        send_sems, recv_sems = sems
        x, y, c = _place()
        return [pltpu.make_async_remote_copy(
            src_ref=ins[a], dst_ref=outs[a], send_sem=send_sems.at[a], recv_sem=recv_sems.at[a],
            device_id=(x, y, 1 - c), device_id_type=MESH) for a in range(na)]

    def start(ins, outs, sems):
        for cp in copies(ins, outs, sems):
            cp.start()

    def finish(ins, outs, sems):
        for cp in copies(ins, outs, sems):
            cp.wait()

    return Hosted(list(arrays), [jax.ShapeDtypeStruct(a.shape, a.dtype) for a in arrays],
                  [pltpu.SemaphoreType.DMA((na,)), pltpu.SemaphoreType.DMA((na,))], start, finish)


def both(first, second):
    na, no, ns = len(first.args), len(first.out_shape), len(first.scratch)

    def start(ins, outs, sems):
        first.start(ins[:na], outs[:no], sems[:ns])
        second.start(ins[na:], outs[no:], sems[ns:])

    def finish(ins, outs, sems):
        first.finish(ins[:na], outs[:no], sems[:ns])
        second.finish(ins[na:], outs[no:], sems[ns:])

    return Hosted(first.args + second.args, first.out_shape + second.out_shape, first.scratch + second.scratch,
                  start, finish)


def siblings4(arrays):
    na = len(arrays)

    def copies(ins, outs, sems):
        send_sems, recv_sems = sems
        x, y, c = _place()
        return [pltpu.make_async_remote_copy(
            src_ref=ins[a].at[2 * j + 1 - c], dst_ref=outs[a].at[j],
            send_sem=send_sems.at[a, j], recv_sem=recv_sems.at[a, j],
            device_id=(x, y, 1 - c), device_id_type=MESH) for a in range(na) for j in range(4)]

    def start(ins, outs, sems):
        for cp in copies(ins, outs, sems):
            cp.start()

    def finish(ins, outs, sems):
        for cp in copies(ins, outs, sems):
            cp.wait()

    return Hosted(list(arrays), [jax.ShapeDtypeStruct((4,) + a.shape[1:], a.dtype) for a in arrays],
                  [pltpu.SemaphoreType.DMA((na, 4)), pltpu.SemaphoreType.DMA((na, 4))], start, finish)


def sibling_blocks(arrays, name):
    return run_hosted(siblings4(arrays), name)


def _row_tile(r):
    for cand in (512, 256, 128, 64, 32, 16, 8):
        if r % cand == 0:
            return cand
    return r


def chip_partial(place, g8s, landed4s, name):
    n = len(g8s)

    def body(place_ref, *refs):
        del place_ref
        for g_ref, l_ref, o_ref in zip(refs[:n], refs[n:2 * n], refs[2 * n:]):
            o_ref[...] = (g_ref[...].astype(F32) + l_ref[...].astype(F32)).astype(BF16)

    own = [pl.BlockSpec((None,) + g.shape[1:], lambda j, s: (2 * j + s[0], 0, 0)) for g in g8s]
    plain = [pl.BlockSpec((None,) + g.shape[1:], lambda j, s: (j, 0, 0)) for g in g8s]
    return pl.pallas_call(
        body, name=name,
        grid_spec=pltpu.PrefetchScalarGridSpec(num_scalar_prefetch=1, grid=(4,), in_specs=own + plain, out_specs=plain),
        out_shape=[jax.ShapeDtypeStruct((4,) + g.shape[1:], BF16) for g in g8s],
    )(place, *g8s, *landed4s)


def shard_sum(place, partial4s, landed3s, name):
    n = len(partial4s)

    def body(place_ref, *refs):
        del place_ref
        for p_ref, l_ref, o_ref in zip(refs[:n], refs[n:2 * n], refs[2 * n:]):
            acc = p_ref[...].astype(F32)
            for k in range(3):
                acc = acc + l_ref[k].astype(F32)
            o_ref[...] = acc

    def halves(p, lead):
        r, ccols = p.shape[1:]
        return (lead, r // 2, ccols)

    return pl.pallas_call(
        body, name=name,
        grid_spec=pltpu.PrefetchScalarGridSpec(
            num_scalar_prefetch=1, grid=(2,),
            in_specs=[pl.BlockSpec(halves(p, None), lambda i, s: (s[1], i, 0)) for p in partial4s]
            + [pl.BlockSpec(halves(p, 3), lambda i, s: (0, i, 0)) for p in partial4s],
            out_specs=[pl.BlockSpec(halves(p, None)[1:], lambda i, s: (i, 0)) for p in partial4s]),
        out_shape=[jax.ShapeDtypeStruct(p.shape[1:], F32) for p in partial4s],
    )(place, *partial4s, *landed3s)


def _adamw_math(w, g, m, v):
    m2 = B1 * m + (1.0 - B1) * g
    v2 = B2 * v + (1.0 - B2) * (g * g)
    m_hat = m2 / (1.0 - B1 ** STEP)
    v_hat = v2 / (1.0 - B2 ** STEP)
    return -LR * (m_hat / (jnp.sqrt(v_hat) + AEPS) + WD * w), m2, v2


def adamw_halves(place, w, mine, theirs, m, v, name):
    r, ccols = w.shape
    hr = r // 2
    tr = _row_tile(hr)
    nt = hr // tr

    def body(place_ref, w_ref, a_ref, b_ref, m_ref, v_ref, g_out, d_out, m_out, v_out):
        g = jnp.where(pl.program_id(0) == place_ref[0], a_ref[...], b_ref[...])
        d, m2, v2 = _adamw_math(w_ref[...], g, m_ref[...], v_ref[...])
        g_out[...] = g
        d_out[...] = d
        m_out[...] = m2
        v_out[...] = v2

    full = pl.BlockSpec((tr, ccols), lambda h, i, s: (h * nt + i, 0))
    part = pl.BlockSpec((tr, ccols), lambda h, i, s: (i, 0))
    return pl.pallas_call(
        body, name=name,
        grid_spec=pltpu.PrefetchScalarGridSpec(
            num_scalar_prefetch=1, grid=(2, nt), in_specs=[full, part, part, full, full], out_specs=[full] * 4),
        out_shape=[jax.ShapeDtypeStruct((r, ccols), F32)] * 4,
    )(place, w, mine, theirs, m, v)


def adamw_group(place, halved, plain, hosted, name):
    rows = halved[0][0].shape[0]
    tr = 64
    nt = rows // 2 // tr
    nh, npl = len(halved), len(plain)

    def body(place_ref, *refs):
        own_in, h_in, own_out, h_out, _, h_sems = hosted.split(refs, 5 * nh + 4 * npl, 4 * nh + 3 * npl)
        half = pl.program_id(0)
        grid_step = half * nt + pl.program_id(1)

        @pl.when(grid_step == 0)
        def _():
            hosted.start(h_in, h_out, h_sems)

        for i in range(nh):
            w_ref, a_ref, b_ref, m_ref, v_ref = own_in[5 * i:5 * i + 5]
            g = jnp.where(half == place_ref[0], a_ref[...], b_ref[...])
            res = (g,) + _adamw_math(w_ref[...], g, m_ref[...], v_ref[...])
            for o_ref, r in zip(own_out[4 * i:4 * i + 4], res):
                o_ref[...] = r
        for i in range(npl):
            w_ref, g_ref, m_ref, v_ref = own_in[5 * nh + 4 * i:5 * nh + 4 * i + 4]
            res = _adamw_math(w_ref[...], g_ref[...], m_ref[...], v_ref[...])
            for o_ref, r in zip(own_out[4 * nh + 3 * i:4 * nh + 3 * i + 3], res):
                o_ref[...] = r

        @pl.when(grid_step == 2 * nt - 1)
        def _():
            hosted.finish(h_in, h_out, h_sems)

    def full(cols):
        return pl.BlockSpec((tr, cols), lambda h, i, s: (h * nt + i, 0))

    def part(cols):
        return pl.BlockSpec((tr, cols), lambda h, i, s: (i, 0))

    in_specs, out_specs, out_shape, args = [], [], [], []
    for w, a, b, m, v in halved:
        cols = w.shape[1]
        in_specs += [full(cols), part(cols), part(cols), full(cols), full(cols)]
        out_specs += [full(cols)] * 4
        out_shape += [jax.ShapeDtypeStruct(w.shape, F32)] * 4
        args += [w, a, b, m, v]
    for w, g, m, v in plain:
        cols = w.shape[1]
        in_specs += [full(cols)] * 4
        out_specs += [full(cols)] * 3
        out_shape += [jax.ShapeDtypeStruct(w.shape, F32)] * 3
        args += [w, g, m, v]
    h_in_specs, h_out_specs = hosted.specs()
    return pl.pallas_call(
        body, name=name,
        grid_spec=pltpu.PrefetchScalarGridSpec(
            num_scalar_prefetch=1, grid=(2, nt), in_specs=in_specs + h_in_specs, out_specs=out_specs + h_out_specs,
            scratch_shapes=hosted.scratch),
        out_shape=out_shape + hosted.out_shape,
        compiler_params=_cp(("arbitrary", "arbitrary")),
    )(place, *args, *hosted.args)


def _silu(x):
    return x * jax.nn.sigmoid(x)


def mod_shard(cin, w_ada, b_shard):
    def body(c_ref, w_ref, b_ref, o_ref):
        o_ref[...] = _nn(_silu(c_ref[...]), w_ref[...]) + b_ref[...]

    return pl.pallas_call(
        body, name="mod_shard", grid=(3,),
        in_specs=[pl.BlockSpec((32, D), lambda j: (0, 0)), pl.BlockSpec((D, 512), lambda j: (0, j)),
                  pl.BlockSpec((1, 512), lambda j: (0, j))],
        out_specs=pl.BlockSpec((32, 512), lambda j: (0, j)),
        out_shape=jax.ShapeDtypeStruct((32, 1536), F32),
    )(cin, w_ada, b_shard)


def ada_grads(cin, gb, gc, w_ada):
    def body(c_ref, gb_ref, gc_ref, w_ref, gw_ref, pc_ref):
        ctx_tot = jnp.sum(gc_ref[...], axis=0, keepdims=True)
        rows = lax.broadcasted_iota(jnp.int32, (16, 512), 0)
        dm = jnp.concatenate([gb_ref[...], jnp.where(rows == 0, ctx_tot, 0.0)], axis=0)
        gw_ref[...] = _tn(_silu(c_ref[...]), dm)
        rows8 = lax.broadcasted_iota(jnp.int32, (8, 512), 0)
        part = _nt(jnp.where(rows8 == 0, ctx_tot, 0.0), w_ref[...])

        @pl.when(pl.program_id(0) == 0)
        def _():
            pc_ref[...] = jnp.zeros_like(pc_ref)

        pc_ref[...] += part

    return pl.pallas_call(
        body, name="ada_grads", grid=(3,),
        in_specs=[pl.BlockSpec((32, D), lambda j: (0, 0)), pl.BlockSpec((16, 512), lambda j: (0, j)),
                  pl.BlockSpec((8, 512), lambda j: (0, j)), pl.BlockSpec((D, 512), lambda j: (0, j))],
        out_specs=[pl.BlockSpec((D, 512), lambda j: (0, j)), pl.BlockSpec((8, D), lambda j: (0, 0))],
        out_shape=[jax.ShapeDtypeStruct((D, 1536), F32), jax.ShapeDtypeStruct((8, D), F32)],
    )(cin, gb, gc, w_ada)


SMALL_SUM_ROWS = 15


def small_update(gsm, gbf, gcf, pcg, params):
    n = len(params)

    def body(*refs):
        gsm_ref, gbf_ref, gcf_ref, pcg_ref = refs[:4]
        wmv, outs, loss_out = refs[4:4 + 3 * n], refs[4 + 3 * n:4 + 7 * n], refs[-1]
        acc = gsm_ref[0]
        for dev in range(1, 8):
            acc = acc + gsm_ref[dev]
        c_ctx = wmv[0][...]
        sg = jax.nn.sigmoid(c_ctx)
        dsilu = pcg_ref[0:1, :] + pcg_ref[2:3, :] + pcg_ref[4:5, :] + pcg_ref[6:7, :]
        lane = lax.broadcasted_iota(jnp.int32, (1, D), 1)
        last = acc[14:15, :]
        grads = [
            dsilu * (sg * (1.0 + c_ctx * (1.0 - sg))),
            jnp.sum(gbf_ref[...], axis=0, keepdims=True) + jnp.sum(gcf_ref[...], axis=0, keepdims=True),
            acc[0:1, :] + acc[1:2, :], acc[2:3, :], acc[3:4, :], acc[4:5, :],
            acc[5:6, 0:512], acc[6:14, :], jnp.where(lane < 8, last, 0.0),
        ]
        loss_out[...] = jnp.broadcast_to(jnp.sum(jnp.where(lane == 8, last, 0.0), axis=1, keepdims=True), (8, 128))
        for i, g in enumerate(grads):
            d, m2, v2 = _adamw_math(wmv[3 * i][...], g, wmv[3 * i + 1][...], wmv[3 * i + 2][...])
            outs[4 * i][...] = g
            outs[4 * i + 1][...] = d
            outs[4 * i + 2][...] = m2
            outs[4 * i + 3][...] = v2

    flat = [a for wmv in params for a in wmv]
    out_shape = [jax.ShapeDtypeStruct(w.shape, F32) for w, _, _ in params for _ in range(4)]
    return pl.pallas_call(
        body, name="small_update", out_shape=out_shape + [jax.ShapeDtypeStruct((8, 128), F32)],
    )(gsm, gbf, gcf, pcg, *flat)


def _pad_row(v, rows):
    flat = v.reshape(-1)
    return jnp.pad(flat, (0, rows * D - flat.shape[0])).reshape(rows, D)


def local_step(x, ctx, tgt, mod3, g_pre_mix, g_post_mix, g_pre_mlp, g_post_mlp, ret_decay, ret_gn, na_rpb,
               wperm, late_weights, early_grads):
    nb = x.shape[0]
    tokens = nb * SEQ
    cos, sin = _rope_tables()
    rd = ret_decay.T.reshape(RH, 2, 1)
    gn = ret_gn.reshape(RH, 1, RD)
    bias = na_bias_table(_rpb_flat(na_rpb))
    h, pret, pna = premix_proj(x, mod3, g_pre_mix, wperm, False, "premix_proj")
    hc, pretc, pnac = premix_proj(ctx, mod3, g_pre_mix, wperm, True, "premix_proj_ctx")
    o_all, mixin, gw_out = retention_fwd(pret, pretc, rd, gn, cos, sin, late_weights(0))
    mixin, gw1, gw2 = na_fwd(pna, pnac, bias, mixin, late_weights(1))
    dx_tail, dmix, h2, du, act, dm, dmixin, dmod_t, dg_t, loss_t = tail_fwd_bwd(
        x, mixin, tgt, mod3, g_post_mix, g_pre_mlp, g_post_mlp, gw_out.reshape(D, D), gw1.reshape(4, D, D),
        gw2.reshape(DFF, D))
    dw_out = weight_grad([(mixin.reshape(tokens, D), dmix.reshape(tokens, D))], "grad_w_out", BF16)
    dw1 = weight_grad([(h2.reshape(tokens, D), du.reshape(tokens, DFF))], "grad_w_mlp1", BF16, col_blocks=True)
    dw2 = weight_grad([(act.reshape(tokens, DFF), dm.reshape(tokens, D))], "grad_w_mlp2", BF16)
    dproj, dprojc, drd, dgn, *landed = retention_bwd(pret, pretc, o_all, dmixin, rd, gn, cos, sin,
                                                     early_grads[0](dw_out, dw1, dw2))
    dproj, dprojc, dpat, *early = na_bwd(pna, pnac, bias, dmixin, dproj, dprojc, early_grads[1](landed))
    dw_in = weight_grad([(h.reshape(tokens, D), dproj.reshape(tokens, IN_W)),
                         (hc.reshape(nb * LC, D), dprojc.reshape(nb * LC, IN_W))], "grad_w_in", tn=IN_W // 2, tk=1024)
    grad_x, dmod_a, dg_a, *late = premix_bwd(x, mod3, g_pre_mix, wperm, dproj, dx_tail, early_grads[2](dw_in),
                                             "premix_bwd")
    dmod_c, dg_c = premix_bwd(ctx, mod3, g_pre_mix, wperm, dprojc, None, no_exchange(), "premix_bwd_ctx")
    dmod = jnp.concatenate([jnp.concatenate([dmod_a[:, 0:2], dmod_t[:, 2:6]], axis=1), dmod_c], axis=0)
    last = jnp.pad(jnp.concatenate([drd[:, :, 0].T.reshape(8), loss_t[0, 0:1]]), (0, D - 9)).reshape(1, D)
    small = jnp.concatenate([dg_a[0:1], dg_c[0:1], dg_t[0:3], _pad_row(dgn, 1), dpat.reshape(8, D), last], axis=0)
    return grad_x, late, early, dmod, small


def kernel(x, c, ctx, c_ctx, w_ada, b_ada, g_pre_mix, g_post_mix, g_pre_mlp, g_post_mlp, w_in, ret_decay, ret_gn, na_rpb, w_out, w_mlp1, w_mlp2, loss_target, m_c_ctx, m_w_ada, m_b_ada, m_g_pre_mix, m_g_post_mix, m_g_pre_mlp, m_g_post_mlp, m_w_in, m_ret_decay, m_ret_gn, m_na_rpb, m_w_out, m_w_mlp1, m_w_mlp2, v_c_ctx, v_w_ada, v_b_ada, v_g_pre_mix, v_g_post_mix, v_g_pre_mlp, v_g_post_mlp, v_w_in, v_ret_decay, v_ret_gn, v_na_rpb, v_w_out, v_w_mlp1, v_w_mlp2):
    px, py, pc = _place()
    dev = 4 * px + 2 * py + pc
    chip = 2 * px + py

    def my_half(w2d):
        rows = w2d.shape[0] // 2
        return lax.dynamic_slice_in_dim(w2d, pc * rows, rows, 0)

    halves = [my_half(w[0]).astype(BF16) for w in (w_in, w_out, w_mlp1, w_mlp2)]
    gw_in, cg = all_gather8([halves[0], jnp.pad(c, ((0, 6), (0, 0)))], "gather_w_in")
    wperm = unpack_w_in(gw_in.reshape(4, D, 896))

    cin = jnp.pad(cg[:, 0:2].reshape(16, D), ((0, 16), (0, 0))) + jnp.pad(c_ctx[None], ((16, 15), (0, 0)))
    mod_mine = mod_shard(cin, w_ada[0], lax.dynamic_slice_in_dim(b_ada, chip * 1536, 1536, 1))
    (mg,) = all_gather8([mod_mine], "gather_mod")
    mod_all = jnp.concatenate([mg[0], mg[2], mg[4], mg[6]], axis=1)
    mod3 = (jnp.pad(lax.dynamic_slice_in_dim(mod_all, 2 * dev, 2, 0), ((0, 1), (0, 0)))
            + jnp.pad(mod_all[16:17], ((2, 0), (0, 0)))).reshape(3, 6, D)

    place = jnp.stack([pc, chip]).astype(jnp.int32)

    early_names = ["w_out", "w_mlp1", "w_mlp2"]
    early_g8, early_partial = [], []

    def early_a(dw_out, dw1, dw2):
        early_g8[:] = [dw_out.reshape(8, 128, D), dw1.reshape(8, 512, D), dw2.reshape(8, 512, D)]
        return siblings4(early_g8)

    def early_b(landed):
        early_partial[:] = chip_partial(place, early_g8, landed, "rs_chip_sum_early")
        return chips3(early_partial)

    late_partial = []

    def late_c(dw_in):
        g8_in = pack_w_in(dw_in).reshape(8, 512, 896)
        (landed_in,) = sibling_blocks([g8_in], "rs_sibling_w_in")
        late_partial[:] = chip_partial(place, [g8_in], [landed_in], "rs_chip_sum_w_in")
        return chips3(late_partial)

    grad_x, (landed3_in,), early_landed, dmod, small = local_step(
        x, ctx, loss_target, mod3, g_pre_mix, g_post_mix, g_pre_mlp, g_post_mlp, ret_decay[0], ret_gn, na_rpb[0],
        wperm, lambda k: gather8(halves[1:2] if k == 0 else halves[2:4]), (early_a, early_b, late_c))
    early_mine = shard_sum(place, early_partial, early_landed, "rs_shard_sum_early")

    pay = jnp.concatenate([dmod.reshape(18, D), small, jnp.zeros((40 - 18 - SMALL_SUM_ROWS, D), F32)], axis=0)
    *early_theirs, gs = run_hosted(both(siblings(early_mine), gather8([pay])), "rs_halves_early_gather_small")
    gbf = gs[:, 0:12].reshape(16, 6 * D)
    gcf = gs[:, 12:18].reshape(8, 6 * D)
    gw_ada, pc_part = ada_grads(cin, lax.dynamic_slice_in_dim(gbf, chip * 1536, 1536, 1),
                                lax.dynamic_slice_in_dim(gcf, chip * 1536, 1536, 1), w_ada[0])
    (mine_in,) = shard_sum(place, late_partial, [landed3_in], "rs_shard_sum_w_in")
    theirs_in, pcg = run_hosted(both(siblings([mine_in]), gather8([pc_part])), "rs_halves_w_in_gather_c_ctx")

    grouped = adamw_group(
        place,
        [(w_mlp1[0], early_mine[1], early_theirs[1], m_w_mlp1[0], v_w_mlp1[0]),
         (w_mlp2[0], early_mine[2], early_theirs[2], m_w_mlp2[0], v_w_mlp2[0])],
        [(w_ada[0], gw_ada, m_w_ada[0], v_w_ada[0])], no_exchange(), "adamw_group")
    d_ada, m_ada, v_ada = grouped[8:11]
    big = [
        [r[None] for r in adamw_halves(place, w_in[0], mine_in, theirs_in, m_w_in[0], v_w_in[0], "adamw_w_in")],
        [r[None] for r in adamw_halves(place, w_out[0], early_mine[0], early_theirs[0], m_w_out[0], v_w_out[0],
                                       "adamw_w_out")],
        [r[None] for r in grouped[0:4]], [r[None] for r in grouped[4:8]],
    ]

    def rpb_rows(t):
        return _rpb_flat(t[0]).reshape(8, D)

    def decay_row(t):
        return jnp.pad(t.reshape(1, 8), ((0, 0), (0, D - 8)))

    views = [lambda t: t.reshape(1, D), lambda t: t, lambda t: t, lambda t: t, lambda t: t, lambda t: t, lambda t: t,
             rpb_rows, decay_row]
    back = [lambda t: t.reshape(D), lambda t: t, lambda t: t, lambda t: t, lambda t: t, lambda t: t, lambda t: t,
            lambda t: _rpb_flat_t(t)[None], lambda t: t[:, 0:8].reshape(1, 2, 4)]
    small_w = (c_ctx, b_ada, g_pre_mix, g_post_mix, g_pre_mlp, g_post_mlp, ret_gn, na_rpb, ret_decay)
    small_m = (m_c_ctx, m_b_ada, m_g_pre_mix, m_g_post_mix, m_g_pre_mlp, m_g_post_mlp, m_ret_gn, m_na_rpb, m_ret_decay)
    small_v = (v_c_ctx, v_b_ada, v_g_pre_mix, v_g_post_mix, v_g_pre_mlp, v_g_post_mlp, v_ret_gn, v_na_rpb, v_ret_decay)
    *res, loss8 = small_update(gs[:, 18:18 + SMALL_SUM_ROWS], gbf, gcf, pcg[:, 0],
                               [(f(w), f(m), f(v)) for f, w, m, v in zip(views, small_w, small_m, small_v)])

    def leaves(ada, idx):
        s_c, s_b, s_g1, s_g2, s_g3, s_g4, s_gn, s_rpb, s_rd = [back[i](res[4 * i + idx]) for i in range(9)]
        return [s_c, ada[None], s_b, s_g1, s_g2, s_g3, s_g4, big[0][idx], s_rd, s_gn, s_rpb,
                big[1][idx], big[2][idx], big[3][idx]]

    return (loss8[0, 0], grad_x, *leaves(gw_ada, 0), *leaves(d_ada, 1), *leaves(m_ada, 2), *leaves(v_ada, 3))
```

```python
import functools

import jax
import jax.numpy as jnp
from jax import lax
from jax.experimental import pallas as pl
from jax.experimental.pallas import tpu as pltpu

F32, BF16 = jnp.float32, jnp.bfloat16
D = 1024
SEQ = 2048
LC = 256
GW = 64
RH, RD, CH = 4, 128, 128
NPAIR = 4
IN_W = 3584
RET_W = 2048
DFF = 4096
EPS = 1e-6
NEG = -1e30
TN = 256
NCH = SEQ // CH
LR, B1, B2, AEPS, WD, STEP = 0.001, 0.9, 0.999, 1e-08, 0.01, 10
MESH = pl.DeviceIdType.MESH
VMEM_LIMIT = 56 * 1024 * 1024


def _cp(sem=None):
    return pltpu.CompilerParams(dimension_semantics=sem, vmem_limit_bytes=VMEM_LIMIT)


def _nn(a, b):
    return jnp.dot(a.astype(BF16), b.astype(BF16), preferred_element_type=F32)


def _nt(a, b):
    return lax.dot_general(a.astype(BF16), b.astype(BF16), (((1,), (1,)), ((), ())), preferred_element_type=F32)


def _tn(a, b):
    return lax.dot_general(a.astype(BF16), b.astype(BF16), (((0,), (0,)), ((), ())), preferred_element_type=F32)


@jax.custom_vjp
def mm_nn(a, b):
    return _nn(a, b)


@jax.custom_vjp
def mm_nt(a, b):
    return _nt(a, b)


@jax.custom_vjp
def mm_tn(a, b):
    return _tn(a, b)


mm_nn.defvjp(lambda a, b: (_nn(a, b), (a, b)), lambda r, g: (_nt(g, r[1]), _tn(r[0], g)))
mm_nt.defvjp(lambda a, b: (_nt(a, b), (a, b)), lambda r, g: (_nn(g, r[1]), _tn(g, r[0])))
mm_tn.defvjp(lambda a, b: (_tn(a, b), (a, b)), lambda r, g: (_nt(r[1], g), _nn(r[0], g)))


def _rms(x):
    return x * lax.rsqrt(jnp.mean(x * x, axis=-1, keepdims=True) + EPS)


def _rms_mod(x, g, sc, sh):
    return (_rms(x) * g) * (1.0 + sc) + sh


def _post_mix(x, mix, gt1, sc2, sh2, g_post_mix, g_pre_mlp):
    x1 = x + gt1 * (_rms(mix) * g_post_mix)
    return x1, _rms_mod(x1, g_pre_mlp, sc2, sh2)


def _head_loss(x1, m, gt2, g_post_mlp, tgt):
    err = x1 + gt2 * (_rms(m) * g_post_mlp) - tgt
    return 0.5 * jnp.sum(jnp.mean(err * err, axis=-1, keepdims=True), axis=0, keepdims=True)


def _ln_gate(o, g, w):
    mu = jnp.mean(o, axis=-1, keepdims=True)
    var = jnp.mean(jnp.square(o - mu), axis=-1, keepdims=True)
    y = (o - mu) * lax.rsqrt(var + EPS)
    return (y * w) * (g * jax.nn.sigmoid(g))


def _swap32(x):
    lane = lax.broadcasted_iota(jnp.int32, x.shape, 1)
    return jnp.where((lane & 32) == 0, pltpu.roll(x, 96, 1), pltpu.roll(x, 32, 1))


def _rope(x, cos, sin):
    return x * cos + _swap32(x) * sin


def _rope_t(g, cos, sin):
    return g * cos + _swap32(g * sin)


def _rope_tables():
    tok = jnp.arange(SEQ)
    pos_r = (tok // GW).astype(F32)
    pos_c = (tok % GW).astype(F32)
    inv = 10000.0 ** (-jnp.arange(32, dtype=F32) / 32)
    ar = pos_r[:, None] * inv[None, :]
    ac = pos_c[:, None] * inv[None, :]
    cos = jnp.concatenate([jnp.cos(ar), jnp.cos(ar), jnp.cos(ac), jnp.cos(ac)], axis=-1)
    sin = jnp.concatenate([-jnp.sin(ar), jnp.sin(ar), -jnp.sin(ac), jnp.sin(ac)], axis=-1)
    return cos, sin


def _chunk_loop(n, body, init, k=4):
    def several(t, carry):
        for i in range(k):
            carry = body(k * t + i, carry)
        return carry

    return lax.fori_loop(0, n // k, several, init)


def _fiota(shape, dim):
    return lax.broadcasted_iota(jnp.int32, shape, dim).astype(F32)


def _ret_state(k, v, s, lg, reverse):
    pos = _fiota((CH, 1), 0)
    b_exp = pos if reverse else (CH - 1.0 - pos)
    return jnp.exp(lg * CH) * s + mm_tn(k * jnp.exp(lg * b_exp), v)


def _ret_chunk(q, k, v, s, lg, reverse):
    i = _fiota((CH, CH), 0)
    j = _fiota((CH, CH), 1)
    diff = (j - i) if reverse else (i - j)
    mask = (diff > 0) if reverse else (diff >= 0)
    decay = jnp.where(mask, jnp.exp(lg * jnp.where(mask, diff, 0.0)), 0.0)
    pos = _fiota((CH, 1), 0)
    a_exp = (CH - pos) if reverse else (pos + 1.0)
    o = mm_nn(mm_nt(q, k) * decay, v) + mm_nn(q * jnp.exp(lg * a_exp), s)
    return o, _ret_state(k, v, s, lg, reverse)


def premix_proj(xin, mod3, g_pre, wperm, is_ctx, name):
    nb, length, _ = xin.shape
    tn = min(TN, length)

    def body(x_ref, mod_ref, g_ref, w_ref, h_ref, pret_ref, pna_ref):
        h = _rms_mod(x_ref[...], g_ref[...], mod_ref[1:2, :], mod_ref[0:1, :])
        hb = h.astype(BF16)
        h_ref[...] = hb
        pret_ref[...] = jnp.dot(hb, w_ref[:, :RET_W], preferred_element_type=F32)
        pna_ref[...] = jnp.dot(hb, w_ref[:, RET_W:], preferred_element_type=F32).astype(BF16)

    return pl.pallas_call(
        body, name=name, grid=(nb, length // tn),
        in_specs=[
            pl.BlockSpec((None, tn, D), lambda b, t: (b, t, 0)),
            pl.BlockSpec((None, 6, D), (lambda b, t: (2, 0, 0)) if is_ctx else (lambda b, t: (b, 0, 0))),
            pl.BlockSpec((1, D), lambda b, t: (0, 0)),
            pl.BlockSpec((D, IN_W), lambda b, t: (0, 0), pipeline_mode=pl.Buffered(1)),
        ],
        out_specs=[
            pl.BlockSpec((None, tn, D), lambda b, t: (b, t, 0)),
            pl.BlockSpec((None, tn, RET_W), lambda b, t: (b, t, 0)),
            pl.BlockSpec((None, tn, IN_W - RET_W), lambda b, t: (b, t, 0)),
        ],
        out_shape=[
            jax.ShapeDtypeStruct((nb, length, D), BF16),
            jax.ShapeDtypeStruct((nb, length, RET_W), F32),
            jax.ShapeDtypeStruct((nb, length, IN_W - RET_W), BF16),
        ],
        compiler_params=_cp(("arbitrary", "arbitrary")),
    )(xin, mod3, g_pre, wperm)


def premix_bwd(xin, mod3, g_pre, wperm, dproj, dx_tail, hosted, name):
    nb, length, _ = xin.shape
    tn = min(TN, length)
    is_ctx = dx_tail is None

    def body(*refs):
        own_in, h_in, own_out, h_out, _, h_sems = hosted.split(refs, 5 if is_ctx else 6, 2 if is_ctx else 3)
        if is_ctx:
            (x_ref, mod_ref, g_ref, w_ref, dp_ref), (dmod_ref, dg_ref) = own_in, own_out
        else:
            (x_ref, mod_ref, g_ref, w_ref, dp_ref, dxt_ref), (dx_ref, dmod_ref, dg_ref) = own_in, own_out
        b, t = pl.program_id(0), pl.program_id(1)
        grid_step = b * (length // tn) + t

        @pl.when(grid_step == 0)
        def _():
            hosted.start(h_in, h_out, h_sems)

        @pl.when(grid_step == nb * (length // tn) - 1)
        def _():
            hosted.finish(h_in, h_out, h_sems)

        dh = lax.dot_general(dp_ref[...], w_ref[...], (((1,), (1,)), ((), ())), preferred_element_type=F32)
        _, vjp = jax.vjp(_rms_mod, x_ref[...], g_ref[...], mod_ref[1:2, :], mod_ref[0:1, :])
        dx, dg, dsc, dsh = vjp(dh)
        if not is_ctx:
            dx_ref[...] = dx + dxt_ref[...]

        @pl.when((t == 0) & ((b == 0) if is_ctx else True))
        def _():
            dmod_ref[...] = jnp.zeros_like(dmod_ref)

        @pl.when((t == 0) & (b == 0))
        def _():
            dg_ref[...] = jnp.zeros_like(dg_ref)

        dmod_ref[0:1, :] += dsh
        dmod_ref[1:2, :] += dsc
        dg_ref[0:1, :] += dg

    tok = lambda b, t: (b, t, 0)
    in_specs = [
        pl.BlockSpec((None, tn, D), tok),
        pl.BlockSpec((None, 6, D), (lambda b, t: (2, 0, 0)) if is_ctx else (lambda b, t: (b, 0, 0))),
        pl.BlockSpec((1, D), lambda b, t: (0, 0)),
        pl.BlockSpec((D, IN_W), lambda b, t: (0, 0), pipeline_mode=pl.Buffered(1)),
        pl.BlockSpec((None, tn, IN_W), tok),
    ]
    args = [xin, mod3, g_pre, wperm, dproj]
    out_specs = [
        pl.BlockSpec((None, 6, D), (lambda b, t: (0, 0, 0)) if is_ctx else (lambda b, t: (b, 0, 0))),
        pl.BlockSpec((8, D), lambda b, t: (0, 0)),
    ]
    out_shape = [jax.ShapeDtypeStruct((1 if is_ctx else nb, 6, D), F32), jax.ShapeDtypeStruct((8, D), F32)]
    if not is_ctx:
        in_specs.append(pl.BlockSpec((None, tn, D), tok))
        args.append(dx_tail)
        out_specs.insert(0, pl.BlockSpec((None, tn, D), tok))
        out_shape.insert(0, jax.ShapeDtypeStruct((nb, length, D), F32))
    h_in_specs, h_out_specs = hosted.specs()
    return pl.pallas_call(
        body, name=name, grid=(nb, length // tn), in_specs=in_specs + h_in_specs, out_specs=out_specs + h_out_specs,
        out_shape=out_shape + hosted.out_shape, scratch_shapes=hosted.scratch,
        compiler_params=_cp(("arbitrary", "arbitrary")),
    )(*args, *hosted.args)


def _ret_specs(order):
    def im(f):
        return lambda *g: f(*order(*g))
    return dict(
        pret=pl.BlockSpec((None, SEQ, 512), im(lambda b, h: (b, 0, h))),
        pretc=pl.BlockSpec((None, LC, 512), im(lambda b, h: (b, 0, h))),
        rd=pl.BlockSpec((None, 2, 1), im(lambda b, h: (h, 0, 0))),
        gn=pl.BlockSpec((None, 1, RD), im(lambda b, h: (h, 0, 0))),
        tab=pl.BlockSpec((SEQ, RD), im(lambda b, h: (0, 0))),
        head=pl.BlockSpec((None, SEQ, RD), im(lambda b, h: (b, 0, h))),
    )


def retention_fwd(pret, pretc, rd, gn, cos, sin, hosted):
    nb = pret.shape[0]
    sp = _ret_specs(lambda b, h: (b, h))

    def body(*refs):
        own_in, h_in, own_out, h_out, own_scr, h_sems = hosted.split(refs, 6, 2)
        p_ref, pc_ref, rd_ref, gn_ref, cos_ref, sin_ref = own_in
        (o_ref, mix_ref), (q_s, k_s, of_s, ob_s) = own_out, own_scr
        grid_step = pl.program_id(0) * RH + pl.program_id(1)

        @pl.when(grid_step == 0)
        def _():
            hosted.start(h_in, h_out, h_sems)

        cos_v, sin_v = cos_ref[...], sin_ref[...]
        q_s[...] = _rope(p_ref[:, 0:128], cos_v, sin_v) * (RD ** -0.5)
        k_s[...] = _rope(p_ref[:, 128:256], cos_v, sin_v)
        lgs, init = [], []
        for rev in (False, True):
            lg = jax.nn.log_sigmoid(rd_ref[int(rev):int(rev) + 1, :])
            s = jnp.zeros((RD, RD), F32)
            for n in ((1, 0) if rev else (0, 1)):
                s = _ret_state(pc_ref[n * CH:(n + 1) * CH, 128:256], pc_ref[n * CH:(n + 1) * CH, 256:384], s, lg, rev)
            lgs.append(lg)
            init.append(s)

        def step(t, carry):
            out = []
            for rev, o_s, s in ((False, of_s, carry[0]), (True, ob_s, carry[1])):
                n = (NCH - 1 - t) if rev else t
                sl = pl.ds(pl.multiple_of(n * CH, CH), CH)
                o, s2 = _ret_chunk(q_s[sl, :], k_s[sl, :], p_ref[sl, 256:384], s, lgs[int(rev)], rev)
                o_s[sl, :] = o
                out.append(s2)
            return tuple(out)

        _chunk_loop(NCH, step, tuple(init))
        o = of_s[...] + ob_s[...]
        o_ref[...] = o
        mix_ref[...] = _ln_gate(o, p_ref[:, 384:512], gn_ref[...]).astype(BF16)

        @pl.when(grid_step == nb * RH - 1)
        def _():
            hosted.finish(h_in, h_out, h_sems)

    h_in_specs, h_out_specs = hosted.specs()
    return pl.pallas_call(
        body, name="retention_fwd", grid=(nb, RH),
        in_specs=[sp["pret"], sp["pretc"], sp["rd"], sp["gn"], sp["tab"], sp["tab"]] + h_in_specs,
        out_specs=[sp["head"], sp["head"]] + h_out_specs,
        out_shape=[jax.ShapeDtypeStruct((nb, SEQ, RH * RD), F32), jax.ShapeDtypeStruct((nb, SEQ, D), BF16)]
        + hosted.out_shape,
        scratch_shapes=[pltpu.VMEM((SEQ, RD), F32)] * 4 + hosted.scratch,
        compiler_params=_cp(("arbitrary", "arbitrary")),
    )(pret, pretc, rd, gn, cos, sin, *hosted.args)


def retention_bwd(pret, pretc, o_all, dmixin, rd, gn, cos, sin, hosted):
    nb = pret.shape[0]
    sp = _ret_specs(lambda h, b: (b, h))

    def body(*refs):
        own_in, h_in, own_out, h_out, own_scr, h_sems = hosted.split(refs, 8, 4)
        p_ref, pc_ref, o_ref, dmix_ref, rd_ref, gn_ref, cos_ref, sin_ref = own_in
        dp_ref, dpc_ref, drd_ref, dgn_ref = own_out
        q_s, k_s, do_s, dqf_s, dkf_s, dvf_s, dqb_s, dkb_s, dvb_s, stf_s, stb_s = own_scr
        b = pl.program_id(1)
        grid_step = pl.program_id(0) * nb + b

        @pl.when(grid_step == 0)
        def _():
            hosted.start(h_in, h_out, h_sems)

        cos_v, sin_v = cos_ref[...], sin_ref[...]
        q_s[...] = _rope(p_ref[:, 0:128], cos_v, sin_v) * (RD ** -0.5)
        k_s[...] = _rope(p_ref[:, 128:256], cos_v, sin_v)
        _, gate_vjp = jax.vjp(_ln_gate, o_ref[...], p_ref[:, 384:512], gn_ref[...])
        do, dg, dgn = gate_vjp(dmix_ref[...].astype(F32))
        do_s[...] = do
        dp_ref[:, 384:512] = dg.astype(BF16)

        @pl.when(b == 0)
        def _():
            drd_ref[...] = jnp.zeros_like(drd_ref)
            dgn_ref[...] = jnp.zeros_like(dgn_ref)

        dgn_ref[...] += dgn
        kcs = [pc_ref[n * CH:(n + 1) * CH, 128:256] for n in (0, 1)]
        vcs = [pc_ref[n * CH:(n + 1) * CH, 256:384] for n in (0, 1)]
        dirs = []
        init = []
        for rev in (False, True):
            rdv = rd_ref[int(rev):int(rev) + 1, :]
            lg = jax.nn.log_sigmoid(rdv)
            order_c = (1, 0) if rev else (0, 1)
            s = jnp.zeros((RD, RD), F32)
            ctx_states = []
            for n in order_c:
                ctx_states.append(s)
                s = _ret_state(kcs[n], vcs[n], s, lg, rev)
            dirs.append((rev, order_c, lg, rdv, ctx_states))
            init.append(s)
        acc = ((dqf_s, dkf_s, dvf_s, stf_s), (dqb_s, dkb_s, dvb_s, stb_s))

        def fstep(t, carry):
            out = []
            for (rev, _, lg, _, _), (_, _, _, st_s), s in zip(dirs, acc, carry):
                n = (NCH - 1 - t) if rev else t
                sl = pl.ds(pl.multiple_of(n * CH, CH), CH)
                st_s[n] = s
                out.append(_ret_state(k_s[sl, :], p_ref[sl, 256:384], s, lg, rev))
            return tuple(out)

        _chunk_loop(NCH, fstep, tuple(init))

        def bstep(t, carry):
            out = []
            for (rev, _, lg, _, _), (dq_s, dk_s, dv_s, st_s), (ds, dlg) in zip(dirs, acc, carry):
                n = t if rev else (NCH - 1 - t)
                sl = pl.ds(pl.multiple_of(n * CH, CH), CH)
                _, vjp = jax.vjp(functools.partial(_ret_chunk, reverse=rev),
                                 q_s[sl, :], k_s[sl, :], p_ref[sl, 256:384], st_s[n], lg)
                dq, dk, dv, ds_prev, dl = vjp((do_s[sl, :], ds))
                dq_s[sl, :] = dq
                dk_s[sl, :] = dk
                dv_s[sl, :] = dv
                out.append((ds_prev, dlg + dl))
            return tuple(out)

        zero_c = (jnp.zeros((RD, RD), F32), jnp.zeros((1, 1), F32))
        res = _chunk_loop(NCH, bstep, (zero_c, zero_c))
        dkc = [None, None]
        dvc = [None, None]
        for (rev, order_c, lg, rdv, ctx_states), (ds, dlg) in zip(dirs, res):
            for idx in (1, 0):
                n = order_c[idx]
                _, vjp = jax.vjp(functools.partial(_ret_state, reverse=rev), kcs[n], vcs[n], ctx_states[idx], lg)
                dk_c, dv_c, ds, dl = vjp(ds)
                dlg = dlg + dl
                dkc[n] = dk_c if dkc[n] is None else dkc[n] + dk_c
                dvc[n] = dv_c if dvc[n] is None else dvc[n] + dv_c
            drd_ref[int(rev):int(rev) + 1, :] += dlg * jax.nn.sigmoid(-rdv)
        dp_ref[:, 0:128] = _rope_t((dqf_s[...] + dqb_s[...]) * (RD ** -0.5), cos_v, sin_v).astype(BF16)
        dp_ref[:, 128:256] = _rope_t(dkf_s[...] + dkb_s[...], cos_v, sin_v).astype(BF16)
        dp_ref[:, 256:384] = (dvf_s[...] + dvb_s[...]).astype(BF16)
        zero = jnp.zeros((CH, RD), BF16)
        for n in (0, 1):
            rows = slice(n * CH, (n + 1) * CH)
            dpc_ref[rows, 0:128] = zero
            dpc_ref[rows, 128:256] = dkc[n].astype(BF16)
            dpc_ref[rows, 256:384] = dvc[n].astype(BF16)
            dpc_ref[rows, 384:512] = zero

        @pl.when(grid_step == RH * nb - 1)
        def _():
            hosted.finish(h_in, h_out, h_sems)

    h_in_specs, h_out_specs = hosted.specs()
    return pl.pallas_call(
        body, name="retention_bwd", grid=(RH, nb),
        in_specs=[sp["pret"], sp["pretc"], sp["head"], sp["head"], sp["rd"], sp["gn"], sp["tab"], sp["tab"]]
        + h_in_specs,
        out_specs=[
            pl.BlockSpec((None, SEQ, 512), lambda h, b: (b, 0, h)),
            pl.BlockSpec((None, LC, 512), lambda h, b: (b, 0, h)),
            pl.BlockSpec((None, 2, 1), lambda h, b: (h, 0, 0)),
            pl.BlockSpec((None, 1, RD), lambda h, b: (h, 0, 0)),
        ] + h_out_specs,
        out_shape=[
            jax.ShapeDtypeStruct((nb, SEQ, IN_W), BF16),
            jax.ShapeDtypeStruct((nb, LC, IN_W), BF16),
            jax.ShapeDtypeStruct((RH, 2, 1), F32),
            jax.ShapeDtypeStruct((RH, 1, RD), F32),
        ] + hosted.out_shape,
        scratch_shapes=[pltpu.VMEM((SEQ, RD), F32)] * 9 + [pltpu.VMEM((NCH, RD, RD), F32)] * 2 + hosted.scratch,
        compiler_params=_cp(("arbitrary", "arbitrary")),
    )(pret, pretc, o_all, dmixin, rd, gn, cos, sin, *hosted.args)


def _rpb_flat(rpb):
    return jnp.pad(rpb, ((0, 0), (0, 1), (0, 33))).reshape(NPAIR, 2, 1, 1024)


def _rpb_flat_t(dflat):
    return dflat.reshape(8, 16, 64)[:, :15, :31]


def _barrel(x, left):
    row = lax.broadcasted_iota(jnp.int32, x.shape, 0)
    n = x.shape[1]
    for bit in range(6):
        s = 1 << bit
        x = jnp.where(((row >> bit) & 1) == 1, pltpu.roll(x, (n - s) if left else s, 1), x)
    return x


NA_TILE_ROWS, NA_BAND_ROWS = 4, 12
NA_Q, NA_K = NA_TILE_ROWS * GW, NA_BAND_ROWS * GW
NA_TILES = SEQ // NA_Q


def _tile_rows(cls):
    if cls == 0:
        return [(qr, 0) for qr in range(4)]
    if cls == 1:
        return [(4, qr) for qr in range(4)]
    return [(4, 4), (5, 4), (6, 4), (7, 4)]


def _na_tile(t):
    start = jnp.clip(4 * t - 4, 0, 32 - NA_BAND_ROWS)
    cls = jnp.where(t == 0, 0, jnp.where(t == NA_TILES - 1, 2, 1))
    return pl.ds(pl.multiple_of(t * NA_Q, NA_Q), NA_Q), pl.ds(pl.multiple_of(start * GW, NA_Q), NA_K), cls


def _na_probs(qst, kb, kc, bias):
    s_loc = _nt(qst, kb) * 0.125 + bias
    s_ctx = _nt(qst, kc) * 0.125
    m = jnp.maximum(jnp.max(s_loc, axis=1, keepdims=True), jnp.max(s_ctx, axis=1, keepdims=True))
    e_loc, e_ctx = jnp.exp(s_loc - m), jnp.exp(s_ctx - m)
    den = jnp.sum(e_loc, axis=1, keepdims=True) + jnp.sum(e_ctx, axis=1, keepdims=True)
    return e_loc / den, e_ctx / den


def _stack_heads(t):
    lane = lax.broadcasted_iota(jnp.int32, t.shape, 1)
    zero = jnp.zeros_like(t)
    return jnp.concatenate([jnp.where(lane < 64, t, zero), jnp.where(lane >= 64, t, zero)], axis=0)


def _unstack_heads(t):
    n = t.shape[0] // 2
    lane = lax.broadcasted_iota(jnp.int32, (n, 128), 1)
    return jnp.where(lane < 64, t[:n], t[n:])


def na_bias_table(flat):
    def body(flat_ref, out_ref):
        qc = lax.broadcasted_iota(jnp.int32, (GW, 512), 0)
        kc = lax.broadcasted_iota(jnp.int32, (GW, 512), 1) & 63
        start = jnp.clip(qc - 8, 0, GW - 16)
        window = (kc >= start) & (kc < start + 16)
        fill = jnp.full((GW, NA_K - 512), NEG, F32)
        for hh in (0, 1):
            skew = _barrel(pltpu.roll(jnp.broadcast_to(flat_ref[hh], (GW, 1024)), 1024 - 15, 1), left=False)
            by_class = [jnp.where(window, (skew if rc == 7 else pltpu.roll(skew, (9 + rc) * 64, 1))[:, 0:512], NEG)
                        for rc in range(8)]
            for cls in range(3):
                for qr, (rc, off) in enumerate(_tile_rows(cls)):
                    w = jnp.concatenate([by_class[rc], fill], axis=1)
                    rows = slice(hh * NA_Q + qr * GW, hh * NA_Q + (qr + 1) * GW)
                    out_ref[cls, rows, :] = pltpu.roll(w, off * GW, 1) if off else w

    return pl.pallas_call(
        body, name="na_bias_table", grid=(NPAIR,),
        in_specs=[pl.BlockSpec((None, 2, 1, 1024), lambda p: (p, 0, 0, 0))],
        out_specs=pl.BlockSpec((None, 3, 2 * NA_Q, NA_K), lambda p: (p, 0, 0, 0)),
        out_shape=jax.ShapeDtypeStruct((NPAIR, 3, 2 * NA_Q, NA_K), F32),
    )(flat)


def na_fwd(pna, pnac, bias, mixin, hosted):
    nb = pna.shape[0]

    def body(*refs):
        (p_ref, pc_ref, bias_ref, _), h_in, (out_ref,), h_out, _, h_sems = hosted.split(refs, 4, 1)
        grid_step = pl.program_id(0) * nb + pl.program_id(1)

        @pl.when(grid_step == 0)
        def _():
            hosted.start(h_in, h_out, h_sems)

        kc, vc = pc_ref[:, 128:256], pc_ref[:, 256:384]

        def tile(t, carry):
            qsl, bsl, cls = _na_tile(t)
            kb, vb = p_ref[bsl, 128:256], p_ref[bsl, 256:384]
            p_loc, p_ctx = _na_probs(_stack_heads(p_ref[qsl, 0:128]), kb, kc, bias_ref[cls])
            out_ref[qsl, :] = _unstack_heads(_nn(p_loc, vb) + _nn(p_ctx, vc)).astype(BF16)
            return carry

        lax.fori_loop(0, NA_TILES, tile, 0, unroll=4)

        @pl.when(grid_step == NPAIR * nb - 1)
        def _():
            hosted.finish(h_in, h_out, h_sems)

    h_in_specs, h_out_specs = hosted.specs()
    return pl.pallas_call(
        body, name="na_fwd", grid=(NPAIR, nb),
        in_specs=[
            pl.BlockSpec((None, SEQ, 384), lambda p, b: (b, 0, p)),
            pl.BlockSpec((None, LC, 384), lambda p, b: (b, 0, p)),
            pl.BlockSpec((None, 3, 2 * NA_Q, NA_K), lambda p, b: (p, 0, 0, 0)),
            pl.BlockSpec(memory_space=pl.ANY),
        ] + h_in_specs,
        out_specs=[pl.BlockSpec((None, SEQ, 128), lambda p, b: (b, 0, 4 + p))] + h_out_specs,
        out_shape=[jax.ShapeDtypeStruct((nb, SEQ, D), BF16)] + hosted.out_shape,
        input_output_aliases={3: 0},
        scratch_shapes=hosted.scratch,
        compiler_params=_cp(("arbitrary", "arbitrary")),
    )(pna, pnac, bias, mixin, *hosted.args)


def na_bwd(pna, pnac, bias, dmixin, dproj, dprojc, hosted):
    nb = pna.shape[0]

    def body(*refs):
        own_in, h_in, own_out, h_out, own_scr, h_sems = hosted.split(refs, 6, 3)
        p_ref, pc_ref, bias_ref, dmix_ref = own_in[:4]
        dp_ref, dpc_ref, dpat_ref = own_out
        dbias_s, dk_s, dv_s, dkc_s, dvc_s, res_s, resc_s = own_scr
        b, part = pl.program_id(1), pl.program_id(2)
        grid_step = (pl.program_id(0) * nb + b) * 3 + part

        @pl.when(grid_step == 0)
        def _():
            hosted.start(h_in, h_out, h_sems)

        @pl.when(grid_step == NPAIR * nb * 3 - 1)
        def _():
            hosted.finish(h_in, h_out, h_sems)

        @pl.when(part == 0)
        def _():
            @pl.when(b == 0)
            def _():
                dbias_s[...] = jnp.zeros_like(dbias_s)

            dk_s[...] = jnp.zeros_like(dk_s)
            dv_s[...] = jnp.zeros_like(dv_s)
            dkc_s[...] = jnp.zeros_like(dkc_s)
            dvc_s[...] = jnp.zeros_like(dvc_s)
            kc, vc = pc_ref[:, 128:256], pc_ref[:, 256:384]

            def tile(t, carry):
                qsl, bsl, cls = _na_tile(t)
                kb, vb = p_ref[bsl, 128:256], p_ref[bsl, 256:384]
                qst, dost = _stack_heads(p_ref[qsl, 0:128]), _stack_heads(dmix_ref[qsl, :])
                p_loc, p_ctx = _na_probs(qst, kb, kc, bias_ref[cls])
                dp_loc, dp_ctx = _nt(dost, vb), _nt(dost, vc)
                delta = (jnp.sum(p_loc * dp_loc, axis=1, keepdims=True)
                         + jnp.sum(p_ctx * dp_ctx, axis=1, keepdims=True))
                ds_loc, ds_ctx = p_loc * (dp_loc - delta), p_ctx * (dp_ctx - delta)
                dbias_s[cls] += ds_loc
                res_s[0, qsl, :] = _unstack_heads((_nn(ds_loc, kb) + _nn(ds_ctx, kc)) * 0.125).astype(BF16)
                dk_s[bsl, :] += _tn(ds_loc, qst) * 0.125
                dv_s[bsl, :] += _tn(p_loc, dost)
                dkc_s[...] += _tn(ds_ctx, qst) * 0.125
                dvc_s[...] += _tn(p_ctx, dost)
                return carry

            lax.fori_loop(0, NA_TILES, tile, 0, unroll=2)
            res_s[1] = dk_s[...].astype(BF16)
            res_s[2] = dv_s[...].astype(BF16)
            resc_s[0] = jnp.zeros((LC, 128), BF16)
            resc_s[1] = dkc_s[...].astype(BF16)
            resc_s[2] = dvc_s[...].astype(BF16)

            @pl.when(b == nb - 1)
            def _():
                for hh in (0, 1):
                    by_class = [None] * 8
                    for cls in range(3):
                        for qr, (rc, off) in enumerate(_tile_rows(cls)):
                            w = dbias_s[cls, hh * NA_Q + qr * GW:hh * NA_Q + (qr + 1) * GW, :]
                            w = (pltpu.roll(w, NA_K - off * GW, 1) if off else w)[:, 0:512]
                            by_class[rc] = w if by_class[rc] is None else by_class[rc] + w
                    skew = jnp.zeros((GW, 1024), F32)
                    for rc in range(8):
                        w = jnp.concatenate([by_class[rc], jnp.zeros((GW, 512), F32)], axis=1)
                        skew = skew + (w if rc == 7 else pltpu.roll(w, (7 - rc) * 64, 1))
                    dpat_ref[hh] = jnp.sum(pltpu.roll(_barrel(skew, left=True), 15, 1), axis=0, keepdims=True)

        dp_ref[...] = res_s[part]
        dpc_ref[...] = resc_s[part]

    h_in_specs, h_out_specs = hosted.specs()
    return pl.pallas_call(
        body, name="na_bwd", grid=(NPAIR, nb, 3),
        in_specs=[
            pl.BlockSpec((None, SEQ, 384), lambda p, b, s: (b, 0, p)),
            pl.BlockSpec((None, LC, 384), lambda p, b, s: (b, 0, p)),
            pl.BlockSpec((None, 3, 2 * NA_Q, NA_K), lambda p, b, s: (p, 0, 0, 0)),
            pl.BlockSpec((None, SEQ, 128), lambda p, b, s: (b, 0, 4 + p)),
            pl.BlockSpec(memory_space=pl.ANY),
            pl.BlockSpec(memory_space=pl.ANY),
        ] + h_in_specs,
        out_specs=[
            pl.BlockSpec((None, SEQ, 128), lambda p, b, s: (b, 0, 16 + 3 * p + s)),
            pl.BlockSpec((None, LC, 128), lambda p, b, s: (b, 0, 16 + 3 * p + s)),
            pl.BlockSpec((None, 2, 1, 1024), lambda p, b, s: (p, 0, 0, 0)),
        ] + h_out_specs,
        out_shape=[
            jax.ShapeDtypeStruct((nb, SEQ, IN_W), BF16),
            jax.ShapeDtypeStruct((nb, LC, IN_W), BF16),
            jax.ShapeDtypeStruct((NPAIR, 2, 1, 1024), F32),
        ] + hosted.out_shape,
        input_output_aliases={4: 0, 5: 1},
        scratch_shapes=[
            pltpu.VMEM((3, 2 * NA_Q, NA_K), F32),
            pltpu.VMEM((SEQ, 128), F32), pltpu.VMEM((SEQ, 128), F32),
            pltpu.VMEM((LC, 128), F32), pltpu.VMEM((LC, 128), F32),
            pltpu.VMEM((3, SEQ, 128), BF16), pltpu.VMEM((3, LC, 128), BF16),
        ] + hosted.scratch,
        compiler_params=_cp(("arbitrary", "arbitrary", "arbitrary")),
    )(pna, pnac, bias, dmixin, dproj, dprojc, *hosted.args)


def tail_fwd_bwd(x, mixin, tgt, mod3, g_post_mix, g_pre_mlp, g_post_mlp, wout, w1, w2):
    nb = x.shape[0]

    def body(x_ref, mi_ref, tgt_ref, mod_ref, gpm_ref, gpl_ref, gpo_ref, wo_ref, w1_ref, w2_ref,
             dx_ref, dmix_ref, h2_ref, du_ref, a_ref, dm_ref, dmi_ref, dmod_ref, dg_ref, loss_ref):
        b, t = pl.program_id(0), pl.program_id(1)
        gt1, sh2, sc2, gt2 = mod_ref[2:3, :], mod_ref[3:4, :], mod_ref[4:5, :], mod_ref[5:6, :]
        mix = jnp.dot(mi_ref[...], wo_ref[...], preferred_element_type=F32)
        (x1, h2), vjp_a = jax.vjp(_post_mix, x_ref[...], mix, gt1, sc2, sh2, gpm_ref[...], gpl_ref[...])
        h2b = h2.astype(BF16)
        h2_ref[...] = h2b
        m = jnp.zeros((TN, D), F32)
        relus = []
        for j in range(4):
            cols = slice(j * D, (j + 1) * D)
            r = jnp.maximum(jnp.dot(h2b, w1_ref[j], preferred_element_type=F32), 0.0)
            ab = (r * r).astype(BF16)
            a_ref[:, cols] = ab
            m = m + jnp.dot(ab, w2_ref[cols, :], preferred_element_type=F32)
            relus.append(r)
        loss, vjp_b = jax.vjp(_head_loss, x1, m, gt2, gpo_ref[...], tgt_ref[...])
        dx1, dm, dgt2, dgpo, _ = vjp_b(jnp.ones((1, 1), F32))
        dmb = dm.astype(BF16)
        dm_ref[...] = dmb
        dh2 = jnp.zeros((TN, D), F32)
        for j in range(4):
            cols = slice(j * D, (j + 1) * D)
            da = lax.dot_general(dmb, w2_ref[cols, :], (((1,), (1,)), ((), ())), preferred_element_type=F32)
            dub = (da * (2.0 * relus[j])).astype(BF16)
            du_ref[:, cols] = dub
            dh2 = dh2 + lax.dot_general(dub, w1_ref[j], (((1,), (1,)), ((), ())), preferred_element_type=F32)
        dx, dmix, dgt1, dsc2, dsh2, dgpm, dgpl = vjp_a((dx1, dh2))
        dx_ref[...] = dx
        dmixb = dmix.astype(BF16)
        dmix_ref[...] = dmixb
        dmi_ref[...] = lax.dot_general(dmixb, wo_ref[...], (((1,), (1,)), ((), ())),
                                       preferred_element_type=F32).astype(BF16)

        @pl.when(t == 0)
        def _():
            dmod_ref[...] = jnp.zeros_like(dmod_ref)

        @pl.when((t == 0) & (b == 0))
        def _():
            dg_ref[...] = jnp.zeros_like(dg_ref)
            loss_ref[...] = jnp.zeros_like(loss_ref)

        dmod_ref[2:3, :] += dgt1
        dmod_ref[3:4, :] += dsh2
        dmod_ref[4:5, :] += dsc2
        dmod_ref[5:6, :] += dgt2
        dg_ref[0:1, :] += dgpm
        dg_ref[1:2, :] += dgpl
        dg_ref[2:3, :] += dgpo
        loss_ref[...] += jnp.broadcast_to(loss, loss_ref.shape)

    tok = lambda b, t: (b, t, 0)
    const = lambda b, t: (0, 0)
    vec = pl.BlockSpec((1, D), const)
    return pl.pallas_call(
        body, name="tail_fwd_bwd", grid=(nb, SEQ // TN),
        in_specs=[
            pl.BlockSpec((None, TN, D), tok), pl.BlockSpec((None, TN, D), tok), pl.BlockSpec((None, TN, D), tok),
            pl.BlockSpec((None, 6, D), lambda b, t: (b, 0, 0)), vec, vec, vec,
            pl.BlockSpec((D, D), const, pipeline_mode=pl.Buffered(1)),
            pl.BlockSpec((4, D, D), lambda b, t: (0, 0, 0), pipeline_mode=pl.Buffered(1)),
            pl.BlockSpec((DFF, D), const, pipeline_mode=pl.Buffered(1)),
        ],
        out_specs=[
            pl.BlockSpec((None, TN, D), tok), pl.BlockSpec((None, TN, D), tok), pl.BlockSpec((None, TN, D), tok),
            pl.BlockSpec((None, TN, DFF), tok), pl.BlockSpec((None, TN, DFF), tok), pl.BlockSpec((None, TN, D), tok),
            pl.BlockSpec((None, TN, D), tok),
            pl.BlockSpec((None, 6, D), lambda b, t: (b, 0, 0)),
            pl.BlockSpec((8, D), const), pl.BlockSpec((8, 128), const),
        ],
        out_shape=[
            jax.ShapeDtypeStruct((nb, SEQ, D), F32), jax.ShapeDtypeStruct((nb, SEQ, D), BF16),
            jax.ShapeDtypeStruct((nb, SEQ, D), BF16), jax.ShapeDtypeStruct((nb, SEQ, DFF), BF16),
            jax.ShapeDtypeStruct((nb, SEQ, DFF), BF16), jax.ShapeDtypeStruct((nb, SEQ, D), BF16),
            jax.ShapeDtypeStruct((nb, SEQ, D), BF16),
            jax.ShapeDtypeStruct((nb, 6, D), F32), jax.ShapeDtypeStruct((8, D), F32),
            jax.ShapeDtypeStruct((8, 128), F32),
        ],
        compiler_params=_cp(("arbitrary", "arbitrary")),
    )(x, mixin, tgt, mod3, g_post_mix, g_pre_mlp, g_post_mlp, wout, w1, w2)


def weight_grad(pairs, name, out_dtype=F32, col_blocks=False, tm=1024, tn=1024, tk=2048):
    m, n = pairs[0][0].shape[1], pairs[0][1].shape[1]
    tn = min(tn, n)
    tks = [min(tk, xa.shape[0]) for xa, _ in pairs]
    steps = [xa.shape[0] // t for (xa, _), t in zip(pairs, tks)]
    total = sum(steps)
    offs = [sum(steps[:i]) for i in range(len(pairs))]

    def body(*refs):
        out_ref, acc = refs[2 * len(pairs)], refs[-1]
        k = pl.program_id(2)

        @pl.when(k == 0)
        def _():
            acc[...] = jnp.zeros_like(acc)

        for i in range(len(pairs)):
            @pl.when((k >= offs[i]) & (k < offs[i] + steps[i]))
            def _(i=i):
                acc[...] += lax.dot_general(refs[2 * i][...], refs[2 * i + 1][...], (((0,), (0,)), ((), ())),
                                            preferred_element_type=F32)

        if out_dtype != F32:
            @pl.when(k == total - 1)
            def _():
                out_ref[...] = acc[...].astype(out_dtype)

    in_specs, args = [], []
    for i, (xa, ya) in enumerate(pairs):
        clamp = lambda k, i=i: jnp.clip(k - offs[i], 0, steps[i] - 1)
        in_specs.append(pl.BlockSpec((tks[i], tm), lambda a, c, k, clamp=clamp: (clamp(k), a)))
        in_specs.append(pl.BlockSpec((tks[i], tn), lambda a, c, k, clamp=clamp: (clamp(k), c)))
        args += [xa, ya]
    if col_blocks:
        out_spec = pl.BlockSpec((None, tm, tn), lambda a, c, k: (c, a, 0))
        out_shape = jax.ShapeDtypeStruct((n // tn, m, tn), out_dtype)
    else:
        out_spec = pl.BlockSpec((tm, tn), lambda a, c, k: (a, c))
        out_shape = jax.ShapeDtypeStruct((m, n), out_dtype)
    return pl.pallas_call(
        body, name=name, grid=(m // tm, n // tn, total), in_specs=in_specs, out_specs=out_spec, out_shape=out_shape,
        scratch_shapes=[] if out_dtype == F32 else [pltpu.VMEM((tm, tn), F32)],
        compiler_params=_cp(("arbitrary", "arbitrary", "arbitrary")),
    )(*args)


def _perm_block(t):
    return 4 * (t % 4) + t // 4 if t < 16 else 16 + 3 * ((t - 16) % 4) + (t - 16) // 4


def unpack_w_in(blocks):
    def body(i_ref, o_ref):
        for t in range(28):
            p = _perm_block(t)
            o_ref[:, p * 128:(p + 1) * 128] = i_ref[t // 7, :, (t % 7) * 128:(t % 7 + 1) * 128]

    return pl.pallas_call(
        body, name="unpack_w_in", grid=(2,),
        in_specs=[pl.BlockSpec((4, D // 2, 896), lambda i: (0, i, 0))],
        out_specs=pl.BlockSpec((D // 2, IN_W), lambda i: (i, 0)),
        out_shape=jax.ShapeDtypeStruct((D, IN_W), BF16),
    )(blocks)


def pack_w_in(dw):
    def body(i_ref, o_ref):
        for t in range(28):
            p = _perm_block(t)
            o_ref[t // 7, :, (t % 7) * 128:(t % 7 + 1) * 128] = i_ref[:, p * 128:(p + 1) * 128].astype(BF16)

    return pl.pallas_call(
        body, name="pack_w_in", grid=(4,),
        in_specs=[pl.BlockSpec((D // 4, IN_W), lambda i: (i, 0))],
        out_specs=pl.BlockSpec((4, D // 4, 896), lambda i: (0, i, 0)),
        out_shape=jax.ShapeDtypeStruct((4, D, 896), BF16),
    )(dw)


def _place():
    return lax.axis_index("x"), lax.axis_index("y"), lax.axis_index("c")


class Hosted:
    def __init__(self, args, out_shape, scratch, start, finish):
        self.args, self.out_shape, self.scratch, self.start, self.finish = args, out_shape, scratch, start, finish

    def specs(self):
        hbm = pl.BlockSpec(memory_space=pl.ANY)
        return [hbm] * len(self.args), [hbm] * len(self.out_shape)

    def split(self, refs, n_in, n_out):
        a, b = len(self.args), len(self.out_shape)
        cuts = [n_in, n_in + a, n_in + a + n_out, n_in + a + n_out + b, len(refs) - len(self.scratch)]
        parts = [refs[i:j] for i, j in zip([0] + cuts, cuts + [len(refs)])]
        return parts[0], parts[1], parts[2], parts[3], parts[4], parts[5]


def no_exchange():
    return Hosted([], [], [], lambda *a: None, lambda *a: None)


def run_hosted(hosted, name):
    def body(*refs):
        _, ins, _, outs, _, sems = hosted.split(refs, 0, 0)
        hosted.start(ins, outs, sems)
        hosted.finish(ins, outs, sems)

    in_specs, out_specs = hosted.specs()
    return pl.pallas_call(body, name=name, in_specs=in_specs, out_specs=out_specs, out_shape=hosted.out_shape,
                          scratch_shapes=hosted.scratch)(*hosted.args)


def gather8(blocks):
    na = len(blocks)

    def copies(ins, outs, sems):
        send_sems, recv_sems, local_sem = sems
        x, y, c = _place()
        me, sibling = (x, y, c), (x, y, 1 - c)
        chips = [(1 - x, y), (x, 1 - y), (1 - x, 1 - y)]

        def slot(o_ref, px, py, pc):
            return o_ref.at[4 * px + 2 * py + pc]

        def copy(a, k, block, to, src=None):
            return pltpu.make_async_remote_copy(
                src_ref=slot(outs[a], *block) if src is None else src, dst_ref=slot(outs[a], *block),
                send_sem=send_sems.at[a, k], recv_sem=recv_sems.at[a, k], device_id=to, device_id_type=MESH)

        mine = [pltpu.make_async_copy(ins[a], slot(outs[a], *me), local_sem.at[a]) for a in range(na)]
        first = []
        for a in range(na):
            first.append(copy(a, 0, me, sibling, src=ins[a]))
            first += [copy(a, 1 + j, me, (*chip, c), src=ins[a]) for j, chip in enumerate(chips)]
        return copy, mine, first, me, sibling, chips, c

    def start(ins, outs, sems):
        _, mine, first, *_ = copies(ins, outs, sems)
        for cp in mine + first:
            cp.start()

    def finish(ins, outs, sems):
        copy, mine, first, me, sibling, chips, c = copies(ins, outs, sems)
        passed = []
        for j, chip in enumerate(chips):
            for a in range(na):
                copy(a, 1 + j, (*chip, c), me).wait_recv()
                cp = copy(a, 4 + j, (*chip, c), sibling)
                cp.start()
                passed.append(cp)
        for a in range(na):
            copy(a, 0, sibling, me).wait_recv()
            for j, chip in enumerate(chips):
                copy(a, 4 + j, (*chip, 1 - c), me).wait_recv()
        for cp in first + passed:
            cp.wait_send()
        for cp in mine:
            cp.wait()

    return Hosted(list(blocks), [jax.ShapeDtypeStruct((8,) + b.shape, b.dtype) for b in blocks],
                  [pltpu.SemaphoreType.DMA((na, 7)), pltpu.SemaphoreType.DMA((na, 7)), pltpu.SemaphoreType.DMA((na,))],
                  start, finish)


def chips3(arrays):
    na = len(arrays)

    def copies(ins, outs, sems):
        send_sems, recv_sems = sems
        x, y, c = _place()
        return [pltpu.make_async_remote_copy(
            src_ref=ins[a].at[2 * px + py], dst_ref=outs[a].at[k], send_sem=send_sems.at[a, k],
            recv_sem=recv_sems.at[a, k], device_id=(px, py, c), device_id_type=MESH)
            for a in range(na) for k, (px, py) in enumerate([(1 - x, y), (x, 1 - y), (1 - x, 1 - y)])]

    def start(ins, outs, sems):
        for cp in copies(ins, outs, sems):
            cp.start()

    def finish(ins, outs, sems):
        for cp in copies(ins, outs, sems):
            cp.wait()

    return Hosted(list(arrays), [jax.ShapeDtypeStruct((3,) + a.shape[1:], a.dtype) for a in arrays],
                  [pltpu.SemaphoreType.DMA((na, 3)), pltpu.SemaphoreType.DMA((na, 3))], start, finish)


def siblings(arrays):
    na = len(arrays)

    def copies(ins, outs, sems):
        send_sems, recv_sems = sems
        x, y, c = _place()
        return [pltpu.make_async_remote_copy(
            src_ref=ins[a], dst_ref=outs[a], send_sem=send_sems.at[a], recv_sem=recv_sems.at[a],
            device_id=(x, y, 1 - c), device_id_type=MESH) for a in range(na)]

    def start(ins, outs, sems):
        for cp in copies(ins, outs, sems):
            cp.start()

    def finish(ins, outs, sems):
        for cp in copies(ins, outs, sems):
            cp.wait()

    return Hosted(list(arrays), [jax.ShapeDtypeStruct(a.shape, a.dtype) for a in arrays],
                  [pltpu.SemaphoreType.DMA((na,)), pltpu.SemaphoreType.DMA((na,))], start, finish)


def both(first, second):
    na, no, ns = len(first.args), len(first.out_shape), len(first.scratch)

    def start(ins, outs, sems):
        first.start(ins[:na], outs[:no], sems[:ns])
        second.start(ins[na:], outs[no:], sems[ns:])

    def finish(ins, outs, sems):
        first.finish(ins[:na], outs[:no], sems[:ns])
        second.finish(ins[na:], outs[no:], sems[ns:])

    return Hosted(first.args + second.args, first.out_shape + second.out_shape, first.scratch + second.scratch,
                  start, finish)


def siblings4(arrays):
    na = len(arrays)

    def copies(ins, outs, sems):
        send_sems, recv_sems = sems
        x, y, c = _place()
        return [pltpu.make_async_remote_copy(
            src_ref=ins[a].at[2 * j + 1 - c], dst_ref=outs[a].at[j],
            send_sem=send_sems.at[a, j], recv_sem=recv_sems.at[a, j],
            device_id=(x, y, 1 - c), device_id_type=MESH) for a in range(na) for j in range(4)]

    def start(ins, outs, sems):
        for cp in copies(ins, outs, sems):
            cp.start()

    def finish(ins, outs, sems):
        for cp in copies(ins, outs, sems):
            cp.wait()

    return Hosted(list(arrays), [jax.ShapeDtypeStruct((4,) + a.shape[1:], a.dtype) for a in arrays],
                  [pltpu.SemaphoreType.DMA((na, 4)), pltpu.SemaphoreType.DMA((na, 4))], start, finish)


def sibling_blocks(arrays, name):
    return run_hosted(siblings4(arrays), name)


def _row_tile(r):
    for cand in (512, 256, 128, 64, 32, 16, 8):
        if r % cand == 0:
            return cand
    return r


def chip_partial(place, g8s, landed4s, name):
    n = len(g8s)

    def body(place_ref, *refs):
        del place_ref
        for g_ref, l_ref, o_ref in zip(refs[:n], refs[n:2 * n], refs[2 * n:]):
            o_ref[...] = (g_ref[...].astype(F32) + l_ref[...].astype(F32)).astype(BF16)

    own = [pl.BlockSpec((None,) + g.shape[1:], lambda j, s: (2 * j + s[0], 0, 0)) for g in g8s]
    plain = [pl.BlockSpec((None,) + g.shape[1:], lambda j, s: (j, 0, 0)) for g in g8s]
    return pl.pallas_call(
        body, name=name,
        grid_spec=pltpu.PrefetchScalarGridSpec(num_scalar_prefetch=1, grid=(4,), in_specs=own + plain, out_specs=plain),
        out_shape=[jax.ShapeDtypeStruct((4,) + g.shape[1:], BF16) for g in g8s],
    )(place, *g8s, *landed4s)


def shard_sum(place, partial4s, landed3s, name):
    n = len(partial4s)

    def body(place_ref, *refs):
        del place_ref
        for p_ref, l_ref, o_ref in zip(refs[:n], refs[n:2 * n], refs[2 * n:]):
            acc = p_ref[...].astype(F32)
            for k in range(3):
                acc = acc + l_ref[k].astype(F32)
            o_ref[...] = acc

    def halves(p, lead):
        r, ccols = p.shape[1:]
        return (lead, r // 2, ccols)

    return pl.pallas_call(
        body, name=name,
        grid_spec=pltpu.PrefetchScalarGridSpec(
            num_scalar_prefetch=1, grid=(2,),
            in_specs=[pl.BlockSpec(halves(p, None), lambda i, s: (s[1], i, 0)) for p in partial4s]
            + [pl.BlockSpec(halves(p, 3), lambda i, s: (0, i, 0)) for p in partial4s],
            out_specs=[pl.BlockSpec(halves(p, None)[1:], lambda i, s: (i, 0)) for p in partial4s]),
        out_shape=[jax.ShapeDtypeStruct(p.shape[1:], F32) for p in partial4s],
    )(place, *partial4s, *landed3s)


def _adamw_math(w, g, m, v):
    m2 = B1 * m + (1.0 - B1) * g
    v2 = B2 * v + (1.0 - B2) * (g * g)
    m_hat = m2 / (1.0 - B1 ** STEP)
    v_hat = v2 / (1.0 - B2 ** STEP)
    return -LR * (m_hat / (jnp.sqrt(v_hat) + AEPS) + WD * w), m2, v2


def adamw_halves(place, w, mine, theirs, m, v, name):
    r, ccols = w.shape
    hr = r // 2
    tr = _row_tile(hr)
    nt = hr // tr

    def body(place_ref, w_ref, a_ref, b_ref, m_ref, v_ref, g_out, d_out, m_out, v_out):
        g = jnp.where(pl.program_id(0) == place_ref[0], a_ref[...], b_ref[...])
        d, m2, v2 = _adamw_math(w_ref[...], g, m_ref[...], v_ref[...])
        g_out[...] = g
        d_out[...] = d
        m_out[...] = m2
        v_out[...] = v2

    full = pl.BlockSpec((tr, ccols), lambda h, i, s: (h * nt + i, 0))
    part = pl.BlockSpec((tr, ccols), lambda h, i, s: (i, 0))
    return pl.pallas_call(
        body, name=name,
        grid_spec=pltpu.PrefetchScalarGridSpec(
            num_scalar_prefetch=1, grid=(2, nt), in_specs=[full, part, part, full, full], out_specs=[full] * 4),
        out_shape=[jax.ShapeDtypeStruct((r, ccols), F32)] * 4,
    )(place, w, mine, theirs, m, v)


def adamw_group(place, halved, plain, hosted, name):
    rows = halved[0][0].shape[0]
    tr = 64
    nt = rows // 2 // tr
    nh, npl = len(halved), len(plain)

    def body(place_ref, *refs):
        own_in, h_in, own_out, h_out, _, h_sems = hosted.split(refs, 5 * nh + 4 * npl, 4 * nh + 3 * npl)
        half = pl.program_id(0)
        grid_step = half * nt + pl.program_id(1)

        @pl.when(grid_step == 0)
        def _():
            hosted.start(h_in, h_out, h_sems)

        for i in range(nh):
            w_ref, a_ref, b_ref, m_ref, v_ref = own_in[5 * i:5 * i + 5]
            g = jnp.where(half == place_ref[0], a_ref[...], b_ref[...])
            res = (g,) + _adamw_math(w_ref[...], g, m_ref[...], v_ref[...])
            for o_ref, r in zip(own_out[4 * i:4 * i + 4], res):
                o_ref[...] = r
        for i in range(npl):
            w_ref, g_ref, m_ref, v_ref = own_in[5 * nh + 4 * i:5 * nh + 4 * i + 4]
            res = _adamw_math(w_ref[...], g_ref[...], m_ref[...], v_ref[...])
            for o_ref, r in zip(own_out[4 * nh + 3 * i:4 * nh + 3 * i + 3], res):
                o_ref[...] = r

        @pl.when(grid_step == 2 * nt - 1)
        def _():
            hosted.finish(h_in, h_out, h_sems)

    def full(cols):
        return pl.BlockSpec((tr, cols), lambda h, i, s: (h * nt + i, 0))

    def part(cols):
        return pl.BlockSpec((tr, cols), lambda h, i, s: (i, 0))

    in_specs, out_specs, out_shape, args = [], [], [], []
    for w, a, b, m, v in halved:
        cols = w.shape[1]
        in_specs += [full(cols), part(cols), part(cols), full(cols), full(cols)]
        out_specs += [full(cols)] * 4
        out_shape += [jax.ShapeDtypeStruct(w.shape, F32)] * 4
        args += [w, a, b, m, v]
    for w, g, m, v in plain:
        cols = w.shape[1]
        in_specs += [full(cols)] * 4
        out_specs += [full(cols)] * 3
        out_shape += [jax.ShapeDtypeStruct(w.shape, F32)] * 3
        args += [w, g, m, v]
    h_in_specs, h_out_specs = hosted.specs()
    return pl.pallas_call(
        body, name=name,
        grid_spec=pltpu.PrefetchScalarGridSpec(
            num_scalar_prefetch=1, grid=(2, nt), in_specs=in_specs + h_in_specs, out_specs=out_specs + h_out_specs,
            scratch_shapes=hosted.scratch),
        out_shape=out_shape + hosted.out_shape,
        compiler_params=_cp(("arbitrary", "arbitrary")),
    )(place, *args, *hosted.args)


def _silu(x):
    return x * jax.nn.sigmoid(x)


def prologue(c_rows, c_ctx_row, w_ada, b_shard, half_w_in):
    shape = jax.ShapeDtypeStruct
    g_w = gather8([half_w_in])
    g_c = gather8([shape((8, D), F32)])
    g_m = gather8([shape((32, 1536), F32)])

    def body(c_ref, cc_ref, w_ref, b_ref, hw_ref, cin_ref, mg_ref, gw_ref, cg_s, ms_s, *sems):
        sw, sc, sm = sems[0:3], sems[3:6], sems[6:9]
        g_w.start([hw_ref], [gw_ref], sw)
        g_c.start([c_ref], [cg_s], sc)
        g_c.finish([c_ref], [cg_s], sc)
        cin_ref[...] = jnp.zeros_like(cin_ref)
        for dev in range(8):
            cin_ref[2 * dev:2 * dev + 2, :] = cg_s[dev, 0:2, :]
        cin_ref[16:17, :] = cc_ref[...]
        ms_s[...] = _nn(_silu(cin_ref[...]), w_ref[...]) + b_ref[...]
        g_m.start([ms_s], [mg_ref], sm)
        g_m.finish([ms_s], [mg_ref], sm)
        g_w.finish([hw_ref], [gw_ref], sw)

    vmem = pl.BlockSpec(memory_space=pltpu.VMEM)
    hbm = pl.BlockSpec(memory_space=pl.ANY)
    return pl.pallas_call(
        body, name="prologue", in_specs=[vmem, vmem, vmem, vmem, hbm], out_specs=[vmem, vmem, hbm],
        out_shape=[shape((32, D), F32), shape((8, 32, 1536), F32)] + g_w.out_shape,
        scratch_shapes=[pltpu.VMEM((8, 8, D), F32), pltpu.VMEM((32, 1536), F32)] + g_w.scratch + g_c.scratch
        + g_m.scratch,
        compiler_params=_cp(),
    )(c_rows, c_ctx_row, w_ada, b_shard, half_w_in)


def ada_grads(cin, gb, gc, w_ada):
    def body(c_ref, gb_ref, gc_ref, w_ref, gw_ref, pc_ref):
        ctx_tot = jnp.sum(gc_ref[...], axis=0, keepdims=True)
        rows = lax.broadcasted_iota(jnp.int32, (16, 512), 0)
        dm = jnp.concatenate([gb_ref[...], jnp.where(rows == 0, ctx_tot, 0.0)], axis=0)
        gw_ref[...] = _tn(_silu(c_ref[...]), dm)
        rows8 = lax.broadcasted_iota(jnp.int32, (8, 512), 0)
        part = _nt(jnp.where(rows8 == 0, ctx_tot, 0.0), w_ref[...])

        @pl.when(pl.program_id(0) == 0)
        def _():
            pc_ref[...] = jnp.zeros_like(pc_ref)

        pc_ref[...] += part

    return pl.pallas_call(
        body, name="ada_grads", grid=(3,),
        in_specs=[pl.BlockSpec((32, D), lambda j: (0, 0)), pl.BlockSpec((16, 512), lambda j: (0, j)),
                  pl.BlockSpec((8, 512), lambda j: (0, j)), pl.BlockSpec((D, 512), lambda j: (0, j))],
        out_specs=[pl.BlockSpec((D, 512), lambda j: (0, j)), pl.BlockSpec((8, D), lambda j: (0, 0))],
        out_shape=[jax.ShapeDtypeStruct((D, 1536), F32), jax.ShapeDtypeStruct((8, D), F32)],
    )(cin, gb, gc, w_ada)


SMALL_SUM_ROWS = 15


def small_update(gsm, gbf, gcf, pcg, params):
    n = len(params)

    def body(*refs):
        gsm_ref, gbf_ref, gcf_ref, pcg_ref = refs[:4]
        wmv, outs, loss_out = refs[4:4 + 3 * n], refs[4 + 3 * n:4 + 7 * n], refs[-1]
        acc = gsm_ref[0]
        for dev in range(1, 8):
            acc = acc + gsm_ref[dev]
        c_ctx = wmv[0][...]
        sg = jax.nn.sigmoid(c_ctx)
        dsilu = pcg_ref[0:1, :] + pcg_ref[2:3, :] + pcg_ref[4:5, :] + pcg_ref[6:7, :]
        lane = lax.broadcasted_iota(jnp.int32, (1, D), 1)
        last = acc[14:15, :]
        grads = [
            dsilu * (sg * (1.0 + c_ctx * (1.0 - sg))),
            jnp.sum(gbf_ref[...], axis=0, keepdims=True) + jnp.sum(gcf_ref[...], axis=0, keepdims=True),
            acc[0:1, :] + acc[1:2, :], acc[2:3, :], acc[3:4, :], acc[4:5, :],
            acc[5:6, 0:512], acc[6:14, :], jnp.where(lane < 8, last, 0.0),
        ]
        loss_out[...] = jnp.broadcast_to(jnp.sum(jnp.where(lane == 8, last, 0.0), axis=1, keepdims=True), (8, 128))
        for i, g in enumerate(grads):
            d, m2, v2 = _adamw_math(wmv[3 * i][...], g, wmv[3 * i + 1][...], wmv[3 * i + 2][...])
            outs[4 * i][...] = g
            outs[4 * i + 1][...] = d
            outs[4 * i + 2][...] = m2
            outs[4 * i + 3][...] = v2

    flat = [a for wmv in params for a in wmv]
    out_shape = [jax.ShapeDtypeStruct(w.shape, F32) for w, _, _ in params for _ in range(4)]
    return pl.pallas_call(
        body, name="small_update", out_shape=out_shape + [jax.ShapeDtypeStruct((8, 128), F32)],
    )(gsm, gbf, gcf, pcg, *flat)


def _pad_row(v, rows):
    flat = v.reshape(-1)
    return jnp.pad(flat, (0, rows * D - flat.shape[0])).reshape(rows, D)


def local_step(x, ctx, tgt, mod3, g_pre_mix, g_post_mix, g_pre_mlp, g_post_mlp, ret_decay, ret_gn, na_rpb,
               wperm, late_weights, early_grads):
    nb = x.shape[0]
    tokens = nb * SEQ
    cos, sin = _rope_tables()
    rd = ret_decay.T.reshape(RH, 2, 1)
    gn = ret_gn.reshape(RH, 1, RD)
    bias = na_bias_table(_rpb_flat(na_rpb))
    h, pret, pna = premix_proj(x, mod3, g_pre_mix, wperm, False, "premix_proj")
    hc, pretc, pnac = premix_proj(ctx, mod3, g_pre_mix, wperm, True, "premix_proj_ctx")
    o_all, mixin, gw_out = retention_fwd(pret, pretc, rd, gn, cos, sin, late_weights(0))
    mixin, gw1, gw2 = na_fwd(pna, pnac, bias, mixin, late_weights(1))
    dx_tail, dmix, h2, du, act, dm, dmixin, dmod_t, dg_t, loss_t = tail_fwd_bwd(
        x, mixin, tgt, mod3, g_post_mix, g_pre_mlp, g_post_mlp, gw_out.reshape(D, D), gw1.reshape(4, D, D),
        gw2.reshape(DFF, D))
    dw_out = weight_grad([(mixin.reshape(tokens, D), dmix.reshape(tokens, D))], "grad_w_out", BF16)
    dw1 = weight_grad([(h2.reshape(tokens, D), du.reshape(tokens, DFF))], "grad_w_mlp1", BF16, col_blocks=True)
    dw2 = weight_grad([(act.reshape(tokens, DFF), dm.reshape(tokens, D))], "grad_w_mlp2", BF16)
    dproj, dprojc, drd, dgn, *landed = retention_bwd(pret, pretc, o_all, dmixin, rd, gn, cos, sin,
                                                     early_grads[0](dw_out, dw1, dw2))
    dproj, dprojc, dpat, *early = na_bwd(pna, pnac, bias, dmixin, dproj, dprojc, early_grads[1](landed))
    dw_in = weight_grad([(h.reshape(tokens, D), dproj.reshape(tokens, IN_W)),
                         (hc.reshape(nb * LC, D), dprojc.reshape(nb * LC, IN_W))], "grad_w_in", tn=IN_W // 2, tk=1024)
    grad_x, dmod_a, dg_a, *late = premix_bwd(x, mod3, g_pre_mix, wperm, dproj, dx_tail, early_grads[2](dw_in),
                                             "premix_bwd")
    dmod_c, dg_c = premix_bwd(ctx, mod3, g_pre_mix, wperm, dprojc, None, no_exchange(), "premix_bwd_ctx")
    dmod = jnp.concatenate([jnp.concatenate([dmod_a[:, 0:2], dmod_t[:, 2:6]], axis=1), dmod_c], axis=0)
    last = jnp.pad(jnp.concatenate([drd[:, :, 0].T.reshape(8), loss_t[0, 0:1]]), (0, D - 9)).reshape(1, D)
    small = jnp.concatenate([dg_a[0:1], dg_c[0:1], dg_t[0:3], _pad_row(dgn, 1), dpat.reshape(8, D), last], axis=0)
    return grad_x, late, early, dmod, small


def kernel(x, c, ctx, c_ctx, w_ada, b_ada, g_pre_mix, g_post_mix, g_pre_mlp, g_post_mlp, w_in, ret_decay, ret_gn, na_rpb, w_out, w_mlp1, w_mlp2, loss_target, m_c_ctx, m_w_ada, m_b_ada, m_g_pre_mix, m_g_post_mix, m_g_pre_mlp, m_g_post_mlp, m_w_in, m_ret_decay, m_ret_gn, m_na_rpb, m_w_out, m_w_mlp1, m_w_mlp2, v_c_ctx, v_w_ada, v_b_ada, v_g_pre_mix, v_g_post_mix, v_g_pre_mlp, v_g_post_mlp, v_w_in, v_ret_decay, v_ret_gn, v_na_rpb, v_w_out, v_w_mlp1, v_w_mlp2):
    px, py, pc = _place()
    dev = 4 * px + 2 * py + pc
    chip = 2 * px + py

    def my_half(w2d):
        rows = w2d.shape[0] // 2
        return lax.dynamic_slice_in_dim(w2d, pc * rows, rows, 0)

    halves = [my_half(w[0]).astype(BF16) for w in (w_in, w_out, w_mlp1, w_mlp2)]
    cin, mg, gw_in = prologue(jnp.pad(c, ((0, 6), (0, 0))), c_ctx[None], w_ada[0],
                              lax.dynamic_slice_in_dim(b_ada, chip * 1536, 1536, 1), halves[0])
    wperm = unpack_w_in(gw_in.reshape(4, D, 896))
    mod_all = jnp.concatenate([mg[0], mg[2], mg[4], mg[6]], axis=1)
    mod3 = (jnp.pad(lax.dynamic_slice_in_dim(mod_all, 2 * dev, 2, 0), ((0, 1), (0, 0)))
            + jnp.pad(mod_all[16:17], ((2, 0), (0, 0)))).reshape(3, 6, D)

    place = jnp.stack([pc, chip]).astype(jnp.int32)

    early_names = ["w_out", "w_mlp1", "w_mlp2"]
    early_g8, early_partial = [], []

    def early_a(dw_out, dw1, dw2):
        early_g8[:] = [dw_out.reshape(8, 128, D), dw1.reshape(8, 512, D), dw2.reshape(8, 512, D)]
        return siblings4(early_g8)

    def early_b(landed):
        early_partial[:] = chip_partial(place, early_g8, landed, "rs_chip_sum_early")
        return chips3(early_partial)

    late_partial = []

    def late_c(dw_in):
        g8_in = pack_w_in(dw_in).reshape(8, 512, 896)
        (landed_in,) = sibling_blocks([g8_in], "rs_sibling_w_in")
        late_partial[:] = chip_partial(place, [g8_in], [landed_in], "rs_chip_sum_w_in")
        return chips3(late_partial)

    grad_x, (landed3_in,), early_landed, dmod, small = local_step(
        x, ctx, loss_target, mod3, g_pre_mix, g_post_mix, g_pre_mlp, g_post_mlp, ret_decay[0], ret_gn, na_rpb[0],
        wperm, lambda k: gather8(halves[1:2] if k == 0 else halves[2:4]), (early_a, early_b, late_c))
    early_mine = shard_sum(place, early_partial, early_landed, "rs_shard_sum_early")

    pay = jnp.concatenate([dmod.reshape(18, D), small, jnp.zeros((40 - 18 - SMALL_SUM_ROWS, D), F32)], axis=0)
    *early_theirs, gs = run_hosted(both(siblings(early_mine), gather8([pay])), "rs_halves_early_gather_small")
    gbf = gs[:, 0:12].reshape(16, 6 * D)
    gcf = gs[:, 12:18].reshape(8, 6 * D)
    gw_ada, pc_part = ada_grads(cin, lax.dynamic_slice_in_dim(gbf, chip * 1536, 1536, 1),
                                lax.dynamic_slice_in_dim(gcf, chip * 1536, 1536, 1), w_ada[0])
    (mine_in,) = shard_sum(place, late_partial, [landed3_in], "rs_shard_sum_w_in")
    theirs_in, pcg = run_hosted(both(siblings([mine_in]), gather8([pc_part])), "rs_halves_w_in_gather_c_ctx")

    grouped = adamw_group(
        place,
        [(w_mlp1[0], early_mine[1], early_theirs[1], m_w_mlp1[0], v_w_mlp1[0]),
         (w_mlp2[0], early_mine[2], early_theirs[2], m_w_mlp2[0], v_w_mlp2[0])],
        [(w_ada[0], gw_ada, m_w_ada[0], v_w_ada[0])], no_exchange(), "adamw_group")
    d_ada, m_ada, v_ada = grouped[8:11]
    big = [
        [r[None] for r in adamw_halves(place, w_in[0], mine_in, theirs_in, m_w_in[0], v_w_in[0], "adamw_w_in")],
        [r[None] for r in adamw_halves(place, w_out[0], early_mine[0], early_theirs[0], m_w_out[0], v_w_out[0],
                                       "adamw_w_out")],
        [r[None] for r in grouped[0:4]], [r[None] for r in grouped[4:8]],
    ]

    def rpb_rows(t):
        return _rpb_flat(t[0]).reshape(8, D)

    def decay_row(t):
        return jnp.pad(t.reshape(1, 8), ((0, 0), (0, D - 8)))

    views = [lambda t: t.reshape(1, D), lambda t: t, lambda t: t, lambda t: t, lambda t: t, lambda t: t, lambda t: t,
             rpb_rows, decay_row]
    back = [lambda t: t.reshape(D), lambda t: t, lambda t: t, lambda t: t, lambda t: t, lambda t: t, lambda t: t,
            lambda t: _rpb_flat_t(t)[None], lambda t: t[:, 0:8].reshape(1, 2, 4)]
    small_w = (c_ctx, b_ada, g_pre_mix, g_post_mix, g_pre_mlp, g_post_mlp, ret_gn, na_rpb, ret_decay)
    small_m = (m_c_ctx, m_b_ada, m_g_pre_mix, m_g_post_mix, m_g_pre_mlp, m_g_post_mlp, m_ret_gn, m_na_rpb, m_ret_decay)
    small_v = (v_c_ctx, v_b_ada, v_g_pre_mix, v_g_post_mix, v_g_pre_mlp, v_g_post_mlp, v_ret_gn, v_na_rpb, v_ret_decay)
    *res, loss8 = small_update(gs[:, 18:18 + SMALL_SUM_ROWS], gbf, gcf, pcg[:, 0],
                               [(f(w), f(m), f(v)) for f, w, m, v in zip(views, small_w, small_m, small_v)])

    def leaves(ada, idx):
        s_c, s_b, s_g1, s_g2, s_g3, s_g4, s_gn, s_rpb, s_rd = [back[i](res[4 * i + idx]) for i in range(9)]
        return [s_c, ada[None], s_b, s_g1, s_g2, s_g3, s_g4, big[0][idx], s_rd, s_gn, s_rpb,
                big[1][idx], big[2][idx], big[3][idx]]

    return (loss8[0, 0], grad_x, *leaves(gw_ada, 0), *leaves(d_ada, 1), *leaves(m_ada, 2), *leaves(v_ada, 3))
```

```python
import functools

import jax
import jax.numpy as jnp
from jax import lax
from jax.experimental import pallas as pl
from jax.experimental.pallas import tpu as pltpu

F32, BF16 = jnp.float32, jnp.bfloat16
D = 1024
SEQ = 2048
LC = 256
GW = 64
RH, RD, CH = 4, 128, 128
NPAIR = 4
IN_W = 3584
RET_W = 2048
DFF = 4096
EPS = 1e-6
NEG = -1e30
TN = 256
NCH = SEQ // CH
LR, B1, B2, AEPS, WD, STEP = 0.001, 0.9, 0.999, 1e-08, 0.01, 10
MESH = pl.DeviceIdType.MESH
VMEM_LIMIT = 56 * 1024 * 1024


def _cp(sem=None):
    return pltpu.CompilerParams(dimension_semantics=sem, vmem_limit_bytes=VMEM_LIMIT)


def _nn(a, b):
    return jnp.dot(a.astype(BF16), b.astype(BF16), preferred_element_type=F32)


def _nt(a, b):
    return lax.dot_general(a.astype(BF16), b.astype(BF16), (((1,), (1,)), ((), ())), preferred_element_type=F32)


def _tn(a, b):
    return lax.dot_general(a.astype(BF16), b.astype(BF16), (((0,), (0,)), ((), ())), preferred_element_type=F32)


@jax.custom_vjp
def mm_tn(a, b):
    return _tn(a, b)


mm_tn.defvjp(lambda a, b: (_tn(a, b), (a, b)), lambda r, g: (_nt(r[1], g), _nn(r[0], g)))


def _rms(x):
    return x * lax.rsqrt(jnp.mean(x * x, axis=-1, keepdims=True) + EPS)


def _rms_mod(x, g, sc, sh):
    return (_rms(x) * g) * (1.0 + sc) + sh


def _post_mix(x, mix, gt1, sc2, sh2, g_post_mix, g_pre_mlp):
    x1 = x + gt1 * (_rms(mix) * g_post_mix)
    return x1, _rms_mod(x1, g_pre_mlp, sc2, sh2)


def _head_loss(x1, m, gt2, g_post_mlp, tgt):
    err = x1 + gt2 * (_rms(m) * g_post_mlp) - tgt
    return 0.5 * jnp.sum(jnp.mean(err * err, axis=-1, keepdims=True), axis=0, keepdims=True)


def _ln_gate(o, g, w):
    mu = jnp.mean(o, axis=-1, keepdims=True)
    var = jnp.mean(jnp.square(o - mu), axis=-1, keepdims=True)
    y = (o - mu) * lax.rsqrt(var + EPS)
    return (y * w) * (g * jax.nn.sigmoid(g))


def _swap32(x):
    lane = lax.broadcasted_iota(jnp.int32, x.shape, 1)
    return jnp.where((lane & 32) == 0, pltpu.roll(x, 96, 1), pltpu.roll(x, 32, 1))


def _rope(x, cos, sin):
    return x * cos + _swap32(x) * sin


def _rope_t(g, cos, sin):
    return g * cos + _swap32(g * sin)


def _rope_tables():
    tok = jnp.arange(SEQ)
    pos_r = (tok // GW).astype(F32)
    pos_c = (tok % GW).astype(F32)
    inv = 10000.0 ** (-jnp.arange(32, dtype=F32) / 32)
    ar = pos_r[:, None] * inv[None, :]
    ac = pos_c[:, None] * inv[None, :]
    cos = jnp.concatenate([jnp.cos(ar), jnp.cos(ar), jnp.cos(ac), jnp.cos(ac)], axis=-1)
    sin = jnp.concatenate([-jnp.sin(ar), jnp.sin(ar), -jnp.sin(ac), jnp.sin(ac)], axis=-1)
    return cos, sin


def _chunk_loop(n, body, init, k=4):
    def several(t, carry):
        for i in range(k):
            carry = body(k * t + i, carry)
        return carry

    return lax.fori_loop(0, n // k, several, init)


def _fiota(shape, dim):
    return lax.broadcasted_iota(jnp.int32, shape, dim).astype(F32)


def _ret_state(k, v, s, lg, reverse):
    pos = _fiota((CH, 1), 0)
    b_exp = pos if reverse else (CH - 1.0 - pos)
    return jnp.exp(lg * CH) * s + mm_tn(k * jnp.exp(lg * b_exp), v)


class _Decays:
    def __init__(self, lgs):
        i, j, pos = _fiota((CH, CH), 0), _fiota((CH, CH), 1), _fiota((CH, 1), 0)
        diffs = (i - j, j - i)
        keep = (diffs[0] >= 0, diffs[1] > 0)
        mats = [jnp.where(m, jnp.exp(lg * jnp.where(m, d, 0.0)), 0.0) for lg, d, m in zip(lgs, diffs, keep)]
        self.mask = mats[0] + mats[1]
        self.dmask = [mats[0] * diffs[0], mats[1] * diffs[1]]
        a_exp, b_exp = (pos + 1.0, CH - pos), (CH - 1.0 - pos, pos)
        self.a = [jnp.exp(lg * e) for lg, e in zip(lgs, a_exp)]
        self.b = [jnp.exp(lg * e) for lg, e in zip(lgs, b_exp)]
        self.da = [a * e for a, e in zip(self.a, a_exp)]
        self.db = [b * e for b, e in zip(self.b, b_exp)]
        self.g = [jnp.exp(lg * CH) for lg in lgs]


def _both(x, w):
    return jnp.concatenate([x * w[0], x * w[1]], axis=1)


def _total(x):
    return jnp.sum(jnp.sum(x, axis=1, keepdims=True), axis=0, keepdims=True)


def _state_pass(dec, init, k_s, v_of, st_s):
    def step(t, carry):
        out = []
        for d, s in enumerate(carry):
            n = (NCH - 1 - t) if d else t
            sl = pl.ds(pl.multiple_of(n * CH, CH), CH)
            st_s[n, d * RD:(d + 1) * RD, :] = s
            out.append(dec.g[d] * s + _tn(k_s[sl, :] * dec.b[d], v_of(sl)))
        return tuple(out)

    _chunk_loop(NCH, step, tuple(init))


def premix_proj(xin, mod3, g_pre, wperm, is_ctx, name):
    nb, length, _ = xin.shape
    tn = min(TN, length)

    def body(x_ref, mod_ref, g_ref, w_ref, h_ref, pret_ref, pna_ref):
        h = _rms_mod(x_ref[...], g_ref[...], mod_ref[1:2, :], mod_ref[0:1, :])
        hb = h.astype(BF16)
        h_ref[...] = hb
        pret_ref[...] = jnp.dot(hb, w_ref[:, :RET_W], preferred_element_type=F32)
        pna_ref[...] = jnp.dot(hb, w_ref[:, RET_W:], preferred_element_type=F32).astype(BF16)

    return pl.pallas_call(
        body, name=name, grid=(nb, length // tn),
        in_specs=[
            pl.BlockSpec((None, tn, D), lambda b, t: (b, t, 0)),
            pl.BlockSpec((None, 6, D), (lambda b, t: (2, 0, 0)) if is_ctx else (lambda b, t: (b, 0, 0))),
            pl.BlockSpec((1, D), lambda b, t: (0, 0)),
            pl.BlockSpec((D, IN_W), lambda b, t: (0, 0), pipeline_mode=pl.Buffered(1)),
        ],
        out_specs=[
            pl.BlockSpec((None, tn, D), lambda b, t: (b, t, 0)),
            pl.BlockSpec((None, tn, RET_W), lambda b, t: (b, t, 0)),
            pl.BlockSpec((None, tn, IN_W - RET_W), lambda b, t: (b, t, 0)),
        ],
        out_shape=[
            jax.ShapeDtypeStruct((nb, length, D), BF16),
            jax.ShapeDtypeStruct((nb, length, RET_W), F32),
            jax.ShapeDtypeStruct((nb, length, IN_W - RET_W), BF16),
        ],
        compiler_params=_cp(("arbitrary", "arbitrary")),
    )(xin, mod3, g_pre, wperm)


def premix_bwd(xin, mod3, g_pre, wperm, dproj, dx_tail, hosted, name):
    nb, length, _ = xin.shape
    tn = min(TN, length)
    is_ctx = dx_tail is None

    def body(*refs):
        own_in, h_in, own_out, h_out, _, h_sems = hosted.split(refs, 5 if is_ctx else 6, 2 if is_ctx else 3)
        if is_ctx:
            (x_ref, mod_ref, g_ref, w_ref, dp_ref), (dmod_ref, dg_ref) = own_in, own_out
        else:
            (x_ref, mod_ref, g_ref, w_ref, dp_ref, dxt_ref), (dx_ref, dmod_ref, dg_ref) = own_in, own_out
        b, t = pl.program_id(0), pl.program_id(1)
        grid_step = b * (length // tn) + t

        @pl.when(grid_step == 0)
        def _():
            hosted.start(h_in, h_out, h_sems)

        @pl.when(grid_step == nb * (length // tn) - 1)
        def _():
            hosted.finish(h_in, h_out, h_sems)

        dh = lax.dot_general(dp_ref[...], w_ref[...], (((1,), (1,)), ((), ())), preferred_element_type=F32)
        _, vjp = jax.vjp(_rms_mod, x_ref[...], g_ref[...], mod_ref[1:2, :], mod_ref[0:1, :])
        dx, dg, dsc, dsh = vjp(dh)
        if not is_ctx:
            dx_ref[...] = dx + dxt_ref[...]

        @pl.when((t == 0) & ((b == 0) if is_ctx else True))
        def _():
            dmod_ref[...] = jnp.zeros_like(dmod_ref)

        @pl.when((t == 0) & (b == 0))
        def _():
            dg_ref[...] = jnp.zeros_like(dg_ref)

        dmod_ref[0:1, :] += dsh
        dmod_ref[1:2, :] += dsc
        dg_ref[0:1, :] += dg

    tok = lambda b, t: (b, t, 0)
    in_specs = [
        pl.BlockSpec((None, tn, D), tok),
        pl.BlockSpec((None, 6, D), (lambda b, t: (2, 0, 0)) if is_ctx else (lambda b, t: (b, 0, 0))),
        pl.BlockSpec((1, D), lambda b, t: (0, 0)),
        pl.BlockSpec((D, IN_W), lambda b, t: (0, 0), pipeline_mode=pl.Buffered(1)),
        pl.BlockSpec((None, tn, IN_W), tok),
    ]
    args = [xin, mod3, g_pre, wperm, dproj]
    out_specs = [
        pl.BlockSpec((None, 6, D), (lambda b, t: (0, 0, 0)) if is_ctx else (lambda b, t: (b, 0, 0))),
        pl.BlockSpec((8, D), lambda b, t: (0, 0)),
    ]
    out_shape = [jax.ShapeDtypeStruct((1 if is_ctx else nb, 6, D), F32), jax.ShapeDtypeStruct((8, D), F32)]
    if not is_ctx:
        in_specs.append(pl.BlockSpec((None, tn, D), tok))
        args.append(dx_tail)
        out_specs.insert(0, pl.BlockSpec((None, tn, D), tok))
        out_shape.insert(0, jax.ShapeDtypeStruct((nb, length, D), F32))
    h_in_specs, h_out_specs = hosted.specs()
    return pl.pallas_call(
        body, name=name, grid=(nb, length // tn), in_specs=in_specs + h_in_specs, out_specs=out_specs + h_out_specs,
        out_shape=out_shape + hosted.out_shape, scratch_shapes=hosted.scratch,
        compiler_params=_cp(("arbitrary", "arbitrary")),
    )(*args, *hosted.args)


def _ret_specs(order):
    def im(f):
        return lambda *g: f(*order(*g))
    return dict(
        pret=pl.BlockSpec((None, SEQ, 512), im(lambda b, h: (b, 0, h))),
        pretc=pl.BlockSpec((None, LC, 512), im(lambda b, h: (b, 0, h))),
        rd=pl.BlockSpec((None, 2, 1), im(lambda b, h: (h, 0, 0))),
        gn=pl.BlockSpec((None, 1, RD), im(lambda b, h: (h, 0, 0))),
        tab=pl.BlockSpec((SEQ, RD), im(lambda b, h: (0, 0))),
        head=pl.BlockSpec((None, SEQ, RD), im(lambda b, h: (b, 0, h))),
    )


def retention_fwd(pret, pretc, rd, gn, cos, sin, hosted):
    nb = pret.shape[0]
    sp = _ret_specs(lambda b, h: (b, h))

    def body(*refs):
        own_in, h_in, own_out, h_out, own_scr, h_sems = hosted.split(refs, 6, 2)
        p_ref, pc_ref, rd_ref, gn_ref, cos_ref, sin_ref = own_in
        (o_ref, mix_ref), (q_s, k_s, o_s, st_s) = own_out, own_scr
        grid_step = pl.program_id(0) * RH + pl.program_id(1)

        @pl.when(grid_step == 0)
        def _():
            hosted.start(h_in, h_out, h_sems)

        cos_v, sin_v = cos_ref[...], sin_ref[...]
        q_s[...] = _rope(p_ref[:, 0:128], cos_v, sin_v) * (RD ** -0.5)
        k_s[...] = _rope(p_ref[:, 128:256], cos_v, sin_v)
        lgs, init = [], []
        for rev in (False, True):
            lg = jax.nn.log_sigmoid(rd_ref[int(rev):int(rev) + 1, :])
            s = jnp.zeros((RD, RD), F32)
            for n in ((1, 0) if rev else (0, 1)):
                s = _ret_state(pc_ref[n * CH:(n + 1) * CH, 128:256], pc_ref[n * CH:(n + 1) * CH, 256:384], s, lg, rev)
            lgs.append(lg)
            init.append(s)

        dec = _Decays(lgs)
        _state_pass(dec, init, k_s, lambda sl: p_ref[sl, 256:384], st_s)

        def chunk(n, carry):
            sl = pl.ds(pl.multiple_of(n * CH, CH), CH)
            q = q_s[sl, :]
            o_s[sl, :] = (_nn(_nt(q, k_s[sl, :]) * dec.mask, p_ref[sl, 256:384]) + _nn(_both(q, dec.a), st_s[n]))
            return carry

        _chunk_loop(NCH, chunk, 0)
        o = o_s[...]
        o_ref[...] = o
        mix_ref[...] = _ln_gate(o, p_ref[:, 384:512], gn_ref[...]).astype(BF16)

        @pl.when(grid_step == nb * RH - 1)
        def _():
            hosted.finish(h_in, h_out, h_sems)

    h_in_specs, h_out_specs = hosted.specs()
    return pl.pallas_call(
        body, name="retention_fwd", grid=(nb, RH),
        in_specs=[sp["pret"], sp["pretc"], sp["rd"], sp["gn"], sp["tab"], sp["tab"]] + h_in_specs,
        out_specs=[sp["head"], sp["head"]] + h_out_specs,
        out_shape=[jax.ShapeDtypeStruct((nb, SEQ, RH * RD), F32), jax.ShapeDtypeStruct((nb, SEQ, D), BF16)]
        + hosted.out_shape,
        scratch_shapes=[pltpu.VMEM((SEQ, RD), F32)] * 3 + [pltpu.VMEM((NCH, 2 * RD, RD), F32)] + hosted.scratch,
        compiler_params=_cp(("arbitrary", "arbitrary")),
    )(pret, pretc, rd, gn, cos, sin, *hosted.args)


def retention_bwd(pret, pretc, o_all, dmixin, rd, gn, cos, sin, hosted):
    nb = pret.shape[0]
    sp = _ret_specs(lambda h, b: (b, h))

    def body(*refs):
        own_in, h_in, own_out, h_out, own_scr, h_sems = hosted.split(refs, 8, 4)
        p_ref, pc_ref, o_ref, dmix_ref, rd_ref, gn_ref, cos_ref, sin_ref = own_in
        dp_ref, dpc_ref, drd_ref, dgn_ref = own_out
        q_s, k_s, do_s, dq_s, dk_s, dv_s, st_s, gst_s = own_scr
        b = pl.program_id(1)
        grid_step = pl.program_id(0) * nb + b

        @pl.when(grid_step == 0)
        def _():
            hosted.start(h_in, h_out, h_sems)

        cos_v, sin_v = cos_ref[...], sin_ref[...]
        q_s[...] = _rope(p_ref[:, 0:128], cos_v, sin_v) * (RD ** -0.5)
        k_s[...] = _rope(p_ref[:, 128:256], cos_v, sin_v)
        _, gate_vjp = jax.vjp(_ln_gate, o_ref[...], p_ref[:, 384:512], gn_ref[...])
        do, dg, dgn = gate_vjp(dmix_ref[...].astype(F32))
        do_s[...] = do
        dp_ref[:, 384:512] = dg.astype(BF16)

        @pl.when(b == 0)
        def _():
            drd_ref[...] = jnp.zeros_like(drd_ref)
            dgn_ref[...] = jnp.zeros_like(dgn_ref)

        dgn_ref[...] += dgn
        kcs = [pc_ref[n * CH:(n + 1) * CH, 128:256] for n in (0, 1)]
        vcs = [pc_ref[n * CH:(n + 1) * CH, 256:384] for n in (0, 1)]
        dirs = []
        init = []
        for rev in (False, True):
            rdv = rd_ref[int(rev):int(rev) + 1, :]
            lg = jax.nn.log_sigmoid(rdv)
            order_c = (1, 0) if rev else (0, 1)
            s = jnp.zeros((RD, RD), F32)
            ctx_states = []
            for n in order_c:
                ctx_states.append(s)
                s = _ret_state(kcs[n], vcs[n], s, lg, rev)
            dirs.append((rev, order_c, lg, rdv, ctx_states))
            init.append(s)
        dec = _Decays([lg for _, _, lg, _, _ in dirs])

        def v_of(sl):
            return p_ref[sl, 256:384]

        _state_pass(dec, init, k_s, v_of, st_s)
        zeros = jnp.zeros((CH, RD), F32)

        def scores_back(n, carry):
            dmask_sum, da_f, da_b = carry
            sl = pl.ds(pl.multiple_of(n * CH, CH), CH)
            q, k, v, do = q_s[sl, :], k_s[sl, :], v_of(sl), do_s[sl, :]
            scores = _nt(q, k)
            d_att = _nt(do, v)
            d_scores = d_att * dec.mask
            d_qa = _nt(do, st_s[n])
            d_qf, d_qb = d_qa[:, 0:RD], d_qa[:, RD:2 * RD]
            dq_s[sl, :] = _nn(d_scores, k) + d_qf * dec.a[0] + d_qb * dec.a[1]
            dk_s[sl, :] = _tn(d_scores, q)
            dv_s[sl, :] = _tn(scores * dec.mask, do)
            gst_s[n] = _tn(_both(q, dec.a), do)
            return dmask_sum + d_att * scores, da_f + d_qf * q, da_b + d_qb * q

        dmask_sum, da_f, da_b = _chunk_loop(NCH, scores_back, (zeros, zeros, zeros))

        def state_back(t, carry):
            out = []
            for d, r in enumerate(carry):
                n = t if d else (NCH - 1 - t)
                rows = slice(d * RD, (d + 1) * RD)
                own = gst_s[n, rows, :]
                gst_s[n, rows, :] = r
                out.append(own + dec.g[d] * r)
            return tuple(out)

        d_states = _chunk_loop(NCH, state_back, (zeros, zeros))

        def updates_back(n, carry):
            db_f, db_b, dg_f, dg_b = carry
            sl = pl.ds(pl.multiple_of(n * CH, CH), CH)
            k, r, s = k_s[sl, :], gst_s[n], st_s[n]
            d_kw = _nt(v_of(sl), r)
            d_kf, d_kb = d_kw[:, 0:RD], d_kw[:, RD:2 * RD]
            dk_s[sl, :] += d_kf * dec.b[0] + d_kb * dec.b[1]
            dv_s[sl, :] += _nn(_both(k, dec.b), r)
            return (db_f + d_kf * k, db_b + d_kb * k, dg_f + r[0:RD, :] * s[0:RD, :],
                    dg_b + r[RD:2 * RD, :] * s[RD:2 * RD, :])

        db_dg = _chunk_loop(NCH, updates_back, (zeros, zeros, zeros, zeros))
        dkc = [None, None]
        dvc = [None, None]
        for d, ((rev, order_c, lg, rdv, ctx_states), ds) in enumerate(zip(dirs, d_states)):
            dlg = (_total(dmask_sum * dec.dmask[d]) + _total((da_f, da_b)[d] * dec.da[d])
                   + _total(db_dg[d] * dec.db[d]) + CH * dec.g[d] * _total(db_dg[2 + d]))
            for idx in (1, 0):
                n = order_c[idx]
                _, vjp = jax.vjp(functools.partial(_ret_state, reverse=rev), kcs[n], vcs[n], ctx_states[idx], lg)
                dk_c, dv_c, ds, dl = vjp(ds)
                dlg = dlg + dl
                dkc[n] = dk_c if dkc[n] is None else dkc[n] + dk_c
                dvc[n] = dv_c if dvc[n] is None else dvc[n] + dv_c
            drd_ref[int(rev):int(rev) + 1, :] += dlg * jax.nn.sigmoid(-rdv)
        dp_ref[:, 0:128] = _rope_t(dq_s[...] * (RD ** -0.5), cos_v, sin_v).astype(BF16)
        dp_ref[:, 128:256] = _rope_t(dk_s[...], cos_v, sin_v).astype(BF16)
        dp_ref[:, 256:384] = dv_s[...].astype(BF16)
        zero = jnp.zeros((CH, RD), BF16)
        for n in (0, 1):
            rows = slice(n * CH, (n + 1) * CH)
            dpc_ref[rows, 0:128] = zero
            dpc_ref[rows, 128:256] = dkc[n].astype(BF16)
            dpc_ref[rows, 256:384] = dvc[n].astype(BF16)
            dpc_ref[rows, 384:512] = zero

        @pl.when(grid_step == RH * nb - 1)
        def _():
            hosted.finish(h_in, h_out, h_sems)

    h_in_specs, h_out_specs = hosted.specs()
    return pl.pallas_call(
        body, name="retention_bwd", grid=(RH, nb),
        in_specs=[sp["pret"], sp["pretc"], sp["head"], sp["head"], sp["rd"], sp["gn"], sp["tab"], sp["tab"]]
        + h_in_specs,
        out_specs=[
            pl.BlockSpec((None, SEQ, 512), lambda h, b: (b, 0, h)),
            pl.BlockSpec((None, LC, 512), lambda h, b: (b, 0, h)),
            pl.BlockSpec((None, 2, 1), lambda h, b: (h, 0, 0)),
            pl.BlockSpec((None, 1, RD), lambda h, b: (h, 0, 0)),
        ] + h_out_specs,
        out_shape=[
            jax.ShapeDtypeStruct((nb, SEQ, IN_W), BF16),
            jax.ShapeDtypeStruct((nb, LC, IN_W), BF16),
            jax.ShapeDtypeStruct((RH, 2, 1), F32),
            jax.ShapeDtypeStruct((RH, 1, RD), F32),
        ] + hosted.out_shape,
        scratch_shapes=[pltpu.VMEM((SEQ, RD), F32)] * 6 + [pltpu.VMEM((NCH, 2 * RD, RD), F32)] * 2 + hosted.scratch,
        compiler_params=_cp(("arbitrary", "arbitrary")),
    )(pret, pretc, o_all, dmixin, rd, gn, cos, sin, *hosted.args)


def _rpb_flat(rpb):
    return jnp.pad(rpb, ((0, 0), (0, 1), (0, 33))).reshape(NPAIR, 2, 1, 1024)


def _rpb_flat_t(dflat):
    return dflat.reshape(8, 16, 64)[:, :15, :31]


def _barrel(x, left):
    row = lax.broadcasted_iota(jnp.int32, x.shape, 0)
    n = x.shape[1]
    for bit in range(6):
        s = 1 << bit
        x = jnp.where(((row >> bit) & 1) == 1, pltpu.roll(x, (n - s) if left else s, 1), x)
    return x


NA_TILE_ROWS, NA_BAND_ROWS = 4, 12
NA_Q, NA_K = NA_TILE_ROWS * GW, NA_BAND_ROWS * GW
NA_TILES = SEQ // NA_Q


def _band_start(r0):
    return min(max(r0 - 4, 0), 32 - NA_BAND_ROWS)


def _tile_layout(t):
    rows = range(t * NA_TILE_ROWS, (t + 1) * NA_TILE_ROWS)
    return tuple((r if r < 4 else (r - 24 if r > 28 else 4), min(max(r - 4, 0), 24) - _band_start(rows[0]))
                 for r in rows)


NA_CLASSES = sorted(set(_tile_layout(t) for t in range(NA_TILES)))


def _tile_rows(cls):
    return NA_CLASSES[cls]


def _na_tile(t):
    start = jnp.clip(NA_TILE_ROWS * t - 4, 0, 32 - NA_BAND_ROWS)
    cls = 0
    for tile in range(NA_TILES):
        cls = jnp.where(t == tile, NA_CLASSES.index(_tile_layout(tile)), cls)
    return pl.ds(pl.multiple_of(t * NA_Q, NA_Q), NA_Q), pl.ds(pl.multiple_of(start * GW, NA_Q), NA_K), cls


def _na_probs(qst, kb, kc, bias):
    s_loc = _nt(qst, kb) * 0.125 + bias
    s_ctx = _nt(qst, kc) * 0.125
    m = jnp.maximum(jnp.max(s_loc, axis=1, keepdims=True), jnp.max(s_ctx, axis=1, keepdims=True))
    e_loc, e_ctx = jnp.exp(s_loc - m), jnp.exp(s_ctx - m)
    den = jnp.sum(e_loc, axis=1, keepdims=True) + jnp.sum(e_ctx, axis=1, keepdims=True)
    return e_loc / den, e_ctx / den


def _stack_heads(t):
    lane = lax.broadcasted_iota(jnp.int32, t.shape, 1)
    zero = jnp.zeros_like(t)
    return jnp.concatenate([jnp.where(lane < 64, t, zero), jnp.where(lane >= 64, t, zero)], axis=0)


def _unstack_heads(t):
    n = t.shape[0] // 2
    lane = lax.broadcasted_iota(jnp.int32, (n, 128), 1)
    return jnp.where(lane < 64, t[:n], t[n:])


def na_bias_table(flat):
    def body(flat_ref, out_ref):
        qc = lax.broadcasted_iota(jnp.int32, (GW, 512), 0)
        kc = lax.broadcasted_iota(jnp.int32, (GW, 512), 1) & 63
        start = jnp.clip(qc - 8, 0, GW - 16)
        window = (kc >= start) & (kc < start + 16)
        fill = jnp.full((GW, NA_K - 512), NEG, F32)
        for hh in (0, 1):
            skew = _barrel(pltpu.roll(jnp.broadcast_to(flat_ref[hh], (GW, 1024)), 1024 - 15, 1), left=False)
            by_class = [jnp.where(window, (skew if rc == 7 else pltpu.roll(skew, (9 + rc) * 64, 1))[:, 0:512], NEG)
                        for rc in range(8)]
            for cls in range(len(NA_CLASSES)):
                for qr, (rc, off) in enumerate(_tile_rows(cls)):
                    w = jnp.concatenate([by_class[rc], fill], axis=1)
                    rows = slice(hh * NA_Q + qr * GW, hh * NA_Q + (qr + 1) * GW)
                    out_ref[cls, rows, :] = pltpu.roll(w, off * GW, 1) if off else w

    return pl.pallas_call(
        body, name="na_bias_table", grid=(NPAIR,),
        in_specs=[pl.BlockSpec((None, 2, 1, 1024), lambda p: (p, 0, 0, 0))],
        out_specs=pl.BlockSpec((None, len(NA_CLASSES), 2 * NA_Q, NA_K), lambda p: (p, 0, 0, 0)),
        out_shape=jax.ShapeDtypeStruct((NPAIR, len(NA_CLASSES), 2 * NA_Q, NA_K), F32),
    )(flat)


def na_fwd(pna, pnac, bias, mixin, hosted):
    nb = pna.shape[0]

    def body(*refs):
        (p_ref, pc_ref, bias_ref, _), h_in, (out_ref,), h_out, _, h_sems = hosted.split(refs, 4, 1)
        grid_step = pl.program_id(0) * nb + pl.program_id(1)

        @pl.when(grid_step == 0)
        def _():
            hosted.start(h_in, h_out, h_sems)

        kc, vc = pc_ref[:, 128:256], pc_ref[:, 256:384]

        def tile(t, carry):
            qsl, bsl, cls = _na_tile(t)
            kb, vb = p_ref[bsl, 128:256], p_ref[bsl, 256:384]
            p_loc, p_ctx = _na_probs(_stack_heads(p_ref[qsl, 0:128]), kb, kc, bias_ref[cls])
            out_ref[qsl, :] = _unstack_heads(_nn(p_loc, vb) + _nn(p_ctx, vc)).astype(BF16)
            return carry

        lax.fori_loop(0, NA_TILES, tile, 0, unroll=4)

        @pl.when(grid_step == NPAIR * nb - 1)
        def _():
            hosted.finish(h_in, h_out, h_sems)

    h_in_specs, h_out_specs = hosted.specs()
    return pl.pallas_call(
        body, name="na_fwd", grid=(NPAIR, nb),
        in_specs=[
            pl.BlockSpec((None, SEQ, 384), lambda p, b: (b, 0, p)),
            pl.BlockSpec((None, LC, 384), lambda p, b: (b, 0, p)),
            pl.BlockSpec((None, len(NA_CLASSES), 2 * NA_Q, NA_K), lambda p, b: (p, 0, 0, 0)),
            pl.BlockSpec(memory_space=pl.ANY),
        ] + h_in_specs,
        out_specs=[pl.BlockSpec((None, SEQ, 128), lambda p, b: (b, 0, 4 + p))] + h_out_specs,
        out_shape=[jax.ShapeDtypeStruct((nb, SEQ, D), BF16)] + hosted.out_shape,
        input_output_aliases={3: 0},
        scratch_shapes=hosted.scratch,
        compiler_params=_cp(("arbitrary", "arbitrary")),
    )(pna, pnac, bias, mixin, *hosted.args)


def na_bwd(pna, pnac, bias, dmixin, dproj, dprojc, hosted):
    nb = pna.shape[0]

    def body(*refs):
        own_in, h_in, own_out, h_out, own_scr, h_sems = hosted.split(refs, 6, 3)
        p_ref, pc_ref, bias_ref, dmix_ref = own_in[:4]
        dp_ref, dpc_ref, dpat_ref = own_out
        dbias_s, dk_s, dv_s, dkc_s, dvc_s, res_s, resc_s = own_scr
        b, part = pl.program_id(1), pl.program_id(2)
        grid_step = (pl.program_id(0) * nb + b) * 3 + part

        @pl.when(grid_step == 0)
        def _():
            hosted.start(h_in, h_out, h_sems)

        @pl.when(grid_step == NPAIR * nb * 3 - 1)
        def _():
            hosted.finish(h_in, h_out, h_sems)

        @pl.when(part == 0)
        def _():
            @pl.when(b == 0)
            def _():
                dbias_s[...] = jnp.zeros_like(dbias_s)

            dk_s[...] = jnp.zeros_like(dk_s)
            dv_s[...] = jnp.zeros_like(dv_s)
            dkc_s[...] = jnp.zeros_like(dkc_s)
            dvc_s[...] = jnp.zeros_like(dvc_s)
            kc, vc = pc_ref[:, 128:256], pc_ref[:, 256:384]

            def tile(t, carry):
                qsl, bsl, cls = _na_tile(t)
                kb, vb = p_ref[bsl, 128:256], p_ref[bsl, 256:384]
                qst, dost = _stack_heads(p_ref[qsl, 0:128]), _stack_heads(dmix_ref[qsl, :])
                p_loc, p_ctx = _na_probs(qst, kb, kc, bias_ref[cls])
                dp_loc, dp_ctx = _nt(dost, vb), _nt(dost, vc)
                delta = (jnp.sum(p_loc * dp_loc, axis=1, keepdims=True)
                         + jnp.sum(p_ctx * dp_ctx, axis=1, keepdims=True))
                ds_loc, ds_ctx = p_loc * (dp_loc - delta), p_ctx * (dp_ctx - delta)
                dbias_s[cls] += ds_loc
                res_s[0, qsl, :] = _unstack_heads((_nn(ds_loc, kb) + _nn(ds_ctx, kc)) * 0.125).astype(BF16)
                dk_s[bsl, :] += _tn(ds_loc, qst) * 0.125
                dv_s[bsl, :] += _tn(p_loc, dost)
                dkc_s[...] += _tn(ds_ctx, qst) * 0.125
                dvc_s[...] += _tn(p_ctx, dost)
                return carry

            lax.fori_loop(0, NA_TILES, tile, 0, unroll=2)
            res_s[1] = dk_s[...].astype(BF16)
            res_s[2] = dv_s[...].astype(BF16)
            resc_s[0] = jnp.zeros((LC, 128), BF16)
            resc_s[1] = dkc_s[...].astype(BF16)
            resc_s[2] = dvc_s[...].astype(BF16)

            @pl.when(b == nb - 1)
            def _():
                for hh in (0, 1):
                    by_class = [None] * 8
                    for cls in range(len(NA_CLASSES)):
                        for qr, (rc, off) in enumerate(_tile_rows(cls)):
                            w = dbias_s[cls, hh * NA_Q + qr * GW:hh * NA_Q + (qr + 1) * GW, :]
                            w = (pltpu.roll(w, NA_K - off * GW, 1) if off else w)[:, 0:512]
                            by_class[rc] = w if by_class[rc] is None else by_class[rc] + w
                    skew = jnp.zeros((GW, 1024), F32)
                    for rc in range(8):
                        w = jnp.concatenate([by_class[rc], jnp.zeros((GW, 512), F32)], axis=1)
                        skew = skew + (w if rc == 7 else pltpu.roll(w, (7 - rc) * 64, 1))
                    dpat_ref[hh] = jnp.sum(pltpu.roll(_barrel(skew, left=True), 15, 1), axis=0, keepdims=True)

        dp_ref[...] = res_s[part]
        dpc_ref[...] = resc_s[part]

    h_in_specs, h_out_specs = hosted.specs()
    return pl.pallas_call(
        body, name="na_bwd", grid=(NPAIR, nb, 3),
        in_specs=[
            pl.BlockSpec((None, SEQ, 384), lambda p, b, s: (b, 0, p)),
            pl.BlockSpec((None, LC, 384), lambda p, b, s: (b, 0, p)),
            pl.BlockSpec((None, len(NA_CLASSES), 2 * NA_Q, NA_K), lambda p, b, s: (p, 0, 0, 0)),
            pl.BlockSpec((None, SEQ, 128), lambda p, b, s: (b, 0, 4 + p)),
            pl.BlockSpec(memory_space=pl.ANY),
            pl.BlockSpec(memory_space=pl.ANY),
        ] + h_in_specs,
        out_specs=[
            pl.BlockSpec((None, SEQ, 128), lambda p, b, s: (b, 0, 16 + 3 * p + s)),
            pl.BlockSpec((None, LC, 128), lambda p, b, s: (b, 0, 16 + 3 * p + s)),
            pl.BlockSpec((None, 2, 1, 1024), lambda p, b, s: (p, 0, 0, 0)),
        ] + h_out_specs,
        out_shape=[
            jax.ShapeDtypeStruct((nb, SEQ, IN_W), BF16),
            jax.ShapeDtypeStruct((nb, LC, IN_W), BF16),
            jax.ShapeDtypeStruct((NPAIR, 2, 1, 1024), F32),
        ] + hosted.out_shape,
        input_output_aliases={4: 0, 5: 1},
        scratch_shapes=[
            pltpu.VMEM((len(NA_CLASSES), 2 * NA_Q, NA_K), F32),
            pltpu.VMEM((SEQ, 128), F32), pltpu.VMEM((SEQ, 128), F32),
            pltpu.VMEM((LC, 128), F32), pltpu.VMEM((LC, 128), F32),
            pltpu.VMEM((3, SEQ, 128), BF16), pltpu.VMEM((3, LC, 128), BF16),
        ] + hosted.scratch,
        compiler_params=_cp(("arbitrary", "arbitrary", "arbitrary")),
    )(pna, pnac, bias, dmixin, dproj, dprojc, *hosted.args)


def tail_fwd_bwd(x, mixin, tgt, mod3, g_post_mix, g_pre_mlp, g_post_mlp, wout, w1, w2):
    nb = x.shape[0]

    def body(x_ref, mi_ref, tgt_ref, mod_ref, gpm_ref, gpl_ref, gpo_ref, wo_ref, w1_ref, w2_ref,
             dx_ref, dmix_ref, h2_ref, du_ref, a_ref, dm_ref, dmi_ref, dmod_ref, dg_ref, loss_ref):
        b, t = pl.program_id(0), pl.program_id(1)
        gt1, sh2, sc2, gt2 = mod_ref[2:3, :], mod_ref[3:4, :], mod_ref[4:5, :], mod_ref[5:6, :]
        mix = jnp.dot(mi_ref[...], wo_ref[...], preferred_element_type=F32)
        (x1, h2), vjp_a = jax.vjp(_post_mix, x_ref[...], mix, gt1, sc2, sh2, gpm_ref[...], gpl_ref[...])
        h2b = h2.astype(BF16)
        h2_ref[...] = h2b
        m = jnp.zeros((TN, D), F32)
        relus = []
        for j in range(4):
            cols = slice(j * D, (j + 1) * D)
            r = jnp.maximum(jnp.dot(h2b, w1_ref[j], preferred_element_type=F32), 0.0)
            ab = (r * r).astype(BF16)
            a_ref[:, cols] = ab
            m = m + jnp.dot(ab, w2_ref[cols, :], preferred_element_type=F32)
            relus.append(r)
        loss, vjp_b = jax.vjp(_head_loss, x1, m, gt2, gpo_ref[...], tgt_ref[...])
        dx1, dm, dgt2, dgpo, _ = vjp_b(jnp.ones((1, 1), F32))
        dmb = dm.astype(BF16)
        dm_ref[...] = dmb
        dh2 = jnp.zeros((TN, D), F32)
        for j in range(4):
            cols = slice(j * D, (j + 1) * D)
            da = lax.dot_general(dmb, w2_ref[cols, :], (((1,), (1,)), ((), ())), preferred_element_type=F32)
            dub = (da * (2.0 * relus[j])).astype(BF16)
            du_ref[:, cols] = dub
            dh2 = dh2 + lax.dot_general(dub, w1_ref[j], (((1,), (1,)), ((), ())), preferred_element_type=F32)
        dx, dmix, dgt1, dsc2, dsh2, dgpm, dgpl = vjp_a((dx1, dh2))
        dx_ref[...] = dx
        dmixb = dmix.astype(BF16)
        dmix_ref[...] = dmixb
        dmi_ref[...] = lax.dot_general(dmixb, wo_ref[...], (((1,), (1,)), ((), ())),
                                       preferred_element_type=F32).astype(BF16)

        @pl.when(t == 0)
        def _():
            dmod_ref[...] = jnp.zeros_like(dmod_ref)

        @pl.when((t == 0) & (b == 0))
        def _():
            dg_ref[...] = jnp.zeros_like(dg_ref)
            loss_ref[...] = jnp.zeros_like(loss_ref)

        dmod_ref[2:3, :] += dgt1
        dmod_ref[3:4, :] += dsh2
        dmod_ref[4:5, :] += dsc2
        dmod_ref[5:6, :] += dgt2
        dg_ref[0:1, :] += dgpm
        dg_ref[1:2, :] += dgpl
        dg_ref[2:3, :] += dgpo
        loss_ref[...] += jnp.broadcast_to(loss, loss_ref.shape)

    tok = lambda b, t: (b, t, 0)
    const = lambda b, t: (0, 0)
    vec = pl.BlockSpec((1, D), const)
    return pl.pallas_call(
        body, name="tail_fwd_bwd", grid=(nb, SEQ // TN),
        in_specs=[
            pl.BlockSpec((None, TN, D), tok), pl.BlockSpec((None, TN, D), tok), pl.BlockSpec((None, TN, D), tok),
            pl.BlockSpec((None, 6, D), lambda b, t: (b, 0, 0)), vec, vec, vec,
            pl.BlockSpec((D, D), const, pipeline_mode=pl.Buffered(1)),
            pl.BlockSpec((4, D, D), lambda b, t: (0, 0, 0), pipeline_mode=pl.Buffered(1)),
            pl.BlockSpec((DFF, D), const, pipeline_mode=pl.Buffered(1)),
        ],
        out_specs=[
            pl.BlockSpec((None, TN, D), tok), pl.BlockSpec((None, TN, D), tok), pl.BlockSpec((None, TN, D), tok),
            pl.BlockSpec((None, TN, DFF), tok), pl.BlockSpec((None, TN, DFF), tok), pl.BlockSpec((None, TN, D), tok),
            pl.BlockSpec((None, TN, D), tok),
            pl.BlockSpec((None, 6, D), lambda b, t: (b, 0, 0)),
            pl.BlockSpec((8, D), const), pl.BlockSpec((8, 128), const),
        ],
        out_shape=[
            jax.ShapeDtypeStruct((nb, SEQ, D), F32), jax.ShapeDtypeStruct((nb, SEQ, D), BF16),
            jax.ShapeDtypeStruct((nb, SEQ, D), BF16), jax.ShapeDtypeStruct((nb, SEQ, DFF), BF16),
            jax.ShapeDtypeStruct((nb, SEQ, DFF), BF16), jax.ShapeDtypeStruct((nb, SEQ, D), BF16),
            jax.ShapeDtypeStruct((nb, SEQ, D), BF16),
            jax.ShapeDtypeStruct((nb, 6, D), F32), jax.ShapeDtypeStruct((8, D), F32),
            jax.ShapeDtypeStruct((8, 128), F32),
        ],
        compiler_params=_cp(("arbitrary", "arbitrary")),
    )(x, mixin, tgt, mod3, g_post_mix, g_pre_mlp, g_post_mlp, wout, w1, w2)


def weight_grad(pairs, name, out_dtype=F32, col_blocks=False, tm=1024, tn=1024, tk=2048):
    m, n = pairs[0][0].shape[1], pairs[0][1].shape[1]
    tn = min(tn, n)
    tks = [min(tk, xa.shape[0]) for xa, _ in pairs]
    steps = [xa.shape[0] // t for (xa, _), t in zip(pairs, tks)]
    total = sum(steps)
    offs = [sum(steps[:i]) for i in range(len(pairs))]

    def body(*refs):
        out_ref, acc = refs[2 * len(pairs)], refs[-1]
        k = pl.program_id(2)

        @pl.when(k == 0)
        def _():
            acc[...] = jnp.zeros_like(acc)

        for i in range(len(pairs)):
            @pl.when((k >= offs[i]) & (k < offs[i] + steps[i]))
            def _(i=i):
                acc[...] += lax.dot_general(refs[2 * i][...], refs[2 * i + 1][...], (((0,), (0,)), ((), ())),
                                            preferred_element_type=F32)

        if out_dtype != F32:
            @pl.when(k == total - 1)
            def _():
                out_ref[...] = acc[...].astype(out_dtype)

    in_specs, args = [], []
    for i, (xa, ya) in enumerate(pairs):
        clamp = lambda k, i=i: jnp.clip(k - offs[i], 0, steps[i] - 1)
        in_specs.append(pl.BlockSpec((tks[i], tm), lambda a, c, k, clamp=clamp: (clamp(k), a)))
        in_specs.append(pl.BlockSpec((tks[i], tn), lambda a, c, k, clamp=clamp: (clamp(k), c)))
        args += [xa, ya]
    if col_blocks:
        out_spec = pl.BlockSpec((None, tm, tn), lambda a, c, k: (c, a, 0))
        out_shape = jax.ShapeDtypeStruct((n // tn, m, tn), out_dtype)
    else:
        out_spec = pl.BlockSpec((tm, tn), lambda a, c, k: (a, c))
        out_shape = jax.ShapeDtypeStruct((m, n), out_dtype)
    return pl.pallas_call(
        body, name=name, grid=(m // tm, n // tn, total), in_specs=in_specs, out_specs=out_spec, out_shape=out_shape,
        scratch_shapes=[] if out_dtype == F32 else [pltpu.VMEM((tm, tn), F32)],
        compiler_params=_cp(("arbitrary", "arbitrary", "arbitrary")),
    )(*args)


def _perm_block(t):
    return 4 * (t % 4) + t // 4 if t < 16 else 16 + 3 * ((t - 16) % 4) + (t - 16) // 4


def unpack_w_in(blocks):
    def body(i_ref, o_ref):
        for t in range(28):
            p = _perm_block(t)
            o_ref[:, p * 128:(p + 1) * 128] = i_ref[t // 7, :, (t % 7) * 128:(t % 7 + 1) * 128]

    return pl.pallas_call(
        body, name="unpack_w_in", grid=(2,),
        in_specs=[pl.BlockSpec((4, D // 2, 896), lambda i: (0, i, 0))],
        out_specs=pl.BlockSpec((D // 2, IN_W), lambda i: (i, 0)),
        out_shape=jax.ShapeDtypeStruct((D, IN_W), BF16),
    )(blocks)


def pack_w_in(dw):
    def body(i_ref, o_ref):
        for t in range(28):
            p = _perm_block(t)
            o_ref[t // 7, :, (t % 7) * 128:(t % 7 + 1) * 128] = i_ref[:, p * 128:(p + 1) * 128].astype(BF16)

    return pl.pallas_call(
        body, name="pack_w_in", grid=(4,),
        in_specs=[pl.BlockSpec((D // 4, IN_W), lambda i: (i, 0))],
        out_specs=pl.BlockSpec((4, D // 4, 896), lambda i: (0, i, 0)),
        out_shape=jax.ShapeDtypeStruct((4, D, 896), BF16),
    )(dw)


def _place():
    return lax.axis_index("x"), lax.axis_index("y"), lax.axis_index("c")


class Hosted:
    def __init__(self, args, out_shape, scratch, start, finish):
        self.args, self.out_shape, self.scratch, self.start, self.finish = args, out_shape, scratch, start, finish

    def specs(self):
        hbm = pl.BlockSpec(memory_space=pl.ANY)
        return [hbm] * len(self.args), [hbm] * len(self.out_shape)

    def split(self, refs, n_in, n_out):
        a, b = len(self.args), len(self.out_shape)
        cuts = [n_in, n_in + a, n_in + a + n_out, n_in + a + n_out + b, len(refs) - len(self.scratch)]
        parts = [refs[i:j] for i, j in zip([0] + cuts, cuts + [len(refs)])]
        return parts[0], parts[1], parts[2], parts[3], parts[4], parts[5]


def no_exchange():
    return Hosted([], [], [], lambda *a: None, lambda *a: None)


def run_hosted(hosted, name):
    def body(*refs):
        _, ins, _, outs, _, sems = hosted.split(refs, 0, 0)
        hosted.start(ins, outs, sems)
        hosted.finish(ins, outs, sems)

    in_specs, out_specs = hosted.specs()
    return pl.pallas_call(body, name=name, in_specs=in_specs, out_specs=out_specs, out_shape=hosted.out_shape,
                          scratch_shapes=hosted.scratch)(*hosted.args)


def gather8(blocks):
    na = len(blocks)

    def copies(ins, outs, sems):
        send_sems, recv_sems, local_sem = sems
        x, y, c = _place()
        me, sibling = (x, y, c), (x, y, 1 - c)
        chips = [(1 - x, y), (x, 1 - y), (1 - x, 1 - y)]

        def slot(o_ref, px, py, pc):
            return o_ref.at[4 * px + 2 * py + pc]

        def copy(a, k, block, to, src=None):
            return pltpu.make_async_remote_copy(
                src_ref=slot(outs[a], *block) if src is None else src, dst_ref=slot(outs[a], *block),
                send_sem=send_sems.at[a, k], recv_sem=recv_sems.at[a, k], device_id=to, device_id_type=MESH)

        mine = [pltpu.make_async_copy(ins[a], slot(outs[a], *me), local_sem.at[a]) for a in range(na)]
        first = []
        for a in range(na):
            first.append(copy(a, 0, me, sibling, src=ins[a]))
            first += [copy(a, 1 + j, me, (*chip, c), src=ins[a]) for j, chip in enumerate(chips)]
        return copy, mine, first, me, sibling, chips, c

    def start(ins, outs, sems):
        _, mine, first, *_ = copies(ins, outs, sems)
        for cp in mine + first:
            cp.start()

    def finish(ins, outs, sems):
        copy, mine, first, me, sibling, chips, c = copies(ins, outs, sems)
        passed = []
        for j, chip in enumerate(chips):
            for a in range(na):
                copy(a, 1 + j, (*chip, c), me).wait_recv()
                cp = copy(a, 4 + j, (*chip, c), sibling)
                cp.start()
                passed.append(cp)
        for a in range(na):
            copy(a, 0, sibling, me).wait_recv()
            for j, chip in enumerate(chips):
                copy(a, 4 + j, (*chip, 1 - c), me).wait_recv()
        for cp in first + passed:
            cp.wait_send()
        for cp in mine:
            cp.wait()

    return Hosted(list(blocks), [jax.ShapeDtypeStruct((8,) + b.shape, b.dtype) for b in blocks],
                  [pltpu.SemaphoreType.DMA((na, 7)), pltpu.SemaphoreType.DMA((na, 7)), pltpu.SemaphoreType.DMA((na,))],
                  start, finish)


def chips3(arrays):
    na = len(arrays)

    def copies(ins, outs, sems):
        send_sems, recv_sems = sems
        x, y, c = _place()
        return [pltpu.make_async_remote_copy(
            src_ref=ins[a].at[2 * px + py], dst_ref=outs[a].at[k], send_sem=send_sems.at[a, k],
            recv_sem=recv_sems.at[a, k], device_id=(px, py, c), device_id_type=MESH)
            for a in range(na) for k, (px, py) in enumerate([(1 - x, y), (x, 1 - y), (1 - x, 1 - y)])]

    def start(ins, outs, sems):
        for cp in copies(ins, outs, sems):
            cp.start()

    def finish(ins, outs, sems):
        for cp in copies(ins, outs, sems):
            cp.wait()

    return Hosted(list(arrays), [jax.ShapeDtypeStruct((3,) + a.shape[1:], a.dtype) for a in arrays],
                  [pltpu.SemaphoreType.DMA((na, 3)), pltpu.SemaphoreType.DMA((na, 3))], start, finish)


def siblings(arrays):
    na = len(arrays)

    def copies(ins, outs, sems):
        send_sems, recv_sems = sems
        x, y, c = _place()
        return [pltpu.make_async_remote_copy(
            src_ref=ins[a], dst_ref=outs[a], send_sem=send_sems.at[a], recv_sem=recv_sems.at[a],
            device_id=(x, y, 1 - c), device_id_type=MESH) for a in range(na)]

    def start(ins, outs, sems):
        for cp in copies(ins, outs, sems):
            cp.start()

    def finish(ins, outs, sems):
        for cp in copies(ins, outs, sems):
            cp.wait()

    return Hosted(list(arrays), [jax.ShapeDtypeStruct(a.shape, a.dtype) for a in arrays],
                  [pltpu.SemaphoreType.DMA((na,)), pltpu.SemaphoreType.DMA((na,))], start, finish)


def both(first, second):
    na, no, ns = len(first.args), len(first.out_shape), len(first.scratch)

    def start(ins, outs, sems):
        first.start(ins[:na], outs[:no], sems[:ns])
        second.start(ins[na:], outs[no:], sems[ns:])

    def finish(ins, outs, sems):
        first.finish(ins[:na], outs[:no], sems[:ns])
        second.finish(ins[na:], outs[no:], sems[ns:])

    return Hosted(first.args + second.args, first.out_shape + second.out_shape, first.scratch + second.scratch,
                  start, finish)


def siblings4(arrays):
    na = len(arrays)

    def copies(ins, outs, sems):
        send_sems, recv_sems = sems
        x, y, c = _place()
        return [pltpu.make_async_remote_copy(
            src_ref=ins[a].at[2 * j + 1 - c], dst_ref=outs[a].at[j],
            send_sem=send_sems.at[a, j], recv_sem=recv_sems.at[a, j],
            device_id=(x, y, 1 - c), device_id_type=MESH) for a in range(na) for j in range(4)]

    def start(ins, outs, sems):
        for cp in copies(ins, outs, sems):
            cp.start()

    def finish(ins, outs, sems):
        for cp in copies(ins, outs, sems):
            cp.wait()

    return Hosted(list(arrays), [jax.ShapeDtypeStruct((4,) + a.shape[1:], a.dtype) for a in arrays],
                  [pltpu.SemaphoreType.DMA((na, 4)), pltpu.SemaphoreType.DMA((na, 4))], start, finish)


def sibling_blocks(arrays, name):
    return run_hosted(siblings4(arrays), name)


def _row_tile(r):
    for cand in (512, 256, 128, 64, 32, 16, 8):
        if r % cand == 0:
            return cand
    return r


def chip_partial(place, g8s, landed4s, name):
    n = len(g8s)

    def body(place_ref, *refs):
        del place_ref
        for g_ref, l_ref, o_ref in zip(refs[:n], refs[n:2 * n], refs[2 * n:]):
            o_ref[...] = (g_ref[...].astype(F32) + l_ref[...].astype(F32)).astype(BF16)

    own = [pl.BlockSpec((None,) + g.shape[1:], lambda j, s: (2 * j + s[0], 0, 0)) for g in g8s]
    plain = [pl.BlockSpec((None,) + g.shape[1:], lambda j, s: (j, 0, 0)) for g in g8s]
    return pl.pallas_call(
        body, name=name,
        grid_spec=pltpu.PrefetchScalarGridSpec(num_scalar_prefetch=1, grid=(4,), in_specs=own + plain, out_specs=plain),
        out_shape=[jax.ShapeDtypeStruct((4,) + g.shape[1:], BF16) for g in g8s],
    )(place, *g8s, *landed4s)


def shard_sum(place, partial4s, landed3s, name):
    n = len(partial4s)

    def body(place_ref, *refs):
        del place_ref
        for p_ref, l_ref, o_ref in zip(refs[:n], refs[n:2 * n], refs[2 * n:]):
            acc = p_ref[...].astype(F32)
            for k in range(3):
                acc = acc + l_ref[k].astype(F32)
            o_ref[...] = acc

    def halves(p, lead):
        r, ccols = p.shape[1:]
        return (lead, r // 2, ccols)

    return pl.pallas_call(
        body, name=name,
        grid_spec=pltpu.PrefetchScalarGridSpec(
            num_scalar_prefetch=1, grid=(2,),
            in_specs=[pl.BlockSpec(halves(p, None), lambda i, s: (s[1], i, 0)) for p in partial4s]
            + [pl.BlockSpec(halves(p, 3), lambda i, s: (0, i, 0)) for p in partial4s],
            out_specs=[pl.BlockSpec(halves(p, None)[1:], lambda i, s: (i, 0)) for p in partial4s]),
        out_shape=[jax.ShapeDtypeStruct(p.shape[1:], F32) for p in partial4s],
    )(place, *partial4s, *landed3s)


def _adamw_math(w, g, m, v):
    m2 = B1 * m + (1.0 - B1) * g
    v2 = B2 * v + (1.0 - B2) * (g * g)
    m_hat = m2 / (1.0 - B1 ** STEP)
    v_hat = v2 / (1.0 - B2 ** STEP)
    return -LR * (m_hat / (jnp.sqrt(v_hat) + AEPS) + WD * w), m2, v2


def adamw_halves(place, w, mine, theirs, m, v, name):
    r, ccols = w.shape
    hr = r // 2
    tr = _row_tile(hr)
    nt = hr // tr

    def body(place_ref, w_ref, a_ref, b_ref, m_ref, v_ref, g_out, d_out, m_out, v_out):
        g = jnp.where(pl.program_id(0) == place_ref[0], a_ref[...], b_ref[...])
        d, m2, v2 = _adamw_math(w_ref[...], g, m_ref[...], v_ref[...])
        g_out[...] = g
        d_out[...] = d
        m_out[...] = m2
        v_out[...] = v2

    full = pl.BlockSpec((tr, ccols), lambda h, i, s: (h * nt + i, 0))
    part = pl.BlockSpec((tr, ccols), lambda h, i, s: (i, 0))
    return pl.pallas_call(
        body, name=name,
        grid_spec=pltpu.PrefetchScalarGridSpec(
            num_scalar_prefetch=1, grid=(2, nt), in_specs=[full, part, part, full, full], out_specs=[full] * 4),
        out_shape=[jax.ShapeDtypeStruct((r, ccols), F32)] * 4,
    )(place, w, mine, theirs, m, v)


def adamw_group(place, halved, plain, hosted, name):
    rows = halved[0][0].shape[0]
    tr = 64
    nt = rows // 2 // tr
    nh, npl = len(halved), len(plain)

    def body(place_ref, *refs):
        own_in, h_in, own_out, h_out, _, h_sems = hosted.split(refs, 5 * nh + 4 * npl, 4 * nh + 3 * npl)
        half = pl.program_id(0)
        grid_step = half * nt + pl.program_id(1)

        @pl.when(grid_step == 0)
        def _():
            hosted.start(h_in, h_out, h_sems)

        for i in range(nh):
            w_ref, a_ref, b_ref, m_ref, v_ref = own_in[5 * i:5 * i + 5]
            g = jnp.where(half == place_ref[0], a_ref[...], b_ref[...])
            res = (g,) + _adamw_math(w_ref[...], g, m_ref[...], v_ref[...])
            for o_ref, r in zip(own_out[4 * i:4 * i + 4], res):
                o_ref[...] = r
        for i in range(npl):
            w_ref, g_ref, m_ref, v_ref = own_in[5 * nh + 4 * i:5 * nh + 4 * i + 4]
            res = _adamw_math(w_ref[...], g_ref[...], m_ref[...], v_ref[...])
            for o_ref, r in zip(own_out[4 * nh + 3 * i:4 * nh + 3 * i + 3], res):
                o_ref[...] = r

        @pl.when(grid_step == 2 * nt - 1)
        def _():
            hosted.finish(h_in, h_out, h_sems)

    def full(cols):
        return pl.BlockSpec((tr, cols), lambda h, i, s: (h * nt + i, 0))

    def part(cols):
        return pl.BlockSpec((tr, cols), lambda h, i, s: (i, 0))

    in_specs, out_specs, out_shape, args = [], [], [], []
    for w, a, b, m, v in halved:
        cols = w.shape[1]
        in_specs += [full(cols), part(cols), part(cols), full(cols), full(cols)]
        out_specs += [full(cols)] * 4
        out_shape += [jax.ShapeDtypeStruct(w.shape, F32)] * 4
        args += [w, a, b, m, v]
    for w, g, m, v in plain:
        cols = w.shape[1]
        in_specs += [full(cols)] * 4
        out_specs += [full(cols)] * 3
        out_shape += [jax.ShapeDtypeStruct(w.shape, F32)] * 3
        args += [w, g, m, v]
    h_in_specs, h_out_specs = hosted.specs()
    return pl.pallas_call(
        body, name=name,
        grid_spec=pltpu.PrefetchScalarGridSpec(
            num_scalar_prefetch=1, grid=(2, nt), in_specs=in_specs + h_in_specs, out_specs=out_specs + h_out_specs,
            scratch_shapes=hosted.scratch),
        out_shape=out_shape + hosted.out_shape,
        compiler_params=_cp(("arbitrary", "arbitrary")),
    )(place, *args, *hosted.args)


def _silu(x):
    return x * jax.nn.sigmoid(x)


def prologue(c_rows, c_ctx_row, w_ada, b_shard, half_w_in):
    shape = jax.ShapeDtypeStruct
    g_w = gather8([half_w_in])
    g_c = gather8([shape((8, D), F32)])
    g_m = gather8([shape((32, 1536), F32)])

    def body(c_ref, cc_ref, w_ref, b_ref, hw_ref, cin_ref, mg_ref, gw_ref, cg_s, ms_s, *sems):
        sw, sc, sm = sems[0:3], sems[3:6], sems[6:9]
        g_w.start([hw_ref], [gw_ref], sw)
        g_c.start([c_ref], [cg_s], sc)
        g_c.finish([c_ref], [cg_s], sc)
        cin_ref[...] = jnp.zeros_like(cin_ref)
        for dev in range(8):
            cin_ref[2 * dev:2 * dev + 2, :] = cg_s[dev, 0:2, :]
        cin_ref[16:17, :] = cc_ref[...]
        ms_s[...] = _nn(_silu(cin_ref[...]), w_ref[...]) + b_ref[...]
        g_m.start([ms_s], [mg_ref], sm)
        g_m.finish([ms_s], [mg_ref], sm)
        g_w.finish([hw_ref], [gw_ref], sw)

    vmem = pl.BlockSpec(memory_space=pltpu.VMEM)
    hbm = pl.BlockSpec(memory_space=pl.ANY)
    return pl.pallas_call(
        body, name="prologue", in_specs=[vmem, vmem, vmem, vmem, hbm], out_specs=[vmem, vmem, hbm],
        out_shape=[shape((32, D), F32), shape((8, 32, 1536), F32)] + g_w.out_shape,
        scratch_shapes=[pltpu.VMEM((8, 8, D), F32), pltpu.VMEM((32, 1536), F32)] + g_w.scratch + g_c.scratch
        + g_m.scratch,
        compiler_params=_cp(),
    )(c_rows, c_ctx_row, w_ada, b_shard, half_w_in)


def ada_grads(cin, gb, gc, w_ada):
    def body(c_ref, gb_ref, gc_ref, w_ref, gw_ref, pc_ref):
        ctx_tot = jnp.sum(gc_ref[...], axis=0, keepdims=True)
        rows = lax.broadcasted_iota(jnp.int32, (16, 512), 0)
        dm = jnp.concatenate([gb_ref[...], jnp.where(rows == 0, ctx_tot, 0.0)], axis=0)
        gw_ref[...] = _tn(_silu(c_ref[...]), dm)
        rows8 = lax.broadcasted_iota(jnp.int32, (8, 512), 0)
        part = _nt(jnp.where(rows8 == 0, ctx_tot, 0.0), w_ref[...])

        @pl.when(pl.program_id(0) == 0)
        def _():
            pc_ref[...] = jnp.zeros_like(pc_ref)

        pc_ref[...] += part

    return pl.pallas_call(
        body, name="ada_grads", grid=(3,),
        in_specs=[pl.BlockSpec((32, D), lambda j: (0, 0)), pl.BlockSpec((16, 512), lambda j: (0, j)),
                  pl.BlockSpec((8, 512), lambda j: (0, j)), pl.BlockSpec((D, 512), lambda j: (0, j))],
        out_specs=[pl.BlockSpec((D, 512), lambda j: (0, j)), pl.BlockSpec((8, D), lambda j: (0, 0))],
        out_shape=[jax.ShapeDtypeStruct((D, 1536), F32), jax.ShapeDtypeStruct((8, D), F32)],
    )(cin, gb, gc, w_ada)


SMALL_SUM_ROWS = 15


def small_update(gsm, gbf, gcf, pcg, params):
    n = len(params)

    def body(*refs):
        gsm_ref, gbf_ref, gcf_ref, pcg_ref = refs[:4]
        wmv, outs, loss_out = refs[4:4 + 3 * n], refs[4 + 3 * n:4 + 7 * n], refs[-1]
        acc = gsm_ref[0]
        for dev in range(1, 8):
            acc = acc + gsm_ref[dev]
        c_ctx = wmv[0][...]
        sg = jax.nn.sigmoid(c_ctx)
        dsilu = pcg_ref[0:1, :] + pcg_ref[2:3, :] + pcg_ref[4:5, :] + pcg_ref[6:7, :]
        lane = lax.broadcasted_iota(jnp.int32, (1, D), 1)
        last = acc[14:15, :]
        grads = [
            dsilu * (sg * (1.0 + c_ctx * (1.0 - sg))),
            jnp.sum(gbf_ref[...], axis=0, keepdims=True) + jnp.sum(gcf_ref[...], axis=0, keepdims=True),
            acc[0:1, :] + acc[1:2, :], acc[2:3, :], acc[3:4, :], acc[4:5, :],
            acc[5:6, 0:512], acc[6:14, :], jnp.where(lane < 8, last, 0.0),
        ]
        loss_out[...] = jnp.broadcast_to(jnp.sum(jnp.where(lane == 8, last, 0.0), axis=1, keepdims=True), (8, 128))
        for i, g in enumerate(grads):
            d, m2, v2 = _adamw_math(wmv[3 * i][...], g, wmv[3 * i + 1][...], wmv[3 * i + 2][...])
            outs[4 * i][...] = g
            outs[4 * i + 1][...] = d
            outs[4 * i + 2][...] = m2
            outs[4 * i + 3][...] = v2

    flat = [a for wmv in params for a in wmv]
    out_shape = [jax.ShapeDtypeStruct(w.shape, F32) for w, _, _ in params for _ in range(4)]
    return pl.pallas_call(
        body, name="small_update", out_shape=out_shape + [jax.ShapeDtypeStruct((8, 128), F32)],
    )(gsm, gbf, gcf, pcg, *flat)


def _pad_row(v, rows):
    flat = v.reshape(-1)
    return jnp.pad(flat, (0, rows * D - flat.shape[0])).reshape(rows, D)


def local_step(x, ctx, tgt, mod3, g_pre_mix, g_post_mix, g_pre_mlp, g_post_mlp, ret_decay, ret_gn, na_rpb,
               wperm, late_weights, early_grads):
    nb = x.shape[0]
    tokens = nb * SEQ
    cos, sin = _rope_tables()
    rd = ret_decay.T.reshape(RH, 2, 1)
    gn = ret_gn.reshape(RH, 1, RD)
    bias = na_bias_table(_rpb_flat(na_rpb))
    h, pret, pna = premix_proj(x, mod3, g_pre_mix, wperm, False, "premix_proj")
    hc, pretc, pnac = premix_proj(ctx, mod3, g_pre_mix, wperm, True, "premix_proj_ctx")
    o_all, mixin, gw_out = retention_fwd(pret, pretc, rd, gn, cos, sin, late_weights(0))
    mixin, gw1, gw2 = na_fwd(pna, pnac, bias, mixin, late_weights(1))
    dx_tail, dmix, h2, du, act, dm, dmixin, dmod_t, dg_t, loss_t = tail_fwd_bwd(
        x, mixin, tgt, mod3, g_post_mix, g_pre_mlp, g_post_mlp, gw_out.reshape(D, D), gw1.reshape(4, D, D),
        gw2.reshape(DFF, D))
    dw_out = weight_grad([(mixin.reshape(tokens, D), dmix.reshape(tokens, D))], "grad_w_out", BF16)
    dw1 = weight_grad([(h2.reshape(tokens, D), du.reshape(tokens, DFF))], "grad_w_mlp1", BF16, col_blocks=True)
    dw2 = weight_grad([(act.reshape(tokens, DFF), dm.reshape(tokens, D))], "grad_w_mlp2", BF16)
    dproj, dprojc, drd, dgn, *landed = retention_bwd(pret, pretc, o_all, dmixin, rd, gn, cos, sin,
                                                     early_grads[0](dw_out, dw1, dw2))
    dproj, dprojc, dpat, *early = na_bwd(pna, pnac, bias, dmixin, dproj, dprojc, early_grads[1](landed))
    dw_in = weight_grad([(h.reshape(tokens, D), dproj.reshape(tokens, IN_W)),
                         (hc.reshape(nb * LC, D), dprojc.reshape(nb * LC, IN_W))], "grad_w_in", tn=IN_W // 2, tk=1024)
    grad_x, dmod_a, dg_a, *late = premix_bwd(x, mod3, g_pre_mix, wperm, dproj, dx_tail, early_grads[2](dw_in),
                                             "premix_bwd")
    dmod_c, dg_c = premix_bwd(ctx, mod3, g_pre_mix, wperm, dprojc, None, no_exchange(), "premix_bwd_ctx")
    dmod = jnp.concatenate([jnp.concatenate([dmod_a[:, 0:2], dmod_t[:, 2:6]], axis=1), dmod_c], axis=0)
    last = jnp.pad(jnp.concatenate([drd[:, :, 0].T.reshape(8), loss_t[0, 0:1]]), (0, D - 9)).reshape(1, D)
    small = jnp.concatenate([dg_a[0:1], dg_c[0:1], dg_t[0:3], _pad_row(dgn, 1), dpat.reshape(8, D), last], axis=0)
    return grad_x, late, early, dmod, small


def kernel(x, c, ctx, c_ctx, w_ada, b_ada, g_pre_mix, g_post_mix, g_pre_mlp, g_post_mlp, w_in, ret_decay, ret_gn, na_rpb, w_out, w_mlp1, w_mlp2, loss_target, m_c_ctx, m_w_ada, m_b_ada, m_g_pre_mix, m_g_post_mix, m_g_pre_mlp, m_g_post_mlp, m_w_in, m_ret_decay, m_ret_gn, m_na_rpb, m_w_out, m_w_mlp1, m_w_mlp2, v_c_ctx, v_w_ada, v_b_ada, v_g_pre_mix, v_g_post_mix, v_g_pre_mlp, v_g_post_mlp, v_w_in, v_ret_decay, v_ret_gn, v_na_rpb, v_w_out, v_w_mlp1, v_w_mlp2):
    px, py, pc = _place()
    dev = 4 * px + 2 * py + pc
    chip = 2 * px + py

    def my_half(w2d):
        rows = w2d.shape[0] // 2
        return lax.dynamic_slice_in_dim(w2d, pc * rows, rows, 0)

    halves = [my_half(w[0]).astype(BF16) for w in (w_in, w_out, w_mlp1, w_mlp2)]
    cin, mg, gw_in = prologue(jnp.pad(c, ((0, 6), (0, 0))), c_ctx[None], w_ada[0],
                              lax.dynamic_slice_in_dim(b_ada, chip * 1536, 1536, 1), halves[0])
    wperm = unpack_w_in(gw_in.reshape(4, D, 896))
    mod_all = jnp.concatenate([mg[0], mg[2], mg[4], mg[6]], axis=1)
    mod3 = (jnp.pad(lax.dynamic_slice_in_dim(mod_all, 2 * dev, 2, 0), ((0, 1), (0, 0)))
            + jnp.pad(mod_all[16:17], ((2, 0), (0, 0)))).reshape(3, 6, D)

    place = jnp.stack([pc, chip]).astype(jnp.int32)

    early_names = ["w_out", "w_mlp1", "w_mlp2"]
    early_g8, early_partial = [], []

    def early_a(dw_out, dw1, dw2):
        early_g8[:] = [dw_out.reshape(8, 128, D), dw1.reshape(8, 512, D), dw2.reshape(8, 512, D)]
        return siblings4(early_g8)

    def early_b(landed):
        early_partial[:] = chip_partial(place, early_g8, landed, "rs_chip_sum_early")
        return chips3(early_partial)

    late_partial = []

    def late_c(dw_in):
        g8_in = pack_w_in(dw_in).reshape(8, 512, 896)
        (landed_in,) = sibling_blocks([g8_in], "rs_sibling_w_in")
        late_partial[:] = chip_partial(place, [g8_in], [landed_in], "rs_chip_sum_w_in")
        return chips3(late_partial)

    grad_x, (landed3_in,), early_landed, dmod, small = local_step(
        x, ctx, loss_target, mod3, g_pre_mix, g_post_mix, g_pre_mlp, g_post_mlp, ret_decay[0], ret_gn, na_rpb[0],
        wperm, lambda k: gather8(halves[1:2] if k == 0 else halves[2:4]), (early_a, early_b, late_c))
    early_mine = shard_sum(place, early_partial, early_landed, "rs_shard_sum_early")

    pay = jnp.concatenate([dmod.reshape(18, D), small, jnp.zeros((40 - 18 - SMALL_SUM_ROWS, D), F32)], axis=0)
    *early_theirs, gs = run_hosted(both(siblings(early_mine), gather8([pay])), "rs_halves_early_gather_small")
    gbf = gs[:, 0:12].reshape(16, 6 * D)
    gcf = gs[:, 12:18].reshape(8, 6 * D)
    gw_ada, pc_part = ada_grads(cin, lax.dynamic_slice_in_dim(gbf, chip * 1536, 1536, 1),
                                lax.dynamic_slice_in_dim(gcf, chip * 1536, 1536, 1), w_ada[0])
    (mine_in,) = shard_sum(place, late_partial, [landed3_in], "rs_shard_sum_w_in")
    theirs_in, pcg = run_hosted(both(siblings([mine_in]), gather8([pc_part])), "rs_halves_w_in_gather_c_ctx")

    grouped = adamw_group(
        place,
        [(w_mlp1[0], early_mine[1], early_theirs[1], m_w_mlp1[0], v_w_mlp1[0]),
         (w_mlp2[0], early_mine[2], early_theirs[2], m_w_mlp2[0], v_w_mlp2[0])],
        [(w_ada[0], gw_ada, m_w_ada[0], v_w_ada[0])], no_exchange(), "adamw_group")
    d_ada, m_ada, v_ada = grouped[8:11]
    big = [
        [r[None] for r in adamw_halves(place, w_in[0], mine_in, theirs_in, m_w_in[0], v_w_in[0], "adamw_w_in")],
        [r[None] for r in adamw_halves(place, w_out[0], early_mine[0], early_theirs[0], m_w_out[0], v_w_out[0],
                                       "adamw_w_out")],
        [r[None] for r in grouped[0:4]], [r[None] for r in grouped[4:8]],
    ]

    def rpb_rows(t):
        return _rpb_flat(t[0]).reshape(8, D)

    def decay_row(t):
        return jnp.pad(t.reshape(1, 8), ((0, 0), (0, D - 8)))

    views = [lambda t: t.reshape(1, D), lambda t: t, lambda t: t, lambda t: t, lambda t: t, lambda t: t, lambda t: t,
             rpb_rows, decay_row]
    back = [lambda t: t.reshape(D), lambda t: t, lambda t: t, lambda t: t, lambda t: t, lambda t: t, lambda t: t,
            lambda t: _rpb_flat_t(t)[None], lambda t: t[:, 0:8].reshape(1, 2, 4)]
    small_w = (c_ctx, b_ada, g_pre_mix, g_post_mix, g_pre_mlp, g_post_mlp, ret_gn, na_rpb, ret_decay)
    small_m = (m_c_ctx, m_b_ada, m_g_pre_mix, m_g_post_mix, m_g_pre_mlp, m_g_post_mlp, m_ret_gn, m_na_rpb, m_ret_decay)
    small_v = (v_c_ctx, v_b_ada, v_g_pre_mix, v_g_post_mix, v_g_pre_mlp, v_g_post_mlp, v_ret_gn, v_na_rpb, v_ret_decay)
    *res, loss8 = small_update(gs[:, 18:18 + SMALL_SUM_ROWS], gbf, gcf, pcg[:, 0],
                               [(f(w), f(m), f(v)) for f, w, m, v in zip(views, small_w, small_m, small_v)])

    def leaves(ada, idx):
        s_c, s_b, s_g1, s_g2, s_g3, s_g4, s_gn, s_rpb, s_rd = [back[i](res[4 * i + idx]) for i in range(9)]
        return [s_c, ada[None], s_b, s_g1, s_g2, s_g3, s_g4, big[0][idx], s_rd, s_gn, s_rpb,
                big[1][idx], big[2][idx], big[3][idx]]

    return (loss8[0, 0], grad_x, *leaves(gw_ada, 0), *leaves(d_ada, 1), *leaves(m_ada, 2), *leaves(v_ada, 3))
```

```python
import functools

import jax
import jax.numpy as jnp
from jax import lax
from jax.experimental import pallas as pl
from jax.experimental.pallas import tpu as pltpu

F32, BF16 = jnp.float32, jnp.bfloat16
D = 1024
SEQ = 2048
LC = 256
GW = 64
RH, RD, CH = 4, 128, 128
NPAIR = 4
IN_W = 3584
RET_W = 2048
DFF = 4096
EPS = 1e-6
NEG = -1e30
TN = 256
NCH = SEQ // CH
LR, B1, B2, AEPS, WD, STEP = 0.001, 0.9, 0.999, 1e-08, 0.01, 10
MESH = pl.DeviceIdType.MESH
VMEM_LIMIT = 56 * 1024 * 1024


def _cp(sem=None):
    return pltpu.CompilerParams(dimension_semantics=sem, vmem_limit_bytes=VMEM_LIMIT)


def _nn(a, b):
    return jnp.dot(a.astype(BF16), b.astype(BF16), preferred_element_type=F32)


def _nt(a, b):
    return lax.dot_general(a.astype(BF16), b.astype(BF16), (((1,), (1,)), ((), ())), preferred_element_type=F32)


def _tn(a, b):
    return lax.dot_general(a.astype(BF16), b.astype(BF16), (((0,), (0,)), ((), ())), preferred_element_type=F32)


@jax.custom_vjp
def mm_tn(a, b):
    return _tn(a, b)


mm_tn.defvjp(lambda a, b: (_tn(a, b), (a, b)), lambda r, g: (_nt(r[1], g), _nn(r[0], g)))


def _rms(x):
    return x * lax.rsqrt(jnp.mean(x * x, axis=-1, keepdims=True) + EPS)


def _rms_mod(x, g, sc, sh):
    return (_rms(x) * g) * (1.0 + sc) + sh


def _post_mix(x, mix, gt1, sc2, sh2, g_post_mix, g_pre_mlp):
    x1 = x + gt1 * (_rms(mix) * g_post_mix)
    return x1, _rms_mod(x1, g_pre_mlp, sc2, sh2)


def _head_loss(x1, m, gt2, g_post_mlp, tgt):
    err = x1 + gt2 * (_rms(m) * g_post_mlp) - tgt
    return 0.5 * jnp.sum(jnp.mean(err * err, axis=-1, keepdims=True), axis=0, keepdims=True)


def _ln_gate(o, g, w):
    mu = jnp.mean(o, axis=-1, keepdims=True)
    var = jnp.mean(jnp.square(o - mu), axis=-1, keepdims=True)
    y = (o - mu) * lax.rsqrt(var + EPS)
    return (y * w) * (g * jax.nn.sigmoid(g))


def _swap32(x):
    lane = lax.broadcasted_iota(jnp.int32, x.shape, 1)
    return jnp.where((lane & 32) == 0, pltpu.roll(x, 96, 1), pltpu.roll(x, 32, 1))


def _rope(x, cos, sin):
    return x * cos + _swap32(x) * sin


def _rope_t(g, cos, sin):
    return g * cos + _swap32(g * sin)


def _rope_tables():
    tok = jnp.arange(SEQ)
    pos_r = (tok // GW).astype(F32)
    pos_c = (tok % GW).astype(F32)
    inv = 10000.0 ** (-jnp.arange(32, dtype=F32) / 32)
    ar = pos_r[:, None] * inv[None, :]
    ac = pos_c[:, None] * inv[None, :]
    cos = jnp.concatenate([jnp.cos(ar), jnp.cos(ar), jnp.cos(ac), jnp.cos(ac)], axis=-1)
    sin = jnp.concatenate([-jnp.sin(ar), jnp.sin(ar), -jnp.sin(ac), jnp.sin(ac)], axis=-1)
    return cos, sin


def _chunk_loop(n, body, init, k=4):
    def several(t, carry):
        for i in range(k):
            carry = body(k * t + i, carry)
        return carry

    return lax.fori_loop(0, n // k, several, init)


def _fiota(shape, dim):
    return lax.broadcasted_iota(jnp.int32, shape, dim).astype(F32)


def _ret_state(k, v, s, lg, reverse):
    pos = _fiota((CH, 1), 0)
    b_exp = pos if reverse else (CH - 1.0 - pos)
    return jnp.exp(lg * CH) * s + mm_tn(k * jnp.exp(lg * b_exp), v)


class _Decays:
    def __init__(self, lgs):
        i, j, pos = _fiota((CH, CH), 0), _fiota((CH, CH), 1), _fiota((CH, 1), 0)
        diffs = (i - j, j - i)
        keep = (diffs[0] >= 0, diffs[1] > 0)
        mats = [jnp.where(m, jnp.exp(lg * jnp.where(m, d, 0.0)), 0.0) for lg, d, m in zip(lgs, diffs, keep)]
        self.mask = mats[0] + mats[1]
        self.dmask = [mats[0] * diffs[0], mats[1] * diffs[1]]
        a_exp, b_exp = (pos + 1.0, CH - pos), (CH - 1.0 - pos, pos)
        self.a = [jnp.exp(lg * e) for lg, e in zip(lgs, a_exp)]
        self.b = [jnp.exp(lg * e) for lg, e in zip(lgs, b_exp)]
        self.da = [a * e for a, e in zip(self.a, a_exp)]
        self.db = [b * e for b, e in zip(self.b, b_exp)]
        self.g = [jnp.exp(lg * CH) for lg in lgs]


def _both(x, w):
    return jnp.concatenate([x * w[0], x * w[1]], axis=1)


def _total(x):
    return jnp.sum(jnp.sum(x, axis=1, keepdims=True), axis=0, keepdims=True)


def _state_pass(dec, init, k_s, v_of, st_s):
    def step(t, carry):
        out = []
        for d, s in enumerate(carry):
            n = (NCH - 1 - t) if d else t
            sl = pl.ds(pl.multiple_of(n * CH, CH), CH)
            st_s[n, d * RD:(d + 1) * RD, :] = s
            out.append(dec.g[d] * s + _tn(k_s[sl, :] * dec.b[d], v_of(sl)))
        return tuple(out)

    _chunk_loop(NCH, step, tuple(init))


def premix_proj(xin, mod3, g_pre, wperm, is_ctx, name):
    nb, length, _ = xin.shape
    tn = min(2 * TN, length)

    def body(x_ref, mod_ref, g_ref, w_ref, h_ref, pret_ref, pna_ref):
        h = _rms_mod(x_ref[...], g_ref[...], mod_ref[1:2, :], mod_ref[0:1, :])
        hb = h.astype(BF16)
        h_ref[...] = hb
        pret_ref[...] = jnp.dot(hb, w_ref[:, :RET_W], preferred_element_type=F32)
        pna_ref[...] = jnp.dot(hb, w_ref[:, RET_W:], preferred_element_type=F32).astype(BF16)

    return pl.pallas_call(
        body, name=name, grid=(nb, length // tn),
        in_specs=[
            pl.BlockSpec((None, tn, D), lambda b, t: (b, t, 0)),
            pl.BlockSpec((None, 6, D), (lambda b, t: (2, 0, 0)) if is_ctx else (lambda b, t: (b, 0, 0))),
            pl.BlockSpec((1, D), lambda b, t: (0, 0)),
            pl.BlockSpec((D, IN_W), lambda b, t: (0, 0), pipeline_mode=pl.Buffered(1)),
        ],
        out_specs=[
            pl.BlockSpec((None, tn, D), lambda b, t: (b, t, 0)),
            pl.BlockSpec((None, tn, RET_W), lambda b, t: (b, t, 0)),
            pl.BlockSpec((None, tn, IN_W - RET_W), lambda b, t: (b, t, 0)),
        ],
        out_shape=[
            jax.ShapeDtypeStruct((nb, length, D), BF16),
            jax.ShapeDtypeStruct((nb, length, RET_W), F32),
            jax.ShapeDtypeStruct((nb, length, IN_W - RET_W), BF16),
        ],
        compiler_params=_cp(("arbitrary", "arbitrary")),
    )(xin, mod3, g_pre, wperm)


def premix_bwd(xin, mod3, g_pre, wperm, dproj, dx_tail, hosted, name):
    nb, length, _ = xin.shape
    tn = min(2 * TN, length)
    is_ctx = dx_tail is None

    def body(*refs):
        own_in, h_in, own_out, h_out, _, h_sems = hosted.split(refs, 5 if is_ctx else 6, 2 if is_ctx else 3)
        if is_ctx:
            (x_ref, mod_ref, g_ref, w_ref, dp_ref), (dmod_ref, dg_ref) = own_in, own_out
        else:
            (x_ref, mod_ref, g_ref, w_ref, dp_ref, dxt_ref), (dx_ref, dmod_ref, dg_ref) = own_in, own_out
        b, t = pl.program_id(0), pl.program_id(1)
        grid_step = b * (length // tn) + t

        @pl.when(grid_step == 0)
        def _():
            hosted.start(h_in, h_out, h_sems)

        @pl.when(grid_step == nb * (length // tn) - 1)
        def _():
            hosted.finish(h_in, h_out, h_sems)

        dh = lax.dot_general(dp_ref[...], w_ref[...], (((1,), (1,)), ((), ())), preferred_element_type=F32)
        _, vjp = jax.vjp(_rms_mod, x_ref[...], g_ref[...], mod_ref[1:2, :], mod_ref[0:1, :])
        dx, dg, dsc, dsh = vjp(dh)
        if not is_ctx:
            dx_ref[...] = dx + dxt_ref[...]

        @pl.when((t == 0) & ((b == 0) if is_ctx else True))
        def _():
            dmod_ref[...] = jnp.zeros_like(dmod_ref)

        @pl.when((t == 0) & (b == 0))
        def _():
            dg_ref[...] = jnp.zeros_like(dg_ref)

        dmod_ref[0:1, :] += dsh
        dmod_ref[1:2, :] += dsc
        dg_ref[0:1, :] += dg

    tok = lambda b, t: (b, t, 0)
    in_specs = [
        pl.BlockSpec((None, tn, D), tok),
        pl.BlockSpec((None, 6, D), (lambda b, t: (2, 0, 0)) if is_ctx else (lambda b, t: (b, 0, 0))),
        pl.BlockSpec((1, D), lambda b, t: (0, 0)),
        pl.BlockSpec((D, IN_W), lambda b, t: (0, 0), pipeline_mode=pl.Buffered(1)),
        pl.BlockSpec((None, tn, IN_W), tok),
    ]
    args = [xin, mod3, g_pre, wperm, dproj]
    out_specs = [
        pl.BlockSpec((None, 6, D), (lambda b, t: (0, 0, 0)) if is_ctx else (lambda b, t: (b, 0, 0))),
        pl.BlockSpec((8, D), lambda b, t: (0, 0)),
    ]
    out_shape = [jax.ShapeDtypeStruct((1 if is_ctx else nb, 6, D), F32), jax.ShapeDtypeStruct((8, D), F32)]
    if not is_ctx:
        in_specs.append(pl.BlockSpec((None, tn, D), tok))
        args.append(dx_tail)
        out_specs.insert(0, pl.BlockSpec((None, tn, D), tok))
        out_shape.insert(0, jax.ShapeDtypeStruct((nb, length, D), F32))
    h_in_specs, h_out_specs = hosted.specs()
    return pl.pallas_call(
        body, name=name, grid=(nb, length // tn), in_specs=in_specs + h_in_specs, out_specs=out_specs + h_out_specs,
        out_shape=out_shape + hosted.out_shape, scratch_shapes=hosted.scratch,
        compiler_params=_cp(("arbitrary", "arbitrary")),
    )(*args, *hosted.args)


def _ret_specs(order):
    def im(f):
        return lambda *g: f(*order(*g))
    return dict(
        pret=pl.BlockSpec((None, SEQ, 512), im(lambda b, h: (b, 0, h))),
        pretc=pl.BlockSpec((None, LC, 512), im(lambda b, h: (b, 0, h))),
        rd=pl.BlockSpec((None, 2, 1), im(lambda b, h: (h, 0, 0))),
        gn=pl.BlockSpec((None, 1, RD), im(lambda b, h: (h, 0, 0))),
        tab=pl.BlockSpec((SEQ, RD), im(lambda b, h: (0, 0))),
        head=pl.BlockSpec((None, SEQ, RD), im(lambda b, h: (b, 0, h))),
    )


def retention_fwd(pret, pretc, rd, gn, cos, sin, hosted):
    nb = pret.shape[0]
    sp = _ret_specs(lambda b, h: (b, h))

    def body(*refs):
        own_in, h_in, own_out, h_out, own_scr, h_sems = hosted.split(refs, 6, 2)
        p_ref, pc_ref, rd_ref, gn_ref, cos_ref, sin_ref = own_in
        (o_ref, mix_ref), (q_s, k_s, o_s, st_s) = own_out, own_scr
        grid_step = pl.program_id(0) * RH + pl.program_id(1)

        @pl.when(grid_step == 0)
        def _():
            hosted.start(h_in, h_out, h_sems)

        cos_v, sin_v = cos_ref[...], sin_ref[...]
        q_s[...] = _rope(p_ref[:, 0:128], cos_v, sin_v) * (RD ** -0.5)
        k_s[...] = _rope(p_ref[:, 128:256], cos_v, sin_v)
        lgs, init = [], []
        for rev in (False, True):
            lg = jax.nn.log_sigmoid(rd_ref[int(rev):int(rev) + 1, :])
            s = jnp.zeros((RD, RD), F32)
            for n in ((1, 0) if rev else (0, 1)):
                s = _ret_state(pc_ref[n * CH:(n + 1) * CH, 128:256], pc_ref[n * CH:(n + 1) * CH, 256:384], s, lg, rev)
            lgs.append(lg)
            init.append(s)

        dec = _Decays(lgs)
        _state_pass(dec, init, k_s, lambda sl: p_ref[sl, 256:384], st_s)

        def chunk(n, carry):
            sl = pl.ds(pl.multiple_of(n * CH, CH), CH)
            q = q_s[sl, :]
            o_s[sl, :] = (_nn(_nt(q, k_s[sl, :]) * dec.mask, p_ref[sl, 256:384]) + _nn(_both(q, dec.a), st_s[n]))
            return carry

        _chunk_loop(NCH, chunk, 0)
        o = o_s[...]
        o_ref[...] = o
        mix_ref[...] = _ln_gate(o, p_ref[:, 384:512], gn_ref[...]).astype(BF16)

        @pl.when(grid_step == nb * RH - 1)
        def _():
            hosted.finish(h_in, h_out, h_sems)

    h_in_specs, h_out_specs = hosted.specs()
    return pl.pallas_call(
        body, name="retention_fwd", grid=(nb, RH),
        in_specs=[sp["pret"], sp["pretc"], sp["rd"], sp["gn"], sp["tab"], sp["tab"]] + h_in_specs,
        out_specs=[sp["head"], sp["head"]] + h_out_specs,
        out_shape=[jax.ShapeDtypeStruct((nb, SEQ, RH * RD), F32), jax.ShapeDtypeStruct((nb, SEQ, D), BF16)]
        + hosted.out_shape,
        scratch_shapes=[pltpu.VMEM((SEQ, RD), F32)] * 3 + [pltpu.VMEM((NCH, 2 * RD, RD), F32)] + hosted.scratch,
        compiler_params=_cp(("arbitrary", "arbitrary")),
    )(pret, pretc, rd, gn, cos, sin, *hosted.args)


def retention_bwd(pret, pretc, o_all, dmixin, rd, gn, cos, sin, hosted):
    nb = pret.shape[0]
    sp = _ret_specs(lambda h, b: (b, h))

    def body(*refs):
        own_in, h_in, own_out, h_out, own_scr, h_sems = hosted.split(refs, 8, 4)
        p_ref, pc_ref, o_ref, dmix_ref, rd_ref, gn_ref, cos_ref, sin_ref = own_in
        dp_ref, dpc_ref, drd_ref, dgn_ref = own_out
        q_s, k_s, do_s, dq_s, dk_s, dv_s, st_s, gst_s = own_scr
        b = pl.program_id(1)
        grid_step = pl.program_id(0) * nb + b

        @pl.when(grid_step == 0)
        def _():
            hosted.start(h_in, h_out, h_sems)

        cos_v, sin_v = cos_ref[...], sin_ref[...]
        q_s[...] = _rope(p_ref[:, 0:128], cos_v, sin_v) * (RD ** -0.5)
        k_s[...] = _rope(p_ref[:, 128:256], cos_v, sin_v)
        _, gate_vjp = jax.vjp(_ln_gate, o_ref[...], p_ref[:, 384:512], gn_ref[...])
        do, dg, dgn = gate_vjp(dmix_ref[...].astype(F32))
        do_s[...] = do
        dp_ref[:, 384:512] = dg.astype(BF16)

        @pl.when(b == 0)
        def _():
            drd_ref[...] = jnp.zeros_like(drd_ref)
            dgn_ref[...] = jnp.zeros_like(dgn_ref)

        dgn_ref[...] += dgn
        kcs = [pc_ref[n * CH:(n + 1) * CH, 128:256] for n in (0, 1)]
        vcs = [pc_ref[n * CH:(n + 1) * CH, 256:384] for n in (0, 1)]
        dirs = []
        init = []
        for rev in (False, True):
            rdv = rd_ref[int(rev):int(rev) + 1, :]
            lg = jax.nn.log_sigmoid(rdv)
            order_c = (1, 0) if rev else (0, 1)
            s = jnp.zeros((RD, RD), F32)
            ctx_states = []
            for n in order_c:
                ctx_states.append(s)
                s = _ret_state(kcs[n], vcs[n], s, lg, rev)
            dirs.append((rev, order_c, lg, rdv, ctx_states))
            init.append(s)
        dec = _Decays([lg for _, _, lg, _, _ in dirs])

        def v_of(sl):
            return p_ref[sl, 256:384]

        _state_pass(dec, init, k_s, v_of, st_s)
        zeros = jnp.zeros((CH, RD), F32)

        def scores_back(n, carry):
            dmask_sum, da_f, da_b = carry
            sl = pl.ds(pl.multiple_of(n * CH, CH), CH)
            q, k, v, do = q_s[sl, :], k_s[sl, :], v_of(sl), do_s[sl, :]
            scores = _nt(q, k)
            d_att = _nt(do, v)
            d_scores = d_att * dec.mask
            d_qa = _nt(do, st_s[n])
            d_qf, d_qb = d_qa[:, 0:RD], d_qa[:, RD:2 * RD]
            dq_s[sl, :] = _nn(d_scores, k) + d_qf * dec.a[0] + d_qb * dec.a[1]
            dk_s[sl, :] = _tn(d_scores, q)
            dv_s[sl, :] = _tn(scores * dec.mask, do)
            gst_s[n] = _tn(_both(q, dec.a), do)
            return dmask_sum + d_att * scores, da_f + d_qf * q, da_b + d_qb * q

        dmask_sum, da_f, da_b = _chunk_loop(NCH, scores_back, (zeros, zeros, zeros))

        def state_back(t, carry):
            out = []
            for d, r in enumerate(carry):
                n = t if d else (NCH - 1 - t)
                rows = slice(d * RD, (d + 1) * RD)
                own = gst_s[n, rows, :]
                gst_s[n, rows, :] = r
                out.append(own + dec.g[d] * r)
            return tuple(out)

        d_states = _chunk_loop(NCH, state_back, (zeros, zeros))

        def updates_back(n, carry):
            db_f, db_b, dg_f, dg_b = carry
            sl = pl.ds(pl.multiple_of(n * CH, CH), CH)
            k, r, s = k_s[sl, :], gst_s[n], st_s[n]
            d_kw = _nt(v_of(sl), r)
            d_kf, d_kb = d_kw[:, 0:RD], d_kw[:, RD:2 * RD]
            dk_s[sl, :] += d_kf * dec.b[0] + d_kb * dec.b[1]
            dv_s[sl, :] += _nn(_both(k, dec.b), r)
            return (db_f + d_kf * k, db_b + d_kb * k, dg_f + r[0:RD, :] * s[0:RD, :],
                    dg_b + r[RD:2 * RD, :] * s[RD:2 * RD, :])

        db_dg = _chunk_loop(NCH, updates_back, (zeros, zeros, zeros, zeros))
        dkc = [None, None]
        dvc = [None, None]
        for d, ((rev, order_c, lg, rdv, ctx_states), ds) in enumerate(zip(dirs, d_states)):
            dlg = (_total(dmask_sum * dec.dmask[d]) + _total((da_f, da_b)[d] * dec.da[d])
                   + _total(db_dg[d] * dec.db[d]) + CH * dec.g[d] * _total(db_dg[2 + d]))
            for idx in (1, 0):
                n = order_c[idx]
                _, vjp = jax.vjp(functools.partial(_ret_state, reverse=rev), kcs[n], vcs[n], ctx_states[idx], lg)
                dk_c, dv_c, ds, dl = vjp(ds)
                dlg = dlg + dl
                dkc[n] = dk_c if dkc[n] is None else dkc[n] + dk_c
                dvc[n] = dv_c if dvc[n] is None else dvc[n] + dv_c
            drd_ref[int(rev):int(rev) + 1, :] += dlg * jax.nn.sigmoid(-rdv)
        dp_ref[:, 0:128] = _rope_t(dq_s[...] * (RD ** -0.5), cos_v, sin_v).astype(BF16)
        dp_ref[:, 128:256] = _rope_t(dk_s[...], cos_v, sin_v).astype(BF16)
        dp_ref[:, 256:384] = dv_s[...].astype(BF16)
        zero = jnp.zeros((CH, RD), BF16)
        for n in (0, 1):
            rows = slice(n * CH, (n + 1) * CH)
            dpc_ref[rows, 0:128] = zero
            dpc_ref[rows, 128:256] = dkc[n].astype(BF16)
            dpc_ref[rows, 256:384] = dvc[n].astype(BF16)
            dpc_ref[rows, 384:512] = zero

        @pl.when(grid_step == RH * nb - 1)
        def _():
            hosted.finish(h_in, h_out, h_sems)

    h_in_specs, h_out_specs = hosted.specs()
    return pl.pallas_call(
        body, name="retention_bwd", grid=(RH, nb),
        in_specs=[sp["pret"], sp["pretc"], sp["head"], sp["head"], sp["rd"], sp["gn"], sp["tab"], sp["tab"]]
        + h_in_specs,
        out_specs=[
            pl.BlockSpec((None, SEQ, 512), lambda h, b: (b, 0, h)),
            pl.BlockSpec((None, LC, 512), lambda h, b: (b, 0, h)),
            pl.BlockSpec((None, 2, 1), lambda h, b: (h, 0, 0)),
            pl.BlockSpec((None, 1, RD), lambda h, b: (h, 0, 0)),
        ] + h_out_specs,
        out_shape=[
            jax.ShapeDtypeStruct((nb, SEQ, IN_W), BF16),
            jax.ShapeDtypeStruct((nb, LC, IN_W), BF16),
            jax.ShapeDtypeStruct((RH, 2, 1), F32),
            jax.ShapeDtypeStruct((RH, 1, RD), F32),
        ] + hosted.out_shape,
        scratch_shapes=[pltpu.VMEM((SEQ, RD), F32)] * 6 + [pltpu.VMEM((NCH, 2 * RD, RD), F32)] * 2 + hosted.scratch,
        compiler_params=_cp(("arbitrary", "arbitrary")),
    )(pret, pretc, o_all, dmixin, rd, gn, cos, sin, *hosted.args)


def _rpb_flat(rpb):
    return jnp.pad(rpb, ((0, 0), (0, 1), (0, 33))).reshape(NPAIR, 2, 1, 1024)


def _rpb_flat_t(dflat):
    return dflat.reshape(8, 16, 64)[:, :15, :31]


def _barrel(x, left):
    row = lax.broadcasted_iota(jnp.int32, x.shape, 0)
    n = x.shape[1]
    for bit in range(6):
        s = 1 << bit
        x = jnp.where(((row >> bit) & 1) == 1, pltpu.roll(x, (n - s) if left else s, 1), x)
    return x


NA_TILE_ROWS, NA_BAND_ROWS = 4, 12
NA_Q, NA_K = NA_TILE_ROWS * GW, NA_BAND_ROWS * GW
NA_TILES = SEQ // NA_Q


def _band_start(r0):
    return min(max(r0 - 4, 0), 32 - NA_BAND_ROWS)


def _tile_layout(t):
    rows = range(t * NA_TILE_ROWS, (t + 1) * NA_TILE_ROWS)
    return tuple((r if r < 4 else (r - 24 if r > 28 else 4), min(max(r - 4, 0), 24) - _band_start(rows[0]))
                 for r in rows)


NA_CLASSES = sorted(set(_tile_layout(t) for t in range(NA_TILES)))


def _tile_rows(cls):
    return NA_CLASSES[cls]


def _na_tile(t):
    start = jnp.clip(NA_TILE_ROWS * t - 4, 0, 32 - NA_BAND_ROWS)
    cls = 0
    for tile in range(NA_TILES):
        cls = jnp.where(t == tile, NA_CLASSES.index(_tile_layout(tile)), cls)
    return pl.ds(pl.multiple_of(t * NA_Q, NA_Q), NA_Q), pl.ds(pl.multiple_of(start * GW, NA_Q), NA_K), cls


def _na_probs(qst, kb, kc, bias):
    s_loc = _nt(qst, kb) + bias
    s_ctx = _nt(qst, kc)
    m = jnp.maximum(jnp.max(s_loc, axis=1, keepdims=True), jnp.max(s_ctx, axis=1, keepdims=True))
    e_loc, e_ctx = jnp.exp(s_loc - m), jnp.exp(s_ctx - m)
    den = jnp.sum(e_loc, axis=1, keepdims=True) + jnp.sum(e_ctx, axis=1, keepdims=True)
    return e_loc / den, e_ctx / den


def _stack_heads(t):
    lane = lax.broadcasted_iota(jnp.int32, t.shape, 1)
    zero = jnp.zeros_like(t)
    return jnp.concatenate([jnp.where(lane < 64, t, zero), jnp.where(lane >= 64, t, zero)], axis=0)


def _unstack_heads(t):
    n = t.shape[0] // 2
    lane = lax.broadcasted_iota(jnp.int32, (n, 128), 1)
    return jnp.where(lane < 64, t[:n], t[n:])


def na_bias_table(flat):
    def body(flat_ref, out_ref):
        qc = lax.broadcasted_iota(jnp.int32, (GW, 512), 0)
        kc = lax.broadcasted_iota(jnp.int32, (GW, 512), 1) & 63
        start = jnp.clip(qc - 8, 0, GW - 16)
        window = (kc >= start) & (kc < start + 16)
        fill = jnp.full((GW, NA_K - 512), NEG, F32)
        for hh in (0, 1):
            skew = _barrel(pltpu.roll(jnp.broadcast_to(flat_ref[hh], (GW, 1024)), 1024 - 15, 1), left=False)
            by_class = [jnp.where(window, (skew if rc == 7 else pltpu.roll(skew, (9 + rc) * 64, 1))[:, 0:512], NEG)
                        for rc in range(8)]
            for cls in range(len(NA_CLASSES)):
                for qr, (rc, off) in enumerate(_tile_rows(cls)):
                    w = jnp.concatenate([by_class[rc], fill], axis=1)
                    rows = slice(hh * NA_Q + qr * GW, hh * NA_Q + (qr + 1) * GW)
                    out_ref[cls, rows, :] = pltpu.roll(w, off * GW, 1) if off else w

    return pl.pallas_call(
        body, name="na_bias_table", grid=(NPAIR,),
        in_specs=[pl.BlockSpec((None, 2, 1, 1024), lambda p: (p, 0, 0, 0))],
        out_specs=pl.BlockSpec((None, len(NA_CLASSES), 2 * NA_Q, NA_K), lambda p: (p, 0, 0, 0)),
        out_shape=jax.ShapeDtypeStruct((NPAIR, len(NA_CLASSES), 2 * NA_Q, NA_K), F32),
    )(flat)


def na_fwd(pna, pnac, bias, mixin, hosted):
    nb = pna.shape[0]

    def body(*refs):
        (p_ref, pc_ref, bias_ref, _), h_in, (out_ref,), h_out, _, h_sems = hosted.split(refs, 4, 1)
        grid_step = pl.program_id(0) * nb + pl.program_id(1)

        @pl.when(grid_step == 0)
        def _():
            hosted.start(h_in, h_out, h_sems)

        kc, vc = pc_ref[:, 128:256], pc_ref[:, 256:384]

        def tile(t, carry):
            qsl, bsl, cls = _na_tile(t)
            kb, vb = p_ref[bsl, 128:256], p_ref[bsl, 256:384]
            p_loc, p_ctx = _na_probs(_stack_heads(p_ref[qsl, 0:128] * 0.125), kb, kc, bias_ref[cls])
            out_ref[qsl, :] = _unstack_heads(_nn(p_loc, vb) + _nn(p_ctx, vc)).astype(BF16)
            return carry

        lax.fori_loop(0, NA_TILES, tile, 0, unroll=4)

        @pl.when(grid_step == NPAIR * nb - 1)
        def _():
            hosted.finish(h_in, h_out, h_sems)

    h_in_specs, h_out_specs = hosted.specs()
    return pl.pallas_call(
        body, name="na_fwd", grid=(NPAIR, nb),
        in_specs=[
            pl.BlockSpec((None, SEQ, 384), lambda p, b: (b, 0, p)),
            pl.BlockSpec((None, LC, 384), lambda p, b: (b, 0, p)),
            pl.BlockSpec((None, len(NA_CLASSES), 2 * NA_Q, NA_K), lambda p, b: (p, 0, 0, 0)),
            pl.BlockSpec(memory_space=pl.ANY),
        ] + h_in_specs,
        out_specs=[pl.BlockSpec((None, SEQ, 128), lambda p, b: (b, 0, 4 + p))] + h_out_specs,
        out_shape=[jax.ShapeDtypeStruct((nb, SEQ, D), BF16)] + hosted.out_shape,
        input_output_aliases={3: 0},
        scratch_shapes=hosted.scratch,
        compiler_params=_cp(("arbitrary", "arbitrary")),
    )(pna, pnac, bias, mixin, *hosted.args)


def na_bwd(pna, pnac, bias, dmixin, dproj, dprojc, hosted):
    nb = pna.shape[0]

    def body(*refs):
        own_in, h_in, own_out, h_out, own_scr, h_sems = hosted.split(refs, 6, 3)
        p_ref, pc_ref, bias_ref, dmix_ref = own_in[:4]
        dp_ref, dpc_ref, dpat_ref = own_out
        dbias_s, dk_s, dv_s, dkc_s, dvc_s, res_s, resc_s = own_scr
        b, part = pl.program_id(1), pl.program_id(2)
        grid_step = (pl.program_id(0) * nb + b) * 3 + part

        @pl.when(grid_step == 0)
        def _():
            hosted.start(h_in, h_out, h_sems)

        @pl.when(grid_step == NPAIR * nb * 3 - 1)
        def _():
            hosted.finish(h_in, h_out, h_sems)

        @pl.when(part == 0)
        def _():
            @pl.when(b == 0)
            def _():
                dbias_s[...] = jnp.zeros_like(dbias_s)

            dk_s[...] = jnp.zeros_like(dk_s)
            dv_s[...] = jnp.zeros_like(dv_s)
            dkc_s[...] = jnp.zeros_like(dkc_s)
            dvc_s[...] = jnp.zeros_like(dvc_s)
            kc, vc = pc_ref[:, 128:256], pc_ref[:, 256:384]

            def tile(t, carry):
                qsl, bsl, cls = _na_tile(t)
                kb, vb = p_ref[bsl, 128:256], p_ref[bsl, 256:384]
                qst, dost = _stack_heads(p_ref[qsl, 0:128] * 0.125), _stack_heads(dmix_ref[qsl, :])
                p_loc, p_ctx = _na_probs(qst, kb, kc, bias_ref[cls])
                dp_loc, dp_ctx = _nt(dost, vb), _nt(dost, vc)
                delta = (jnp.sum(p_loc * dp_loc, axis=1, keepdims=True)
                         + jnp.sum(p_ctx * dp_ctx, axis=1, keepdims=True))
                ds_loc, ds_ctx = p_loc * (dp_loc - delta), p_ctx * (dp_ctx - delta)
                dbias_s[cls] += ds_loc
                res_s[0, qsl, :] = _unstack_heads((_nn(ds_loc, kb) + _nn(ds_ctx, kc)) * 0.125).astype(BF16)
                dk_s[bsl, :] += _tn(ds_loc, qst)
                dv_s[bsl, :] += _tn(p_loc, dost)
                dkc_s[...] += _tn(ds_ctx, qst)
                dvc_s[...] += _tn(p_ctx, dost)
                return carry

            lax.fori_loop(0, NA_TILES, tile, 0, unroll=2)
            res_s[1] = dk_s[...].astype(BF16)
            res_s[2] = dv_s[...].astype(BF16)
            resc_s[0] = jnp.zeros((LC, 128), BF16)
            resc_s[1] = dkc_s[...].astype(BF16)
            resc_s[2] = dvc_s[...].astype(BF16)

            @pl.when(b == nb - 1)
            def _():
                for hh in (0, 1):
                    by_class = [None] * 8
                    for cls in range(len(NA_CLASSES)):
                        for qr, (rc, off) in enumerate(_tile_rows(cls)):
                            w = dbias_s[cls, hh * NA_Q + qr * GW:hh * NA_Q + (qr + 1) * GW, :]
                            w = (pltpu.roll(w, NA_K - off * GW, 1) if off else w)[:, 0:512]
                            by_class[rc] = w if by_class[rc] is None else by_class[rc] + w
                    skew = jnp.zeros((GW, 1024), F32)
                    for rc in range(8):
                        w = jnp.concatenate([by_class[rc], jnp.zeros((GW, 512), F32)], axis=1)
                        skew = skew + (w if rc == 7 else pltpu.roll(w, (7 - rc) * 64, 1))
                    dpat_ref[hh] = jnp.sum(pltpu.roll(_barrel(skew, left=True), 15, 1), axis=0, keepdims=True)

        dp_ref[...] = res_s[part]
        dpc_ref[...] = resc_s[part]

    h_in_specs, h_out_specs = hosted.specs()
    return pl.pallas_call(
        body, name="na_bwd", grid=(NPAIR, nb, 3),
        in_specs=[
            pl.BlockSpec((None, SEQ, 384), lambda p, b, s: (b, 0, p)),
            pl.BlockSpec((None, LC, 384), lambda p, b, s: (b, 0, p)),
            pl.BlockSpec((None, len(NA_CLASSES), 2 * NA_Q, NA_K), lambda p, b, s: (p, 0, 0, 0)),
            pl.BlockSpec((None, SEQ, 128), lambda p, b, s: (b, 0, 4 + p)),
            pl.BlockSpec(memory_space=pl.ANY),
            pl.BlockSpec(memory_space=pl.ANY),
        ] + h_in_specs,
        out_specs=[
            pl.BlockSpec((None, SEQ, 128), lambda p, b, s: (b, 0, 16 + 3 * p + s)),
            pl.BlockSpec((None, LC, 128), lambda p, b, s: (b, 0, 16 + 3 * p + s)),
            pl.BlockSpec((None, 2, 1, 1024), lambda p, b, s: (p, 0, 0, 0)),
        ] + h_out_specs,
        out_shape=[
            jax.ShapeDtypeStruct((nb, SEQ, IN_W), BF16),
            jax.ShapeDtypeStruct((nb, LC, IN_W), BF16),
            jax.ShapeDtypeStruct((NPAIR, 2, 1, 1024), F32),
        ] + hosted.out_shape,
        input_output_aliases={4: 0, 5: 1},
        scratch_shapes=[
            pltpu.VMEM((len(NA_CLASSES), 2 * NA_Q, NA_K), F32),
            pltpu.VMEM((SEQ, 128), F32), pltpu.VMEM((SEQ, 128), F32),
            pltpu.VMEM((LC, 128), F32), pltpu.VMEM((LC, 128), F32),
            pltpu.VMEM((3, SEQ, 128), BF16), pltpu.VMEM((3, LC, 128), BF16),
        ] + hosted.scratch,
        compiler_params=_cp(("arbitrary", "arbitrary", "arbitrary")),
    )(pna, pnac, bias, dmixin, dproj, dprojc, *hosted.args)


def tail_fwd_bwd(x, mixin, tgt, mod3, g_post_mix, g_pre_mlp, g_post_mlp, wout, w1, w2):
    nb = x.shape[0]

    def body(x_ref, mi_ref, tgt_ref, mod_ref, gpm_ref, gpl_ref, gpo_ref, wo_ref, w1_ref, w2_ref,
             dx_ref, dmix_ref, h2_ref, du_ref, a_ref, dm_ref, dmi_ref, dmod_ref, dg_ref, loss_ref):
        b, t = pl.program_id(0), pl.program_id(1)
        gt1, sh2, sc2, gt2 = mod_ref[2:3, :], mod_ref[3:4, :], mod_ref[4:5, :], mod_ref[5:6, :]
        mix = jnp.dot(mi_ref[...], wo_ref[...], preferred_element_type=F32)
        (x1, h2), vjp_a = jax.vjp(_post_mix, x_ref[...], mix, gt1, sc2, sh2, gpm_ref[...], gpl_ref[...])
        h2b = h2.astype(BF16)
        h2_ref[...] = h2b
        m = jnp.zeros((TN, D), F32)
        relus = []
        for j in range(4):
            cols = slice(j * D, (j + 1) * D)
            r = jnp.maximum(jnp.dot(h2b, w1_ref[j], preferred_element_type=F32), 0.0)
            ab = (r * r).astype(BF16)
            a_ref[:, cols] = ab
            m = m + jnp.dot(ab, w2_ref[cols, :], preferred_element_type=F32)
            relus.append(r)
        loss, vjp_b = jax.vjp(_head_loss, x1, m, gt2, gpo_ref[...], tgt_ref[...])
        dx1, dm, dgt2, dgpo, _ = vjp_b(jnp.ones((1, 1), F32))
        dmb = dm.astype(BF16)
        dm_ref[...] = dmb
        dh2 = jnp.zeros((TN, D), F32)
        for j in range(4):
            cols = slice(j * D, (j + 1) * D)
            da = lax.dot_general(dmb, w2_ref[cols, :], (((1,), (1,)), ((), ())), preferred_element_type=F32)
            dub = (da * (2.0 * relus[j])).astype(BF16)
            du_ref[:, cols] = dub
            dh2 = dh2 + lax.dot_general(dub, w1_ref[j], (((1,), (1,)), ((), ())), preferred_element_type=F32)
        dx, dmix, dgt1, dsc2, dsh2, dgpm, dgpl = vjp_a((dx1, dh2))
        dx_ref[...] = dx
        dmixb = dmix.astype(BF16)
        dmix_ref[...] = dmixb
        dmi_ref[...] = lax.dot_general(dmixb, wo_ref[...], (((1,), (1,)), ((), ())),
                                       preferred_element_type=F32).astype(BF16)

        @pl.when(t == 0)
        def _():
            dmod_ref[...] = jnp.zeros_like(dmod_ref)

        @pl.when((t == 0) & (b == 0))
        def _():
            dg_ref[...] = jnp.zeros_like(dg_ref)
            loss_ref[...] = jnp.zeros_like(loss_ref)

        dmod_ref[2:3, :] += dgt1
        dmod_ref[3:4, :] += dsh2
        dmod_ref[4:5, :] += dsc2
        dmod_ref[5:6, :] += dgt2
        dg_ref[0:1, :] += dgpm
        dg_ref[1:2, :] += dgpl
        dg_ref[2:3, :] += dgpo
        loss_ref[...] += jnp.broadcast_to(loss, loss_ref.shape)

    tok = lambda b, t: (b, t, 0)
    const = lambda b, t: (0, 0)
    vec = pl.BlockSpec((1, D), const)
    return pl.pallas_call(
        body, name="tail_fwd_bwd", grid=(nb, SEQ // TN),
        in_specs=[
            pl.BlockSpec((None, TN, D), tok), pl.BlockSpec((None, TN, D), tok), pl.BlockSpec((None, TN, D), tok),
            pl.BlockSpec((None, 6, D), lambda b, t: (b, 0, 0)), vec, vec, vec,
            pl.BlockSpec((D, D), const, pipeline_mode=pl.Buffered(1)),
            pl.BlockSpec((4, D, D), lambda b, t: (0, 0, 0), pipeline_mode=pl.Buffered(1)),
            pl.BlockSpec((DFF, D), const, pipeline_mode=pl.Buffered(1)),
        ],
        out_specs=[
            pl.BlockSpec((None, TN, D), tok), pl.BlockSpec((None, TN, D), tok), pl.BlockSpec((None, TN, D), tok),
            pl.BlockSpec((None, TN, DFF), tok), pl.BlockSpec((None, TN, DFF), tok), pl.BlockSpec((None, TN, D), tok),
            pl.BlockSpec((None, TN, D), tok),
            pl.BlockSpec((None, 6, D), lambda b, t: (b, 0, 0)),
            pl.BlockSpec((8, D), const), pl.BlockSpec((8, 128), const),
        ],
        out_shape=[
            jax.ShapeDtypeStruct((nb, SEQ, D), F32), jax.ShapeDtypeStruct((nb, SEQ, D), BF16),
            jax.ShapeDtypeStruct((nb, SEQ, D), BF16), jax.ShapeDtypeStruct((nb, SEQ, DFF), BF16),
            jax.ShapeDtypeStruct((nb, SEQ, DFF), BF16), jax.ShapeDtypeStruct((nb, SEQ, D), BF16),
            jax.ShapeDtypeStruct((nb, SEQ, D), BF16),
            jax.ShapeDtypeStruct((nb, 6, D), F32), jax.ShapeDtypeStruct((8, D), F32),
            jax.ShapeDtypeStruct((8, 128), F32),
        ],
        compiler_params=_cp(("arbitrary", "arbitrary")),
    )(x, mixin, tgt, mod3, g_post_mix, g_pre_mlp, g_post_mlp, wout, w1, w2)


def weight_grad(pairs, name, out_dtype=F32, col_blocks=False, tm=1024, tn=1024, tk=2048):
    m, n = pairs[0][0].shape[1], pairs[0][1].shape[1]
    tn = min(tn, n)
    tks = [min(tk, xa.shape[0]) for xa, _ in pairs]
    steps = [xa.shape[0] // t for (xa, _), t in zip(pairs, tks)]
    total = sum(steps)
    offs = [sum(steps[:i]) for i in range(len(pairs))]

    def body(*refs):
        out_ref, acc = refs[2 * len(pairs)], refs[-1]
        k = pl.program_id(2)

        @pl.when(k == 0)
        def _():
            acc[...] = jnp.zeros_like(acc)

        for i in range(len(pairs)):
            @pl.when((k >= offs[i]) & (k < offs[i] + steps[i]))
            def _(i=i):
                acc[...] += lax.dot_general(refs[2 * i][...], refs[2 * i + 1][...], (((0,), (0,)), ((), ())),
                                            preferred_element_type=F32)

        if out_dtype != F32:
            @pl.when(k == total - 1)
            def _():
                out_ref[...] = acc[...].astype(out_dtype)

    in_specs, args = [], []
    for i, (xa, ya) in enumerate(pairs):
        clamp = lambda k, i=i: jnp.clip(k - offs[i], 0, steps[i] - 1)
        in_specs.append(pl.BlockSpec((tks[i], tm), lambda a, c, k, clamp=clamp: (clamp(k), a)))
        in_specs.append(pl.BlockSpec((tks[i], tn), lambda a, c, k, clamp=clamp: (clamp(k), c)))
        args += [xa, ya]
    if col_blocks:
        out_spec = pl.BlockSpec((None, tm, tn), lambda a, c, k: (c, a, 0))
        out_shape = jax.ShapeDtypeStruct((n // tn, m, tn), out_dtype)
    else:
        out_spec = pl.BlockSpec((tm, tn), lambda a, c, k: (a, c))
        out_shape = jax.ShapeDtypeStruct((m, n), out_dtype)
    return pl.pallas_call(
        body, name=name, grid=(m // tm, n // tn, total), in_specs=in_specs, out_specs=out_spec, out_shape=out_shape,
        scratch_shapes=[] if out_dtype == F32 else [pltpu.VMEM((tm, tn), F32)],
        compiler_params=_cp(("arbitrary", "arbitrary", "arbitrary")),
    )(*args)


def _perm_block(t):
    return 4 * (t % 4) + t // 4 if t < 16 else 16 + 3 * ((t - 16) % 4) + (t - 16) // 4


def unpack_w_in(blocks):
    def body(i_ref, o_ref):
        for t in range(28):
            p = _perm_block(t)
            o_ref[:, p * 128:(p + 1) * 128] = i_ref[t // 7, :, (t % 7) * 128:(t % 7 + 1) * 128]

    return pl.pallas_call(
        body, name="unpack_w_in", grid=(2,),
        in_specs=[pl.BlockSpec((4, D // 2, 896), lambda i: (0, i, 0))],
        out_specs=pl.BlockSpec((D // 2, IN_W), lambda i: (i, 0)),
        out_shape=jax.ShapeDtypeStruct((D, IN_W), BF16),
    )(blocks)


def pack_w_in(dw):
    def body(i_ref, o_ref):
        for t in range(28):
            p = _perm_block(t)
            o_ref[t // 7, :, (t % 7) * 128:(t % 7 + 1) * 128] = i_ref[:, p * 128:(p + 1) * 128].astype(BF16)

    return pl.pallas_call(
        body, name="pack_w_in", grid=(4,),
        in_specs=[pl.BlockSpec((D // 4, IN_W), lambda i: (i, 0))],
        out_specs=pl.BlockSpec((4, D // 4, 896), lambda i: (0, i, 0)),
        out_shape=jax.ShapeDtypeStruct((4, D, 896), BF16),
    )(dw)


def _place():
    return lax.axis_index("x"), lax.axis_index("y"), lax.axis_index("c")


class Hosted:
    def __init__(self, args, out_shape, scratch, start, finish):
        self.args, self.out_shape, self.scratch, self.start, self.finish = args, out_shape, scratch, start, finish

    def specs(self):
        hbm = pl.BlockSpec(memory_space=pl.ANY)
        return [hbm] * len(self.args), [hbm] * len(self.out_shape)

    def split(self, refs, n_in, n_out):
        a, b = len(self.args), len(self.out_shape)
        cuts = [n_in, n_in + a, n_in + a + n_out, n_in + a + n_out + b, len(refs) - len(self.scratch)]
        parts = [refs[i:j] for i, j in zip([0] + cuts, cuts + [len(refs)])]
        return parts[0], parts[1], parts[2], parts[3], parts[4], parts[5]


def no_exchange():
    return Hosted([], [], [], lambda *a: None, lambda *a: None)


def run_hosted(hosted, name):
    def body(*refs):
        _, ins, _, outs, _, sems = hosted.split(refs, 0, 0)
        hosted.start(ins, outs, sems)
        hosted.finish(ins, outs, sems)

    in_specs, out_specs = hosted.specs()
    return pl.pallas_call(body, name=name, in_specs=in_specs, out_specs=out_specs, out_shape=hosted.out_shape,
                          scratch_shapes=hosted.scratch)(*hosted.args)


def gather8(blocks):
    na = len(blocks)

    def copies(ins, outs, sems):
        send_sems, recv_sems, local_sem = sems
        x, y, c = _place()
        me, sibling = (x, y, c), (x, y, 1 - c)
        chips = [(1 - x, y), (x, 1 - y), (1 - x, 1 - y)]

        def slot(o_ref, px, py, pc):
            return o_ref.at[4 * px + 2 * py + pc]

        def copy(a, k, block, to, src=None):
            return pltpu.make_async_remote_copy(
                src_ref=slot(outs[a], *block) if src is None else src, dst_ref=slot(outs[a], *block),
                send_sem=send_sems.at[a, k], recv_sem=recv_sems.at[a, k], device_id=to, device_id_type=MESH)

        mine = [pltpu.make_async_copy(ins[a], slot(outs[a], *me), local_sem.at[a]) for a in range(na)]
        first = []
        for a in range(na):
            first.append(copy(a, 0, me, sibling, src=ins[a]))
            first += [copy(a, 1 + j, me, (*chip, c), src=ins[a]) for j, chip in enumerate(chips)]
        return copy, mine, first, me, sibling, chips, c

    def start(ins, outs, sems):
        _, mine, first, *_ = copies(ins, outs, sems)
        for cp in mine + first:
            cp.start()

    def finish(ins, outs, sems):
        copy, mine, first, me, sibling, chips, c = copies(ins, outs, sems)
        passed = []
        for j, chip in enumerate(chips):
            for a in range(na):
                copy(a, 1 + j, (*chip, c), me).wait_recv()
                cp = copy(a, 4 + j, (*chip, c), sibling)
                cp.start()
                passed.append(cp)
        for a in range(na):
            copy(a, 0, sibling, me).wait_recv()
            for j, chip in enumerate(chips):
                copy(a, 4 + j, (*chip, 1 - c), me).wait_recv()
        for cp in first + passed:
            cp.wait_send()
        for cp in mine:
            cp.wait()

    return Hosted(list(blocks), [jax.ShapeDtypeStruct((8,) + b.shape, b.dtype) for b in blocks],
                  [pltpu.SemaphoreType.DMA((na, 7)), pltpu.SemaphoreType.DMA((na, 7)), pltpu.SemaphoreType.DMA((na,))],
                  start, finish)


def chips3(arrays):
    na = len(arrays)

    def copies(ins, outs, sems):
        send_sems, recv_sems = sems
        x, y, c = _place()
        return [pltpu.make_async_remote_copy(
            src_ref=ins[a].at[2 * px + py], dst_ref=outs[a].at[k], send_sem=send_sems.at[a, k],
            recv_sem=recv_sems.at[a, k], device_id=(px, py, c), device_id_type=MESH)
            for a in range(na) for k, (px, py) in enumerate([(1 - x, y), (x, 1 - y), (1 - x, 1 - y)])]

    def start(ins, outs, sems):
        for cp in copies(ins, outs, sems):
            cp.start()

    def finish(ins, outs, sems):
        for cp in copies(ins, outs, sems):
            cp.wait()

    return Hosted(list(arrays), [jax.ShapeDtypeStruct((3,) + a.shape[1:], a.dtype) for a in arrays],
                  [pltpu.SemaphoreType.DMA((na, 3)), pltpu.SemaphoreType.DMA((na, 3))], start, finish)


def siblings(arrays):
    na = len(arrays)

    def copies(ins, outs, sems):
        send_sems, recv_sems = sems
        x, y, c = _place()
        return [pltpu.make_async_remote_copy(
            src_ref=ins[a], dst_ref=outs[a], send_sem=send_sems.at[a], recv_sem=recv_sems.at[a],
            device_id=(x, y, 1 - c), device_id_type=MESH) for a in range(na)]

    def start(ins, outs, sems):
        for cp in copies(ins, outs, sems):
            cp.start()

    def finish(ins, outs, sems):
        for cp in copies(ins, outs, sems):
            cp.wait()

    return Hosted(list(arrays), [jax.ShapeDtypeStruct(a.shape, a.dtype) for a in arrays],
                  [pltpu.SemaphoreType.DMA((na,)), pltpu.SemaphoreType.DMA((na,))], start, finish)


def both(first, second):
    na, no, ns = len(first.args), len(first.out_shape), len(first.scratch)

    def start(ins, outs, sems):
        first.start(ins[:na], outs[:no], sems[:ns])
        second.start(ins[na:], outs[no:], sems[ns:])

    def finish(ins, outs, sems):
        first.finish(ins[:na], outs[:no], sems[:ns])
        second.finish(ins[na:], outs[no:], sems[ns:])

    return Hosted(first.args + second.args, first.out_shape + second.out_shape, first.scratch + second.scratch,
                  start, finish)


def siblings4(arrays):
    na = len(arrays)

    def copies(ins, outs, sems):
        send_sems, recv_sems = sems
        x, y, c = _place()
        return [pltpu.make_async_remote_copy(
            src_ref=ins[a].at[2 * j + 1 - c], dst_ref=outs[a].at[j],
            send_sem=send_sems.at[a, j], recv_sem=recv_sems.at[a, j],
            device_id=(x, y, 1 - c), device_id_type=MESH) for a in range(na) for j in range(4)]

    def start(ins, outs, sems):
        for cp in copies(ins, outs, sems):
            cp.start()

    def finish(ins, outs, sems):
        for cp in copies(ins, outs, sems):
            cp.wait()

    return Hosted(list(arrays), [jax.ShapeDtypeStruct((4,) + a.shape[1:], a.dtype) for a in arrays],
                  [pltpu.SemaphoreType.DMA((na, 4)), pltpu.SemaphoreType.DMA((na, 4))], start, finish)


def sibling_blocks(arrays, name):
    return run_hosted(siblings4(arrays), name)


def _row_tile(r):
    for cand in (512, 256, 128, 64, 32, 16, 8):
        if r % cand == 0:
            return cand
    return r


def chip_partial(place, g8s, landed4s, name):
    n = len(g8s)

    def body(place_ref, *refs):
        del place_ref
        for g_ref, l_ref, o_ref in zip(refs[:n], refs[n:2 * n], refs[2 * n:]):
            o_ref[...] = (g_ref[...].astype(F32) + l_ref[...].astype(F32)).astype(BF16)

    own = [pl.BlockSpec((None,) + g.shape[1:], lambda j, s: (2 * j + s[0], 0, 0)) for g in g8s]
    plain = [pl.BlockSpec((None,) + g.shape[1:], lambda j, s: (j, 0, 0)) for g in g8s]
    return pl.pallas_call(
        body, name=name,
        grid_spec=pltpu.PrefetchScalarGridSpec(num_scalar_prefetch=1, grid=(4,), in_specs=own + plain, out_specs=plain),
        out_shape=[jax.ShapeDtypeStruct((4,) + g.shape[1:], BF16) for g in g8s],
    )(place, *g8s, *landed4s)


def shard_sum(place, partial4s, landed3s, name):
    n = len(partial4s)

    def body(place_ref, *refs):
        del place_ref
        for p_ref, l_ref, o_ref in zip(refs[:n], refs[n:2 * n], refs[2 * n:]):
            acc = p_ref[...].astype(F32)
            for k in range(3):
                acc = acc + l_ref[k].astype(F32)
            o_ref[...] = acc

    def halves(p, lead):
        r, ccols = p.shape[1:]
        return (lead, r // 2, ccols)

    return pl.pallas_call(
        body, name=name,
        grid_spec=pltpu.PrefetchScalarGridSpec(
            num_scalar_prefetch=1, grid=(2,),
            in_specs=[pl.BlockSpec(halves(p, None), lambda i, s: (s[1], i, 0)) for p in partial4s]
            + [pl.BlockSpec(halves(p, 3), lambda i, s: (0, i, 0)) for p in partial4s],
            out_specs=[pl.BlockSpec(halves(p, None)[1:], lambda i, s: (i, 0)) for p in partial4s]),
        out_shape=[jax.ShapeDtypeStruct(p.shape[1:], F32) for p in partial4s],
    )(place, *partial4s, *landed3s)


def _adamw_math(w, g, m, v):
    m2 = B1 * m + (1.0 - B1) * g
    v2 = B2 * v + (1.0 - B2) * (g * g)
    m_hat = m2 / (1.0 - B1 ** STEP)
    v_hat = v2 / (1.0 - B2 ** STEP)
    return -LR * (m_hat / (jnp.sqrt(v_hat) + AEPS) + WD * w), m2, v2


def adamw_halves(place, w, mine, theirs, m, v, name):
    r, ccols = w.shape
    hr = r // 2
    tr = _row_tile(hr)
    nt = hr // tr

    def body(place_ref, w_ref, a_ref, b_ref, m_ref, v_ref, g_out, d_out, m_out, v_out):
        g = jnp.where(pl.program_id(0) == place_ref[0], a_ref[...], b_ref[...])
        d, m2, v2 = _adamw_math(w_ref[...], g, m_ref[...], v_ref[...])
        g_out[...] = g
        d_out[...] = d
        m_out[...] = m2
        v_out[...] = v2

    full = pl.BlockSpec((tr, ccols), lambda h, i, s: (h * nt + i, 0))
    part = pl.BlockSpec((tr, ccols), lambda h, i, s: (i, 0))
    return pl.pallas_call(
        body, name=name,
        grid_spec=pltpu.PrefetchScalarGridSpec(
            num_scalar_prefetch=1, grid=(2, nt), in_specs=[full, part, part, full, full], out_specs=[full] * 4),
        out_shape=[jax.ShapeDtypeStruct((r, ccols), F32)] * 4,
    )(place, w, mine, theirs, m, v)


def adamw_group(place, halved, plain, hosted, name):
    rows = halved[0][0].shape[0]
    tr = 64
    nt = rows // 2 // tr
    nh, npl = len(halved), len(plain)

    def body(place_ref, *refs):
        own_in, h_in, own_out, h_out, _, h_sems = hosted.split(refs, 5 * nh + 4 * npl, 4 * nh + 3 * npl)
        half = pl.program_id(0)
        grid_step = half * nt + pl.program_id(1)

        @pl.when(grid_step == 0)
        def _():
            hosted.start(h_in, h_out, h_sems)

        for i in range(nh):
            w_ref, a_ref, b_ref, m_ref, v_ref = own_in[5 * i:5 * i + 5]
            g = jnp.where(half == place_ref[0], a_ref[...], b_ref[...])
            res = (g,) + _adamw_math(w_ref[...], g, m_ref[...], v_ref[...])
            for o_ref, r in zip(own_out[4 * i:4 * i + 4], res):
                o_ref[...] = r
        for i in range(npl):
            w_ref, g_ref, m_ref, v_ref = own_in[5 * nh + 4 * i:5 * nh + 4 * i + 4]
            res = _adamw_math(w_ref[...], g_ref[...], m_ref[...], v_ref[...])
            for o_ref, r in zip(own_out[4 * nh + 3 * i:4 * nh + 3 * i + 3], res):
                o_ref[...] = r

        @pl.when(grid_step == 2 * nt - 1)
        def _():
            hosted.finish(h_in, h_out, h_sems)

    def full(cols):
        return pl.BlockSpec((tr, cols), lambda h, i, s: (h * nt + i, 0))

    def part(cols):
        return pl.BlockSpec((tr, cols), lambda h, i, s: (i, 0))

    in_specs, out_specs, out_shape, args = [], [], [], []
    for w, a, b, m, v in halved:
        cols = w.shape[1]
        in_specs += [full(cols), part(cols), part(cols), full(cols), full(cols)]
        out_specs += [full(cols)] * 4
        out_shape += [jax.ShapeDtypeStruct(w.shape, F32)] * 4
        args += [w, a, b, m, v]
    for w, g, m, v in plain:
        cols = w.shape[1]
        in_specs += [full(cols)] * 4
        out_specs += [full(cols)] * 3
        out_shape += [jax.ShapeDtypeStruct(w.shape, F32)] * 3
        args += [w, g, m, v]
    h_in_specs, h_out_specs = hosted.specs()
    return pl.pallas_call(
        body, name=name,
        grid_spec=pltpu.PrefetchScalarGridSpec(
            num_scalar_prefetch=1, grid=(2, nt), in_specs=in_specs + h_in_specs, out_specs=out_specs + h_out_specs,
            scratch_shapes=hosted.scratch),
        out_shape=out_shape + hosted.out_shape,
        compiler_params=_cp(("arbitrary", "arbitrary")),
    )(place, *args, *hosted.args)


def _silu(x):
    return x * jax.nn.sigmoid(x)


def prologue(c_rows, c_ctx_row, w_ada, b_shard, half_w_in):
    shape = jax.ShapeDtypeStruct
    g_w = gather8([half_w_in])
    g_c = gather8([shape((8, D), F32)])
    g_m = gather8([shape((32, 1536), F32)])

    def body(c_ref, cc_ref, w_ref, b_ref, hw_ref, cin_ref, mg_ref, gw_ref, cg_s, ms_s, *sems):
        sw, sc, sm = sems[0:3], sems[3:6], sems[6:9]
        g_w.start([hw_ref], [gw_ref], sw)
        g_c.start([c_ref], [cg_s], sc)
        g_c.finish([c_ref], [cg_s], sc)
        cin_ref[...] = jnp.zeros_like(cin_ref)
        for dev in range(8):
            cin_ref[2 * dev:2 * dev + 2, :] = cg_s[dev, 0:2, :]
        cin_ref[16:17, :] = cc_ref[...]
        ms_s[...] = _nn(_silu(cin_ref[...]), w_ref[...]) + b_ref[...]
        g_m.start([ms_s], [mg_ref], sm)
        g_m.finish([ms_s], [mg_ref], sm)
        g_w.finish([hw_ref], [gw_ref], sw)

    vmem = pl.BlockSpec(memory_space=pltpu.VMEM)
    hbm = pl.BlockSpec(memory_space=pl.ANY)
    return pl.pallas_call(
        body, name="prologue", in_specs=[vmem, vmem, vmem, vmem, hbm], out_specs=[vmem, vmem, hbm],
        out_shape=[shape((32, D), F32), shape((8, 32, 1536), F32)] + g_w.out_shape,
        scratch_shapes=[pltpu.VMEM((8, 8, D), F32), pltpu.VMEM((32, 1536), F32)] + g_w.scratch + g_c.scratch
        + g_m.scratch,
        compiler_params=_cp(),
    )(c_rows, c_ctx_row, w_ada, b_shard, half_w_in)


def ada_grads(cin, gb, gc, w_ada):
    def body(c_ref, gb_ref, gc_ref, w_ref, gw_ref, pc_ref):
        ctx_tot = jnp.sum(gc_ref[...], axis=0, keepdims=True)
        rows = lax.broadcasted_iota(jnp.int32, (16, 512), 0)
        dm = jnp.concatenate([gb_ref[...], jnp.where(rows == 0, ctx_tot, 0.0)], axis=0)
        gw_ref[...] = _tn(_silu(c_ref[...]), dm)
        rows8 = lax.broadcasted_iota(jnp.int32, (8, 512), 0)
        part = _nt(jnp.where(rows8 == 0, ctx_tot, 0.0), w_ref[...])

        @pl.when(pl.program_id(0) == 0)
        def _():
            pc_ref[...] = jnp.zeros_like(pc_ref)

        pc_ref[...] += part

    return pl.pallas_call(
        body, name="ada_grads", grid=(3,),
        in_specs=[pl.BlockSpec((32, D), lambda j: (0, 0)), pl.BlockSpec((16, 512), lambda j: (0, j)),
                  pl.BlockSpec((8, 512), lambda j: (0, j)), pl.BlockSpec((D, 512), lambda j: (0, j))],
        out_specs=[pl.BlockSpec((D, 512), lambda j: (0, j)), pl.BlockSpec((8, D), lambda j: (0, 0))],
        out_shape=[jax.ShapeDtypeStruct((D, 1536), F32), jax.ShapeDtypeStruct((8, D), F32)],
    )(cin, gb, gc, w_ada)


SMALL_SUM_ROWS = 15


def small_update(gsm, gbf, gcf, pcg, params):
    n = len(params)

    def body(*refs):
        gsm_ref, gbf_ref, gcf_ref, pcg_ref = refs[:4]
        wmv, outs, loss_out = refs[4:4 + 3 * n], refs[4 + 3 * n:4 + 7 * n], refs[-1]
        acc = gsm_ref[0]
        for dev in range(1, 8):
            acc = acc + gsm_ref[dev]
        c_ctx = wmv[0][...]
        sg = jax.nn.sigmoid(c_ctx)
        dsilu = pcg_ref[0:1, :] + pcg_ref[2:3, :] + pcg_ref[4:5, :] + pcg_ref[6:7, :]
        lane = lax.broadcasted_iota(jnp.int32, (1, D), 1)
        last = acc[14:15, :]
        grads = [
            dsilu * (sg * (1.0 + c_ctx * (1.0 - sg))),
            jnp.sum(gbf_ref[...], axis=0, keepdims=True) + jnp.sum(gcf_ref[...], axis=0, keepdims=True),
            acc[0:1, :] + acc[1:2, :], acc[2:3, :], acc[3:4, :], acc[4:5, :],
            acc[5:6, 0:512], acc[6:14, :], jnp.where(lane < 8, last, 0.0),
        ]
        loss_out[...] = jnp.broadcast_to(jnp.sum(jnp.where(lane == 8, last, 0.0), axis=1, keepdims=True), (8, 128))
        for i, g in enumerate(grads):
            d, m2, v2 = _adamw_math(wmv[3 * i][...], g, wmv[3 * i + 1][...], wmv[3 * i + 2][...])
            outs[4 * i][...] = g
            outs[4 * i + 1][...] = d
            outs[4 * i + 2][...] = m2
            outs[4 * i + 3][...] = v2

    flat = [a for wmv in params for a in wmv]
    out_shape = [jax.ShapeDtypeStruct(w.shape, F32) for w, _, _ in params for _ in range(4)]
    return pl.pallas_call(
        body, name="small_update", out_shape=out_shape + [jax.ShapeDtypeStruct((8, 128), F32)],
    )(gsm, gbf, gcf, pcg, *flat)


def _pad_row(v, rows):
    flat = v.reshape(-1)
    return jnp.pad(flat, (0, rows * D - flat.shape[0])).reshape(rows, D)


def local_step(x, ctx, tgt, mod3, g_pre_mix, g_post_mix, g_pre_mlp, g_post_mlp, ret_decay, ret_gn, na_rpb,
               wperm, late_weights, early_grads):
    nb = x.shape[0]
    tokens = nb * SEQ
    cos, sin = _rope_tables()
    rd = ret_decay.T.reshape(RH, 2, 1)
    gn = ret_gn.reshape(RH, 1, RD)
    bias = na_bias_table(_rpb_flat(na_rpb))
    h, pret, pna = premix_proj(x, mod3, g_pre_mix, wperm, False, "premix_proj")
    hc, pretc, pnac = premix_proj(ctx, mod3, g_pre_mix, wperm, True, "premix_proj_ctx")
    o_all, mixin, gw_out = retention_fwd(pret, pretc, rd, gn, cos, sin, late_weights(0))
    mixin, gw1, gw2 = na_fwd(pna, pnac, bias, mixin, late_weights(1))
    dx_tail, dmix, h2, du, act, dm, dmixin, dmod_t, dg_t, loss_t = tail_fwd_bwd(
        x, mixin, tgt, mod3, g_post_mix, g_pre_mlp, g_post_mlp, gw_out.reshape(D, D), gw1.reshape(4, D, D),
        gw2.reshape(DFF, D))
    dw_out = weight_grad([(mixin.reshape(tokens, D), dmix.reshape(tokens, D))], "grad_w_out", BF16)
    dw1 = weight_grad([(h2.reshape(tokens, D), du.reshape(tokens, DFF))], "grad_w_mlp1", BF16, col_blocks=True)
    dw2 = weight_grad([(act.reshape(tokens, DFF), dm.reshape(tokens, D))], "grad_w_mlp2", BF16)
    dproj, dprojc, drd, dgn, *landed = retention_bwd(pret, pretc, o_all, dmixin, rd, gn, cos, sin,
                                                     early_grads[0](dw_out, dw1, dw2))
    dproj, dprojc, dpat, *early = na_bwd(pna, pnac, bias, dmixin, dproj, dprojc, early_grads[1](landed))
    dw_in = weight_grad([(h.reshape(tokens, D), dproj.reshape(tokens, IN_W)),
                         (hc.reshape(nb * LC, D), dprojc.reshape(nb * LC, IN_W))], "grad_w_in", tn=IN_W // 2, tk=1024)
    grad_x, dmod_a, dg_a, *late = premix_bwd(x, mod3, g_pre_mix, wperm, dproj, dx_tail, early_grads[2](dw_in),
                                             "premix_bwd")
    dmod_c, dg_c = premix_bwd(ctx, mod3, g_pre_mix, wperm, dprojc, None, no_exchange(), "premix_bwd_ctx")
    dmod = jnp.concatenate([jnp.concatenate([dmod_a[:, 0:2], dmod_t[:, 2:6]], axis=1), dmod_c], axis=0)
    last = jnp.pad(jnp.concatenate([drd[:, :, 0].T.reshape(8), loss_t[0, 0:1]]), (0, D - 9)).reshape(1, D)
    small = jnp.concatenate([dg_a[0:1], dg_c[0:1], dg_t[0:3], _pad_row(dgn, 1), dpat.reshape(8, D), last], axis=0)
    return grad_x, late, early, dmod, small


def kernel(x, c, ctx, c_ctx, w_ada, b_ada, g_pre_mix, g_post_mix, g_pre_mlp, g_post_mlp, w_in, ret_decay, ret_gn, na_rpb, w_out, w_mlp1, w_mlp2, loss_target, m_c_ctx, m_w_ada, m_b_ada, m_g_pre_mix, m_g_post_mix, m_g_pre_mlp, m_g_post_mlp, m_w_in, m_ret_decay, m_ret_gn, m_na_rpb, m_w_out, m_w_mlp1, m_w_mlp2, v_c_ctx, v_w_ada, v_b_ada, v_g_pre_mix, v_g_post_mix, v_g_pre_mlp, v_g_post_mlp, v_w_in, v_ret_decay, v_ret_gn, v_na_rpb, v_w_out, v_w_mlp1, v_w_mlp2):
    px, py, pc = _place()
    dev = 4 * px + 2 * py + pc
    chip = 2 * px + py

    def my_half(w2d):
        rows = w2d.shape[0] // 2
        return lax.dynamic_slice_in_dim(w2d, pc * rows, rows, 0)

    halves = [my_half(w[0]).astype(BF16) for w in (w_in, w_out, w_mlp1, w_mlp2)]
    cin, mg, gw_in = prologue(jnp.pad(c, ((0, 6), (0, 0))), c_ctx[None], w_ada[0],
                              lax.dynamic_slice_in_dim(b_ada, chip * 1536, 1536, 1), halves[0])
    wperm = unpack_w_in(gw_in.reshape(4, D, 896))
    mod_all = jnp.concatenate([mg[0], mg[2], mg[4], mg[6]], axis=1)
    mod3 = (jnp.pad(lax.dynamic_slice_in_dim(mod_all, 2 * dev, 2, 0), ((0, 1), (0, 0)))
            + jnp.pad(mod_all[16:17], ((2, 0), (0, 0)))).reshape(3, 6, D)

    place = jnp.stack([pc, chip]).astype(jnp.int32)

    early_names = ["w_out", "w_mlp1", "w_mlp2"]
    early_g8, early_partial = [], []

    def early_a(dw_out, dw1, dw2):
        early_g8[:] = [dw_out.reshape(8, 128, D), dw1.reshape(8, 512, D), dw2.reshape(8, 512, D)]
        return siblings4(early_g8)

    def early_b(landed):
        early_partial[:] = chip_partial(place, early_g8, landed, "rs_chip_sum_early")
        return chips3(early_partial)

    late_partial = []

    def late_c(dw_in):
        g8_in = pack_w_in(dw_in).reshape(8, 512, 896)
        (landed_in,) = sibling_blocks([g8_in], "rs_sibling_w_in")
        late_partial[:] = chip_partial(place, [g8_in], [landed_in], "rs_chip_sum_w_in")
        return chips3(late_partial)

    grad_x, (landed3_in,), early_landed, dmod, small = local_step(
        x, ctx, loss_target, mod3, g_pre_mix, g_post_mix, g_pre_mlp, g_post_mlp, ret_decay[0], ret_gn, na_rpb[0],
        wperm, lambda k: gather8(halves[1:2] if k == 0 else halves[2:4]), (early_a, early_b, late_c))
    early_mine = shard_sum(place, early_partial, early_landed, "rs_shard_sum_early")

    pay = jnp.concatenate([dmod.reshape(18, D), small, jnp.zeros((40 - 18 - SMALL_SUM_ROWS, D), F32)], axis=0)
    *early_theirs, gs = run_hosted(both(siblings(early_mine), gather8([pay])), "rs_halves_early_gather_small")
    gbf = gs[:, 0:12].reshape(16, 6 * D)
    gcf = gs[:, 12:18].reshape(8, 6 * D)
    gw_ada, pc_part = ada_grads(cin, lax.dynamic_slice_in_dim(gbf, chip * 1536, 1536, 1),
                                lax.dynamic_slice_in_dim(gcf, chip * 1536, 1536, 1), w_ada[0])
    (mine_in,) = shard_sum(place, late_partial, [landed3_in], "rs_shard_sum_w_in")
    theirs_in, pcg = run_hosted(both(siblings([mine_in]), gather8([pc_part])), "rs_halves_w_in_gather_c_ctx")

    grouped = adamw_group(
        place,
        [(w_mlp1[0], early_mine[1], early_theirs[1], m_w_mlp1[0], v_w_mlp1[0]),
         (w_mlp2[0], early_mine[2], early_theirs[2], m_w_mlp2[0], v_w_mlp2[0])],
        [(w_ada[0], gw_ada, m_w_ada[0], v_w_ada[0])], no_exchange(), "adamw_group")
    d_ada, m_ada, v_ada = grouped[8:11]
    big = [
        [r[None] for r in adamw_halves(place, w_in[0], mine_in, theirs_in, m_w_in[0], v_w_in[0], "adamw_w_in")],
        [r[None] for r in adamw_halves(place, w_out[0], early_mine[0], early_theirs[0], m_w_out[0], v_w_out[0],
                                       "adamw_w_out")],
        [r[None] for r in grouped[0:4]], [r[None] for r in grouped[4:8]],
    ]

    def rpb_rows(t):
        return _rpb_flat(t[0]).reshape(8, D)

    def decay_row(t):
        return jnp.pad(t.reshape(1, 8), ((0, 0), (0, D - 8)))

    views = [lambda t: t.reshape(1, D), lambda t: t, lambda t: t, lambda t: t, lambda t: t, lambda t: t, lambda t: t,
             rpb_rows, decay_row]
    back = [lambda t: t.reshape(D), lambda t: t, lambda t: t, lambda t: t, lambda t: t, lambda t: t, lambda t: t,
            lambda t: _rpb_flat_t(t)[None], lambda t: t[:, 0:8].reshape(1, 2, 4)]
    small_w = (c_ctx, b_ada, g_pre_mix, g_post_mix, g_pre_mlp, g_post_mlp, ret_gn, na_rpb, ret_decay)
    small_m = (m_c_ctx, m_b_ada, m_g_pre_mix, m_g_post_mix, m_g_pre_mlp, m_g_post_mlp, m_ret_gn, m_na_rpb, m_ret_decay)
    small_v = (v_c_ctx, v_b_ada, v_g_pre_mix, v_g_post_mix, v_g_pre_mlp, v_g_post_mlp, v_ret_gn, v_na_rpb, v_ret_decay)
    *res, loss8 = small_update(gs[:, 18:18 + SMALL_SUM_ROWS], gbf, gcf, pcg[:, 0],
                               [(f(w), f(m), f(v)) for f, w, m, v in zip(views, small_w, small_m, small_v)])

    def leaves(ada, idx):
        s_c, s_b, s_g1, s_g2, s_g3, s_g4, s_gn, s_rpb, s_rd = [back[i](res[4 * i + idx]) for i in range(9)]
        return [s_c, ada[None], s_b, s_g1, s_g2, s_g3, s_g4, big[0][idx], s_rd, s_gn, s_rpb,
                big[1][idx], big[2][idx], big[3][idx]]

    return (loss8[0, 0], grad_x, *leaves(gw_ada, 0), *leaves(d_ada, 1), *leaves(m_ada, 2), *leaves(v_ada, 3))
```

```python
import functools

import jax
import jax.numpy as jnp
from jax import lax
from jax.experimental import pallas as pl
from jax.experimental.pallas import tpu as pltpu

F32, BF16 = jnp.float32, jnp.bfloat16
D = 1024
SEQ = 2048
LC = 256
GW = 64
RH, RD, CH = 4, 128, 128
NPAIR = 4
IN_W = 3584
RET_W = 2048
DFF = 4096
EPS = 1e-6
NEG = -1e30
TN = 256
NCH = SEQ // CH
LR, B1, B2, AEPS, WD, STEP = 0.001, 0.9, 0.999, 1e-08, 0.01, 10
MESH = pl.DeviceIdType.MESH
VMEM_LIMIT = 56 * 1024 * 1024


def _cp(sem=None):
    return pltpu.CompilerParams(dimension_semantics=sem, vmem_limit_bytes=VMEM_LIMIT)


def _nn(a, b):
    return jnp.dot(a.astype(BF16), b.astype(BF16), preferred_element_type=F32)


def _nt(a, b):
    return lax.dot_general(a.astype(BF16), b.astype(BF16), (((1,), (1,)), ((), ())), preferred_element_type=F32)


def _tn(a, b):
    return lax.dot_general(a.astype(BF16), b.astype(BF16), (((0,), (0,)), ((), ())), preferred_element_type=F32)


@jax.custom_vjp
def mm_tn(a, b):
    return _tn(a, b)


mm_tn.defvjp(lambda a, b: (_tn(a, b), (a, b)), lambda r, g: (_nt(r[1], g), _nn(r[0], g)))


def _rms(x):
    return x * lax.rsqrt(jnp.mean(x * x, axis=-1, keepdims=True) + EPS)


def _rms_mod(x, g, sc, sh):
    return (_rms(x) * g) * (1.0 + sc) + sh


def _post_mix(x, mix, gt1, sc2, sh2, g_post_mix, g_pre_mlp):
    x1 = x + gt1 * (_rms(mix) * g_post_mix)
    return x1, _rms_mod(x1, g_pre_mlp, sc2, sh2)


def _head_loss(x1, m, gt2, g_post_mlp, tgt):
    err = x1 + gt2 * (_rms(m) * g_post_mlp) - tgt
    return 0.5 * jnp.sum(jnp.mean(err * err, axis=-1, keepdims=True), axis=0, keepdims=True)


def _ln_gate(o, g, w):
    mu = jnp.mean(o, axis=-1, keepdims=True)
    var = jnp.mean(jnp.square(o - mu), axis=-1, keepdims=True)
    y = (o - mu) * lax.rsqrt(var + EPS)
    return (y * w) * (g * jax.nn.sigmoid(g))


def _swap32(x):
    lane = lax.broadcasted_iota(jnp.int32, x.shape, 1)
    return jnp.where((lane & 32) == 0, pltpu.roll(x, 96, 1), pltpu.roll(x, 32, 1))


def _rope(x, cos, sin):
    return x * cos + _swap32(x) * sin


def _rope_t(g, cos, sin):
    return g * cos + _swap32(g * sin)


def _rope_tables():
    tok = jnp.arange(SEQ)
    pos_r = (tok // GW).astype(F32)
    pos_c = (tok % GW).astype(F32)
    inv = 10000.0 ** (-jnp.arange(32, dtype=F32) / 32)
    ar = pos_r[:, None] * inv[None, :]
    ac = pos_c[:, None] * inv[None, :]
    cos = jnp.concatenate([jnp.cos(ar), jnp.cos(ar), jnp.cos(ac), jnp.cos(ac)], axis=-1)
    sin = jnp.concatenate([-jnp.sin(ar), jnp.sin(ar), -jnp.sin(ac), jnp.sin(ac)], axis=-1)
    return cos, sin


def _chunk_loop(n, body, init, k=4):
    def several(t, carry):
        for i in range(k):
            carry = body(k * t + i, carry)
        return carry

    return lax.fori_loop(0, n // k, several, init)


def _fiota(shape, dim):
    return lax.broadcasted_iota(jnp.int32, shape, dim).astype(F32)


def _ret_state(k, v, s, lg, reverse):
    pos = _fiota((CH, 1), 0)
    b_exp = pos if reverse else (CH - 1.0 - pos)
    return jnp.exp(lg * CH) * s + mm_tn(k * jnp.exp(lg * b_exp), v)


class _Decays:
    def __init__(self, lgs):
        i, j, pos = _fiota((CH, CH), 0), _fiota((CH, CH), 1), _fiota((CH, 1), 0)
        diffs = (i - j, j - i)
        keep = (diffs[0] >= 0, diffs[1] > 0)
        mats = [jnp.where(m, jnp.exp(lg * jnp.where(m, d, 0.0)), 0.0) for lg, d, m in zip(lgs, diffs, keep)]
        self.mask = mats[0] + mats[1]
        self.dmask = [mats[0] * diffs[0], mats[1] * diffs[1]]
        a_exp, b_exp = (pos + 1.0, CH - pos), (CH - 1.0 - pos, pos)
        self.a = [jnp.exp(lg * e) for lg, e in zip(lgs, a_exp)]
        self.b = [jnp.exp(lg * e) for lg, e in zip(lgs, b_exp)]
        self.da = [a * e for a, e in zip(self.a, a_exp)]
        self.db = [b * e for b, e in zip(self.b, b_exp)]
        self.g = [jnp.exp(lg * CH) for lg in lgs]


def _both(x, w):
    return jnp.concatenate([x * w[0], x * w[1]], axis=1)


def _total(x):
    return jnp.sum(jnp.sum(x, axis=1, keepdims=True), axis=0, keepdims=True)


def _state_pass(dec, init, k_s, v_of, st_s):
    def step(t, carry):
        out = []
        for d, s in enumerate(carry):
            n = (NCH - 1 - t) if d else t
            sl = pl.ds(pl.multiple_of(n * CH, CH), CH)
            st_s[n, d * RD:(d + 1) * RD, :] = s
            out.append(dec.g[d] * s + _tn(k_s[sl, :] * dec.b[d], v_of(sl)))
        return tuple(out)

    _chunk_loop(NCH, step, tuple(init))


def premix_proj(xin, mod3, g_pre, wperm, is_ctx, name):
    nb, length, _ = xin.shape
    tn = min(2 * TN, length)

    def body(x_ref, mod_ref, g_ref, w_ref, h_ref, pret_ref, pna_ref):
        h = _rms_mod(x_ref[...], g_ref[...], mod_ref[1:2, :], mod_ref[0:1, :])
        hb = h.astype(BF16)
        h_ref[...] = hb
        pret_ref[...] = jnp.dot(hb, w_ref[:, :RET_W], preferred_element_type=F32)
        pna_ref[...] = jnp.dot(hb, w_ref[:, RET_W:], preferred_element_type=F32).astype(BF16)

    return pl.pallas_call(
        body, name=name, grid=(nb, length // tn),
        in_specs=[
            pl.BlockSpec((None, tn, D), lambda b, t: (b, t, 0)),
            pl.BlockSpec((None, 6, D), (lambda b, t: (2, 0, 0)) if is_ctx else (lambda b, t: (b, 0, 0))),
            pl.BlockSpec((1, D), lambda b, t: (0, 0)),
            pl.BlockSpec((D, IN_W), lambda b, t: (0, 0), pipeline_mode=pl.Buffered(1)),
        ],
        out_specs=[
            pl.BlockSpec((None, tn, D), lambda b, t: (b, t, 0)),
            pl.BlockSpec((None, tn, RET_W), lambda b, t: (b, t, 0)),
            pl.BlockSpec((None, tn, IN_W - RET_W), lambda b, t: (b, t, 0)),
        ],
        out_shape=[
            jax.ShapeDtypeStruct((nb, length, D), BF16),
            jax.ShapeDtypeStruct((nb, length, RET_W), F32),
            jax.ShapeDtypeStruct((nb, length, IN_W - RET_W), BF16),
        ],
        compiler_params=_cp(("arbitrary", "arbitrary")),
    )(xin, mod3, g_pre, wperm)


def premix_bwd(xin, mod3, g_pre, wperm, dproj, dx_tail, hosted, name):
    nb, length, _ = xin.shape
    tn = min(2 * TN, length)
    is_ctx = dx_tail is None

    def body(*refs):
        own_in, h_in, own_out, h_out, _, h_sems = hosted.split(refs, 5 if is_ctx else 6, 2 if is_ctx else 3)
        if is_ctx:
            (x_ref, mod_ref, g_ref, w_ref, dp_ref), (dmod_ref, dg_ref) = own_in, own_out
        else:
            (x_ref, mod_ref, g_ref, w_ref, dp_ref, dxt_ref), (dx_ref, dmod_ref, dg_ref) = own_in, own_out
        b, t = pl.program_id(0), pl.program_id(1)
        grid_step = b * (length // tn) + t

        @pl.when(grid_step == 0)
        def _():
            hosted.start(h_in, h_out, h_sems)

        @pl.when(grid_step == nb * (length // tn) - 1)
        def _():
            hosted.finish(h_in, h_out, h_sems)

        dh = lax.dot_general(dp_ref[...], w_ref[...], (((1,), (1,)), ((), ())), preferred_element_type=F32)
        _, vjp = jax.vjp(_rms_mod, x_ref[...], g_ref[...], mod_ref[1:2, :], mod_ref[0:1, :])
        dx, dg, dsc, dsh = vjp(dh)
        if not is_ctx:
            dx_ref[...] = dx + dxt_ref[...]

        @pl.when((t == 0) & ((b == 0) if is_ctx else True))
        def _():
            dmod_ref[...] = jnp.zeros_like(dmod_ref)

        @pl.when((t == 0) & (b == 0))
        def _():
            dg_ref[...] = jnp.zeros_like(dg_ref)

        dmod_ref[0:1, :] += dsh
        dmod_ref[1:2, :] += dsc
        dg_ref[0:1, :] += dg

    tok = lambda b, t: (b, t, 0)
    in_specs = [
        pl.BlockSpec((None, tn, D), tok),
        pl.BlockSpec((None, 6, D), (lambda b, t: (2, 0, 0)) if is_ctx else (lambda b, t: (b, 0, 0))),
        pl.BlockSpec((1, D), lambda b, t: (0, 0)),
        pl.BlockSpec((D, IN_W), lambda b, t: (0, 0), pipeline_mode=pl.Buffered(1)),
        pl.BlockSpec((None, tn, IN_W), tok),
    ]
    args = [xin, mod3, g_pre, wperm, dproj]
    out_specs = [
        pl.BlockSpec((None, 6, D), (lambda b, t: (0, 0, 0)) if is_ctx else (lambda b, t: (b, 0, 0))),
        pl.BlockSpec((8, D), lambda b, t: (0, 0)),
    ]
    out_shape = [jax.ShapeDtypeStruct((1 if is_ctx else nb, 6, D), F32), jax.ShapeDtypeStruct((8, D), F32)]
    if not is_ctx:
        in_specs.append(pl.BlockSpec((None, tn, D), tok))
        args.append(dx_tail)
        out_specs.insert(0, pl.BlockSpec((None, tn, D), tok))
        out_shape.insert(0, jax.ShapeDtypeStruct((nb, length, D), F32))
    h_in_specs, h_out_specs = hosted.specs()
    return pl.pallas_call(
        body, name=name, grid=(nb, length // tn), in_specs=in_specs + h_in_specs, out_specs=out_specs + h_out_specs,
        out_shape=out_shape + hosted.out_shape, scratch_shapes=hosted.scratch,
        compiler_params=_cp(("arbitrary", "arbitrary")),
    )(*args, *hosted.args)


def _ret_specs(order):
    def im(f):
        return lambda *g: f(*order(*g))
    return dict(
        pret=pl.BlockSpec((None, SEQ, 512), im(lambda b, h: (b, 0, h))),
        pretc=pl.BlockSpec((None, LC, 512), im(lambda b, h: (b, 0, h))),
        rd=pl.BlockSpec((None, 2, 1), im(lambda b, h: (h, 0, 0))),
        gn=pl.BlockSpec((None, 1, RD), im(lambda b, h: (h, 0, 0))),
        tab=pl.BlockSpec((SEQ, RD), im(lambda b, h: (0, 0))),
        head=pl.BlockSpec((None, SEQ, RD), im(lambda b, h: (b, 0, h))),
    )


def retention_fwd(pret, pretc, rd, gn, cos, sin, hosted):
    nb = pret.shape[0]
    sp = _ret_specs(lambda b, h: (b, h))

    def body(*refs):
        own_in, h_in, own_out, h_out, own_scr, h_sems = hosted.split(refs, 6, 2)
        p_ref, pc_ref, rd_ref, gn_ref, cos_ref, sin_ref = own_in
        (o_ref, mix_ref), (q_s, k_s, o_s, st_s) = own_out, own_scr
        grid_step = pl.program_id(0) * RH + pl.program_id(1)

        @pl.when(grid_step == 0)
        def _():
            hosted.start(h_in, h_out, h_sems)

        cos_v, sin_v = cos_ref[...], sin_ref[...]
        q_s[...] = _rope(p_ref[:, 0:128], cos_v, sin_v) * (RD ** -0.5)
        k_s[...] = _rope(p_ref[:, 128:256], cos_v, sin_v)
        lgs, init = [], []
        for rev in (False, True):
            lg = jax.nn.log_sigmoid(rd_ref[int(rev):int(rev) + 1, :])
            s = jnp.zeros((RD, RD), F32)
            for n in ((1, 0) if rev else (0, 1)):
                s = _ret_state(pc_ref[n * CH:(n + 1) * CH, 128:256], pc_ref[n * CH:(n + 1) * CH, 256:384], s, lg, rev)
            lgs.append(lg)
            init.append(s)

        dec = _Decays(lgs)
        _state_pass(dec, init, k_s, lambda sl: p_ref[sl, 256:384], st_s)

        def chunk(n, carry):
            sl = pl.ds(pl.multiple_of(n * CH, CH), CH)
            q = q_s[sl, :]
            o_s[sl, :] = (_nn(_nt(q, k_s[sl, :]) * dec.mask, p_ref[sl, 256:384]) + _nn(_both(q, dec.a), st_s[n]))
            return carry

        _chunk_loop(NCH, chunk, 0)
        o = o_s[...]
        o_ref[...] = o
        mix_ref[...] = _ln_gate(o, p_ref[:, 384:512], gn_ref[...]).astype(BF16)

        @pl.when(grid_step == nb * RH - 1)
        def _():
            hosted.finish(h_in, h_out, h_sems)

    h_in_specs, h_out_specs = hosted.specs()
    return pl.pallas_call(
        body, name="retention_fwd", grid=(nb, RH),
        in_specs=[sp["pret"], sp["pretc"], sp["rd"], sp["gn"], sp["tab"], sp["tab"]] + h_in_specs,
        out_specs=[sp["head"], sp["head"]] + h_out_specs,
        out_shape=[jax.ShapeDtypeStruct((nb, SEQ, RH * RD), F32), jax.ShapeDtypeStruct((nb, SEQ, D), BF16)]
        + hosted.out_shape,
        scratch_shapes=[pltpu.VMEM((SEQ, RD), F32)] * 3 + [pltpu.VMEM((NCH, 2 * RD, RD), F32)] + hosted.scratch,
        compiler_params=_cp(("arbitrary", "arbitrary")),
    )(pret, pretc, rd, gn, cos, sin, *hosted.args)


def retention_bwd(pret, pretc, o_all, dmixin, rd, gn, cos, sin, hosted):
    nb = pret.shape[0]
    sp = _ret_specs(lambda h, b: (b, h))

    def body(*refs):
        own_in, h_in, own_out, h_out, own_scr, h_sems = hosted.split(refs, 8, 4)
        p_ref, pc_ref, o_ref, dmix_ref, rd_ref, gn_ref, cos_ref, sin_ref = own_in
        dp_ref, dpc_ref, drd_ref, dgn_ref = own_out
        q_s, k_s, do_s, dq_s, dk_s, dv_s, st_s, gst_s = own_scr
        b = pl.program_id(1)
        grid_step = pl.program_id(0) * nb + b

        @pl.when(grid_step == 0)
        def _():
            hosted.start(h_in, h_out, h_sems)

        cos_v, sin_v = cos_ref[...], sin_ref[...]
        q_s[...] = _rope(p_ref[:, 0:128], cos_v, sin_v) * (RD ** -0.5)
        k_s[...] = _rope(p_ref[:, 128:256], cos_v, sin_v)
        _, gate_vjp = jax.vjp(_ln_gate, o_ref[...], p_ref[:, 384:512], gn_ref[...])
        do, dg, dgn = gate_vjp(dmix_ref[...].astype(F32))
        do_s[...] = do
        dp_ref[:, 384:512] = dg.astype(BF16)

        @pl.when(b == 0)
        def _():
            drd_ref[...] = jnp.zeros_like(drd_ref)
            dgn_ref[...] = jnp.zeros_like(dgn_ref)

        dgn_ref[...] += dgn
        kcs = [pc_ref[n * CH:(n + 1) * CH, 128:256] for n in (0, 1)]
        vcs = [pc_ref[n * CH:(n + 1) * CH, 256:384] for n in (0, 1)]
        dirs = []
        init = []
        for rev in (False, True):
            rdv = rd_ref[int(rev):int(rev) + 1, :]
            lg = jax.nn.log_sigmoid(rdv)
            order_c = (1, 0) if rev else (0, 1)
            s = jnp.zeros((RD, RD), F32)
            ctx_states = []
            for n in order_c:
                ctx_states.append(s)
                s = _ret_state(kcs[n], vcs[n], s, lg, rev)
            dirs.append((rev, order_c, lg, rdv, ctx_states))
            init.append(s)
        dec = _Decays([lg for _, _, lg, _, _ in dirs])

        def v_of(sl):
            return p_ref[sl, 256:384]

        _state_pass(dec, init, k_s, v_of, st_s)
        zeros = jnp.zeros((CH, RD), F32)

        def scores_back(n, carry):
            dmask_sum, da_f, da_b = carry
            sl = pl.ds(pl.multiple_of(n * CH, CH), CH)
            q, k, v, do = q_s[sl, :], k_s[sl, :], v_of(sl), do_s[sl, :]
            scores = _nt(q, k)
            d_att = _nt(do, v)
            d_scores = d_att * dec.mask
            d_qa = _nt(do, st_s[n])
            d_qf, d_qb = d_qa[:, 0:RD], d_qa[:, RD:2 * RD]
            dq_s[sl, :] = _nn(d_scores, k) + d_qf * dec.a[0] + d_qb * dec.a[1]
            dk_s[sl, :] = _tn(d_scores, q)
            dv_s[sl, :] = _tn(scores * dec.mask, do)
            gst_s[n] = _tn(_both(q, dec.a), do)
            return dmask_sum + d_att * scores, da_f + d_qf * q, da_b + d_qb * q

        dmask_sum, da_f, da_b = _chunk_loop(NCH, scores_back, (zeros, zeros, zeros))

        def state_back(t, carry):
            out = []
            for d, r in enumerate(carry):
                n = t if d else (NCH - 1 - t)
                rows = slice(d * RD, (d + 1) * RD)
                own = gst_s[n, rows, :]
                gst_s[n, rows, :] = r
                out.append(own + dec.g[d] * r)
            return tuple(out)

        d_states = _chunk_loop(NCH, state_back, (zeros, zeros))

        def updates_back(n, carry):
            db_f, db_b, dg_f, dg_b = carry
            sl = pl.ds(pl.multiple_of(n * CH, CH), CH)
            k, r, s = k_s[sl, :], gst_s[n], st_s[n]
            d_kw = _nt(v_of(sl), r)
            d_kf, d_kb = d_kw[:, 0:RD], d_kw[:, RD:2 * RD]
            dk_s[sl, :] += d_kf * dec.b[0] + d_kb * dec.b[1]
            dv_s[sl, :] += _nn(_both(k, dec.b), r)
            return (db_f + d_kf * k, db_b + d_kb * k, dg_f + r[0:RD, :] * s[0:RD, :],
                    dg_b + r[RD:2 * RD, :] * s[RD:2 * RD, :])

        db_dg = _chunk_loop(NCH, updates_back, (zeros, zeros, zeros, zeros))
        dkc = [None, None]
        dvc = [None, None]
        for d, ((rev, order_c, lg, rdv, ctx_states), ds) in enumerate(zip(dirs, d_states)):
            dlg = (_total(dmask_sum * dec.dmask[d]) + _total((da_f, da_b)[d] * dec.da[d])
                   + _total(db_dg[d] * dec.db[d]) + CH * dec.g[d] * _total(db_dg[2 + d]))
            for idx in (1, 0):
                n = order_c[idx]
                _, vjp = jax.vjp(functools.partial(_ret_state, reverse=rev), kcs[n], vcs[n], ctx_states[idx], lg)
                dk_c, dv_c, ds, dl = vjp(ds)
                dlg = dlg + dl
                dkc[n] = dk_c if dkc[n] is None else dkc[n] + dk_c
                dvc[n] = dv_c if dvc[n] is None else dvc[n] + dv_c
            drd_ref[int(rev):int(rev) + 1, :] += dlg * jax.nn.sigmoid(-rdv)
        dp_ref[:, 0:128] = _rope_t(dq_s[...] * (RD ** -0.5), cos_v, sin_v).astype(BF16)
        dp_ref[:, 128:256] = _rope_t(dk_s[...], cos_v, sin_v).astype(BF16)
        dp_ref[:, 256:384] = dv_s[...].astype(BF16)
        zero = jnp.zeros((CH, RD), BF16)
        for n in (0, 1):
            rows = slice(n * CH, (n + 1) * CH)
            dpc_ref[rows, 0:128] = zero
            dpc_ref[rows, 128:256] = dkc[n].astype(BF16)
            dpc_ref[rows, 256:384] = dvc[n].astype(BF16)
            dpc_ref[rows, 384:512] = zero

        @pl.when(grid_step == RH * nb - 1)
        def _():
            hosted.finish(h_in, h_out, h_sems)

    h_in_specs, h_out_specs = hosted.specs()
    return pl.pallas_call(
        body, name="retention_bwd", grid=(RH, nb),
        in_specs=[sp["pret"], sp["pretc"], sp["head"], sp["head"], sp["rd"], sp["gn"], sp["tab"], sp["tab"]]
        + h_in_specs,
        out_specs=[
            pl.BlockSpec((None, SEQ, 512), lambda h, b: (b, 0, h)),
            pl.BlockSpec((None, LC, 512), lambda h, b: (b, 0, h)),
            pl.BlockSpec((None, 2, 1), lambda h, b: (h, 0, 0)),
            pl.BlockSpec((None, 1, RD), lambda h, b: (h, 0, 0)),
        ] + h_out_specs,
        out_shape=[
            jax.ShapeDtypeStruct((nb, SEQ, IN_W), BF16),
            jax.ShapeDtypeStruct((nb, LC, IN_W), BF16),
            jax.ShapeDtypeStruct((RH, 2, 1), F32),
            jax.ShapeDtypeStruct((RH, 1, RD), F32),
        ] + hosted.out_shape,
        scratch_shapes=[pltpu.VMEM((SEQ, RD), F32)] * 6 + [pltpu.VMEM((NCH, 2 * RD, RD), F32)] * 2 + hosted.scratch,
        compiler_params=_cp(("arbitrary", "arbitrary")),
    )(pret, pretc, o_all, dmixin, rd, gn, cos, sin, *hosted.args)


def _rpb_flat(rpb):
    return jnp.pad(rpb, ((0, 0), (0, 1), (0, 33))).reshape(NPAIR, 2, 1, 1024)


def _rpb_flat_t(dflat):
    return dflat.reshape(8, 16, 64)[:, :15, :31]


def _barrel(x, left):
    row = lax.broadcasted_iota(jnp.int32, x.shape, 0)
    n = x.shape[1]
    for bit in range(6):
        s = 1 << bit
        x = jnp.where(((row >> bit) & 1) == 1, pltpu.roll(x, (n - s) if left else s, 1), x)
    return x


NA_TILE_ROWS, NA_BAND_ROWS = 4, 12
NA_Q, NA_K = NA_TILE_ROWS * GW, NA_BAND_ROWS * GW
NA_TILES = SEQ // NA_Q


def _band_start(r0):
    return min(max(r0 - 4, 0), 32 - NA_BAND_ROWS)


def _tile_layout(t):
    rows = range(t * NA_TILE_ROWS, (t + 1) * NA_TILE_ROWS)
    return tuple((r if r < 4 else (r - 24 if r > 28 else 4), min(max(r - 4, 0), 24) - _band_start(rows[0]))
                 for r in rows)


NA_CLASSES = sorted(set(_tile_layout(t) for t in range(NA_TILES)))


def _tile_rows(cls):
    return NA_CLASSES[cls]


def _na_tile(t):
    start = jnp.clip(NA_TILE_ROWS * t - 4, 0, 32 - NA_BAND_ROWS)
    cls = 0
    for tile in range(NA_TILES):
        cls = jnp.where(t == tile, NA_CLASSES.index(_tile_layout(tile)), cls)
    return pl.ds(pl.multiple_of(t * NA_Q, NA_Q), NA_Q), pl.ds(pl.multiple_of(start * GW, NA_Q), NA_K), cls


def _na_probs(qst, kb, kc, bias):
    s_loc = _nt(qst, kb) + bias
    s_ctx = _nt(qst, kc)
    m = jnp.maximum(jnp.max(s_loc, axis=1, keepdims=True), jnp.max(s_ctx, axis=1, keepdims=True))
    e_loc, e_ctx = jnp.exp(s_loc - m), jnp.exp(s_ctx - m)
    den = jnp.sum(e_loc, axis=1, keepdims=True) + jnp.sum(e_ctx, axis=1, keepdims=True)
    return e_loc / den, e_ctx / den


def _stack_heads(t):
    lane = lax.broadcasted_iota(jnp.int32, t.shape, 1)
    zero = jnp.zeros_like(t)
    return jnp.concatenate([jnp.where(lane < 64, t, zero), jnp.where(lane >= 64, t, zero)], axis=0)


def _unstack_heads(t):
    n = t.shape[0] // 2
    lane = lax.broadcasted_iota(jnp.int32, (n, 128), 1)
    return jnp.where(lane < 64, t[:n], t[n:])


def na_bias_table(flat):
    def body(flat_ref, out_ref):
        qc = lax.broadcasted_iota(jnp.int32, (GW, 512), 0)
        kc = lax.broadcasted_iota(jnp.int32, (GW, 512), 1) & 63
        start = jnp.clip(qc - 8, 0, GW - 16)
        window = (kc >= start) & (kc < start + 16)
        fill = jnp.full((GW, NA_K - 512), NEG, F32)
        for hh in (0, 1):
            skew = _barrel(pltpu.roll(jnp.broadcast_to(flat_ref[hh], (GW, 1024)), 1024 - 15, 1), left=False)
            by_class = [jnp.where(window, (skew if rc == 7 else pltpu.roll(skew, (9 + rc) * 64, 1))[:, 0:512], NEG)
                        for rc in range(8)]
            for cls in range(len(NA_CLASSES)):
                for qr, (rc, off) in enumerate(_tile_rows(cls)):
                    w = jnp.concatenate([by_class[rc], fill], axis=1)
                    rows = slice(hh * NA_Q + qr * GW, hh * NA_Q + (qr + 1) * GW)
                    out_ref[cls, rows, :] = pltpu.roll(w, off * GW, 1) if off else w

    return pl.pallas_call(
        body, name="na_bias_table", grid=(NPAIR,),
        in_specs=[pl.BlockSpec((None, 2, 1, 1024), lambda p: (p, 0, 0, 0))],
        out_specs=pl.BlockSpec((None, len(NA_CLASSES), 2 * NA_Q, NA_K), lambda p: (p, 0, 0, 0)),
        out_shape=jax.ShapeDtypeStruct((NPAIR, len(NA_CLASSES), 2 * NA_Q, NA_K), F32),
    )(flat)


def na_fwd(pna, pnac, bias, mixin, hosted):
    nb = pna.shape[0]

    def body(*refs):
        (p_ref, pc_ref, bias_ref, _), h_in, (out_ref,), h_out, _, h_sems = hosted.split(refs, 4, 1)
        grid_step = pl.program_id(0) * nb + pl.program_id(1)

        @pl.when(grid_step == 0)
        def _():
            hosted.start(h_in, h_out, h_sems)

        kc, vc = pc_ref[:, 128:256], pc_ref[:, 256:384]

        def tile(t, carry):
            qsl, bsl, cls = _na_tile(t)
            kb, vb = p_ref[bsl, 128:256], p_ref[bsl, 256:384]
            p_loc, p_ctx = _na_probs(_stack_heads(p_ref[qsl, 0:128] * 0.125), kb, kc, bias_ref[cls])
            out_ref[qsl, :] = _unstack_heads(_nn(p_loc, vb) + _nn(p_ctx, vc)).astype(BF16)
            return carry

        lax.fori_loop(0, NA_TILES, tile, 0, unroll=4)

        @pl.when(grid_step == NPAIR * nb - 1)
        def _():
            hosted.finish(h_in, h_out, h_sems)

    h_in_specs, h_out_specs = hosted.specs()
    return pl.pallas_call(
        body, name="na_fwd", grid=(NPAIR, nb),
        in_specs=[
            pl.BlockSpec((None, SEQ, 384), lambda p, b: (b, 0, p)),
            pl.BlockSpec((None, LC, 384), lambda p, b: (b, 0, p)),
            pl.BlockSpec((None, len(NA_CLASSES), 2 * NA_Q, NA_K), lambda p, b: (p, 0, 0, 0)),
            pl.BlockSpec(memory_space=pl.ANY),
        ] + h_in_specs,
        out_specs=[pl.BlockSpec((None, SEQ, 128), lambda p, b: (b, 0, 4 + p))] + h_out_specs,
        out_shape=[jax.ShapeDtypeStruct((nb, SEQ, D), BF16)] + hosted.out_shape,
        input_output_aliases={3: 0},
        scratch_shapes=hosted.scratch,
        compiler_params=_cp(("arbitrary", "arbitrary")),
    )(pna, pnac, bias, mixin, *hosted.args)


def na_bwd(pna, pnac, bias, dmixin, dproj, dprojc, hosted):
    nb = pna.shape[0]

    def body(*refs):
        own_in, h_in, own_out, h_out, own_scr, h_sems = hosted.split(refs, 6, 3)
        p_ref, pc_ref, bias_ref, dmix_ref = own_in[:4]
        dp_ref, dpc_ref, dpat_ref = own_out
        dbias_s, dk_s, dv_s, dkc_s, dvc_s, res_s, resc_s = own_scr
        b, part = pl.program_id(1), pl.program_id(2)
        grid_step = (pl.program_id(0) * nb + b) * 3 + part

        @pl.when(grid_step == 0)
        def _():
            hosted.start(h_in, h_out, h_sems)

        @pl.when(grid_step == NPAIR * nb * 3 - 1)
        def _():
            hosted.finish(h_in, h_out, h_sems)

        @pl.when(part == 0)
        def _():
            @pl.when(b == 0)
            def _():
                dbias_s[...] = jnp.zeros_like(dbias_s)

            dk_s[...] = jnp.zeros_like(dk_s)
            dv_s[...] = jnp.zeros_like(dv_s)
            dkc_s[...] = jnp.zeros_like(dkc_s)
            dvc_s[...] = jnp.zeros_like(dvc_s)
            kc, vc = pc_ref[:, 128:256], pc_ref[:, 256:384]

            def tile(t, carry):
                qsl, bsl, cls = _na_tile(t)
                kb, vb = p_ref[bsl, 128:256], p_ref[bsl, 256:384]
                qst, dost = _stack_heads(p_ref[qsl, 0:128] * 0.125), _stack_heads(dmix_ref[qsl, :])
                p_loc, p_ctx = _na_probs(qst, kb, kc, bias_ref[cls])
                dp_loc, dp_ctx = _nt(dost, vb), _nt(dost, vc)
                delta = (jnp.sum(p_loc * dp_loc, axis=1, keepdims=True)
                         + jnp.sum(p_ctx * dp_ctx, axis=1, keepdims=True))
                ds_loc, ds_ctx = p_loc * (dp_loc - delta), p_ctx * (dp_ctx - delta)
                dbias_s[cls] += ds_loc
                res_s[0, qsl, :] = _unstack_heads((_nn(ds_loc, kb) + _nn(ds_ctx, kc)) * 0.125).astype(BF16)
                dk_s[bsl, :] += _tn(ds_loc, qst)
                dv_s[bsl, :] += _tn(p_loc, dost)
                dkc_s[...] += _tn(ds_ctx, qst)
                dvc_s[...] += _tn(p_ctx, dost)
                return carry

            lax.fori_loop(0, NA_TILES, tile, 0, unroll=2)
            res_s[1] = dk_s[...].astype(BF16)
            res_s[2] = dv_s[...].astype(BF16)
            resc_s[0] = jnp.zeros((LC, 128), BF16)
            resc_s[1] = dkc_s[...].astype(BF16)
            resc_s[2] = dvc_s[...].astype(BF16)

            @pl.when(b == nb - 1)
            def _():
                for hh in (0, 1):
                    by_class = [None] * 8
                    for cls in range(len(NA_CLASSES)):
                        for qr, (rc, off) in enumerate(_tile_rows(cls)):
                            w = dbias_s[cls, hh * NA_Q + qr * GW:hh * NA_Q + (qr + 1) * GW, :]
                            w = (pltpu.roll(w, NA_K - off * GW, 1) if off else w)[:, 0:512]
                            by_class[rc] = w if by_class[rc] is None else by_class[rc] + w
                    skew = jnp.zeros((GW, 1024), F32)
                    for rc in range(8):
                        w = jnp.concatenate([by_class[rc], jnp.zeros((GW, 512), F32)], axis=1)
                        skew = skew + (w if rc == 7 else pltpu.roll(w, (7 - rc) * 64, 1))
                    dpat_ref[hh] = jnp.sum(pltpu.roll(_barrel(skew, left=True), 15, 1), axis=0, keepdims=True)

        dp_ref[...] = res_s[part]
        dpc_ref[...] = resc_s[part]

    h_in_specs, h_out_specs = hosted.specs()
    return pl.pallas_call(
        body, name="na_bwd", grid=(NPAIR, nb, 3),
        in_specs=[
            pl.BlockSpec((None, SEQ, 384), lambda p, b, s: (b, 0, p)),
            pl.BlockSpec((None, LC, 384), lambda p, b, s: (b, 0, p)),
            pl.BlockSpec((None, len(NA_CLASSES), 2 * NA_Q, NA_K), lambda p, b, s: (p, 0, 0, 0)),
            pl.BlockSpec((None, SEQ, 128), lambda p, b, s: (b, 0, 4 + p)),
            pl.BlockSpec(memory_space=pl.ANY),
            pl.BlockSpec(memory_space=pl.ANY),
        ] + h_in_specs,
        out_specs=[
            pl.BlockSpec((None, SEQ, 128), lambda p, b, s: (b, 0, 16 + 3 * p + s)),
            pl.BlockSpec((None, LC, 128), lambda p, b, s: (b, 0, 16 + 3 * p + s)),
            pl.BlockSpec((None, 2, 1, 1024), lambda p, b, s: (p, 0, 0, 0)),
        ] + h_out_specs,
        out_shape=[
            jax.ShapeDtypeStruct((nb, SEQ, IN_W), BF16),
            jax.ShapeDtypeStruct((nb, LC, IN_W), BF16),
            jax.ShapeDtypeStruct((NPAIR, 2, 1, 1024), F32),
        ] + hosted.out_shape,
        input_output_aliases={4: 0, 5: 1},
        scratch_shapes=[
            pltpu.VMEM((len(NA_CLASSES), 2 * NA_Q, NA_K), F32),
            pltpu.VMEM((SEQ, 128), F32), pltpu.VMEM((SEQ, 128), F32),
            pltpu.VMEM((LC, 128), F32), pltpu.VMEM((LC, 128), F32),
            pltpu.VMEM((3, SEQ, 128), BF16), pltpu.VMEM((3, LC, 128), BF16),
        ] + hosted.scratch,
        compiler_params=_cp(("arbitrary", "arbitrary", "arbitrary")),
    )(pna, pnac, bias, dmixin, dproj, dprojc, *hosted.args)


def tail_fwd_bwd(x, mixin, tgt, mod3, g_post_mix, g_pre_mlp, g_post_mlp, wout, w1, w2):
    nb = x.shape[0]

    def body(x_ref, mi_ref, tgt_ref, mod_ref, gpm_ref, gpl_ref, gpo_ref, wo_ref, w1_ref, w2_ref,
             dx_ref, dmix_ref, h2_ref, du_ref, a_ref, dm_ref, dmi_ref, dmod_ref, dg_ref, loss_ref):
        b, t = pl.program_id(0), pl.program_id(1)
        gt1, sh2, sc2, gt2 = mod_ref[2:3, :], mod_ref[3:4, :], mod_ref[4:5, :], mod_ref[5:6, :]
        mix = jnp.dot(mi_ref[...], wo_ref[...], preferred_element_type=F32)
        (x1, h2), vjp_a = jax.vjp(_post_mix, x_ref[...], mix, gt1, sc2, sh2, gpm_ref[...], gpl_ref[...])
        h2b = h2.astype(BF16)
        h2_ref[...] = h2b
        m = jnp.zeros((TN, D), F32)
        relus = []
        for j in range(4):
            cols = slice(j * D, (j + 1) * D)
            r = jnp.maximum(jnp.dot(h2b, w1_ref[j], preferred_element_type=F32), 0.0)
            ab = (r * r).astype(BF16)
            a_ref[:, cols] = ab
            m = m + jnp.dot(ab, w2_ref[cols, :], preferred_element_type=F32)
            relus.append(r)
        loss, vjp_b = jax.vjp(_head_loss, x1, m, gt2, gpo_ref[...], tgt_ref[...])
        dx1, dm, dgt2, dgpo, _ = vjp_b(jnp.ones((1, 1), F32))
        dmb = dm.astype(BF16)
        dm_ref[...] = dmb
        dh2 = jnp.zeros((TN, D), F32)
        for j in range(4):
            cols = slice(j * D, (j + 1) * D)
            da = lax.dot_general(dmb, w2_ref[cols, :], (((1,), (1,)), ((), ())), preferred_element_type=F32)
            dub = (da * (2.0 * relus[j])).astype(BF16)
            du_ref[:, cols] = dub
            dh2 = dh2 + lax.dot_general(dub, w1_ref[j], (((1,), (1,)), ((), ())), preferred_element_type=F32)
        dx, dmix, dgt1, dsc2, dsh2, dgpm, dgpl = vjp_a((dx1, dh2))
        dx_ref[...] = dx
        dmixb = dmix.astype(BF16)
        dmix_ref[...] = dmixb
        dmi_ref[...] = lax.dot_general(dmixb, wo_ref[...], (((1,), (1,)), ((), ())),
                                       preferred_element_type=F32).astype(BF16)

        @pl.when(t == 0)
        def _():
            dmod_ref[...] = jnp.zeros_like(dmod_ref)

        @pl.when((t == 0) & (b == 0))
        def _():
            dg_ref[...] = jnp.zeros_like(dg_ref)
            loss_ref[...] = jnp.zeros_like(loss_ref)

        dmod_ref[2:3, :] += dgt1
        dmod_ref[3:4, :] += dsh2
        dmod_ref[4:5, :] += dsc2
        dmod_ref[5:6, :] += dgt2
        dg_ref[0:1, :] += dgpm
        dg_ref[1:2, :] += dgpl
        dg_ref[2:3, :] += dgpo
        loss_ref[...] += jnp.broadcast_to(loss, loss_ref.shape)

    tok = lambda b, t: (b, t, 0)
    const = lambda b, t: (0, 0)
    vec = pl.BlockSpec((1, D), const)
    return pl.pallas_call(
        body, name="tail_fwd_bwd", grid=(nb, SEQ // TN),
        in_specs=[
            pl.BlockSpec((None, TN, D), tok), pl.BlockSpec((None, TN, D), tok), pl.BlockSpec((None, TN, D), tok),
            pl.BlockSpec((None, 6, D), lambda b, t: (b, 0, 0)), vec, vec, vec,
            pl.BlockSpec((D, D), const, pipeline_mode=pl.Buffered(1)),
            pl.BlockSpec((4, D, D), lambda b, t: (0, 0, 0), pipeline_mode=pl.Buffered(1)),
            pl.BlockSpec((DFF, D), const, pipeline_mode=pl.Buffered(1)),
        ],
        out_specs=[
            pl.BlockSpec((None, TN, D), tok), pl.BlockSpec((None, TN, D), tok), pl.BlockSpec((None, TN, D), tok),
            pl.BlockSpec((None, TN, DFF), tok), pl.BlockSpec((None, TN, DFF), tok), pl.BlockSpec((None, TN, D), tok),
            pl.BlockSpec((None, TN, D), tok),
            pl.BlockSpec((None, 6, D), lambda b, t: (b, 0, 0)),
            pl.BlockSpec((8, D), const), pl.BlockSpec((8, 128), const),
        ],
        out_shape=[
            jax.ShapeDtypeStruct((nb, SEQ, D), F32), jax.ShapeDtypeStruct((nb, SEQ, D), BF16),
            jax.ShapeDtypeStruct((nb, SEQ, D), BF16), jax.ShapeDtypeStruct((nb, SEQ, DFF), BF16),
            jax.ShapeDtypeStruct((nb, SEQ, DFF), BF16), jax.ShapeDtypeStruct((nb, SEQ, D), BF16),
            jax.ShapeDtypeStruct((nb, SEQ, D), BF16),
            jax.ShapeDtypeStruct((nb, 6, D), F32), jax.ShapeDtypeStruct((8, D), F32),
            jax.ShapeDtypeStruct((8, 128), F32),
        ],
        compiler_params=_cp(("arbitrary", "arbitrary")),
    )(x, mixin, tgt, mod3, g_post_mix, g_pre_mlp, g_post_mlp, wout, w1, w2)


def weight_grad(pairs, name, out_dtype=F32, col_blocks=False, tm=1024, tn=1024, tk=2048):
    m, n = pairs[0][0].shape[1], pairs[0][1].shape[1]
    tn = min(tn, n)
    tks = [min(tk, xa.shape[0]) for xa, _ in pairs]
    steps = [xa.shape[0] // t for (xa, _), t in zip(pairs, tks)]
    total = sum(steps)
    offs = [sum(steps[:i]) for i in range(len(pairs))]

    def body(*refs):
        out_ref, acc = refs[2 * len(pairs)], refs[-1]
        k = pl.program_id(2)

        @pl.when(k == 0)
        def _():
            acc[...] = jnp.zeros_like(acc)

        for i in range(len(pairs)):
            @pl.when((k >= offs[i]) & (k < offs[i] + steps[i]))
            def _(i=i):
                acc[...] += lax.dot_general(refs[2 * i][...], refs[2 * i + 1][...], (((0,), (0,)), ((), ())),
                                            preferred_element_type=F32)

        if out_dtype != F32:
            @pl.when(k == total - 1)
            def _():
                out_ref[...] = acc[...].astype(out_dtype)

    in_specs, args = [], []
    for i, (xa, ya) in enumerate(pairs):
        clamp = lambda k, i=i: jnp.clip(k - offs[i], 0, steps[i] - 1)
        in_specs.append(pl.BlockSpec((tks[i], tm), lambda a, c, k, clamp=clamp: (clamp(k), a)))
        in_specs.append(pl.BlockSpec((tks[i], tn), lambda a, c, k, clamp=clamp: (clamp(k), c)))
        args += [xa, ya]
    if col_blocks:
        out_spec = pl.BlockSpec((None, tm, tn), lambda a, c, k: (c, a, 0))
        out_shape = jax.ShapeDtypeStruct((n // tn, m, tn), out_dtype)
    else:
        out_spec = pl.BlockSpec((tm, tn), lambda a, c, k: (a, c))
        out_shape = jax.ShapeDtypeStruct((m, n), out_dtype)
    return pl.pallas_call(
        body, name=name, grid=(m // tm, n // tn, total), in_specs=in_specs, out_specs=out_spec, out_shape=out_shape,
        scratch_shapes=[] if out_dtype == F32 else [pltpu.VMEM((tm, tn), F32)],
        compiler_params=_cp(("arbitrary", "arbitrary", "arbitrary")),
    )(*args)


def _perm_block(t):
    return 4 * (t % 4) + t // 4 if t < 16 else 16 + 3 * ((t - 16) % 4) + (t - 16) // 4


def unpack_w_in(blocks):
    def body(i_ref, o_ref):
        for t in range(28):
            p = _perm_block(t)
            o_ref[:, p * 128:(p + 1) * 128] = i_ref[t // 7, :, (t % 7) * 128:(t % 7 + 1) * 128]

    return pl.pallas_call(
        body, name="unpack_w_in", grid=(2,),
        in_specs=[pl.BlockSpec((4, D // 2, 896), lambda i: (0, i, 0))],
        out_specs=pl.BlockSpec((D // 2, IN_W), lambda i: (i, 0)),
        out_shape=jax.ShapeDtypeStruct((D, IN_W), BF16),
    )(blocks)


def pack_w_in(dw):
    def body(i_ref, o_ref):
        for t in range(28):
            p = _perm_block(t)
            o_ref[t // 7, :, (t % 7) * 128:(t % 7 + 1) * 128] = i_ref[:, p * 128:(p + 1) * 128].astype(BF16)

    return pl.pallas_call(
        body, name="pack_w_in", grid=(4,),
        in_specs=[pl.BlockSpec((D // 4, IN_W), lambda i: (i, 0))],
        out_specs=pl.BlockSpec((4, D // 4, 896), lambda i: (0, i, 0)),
        out_shape=jax.ShapeDtypeStruct((4, D, 896), BF16),
    )(dw)


def _place():
    return lax.axis_index("x"), lax.axis_index("y"), lax.axis_index("c")


class Hosted:
    def __init__(self, args, out_shape, scratch, start, finish):
        self.args, self.out_shape, self.scratch, self.start, self.finish = args, out_shape, scratch, start, finish

    def specs(self):
        hbm = pl.BlockSpec(memory_space=pl.ANY)
        return [hbm] * len(self.args), [hbm] * len(self.out_shape)

    def split(self, refs, n_in, n_out):
        a, b = len(self.args), len(self.out_shape)
        cuts = [n_in, n_in + a, n_in + a + n_out, n_in + a + n_out + b, len(refs) - len(self.scratch)]
        parts = [refs[i:j] for i, j in zip([0] + cuts, cuts + [len(refs)])]
        return parts[0], parts[1], parts[2], parts[3], parts[4], parts[5]


def no_exchange():
    return Hosted([], [], [], lambda *a: None, lambda *a: None)


def run_hosted(hosted, name):
    def body(*refs):
        _, ins, _, outs, _, sems = hosted.split(refs, 0, 0)
        hosted.start(ins, outs, sems)
        hosted.finish(ins, outs, sems)

    in_specs, out_specs = hosted.specs()
    return pl.pallas_call(body, name=name, in_specs=in_specs, out_specs=out_specs, out_shape=hosted.out_shape,
                          scratch_shapes=hosted.scratch)(*hosted.args)


def gather8(blocks):
    na = len(blocks)

    def copies(ins, outs, sems):
        send_sems, recv_sems, local_sem = sems
        x, y, c = _place()
        me, sibling = (x, y, c), (x, y, 1 - c)
        chips = [(1 - x, y), (x, 1 - y), (1 - x, 1 - y)]

        def slot(o_ref, px, py, pc):
            return o_ref.at[4 * px + 2 * py + pc]

        def copy(a, k, block, to, src=None):
            return pltpu.make_async_remote_copy(
                src_ref=slot(outs[a], *block) if src is None else src, dst_ref=slot(outs[a], *block),
                send_sem=send_sems.at[a, k], recv_sem=recv_sems.at[a, k], device_id=to, device_id_type=MESH)

        mine = [pltpu.make_async_copy(ins[a], slot(outs[a], *me), local_sem.at[a]) for a in range(na)]
        first = []
        for a in range(na):
            first.append(copy(a, 0, me, sibling, src=ins[a]))
            first += [copy(a, 1 + j, me, (*chip, c), src=ins[a]) for j, chip in enumerate(chips)]
        return copy, mine, first, me, sibling, chips, c

    def start(ins, outs, sems):
        _, mine, first, *_ = copies(ins, outs, sems)
        for cp in mine + first:
            cp.start()

    def finish(ins, outs, sems):
        copy, mine, first, me, sibling, chips, c = copies(ins, outs, sems)
        passed = []
        for j, chip in enumerate(chips):
            for a in range(na):
                copy(a, 1 + j, (*chip, c), me).wait_recv()
                cp = copy(a, 4 + j, (*chip, c), sibling)
                cp.start()
                passed.append(cp)
        for a in range(na):
            copy(a, 0, sibling, me).wait_recv()
            for j, chip in enumerate(chips):
                copy(a, 4 + j, (*chip, 1 - c), me).wait_recv()
        for cp in first + passed:
            cp.wait_send()
        for cp in mine:
            cp.wait()

    return Hosted(list(blocks), [jax.ShapeDtypeStruct((8,) + b.shape, b.dtype) for b in blocks],
                  [pltpu.SemaphoreType.DMA((na, 7)), pltpu.SemaphoreType.DMA((na, 7)), pltpu.SemaphoreType.DMA((na,))],
                  start, finish)


def chips3(arrays):
    na = len(arrays)

    def copies(ins, outs, sems):
        send_sems, recv_sems = sems
        x, y, c = _place()
        return [pltpu.make_async_remote_copy(
            src_ref=ins[a].at[2 * px + py], dst_ref=outs[a].at[k], send_sem=send_sems.at[a, k],
            recv_sem=recv_sems.at[a, k], device_id=(px, py, c), device_id_type=MESH)
            for a in range(na) for k, (px, py) in enumerate([(1 - x, y), (x, 1 - y), (1 - x, 1 - y)])]

    def start(ins, outs, sems):
        for cp in copies(ins, outs, sems):
            cp.start()

    def finish(ins, outs, sems):
        for cp in copies(ins, outs, sems):
            cp.wait()

    return Hosted(list(arrays), [jax.ShapeDtypeStruct((3,) + a.shape[1:], a.dtype) for a in arrays],
                  [pltpu.SemaphoreType.DMA((na, 3)), pltpu.SemaphoreType.DMA((na, 3))], start, finish)


def siblings(arrays):
    na = len(arrays)

    def copies(ins, outs, sems):
        send_sems, recv_sems = sems
        x, y, c = _place()
        return [pltpu.make_async_remote_copy(
            src_ref=ins[a], dst_ref=outs[a], send_sem=send_sems.at[a], recv_sem=recv_sems.at[a],
            device_id=(x, y, 1 - c), device_id_type=MESH) for a in range(na)]

    def start(ins, outs, sems):
        for cp in copies(ins, outs, sems):
            cp.start()

    def finish(ins, outs, sems):
        for cp in copies(ins, outs, sems):
            cp.wait()

    return Hosted(list(arrays), [jax.ShapeDtypeStruct(a.shape, a.dtype) for a in arrays],
                  [pltpu.SemaphoreType.DMA((na,)), pltpu.SemaphoreType.DMA((na,))], start, finish)


def both(first, second):
    na, no, ns = len(first.args), len(first.out_shape), len(first.scratch)

    def start(ins, outs, sems):
        first.start(ins[:na], outs[:no], sems[:ns])
        second.start(ins[na:], outs[no:], sems[ns:])

    def finish(ins, outs, sems):
        first.finish(ins[:na], outs[:no], sems[:ns])
        second.finish(ins[na:], outs[no:], sems[ns:])

    return Hosted(first.args + second.args, first.out_shape + second.out_shape, first.scratch + second.scratch,
                  start, finish)


def siblings4(arrays):
    na = len(arrays)

    def copies(ins, outs, sems):
        send_sems, recv_sems = sems
        x, y, c = _place()
        return [pltpu.make_async_remote_copy(
            src_ref=ins[a].at[2 * j + 1 - c], dst_ref=outs[a].at[j],
            send_sem=send_sems.at[a, j], recv_sem=recv_sems.at[a, j],
            device_id=(x, y, 1 - c), device_id_type=MESH) for a in range(na) for j in range(4)]

    def start(ins, outs, sems):
        for cp in copies(ins, outs, sems):
            cp.start()

    def finish(ins, outs, sems):
        for cp in copies(ins, outs, sems):
            cp.wait()

    return Hosted(list(arrays), [jax.ShapeDtypeStruct((4,) + a.shape[1:], a.dtype) for a in arrays],
                  [pltpu.SemaphoreType.DMA((na, 4)), pltpu.SemaphoreType.DMA((na, 4))], start, finish)


def sibling_blocks(arrays, name):
    return run_hosted(siblings4(arrays), name)


def _row_tile(r):
    for cand in (512, 256, 128, 64, 32, 16, 8):
        if r % cand == 0:
            return cand
    return r


def chip_partial(place, g8s, landed4s, name):
    n = len(g8s)

    def body(place_ref, *refs):
        del place_ref
        for g_ref, l_ref, o_ref in zip(refs[:n], refs[n:2 * n], refs[2 * n:]):
            o_ref[...] = (g_ref[...].astype(F32) + l_ref[...].astype(F32)).astype(BF16)

    own = [pl.BlockSpec((None,) + g.shape[1:], lambda j, s: (2 * j + s[0], 0, 0)) for g in g8s]
    plain = [pl.BlockSpec((None,) + g.shape[1:], lambda j, s: (j, 0, 0)) for g in g8s]
    return pl.pallas_call(
        body, name=name,
        grid_spec=pltpu.PrefetchScalarGridSpec(num_scalar_prefetch=1, grid=(4,), in_specs=own + plain, out_specs=plain),
        out_shape=[jax.ShapeDtypeStruct((4,) + g.shape[1:], BF16) for g in g8s],
    )(place, *g8s, *landed4s)


def shard_sum(place, partial4s, landed3s, name):
    n = len(partial4s)

    def body(place_ref, *refs):
        del place_ref
        for p_ref, l_ref, o_ref in zip(refs[:n], refs[n:2 * n], refs[2 * n:]):
            acc = p_ref[...].astype(F32)
            for k in range(3):
                acc = acc + l_ref[k].astype(F32)
            o_ref[...] = acc

    def halves(p, lead):
        r, ccols = p.shape[1:]
        return (lead, r // 2, ccols)

    return pl.pallas_call(
        body, name=name,
        grid_spec=pltpu.PrefetchScalarGridSpec(
            num_scalar_prefetch=1, grid=(2,),
            in_specs=[pl.BlockSpec(halves(p, None), lambda i, s: (s[1], i, 0)) for p in partial4s]
            + [pl.BlockSpec(halves(p, 3), lambda i, s: (0, i, 0)) for p in partial4s],
            out_specs=[pl.BlockSpec(halves(p, None)[1:], lambda i, s: (i, 0)) for p in partial4s]),
        out_shape=[jax.ShapeDtypeStruct(p.shape[1:], F32) for p in partial4s],
    )(place, *partial4s, *landed3s)


def _adamw_math(w, g, m, v):
    m2 = B1 * m + (1.0 - B1) * g
    v2 = B2 * v + (1.0 - B2) * (g * g)
    m_hat = m2 / (1.0 - B1 ** STEP)
    v_hat = v2 / (1.0 - B2 ** STEP)
    return -LR * (m_hat / (jnp.sqrt(v_hat) + AEPS) + WD * w), m2, v2


def adamw_halves(place, w, mine, theirs, m, v, name):
    r, ccols = w.shape
    hr = r // 2
    tr = _row_tile(hr)
    nt = hr // tr

    def body(place_ref, w_ref, a_ref, b_ref, m_ref, v_ref, g_out, d_out, m_out, v_out):
        g = jnp.where(pl.program_id(0) == place_ref[0], a_ref[...], b_ref[...])
        d, m2, v2 = _adamw_math(w_ref[...], g, m_ref[...], v_ref[...])
        g_out[...] = g
        d_out[...] = d
        m_out[...] = m2
        v_out[...] = v2

    full = pl.BlockSpec((tr, ccols), lambda h, i, s: (h * nt + i, 0))
    part = pl.BlockSpec((tr, ccols), lambda h, i, s: (i, 0))
    return pl.pallas_call(
        body, name=name,
        grid_spec=pltpu.PrefetchScalarGridSpec(
            num_scalar_prefetch=1, grid=(2, nt), in_specs=[full, part, part, full, full], out_specs=[full] * 4),
        out_shape=[jax.ShapeDtypeStruct((r, ccols), F32)] * 4,
    )(place, w, mine, theirs, m, v)


def adamw_group(place, halved, plain, hosted, name):
    rows = halved[0][0].shape[0]
    tr = 64
    nt = rows // 2 // tr
    nh, npl = len(halved), len(plain)

    def body(place_ref, *refs):
        own_in, h_in, own_out, h_out, _, h_sems = hosted.split(refs, 5 * nh + 4 * npl, 4 * nh + 3 * npl)
        half = pl.program_id(0)
        grid_step = half * nt + pl.program_id(1)

        @pl.when(grid_step == 0)
        def _():
            hosted.start(h_in, h_out, h_sems)

        for i in range(nh):
            w_ref, a_ref, b_ref, m_ref, v_ref = own_in[5 * i:5 * i + 5]
            g = jnp.where(half == place_ref[0], a_ref[...], b_ref[...])
            res = (g,) + _adamw_math(w_ref[...], g, m_ref[...], v_ref[...])
            for o_ref, r in zip(own_out[4 * i:4 * i + 4], res):
                o_ref[...] = r
        for i in range(npl):
            w_ref, g_ref, m_ref, v_ref = own_in[5 * nh + 4 * i:5 * nh + 4 * i + 4]
            res = _adamw_math(w_ref[...], g_ref[...], m_ref[...], v_ref[...])
            for o_ref, r in zip(own_out[4 * nh + 3 * i:4 * nh + 3 * i + 3], res):
                o_ref[...] = r

        @pl.when(grid_step == 2 * nt - 1)
        def _():
            hosted.finish(h_in, h_out, h_sems)

    def full(cols):
        return pl.BlockSpec((tr, cols), lambda h, i, s: (h * nt + i, 0))

    def part(cols):
        return pl.BlockSpec((tr, cols), lambda h, i, s: (i, 0))

    in_specs, out_specs, out_shape, args = [], [], [], []
    for w, a, b, m, v in halved:
        cols = w.shape[1]
        in_specs += [full(cols), part(cols), part(cols), full(cols), full(cols)]
        out_specs += [full(cols)] * 4
        out_shape += [jax.ShapeDtypeStruct(w.shape, F32)] * 4
        args += [w, a, b, m, v]
    for w, g, m, v in plain:
        cols = w.shape[1]
        in_specs += [full(cols)] * 4
        out_specs += [full(cols)] * 3
        out_shape += [jax.ShapeDtypeStruct(w.shape, F32)] * 3
        args += [w, g, m, v]
    h_in_specs, h_out_specs = hosted.specs()
    return pl.pallas_call(
        body, name=name,
        grid_spec=pltpu.PrefetchScalarGridSpec(
            num_scalar_prefetch=1, grid=(2, nt), in_specs=in_specs + h_in_specs, out_specs=out_specs + h_out_specs,
            scratch_shapes=hosted.scratch),
        out_shape=out_shape + hosted.out_shape,
        compiler_params=_cp(("arbitrary", "arbitrary")),
    )(place, *args, *hosted.args)


def _silu(x):
    return x * jax.nn.sigmoid(x)


def prologue(c_rows, c_ctx_row, w_ada, b_shard, half_w_in, late_shards):
    shape = jax.ShapeDtypeStruct
    n_late = len(late_shards)
    half_shapes = [(w.shape[0] // 2, w.shape[1]) for w in late_shards]
    g_w = gather8([half_w_in])
    g_c = gather8([shape((8, D), F32)])
    g_m = gather8([shape((32, 1536), F32)])

    def body(*refs):
        c_ref, cc_ref, w_ref, b_ref, hw_ref = refs[:5]
        late_refs = refs[5:5 + n_late]
        cin_ref, mg_ref, gw_ref = refs[5 + n_late:8 + n_late]
        half_refs = refs[8 + n_late:8 + 2 * n_late]
        cg_s, ms_s = refs[8 + 2 * n_late:10 + 2 * n_late]
        stage = refs[10 + 2 * n_late:10 + 3 * n_late]
        load_sem = refs[10 + 3 * n_late]
        sems = refs[11 + 3 * n_late:]
        sw, sc, sm = sems[0:3], sems[3:6], sems[6:9]
        core = lax.axis_index("c")
        g_w.start([hw_ref], [gw_ref], sw)
        g_c.start([c_ref], [cg_s], sc)
        loads = [pltpu.make_async_copy(late_refs[a].at[pl.ds(core * half_shapes[a][0], half_shapes[a][0]), :],
                                       stage[a], load_sem.at[a]) for a in range(n_late)]
        for cp in loads:
            cp.start()
        g_c.finish([c_ref], [cg_s], sc)
        cin_ref[...] = jnp.zeros_like(cin_ref)
        for dev in range(8):
            cin_ref[2 * dev:2 * dev + 2, :] = cg_s[dev, 0:2, :]
        cin_ref[16:17, :] = cc_ref[...]
        ms_s[...] = _nn(_silu(cin_ref[...]), w_ref[...]) + b_ref[...]
        g_m.start([ms_s], [mg_ref], sm)
        for a, cp in enumerate(loads):
            cp.wait()
            half_refs[a][...] = stage[a][...].astype(BF16)
        g_m.finish([ms_s], [mg_ref], sm)
        g_w.finish([hw_ref], [gw_ref], sw)

    vmem = pl.BlockSpec(memory_space=pltpu.VMEM)
    hbm = pl.BlockSpec(memory_space=pl.ANY)
    return pl.pallas_call(
        body, name="prologue", in_specs=[vmem, vmem, vmem, vmem, hbm] + [hbm] * n_late,
        out_specs=[vmem, vmem, hbm] + [vmem] * n_late,
        out_shape=[shape((32, D), F32), shape((8, 32, 1536), F32)] + g_w.out_shape
        + [shape(s, BF16) for s in half_shapes],
        scratch_shapes=[pltpu.VMEM((8, 8, D), F32), pltpu.VMEM((32, 1536), F32)]
        + [pltpu.VMEM(s, F32) for s in half_shapes] + [pltpu.SemaphoreType.DMA((n_late,))]
        + g_w.scratch + g_c.scratch + g_m.scratch,
        compiler_params=_cp(),
    )(c_rows, c_ctx_row, w_ada, b_shard, half_w_in, *late_shards)


def ada_grads(cin, gb, gc, w_ada):
    def body(c_ref, gb_ref, gc_ref, w_ref, gw_ref, pc_ref):
        ctx_tot = jnp.sum(gc_ref[...], axis=0, keepdims=True)
        rows = lax.broadcasted_iota(jnp.int32, (16, 512), 0)
        dm = jnp.concatenate([gb_ref[...], jnp.where(rows == 0, ctx_tot, 0.0)], axis=0)
        gw_ref[...] = _tn(_silu(c_ref[...]), dm)
        rows8 = lax.broadcasted_iota(jnp.int32, (8, 512), 0)
        part = _nt(jnp.where(rows8 == 0, ctx_tot, 0.0), w_ref[...])

        @pl.when(pl.program_id(0) == 0)
        def _():
            pc_ref[...] = jnp.zeros_like(pc_ref)

        pc_ref[...] += part

    return pl.pallas_call(
        body, name="ada_grads", grid=(3,),
        in_specs=[pl.BlockSpec((32, D), lambda j: (0, 0)), pl.BlockSpec((16, 512), lambda j: (0, j)),
                  pl.BlockSpec((8, 512), lambda j: (0, j)), pl.BlockSpec((D, 512), lambda j: (0, j))],
        out_specs=[pl.BlockSpec((D, 512), lambda j: (0, j)), pl.BlockSpec((8, D), lambda j: (0, 0))],
        out_shape=[jax.ShapeDtypeStruct((D, 1536), F32), jax.ShapeDtypeStruct((8, D), F32)],
    )(cin, gb, gc, w_ada)


SMALL_SUM_ROWS = 15


def small_update(gsm, gbf, gcf, pcg, params):
    n = len(params)

    def body(*refs):
        gsm_ref, gbf_ref, gcf_ref, pcg_ref = refs[:4]
        wmv, outs, loss_out = refs[4:4 + 3 * n], refs[4 + 3 * n:4 + 7 * n], refs[-1]
        acc = gsm_ref[0]
        for dev in range(1, 8):
            acc = acc + gsm_ref[dev]
        c_ctx = wmv[0][...]
        sg = jax.nn.sigmoid(c_ctx)
        dsilu = pcg_ref[0:1, :] + pcg_ref[2:3, :] + pcg_ref[4:5, :] + pcg_ref[6:7, :]
        lane = lax.broadcasted_iota(jnp.int32, (1, D), 1)
        last = acc[14:15, :]
        grads = [
            dsilu * (sg * (1.0 + c_ctx * (1.0 - sg))),
            jnp.sum(gbf_ref[...], axis=0, keepdims=True) + jnp.sum(gcf_ref[...], axis=0, keepdims=True),
            acc[0:1, :] + acc[1:2, :], acc[2:3, :], acc[3:4, :], acc[4:5, :],
            acc[5:6, 0:512], acc[6:14, :], jnp.where(lane < 8, last, 0.0),
        ]
        loss_out[...] = jnp.broadcast_to(jnp.sum(jnp.where(lane == 8, last, 0.0), axis=1, keepdims=True), (8, 128))
        for i, g in enumerate(grads):
            d, m2, v2 = _adamw_math(wmv[3 * i][...], g, wmv[3 * i + 1][...], wmv[3 * i + 2][...])
            outs[4 * i][...] = g
            outs[4 * i + 1][...] = d
            outs[4 * i + 2][...] = m2
            outs[4 * i + 3][...] = v2

    flat = [a for wmv in params for a in wmv]
    out_shape = [jax.ShapeDtypeStruct(w.shape, F32) for w, _, _ in params for _ in range(4)]
    return pl.pallas_call(
        body, name="small_update", out_shape=out_shape + [jax.ShapeDtypeStruct((8, 128), F32)],
    )(gsm, gbf, gcf, pcg, *flat)


def _pad_row(v, rows):
    flat = v.reshape(-1)
    return jnp.pad(flat, (0, rows * D - flat.shape[0])).reshape(rows, D)


def local_step(x, ctx, tgt, mod3, g_pre_mix, g_post_mix, g_pre_mlp, g_post_mlp, ret_decay, ret_gn, na_rpb,
               wperm, late_weights, early_grads):
    nb = x.shape[0]
    tokens = nb * SEQ
    cos, sin = _rope_tables()
    rd = ret_decay.T.reshape(RH, 2, 1)
    gn = ret_gn.reshape(RH, 1, RD)
    bias = na_bias_table(_rpb_flat(na_rpb))
    h, pret, pna = premix_proj(x, mod3, g_pre_mix, wperm, False, "premix_proj")
    hc, pretc, pnac = premix_proj(ctx, mod3, g_pre_mix, wperm, True, "premix_proj_ctx")
    o_all, mixin, gw_out = retention_fwd(pret, pretc, rd, gn, cos, sin, late_weights(0))
    mixin, gw1, gw2 = na_fwd(pna, pnac, bias, mixin, late_weights(1))
    dx_tail, dmix, h2, du, act, dm, dmixin, dmod_t, dg_t, loss_t = tail_fwd_bwd(
        x, mixin, tgt, mod3, g_post_mix, g_pre_mlp, g_post_mlp, gw_out.reshape(D, D), gw1.reshape(4, D, D),
        gw2.reshape(DFF, D))
    dw_out = weight_grad([(mixin.reshape(tokens, D), dmix.reshape(tokens, D))], "grad_w_out", BF16)
    dw1 = weight_grad([(h2.reshape(tokens, D), du.reshape(tokens, DFF))], "grad_w_mlp1", BF16, col_blocks=True)
    dw2 = weight_grad([(act.reshape(tokens, DFF), dm.reshape(tokens, D))], "grad_w_mlp2", BF16)
    dproj, dprojc, drd, dgn, *landed = retention_bwd(pret, pretc, o_all, dmixin, rd, gn, cos, sin,
                                                     early_grads[0](dw_out, dw1, dw2))
    dproj, dprojc, dpat, *early = na_bwd(pna, pnac, bias, dmixin, dproj, dprojc, early_grads[1](landed))
    dw_in = weight_grad([(h.reshape(tokens, D), dproj.reshape(tokens, IN_W)),
                         (hc.reshape(nb * LC, D), dprojc.reshape(nb * LC, IN_W))], "grad_w_in", tn=IN_W // 2, tk=1024)
    grad_x, dmod_a, dg_a, *late = premix_bwd(x, mod3, g_pre_mix, wperm, dproj, dx_tail, early_grads[2](dw_in),
                                             "premix_bwd")
    dmod_c, dg_c = premix_bwd(ctx, mod3, g_pre_mix, wperm, dprojc, None, no_exchange(), "premix_bwd_ctx")
    dmod = jnp.concatenate([jnp.concatenate([dmod_a[:, 0:2], dmod_t[:, 2:6]], axis=1), dmod_c], axis=0)
    last = jnp.pad(jnp.concatenate([drd[:, :, 0].T.reshape(8), loss_t[0, 0:1]]), (0, D - 9)).reshape(1, D)
    small = jnp.concatenate([dg_a[0:1], dg_c[0:1], dg_t[0:3], _pad_row(dgn, 1), dpat.reshape(8, D), last], axis=0)
    return grad_x, late, early, dmod, small


def kernel(x, c, ctx, c_ctx, w_ada, b_ada, g_pre_mix, g_post_mix, g_pre_mlp, g_post_mlp, w_in, ret_decay, ret_gn, na_rpb, w_out, w_mlp1, w_mlp2, loss_target, m_c_ctx, m_w_ada, m_b_ada, m_g_pre_mix, m_g_post_mix, m_g_pre_mlp, m_g_post_mlp, m_w_in, m_ret_decay, m_ret_gn, m_na_rpb, m_w_out, m_w_mlp1, m_w_mlp2, v_c_ctx, v_w_ada, v_b_ada, v_g_pre_mix, v_g_post_mix, v_g_pre_mlp, v_g_post_mlp, v_w_in, v_ret_decay, v_ret_gn, v_na_rpb, v_w_out, v_w_mlp1, v_w_mlp2):
    px, py, pc = _place()
    dev = 4 * px + 2 * py + pc
    chip = 2 * px + py

    half_w_in = lax.dynamic_slice_in_dim(w_in[0], pc * (D // 2), D // 2, 0).astype(BF16)
    cin, mg, gw_in, *late_halves = prologue(
        jnp.pad(c, ((0, 6), (0, 0))), c_ctx[None], w_ada[0], lax.dynamic_slice_in_dim(b_ada, chip * 1536, 1536, 1),
        half_w_in, [w_out[0], w_mlp1[0], w_mlp2[0]])
    halves = [half_w_in] + late_halves
    wperm = unpack_w_in(gw_in.reshape(4, D, 896))
    mod_all = jnp.concatenate([mg[0], mg[2], mg[4], mg[6]], axis=1)
    mod3 = (jnp.pad(lax.dynamic_slice_in_dim(mod_all, 2 * dev, 2, 0), ((0, 1), (0, 0)))
            + jnp.pad(mod_all[16:17], ((2, 0), (0, 0)))).reshape(3, 6, D)

    place = jnp.stack([pc, chip]).astype(jnp.int32)

    early_names = ["w_out", "w_mlp1", "w_mlp2"]
    early_g8, early_partial = [], []

    def early_a(dw_out, dw1, dw2):
        early_g8[:] = [dw_out.reshape(8, 128, D), dw1.reshape(8, 512, D), dw2.reshape(8, 512, D)]
        return siblings4(early_g8)

    def early_b(landed):
        early_partial[:] = chip_partial(place, early_g8, landed, "rs_chip_sum_early")
        return chips3(early_partial)

    late_partial = []

    def late_c(dw_in):
        g8_in = pack_w_in(dw_in).reshape(8, 512, 896)
        (landed_in,) = sibling_blocks([g8_in], "rs_sibling_w_in")
        late_partial[:] = chip_partial(place, [g8_in], [landed_in], "rs_chip_sum_w_in")
        return chips3(late_partial)

    grad_x, (landed3_in,), early_landed, dmod, small = local_step(
        x, ctx, loss_target, mod3, g_pre_mix, g_post_mix, g_pre_mlp, g_post_mlp, ret_decay[0], ret_gn, na_rpb[0],
        wperm, lambda k: gather8(halves[1:2] if k == 0 else halves[2:4]), (early_a, early_b, late_c))
    early_mine = shard_sum(place, early_partial, early_landed, "rs_shard_sum_early")

    pay = jnp.concatenate([dmod.reshape(18, D), small, jnp.zeros((40 - 18 - SMALL_SUM_ROWS, D), F32)], axis=0)
    *early_theirs, gs = run_hosted(both(siblings(early_mine), gather8([pay])), "rs_halves_early_gather_small")
    gbf = gs[:, 0:12].reshape(16, 6 * D)
    gcf = gs[:, 12:18].reshape(8, 6 * D)
    gw_ada, pc_part = ada_grads(cin, lax.dynamic_slice_in_dim(gbf, chip * 1536, 1536, 1),
                                lax.dynamic_slice_in_dim(gcf, chip * 1536, 1536, 1), w_ada[0])
    (mine_in,) = shard_sum(place, late_partial, [landed3_in], "rs_shard_sum_w_in")
    theirs_in, pcg = run_hosted(both(siblings([mine_in]), gather8([pc_part])), "rs_halves_w_in_gather_c_ctx")

    grouped = adamw_group(
        place,
        [(w_mlp1[0], early_mine[1], early_theirs[1], m_w_mlp1[0], v_w_mlp1[0]),
         (w_mlp2[0], early_mine[2], early_theirs[2], m_w_mlp2[0], v_w_mlp2[0])],
        [(w_ada[0], gw_ada, m_w_ada[0], v_w_ada[0])], no_exchange(), "adamw_group")
    d_ada, m_ada, v_ada = grouped[8:11]
    big = [
        [r[None] for r in adamw_halves(place, w_in[0], mine_in, theirs_in, m_w_in[0], v_w_in[0], "adamw_w_in")],
        [r[None] for r in adamw_halves(place, w_out[0], early_mine[0], early_theirs[0], m_w_out[0], v_w_out[0],
                                       "adamw_w_out")],
        [r[None] for r in grouped[0:4]], [r[None] for r in grouped[4:8]],
    ]

    def rpb_rows(t):
        return _rpb_flat(t[0]).reshape(8, D)

    def decay_row(t):
        return jnp.pad(t.reshape(1, 8), ((0, 0), (0, D - 8)))

    views = [lambda t: t.reshape(1, D), lambda t: t, lambda t: t, lambda t: t, lambda t: t, lambda t: t, lambda t: t,
             rpb_rows, decay_row]
    back = [lambda t: t.reshape(D), lambda t: t, lambda t: t, lambda t: t, lambda t: t, lambda t: t, lambda t: t,
            lambda t: _rpb_flat_t(t)[None], lambda t: t[:, 0:8].reshape(1, 2, 4)]
    small_w = (c_ctx, b_ada, g_pre_mix, g_post_mix, g_pre_mlp, g_post_mlp, ret_gn, na_rpb, ret_decay)
    small_m = (m_c_ctx, m_b_ada, m_g_pre_mix, m_g_post_mix, m_g_pre_mlp, m_g_post_mlp, m_ret_gn, m_na_rpb, m_ret_decay)
    small_v = (v_c_ctx, v_b_ada, v_g_pre_mix, v_g_post_mix, v_g_pre_mlp, v_g_post_mlp, v_ret_gn, v_na_rpb, v_ret_decay)
    *res, loss8 = small_update(gs[:, 18:18 + SMALL_SUM_ROWS], gbf, gcf, pcg[:, 0],
                               [(f(w), f(m), f(v)) for f, w, m, v in zip(views, small_w, small_m, small_v)])

    def leaves(ada, idx):
        s_c, s_b, s_g1, s_g2, s_g3, s_g4, s_gn, s_rpb, s_rd = [back[i](res[4 * i + idx]) for i in range(9)]
        return [s_c, ada[None], s_b, s_g1, s_g2, s_g3, s_g4, big[0][idx], s_rd, s_gn, s_rpb,
                big[1][idx], big[2][idx], big[3][idx]]

    return (loss8[0, 0], grad_x, *leaves(gw_ada, 0), *leaves(d_ada, 1), *leaves(m_ada, 2), *leaves(v_ada, 3))
```

```python
import functools
import math

import jax
import jax.numpy as jnp
from jax import lax
from jax.experimental import pallas as pl
from jax.experimental.pallas import tpu as pltpu

F32, BF16 = jnp.float32, jnp.bfloat16
D = 1024
SEQ = 2048
LC = 256
GW = 64
RH, RD, CH = 4, 128, 128
NPAIR = 4
IN_W = 3584
RET_W = 2048
DFF = 4096
EPS = 1e-6
NEG = -1e30
TN = 256
NCH = SEQ // CH
LR, B1, B2, AEPS, WD, STEP = 0.001, 0.9, 0.999, 1e-08, 0.01, 10
MESH = pl.DeviceIdType.MESH
VMEM_LIMIT = 56 * 1024 * 1024


def _cp(sem=None):
    return pltpu.CompilerParams(dimension_semantics=sem, vmem_limit_bytes=VMEM_LIMIT)


def _nn(a, b):
    return jnp.dot(a.astype(BF16), b.astype(BF16), preferred_element_type=F32)


def _nt(a, b):
    return lax.dot_general(a.astype(BF16), b.astype(BF16), (((1,), (1,)), ((), ())), preferred_element_type=F32)


def _tn(a, b):
    return lax.dot_general(a.astype(BF16), b.astype(BF16), (((0,), (0,)), ((), ())), preferred_element_type=F32)


@jax.custom_vjp
def mm_tn(a, b):
    return _tn(a, b)


mm_tn.defvjp(lambda a, b: (_tn(a, b), (a, b)), lambda r, g: (_nt(r[1], g), _nn(r[0], g)))


def _rms(x):
    return x * lax.rsqrt(jnp.mean(x * x, axis=-1, keepdims=True) + EPS)


def _rms_mod(x, g, sc, sh):
    return (_rms(x) * g) * (1.0 + sc) + sh


def _post_mix(x, mix, gt1, sc2, sh2, g_post_mix, g_pre_mlp):
    x1 = x + gt1 * (_rms(mix) * g_post_mix)
    return x1, _rms_mod(x1, g_pre_mlp, sc2, sh2)


def _head_loss(x1, m, gt2, g_post_mlp, tgt):
    err = x1 + gt2 * (_rms(m) * g_post_mlp) - tgt
    return 0.5 * jnp.sum(jnp.mean(err * err, axis=-1, keepdims=True), axis=0, keepdims=True)


def _ln_gate(o, g, w):
    mu = jnp.mean(o, axis=-1, keepdims=True)
    var = jnp.mean(jnp.square(o - mu), axis=-1, keepdims=True)
    y = (o - mu) * lax.rsqrt(var + EPS)
    return (y * w) * (g * jax.nn.sigmoid(g))


def _swap32(x):
    lane = lax.broadcasted_iota(jnp.int32, x.shape, 1)
    return jnp.where((lane & 32) == 0, pltpu.roll(x, 96, 1), pltpu.roll(x, 32, 1))


def _rope(x, cos, sin):
    return x * cos + _swap32(x) * sin


def _rope_t(g, cos, sin):
    return g * cos + _swap32(g * sin)


def _rope_tables():
    tok = lax.broadcasted_iota(jnp.int32, (SEQ, RD), 0)
    lane = lax.broadcasted_iota(jnp.int32, (SEQ, RD), 1)
    pos = jnp.where(lane < 64, tok >> 6, tok & (GW - 1)).astype(F32)
    ang = pos * jnp.exp((lane & 31).astype(F32) * (-math.log(10000.0) / 32))
    return jnp.cos(ang), jnp.where((lane & 32) == 0, -jnp.sin(ang), jnp.sin(ang))


def _chunk_loop(n, body, init, k=4):
    def several(t, carry):
        for i in range(k):
            carry = body(k * t + i, carry)
        return carry

    return lax.fori_loop(0, n // k, several, init)


def _fiota(shape, dim):
    return lax.broadcasted_iota(jnp.int32, shape, dim).astype(F32)


def _ret_state(k, v, s, lg, reverse):
    pos = _fiota((CH, 1), 0)
    b_exp = pos if reverse else (CH - 1.0 - pos)
    return jnp.exp(lg * CH) * s + mm_tn(k * jnp.exp(lg * b_exp), v)


class _Decays:
    def __init__(self, lgs):
        i, j, pos = _fiota((CH, CH), 0), _fiota((CH, CH), 1), _fiota((CH, 1), 0)
        diffs = (i - j, j - i)
        keep = (diffs[0] >= 0, diffs[1] > 0)
        mats = [jnp.where(m, jnp.exp(lg * jnp.where(m, d, 0.0)), 0.0) for lg, d, m in zip(lgs, diffs, keep)]
        self.mask = mats[0] + mats[1]
        self.dmask = [mats[0] * diffs[0], mats[1] * diffs[1]]
        a_exp, b_exp = (pos + 1.0, CH - pos), (CH - 1.0 - pos, pos)
        self.a = [jnp.exp(lg * e) for lg, e in zip(lgs, a_exp)]
        self.b = [jnp.exp(lg * e) for lg, e in zip(lgs, b_exp)]
        self.da = [a * e for a, e in zip(self.a, a_exp)]
        self.db = [b * e for b, e in zip(self.b, b_exp)]
        self.g = [jnp.exp(lg * CH) for lg in lgs]


def _both(x, w):
    return jnp.concatenate([x * w[0], x * w[1]], axis=1)


def _total(x):
    return jnp.sum(jnp.sum(x, axis=1, keepdims=True), axis=0, keepdims=True)


def _state_pass(dec, init, k_s, v_of, st_s):
    def step(t, carry):
        out = []
        for d, s in enumerate(carry):
            n = (NCH - 1 - t) if d else t
            sl = pl.ds(pl.multiple_of(n * CH, CH), CH)
            st_s[n, d * RD:(d + 1) * RD, :] = s
            out.append(dec.g[d] * s + _tn(k_s[sl, :] * dec.b[d], v_of(sl)))
        return tuple(out)

    _chunk_loop(NCH, step, tuple(init))


def premix_proj(xin, mod3, g_pre, wperm, is_ctx, name):
    nb, length, _ = xin.shape
    tn = min(2 * TN, length)

    def body(x_ref, mod_ref, g_ref, w_ref, h_ref, pret_ref, pna_ref):
        h = _rms_mod(x_ref[...], g_ref[...], mod_ref[1:2, :], mod_ref[0:1, :])
        hb = h.astype(BF16)
        h_ref[...] = hb
        pret_ref[...] = jnp.dot(hb, w_ref[:, :RET_W], preferred_element_type=F32)
        pna_ref[...] = jnp.dot(hb, w_ref[:, RET_W:], preferred_element_type=F32).astype(BF16)

    return pl.pallas_call(
        body, name=name, grid=(nb, length // tn),
        in_specs=[
            pl.BlockSpec((None, tn, D), lambda b, t: (b, t, 0)),
            pl.BlockSpec((None, 6, D), (lambda b, t: (2, 0, 0)) if is_ctx else (lambda b, t: (b, 0, 0))),
            pl.BlockSpec((1, D), lambda b, t: (0, 0)),
            pl.BlockSpec((D, IN_W), lambda b, t: (0, 0), pipeline_mode=pl.Buffered(1)),
        ],
        out_specs=[
            pl.BlockSpec((None, tn, D), lambda b, t: (b, t, 0)),
            pl.BlockSpec((None, tn, RET_W), lambda b, t: (b, t, 0)),
            pl.BlockSpec((None, tn, IN_W - RET_W), lambda b, t: (b, t, 0)),
        ],
        out_shape=[
            jax.ShapeDtypeStruct((nb, length, D), BF16),
            jax.ShapeDtypeStruct((nb, length, RET_W), F32),
            jax.ShapeDtypeStruct((nb, length, IN_W - RET_W), BF16),
        ],
        compiler_params=_cp(("arbitrary", "arbitrary")),
    )(xin, mod3, g_pre, wperm)


def premix_bwd(xin, mod3, g_pre, wperm, dproj, dx_tail, hosted, name):
    nb, length, _ = xin.shape
    tn = min(2 * TN, length)
    is_ctx = dx_tail is None

    def body(*refs):
        own_in, h_in, own_out, h_out, _, h_sems = hosted.split(refs, 5 if is_ctx else 6, 2 if is_ctx else 3)
        if is_ctx:
            (x_ref, mod_ref, g_ref, w_ref, dp_ref), (dmod_ref, dg_ref) = own_in, own_out
        else:
            (x_ref, mod_ref, g_ref, w_ref, dp_ref, dxt_ref), (dx_ref, dmod_ref, dg_ref) = own_in, own_out
        b, t = pl.program_id(0), pl.program_id(1)
        grid_step = b * (length // tn) + t

        @pl.when(grid_step == 0)
        def _():
            hosted.start(h_in, h_out, h_sems)

        @pl.when(grid_step == nb * (length // tn) - 1)
        def _():
            hosted.finish(h_in, h_out, h_sems)

        dh = lax.dot_general(dp_ref[...], w_ref[...], (((1,), (1,)), ((), ())), preferred_element_type=F32)
        _, vjp = jax.vjp(_rms_mod, x_ref[...], g_ref[...], mod_ref[1:2, :], mod_ref[0:1, :])
        dx, dg, dsc, dsh = vjp(dh)
        if not is_ctx:
            dx_ref[...] = dx + dxt_ref[...]

        @pl.when((t == 0) & ((b == 0) if is_ctx else True))
        def _():
            dmod_ref[...] = jnp.zeros_like(dmod_ref)

        @pl.when((t == 0) & (b == 0))
        def _():
            dg_ref[...] = jnp.zeros_like(dg_ref)

        dmod_ref[0:1, :] += dsh
        dmod_ref[1:2, :] += dsc
        dg_ref[0:1, :] += dg

    tok = lambda b, t: (b, t, 0)
    in_specs = [
        pl.BlockSpec((None, tn, D), tok),
        pl.BlockSpec((None, 6, D), (lambda b, t: (2, 0, 0)) if is_ctx else (lambda b, t: (b, 0, 0))),
        pl.BlockSpec((1, D), lambda b, t: (0, 0)),
        pl.BlockSpec((D, IN_W), lambda b, t: (0, 0), pipeline_mode=pl.Buffered(1)),
        pl.BlockSpec((None, tn, IN_W), tok),
    ]
    args = [xin, mod3, g_pre, wperm, dproj]
    out_specs = [
        pl.BlockSpec((None, 6, D), (lambda b, t: (0, 0, 0)) if is_ctx else (lambda b, t: (b, 0, 0))),
        pl.BlockSpec((8, D), lambda b, t: (0, 0)),
    ]
    out_shape = [jax.ShapeDtypeStruct((1 if is_ctx else nb, 6, D), F32), jax.ShapeDtypeStruct((8, D), F32)]
    if not is_ctx:
        in_specs.append(pl.BlockSpec((None, tn, D), tok))
        args.append(dx_tail)
        out_specs.insert(0, pl.BlockSpec((None, tn, D), tok))
        out_shape.insert(0, jax.ShapeDtypeStruct((nb, length, D), F32))
    h_in_specs, h_out_specs = hosted.specs()
    return pl.pallas_call(
        body, name=name, grid=(nb, length // tn), in_specs=in_specs + h_in_specs, out_specs=out_specs + h_out_specs,
        out_shape=out_shape + hosted.out_shape, scratch_shapes=hosted.scratch,
        compiler_params=_cp(("arbitrary", "arbitrary")),
    )(*args, *hosted.args)


def _ret_specs(order):
    def im(f):
        return lambda *g: f(*order(*g))
    return dict(
        pret=pl.BlockSpec((None, SEQ, 512), im(lambda b, h: (b, 0, h))),
        pretc=pl.BlockSpec((None, LC, 512), im(lambda b, h: (b, 0, h))),
        rd=pl.BlockSpec((None, 2, 1), im(lambda b, h: (h, 0, 0))),
        gn=pl.BlockSpec((None, 1, RD), im(lambda b, h: (h, 0, 0))),
        tab=pl.BlockSpec((SEQ, RD), im(lambda b, h: (0, 0))),
        head=pl.BlockSpec((None, SEQ, RD), im(lambda b, h: (b, 0, h))),
    )


def retention_fwd(pret, pretc, rd, gn, cos, sin, hosted):
    nb = pret.shape[0]
    sp = _ret_specs(lambda b, h: (b, h))

    def body(*refs):
        own_in, h_in, own_out, h_out, own_scr, h_sems = hosted.split(refs, 6, 2)
        p_ref, pc_ref, rd_ref, gn_ref, cos_ref, sin_ref = own_in
        (o_ref, mix_ref), (q_s, k_s, o_s, st_s) = own_out, own_scr
        grid_step = pl.program_id(0) * RH + pl.program_id(1)

        @pl.when(grid_step == 0)
        def _():
            hosted.start(h_in, h_out, h_sems)

        cos_v, sin_v = cos_ref[...], sin_ref[...]
        q_s[...] = _rope(p_ref[:, 0:128], cos_v, sin_v) * (RD ** -0.5)
        k_s[...] = _rope(p_ref[:, 128:256], cos_v, sin_v)
        lgs, init = [], []
        for rev in (False, True):
            lg = jax.nn.log_sigmoid(rd_ref[int(rev):int(rev) + 1, :])
            s = jnp.zeros((RD, RD), F32)
            for n in ((1, 0) if rev else (0, 1)):
                s = _ret_state(pc_ref[n * CH:(n + 1) * CH, 128:256], pc_ref[n * CH:(n + 1) * CH, 256:384], s, lg, rev)
            lgs.append(lg)
            init.append(s)

        dec = _Decays(lgs)
        _state_pass(dec, init, k_s, lambda sl: p_ref[sl, 256:384], st_s)

        def chunk(n, carry):
            sl = pl.ds(pl.multiple_of(n * CH, CH), CH)
            q = q_s[sl, :]
            o_s[sl, :] = (_nn(_nt(q, k_s[sl, :]) * dec.mask, p_ref[sl, 256:384]) + _nn(_both(q, dec.a), st_s[n]))
            return carry

        _chunk_loop(NCH, chunk, 0)
        o = o_s[...]
        o_ref[...] = o
        mix_ref[...] = _ln_gate(o, p_ref[:, 384:512], gn_ref[...]).astype(BF16)

        @pl.when(grid_step == nb * RH - 1)
        def _():
            hosted.finish(h_in, h_out, h_sems)

    h_in_specs, h_out_specs = hosted.specs()
    return pl.pallas_call(
        body, name="retention_fwd", grid=(nb, RH),
        in_specs=[sp["pret"], sp["pretc"], sp["rd"], sp["gn"], sp["tab"], sp["tab"]] + h_in_specs,
        out_specs=[sp["head"], sp["head"]] + h_out_specs,
        out_shape=[jax.ShapeDtypeStruct((nb, SEQ, RH * RD), F32), jax.ShapeDtypeStruct((nb, SEQ, D), BF16)]
        + hosted.out_shape,
        scratch_shapes=[pltpu.VMEM((SEQ, RD), F32)] * 3 + [pltpu.VMEM((NCH, 2 * RD, RD), F32)] + hosted.scratch,
        compiler_params=_cp(("arbitrary", "arbitrary")),
    )(pret, pretc, rd, gn, cos, sin, *hosted.args)


def retention_bwd(pret, pretc, o_all, dmixin, rd, gn, cos, sin, hosted):
    nb = pret.shape[0]
    sp = _ret_specs(lambda h, b: (b, h))

    def body(*refs):
        own_in, h_in, own_out, h_out, own_scr, h_sems = hosted.split(refs, 8, 4)
        p_ref, pc_ref, o_ref, dmix_ref, rd_ref, gn_ref, cos_ref, sin_ref = own_in
        dp_ref, dpc_ref, drd_ref, dgn_ref = own_out
        q_s, k_s, do_s, dq_s, dk_s, dv_s, st_s, gst_s = own_scr
        b = pl.program_id(1)
        grid_step = pl.program_id(0) * nb + b

        @pl.when(grid_step == 0)
        def _():
            hosted.start(h_in, h_out, h_sems)

        cos_v, sin_v = cos_ref[...], sin_ref[...]
        q_s[...] = _rope(p_ref[:, 0:128], cos_v, sin_v) * (RD ** -0.5)
        k_s[...] = _rope(p_ref[:, 128:256], cos_v, sin_v)
        _, gate_vjp = jax.vjp(_ln_gate, o_ref[...], p_ref[:, 384:512], gn_ref[...])
        do, dg, dgn = gate_vjp(dmix_ref[...].astype(F32))
        do_s[...] = do
        dp_ref[:, 384:512] = dg.astype(BF16)

        @pl.when(b == 0)
        def _():
            drd_ref[...] = jnp.zeros_like(drd_ref)
            dgn_ref[...] = jnp.zeros_like(dgn_ref)

        dgn_ref[...] += dgn
        kcs = [pc_ref[n * CH:(n + 1) * CH, 128:256] for n in (0, 1)]
        vcs = [pc_ref[n * CH:(n + 1) * CH, 256:384] for n in (0, 1)]
        dirs = []
        init = []
        for rev in (False, True):
            rdv = rd_ref[int(rev):int(rev) + 1, :]
            lg = jax.nn.log_sigmoid(rdv)
            order_c = (1, 0) if rev else (0, 1)
            s = jnp.zeros((RD, RD), F32)
            ctx_states = []
            for n in order_c:
                ctx_states.append(s)
                s = _ret_state(kcs[n], vcs[n], s, lg, rev)
            dirs.append((rev, order_c, lg, rdv, ctx_states))
            init.append(s)
        dec = _Decays([lg for _, _, lg, _, _ in dirs])

        def v_of(sl):
            return p_ref[sl, 256:384]

        _state_pass(dec, init, k_s, v_of, st_s)
        zeros = jnp.zeros((CH, RD), F32)

        def scores_back(n, carry):
            dmask_sum, da_f, da_b = carry
            sl = pl.ds(pl.multiple_of(n * CH, CH), CH)
            q, k, v, do = q_s[sl, :], k_s[sl, :], v_of(sl), do_s[sl, :]
            scores = _nt(q, k)
            d_att = _nt(do, v)
            d_scores = d_att * dec.mask
            d_qa = _nt(do, st_s[n])
            d_qf, d_qb = d_qa[:, 0:RD], d_qa[:, RD:2 * RD]
            dq_s[sl, :] = _nn(d_scores, k) + d_qf * dec.a[0] + d_qb * dec.a[1]
            dk_s[sl, :] = _tn(d_scores, q)
            dv_s[sl, :] = _tn(scores * dec.mask, do)
            gst_s[n] = _tn(_both(q, dec.a), do)
            return dmask_sum + d_att * scores, da_f + d_qf * q, da_b + d_qb * q

        dmask_sum, da_f, da_b = _chunk_loop(NCH, scores_back, (zeros, zeros, zeros))

        def state_back(t, carry):
            out = []
            for d, r in enumerate(carry):
                n = t if d else (NCH - 1 - t)
                rows = slice(d * RD, (d + 1) * RD)
                own = gst_s[n, rows, :]
                gst_s[n, rows, :] = r
                out.append(own + dec.g[d] * r)
            return tuple(out)

        d_states = _chunk_loop(NCH, state_back, (zeros, zeros))

        def updates_back(n, carry):
            db_f, db_b, dg_f, dg_b = carry
            sl = pl.ds(pl.multiple_of(n * CH, CH), CH)
            k, r, s = k_s[sl, :], gst_s[n], st_s[n]
            d_kw = _nt(v_of(sl), r)
            d_kf, d_kb = d_kw[:, 0:RD], d_kw[:, RD:2 * RD]
            dk_s[sl, :] += d_kf * dec.b[0] + d_kb * dec.b[1]
            dv_s[sl, :] += _nn(_both(k, dec.b), r)
            return (db_f + d_kf * k, db_b + d_kb * k, dg_f + r[0:RD, :] * s[0:RD, :],
                    dg_b + r[RD:2 * RD, :] * s[RD:2 * RD, :])

        db_dg = _chunk_loop(NCH, updates_back, (zeros, zeros, zeros, zeros))
        dkc = [None, None]
        dvc = [None, None]
        for d, ((rev, order_c, lg, rdv, ctx_states), ds) in enumerate(zip(dirs, d_states)):
            dlg = (_total(dmask_sum * dec.dmask[d]) + _total((da_f, da_b)[d] * dec.da[d])
                   + _total(db_dg[d] * dec.db[d]) + CH * dec.g[d] * _total(db_dg[2 + d]))
            for idx in (1, 0):
                n = order_c[idx]
                _, vjp = jax.vjp(functools.partial(_ret_state, reverse=rev), kcs[n], vcs[n], ctx_states[idx], lg)
                dk_c, dv_c, ds, dl = vjp(ds)
                dlg = dlg + dl
                dkc[n] = dk_c if dkc[n] is None else dkc[n] + dk_c
                dvc[n] = dv_c if dvc[n] is None else dvc[n] + dv_c
            drd_ref[int(rev):int(rev) + 1, :] += dlg * jax.nn.sigmoid(-rdv)
        dp_ref[:, 0:128] = _rope_t(dq_s[...] * (RD ** -0.5), cos_v, sin_v).astype(BF16)
        dp_ref[:, 128:256] = _rope_t(dk_s[...], cos_v, sin_v).astype(BF16)
        dp_ref[:, 256:384] = dv_s[...].astype(BF16)
        zero = jnp.zeros((CH, RD), BF16)
        for n in (0, 1):
            rows = slice(n * CH, (n + 1) * CH)
            dpc_ref[rows, 0:128] = zero
            dpc_ref[rows, 128:256] = dkc[n].astype(BF16)
            dpc_ref[rows, 256:384] = dvc[n].astype(BF16)
            dpc_ref[rows, 384:512] = zero

        @pl.when(grid_step == RH * nb - 1)
        def _():
            hosted.finish(h_in, h_out, h_sems)

    h_in_specs, h_out_specs = hosted.specs()
    return pl.pallas_call(
        body, name="retention_bwd", grid=(RH, nb),
        in_specs=[sp["pret"], sp["pretc"], sp["head"], sp["head"], sp["rd"], sp["gn"], sp["tab"], sp["tab"]]
        + h_in_specs,
        out_specs=[
            pl.BlockSpec((None, SEQ, 512), lambda h, b: (b, 0, h)),
            pl.BlockSpec((None, LC, 512), lambda h, b: (b, 0, h)),
            pl.BlockSpec((None, 2, 1), lambda h, b: (h, 0, 0)),
            pl.BlockSpec((None, 1, RD), lambda h, b: (h, 0, 0)),
        ] + h_out_specs,
        out_shape=[
            jax.ShapeDtypeStruct((nb, SEQ, IN_W), BF16),
            jax.ShapeDtypeStruct((nb, LC, IN_W), BF16),
            jax.ShapeDtypeStruct((RH, 2, 1), F32),
            jax.ShapeDtypeStruct((RH, 1, RD), F32),
        ] + hosted.out_shape,
        scratch_shapes=[pltpu.VMEM((SEQ, RD), F32)] * 6 + [pltpu.VMEM((NCH, 2 * RD, RD), F32)] * 2 + hosted.scratch,
        compiler_params=_cp(("arbitrary", "arbitrary")),
    )(pret, pretc, o_all, dmixin, rd, gn, cos, sin, *hosted.args)


def _rpb_flat(rpb):
    return jnp.pad(rpb, ((0, 0), (0, 1), (0, 33))).reshape(NPAIR, 2, 1, 1024)


def _rpb_flat_t(dflat):
    return dflat.reshape(8, 16, 64)[:, :15, :31]


def _barrel(x, left):
    row = lax.broadcasted_iota(jnp.int32, x.shape, 0)
    n = x.shape[1]
    for bit in range(6):
        s = 1 << bit
        x = jnp.where(((row >> bit) & 1) == 1, pltpu.roll(x, (n - s) if left else s, 1), x)
    return x


NA_TILE_ROWS, NA_BAND_ROWS = 4, 12
NA_Q, NA_K = NA_TILE_ROWS * GW, NA_BAND_ROWS * GW
NA_TILES = SEQ // NA_Q


def _band_start(r0):
    return min(max(r0 - 4, 0), 32 - NA_BAND_ROWS)


def _tile_layout(t):
    rows = range(t * NA_TILE_ROWS, (t + 1) * NA_TILE_ROWS)
    return tuple((r if r < 4 else (r - 24 if r > 28 else 4), min(max(r - 4, 0), 24) - _band_start(rows[0]))
                 for r in rows)


NA_CLASSES = sorted(set(_tile_layout(t) for t in range(NA_TILES)))


def _tile_rows(cls):
    return NA_CLASSES[cls]


def _na_tile(t):
    start = jnp.clip(NA_TILE_ROWS * t - 4, 0, 32 - NA_BAND_ROWS)
    cls = 0
    for tile in range(NA_TILES):
        cls = jnp.where(t == tile, NA_CLASSES.index(_tile_layout(tile)), cls)
    return pl.ds(pl.multiple_of(t * NA_Q, NA_Q), NA_Q), pl.ds(pl.multiple_of(start * GW, NA_Q), NA_K), cls


def _na_probs(qst, kb, kc, bias):
    s_loc = _nt(qst, kb) + bias
    s_ctx = _nt(qst, kc)
    m = jnp.maximum(jnp.max(s_loc, axis=1, keepdims=True), jnp.max(s_ctx, axis=1, keepdims=True))
    e_loc, e_ctx = jnp.exp(s_loc - m), jnp.exp(s_ctx - m)
    den = jnp.sum(e_loc, axis=1, keepdims=True) + jnp.sum(e_ctx, axis=1, keepdims=True)
    return e_loc / den, e_ctx / den


def _stack_heads(t):
    lane = lax.broadcasted_iota(jnp.int32, t.shape, 1)
    zero = jnp.zeros_like(t)
    return jnp.concatenate([jnp.where(lane < 64, t, zero), jnp.where(lane >= 64, t, zero)], axis=0)


def _unstack_heads(t):
    n = t.shape[0] // 2
    lane = lax.broadcasted_iota(jnp.int32, (n, 128), 1)
    return jnp.where(lane < 64, t[:n], t[n:])


def na_bias_table(flat):
    def body(flat_ref, out_ref):
        qc = lax.broadcasted_iota(jnp.int32, (GW, 512), 0)
        kc = lax.broadcasted_iota(jnp.int32, (GW, 512), 1) & 63
        start = jnp.clip(qc - 8, 0, GW - 16)
        window = (kc >= start) & (kc < start + 16)
        fill = jnp.full((GW, NA_K - 512), NEG, F32)
        for hh in (0, 1):
            skew = _barrel(pltpu.roll(jnp.broadcast_to(flat_ref[hh], (GW, 1024)), 1024 - 15, 1), left=False)
            by_class = [jnp.where(window, (skew if rc == 7 else pltpu.roll(skew, (9 + rc) * 64, 1))[:, 0:512], NEG)
                        for rc in range(8)]
            for cls in range(len(NA_CLASSES)):
                for qr, (rc, off) in enumerate(_tile_rows(cls)):
                    w = jnp.concatenate([by_class[rc], fill], axis=1)
                    rows = slice(hh * NA_Q + qr * GW, hh * NA_Q + (qr + 1) * GW)
                    out_ref[cls, rows, :] = pltpu.roll(w, off * GW, 1) if off else w

    return pl.pallas_call(
        body, name="na_bias_table", grid=(NPAIR,),
        in_specs=[pl.BlockSpec((None, 2, 1, 1024), lambda p: (p, 0, 0, 0))],
        out_specs=pl.BlockSpec((None, len(NA_CLASSES), 2 * NA_Q, NA_K), lambda p: (p, 0, 0, 0)),
        out_shape=jax.ShapeDtypeStruct((NPAIR, len(NA_CLASSES), 2 * NA_Q, NA_K), F32),
    )(flat)


def na_fwd(pna, pnac, bias, mixin, hosted):
    nb = pna.shape[0]

    def body(*refs):
        (p_ref, pc_ref, bias_ref, _), h_in, (out_ref,), h_out, _, h_sems = hosted.split(refs, 4, 1)
        grid_step = pl.program_id(0) * nb + pl.program_id(1)

        @pl.when(grid_step == 0)
        def _():
            hosted.start(h_in, h_out, h_sems)

        kc, vc = pc_ref[:, 128:256], pc_ref[:, 256:384]

        def tile(t, carry):
            qsl, bsl, cls = _na_tile(t)
            kb, vb = p_ref[bsl, 128:256], p_ref[bsl, 256:384]
            p_loc, p_ctx = _na_probs(_stack_heads(p_ref[qsl, 0:128] * 0.125), kb, kc, bias_ref[cls])
            out_ref[qsl, :] = _unstack_heads(_nn(p_loc, vb) + _nn(p_ctx, vc)).astype(BF16)
            return carry

        lax.fori_loop(0, NA_TILES, tile, 0, unroll=4)

        @pl.when(grid_step == NPAIR * nb - 1)
        def _():
            hosted.finish(h_in, h_out, h_sems)

    h_in_specs, h_out_specs = hosted.specs()
    return pl.pallas_call(
        body, name="na_fwd", grid=(NPAIR, nb),
        in_specs=[
            pl.BlockSpec((None, SEQ, 384), lambda p, b: (b, 0, p)),
            pl.BlockSpec((None, LC, 384), lambda p, b: (b, 0, p)),
            pl.BlockSpec((None, len(NA_CLASSES), 2 * NA_Q, NA_K), lambda p, b: (p, 0, 0, 0)),
            pl.BlockSpec(memory_space=pl.ANY),
        ] + h_in_specs,
        out_specs=[pl.BlockSpec((None, SEQ, 128), lambda p, b: (b, 0, 4 + p))] + h_out_specs,
        out_shape=[jax.ShapeDtypeStruct((nb, SEQ, D), BF16)] + hosted.out_shape,
        input_output_aliases={3: 0},
        scratch_shapes=hosted.scratch,
        compiler_params=_cp(("arbitrary", "arbitrary")),
    )(pna, pnac, bias, mixin, *hosted.args)


def na_bwd(pna, pnac, bias, dmixin, dproj, dprojc, hosted):
    nb = pna.shape[0]

    def body(*refs):
        own_in, h_in, own_out, h_out, own_scr, h_sems = hosted.split(refs, 6, 3)
        p_ref, pc_ref, bias_ref, dmix_ref = own_in[:4]
        dp_ref, dpc_ref, dpat_ref = own_out
        dbias_s, dk_s, dv_s, dkc_s, dvc_s, res_s, resc_s = own_scr
        b, part = pl.program_id(1), pl.program_id(2)
        grid_step = (pl.program_id(0) * nb + b) * 3 + part

        @pl.when(grid_step == 0)
        def _():
            hosted.start(h_in, h_out, h_sems)

        @pl.when(grid_step == NPAIR * nb * 3 - 1)
        def _():
            hosted.finish(h_in, h_out, h_sems)

        @pl.when(part == 0)
        def _():
            @pl.when(b == 0)
            def _():
                dbias_s[...] = jnp.zeros_like(dbias_s)

            dk_s[...] = jnp.zeros_like(dk_s)
            dv_s[...] = jnp.zeros_like(dv_s)
            dkc_s[...] = jnp.zeros_like(dkc_s)
            dvc_s[...] = jnp.zeros_like(dvc_s)
            kc, vc = pc_ref[:, 128:256], pc_ref[:, 256:384]

            def tile(t, carry):
                qsl, bsl, cls = _na_tile(t)
                kb, vb = p_ref[bsl, 128:256], p_ref[bsl, 256:384]
                qst, dost = _stack_heads(p_ref[qsl, 0:128] * 0.125), _stack_heads(dmix_ref[qsl, :])
                p_loc, p_ctx = _na_probs(qst, kb, kc, bias_ref[cls])
                dp_loc, dp_ctx = _nt(dost, vb), _nt(dost, vc)
                delta = (jnp.sum(p_loc * dp_loc, axis=1, keepdims=True)
                         + jnp.sum(p_ctx * dp_ctx, axis=1, keepdims=True))
                ds_loc, ds_ctx = p_loc * (dp_loc - delta), p_ctx * (dp_ctx - delta)
                dbias_s[cls] += ds_loc
                res_s[0, qsl, :] = _unstack_heads((_nn(ds_loc, kb) + _nn(ds_ctx, kc)) * 0.125).astype(BF16)
                dk_s[bsl, :] += _tn(ds_loc, qst)
                dv_s[bsl, :] += _tn(p_loc, dost)
                dkc_s[...] += _tn(ds_ctx, qst)
                dvc_s[...] += _tn(p_ctx, dost)
                return carry

            lax.fori_loop(0, NA_TILES, tile, 0, unroll=2)
            res_s[1] = dk_s[...].astype(BF16)
            res_s[2] = dv_s[...].astype(BF16)
            resc_s[0] = jnp.zeros((LC, 128), BF16)
            resc_s[1] = dkc_s[...].astype(BF16)
            resc_s[2] = dvc_s[...].astype(BF16)

            @pl.when(b == nb - 1)
            def _():
                for hh in (0, 1):
                    by_class = [None] * 8
                    for cls in range(len(NA_CLASSES)):
                        for qr, (rc, off) in enumerate(_tile_rows(cls)):
                            w = dbias_s[cls, hh * NA_Q + qr * GW:hh * NA_Q + (qr + 1) * GW, :]
                            w = (pltpu.roll(w, NA_K - off * GW, 1) if off else w)[:, 0:512]
                            by_class[rc] = w if by_class[rc] is None else by_class[rc] + w
                    skew = jnp.zeros((GW, 1024), F32)
                    for rc in range(8):
                        w = jnp.concatenate([by_class[rc], jnp.zeros((GW, 512), F32)], axis=1)
                        skew = skew + (w if rc == 7 else pltpu.roll(w, (7 - rc) * 64, 1))
                    dpat_ref[hh] = jnp.sum(pltpu.roll(_barrel(skew, left=True), 15, 1), axis=0, keepdims=True)

        dp_ref[...] = res_s[part]
        dpc_ref[...] = resc_s[part]

    h_in_specs, h_out_specs = hosted.specs()
    return pl.pallas_call(
        body, name="na_bwd", grid=(NPAIR, nb, 3),
        in_specs=[
            pl.BlockSpec((None, SEQ, 384), lambda p, b, s: (b, 0, p)),
            pl.BlockSpec((None, LC, 384), lambda p, b, s: (b, 0, p)),
            pl.BlockSpec((None, len(NA_CLASSES), 2 * NA_Q, NA_K), lambda p, b, s: (p, 0, 0, 0)),
            pl.BlockSpec((None, SEQ, 128), lambda p, b, s: (b, 0, 4 + p)),
            pl.BlockSpec(memory_space=pl.ANY),
            pl.BlockSpec(memory_space=pl.ANY),
        ] + h_in_specs,
        out_specs=[
            pl.BlockSpec((None, SEQ, 128), lambda p, b, s: (b, 0, 16 + 3 * p + s)),
            pl.BlockSpec((None, LC, 128), lambda p, b, s: (b, 0, 16 + 3 * p + s)),
            pl.BlockSpec((None, 2, 1, 1024), lambda p, b, s: (p, 0, 0, 0)),
        ] + h_out_specs,
        out_shape=[
            jax.ShapeDtypeStruct((nb, SEQ, IN_W), BF16),
            jax.ShapeDtypeStruct((nb, LC, IN_W), BF16),
            jax.ShapeDtypeStruct((NPAIR, 2, 1, 1024), F32),
        ] + hosted.out_shape,
        input_output_aliases={4: 0, 5: 1},
        scratch_shapes=[
            pltpu.VMEM((len(NA_CLASSES), 2 * NA_Q, NA_K), F32),
            pltpu.VMEM((SEQ, 128), F32), pltpu.VMEM((SEQ, 128), F32),
            pltpu.VMEM((LC, 128), F32), pltpu.VMEM((LC, 128), F32),
            pltpu.VMEM((3, SEQ, 128), BF16), pltpu.VMEM((3, LC, 128), BF16),
        ] + hosted.scratch,
        compiler_params=_cp(("arbitrary", "arbitrary", "arbitrary")),
    )(pna, pnac, bias, dmixin, dproj, dprojc, *hosted.args)


def tail_fwd_bwd(x, mixin, tgt, mod3, g_post_mix, g_pre_mlp, g_post_mlp, wout, w1, w2):
    nb = x.shape[0]

    def body(x_ref, mi_ref, tgt_ref, mod_ref, gpm_ref, gpl_ref, gpo_ref, wo_ref, w1_ref, w2_ref,
             dx_ref, dmix_ref, h2_ref, du_ref, a_ref, dm_ref, dmi_ref, dmod_ref, dg_ref, loss_ref):
        b, t = pl.program_id(0), pl.program_id(1)
        gt1, sh2, sc2, gt2 = mod_ref[2:3, :], mod_ref[3:4, :], mod_ref[4:5, :], mod_ref[5:6, :]
        mix = jnp.dot(mi_ref[...], wo_ref[...], preferred_element_type=F32)
        (x1, h2), vjp_a = jax.vjp(_post_mix, x_ref[...], mix, gt1, sc2, sh2, gpm_ref[...], gpl_ref[...])
        h2b = h2.astype(BF16)
        h2_ref[...] = h2b
        m = jnp.zeros((TN, D), F32)
        relus = []
        for j in range(4):
            cols = slice(j * D, (j + 1) * D)
            r = jnp.maximum(jnp.dot(h2b, w1_ref[j], preferred_element_type=F32), 0.0)
            ab = (r * r).astype(BF16)
            a_ref[:, cols] = ab
            m = m + jnp.dot(ab, w2_ref[cols, :], preferred_element_type=F32)
            relus.append(r)
        loss, vjp_b = jax.vjp(_head_loss, x1, m, gt2, gpo_ref[...], tgt_ref[...])
        dx1, dm, dgt2, dgpo, _ = vjp_b(jnp.ones((1, 1), F32))
        dmb = dm.astype(BF16)
        dm_ref[...] = dmb
        dh2 = jnp.zeros((TN, D), F32)
        for j in range(4):
            cols = slice(j * D, (j + 1) * D)
            da = lax.dot_general(dmb, w2_ref[cols, :], (((1,), (1,)), ((), ())), preferred_element_type=F32)
            dub = (da * (2.0 * relus[j])).astype(BF16)
            du_ref[:, cols] = dub
            dh2 = dh2 + lax.dot_general(dub, w1_ref[j], (((1,), (1,)), ((), ())), preferred_element_type=F32)
        dx, dmix, dgt1, dsc2, dsh2, dgpm, dgpl = vjp_a((dx1, dh2))
        dx_ref[...] = dx
        dmixb = dmix.astype(BF16)
        dmix_ref[...] = dmixb
        dmi_ref[...] = lax.dot_general(dmixb, wo_ref[...], (((1,), (1,)), ((), ())),
                                       preferred_element_type=F32).astype(BF16)

        @pl.when(t == 0)
        def _():
            dmod_ref[...] = jnp.zeros_like(dmod_ref)

        @pl.when((t == 0) & (b == 0))
        def _():
            dg_ref[...] = jnp.zeros_like(dg_ref)
            loss_ref[...] = jnp.zeros_like(loss_ref)

        dmod_ref[2:3, :] += dgt1
        dmod_ref[3:4, :] += dsh2
        dmod_ref[4:5, :] += dsc2
        dmod_ref[5:6, :] += dgt2
        dg_ref[0:1, :] += dgpm
        dg_ref[1:2, :] += dgpl
        dg_ref[2:3, :] += dgpo
        loss_ref[...] += jnp.broadcast_to(loss, loss_ref.shape)

    tok = lambda b, t: (b, t, 0)
    const = lambda b, t: (0, 0)
    vec = pl.BlockSpec((1, D), const)
    return pl.pallas_call(
        body, name="tail_fwd_bwd", grid=(nb, SEQ // TN),
        in_specs=[
            pl.BlockSpec((None, TN, D), tok), pl.BlockSpec((None, TN, D), tok), pl.BlockSpec((None, TN, D), tok),
            pl.BlockSpec((None, 6, D), lambda b, t: (b, 0, 0)), vec, vec, vec,
            pl.BlockSpec((D, D), const, pipeline_mode=pl.Buffered(1)),
            pl.BlockSpec((4, D, D), lambda b, t: (0, 0, 0), pipeline_mode=pl.Buffered(1)),
            pl.BlockSpec((DFF, D), const, pipeline_mode=pl.Buffered(1)),
        ],
        out_specs=[
            pl.BlockSpec((None, TN, D), tok), pl.BlockSpec((None, TN, D), tok), pl.BlockSpec((None, TN, D), tok),
            pl.BlockSpec((None, TN, DFF), tok), pl.BlockSpec((None, TN, DFF), tok), pl.BlockSpec((None, TN, D), tok),
            pl.BlockSpec((None, TN, D), tok),
            pl.BlockSpec((None, 6, D), lambda b, t: (b, 0, 0)),
            pl.BlockSpec((8, D), const), pl.BlockSpec((8, 128), const),
        ],
        out_shape=[
            jax.ShapeDtypeStruct((nb, SEQ, D), F32), jax.ShapeDtypeStruct((nb, SEQ, D), BF16),
            jax.ShapeDtypeStruct((nb, SEQ, D), BF16), jax.ShapeDtypeStruct((nb, SEQ, DFF), BF16),
            jax.ShapeDtypeStruct((nb, SEQ, DFF), BF16), jax.ShapeDtypeStruct((nb, SEQ, D), BF16),
            jax.ShapeDtypeStruct((nb, SEQ, D), BF16),
            jax.ShapeDtypeStruct((nb, 6, D), F32), jax.ShapeDtypeStruct((8, D), F32),
            jax.ShapeDtypeStruct((8, 128), F32),
        ],
        compiler_params=_cp(("arbitrary", "arbitrary")),
    )(x, mixin, tgt, mod3, g_post_mix, g_pre_mlp, g_post_mlp, wout, w1, w2)


def weight_grad(pairs, name, out_dtype=F32, col_blocks=False, tm=1024, tn=1024, tk=2048):
    m, n = pairs[0][0].shape[1], pairs[0][1].shape[1]
    tn = min(tn, n)
    tks = [min(tk, xa.shape[0]) for xa, _ in pairs]
    steps = [xa.shape[0] // t for (xa, _), t in zip(pairs, tks)]
    total = sum(steps)
    offs = [sum(steps[:i]) for i in range(len(pairs))]

    def body(*refs):
        out_ref, acc = refs[2 * len(pairs)], refs[-1]
        k = pl.program_id(2)

        @pl.when(k == 0)
        def _():
            acc[...] = jnp.zeros_like(acc)

        for i in range(len(pairs)):
            @pl.when((k >= offs[i]) & (k < offs[i] + steps[i]))
            def _(i=i):
                acc[...] += lax.dot_general(refs[2 * i][...], refs[2 * i + 1][...], (((0,), (0,)), ((), ())),
                                            preferred_element_type=F32)

        if out_dtype != F32:
            @pl.when(k == total - 1)
            def _():
                out_ref[...] = acc[...].astype(out_dtype)

    in_specs, args = [], []
    for i, (xa, ya) in enumerate(pairs):
        clamp = lambda k, i=i: jnp.clip(k - offs[i], 0, steps[i] - 1)
        in_specs.append(pl.BlockSpec((tks[i], tm), lambda a, c, k, clamp=clamp: (clamp(k), a)))
        in_specs.append(pl.BlockSpec((tks[i], tn), lambda a, c, k, clamp=clamp: (clamp(k), c)))
        args += [xa, ya]
    if col_blocks:
        out_spec = pl.BlockSpec((None, tm, tn), lambda a, c, k: (c, a, 0))
        out_shape = jax.ShapeDtypeStruct((n // tn, m, tn), out_dtype)
    else:
        out_spec = pl.BlockSpec((tm, tn), lambda a, c, k: (a, c))
        out_shape = jax.ShapeDtypeStruct((m, n), out_dtype)
    return pl.pallas_call(
        body, name=name, grid=(m // tm, n // tn, total), in_specs=in_specs, out_specs=out_spec, out_shape=out_shape,
        scratch_shapes=[] if out_dtype == F32 else [pltpu.VMEM((tm, tn), F32)],
        compiler_params=_cp(("arbitrary", "arbitrary", "arbitrary")),
    )(*args)


def _perm_block(t):
    return 4 * (t % 4) + t // 4 if t < 16 else 16 + 3 * ((t - 16) % 4) + (t - 16) // 4


def unpack_w_in(blocks):
    def body(i_ref, o_ref):
        for t in range(28):
            p = _perm_block(t)
            o_ref[:, p * 128:(p + 1) * 128] = i_ref[t // 7, :, (t % 7) * 128:(t % 7 + 1) * 128]

    return pl.pallas_call(
        body, name="unpack_w_in", grid=(2,),
        in_specs=[pl.BlockSpec((4, D // 2, 896), lambda i: (0, i, 0))],
        out_specs=pl.BlockSpec((D // 2, IN_W), lambda i: (i, 0)),
        out_shape=jax.ShapeDtypeStruct((D, IN_W), BF16),
    )(blocks)


def pack_w_in(dw):
    def body(i_ref, o_ref):
        for t in range(28):
            p = _perm_block(t)
            o_ref[t // 7, :, (t % 7) * 128:(t % 7 + 1) * 128] = i_ref[:, p * 128:(p + 1) * 128].astype(BF16)

    return pl.pallas_call(
        body, name="pack_w_in", grid=(4,),
        in_specs=[pl.BlockSpec((D // 4, IN_W), lambda i: (i, 0))],
        out_specs=pl.BlockSpec((4, D // 4, 896), lambda i: (0, i, 0)),
        out_shape=jax.ShapeDtypeStruct((4, D, 896), BF16),
    )(dw)


def _place():
    return lax.axis_index("x"), lax.axis_index("y"), lax.axis_index("c")


class Hosted:
    def __init__(self, args, out_shape, scratch, start, finish):
        self.args, self.out_shape, self.scratch, self.start, self.finish = args, out_shape, scratch, start, finish

    def specs(self):
        hbm = pl.BlockSpec(memory_space=pl.ANY)
        return [hbm] * len(self.args), [hbm] * len(self.out_shape)

    def split(self, refs, n_in, n_out):
        a, b = len(self.args), len(self.out_shape)
        cuts = [n_in, n_in + a, n_in + a + n_out, n_in + a + n_out + b, len(refs) - len(self.scratch)]
        parts = [refs[i:j] for i, j in zip([0] + cuts, cuts + [len(refs)])]
        return parts[0], parts[1], parts[2], parts[3], parts[4], parts[5]


def no_exchange():
    return Hosted([], [], [], lambda *a: None, lambda *a: None)


def run_hosted(hosted, name):
    def body(*refs):
        _, ins, _, outs, _, sems = hosted.split(refs, 0, 0)
        hosted.start(ins, outs, sems)
        hosted.finish(ins, outs, sems)

    in_specs, out_specs = hosted.specs()
    return pl.pallas_call(body, name=name, in_specs=in_specs, out_specs=out_specs, out_shape=hosted.out_shape,
                          scratch_shapes=hosted.scratch)(*hosted.args)


def gather8(blocks):
    na = len(blocks)

    def copies(ins, outs, sems):
        send_sems, recv_sems, local_sem = sems
        x, y, c = _place()
        me, sibling = (x, y, c), (x, y, 1 - c)
        chips = [(1 - x, y), (x, 1 - y), (1 - x, 1 - y)]

        def slot(o_ref, px, py, pc):
            return o_ref.at[4 * px + 2 * py + pc]

        def copy(a, k, block, to, src=None):
            return pltpu.make_async_remote_copy(
                src_ref=slot(outs[a], *block) if src is None else src, dst_ref=slot(outs[a], *block),
                send_sem=send_sems.at[a, k], recv_sem=recv_sems.at[a, k], device_id=to, device_id_type=MESH)

        mine = [pltpu.make_async_copy(ins[a], slot(outs[a], *me), local_sem.at[a]) for a in range(na)]
        first = []
        for a in range(na):
            first.append(copy(a, 0, me, sibling, src=ins[a]))
            first += [copy(a, 1 + j, me, (*chip, c), src=ins[a]) for j, chip in enumerate(chips)]
        return copy, mine, first, me, sibling, chips, c

    def start(ins, outs, sems):
        _, mine, first, *_ = copies(ins, outs, sems)
        for cp in mine + first:
            cp.start()

    def finish(ins, outs, sems):
        copy, mine, first, me, sibling, chips, c = copies(ins, outs, sems)
        passed = []
        for j, chip in enumerate(chips):
            for a in range(na):
                copy(a, 1 + j, (*chip, c), me).wait_recv()
                cp = copy(a, 4 + j, (*chip, c), sibling)
                cp.start()
                passed.append(cp)
        for a in range(na):
            copy(a, 0, sibling, me).wait_recv()
            for j, chip in enumerate(chips):
                copy(a, 4 + j, (*chip, 1 - c), me).wait_recv()
        for cp in first + passed:
            cp.wait_send()
        for cp in mine:
            cp.wait()

    return Hosted(list(blocks), [jax.ShapeDtypeStruct((8,) + b.shape, b.dtype) for b in blocks],
                  [pltpu.SemaphoreType.DMA((na, 7)), pltpu.SemaphoreType.DMA((na, 7)), pltpu.SemaphoreType.DMA((na,))],
                  start, finish)


def chips3(arrays):
    na = len(arrays)

    def copies(ins, outs, sems):
        send_sems, recv_sems = sems
        x, y, c = _place()
        return [pltpu.make_async_remote_copy(
            src_ref=ins[a].at[2 * px + py], dst_ref=outs[a].at[k], send_sem=send_sems.at[a, k],
            recv_sem=recv_sems.at[a, k], device_id=(px, py, c), device_id_type=MESH)
            for a in range(na) for k, (px, py) in enumerate([(1 - x, y), (x, 1 - y), (1 - x, 1 - y)])]

    def start(ins, outs, sems):
        for cp in copies(ins, outs, sems):
            cp.start()

    def finish(ins, outs, sems):
        for cp in copies(ins, outs, sems):
            cp.wait()

    return Hosted(list(arrays), [jax.ShapeDtypeStruct((3,) + a.shape[1:], a.dtype) for a in arrays],
                  [pltpu.SemaphoreType.DMA((na, 3)), pltpu.SemaphoreType.DMA((na, 3))], start, finish)


def siblings(arrays):
    na = len(arrays)

    def copies(ins, outs, sems):
        send_sems, recv_sems = sems
        x, y, c = _place()
        return [pltpu.make_async_remote_copy(
            src_ref=ins[a], dst_ref=outs[a], send_sem=send_sems.at[a], recv_sem=recv_sems.at[a],
            device_id=(x, y, 1 - c), device_id_type=MESH) for a in range(na)]

    def start(ins, outs, sems):
        for cp in copies(ins, outs, sems):
            cp.start()

    def finish(ins, outs, sems):
        for cp in copies(ins, outs, sems):
            cp.wait()

    return Hosted(list(arrays), [jax.ShapeDtypeStruct(a.shape, a.dtype) for a in arrays],
                  [pltpu.SemaphoreType.DMA((na,)), pltpu.SemaphoreType.DMA((na,))], start, finish)


def both(first, second):
    na, no, ns = len(first.args), len(first.out_shape), len(first.scratch)

    def start(ins, outs, sems):
        first.start(ins[:na], outs[:no], sems[:ns])
        second.start(ins[na:], outs[no:], sems[ns:])

    def finish(ins, outs, sems):
        first.finish(ins[:na], outs[:no], sems[:ns])
        second.finish(ins[na:], outs[no:], sems[ns:])

    return Hosted(first.args + second.args, first.out_shape + second.out_shape, first.scratch + second.scratch,
                  start, finish)


def siblings4(arrays):
    na = len(arrays)

    def copies(ins, outs, sems):
        send_sems, recv_sems = sems
        x, y, c = _place()
        return [pltpu.make_async_remote_copy(
            src_ref=ins[a].at[2 * j + 1 - c], dst_ref=outs[a].at[j],
            send_sem=send_sems.at[a, j], recv_sem=recv_sems.at[a, j],
            device_id=(x, y, 1 - c), device_id_type=MESH) for a in range(na) for j in range(4)]

    def start(ins, outs, sems):
        for cp in copies(ins, outs, sems):
            cp.start()

    def finish(ins, outs, sems):
        for cp in copies(ins, outs, sems):
            cp.wait()

    return Hosted(list(arrays), [jax.ShapeDtypeStruct((4,) + a.shape[1:], a.dtype) for a in arrays],
                  [pltpu.SemaphoreType.DMA((na, 4)), pltpu.SemaphoreType.DMA((na, 4))], start, finish)


def sibling_blocks(arrays, name):
    return run_hosted(siblings4(arrays), name)


def _row_tile(r):
    for cand in (512, 256, 128, 64, 32, 16, 8):
        if r % cand == 0:
            return cand
    return r


def chip_partial(place, g8s, landed4s, name):
    n = len(g8s)

    def body(place_ref, *refs):
        del place_ref
        for g_ref, l_ref, o_ref in zip(refs[:n], refs[n:2 * n], refs[2 * n:]):
            o_ref[...] = (g_ref[...].astype(F32) + l_ref[...].astype(F32)).astype(BF16)

    own = [pl.BlockSpec((None,) + g.shape[1:], lambda j, s: (2 * j + s[0], 0, 0)) for g in g8s]
    plain = [pl.BlockSpec((None,) + g.shape[1:], lambda j, s: (j, 0, 0)) for g in g8s]
    return pl.pallas_call(
        body, name=name,
        grid_spec=pltpu.PrefetchScalarGridSpec(num_scalar_prefetch=1, grid=(4,), in_specs=own + plain, out_specs=plain),
        out_shape=[jax.ShapeDtypeStruct((4,) + g.shape[1:], BF16) for g in g8s],
    )(place, *g8s, *landed4s)


def shard_sum(place, partial4s, landed3s, name):
    n = len(partial4s)

    def body(place_ref, *refs):
        del place_ref
        for p_ref, l_ref, o_ref in zip(refs[:n], refs[n:2 * n], refs[2 * n:]):
            acc = p_ref[...].astype(F32)
            for k in range(3):
                acc = acc + l_ref[k].astype(F32)
            o_ref[...] = acc

    def halves(p, lead):
        r, ccols = p.shape[1:]
        return (lead, r // 2, ccols)

    return pl.pallas_call(
        body, name=name,
        grid_spec=pltpu.PrefetchScalarGridSpec(
            num_scalar_prefetch=1, grid=(2,),
            in_specs=[pl.BlockSpec(halves(p, None), lambda i, s: (s[1], i, 0)) for p in partial4s]
            + [pl.BlockSpec(halves(p, 3), lambda i, s: (0, i, 0)) for p in partial4s],
            out_specs=[pl.BlockSpec(halves(p, None)[1:], lambda i, s: (i, 0)) for p in partial4s]),
        out_shape=[jax.ShapeDtypeStruct(p.shape[1:], F32) for p in partial4s],
    )(place, *partial4s, *landed3s)


def _adamw_math(w, g, m, v):
    m2 = B1 * m + (1.0 - B1) * g
    v2 = B2 * v + (1.0 - B2) * (g * g)
    m_hat = m2 / (1.0 - B1 ** STEP)
    v_hat = v2 / (1.0 - B2 ** STEP)
    return -LR * (m_hat / (jnp.sqrt(v_hat) + AEPS) + WD * w), m2, v2


def adamw_halves(place, w, mine, theirs, m, v, name):
    r, ccols = w.shape
    hr = r // 2
    tr = _row_tile(hr)
    nt = hr // tr

    def body(place_ref, w_ref, a_ref, b_ref, m_ref, v_ref, g_out, d_out, m_out, v_out):
        g = jnp.where(pl.program_id(0) == place_ref[0], a_ref[...], b_ref[...])
        d, m2, v2 = _adamw_math(w_ref[...], g, m_ref[...], v_ref[...])
        g_out[...] = g
        d_out[...] = d
        m_out[...] = m2
        v_out[...] = v2

    full = pl.BlockSpec((tr, ccols), lambda h, i, s: (h * nt + i, 0))
    part = pl.BlockSpec((tr, ccols), lambda h, i, s: (i, 0))
    return pl.pallas_call(
        body, name=name,
        grid_spec=pltpu.PrefetchScalarGridSpec(
            num_scalar_prefetch=1, grid=(2, nt), in_specs=[full, part, part, full, full], out_specs=[full] * 4),
        out_shape=[jax.ShapeDtypeStruct((r, ccols), F32)] * 4,
    )(place, w, mine, theirs, m, v)


def adamw_group(place, halved, plain, hosted, name):
    rows = halved[0][0].shape[0]
    tr = 64
    nt = rows // 2 // tr
    nh, npl = len(halved), len(plain)

    def body(place_ref, *refs):
        own_in, h_in, own_out, h_out, _, h_sems = hosted.split(refs, 5 * nh + 4 * npl, 4 * nh + 3 * npl)
        half = pl.program_id(0)
        grid_step = half * nt + pl.program_id(1)

        @pl.when(grid_step == 0)
        def _():
            hosted.start(h_in, h_out, h_sems)

        for i in range(nh):
            w_ref, a_ref, b_ref, m_ref, v_ref = own_in[5 * i:5 * i + 5]
            g = jnp.where(half == place_ref[0], a_ref[...], b_ref[...])
            res = (g,) + _adamw_math(w_ref[...], g, m_ref[...], v_ref[...])
            for o_ref, r in zip(own_out[4 * i:4 * i + 4], res):
                o_ref[...] = r
        for i in range(npl):
            w_ref, g_ref, m_ref, v_ref = own_in[5 * nh + 4 * i:5 * nh + 4 * i + 4]
            res = _adamw_math(w_ref[...], g_ref[...], m_ref[...], v_ref[...])
            for o_ref, r in zip(own_out[4 * nh + 3 * i:4 * nh + 3 * i + 3], res):
                o_ref[...] = r

        @pl.when(grid_step == 2 * nt - 1)
        def _():
            hosted.finish(h_in, h_out, h_sems)

    def full(cols):
        return pl.BlockSpec((tr, cols), lambda h, i, s: (h * nt + i, 0))

    def part(cols):
        return pl.BlockSpec((tr, cols), lambda h, i, s: (i, 0))

    in_specs, out_specs, out_shape, args = [], [], [], []
    for w, a, b, m, v in halved:
        cols = w.shape[1]
        in_specs += [full(cols), part(cols), part(cols), full(cols), full(cols)]
        out_specs += [full(cols)] * 4
        out_shape += [jax.ShapeDtypeStruct(w.shape, F32)] * 4
        args += [w, a, b, m, v]
    for w, g, m, v in plain:
        cols = w.shape[1]
        in_specs += [full(cols)] * 4
        out_specs += [full(cols)] * 3
        out_shape += [jax.ShapeDtypeStruct(w.shape, F32)] * 3
        args += [w, g, m, v]
    h_in_specs, h_out_specs = hosted.specs()
    return pl.pallas_call(
        body, name=name,
        grid_spec=pltpu.PrefetchScalarGridSpec(
            num_scalar_prefetch=1, grid=(2, nt), in_specs=in_specs + h_in_specs, out_specs=out_specs + h_out_specs,
            scratch_shapes=hosted.scratch),
        out_shape=out_shape + hosted.out_shape,
        compiler_params=_cp(("arbitrary", "arbitrary")),
    )(place, *args, *hosted.args)


def _silu(x):
    return x * jax.nn.sigmoid(x)


def prologue(c_rows, c_ctx_row, w_ada, b_shard, half_w_in, late_shards):
    shape = jax.ShapeDtypeStruct
    n_late = len(late_shards)
    half_shapes = [(w.shape[0] // 2, w.shape[1]) for w in late_shards]
    g_w = gather8([half_w_in])
    g_c = gather8([shape((8, D), F32)])
    g_m = gather8([shape((32, 1536), F32)])

    def body(*refs):
        c_ref, cc_ref, w_ref, b_ref, hw_ref = refs[:5]
        late_refs = refs[5:5 + n_late]
        cin_ref, mg_ref, gw_ref, cos_ref, sin_ref = refs[5 + n_late:10 + n_late]
        rest = refs[10 + n_late:]
        half_refs, (cg_s, ms_s) = rest[:n_late], rest[n_late:n_late + 2]
        stage, load_sem, sems = rest[n_late + 2:2 * n_late + 2], rest[2 * n_late + 2], rest[2 * n_late + 3:]
        sw, sc, sm = sems[0:3], sems[3:6], sems[6:9]
        core = lax.axis_index("c")
        g_w.start([hw_ref], [gw_ref], sw)
        g_c.start([c_ref], [cg_s], sc)
        loads = [pltpu.make_async_copy(late_refs[a].at[pl.ds(core * half_shapes[a][0], half_shapes[a][0]), :],
                                       stage[a], load_sem.at[a]) for a in range(n_late)]
        for cp in loads:
            cp.start()
        g_c.finish([c_ref], [cg_s], sc)
        cin_ref[...] = jnp.zeros_like(cin_ref)
        for dev in range(8):
            cin_ref[2 * dev:2 * dev + 2, :] = cg_s[dev, 0:2, :]
        cin_ref[16:17, :] = cc_ref[...]
        ms_s[...] = _nn(_silu(cin_ref[...]), w_ref[...]) + b_ref[...]
        g_m.start([ms_s], [mg_ref], sm)
        for a, cp in enumerate(loads):
            cp.wait()
            half_refs[a][...] = stage[a][...].astype(BF16)
        cos_ref[...], sin_ref[...] = _rope_tables()
        g_m.finish([ms_s], [mg_ref], sm)
        g_w.finish([hw_ref], [gw_ref], sw)

    vmem = pl.BlockSpec(memory_space=pltpu.VMEM)
    hbm = pl.BlockSpec(memory_space=pl.ANY)
    return pl.pallas_call(
        body, name="prologue", in_specs=[vmem, vmem, vmem, vmem, hbm] + [hbm] * n_late,
        out_specs=[vmem, vmem, hbm, vmem, vmem] + [vmem] * n_late,
        out_shape=[shape((32, D), F32), shape((8, 32, 1536), F32)] + g_w.out_shape
        + [shape((SEQ, RD), F32)] * 2 + [shape(s, BF16) for s in half_shapes],
        scratch_shapes=[pltpu.VMEM((8, 8, D), F32), pltpu.VMEM((32, 1536), F32)]
        + [pltpu.VMEM(s, F32) for s in half_shapes] + [pltpu.SemaphoreType.DMA((n_late,))]
        + g_w.scratch + g_c.scratch + g_m.scratch,
        compiler_params=_cp(),
    )(c_rows, c_ctx_row, w_ada, b_shard, half_w_in, *late_shards)


def ada_grads(cin, gb, gc, w_ada):
    def body(c_ref, gb_ref, gc_ref, w_ref, gw_ref, pc_ref):
        ctx_tot = jnp.sum(gc_ref[...], axis=0, keepdims=True)
        rows = lax.broadcasted_iota(jnp.int32, (16, 512), 0)
        dm = jnp.concatenate([gb_ref[...], jnp.where(rows == 0, ctx_tot, 0.0)], axis=0)
        gw_ref[...] = _tn(_silu(c_ref[...]), dm)
        rows8 = lax.broadcasted_iota(jnp.int32, (8, 512), 0)
        part = _nt(jnp.where(rows8 == 0, ctx_tot, 0.0), w_ref[...])

        @pl.when(pl.program_id(0) == 0)
        def _():
            pc_ref[...] = jnp.zeros_like(pc_ref)

        pc_ref[...] += part

    return pl.pallas_call(
        body, name="ada_grads", grid=(3,),
        in_specs=[pl.BlockSpec((32, D), lambda j: (0, 0)), pl.BlockSpec((16, 512), lambda j: (0, j)),
                  pl.BlockSpec((8, 512), lambda j: (0, j)), pl.BlockSpec((D, 512), lambda j: (0, j))],
        out_specs=[pl.BlockSpec((D, 512), lambda j: (0, j)), pl.BlockSpec((8, D), lambda j: (0, 0))],
        out_shape=[jax.ShapeDtypeStruct((D, 1536), F32), jax.ShapeDtypeStruct((8, D), F32)],
    )(cin, gb, gc, w_ada)


SMALL_SUM_ROWS = 15


def small_update(gsm, gbf, gcf, pcg, params):
    n = len(params)

    def body(*refs):
        gsm_ref, gbf_ref, gcf_ref, pcg_ref = refs[:4]
        wmv, outs, loss_out = refs[4:4 + 3 * n], refs[4 + 3 * n:4 + 7 * n], refs[-1]
        acc = gsm_ref[0]
        for dev in range(1, 8):
            acc = acc + gsm_ref[dev]
        c_ctx = wmv[0][...]
        sg = jax.nn.sigmoid(c_ctx)
        dsilu = pcg_ref[0:1, :] + pcg_ref[2:3, :] + pcg_ref[4:5, :] + pcg_ref[6:7, :]
        lane = lax.broadcasted_iota(jnp.int32, (1, D), 1)
        last = acc[14:15, :]
        grads = [
            dsilu * (sg * (1.0 + c_ctx * (1.0 - sg))),
            jnp.sum(gbf_ref[...], axis=0, keepdims=True) + jnp.sum(gcf_ref[...], axis=0, keepdims=True),
            acc[0:1, :] + acc[1:2, :], acc[2:3, :], acc[3:4, :], acc[4:5, :],
            acc[5:6, 0:512], acc[6:14, :], jnp.where(lane < 8, last, 0.0),
        ]
        loss_out[...] = jnp.broadcast_to(jnp.sum(jnp.where(lane == 8, last, 0.0), axis=1, keepdims=True), (8, 128))
        for i, g in enumerate(grads):
            d, m2, v2 = _adamw_math(wmv[3 * i][...], g, wmv[3 * i + 1][...], wmv[3 * i + 2][...])
            outs[4 * i][...] = g
            outs[4 * i + 1][...] = d
            outs[4 * i + 2][...] = m2
            outs[4 * i + 3][...] = v2

    flat = [a for wmv in params for a in wmv]
    out_shape = [jax.ShapeDtypeStruct(w.shape, F32) for w, _, _ in params for _ in range(4)]
    return pl.pallas_call(
        body, name="small_update", out_shape=out_shape + [jax.ShapeDtypeStruct((8, 128), F32)],
    )(gsm, gbf, gcf, pcg, *flat)


def _pad_row(v, rows):
    flat = v.reshape(-1)
    return jnp.pad(flat, (0, rows * D - flat.shape[0])).reshape(rows, D)


def local_step(x, ctx, tgt, mod3, rope, g_pre_mix, g_post_mix, g_pre_mlp, g_post_mlp, ret_decay, ret_gn, na_rpb,
               wperm, late_weights, early_grads):
    nb = x.shape[0]
    tokens = nb * SEQ
    cos, sin = rope
    rd = ret_decay.T.reshape(RH, 2, 1)
    gn = ret_gn.reshape(RH, 1, RD)
    bias = na_bias_table(_rpb_flat(na_rpb))
    h, pret, pna = premix_proj(x, mod3, g_pre_mix, wperm, False, "premix_proj")
    hc, pretc, pnac = premix_proj(ctx, mod3, g_pre_mix, wperm, True, "premix_proj_ctx")
    o_all, mixin, gw_out = retention_fwd(pret, pretc, rd, gn, cos, sin, late_weights(0))
    mixin, gw1, gw2 = na_fwd(pna, pnac, bias, mixin, late_weights(1))
    dx_tail, dmix, h2, du, act, dm, dmixin, dmod_t, dg_t, loss_t = tail_fwd_bwd(
        x, mixin, tgt, mod3, g_post_mix, g_pre_mlp, g_post_mlp, gw_out.reshape(D, D), gw1.reshape(4, D, D),
        gw2.reshape(DFF, D))
    dw_out = weight_grad([(mixin.reshape(tokens, D), dmix.reshape(tokens, D))], "grad_w_out", BF16)
    dw1 = weight_grad([(h2.reshape(tokens, D), du.reshape(tokens, DFF))], "grad_w_mlp1", BF16, col_blocks=True)
    dw2 = weight_grad([(act.reshape(tokens, DFF), dm.reshape(tokens, D))], "grad_w_mlp2", BF16)
    dproj, dprojc, drd, dgn, *landed = retention_bwd(pret, pretc, o_all, dmixin, rd, gn, cos, sin,
                                                     early_grads[0](dw_out, dw1, dw2))
    dproj, dprojc, dpat, *early = na_bwd(pna, pnac, bias, dmixin, dproj, dprojc, early_grads[1](landed))
    dw_in = weight_grad([(h.reshape(tokens, D), dproj.reshape(tokens, IN_W)),
                         (hc.reshape(nb * LC, D), dprojc.reshape(nb * LC, IN_W))], "grad_w_in", tn=IN_W // 2, tk=1024)
    grad_x, dmod_a, dg_a, *late = premix_bwd(x, mod3, g_pre_mix, wperm, dproj, dx_tail, early_grads[2](dw_in),
                                             "premix_bwd")
    dmod_c, dg_c = premix_bwd(ctx, mod3, g_pre_mix, wperm, dprojc, None, no_exchange(), "premix_bwd_ctx")
    dmod = jnp.concatenate([jnp.concatenate([dmod_a[:, 0:2], dmod_t[:, 2:6]], axis=1), dmod_c], axis=0)
    last = jnp.pad(jnp.concatenate([drd[:, :, 0].T.reshape(8), loss_t[0, 0:1]]), (0, D - 9)).reshape(1, D)
    small = jnp.concatenate([dg_a[0:1], dg_c[0:1], dg_t[0:3], _pad_row(dgn, 1), dpat.reshape(8, D), last], axis=0)
    return grad_x, late, early, dmod, small


def kernel(x, c, ctx, c_ctx, w_ada, b_ada, g_pre_mix, g_post_mix, g_pre_mlp, g_post_mlp, w_in, ret_decay, ret_gn, na_rpb, w_out, w_mlp1, w_mlp2, loss_target, m_c_ctx, m_w_ada, m_b_ada, m_g_pre_mix, m_g_post_mix, m_g_pre_mlp, m_g_post_mlp, m_w_in, m_ret_decay, m_ret_gn, m_na_rpb, m_w_out, m_w_mlp1, m_w_mlp2, v_c_ctx, v_w_ada, v_b_ada, v_g_pre_mix, v_g_post_mix, v_g_pre_mlp, v_g_post_mlp, v_w_in, v_ret_decay, v_ret_gn, v_na_rpb, v_w_out, v_w_mlp1, v_w_mlp2):
    px, py, pc = _place()
    dev = 4 * px + 2 * py + pc
    chip = 2 * px + py

    half_w_in = lax.dynamic_slice_in_dim(w_in[0], pc * (D // 2), D // 2, 0).astype(BF16)
    cin, mg, gw_in, cos, sin, *late_halves = prologue(
        jnp.pad(c, ((0, 6), (0, 0))), c_ctx[None], w_ada[0], lax.dynamic_slice_in_dim(b_ada, chip * 1536, 1536, 1),
        half_w_in, [w_out[0], w_mlp1[0], w_mlp2[0]])
    halves = [half_w_in] + late_halves
    wperm = unpack_w_in(gw_in.reshape(4, D, 896))
    mod_all = jnp.concatenate([mg[0], mg[2], mg[4], mg[6]], axis=1)
    mod3 = (jnp.pad(lax.dynamic_slice_in_dim(mod_all, 2 * dev, 2, 0), ((0, 1), (0, 0)))
            + jnp.pad(mod_all[16:17], ((2, 0), (0, 0)))).reshape(3, 6, D)

    place = jnp.stack([pc, chip]).astype(jnp.int32)

    early_names = ["w_out", "w_mlp1", "w_mlp2"]
    early_g8, early_partial = [], []

    def early_a(dw_out, dw1, dw2):
        early_g8[:] = [dw_out.reshape(8, 128, D), dw1.reshape(8, 512, D), dw2.reshape(8, 512, D)]
        return siblings4(early_g8)

    def early_b(landed):
        early_partial[:] = chip_partial(place, early_g8, landed, "rs_chip_sum_early")
        return chips3(early_partial)

    late_partial = []

    def late_c(dw_in):
        g8_in = pack_w_in(dw_in).reshape(8, 512, 896)
        (landed_in,) = sibling_blocks([g8_in], "rs_sibling_w_in")
        late_partial[:] = chip_partial(place, [g8_in], [landed_in], "rs_chip_sum_w_in")
        return chips3(late_partial)

    grad_x, (landed3_in,), early_landed, dmod, small = local_step(
        x, ctx, loss_target, mod3, (cos, sin), g_pre_mix, g_post_mix, g_pre_mlp, g_post_mlp, ret_decay[0], ret_gn,
        na_rpb[0],
        wperm, lambda k: gather8(halves[1:2] if k == 0 else halves[2:4]), (early_a, early_b, late_c))
    early_mine = shard_sum(place, early_partial, early_landed, "rs_shard_sum_early")

    pay = jnp.concatenate([dmod.reshape(18, D), small, jnp.zeros((40 - 18 - SMALL_SUM_ROWS, D), F32)], axis=0)
    *early_theirs, gs = run_hosted(both(siblings(early_mine), gather8([pay])), "rs_halves_early_gather_small")
    gbf = gs[:, 0:12].reshape(16, 6 * D)
    gcf = gs[:, 12:18].reshape(8, 6 * D)
    gw_ada, pc_part = ada_grads(cin, lax.dynamic_slice_in_dim(gbf, chip * 1536, 1536, 1),
                                lax.dynamic_slice_in_dim(gcf, chip * 1536, 1536, 1), w_ada[0])
    (mine_in,) = shard_sum(place, late_partial, [landed3_in], "rs_shard_sum_w_in")
    theirs_in, pcg = run_hosted(both(siblings([mine_in]), gather8([pc_part])), "rs_halves_w_in_gather_c_ctx")

    grouped = adamw_group(
        place,
        [(w_mlp1[0], early_mine[1], early_theirs[1], m_w_mlp1[0], v_w_mlp1[0]),
         (w_mlp2[0], early_mine[2], early_theirs[2], m_w_mlp2[0], v_w_mlp2[0])],
        [(w_ada[0], gw_ada, m_w_ada[0], v_w_ada[0])], no_exchange(), "adamw_group")
    d_ada, m_ada, v_ada = grouped[8:11]
    big = [
        [r[None] for r in adamw_halves(place, w_in[0], mine_in, theirs_in, m_w_in[0], v_w_in[0], "adamw_w_in")],
        [r[None] for r in adamw_halves(place, w_out[0], early_mine[0], early_theirs[0], m_w_out[0], v_w_out[0],
                                       "adamw_w_out")],
        [r[None] for r in grouped[0:4]], [r[None] for r in grouped[4:8]],
    ]

    def rpb_rows(t):
        return _rpb_flat(t[0]).reshape(8, D)

    def decay_row(t):
        return jnp.pad(t.reshape(1, 8), ((0, 0), (0, D - 8)))

    views = [lambda t: t.reshape(1, D), lambda t: t, lambda t: t, lambda t: t, lambda t: t, lambda t: t, lambda t: t,
             rpb_rows, decay_row]
    back = [lambda t: t.reshape(D), lambda t: t, lambda t: t, lambda t: t, lambda t: t, lambda t: t, lambda t: t,
            lambda t: _rpb_flat_t(t)[None], lambda t: t[:, 0:8].reshape(1, 2, 4)]
    small_w = (c_ctx, b_ada, g_pre_mix, g_post_mix, g_pre_mlp, g_post_mlp, ret_gn, na_rpb, ret_decay)
    small_m = (m_c_ctx, m_b_ada, m_g_pre_mix, m_g_post_mix, m_g_pre_mlp, m_g_post_mlp, m_ret_gn, m_na_rpb, m_ret_decay)
    small_v = (v_c_ctx, v_b_ada, v_g_pre_mix, v_g_post_mix, v_g_pre_mlp, v_g_post_mlp, v_ret_gn, v_na_rpb, v_ret_decay)
    *res, loss8 = small_update(gs[:, 18:18 + SMALL_SUM_ROWS], gbf, gcf, pcg[:, 0],
                               [(f(w), f(m), f(v)) for f, w, m, v in zip(views, small_w, small_m, small_v)])

    def leaves(ada, idx):
        s_c, s_b, s_g1, s_g2, s_g3, s_g4, s_gn, s_rpb, s_rd = [back[i](res[4 * i + idx]) for i in range(9)]
        return [s_c, ada[None], s_b, s_g1, s_g2, s_g3, s_g4, big[0][idx], s_rd, s_gn, s_rpb,
                big[1][idx], big[2][idx], big[3][idx]]

    return (loss8[0, 0], grad_x, *leaves(gw_ada, 0), *leaves(d_ada, 1), *leaves(m_ada, 2), *leaves(v_ada, 3))
```

```python
import functools
import math

import jax
import jax.numpy as jnp
from jax import lax
from jax.experimental import pallas as pl
from jax.experimental.pallas import tpu as pltpu

F32, BF16 = jnp.float32, jnp.bfloat16
D = 1024
SEQ = 2048
LC = 256
GW = 64
RH, RD, CH = 4, 128, 128
NPAIR = 4
IN_W = 3584
RET_W = 2048
DFF = 4096
EPS = 1e-6
NEG = -1e30
TN = 256
NCH = SEQ // CH
LR, B1, B2, AEPS, WD, STEP = 0.001, 0.9, 0.999, 1e-08, 0.01, 10
MESH = pl.DeviceIdType.MESH
VMEM_LIMIT = 56 * 1024 * 1024


def _cp(sem=None):
    return pltpu.CompilerParams(dimension_semantics=sem, vmem_limit_bytes=VMEM_LIMIT)


def _nn(a, b):
    return jnp.dot(a.astype(BF16), b.astype(BF16), preferred_element_type=F32)


def _nt(a, b):
    return lax.dot_general(a.astype(BF16), b.astype(BF16), (((1,), (1,)), ((), ())), preferred_element_type=F32)


def _tn(a, b):
    return lax.dot_general(a.astype(BF16), b.astype(BF16), (((0,), (0,)), ((), ())), preferred_element_type=F32)


@jax.custom_vjp
def mm_tn(a, b):
    return _tn(a, b)


mm_tn.defvjp(lambda a, b: (_tn(a, b), (a, b)), lambda r, g: (_nt(r[1], g), _nn(r[0], g)))


def _rms(x):
    return x * lax.rsqrt(jnp.mean(x * x, axis=-1, keepdims=True) + EPS)


def _rms_mod(x, g, sc, sh):
    return (_rms(x) * g) * (1.0 + sc) + sh


def _post_mix(x, mix, gt1, sc2, sh2, g_post_mix, g_pre_mlp):
    x1 = x + gt1 * (_rms(mix) * g_post_mix)
    return x1, _rms_mod(x1, g_pre_mlp, sc2, sh2)


def _head_loss(x1, m, gt2, g_post_mlp, tgt):
    err = x1 + gt2 * (_rms(m) * g_post_mlp) - tgt
    return 0.5 * jnp.sum(jnp.mean(err * err, axis=-1, keepdims=True), axis=0, keepdims=True)


def _ln_gate(o, g, w):
    mu = jnp.mean(o, axis=-1, keepdims=True)
    var = jnp.mean(jnp.square(o - mu), axis=-1, keepdims=True)
    y = (o - mu) * lax.rsqrt(var + EPS)
    return (y * w) * (g * jax.nn.sigmoid(g))


def _swap32(x):
    lane = lax.broadcasted_iota(jnp.int32, x.shape, 1)
    return jnp.where((lane & 32) == 0, pltpu.roll(x, 96, 1), pltpu.roll(x, 32, 1))


def _rope(x, cos, sin):
    return x * cos + _swap32(x) * sin


def _rope_t(g, cos, sin):
    return g * cos + _swap32(g * sin)


def _rope_tables():
    tok = lax.broadcasted_iota(jnp.int32, (SEQ, RD), 0)
    lane = lax.broadcasted_iota(jnp.int32, (SEQ, RD), 1)
    pos = jnp.where(lane < 64, tok >> 6, tok & (GW - 1)).astype(F32)
    ang = pos * jnp.exp((lane & 31).astype(F32) * (-math.log(10000.0) / 32))
    return jnp.cos(ang), jnp.where((lane & 32) == 0, -jnp.sin(ang), jnp.sin(ang))


def _chunk_loop(n, body, init, k=4):
    def several(t, carry):
        for i in range(k):
            carry = body(k * t + i, carry)
        return carry

    return lax.fori_loop(0, n // k, several, init)


def _fiota(shape, dim):
    return lax.broadcasted_iota(jnp.int32, shape, dim).astype(F32)


def _ret_state(k, v, s, lg, reverse):
    pos = _fiota((CH, 1), 0)
    b_exp = pos if reverse else (CH - 1.0 - pos)
    return jnp.exp(lg * CH) * s + mm_tn(k * jnp.exp(lg * b_exp), v)


class _Decays:
    def __init__(self, lgs):
        i, j, pos = _fiota((CH, CH), 0), _fiota((CH, CH), 1), _fiota((CH, 1), 0)
        diffs = (i - j, j - i)
        keep = (diffs[0] >= 0, diffs[1] > 0)
        mats = [jnp.where(m, jnp.exp(lg * jnp.where(m, d, 0.0)), 0.0) for lg, d, m in zip(lgs, diffs, keep)]
        self.mask = mats[0] + mats[1]
        self.dmask = [mats[0] * diffs[0], mats[1] * diffs[1]]
        a_exp, b_exp = (pos + 1.0, CH - pos), (CH - 1.0 - pos, pos)
        self.a = [jnp.exp(lg * e) for lg, e in zip(lgs, a_exp)]
        self.b = [jnp.exp(lg * e) for lg, e in zip(lgs, b_exp)]
        self.da = [a * e for a, e in zip(self.a, a_exp)]
        self.db = [b * e for b, e in zip(self.b, b_exp)]
        self.g = [jnp.exp(lg * CH) for lg in lgs]


def _both(x, w):
    return jnp.concatenate([x * w[0], x * w[1]], axis=1)


def _total(x):
    return jnp.sum(jnp.sum(x, axis=1, keepdims=True), axis=0, keepdims=True)


def _state_pass(dec, init, k_s, v_of, st_s):
    def step(t, carry):
        out = []
        for d, s in enumerate(carry):
            n = (NCH - 1 - t) if d else t
            sl = pl.ds(pl.multiple_of(n * CH, CH), CH)
            st_s[n, d * RD:(d + 1) * RD, :] = s
            out.append(dec.g[d] * s + _tn(k_s[sl, :] * dec.b[d], v_of(sl)))
        return tuple(out)

    _chunk_loop(NCH, step, tuple(init))


def premix_proj(xin, mod3, g_pre, wperm, is_ctx, name):
    nb, length, _ = xin.shape
    tn = min(2 * TN, length)

    def body(x_ref, mod_ref, g_ref, w_ref, h_ref, pret_ref, pna_ref):
        h = _rms_mod(x_ref[...], g_ref[...], mod_ref[1:2, :], mod_ref[0:1, :])
        hb = h.astype(BF16)
        h_ref[...] = hb
        pret_ref[...] = jnp.dot(hb, w_ref[:, :RET_W], preferred_element_type=F32)
        pna_ref[...] = jnp.dot(hb, w_ref[:, RET_W:], preferred_element_type=F32).astype(BF16)

    return pl.pallas_call(
        body, name=name, grid=(nb, length // tn),
        in_specs=[
            pl.BlockSpec((None, tn, D), lambda b, t: (b, t, 0)),
            pl.BlockSpec((None, 6, D), (lambda b, t: (2, 0, 0)) if is_ctx else (lambda b, t: (b, 0, 0))),
            pl.BlockSpec((1, D), lambda b, t: (0, 0)),
            pl.BlockSpec((D, IN_W), lambda b, t: (0, 0), pipeline_mode=pl.Buffered(1)),
        ],
        out_specs=[
            pl.BlockSpec((None, tn, D), lambda b, t: (b, t, 0)),
            pl.BlockSpec((None, tn, RET_W), lambda b, t: (b, t, 0)),
            pl.BlockSpec((None, tn, IN_W - RET_W), lambda b, t: (b, t, 0)),
        ],
        out_shape=[
            jax.ShapeDtypeStruct((nb, length, D), BF16),
            jax.ShapeDtypeStruct((nb, length, RET_W), F32),
            jax.ShapeDtypeStruct((nb, length, IN_W - RET_W), BF16),
        ],
        compiler_params=_cp(("arbitrary", "arbitrary")),
    )(xin, mod3, g_pre, wperm)


def premix_bwd(xin, mod3, g_pre, wperm, dproj, dx_tail, hosted, name):
    nb, length, _ = xin.shape
    tn = min(2 * TN, length)
    is_ctx = dx_tail is None

    def body(*refs):
        own_in, h_in, own_out, h_out, _, h_sems = hosted.split(refs, 5 if is_ctx else 6, 2 if is_ctx else 3)
        if is_ctx:
            (x_ref, mod_ref, g_ref, w_ref, dp_ref), (dmod_ref, dg_ref) = own_in, own_out
        else:
            (x_ref, mod_ref, g_ref, w_ref, dp_ref, dxt_ref), (dx_ref, dmod_ref, dg_ref) = own_in, own_out
        b, t = pl.program_id(0), pl.program_id(1)
        grid_step = b * (length // tn) + t

        @pl.when(grid_step == 0)
        def _():
            hosted.start(h_in, h_out, h_sems)

        @pl.when(grid_step == nb * (length // tn) - 1)
        def _():
            hosted.finish(h_in, h_out, h_sems)

        dh = lax.dot_general(dp_ref[...], w_ref[...], (((1,), (1,)), ((), ())), preferred_element_type=F32)
        _, vjp = jax.vjp(_rms_mod, x_ref[...], g_ref[...], mod_ref[1:2, :], mod_ref[0:1, :])
        dx, dg, dsc, dsh = vjp(dh)
        if not is_ctx:
            dx_ref[...] = dx + dxt_ref[...]

        @pl.when((t == 0) & ((b == 0) if is_ctx else True))
        def _():
            dmod_ref[...] = jnp.zeros_like(dmod_ref)

        @pl.when((t == 0) & (b == 0))
        def _():
            dg_ref[...] = jnp.zeros_like(dg_ref)

        dmod_ref[0:1, :] += dsh
        dmod_ref[1:2, :] += dsc
        dg_ref[0:1, :] += dg

    tok = lambda b, t: (b, t, 0)
    in_specs = [
        pl.BlockSpec((None, tn, D), tok),
        pl.BlockSpec((None, 6, D), (lambda b, t: (2, 0, 0)) if is_ctx else (lambda b, t: (b, 0, 0))),
        pl.BlockSpec((1, D), lambda b, t: (0, 0)),
        pl.BlockSpec((D, IN_W), lambda b, t: (0, 0), pipeline_mode=pl.Buffered(1)),
        pl.BlockSpec((None, tn, IN_W), tok),
    ]
    args = [xin, mod3, g_pre, wperm, dproj]
    out_specs = [
        pl.BlockSpec((None, 6, D), (lambda b, t: (0, 0, 0)) if is_ctx else (lambda b, t: (b, 0, 0))),
        pl.BlockSpec((8, D), lambda b, t: (0, 0)),
    ]
    out_shape = [jax.ShapeDtypeStruct((1 if is_ctx else nb, 6, D), F32), jax.ShapeDtypeStruct((8, D), F32)]
    if not is_ctx:
        in_specs.append(pl.BlockSpec((None, tn, D), tok))
        args.append(dx_tail)
        out_specs.insert(0, pl.BlockSpec((None, tn, D), tok))
        out_shape.insert(0, jax.ShapeDtypeStruct((nb, length, D), F32))
    h_in_specs, h_out_specs = hosted.specs()
    return pl.pallas_call(
        body, name=name, grid=(nb, length // tn), in_specs=in_specs + h_in_specs, out_specs=out_specs + h_out_specs,
        out_shape=out_shape + hosted.out_shape, scratch_shapes=hosted.scratch,
        compiler_params=_cp(("arbitrary", "arbitrary")),
    )(*args, *hosted.args)


def _ret_specs(order):
    def im(f):
        return lambda *g: f(*order(*g))
    return dict(
        pret=pl.BlockSpec((None, SEQ, 512), im(lambda b, h: (b, 0, h))),
        pretc=pl.BlockSpec((None, LC, 512), im(lambda b, h: (b, 0, h))),
        rd=pl.BlockSpec((None, 2, 1), im(lambda b, h: (h, 0, 0))),
        gn=pl.BlockSpec((None, 1, RD), im(lambda b, h: (h, 0, 0))),
        tab=pl.BlockSpec((SEQ, RD), im(lambda b, h: (0, 0))),
        head=pl.BlockSpec((None, SEQ, RD), im(lambda b, h: (b, 0, h))),
    )


def retention_fwd(pret, pretc, rd, gn, cos, sin, hosted):
    nb = pret.shape[0]
    sp = _ret_specs(lambda b, h: (b, h))

    def body(*refs):
        own_in, h_in, own_out, h_out, own_scr, h_sems = hosted.split(refs, 6, 2)
        p_ref, pc_ref, rd_ref, gn_ref, cos_ref, sin_ref = own_in
        (o_ref, mix_ref), (q_s, k_s, o_s, st_s) = own_out, own_scr
        grid_step = pl.program_id(0) * RH + pl.program_id(1)

        @pl.when(grid_step == 0)
        def _():
            hosted.start(h_in, h_out, h_sems)

        cos_v, sin_v = cos_ref[...], sin_ref[...]
        q_s[...] = _rope(p_ref[:, 0:128], cos_v, sin_v) * (RD ** -0.5)
        k_s[...] = _rope(p_ref[:, 128:256], cos_v, sin_v)
        lgs, init = [], []
        for rev in (False, True):
            lg = jax.nn.log_sigmoid(rd_ref[int(rev):int(rev) + 1, :])
            s = jnp.zeros((RD, RD), F32)
            for n in ((1, 0) if rev else (0, 1)):
                s = _ret_state(pc_ref[n * CH:(n + 1) * CH, 128:256], pc_ref[n * CH:(n + 1) * CH, 256:384], s, lg, rev)
            lgs.append(lg)
            init.append(s)

        dec = _Decays(lgs)
        _state_pass(dec, init, k_s, lambda sl: p_ref[sl, 256:384], st_s)

        def chunk(n, carry):
            sl = pl.ds(pl.multiple_of(n * CH, CH), CH)
            q = q_s[sl, :]
            o_s[sl, :] = (_nn(_nt(q, k_s[sl, :]) * dec.mask, p_ref[sl, 256:384]) + _nn(_both(q, dec.a), st_s[n]))
            return carry

        _chunk_loop(NCH, chunk, 0)
        o = o_s[...]
        o_ref[...] = o
        mix_ref[...] = _ln_gate(o, p_ref[:, 384:512], gn_ref[...]).astype(BF16)

        @pl.when(grid_step == nb * RH - 1)
        def _():
            hosted.finish(h_in, h_out, h_sems)

    h_in_specs, h_out_specs = hosted.specs()
    return pl.pallas_call(
        body, name="retention_fwd", grid=(nb, RH),
        in_specs=[sp["pret"], sp["pretc"], sp["rd"], sp["gn"], sp["tab"], sp["tab"]] + h_in_specs,
        out_specs=[sp["head"], sp["head"]] + h_out_specs,
        out_shape=[jax.ShapeDtypeStruct((nb, SEQ, RH * RD), F32), jax.ShapeDtypeStruct((nb, SEQ, D), BF16)]
        + hosted.out_shape,
        scratch_shapes=[pltpu.VMEM((SEQ, RD), F32)] * 3 + [pltpu.VMEM((NCH, 2 * RD, RD), F32)] + hosted.scratch,
        compiler_params=_cp(("arbitrary", "arbitrary")),
    )(pret, pretc, rd, gn, cos, sin, *hosted.args)


def retention_bwd(pret, pretc, o_all, dmixin, rd, gn, cos, sin, hosted):
    nb = pret.shape[0]
    sp = _ret_specs(lambda h, b: (b, h))

    def body(*refs):
        own_in, h_in, own_out, h_out, own_scr, h_sems = hosted.split(refs, 8, 4)
        p_ref, pc_ref, o_ref, dmix_ref, rd_ref, gn_ref, cos_ref, sin_ref = own_in
        dp_ref, dpc_ref, drd_ref, dgn_ref = own_out
        q_s, k_s, do_s, dq_s, dk_s, dv_s, st_s, gst_s = own_scr
        b = pl.program_id(1)
        grid_step = pl.program_id(0) * nb + b

        @pl.when(grid_step == 0)
        def _():
            hosted.start(h_in, h_out, h_sems)

        cos_v, sin_v = cos_ref[...], sin_ref[...]
        q_s[...] = _rope(p_ref[:, 0:128], cos_v, sin_v) * (RD ** -0.5)
        k_s[...] = _rope(p_ref[:, 128:256], cos_v, sin_v)
        _, gate_vjp = jax.vjp(_ln_gate, o_ref[...], p_ref[:, 384:512], gn_ref[...])
        do, dg, dgn = gate_vjp(dmix_ref[...].astype(F32))
        do_s[...] = do
        dp_ref[:, 384:512] = dg.astype(BF16)

        @pl.when(b == 0)
        def _():
            drd_ref[...] = jnp.zeros_like(drd_ref)
            dgn_ref[...] = jnp.zeros_like(dgn_ref)

        dgn_ref[...] += dgn
        kcs = [pc_ref[n * CH:(n + 1) * CH, 128:256] for n in (0, 1)]
        vcs = [pc_ref[n * CH:(n + 1) * CH, 256:384] for n in (0, 1)]
        dirs = []
        init = []
        for rev in (False, True):
            rdv = rd_ref[int(rev):int(rev) + 1, :]
            lg = jax.nn.log_sigmoid(rdv)
            order_c = (1, 0) if rev else (0, 1)
            s = jnp.zeros((RD, RD), F32)
            ctx_states = []
            for n in order_c:
                ctx_states.append(s)
                s = _ret_state(kcs[n], vcs[n], s, lg, rev)
            dirs.append((rev, order_c, lg, rdv, ctx_states))
            init.append(s)
        dec = _Decays([lg for _, _, lg, _, _ in dirs])

        def v_of(sl):
            return p_ref[sl, 256:384]

        _state_pass(dec, init, k_s, v_of, st_s)
        zeros = jnp.zeros((CH, RD), F32)

        def scores_back(n, carry):
            dmask_sum, da_f, da_b = carry
            sl = pl.ds(pl.multiple_of(n * CH, CH), CH)
            q, k, v, do = q_s[sl, :], k_s[sl, :], v_of(sl), do_s[sl, :]
            scores = _nt(q, k)
            d_att = _nt(do, v)
            d_scores = d_att * dec.mask
            d_qa = _nt(do, st_s[n])
            d_qf, d_qb = d_qa[:, 0:RD], d_qa[:, RD:2 * RD]
            dq_s[sl, :] = _nn(d_scores, k) + d_qf * dec.a[0] + d_qb * dec.a[1]
            dk_s[sl, :] = _tn(d_scores, q)
            dv_s[sl, :] = _tn(scores * dec.mask, do)
            gst_s[n] = _tn(_both(q, dec.a), do)
            return dmask_sum + d_att * scores, da_f + d_qf * q, da_b + d_qb * q

        dmask_sum, da_f, da_b = _chunk_loop(NCH, scores_back, (zeros, zeros, zeros))

        def state_back(t, carry):
            out = []
            for d, r in enumerate(carry):
                n = t if d else (NCH - 1 - t)
                rows = slice(d * RD, (d + 1) * RD)
                own = gst_s[n, rows, :]
                gst_s[n, rows, :] = r
                out.append(own + dec.g[d] * r)
            return tuple(out)

        d_states = _chunk_loop(NCH, state_back, (zeros, zeros))

        def updates_back(n, carry):
            db_f, db_b, dg_f, dg_b = carry
            sl = pl.ds(pl.multiple_of(n * CH, CH), CH)
            k, r, s = k_s[sl, :], gst_s[n], st_s[n]
            d_kw = _nt(v_of(sl), r)
            d_kf, d_kb = d_kw[:, 0:RD], d_kw[:, RD:2 * RD]
            dk_s[sl, :] += d_kf * dec.b[0] + d_kb * dec.b[1]
            dv_s[sl, :] += _nn(_both(k, dec.b), r)
            return (db_f + d_kf * k, db_b + d_kb * k, dg_f + r[0:RD, :] * s[0:RD, :],
                    dg_b + r[RD:2 * RD, :] * s[RD:2 * RD, :])

        db_dg = _chunk_loop(NCH, updates_back, (zeros, zeros, zeros, zeros))
        dkc = [None, None]
        dvc = [None, None]
        for d, ((rev, order_c, lg, rdv, ctx_states), ds) in enumerate(zip(dirs, d_states)):
            dlg = (_total(dmask_sum * dec.dmask[d]) + _total((da_f, da_b)[d] * dec.da[d])
                   + _total(db_dg[d] * dec.db[d]) + CH * dec.g[d] * _total(db_dg[2 + d]))
            for idx in (1, 0):
                n = order_c[idx]
                _, vjp = jax.vjp(functools.partial(_ret_state, reverse=rev), kcs[n], vcs[n], ctx_states[idx], lg)
                dk_c, dv_c, ds, dl = vjp(ds)
                dlg = dlg + dl
                dkc[n] = dk_c if dkc[n] is None else dkc[n] + dk_c
                dvc[n] = dv_c if dvc[n] is None else dvc[n] + dv_c
            drd_ref[int(rev):int(rev) + 1, :] += dlg * jax.nn.sigmoid(-rdv)
        dp_ref[:, 0:128] = _rope_t(dq_s[...] * (RD ** -0.5), cos_v, sin_v).astype(BF16)
        dp_ref[:, 128:256] = _rope_t(dk_s[...], cos_v, sin_v).astype(BF16)
        dp_ref[:, 256:384] = dv_s[...].astype(BF16)
        zero = jnp.zeros((CH, RD), BF16)
        for n in (0, 1):
            rows = slice(n * CH, (n + 1) * CH)
            dpc_ref[rows, 0:128] = zero
            dpc_ref[rows, 128:256] = dkc[n].astype(BF16)
            dpc_ref[rows, 256:384] = dvc[n].astype(BF16)
            dpc_ref[rows, 384:512] = zero

        @pl.when(grid_step == RH * nb - 1)
        def _():
            hosted.finish(h_in, h_out, h_sems)

    h_in_specs, h_out_specs = hosted.specs()
    return pl.pallas_call(
        body, name="retention_bwd", grid=(RH, nb),
        in_specs=[sp["pret"], sp["pretc"], sp["head"], sp["head"], sp["rd"], sp["gn"], sp["tab"], sp["tab"]]
        + h_in_specs,
        out_specs=[
            pl.BlockSpec((None, SEQ, 512), lambda h, b: (b, 0, h)),
            pl.BlockSpec((None, LC, 512), lambda h, b: (b, 0, h)),
            pl.BlockSpec((None, 2, 1), lambda h, b: (h, 0, 0)),
            pl.BlockSpec((None, 1, RD), lambda h, b: (h, 0, 0)),
        ] + h_out_specs,
        out_shape=[
            jax.ShapeDtypeStruct((nb, SEQ, IN_W), BF16),
            jax.ShapeDtypeStruct((nb, LC, IN_W), BF16),
            jax.ShapeDtypeStruct((RH, 2, 1), F32),
            jax.ShapeDtypeStruct((RH, 1, RD), F32),
        ] + hosted.out_shape,
        scratch_shapes=[pltpu.VMEM((SEQ, RD), F32)] * 6 + [pltpu.VMEM((NCH, 2 * RD, RD), F32)] * 2 + hosted.scratch,
        compiler_params=_cp(("arbitrary", "arbitrary")),
    )(pret, pretc, o_all, dmixin, rd, gn, cos, sin, *hosted.args)


def _rpb_flat(rpb):
    return jnp.pad(rpb, ((0, 0), (0, 1), (0, 33))).reshape(NPAIR, 2, 1, 1024)


def _rpb_flat_t(dflat):
    return dflat.reshape(8, 16, 64)[:, :15, :31]


def _barrel(x, left):
    row = lax.broadcasted_iota(jnp.int32, x.shape, 0)
    n = x.shape[1]
    for bit in range(6):
        s = 1 << bit
        x = jnp.where(((row >> bit) & 1) == 1, pltpu.roll(x, (n - s) if left else s, 1), x)
    return x


NA_TILE_ROWS, NA_BAND_ROWS = 4, 12
NA_Q, NA_K = NA_TILE_ROWS * GW, NA_BAND_ROWS * GW
NA_TILES = SEQ // NA_Q


def _band_start(r0):
    return min(max(r0 - 4, 0), 32 - NA_BAND_ROWS)


def _tile_layout(t):
    rows = range(t * NA_TILE_ROWS, (t + 1) * NA_TILE_ROWS)
    return tuple((r if r < 4 else (r - 24 if r > 28 else 4), min(max(r - 4, 0), 24) - _band_start(rows[0]))
                 for r in rows)


NA_CLASSES = sorted(set(_tile_layout(t) for t in range(NA_TILES)))


def _tile_rows(cls):
    return NA_CLASSES[cls]


def _na_tile(t):
    start = jnp.clip(NA_TILE_ROWS * t - 4, 0, 32 - NA_BAND_ROWS)
    cls = 0
    for tile in range(NA_TILES):
        cls = jnp.where(t == tile, NA_CLASSES.index(_tile_layout(tile)), cls)
    return pl.ds(pl.multiple_of(t * NA_Q, NA_Q), NA_Q), pl.ds(pl.multiple_of(start * GW, NA_Q), NA_K), cls


def _na_probs(qst, kb, kc, bias):
    s_loc = _nt(qst, kb) + bias
    s_ctx = _nt(qst, kc)
    m = jnp.maximum(jnp.max(s_loc, axis=1, keepdims=True), jnp.max(s_ctx, axis=1, keepdims=True))
    e_loc, e_ctx = jnp.exp(s_loc - m), jnp.exp(s_ctx - m)
    den = jnp.sum(e_loc, axis=1, keepdims=True) + jnp.sum(e_ctx, axis=1, keepdims=True)
    return e_loc / den, e_ctx / den


def _stack_heads(t):
    lane = lax.broadcasted_iota(jnp.int32, t.shape, 1)
    zero = jnp.zeros_like(t)
    return jnp.concatenate([jnp.where(lane < 64, t, zero), jnp.where(lane >= 64, t, zero)], axis=0)


def _unstack_heads(t):
    n = t.shape[0] // 2
    lane = lax.broadcasted_iota(jnp.int32, (n, 128), 1)
    return jnp.where(lane < 64, t[:n], t[n:])


NA_BIAS_SHAPE = (len(NA_CLASSES), 2 * NA_Q, NA_K)


def _na_bias_pair(flat_ref, out_ref):
    qc = lax.broadcasted_iota(jnp.int32, (GW, 512), 0)
    kc = lax.broadcasted_iota(jnp.int32, (GW, 512), 1) & 63
    start = jnp.clip(qc - 8, 0, GW - 16)
    window = (kc >= start) & (kc < start + 16)
    fill = jnp.full((GW, NA_K - 512), NEG, F32)
    for hh in (0, 1):
        skew = _barrel(pltpu.roll(jnp.broadcast_to(flat_ref[hh], (GW, 1024)), 1024 - 15, 1), left=False)
        by_class = [jnp.where(window, (skew if rc == 7 else pltpu.roll(skew, (9 + rc) * 64, 1))[:, 0:512], NEG)
                    for rc in range(8)]
        for cls in range(len(NA_CLASSES)):
            for qr, (rc, off) in enumerate(_tile_rows(cls)):
                w = jnp.concatenate([by_class[rc], fill], axis=1)
                rows = slice(hh * NA_Q + qr * GW, hh * NA_Q + (qr + 1) * GW)
                out_ref[cls, rows, :] = pltpu.roll(w, off * GW, 1) if off else w


def na_fwd(pna, pnac, bias, mixin, hosted):
    nb = pna.shape[0]

    def body(*refs):
        (p_ref, pc_ref, bias_ref, _), h_in, (out_ref,), h_out, _, h_sems = hosted.split(refs, 4, 1)
        grid_step = pl.program_id(0) * nb + pl.program_id(1)

        @pl.when(grid_step == 0)
        def _():
            hosted.start(h_in, h_out, h_sems)

        kc, vc = pc_ref[:, 128:256], pc_ref[:, 256:384]

        def tile(t, carry):
            qsl, bsl, cls = _na_tile(t)
            kb, vb = p_ref[bsl, 128:256], p_ref[bsl, 256:384]
            p_loc, p_ctx = _na_probs(_stack_heads(p_ref[qsl, 0:128] * 0.125), kb, kc, bias_ref[cls])
            out_ref[qsl, :] = _unstack_heads(_nn(p_loc, vb) + _nn(p_ctx, vc)).astype(BF16)
            return carry

        lax.fori_loop(0, NA_TILES, tile, 0, unroll=4)

        @pl.when(grid_step == NPAIR * nb - 1)
        def _():
            hosted.finish(h_in, h_out, h_sems)

    h_in_specs, h_out_specs = hosted.specs()
    return pl.pallas_call(
        body, name="na_fwd", grid=(NPAIR, nb),
        in_specs=[
            pl.BlockSpec((None, SEQ, 384), lambda p, b: (b, 0, p)),
            pl.BlockSpec((None, LC, 384), lambda p, b: (b, 0, p)),
            pl.BlockSpec((None, len(NA_CLASSES), 2 * NA_Q, NA_K), lambda p, b: (p, 0, 0, 0)),
            pl.BlockSpec(memory_space=pl.ANY),
        ] + h_in_specs,
        out_specs=[pl.BlockSpec((None, SEQ, 128), lambda p, b: (b, 0, 4 + p))] + h_out_specs,
        out_shape=[jax.ShapeDtypeStruct((nb, SEQ, D), BF16)] + hosted.out_shape,
        input_output_aliases={3: 0},
        scratch_shapes=hosted.scratch,
        compiler_params=_cp(("arbitrary", "arbitrary")),
    )(pna, pnac, bias, mixin, *hosted.args)


def na_bwd(pna, pnac, bias, dmixin, dproj, dprojc, hosted):
    nb = pna.shape[0]

    def body(*refs):
        own_in, h_in, own_out, h_out, own_scr, h_sems = hosted.split(refs, 6, 3)
        p_ref, pc_ref, bias_ref, dmix_ref = own_in[:4]
        dp_ref, dpc_ref, dpat_ref = own_out
        dbias_s, dk_s, dv_s, dkc_s, dvc_s, res_s, resc_s = own_scr
        b, part = pl.program_id(1), pl.program_id(2)
        grid_step = (pl.program_id(0) * nb + b) * 3 + part

        @pl.when(grid_step == 0)
        def _():
            hosted.start(h_in, h_out, h_sems)

        @pl.when(grid_step == NPAIR * nb * 3 - 1)
        def _():
            hosted.finish(h_in, h_out, h_sems)

        @pl.when(part == 0)
        def _():
            @pl.when(b == 0)
            def _():
                dbias_s[...] = jnp.zeros_like(dbias_s)

            dk_s[...] = jnp.zeros_like(dk_s)
            dv_s[...] = jnp.zeros_like(dv_s)
            dkc_s[...] = jnp.zeros_like(dkc_s)
            dvc_s[...] = jnp.zeros_like(dvc_s)
            kc, vc = pc_ref[:, 128:256], pc_ref[:, 256:384]

            def tile(t, carry):
                qsl, bsl, cls = _na_tile(t)
                kb, vb = p_ref[bsl, 128:256], p_ref[bsl, 256:384]
                qst, dost = _stack_heads(p_ref[qsl, 0:128] * 0.125), _stack_heads(dmix_ref[qsl, :])
                p_loc, p_ctx = _na_probs(qst, kb, kc, bias_ref[cls])
                dp_loc, dp_ctx = _nt(dost, vb), _nt(dost, vc)
                delta = (jnp.sum(p_loc * dp_loc, axis=1, keepdims=True)
                         + jnp.sum(p_ctx * dp_ctx, axis=1, keepdims=True))
                ds_loc, ds_ctx = p_loc * (dp_loc - delta), p_ctx * (dp_ctx - delta)
                dbias_s[cls] += ds_loc
                res_s[0, qsl, :] = _unstack_heads((_nn(ds_loc, kb) + _nn(ds_ctx, kc)) * 0.125).astype(BF16)
                dk_s[bsl, :] += _tn(ds_loc, qst)
                dv_s[bsl, :] += _tn(p_loc, dost)
                dkc_s[...] += _tn(ds_ctx, qst)
                dvc_s[...] += _tn(p_ctx, dost)
                return carry

            lax.fori_loop(0, NA_TILES, tile, 0, unroll=2)
            res_s[1] = dk_s[...].astype(BF16)
            res_s[2] = dv_s[...].astype(BF16)
            resc_s[0] = jnp.zeros((LC, 128), BF16)
            resc_s[1] = dkc_s[...].astype(BF16)
            resc_s[2] = dvc_s[...].astype(BF16)

            @pl.when(b == nb - 1)
            def _():
                for hh in (0, 1):
                    by_class = [None] * 8
                    for cls in range(len(NA_CLASSES)):
                        for qr, (rc, off) in enumerate(_tile_rows(cls)):
                            w = dbias_s[cls, hh * NA_Q + qr * GW:hh * NA_Q + (qr + 1) * GW, :]
                            w = (pltpu.roll(w, NA_K - off * GW, 1) if off else w)[:, 0:512]
                            by_class[rc] = w if by_class[rc] is None else by_class[rc] + w
                    skew = jnp.zeros((GW, 1024), F32)
                    for rc in range(8):
                        w = jnp.concatenate([by_class[rc], jnp.zeros((GW, 512), F32)], axis=1)
                        skew = skew + (w if rc == 7 else pltpu.roll(w, (7 - rc) * 64, 1))
                    dpat_ref[hh] = jnp.sum(pltpu.roll(_barrel(skew, left=True), 15, 1), axis=0, keepdims=True)

        dp_ref[...] = res_s[part]
        dpc_ref[...] = resc_s[part]

    h_in_specs, h_out_specs = hosted.specs()
    return pl.pallas_call(
        body, name="na_bwd", grid=(NPAIR, nb, 3),
        in_specs=[
            pl.BlockSpec((None, SEQ, 384), lambda p, b, s: (b, 0, p)),
            pl.BlockSpec((None, LC, 384), lambda p, b, s: (b, 0, p)),
            pl.BlockSpec((None, len(NA_CLASSES), 2 * NA_Q, NA_K), lambda p, b, s: (p, 0, 0, 0)),
            pl.BlockSpec((None, SEQ, 128), lambda p, b, s: (b, 0, 4 + p)),
            pl.BlockSpec(memory_space=pl.ANY),
            pl.BlockSpec(memory_space=pl.ANY),
        ] + h_in_specs,
        out_specs=[
            pl.BlockSpec((None, SEQ, 128), lambda p, b, s: (b, 0, 16 + 3 * p + s)),
            pl.BlockSpec((None, LC, 128), lambda p, b, s: (b, 0, 16 + 3 * p + s)),
            pl.BlockSpec((None, 2, 1, 1024), lambda p, b, s: (p, 0, 0, 0)),
        ] + h_out_specs,
        out_shape=[
            jax.ShapeDtypeStruct((nb, SEQ, IN_W), BF16),
            jax.ShapeDtypeStruct((nb, LC, IN_W), BF16),
            jax.ShapeDtypeStruct((NPAIR, 2, 1, 1024), F32),
        ] + hosted.out_shape,
        input_output_aliases={4: 0, 5: 1},
        scratch_shapes=[
            pltpu.VMEM((len(NA_CLASSES), 2 * NA_Q, NA_K), F32),
            pltpu.VMEM((SEQ, 128), F32), pltpu.VMEM((SEQ, 128), F32),
            pltpu.VMEM((LC, 128), F32), pltpu.VMEM((LC, 128), F32),
            pltpu.VMEM((3, SEQ, 128), BF16), pltpu.VMEM((3, LC, 128), BF16),
        ] + hosted.scratch,
        compiler_params=_cp(("arbitrary", "arbitrary", "arbitrary")),
    )(pna, pnac, bias, dmixin, dproj, dprojc, *hosted.args)


def tail_fwd_bwd(x, mixin, tgt, mod3, g_post_mix, g_pre_mlp, g_post_mlp, wout, w1, w2):
    nb = x.shape[0]

    def body(x_ref, mi_ref, tgt_ref, mod_ref, gpm_ref, gpl_ref, gpo_ref, wo_ref, w1_ref, w2_ref,
             dx_ref, dmix_ref, h2_ref, du_ref, a_ref, dm_ref, dmi_ref, dmod_ref, dg_ref, loss_ref):
        b, t = pl.program_id(0), pl.program_id(1)
        gt1, sh2, sc2, gt2 = mod_ref[2:3, :], mod_ref[3:4, :], mod_ref[4:5, :], mod_ref[5:6, :]
        mix = jnp.dot(mi_ref[...], wo_ref[...], preferred_element_type=F32)
        (x1, h2), vjp_a = jax.vjp(_post_mix, x_ref[...], mix, gt1, sc2, sh2, gpm_ref[...], gpl_ref[...])
        h2b = h2.astype(BF16)
        h2_ref[...] = h2b
        m = jnp.zeros((TN, D), F32)
        relus = []
        for j in range(4):
            cols = slice(j * D, (j + 1) * D)
            r = jnp.maximum(jnp.dot(h2b, w1_ref[j], preferred_element_type=F32), 0.0)
            ab = (r * r).astype(BF16)
            a_ref[:, cols] = ab
            m = m + jnp.dot(ab, w2_ref[cols, :], preferred_element_type=F32)
            relus.append(r)
        loss, vjp_b = jax.vjp(_head_loss, x1, m, gt2, gpo_ref[...], tgt_ref[...])
        dx1, dm, dgt2, dgpo, _ = vjp_b(jnp.ones((1, 1), F32))
        dmb = dm.astype(BF16)
        dm_ref[...] = dmb
        dh2 = jnp.zeros((TN, D), F32)
        for j in range(4):
            cols = slice(j * D, (j + 1) * D)
            da = lax.dot_general(dmb, w2_ref[cols, :], (((1,), (1,)), ((), ())), preferred_element_type=F32)
            dub = (da * (2.0 * relus[j])).astype(BF16)
            du_ref[:, cols] = dub
            dh2 = dh2 + lax.dot_general(dub, w1_ref[j], (((1,), (1,)), ((), ())), preferred_element_type=F32)
        dx, dmix, dgt1, dsc2, dsh2, dgpm, dgpl = vjp_a((dx1, dh2))
        dx_ref[...] = dx
        dmixb = dmix.astype(BF16)
        dmix_ref[...] = dmixb
        dmi_ref[...] = lax.dot_general(dmixb, wo_ref[...], (((1,), (1,)), ((), ())),
                                       preferred_element_type=F32).astype(BF16)

        @pl.when(t == 0)
        def _():
            dmod_ref[...] = jnp.zeros_like(dmod_ref)

        @pl.when((t == 0) & (b == 0))
        def _():
            dg_ref[...] = jnp.zeros_like(dg_ref)
            loss_ref[...] = jnp.zeros_like(loss_ref)

        dmod_ref[2:3, :] += dgt1
        dmod_ref[3:4, :] += dsh2
        dmod_ref[4:5, :] += dsc2
        dmod_ref[5:6, :] += dgt2
        dg_ref[0:1, :] += dgpm
        dg_ref[1:2, :] += dgpl
        dg_ref[2:3, :] += dgpo
        loss_ref[...] += jnp.broadcast_to(loss, loss_ref.shape)

    tok = lambda b, t: (b, t, 0)
    const = lambda b, t: (0, 0)
    vec = pl.BlockSpec((1, D), const)
    return pl.pallas_call(
        body, name="tail_fwd_bwd", grid=(nb, SEQ // TN),
        in_specs=[
            pl.BlockSpec((None, TN, D), tok), pl.BlockSpec((None, TN, D), tok), pl.BlockSpec((None, TN, D), tok),
            pl.BlockSpec((None, 6, D), lambda b, t: (b, 0, 0)), vec, vec, vec,
            pl.BlockSpec((D, D), const, pipeline_mode=pl.Buffered(1)),
            pl.BlockSpec((4, D, D), lambda b, t: (0, 0, 0), pipeline_mode=pl.Buffered(1)),
            pl.BlockSpec((DFF, D), const, pipeline_mode=pl.Buffered(1)),
        ],
        out_specs=[
            pl.BlockSpec((None, TN, D), tok), pl.BlockSpec((None, TN, D), tok), pl.BlockSpec((None, TN, D), tok),
            pl.BlockSpec((None, TN, DFF), tok), pl.BlockSpec((None, TN, DFF), tok), pl.BlockSpec((None, TN, D), tok),
            pl.BlockSpec((None, TN, D), tok),
            pl.BlockSpec((None, 6, D), lambda b, t: (b, 0, 0)),
            pl.BlockSpec((8, D), const), pl.BlockSpec((8, 128), const),
        ],
        out_shape=[
            jax.ShapeDtypeStruct((nb, SEQ, D), F32), jax.ShapeDtypeStruct((nb, SEQ, D), BF16),
            jax.ShapeDtypeStruct((nb, SEQ, D), BF16), jax.ShapeDtypeStruct((nb, SEQ, DFF), BF16),
            jax.ShapeDtypeStruct((nb, SEQ, DFF), BF16), jax.ShapeDtypeStruct((nb, SEQ, D), BF16),
            jax.ShapeDtypeStruct((nb, SEQ, D), BF16),
            jax.ShapeDtypeStruct((nb, 6, D), F32), jax.ShapeDtypeStruct((8, D), F32),
            jax.ShapeDtypeStruct((8, 128), F32),
        ],
        compiler_params=_cp(("arbitrary", "arbitrary")),
    )(x, mixin, tgt, mod3, g_post_mix, g_pre_mlp, g_post_mlp, wout, w1, w2)


def weight_grad(pairs, name, out_dtype=F32, col_blocks=False, tm=1024, tn=1024, tk=2048):
    m, n = pairs[0][0].shape[1], pairs[0][1].shape[1]
    tn = min(tn, n)
    tks = [min(tk, xa.shape[0]) for xa, _ in pairs]
    steps = [xa.shape[0] // t for (xa, _), t in zip(pairs, tks)]
    total = sum(steps)
    offs = [sum(steps[:i]) for i in range(len(pairs))]

    def body(*refs):
        out_ref, acc = refs[2 * len(pairs)], refs[-1]
        k = pl.program_id(2)

        @pl.when(k == 0)
        def _():
            acc[...] = jnp.zeros_like(acc)

        for i in range(len(pairs)):
            @pl.when((k >= offs[i]) & (k < offs[i] + steps[i]))
            def _(i=i):
                acc[...] += lax.dot_general(refs[2 * i][...], refs[2 * i + 1][...], (((0,), (0,)), ((), ())),
                                            preferred_element_type=F32)

        if out_dtype != F32:
            @pl.when(k == total - 1)
            def _():
                out_ref[...] = acc[...].astype(out_dtype)

    in_specs, args = [], []
    for i, (xa, ya) in enumerate(pairs):
        clamp = lambda k, i=i: jnp.clip(k - offs[i], 0, steps[i] - 1)
        in_specs.append(pl.BlockSpec((tks[i], tm), lambda a, c, k, clamp=clamp: (clamp(k), a)))
        in_specs.append(pl.BlockSpec((tks[i], tn), lambda a, c, k, clamp=clamp: (clamp(k), c)))
        args += [xa, ya]
    if col_blocks:
        out_spec = pl.BlockSpec((None, tm, tn), lambda a, c, k: (c, a, 0))
        out_shape = jax.ShapeDtypeStruct((n // tn, m, tn), out_dtype)
    else:
        out_spec = pl.BlockSpec((tm, tn), lambda a, c, k: (a, c))
        out_shape = jax.ShapeDtypeStruct((m, n), out_dtype)
    return pl.pallas_call(
        body, name=name, grid=(m // tm, n // tn, total), in_specs=in_specs, out_specs=out_spec, out_shape=out_shape,
        scratch_shapes=[] if out_dtype == F32 else [pltpu.VMEM((tm, tn), F32)],
        compiler_params=_cp(("arbitrary", "arbitrary", "arbitrary")),
    )(*args)


def _perm_block(t):
    return 4 * (t % 4) + t // 4 if t < 16 else 16 + 3 * ((t - 16) % 4) + (t - 16) // 4


def unpack_w_in(blocks):
    def body(i_ref, o_ref):
        for t in range(28):
            p = _perm_block(t)
            o_ref[:, p * 128:(p + 1) * 128] = i_ref[t // 7, :, (t % 7) * 128:(t % 7 + 1) * 128]

    return pl.pallas_call(
        body, name="unpack_w_in", grid=(2,),
        in_specs=[pl.BlockSpec((4, D // 2, 896), lambda i: (0, i, 0))],
        out_specs=pl.BlockSpec((D // 2, IN_W), lambda i: (i, 0)),
        out_shape=jax.ShapeDtypeStruct((D, IN_W), BF16),
    )(blocks)


def pack_w_in(dw):
    def body(i_ref, o_ref):
        for t in range(28):
            p = _perm_block(t)
            o_ref[t // 7, :, (t % 7) * 128:(t % 7 + 1) * 128] = i_ref[:, p * 128:(p + 1) * 128].astype(BF16)

    return pl.pallas_call(
        body, name="pack_w_in", grid=(4,),
        in_specs=[pl.BlockSpec((D // 4, IN_W), lambda i: (i, 0))],
        out_specs=pl.BlockSpec((4, D // 4, 896), lambda i: (0, i, 0)),
        out_shape=jax.ShapeDtypeStruct((4, D, 896), BF16),
    )(dw)


def _place():
    return lax.axis_index("x"), lax.axis_index("y"), lax.axis_index("c")


class Hosted:
    def __init__(self, args, out_shape, scratch, start, finish):
        self.args, self.out_shape, self.scratch, self.start, self.finish = args, out_shape, scratch, start, finish

    def specs(self):
        hbm = pl.BlockSpec(memory_space=pl.ANY)
        return [hbm] * len(self.args), [hbm] * len(self.out_shape)

    def split(self, refs, n_in, n_out):
        a, b = len(self.args), len(self.out_shape)
        cuts = [n_in, n_in + a, n_in + a + n_out, n_in + a + n_out + b, len(refs) - len(self.scratch)]
        parts = [refs[i:j] for i, j in zip([0] + cuts, cuts + [len(refs)])]
        return parts[0], parts[1], parts[2], parts[3], parts[4], parts[5]


def no_exchange():
    return Hosted([], [], [], lambda *a: None, lambda *a: None)


def run_hosted(hosted, name):
    def body(*refs):
        _, ins, _, outs, _, sems = hosted.split(refs, 0, 0)
        hosted.start(ins, outs, sems)
        hosted.finish(ins, outs, sems)

    in_specs, out_specs = hosted.specs()
    return pl.pallas_call(body, name=name, in_specs=in_specs, out_specs=out_specs, out_shape=hosted.out_shape,
                          scratch_shapes=hosted.scratch)(*hosted.args)


def gather8(blocks):
    na = len(blocks)

    def copies(ins, outs, sems):
        send_sems, recv_sems, local_sem = sems
        x, y, c = _place()
        me, sibling = (x, y, c), (x, y, 1 - c)
        chips = [(1 - x, y), (x, 1 - y), (1 - x, 1 - y)]

        def slot(o_ref, px, py, pc):
            return o_ref.at[4 * px + 2 * py + pc]

        def copy(a, k, block, to, src=None):
            return pltpu.make_async_remote_copy(
                src_ref=slot(outs[a], *block) if src is None else src, dst_ref=slot(outs[a], *block),
                send_sem=send_sems.at[a, k], recv_sem=recv_sems.at[a, k], device_id=to, device_id_type=MESH)

        mine = [pltpu.make_async_copy(ins[a], slot(outs[a], *me), local_sem.at[a]) for a in range(na)]
        first = []
        for a in range(na):
            first.append(copy(a, 0, me, sibling, src=ins[a]))
            first += [copy(a, 1 + j, me, (*chip, c), src=ins[a]) for j, chip in enumerate(chips)]
        return copy, mine, first, me, sibling, chips, c

    def start(ins, outs, sems):
        _, mine, first, *_ = copies(ins, outs, sems)
        for cp in mine + first:
            cp.start()

    def finish(ins, outs, sems):
        copy, mine, first, me, sibling, chips, c = copies(ins, outs, sems)
        passed = []
        for j, chip in enumerate(chips):
            for a in range(na):
                copy(a, 1 + j, (*chip, c), me).wait_recv()
                cp = copy(a, 4 + j, (*chip, c), sibling)
                cp.start()
                passed.append(cp)
        for a in range(na):
            copy(a, 0, sibling, me).wait_recv()
            for j, chip in enumerate(chips):
                copy(a, 4 + j, (*chip, 1 - c), me).wait_recv()
        for cp in first + passed:
            cp.wait_send()
        for cp in mine:
            cp.wait()

    return Hosted(list(blocks), [jax.ShapeDtypeStruct((8,) + b.shape, b.dtype) for b in blocks],
                  [pltpu.SemaphoreType.DMA((na, 7)), pltpu.SemaphoreType.DMA((na, 7)), pltpu.SemaphoreType.DMA((na,))],
                  start, finish)


def chips3(arrays):
    na = len(arrays)

    def copies(ins, outs, sems):
        send_sems, recv_sems = sems
        x, y, c = _place()
        return [pltpu.make_async_remote_copy(
            src_ref=ins[a].at[2 * px + py], dst_ref=outs[a].at[k], send_sem=send_sems.at[a, k],
            recv_sem=recv_sems.at[a, k], device_id=(px, py, c), device_id_type=MESH)
            for a in range(na) for k, (px, py) in enumerate([(1 - x, y), (x, 1 - y), (1 - x, 1 - y)])]

    def start(ins, outs, sems):
        for cp in copies(ins, outs, sems):
            cp.start()

    def finish(ins, outs, sems):
        for cp in copies(ins, outs, sems):
            cp.wait()

    return Hosted(list(arrays), [jax.ShapeDtypeStruct((3,) + a.shape[1:], a.dtype) for a in arrays],
                  [pltpu.SemaphoreType.DMA((na, 3)), pltpu.SemaphoreType.DMA((na, 3))], start, finish)


def siblings(arrays):
    na = len(arrays)

    def copies(ins, outs, sems):
        send_sems, recv_sems = sems
        x, y, c = _place()
        return [pltpu.make_async_remote_copy(
            src_ref=ins[a], dst_ref=outs[a], send_sem=send_sems.at[a], recv_sem=recv_sems.at[a],
            device_id=(x, y, 1 - c), device_id_type=MESH) for a in range(na)]

    def start(ins, outs, sems):
        for cp in copies(ins, outs, sems):
            cp.start()

    def finish(ins, outs, sems):
        for cp in copies(ins, outs, sems):
            cp.wait()

    return Hosted(list(arrays), [jax.ShapeDtypeStruct(a.shape, a.dtype) for a in arrays],
                  [pltpu.SemaphoreType.DMA((na,)), pltpu.SemaphoreType.DMA((na,))], start, finish)


def both(first, second):
    na, no, ns = len(first.args), len(first.out_shape), len(first.scratch)

    def start(ins, outs, sems):
        first.start(ins[:na], outs[:no], sems[:ns])
        second.start(ins[na:], outs[no:], sems[ns:])

    def finish(ins, outs, sems):
        first.finish(ins[:na], outs[:no], sems[:ns])
        second.finish(ins[na:], outs[no:], sems[ns:])

    return Hosted(first.args + second.args, first.out_shape + second.out_shape, first.scratch + second.scratch,
                  start, finish)


def siblings4(arrays):
    na = len(arrays)

    def copies(ins, outs, sems):
        send_sems, recv_sems = sems
        x, y, c = _place()
        return [pltpu.make_async_remote_copy(
            src_ref=ins[a].at[2 * j + 1 - c], dst_ref=outs[a].at[j],
            send_sem=send_sems.at[a, j], recv_sem=recv_sems.at[a, j],
            device_id=(x, y, 1 - c), device_id_type=MESH) for a in range(na) for j in range(4)]

    def start(ins, outs, sems):
        for cp in copies(ins, outs, sems):
            cp.start()

    def finish(ins, outs, sems):
        for cp in copies(ins, outs, sems):
            cp.wait()

    return Hosted(list(arrays), [jax.ShapeDtypeStruct((4,) + a.shape[1:], a.dtype) for a in arrays],
                  [pltpu.SemaphoreType.DMA((na, 4)), pltpu.SemaphoreType.DMA((na, 4))], start, finish)


def sibling_blocks(arrays, name):
    return run_hosted(siblings4(arrays), name)


def _row_tile(r):
    for cand in (512, 256, 128, 64, 32, 16, 8):
        if r % cand == 0:
            return cand
    return r


def chip_partial(place, g8s, landed4s, name):
    n = len(g8s)

    def body(place_ref, *refs):
        del place_ref
        for g_ref, l_ref, o_ref in zip(refs[:n], refs[n:2 * n], refs[2 * n:]):
            o_ref[...] = (g_ref[...].astype(F32) + l_ref[...].astype(F32)).astype(BF16)

    own = [pl.BlockSpec((None,) + g.shape[1:], lambda j, s: (2 * j + s[0], 0, 0)) for g in g8s]
    plain = [pl.BlockSpec((None,) + g.shape[1:], lambda j, s: (j, 0, 0)) for g in g8s]
    return pl.pallas_call(
        body, name=name,
        grid_spec=pltpu.PrefetchScalarGridSpec(num_scalar_prefetch=1, grid=(4,), in_specs=own + plain, out_specs=plain),
        out_shape=[jax.ShapeDtypeStruct((4,) + g.shape[1:], BF16) for g in g8s],
    )(place, *g8s, *landed4s)


def shard_sum(place, partial4s, landed3s, name):
    n = len(partial4s)

    def body(place_ref, *refs):
        del place_ref
        for p_ref, l_ref, o_ref in zip(refs[:n], refs[n:2 * n], refs[2 * n:]):
            acc = p_ref[...].astype(F32)
            for k in range(3):
                acc = acc + l_ref[k].astype(F32)
            o_ref[...] = acc

    def halves(p, lead):
        r, ccols = p.shape[1:]
        return (lead, r // 2, ccols)

    return pl.pallas_call(
        body, name=name,
        grid_spec=pltpu.PrefetchScalarGridSpec(
            num_scalar_prefetch=1, grid=(2,),
            in_specs=[pl.BlockSpec(halves(p, None), lambda i, s: (s[1], i, 0)) for p in partial4s]
            + [pl.BlockSpec(halves(p, 3), lambda i, s: (0, i, 0)) for p in partial4s],
            out_specs=[pl.BlockSpec(halves(p, None)[1:], lambda i, s: (i, 0)) for p in partial4s]),
        out_shape=[jax.ShapeDtypeStruct(p.shape[1:], F32) for p in partial4s],
    )(place, *partial4s, *landed3s)


def _adamw_math(w, g, m, v):
    m2 = B1 * m + (1.0 - B1) * g
    v2 = B2 * v + (1.0 - B2) * (g * g)
    m_hat = m2 / (1.0 - B1 ** STEP)
    v_hat = v2 / (1.0 - B2 ** STEP)
    return -LR * (m_hat / (jnp.sqrt(v_hat) + AEPS) + WD * w), m2, v2


def adamw_halves(place, w, mine, theirs, m, v, name):
    r, ccols = w.shape
    hr = r // 2
    tr = _row_tile(hr)
    nt = hr // tr

    def body(place_ref, w_ref, a_ref, b_ref, m_ref, v_ref, g_out, d_out, m_out, v_out):
        g = jnp.where(pl.program_id(0) == place_ref[0], a_ref[...], b_ref[...])
        d, m2, v2 = _adamw_math(w_ref[...], g, m_ref[...], v_ref[...])
        g_out[...] = g
        d_out[...] = d
        m_out[...] = m2
        v_out[...] = v2

    full = pl.BlockSpec((tr, ccols), lambda h, i, s: (h * nt + i, 0))
    part = pl.BlockSpec((tr, ccols), lambda h, i, s: (i, 0))
    return pl.pallas_call(
        body, name=name,
        grid_spec=pltpu.PrefetchScalarGridSpec(
            num_scalar_prefetch=1, grid=(2, nt), in_specs=[full, part, part, full, full], out_specs=[full] * 4),
        out_shape=[jax.ShapeDtypeStruct((r, ccols), F32)] * 4,
    )(place, w, mine, theirs, m, v)


def adamw_group(place, halved, plain, hosted, name):
    rows = halved[0][0].shape[0]
    tr = 64
    nt = rows // 2 // tr
    nh, npl = len(halved), len(plain)

    def body(place_ref, *refs):
        own_in, h_in, own_out, h_out, _, h_sems = hosted.split(refs, 5 * nh + 4 * npl, 4 * nh + 3 * npl)
        half = pl.program_id(0)
        grid_step = half * nt + pl.program_id(1)

        @pl.when(grid_step == 0)
        def _():
            hosted.start(h_in, h_out, h_sems)

        for i in range(nh):
            w_ref, a_ref, b_ref, m_ref, v_ref = own_in[5 * i:5 * i + 5]
            g = jnp.where(half == place_ref[0], a_ref[...], b_ref[...])
            res = (g,) + _adamw_math(w_ref[...], g, m_ref[...], v_ref[...])
            for o_ref, r in zip(own_out[4 * i:4 * i + 4], res):
                o_ref[...] = r
        for i in range(npl):
            w_ref, g_ref, m_ref, v_ref = own_in[5 * nh + 4 * i:5 * nh + 4 * i + 4]
            res = _adamw_math(w_ref[...], g_ref[...], m_ref[...], v_ref[...])
            for o_ref, r in zip(own_out[4 * nh + 3 * i:4 * nh + 3 * i + 3], res):
                o_ref[...] = r

        @pl.when(grid_step == 2 * nt - 1)
        def _():
            hosted.finish(h_in, h_out, h_sems)

    def full(cols):
        return pl.BlockSpec((tr, cols), lambda h, i, s: (h * nt + i, 0))

    def part(cols):
        return pl.BlockSpec((tr, cols), lambda h, i, s: (i, 0))

    in_specs, out_specs, out_shape, args = [], [], [], []
    for w, a, b, m, v in halved:
        cols = w.shape[1]
        in_specs += [full(cols), part(cols), part(cols), full(cols), full(cols)]
        out_specs += [full(cols)] * 4
        out_shape += [jax.ShapeDtypeStruct(w.shape, F32)] * 4
        args += [w, a, b, m, v]
    for w, g, m, v in plain:
        cols = w.shape[1]
        in_specs += [full(cols)] * 4
        out_specs += [full(cols)] * 3
        out_shape += [jax.ShapeDtypeStruct(w.shape, F32)] * 3
        args += [w, g, m, v]
    h_in_specs, h_out_specs = hosted.specs()
    return pl.pallas_call(
        body, name=name,
        grid_spec=pltpu.PrefetchScalarGridSpec(
            num_scalar_prefetch=1, grid=(2, nt), in_specs=in_specs + h_in_specs, out_specs=out_specs + h_out_specs,
            scratch_shapes=hosted.scratch),
        out_shape=out_shape + hosted.out_shape,
        compiler_params=_cp(("arbitrary", "arbitrary")),
    )(place, *args, *hosted.args)


def _silu(x):
    return x * jax.nn.sigmoid(x)


def prologue(c_rows, c_ctx_row, w_ada, b_shard, rpb_flat, half_w_in, late_shards):
    shape = jax.ShapeDtypeStruct
    n_late = len(late_shards)
    half_shapes = [(w.shape[0] // 2, w.shape[1]) for w in late_shards]
    g_w = gather8([half_w_in])
    g_c = gather8([shape((8, D), F32)])
    g_m = gather8([shape((32, 1536), F32)])

    def body(*refs):
        c_ref, cc_ref, w_ref, b_ref, flat_ref, hw_ref = refs[:6]
        late_refs = refs[6:6 + n_late]
        cin_ref, mg_ref, gw_ref, bias_ref, cos_ref, sin_ref = refs[6 + n_late:12 + n_late]
        rest = refs[12 + n_late:]
        half_refs, (cg_s, ms_s, bias_s) = rest[:n_late], rest[n_late:n_late + 3]
        stage, (load_sem, bias_sem), sems = rest[n_late + 3:2 * n_late + 3], rest[2 * n_late + 3:2 * n_late + 5], \
            rest[2 * n_late + 5:]
        sw, sc, sm = sems[0:3], sems[3:6], sems[6:9]
        core = lax.axis_index("c")
        g_w.start([hw_ref], [gw_ref], sw)
        g_c.start([c_ref], [cg_s], sc)
        loads = [pltpu.make_async_copy(late_refs[a].at[pl.ds(core * half_shapes[a][0], half_shapes[a][0]), :],
                                       stage[a], load_sem.at[a]) for a in range(n_late)]
        for cp in loads:
            cp.start()
        g_c.finish([c_ref], [cg_s], sc)
        cin_ref[...] = jnp.zeros_like(cin_ref)
        for dev in range(8):
            cin_ref[2 * dev:2 * dev + 2, :] = cg_s[dev, 0:2, :]
        cin_ref[16:17, :] = cc_ref[...]
        ms_s[...] = _nn(_silu(cin_ref[...]), w_ref[...]) + b_ref[...]
        g_m.start([ms_s], [mg_ref], sm)
        for a, cp in enumerate(loads):
            cp.wait()
            half_refs[a][...] = stage[a][...].astype(BF16)
        cos_ref[...], sin_ref[...] = _rope_tables()
        stores = []
        for pair in range(NPAIR):
            if pair >= 2:
                stores[pair - 2].wait()
            _na_bias_pair(flat_ref.at[pair], bias_s.at[pair % 2])
            stores.append(pltpu.make_async_copy(bias_s.at[pair % 2], bias_ref.at[pair], bias_sem.at[pair % 2]))
            stores[pair].start()
        for cp in stores[-2:]:
            cp.wait()
        g_m.finish([ms_s], [mg_ref], sm)
        g_w.finish([hw_ref], [gw_ref], sw)

    vmem = pl.BlockSpec(memory_space=pltpu.VMEM)
    hbm = pl.BlockSpec(memory_space=pl.ANY)
    return pl.pallas_call(
        body, name="prologue", in_specs=[vmem, vmem, vmem, vmem, vmem, hbm] + [hbm] * n_late,
        out_specs=[vmem, vmem, hbm, hbm, vmem, vmem] + [vmem] * n_late,
        out_shape=[shape((32, D), F32), shape((8, 32, 1536), F32)] + g_w.out_shape
        + [shape((NPAIR,) + NA_BIAS_SHAPE, F32)] + [shape((SEQ, RD), F32)] * 2 + [shape(s, BF16) for s in half_shapes],
        scratch_shapes=[pltpu.VMEM((8, 8, D), F32), pltpu.VMEM((32, 1536), F32), pltpu.VMEM((2,) + NA_BIAS_SHAPE, F32)]
        + [pltpu.VMEM(s, F32) for s in half_shapes]
        + [pltpu.SemaphoreType.DMA((n_late,)), pltpu.SemaphoreType.DMA((2,))]
        + g_w.scratch + g_c.scratch + g_m.scratch,
        compiler_params=_cp(),
    )(c_rows, c_ctx_row, w_ada, b_shard, rpb_flat, half_w_in, *late_shards)


def ada_grads(cin, gb, gc, w_ada):
    def body(c_ref, gb_ref, gc_ref, w_ref, gw_ref, pc_ref):
        ctx_tot = jnp.sum(gc_ref[...], axis=0, keepdims=True)
        rows = lax.broadcasted_iota(jnp.int32, (16, 512), 0)
        dm = jnp.concatenate([gb_ref[...], jnp.where(rows == 0, ctx_tot, 0.0)], axis=0)
        gw_ref[...] = _tn(_silu(c_ref[...]), dm)
        rows8 = lax.broadcasted_iota(jnp.int32, (8, 512), 0)
        part = _nt(jnp.where(rows8 == 0, ctx_tot, 0.0), w_ref[...])

        @pl.when(pl.program_id(0) == 0)
        def _():
            pc_ref[...] = jnp.zeros_like(pc_ref)

        pc_ref[...] += part

    return pl.pallas_call(
        body, name="ada_grads", grid=(3,),
        in_specs=[pl.BlockSpec((32, D), lambda j: (0, 0)), pl.BlockSpec((16, 512), lambda j: (0, j)),
                  pl.BlockSpec((8, 512), lambda j: (0, j)), pl.BlockSpec((D, 512), lambda j: (0, j))],
        out_specs=[pl.BlockSpec((D, 512), lambda j: (0, j)), pl.BlockSpec((8, D), lambda j: (0, 0))],
        out_shape=[jax.ShapeDtypeStruct((D, 1536), F32), jax.ShapeDtypeStruct((8, D), F32)],
    )(cin, gb, gc, w_ada)


SMALL_SUM_ROWS = 15


def small_update(gsm, gbf, gcf, pcg, params):
    n = len(params)

    def body(*refs):
        gsm_ref, gbf_ref, gcf_ref, pcg_ref = refs[:4]
        wmv, outs, loss_out = refs[4:4 + 3 * n], refs[4 + 3 * n:4 + 7 * n], refs[-1]
        acc = gsm_ref[0]
        for dev in range(1, 8):
            acc = acc + gsm_ref[dev]
        c_ctx = wmv[0][...]
        sg = jax.nn.sigmoid(c_ctx)
        dsilu = pcg_ref[0:1, :] + pcg_ref[2:3, :] + pcg_ref[4:5, :] + pcg_ref[6:7, :]
        lane = lax.broadcasted_iota(jnp.int32, (1, D), 1)
        last = acc[14:15, :]
        grads = [
            dsilu * (sg * (1.0 + c_ctx * (1.0 - sg))),
            jnp.sum(gbf_ref[...], axis=0, keepdims=True) + jnp.sum(gcf_ref[...], axis=0, keepdims=True),
            acc[0:1, :] + acc[1:2, :], acc[2:3, :], acc[3:4, :], acc[4:5, :],
            acc[5:6, 0:512], acc[6:14, :], jnp.where(lane < 8, last, 0.0),
        ]
        loss_out[...] = jnp.broadcast_to(jnp.sum(jnp.where(lane == 8, last, 0.0), axis=1, keepdims=True), (8, 128))
        for i, g in enumerate(grads):
            d, m2, v2 = _adamw_math(wmv[3 * i][...], g, wmv[3 * i + 1][...], wmv[3 * i + 2][...])
            outs[4 * i][...] = g
            outs[4 * i + 1][...] = d
            outs[4 * i + 2][...] = m2
            outs[4 * i + 3][...] = v2

    flat = [a for wmv in params for a in wmv]
    out_shape = [jax.ShapeDtypeStruct(w.shape, F32) for w, _, _ in params for _ in range(4)]
    return pl.pallas_call(
        body, name="small_update", out_shape=out_shape + [jax.ShapeDtypeStruct((8, 128), F32)],
    )(gsm, gbf, gcf, pcg, *flat)


def _pad_row(v, rows):
    flat = v.reshape(-1)
    return jnp.pad(flat, (0, rows * D - flat.shape[0])).reshape(rows, D)


def local_step(x, ctx, tgt, mod3, rope, bias, g_pre_mix, g_post_mix, g_pre_mlp, g_post_mlp, ret_decay, ret_gn,
               wperm, late_weights, early_grads):
    nb = x.shape[0]
    tokens = nb * SEQ
    cos, sin = rope
    rd = ret_decay.T.reshape(RH, 2, 1)
    gn = ret_gn.reshape(RH, 1, RD)
    h, pret, pna = premix_proj(x, mod3, g_pre_mix, wperm, False, "premix_proj")
    hc, pretc, pnac = premix_proj(ctx, mod3, g_pre_mix, wperm, True, "premix_proj_ctx")
    o_all, mixin, gw_out = retention_fwd(pret, pretc, rd, gn, cos, sin, late_weights(0))
    mixin, gw1, gw2 = na_fwd(pna, pnac, bias, mixin, late_weights(1))
    dx_tail, dmix, h2, du, act, dm, dmixin, dmod_t, dg_t, loss_t = tail_fwd_bwd(
        x, mixin, tgt, mod3, g_post_mix, g_pre_mlp, g_post_mlp, gw_out.reshape(D, D), gw1.reshape(4, D, D),
        gw2.reshape(DFF, D))
    dw_out = weight_grad([(mixin.reshape(tokens, D), dmix.reshape(tokens, D))], "grad_w_out", BF16)
    dw1 = weight_grad([(h2.reshape(tokens, D), du.reshape(tokens, DFF))], "grad_w_mlp1", BF16, col_blocks=True)
    dw2 = weight_grad([(act.reshape(tokens, DFF), dm.reshape(tokens, D))], "grad_w_mlp2", BF16)
    dproj, dprojc, drd, dgn, *landed = retention_bwd(pret, pretc, o_all, dmixin, rd, gn, cos, sin,
                                                     early_grads[0](dw_out, dw1, dw2))
    dproj, dprojc, dpat, *early = na_bwd(pna, pnac, bias, dmixin, dproj, dprojc, early_grads[1](landed))
    dw_in = weight_grad([(h.reshape(tokens, D), dproj.reshape(tokens, IN_W)),
                         (hc.reshape(nb * LC, D), dprojc.reshape(nb * LC, IN_W))], "grad_w_in", tn=IN_W // 2, tk=1024)
    grad_x, dmod_a, dg_a, *late = premix_bwd(x, mod3, g_pre_mix, wperm, dproj, dx_tail, early_grads[2](dw_in),
                                             "premix_bwd")
    dmod_c, dg_c = premix_bwd(ctx, mod3, g_pre_mix, wperm, dprojc, None, no_exchange(), "premix_bwd_ctx")
    dmod = jnp.concatenate([jnp.concatenate([dmod_a[:, 0:2], dmod_t[:, 2:6]], axis=1), dmod_c], axis=0)
    last = jnp.pad(jnp.concatenate([drd[:, :, 0].T.reshape(8), loss_t[0, 0:1]]), (0, D - 9)).reshape(1, D)
    small = jnp.concatenate([dg_a[0:1], dg_c[0:1], dg_t[0:3], _pad_row(dgn, 1), dpat.reshape(8, D), last], axis=0)
    return grad_x, late, early, dmod, small


def kernel(x, c, ctx, c_ctx, w_ada, b_ada, g_pre_mix, g_post_mix, g_pre_mlp, g_post_mlp, w_in, ret_decay, ret_gn, na_rpb, w_out, w_mlp1, w_mlp2, loss_target, m_c_ctx, m_w_ada, m_b_ada, m_g_pre_mix, m_g_post_mix, m_g_pre_mlp, m_g_post_mlp, m_w_in, m_ret_decay, m_ret_gn, m_na_rpb, m_w_out, m_w_mlp1, m_w_mlp2, v_c_ctx, v_w_ada, v_b_ada, v_g_pre_mix, v_g_post_mix, v_g_pre_mlp, v_g_post_mlp, v_w_in, v_ret_decay, v_ret_gn, v_na_rpb, v_w_out, v_w_mlp1, v_w_mlp2):
    px, py, pc = _place()
    dev = 4 * px + 2 * py + pc
    chip = 2 * px + py

    half_w_in = lax.dynamic_slice_in_dim(w_in[0], pc * (D // 2), D // 2, 0).astype(BF16)
    cin, mg, gw_in, bias, cos, sin, *late_halves = prologue(
        jnp.pad(c, ((0, 6), (0, 0))), c_ctx[None], w_ada[0], lax.dynamic_slice_in_dim(b_ada, chip * 1536, 1536, 1),
        _rpb_flat(na_rpb[0]), half_w_in, [w_out[0], w_mlp1[0], w_mlp2[0]])
    halves = [half_w_in] + late_halves
    wperm = unpack_w_in(gw_in.reshape(4, D, 896))
    mod_all = jnp.concatenate([mg[0], mg[2], mg[4], mg[6]], axis=1)
    mod3 = (jnp.pad(lax.dynamic_slice_in_dim(mod_all, 2 * dev, 2, 0), ((0, 1), (0, 0)))
            + jnp.pad(mod_all[16:17], ((2, 0), (0, 0)))).reshape(3, 6, D)

    place = jnp.stack([pc, chip]).astype(jnp.int32)

    early_names = ["w_out", "w_mlp1", "w_mlp2"]
    early_g8, early_partial = [], []

    def early_a(dw_out, dw1, dw2):
        early_g8[:] = [dw_out.reshape(8, 128, D), dw1.reshape(8, 512, D), dw2.reshape(8, 512, D)]
        return siblings4(early_g8)

    def early_b(landed):
        early_partial[:] = chip_partial(place, early_g8, landed, "rs_chip_sum_early")
        return chips3(early_partial)

    late_partial = []

    def late_c(dw_in):
        g8_in = pack_w_in(dw_in).reshape(8, 512, 896)
        (landed_in,) = sibling_blocks([g8_in], "rs_sibling_w_in")
        late_partial[:] = chip_partial(place, [g8_in], [landed_in], "rs_chip_sum_w_in")
        return chips3(late_partial)

    grad_x, (landed3_in,), early_landed, dmod, small = local_step(
        x, ctx, loss_target, mod3, (cos, sin), bias, g_pre_mix, g_post_mix, g_pre_mlp, g_post_mlp, ret_decay[0], ret_gn,
        wperm, lambda k: gather8(halves[1:2] if k == 0 else halves[2:4]), (early_a, early_b, late_c))
    early_mine = shard_sum(place, early_partial, early_landed, "rs_shard_sum_early")

    pay = jnp.concatenate([dmod.reshape(18, D), small, jnp.zeros((40 - 18 - SMALL_SUM_ROWS, D), F32)], axis=0)
    *early_theirs, gs = run_hosted(both(siblings(early_mine), gather8([pay])), "rs_halves_early_gather_small")
    gbf = gs[:, 0:12].reshape(16, 6 * D)
    gcf = gs[:, 12:18].reshape(8, 6 * D)
    gw_ada, pc_part = ada_grads(cin, lax.dynamic_slice_in_dim(gbf, chip * 1536, 1536, 1),
                                lax.dynamic_slice_in_dim(gcf, chip * 1536, 1536, 1), w_ada[0])
    (mine_in,) = shard_sum(place, late_partial, [landed3_in], "rs_shard_sum_w_in")
    theirs_in, pcg = run_hosted(both(siblings([mine_in]), gather8([pc_part])), "rs_halves_w_in_gather_c_ctx")

    grouped = adamw_group(
        place,
        [(w_mlp1[0], early_mine[1], early_theirs[1], m_w_mlp1[0], v_w_mlp1[0]),
         (w_mlp2[0], early_mine[2], early_theirs[2], m_w_mlp2[0], v_w_mlp2[0])],
        [(w_ada[0], gw_ada, m_w_ada[0], v_w_ada[0])], no_exchange(), "adamw_group")
    d_ada, m_ada, v_ada = grouped[8:11]
    big = [
        [r[None] for r in adamw_halves(place, w_in[0], mine_in, theirs_in, m_w_in[0], v_w_in[0], "adamw_w_in")],
        [r[None] for r in adamw_halves(place, w_out[0], early_mine[0], early_theirs[0], m_w_out[0], v_w_out[0],
                                       "adamw_w_out")],
        [r[None] for r in grouped[0:4]], [r[None] for r in grouped[4:8]],
    ]

    def rpb_rows(t):
        return _rpb_flat(t[0]).reshape(8, D)

    def decay_row(t):
        return jnp.pad(t.reshape(1, 8), ((0, 0), (0, D - 8)))

    views = [lambda t: t.reshape(1, D), lambda t: t, lambda t: t, lambda t: t, lambda t: t, lambda t: t, lambda t: t,
             rpb_rows, decay_row]
    back = [lambda t: t.reshape(D), lambda t: t, lambda t: t, lambda t: t, lambda t: t, lambda t: t, lambda t: t,
            lambda t: _rpb_flat_t(t)[None], lambda t: t[:, 0:8].reshape(1, 2, 4)]
    small_w = (c_ctx, b_ada, g_pre_mix, g_post_mix, g_pre_mlp, g_post_mlp, ret_gn, na_rpb, ret_decay)
    small_m = (m_c_ctx, m_b_ada, m_g_pre_mix, m_g_post_mix, m_g_pre_mlp, m_g_post_mlp, m_ret_gn, m_na_rpb, m_ret_decay)
    small_v = (v_c_ctx, v_b_ada, v_g_pre_mix, v_g_post_mix, v_g_pre_mlp, v_g_post_mlp, v_ret_gn, v_na_rpb, v_ret_decay)
    *res, loss8 = small_update(gs[:, 18:18 + SMALL_SUM_ROWS], gbf, gcf, pcg[:, 0],
                               [(f(w), f(m), f(v)) for f, w, m, v in zip(views, small_w, small_m, small_v)])

    def leaves(ada, idx):
        s_c, s_b, s_g1, s_g2, s_g3, s_g4, s_gn, s_rpb, s_rd = [back[i](res[4 * i + idx]) for i in range(9)]
        return [s_c, ada[None], s_b, s_g1, s_g2, s_g3, s_g4, big[0][idx], s_rd, s_gn, s_rpb,
                big[1][idx], big[2][idx], big[3][idx]]

    return (loss8[0, 0], grad_x, *leaves(gw_ada, 0), *leaves(d_ada, 1), *leaves(m_ada, 2), *leaves(v_ada, 3))
```

```python
import functools
import math

import jax
import jax.numpy as jnp
from jax import lax
from jax.experimental import pallas as pl
from jax.experimental.pallas import tpu as pltpu

F32, BF16 = jnp.float32, jnp.bfloat16
D = 1024
SEQ = 2048
LC = 256
GW = 64
RH, RD, CH = 4, 128, 128
NPAIR = 4
IN_W = 3584
RET_W = 2048
DFF = 4096
EPS = 1e-6
NEG = -1e30
TN = 256
NCH = SEQ // CH
LR, B1, B2, AEPS, WD, STEP = 0.001, 0.9, 0.999, 1e-08, 0.01, 10
MESH = pl.DeviceIdType.MESH
VMEM_LIMIT = 56 * 1024 * 1024


def _cp(sem=None):
    return pltpu.CompilerParams(dimension_semantics=sem, vmem_limit_bytes=VMEM_LIMIT)


def _nn(a, b):
    return jnp.dot(a.astype(BF16), b.astype(BF16), preferred_element_type=F32)


def _nt(a, b):
    return lax.dot_general(a.astype(BF16), b.astype(BF16), (((1,), (1,)), ((), ())), preferred_element_type=F32)


def _tn(a, b):
    return lax.dot_general(a.astype(BF16), b.astype(BF16), (((0,), (0,)), ((), ())), preferred_element_type=F32)


@jax.custom_vjp
def mm_tn(a, b):
    return _tn(a, b)


mm_tn.defvjp(lambda a, b: (_tn(a, b), (a, b)), lambda r, g: (_nt(r[1], g), _nn(r[0], g)))


def _rms(x):
    return x * lax.rsqrt(jnp.mean(x * x, axis=-1, keepdims=True) + EPS)


def _rms_mod(x, g, sc, sh):
    return (_rms(x) * g) * (1.0 + sc) + sh


def _post_mix(x, mix, gt1, sc2, sh2, g_post_mix, g_pre_mlp):
    x1 = x + gt1 * (_rms(mix) * g_post_mix)
    return x1, _rms_mod(x1, g_pre_mlp, sc2, sh2)


def _head_loss(x1, m, gt2, g_post_mlp, tgt):
    err = x1 + gt2 * (_rms(m) * g_post_mlp) - tgt
    return 0.5 * jnp.sum(jnp.mean(err * err, axis=-1, keepdims=True), axis=0, keepdims=True)


def _ln_gate(o, g, w):
    mu = jnp.mean(o, axis=-1, keepdims=True)
    var = jnp.mean(jnp.square(o - mu), axis=-1, keepdims=True)
    y = (o - mu) * lax.rsqrt(var + EPS)
    return (y * w) * (g * jax.nn.sigmoid(g))


def _swap32(x):
    lane = lax.broadcasted_iota(jnp.int32, x.shape, 1)
    return jnp.where((lane & 32) == 0, pltpu.roll(x, 96, 1), pltpu.roll(x, 32, 1))


def _rope(x, cos, sin):
    return x * cos + _swap32(x) * sin


def _rope_t(g, cos, sin):
    return g * cos + _swap32(g * sin)


def _rope_tables():
    tok = lax.broadcasted_iota(jnp.int32, (SEQ, RD), 0)
    lane = lax.broadcasted_iota(jnp.int32, (SEQ, RD), 1)
    pos = jnp.where(lane < 64, tok >> 6, tok & (GW - 1)).astype(F32)
    ang = pos * jnp.exp((lane & 31).astype(F32) * (-math.log(10000.0) / 32))
    return jnp.cos(ang), jnp.where((lane & 32) == 0, -jnp.sin(ang), jnp.sin(ang))


def _chunk_loop(n, body, init, k=4):
    def several(t, carry):
        for i in range(k):
            carry = body(k * t + i, carry)
        return carry

    return lax.fori_loop(0, n // k, several, init)


def _fiota(shape, dim):
    return lax.broadcasted_iota(jnp.int32, shape, dim).astype(F32)


def _ret_state(k, v, s, lg, reverse):
    pos = _fiota((CH, 1), 0)
    b_exp = pos if reverse else (CH - 1.0 - pos)
    return jnp.exp(lg * CH) * s + mm_tn(k * jnp.exp(lg * b_exp), v)


class _Decays:
    def __init__(self, lgs):
        i, j, pos = _fiota((CH, CH), 0), _fiota((CH, CH), 1), _fiota((CH, 1), 0)
        diffs = (i - j, j - i)
        keep = (diffs[0] >= 0, diffs[1] > 0)
        mats = [jnp.where(m, jnp.exp(lg * jnp.where(m, d, 0.0)), 0.0) for lg, d, m in zip(lgs, diffs, keep)]
        self.mask = mats[0] + mats[1]
        self.dmask = [mats[0] * diffs[0], mats[1] * diffs[1]]
        a_exp, b_exp = (pos + 1.0, CH - pos), (CH - 1.0 - pos, pos)
        self.a = [jnp.exp(lg * e) for lg, e in zip(lgs, a_exp)]
        self.b = [jnp.exp(lg * e) for lg, e in zip(lgs, b_exp)]
        self.da = [a * e for a, e in zip(self.a, a_exp)]
        self.db = [b * e for b, e in zip(self.b, b_exp)]
        self.g = [jnp.exp(lg * CH) for lg in lgs]


def _both(x, w):
    return jnp.concatenate([x * w[0], x * w[1]], axis=1)


def _total(x):
    return jnp.sum(jnp.sum(x, axis=1, keepdims=True), axis=0, keepdims=True)


def _state_pass(dec, init, k_s, v_of, st_s):
    def step(t, carry):
        out = []
        for d, s in enumerate(carry):
            n = (NCH - 1 - t) if d else t
            sl = pl.ds(pl.multiple_of(n * CH, CH), CH)
            st_s[n, d * RD:(d + 1) * RD, :] = s
            out.append(dec.g[d] * s + _tn(k_s[sl, :] * dec.b[d], v_of(sl)))
        return tuple(out)

    _chunk_loop(NCH, step, tuple(init))


def premix_proj(xin, mod3, g_pre, wperm, is_ctx, name):
    nb, length, _ = xin.shape
    tn = min(2 * TN, length)

    def body(x_ref, mod_ref, g_ref, w_ref, h_ref, pret_ref, pna_ref):
        h = _rms_mod(x_ref[...], g_ref[...], mod_ref[1:2, :], mod_ref[0:1, :])
        hb = h.astype(BF16)
        h_ref[...] = hb
        pret_ref[...] = jnp.dot(hb, w_ref[:, :RET_W], preferred_element_type=F32)
        pna_ref[...] = jnp.dot(hb, w_ref[:, RET_W:], preferred_element_type=F32).astype(BF16)

    return pl.pallas_call(
        body, name=name, grid=(nb, length // tn),
        in_specs=[
            pl.BlockSpec((None, tn, D), lambda b, t: (b, t, 0)),
            pl.BlockSpec((None, 6, D), (lambda b, t: (2, 0, 0)) if is_ctx else (lambda b, t: (b, 0, 0))),
            pl.BlockSpec((1, D), lambda b, t: (0, 0)),
            pl.BlockSpec((D, IN_W), lambda b, t: (0, 0), pipeline_mode=pl.Buffered(1)),
        ],
        out_specs=[
            pl.BlockSpec((None, tn, D), lambda b, t: (b, t, 0)),
            pl.BlockSpec((None, tn, RET_W), lambda b, t: (b, t, 0)),
            pl.BlockSpec((None, tn, IN_W - RET_W), lambda b, t: (b, t, 0)),
        ],
        out_shape=[
            jax.ShapeDtypeStruct((nb, length, D), BF16),
            jax.ShapeDtypeStruct((nb, length, RET_W), F32),
            jax.ShapeDtypeStruct((nb, length, IN_W - RET_W), BF16),
        ],
        compiler_params=_cp(("arbitrary", "arbitrary")),
    )(xin, mod3, g_pre, wperm)


def premix_bwd(xin, mod3, g_pre, wperm, dproj, dx_tail, hosted, name):
    nb, length, _ = xin.shape
    tn = min(2 * TN, length)
    is_ctx = dx_tail is None

    def body(*refs):
        own_in, h_in, own_out, h_out, _, h_sems = hosted.split(refs, 5 if is_ctx else 6, 2 if is_ctx else 3)
        if is_ctx:
            (x_ref, mod_ref, g_ref, w_ref, dp_ref), (dmod_ref, dg_ref) = own_in, own_out
        else:
            (x_ref, mod_ref, g_ref, w_ref, dp_ref, dxt_ref), (dx_ref, dmod_ref, dg_ref) = own_in, own_out
        b, t = pl.program_id(0), pl.program_id(1)
        grid_step = b * (length // tn) + t

        @pl.when(grid_step == 0)
        def _():
            hosted.start(h_in, h_out, h_sems)

        @pl.when(grid_step == nb * (length // tn) - 1)
        def _():
            hosted.finish(h_in, h_out, h_sems)

        dh = lax.dot_general(dp_ref[...], w_ref[...], (((1,), (1,)), ((), ())), preferred_element_type=F32)
        _, vjp = jax.vjp(_rms_mod, x_ref[...], g_ref[...], mod_ref[1:2, :], mod_ref[0:1, :])
        dx, dg, dsc, dsh = vjp(dh)
        if not is_ctx:
            dx_ref[...] = dx + dxt_ref[...]

        @pl.when((t == 0) & ((b == 0) if is_ctx else True))
        def _():
            dmod_ref[...] = jnp.zeros_like(dmod_ref)

        @pl.when((t == 0) & (b == 0))
        def _():
            dg_ref[...] = jnp.zeros_like(dg_ref)

        dmod_ref[0:1, :] += dsh
        dmod_ref[1:2, :] += dsc
        dg_ref[0:1, :] += dg

    tok = lambda b, t: (b, t, 0)
    in_specs = [
        pl.BlockSpec((None, tn, D), tok),
        pl.BlockSpec((None, 6, D), (lambda b, t: (2, 0, 0)) if is_ctx else (lambda b, t: (b, 0, 0))),
        pl.BlockSpec((1, D), lambda b, t: (0, 0)),
        pl.BlockSpec((D, IN_W), lambda b, t: (0, 0), pipeline_mode=pl.Buffered(1)),
        pl.BlockSpec((None, tn, IN_W), tok),
    ]
    args = [xin, mod3, g_pre, wperm, dproj]
    out_specs = [
        pl.BlockSpec((None, 6, D), (lambda b, t: (0, 0, 0)) if is_ctx else (lambda b, t: (b, 0, 0))),
        pl.BlockSpec((8, D), lambda b, t: (0, 0)),
    ]
    out_shape = [jax.ShapeDtypeStruct((1 if is_ctx else nb, 6, D), F32), jax.ShapeDtypeStruct((8, D), F32)]
    if not is_ctx:
        in_specs.append(pl.BlockSpec((None, tn, D), tok))
        args.append(dx_tail)
        out_specs.insert(0, pl.BlockSpec((None, tn, D), tok))
        out_shape.insert(0, jax.ShapeDtypeStruct((nb, length, D), F32))
    h_in_specs, h_out_specs = hosted.specs()
    return pl.pallas_call(
        body, name=name, grid=(nb, length // tn), in_specs=in_specs + h_in_specs, out_specs=out_specs + h_out_specs,
        out_shape=out_shape + hosted.out_shape, scratch_shapes=hosted.scratch,
        compiler_params=_cp(("arbitrary", "arbitrary")),
    )(*args, *hosted.args)


def _ret_specs(order):
    def im(f):
        return lambda *g: f(*order(*g))
    return dict(
        pret=pl.BlockSpec((None, SEQ, 512), im(lambda b, h: (b, 0, h))),
        pretc=pl.BlockSpec((None, LC, 512), im(lambda b, h: (b, 0, h))),
        rd=pl.BlockSpec((None, 2, 1), im(lambda b, h: (h, 0, 0))),
        gn=pl.BlockSpec((None, 1, RD), im(lambda b, h: (h, 0, 0))),
        tab=pl.BlockSpec((SEQ, RD), im(lambda b, h: (0, 0))),
        head=pl.BlockSpec((None, SEQ, RD), im(lambda b, h: (b, 0, h))),
    )


def retention_fwd(pret, pretc, rd, gn, cos, sin, hosted):
    nb = pret.shape[0]
    sp = _ret_specs(lambda b, h: (b, h))

    def body(*refs):
        own_in, h_in, own_out, h_out, own_scr, h_sems = hosted.split(refs, 6, 2)
        p_ref, pc_ref, rd_ref, gn_ref, cos_ref, sin_ref = own_in
        (o_ref, mix_ref), (q_s, k_s, o_s, st_s) = own_out, own_scr
        grid_step = pl.program_id(0) * RH + pl.program_id(1)

        @pl.when(grid_step == 0)
        def _():
            hosted.start(h_in, h_out, h_sems)

        cos_v, sin_v = cos_ref[...], sin_ref[...]
        q_s[...] = _rope(p_ref[:, 0:128], cos_v, sin_v) * (RD ** -0.5)
        k_s[...] = _rope(p_ref[:, 128:256], cos_v, sin_v)
        lgs, init = [], []
        for rev in (False, True):
            lg = jax.nn.log_sigmoid(rd_ref[int(rev):int(rev) + 1, :])
            s = jnp.zeros((RD, RD), F32)
            for n in ((1, 0) if rev else (0, 1)):
                s = _ret_state(pc_ref[n * CH:(n + 1) * CH, 128:256], pc_ref[n * CH:(n + 1) * CH, 256:384], s, lg, rev)
            lgs.append(lg)
            init.append(s)

        dec = _Decays(lgs)
        _state_pass(dec, init, k_s, lambda sl: p_ref[sl, 256:384], st_s)

        def chunk(n, carry):
            sl = pl.ds(pl.multiple_of(n * CH, CH), CH)
            q = q_s[sl, :]
            o_s[sl, :] = (_nn(_nt(q, k_s[sl, :]) * dec.mask, p_ref[sl, 256:384]) + _nn(_both(q, dec.a), st_s[n]))
            return carry

        _chunk_loop(NCH, chunk, 0)
        o = o_s[...]
        o_ref[...] = o
        mix_ref[...] = _ln_gate(o, p_ref[:, 384:512], gn_ref[...]).astype(BF16)

        @pl.when(grid_step == nb * RH - 1)
        def _():
            hosted.finish(h_in, h_out, h_sems)

    h_in_specs, h_out_specs = hosted.specs()
    return pl.pallas_call(
        body, name="retention_fwd", grid=(nb, RH),
        in_specs=[sp["pret"], sp["pretc"], sp["rd"], sp["gn"], sp["tab"], sp["tab"]] + h_in_specs,
        out_specs=[sp["head"], sp["head"]] + h_out_specs,
        out_shape=[jax.ShapeDtypeStruct((nb, SEQ, RH * RD), F32), jax.ShapeDtypeStruct((nb, SEQ, D), BF16)]
        + hosted.out_shape,
        scratch_shapes=[pltpu.VMEM((SEQ, RD), F32)] * 3 + [pltpu.VMEM((NCH, 2 * RD, RD), F32)] + hosted.scratch,
        compiler_params=_cp(("arbitrary", "arbitrary")),
    )(pret, pretc, rd, gn, cos, sin, *hosted.args)


def retention_bwd(pret, pretc, o_all, dmixin, rd, gn, cos, sin, hosted):
    nb = pret.shape[0]
    sp = _ret_specs(lambda h, b: (b, h))

    def body(*refs):
        own_in, h_in, own_out, h_out, own_scr, h_sems = hosted.split(refs, 8, 4)
        p_ref, pc_ref, o_ref, dmix_ref, rd_ref, gn_ref, cos_ref, sin_ref = own_in
        dp_ref, dpc_ref, drd_ref, dgn_ref = own_out
        q_s, k_s, do_s, dq_s, dk_s, dv_s, st_s, gst_s = own_scr
        b = pl.program_id(1)
        grid_step = pl.program_id(0) * nb + b

        @pl.when(grid_step == 0)
        def _():
            hosted.start(h_in, h_out, h_sems)

        cos_v, sin_v = cos_ref[...], sin_ref[...]
        q_s[...] = _rope(p_ref[:, 0:128], cos_v, sin_v) * (RD ** -0.5)
        k_s[...] = _rope(p_ref[:, 128:256], cos_v, sin_v)
        _, gate_vjp = jax.vjp(_ln_gate, o_ref[...], p_ref[:, 384:512], gn_ref[...])
        do, dg, dgn = gate_vjp(dmix_ref[...].astype(F32))
        do_s[...] = do
        dp_ref[:, 384:512] = dg.astype(BF16)

        @pl.when(b == 0)
        def _():
            drd_ref[...] = jnp.zeros_like(drd_ref)
            dgn_ref[...] = jnp.zeros_like(dgn_ref)

        dgn_ref[...] += dgn
        kcs = [pc_ref[n * CH:(n + 1) * CH, 128:256] for n in (0, 1)]
        vcs = [pc_ref[n * CH:(n + 1) * CH, 256:384] for n in (0, 1)]
        dirs = []
        init = []
        for rev in (False, True):
            rdv = rd_ref[int(rev):int(rev) + 1, :]
            lg = jax.nn.log_sigmoid(rdv)
            order_c = (1, 0) if rev else (0, 1)
            s = jnp.zeros((RD, RD), F32)
            ctx_states = []
            for n in order_c:
                ctx_states.append(s)
                s = _ret_state(kcs[n], vcs[n], s, lg, rev)
            dirs.append((rev, order_c, lg, rdv, ctx_states))
            init.append(s)
        dec = _Decays([lg for _, _, lg, _, _ in dirs])

        def v_of(sl):
            return p_ref[sl, 256:384]

        _state_pass(dec, init, k_s, v_of, st_s)
        zeros = jnp.zeros((CH, RD), F32)

        def scores_back(n, carry):
            dmask_sum, da_f, da_b = carry
            sl = pl.ds(pl.multiple_of(n * CH, CH), CH)
            q, k, v, do = q_s[sl, :], k_s[sl, :], v_of(sl), do_s[sl, :]
            scores = _nt(q, k)
            d_att = _nt(do, v)
            d_scores = d_att * dec.mask
            d_qa = _nt(do, st_s[n])
            d_qf, d_qb = d_qa[:, 0:RD], d_qa[:, RD:2 * RD]
            dq_s[sl, :] = _nn(d_scores, k) + d_qf * dec.a[0] + d_qb * dec.a[1]
            dk_s[sl, :] = _tn(d_scores, q)
            dv_s[sl, :] = _tn(scores * dec.mask, do)
            gst_s[n] = _tn(_both(q, dec.a), do)
            return dmask_sum + d_att * scores, da_f + d_qf * q, da_b + d_qb * q

        dmask_sum, da_f, da_b = _chunk_loop(NCH, scores_back, (zeros, zeros, zeros))

        def state_back(t, carry):
            out = []
            for d, r in enumerate(carry):
                n = t if d else (NCH - 1 - t)
                rows = slice(d * RD, (d + 1) * RD)
                own = gst_s[n, rows, :]
                gst_s[n, rows, :] = r
                out.append(own + dec.g[d] * r)
            return tuple(out)

        d_states = _chunk_loop(NCH, state_back, (zeros, zeros))

        def updates_back(n, carry):
            db_f, db_b, dg_f, dg_b = carry
            sl = pl.ds(pl.multiple_of(n * CH, CH), CH)
            k, r, s = k_s[sl, :], gst_s[n], st_s[n]
            d_kw = _nt(v_of(sl), r)
            d_kf, d_kb = d_kw[:, 0:RD], d_kw[:, RD:2 * RD]
            dk_s[sl, :] += d_kf * dec.b[0] + d_kb * dec.b[1]
            dv_s[sl, :] += _nn(_both(k, dec.b), r)
            return (db_f + d_kf * k, db_b + d_kb * k, dg_f + r[0:RD, :] * s[0:RD, :],
                    dg_b + r[RD:2 * RD, :] * s[RD:2 * RD, :])

        db_dg = _chunk_loop(NCH, updates_back, (zeros, zeros, zeros, zeros))
        dkc = [None, None]
        dvc = [None, None]
        for d, ((rev, order_c, lg, rdv, ctx_states), ds) in enumerate(zip(dirs, d_states)):
            dlg = (_total(dmask_sum * dec.dmask[d]) + _total((da_f, da_b)[d] * dec.da[d])
                   + _total(db_dg[d] * dec.db[d]) + CH * dec.g[d] * _total(db_dg[2 + d]))
            for idx in (1, 0):
                n = order_c[idx]
                _, vjp = jax.vjp(functools.partial(_ret_state, reverse=rev), kcs[n], vcs[n], ctx_states[idx], lg)
                dk_c, dv_c, ds, dl = vjp(ds)
                dlg = dlg + dl
                dkc[n] = dk_c if dkc[n] is None else dkc[n] + dk_c
                dvc[n] = dv_c if dvc[n] is None else dvc[n] + dv_c
            drd_ref[int(rev):int(rev) + 1, :] += dlg * jax.nn.sigmoid(-rdv)
        dp_ref[:, 0:128] = _rope_t(dq_s[...] * (RD ** -0.5), cos_v, sin_v).astype(BF16)
        dp_ref[:, 128:256] = _rope_t(dk_s[...], cos_v, sin_v).astype(BF16)
        dp_ref[:, 256:384] = dv_s[...].astype(BF16)
        zero = jnp.zeros((CH, RD), BF16)
        for n in (0, 1):
            rows = slice(n * CH, (n + 1) * CH)
            dpc_ref[rows, 0:128] = zero
            dpc_ref[rows, 128:256] = dkc[n].astype(BF16)
            dpc_ref[rows, 256:384] = dvc[n].astype(BF16)
            dpc_ref[rows, 384:512] = zero

        @pl.when(grid_step == RH * nb - 1)
        def _():
            hosted.finish(h_in, h_out, h_sems)

    h_in_specs, h_out_specs = hosted.specs()
    return pl.pallas_call(
        body, name="retention_bwd", grid=(RH, nb),
        in_specs=[sp["pret"], sp["pretc"], sp["head"], sp["head"], sp["rd"], sp["gn"], sp["tab"], sp["tab"]]
        + h_in_specs,
        out_specs=[
            pl.BlockSpec((None, SEQ, 512), lambda h, b: (b, 0, h)),
            pl.BlockSpec((None, LC, 512), lambda h, b: (b, 0, h)),
            pl.BlockSpec((None, 2, 1), lambda h, b: (h, 0, 0)),
            pl.BlockSpec((None, 1, RD), lambda h, b: (h, 0, 0)),
        ] + h_out_specs,
        out_shape=[
            jax.ShapeDtypeStruct((nb, SEQ, IN_W), BF16),
            jax.ShapeDtypeStruct((nb, LC, IN_W), BF16),
            jax.ShapeDtypeStruct((RH, 2, 1), F32),
            jax.ShapeDtypeStruct((RH, 1, RD), F32),
        ] + hosted.out_shape,
        scratch_shapes=[pltpu.VMEM((SEQ, RD), F32)] * 6 + [pltpu.VMEM((NCH, 2 * RD, RD), F32)] * 2 + hosted.scratch,
        compiler_params=_cp(("arbitrary", "arbitrary")),
    )(pret, pretc, o_all, dmixin, rd, gn, cos, sin, *hosted.args)


def _rpb_flat(rpb):
    return jnp.pad(rpb, ((0, 0), (0, 1), (0, 33))).reshape(NPAIR, 2, 1, 1024)


def _rpb_flat_t(dflat):
    return dflat.reshape(8, 16, 64)[:, :15, :31]


def _barrel(x, left):
    row = lax.broadcasted_iota(jnp.int32, x.shape, 0)
    n = x.shape[1]
    for bit in range(6):
        s = 1 << bit
        x = jnp.where(((row >> bit) & 1) == 1, pltpu.roll(x, (n - s) if left else s, 1), x)
    return x


NA_TILE_ROWS, NA_BAND_ROWS = 4, 12
NA_Q, NA_K = NA_TILE_ROWS * GW, NA_BAND_ROWS * GW
NA_TILES = SEQ // NA_Q


def _band_start(r0):
    return min(max(r0 - 4, 0), 32 - NA_BAND_ROWS)


def _tile_layout(t):
    rows = range(t * NA_TILE_ROWS, (t + 1) * NA_TILE_ROWS)
    return tuple((r if r < 4 else (r - 24 if r > 28 else 4), min(max(r - 4, 0), 24) - _band_start(rows[0]))
                 for r in rows)


NA_CLASSES = sorted(set(_tile_layout(t) for t in range(NA_TILES)))


def _tile_rows(cls):
    return NA_CLASSES[cls]


def _na_tile(t):
    start = jnp.clip(NA_TILE_ROWS * t - 4, 0, 32 - NA_BAND_ROWS)
    cls = 0
    for tile in range(NA_TILES):
        cls = jnp.where(t == tile, NA_CLASSES.index(_tile_layout(tile)), cls)
    return pl.ds(pl.multiple_of(t * NA_Q, NA_Q), NA_Q), pl.ds(pl.multiple_of(start * GW, NA_Q), NA_K), cls


def _na_probs(qst, kb, kc, bias):
    s_loc = _nt(qst, kb) + bias
    s_ctx = _nt(qst, kc)
    m = jnp.maximum(jnp.max(s_loc, axis=1, keepdims=True), jnp.max(s_ctx, axis=1, keepdims=True))
    e_loc, e_ctx = jnp.exp(s_loc - m), jnp.exp(s_ctx - m)
    den = jnp.sum(e_loc, axis=1, keepdims=True) + jnp.sum(e_ctx, axis=1, keepdims=True)
    return e_loc / den, e_ctx / den


def _stack_heads(t):
    lane = lax.broadcasted_iota(jnp.int32, t.shape, 1)
    zero = jnp.zeros_like(t)
    return jnp.concatenate([jnp.where(lane < 64, t, zero), jnp.where(lane >= 64, t, zero)], axis=0)


def _unstack_heads(t):
    n = t.shape[0] // 2
    lane = lax.broadcasted_iota(jnp.int32, (n, 128), 1)
    return jnp.where(lane < 64, t[:n], t[n:])


NA_BIAS_SHAPE = (len(NA_CLASSES), 2 * NA_Q, NA_K)


def _na_bias_pair(flat_ref, out_ref):
    qc = lax.broadcasted_iota(jnp.int32, (GW, 512), 0)
    kc = lax.broadcasted_iota(jnp.int32, (GW, 512), 1) & 63
    start = jnp.clip(qc - 8, 0, GW - 16)
    window = (kc >= start) & (kc < start + 16)
    fill = jnp.full((GW, NA_K - 512), NEG, F32)
    for hh in (0, 1):
        skew = _barrel(pltpu.roll(jnp.broadcast_to(flat_ref[hh], (GW, 1024)), 1024 - 15, 1), left=False)
        by_class = [jnp.where(window, (skew if rc == 7 else pltpu.roll(skew, (9 + rc) * 64, 1))[:, 0:512], NEG)
                    for rc in range(8)]
        for cls in range(len(NA_CLASSES)):
            for qr, (rc, off) in enumerate(_tile_rows(cls)):
                w = jnp.concatenate([by_class[rc], fill], axis=1)
                rows = slice(hh * NA_Q + qr * GW, hh * NA_Q + (qr + 1) * GW)
                out_ref[cls, rows, :] = pltpu.roll(w, off * GW, 1) if off else w


def na_fwd(pna, pnac, bias, mixin, hosted):
    nb = pna.shape[0]

    def body(*refs):
        (p_ref, pc_ref, bias_ref, _), h_in, (out_ref,), h_out, _, h_sems = hosted.split(refs, 4, 1)
        grid_step = pl.program_id(0) * nb + pl.program_id(1)

        @pl.when(grid_step == 0)
        def _():
            hosted.start(h_in, h_out, h_sems)

        kc, vc = pc_ref[:, 128:256], pc_ref[:, 256:384]

        def tile(t, carry):
            qsl, bsl, cls = _na_tile(t)
            kb, vb = p_ref[bsl, 128:256], p_ref[bsl, 256:384]
            p_loc, p_ctx = _na_probs(_stack_heads(p_ref[qsl, 0:128] * 0.125), kb, kc, bias_ref[cls])
            out_ref[qsl, :] = _unstack_heads(_nn(p_loc, vb) + _nn(p_ctx, vc)).astype(BF16)
            return carry

        lax.fori_loop(0, NA_TILES, tile, 0, unroll=4)

        @pl.when(grid_step == NPAIR * nb - 1)
        def _():
            hosted.finish(h_in, h_out, h_sems)

    h_in_specs, h_out_specs = hosted.specs()
    return pl.pallas_call(
        body, name="na_fwd", grid=(NPAIR, nb),
        in_specs=[
            pl.BlockSpec((None, SEQ, 384), lambda p, b: (b, 0, p)),
            pl.BlockSpec((None, LC, 384), lambda p, b: (b, 0, p)),
            pl.BlockSpec((None, len(NA_CLASSES), 2 * NA_Q, NA_K), lambda p, b: (p, 0, 0, 0)),
            pl.BlockSpec(memory_space=pl.ANY),
        ] + h_in_specs,
        out_specs=[pl.BlockSpec((None, SEQ, 128), lambda p, b: (b, 0, 4 + p))] + h_out_specs,
        out_shape=[jax.ShapeDtypeStruct((nb, SEQ, D), BF16)] + hosted.out_shape,
        input_output_aliases={3: 0},
        scratch_shapes=hosted.scratch,
        compiler_params=_cp(("arbitrary", "arbitrary")),
    )(pna, pnac, bias, mixin, *hosted.args)


def na_bwd(pna, pnac, bias, dmixin, dproj, dprojc, hosted):
    nb = pna.shape[0]

    def body(*refs):
        own_in, h_in, own_out, h_out, own_scr, h_sems = hosted.split(refs, 6, 3)
        p_ref, pc_ref, bias_ref, dmix_ref = own_in[:4]
        dp_ref, dpc_ref, dpat_ref = own_out
        dbias_s, dk_s, dv_s, dkc_s, dvc_s, res_s, resc_s = own_scr
        b, part = pl.program_id(1), pl.program_id(2)
        grid_step = (pl.program_id(0) * nb + b) * 3 + part

        @pl.when(grid_step == 0)
        def _():
            hosted.start(h_in, h_out, h_sems)

        @pl.when(grid_step == NPAIR * nb * 3 - 1)
        def _():
            hosted.finish(h_in, h_out, h_sems)

        @pl.when(part == 0)
        def _():
            @pl.when(b == 0)
            def _():
                dbias_s[...] = jnp.zeros_like(dbias_s)

            dk_s[...] = jnp.zeros_like(dk_s)
            dv_s[...] = jnp.zeros_like(dv_s)
            dkc_s[...] = jnp.zeros_like(dkc_s)
            dvc_s[...] = jnp.zeros_like(dvc_s)
            kc, vc = pc_ref[:, 128:256], pc_ref[:, 256:384]

            def tile(t, carry):
                qsl, bsl, cls = _na_tile(t)
                kb, vb = p_ref[bsl, 128:256], p_ref[bsl, 256:384]
                qst, dost = _stack_heads(p_ref[qsl, 0:128] * 0.125), _stack_heads(dmix_ref[qsl, :])
                p_loc, p_ctx = _na_probs(qst, kb, kc, bias_ref[cls])
                dp_loc, dp_ctx = _nt(dost, vb), _nt(dost, vc)
                delta = (jnp.sum(p_loc * dp_loc, axis=1, keepdims=True)
                         + jnp.sum(p_ctx * dp_ctx, axis=1, keepdims=True))
                ds_loc, ds_ctx = p_loc * (dp_loc - delta), p_ctx * (dp_ctx - delta)
                dbias_s[cls] += ds_loc
                res_s[0, qsl, :] = _unstack_heads((_nn(ds_loc, kb) + _nn(ds_ctx, kc)) * 0.125).astype(BF16)
                dk_s[bsl, :] += _tn(ds_loc, qst)
                dv_s[bsl, :] += _tn(p_loc, dost)
                dkc_s[...] += _tn(ds_ctx, qst)
                dvc_s[...] += _tn(p_ctx, dost)
                return carry

            lax.fori_loop(0, NA_TILES, tile, 0, unroll=2)
            res_s[1] = dk_s[...].astype(BF16)
            res_s[2] = dv_s[...].astype(BF16)
            resc_s[0] = jnp.zeros((LC, 128), BF16)
            resc_s[1] = dkc_s[...].astype(BF16)
            resc_s[2] = dvc_s[...].astype(BF16)

            @pl.when(b == nb - 1)
            def _():
                for hh in (0, 1):
                    by_class = [None] * 8
                    for cls in range(len(NA_CLASSES)):
                        for qr, (rc, off) in enumerate(_tile_rows(cls)):
                            w = dbias_s[cls, hh * NA_Q + qr * GW:hh * NA_Q + (qr + 1) * GW, :]
                            w = (pltpu.roll(w, NA_K - off * GW, 1) if off else w)[:, 0:512]
                            by_class[rc] = w if by_class[rc] is None else by_class[rc] + w
                    skew = jnp.zeros((GW, 1024), F32)
                    for rc in range(8):
                        w = jnp.concatenate([by_class[rc], jnp.zeros((GW, 512), F32)], axis=1)
                        skew = skew + (w if rc == 7 else pltpu.roll(w, (7 - rc) * 64, 1))
                    dpat_ref[hh] = jnp.sum(pltpu.roll(_barrel(skew, left=True), 15, 1), axis=0, keepdims=True)

        dp_ref[...] = res_s[part]
        dpc_ref[...] = resc_s[part]

    h_in_specs, h_out_specs = hosted.specs()
    return pl.pallas_call(
        body, name="na_bwd", grid=(NPAIR, nb, 3),
        in_specs=[
            pl.BlockSpec((None, SEQ, 384), lambda p, b, s: (b, 0, p)),
            pl.BlockSpec((None, LC, 384), lambda p, b, s: (b, 0, p)),
            pl.BlockSpec((None, len(NA_CLASSES), 2 * NA_Q, NA_K), lambda p, b, s: (p, 0, 0, 0)),
            pl.BlockSpec((None, SEQ, 128), lambda p, b, s: (b, 0, 4 + p)),
            pl.BlockSpec(memory_space=pl.ANY),
            pl.BlockSpec(memory_space=pl.ANY),
        ] + h_in_specs,
        out_specs=[
            pl.BlockSpec((None, SEQ, 128), lambda p, b, s: (b, 0, 16 + 3 * p + s)),
            pl.BlockSpec((None, LC, 128), lambda p, b, s: (b, 0, 16 + 3 * p + s)),
            pl.BlockSpec((None, 2, 1, 1024), lambda p, b, s: (p, 0, 0, 0)),
        ] + h_out_specs,
        out_shape=[
            jax.ShapeDtypeStruct((nb, SEQ, IN_W), BF16),
            jax.ShapeDtypeStruct((nb, LC, IN_W), BF16),
            jax.ShapeDtypeStruct((NPAIR, 2, 1, 1024), F32),
        ] + hosted.out_shape,
        input_output_aliases={4: 0, 5: 1},
        scratch_shapes=[
            pltpu.VMEM((len(NA_CLASSES), 2 * NA_Q, NA_K), F32),
            pltpu.VMEM((SEQ, 128), F32), pltpu.VMEM((SEQ, 128), F32),
            pltpu.VMEM((LC, 128), F32), pltpu.VMEM((LC, 128), F32),
            pltpu.VMEM((3, SEQ, 128), BF16), pltpu.VMEM((3, LC, 128), BF16),
        ] + hosted.scratch,
        compiler_params=_cp(("arbitrary", "arbitrary", "arbitrary")),
    )(pna, pnac, bias, dmixin, dproj, dprojc, *hosted.args)


def tail_fwd_bwd(x, mixin, tgt, mod3, g_post_mix, g_pre_mlp, g_post_mlp, wout, w1, w2):
    nb = x.shape[0]

    def body(x_ref, mi_ref, tgt_ref, mod_ref, gpm_ref, gpl_ref, gpo_ref, wo_ref, w1_ref, w2_ref,
             dx_ref, dmix_ref, h2_ref, du_ref, a_ref, dm_ref, dmi_ref, dmod_ref, dg_ref, loss_ref):
        b, t = pl.program_id(0), pl.program_id(1)
        gt1, sh2, sc2, gt2 = mod_ref[2:3, :], mod_ref[3:4, :], mod_ref[4:5, :], mod_ref[5:6, :]
        mix = jnp.dot(mi_ref[...], wo_ref[...], preferred_element_type=F32)
        (x1, h2), vjp_a = jax.vjp(_post_mix, x_ref[...], mix, gt1, sc2, sh2, gpm_ref[...], gpl_ref[...])
        h2b = h2.astype(BF16)
        h2_ref[...] = h2b
        m = jnp.zeros((TN, D), F32)
        relus = []
        for j in range(4):
            cols = slice(j * D, (j + 1) * D)
            r = jnp.maximum(jnp.dot(h2b, w1_ref[j], preferred_element_type=F32), 0.0)
            ab = (r * r).astype(BF16)
            a_ref[:, cols] = ab
            m = m + jnp.dot(ab, w2_ref[cols, :], preferred_element_type=F32)
            relus.append(r)
        loss, vjp_b = jax.vjp(_head_loss, x1, m, gt2, gpo_ref[...], tgt_ref[...])
        dx1, dm, dgt2, dgpo, _ = vjp_b(jnp.ones((1, 1), F32))
        dmb = dm.astype(BF16)
        dm_ref[...] = dmb
        dh2 = jnp.zeros((TN, D), F32)
        for j in range(4):
            cols = slice(j * D, (j + 1) * D)
            da = lax.dot_general(dmb, w2_ref[cols, :], (((1,), (1,)), ((), ())), preferred_element_type=F32)
            dub = (da * (2.0 * relus[j])).astype(BF16)
            du_ref[:, cols] = dub
            dh2 = dh2 + lax.dot_general(dub, w1_ref[j], (((1,), (1,)), ((), ())), preferred_element_type=F32)
        dx, dmix, dgt1, dsc2, dsh2, dgpm, dgpl = vjp_a((dx1, dh2))
        dx_ref[...] = dx
        dmixb = dmix.astype(BF16)
        dmix_ref[...] = dmixb
        dmi_ref[...] = lax.dot_general(dmixb, wo_ref[...], (((1,), (1,)), ((), ())),
                                       preferred_element_type=F32).astype(BF16)

        @pl.when(t == 0)
        def _():
            dmod_ref[...] = jnp.zeros_like(dmod_ref)

        @pl.when((t == 0) & (b == 0))
        def _():
            dg_ref[...] = jnp.zeros_like(dg_ref)
            loss_ref[...] = jnp.zeros_like(loss_ref)

        dmod_ref[2:3, :] += dgt1
        dmod_ref[3:4, :] += dsh2
        dmod_ref[4:5, :] += dsc2
        dmod_ref[5:6, :] += dgt2
        dg_ref[0:1, :] += dgpm
        dg_ref[1:2, :] += dgpl
        dg_ref[2:3, :] += dgpo
        loss_ref[...] += jnp.broadcast_to(loss, loss_ref.shape)

    tok = lambda b, t: (b, t, 0)
    const = lambda b, t: (0, 0)
    vec = pl.BlockSpec((1, D), const)
    return pl.pallas_call(
        body, name="tail_fwd_bwd", grid=(nb, SEQ // TN),
        in_specs=[
            pl.BlockSpec((None, TN, D), tok), pl.BlockSpec((None, TN, D), tok), pl.BlockSpec((None, TN, D), tok),
            pl.BlockSpec((None, 6, D), lambda b, t: (b, 0, 0)), vec, vec, vec,
            pl.BlockSpec((D, D), const, pipeline_mode=pl.Buffered(1)),
            pl.BlockSpec((4, D, D), lambda b, t: (0, 0, 0), pipeline_mode=pl.Buffered(1)),
            pl.BlockSpec((DFF, D), const, pipeline_mode=pl.Buffered(1)),
        ],
        out_specs=[
            pl.BlockSpec((None, TN, D), tok), pl.BlockSpec((None, TN, D), tok), pl.BlockSpec((None, TN, D), tok),
            pl.BlockSpec((None, TN, DFF), tok), pl.BlockSpec((None, TN, DFF), tok), pl.BlockSpec((None, TN, D), tok),
            pl.BlockSpec((None, TN, D), tok),
            pl.BlockSpec((None, 6, D), lambda b, t: (b, 0, 0)),
            pl.BlockSpec((8, D), const), pl.BlockSpec((8, 128), const),
        ],
        out_shape=[
            jax.ShapeDtypeStruct((nb, SEQ, D), F32), jax.ShapeDtypeStruct((nb, SEQ, D), BF16),
            jax.ShapeDtypeStruct((nb, SEQ, D), BF16), jax.ShapeDtypeStruct((nb, SEQ, DFF), BF16),
            jax.ShapeDtypeStruct((nb, SEQ, DFF), BF16), jax.ShapeDtypeStruct((nb, SEQ, D), BF16),
            jax.ShapeDtypeStruct((nb, SEQ, D), BF16),
            jax.ShapeDtypeStruct((nb, 6, D), F32), jax.ShapeDtypeStruct((8, D), F32),
            jax.ShapeDtypeStruct((8, 128), F32),
        ],
        compiler_params=_cp(("arbitrary", "arbitrary")),
    )(x, mixin, tgt, mod3, g_post_mix, g_pre_mlp, g_post_mlp, wout, w1, w2)


def weight_grad(pairs, name, out_dtype=F32, col_blocks=False, tm=1024, tn=1024, tk=2048):
    m, n = pairs[0][0].shape[1], pairs[0][1].shape[1]
    tn = min(tn, n)
    tks = [min(tk, xa.shape[0]) for xa, _ in pairs]
    steps = [xa.shape[0] // t for (xa, _), t in zip(pairs, tks)]
    total = sum(steps)
    offs = [sum(steps[:i]) for i in range(len(pairs))]

    def body(*refs):
        out_ref, acc = refs[2 * len(pairs)], refs[-1]
        k = pl.program_id(2)

        @pl.when(k == 0)
        def _():
            acc[...] = jnp.zeros_like(acc)

        for i in range(len(pairs)):
            @pl.when((k >= offs[i]) & (k < offs[i] + steps[i]))
            def _(i=i):
                acc[...] += lax.dot_general(refs[2 * i][...], refs[2 * i + 1][...], (((0,), (0,)), ((), ())),
                                            preferred_element_type=F32)

        if out_dtype != F32:
            @pl.when(k == total - 1)
            def _():
                out_ref[...] = acc[...].astype(out_dtype)

    in_specs, args = [], []
    for i, (xa, ya) in enumerate(pairs):
        clamp = lambda k, i=i: jnp.clip(k - offs[i], 0, steps[i] - 1)
        in_specs.append(pl.BlockSpec((tks[i], tm), lambda a, c, k, clamp=clamp: (clamp(k), a)))
        in_specs.append(pl.BlockSpec((tks[i], tn), lambda a, c, k, clamp=clamp: (clamp(k), c)))
        args += [xa, ya]
    if col_blocks:
        out_spec = pl.BlockSpec((None, tm, tn), lambda a, c, k: (c, a, 0))
        out_shape = jax.ShapeDtypeStruct((n // tn, m, tn), out_dtype)
    else:
        out_spec = pl.BlockSpec((tm, tn), lambda a, c, k: (a, c))
        out_shape = jax.ShapeDtypeStruct((m, n), out_dtype)
    return pl.pallas_call(
        body, name=name, grid=(m // tm, n // tn, total), in_specs=in_specs, out_specs=out_spec, out_shape=out_shape,
        scratch_shapes=[] if out_dtype == F32 else [pltpu.VMEM((tm, tn), F32)],
        compiler_params=_cp(("arbitrary", "arbitrary", "arbitrary")),
    )(*args)


def _perm_block(t):
    return 4 * (t % 4) + t // 4 if t < 16 else 16 + 3 * ((t - 16) % 4) + (t - 16) // 4


def unpack_w_in(blocks):
    def body(i_ref, o_ref):
        for t in range(28):
            p = _perm_block(t)
            o_ref[:, p * 128:(p + 1) * 128] = i_ref[t // 7, :, (t % 7) * 128:(t % 7 + 1) * 128]

    return pl.pallas_call(
        body, name="unpack_w_in", grid=(2,),
        in_specs=[pl.BlockSpec((4, D // 2, 896), lambda i: (0, i, 0))],
        out_specs=pl.BlockSpec((D // 2, IN_W), lambda i: (i, 0)),
        out_shape=jax.ShapeDtypeStruct((D, IN_W), BF16),
    )(blocks)


def pack_w_in(dw):
    def body(i_ref, o_ref):
        for t in range(28):
            p = _perm_block(t)
            o_ref[t // 7, :, (t % 7) * 128:(t % 7 + 1) * 128] = i_ref[:, p * 128:(p + 1) * 128].astype(BF16)

    return pl.pallas_call(
        body, name="pack_w_in", grid=(4,),
        in_specs=[pl.BlockSpec((D // 4, IN_W), lambda i: (i, 0))],
        out_specs=pl.BlockSpec((4, D // 4, 896), lambda i: (0, i, 0)),
        out_shape=jax.ShapeDtypeStruct((4, D, 896), BF16),
    )(dw)


def _place():
    return lax.axis_index("x"), lax.axis_index("y"), lax.axis_index("c")


class Hosted:
    def __init__(self, args, out_shape, scratch, start, finish):
        self.args, self.out_shape, self.scratch, self.start, self.finish = args, out_shape, scratch, start, finish

    def specs(self):
        hbm = pl.BlockSpec(memory_space=pl.ANY)
        return [hbm] * len(self.args), [hbm] * len(self.out_shape)

    def split(self, refs, n_in, n_out):
        a, b = len(self.args), len(self.out_shape)
        cuts = [n_in, n_in + a, n_in + a + n_out, n_in + a + n_out + b, len(refs) - len(self.scratch)]
        parts = [refs[i:j] for i, j in zip([0] + cuts, cuts + [len(refs)])]
        return parts[0], parts[1], parts[2], parts[3], parts[4], parts[5]


def no_exchange():
    return Hosted([], [], [], lambda *a: None, lambda *a: None)


def run_hosted(hosted, name):
    def body(*refs):
        _, ins, _, outs, _, sems = hosted.split(refs, 0, 0)
        hosted.start(ins, outs, sems)
        hosted.finish(ins, outs, sems)

    in_specs, out_specs = hosted.specs()
    return pl.pallas_call(body, name=name, in_specs=in_specs, out_specs=out_specs, out_shape=hosted.out_shape,
                          scratch_shapes=hosted.scratch)(*hosted.args)


def gather8(blocks):
    na = len(blocks)

    def copies(ins, outs, sems):
        send_sems, recv_sems, local_sem = sems
        x, y, c = _place()
        me, sibling = (x, y, c), (x, y, 1 - c)
        chips = [(1 - x, y), (x, 1 - y), (1 - x, 1 - y)]

        def slot(o_ref, px, py, pc):
            return o_ref.at[4 * px + 2 * py + pc]

        def copy(a, k, block, to, src=None):
            return pltpu.make_async_remote_copy(
                src_ref=slot(outs[a], *block) if src is None else src, dst_ref=slot(outs[a], *block),
                send_sem=send_sems.at[a, k], recv_sem=recv_sems.at[a, k], device_id=to, device_id_type=MESH)

        mine = [pltpu.make_async_copy(ins[a], slot(outs[a], *me), local_sem.at[a]) for a in range(na)]
        first = []
        for a in range(na):
            first.append(copy(a, 0, me, sibling, src=ins[a]))
            first += [copy(a, 1 + j, me, (*chip, c), src=ins[a]) for j, chip in enumerate(chips)]
        return copy, mine, first, me, sibling, chips, c

    def start(ins, outs, sems):
        _, mine, first, *_ = copies(ins, outs, sems)
        for cp in mine + first:
            cp.start()

    def finish(ins, outs, sems):
        copy, mine, first, me, sibling, chips, c = copies(ins, outs, sems)
        passed = []
        for j, chip in enumerate(chips):
            for a in range(na):
                copy(a, 1 + j, (*chip, c), me).wait_recv()
                cp = copy(a, 4 + j, (*chip, c), sibling)
                cp.start()
                passed.append(cp)
        for a in range(na):
            copy(a, 0, sibling, me).wait_recv()
            for j, chip in enumerate(chips):
                copy(a, 4 + j, (*chip, 1 - c), me).wait_recv()
        for cp in first + passed:
            cp.wait_send()
        for cp in mine:
            cp.wait()

    return Hosted(list(blocks), [jax.ShapeDtypeStruct((8,) + b.shape, b.dtype) for b in blocks],
                  [pltpu.SemaphoreType.DMA((na, 7)), pltpu.SemaphoreType.DMA((na, 7)), pltpu.SemaphoreType.DMA((na,))],
                  start, finish)


def chips3(arrays):
    na = len(arrays)

    def copies(ins, outs, sems):
        send_sems, recv_sems = sems
        x, y, c = _place()
        return [pltpu.make_async_remote_copy(
            src_ref=ins[a].at[2 * px + py], dst_ref=outs[a].at[k], send_sem=send_sems.at[a, k],
            recv_sem=recv_sems.at[a, k], device_id=(px, py, c), device_id_type=MESH)
            for a in range(na) for k, (px, py) in enumerate([(1 - x, y), (x, 1 - y), (1 - x, 1 - y)])]

    def start(ins, outs, sems):
        for cp in copies(ins, outs, sems):
            cp.start()

    def finish(ins, outs, sems):
        for cp in copies(ins, outs, sems):
            cp.wait()

    return Hosted(list(arrays), [jax.ShapeDtypeStruct((3,) + a.shape[1:], a.dtype) for a in arrays],
                  [pltpu.SemaphoreType.DMA((na, 3)), pltpu.SemaphoreType.DMA((na, 3))], start, finish)


def siblings(arrays):
    na = len(arrays)

    def copies(ins, outs, sems):
        send_sems, recv_sems = sems
        x, y, c = _place()
        return [pltpu.make_async_remote_copy(
            src_ref=ins[a], dst_ref=outs[a], send_sem=send_sems.at[a], recv_sem=recv_sems.at[a],
            device_id=(x, y, 1 - c), device_id_type=MESH) for a in range(na)]

    def start(ins, outs, sems):
        for cp in copies(ins, outs, sems):
            cp.start()

    def finish(ins, outs, sems):
        for cp in copies(ins, outs, sems):
            cp.wait()

    return Hosted(list(arrays), [jax.ShapeDtypeStruct(a.shape, a.dtype) for a in arrays],
                  [pltpu.SemaphoreType.DMA((na,)), pltpu.SemaphoreType.DMA((na,))], start, finish)


def both(first, second):
    na, no, ns = len(first.args), len(first.out_shape), len(first.scratch)

    def start(ins, outs, sems):
        first.start(ins[:na], outs[:no], sems[:ns])
        second.start(ins[na:], outs[no:], sems[ns:])

    def finish(ins, outs, sems):
        first.finish(ins[:na], outs[:no], sems[:ns])
        second.finish(ins[na:], outs[no:], sems[ns:])

    return Hosted(first.args + second.args, first.out_shape + second.out_shape, first.scratch + second.scratch,
                  start, finish)


def siblings4(arrays):
    na = len(arrays)

    def copies(ins, outs, sems):
        send_sems, recv_sems = sems
        x, y, c = _place()
        return [pltpu.make_async_remote_copy(
            src_ref=ins[a].at[2 * j + 1 - c], dst_ref=outs[a].at[j],
            send_sem=send_sems.at[a, j], recv_sem=recv_sems.at[a, j],
            device_id=(x, y, 1 - c), device_id_type=MESH) for a in range(na) for j in range(4)]

    def start(ins, outs, sems):
        for cp in copies(ins, outs, sems):
            cp.start()

    def finish(ins, outs, sems):
        for cp in copies(ins, outs, sems):
            cp.wait()

    return Hosted(list(arrays), [jax.ShapeDtypeStruct((4,) + a.shape[1:], a.dtype) for a in arrays],
                  [pltpu.SemaphoreType.DMA((na, 4)), pltpu.SemaphoreType.DMA((na, 4))], start, finish)


def sibling_blocks(arrays, name):
    return run_hosted(siblings4(arrays), name)


def _row_tile(r):
    for cand in (512, 256, 128, 64, 32, 16, 8):
        if r % cand == 0:
            return cand
    return r


def chip_partial(place, g8s, landed4s, name):
    n = len(g8s)

    def body(place_ref, *refs):
        del place_ref
        for g_ref, l_ref, o_ref in zip(refs[:n], refs[n:2 * n], refs[2 * n:]):
            o_ref[...] = (g_ref[...].astype(F32) + l_ref[...].astype(F32)).astype(BF16)

    own = [pl.BlockSpec((None,) + g.shape[1:], lambda j, s: (2 * j + s[0], 0, 0)) for g in g8s]
    plain = [pl.BlockSpec((None,) + g.shape[1:], lambda j, s: (j, 0, 0)) for g in g8s]
    return pl.pallas_call(
        body, name=name,
        grid_spec=pltpu.PrefetchScalarGridSpec(num_scalar_prefetch=1, grid=(4,), in_specs=own + plain, out_specs=plain),
        out_shape=[jax.ShapeDtypeStruct((4,) + g.shape[1:], BF16) for g in g8s],
    )(place, *g8s, *landed4s)


def shard_sum(place, partial4s, landed3s, name):
    n = len(partial4s)

    def body(place_ref, *refs):
        del place_ref
        for p_ref, l_ref, o_ref in zip(refs[:n], refs[n:2 * n], refs[2 * n:]):
            acc = p_ref[...].astype(F32)
            for k in range(3):
                acc = acc + l_ref[k].astype(F32)
            o_ref[...] = acc

    def halves(p, lead):
        r, ccols = p.shape[1:]
        return (lead, r // 2, ccols)

    return pl.pallas_call(
        body, name=name,
        grid_spec=pltpu.PrefetchScalarGridSpec(
            num_scalar_prefetch=1, grid=(2,),
            in_specs=[pl.BlockSpec(halves(p, None), lambda i, s: (s[1], i, 0)) for p in partial4s]
            + [pl.BlockSpec(halves(p, 3), lambda i, s: (0, i, 0)) for p in partial4s],
            out_specs=[pl.BlockSpec(halves(p, None)[1:], lambda i, s: (i, 0)) for p in partial4s]),
        out_shape=[jax.ShapeDtypeStruct(p.shape[1:], F32) for p in partial4s],
    )(place, *partial4s, *landed3s)


def _adamw_math(w, g, m, v):
    m2 = B1 * m + (1.0 - B1) * g
    v2 = B2 * v + (1.0 - B2) * (g * g)
    m_hat = m2 / (1.0 - B1 ** STEP)
    v_hat = v2 / (1.0 - B2 ** STEP)
    return -LR * (m_hat / (jnp.sqrt(v_hat) + AEPS) + WD * w), m2, v2


def adamw_halves(place, w, mine, theirs, m, v, name):
    r, ccols = w.shape
    hr = r // 2
    tr = _row_tile(hr)
    nt = hr // tr

    def body(place_ref, w_ref, a_ref, b_ref, m_ref, v_ref, g_out, d_out, m_out, v_out):
        g = jnp.where(pl.program_id(0) == place_ref[0], a_ref[...], b_ref[...])
        d, m2, v2 = _adamw_math(w_ref[...], g, m_ref[...], v_ref[...])
        g_out[...] = g
        d_out[...] = d
        m_out[...] = m2
        v_out[...] = v2

    full = pl.BlockSpec((tr, ccols), lambda h, i, s: (h * nt + i, 0))
    part = pl.BlockSpec((tr, ccols), lambda h, i, s: (i, 0))
    return pl.pallas_call(
        body, name=name,
        grid_spec=pltpu.PrefetchScalarGridSpec(
            num_scalar_prefetch=1, grid=(2, nt), in_specs=[full, part, part, full, full], out_specs=[full] * 4),
        out_shape=[jax.ShapeDtypeStruct((r, ccols), F32)] * 4,
    )(place, w, mine, theirs, m, v)


def adamw_group(place, halved, plain, hosted, name):
    rows = halved[0][0].shape[0]
    tr = 64
    nt = rows // 2 // tr
    nh, npl = len(halved), len(plain)

    def body(place_ref, *refs):
        own_in, h_in, own_out, h_out, _, h_sems = hosted.split(refs, 5 * nh + 4 * npl, 4 * nh + 3 * npl)
        half = pl.program_id(0)
        grid_step = half * nt + pl.program_id(1)

        @pl.when(grid_step == 0)
        def _():
            hosted.start(h_in, h_out, h_sems)

        for i in range(nh):
            w_ref, a_ref, b_ref, m_ref, v_ref = own_in[5 * i:5 * i + 5]
            g = jnp.where(half == place_ref[0], a_ref[...], b_ref[...])
            res = (g,) + _adamw_math(w_ref[...], g, m_ref[...], v_ref[...])
            for o_ref, r in zip(own_out[4 * i:4 * i + 4], res):
                o_ref[...] = r
        for i in range(npl):
            w_ref, g_ref, m_ref, v_ref = own_in[5 * nh + 4 * i:5 * nh + 4 * i + 4]
            res = _adamw_math(w_ref[...], g_ref[...], m_ref[...], v_ref[...])
            for o_ref, r in zip(own_out[4 * nh + 3 * i:4 * nh + 3 * i + 3], res):
                o_ref[...] = r

        @pl.when(grid_step == 2 * nt - 1)
        def _():
            hosted.finish(h_in, h_out, h_sems)

    def full(cols):
        return pl.BlockSpec((tr, cols), lambda h, i, s: (h * nt + i, 0))

    def part(cols):
        return pl.BlockSpec((tr, cols), lambda h, i, s: (i, 0))

    in_specs, out_specs, out_shape, args = [], [], [], []
    for w, a, b, m, v in halved:
        cols = w.shape[1]
        in_specs += [full(cols), part(cols), part(cols), full(cols), full(cols)]
        out_specs += [full(cols)] * 4
        out_shape += [jax.ShapeDtypeStruct(w.shape, F32)] * 4
        args += [w, a, b, m, v]
    for w, g, m, v in plain:
        cols = w.shape[1]
        in_specs += [full(cols)] * 4
        out_specs += [full(cols)] * 3
        out_shape += [jax.ShapeDtypeStruct(w.shape, F32)] * 3
        args += [w, g, m, v]
    h_in_specs, h_out_specs = hosted.specs()
    return pl.pallas_call(
        body, name=name,
        grid_spec=pltpu.PrefetchScalarGridSpec(
            num_scalar_prefetch=1, grid=(2, nt), in_specs=in_specs + h_in_specs, out_specs=out_specs + h_out_specs,
            scratch_shapes=hosted.scratch),
        out_shape=out_shape + hosted.out_shape,
        compiler_params=_cp(("arbitrary", "arbitrary")),
    )(place, *args, *hosted.args)


def _silu(x):
    return x * jax.nn.sigmoid(x)


def prologue(c_rows, c_ctx_row, w_ada, b_shard, rpb_flat, half_w_in, late_shards):
    shape = jax.ShapeDtypeStruct
    n_late = len(late_shards)
    half_shapes = [(w.shape[0] // 2, w.shape[1]) for w in late_shards]
    g_w = gather8([half_w_in])
    g_c = gather8([shape((8, D), F32)])
    g_m = gather8([shape((32, 1536), F32)])

    def body(*refs):
        c_ref, cc_ref, w_ref, b_ref, flat_ref, hw_ref = refs[:6]
        late_refs = refs[6:6 + n_late]
        cin_ref, mg_ref, gw_ref, bias_ref, cos_ref, sin_ref = refs[6 + n_late:12 + n_late]
        rest = refs[12 + n_late:]
        half_refs, (cg_s, ms_s, bias_s) = rest[:n_late], rest[n_late:n_late + 3]
        stage, (load_sem, bias_sem), sems = rest[n_late + 3:2 * n_late + 3], rest[2 * n_late + 3:2 * n_late + 5], \
            rest[2 * n_late + 5:]
        sw, sc, sm = sems[0:3], sems[3:6], sems[6:9]
        core = lax.axis_index("c")
        g_c.start([c_ref], [cg_s], sc)
        loads = [pltpu.make_async_copy(late_refs[a].at[pl.ds(core * half_shapes[a][0], half_shapes[a][0]), :],
                                       stage[a], load_sem.at[a]) for a in range(n_late)]
        for cp in loads:
            cp.start()
        g_c.finish([c_ref], [cg_s], sc)
        cin_ref[...] = jnp.zeros_like(cin_ref)
        for dev in range(8):
            cin_ref[2 * dev:2 * dev + 2, :] = cg_s[dev, 0:2, :]
        cin_ref[16:17, :] = cc_ref[...]
        ms_s[...] = _nn(_silu(cin_ref[...]), w_ref[...]) + b_ref[...]
        g_m.start([ms_s], [mg_ref], sm)
        g_w.start([hw_ref], [gw_ref], sw)
        for a, cp in enumerate(loads):
            cp.wait()
            half_refs[a][...] = stage[a][...].astype(BF16)
        cos_ref[...], sin_ref[...] = _rope_tables()
        stores = []
        for pair in range(NPAIR):
            if pair >= 2:
                stores[pair - 2].wait()
            _na_bias_pair(flat_ref.at[pair], bias_s.at[pair % 2])
            stores.append(pltpu.make_async_copy(bias_s.at[pair % 2], bias_ref.at[pair], bias_sem.at[pair % 2]))
            stores[pair].start()
        for cp in stores[-2:]:
            cp.wait()
        g_m.finish([ms_s], [mg_ref], sm)
        g_w.finish([hw_ref], [gw_ref], sw)

    vmem = pl.BlockSpec(memory_space=pltpu.VMEM)
    hbm = pl.BlockSpec(memory_space=pl.ANY)
    return pl.pallas_call(
        body, name="prologue", in_specs=[vmem, vmem, vmem, vmem, vmem, hbm] + [hbm] * n_late,
        out_specs=[vmem, vmem, hbm, hbm, vmem, vmem] + [vmem] * n_late,
        out_shape=[shape((32, D), F32), shape((8, 32, 1536), F32)] + g_w.out_shape
        + [shape((NPAIR,) + NA_BIAS_SHAPE, F32)] + [shape((SEQ, RD), F32)] * 2 + [shape(s, BF16) for s in half_shapes],
        scratch_shapes=[pltpu.VMEM((8, 8, D), F32), pltpu.VMEM((32, 1536), F32), pltpu.VMEM((2,) + NA_BIAS_SHAPE, F32)]
        + [pltpu.VMEM(s, F32) for s in half_shapes]
        + [pltpu.SemaphoreType.DMA((n_late,)), pltpu.SemaphoreType.DMA((2,))]
        + g_w.scratch + g_c.scratch + g_m.scratch,
        compiler_params=_cp(),
    )(c_rows, c_ctx_row, w_ada, b_shard, rpb_flat, half_w_in, *late_shards)


def ada_grads(cin, gb, gc, w_ada):
    def body(c_ref, gb_ref, gc_ref, w_ref, gw_ref, pc_ref):
        ctx_tot = jnp.sum(gc_ref[...], axis=0, keepdims=True)
        rows = lax.broadcasted_iota(jnp.int32, (16, 512), 0)
        dm = jnp.concatenate([gb_ref[...], jnp.where(rows == 0, ctx_tot, 0.0)], axis=0)
        gw_ref[...] = _tn(_silu(c_ref[...]), dm)
        rows8 = lax.broadcasted_iota(jnp.int32, (8, 512), 0)
        part = _nt(jnp.where(rows8 == 0, ctx_tot, 0.0), w_ref[...])

        @pl.when(pl.program_id(0) == 0)
        def _():
            pc_ref[...] = jnp.zeros_like(pc_ref)

        pc_ref[...] += part

    return pl.pallas_call(
        body, name="ada_grads", grid=(3,),
        in_specs=[pl.BlockSpec((32, D), lambda j: (0, 0)), pl.BlockSpec((16, 512), lambda j: (0, j)),
                  pl.BlockSpec((8, 512), lambda j: (0, j)), pl.BlockSpec((D, 512), lambda j: (0, j))],
        out_specs=[pl.BlockSpec((D, 512), lambda j: (0, j)), pl.BlockSpec((8, D), lambda j: (0, 0))],
        out_shape=[jax.ShapeDtypeStruct((D, 1536), F32), jax.ShapeDtypeStruct((8, D), F32)],
    )(cin, gb, gc, w_ada)


SMALL_SUM_ROWS = 15


def small_update(gsm, gbf, gcf, pcg, params):
    n = len(params)

    def body(*refs):
        gsm_ref, gbf_ref, gcf_ref, pcg_ref = refs[:4]
        wmv, outs, loss_out = refs[4:4 + 3 * n], refs[4 + 3 * n:4 + 7 * n], refs[-1]
        acc = gsm_ref[0]
        for dev in range(1, 8):
            acc = acc + gsm_ref[dev]
        c_ctx = wmv[0][...]
        sg = jax.nn.sigmoid(c_ctx)
        dsilu = pcg_ref[0:1, :] + pcg_ref[2:3, :] + pcg_ref[4:5, :] + pcg_ref[6:7, :]
        lane = lax.broadcasted_iota(jnp.int32, (1, D), 1)
        last = acc[14:15, :]
        grads = [
            dsilu * (sg * (1.0 + c_ctx * (1.0 - sg))),
            jnp.sum(gbf_ref[...], axis=0, keepdims=True) + jnp.sum(gcf_ref[...], axis=0, keepdims=True),
            acc[0:1, :] + acc[1:2, :], acc[2:3, :], acc[3:4, :], acc[4:5, :],
            acc[5:6, 0:512], acc[6:14, :], jnp.where(lane < 8, last, 0.0),
        ]
        loss_out[...] = jnp.broadcast_to(jnp.sum(jnp.where(lane == 8, last, 0.0), axis=1, keepdims=True), (8, 128))
        for i, g in enumerate(grads):
            d, m2, v2 = _adamw_math(wmv[3 * i][...], g, wmv[3 * i + 1][...], wmv[3 * i + 2][...])
            outs[4 * i][...] = g
            outs[4 * i + 1][...] = d
            outs[4 * i + 2][...] = m2
            outs[4 * i + 3][...] = v2

    flat = [a for wmv in params for a in wmv]
    out_shape = [jax.ShapeDtypeStruct(w.shape, F32) for w, _, _ in params for _ in range(4)]
    return pl.pallas_call(
        body, name="small_update", out_shape=out_shape + [jax.ShapeDtypeStruct((8, 128), F32)],
    )(gsm, gbf, gcf, pcg, *flat)


def _pad_row(v, rows):
    flat = v.reshape(-1)
    return jnp.pad(flat, (0, rows * D - flat.shape[0])).reshape(rows, D)


def local_step(x, ctx, tgt, mod3, rope, bias, g_pre_mix, g_post_mix, g_pre_mlp, g_post_mlp, ret_decay, ret_gn,
               wperm, late_weights, early_grads):
    nb = x.shape[0]
    tokens = nb * SEQ
    cos, sin = rope
    rd = ret_decay.T.reshape(RH, 2, 1)
    gn = ret_gn.reshape(RH, 1, RD)
    h, pret, pna = premix_proj(x, mod3, g_pre_mix, wperm, False, "premix_proj")
    hc, pretc, pnac = premix_proj(ctx, mod3, g_pre_mix, wperm, True, "premix_proj_ctx")
    o_all, mixin, gw_out = retention_fwd(pret, pretc, rd, gn, cos, sin, late_weights(0))
    mixin, gw1, gw2 = na_fwd(pna, pnac, bias, mixin, late_weights(1))
    dx_tail, dmix, h2, du, act, dm, dmixin, dmod_t, dg_t, loss_t = tail_fwd_bwd(
        x, mixin, tgt, mod3, g_post_mix, g_pre_mlp, g_post_mlp, gw_out.reshape(D, D), gw1.reshape(4, D, D),
        gw2.reshape(DFF, D))
    dw_out = weight_grad([(mixin.reshape(tokens, D), dmix.reshape(tokens, D))], "grad_w_out", BF16)
    dw1 = weight_grad([(h2.reshape(tokens, D), du.reshape(tokens, DFF))], "grad_w_mlp1", BF16, col_blocks=True)
    dw2 = weight_grad([(act.reshape(tokens, DFF), dm.reshape(tokens, D))], "grad_w_mlp2", BF16)
    dproj, dprojc, drd, dgn, *landed = retention_bwd(pret, pretc, o_all, dmixin, rd, gn, cos, sin,
                                                     early_grads[0](dw_out, dw1, dw2))
    dproj, dprojc, dpat, *early = na_bwd(pna, pnac, bias, dmixin, dproj, dprojc, early_grads[1](landed))
    dw_in = weight_grad([(h.reshape(tokens, D), dproj.reshape(tokens, IN_W)),
                         (hc.reshape(nb * LC, D), dprojc.reshape(nb * LC, IN_W))], "grad_w_in", tn=IN_W // 2, tk=1024)
    grad_x, dmod_a, dg_a, *late = premix_bwd(x, mod3, g_pre_mix, wperm, dproj, dx_tail, early_grads[2](dw_in),
                                             "premix_bwd")
    dmod_c, dg_c = premix_bwd(ctx, mod3, g_pre_mix, wperm, dprojc, None, no_exchange(), "premix_bwd_ctx")
    dmod = jnp.concatenate([jnp.concatenate([dmod_a[:, 0:2], dmod_t[:, 2:6]], axis=1), dmod_c], axis=0)
    last = jnp.pad(jnp.concatenate([drd[:, :, 0].T.reshape(8), loss_t[0, 0:1]]), (0, D - 9)).reshape(1, D)
    small = jnp.concatenate([dg_a[0:1], dg_c[0:1], dg_t[0:3], _pad_row(dgn, 1), dpat.reshape(8, D), last], axis=0)
    return grad_x, late, early, dmod, small


def kernel(x, c, ctx, c_ctx, w_ada, b_ada, g_pre_mix, g_post_mix, g_pre_mlp, g_post_mlp, w_in, ret_decay, ret_gn, na_rpb, w_out, w_mlp1, w_mlp2, loss_target, m_c_ctx, m_w_ada, m_b_ada, m_g_pre_mix, m_g_post_mix, m_g_pre_mlp, m_g_post_mlp, m_w_in, m_ret_decay, m_ret_gn, m_na_rpb, m_w_out, m_w_mlp1, m_w_mlp2, v_c_ctx, v_w_ada, v_b_ada, v_g_pre_mix, v_g_post_mix, v_g_pre_mlp, v_g_post_mlp, v_w_in, v_ret_decay, v_ret_gn, v_na_rpb, v_w_out, v_w_mlp1, v_w_mlp2):
    px, py, pc = _place()
    dev = 4 * px + 2 * py + pc
    chip = 2 * px + py

    half_w_in = lax.dynamic_slice_in_dim(w_in[0], pc * (D // 2), D // 2, 0).astype(BF16)
    cin, mg, gw_in, bias, cos, sin, *late_halves = prologue(
        jnp.pad(c, ((0, 6), (0, 0))), c_ctx[None], w_ada[0], lax.dynamic_slice_in_dim(b_ada, chip * 1536, 1536, 1),
        _rpb_flat(na_rpb[0]), half_w_in, [w_out[0], w_mlp1[0], w_mlp2[0]])
    halves = [half_w_in] + late_halves
    wperm = unpack_w_in(gw_in.reshape(4, D, 896))
    mod_all = jnp.concatenate([mg[0], mg[2], mg[4], mg[6]], axis=1)
    mod3 = (jnp.pad(lax.dynamic_slice_in_dim(mod_all, 2 * dev, 2, 0), ((0, 1), (0, 0)))
            + jnp.pad(mod_all[16:17], ((2, 0), (0, 0)))).reshape(3, 6, D)

    place = jnp.stack([pc, chip]).astype(jnp.int32)

    early_names = ["w_out", "w_mlp1", "w_mlp2"]
    early_g8, early_partial = [], []

    def early_a(dw_out, dw1, dw2):
        early_g8[:] = [dw_out.reshape(8, 128, D), dw1.reshape(8, 512, D), dw2.reshape(8, 512, D)]
        return siblings4(early_g8)

    def early_b(landed):
        early_partial[:] = chip_partial(place, early_g8, landed, "rs_chip_sum_early")
        return chips3(early_partial)

    late_partial = []

    def late_c(dw_in):
        g8_in = pack_w_in(dw_in).reshape(8, 512, 896)
        (landed_in,) = sibling_blocks([g8_in], "rs_sibling_w_in")
        late_partial[:] = chip_partial(place, [g8_in], [landed_in], "rs_chip_sum_w_in")
        return chips3(late_partial)

    grad_x, (landed3_in,), early_landed, dmod, small = local_step(
        x, ctx, loss_target, mod3, (cos, sin), bias, g_pre_mix, g_post_mix, g_pre_mlp, g_post_mlp, ret_decay[0], ret_gn,
        wperm, lambda k: gather8(halves[1:2] if k == 0 else halves[2:4]), (early_a, early_b, late_c))
    early_mine = shard_sum(place, early_partial, early_landed, "rs_shard_sum_early")

    pay = jnp.concatenate([dmod.reshape(18, D), small, jnp.zeros((40 - 18 - SMALL_SUM_ROWS, D), F32)], axis=0)
    *early_theirs, gs = run_hosted(both(siblings(early_mine), gather8([pay])), "rs_halves_early_gather_small")
    gbf = gs[:, 0:12].reshape(16, 6 * D)
    gcf = gs[:, 12:18].reshape(8, 6 * D)
    gw_ada, pc_part = ada_grads(cin, lax.dynamic_slice_in_dim(gbf, chip * 1536, 1536, 1),
                                lax.dynamic_slice_in_dim(gcf, chip * 1536, 1536, 1), w_ada[0])
    (mine_in,) = shard_sum(place, late_partial, [landed3_in], "rs_shard_sum_w_in")
    theirs_in, pcg = run_hosted(both(siblings([mine_in]), gather8([pc_part])), "rs_halves_w_in_gather_c_ctx")

    grouped = adamw_group(
        place,
        [(w_mlp1[0], early_mine[1], early_theirs[1], m_w_mlp1[0], v_w_mlp1[0]),
         (w_mlp2[0], early_mine[2], early_theirs[2], m_w_mlp2[0], v_w_mlp2[0])],
        [(w_ada[0], gw_ada, m_w_ada[0], v_w_ada[0])], no_exchange(), "adamw_group")
    d_ada, m_ada, v_ada = grouped[8:11]
    big = [
        [r[None] for r in adamw_halves(place, w_in[0], mine_in, theirs_in, m_w_in[0], v_w_in[0], "adamw_w_in")],
        [r[None] for r in adamw_halves(place, w_out[0], early_mine[0], early_theirs[0], m_w_out[0], v_w_out[0],
                                       "adamw_w_out")],
        [r[None] for r in grouped[0:4]], [r[None] for r in grouped[4:8]],
    ]

    def rpb_rows(t):
        return _rpb_flat(t[0]).reshape(8, D)

    def decay_row(t):
        return jnp.pad(t.reshape(1, 8), ((0, 0), (0, D - 8)))

    views = [lambda t: t.reshape(1, D), lambda t: t, lambda t: t, lambda t: t, lambda t: t, lambda t: t, lambda t: t,
             rpb_rows, decay_row]
    back = [lambda t: t.reshape(D), lambda t: t, lambda t: t, lambda t: t, lambda t: t, lambda t: t, lambda t: t,
            lambda t: _rpb_flat_t(t)[None], lambda t: t[:, 0:8].reshape(1, 2, 4)]
    small_w = (c_ctx, b_ada, g_pre_mix, g_post_mix, g_pre_mlp, g_post_mlp, ret_gn, na_rpb, ret_decay)
    small_m = (m_c_ctx, m_b_ada, m_g_pre_mix, m_g_post_mix, m_g_pre_mlp, m_g_post_mlp, m_ret_gn, m_na_rpb, m_ret_decay)
    small_v = (v_c_ctx, v_b_ada, v_g_pre_mix, v_g_post_mix, v_g_pre_mlp, v_g_post_mlp, v_ret_gn, v_na_rpb, v_ret_decay)
    *res, loss8 = small_update(gs[:, 18:18 + SMALL_SUM_ROWS], gbf, gcf, pcg[:, 0],
                               [(f(w), f(m), f(v)) for f, w, m, v in zip(views, small_w, small_m, small_v)])

    def leaves(ada, idx):
        s_c, s_b, s_g1, s_g2, s_g3, s_g4, s_gn, s_rpb, s_rd = [back[i](res[4 * i + idx]) for i in range(9)]
        return [s_c, ada[None], s_b, s_g1, s_g2, s_g3, s_g4, big[0][idx], s_rd, s_gn, s_rpb,
                big[1][idx], big[2][idx], big[3][idx]]

    return (loss8[0, 0], grad_x, *leaves(gw_ada, 0), *leaves(d_ada, 1), *leaves(m_ada, 2), *leaves(v_ada, 3))
```

```python
import functools
import math

import jax
import jax.numpy as jnp
from jax import lax
from jax.experimental import pallas as pl
from jax.experimental.pallas import tpu as pltpu

F32, BF16 = jnp.float32, jnp.bfloat16
D = 1024
SEQ = 2048
LC = 256
GW = 64
RH, RD, CH = 4, 128, 128
NPAIR = 4
IN_W = 3584
RET_W = 2048
DFF = 4096
EPS = 1e-6
NEG = -1e30
TN = 256
NCH = SEQ // CH
LR, B1, B2, AEPS, WD, STEP = 0.001, 0.9, 0.999, 1e-08, 0.01, 10
MESH = pl.DeviceIdType.MESH
VMEM_LIMIT = 56 * 1024 * 1024


def _cp(sem=None):
    return pltpu.CompilerParams(dimension_semantics=sem, vmem_limit_bytes=VMEM_LIMIT)


def _nn(a, b):
    return jnp.dot(a.astype(BF16), b.astype(BF16), preferred_element_type=F32)


def _nt(a, b):
    return lax.dot_general(a.astype(BF16), b.astype(BF16), (((1,), (1,)), ((), ())), preferred_element_type=F32)


def _tn(a, b):
    return lax.dot_general(a.astype(BF16), b.astype(BF16), (((0,), (0,)), ((), ())), preferred_element_type=F32)


@jax.custom_vjp
def mm_tn(a, b):
    return _tn(a, b)


mm_tn.defvjp(lambda a, b: (_tn(a, b), (a, b)), lambda r, g: (_nt(r[1], g), _nn(r[0], g)))


def _rms(x):
    return x * lax.rsqrt(jnp.mean(x * x, axis=-1, keepdims=True) + EPS)


def _rms_mod(x, g, sc, sh):
    return (_rms(x) * g) * (1.0 + sc) + sh


def _post_mix(x, mix, gt1, sc2, sh2, g_post_mix, g_pre_mlp):
    x1 = x + gt1 * (_rms(mix) * g_post_mix)
    return x1, _rms_mod(x1, g_pre_mlp, sc2, sh2)


def _head_loss(x1, m, gt2, g_post_mlp, tgt):
    err = x1 + gt2 * (_rms(m) * g_post_mlp) - tgt
    return 0.5 * jnp.sum(jnp.mean(err * err, axis=-1, keepdims=True), axis=0, keepdims=True)


def _ln_gate(o, g, w):
    mu = jnp.mean(o, axis=-1, keepdims=True)
    var = jnp.mean(jnp.square(o - mu), axis=-1, keepdims=True)
    y = (o - mu) * lax.rsqrt(var + EPS)
    return (y * w) * (g * jax.nn.sigmoid(g))


def _swap32(x):
    lane = lax.broadcasted_iota(jnp.int32, x.shape, 1)
    return jnp.where((lane & 32) == 0, pltpu.roll(x, 96, 1), pltpu.roll(x, 32, 1))


def _rope(x, cos, sin):
    return x * cos + _swap32(x) * sin


def _rope_t(g, cos, sin):
    return g * cos + _swap32(g * sin)


def _rope_tables():
    tok = lax.broadcasted_iota(jnp.int32, (SEQ, RD), 0)
    lane = lax.broadcasted_iota(jnp.int32, (SEQ, RD), 1)
    pos = jnp.where(lane < 64, tok >> 6, tok & (GW - 1)).astype(F32)
    ang = pos * jnp.exp((lane & 31).astype(F32) * (-math.log(10000.0) / 32))
    return jnp.cos(ang), jnp.where((lane & 32) == 0, -jnp.sin(ang), jnp.sin(ang))


def _chunk_loop(n, body, init, k=4):
    def several(t, carry):
        for i in range(k):
            carry = body(k * t + i, carry)
        return carry

    return lax.fori_loop(0, n // k, several, init)


def _fiota(shape, dim):
    return lax.broadcasted_iota(jnp.int32, shape, dim).astype(F32)


def _ret_state(k, v, s, lg, reverse):
    pos = _fiota((CH, 1), 0)
    b_exp = pos if reverse else (CH - 1.0 - pos)
    return jnp.exp(lg * CH) * s + mm_tn(k * jnp.exp(lg * b_exp), v)


class _Decays:
    def __init__(self, lgs):
        i, j, pos = _fiota((CH, CH), 0), _fiota((CH, CH), 1), _fiota((CH, 1), 0)
        diffs = (i - j, j - i)
        keep = (diffs[0] >= 0, diffs[1] > 0)
        mats = [jnp.where(m, jnp.exp(lg * jnp.where(m, d, 0.0)), 0.0) for lg, d, m in zip(lgs, diffs, keep)]
        self.mask = mats[0] + mats[1]
        self.dmask = [mats[0] * diffs[0], mats[1] * diffs[1]]
        a_exp, b_exp = (pos + 1.0, CH - pos), (CH - 1.0 - pos, pos)
        self.a = [jnp.exp(lg * e) for lg, e in zip(lgs, a_exp)]
        self.b = [jnp.exp(lg * e) for lg, e in zip(lgs, b_exp)]
        self.da = [a * e for a, e in zip(self.a, a_exp)]
        self.db = [b * e for b, e in zip(self.b, b_exp)]
        self.g = [jnp.exp(lg * CH) for lg in lgs]


def _both(x, w):
    return jnp.concatenate([x * w[0], x * w[1]], axis=1)


def _total(x):
    return jnp.sum(jnp.sum(x, axis=1, keepdims=True), axis=0, keepdims=True)


def _state_pass(dec, init, k_s, v_of, st_s):
    def step(t, carry):
        out = []
        for d, s in enumerate(carry):
            n = (NCH - 1 - t) if d else t
            sl = pl.ds(pl.multiple_of(n * CH, CH), CH)
            st_s[n, d * RD:(d + 1) * RD, :] = s
            out.append(dec.g[d] * s + _tn(k_s[sl, :] * dec.b[d], v_of(sl)))
        return tuple(out)

    _chunk_loop(NCH, step, tuple(init))


def premix_proj(xin, mod3, g_pre, wperm, is_ctx, name):
    nb, length, _ = xin.shape
    tn = min(2 * TN, length)

    def body(x_ref, mod_ref, g_ref, w_ref, h_ref, pret_ref, pna_ref):
        h = _rms_mod(x_ref[...], g_ref[...], mod_ref[1:2, :], mod_ref[0:1, :])
        hb = h.astype(BF16)
        h_ref[...] = hb
        pret_ref[...] = jnp.dot(hb, w_ref[:, :RET_W], preferred_element_type=F32)
        pna_ref[...] = jnp.dot(hb, w_ref[:, RET_W:], preferred_element_type=F32).astype(BF16)

    return pl.pallas_call(
        body, name=name, grid=(nb, length // tn),
        in_specs=[
            pl.BlockSpec((None, tn, D), lambda b, t: (b, t, 0)),
            pl.BlockSpec((None, 6, D), (lambda b, t: (2, 0, 0)) if is_ctx else (lambda b, t: (b, 0, 0))),
            pl.BlockSpec((1, D), lambda b, t: (0, 0)),
            pl.BlockSpec((D, IN_W), lambda b, t: (0, 0), pipeline_mode=pl.Buffered(1)),
        ],
        out_specs=[
            pl.BlockSpec((None, tn, D), lambda b, t: (b, t, 0)),
            pl.BlockSpec((None, tn, RET_W), lambda b, t: (b, t, 0)),
            pl.BlockSpec((None, tn, IN_W - RET_W), lambda b, t: (b, t, 0)),
        ],
        out_shape=[
            jax.ShapeDtypeStruct((nb, length, D), BF16),
            jax.ShapeDtypeStruct((nb, length, RET_W), F32),
            jax.ShapeDtypeStruct((nb, length, IN_W - RET_W), BF16),
        ],
        compiler_params=_cp(("arbitrary", "arbitrary")),
    )(xin, mod3, g_pre, wperm)


def premix_bwd(xin, mod3, g_pre, wperm, dproj, dx_tail, hosted, name):
    nb, length, _ = xin.shape
    tn = min(TN, length)
    is_ctx = dx_tail is None

    def body(*refs):
        own_in, h_in, own_out, h_out, _, h_sems = hosted.split(refs, 5 if is_ctx else 6, 2 if is_ctx else 3)
        if is_ctx:
            (x_ref, mod_ref, g_ref, w_ref, dp_ref), (dmod_ref, dg_ref) = own_in, own_out
        else:
            (x_ref, mod_ref, g_ref, w_ref, dp_ref, dxt_ref), (dx_ref, dmod_ref, dg_ref) = own_in, own_out
        b, t = pl.program_id(0), pl.program_id(1)
        grid_step = b * (length // tn) + t

        @pl.when(grid_step == 0)
        def _():
            hosted.start(h_in, h_out, h_sems)

        @pl.when(grid_step == nb * (length // tn) - 1)
        def _():
            hosted.finish(h_in, h_out, h_sems)

        dh = lax.dot_general(dp_ref[...], w_ref[...], (((1,), (1,)), ((), ())), preferred_element_type=F32)
        _, vjp = jax.vjp(_rms_mod, x_ref[...], g_ref[...], mod_ref[1:2, :], mod_ref[0:1, :])
        dx, dg, dsc, dsh = vjp(dh)
        if not is_ctx:
            dx_ref[...] = dx + dxt_ref[...]

        @pl.when((t == 0) & ((b == 0) if is_ctx else True))
        def _():
            dmod_ref[...] = jnp.zeros_like(dmod_ref)

        @pl.when((t == 0) & (b == 0))
        def _():
            dg_ref[...] = jnp.zeros_like(dg_ref)

        dmod_ref[0:1, :] += dsh
        dmod_ref[1:2, :] += dsc
        dg_ref[0:1, :] += dg

    tok = lambda b, t: (b, t, 0)
    in_specs = [
        pl.BlockSpec((None, tn, D), tok),
        pl.BlockSpec((None, 6, D), (lambda b, t: (2, 0, 0)) if is_ctx else (lambda b, t: (b, 0, 0))),
        pl.BlockSpec((1, D), lambda b, t: (0, 0)),
        pl.BlockSpec((D, IN_W), lambda b, t: (0, 0), pipeline_mode=pl.Buffered(1)),
        pl.BlockSpec((None, tn, IN_W), tok),
    ]
    args = [xin, mod3, g_pre, wperm, dproj]
    out_specs = [
        pl.BlockSpec((None, 6, D), (lambda b, t: (0, 0, 0)) if is_ctx else (lambda b, t: (b, 0, 0))),
        pl.BlockSpec((8, D), lambda b, t: (0, 0)),
    ]
    out_shape = [jax.ShapeDtypeStruct((1 if is_ctx else nb, 6, D), F32), jax.ShapeDtypeStruct((8, D), F32)]
    if not is_ctx:
        in_specs.append(pl.BlockSpec((None, tn, D), tok))
        args.append(dx_tail)
        out_specs.insert(0, pl.BlockSpec((None, tn, D), tok))
        out_shape.insert(0, jax.ShapeDtypeStruct((nb, length, D), F32))
    h_in_specs, h_out_specs = hosted.specs()
    return pl.pallas_call(
        body, name=name, grid=(nb, length // tn), in_specs=in_specs + h_in_specs, out_specs=out_specs + h_out_specs,
        out_shape=out_shape + hosted.out_shape, scratch_shapes=hosted.scratch,
        compiler_params=_cp(("arbitrary", "arbitrary")),
    )(*args, *hosted.args)


def _ret_specs(order):
    def im(f):
        return lambda *g: f(*order(*g))
    return dict(
        pret=pl.BlockSpec((None, SEQ, 512), im(lambda b, h: (b, 0, h))),
        pretc=pl.BlockSpec((None, LC, 512), im(lambda b, h: (b, 0, h))),
        rd=pl.BlockSpec((None, 2, 1), im(lambda b, h: (h, 0, 0))),
        gn=pl.BlockSpec((None, 1, RD), im(lambda b, h: (h, 0, 0))),
        tab=pl.BlockSpec((SEQ, RD), im(lambda b, h: (0, 0))),
        head=pl.BlockSpec((None, SEQ, RD), im(lambda b, h: (b, 0, h))),
    )


def retention_fwd(pret, pretc, rd, gn, cos, sin, hosted):
    nb = pret.shape[0]
    sp = _ret_specs(lambda b, h: (b, h))

    def body(*refs):
        own_in, h_in, own_out, h_out, own_scr, h_sems = hosted.split(refs, 6, 2)
        p_ref, pc_ref, rd_ref, gn_ref, cos_ref, sin_ref = own_in
        (o_ref, mix_ref), (q_s, k_s, o_s, st_s) = own_out, own_scr
        grid_step = pl.program_id(0) * RH + pl.program_id(1)

        @pl.when(grid_step == 0)
        def _():
            hosted.start(h_in, h_out, h_sems)

        cos_v, sin_v = cos_ref[...], sin_ref[...]
        q_s[...] = _rope(p_ref[:, 0:128], cos_v, sin_v) * (RD ** -0.5)
        k_s[...] = _rope(p_ref[:, 128:256], cos_v, sin_v)
        lgs, init = [], []
        for rev in (False, True):
            lg = jax.nn.log_sigmoid(rd_ref[int(rev):int(rev) + 1, :])
            s = jnp.zeros((RD, RD), F32)
            for n in ((1, 0) if rev else (0, 1)):
                s = _ret_state(pc_ref[n * CH:(n + 1) * CH, 128:256], pc_ref[n * CH:(n + 1) * CH, 256:384], s, lg, rev)
            lgs.append(lg)
            init.append(s)

        dec = _Decays(lgs)
        _state_pass(dec, init, k_s, lambda sl: p_ref[sl, 256:384], st_s)

        def chunk(n, carry):
            sl = pl.ds(pl.multiple_of(n * CH, CH), CH)
            q = q_s[sl, :]
            o_s[sl, :] = (_nn(_nt(q, k_s[sl, :]) * dec.mask, p_ref[sl, 256:384]) + _nn(_both(q, dec.a), st_s[n]))
            return carry

        _chunk_loop(NCH, chunk, 0)
        o = o_s[...]
        o_ref[...] = o
        mix_ref[...] = _ln_gate(o, p_ref[:, 384:512], gn_ref[...]).astype(BF16)

        @pl.when(grid_step == nb * RH - 1)
        def _():
            hosted.finish(h_in, h_out, h_sems)

    h_in_specs, h_out_specs = hosted.specs()
    return pl.pallas_call(
        body, name="retention_fwd", grid=(nb, RH),
        in_specs=[sp["pret"], sp["pretc"], sp["rd"], sp["gn"], sp["tab"], sp["tab"]] + h_in_specs,
        out_specs=[sp["head"], sp["head"]] + h_out_specs,
        out_shape=[jax.ShapeDtypeStruct((nb, SEQ, RH * RD), F32), jax.ShapeDtypeStruct((nb, SEQ, D), BF16)]
        + hosted.out_shape,
        scratch_shapes=[pltpu.VMEM((SEQ, RD), F32)] * 3 + [pltpu.VMEM((NCH, 2 * RD, RD), F32)] + hosted.scratch,
        compiler_params=_cp(("arbitrary", "arbitrary")),
    )(pret, pretc, rd, gn, cos, sin, *hosted.args)


def retention_bwd(pret, pretc, o_all, dmixin, rd, gn, cos, sin, hosted):
    nb = pret.shape[0]
    sp = _ret_specs(lambda h, b: (b, h))

    def body(*refs):
        own_in, h_in, own_out, h_out, own_scr, h_sems = hosted.split(refs, 8, 4)
        p_ref, pc_ref, o_ref, dmix_ref, rd_ref, gn_ref, cos_ref, sin_ref = own_in
        dp_ref, dpc_ref, drd_ref, dgn_ref = own_out
        q_s, k_s, do_s, dq_s, dk_s, dv_s, st_s, gst_s = own_scr
        b = pl.program_id(1)
        grid_step = pl.program_id(0) * nb + b

        @pl.when(grid_step == 0)
        def _():
            hosted.start(h_in, h_out, h_sems)

        cos_v, sin_v = cos_ref[...], sin_ref[...]
        q_s[...] = _rope(p_ref[:, 0:128], cos_v, sin_v) * (RD ** -0.5)
        k_s[...] = _rope(p_ref[:, 128:256], cos_v, sin_v)
        _, gate_vjp = jax.vjp(_ln_gate, o_ref[...], p_ref[:, 384:512], gn_ref[...])
        do, dg, dgn = gate_vjp(dmix_ref[...].astype(F32))
        do_s[...] = do
        dp_ref[:, 384:512] = dg.astype(BF16)

        @pl.when(b == 0)
        def _():
            drd_ref[...] = jnp.zeros_like(drd_ref)
            dgn_ref[...] = jnp.zeros_like(dgn_ref)

        dgn_ref[...] += dgn
        kcs = [pc_ref[n * CH:(n + 1) * CH, 128:256] for n in (0, 1)]
        vcs = [pc_ref[n * CH:(n + 1) * CH, 256:384] for n in (0, 1)]
        dirs = []
        init = []
        for rev in (False, True):
            rdv = rd_ref[int(rev):int(rev) + 1, :]
            lg = jax.nn.log_sigmoid(rdv)
            order_c = (1, 0) if rev else (0, 1)
            s = jnp.zeros((RD, RD), F32)
            ctx_states = []
            for n in order_c:
                ctx_states.append(s)
                s = _ret_state(kcs[n], vcs[n], s, lg, rev)
            dirs.append((rev, order_c, lg, rdv, ctx_states))
            init.append(s)
        dec = _Decays([lg for _, _, lg, _, _ in dirs])

        def v_of(sl):
            return p_ref[sl, 256:384]

        _state_pass(dec, init, k_s, v_of, st_s)
        zeros = jnp.zeros((CH, RD), F32)

        def scores_back(n, carry):
            dmask_sum, da_f, da_b = carry
            sl = pl.ds(pl.multiple_of(n * CH, CH), CH)
            q, k, v, do = q_s[sl, :], k_s[sl, :], v_of(sl), do_s[sl, :]
            scores = _nt(q, k)
            d_att = _nt(do, v)
            d_scores = d_att * dec.mask
            d_qa = _nt(do, st_s[n])
            d_qf, d_qb = d_qa[:, 0:RD], d_qa[:, RD:2 * RD]
            dq_s[sl, :] = _nn(d_scores, k) + d_qf * dec.a[0] + d_qb * dec.a[1]
            dk_s[sl, :] = _tn(d_scores, q)
            dv_s[sl, :] = _tn(scores * dec.mask, do)
            gst_s[n] = _tn(_both(q, dec.a), do)
            return dmask_sum + d_att * scores, da_f + d_qf * q, da_b + d_qb * q

        dmask_sum, da_f, da_b = _chunk_loop(NCH, scores_back, (zeros, zeros, zeros))

        def state_back(t, carry):
            out = []
            for d, r in enumerate(carry):
                n = t if d else (NCH - 1 - t)
                rows = slice(d * RD, (d + 1) * RD)
                own = gst_s[n, rows, :]
                gst_s[n, rows, :] = r
                out.append(own + dec.g[d] * r)
            return tuple(out)

        d_states = _chunk_loop(NCH, state_back, (zeros, zeros))

        def updates_back(n, carry):
            db_f, db_b, dg_f, dg_b = carry
            sl = pl.ds(pl.multiple_of(n * CH, CH), CH)
            k, r, s = k_s[sl, :], gst_s[n], st_s[n]
            d_kw = _nt(v_of(sl), r)
            d_kf, d_kb = d_kw[:, 0:RD], d_kw[:, RD:2 * RD]
            dk_s[sl, :] += d_kf * dec.b[0] + d_kb * dec.b[1]
            dv_s[sl, :] += _nn(_both(k, dec.b), r)
            return (db_f + d_kf * k, db_b + d_kb * k, dg_f + r[0:RD, :] * s[0:RD, :],
                    dg_b + r[RD:2 * RD, :] * s[RD:2 * RD, :])

        db_dg = _chunk_loop(NCH, updates_back, (zeros, zeros, zeros, zeros))
        dkc = [None, None]
        dvc = [None, None]
        for d, ((rev, order_c, lg, rdv, ctx_states), ds) in enumerate(zip(dirs, d_states)):
            dlg = (_total(dmask_sum * dec.dmask[d]) + _total((da_f, da_b)[d] * dec.da[d])
                   + _total(db_dg[d] * dec.db[d]) + CH * dec.g[d] * _total(db_dg[2 + d]))
            for idx in (1, 0):
                n = order_c[idx]
                _, vjp = jax.vjp(functools.partial(_ret_state, reverse=rev), kcs[n], vcs[n], ctx_states[idx], lg)
                dk_c, dv_c, ds, dl = vjp(ds)
                dlg = dlg + dl
                dkc[n] = dk_c if dkc[n] is None else dkc[n] + dk_c
                dvc[n] = dv_c if dvc[n] is None else dvc[n] + dv_c
            drd_ref[int(rev):int(rev) + 1, :] += dlg * jax.nn.sigmoid(-rdv)
        dp_ref[:, 0:128] = _rope_t(dq_s[...] * (RD ** -0.5), cos_v, sin_v).astype(BF16)
        dp_ref[:, 128:256] = _rope_t(dk_s[...], cos_v, sin_v).astype(BF16)
        dp_ref[:, 256:384] = dv_s[...].astype(BF16)
        zero = jnp.zeros((CH, RD), BF16)
        for n in (0, 1):
            rows = slice(n * CH, (n + 1) * CH)
            dpc_ref[rows, 0:128] = zero
            dpc_ref[rows, 128:256] = dkc[n].astype(BF16)
            dpc_ref[rows, 256:384] = dvc[n].astype(BF16)
            dpc_ref[rows, 384:512] = zero

        @pl.when(grid_step == RH * nb - 1)
        def _():
            hosted.finish(h_in, h_out, h_sems)

    h_in_specs, h_out_specs = hosted.specs()
    return pl.pallas_call(
        body, name="retention_bwd", grid=(RH, nb),
        in_specs=[sp["pret"], sp["pretc"], sp["head"], sp["head"], sp["rd"], sp["gn"], sp["tab"], sp["tab"]]
        + h_in_specs,
        out_specs=[
            pl.BlockSpec((None, SEQ, 512), lambda h, b: (b, 0, h)),
            pl.BlockSpec((None, LC, 512), lambda h, b: (b, 0, h)),
            pl.BlockSpec((None, 2, 1), lambda h, b: (h, 0, 0)),
            pl.BlockSpec((None, 1, RD), lambda h, b: (h, 0, 0)),
        ] + h_out_specs,
        out_shape=[
            jax.ShapeDtypeStruct((nb, SEQ, IN_W), BF16),
            jax.ShapeDtypeStruct((nb, LC, IN_W), BF16),
            jax.ShapeDtypeStruct((RH, 2, 1), F32),
            jax.ShapeDtypeStruct((RH, 1, RD), F32),
        ] + hosted.out_shape,
        scratch_shapes=[pltpu.VMEM((SEQ, RD), F32)] * 6 + [pltpu.VMEM((NCH, 2 * RD, RD), F32)] * 2 + hosted.scratch,
        compiler_params=_cp(("arbitrary", "arbitrary")),
    )(pret, pretc, o_all, dmixin, rd, gn, cos, sin, *hosted.args)


def _rpb_flat(rpb):
    return jnp.pad(rpb, ((0, 0), (0, 1), (0, 33))).reshape(NPAIR, 2, 1, 1024)


def _rpb_flat_t(dflat):
    return dflat.reshape(8, 16, 64)[:, :15, :31]


def _barrel(x, left):
    row = lax.broadcasted_iota(jnp.int32, x.shape, 0)
    n = x.shape[1]
    for bit in range(6):
        s = 1 << bit
        x = jnp.where(((row >> bit) & 1) == 1, pltpu.roll(x, (n - s) if left else s, 1), x)
    return x


NA_TILE_ROWS, NA_BAND_ROWS = 4, 12
NA_Q, NA_K = NA_TILE_ROWS * GW, NA_BAND_ROWS * GW
NA_TILES = SEQ // NA_Q


def _band_start(r0):
    return min(max(r0 - 4, 0), 32 - NA_BAND_ROWS)


def _tile_layout(t):
    rows = range(t * NA_TILE_ROWS, (t + 1) * NA_TILE_ROWS)
    return tuple((r if r < 4 else (r - 24 if r > 28 else 4), min(max(r - 4, 0), 24) - _band_start(rows[0]))
                 for r in rows)


NA_CLASSES = sorted(set(_tile_layout(t) for t in range(NA_TILES)))


def _tile_rows(cls):
    return NA_CLASSES[cls]


def _na_tile(t):
    start = jnp.clip(NA_TILE_ROWS * t - 4, 0, 32 - NA_BAND_ROWS)
    cls = 0
    for tile in range(NA_TILES):
        cls = jnp.where(t == tile, NA_CLASSES.index(_tile_layout(tile)), cls)
    return pl.ds(pl.multiple_of(t * NA_Q, NA_Q), NA_Q), pl.ds(pl.multiple_of(start * GW, NA_Q), NA_K), cls


def _na_probs(qst, kb, kc, bias):
    s_loc = _nt(qst, kb) + bias
    s_ctx = _nt(qst, kc)
    m = jnp.maximum(jnp.max(s_loc, axis=1, keepdims=True), jnp.max(s_ctx, axis=1, keepdims=True))
    e_loc, e_ctx = jnp.exp(s_loc - m), jnp.exp(s_ctx - m)
    den = jnp.sum(e_loc, axis=1, keepdims=True) + jnp.sum(e_ctx, axis=1, keepdims=True)
    return e_loc / den, e_ctx / den


def _stack_heads(t):
    lane = lax.broadcasted_iota(jnp.int32, t.shape, 1)
    zero = jnp.zeros_like(t)
    return jnp.concatenate([jnp.where(lane < 64, t, zero), jnp.where(lane >= 64, t, zero)], axis=0)


def _unstack_heads(t):
    n = t.shape[0] // 2
    lane = lax.broadcasted_iota(jnp.int32, (n, 128), 1)
    return jnp.where(lane < 64, t[:n], t[n:])


NA_BIAS_SHAPE = (len(NA_CLASSES), 2 * NA_Q, NA_K)


def _na_bias_pair(flat_ref, out_ref):
    qc = lax.broadcasted_iota(jnp.int32, (GW, 512), 0)
    kc = lax.broadcasted_iota(jnp.int32, (GW, 512), 1) & 63
    start = jnp.clip(qc - 8, 0, GW - 16)
    window = (kc >= start) & (kc < start + 16)
    fill = jnp.full((GW, NA_K - 512), NEG, F32)
    for hh in (0, 1):
        skew = _barrel(pltpu.roll(jnp.broadcast_to(flat_ref[hh], (GW, 1024)), 1024 - 15, 1), left=False)
        by_class = [jnp.where(window, (skew if rc == 7 else pltpu.roll(skew, (9 + rc) * 64, 1))[:, 0:512], NEG)
                    for rc in range(8)]
        for cls in range(len(NA_CLASSES)):
            for qr, (rc, off) in enumerate(_tile_rows(cls)):
                w = jnp.concatenate([by_class[rc], fill], axis=1)
                rows = slice(hh * NA_Q + qr * GW, hh * NA_Q + (qr + 1) * GW)
                out_ref[cls, rows, :] = pltpu.roll(w, off * GW, 1) if off else w


def na_fwd(pna, pnac, bias, mixin, hosted):
    nb = pna.shape[0]

    def body(*refs):
        (p_ref, pc_ref, bias_ref, _), h_in, (out_ref,), h_out, _, h_sems = hosted.split(refs, 4, 1)
        grid_step = pl.program_id(0) * nb + pl.program_id(1)

        @pl.when(grid_step == 0)
        def _():
            hosted.start(h_in, h_out, h_sems)

        kc, vc = pc_ref[:, 128:256], pc_ref[:, 256:384]

        def tile(t, carry):
            qsl, bsl, cls = _na_tile(t)
            kb, vb = p_ref[bsl, 128:256], p_ref[bsl, 256:384]
            p_loc, p_ctx = _na_probs(_stack_heads(p_ref[qsl, 0:128] * 0.125), kb, kc, bias_ref[cls])
            out_ref[qsl, :] = _unstack_heads(_nn(p_loc, vb) + _nn(p_ctx, vc)).astype(BF16)
            return carry

        lax.fori_loop(0, NA_TILES, tile, 0, unroll=4)

        @pl.when(grid_step == NPAIR * nb - 1)
        def _():
            hosted.finish(h_in, h_out, h_sems)

    h_in_specs, h_out_specs = hosted.specs()
    return pl.pallas_call(
        body, name="na_fwd", grid=(NPAIR, nb),
        in_specs=[
            pl.BlockSpec((None, SEQ, 384), lambda p, b: (b, 0, p)),
            pl.BlockSpec((None, LC, 384), lambda p, b: (b, 0, p)),
            pl.BlockSpec((None, len(NA_CLASSES), 2 * NA_Q, NA_K), lambda p, b: (p, 0, 0, 0)),
            pl.BlockSpec(memory_space=pl.ANY),
        ] + h_in_specs,
        out_specs=[pl.BlockSpec((None, SEQ, 128), lambda p, b: (b, 0, 4 + p))] + h_out_specs,
        out_shape=[jax.ShapeDtypeStruct((nb, SEQ, D), BF16)] + hosted.out_shape,
        input_output_aliases={3: 0},
        scratch_shapes=hosted.scratch,
        compiler_params=_cp(("arbitrary", "arbitrary")),
    )(pna, pnac, bias, mixin, *hosted.args)


def na_bwd(pna, pnac, bias, dmixin, dproj, dprojc, hosted):
    nb = pna.shape[0]

    def body(*refs):
        own_in, h_in, own_out, h_out, own_scr, h_sems = hosted.split(refs, 6, 3)
        p_ref, pc_ref, bias_ref, dmix_ref = own_in[:4]
        dp_ref, dpc_ref, dpat_ref = own_out
        dbias_s, dk_s, dv_s, dkc_s, dvc_s, res_s, resc_s = own_scr
        b, part = pl.program_id(1), pl.program_id(2)
        grid_step = (pl.program_id(0) * nb + b) * 3 + part

        @pl.when(grid_step == 0)
        def _():
            hosted.start(h_in, h_out, h_sems)

        @pl.when(grid_step == NPAIR * nb * 3 - 1)
        def _():
            hosted.finish(h_in, h_out, h_sems)

        @pl.when(part == 0)
        def _():
            @pl.when(b == 0)
            def _():
                dbias_s[...] = jnp.zeros_like(dbias_s)

            dk_s[...] = jnp.zeros_like(dk_s)
            dv_s[...] = jnp.zeros_like(dv_s)
            dkc_s[...] = jnp.zeros_like(dkc_s)
            dvc_s[...] = jnp.zeros_like(dvc_s)
            kc, vc = pc_ref[:, 128:256], pc_ref[:, 256:384]

            def tile(t, carry):
                qsl, bsl, cls = _na_tile(t)
                kb, vb = p_ref[bsl, 128:256], p_ref[bsl, 256:384]
                qst, dost = _stack_heads(p_ref[qsl, 0:128] * 0.125), _stack_heads(dmix_ref[qsl, :])
                p_loc, p_ctx = _na_probs(qst, kb, kc, bias_ref[cls])
                dp_loc, dp_ctx = _nt(dost, vb), _nt(dost, vc)
                delta = (jnp.sum(p_loc * dp_loc, axis=1, keepdims=True)
                         + jnp.sum(p_ctx * dp_ctx, axis=1, keepdims=True))
                ds_loc, ds_ctx = p_loc * (dp_loc - delta), p_ctx * (dp_ctx - delta)
                dbias_s[cls] += ds_loc
                res_s[0, qsl, :] = _unstack_heads((_nn(ds_loc, kb) + _nn(ds_ctx, kc)) * 0.125).astype(BF16)
                dk_s[bsl, :] += _tn(ds_loc, qst)
                dv_s[bsl, :] += _tn(p_loc, dost)
                dkc_s[...] += _tn(ds_ctx, qst)
                dvc_s[...] += _tn(p_ctx, dost)
                return carry

            lax.fori_loop(0, NA_TILES, tile, 0, unroll=2)
            res_s[1] = dk_s[...].astype(BF16)
            res_s[2] = dv_s[...].astype(BF16)
            resc_s[0] = jnp.zeros((LC, 128), BF16)
            resc_s[1] = dkc_s[...].astype(BF16)
            resc_s[2] = dvc_s[...].astype(BF16)

            @pl.when(b == nb - 1)
            def _():
                for hh in (0, 1):
                    by_class = [None] * 8
                    for cls in range(len(NA_CLASSES)):
                        for qr, (rc, off) in enumerate(_tile_rows(cls)):
                            w = dbias_s[cls, hh * NA_Q + qr * GW:hh * NA_Q + (qr + 1) * GW, :]
                            w = (pltpu.roll(w, NA_K - off * GW, 1) if off else w)[:, 0:512]
                            by_class[rc] = w if by_class[rc] is None else by_class[rc] + w
                    skew = jnp.zeros((GW, 1024), F32)
                    for rc in range(8):
                        w = jnp.concatenate([by_class[rc], jnp.zeros((GW, 512), F32)], axis=1)
                        skew = skew + (w if rc == 7 else pltpu.roll(w, (7 - rc) * 64, 1))
                    dpat_ref[hh] = jnp.sum(pltpu.roll(_barrel(skew, left=True), 15, 1), axis=0, keepdims=True)

        dp_ref[...] = res_s[part]
        dpc_ref[...] = resc_s[part]

    h_in_specs, h_out_specs = hosted.specs()
    return pl.pallas_call(
        body, name="na_bwd", grid=(NPAIR, nb, 3),
        in_specs=[
            pl.BlockSpec((None, SEQ, 384), lambda p, b, s: (b, 0, p)),
            pl.BlockSpec((None, LC, 384), lambda p, b, s: (b, 0, p)),
            pl.BlockSpec((None, len(NA_CLASSES), 2 * NA_Q, NA_K), lambda p, b, s: (p, 0, 0, 0)),
            pl.BlockSpec((None, SEQ, 128), lambda p, b, s: (b, 0, 4 + p)),
            pl.BlockSpec(memory_space=pl.ANY),
            pl.BlockSpec(memory_space=pl.ANY),
        ] + h_in_specs,
        out_specs=[
            pl.BlockSpec((None, SEQ, 128), lambda p, b, s: (b, 0, 16 + 3 * p + s)),
            pl.BlockSpec((None, LC, 128), lambda p, b, s: (b, 0, 16 + 3 * p + s)),
            pl.BlockSpec((None, 2, 1, 1024), lambda p, b, s: (p, 0, 0, 0)),
        ] + h_out_specs,
        out_shape=[
            jax.ShapeDtypeStruct((nb, SEQ, IN_W), BF16),
            jax.ShapeDtypeStruct((nb, LC, IN_W), BF16),
            jax.ShapeDtypeStruct((NPAIR, 2, 1, 1024), F32),
        ] + hosted.out_shape,
        input_output_aliases={4: 0, 5: 1},
        scratch_shapes=[
            pltpu.VMEM((len(NA_CLASSES), 2 * NA_Q, NA_K), F32),
            pltpu.VMEM((SEQ, 128), F32), pltpu.VMEM((SEQ, 128), F32),
            pltpu.VMEM((LC, 128), F32), pltpu.VMEM((LC, 128), F32),
            pltpu.VMEM((3, SEQ, 128), BF16), pltpu.VMEM((3, LC, 128), BF16),
        ] + hosted.scratch,
        compiler_params=_cp(("arbitrary", "arbitrary", "arbitrary")),
    )(pna, pnac, bias, dmixin, dproj, dprojc, *hosted.args)


def tail_fwd_bwd(x, mixin, tgt, mod3, g_post_mix, g_pre_mlp, g_post_mlp, wout, w1, w2):
    nb = x.shape[0]

    def body(x_ref, mi_ref, tgt_ref, mod_ref, gpm_ref, gpl_ref, gpo_ref, wo_ref, w1_ref, w2_ref,
             dx_ref, dmix_ref, h2_ref, du_ref, a_ref, dm_ref, dmi_ref, dmod_ref, dg_ref, loss_ref):
        b, t = pl.program_id(0), pl.program_id(1)
        gt1, sh2, sc2, gt2 = mod_ref[2:3, :], mod_ref[3:4, :], mod_ref[4:5, :], mod_ref[5:6, :]
        mix = jnp.dot(mi_ref[...], wo_ref[...], preferred_element_type=F32)
        (x1, h2), vjp_a = jax.vjp(_post_mix, x_ref[...], mix, gt1, sc2, sh2, gpm_ref[...], gpl_ref[...])
        h2b = h2.astype(BF16)
        h2_ref[...] = h2b
        m = jnp.zeros((TN, D), F32)
        relus = []
        for j in range(4):
            cols = slice(j * D, (j + 1) * D)
            r = jnp.maximum(jnp.dot(h2b, w1_ref[j], preferred_element_type=F32), 0.0)
            ab = (r * r).astype(BF16)
            a_ref[:, cols] = ab
            m = m + jnp.dot(ab, w2_ref[cols, :], preferred_element_type=F32)
            relus.append(r)
        loss, vjp_b = jax.vjp(_head_loss, x1, m, gt2, gpo_ref[...], tgt_ref[...])
        dx1, dm, dgt2, dgpo, _ = vjp_b(jnp.ones((1, 1), F32))
        dmb = dm.astype(BF16)
        dm_ref[...] = dmb
        dh2 = jnp.zeros((TN, D), F32)
        for j in range(4):
            cols = slice(j * D, (j + 1) * D)
            da = lax.dot_general(dmb, w2_ref[cols, :], (((1,), (1,)), ((), ())), preferred_element_type=F32)
            dub = (da * (2.0 * relus[j])).astype(BF16)
            du_ref[:, cols] = dub
            dh2 = dh2 + lax.dot_general(dub, w1_ref[j], (((1,), (1,)), ((), ())), preferred_element_type=F32)
        dx, dmix, dgt1, dsc2, dsh2, dgpm, dgpl = vjp_a((dx1, dh2))
        dx_ref[...] = dx
        dmixb = dmix.astype(BF16)
        dmix_ref[...] = dmixb
        dmi_ref[...] = lax.dot_general(dmixb, wo_ref[...], (((1,), (1,)), ((), ())),
                                       preferred_element_type=F32).astype(BF16)

        @pl.when(t == 0)
        def _():
            dmod_ref[...] = jnp.zeros_like(dmod_ref)

        @pl.when((t == 0) & (b == 0))
        def _():
            dg_ref[...] = jnp.zeros_like(dg_ref)
            loss_ref[...] = jnp.zeros_like(loss_ref)

        dmod_ref[2:3, :] += dgt1
        dmod_ref[3:4, :] += dsh2
        dmod_ref[4:5, :] += dsc2
        dmod_ref[5:6, :] += dgt2
        dg_ref[0:1, :] += dgpm
        dg_ref[1:2, :] += dgpl
        dg_ref[2:3, :] += dgpo
        loss_ref[...] += jnp.broadcast_to(loss, loss_ref.shape)

    tok = lambda b, t: (b, t, 0)
    const = lambda b, t: (0, 0)
    vec = pl.BlockSpec((1, D), const)
    return pl.pallas_call(
        body, name="tail_fwd_bwd", grid=(nb, SEQ // TN),
        in_specs=[
            pl.BlockSpec((None, TN, D), tok), pl.BlockSpec((None, TN, D), tok), pl.BlockSpec((None, TN, D), tok),
            pl.BlockSpec((None, 6, D), lambda b, t: (b, 0, 0)), vec, vec, vec,
            pl.BlockSpec((D, D), const, pipeline_mode=pl.Buffered(1)),
            pl.BlockSpec((4, D, D), lambda b, t: (0, 0, 0), pipeline_mode=pl.Buffered(1)),
            pl.BlockSpec((DFF, D), const, pipeline_mode=pl.Buffered(1)),
        ],
        out_specs=[
            pl.BlockSpec((None, TN, D), tok), pl.BlockSpec((None, TN, D), tok), pl.BlockSpec((None, TN, D), tok),
            pl.BlockSpec((None, TN, DFF), tok), pl.BlockSpec((None, TN, DFF), tok), pl.BlockSpec((None, TN, D), tok),
            pl.BlockSpec((None, TN, D), tok),
            pl.BlockSpec((None, 6, D), lambda b, t: (b, 0, 0)),
            pl.BlockSpec((8, D), const), pl.BlockSpec((8, 128), const),
        ],
        out_shape=[
            jax.ShapeDtypeStruct((nb, SEQ, D), F32), jax.ShapeDtypeStruct((nb, SEQ, D), BF16),
            jax.ShapeDtypeStruct((nb, SEQ, D), BF16), jax.ShapeDtypeStruct((nb, SEQ, DFF), BF16),
            jax.ShapeDtypeStruct((nb, SEQ, DFF), BF16), jax.ShapeDtypeStruct((nb, SEQ, D), BF16),
            jax.ShapeDtypeStruct((nb, SEQ, D), BF16),
            jax.ShapeDtypeStruct((nb, 6, D), F32), jax.ShapeDtypeStruct((8, D), F32),
            jax.ShapeDtypeStruct((8, 128), F32),
        ],
        compiler_params=_cp(("arbitrary", "arbitrary")),
    )(x, mixin, tgt, mod3, g_post_mix, g_pre_mlp, g_post_mlp, wout, w1, w2)


def weight_grad(pairs, name, out_dtype=F32, col_blocks=False, tm=1024, tn=1024, tk=2048):
    m, n = pairs[0][0].shape[1], pairs[0][1].shape[1]
    tn = min(tn, n)
    tks = [min(tk, xa.shape[0]) for xa, _ in pairs]
    steps = [xa.shape[0] // t for (xa, _), t in zip(pairs, tks)]
    total = sum(steps)
    offs = [sum(steps[:i]) for i in range(len(pairs))]

    def body(*refs):
        out_ref, acc = refs[2 * len(pairs)], refs[-1]
        k = pl.program_id(2)

        @pl.when(k == 0)
        def _():
            acc[...] = jnp.zeros_like(acc)

        for i in range(len(pairs)):
            @pl.when((k >= offs[i]) & (k < offs[i] + steps[i]))
            def _(i=i):
                acc[...] += lax.dot_general(refs[2 * i][...], refs[2 * i + 1][...], (((0,), (0,)), ((), ())),
                                            preferred_element_type=F32)

        if out_dtype != F32:
            @pl.when(k == total - 1)
            def _():
                out_ref[...] = acc[...].astype(out_dtype)

    in_specs, args = [], []
    for i, (xa, ya) in enumerate(pairs):
        clamp = lambda k, i=i: jnp.clip(k - offs[i], 0, steps[i] - 1)
        in_specs.append(pl.BlockSpec((tks[i], tm), lambda a, c, k, clamp=clamp: (clamp(k), a)))
        in_specs.append(pl.BlockSpec((tks[i], tn), lambda a, c, k, clamp=clamp: (clamp(k), c)))
        args += [xa, ya]
    if col_blocks:
        out_spec = pl.BlockSpec((None, tm, tn), lambda a, c, k: (c, a, 0))
        out_shape = jax.ShapeDtypeStruct((n // tn, m, tn), out_dtype)
    else:
        out_spec = pl.BlockSpec((tm, tn), lambda a, c, k: (a, c))
        out_shape = jax.ShapeDtypeStruct((m, n), out_dtype)
    return pl.pallas_call(
        body, name=name, grid=(m // tm, n // tn, total), in_specs=in_specs, out_specs=out_spec, out_shape=out_shape,
        scratch_shapes=[] if out_dtype == F32 else [pltpu.VMEM((tm, tn), F32)],
        compiler_params=_cp(("arbitrary", "arbitrary", "arbitrary")),
    )(*args)


def _perm_block(t):
    return 4 * (t % 4) + t // 4 if t < 16 else 16 + 3 * ((t - 16) % 4) + (t - 16) // 4


def unpack_w_in(blocks):
    def body(i_ref, o_ref):
        for t in range(28):
            p = _perm_block(t)
            o_ref[:, p * 128:(p + 1) * 128] = i_ref[t // 7, :, (t % 7) * 128:(t % 7 + 1) * 128]

    return pl.pallas_call(
        body, name="unpack_w_in", grid=(2,),
        in_specs=[pl.BlockSpec((4, D // 2, 896), lambda i: (0, i, 0))],
        out_specs=pl.BlockSpec((D // 2, IN_W), lambda i: (i, 0)),
        out_shape=jax.ShapeDtypeStruct((D, IN_W), BF16),
    )(blocks)


def pack_w_in(dw):
    def body(i_ref, o_ref):
        for t in range(28):
            p = _perm_block(t)
            o_ref[t // 7, :, (t % 7) * 128:(t % 7 + 1) * 128] = i_ref[:, p * 128:(p + 1) * 128].astype(BF16)

    return pl.pallas_call(
        body, name="pack_w_in", grid=(4,),
        in_specs=[pl.BlockSpec((D // 4, IN_W), lambda i: (i, 0))],
        out_specs=pl.BlockSpec((4, D // 4, 896), lambda i: (0, i, 0)),
        out_shape=jax.ShapeDtypeStruct((4, D, 896), BF16),
    )(dw)


def _place():
    return lax.axis_index("x"), lax.axis_index("y"), lax.axis_index("c")


class Hosted:
    def __init__(self, args, out_shape, scratch, start, finish):
        self.args, self.out_shape, self.scratch, self.start, self.finish = args, out_shape, scratch, start, finish

    def specs(self):
        hbm = pl.BlockSpec(memory_space=pl.ANY)
        return [hbm] * len(self.args), [hbm] * len(self.out_shape)

    def split(self, refs, n_in, n_out):
        a, b = len(self.args), len(self.out_shape)
        cuts = [n_in, n_in + a, n_in + a + n_out, n_in + a + n_out + b, len(refs) - len(self.scratch)]
        parts = [refs[i:j] for i, j in zip([0] + cuts, cuts + [len(refs)])]
        return parts[0], parts[1], parts[2], parts[3], parts[4], parts[5]


def no_exchange():
    return Hosted([], [], [], lambda *a: None, lambda *a: None)


def run_hosted(hosted, name):
    def body(*refs):
        _, ins, _, outs, _, sems = hosted.split(refs, 0, 0)
        hosted.start(ins, outs, sems)
        hosted.finish(ins, outs, sems)

    in_specs, out_specs = hosted.specs()
    return pl.pallas_call(body, name=name, in_specs=in_specs, out_specs=out_specs, out_shape=hosted.out_shape,
                          scratch_shapes=hosted.scratch)(*hosted.args)


def gather8(blocks):
    na = len(blocks)

    def copies(ins, outs, sems):
        send_sems, recv_sems, local_sem = sems
        x, y, c = _place()
        me, sibling = (x, y, c), (x, y, 1 - c)
        chips = [(1 - x, y), (x, 1 - y), (1 - x, 1 - y)]

        def slot(o_ref, px, py, pc):
            return o_ref.at[4 * px + 2 * py + pc]

        def copy(a, k, block, to, src=None):
            return pltpu.make_async_remote_copy(
                src_ref=slot(outs[a], *block) if src is None else src, dst_ref=slot(outs[a], *block),
                send_sem=send_sems.at[a, k], recv_sem=recv_sems.at[a, k], device_id=to, device_id_type=MESH)

        mine = [pltpu.make_async_copy(ins[a], slot(outs[a], *me), local_sem.at[a]) for a in range(na)]
        first = []
        for a in range(na):
            first.append(copy(a, 0, me, sibling, src=ins[a]))
            first += [copy(a, 1 + j, me, (*chip, c), src=ins[a]) for j, chip in enumerate(chips)]
        return copy, mine, first, me, sibling, chips, c

    def start(ins, outs, sems):
        _, mine, first, *_ = copies(ins, outs, sems)
        for cp in mine + first:
            cp.start()

    def finish(ins, outs, sems):
        copy, mine, first, me, sibling, chips, c = copies(ins, outs, sems)
        passed = []
        for j, chip in enumerate(chips):
            for a in range(na):
                copy(a, 1 + j, (*chip, c), me).wait_recv()
                cp = copy(a, 4 + j, (*chip, c), sibling)
                cp.start()
                passed.append(cp)
        for a in range(na):
            copy(a, 0, sibling, me).wait_recv()
            for j, chip in enumerate(chips):
                copy(a, 4 + j, (*chip, 1 - c), me).wait_recv()
        for cp in first + passed:
            cp.wait_send()
        for cp in mine:
            cp.wait()

    return Hosted(list(blocks), [jax.ShapeDtypeStruct((8,) + b.shape, b.dtype) for b in blocks],
                  [pltpu.SemaphoreType.DMA((na, 7)), pltpu.SemaphoreType.DMA((na, 7)), pltpu.SemaphoreType.DMA((na,))],
                  start, finish)


def chips3(arrays):
    na = len(arrays)

    def copies(ins, outs, sems):
        send_sems, recv_sems = sems
        x, y, c = _place()
        return [pltpu.make_async_remote_copy(
            src_ref=ins[a].at[2 * px + py], dst_ref=outs[a].at[k], send_sem=send_sems.at[a, k],
            recv_sem=recv_sems.at[a, k], device_id=(px, py, c), device_id_type=MESH)
            for a in range(na) for k, (px, py) in enumerate([(1 - x, y), (x, 1 - y), (1 - x, 1 - y)])]

    def start(ins, outs, sems):
        for cp in copies(ins, outs, sems):
            cp.start()

    def finish(ins, outs, sems):
        for cp in copies(ins, outs, sems):
            cp.wait()

    return Hosted(list(arrays), [jax.ShapeDtypeStruct((3,) + a.shape[1:], a.dtype) for a in arrays],
                  [pltpu.SemaphoreType.DMA((na, 3)), pltpu.SemaphoreType.DMA((na, 3))], start, finish)


def siblings(arrays):
    na = len(arrays)

    def copies(ins, outs, sems):
        send_sems, recv_sems = sems
        x, y, c = _place()
        return [pltpu.make_async_remote_copy(
            src_ref=ins[a], dst_ref=outs[a], send_sem=send_sems.at[a], recv_sem=recv_sems.at[a],
            device_id=(x, y, 1 - c), device_id_type=MESH) for a in range(na)]

    def start(ins, outs, sems):
        for cp in copies(ins, outs, sems):
            cp.start()

    def finish(ins, outs, sems):
        for cp in copies(ins, outs, sems):
            cp.wait()

    return Hosted(list(arrays), [jax.ShapeDtypeStruct(a.shape, a.dtype) for a in arrays],
                  [pltpu.SemaphoreType.DMA((na,)), pltpu.SemaphoreType.DMA((na,))], start, finish)


def both(first, second):
    na, no, ns = len(first.args), len(first.out_shape), len(first.scratch)

    def start(ins, outs, sems):
        first.start(ins[:na], outs[:no], sems[:ns])
        second.start(ins[na:], outs[no:], sems[ns:])

    def finish(ins, outs, sems):
        first.finish(ins[:na], outs[:no], sems[:ns])
        second.finish(ins[na:], outs[no:], sems[ns:])

    return Hosted(first.args + second.args, first.out_shape + second.out_shape, first.scratch + second.scratch,
                  start, finish)


def siblings4(arrays):
    na = len(arrays)

    def copies(ins, outs, sems):
        send_sems, recv_sems = sems
        x, y, c = _place()
        return [pltpu.make_async_remote_copy(
            src_ref=ins[a].at[2 * j + 1 - c], dst_ref=outs[a].at[j],
            send_sem=send_sems.at[a, j], recv_sem=recv_sems.at[a, j],
            device_id=(x, y, 1 - c), device_id_type=MESH) for a in range(na) for j in range(4)]

    def start(ins, outs, sems):
        for cp in copies(ins, outs, sems):
            cp.start()

    def finish(ins, outs, sems):
        for cp in copies(ins, outs, sems):
            cp.wait()

    return Hosted(list(arrays), [jax.ShapeDtypeStruct((4,) + a.shape[1:], a.dtype) for a in arrays],
                  [pltpu.SemaphoreType.DMA((na, 4)), pltpu.SemaphoreType.DMA((na, 4))], start, finish)


def _row_tile(r):
    for cand in (512, 256, 128, 64, 32, 16, 8):
        if r % cand == 0:
            return cand
    return r


def chip_partial(place, g8s, landed4s, name):
    n = len(g8s)

    def body(place_ref, *refs):
        del place_ref
        for g_ref, l_ref, o_ref in zip(refs[:n], refs[n:2 * n], refs[2 * n:]):
            o_ref[...] = (g_ref[...].astype(F32) + l_ref[...].astype(F32)).astype(BF16)

    own = [pl.BlockSpec((None,) + g.shape[1:], lambda j, s: (2 * j + s[0], 0, 0)) for g in g8s]
    plain = [pl.BlockSpec((None,) + g.shape[1:], lambda j, s: (j, 0, 0)) for g in g8s]
    return pl.pallas_call(
        body, name=name,
        grid_spec=pltpu.PrefetchScalarGridSpec(num_scalar_prefetch=1, grid=(4,), in_specs=own + plain, out_specs=plain),
        out_shape=[jax.ShapeDtypeStruct((4,) + g.shape[1:], BF16) for g in g8s],
    )(place, *g8s, *landed4s)


def shard_sum(place, partial4s, landed3s, name):
    n = len(partial4s)

    def body(place_ref, *refs):
        del place_ref
        for p_ref, l_ref, o_ref in zip(refs[:n], refs[n:2 * n], refs[2 * n:]):
            acc = p_ref[...].astype(F32)
            for k in range(3):
                acc = acc + l_ref[k].astype(F32)
            o_ref[...] = acc

    def halves(p, lead):
        r, ccols = p.shape[1:]
        return (lead, r // 2, ccols)

    return pl.pallas_call(
        body, name=name,
        grid_spec=pltpu.PrefetchScalarGridSpec(
            num_scalar_prefetch=1, grid=(2,),
            in_specs=[pl.BlockSpec(halves(p, None), lambda i, s: (s[1], i, 0)) for p in partial4s]
            + [pl.BlockSpec(halves(p, 3), lambda i, s: (0, i, 0)) for p in partial4s],
            out_specs=[pl.BlockSpec(halves(p, None)[1:], lambda i, s: (i, 0)) for p in partial4s]),
        out_shape=[jax.ShapeDtypeStruct(p.shape[1:], F32) for p in partial4s],
    )(place, *partial4s, *landed3s)


def _adamw_math(w, g, m, v):
    m2 = B1 * m + (1.0 - B1) * g
    v2 = B2 * v + (1.0 - B2) * (g * g)
    m_hat = m2 / (1.0 - B1 ** STEP)
    v_hat = v2 / (1.0 - B2 ** STEP)
    return -LR * (m_hat / (jnp.sqrt(v_hat) + AEPS) + WD * w), m2, v2


def adamw_halves(place, w, mine, theirs, m, v, name):
    r, ccols = w.shape
    hr = r // 2
    tr = _row_tile(hr)
    nt = hr // tr

    def body(place_ref, w_ref, a_ref, b_ref, m_ref, v_ref, g_out, d_out, m_out, v_out):
        g = jnp.where(pl.program_id(0) == place_ref[0], a_ref[...], b_ref[...])
        d, m2, v2 = _adamw_math(w_ref[...], g, m_ref[...], v_ref[...])
        g_out[...] = g
        d_out[...] = d
        m_out[...] = m2
        v_out[...] = v2

    full = pl.BlockSpec((tr, ccols), lambda h, i, s: (h * nt + i, 0))
    part = pl.BlockSpec((tr, ccols), lambda h, i, s: (i, 0))
    return pl.pallas_call(
        body, name=name,
        grid_spec=pltpu.PrefetchScalarGridSpec(
            num_scalar_prefetch=1, grid=(2, nt), in_specs=[full, part, part, full, full], out_specs=[full] * 4),
        out_shape=[jax.ShapeDtypeStruct((r, ccols), F32)] * 4,
    )(place, w, mine, theirs, m, v)


def adamw_group(place, halved, plain, hosted, name):
    rows = halved[0][0].shape[0]
    tr = 64
    nt = rows // 2 // tr
    nh, npl = len(halved), len(plain)

    def body(place_ref, *refs):
        own_in, h_in, own_out, h_out, _, h_sems = hosted.split(refs, 5 * nh + 4 * npl, 4 * nh + 3 * npl)
        half = pl.program_id(0)
        grid_step = half * nt + pl.program_id(1)

        @pl.when(grid_step == 0)
        def _():
            hosted.start(h_in, h_out, h_sems)

        for i in range(nh):
            w_ref, a_ref, b_ref, m_ref, v_ref = own_in[5 * i:5 * i + 5]
            g = jnp.where(half == place_ref[0], a_ref[...], b_ref[...])
            res = (g,) + _adamw_math(w_ref[...], g, m_ref[...], v_ref[...])
            for o_ref, r in zip(own_out[4 * i:4 * i + 4], res):
                o_ref[...] = r
        for i in range(npl):
            w_ref, g_ref, m_ref, v_ref = own_in[5 * nh + 4 * i:5 * nh + 4 * i + 4]
            res = _adamw_math(w_ref[...], g_ref[...], m_ref[...], v_ref[...])
            for o_ref, r in zip(own_out[4 * nh + 3 * i:4 * nh + 3 * i + 3], res):
                o_ref[...] = r

        @pl.when(grid_step == 2 * nt - 1)
        def _():
            hosted.finish(h_in, h_out, h_sems)

    def full(cols):
        return pl.BlockSpec((tr, cols), lambda h, i, s: (h * nt + i, 0))

    def part(cols):
        return pl.BlockSpec((tr, cols), lambda h, i, s: (i, 0))

    in_specs, out_specs, out_shape, args = [], [], [], []
    for w, a, b, m, v in halved:
        cols = w.shape[1]
        in_specs += [full(cols), part(cols), part(cols), full(cols), full(cols)]
        out_specs += [full(cols)] * 4
        out_shape += [jax.ShapeDtypeStruct(w.shape, F32)] * 4
        args += [w, a, b, m, v]
    for w, g, m, v in plain:
        cols = w.shape[1]
        in_specs += [full(cols)] * 4
        out_specs += [full(cols)] * 3
        out_shape += [jax.ShapeDtypeStruct(w.shape, F32)] * 3
        args += [w, g, m, v]
    h_in_specs, h_out_specs = hosted.specs()
    return pl.pallas_call(
        body, name=name,
        grid_spec=pltpu.PrefetchScalarGridSpec(
            num_scalar_prefetch=1, grid=(2, nt), in_specs=in_specs + h_in_specs, out_specs=out_specs + h_out_specs,
            scratch_shapes=hosted.scratch),
        out_shape=out_shape + hosted.out_shape,
        compiler_params=_cp(("arbitrary", "arbitrary")),
    )(place, *args, *hosted.args)


def _silu(x):
    return x * jax.nn.sigmoid(x)


def prologue(c_rows, c_ctx_row, w_ada, b_shard, rpb_flat, half_w_in, late_shards):
    shape = jax.ShapeDtypeStruct
    n_late = len(late_shards)
    half_shapes = [(w.shape[0] // 2, w.shape[1]) for w in late_shards]
    g_w = gather8([half_w_in])
    g_c = gather8([shape((8, D), F32)])
    g_m = gather8([shape((32, 1536), F32)])

    def body(*refs):
        c_ref, cc_ref, w_ref, b_ref, flat_ref, hw_ref = refs[:6]
        late_refs = refs[6:6 + n_late]
        cin_ref, mg_ref, gw_ref, bias_ref, cos_ref, sin_ref = refs[6 + n_late:12 + n_late]
        rest = refs[12 + n_late:]
        half_refs, (cg_s, ms_s, bias_s) = rest[:n_late], rest[n_late:n_late + 3]
        stage, (load_sem, bias_sem), sems = rest[n_late + 3:2 * n_late + 3], rest[2 * n_late + 3:2 * n_late + 5], \
            rest[2 * n_late + 5:]
        sw, sc, sm = sems[0:3], sems[3:6], sems[6:9]
        core = lax.axis_index("c")
        g_c.start([c_ref], [cg_s], sc)
        loads = [pltpu.make_async_copy(late_refs[a].at[pl.ds(core * half_shapes[a][0], half_shapes[a][0]), :],
                                       stage[a], load_sem.at[a]) for a in range(n_late)]
        for cp in loads:
            cp.start()
        g_c.finish([c_ref], [cg_s], sc)
        cin_ref[...] = jnp.zeros_like(cin_ref)
        for dev in range(8):
            cin_ref[2 * dev:2 * dev + 2, :] = cg_s[dev, 0:2, :]
        cin_ref[16:17, :] = cc_ref[...]
        ms_s[...] = _nn(_silu(cin_ref[...]), w_ref[...]) + b_ref[...]
        g_m.start([ms_s], [mg_ref], sm)
        g_w.start([hw_ref], [gw_ref], sw)
        for a, cp in enumerate(loads):
            cp.wait()
            half_refs[a][...] = stage[a][...].astype(BF16)
        cos_ref[...], sin_ref[...] = _rope_tables()
        stores = []
        for pair in range(NPAIR):
            if pair >= 2:
                stores[pair - 2].wait()
            _na_bias_pair(flat_ref.at[pair], bias_s.at[pair % 2])
            stores.append(pltpu.make_async_copy(bias_s.at[pair % 2], bias_ref.at[pair], bias_sem.at[pair % 2]))
            stores[pair].start()
        for cp in stores[-2:]:
            cp.wait()
        g_m.finish([ms_s], [mg_ref], sm)
        g_w.finish([hw_ref], [gw_ref], sw)

    vmem = pl.BlockSpec(memory_space=pltpu.VMEM)
    hbm = pl.BlockSpec(memory_space=pl.ANY)
    return pl.pallas_call(
        body, name="prologue", in_specs=[vmem, vmem, vmem, vmem, vmem, hbm] + [hbm] * n_late,
        out_specs=[vmem, vmem, hbm, hbm, vmem, vmem] + [vmem] * n_late,
        out_shape=[shape((32, D), F32), shape((8, 32, 1536), F32)] + g_w.out_shape
        + [shape((NPAIR,) + NA_BIAS_SHAPE, F32)] + [shape((SEQ, RD), F32)] * 2 + [shape(s, BF16) for s in half_shapes],
        scratch_shapes=[pltpu.VMEM((8, 8, D), F32), pltpu.VMEM((32, 1536), F32), pltpu.VMEM((2,) + NA_BIAS_SHAPE, F32)]
        + [pltpu.VMEM(s, F32) for s in half_shapes]
        + [pltpu.SemaphoreType.DMA((n_late,)), pltpu.SemaphoreType.DMA((2,))]
        + g_w.scratch + g_c.scratch + g_m.scratch,
        compiler_params=_cp(),
    )(c_rows, c_ctx_row, w_ada, b_shard, rpb_flat, half_w_in, *late_shards)


def ada_grads(cin, gb, gc, w_ada):
    def body(c_ref, gb_ref, gc_ref, w_ref, gw_ref, pc_ref):
        ctx_tot = jnp.sum(gc_ref[...], axis=0, keepdims=True)
        rows = lax.broadcasted_iota(jnp.int32, (16, 512), 0)
        dm = jnp.concatenate([gb_ref[...], jnp.where(rows == 0, ctx_tot, 0.0)], axis=0)
        gw_ref[...] = _tn(_silu(c_ref[...]), dm)
        rows8 = lax.broadcasted_iota(jnp.int32, (8, 512), 0)
        part = _nt(jnp.where(rows8 == 0, ctx_tot, 0.0), w_ref[...])

        @pl.when(pl.program_id(0) == 0)
        def _():
            pc_ref[...] = jnp.zeros_like(pc_ref)

        pc_ref[...] += part

    return pl.pallas_call(
        body, name="ada_grads", grid=(3,),
        in_specs=[pl.BlockSpec((32, D), lambda j: (0, 0)), pl.BlockSpec((16, 512), lambda j: (0, j)),
                  pl.BlockSpec((8, 512), lambda j: (0, j)), pl.BlockSpec((D, 512), lambda j: (0, j))],
        out_specs=[pl.BlockSpec((D, 512), lambda j: (0, j)), pl.BlockSpec((8, D), lambda j: (0, 0))],
        out_shape=[jax.ShapeDtypeStruct((D, 1536), F32), jax.ShapeDtypeStruct((8, D), F32)],
    )(cin, gb, gc, w_ada)


SMALL_SUM_ROWS = 15


def small_update(gsm, gbf, gcf, pcg, params):
    n = len(params)

    def body(*refs):
        gsm_ref, gbf_ref, gcf_ref, pcg_ref = refs[:4]
        wmv, outs, loss_out = refs[4:4 + 3 * n], refs[4 + 3 * n:4 + 7 * n], refs[-1]
        acc = gsm_ref[0]
        for dev in range(1, 8):
            acc = acc + gsm_ref[dev]
        c_ctx = wmv[0][...]
        sg = jax.nn.sigmoid(c_ctx)
        dsilu = pcg_ref[0:1, :] + pcg_ref[2:3, :] + pcg_ref[4:5, :] + pcg_ref[6:7, :]
        lane = lax.broadcasted_iota(jnp.int32, (1, D), 1)
        last = acc[14:15, :]
        grads = [
            dsilu * (sg * (1.0 + c_ctx * (1.0 - sg))),
            jnp.sum(gbf_ref[...], axis=0, keepdims=True) + jnp.sum(gcf_ref[...], axis=0, keepdims=True),
            acc[0:1, :] + acc[1:2, :], acc[2:3, :], acc[3:4, :], acc[4:5, :],
            acc[5:6, 0:512], acc[6:14, :], jnp.where(lane < 8, last, 0.0),
        ]
        loss_out[...] = jnp.broadcast_to(jnp.sum(jnp.where(lane == 8, last, 0.0), axis=1, keepdims=True), (8, 128))
        for i, g in enumerate(grads):
            d, m2, v2 = _adamw_math(wmv[3 * i][...], g, wmv[3 * i + 1][...], wmv[3 * i + 2][...])
            outs[4 * i][...] = g
            outs[4 * i + 1][...] = d
            outs[4 * i + 2][...] = m2
            outs[4 * i + 3][...] = v2

    flat = [a for wmv in params for a in wmv]
    out_shape = [jax.ShapeDtypeStruct(w.shape, F32) for w, _, _ in params for _ in range(4)]
    return pl.pallas_call(
        body, name="small_update", out_shape=out_shape + [jax.ShapeDtypeStruct((8, 128), F32)],
    )(gsm, gbf, gcf, pcg, *flat)


def _pad_row(v, rows):
    flat = v.reshape(-1)
    return jnp.pad(flat, (0, rows * D - flat.shape[0])).reshape(rows, D)


def local_step(x, ctx, tgt, mod3, rope, bias, g_pre_mix, g_post_mix, g_pre_mlp, g_post_mlp, ret_decay, ret_gn,
               wperm, late_weights, early_grads):
    nb = x.shape[0]
    tokens = nb * SEQ
    cos, sin = rope
    rd = ret_decay.T.reshape(RH, 2, 1)
    gn = ret_gn.reshape(RH, 1, RD)
    h, pret, pna = premix_proj(x, mod3, g_pre_mix, wperm, False, "premix_proj")
    hc, pretc, pnac = premix_proj(ctx, mod3, g_pre_mix, wperm, True, "premix_proj_ctx")
    o_all, mixin, gw_out = retention_fwd(pret, pretc, rd, gn, cos, sin, late_weights(0))
    mixin, gw1, gw2 = na_fwd(pna, pnac, bias, mixin, late_weights(1))
    dx_tail, dmix, h2, du, act, dm, dmixin, dmod_t, dg_t, loss_t = tail_fwd_bwd(
        x, mixin, tgt, mod3, g_post_mix, g_pre_mlp, g_post_mlp, gw_out.reshape(D, D), gw1.reshape(4, D, D),
        gw2.reshape(DFF, D))
    dw_out = weight_grad([(mixin.reshape(tokens, D), dmix.reshape(tokens, D))], "grad_w_out", BF16)
    dw1 = weight_grad([(h2.reshape(tokens, D), du.reshape(tokens, DFF))], "grad_w_mlp1", BF16, col_blocks=True)
    dw2 = weight_grad([(act.reshape(tokens, DFF), dm.reshape(tokens, D))], "grad_w_mlp2", BF16)
    dproj, dprojc, drd, dgn, *landed = retention_bwd(pret, pretc, o_all, dmixin, rd, gn, cos, sin,
                                                     early_grads[0](dw_out, dw1, dw2))
    dproj, dprojc, dpat, *early = na_bwd(pna, pnac, bias, dmixin, dproj, dprojc, early_grads[1](landed))
    dw_in = weight_grad([(h.reshape(tokens, D), dproj.reshape(tokens, IN_W)),
                         (hc.reshape(nb * LC, D), dprojc.reshape(nb * LC, IN_W))], "grad_w_in", tn=IN_W // 2, tk=1024)
    dmod_c, dg_c, *late = premix_bwd(ctx, mod3, g_pre_mix, wperm, dprojc, None, early_grads[2](dw_in), "premix_bwd_ctx")
    grad_x, dmod_a, dg_a, *late = premix_bwd(x, mod3, g_pre_mix, wperm, dproj, dx_tail, early_grads[3](late),
                                             "premix_bwd")
    dmod = jnp.concatenate([jnp.concatenate([dmod_a[:, 0:2], dmod_t[:, 2:6]], axis=1), dmod_c], axis=0)
    last = jnp.pad(jnp.concatenate([drd[:, :, 0].T.reshape(8), loss_t[0, 0:1]]), (0, D - 9)).reshape(1, D)
    small = jnp.concatenate([dg_a[0:1], dg_c[0:1], dg_t[0:3], _pad_row(dgn, 1), dpat.reshape(8, D), last], axis=0)
    return grad_x, late, early, dmod, small


def kernel(x, c, ctx, c_ctx, w_ada, b_ada, g_pre_mix, g_post_mix, g_pre_mlp, g_post_mlp, w_in, ret_decay, ret_gn, na_rpb, w_out, w_mlp1, w_mlp2, loss_target, m_c_ctx, m_w_ada, m_b_ada, m_g_pre_mix, m_g_post_mix, m_g_pre_mlp, m_g_post_mlp, m_w_in, m_ret_decay, m_ret_gn, m_na_rpb, m_w_out, m_w_mlp1, m_w_mlp2, v_c_ctx, v_w_ada, v_b_ada, v_g_pre_mix, v_g_post_mix, v_g_pre_mlp, v_g_post_mlp, v_w_in, v_ret_decay, v_ret_gn, v_na_rpb, v_w_out, v_w_mlp1, v_w_mlp2):
    px, py, pc = _place()
    dev = 4 * px + 2 * py + pc
    chip = 2 * px + py

    half_w_in = lax.dynamic_slice_in_dim(w_in[0], pc * (D // 2), D // 2, 0).astype(BF16)
    cin, mg, gw_in, bias, cos, sin, *late_halves = prologue(
        jnp.pad(c, ((0, 6), (0, 0))), c_ctx[None], w_ada[0], lax.dynamic_slice_in_dim(b_ada, chip * 1536, 1536, 1),
        _rpb_flat(na_rpb[0]), half_w_in, [w_out[0], w_mlp1[0], w_mlp2[0]])
    halves = [half_w_in] + late_halves
    wperm = unpack_w_in(gw_in.reshape(4, D, 896))
    mod_all = jnp.concatenate([mg[0], mg[2], mg[4], mg[6]], axis=1)
    mod3 = (jnp.pad(lax.dynamic_slice_in_dim(mod_all, 2 * dev, 2, 0), ((0, 1), (0, 0)))
            + jnp.pad(mod_all[16:17], ((2, 0), (0, 0)))).reshape(3, 6, D)

    place = jnp.stack([pc, chip]).astype(jnp.int32)

    early_names = ["w_out", "w_mlp1", "w_mlp2"]
    early_g8, early_partial = [], []

    def early_a(dw_out, dw1, dw2):
        early_g8[:] = [dw_out.reshape(8, 128, D), dw1.reshape(8, 512, D), dw2.reshape(8, 512, D)]
        return siblings4(early_g8)

    def early_b(landed):
        early_partial[:] = chip_partial(place, early_g8, landed, "rs_chip_sum_early")
        return chips3(early_partial)

    late_partial = []

    late_g8 = []

    def late_c(dw_in):
        late_g8[:] = [pack_w_in(dw_in).reshape(8, 512, 896)]
        return siblings4(late_g8)

    def late_d(landed):
        late_partial[:] = chip_partial(place, late_g8, landed, "rs_chip_sum_w_in")
        return chips3(late_partial)

    grad_x, (landed3_in,), early_landed, dmod, small = local_step(
        x, ctx, loss_target, mod3, (cos, sin), bias, g_pre_mix, g_post_mix, g_pre_mlp, g_post_mlp, ret_decay[0], ret_gn,
        wperm, lambda k: gather8(halves[1:2] if k == 0 else halves[2:4]), (early_a, early_b, late_c, late_d))
    early_mine = shard_sum(place, early_partial, early_landed, "rs_shard_sum_early")

    pay = jnp.concatenate([dmod.reshape(18, D), small, jnp.zeros((40 - 18 - SMALL_SUM_ROWS, D), F32)], axis=0)
    *early_theirs, gs = run_hosted(both(siblings(early_mine), gather8([pay])), "rs_halves_early_gather_small")
    gbf = gs[:, 0:12].reshape(16, 6 * D)
    gcf = gs[:, 12:18].reshape(8, 6 * D)
    gw_ada, pc_part = ada_grads(cin, lax.dynamic_slice_in_dim(gbf, chip * 1536, 1536, 1),
                                lax.dynamic_slice_in_dim(gcf, chip * 1536, 1536, 1), w_ada[0])
    (mine_in,) = shard_sum(place, late_partial, [landed3_in], "rs_shard_sum_w_in")
    theirs_in, pcg = run_hosted(both(siblings([mine_in]), gather8([pc_part])), "rs_halves_w_in_gather_c_ctx")

    grouped = adamw_group(
        place,
        [(w_mlp1[0], early_mine[1], early_theirs[1], m_w_mlp1[0], v_w_mlp1[0]),
         (w_mlp2[0], early_mine[2], early_theirs[2], m_w_mlp2[0], v_w_mlp2[0])],
        [(w_ada[0], gw_ada, m_w_ada[0], v_w_ada[0])], no_exchange(), "adamw_group")
    d_ada, m_ada, v_ada = grouped[8:11]
    big = [
        [r[None] for r in adamw_halves(place, w_in[0], mine_in, theirs_in, m_w_in[0], v_w_in[0], "adamw_w_in")],
        [r[None] for r in adamw_halves(place, w_out[0], early_mine[0], early_theirs[0], m_w_out[0], v_w_out[0],
                                       "adamw_w_out")],
        [r[None] for r in grouped[0:4]], [r[None] for r in grouped[4:8]],
    ]

    def rpb_rows(t):
        return _rpb_flat(t[0]).reshape(8, D)

    def decay_row(t):
        return jnp.pad(t.reshape(1, 8), ((0, 0), (0, D - 8)))

    views = [lambda t: t.reshape(1, D), lambda t: t, lambda t: t, lambda t: t, lambda t: t, lambda t: t, lambda t: t,
             rpb_rows, decay_row]
    back = [lambda t: t.reshape(D), lambda t: t, lambda t: t, lambda t: t, lambda t: t, lambda t: t, lambda t: t,
            lambda t: _rpb_flat_t(t)[None], lambda t: t[:, 0:8].reshape(1, 2, 4)]
    small_w = (c_ctx, b_ada, g_pre_mix, g_post_mix, g_pre_mlp, g_post_mlp, ret_gn, na_rpb, ret_decay)
    small_m = (m_c_ctx, m_b_ada, m_g_pre_mix, m_g_post_mix, m_g_pre_mlp, m_g_post_mlp, m_ret_gn, m_na_rpb, m_ret_decay)
    small_v = (v_c_ctx, v_b_ada, v_g_pre_mix, v_g_post_mix, v_g_pre_mlp, v_g_post_mlp, v_ret_gn, v_na_rpb, v_ret_decay)
    *res, loss8 = small_update(gs[:, 18:18 + SMALL_SUM_ROWS], gbf, gcf, pcg[:, 0],
                               [(f(w), f(m), f(v)) for f, w, m, v in zip(views, small_w, small_m, small_v)])

    def leaves(ada, idx):
        s_c, s_b, s_g1, s_g2, s_g3, s_g4, s_gn, s_rpb, s_rd = [back[i](res[4 * i + idx]) for i in range(9)]
        return [s_c, ada[None], s_b, s_g1, s_g2, s_g3, s_g4, big[0][idx], s_rd, s_gn, s_rpb,
                big[1][idx], big[2][idx], big[3][idx]]

    return (loss8[0, 0], grad_x, *leaves(gw_ada, 0), *leaves(d_ada, 1), *leaves(m_ada, 2), *leaves(v_ada, 3))
```

```python
import functools
import math

import jax
import jax.numpy as jnp
from jax import lax
from jax.experimental import pallas as pl
from jax.experimental.pallas import tpu as pltpu

F32, BF16 = jnp.float32, jnp.bfloat16
D = 1024
SEQ = 2048
LC = 256
GW = 64
RH, RD, CH = 4, 128, 128
NPAIR = 4
IN_W = 3584
RET_W = 2048
DFF = 4096
EPS = 1e-6
NEG = -1e30
TN = 256
NCH = SEQ // CH
LR, B1, B2, AEPS, WD, STEP = 0.001, 0.9, 0.999, 1e-08, 0.01, 10
MESH = pl.DeviceIdType.MESH
VMEM_LIMIT = 56 * 1024 * 1024


def _cp(sem=None):
    return pltpu.CompilerParams(dimension_semantics=sem, vmem_limit_bytes=VMEM_LIMIT)


def _nn(a, b):
    return jnp.dot(a.astype(BF16), b.astype(BF16), preferred_element_type=F32)


def _nt(a, b):
    return lax.dot_general(a.astype(BF16), b.astype(BF16), (((1,), (1,)), ((), ())), preferred_element_type=F32)


def _tn(a, b):
    return lax.dot_general(a.astype(BF16), b.astype(BF16), (((0,), (0,)), ((), ())), preferred_element_type=F32)


@jax.custom_vjp
def mm_tn(a, b):
    return _tn(a, b)


mm_tn.defvjp(lambda a, b: (_tn(a, b), (a, b)), lambda r, g: (_nt(r[1], g), _nn(r[0], g)))


def _rms(x):
    return x * lax.rsqrt(jnp.mean(x * x, axis=-1, keepdims=True) + EPS)


def _rms_mod(x, g, sc, sh):
    return (_rms(x) * g) * (1.0 + sc) + sh


def _post_mix(x, mix, gt1, sc2, sh2, g_post_mix, g_pre_mlp):
    x1 = x + gt1 * (_rms(mix) * g_post_mix)
    return x1, _rms_mod(x1, g_pre_mlp, sc2, sh2)


def _head_loss(x1, m, gt2, g_post_mlp, tgt):
    err = x1 + gt2 * (_rms(m) * g_post_mlp) - tgt
    return 0.5 * jnp.sum(jnp.mean(err * err, axis=-1, keepdims=True), axis=0, keepdims=True)


def _ln_gate(o, g, w):
    mu = jnp.mean(o, axis=-1, keepdims=True)
    var = jnp.mean(jnp.square(o - mu), axis=-1, keepdims=True)
    y = (o - mu) * lax.rsqrt(var + EPS)
    return (y * w) * (g * jax.nn.sigmoid(g))


def _pair_order(x):
    lane = lax.broadcasted_iota(jnp.int32, x.shape, 1)
    return jnp.where((lane >= 32) & (lane < 64), pltpu.roll(x, 96, 1),
                     jnp.where((lane >= 64) & (lane < 96), pltpu.roll(x, 32, 1), x))


def _rope(x, cos, sin):
    return x * cos + pltpu.roll(x, 64, 1) * sin


def _rope_t(g, cos, sin):
    return g * cos + pltpu.roll(g * sin, 64, 1)


def _rope_tables():
    tok = lax.broadcasted_iota(jnp.int32, (SEQ, RD), 0)
    lane = lax.broadcasted_iota(jnp.int32, (SEQ, RD), 1)
    pos = jnp.where((lane & 32) == 0, tok >> 6, tok & (GW - 1)).astype(F32)
    ang = pos * jnp.exp((lane & 31).astype(F32) * (-math.log(10000.0) / 32))
    return jnp.cos(ang), jnp.where(lane < 64, -jnp.sin(ang), jnp.sin(ang))


def _chunk_loop(n, body, init, k=4):
    def several(t, carry):
        for i in range(k):
            carry = body(k * t + i, carry)
        return carry

    return lax.fori_loop(0, n // k, several, init)


def _fiota(shape, dim):
    return lax.broadcasted_iota(jnp.int32, shape, dim).astype(F32)


def _ret_state(k, v, s, lg, reverse):
    pos = _fiota((CH, 1), 0)
    b_exp = pos if reverse else (CH - 1.0 - pos)
    return jnp.exp(lg * CH) * s + mm_tn(k * jnp.exp(lg * b_exp), v)


class _Decays:
    def __init__(self, lgs):
        i, j, pos = _fiota((CH, CH), 0), _fiota((CH, CH), 1), _fiota((CH, 1), 0)
        diffs = (i - j, j - i)
        keep = (diffs[0] >= 0, diffs[1] > 0)
        mats = [jnp.where(m, jnp.exp(lg * jnp.where(m, d, 0.0)), 0.0) for lg, d, m in zip(lgs, diffs, keep)]
        self.mask = mats[0] + mats[1]
        self.dmask = [mats[0] * diffs[0], mats[1] * diffs[1]]
        a_exp, b_exp = (pos + 1.0, CH - pos), (CH - 1.0 - pos, pos)
        self.a = [jnp.exp(lg * e) for lg, e in zip(lgs, a_exp)]
        self.b = [jnp.exp(lg * e) for lg, e in zip(lgs, b_exp)]
        self.da = [a * e for a, e in zip(self.a, a_exp)]
        self.db = [b * e for b, e in zip(self.b, b_exp)]
        self.g = [jnp.exp(lg * CH) for lg in lgs]


def _both(x, w):
    return jnp.concatenate([x * w[0], x * w[1]], axis=1)


def _total(x):
    return jnp.sum(jnp.sum(x, axis=1, keepdims=True), axis=0, keepdims=True)


def _state_pass(dec, init, k_s, v_of, st_s):
    def step(t, carry):
        out = []
        for d, s in enumerate(carry):
            n = (NCH - 1 - t) if d else t
            sl = pl.ds(pl.multiple_of(n * CH, CH), CH)
            st_s[n, d * RD:(d + 1) * RD, :] = s
            out.append(dec.g[d] * s + _tn(k_s[sl, :] * dec.b[d], v_of(sl)))
        return tuple(out)

    _chunk_loop(NCH, step, tuple(init))


def premix_proj(xin, mod3, g_pre, wperm, is_ctx, name):
    nb, length, _ = xin.shape
    tn = min(2 * TN, length)

    def body(x_ref, mod_ref, g_ref, w_ref, h_ref, pret_ref, pna_ref):
        h = _rms_mod(x_ref[...], g_ref[...], mod_ref[1:2, :], mod_ref[0:1, :])
        hb = h.astype(BF16)
        h_ref[...] = hb
        pret_ref[...] = jnp.dot(hb, w_ref[:, :RET_W], preferred_element_type=F32)
        pna_ref[...] = jnp.dot(hb, w_ref[:, RET_W:], preferred_element_type=F32).astype(BF16)

    return pl.pallas_call(
        body, name=name, grid=(nb, length // tn),
        in_specs=[
            pl.BlockSpec((None, tn, D), lambda b, t: (b, t, 0)),
            pl.BlockSpec((None, 6, D), (lambda b, t: (2, 0, 0)) if is_ctx else (lambda b, t: (b, 0, 0))),
            pl.BlockSpec((1, D), lambda b, t: (0, 0)),
            pl.BlockSpec((D, IN_W), lambda b, t: (0, 0), pipeline_mode=pl.Buffered(1)),
        ],
        out_specs=[
            pl.BlockSpec((None, tn, D), lambda b, t: (b, t, 0)),
            pl.BlockSpec((None, tn, RET_W), lambda b, t: (b, t, 0)),
            pl.BlockSpec((None, tn, IN_W - RET_W), lambda b, t: (b, t, 0)),
        ],
        out_shape=[
            jax.ShapeDtypeStruct((nb, length, D), BF16),
            jax.ShapeDtypeStruct((nb, length, RET_W), F32),
            jax.ShapeDtypeStruct((nb, length, IN_W - RET_W), BF16),
        ],
        compiler_params=_cp(("arbitrary", "arbitrary")),
    )(xin, mod3, g_pre, wperm)


def premix_bwd(xin, mod3, g_pre, wperm, dproj, dx_tail, hosted, name):
    nb, length, _ = xin.shape
    tn = min(TN, length)
    is_ctx = dx_tail is None

    def body(*refs):
        own_in, h_in, own_out, h_out, _, h_sems = hosted.split(refs, 5 if is_ctx else 6, 2 if is_ctx else 3)
        if is_ctx:
            (x_ref, mod_ref, g_ref, w_ref, dp_ref), (dmod_ref, dg_ref) = own_in, own_out
        else:
            (x_ref, mod_ref, g_ref, w_ref, dp_ref, dxt_ref), (dx_ref, dmod_ref, dg_ref) = own_in, own_out
        b, t = pl.program_id(0), pl.program_id(1)
        grid_step = b * (length // tn) + t

        @pl.when(grid_step == 0)
        def _():
            hosted.start(h_in, h_out, h_sems)

        @pl.when(grid_step == nb * (length // tn) - 1)
        def _():
            hosted.finish(h_in, h_out, h_sems)

        dh = lax.dot_general(dp_ref[...], w_ref[...], (((1,), (1,)), ((), ())), preferred_element_type=F32)
        _, vjp = jax.vjp(_rms_mod, x_ref[...], g_ref[...], mod_ref[1:2, :], mod_ref[0:1, :])
        dx, dg, dsc, dsh = vjp(dh)
        if not is_ctx:
            dx_ref[...] = dx + dxt_ref[...]

        @pl.when((t == 0) & ((b == 0) if is_ctx else True))
        def _():
            dmod_ref[...] = jnp.zeros_like(dmod_ref)

        @pl.when((t == 0) & (b == 0))
        def _():
            dg_ref[...] = jnp.zeros_like(dg_ref)

        dmod_ref[0:1, :] += dsh
        dmod_ref[1:2, :] += dsc
        dg_ref[0:1, :] += dg

    tok = lambda b, t: (b, t, 0)
    in_specs = [
        pl.BlockSpec((None, tn, D), tok),
        pl.BlockSpec((None, 6, D), (lambda b, t: (2, 0, 0)) if is_ctx else (lambda b, t: (b, 0, 0))),
        pl.BlockSpec((1, D), lambda b, t: (0, 0)),
        pl.BlockSpec((D, IN_W), lambda b, t: (0, 0), pipeline_mode=pl.Buffered(1)),
        pl.BlockSpec((None, tn, IN_W), tok),
    ]
    args = [xin, mod3, g_pre, wperm, dproj]
    out_specs = [
        pl.BlockSpec((None, 6, D), (lambda b, t: (0, 0, 0)) if is_ctx else (lambda b, t: (b, 0, 0))),
        pl.BlockSpec((8, D), lambda b, t: (0, 0)),
    ]
    out_shape = [jax.ShapeDtypeStruct((1 if is_ctx else nb, 6, D), F32), jax.ShapeDtypeStruct((8, D), F32)]
    if not is_ctx:
        in_specs.append(pl.BlockSpec((None, tn, D), tok))
        args.append(dx_tail)
        out_specs.insert(0, pl.BlockSpec((None, tn, D), tok))
        out_shape.insert(0, jax.ShapeDtypeStruct((nb, length, D), F32))
    h_in_specs, h_out_specs = hosted.specs()
    return pl.pallas_call(
        body, name=name, grid=(nb, length // tn), in_specs=in_specs + h_in_specs, out_specs=out_specs + h_out_specs,
        out_shape=out_shape + hosted.out_shape, scratch_shapes=hosted.scratch,
        compiler_params=_cp(("arbitrary", "arbitrary")),
    )(*args, *hosted.args)


def _ret_specs(order):
    def im(f):
        return lambda *g: f(*order(*g))
    return dict(
        pret=pl.BlockSpec((None, SEQ, 512), im(lambda b, h: (b, 0, h))),
        pretc=pl.BlockSpec((None, LC, 512), im(lambda b, h: (b, 0, h))),
        rd=pl.BlockSpec((None, 2, 1), im(lambda b, h: (h, 0, 0))),
        gn=pl.BlockSpec((None, 1, RD), im(lambda b, h: (h, 0, 0))),
        tab=pl.BlockSpec((SEQ, RD), im(lambda b, h: (0, 0))),
        head=pl.BlockSpec((None, SEQ, RD), im(lambda b, h: (b, 0, h))),
    )


def retention_fwd(pret, pretc, rd, gn, cos, sin, hosted):
    nb = pret.shape[0]
    sp = _ret_specs(lambda b, h: (b, h))

    def body(*refs):
        own_in, h_in, own_out, h_out, own_scr, h_sems = hosted.split(refs, 6, 2)
        p_ref, pc_ref, rd_ref, gn_ref, cos_ref, sin_ref = own_in
        (o_ref, mix_ref), (q_s, k_s, o_s, st_s) = own_out, own_scr
        grid_step = pl.program_id(0) * RH + pl.program_id(1)

        @pl.when(grid_step == 0)
        def _():
            hosted.start(h_in, h_out, h_sems)

        cos_v, sin_v = cos_ref[...], sin_ref[...]
        q_s[...] = _rope(p_ref[:, 0:128], cos_v, sin_v) * (RD ** -0.5)
        k_s[...] = _rope(p_ref[:, 128:256], cos_v, sin_v)
        lgs, init = [], []
        for rev in (False, True):
            lg = jax.nn.log_sigmoid(rd_ref[int(rev):int(rev) + 1, :])
            s = jnp.zeros((RD, RD), F32)
            for n in ((1, 0) if rev else (0, 1)):
                s = _ret_state(pc_ref[n * CH:(n + 1) * CH, 128:256], pc_ref[n * CH:(n + 1) * CH, 256:384], s, lg, rev)
            lgs.append(lg)
            init.append(s)

        dec = _Decays(lgs)
        _state_pass(dec, init, k_s, lambda sl: p_ref[sl, 256:384], st_s)

        def chunk(n, carry):
            sl = pl.ds(pl.multiple_of(n * CH, CH), CH)
            q = q_s[sl, :]
            o_s[sl, :] = (_nn(_nt(q, k_s[sl, :]) * dec.mask, p_ref[sl, 256:384]) + _nn(_both(q, dec.a), st_s[n]))
            return carry

        _chunk_loop(NCH, chunk, 0)
        o = o_s[...]
        o_ref[...] = o
        mix_ref[...] = _ln_gate(o, p_ref[:, 384:512], gn_ref[...]).astype(BF16)

        @pl.when(grid_step == nb * RH - 1)
        def _():
            hosted.finish(h_in, h_out, h_sems)

    h_in_specs, h_out_specs = hosted.specs()
    return pl.pallas_call(
        body, name="retention_fwd", grid=(nb, RH),
        in_specs=[sp["pret"], sp["pretc"], sp["rd"], sp["gn"], sp["tab"], sp["tab"]] + h_in_specs,
        out_specs=[sp["head"], sp["head"]] + h_out_specs,
        out_shape=[jax.ShapeDtypeStruct((nb, SEQ, RH * RD), F32), jax.ShapeDtypeStruct((nb, SEQ, D), BF16)]
        + hosted.out_shape,
        scratch_shapes=[pltpu.VMEM((SEQ, RD), F32)] * 3 + [pltpu.VMEM((NCH, 2 * RD, RD), F32)] + hosted.scratch,
        compiler_params=_cp(("arbitrary", "arbitrary")),
    )(pret, pretc, rd, gn, cos, sin, *hosted.args)


def retention_bwd(pret, pretc, o_all, dmixin, rd, gn, cos, sin, hosted):
    nb = pret.shape[0]
    sp = _ret_specs(lambda h, b: (b, h))

    def body(*refs):
        own_in, h_in, own_out, h_out, own_scr, h_sems = hosted.split(refs, 8, 4)
        p_ref, pc_ref, o_ref, dmix_ref, rd_ref, gn_ref, cos_ref, sin_ref = own_in
        dp_ref, dpc_ref, drd_ref, dgn_ref = own_out
        q_s, k_s, do_s, dq_s, dk_s, dv_s, st_s, gst_s = own_scr
        b = pl.program_id(1)
        grid_step = pl.program_id(0) * nb + b

        @pl.when(grid_step == 0)
        def _():
            hosted.start(h_in, h_out, h_sems)

        cos_v, sin_v = cos_ref[...], sin_ref[...]
        q_s[...] = _rope(p_ref[:, 0:128], cos_v, sin_v) * (RD ** -0.5)
        k_s[...] = _rope(p_ref[:, 128:256], cos_v, sin_v)
        _, gate_vjp = jax.vjp(_ln_gate, o_ref[...], p_ref[:, 384:512], gn_ref[...])
        do, dg, dgn = gate_vjp(dmix_ref[...].astype(F32))
        do_s[...] = do
        dp_ref[:, 384:512] = dg.astype(BF16)

        @pl.when(b == 0)
        def _():
            drd_ref[...] = jnp.zeros_like(drd_ref)
            dgn_ref[...] = jnp.zeros_like(dgn_ref)

        dgn_ref[...] += dgn
        kcs = [pc_ref[n * CH:(n + 1) * CH, 128:256] for n in (0, 1)]
        vcs = [pc_ref[n * CH:(n + 1) * CH, 256:384] for n in (0, 1)]
        dirs = []
        init = []
        for rev in (False, True):
            rdv = rd_ref[int(rev):int(rev) + 1, :]
            lg = jax.nn.log_sigmoid(rdv)
            order_c = (1, 0) if rev else (0, 1)
            s = jnp.zeros((RD, RD), F32)
            ctx_states = []
            for n in order_c:
                ctx_states.append(s)
                s = _ret_state(kcs[n], vcs[n], s, lg, rev)
            dirs.append((rev, order_c, lg, rdv, ctx_states))
            init.append(s)
        dec = _Decays([lg for _, _, lg, _, _ in dirs])

        def v_of(sl):
            return p_ref[sl, 256:384]

        _state_pass(dec, init, k_s, v_of, st_s)
        zeros = jnp.zeros((CH, RD), F32)

        def scores_back(n, carry):
            dmask_sum, da_f, da_b = carry
            sl = pl.ds(pl.multiple_of(n * CH, CH), CH)
            q, k, v, do = q_s[sl, :], k_s[sl, :], v_of(sl), do_s[sl, :]
            scores = _nt(q, k)
            d_att = _nt(do, v)
            d_scores = d_att * dec.mask
            d_qa = _nt(do, st_s[n])
            d_qf, d_qb = d_qa[:, 0:RD], d_qa[:, RD:2 * RD]
            dq_s[sl, :] = _nn(d_scores, k) + d_qf * dec.a[0] + d_qb * dec.a[1]
            dk_s[sl, :] = _tn(d_scores, q)
            dv_s[sl, :] = _tn(scores * dec.mask, do)
            gst_s[n] = _tn(_both(q, dec.a), do)
            return dmask_sum + d_att * scores, da_f + d_qf * q, da_b + d_qb * q

        dmask_sum, da_f, da_b = _chunk_loop(NCH, scores_back, (zeros, zeros, zeros))

        def state_back(t, carry):
            out = []
            for d, r in enumerate(carry):
                n = t if d else (NCH - 1 - t)
                rows = slice(d * RD, (d + 1) * RD)
                own = gst_s[n, rows, :]
                gst_s[n, rows, :] = r
                out.append(own + dec.g[d] * r)
            return tuple(out)

        d_states = _chunk_loop(NCH, state_back, (zeros, zeros))

        def updates_back(n, carry):
            db_f, db_b, dg_f, dg_b = carry
            sl = pl.ds(pl.multiple_of(n * CH, CH), CH)
            k, r, s = k_s[sl, :], gst_s[n], st_s[n]
            d_kw = _nt(v_of(sl), r)
            d_kf, d_kb = d_kw[:, 0:RD], d_kw[:, RD:2 * RD]
            dk_s[sl, :] += d_kf * dec.b[0] + d_kb * dec.b[1]
            dv_s[sl, :] += _nn(_both(k, dec.b), r)
            return (db_f + d_kf * k, db_b + d_kb * k, dg_f + r[0:RD, :] * s[0:RD, :],
                    dg_b + r[RD:2 * RD, :] * s[RD:2 * RD, :])

        db_dg = _chunk_loop(NCH, updates_back, (zeros, zeros, zeros, zeros))
        dkc = [None, None]
        dvc = [None, None]
        for d, ((rev, order_c, lg, rdv, ctx_states), ds) in enumerate(zip(dirs, d_states)):
            dlg = (_total(dmask_sum * dec.dmask[d]) + _total((da_f, da_b)[d] * dec.da[d])
                   + _total(db_dg[d] * dec.db[d]) + CH * dec.g[d] * _total(db_dg[2 + d]))
            for idx in (1, 0):
                n = order_c[idx]
                _, vjp = jax.vjp(functools.partial(_ret_state, reverse=rev), kcs[n], vcs[n], ctx_states[idx], lg)
                dk_c, dv_c, ds, dl = vjp(ds)
                dlg = dlg + dl
                dkc[n] = dk_c if dkc[n] is None else dkc[n] + dk_c
                dvc[n] = dv_c if dvc[n] is None else dvc[n] + dv_c
            drd_ref[int(rev):int(rev) + 1, :] += dlg * jax.nn.sigmoid(-rdv)
        dp_ref[:, 0:128] = _rope_t(dq_s[...] * (RD ** -0.5), cos_v, sin_v).astype(BF16)
        dp_ref[:, 128:256] = _rope_t(dk_s[...], cos_v, sin_v).astype(BF16)
        dp_ref[:, 256:384] = dv_s[...].astype(BF16)
        zero = jnp.zeros((CH, RD), BF16)
        for n in (0, 1):
            rows = slice(n * CH, (n + 1) * CH)
            dpc_ref[rows, 0:128] = zero
            dpc_ref[rows, 128:256] = dkc[n].astype(BF16)
            dpc_ref[rows, 256:384] = dvc[n].astype(BF16)
            dpc_ref[rows, 384:512] = zero

        @pl.when(grid_step == RH * nb - 1)
        def _():
            hosted.finish(h_in, h_out, h_sems)

    h_in_specs, h_out_specs = hosted.specs()
    return pl.pallas_call(
        body, name="retention_bwd", grid=(RH, nb),
        in_specs=[sp["pret"], sp["pretc"], sp["head"], sp["head"], sp["rd"], sp["gn"], sp["tab"], sp["tab"]]
        + h_in_specs,
        out_specs=[
            pl.BlockSpec((None, SEQ, 512), lambda h, b: (b, 0, h)),
            pl.BlockSpec((None, LC, 512), lambda h, b: (b, 0, h)),
            pl.BlockSpec((None, 2, 1), lambda h, b: (h, 0, 0)),
            pl.BlockSpec((None, 1, RD), lambda h, b: (h, 0, 0)),
        ] + h_out_specs,
        out_shape=[
            jax.ShapeDtypeStruct((nb, SEQ, IN_W), BF16),
            jax.ShapeDtypeStruct((nb, LC, IN_W), BF16),
            jax.ShapeDtypeStruct((RH, 2, 1), F32),
            jax.ShapeDtypeStruct((RH, 1, RD), F32),
        ] + hosted.out_shape,
        scratch_shapes=[pltpu.VMEM((SEQ, RD), F32)] * 6 + [pltpu.VMEM((NCH, 2 * RD, RD), F32)] * 2 + hosted.scratch,
        compiler_params=_cp(("arbitrary", "arbitrary")),
    )(pret, pretc, o_all, dmixin, rd, gn, cos, sin, *hosted.args)


def _rpb_flat(rpb):
    return jnp.pad(rpb, ((0, 0), (0, 1), (0, 33))).reshape(NPAIR, 2, 1, 1024)


def _rpb_flat_t(dflat):
    return dflat.reshape(8, 16, 64)[:, :15, :31]


def _barrel(x, left):
    row = lax.broadcasted_iota(jnp.int32, x.shape, 0)
    n = x.shape[1]
    for bit in range(6):
        s = 1 << bit
        x = jnp.where(((row >> bit) & 1) == 1, pltpu.roll(x, (n - s) if left else s, 1), x)
    return x


NA_TILE_ROWS, NA_BAND_ROWS = 4, 12
NA_Q, NA_K = NA_TILE_ROWS * GW, NA_BAND_ROWS * GW
NA_TILES = SEQ // NA_Q


def _band_start(r0):
    return min(max(r0 - 4, 0), 32 - NA_BAND_ROWS)


def _tile_layout(t):
    rows = range(t * NA_TILE_ROWS, (t + 1) * NA_TILE_ROWS)
    return tuple((r if r < 4 else (r - 24 if r > 28 else 4), min(max(r - 4, 0), 24) - _band_start(rows[0]))
                 for r in rows)


NA_CLASSES = sorted(set(_tile_layout(t) for t in range(NA_TILES)))


def _tile_rows(cls):
    return NA_CLASSES[cls]


def _na_tile(t):
    start = jnp.clip(NA_TILE_ROWS * t - 4, 0, 32 - NA_BAND_ROWS)
    cls = 0
    for tile in range(NA_TILES):
        cls = jnp.where(t == tile, NA_CLASSES.index(_tile_layout(tile)), cls)
    return pl.ds(pl.multiple_of(t * NA_Q, NA_Q), NA_Q), pl.ds(pl.multiple_of(start * GW, NA_Q), NA_K), cls


def _na_probs(qst, kb, kc, bias):
    s_loc = _nt(qst, kb) + bias
    s_ctx = _nt(qst, kc)
    m = jnp.maximum(jnp.max(s_loc, axis=1, keepdims=True), jnp.max(s_ctx, axis=1, keepdims=True))
    e_loc, e_ctx = jnp.exp(s_loc - m), jnp.exp(s_ctx - m)
    den = jnp.sum(e_loc, axis=1, keepdims=True) + jnp.sum(e_ctx, axis=1, keepdims=True)
    return e_loc / den, e_ctx / den


def _stack_heads(t):
    lane = lax.broadcasted_iota(jnp.int32, t.shape, 1)
    zero = jnp.zeros_like(t)
    return jnp.concatenate([jnp.where(lane < 64, t, zero), jnp.where(lane >= 64, t, zero)], axis=0)


def _unstack_heads(t):
    n = t.shape[0] // 2
    lane = lax.broadcasted_iota(jnp.int32, (n, 128), 1)
    return jnp.where(lane < 64, t[:n], t[n:])


NA_BIAS_SHAPE = (len(NA_CLASSES), 2 * NA_Q, NA_K)


def _na_bias_pair(flat_ref, out_ref):
    qc = lax.broadcasted_iota(jnp.int32, (GW, 512), 0)
    kc = lax.broadcasted_iota(jnp.int32, (GW, 512), 1) & 63
    start = jnp.clip(qc - 8, 0, GW - 16)
    window = (kc >= start) & (kc < start + 16)
    fill = jnp.full((GW, NA_K - 512), NEG, F32)
    for hh in (0, 1):
        skew = _barrel(pltpu.roll(jnp.broadcast_to(flat_ref[hh], (GW, 1024)), 1024 - 15, 1), left=False)
        by_class = [jnp.where(window, (skew if rc == 7 else pltpu.roll(skew, (9 + rc) * 64, 1))[:, 0:512], NEG)
                    for rc in range(8)]
        for cls in range(len(NA_CLASSES)):
            for qr, (rc, off) in enumerate(_tile_rows(cls)):
                w = jnp.concatenate([by_class[rc], fill], axis=1)
                rows = slice(hh * NA_Q + qr * GW, hh * NA_Q + (qr + 1) * GW)
                out_ref[cls, rows, :] = pltpu.roll(w, off * GW, 1) if off else w


def na_fwd(pna, pnac, bias, mixin, hosted):
    nb = pna.shape[0]

    def body(*refs):
        (p_ref, pc_ref, bias_ref, _), h_in, (out_ref,), h_out, _, h_sems = hosted.split(refs, 4, 1)
        grid_step = pl.program_id(0) * nb + pl.program_id(1)

        @pl.when(grid_step == 0)
        def _():
            hosted.start(h_in, h_out, h_sems)

        kc, vc = pc_ref[:, 128:256], pc_ref[:, 256:384]

        def tile(t, carry):
            qsl, bsl, cls = _na_tile(t)
            kb, vb = p_ref[bsl, 128:256], p_ref[bsl, 256:384]
            p_loc, p_ctx = _na_probs(_stack_heads(p_ref[qsl, 0:128] * 0.125), kb, kc, bias_ref[cls])
            out_ref[qsl, :] = _unstack_heads(_nn(p_loc, vb) + _nn(p_ctx, vc)).astype(BF16)
            return carry

        lax.fori_loop(0, NA_TILES, tile, 0, unroll=4)

        @pl.when(grid_step == NPAIR * nb - 1)
        def _():
            hosted.finish(h_in, h_out, h_sems)

    h_in_specs, h_out_specs = hosted.specs()
    return pl.pallas_call(
        body, name="na_fwd", grid=(NPAIR, nb),
        in_specs=[
            pl.BlockSpec((None, SEQ, 384), lambda p, b: (b, 0, p)),
            pl.BlockSpec((None, LC, 384), lambda p, b: (b, 0, p)),
            pl.BlockSpec((None, len(NA_CLASSES), 2 * NA_Q, NA_K), lambda p, b: (p, 0, 0, 0)),
            pl.BlockSpec(memory_space=pl.ANY),
        ] + h_in_specs,
        out_specs=[pl.BlockSpec((None, SEQ, 128), lambda p, b: (b, 0, 4 + p))] + h_out_specs,
        out_shape=[jax.ShapeDtypeStruct((nb, SEQ, D), BF16)] + hosted.out_shape,
        input_output_aliases={3: 0},
        scratch_shapes=hosted.scratch,
        compiler_params=_cp(("arbitrary", "arbitrary")),
    )(pna, pnac, bias, mixin, *hosted.args)


def na_bwd(pna, pnac, bias, dmixin, dproj, dprojc, hosted):
    nb = pna.shape[0]

    def body(*refs):
        own_in, h_in, own_out, h_out, own_scr, h_sems = hosted.split(refs, 6, 3)
        p_ref, pc_ref, bias_ref, dmix_ref = own_in[:4]
        dp_ref, dpc_ref, dpat_ref = own_out
        dbias_s, dk_s, dv_s, dkc_s, dvc_s, res_s, resc_s = own_scr
        b, part = pl.program_id(1), pl.program_id(2)
        grid_step = (pl.program_id(0) * nb + b) * 3 + part

        @pl.when(grid_step == 0)
        def _():
            hosted.start(h_in, h_out, h_sems)

        @pl.when(grid_step == NPAIR * nb * 3 - 1)
        def _():
            hosted.finish(h_in, h_out, h_sems)

        @pl.when(part == 0)
        def _():
            @pl.when(b == 0)
            def _():
                dbias_s[...] = jnp.zeros_like(dbias_s)

            dk_s[...] = jnp.zeros_like(dk_s)
            dv_s[...] = jnp.zeros_like(dv_s)
            dkc_s[...] = jnp.zeros_like(dkc_s)
            dvc_s[...] = jnp.zeros_like(dvc_s)
            kc, vc = pc_ref[:, 128:256], pc_ref[:, 256:384]

            def tile(t, carry):
                qsl, bsl, cls = _na_tile(t)
                kb, vb = p_ref[bsl, 128:256], p_ref[bsl, 256:384]
                qst, dost = _stack_heads(p_ref[qsl, 0:128] * 0.125), _stack_heads(dmix_ref[qsl, :])
                p_loc, p_ctx = _na_probs(qst, kb, kc, bias_ref[cls])
                dp_loc, dp_ctx = _nt(dost, vb), _nt(dost, vc)
                delta = (jnp.sum(p_loc * dp_loc, axis=1, keepdims=True)
                         + jnp.sum(p_ctx * dp_ctx, axis=1, keepdims=True))
                ds_loc, ds_ctx = p_loc * (dp_loc - delta), p_ctx * (dp_ctx - delta)
                dbias_s[cls] += ds_loc
                res_s[0, qsl, :] = _unstack_heads((_nn(ds_loc, kb) + _nn(ds_ctx, kc)) * 0.125).astype(BF16)
                dk_s[bsl, :] += _tn(ds_loc, qst)
                dv_s[bsl, :] += _tn(p_loc, dost)
                dkc_s[...] += _tn(ds_ctx, qst)
                dvc_s[...] += _tn(p_ctx, dost)
                return carry

            lax.fori_loop(0, NA_TILES, tile, 0, unroll=2)
            res_s[1] = dk_s[...].astype(BF16)
            res_s[2] = dv_s[...].astype(BF16)
            resc_s[0] = jnp.zeros((LC, 128), BF16)
            resc_s[1] = dkc_s[...].astype(BF16)
            resc_s[2] = dvc_s[...].astype(BF16)

            @pl.when(b == nb - 1)
            def _():
                for hh in (0, 1):
                    by_class = [None] * 8
                    for cls in range(len(NA_CLASSES)):
                        for qr, (rc, off) in enumerate(_tile_rows(cls)):
                            w = dbias_s[cls, hh * NA_Q + qr * GW:hh * NA_Q + (qr + 1) * GW, :]
                            w = (pltpu.roll(w, NA_K - off * GW, 1) if off else w)[:, 0:512]
                            by_class[rc] = w if by_class[rc] is None else by_class[rc] + w
                    skew = jnp.zeros((GW, 1024), F32)
                    for rc in range(8):
                        w = jnp.concatenate([by_class[rc], jnp.zeros((GW, 512), F32)], axis=1)
                        skew = skew + (w if rc == 7 else pltpu.roll(w, (7 - rc) * 64, 1))
                    dpat_ref[hh] = jnp.sum(pltpu.roll(_barrel(skew, left=True), 15, 1), axis=0, keepdims=True)

        dp_ref[...] = res_s[part]
        dpc_ref[...] = resc_s[part]

    h_in_specs, h_out_specs = hosted.specs()
    return pl.pallas_call(
        body, name="na_bwd", grid=(NPAIR, nb, 3),
        in_specs=[
            pl.BlockSpec((None, SEQ, 384), lambda p, b, s: (b, 0, p)),
            pl.BlockSpec((None, LC, 384), lambda p, b, s: (b, 0, p)),
            pl.BlockSpec((None, len(NA_CLASSES), 2 * NA_Q, NA_K), lambda p, b, s: (p, 0, 0, 0)),
            pl.BlockSpec((None, SEQ, 128), lambda p, b, s: (b, 0, 4 + p)),
            pl.BlockSpec(memory_space=pl.ANY),
            pl.BlockSpec(memory_space=pl.ANY),
        ] + h_in_specs,
        out_specs=[
            pl.BlockSpec((None, SEQ, 128), lambda p, b, s: (b, 0, 16 + 3 * p + s)),
            pl.BlockSpec((None, LC, 128), lambda p, b, s: (b, 0, 16 + 3 * p + s)),
            pl.BlockSpec((None, 2, 1, 1024), lambda p, b, s: (p, 0, 0, 0)),
        ] + h_out_specs,
        out_shape=[
            jax.ShapeDtypeStruct((nb, SEQ, IN_W), BF16),
            jax.ShapeDtypeStruct((nb, LC, IN_W), BF16),
            jax.ShapeDtypeStruct((NPAIR, 2, 1, 1024), F32),
        ] + hosted.out_shape,
        input_output_aliases={4: 0, 5: 1},
        scratch_shapes=[
            pltpu.VMEM((len(NA_CLASSES), 2 * NA_Q, NA_K), F32),
            pltpu.VMEM((SEQ, 128), F32), pltpu.VMEM((SEQ, 128), F32),
            pltpu.VMEM((LC, 128), F32), pltpu.VMEM((LC, 128), F32),
            pltpu.VMEM((3, SEQ, 128), BF16), pltpu.VMEM((3, LC, 128), BF16),
        ] + hosted.scratch,
        compiler_params=_cp(("arbitrary", "arbitrary", "arbitrary")),
    )(pna, pnac, bias, dmixin, dproj, dprojc, *hosted.args)


def tail_fwd_bwd(x, mixin, tgt, mod3, g_post_mix, g_pre_mlp, g_post_mlp, wout, w1, w2):
    nb = x.shape[0]

    def body(x_ref, mi_ref, tgt_ref, mod_ref, gpm_ref, gpl_ref, gpo_ref, wo_ref, w1_ref, w2_ref,
             dx_ref, dmix_ref, h2_ref, du_ref, a_ref, dm_ref, dmi_ref, dmod_ref, dg_ref, loss_ref):
        b, t = pl.program_id(0), pl.program_id(1)
        gt1, sh2, sc2, gt2 = mod_ref[2:3, :], mod_ref[3:4, :], mod_ref[4:5, :], mod_ref[5:6, :]
        mix = jnp.dot(mi_ref[...], wo_ref[...], preferred_element_type=F32)
        (x1, h2), vjp_a = jax.vjp(_post_mix, x_ref[...], mix, gt1, sc2, sh2, gpm_ref[...], gpl_ref[...])
        h2b = h2.astype(BF16)
        h2_ref[...] = h2b
        m = jnp.zeros((TN, D), F32)
        relus = []
        for j in range(4):
            cols = slice(j * D, (j + 1) * D)
            r = jnp.maximum(jnp.dot(h2b, w1_ref[j], preferred_element_type=F32), 0.0)
            ab = (r * r).astype(BF16)
            a_ref[:, cols] = ab
            m = m + jnp.dot(ab, w2_ref[cols, :], preferred_element_type=F32)
            relus.append(r)
        loss, vjp_b = jax.vjp(_head_loss, x1, m, gt2, gpo_ref[...], tgt_ref[...])
        dx1, dm, dgt2, dgpo, _ = vjp_b(jnp.ones((1, 1), F32))
        dmb = dm.astype(BF16)
        dm_ref[...] = dmb
        dh2 = jnp.zeros((TN, D), F32)
        for j in range(4):
            cols = slice(j * D, (j + 1) * D)
            da = lax.dot_general(dmb, w2_ref[cols, :], (((1,), (1,)), ((), ())), preferred_element_type=F32)
            dub = (da * (2.0 * relus[j])).astype(BF16)
            du_ref[:, cols] = dub
            dh2 = dh2 + lax.dot_general(dub, w1_ref[j], (((1,), (1,)), ((), ())), preferred_element_type=F32)
        dx, dmix, dgt1, dsc2, dsh2, dgpm, dgpl = vjp_a((dx1, dh2))
        dx_ref[...] = dx
        dmixb = dmix.astype(BF16)
        dmix_ref[...] = dmixb
        dmi_ref[...] = lax.dot_general(dmixb, wo_ref[...], (((1,), (1,)), ((), ())),
                                       preferred_element_type=F32).astype(BF16)

        @pl.when(t == 0)
        def _():
            dmod_ref[...] = jnp.zeros_like(dmod_ref)

        @pl.when((t == 0) & (b == 0))
        def _():
            dg_ref[...] = jnp.zeros_like(dg_ref)
            loss_ref[...] = jnp.zeros_like(loss_ref)

        dmod_ref[2:3, :] += dgt1
        dmod_ref[3:4, :] += dsh2
        dmod_ref[4:5, :] += dsc2
        dmod_ref[5:6, :] += dgt2
        dg_ref[0:1, :] += dgpm
        dg_ref[1:2, :] += dgpl
        dg_ref[2:3, :] += dgpo
        loss_ref[...] += jnp.broadcast_to(loss, loss_ref.shape)

    tok = lambda b, t: (b, t, 0)
    const = lambda b, t: (0, 0)
    vec = pl.BlockSpec((1, D), const)
    return pl.pallas_call(
        body, name="tail_fwd_bwd", grid=(nb, SEQ // TN),
        in_specs=[
            pl.BlockSpec((None, TN, D), tok), pl.BlockSpec((None, TN, D), tok), pl.BlockSpec((None, TN, D), tok),
            pl.BlockSpec((None, 6, D), lambda b, t: (b, 0, 0)), vec, vec, vec,
            pl.BlockSpec((D, D), const, pipeline_mode=pl.Buffered(1)),
            pl.BlockSpec((4, D, D), lambda b, t: (0, 0, 0), pipeline_mode=pl.Buffered(1)),
            pl.BlockSpec((DFF, D), const, pipeline_mode=pl.Buffered(1)),
        ],
        out_specs=[
            pl.BlockSpec((None, TN, D), tok), pl.BlockSpec((None, TN, D), tok), pl.BlockSpec((None, TN, D), tok),
            pl.BlockSpec((None, TN, DFF), tok), pl.BlockSpec((None, TN, DFF), tok), pl.BlockSpec((None, TN, D), tok),
            pl.BlockSpec((None, TN, D), tok),
            pl.BlockSpec((None, 6, D), lambda b, t: (b, 0, 0)),
            pl.BlockSpec((8, D), const), pl.BlockSpec((8, 128), const),
        ],
        out_shape=[
            jax.ShapeDtypeStruct((nb, SEQ, D), F32), jax.ShapeDtypeStruct((nb, SEQ, D), BF16),
            jax.ShapeDtypeStruct((nb, SEQ, D), BF16), jax.ShapeDtypeStruct((nb, SEQ, DFF), BF16),
            jax.ShapeDtypeStruct((nb, SEQ, DFF), BF16), jax.ShapeDtypeStruct((nb, SEQ, D), BF16),
            jax.ShapeDtypeStruct((nb, SEQ, D), BF16),
            jax.ShapeDtypeStruct((nb, 6, D), F32), jax.ShapeDtypeStruct((8, D), F32),
            jax.ShapeDtypeStruct((8, 128), F32),
        ],
        compiler_params=_cp(("arbitrary", "arbitrary")),
    )(x, mixin, tgt, mod3, g_post_mix, g_pre_mlp, g_post_mlp, wout, w1, w2)


def weight_grad(pairs, name, out_dtype=F32, col_blocks=False, tm=1024, tn=1024, tk=2048):
    m, n = pairs[0][0].shape[1], pairs[0][1].shape[1]
    tn = min(tn, n)
    tks = [min(tk, xa.shape[0]) for xa, _ in pairs]
    steps = [xa.shape[0] // t for (xa, _), t in zip(pairs, tks)]
    total = sum(steps)
    offs = [sum(steps[:i]) for i in range(len(pairs))]

    def body(*refs):
        out_ref, acc = refs[2 * len(pairs)], refs[-1]
        k = pl.program_id(2)

        @pl.when(k == 0)
        def _():
            acc[...] = jnp.zeros_like(acc)

        for i in range(len(pairs)):
            @pl.when((k >= offs[i]) & (k < offs[i] + steps[i]))
            def _(i=i):
                acc[...] += lax.dot_general(refs[2 * i][...], refs[2 * i + 1][...], (((0,), (0,)), ((), ())),
                                            preferred_element_type=F32)

        if out_dtype != F32:
            @pl.when(k == total - 1)
            def _():
                out_ref[...] = acc[...].astype(out_dtype)

    in_specs, args = [], []
    for i, (xa, ya) in enumerate(pairs):
        clamp = lambda k, i=i: jnp.clip(k - offs[i], 0, steps[i] - 1)
        in_specs.append(pl.BlockSpec((tks[i], tm), lambda a, c, k, clamp=clamp: (clamp(k), a)))
        in_specs.append(pl.BlockSpec((tks[i], tn), lambda a, c, k, clamp=clamp: (clamp(k), c)))
        args += [xa, ya]
    if col_blocks:
        out_spec = pl.BlockSpec((None, tm, tn), lambda a, c, k: (c, a, 0))
        out_shape = jax.ShapeDtypeStruct((n // tn, m, tn), out_dtype)
    else:
        out_spec = pl.BlockSpec((tm, tn), lambda a, c, k: (a, c))
        out_shape = jax.ShapeDtypeStruct((m, n), out_dtype)
    return pl.pallas_call(
        body, name=name, grid=(m // tm, n // tn, total), in_specs=in_specs, out_specs=out_spec, out_shape=out_shape,
        scratch_shapes=[] if out_dtype == F32 else [pltpu.VMEM((tm, tn), F32)],
        compiler_params=_cp(("arbitrary", "arbitrary", "arbitrary")),
    )(*args)


def _perm_block(t):
    return 4 * (t % 4) + t // 4 if t < 16 else 16 + 3 * ((t - 16) % 4) + (t - 16) // 4


def _is_rope_block(p):
    return p < 16 and p % 4 < 2


def unpack_w_in(blocks):
    def body(i_ref, o_ref):
        for t in range(28):
            p = _perm_block(t)
            blk = i_ref[t // 7, :, (t % 7) * 128:(t % 7 + 1) * 128]
            if _is_rope_block(p):
                blk = _pair_order(blk.astype(F32)).astype(BF16)
            o_ref[:, p * 128:(p + 1) * 128] = blk

    return pl.pallas_call(
        body, name="unpack_w_in", grid=(2,),
        in_specs=[pl.BlockSpec((4, D // 2, 896), lambda i: (0, i, 0))],
        out_specs=pl.BlockSpec((D // 2, IN_W), lambda i: (i, 0)),
        out_shape=jax.ShapeDtypeStruct((D, IN_W), BF16),
    )(blocks)


def pack_w_in(dw):
    def body(i_ref, o_ref):
        for t in range(28):
            p = _perm_block(t)
            blk = i_ref[:, p * 128:(p + 1) * 128]
            if _is_rope_block(p):
                blk = _pair_order(blk)
            o_ref[t // 7, :, (t % 7) * 128:(t % 7 + 1) * 128] = blk.astype(BF16)

    return pl.pallas_call(
        body, name="pack_w_in", grid=(4,),
        in_specs=[pl.BlockSpec((D // 4, IN_W), lambda i: (i, 0))],
        out_specs=pl.BlockSpec((4, D // 4, 896), lambda i: (0, i, 0)),
        out_shape=jax.ShapeDtypeStruct((4, D, 896), BF16),
    )(dw)


def _place():
    return lax.axis_index("x"), lax.axis_index("y"), lax.axis_index("c")


class Hosted:
    def __init__(self, args, out_shape, scratch, start, finish):
        self.args, self.out_shape, self.scratch, self.start, self.finish = args, out_shape, scratch, start, finish

    def specs(self):
        hbm = pl.BlockSpec(memory_space=pl.ANY)
        return [hbm] * len(self.args), [hbm] * len(self.out_shape)

    def split(self, refs, n_in, n_out):
        a, b = len(self.args), len(self.out_shape)
        cuts = [n_in, n_in + a, n_in + a + n_out, n_in + a + n_out + b, len(refs) - len(self.scratch)]
        parts = [refs[i:j] for i, j in zip([0] + cuts, cuts + [len(refs)])]
        return parts[0], parts[1], parts[2], parts[3], parts[4], parts[5]


def no_exchange():
    return Hosted([], [], [], lambda *a: None, lambda *a: None)


def run_hosted(hosted, name):
    def body(*refs):
        _, ins, _, outs, _, sems = hosted.split(refs, 0, 0)
        hosted.start(ins, outs, sems)
        hosted.finish(ins, outs, sems)

    in_specs, out_specs = hosted.specs()
    return pl.pallas_call(body, name=name, in_specs=in_specs, out_specs=out_specs, out_shape=hosted.out_shape,
                          scratch_shapes=hosted.scratch)(*hosted.args)


def gather8(blocks):
    na = len(blocks)

    def copies(ins, outs, sems):
        send_sems, recv_sems, local_sem = sems
        x, y, c = _place()
        me, sibling = (x, y, c), (x, y, 1 - c)
        chips = [(1 - x, y), (x, 1 - y), (1 - x, 1 - y)]

        def slot(o_ref, px, py, pc):
            return o_ref.at[4 * px + 2 * py + pc]

        def copy(a, k, block, to, src=None):
            return pltpu.make_async_remote_copy(
                src_ref=slot(outs[a], *block) if src is None else src, dst_ref=slot(outs[a], *block),
                send_sem=send_sems.at[a, k], recv_sem=recv_sems.at[a, k], device_id=to, device_id_type=MESH)

        mine = [pltpu.make_async_copy(ins[a], slot(outs[a], *me), local_sem.at[a]) for a in range(na)]
        first = []
        for a in range(na):
            first.append(copy(a, 0, me, sibling, src=ins[a]))
            first += [copy(a, 1 + j, me, (*chip, c), src=ins[a]) for j, chip in enumerate(chips)]
        return copy, mine, first, me, sibling, chips, c

    def start(ins, outs, sems):
        _, mine, first, *_ = copies(ins, outs, sems)
        for cp in mine + first:
            cp.start()

    def finish(ins, outs, sems):
        copy, mine, first, me, sibling, chips, c = copies(ins, outs, sems)
        passed = []
        for j, chip in enumerate(chips):
            for a in range(na):
                copy(a, 1 + j, (*chip, c), me).wait_recv()
                cp = copy(a, 4 + j, (*chip, c), sibling)
                cp.start()
                passed.append(cp)
        for a in range(na):
            copy(a, 0, sibling, me).wait_recv()
            for j, chip in enumerate(chips):
                copy(a, 4 + j, (*chip, 1 - c), me).wait_recv()
        for cp in first + passed:
            cp.wait_send()
        for cp in mine:
            cp.wait()

    return Hosted(list(blocks), [jax.ShapeDtypeStruct((8,) + b.shape, b.dtype) for b in blocks],
                  [pltpu.SemaphoreType.DMA((na, 7)), pltpu.SemaphoreType.DMA((na, 7)), pltpu.SemaphoreType.DMA((na,))],
                  start, finish)


def chips3(arrays):
    na = len(arrays)

    def copies(ins, outs, sems):
        send_sems, recv_sems = sems
        x, y, c = _place()
        return [pltpu.make_async_remote_copy(
            src_ref=ins[a].at[2 * px + py], dst_ref=outs[a].at[k], send_sem=send_sems.at[a, k],
            recv_sem=recv_sems.at[a, k], device_id=(px, py, c), device_id_type=MESH)
            for a in range(na) for k, (px, py) in enumerate([(1 - x, y), (x, 1 - y), (1 - x, 1 - y)])]

    def start(ins, outs, sems):
        for cp in copies(ins, outs, sems):
            cp.start()

    def finish(ins, outs, sems):
        for cp in copies(ins, outs, sems):
            cp.wait()

    return Hosted(list(arrays), [jax.ShapeDtypeStruct((3,) + a.shape[1:], a.dtype) for a in arrays],
                  [pltpu.SemaphoreType.DMA((na, 3)), pltpu.SemaphoreType.DMA((na, 3))], start, finish)


def siblings(arrays):
    na = len(arrays)

    def copies(ins, outs, sems):
        send_sems, recv_sems = sems
        x, y, c = _place()
        return [pltpu.make_async_remote_copy(
            src_ref=ins[a], dst_ref=outs[a], send_sem=send_sems.at[a], recv_sem=recv_sems.at[a],
            device_id=(x, y, 1 - c), device_id_type=MESH) for a in range(na)]

    def start(ins, outs, sems):
        for cp in copies(ins, outs, sems):
            cp.start()

    def finish(ins, outs, sems):
        for cp in copies(ins, outs, sems):
            cp.wait()

    return Hosted(list(arrays), [jax.ShapeDtypeStruct(a.shape, a.dtype) for a in arrays],
                  [pltpu.SemaphoreType.DMA((na,)), pltpu.SemaphoreType.DMA((na,))], start, finish)


def both(first, second):
    na, no, ns = len(first.args), len(first.out_shape), len(first.scratch)

    def start(ins, outs, sems):
        first.start(ins[:na], outs[:no], sems[:ns])
        second.start(ins[na:], outs[no:], sems[ns:])

    def finish(ins, outs, sems):
        first.finish(ins[:na], outs[:no], sems[:ns])
        second.finish(ins[na:], outs[no:], sems[ns:])

    return Hosted(first.args + second.args, first.out_shape + second.out_shape, first.scratch + second.scratch,
                  start, finish)


def siblings4(arrays):
    na = len(arrays)

    def copies(ins, outs, sems):
        send_sems, recv_sems = sems
        x, y, c = _place()
        return [pltpu.make_async_remote_copy(
            src_ref=ins[a].at[2 * j + 1 - c], dst_ref=outs[a].at[j],
            send_sem=send_sems.at[a, j], recv_sem=recv_sems.at[a, j],
            device_id=(x, y, 1 - c), device_id_type=MESH) for a in range(na) for j in range(4)]

    def start(ins, outs, sems):
        for cp in copies(ins, outs, sems):
            cp.start()

    def finish(ins, outs, sems):
        for cp in copies(ins, outs, sems):
            cp.wait()

    return Hosted(list(arrays), [jax.ShapeDtypeStruct((4,) + a.shape[1:], a.dtype) for a in arrays],
                  [pltpu.SemaphoreType.DMA((na, 4)), pltpu.SemaphoreType.DMA((na, 4))], start, finish)


def _row_tile(r):
    for cand in (512, 256, 128, 64, 32, 16, 8):
        if r % cand == 0:
            return cand
    return r


def chip_partial(place, g8s, landed4s, name):
    n = len(g8s)

    def body(place_ref, *refs):
        del place_ref
        for g_ref, l_ref, o_ref in zip(refs[:n], refs[n:2 * n], refs[2 * n:]):
            o_ref[...] = (g_ref[...].astype(F32) + l_ref[...].astype(F32)).astype(BF16)

    own = [pl.BlockSpec((None,) + g.shape[1:], lambda j, s: (2 * j + s[0], 0, 0)) for g in g8s]
    plain = [pl.BlockSpec((None,) + g.shape[1:], lambda j, s: (j, 0, 0)) for g in g8s]
    return pl.pallas_call(
        body, name=name,
        grid_spec=pltpu.PrefetchScalarGridSpec(num_scalar_prefetch=1, grid=(4,), in_specs=own + plain, out_specs=plain),
        out_shape=[jax.ShapeDtypeStruct((4,) + g.shape[1:], BF16) for g in g8s],
    )(place, *g8s, *landed4s)


def shard_sum(place, partial4s, landed3s, name):
    n = len(partial4s)

    def body(place_ref, *refs):
        del place_ref
        for p_ref, l_ref, o_ref in zip(refs[:n], refs[n:2 * n], refs[2 * n:]):
            acc = p_ref[...].astype(F32)
            for k in range(3):
                acc = acc + l_ref[k].astype(F32)
            o_ref[...] = acc

    def halves(p, lead):
        r, ccols = p.shape[1:]
        return (lead, r // 2, ccols)

    return pl.pallas_call(
        body, name=name,
        grid_spec=pltpu.PrefetchScalarGridSpec(
            num_scalar_prefetch=1, grid=(2,),
            in_specs=[pl.BlockSpec(halves(p, None), lambda i, s: (s[1], i, 0)) for p in partial4s]
            + [pl.BlockSpec(halves(p, 3), lambda i, s: (0, i, 0)) for p in partial4s],
            out_specs=[pl.BlockSpec(halves(p, None)[1:], lambda i, s: (i, 0)) for p in partial4s]),
        out_shape=[jax.ShapeDtypeStruct(p.shape[1:], F32) for p in partial4s],
    )(place, *partial4s, *landed3s)


def _adamw_math(w, g, m, v):
    m2 = B1 * m + (1.0 - B1) * g
    v2 = B2 * v + (1.0 - B2) * (g * g)
    m_hat = m2 / (1.0 - B1 ** STEP)
    v_hat = v2 / (1.0 - B2 ** STEP)
    return -LR * (m_hat / (jnp.sqrt(v_hat) + AEPS) + WD * w), m2, v2


def adamw_halves(place, w, mine, theirs, m, v, name):
    r, ccols = w.shape
    hr = r // 2
    tr = _row_tile(hr)
    nt = hr // tr

    def body(place_ref, w_ref, a_ref, b_ref, m_ref, v_ref, g_out, d_out, m_out, v_out):
        g = jnp.where(pl.program_id(0) == place_ref[0], a_ref[...], b_ref[...])
        d, m2, v2 = _adamw_math(w_ref[...], g, m_ref[...], v_ref[...])
        g_out[...] = g
        d_out[...] = d
        m_out[...] = m2
        v_out[...] = v2

    full = pl.BlockSpec((tr, ccols), lambda h, i, s: (h * nt + i, 0))
    part = pl.BlockSpec((tr, ccols), lambda h, i, s: (i, 0))
    return pl.pallas_call(
        body, name=name,
        grid_spec=pltpu.PrefetchScalarGridSpec(
            num_scalar_prefetch=1, grid=(2, nt), in_specs=[full, part, part, full, full], out_specs=[full] * 4),
        out_shape=[jax.ShapeDtypeStruct((r, ccols), F32)] * 4,
    )(place, w, mine, theirs, m, v)


def adamw_group(place, halved, plain, hosted, name):
    rows = halved[0][0].shape[0]
    tr = 128
    nt = rows // 2 // tr
    nh, npl = len(halved), len(plain)

    def body(place_ref, *refs):
        own_in, h_in, own_out, h_out, _, h_sems = hosted.split(refs, 5 * nh + 4 * npl, 4 * nh + 3 * npl)
        half = pl.program_id(0)
        grid_step = half * nt + pl.program_id(1)

        @pl.when(grid_step == 0)
        def _():
            hosted.start(h_in, h_out, h_sems)

        for i in range(nh):
            w_ref, a_ref, b_ref, m_ref, v_ref = own_in[5 * i:5 * i + 5]
            g = jnp.where(half == place_ref[0], a_ref[...], b_ref[...])
            res = (g,) + _adamw_math(w_ref[...], g, m_ref[...], v_ref[...])
            for o_ref, r in zip(own_out[4 * i:4 * i + 4], res):
                o_ref[...] = r
        for i in range(npl):
            w_ref, g_ref, m_ref, v_ref = own_in[5 * nh + 4 * i:5 * nh + 4 * i + 4]
            res = _adamw_math(w_ref[...], g_ref[...], m_ref[...], v_ref[...])
            for o_ref, r in zip(own_out[4 * nh + 3 * i:4 * nh + 3 * i + 3], res):
                o_ref[...] = r

        @pl.when(grid_step == 2 * nt - 1)
        def _():
            hosted.finish(h_in, h_out, h_sems)

    def full(cols):
        return pl.BlockSpec((tr, cols), lambda h, i, s: (h * nt + i, 0))

    def part(cols):
        return pl.BlockSpec((tr, cols), lambda h, i, s: (i, 0))

    in_specs, out_specs, out_shape, args = [], [], [], []
    for w, a, b, m, v in halved:
        cols = w.shape[1]
        in_specs += [full(cols), part(cols), part(cols), full(cols), full(cols)]
        out_specs += [full(cols)] * 4
        out_shape += [jax.ShapeDtypeStruct(w.shape, F32)] * 4
        args += [w, a, b, m, v]
    for w, g, m, v in plain:
        cols = w.shape[1]
        in_specs += [full(cols)] * 4
        out_specs += [full(cols)] * 3
        out_shape += [jax.ShapeDtypeStruct(w.shape, F32)] * 3
        args += [w, g, m, v]
    h_in_specs, h_out_specs = hosted.specs()
    return pl.pallas_call(
        body, name=name,
        grid_spec=pltpu.PrefetchScalarGridSpec(
            num_scalar_prefetch=1, grid=(2, nt), in_specs=in_specs + h_in_specs, out_specs=out_specs + h_out_specs,
            scratch_shapes=hosted.scratch),
        out_shape=out_shape + hosted.out_shape,
        compiler_params=_cp(("arbitrary", "arbitrary")),
    )(place, *args, *hosted.args)


def _silu(x):
    return x * jax.nn.sigmoid(x)


def prologue(c_rows, c_ctx_row, w_ada, b_shard, rpb_flat, half_w_in, late_shards):
    shape = jax.ShapeDtypeStruct
    n_late = len(late_shards)
    half_shapes = [(w.shape[0] // 2, w.shape[1]) for w in late_shards]
    g_w = gather8([half_w_in])
    g_c = gather8([shape((8, D), F32)])
    g_m = gather8([shape((32, 1536), F32)])

    def body(*refs):
        c_ref, cc_ref, w_ref, b_ref, flat_ref, hw_ref = refs[:6]
        late_refs = refs[6:6 + n_late]
        cin_ref, mg_ref, gw_ref, bias_ref, cos_ref, sin_ref = refs[6 + n_late:12 + n_late]
        rest = refs[12 + n_late:]
        half_refs, (cg_s, ms_s, bias_s) = rest[:n_late], rest[n_late:n_late + 3]
        stage, (load_sem, bias_sem), sems = rest[n_late + 3:2 * n_late + 3], rest[2 * n_late + 3:2 * n_late + 5], \
            rest[2 * n_late + 5:]
        sw, sc, sm = sems[0:3], sems[3:6], sems[6:9]
        core = lax.axis_index("c")
        g_c.start([c_ref], [cg_s], sc)
        loads = [pltpu.make_async_copy(late_refs[a].at[pl.ds(core * half_shapes[a][0], half_shapes[a][0]), :],
                                       stage[a], load_sem.at[a]) for a in range(n_late)]
        for cp in loads:
            cp.start()
        g_c.finish([c_ref], [cg_s], sc)
        cin_ref[...] = jnp.zeros_like(cin_ref)
        for dev in range(8):
            cin_ref[2 * dev:2 * dev + 2, :] = cg_s[dev, 0:2, :]
        cin_ref[16:17, :] = cc_ref[...]
        ms_s[...] = _nn(_silu(cin_ref[...]), w_ref[...]) + b_ref[...]
        g_m.start([ms_s], [mg_ref], sm)
        g_w.start([hw_ref], [gw_ref], sw)
        for a, cp in enumerate(loads):
            cp.wait()
            half_refs[a][...] = stage[a][...].astype(BF16)
        cos_ref[...], sin_ref[...] = _rope_tables()
        stores = []
        for pair in range(NPAIR):
            if pair >= 2:
                stores[pair - 2].wait()
            _na_bias_pair(flat_ref.at[pair], bias_s.at[pair % 2])
            stores.append(pltpu.make_async_copy(bias_s.at[pair % 2], bias_ref.at[pair], bias_sem.at[pair % 2]))
            stores[pair].start()
        for cp in stores[-2:]:
            cp.wait()
        g_m.finish([ms_s], [mg_ref], sm)
        g_w.finish([hw_ref], [gw_ref], sw)

    vmem = pl.BlockSpec(memory_space=pltpu.VMEM)
    hbm = pl.BlockSpec(memory_space=pl.ANY)
    return pl.pallas_call(
        body, name="prologue", in_specs=[vmem, vmem, vmem, vmem, vmem, hbm] + [hbm] * n_late,
        out_specs=[vmem, vmem, hbm, hbm, vmem, vmem] + [vmem] * n_late,
        out_shape=[shape((32, D), F32), shape((8, 32, 1536), F32)] + g_w.out_shape
        + [shape((NPAIR,) + NA_BIAS_SHAPE, F32)] + [shape((SEQ, RD), F32)] * 2 + [shape(s, BF16) for s in half_shapes],
        scratch_shapes=[pltpu.VMEM((8, 8, D), F32), pltpu.VMEM((32, 1536), F32), pltpu.VMEM((2,) + NA_BIAS_SHAPE, F32)]
        + [pltpu.VMEM(s, F32) for s in half_shapes]
        + [pltpu.SemaphoreType.DMA((n_late,)), pltpu.SemaphoreType.DMA((2,))]
        + g_w.scratch + g_c.scratch + g_m.scratch,
        compiler_params=_cp(),
    )(c_rows, c_ctx_row, w_ada, b_shard, rpb_flat, half_w_in, *late_shards)


def ada_grads(cin, gb, gc, w_ada):
    def body(c_ref, gb_ref, gc_ref, w_ref, gw_ref, pc_ref):
        ctx_tot = jnp.sum(gc_ref[...], axis=0, keepdims=True)
        rows = lax.broadcasted_iota(jnp.int32, (16, 512), 0)
        dm = jnp.concatenate([gb_ref[...], jnp.where(rows == 0, ctx_tot, 0.0)], axis=0)
        gw_ref[...] = _tn(_silu(c_ref[...]), dm)
        rows8 = lax.broadcasted_iota(jnp.int32, (8, 512), 0)
        part = _nt(jnp.where(rows8 == 0, ctx_tot, 0.0), w_ref[...])

        @pl.when(pl.program_id(0) == 0)
        def _():
            pc_ref[...] = jnp.zeros_like(pc_ref)

        pc_ref[...] += part

    return pl.pallas_call(
        body, name="ada_grads", grid=(3,),
        in_specs=[pl.BlockSpec((32, D), lambda j: (0, 0)), pl.BlockSpec((16, 512), lambda j: (0, j)),
                  pl.BlockSpec((8, 512), lambda j: (0, j)), pl.BlockSpec((D, 512), lambda j: (0, j))],
        out_specs=[pl.BlockSpec((D, 512), lambda j: (0, j)), pl.BlockSpec((8, D), lambda j: (0, 0))],
        out_shape=[jax.ShapeDtypeStruct((D, 1536), F32), jax.ShapeDtypeStruct((8, D), F32)],
    )(cin, gb, gc, w_ada)


SMALL_SUM_ROWS = 15


def small_update(gsm, gbf, gcf, pcg, params):
    n = len(params)

    def body(*refs):
        gsm_ref, gbf_ref, gcf_ref, pcg_ref = refs[:4]
        wmv, outs, loss_out = refs[4:4 + 3 * n], refs[4 + 3 * n:4 + 7 * n], refs[-1]
        acc = gsm_ref[0]
        for dev in range(1, 8):
            acc = acc + gsm_ref[dev]
        c_ctx = wmv[0][...]
        sg = jax.nn.sigmoid(c_ctx)
        dsilu = pcg_ref[0:1, :] + pcg_ref[2:3, :] + pcg_ref[4:5, :] + pcg_ref[6:7, :]
        lane = lax.broadcasted_iota(jnp.int32, (1, D), 1)
        last = acc[14:15, :]
        grads = [
            dsilu * (sg * (1.0 + c_ctx * (1.0 - sg))),
            jnp.sum(gbf_ref[...], axis=0, keepdims=True) + jnp.sum(gcf_ref[...], axis=0, keepdims=True),
            acc[0:1, :] + acc[1:2, :], acc[2:3, :], acc[3:4, :], acc[4:5, :],
            acc[5:6, 0:512], acc[6:14, :], jnp.where(lane < 8, last, 0.0),
        ]
        loss_out[...] = jnp.broadcast_to(jnp.sum(jnp.where(lane == 8, last, 0.0), axis=1, keepdims=True), (8, 128))
        for i, g in enumerate(grads):
            d, m2, v2 = _adamw_math(wmv[3 * i][...], g, wmv[3 * i + 1][...], wmv[3 * i + 2][...])
            outs[4 * i][...] = g
            outs[4 * i + 1][...] = d
            outs[4 * i + 2][...] = m2
            outs[4 * i + 3][...] = v2

    flat = [a for wmv in params for a in wmv]
    out_shape = [jax.ShapeDtypeStruct(w.shape, F32) for w, _, _ in params for _ in range(4)]
    return pl.pallas_call(
        body, name="small_update", out_shape=out_shape + [jax.ShapeDtypeStruct((8, 128), F32)],
    )(gsm, gbf, gcf, pcg, *flat)


def _pad_row(v, rows):
    flat = v.reshape(-1)
    return jnp.pad(flat, (0, rows * D - flat.shape[0])).reshape(rows, D)


def local_step(x, ctx, tgt, mod3, rope, bias, g_pre_mix, g_post_mix, g_pre_mlp, g_post_mlp, ret_decay, ret_gn,
               wperm, late_weights, early_grads):
    nb = x.shape[0]
    tokens = nb * SEQ
    cos, sin = rope
    rd = ret_decay.T.reshape(RH, 2, 1)
    gn = ret_gn.reshape(RH, 1, RD)
    h, pret, pna = premix_proj(x, mod3, g_pre_mix, wperm, False, "premix_proj")
    hc, pretc, pnac = premix_proj(ctx, mod3, g_pre_mix, wperm, True, "premix_proj_ctx")
    o_all, mixin, gw_out = retention_fwd(pret, pretc, rd, gn, cos, sin, late_weights(0))
    mixin, gw1, gw2 = na_fwd(pna, pnac, bias, mixin, late_weights(1))
    dx_tail, dmix, h2, du, act, dm, dmixin, dmod_t, dg_t, loss_t = tail_fwd_bwd(
        x, mixin, tgt, mod3, g_post_mix, g_pre_mlp, g_post_mlp, gw_out.reshape(D, D), gw1.reshape(4, D, D),
        gw2.reshape(DFF, D))
    dw_out = weight_grad([(mixin.reshape(tokens, D), dmix.reshape(tokens, D))], "grad_w_out", BF16)
    dw1 = weight_grad([(h2.reshape(tokens, D), du.reshape(tokens, DFF))], "grad_w_mlp1", BF16, col_blocks=True)
    dw2 = weight_grad([(act.reshape(tokens, DFF), dm.reshape(tokens, D))], "grad_w_mlp2", BF16)
    dproj, dprojc, drd, dgn, *landed = retention_bwd(pret, pretc, o_all, dmixin, rd, gn, cos, sin,
                                                     early_grads[0](dw_out, dw1, dw2))
    dproj, dprojc, dpat, *early = na_bwd(pna, pnac, bias, dmixin, dproj, dprojc, early_grads[1](landed))
    dw_in = weight_grad([(h.reshape(tokens, D), dproj.reshape(tokens, IN_W)),
                         (hc.reshape(nb * LC, D), dprojc.reshape(nb * LC, IN_W))], "grad_w_in", tn=IN_W // 2, tk=1024)
    dmod_c, dg_c, *late = premix_bwd(ctx, mod3, g_pre_mix, wperm, dprojc, None, early_grads[2](dw_in), "premix_bwd_ctx")
    grad_x, dmod_a, dg_a, *late = premix_bwd(x, mod3, g_pre_mix, wperm, dproj, dx_tail, early_grads[3](late),
                                             "premix_bwd")
    dmod = jnp.concatenate([jnp.concatenate([dmod_a[:, 0:2], dmod_t[:, 2:6]], axis=1), dmod_c], axis=0)
    last = jnp.pad(jnp.concatenate([drd[:, :, 0].T.reshape(8), loss_t[0, 0:1]]), (0, D - 9)).reshape(1, D)
    small = jnp.concatenate([dg_a[0:1], dg_c[0:1], dg_t[0:3], _pad_row(dgn, 1), dpat.reshape(8, D), last], axis=0)
    return grad_x, late, early, dmod, small


def kernel(x, c, ctx, c_ctx, w_ada, b_ada, g_pre_mix, g_post_mix, g_pre_mlp, g_post_mlp, w_in, ret_decay, ret_gn, na_rpb, w_out, w_mlp1, w_mlp2, loss_target, m_c_ctx, m_w_ada, m_b_ada, m_g_pre_mix, m_g_post_mix, m_g_pre_mlp, m_g_post_mlp, m_w_in, m_ret_decay, m_ret_gn, m_na_rpb, m_w_out, m_w_mlp1, m_w_mlp2, v_c_ctx, v_w_ada, v_b_ada, v_g_pre_mix, v_g_post_mix, v_g_pre_mlp, v_g_post_mlp, v_w_in, v_ret_decay, v_ret_gn, v_na_rpb, v_w_out, v_w_mlp1, v_w_mlp2):
    px, py, pc = _place()
    dev = 4 * px + 2 * py + pc
    chip = 2 * px + py

    half_w_in = lax.dynamic_slice_in_dim(w_in[0], pc * (D // 2), D // 2, 0).astype(BF16)
    cin, mg, gw_in, bias, cos, sin, *late_halves = prologue(
        jnp.pad(c, ((0, 6), (0, 0))), c_ctx[None], w_ada[0], lax.dynamic_slice_in_dim(b_ada, chip * 1536, 1536, 1),
        _rpb_flat(na_rpb[0]), half_w_in, [w_out[0], w_mlp1[0], w_mlp2[0]])
    halves = [half_w_in] + late_halves
    wperm = unpack_w_in(gw_in.reshape(4, D, 896))
    mod_all = jnp.concatenate([mg[0], mg[2], mg[4], mg[6]], axis=1)
    mod3 = (jnp.pad(lax.dynamic_slice_in_dim(mod_all, 2 * dev, 2, 0), ((0, 1), (0, 0)))
            + jnp.pad(mod_all[16:17], ((2, 0), (0, 0)))).reshape(3, 6, D)

    place = jnp.stack([pc, chip]).astype(jnp.int32)

    early_names = ["w_out", "w_mlp1", "w_mlp2"]
    early_g8, early_partial = [], []

    def early_a(dw_out, dw1, dw2):
        early_g8[:] = [dw_out.reshape(8, 128, D), dw1.reshape(8, 512, D), dw2.reshape(8, 512, D)]
        return siblings4(early_g8)

    def early_b(landed):
        early_partial[:] = chip_partial(place, early_g8, landed, "rs_chip_sum_early")
        return chips3(early_partial)

    late_partial = []

    late_g8 = []

    def late_c(dw_in):
        late_g8[:] = [pack_w_in(dw_in).reshape(8, 512, 896)]
        return siblings4(late_g8)

    def late_d(landed):
        late_partial[:] = chip_partial(place, late_g8, landed, "rs_chip_sum_w_in")
        return chips3(late_partial)

    grad_x, (landed3_in,), early_landed, dmod, small = local_step(
        x, ctx, loss_target, mod3, (cos, sin), bias, g_pre_mix, g_post_mix, g_pre_mlp, g_post_mlp, ret_decay[0], ret_gn,
        wperm, lambda k: gather8(halves[1:2] if k == 0 else halves[2:4]), (early_a, early_b, late_c, late_d))
    early_mine = shard_sum(place, early_partial, early_landed, "rs_shard_sum_early")

    pay = jnp.concatenate([dmod.reshape(18, D), small, jnp.zeros((40 - 18 - SMALL_SUM_ROWS, D), F32)], axis=0)
    *early_theirs, gs = run_hosted(both(siblings(early_mine), gather8([pay])), "rs_halves_early_gather_small")
    gbf = gs[:, 0:12].reshape(16, 6 * D)
    gcf = gs[:, 12:18].reshape(8, 6 * D)
    gw_ada, pc_part = ada_grads(cin, lax.dynamic_slice_in_dim(gbf, chip * 1536, 1536, 1),
                                lax.dynamic_slice_in_dim(gcf, chip * 1536, 1536, 1), w_ada[0])
    (mine_in,) = shard_sum(place, late_partial, [landed3_in], "rs_shard_sum_w_in")
    theirs_in, pcg = run_hosted(both(siblings([mine_in]), gather8([pc_part])), "rs_halves_w_in_gather_c_ctx")

    grouped = adamw_group(
        place,
        [(w_mlp1[0], early_mine[1], early_theirs[1], m_w_mlp1[0], v_w_mlp1[0]),
         (w_mlp2[0], early_mine[2], early_theirs[2], m_w_mlp2[0], v_w_mlp2[0])],
        [(w_ada[0], gw_ada, m_w_ada[0], v_w_ada[0])], no_exchange(), "adamw_group")
    d_ada, m_ada, v_ada = grouped[8:11]
    big = [
        [r[None] for r in adamw_halves(place, w_in[0], mine_in, theirs_in, m_w_in[0], v_w_in[0], "adamw_w_in")],
        [r[None] for r in adamw_halves(place, w_out[0], early_mine[0], early_theirs[0], m_w_out[0], v_w_out[0],
                                       "adamw_w_out")],
        [r[None] for r in grouped[0:4]], [r[None] for r in grouped[4:8]],
    ]

    def rpb_rows(t):
        return _rpb_flat(t[0]).reshape(8, D)

    def decay_row(t):
        return jnp.pad(t.reshape(1, 8), ((0, 0), (0, D - 8)))

    views = [lambda t: t.reshape(1, D), lambda t: t, lambda t: t, lambda t: t, lambda t: t, lambda t: t, lambda t: t,
             rpb_rows, decay_row]
    back = [lambda t: t.reshape(D), lambda t: t, lambda t: t, lambda t: t, lambda t: t, lambda t: t, lambda t: t,
            lambda t: _rpb_flat_t(t)[None], lambda t: t[:, 0:8].reshape(1, 2, 4)]
    small_w = (c_ctx, b_ada, g_pre_mix, g_post_mix, g_pre_mlp, g_post_mlp, ret_gn, na_rpb, ret_decay)
    small_m = (m_c_ctx, m_b_ada, m_g_pre_mix, m_g_post_mix, m_g_pre_mlp, m_g_post_mlp, m_ret_gn, m_na_rpb, m_ret_decay)
    small_v = (v_c_ctx, v_b_ada, v_g_pre_mix, v_g_post_mix, v_g_pre_mlp, v_g_post_mlp, v_ret_gn, v_na_rpb, v_ret_decay)
    *res, loss8 = small_update(gs[:, 18:18 + SMALL_SUM_ROWS], gbf, gcf, pcg[:, 0],
                               [(f(w), f(m), f(v)) for f, w, m, v in zip(views, small_w, small_m, small_v)])

    def leaves(ada, idx):
        s_c, s_b, s_g1, s_g2, s_g3, s_g4, s_gn, s_rpb, s_rd = [back[i](res[4 * i + idx]) for i in range(9)]
        return [s_c, ada[None], s_b, s_g1, s_g2, s_g3, s_g4, big[0][idx], s_rd, s_gn, s_rpb,
                big[1][idx], big[2][idx], big[3][idx]]

    return (loss8[0, 0], grad_x, *leaves(gw_ada, 0), *leaves(d_ada, 1), *leaves(m_ada, 2), *leaves(v_ada, 3))
```

```python
import functools
import math

import jax
import jax.numpy as jnp
from jax import lax
from jax.experimental import pallas as pl
from jax.experimental.pallas import tpu as pltpu

F32, BF16 = jnp.float32, jnp.bfloat16
D = 1024
SEQ = 2048
LC = 256
GW = 64
RH, RD, CH = 4, 128, 128
NPAIR = 4
IN_W = 3584
RET_W = 2048
DFF = 4096
EPS = 1e-6
NEG = -1e30
TN = 256
NCH = SEQ // CH
LR, B1, B2, AEPS, WD, STEP = 0.001, 0.9, 0.999, 1e-08, 0.01, 10
MESH = pl.DeviceIdType.MESH
VMEM_LIMIT = 56 * 1024 * 1024


def _cp(sem=None):
    return pltpu.CompilerParams(dimension_semantics=sem, vmem_limit_bytes=VMEM_LIMIT)


def _nn(a, b):
    return jnp.dot(a.astype(BF16), b.astype(BF16), preferred_element_type=F32)


def _nt(a, b):
    return lax.dot_general(a.astype(BF16), b.astype(BF16), (((1,), (1,)), ((), ())), preferred_element_type=F32)


def _tn(a, b):
    return lax.dot_general(a.astype(BF16), b.astype(BF16), (((0,), (0,)), ((), ())), preferred_element_type=F32)


@jax.custom_vjp
def mm_tn(a, b):
    return _tn(a, b)


mm_tn.defvjp(lambda a, b: (_tn(a, b), (a, b)), lambda r, g: (_nt(r[1], g), _nn(r[0], g)))


def _rms(x):
    return x * lax.rsqrt(jnp.mean(x * x, axis=-1, keepdims=True) + EPS)


def _rms_mod(x, g, sc, sh):
    return (_rms(x) * g) * (1.0 + sc) + sh


def _post_mix(x, mix, gt1, sc2, sh2, g_post_mix, g_pre_mlp):
    x1 = x + gt1 * (_rms(mix) * g_post_mix)
    return x1, _rms_mod(x1, g_pre_mlp, sc2, sh2)


def _head_loss(x1, m, gt2, g_post_mlp, tgt):
    err = x1 + gt2 * (_rms(m) * g_post_mlp) - tgt
    return 0.5 * jnp.sum(jnp.mean(err * err, axis=-1, keepdims=True), axis=0, keepdims=True)


def _ln_gate(o, g, w):
    mu = jnp.mean(o, axis=-1, keepdims=True)
    var = jnp.mean(jnp.square(o - mu), axis=-1, keepdims=True)
    y = (o - mu) * lax.rsqrt(var + EPS)
    return (y * w) * (g * jax.nn.sigmoid(g))


def _pair_order(x):
    lane = lax.broadcasted_iota(jnp.int32, x.shape, 1)
    return jnp.where((lane >= 32) & (lane < 64), pltpu.roll(x, 96, 1),
                     jnp.where((lane >= 64) & (lane < 96), pltpu.roll(x, 32, 1), x))


def _rope(x, cos, sin):
    return x * cos + pltpu.roll(x, 64, 1) * sin


def _rope_t(g, cos, sin):
    return g * cos + pltpu.roll(g * sin, 64, 1)


def _rope_tables():
    tok = lax.broadcasted_iota(jnp.int32, (SEQ, RD), 0)
    lane = lax.broadcasted_iota(jnp.int32, (SEQ, RD), 1)
    pos = jnp.where((lane & 32) == 0, tok >> 6, tok & (GW - 1)).astype(F32)
    ang = pos * jnp.exp((lane & 31).astype(F32) * (-math.log(10000.0) / 32))
    return jnp.cos(ang), jnp.where(lane < 64, -jnp.sin(ang), jnp.sin(ang))


def _chunk_loop(n, body, init, k=4):
    def several(t, carry):
        for i in range(k):
            carry = body(k * t + i, carry)
        return carry

    return lax.fori_loop(0, n // k, several, init)


def _fiota(shape, dim):
    return lax.broadcasted_iota(jnp.int32, shape, dim).astype(F32)


def _ret_state(k, v, s, lg, reverse):
    pos = _fiota((CH, 1), 0)
    b_exp = pos if reverse else (CH - 1.0 - pos)
    return jnp.exp(lg * CH) * s + mm_tn(k * jnp.exp(lg * b_exp), v)


class _Decays:
    def __init__(self, lgs):
        i, j, pos = _fiota((CH, CH), 0), _fiota((CH, CH), 1), _fiota((CH, 1), 0)
        diffs = (i - j, j - i)
        keep = (diffs[0] >= 0, diffs[1] > 0)
        mats = [jnp.where(m, jnp.exp(lg * jnp.where(m, d, 0.0)), 0.0) for lg, d, m in zip(lgs, diffs, keep)]
        self.mask = mats[0] + mats[1]
        self.dmask = [mats[0] * diffs[0], mats[1] * diffs[1]]
        a_exp, b_exp = (pos + 1.0, CH - pos), (CH - 1.0 - pos, pos)
        self.a = [jnp.exp(lg * e) for lg, e in zip(lgs, a_exp)]
        self.b = [jnp.exp(lg * e) for lg, e in zip(lgs, b_exp)]
        self.da = [a * e for a, e in zip(self.a, a_exp)]
        self.db = [b * e for b, e in zip(self.b, b_exp)]
        self.g = [jnp.exp(lg * CH) for lg in lgs]


def _both(x, w):
    return jnp.concatenate([x * w[0], x * w[1]], axis=1)


def _total(x):
    return jnp.sum(jnp.sum(x, axis=1, keepdims=True), axis=0, keepdims=True)


def _state_pass(dec, init, k_s, v_of, st_s):
    def step(t, carry):
        out = []
        for d, s in enumerate(carry):
            n = (NCH - 1 - t) if d else t
            sl = pl.ds(pl.multiple_of(n * CH, CH), CH)
            st_s[n, d * RD:(d + 1) * RD, :] = s
            out.append(dec.g[d] * s + _tn(k_s[sl, :] * dec.b[d], v_of(sl)))
        return tuple(out)

    _chunk_loop(NCH, step, tuple(init))


def premix_proj(xin, mod3, g_pre, wperm, is_ctx, name):
    nb, length, _ = xin.shape
    tn = min(2 * TN, length)

    def body(x_ref, mod_ref, g_ref, w_ref, h_ref, pret_ref, pna_ref):
        h = _rms_mod(x_ref[...], g_ref[...], mod_ref[1:2, :], mod_ref[0:1, :])
        hb = h.astype(BF16)
        h_ref[...] = hb
        pret_ref[...] = jnp.dot(hb, w_ref[:, :RET_W], preferred_element_type=F32)
        pna_ref[...] = jnp.dot(hb, w_ref[:, RET_W:], preferred_element_type=F32).astype(BF16)

    return pl.pallas_call(
        body, name=name, grid=(nb, length // tn),
        in_specs=[
            pl.BlockSpec((None, tn, D), lambda b, t: (b, t, 0)),
            pl.BlockSpec((None, 6, D), (lambda b, t: (2, 0, 0)) if is_ctx else (lambda b, t: (b, 0, 0))),
            pl.BlockSpec((1, D), lambda b, t: (0, 0)),
            pl.BlockSpec((D, IN_W), lambda b, t: (0, 0), pipeline_mode=pl.Buffered(1)),
        ],
        out_specs=[
            pl.BlockSpec((None, tn, D), lambda b, t: (b, t, 0)),
            pl.BlockSpec((None, tn, RET_W), lambda b, t: (b, t, 0)),
            pl.BlockSpec((None, tn, IN_W - RET_W), lambda b, t: (b, t, 0)),
        ],
        out_shape=[
            jax.ShapeDtypeStruct((nb, length, D), BF16),
            jax.ShapeDtypeStruct((nb, length, RET_W), F32),
            jax.ShapeDtypeStruct((nb, length, IN_W - RET_W), BF16),
        ],
        compiler_params=_cp(("arbitrary", "arbitrary")),
    )(xin, mod3, g_pre, wperm)


def premix_bwd(xin, mod3, g_pre, wperm, dproj, dx_tail, hosted, name):
    nb, length, _ = xin.shape
    tn = min(TN, length)
    is_ctx = dx_tail is None

    def body(*refs):
        own_in, h_in, own_out, h_out, _, h_sems = hosted.split(refs, 5 if is_ctx else 6, 2 if is_ctx else 3)
        if is_ctx:
            (x_ref, mod_ref, g_ref, w_ref, dp_ref), (dmod_ref, dg_ref) = own_in, own_out
        else:
            (x_ref, mod_ref, g_ref, w_ref, dp_ref, dxt_ref), (dx_ref, dmod_ref, dg_ref) = own_in, own_out
        b, t = pl.program_id(0), pl.program_id(1)
        grid_step = b * (length // tn) + t

        @pl.when(grid_step == 0)
        def _():
            hosted.start(h_in, h_out, h_sems)

        @pl.when(grid_step == nb * (length // tn) - 1)
        def _():
            hosted.finish(h_in, h_out, h_sems)

        dh = lax.dot_general(dp_ref[...], w_ref[...], (((1,), (1,)), ((), ())), preferred_element_type=F32)
        _, vjp = jax.vjp(_rms_mod, x_ref[...], g_ref[...], mod_ref[1:2, :], mod_ref[0:1, :])
        dx, dg, dsc, dsh = vjp(dh)
        if not is_ctx:
            dx_ref[...] = dx + dxt_ref[...]

        @pl.when((t == 0) & ((b == 0) if is_ctx else True))
        def _():
            dmod_ref[...] = jnp.zeros_like(dmod_ref)

        @pl.when((t == 0) & (b == 0))
        def _():
            dg_ref[...] = jnp.zeros_like(dg_ref)

        dmod_ref[0:1, :] += dsh
        dmod_ref[1:2, :] += dsc
        dg_ref[0:1, :] += dg

    tok = lambda b, t: (b, t, 0)
    in_specs = [
        pl.BlockSpec((None, tn, D), tok),
        pl.BlockSpec((None, 6, D), (lambda b, t: (2, 0, 0)) if is_ctx else (lambda b, t: (b, 0, 0))),
        pl.BlockSpec((1, D), lambda b, t: (0, 0)),
        pl.BlockSpec((D, IN_W), lambda b, t: (0, 0), pipeline_mode=pl.Buffered(1)),
        pl.BlockSpec((None, tn, IN_W), tok),
    ]
    args = [xin, mod3, g_pre, wperm, dproj]
    out_specs = [
        pl.BlockSpec((None, 6, D), (lambda b, t: (0, 0, 0)) if is_ctx else (lambda b, t: (b, 0, 0))),
        pl.BlockSpec((8, D), lambda b, t: (0, 0)),
    ]
    out_shape = [jax.ShapeDtypeStruct((1 if is_ctx else nb, 6, D), F32), jax.ShapeDtypeStruct((8, D), F32)]
    if not is_ctx:
        in_specs.append(pl.BlockSpec((None, tn, D), tok))
        args.append(dx_tail)
        out_specs.insert(0, pl.BlockSpec((None, tn, D), tok))
        out_shape.insert(0, jax.ShapeDtypeStruct((nb, length, D), F32))
    h_in_specs, h_out_specs = hosted.specs()
    return pl.pallas_call(
        body, name=name, grid=(nb, length // tn), in_specs=in_specs + h_in_specs, out_specs=out_specs + h_out_specs,
        out_shape=out_shape + hosted.out_shape, scratch_shapes=hosted.scratch,
        compiler_params=_cp(("arbitrary", "arbitrary")),
    )(*args, *hosted.args)


def _ret_specs(order):
    def im(f):
        return lambda *g: f(*order(*g))
    return dict(
        pret=pl.BlockSpec((None, SEQ, 512), im(lambda b, h: (b, 0, h))),
        pretc=pl.BlockSpec((None, LC, 512), im(lambda b, h: (b, 0, h))),
        rd=pl.BlockSpec((None, 2, 1), im(lambda b, h: (h, 0, 0))),
        gn=pl.BlockSpec((None, 1, RD), im(lambda b, h: (h, 0, 0))),
        tab=pl.BlockSpec((SEQ, RD), im(lambda b, h: (0, 0))),
        head=pl.BlockSpec((None, SEQ, RD), im(lambda b, h: (b, 0, h))),
    )


def retention_fwd(pret, pretc, rd, gn, cos, sin, hosted):
    nb = pret.shape[0]
    sp = _ret_specs(lambda b, h: (b, h))

    def body(*refs):
        own_in, h_in, own_out, h_out, own_scr, h_sems = hosted.split(refs, 6, 2)
        p_ref, pc_ref, rd_ref, gn_ref, cos_ref, sin_ref = own_in
        (o_ref, mix_ref), (q_s, k_s, o_s, st_s) = own_out, own_scr
        grid_step = pl.program_id(0) * RH + pl.program_id(1)

        @pl.when(grid_step == 0)
        def _():
            hosted.start(h_in, h_out, h_sems)

        cos_v, sin_v = cos_ref[...], sin_ref[...]
        q_s[...] = _rope(p_ref[:, 0:128], cos_v, sin_v) * (RD ** -0.5)
        k_s[...] = _rope(p_ref[:, 128:256], cos_v, sin_v)
        lgs, init = [], []
        for rev in (False, True):
            lg = jax.nn.log_sigmoid(rd_ref[int(rev):int(rev) + 1, :])
            s = jnp.zeros((RD, RD), F32)
            for n in ((1, 0) if rev else (0, 1)):
                s = _ret_state(pc_ref[n * CH:(n + 1) * CH, 128:256], pc_ref[n * CH:(n + 1) * CH, 256:384], s, lg, rev)
            lgs.append(lg)
            init.append(s)

        dec = _Decays(lgs)
        _state_pass(dec, init, k_s, lambda sl: p_ref[sl, 256:384], st_s)

        def chunk(n, carry):
            sl = pl.ds(pl.multiple_of(n * CH, CH), CH)
            q = q_s[sl, :]
            o_s[sl, :] = (_nn(_nt(q, k_s[sl, :]) * dec.mask, p_ref[sl, 256:384]) + _nn(_both(q, dec.a), st_s[n]))
            return carry

        _chunk_loop(NCH, chunk, 0)
        o = o_s[...]
        o_ref[...] = o
        mix_ref[...] = _ln_gate(o, p_ref[:, 384:512], gn_ref[...]).astype(BF16)

        @pl.when(grid_step == nb * RH - 1)
        def _():
            hosted.finish(h_in, h_out, h_sems)

    h_in_specs, h_out_specs = hosted.specs()
    return pl.pallas_call(
        body, name="retention_fwd", grid=(nb, RH),
        in_specs=[sp["pret"], sp["pretc"], sp["rd"], sp["gn"], sp["tab"], sp["tab"]] + h_in_specs,
        out_specs=[sp["head"], sp["head"]] + h_out_specs,
        out_shape=[jax.ShapeDtypeStruct((nb, SEQ, RH * RD), F32), jax.ShapeDtypeStruct((nb, SEQ, D), BF16)]
        + hosted.out_shape,
        scratch_shapes=[pltpu.VMEM((SEQ, RD), F32)] * 3 + [pltpu.VMEM((NCH, 2 * RD, RD), F32)] + hosted.scratch,
        compiler_params=_cp(("arbitrary", "arbitrary")),
    )(pret, pretc, rd, gn, cos, sin, *hosted.args)


def retention_bwd(pret, pretc, o_all, dmixin, rd, gn, cos, sin, hosted):
    nb = pret.shape[0]
    sp = _ret_specs(lambda h, b: (b, h))

    def body(*refs):
        own_in, h_in, own_out, h_out, own_scr, h_sems = hosted.split(refs, 8, 4)
        p_ref, pc_ref, o_ref, dmix_ref, rd_ref, gn_ref, cos_ref, sin_ref = own_in
        dp_ref, dpc_ref, drd_ref, dgn_ref = own_out
        q_s, k_s, do_s, dq_s, dk_s, dv_s, st_s, gst_s = own_scr
        b = pl.program_id(1)
        grid_step = pl.program_id(0) * nb + b

        @pl.when(grid_step == 0)
        def _():
            hosted.start(h_in, h_out, h_sems)

        cos_v, sin_v = cos_ref[...], sin_ref[...]
        q_s[...] = _rope(p_ref[:, 0:128], cos_v, sin_v) * (RD ** -0.5)
        k_s[...] = _rope(p_ref[:, 128:256], cos_v, sin_v)
        _, gate_vjp = jax.vjp(_ln_gate, o_ref[...], p_ref[:, 384:512], gn_ref[...])
        do, dg, dgn = gate_vjp(dmix_ref[...].astype(F32))
        do_s[...] = do
        dp_ref[:, 384:512] = dg.astype(BF16)

        @pl.when(b == 0)
        def _():
            drd_ref[...] = jnp.zeros_like(drd_ref)
            dgn_ref[...] = jnp.zeros_like(dgn_ref)

        dgn_ref[...] += dgn
        kcs = [pc_ref[n * CH:(n + 1) * CH, 128:256] for n in (0, 1)]
        vcs = [pc_ref[n * CH:(n + 1) * CH, 256:384] for n in (0, 1)]
        dirs = []
        init = []
        for rev in (False, True):
            rdv = rd_ref[int(rev):int(rev) + 1, :]
            lg = jax.nn.log_sigmoid(rdv)
            order_c = (1, 0) if rev else (0, 1)
            s = jnp.zeros((RD, RD), F32)
            ctx_states = []
            for n in order_c:
                ctx_states.append(s)
                s = _ret_state(kcs[n], vcs[n], s, lg, rev)
            dirs.append((rev, order_c, lg, rdv, ctx_states))
            init.append(s)
        dec = _Decays([lg for _, _, lg, _, _ in dirs])

        def v_of(sl):
            return p_ref[sl, 256:384]

        _state_pass(dec, init, k_s, v_of, st_s)
        zeros = jnp.zeros((CH, RD), F32)

        def scores_back(n, carry):
            dmask_sum, da_f, da_b = carry
            sl = pl.ds(pl.multiple_of(n * CH, CH), CH)
            q, k, v, do = q_s[sl, :], k_s[sl, :], v_of(sl), do_s[sl, :]
            scores = _nt(q, k)
            d_att = _nt(do, v)
            d_scores = d_att * dec.mask
            d_qa = _nt(do, st_s[n])
            d_qf, d_qb = d_qa[:, 0:RD], d_qa[:, RD:2 * RD]
            dq_s[sl, :] = _nn(d_scores, k) + d_qf * dec.a[0] + d_qb * dec.a[1]
            dk_s[sl, :] = _tn(d_scores, q)
            dv_s[sl, :] = _tn(scores * dec.mask, do)
            gst_s[n] = _tn(_both(q, dec.a), do)
            return dmask_sum + d_att * scores, da_f + d_qf * q, da_b + d_qb * q

        dmask_sum, da_f, da_b = _chunk_loop(NCH, scores_back, (zeros, zeros, zeros))

        def state_back(t, carry):
            out = []
            for d, r in enumerate(carry):
                n = t if d else (NCH - 1 - t)
                rows = slice(d * RD, (d + 1) * RD)
                own = gst_s[n, rows, :]
                gst_s[n, rows, :] = r
                out.append(own + dec.g[d] * r)
            return tuple(out)

        d_states = _chunk_loop(NCH, state_back, (zeros, zeros))

        def updates_back(n, carry):
            db_f, db_b, dg_f, dg_b = carry
            sl = pl.ds(pl.multiple_of(n * CH, CH), CH)
            k, r, s = k_s[sl, :], gst_s[n], st_s[n]
            d_kw = _nt(v_of(sl), r)
            d_kf, d_kb = d_kw[:, 0:RD], d_kw[:, RD:2 * RD]
            dk_s[sl, :] += d_kf * dec.b[0] + d_kb * dec.b[1]
            dv_s[sl, :] += _nn(_both(k, dec.b), r)
            return (db_f + d_kf * k, db_b + d_kb * k, dg_f + r[0:RD, :] * s[0:RD, :],
                    dg_b + r[RD:2 * RD, :] * s[RD:2 * RD, :])

        db_dg = _chunk_loop(NCH, updates_back, (zeros, zeros, zeros, zeros))
        dkc = [None, None]
        dvc = [None, None]
        for d, ((rev, order_c, lg, rdv, ctx_states), ds) in enumerate(zip(dirs, d_states)):
            dlg = (_total(dmask_sum * dec.dmask[d]) + _total((da_f, da_b)[d] * dec.da[d])
                   + _total(db_dg[d] * dec.db[d]) + CH * dec.g[d] * _total(db_dg[2 + d]))
            for idx in (1, 0):
                n = order_c[idx]
                _, vjp = jax.vjp(functools.partial(_ret_state, reverse=rev), kcs[n], vcs[n], ctx_states[idx], lg)
                dk_c, dv_c, ds, dl = vjp(ds)
                dlg = dlg + dl
                dkc[n] = dk_c if dkc[n] is None else dkc[n] + dk_c
                dvc[n] = dv_c if dvc[n] is None else dvc[n] + dv_c
            drd_ref[int(rev):int(rev) + 1, :] += dlg * jax.nn.sigmoid(-rdv)
        dp_ref[:, 0:128] = _rope_t(dq_s[...] * (RD ** -0.5), cos_v, sin_v).astype(BF16)
        dp_ref[:, 128:256] = _rope_t(dk_s[...], cos_v, sin_v).astype(BF16)
        dp_ref[:, 256:384] = dv_s[...].astype(BF16)
        zero = jnp.zeros((CH, RD), BF16)
        for n in (0, 1):
            rows = slice(n * CH, (n + 1) * CH)
            dpc_ref[rows, 0:128] = zero
            dpc_ref[rows, 128:256] = dkc[n].astype(BF16)
            dpc_ref[rows, 256:384] = dvc[n].astype(BF16)
            dpc_ref[rows, 384:512] = zero

        @pl.when(grid_step == RH * nb - 1)
        def _():
            hosted.finish(h_in, h_out, h_sems)

    h_in_specs, h_out_specs = hosted.specs()
    return pl.pallas_call(
        body, name="retention_bwd", grid=(RH, nb),
        in_specs=[sp["pret"], sp["pretc"], sp["head"], sp["head"], sp["rd"], sp["gn"], sp["tab"], sp["tab"]]
        + h_in_specs,
        out_specs=[
            pl.BlockSpec((None, SEQ, 512), lambda h, b: (b, 0, h)),
            pl.BlockSpec((None, LC, 512), lambda h, b: (b, 0, h)),
            pl.BlockSpec((None, 2, 1), lambda h, b: (h, 0, 0)),
            pl.BlockSpec((None, 1, RD), lambda h, b: (h, 0, 0)),
        ] + h_out_specs,
        out_shape=[
            jax.ShapeDtypeStruct((nb, SEQ, IN_W), BF16),
            jax.ShapeDtypeStruct((nb, LC, IN_W), BF16),
            jax.ShapeDtypeStruct((RH, 2, 1), F32),
            jax.ShapeDtypeStruct((RH, 1, RD), F32),
        ] + hosted.out_shape,
        scratch_shapes=[pltpu.VMEM((SEQ, RD), F32)] * 6 + [pltpu.VMEM((NCH, 2 * RD, RD), F32)] * 2 + hosted.scratch,
        compiler_params=_cp(("arbitrary", "arbitrary")),
    )(pret, pretc, o_all, dmixin, rd, gn, cos, sin, *hosted.args)


def _rpb_flat(rpb):
    return jnp.pad(rpb, ((0, 0), (0, 1), (0, 33))).reshape(NPAIR, 2, 1, 1024)


def _rpb_flat_t(dflat):
    return dflat.reshape(8, 16, 64)[:, :15, :31]


def _barrel(x, left):
    row = lax.broadcasted_iota(jnp.int32, x.shape, 0)
    n = x.shape[1]
    for bit in range(6):
        s = 1 << bit
        x = jnp.where(((row >> bit) & 1) == 1, pltpu.roll(x, (n - s) if left else s, 1), x)
    return x


NA_TILE_ROWS, NA_BAND_ROWS = 4, 12
NA_Q, NA_K = NA_TILE_ROWS * GW, NA_BAND_ROWS * GW
NA_TILES = SEQ // NA_Q


def _band_start(r0):
    return min(max(r0 - 4, 0), 32 - NA_BAND_ROWS)


def _tile_layout(t):
    rows = range(t * NA_TILE_ROWS, (t + 1) * NA_TILE_ROWS)
    return tuple((r if r < 4 else (r - 24 if r > 28 else 4), min(max(r - 4, 0), 24) - _band_start(rows[0]))
                 for r in rows)


NA_CLASSES = sorted(set(_tile_layout(t) for t in range(NA_TILES)))


def _tile_rows(cls):
    return NA_CLASSES[cls]


def _na_tile(t):
    start = jnp.clip(NA_TILE_ROWS * t - 4, 0, 32 - NA_BAND_ROWS)
    cls = 0
    for tile in range(NA_TILES):
        cls = jnp.where(t == tile, NA_CLASSES.index(_tile_layout(tile)), cls)
    return pl.ds(pl.multiple_of(t * NA_Q, NA_Q), NA_Q), pl.ds(pl.multiple_of(start * GW, NA_Q), NA_K), cls


def _na_probs(qst, keys, bias):
    s = _nt(qst, keys) + bias
    e = jnp.exp(s - jnp.max(s, axis=1, keepdims=True))
    return e / jnp.sum(e, axis=1, keepdims=True)


def _stack_heads(t):
    lane = lax.broadcasted_iota(jnp.int32, t.shape, 1)
    zero = jnp.zeros_like(t)
    return jnp.concatenate([jnp.where(lane < 64, t, zero), jnp.where(lane >= 64, t, zero)], axis=0)


def _unstack_heads(t):
    n = t.shape[0] // 2
    lane = lax.broadcasted_iota(jnp.int32, (n, 128), 1)
    return jnp.where(lane < 64, t[:n], t[n:])


NA_BIAS_SHAPE = (len(NA_CLASSES), 2 * NA_Q, NA_K + LC)


def _na_bias_pair(flat_ref, out_ref):
    qc = lax.broadcasted_iota(jnp.int32, (GW, 512), 0)
    kc = lax.broadcasted_iota(jnp.int32, (GW, 512), 1) & 63
    start = jnp.clip(qc - 8, 0, GW - 16)
    window = (kc >= start) & (kc < start + 16)
    fill = jnp.full((GW, NA_K - 512), NEG, F32)
    for hh in (0, 1):
        skew = _barrel(pltpu.roll(jnp.broadcast_to(flat_ref[hh], (GW, 1024)), 1024 - 15, 1), left=False)
        by_class = [jnp.where(window, (skew if rc == 7 else pltpu.roll(skew, (9 + rc) * 64, 1))[:, 0:512], NEG)
                    for rc in range(8)]
        for cls in range(len(NA_CLASSES)):
            for qr, (rc, off) in enumerate(_tile_rows(cls)):
                w = jnp.concatenate([by_class[rc], fill], axis=1)
                rows = slice(hh * NA_Q + qr * GW, hh * NA_Q + (qr + 1) * GW)
                out_ref[cls, rows, 0:NA_K] = pltpu.roll(w, off * GW, 1) if off else w
                out_ref[cls, rows, NA_K:] = jnp.zeros((GW, LC), F32)


def na_fwd(pna, pnac, bias, mixin, hosted):
    nb = pna.shape[0]

    def body(*refs):
        (p_ref, pc_ref, bias_ref, _), h_in, (out_ref,), h_out, _, h_sems = hosted.split(refs, 4, 1)
        grid_step = pl.program_id(0) * nb + pl.program_id(1)

        @pl.when(grid_step == 0)
        def _():
            hosted.start(h_in, h_out, h_sems)

        kc, vc = pc_ref[:, 128:256], pc_ref[:, 256:384]

        def tile(t, carry):
            qsl, bsl, cls = _na_tile(t)
            keys = jnp.concatenate([p_ref[bsl, 128:256], kc], axis=0)
            values = jnp.concatenate([p_ref[bsl, 256:384], vc], axis=0)
            probs = _na_probs(_stack_heads(p_ref[qsl, 0:128] * 0.125), keys, bias_ref[cls])
            out_ref[qsl, :] = _unstack_heads(_nn(probs, values)).astype(BF16)
            return carry

        lax.fori_loop(0, NA_TILES, tile, 0, unroll=4)

        @pl.when(grid_step == NPAIR * nb - 1)
        def _():
            hosted.finish(h_in, h_out, h_sems)

    h_in_specs, h_out_specs = hosted.specs()
    return pl.pallas_call(
        body, name="na_fwd", grid=(NPAIR, nb),
        in_specs=[
            pl.BlockSpec((None, SEQ, 384), lambda p, b: (b, 0, p)),
            pl.BlockSpec((None, LC, 384), lambda p, b: (b, 0, p)),
            pl.BlockSpec((None,) + NA_BIAS_SHAPE, lambda p, b: (p, 0, 0, 0)),
            pl.BlockSpec(memory_space=pl.ANY),
        ] + h_in_specs,
        out_specs=[pl.BlockSpec((None, SEQ, 128), lambda p, b: (b, 0, 4 + p))] + h_out_specs,
        out_shape=[jax.ShapeDtypeStruct((nb, SEQ, D), BF16)] + hosted.out_shape,
        input_output_aliases={3: 0},
        scratch_shapes=hosted.scratch,
        compiler_params=_cp(("arbitrary", "arbitrary")),
    )(pna, pnac, bias, mixin, *hosted.args)


def na_bwd(pna, pnac, bias, dmixin, dproj, dprojc, hosted):
    nb = pna.shape[0]

    def body(*refs):
        own_in, h_in, own_out, h_out, own_scr, h_sems = hosted.split(refs, 6, 3)
        p_ref, pc_ref, bias_ref, dmix_ref = own_in[:4]
        dp_ref, dpc_ref, dpat_ref = own_out
        dbias_s, dk_s, dv_s, dkc_s, dvc_s, res_s, resc_s = own_scr
        b, part = pl.program_id(1), pl.program_id(2)
        grid_step = (pl.program_id(0) * nb + b) * 3 + part

        @pl.when(grid_step == 0)
        def _():
            hosted.start(h_in, h_out, h_sems)

        @pl.when(grid_step == NPAIR * nb * 3 - 1)
        def _():
            hosted.finish(h_in, h_out, h_sems)

        @pl.when(part == 0)
        def _():
            @pl.when(b == 0)
            def _():
                dbias_s[...] = jnp.zeros_like(dbias_s)

            dk_s[...] = jnp.zeros_like(dk_s)
            dv_s[...] = jnp.zeros_like(dv_s)
            dkc_s[...] = jnp.zeros_like(dkc_s)
            dvc_s[...] = jnp.zeros_like(dvc_s)
            kc, vc = pc_ref[:, 128:256], pc_ref[:, 256:384]

            def tile(t, carry):
                qsl, bsl, cls = _na_tile(t)
                kb, vb = p_ref[bsl, 128:256], p_ref[bsl, 256:384]
                qst, dost = _stack_heads(p_ref[qsl, 0:128] * 0.125), _stack_heads(dmix_ref[qsl, :])
                s_loc, s_ctx = _nt(qst, kb) + bias_ref[cls, :, 0:NA_K], _nt(qst, kc)
                m = jnp.maximum(jnp.max(s_loc, axis=1, keepdims=True), jnp.max(s_ctx, axis=1, keepdims=True))
                e_loc, e_ctx = jnp.exp(s_loc - m), jnp.exp(s_ctx - m)
                den = jnp.sum(e_loc, axis=1, keepdims=True) + jnp.sum(e_ctx, axis=1, keepdims=True)
                p_loc, p_ctx = e_loc / den, e_ctx / den
                dp_loc, dp_ctx = _nt(dost, vb), _nt(dost, vc)
                delta = (jnp.sum(p_loc * dp_loc, axis=1, keepdims=True)
                         + jnp.sum(p_ctx * dp_ctx, axis=1, keepdims=True))
                ds_loc, ds_ctx = p_loc * (dp_loc - delta), p_ctx * (dp_ctx - delta)
                dbias_s[cls] += ds_loc
                res_s[0, qsl, :] = _unstack_heads((_nn(ds_loc, kb) + _nn(ds_ctx, kc)) * 0.125).astype(BF16)
                dk_s[bsl, :] += _tn(ds_loc, qst)
                dv_s[bsl, :] += _tn(p_loc, dost)
                dkc_s[...] += _tn(ds_ctx, qst)
                dvc_s[...] += _tn(p_ctx, dost)
                return carry

            lax.fori_loop(0, NA_TILES, tile, 0, unroll=2)
            res_s[1] = dk_s[...].astype(BF16)
            res_s[2] = dv_s[...].astype(BF16)
            resc_s[0] = jnp.zeros((LC, 128), BF16)
            resc_s[1] = dkc_s[...].astype(BF16)
            resc_s[2] = dvc_s[...].astype(BF16)

            @pl.when(b == nb - 1)
            def _():
                for hh in (0, 1):
                    by_class = [None] * 8
                    for cls in range(len(NA_CLASSES)):
                        for qr, (rc, off) in enumerate(_tile_rows(cls)):
                            w = dbias_s[cls, hh * NA_Q + qr * GW:hh * NA_Q + (qr + 1) * GW, :]
                            w = (pltpu.roll(w, NA_K - off * GW, 1) if off else w)[:, 0:512]
                            by_class[rc] = w if by_class[rc] is None else by_class[rc] + w
                    skew = jnp.zeros((GW, 1024), F32)
                    for rc in range(8):
                        w = jnp.concatenate([by_class[rc], jnp.zeros((GW, 512), F32)], axis=1)
                        skew = skew + (w if rc == 7 else pltpu.roll(w, (7 - rc) * 64, 1))
                    dpat_ref[hh] = jnp.sum(pltpu.roll(_barrel(skew, left=True), 15, 1), axis=0, keepdims=True)

        dp_ref[...] = res_s[part]
        dpc_ref[...] = resc_s[part]

    h_in_specs, h_out_specs = hosted.specs()
    return pl.pallas_call(
        body, name="na_bwd", grid=(NPAIR, nb, 3),
        in_specs=[
            pl.BlockSpec((None, SEQ, 384), lambda p, b, s: (b, 0, p)),
            pl.BlockSpec((None, LC, 384), lambda p, b, s: (b, 0, p)),
            pl.BlockSpec((None,) + NA_BIAS_SHAPE, lambda p, b, s: (p, 0, 0, 0)),
            pl.BlockSpec((None, SEQ, 128), lambda p, b, s: (b, 0, 4 + p)),
            pl.BlockSpec(memory_space=pl.ANY),
            pl.BlockSpec(memory_space=pl.ANY),
        ] + h_in_specs,
        out_specs=[
            pl.BlockSpec((None, SEQ, 128), lambda p, b, s: (b, 0, 16 + 3 * p + s)),
            pl.BlockSpec((None, LC, 128), lambda p, b, s: (b, 0, 16 + 3 * p + s)),
            pl.BlockSpec((None, 2, 1, 1024), lambda p, b, s: (p, 0, 0, 0)),
        ] + h_out_specs,
        out_shape=[
            jax.ShapeDtypeStruct((nb, SEQ, IN_W), BF16),
            jax.ShapeDtypeStruct((nb, LC, IN_W), BF16),
            jax.ShapeDtypeStruct((NPAIR, 2, 1, 1024), F32),
        ] + hosted.out_shape,
        input_output_aliases={4: 0, 5: 1},
        scratch_shapes=[
            pltpu.VMEM((len(NA_CLASSES), 2 * NA_Q, NA_K), F32),
            pltpu.VMEM((SEQ, 128), F32), pltpu.VMEM((SEQ, 128), F32),
            pltpu.VMEM((LC, 128), F32), pltpu.VMEM((LC, 128), F32),
            pltpu.VMEM((3, SEQ, 128), BF16), pltpu.VMEM((3, LC, 128), BF16),
        ] + hosted.scratch,
        compiler_params=_cp(("arbitrary", "arbitrary", "arbitrary")),
    )(pna, pnac, bias, dmixin, dproj, dprojc, *hosted.args)


def tail_fwd_bwd(x, mixin, tgt, mod3, g_post_mix, g_pre_mlp, g_post_mlp, wout, w1, w2):
    nb = x.shape[0]

    def body(x_ref, mi_ref, tgt_ref, mod_ref, gpm_ref, gpl_ref, gpo_ref, wo_ref, w1_ref, w2_ref,
             dx_ref, dmix_ref, h2_ref, du_ref, a_ref, dm_ref, dmi_ref, dmod_ref, dg_ref, loss_ref):
        b, t = pl.program_id(0), pl.program_id(1)
        gt1, sh2, sc2, gt2 = mod_ref[2:3, :], mod_ref[3:4, :], mod_ref[4:5, :], mod_ref[5:6, :]
        mix = jnp.dot(mi_ref[...], wo_ref[...], preferred_element_type=F32)
        (x1, h2), vjp_a = jax.vjp(_post_mix, x_ref[...], mix, gt1, sc2, sh2, gpm_ref[...], gpl_ref[...])
        h2b = h2.astype(BF16)
        h2_ref[...] = h2b
        m = jnp.zeros((TN, D), F32)
        relus = []
        for j in range(4):
            cols = slice(j * D, (j + 1) * D)
            r = jnp.maximum(jnp.dot(h2b, w1_ref[j], preferred_element_type=F32), 0.0)
            ab = (r * r).astype(BF16)
            a_ref[:, cols] = ab
            m = m + jnp.dot(ab, w2_ref[cols, :], preferred_element_type=F32)
            relus.append(r)
        loss, vjp_b = jax.vjp(_head_loss, x1, m, gt2, gpo_ref[...], tgt_ref[...])
        dx1, dm, dgt2, dgpo, _ = vjp_b(jnp.ones((1, 1), F32))
        dmb = dm.astype(BF16)
        dm_ref[...] = dmb
        dh2 = jnp.zeros((TN, D), F32)
        for j in range(4):
            cols = slice(j * D, (j + 1) * D)
            da = lax.dot_general(dmb, w2_ref[cols, :], (((1,), (1,)), ((), ())), preferred_element_type=F32)
            dub = (da * (2.0 * relus[j])).astype(BF16)
            du_ref[:, cols] = dub
            dh2 = dh2 + lax.dot_general(dub, w1_ref[j], (((1,), (1,)), ((), ())), preferred_element_type=F32)
        dx, dmix, dgt1, dsc2, dsh2, dgpm, dgpl = vjp_a((dx1, dh2))
        dx_ref[...] = dx
        dmixb = dmix.astype(BF16)
        dmix_ref[...] = dmixb
        dmi_ref[...] = lax.dot_general(dmixb, wo_ref[...], (((1,), (1,)), ((), ())),
                                       preferred_element_type=F32).astype(BF16)

        @pl.when(t == 0)
        def _():
            dmod_ref[...] = jnp.zeros_like(dmod_ref)

        @pl.when((t == 0) & (b == 0))
        def _():
            dg_ref[...] = jnp.zeros_like(dg_ref)
            loss_ref[...] = jnp.zeros_like(loss_ref)

        dmod_ref[2:3, :] += dgt1
        dmod_ref[3:4, :] += dsh2
        dmod_ref[4:5, :] += dsc2
        dmod_ref[5:6, :] += dgt2
        dg_ref[0:1, :] += dgpm
        dg_ref[1:2, :] += dgpl
        dg_ref[2:3, :] += dgpo
        loss_ref[...] += jnp.broadcast_to(loss, loss_ref.shape)

    tok = lambda b, t: (b, t, 0)
    const = lambda b, t: (0, 0)
    vec = pl.BlockSpec((1, D), const)
    return pl.pallas_call(
        body, name="tail_fwd_bwd", grid=(nb, SEQ // TN),
        in_specs=[
            pl.BlockSpec((None, TN, D), tok), pl.BlockSpec((None, TN, D), tok), pl.BlockSpec((None, TN, D), tok),
            pl.BlockSpec((None, 6, D), lambda b, t: (b, 0, 0)), vec, vec, vec,
            pl.BlockSpec((D, D), const, pipeline_mode=pl.Buffered(1)),
            pl.BlockSpec((4, D, D), lambda b, t: (0, 0, 0), pipeline_mode=pl.Buffered(1)),
            pl.BlockSpec((DFF, D), const, pipeline_mode=pl.Buffered(1)),
        ],
        out_specs=[
            pl.BlockSpec((None, TN, D), tok), pl.BlockSpec((None, TN, D), tok), pl.BlockSpec((None, TN, D), tok),
            pl.BlockSpec((None, TN, DFF), tok), pl.BlockSpec((None, TN, DFF), tok), pl.BlockSpec((None, TN, D), tok),
            pl.BlockSpec((None, TN, D), tok),
            pl.BlockSpec((None, 6, D), lambda b, t: (b, 0, 0)),
            pl.BlockSpec((8, D), const), pl.BlockSpec((8, 128), const),
        ],
        out_shape=[
            jax.ShapeDtypeStruct((nb, SEQ, D), F32), jax.ShapeDtypeStruct((nb, SEQ, D), BF16),
            jax.ShapeDtypeStruct((nb, SEQ, D), BF16), jax.ShapeDtypeStruct((nb, SEQ, DFF), BF16),
            jax.ShapeDtypeStruct((nb, SEQ, DFF), BF16), jax.ShapeDtypeStruct((nb, SEQ, D), BF16),
            jax.ShapeDtypeStruct((nb, SEQ, D), BF16),
            jax.ShapeDtypeStruct((nb, 6, D), F32), jax.ShapeDtypeStruct((8, D), F32),
            jax.ShapeDtypeStruct((8, 128), F32),
        ],
        compiler_params=_cp(("arbitrary", "arbitrary")),
    )(x, mixin, tgt, mod3, g_post_mix, g_pre_mlp, g_post_mlp, wout, w1, w2)


def weight_grad(pairs, name, out_dtype=F32, col_blocks=False, tm=1024, tn=1024, tk=2048):
    m, n = pairs[0][0].shape[1], pairs[0][1].shape[1]
    tn = min(tn, n)
    tks = [min(tk, xa.shape[0]) for xa, _ in pairs]
    steps = [xa.shape[0] // t for (xa, _), t in zip(pairs, tks)]
    total = sum(steps)
    offs = [sum(steps[:i]) for i in range(len(pairs))]

    def body(*refs):
        out_ref, acc = refs[2 * len(pairs)], refs[-1]
        k = pl.program_id(2)

        @pl.when(k == 0)
        def _():
            acc[...] = jnp.zeros_like(acc)

        for i in range(len(pairs)):
            @pl.when((k >= offs[i]) & (k < offs[i] + steps[i]))
            def _(i=i):
                acc[...] += lax.dot_general(refs[2 * i][...], refs[2 * i + 1][...], (((0,), (0,)), ((), ())),
                                            preferred_element_type=F32)

        if out_dtype != F32:
            @pl.when(k == total - 1)
            def _():
                out_ref[...] = acc[...].astype(out_dtype)

    in_specs, args = [], []
    for i, (xa, ya) in enumerate(pairs):
        clamp = lambda k, i=i: jnp.clip(k - offs[i], 0, steps[i] - 1)
        in_specs.append(pl.BlockSpec((tks[i], tm), lambda a, c, k, clamp=clamp: (clamp(k), a)))
        in_specs.append(pl.BlockSpec((tks[i], tn), lambda a, c, k, clamp=clamp: (clamp(k), c)))
        args += [xa, ya]
    if col_blocks:
        out_spec = pl.BlockSpec((None, tm, tn), lambda a, c, k: (c, a, 0))
        out_shape = jax.ShapeDtypeStruct((n // tn, m, tn), out_dtype)
    else:
        out_spec = pl.BlockSpec((tm, tn), lambda a, c, k: (a, c))
        out_shape = jax.ShapeDtypeStruct((m, n), out_dtype)
    return pl.pallas_call(
        body, name=name, grid=(m // tm, n // tn, total), in_specs=in_specs, out_specs=out_spec, out_shape=out_shape,
        scratch_shapes=[] if out_dtype == F32 else [pltpu.VMEM((tm, tn), F32)],
        compiler_params=_cp(("arbitrary", "arbitrary", "arbitrary")),
    )(*args)


def _perm_block(t):
    return 4 * (t % 4) + t // 4 if t < 16 else 16 + 3 * ((t - 16) % 4) + (t - 16) // 4


def _is_rope_block(p):
    return p < 16 and p % 4 < 2


def unpack_w_in(blocks):
    def body(i_ref, o_ref):
        for t in range(28):
            p = _perm_block(t)
            blk = i_ref[t // 7, :, (t % 7) * 128:(t % 7 + 1) * 128]
            if _is_rope_block(p):
                blk = _pair_order(blk.astype(F32)).astype(BF16)
            o_ref[:, p * 128:(p + 1) * 128] = blk

    return pl.pallas_call(
        body, name="unpack_w_in", grid=(2,),
        in_specs=[pl.BlockSpec((4, D // 2, 896), lambda i: (0, i, 0))],
        out_specs=pl.BlockSpec((D // 2, IN_W), lambda i: (i, 0)),
        out_shape=jax.ShapeDtypeStruct((D, IN_W), BF16),
    )(blocks)


def pack_w_in(dw):
    def body(i_ref, o_ref):
        for t in range(28):
            p = _perm_block(t)
            blk = i_ref[:, p * 128:(p + 1) * 128]
            if _is_rope_block(p):
                blk = _pair_order(blk)
            o_ref[t // 7, :, (t % 7) * 128:(t % 7 + 1) * 128] = blk.astype(BF16)

    return pl.pallas_call(
        body, name="pack_w_in", grid=(4,),
        in_specs=[pl.BlockSpec((D // 4, IN_W), lambda i: (i, 0))],
        out_specs=pl.BlockSpec((4, D // 4, 896), lambda i: (0, i, 0)),
        out_shape=jax.ShapeDtypeStruct((4, D, 896), BF16),
    )(dw)


def _place():
    return lax.axis_index("x"), lax.axis_index("y"), lax.axis_index("c")


class Hosted:
    def __init__(self, args, out_shape, scratch, start, finish):
        self.args, self.out_shape, self.scratch, self.start, self.finish = args, out_shape, scratch, start, finish

    def specs(self):
        hbm = pl.BlockSpec(memory_space=pl.ANY)
        return [hbm] * len(self.args), [hbm] * len(self.out_shape)

    def split(self, refs, n_in, n_out):
        a, b = len(self.args), len(self.out_shape)
        cuts = [n_in, n_in + a, n_in + a + n_out, n_in + a + n_out + b, len(refs) - len(self.scratch)]
        parts = [refs[i:j] for i, j in zip([0] + cuts, cuts + [len(refs)])]
        return parts[0], parts[1], parts[2], parts[3], parts[4], parts[5]


def no_exchange():
    return Hosted([], [], [], lambda *a: None, lambda *a: None)


def run_hosted(hosted, name):
    def body(*refs):
        _, ins, _, outs, _, sems = hosted.split(refs, 0, 0)
        hosted.start(ins, outs, sems)
        hosted.finish(ins, outs, sems)

    in_specs, out_specs = hosted.specs()
    return pl.pallas_call(body, name=name, in_specs=in_specs, out_specs=out_specs, out_shape=hosted.out_shape,
                          scratch_shapes=hosted.scratch)(*hosted.args)


def gather8(blocks):
    na = len(blocks)

    def copies(ins, outs, sems):
        send_sems, recv_sems, local_sem = sems
        x, y, c = _place()
        me, sibling = (x, y, c), (x, y, 1 - c)
        chips = [(1 - x, y), (x, 1 - y), (1 - x, 1 - y)]

        def slot(o_ref, px, py, pc):
            return o_ref.at[4 * px + 2 * py + pc]

        def copy(a, k, block, to, src=None):
            return pltpu.make_async_remote_copy(
                src_ref=slot(outs[a], *block) if src is None else src, dst_ref=slot(outs[a], *block),
                send_sem=send_sems.at[a, k], recv_sem=recv_sems.at[a, k], device_id=to, device_id_type=MESH)

        mine = [pltpu.make_async_copy(ins[a], slot(outs[a], *me), local_sem.at[a]) for a in range(na)]
        first = []
        for a in range(na):
            first.append(copy(a, 0, me, sibling, src=ins[a]))
            first += [copy(a, 1 + j, me, (*chip, c), src=ins[a]) for j, chip in enumerate(chips)]
        return copy, mine, first, me, sibling, chips, c

    def start(ins, outs, sems):
        _, mine, first, *_ = copies(ins, outs, sems)
        for cp in mine + first:
            cp.start()

    def finish(ins, outs, sems):
        copy, mine, first, me, sibling, chips, c = copies(ins, outs, sems)
        passed = []
        for j, chip in enumerate(chips):
            for a in range(na):
                copy(a, 1 + j, (*chip, c), me).wait_recv()
                cp = copy(a, 4 + j, (*chip, c), sibling)
                cp.start()
                passed.append(cp)
        for a in range(na):
            copy(a, 0, sibling, me).wait_recv()
            for j, chip in enumerate(chips):
                copy(a, 4 + j, (*chip, 1 - c), me).wait_recv()
        for cp in first + passed:
            cp.wait_send()
        for cp in mine:
            cp.wait()

    return Hosted(list(blocks), [jax.ShapeDtypeStruct((8,) + b.shape, b.dtype) for b in blocks],
                  [pltpu.SemaphoreType.DMA((na, 7)), pltpu.SemaphoreType.DMA((na, 7)), pltpu.SemaphoreType.DMA((na,))],
                  start, finish)


def chips3(arrays):
    na = len(arrays)

    def copies(ins, outs, sems):
        send_sems, recv_sems = sems
        x, y, c = _place()
        return [pltpu.make_async_remote_copy(
            src_ref=ins[a].at[2 * px + py], dst_ref=outs[a].at[k], send_sem=send_sems.at[a, k],
            recv_sem=recv_sems.at[a, k], device_id=(px, py, c), device_id_type=MESH)
            for a in range(na) for k, (px, py) in enumerate([(1 - x, y), (x, 1 - y), (1 - x, 1 - y)])]

    def start(ins, outs, sems):
        for cp in copies(ins, outs, sems):
            cp.start()

    def finish(ins, outs, sems):
        for cp in copies(ins, outs, sems):
            cp.wait()

    return Hosted(list(arrays), [jax.ShapeDtypeStruct((3,) + a.shape[1:], a.dtype) for a in arrays],
                  [pltpu.SemaphoreType.DMA((na, 3)), pltpu.SemaphoreType.DMA((na, 3))], start, finish)


def siblings(arrays):
    na = len(arrays)

    def copies(ins, outs, sems):
        send_sems, recv_sems = sems
        x, y, c = _place()
        return [pltpu.make_async_remote_copy(
            src_ref=ins[a], dst_ref=outs[a], send_sem=send_sems.at[a], recv_sem=recv_sems.at[a],
            device_id=(x, y, 1 - c), device_id_type=MESH) for a in range(na)]

    def start(ins, outs, sems):
        for cp in copies(ins, outs, sems):
            cp.start()

    def finish(ins, outs, sems):
        for cp in copies(ins, outs, sems):
            cp.wait()

    return Hosted(list(arrays), [jax.ShapeDtypeStruct(a.shape, a.dtype) for a in arrays],
                  [pltpu.SemaphoreType.DMA((na,)), pltpu.SemaphoreType.DMA((na,))], start, finish)


def both(first, second):
    na, no, ns = len(first.args), len(first.out_shape), len(first.scratch)

    def start(ins, outs, sems):
        first.start(ins[:na], outs[:no], sems[:ns])
        second.start(ins[na:], outs[no:], sems[ns:])

    def finish(ins, outs, sems):
        first.finish(ins[:na], outs[:no], sems[:ns])
        second.finish(ins[na:], outs[no:], sems[ns:])

    return Hosted(first.args + second.args, first.out_shape + second.out_shape, first.scratch + second.scratch,
                  start, finish)


def siblings4(arrays):
    na = len(arrays)

    def copies(ins, outs, sems):
        send_sems, recv_sems = sems
        x, y, c = _place()
        return [pltpu.make_async_remote_copy(
            src_ref=ins[a].at[2 * j + 1 - c], dst_ref=outs[a].at[j],
            send_sem=send_sems.at[a, j], recv_sem=recv_sems.at[a, j],
            device_id=(x, y, 1 - c), device_id_type=MESH) for a in range(na) for j in range(4)]

    def start(ins, outs, sems):
        for cp in copies(ins, outs, sems):
            cp.start()

    def finish(ins, outs, sems):
        for cp in copies(ins, outs, sems):
            cp.wait()

    return Hosted(list(arrays), [jax.ShapeDtypeStruct((4,) + a.shape[1:], a.dtype) for a in arrays],
                  [pltpu.SemaphoreType.DMA((na, 4)), pltpu.SemaphoreType.DMA((na, 4))], start, finish)


def _row_tile(r):
    for cand in (512, 256, 128, 64, 32, 16, 8):
        if r % cand == 0:
            return cand
    return r


def chip_partial(place, g8s, landed4s, name):
    n = len(g8s)

    def body(place_ref, *refs):
        del place_ref
        for g_ref, l_ref, o_ref in zip(refs[:n], refs[n:2 * n], refs[2 * n:]):
            o_ref[...] = (g_ref[...].astype(F32) + l_ref[...].astype(F32)).astype(BF16)

    own = [pl.BlockSpec((None,) + g.shape[1:], lambda j, s: (2 * j + s[0], 0, 0)) for g in g8s]
    plain = [pl.BlockSpec((None,) + g.shape[1:], lambda j, s: (j, 0, 0)) for g in g8s]
    return pl.pallas_call(
        body, name=name,
        grid_spec=pltpu.PrefetchScalarGridSpec(num_scalar_prefetch=1, grid=(4,), in_specs=own + plain, out_specs=plain),
        out_shape=[jax.ShapeDtypeStruct((4,) + g.shape[1:], BF16) for g in g8s],
    )(place, *g8s, *landed4s)


def shard_sum(place, partial4s, landed3s, name):
    n = len(partial4s)

    def body(place_ref, *refs):
        del place_ref
        for p_ref, l_ref, o_ref in zip(refs[:n], refs[n:2 * n], refs[2 * n:]):
            acc = p_ref[...].astype(F32)
            for k in range(3):
                acc = acc + l_ref[k].astype(F32)
            o_ref[...] = acc

    def halves(p, lead):
        r, ccols = p.shape[1:]
        return (lead, r // 2, ccols)

    return pl.pallas_call(
        body, name=name,
        grid_spec=pltpu.PrefetchScalarGridSpec(
            num_scalar_prefetch=1, grid=(2,),
            in_specs=[pl.BlockSpec(halves(p, None), lambda i, s: (s[1], i, 0)) for p in partial4s]
            + [pl.BlockSpec(halves(p, 3), lambda i, s: (0, i, 0)) for p in partial4s],
            out_specs=[pl.BlockSpec(halves(p, None)[1:], lambda i, s: (i, 0)) for p in partial4s]),
        out_shape=[jax.ShapeDtypeStruct(p.shape[1:], F32) for p in partial4s],
    )(place, *partial4s, *landed3s)


def _adamw_math(w, g, m, v):
    m2 = B1 * m + (1.0 - B1) * g
    v2 = B2 * v + (1.0 - B2) * (g * g)
    m_hat = m2 / (1.0 - B1 ** STEP)
    v_hat = v2 / (1.0 - B2 ** STEP)
    return -LR * (m_hat / (jnp.sqrt(v_hat) + AEPS) + WD * w), m2, v2


def adamw_halves(place, w, mine, theirs, m, v, name):
    r, ccols = w.shape
    hr = r // 2
    tr = _row_tile(hr)
    nt = hr // tr

    def body(place_ref, w_ref, a_ref, b_ref, m_ref, v_ref, g_out, d_out, m_out, v_out):
        g = jnp.where(pl.program_id(0) == place_ref[0], a_ref[...], b_ref[...])
        d, m2, v2 = _adamw_math(w_ref[...], g, m_ref[...], v_ref[...])
        g_out[...] = g
        d_out[...] = d
        m_out[...] = m2
        v_out[...] = v2

    full = pl.BlockSpec((tr, ccols), lambda h, i, s: (h * nt + i, 0))
    part = pl.BlockSpec((tr, ccols), lambda h, i, s: (i, 0))
    return pl.pallas_call(
        body, name=name,
        grid_spec=pltpu.PrefetchScalarGridSpec(
            num_scalar_prefetch=1, grid=(2, nt), in_specs=[full, part, part, full, full], out_specs=[full] * 4),
        out_shape=[jax.ShapeDtypeStruct((r, ccols), F32)] * 4,
    )(place, w, mine, theirs, m, v)


def adamw_group(place, halved, plain, hosted, name):
    rows = halved[0][0].shape[0]
    tr = 128
    nt = rows // 2 // tr
    nh, npl = len(halved), len(plain)

    def body(place_ref, *refs):
        own_in, h_in, own_out, h_out, _, h_sems = hosted.split(refs, 5 * nh + 4 * npl, 4 * nh + 3 * npl)
        half = pl.program_id(0)
        grid_step = half * nt + pl.program_id(1)

        @pl.when(grid_step == 0)
        def _():
            hosted.start(h_in, h_out, h_sems)

        for i in range(nh):
            w_ref, a_ref, b_ref, m_ref, v_ref = own_in[5 * i:5 * i + 5]
            g = jnp.where(half == place_ref[0], a_ref[...], b_ref[...])
            res = (g,) + _adamw_math(w_ref[...], g, m_ref[...], v_ref[...])
            for o_ref, r in zip(own_out[4 * i:4 * i + 4], res):
                o_ref[...] = r
        for i in range(npl):
            w_ref, g_ref, m_ref, v_ref = own_in[5 * nh + 4 * i:5 * nh + 4 * i + 4]
            res = _adamw_math(w_ref[...], g_ref[...], m_ref[...], v_ref[...])
            for o_ref, r in zip(own_out[4 * nh + 3 * i:4 * nh + 3 * i + 3], res):
                o_ref[...] = r

        @pl.when(grid_step == 2 * nt - 1)
        def _():
            hosted.finish(h_in, h_out, h_sems)

    def full(cols):
        return pl.BlockSpec((tr, cols), lambda h, i, s: (h * nt + i, 0))

    def part(cols):
        return pl.BlockSpec((tr, cols), lambda h, i, s: (i, 0))

    in_specs, out_specs, out_shape, args = [], [], [], []
    for w, a, b, m, v in halved:
        cols = w.shape[1]
        in_specs += [full(cols), part(cols), part(cols), full(cols), full(cols)]
        out_specs += [full(cols)] * 4
        out_shape += [jax.ShapeDtypeStruct(w.shape, F32)] * 4
        args += [w, a, b, m, v]
    for w, g, m, v in plain:
        cols = w.shape[1]
        in_specs += [full(cols)] * 4
        out_specs += [full(cols)] * 3
        out_shape += [jax.ShapeDtypeStruct(w.shape, F32)] * 3
        args += [w, g, m, v]
    h_in_specs, h_out_specs = hosted.specs()
    return pl.pallas_call(
        body, name=name,
        grid_spec=pltpu.PrefetchScalarGridSpec(
            num_scalar_prefetch=1, grid=(2, nt), in_specs=in_specs + h_in_specs, out_specs=out_specs + h_out_specs,
            scratch_shapes=hosted.scratch),
        out_shape=out_shape + hosted.out_shape,
        compiler_params=_cp(("arbitrary", "arbitrary")),
    )(place, *args, *hosted.args)


def _silu(x):
    return x * jax.nn.sigmoid(x)


def prologue(c_rows, c_ctx_row, w_ada, b_shard, rpb_flat, half_w_in, late_shards):
    shape = jax.ShapeDtypeStruct
    n_late = len(late_shards)
    half_shapes = [(w.shape[0] // 2, w.shape[1]) for w in late_shards]
    g_w = gather8([half_w_in])
    g_c = gather8([shape((8, D), F32)])
    g_m = gather8([shape((32, 1536), F32)])

    def body(*refs):
        c_ref, cc_ref, w_ref, b_ref, flat_ref, hw_ref = refs[:6]
        late_refs = refs[6:6 + n_late]
        cin_ref, mg_ref, gw_ref, bias_ref, cos_ref, sin_ref = refs[6 + n_late:12 + n_late]
        rest = refs[12 + n_late:]
        half_refs, (cg_s, ms_s, bias_s) = rest[:n_late], rest[n_late:n_late + 3]
        stage, (load_sem, bias_sem), sems = rest[n_late + 3:2 * n_late + 3], rest[2 * n_late + 3:2 * n_late + 5], \
            rest[2 * n_late + 5:]
        sw, sc, sm = sems[0:3], sems[3:6], sems[6:9]
        core = lax.axis_index("c")
        g_c.start([c_ref], [cg_s], sc)
        loads = [pltpu.make_async_copy(late_refs[a].at[pl.ds(core * half_shapes[a][0], half_shapes[a][0]), :],
                                       stage[a], load_sem.at[a]) for a in range(n_late)]
        for cp in loads:
            cp.start()
        g_c.finish([c_ref], [cg_s], sc)
        cin_ref[...] = jnp.zeros_like(cin_ref)
        for dev in range(8):
            cin_ref[2 * dev:2 * dev + 2, :] = cg_s[dev, 0:2, :]
        cin_ref[16:17, :] = cc_ref[...]
        ms_s[...] = _nn(_silu(cin_ref[...]), w_ref[...]) + b_ref[...]
        g_m.start([ms_s], [mg_ref], sm)
        g_w.start([hw_ref], [gw_ref], sw)
        for a, cp in enumerate(loads):
            cp.wait()
            half_refs[a][...] = stage[a][...].astype(BF16)
        cos_ref[...], sin_ref[...] = _rope_tables()
        stores = []
        for pair in range(NPAIR):
            if pair >= 2:
                stores[pair - 2].wait()
            _na_bias_pair(flat_ref.at[pair], bias_s.at[pair % 2])
            stores.append(pltpu.make_async_copy(bias_s.at[pair % 2], bias_ref.at[pair], bias_sem.at[pair % 2]))
            stores[pair].start()
        for cp in stores[-2:]:
            cp.wait()
        g_m.finish([ms_s], [mg_ref], sm)
        g_w.finish([hw_ref], [gw_ref], sw)

    vmem = pl.BlockSpec(memory_space=pltpu.VMEM)
    hbm = pl.BlockSpec(memory_space=pl.ANY)
    return pl.pallas_call(
        body, name="prologue", in_specs=[vmem, vmem, vmem, vmem, vmem, hbm] + [hbm] * n_late,
        out_specs=[vmem, vmem, hbm, hbm, vmem, vmem] + [vmem] * n_late,
        out_shape=[shape((32, D), F32), shape((8, 32, 1536), F32)] + g_w.out_shape
        + [shape((NPAIR,) + NA_BIAS_SHAPE, F32)] + [shape((SEQ, RD), F32)] * 2 + [shape(s, BF16) for s in half_shapes],
        scratch_shapes=[pltpu.VMEM((8, 8, D), F32), pltpu.VMEM((32, 1536), F32), pltpu.VMEM((2,) + NA_BIAS_SHAPE, F32)]
        + [pltpu.VMEM(s, F32) for s in half_shapes]
        + [pltpu.SemaphoreType.DMA((n_late,)), pltpu.SemaphoreType.DMA((2,))]
        + g_w.scratch + g_c.scratch + g_m.scratch,
        compiler_params=_cp(),
    )(c_rows, c_ctx_row, w_ada, b_shard, rpb_flat, half_w_in, *late_shards)


def ada_grads(cin, gb, gc, w_ada):
    def body(c_ref, gb_ref, gc_ref, w_ref, gw_ref, pc_ref):
        ctx_tot = jnp.sum(gc_ref[...], axis=0, keepdims=True)
        rows = lax.broadcasted_iota(jnp.int32, (16, 512), 0)
        dm = jnp.concatenate([gb_ref[...], jnp.where(rows == 0, ctx_tot, 0.0)], axis=0)
        gw_ref[...] = _tn(_silu(c_ref[...]), dm)
        rows8 = lax.broadcasted_iota(jnp.int32, (8, 512), 0)
        part = _nt(jnp.where(rows8 == 0, ctx_tot, 0.0), w_ref[...])

        @pl.when(pl.program_id(0) == 0)
        def _():
            pc_ref[...] = jnp.zeros_like(pc_ref)

        pc_ref[...] += part

    return pl.pallas_call(
        body, name="ada_grads", grid=(3,),
        in_specs=[pl.BlockSpec((32, D), lambda j: (0, 0)), pl.BlockSpec((16, 512), lambda j: (0, j)),
                  pl.BlockSpec((8, 512), lambda j: (0, j)), pl.BlockSpec((D, 512), lambda j: (0, j))],
        out_specs=[pl.BlockSpec((D, 512), lambda j: (0, j)), pl.BlockSpec((8, D), lambda j: (0, 0))],
        out_shape=[jax.ShapeDtypeStruct((D, 1536), F32), jax.ShapeDtypeStruct((8, D), F32)],
    )(cin, gb, gc, w_ada)


SMALL_SUM_ROWS = 15


def small_update(gsm, gbf, gcf, pcg, params):
    n = len(params)

    def body(*refs):
        gsm_ref, gbf_ref, gcf_ref, pcg_ref = refs[:4]
        wmv, outs, loss_out = refs[4:4 + 3 * n], refs[4 + 3 * n:4 + 7 * n], refs[-1]
        acc = gsm_ref[0]
        for dev in range(1, 8):
            acc = acc + gsm_ref[dev]
        c_ctx = wmv[0][...]
        sg = jax.nn.sigmoid(c_ctx)
        dsilu = pcg_ref[0:1, :] + pcg_ref[2:3, :] + pcg_ref[4:5, :] + pcg_ref[6:7, :]
        lane = lax.broadcasted_iota(jnp.int32, (1, D), 1)
        last = acc[14:15, :]
        grads = [
            dsilu * (sg * (1.0 + c_ctx * (1.0 - sg))),
            jnp.sum(gbf_ref[...], axis=0, keepdims=True) + jnp.sum(gcf_ref[...], axis=0, keepdims=True),
            acc[0:1, :] + acc[1:2, :], acc[2:3, :], acc[3:4, :], acc[4:5, :],
            acc[5:6, 0:512], acc[6:14, :], jnp.where(lane < 8, last, 0.0),
        ]
        loss_out[...] = jnp.broadcast_to(jnp.sum(jnp.where(lane == 8, last, 0.0), axis=1, keepdims=True), (8, 128))
        for i, g in enumerate(grads):
            d, m2, v2 = _adamw_math(wmv[3 * i][...], g, wmv[3 * i + 1][...], wmv[3 * i + 2][...])
            outs[4 * i][...] = g
            outs[4 * i + 1][...] = d
            outs[4 * i + 2][...] = m2
            outs[4 * i + 3][...] = v2

    flat = [a for wmv in params for a in wmv]
    out_shape = [jax.ShapeDtypeStruct(w.shape, F32) for w, _, _ in params for _ in range(4)]
    return pl.pallas_call(
        body, name="small_update", out_shape=out_shape + [jax.ShapeDtypeStruct((8, 128), F32)],
    )(gsm, gbf, gcf, pcg, *flat)


def _pad_row(v, rows):
    flat = v.reshape(-1)
    return jnp.pad(flat, (0, rows * D - flat.shape[0])).reshape(rows, D)


def local_step(x, ctx, tgt, mod3, rope, bias, g_pre_mix, g_post_mix, g_pre_mlp, g_post_mlp, ret_decay, ret_gn,
               wperm, late_weights, early_grads):
    nb = x.shape[0]
    tokens = nb * SEQ
    cos, sin = rope
    rd = ret_decay.T.reshape(RH, 2, 1)
    gn = ret_gn.reshape(RH, 1, RD)
    h, pret, pna = premix_proj(x, mod3, g_pre_mix, wperm, False, "premix_proj")
    hc, pretc, pnac = premix_proj(ctx, mod3, g_pre_mix, wperm, True, "premix_proj_ctx")
    o_all, mixin, gw_out = retention_fwd(pret, pretc, rd, gn, cos, sin, late_weights(0))
    mixin, gw1, gw2 = na_fwd(pna, pnac, bias, mixin, late_weights(1))
    dx_tail, dmix, h2, du, act, dm, dmixin, dmod_t, dg_t, loss_t = tail_fwd_bwd(
        x, mixin, tgt, mod3, g_post_mix, g_pre_mlp, g_post_mlp, gw_out.reshape(D, D), gw1.reshape(4, D, D),
        gw2.reshape(DFF, D))
    dw_out = weight_grad([(mixin.reshape(tokens, D), dmix.reshape(tokens, D))], "grad_w_out", BF16)
    dw1 = weight_grad([(h2.reshape(tokens, D), du.reshape(tokens, DFF))], "grad_w_mlp1", BF16, col_blocks=True)
    dw2 = weight_grad([(act.reshape(tokens, DFF), dm.reshape(tokens, D))], "grad_w_mlp2", BF16)
    dproj, dprojc, drd, dgn, *landed = retention_bwd(pret, pretc, o_all, dmixin, rd, gn, cos, sin,
                                                     early_grads[0](dw_out, dw1, dw2))
    dproj, dprojc, dpat, *early = na_bwd(pna, pnac, bias, dmixin, dproj, dprojc, early_grads[1](landed))
    dw_in = weight_grad([(h.reshape(tokens, D), dproj.reshape(tokens, IN_W)),
                         (hc.reshape(nb * LC, D), dprojc.reshape(nb * LC, IN_W))], "grad_w_in", tn=IN_W // 2, tk=1024)
    dmod_c, dg_c, *late = premix_bwd(ctx, mod3, g_pre_mix, wperm, dprojc, None, early_grads[2](dw_in), "premix_bwd_ctx")
    grad_x, dmod_a, dg_a, *late = premix_bwd(x, mod3, g_pre_mix, wperm, dproj, dx_tail, early_grads[3](late),
                                             "premix_bwd")
    dmod = jnp.concatenate([jnp.concatenate([dmod_a[:, 0:2], dmod_t[:, 2:6]], axis=1), dmod_c], axis=0)
    last = jnp.pad(jnp.concatenate([drd[:, :, 0].T.reshape(8), loss_t[0, 0:1]]), (0, D - 9)).reshape(1, D)
    small = jnp.concatenate([dg_a[0:1], dg_c[0:1], dg_t[0:3], _pad_row(dgn, 1), dpat.reshape(8, D), last], axis=0)
    return grad_x, late, early, dmod, small


def kernel(x, c, ctx, c_ctx, w_ada, b_ada, g_pre_mix, g_post_mix, g_pre_mlp, g_post_mlp, w_in, ret_decay, ret_gn, na_rpb, w_out, w_mlp1, w_mlp2, loss_target, m_c_ctx, m_w_ada, m_b_ada, m_g_pre_mix, m_g_post_mix, m_g_pre_mlp, m_g_post_mlp, m_w_in, m_ret_decay, m_ret_gn, m_na_rpb, m_w_out, m_w_mlp1, m_w_mlp2, v_c_ctx, v_w_ada, v_b_ada, v_g_pre_mix, v_g_post_mix, v_g_pre_mlp, v_g_post_mlp, v_w_in, v_ret_decay, v_ret_gn, v_na_rpb, v_w_out, v_w_mlp1, v_w_mlp2):
    px, py, pc = _place()
    dev = 4 * px + 2 * py + pc
    chip = 2 * px + py

    half_w_in = lax.dynamic_slice_in_dim(w_in[0], pc * (D // 2), D // 2, 0).astype(BF16)
    cin, mg, gw_in, bias, cos, sin, *late_halves = prologue(
        jnp.pad(c, ((0, 6), (0, 0))), c_ctx[None], w_ada[0], lax.dynamic_slice_in_dim(b_ada, chip * 1536, 1536, 1),
        _rpb_flat(na_rpb[0]), half_w_in, [w_out[0], w_mlp1[0], w_mlp2[0]])
    halves = [half_w_in] + late_halves
    wperm = unpack_w_in(gw_in.reshape(4, D, 896))
    mod_all = jnp.concatenate([mg[0], mg[2], mg[4], mg[6]], axis=1)
    mod3 = (jnp.pad(lax.dynamic_slice_in_dim(mod_all, 2 * dev, 2, 0), ((0, 1), (0, 0)))
            + jnp.pad(mod_all[16:17], ((2, 0), (0, 0)))).reshape(3, 6, D)

    place = jnp.stack([pc, chip]).astype(jnp.int32)

    early_names = ["w_out", "w_mlp1", "w_mlp2"]
    early_g8, early_partial = [], []

    def early_a(dw_out, dw1, dw2):
        early_g8[:] = [dw_out.reshape(8, 128, D), dw1.reshape(8, 512, D), dw2.reshape(8, 512, D)]
        return siblings4(early_g8)

    def early_b(landed):
        early_partial[:] = chip_partial(place, early_g8, landed, "rs_chip_sum_early")
        return chips3(early_partial)

    late_partial = []

    late_g8 = []

    def late_c(dw_in):
        late_g8[:] = [pack_w_in(dw_in).reshape(8, 512, 896)]
        return siblings4(late_g8)

    def late_d(landed):
        late_partial[:] = chip_partial(place, late_g8, landed, "rs_chip_sum_w_in")
        return chips3(late_partial)

    grad_x, (landed3_in,), early_landed, dmod, small = local_step(
        x, ctx, loss_target, mod3, (cos, sin), bias, g_pre_mix, g_post_mix, g_pre_mlp, g_post_mlp, ret_decay[0], ret_gn,
        wperm, lambda k: gather8(halves[1:2] if k == 0 else halves[2:4]), (early_a, early_b, late_c, late_d))
    early_mine = shard_sum(place, early_partial, early_landed, "rs_shard_sum_early")

    pay = jnp.concatenate([dmod.reshape(18, D), small, jnp.zeros((40 - 18 - SMALL_SUM_ROWS, D), F32)], axis=0)
    *early_theirs, gs = run_hosted(both(siblings(early_mine), gather8([pay])), "rs_halves_early_gather_small")
    gbf = gs[:, 0:12].reshape(16, 6 * D)
    gcf = gs[:, 12:18].reshape(8, 6 * D)
    gw_ada, pc_part = ada_grads(cin, lax.dynamic_slice_in_dim(gbf, chip * 1536, 1536, 1),
                                lax.dynamic_slice_in_dim(gcf, chip * 1536, 1536, 1), w_ada[0])
    (mine_in,) = shard_sum(place, late_partial, [landed3_in], "rs_shard_sum_w_in")
    theirs_in, pcg = run_hosted(both(siblings([mine_in]), gather8([pc_part])), "rs_halves_w_in_gather_c_ctx")

    grouped = adamw_group(
        place,
        [(w_mlp1[0], early_mine[1], early_theirs[1], m_w_mlp1[0], v_w_mlp1[0]),
         (w_mlp2[0], early_mine[2], early_theirs[2], m_w_mlp2[0], v_w_mlp2[0])],
        [(w_ada[0], gw_ada, m_w_ada[0], v_w_ada[0])], no_exchange(), "adamw_group")
    d_ada, m_ada, v_ada = grouped[8:11]
    big = [
        [r[None] for r in adamw_halves(place, w_in[0], mine_in, theirs_in, m_w_in[0], v_w_in[0], "adamw_w_in")],
        [r[None] for r in adamw_halves(place, w_out[0], early_mine[0], early_theirs[0], m_w_out[0], v_w_out[0],
                                       "adamw_w_out")],
        [r[None] for r in grouped[0:4]], [r[None] for r in grouped[4:8]],
    ]

    def rpb_rows(t):
        return _rpb_flat(t[0]).reshape(8, D)

    def decay_row(t):
        return jnp.pad(t.reshape(1, 8), ((0, 0), (0, D - 8)))

    views = [lambda t: t.reshape(1, D), lambda t: t, lambda t: t, lambda t: t, lambda t: t, lambda t: t, lambda t: t,
             rpb_rows, decay_row]
    back = [lambda t: t.reshape(D), lambda t: t, lambda t: t, lambda t: t, lambda t: t, lambda t: t, lambda t: t,
            lambda t: _rpb_flat_t(t)[None], lambda t: t[:, 0:8].reshape(1, 2, 4)]
    small_w = (c_ctx, b_ada, g_pre_mix, g_post_mix, g_pre_mlp, g_post_mlp, ret_gn, na_rpb, ret_decay)
    small_m = (m_c_ctx, m_b_ada, m_g_pre_mix, m_g_post_mix, m_g_pre_mlp, m_g_post_mlp, m_ret_gn, m_na_rpb, m_ret_decay)
    small_v = (v_c_ctx, v_b_ada, v_g_pre_mix, v_g_post_mix, v_g_pre_mlp, v_g_post_mlp, v_ret_gn, v_na_rpb, v_ret_decay)
    *res, loss8 = small_update(gs[:, 18:18 + SMALL_SUM_ROWS], gbf, gcf, pcg[:, 0],
                               [(f(w), f(m), f(v)) for f, w, m, v in zip(views, small_w, small_m, small_v)])

    def leaves(ada, idx):
        s_c, s_b, s_g1, s_g2, s_g3, s_g4, s_gn, s_rpb, s_rd = [back[i](res[4 * i + idx]) for i in range(9)]
        return [s_c, ada[None], s_b, s_g1, s_g2, s_g3, s_g4, big[0][idx], s_rd, s_gn, s_rpb,
                big[1][idx], big[2][idx], big[3][idx]]

    return (loss8[0, 0], grad_x, *leaves(gw_ada, 0), *leaves(d_ada, 1), *leaves(m_ada, 2), *leaves(v_ada, 3))
```

```python
import functools
import math

import jax
import jax.numpy as jnp
from jax import lax
from jax.experimental import pallas as pl
from jax.experimental.pallas import tpu as pltpu

F32, BF16 = jnp.float32, jnp.bfloat16
D = 1024
SEQ = 2048
LC = 256
GW = 64
RH, RD, CH = 4, 128, 128
NPAIR = 4
IN_W = 3584
RET_W = 2048
DFF = 4096
EPS = 1e-6
NEG = -1e30
TN = 256
NCH = SEQ // CH
LR, B1, B2, AEPS, WD, STEP = 0.001, 0.9, 0.999, 1e-08, 0.01, 10
MESH = pl.DeviceIdType.MESH
VMEM_LIMIT = 56 * 1024 * 1024


def _cp(sem=None):
    return pltpu.CompilerParams(dimension_semantics=sem, vmem_limit_bytes=VMEM_LIMIT)


def _nn(a, b):
    return jnp.dot(a.astype(BF16), b.astype(BF16), preferred_element_type=F32)


def _nt(a, b):
    return lax.dot_general(a.astype(BF16), b.astype(BF16), (((1,), (1,)), ((), ())), preferred_element_type=F32)


def _tn(a, b):
    return lax.dot_general(a.astype(BF16), b.astype(BF16), (((0,), (0,)), ((), ())), preferred_element_type=F32)


@jax.custom_vjp
def mm_tn(a, b):
    return _tn(a, b)


mm_tn.defvjp(lambda a, b: (_tn(a, b), (a, b)), lambda r, g: (_nt(r[1], g), _nn(r[0], g)))


def _rms(x):
    return x * lax.rsqrt(jnp.mean(x * x, axis=-1, keepdims=True) + EPS)


def _rms_mod(x, g, sc, sh):
    return (_rms(x) * g) * (1.0 + sc) + sh


def _post_mix(x, mix, gt1, sc2, sh2, g_post_mix, g_pre_mlp):
    x1 = x + gt1 * (_rms(mix) * g_post_mix)
    return x1, _rms_mod(x1, g_pre_mlp, sc2, sh2)


def _head_loss(x1, m, gt2, g_post_mlp, tgt):
    err = x1 + gt2 * (_rms(m) * g_post_mlp) - tgt
    return 0.5 * jnp.sum(jnp.mean(err * err, axis=-1, keepdims=True), axis=0, keepdims=True)


def _ln_gate(o, g, w):
    mu = jnp.mean(o, axis=-1, keepdims=True)
    var = jnp.mean(jnp.square(o - mu), axis=-1, keepdims=True)
    y = (o - mu) * lax.rsqrt(var + EPS)
    return (y * w) * (g * jax.nn.sigmoid(g))


def _pair_order(x):
    lane = lax.broadcasted_iota(jnp.int32, x.shape, 1)
    return jnp.where((lane >= 32) & (lane < 64), pltpu.roll(x, 96, 1),
                     jnp.where((lane >= 64) & (lane < 96), pltpu.roll(x, 32, 1), x))


def _rope(x, cos, sin):
    return x * cos + pltpu.roll(x, 64, 1) * sin


def _rope_t(g, cos, sin):
    return g * cos + pltpu.roll(g * sin, 64, 1)


def _rope_tables():
    tok = lax.broadcasted_iota(jnp.int32, (SEQ, RD), 0)
    lane = lax.broadcasted_iota(jnp.int32, (SEQ, RD), 1)
    pos = jnp.where((lane & 32) == 0, tok >> 6, tok & (GW - 1)).astype(F32)
    ang = pos * jnp.exp((lane & 31).astype(F32) * (-math.log(10000.0) / 32))
    return jnp.cos(ang), jnp.where(lane < 64, -jnp.sin(ang), jnp.sin(ang))


def _chunk_loop(n, body, init, k=4):
    def several(t, carry):
        for i in range(k):
            carry = body(k * t + i, carry)
        return carry

    return lax.fori_loop(0, n // k, several, init)


def _fiota(shape, dim):
    return lax.broadcasted_iota(jnp.int32, shape, dim).astype(F32)


def _ret_state(k, v, s, lg, reverse):
    pos = _fiota((CH, 1), 0)
    b_exp = pos if reverse else (CH - 1.0 - pos)
    return jnp.exp(lg * CH) * s + mm_tn(k * jnp.exp(lg * b_exp), v)


class _Decays:
    def __init__(self, lgs):
        i, j, pos = _fiota((CH, CH), 0), _fiota((CH, CH), 1), _fiota((CH, 1), 0)
        diffs = (i - j, j - i)
        keep = (diffs[0] >= 0, diffs[1] > 0)
        mats = [jnp.where(m, jnp.exp(lg * jnp.where(m, d, 0.0)), 0.0) for lg, d, m in zip(lgs, diffs, keep)]
        self.mask = mats[0] + mats[1]
        self.dmask = [mats[0] * diffs[0], mats[1] * diffs[1]]
        a_exp, b_exp = (pos + 1.0, CH - pos), (CH - 1.0 - pos, pos)
        self.a = [jnp.exp(lg * e) for lg, e in zip(lgs, a_exp)]
        self.b = [jnp.exp(lg * e) for lg, e in zip(lgs, b_exp)]
        self.da = [a * e for a, e in zip(self.a, a_exp)]
        self.db = [b * e for b, e in zip(self.b, b_exp)]
        self.g = [jnp.exp(lg * CH) for lg in lgs]


def _both(x, w):
    return jnp.concatenate([x * w[0], x * w[1]], axis=1)


def _total(x):
    return jnp.sum(jnp.sum(x, axis=1, keepdims=True), axis=0, keepdims=True)


def _state_pass(dec, init, k_s, v_of, st_s):
    def step(t, carry):
        out = []
        for d, s in enumerate(carry):
            n = (NCH - 1 - t) if d else t
            sl = pl.ds(pl.multiple_of(n * CH, CH), CH)
            st_s[n, d * RD:(d + 1) * RD, :] = s
            out.append(dec.g[d] * s + _tn(k_s[sl, :] * dec.b[d], v_of(sl)))
        return tuple(out)

    _chunk_loop(NCH, step, tuple(init))


def premix_proj(xin, mod3, g_pre, wperm, is_ctx, name):
    nb, length, _ = xin.shape
    tn = min(2 * TN, length)

    def body(x_ref, mod_ref, g_ref, w_ref, h_ref, pret_ref, pna_ref):
        h = _rms_mod(x_ref[...], g_ref[...], mod_ref[1:2, :], mod_ref[0:1, :])
        hb = h.astype(BF16)
        h_ref[...] = hb
        pret_ref[...] = jnp.dot(hb, w_ref[:, :RET_W], preferred_element_type=F32)
        pna_ref[...] = jnp.dot(hb, w_ref[:, RET_W:], preferred_element_type=F32).astype(BF16)

    return pl.pallas_call(
        body, name=name, grid=(nb, length // tn),
        in_specs=[
            pl.BlockSpec((None, tn, D), lambda b, t: (b, t, 0)),
            pl.BlockSpec((None, 6, D), (lambda b, t: (2, 0, 0)) if is_ctx else (lambda b, t: (b, 0, 0))),
            pl.BlockSpec((1, D), lambda b, t: (0, 0)),
            pl.BlockSpec((D, IN_W), lambda b, t: (0, 0), pipeline_mode=pl.Buffered(1)),
        ],
        out_specs=[
            pl.BlockSpec((None, tn, D), lambda b, t: (b, t, 0)),
            pl.BlockSpec((None, tn, RET_W), lambda b, t: (b, t, 0)),
            pl.BlockSpec((None, tn, IN_W - RET_W), lambda b, t: (b, t, 0)),
        ],
        out_shape=[
            jax.ShapeDtypeStruct((nb, length, D), BF16),
            jax.ShapeDtypeStruct((nb, length, RET_W), F32),
            jax.ShapeDtypeStruct((nb, length, IN_W - RET_W), BF16),
        ],
        compiler_params=_cp(("arbitrary", "arbitrary")),
    )(xin, mod3, g_pre, wperm)


def premix_bwd(xin, mod3, g_pre, wperm, dproj, dx_tail, hosted, name):
    nb, length, _ = xin.shape
    tn = min(TN, length)
    is_ctx = dx_tail is None

    def body(*refs):
        own_in, h_in, own_out, h_out, _, h_sems = hosted.split(refs, 5 if is_ctx else 6, 2 if is_ctx else 3)
        if is_ctx:
            (x_ref, mod_ref, g_ref, w_ref, dp_ref), (dmod_ref, dg_ref) = own_in, own_out
        else:
            (x_ref, mod_ref, g_ref, w_ref, dp_ref, dxt_ref), (dx_ref, dmod_ref, dg_ref) = own_in, own_out
        b, t = pl.program_id(0), pl.program_id(1)
        grid_step = b * (length // tn) + t

        @pl.when(grid_step == 0)
        def _():
            hosted.start(h_in, h_out, h_sems)

        @pl.when(grid_step == nb * (length // tn) - 1)
        def _():
            hosted.finish(h_in, h_out, h_sems)

        dh = lax.dot_general(dp_ref[...], w_ref[...], (((1,), (1,)), ((), ())), preferred_element_type=F32)
        _, vjp = jax.vjp(_rms_mod, x_ref[...], g_ref[...], mod_ref[1:2, :], mod_ref[0:1, :])
        dx, dg, dsc, dsh = vjp(dh)
        if not is_ctx:
            dx_ref[...] = dx + dxt_ref[...]

        @pl.when((t == 0) & ((b == 0) if is_ctx else True))
        def _():
            dmod_ref[...] = jnp.zeros_like(dmod_ref)

        @pl.when((t == 0) & (b == 0))
        def _():
            dg_ref[...] = jnp.zeros_like(dg_ref)

        dmod_ref[0:1, :] += dsh
        dmod_ref[1:2, :] += dsc
        dg_ref[0:1, :] += dg

    tok = lambda b, t: (b, t, 0)
    in_specs = [
        pl.BlockSpec((None, tn, D), tok),
        pl.BlockSpec((None, 6, D), (lambda b, t: (2, 0, 0)) if is_ctx else (lambda b, t: (b, 0, 0))),
        pl.BlockSpec((1, D), lambda b, t: (0, 0)),
        pl.BlockSpec((D, IN_W), lambda b, t: (0, 0), pipeline_mode=pl.Buffered(1)),
        pl.BlockSpec((None, tn, IN_W), tok),
    ]
    args = [xin, mod3, g_pre, wperm, dproj]
    out_specs = [
        pl.BlockSpec((None, 6, D), (lambda b, t: (0, 0, 0)) if is_ctx else (lambda b, t: (b, 0, 0))),
        pl.BlockSpec((8, D), lambda b, t: (0, 0)),
    ]
    out_shape = [jax.ShapeDtypeStruct((1 if is_ctx else nb, 6, D), F32), jax.ShapeDtypeStruct((8, D), F32)]
    if not is_ctx:
        in_specs.append(pl.BlockSpec((None, tn, D), tok))
        args.append(dx_tail)
        out_specs.insert(0, pl.BlockSpec((None, tn, D), tok))
        out_shape.insert(0, jax.ShapeDtypeStruct((nb, length, D), F32))
    h_in_specs, h_out_specs = hosted.specs()
    return pl.pallas_call(
        body, name=name, grid=(nb, length // tn), in_specs=in_specs + h_in_specs, out_specs=out_specs + h_out_specs,
        out_shape=out_shape + hosted.out_shape, scratch_shapes=hosted.scratch,
        compiler_params=_cp(("arbitrary", "arbitrary")),
    )(*args, *hosted.args)


def _ret_specs(order):
    def im(f):
        return lambda *g: f(*order(*g))
    return dict(
        pret=pl.BlockSpec((None, SEQ, 512), im(lambda b, h: (b, 0, h))),
        pretc=pl.BlockSpec((None, LC, 512), im(lambda b, h: (b, 0, h))),
        rd=pl.BlockSpec((None, 2, 1), im(lambda b, h: (h, 0, 0))),
        gn=pl.BlockSpec((None, 1, RD), im(lambda b, h: (h, 0, 0))),
        tab=pl.BlockSpec((SEQ, RD), im(lambda b, h: (0, 0))),
        head=pl.BlockSpec((None, SEQ, RD), im(lambda b, h: (b, 0, h))),
    )


def retention_fwd(pret, pretc, rd, gn, cos, sin, hosted):
    nb = pret.shape[0]
    sp = _ret_specs(lambda b, h: (b, h))

    def body(*refs):
        own_in, h_in, own_out, h_out, own_scr, h_sems = hosted.split(refs, 6, 2)
        p_ref, pc_ref, rd_ref, gn_ref, cos_ref, sin_ref = own_in
        (o_ref, mix_ref), (q_s, k_s, o_s, st_s) = own_out, own_scr
        grid_step = pl.program_id(0) * RH + pl.program_id(1)

        @pl.when(grid_step == 0)
        def _():
            hosted.start(h_in, h_out, h_sems)

        cos_v, sin_v = cos_ref[...], sin_ref[...]
        q_s[...] = _rope(p_ref[:, 0:128], cos_v, sin_v) * (RD ** -0.5)
        k_s[...] = _rope(p_ref[:, 128:256], cos_v, sin_v)
        lgs, init = [], []
        for rev in (False, True):
            lg = jax.nn.log_sigmoid(rd_ref[int(rev):int(rev) + 1, :])
            s = jnp.zeros((RD, RD), F32)
            for n in ((1, 0) if rev else (0, 1)):
                s = _ret_state(pc_ref[n * CH:(n + 1) * CH, 128:256], pc_ref[n * CH:(n + 1) * CH, 256:384], s, lg, rev)
            lgs.append(lg)
            init.append(s)

        dec = _Decays(lgs)
        _state_pass(dec, init, k_s, lambda sl: p_ref[sl, 256:384], st_s)

        def chunk(n, carry):
            sl = pl.ds(pl.multiple_of(n * CH, CH), CH)
            q = q_s[sl, :]
            o_s[sl, :] = (_nn(_nt(q, k_s[sl, :]) * dec.mask, p_ref[sl, 256:384]) + _nn(_both(q, dec.a), st_s[n]))
            return carry

        _chunk_loop(NCH, chunk, 0)
        o = o_s[...]
        o_ref[...] = o
        mix_ref[...] = _ln_gate(o, p_ref[:, 384:512], gn_ref[...]).astype(BF16)

        @pl.when(grid_step == nb * RH - 1)
        def _():
            hosted.finish(h_in, h_out, h_sems)

    h_in_specs, h_out_specs = hosted.specs()
    return pl.pallas_call(
        body, name="retention_fwd", grid=(nb, RH),
        in_specs=[sp["pret"], sp["pretc"], sp["rd"], sp["gn"], sp["tab"], sp["tab"]] + h_in_specs,
        out_specs=[sp["head"], sp["head"]] + h_out_specs,
        out_shape=[jax.ShapeDtypeStruct((nb, SEQ, RH * RD), F32), jax.ShapeDtypeStruct((nb, SEQ, D), BF16)]
        + hosted.out_shape,
        scratch_shapes=[pltpu.VMEM((SEQ, RD), F32)] * 3 + [pltpu.VMEM((NCH, 2 * RD, RD), F32)] + hosted.scratch,
        compiler_params=_cp(("arbitrary", "arbitrary")),
    )(pret, pretc, rd, gn, cos, sin, *hosted.args)


def retention_bwd(pret, pretc, o_all, dmixin, rd, gn, cos, sin, hosted):
    nb = pret.shape[0]
    sp = _ret_specs(lambda h, b: (b, h))

    def body(*refs):
        own_in, h_in, own_out, h_out, own_scr, h_sems = hosted.split(refs, 8, 4)
        p_ref, pc_ref, o_ref, dmix_ref, rd_ref, gn_ref, cos_ref, sin_ref = own_in
        dp_ref, dpc_ref, drd_ref, dgn_ref = own_out
        q_s, k_s, do_s, dq_s, dk_s, dv_s, st_s, gst_s = own_scr
        b = pl.program_id(1)
        grid_step = pl.program_id(0) * nb + b

        @pl.when(grid_step == 0)
        def _():
            hosted.start(h_in, h_out, h_sems)

        cos_v, sin_v = cos_ref[...], sin_ref[...]
        q_s[...] = _rope(p_ref[:, 0:128], cos_v, sin_v) * (RD ** -0.5)
        k_s[...] = _rope(p_ref[:, 128:256], cos_v, sin_v)
        _, gate_vjp = jax.vjp(_ln_gate, o_ref[...], p_ref[:, 384:512], gn_ref[...])
        do, dg, dgn = gate_vjp(dmix_ref[...].astype(F32))
        do_s[...] = do
        dp_ref[:, 384:512] = dg.astype(BF16)

        @pl.when(b == 0)
        def _():
            drd_ref[...] = jnp.zeros_like(drd_ref)
            dgn_ref[...] = jnp.zeros_like(dgn_ref)

        dgn_ref[...] += dgn
        kcs = [pc_ref[n * CH:(n + 1) * CH, 128:256] for n in (0, 1)]
        vcs = [pc_ref[n * CH:(n + 1) * CH, 256:384] for n in (0, 1)]
        dirs = []
        init = []
        for rev in (False, True):
            rdv = rd_ref[int(rev):int(rev) + 1, :]
            lg = jax.nn.log_sigmoid(rdv)
            order_c = (1, 0) if rev else (0, 1)
            s = jnp.zeros((RD, RD), F32)
            ctx_states = []
            for n in order_c:
                ctx_states.append(s)
                s = _ret_state(kcs[n], vcs[n], s, lg, rev)
            dirs.append((rev, order_c, lg, rdv, ctx_states))
            init.append(s)
        dec = _Decays([lg for _, _, lg, _, _ in dirs])

        def v_of(sl):
            return p_ref[sl, 256:384]

        _state_pass(dec, init, k_s, v_of, st_s)
        zeros = jnp.zeros((CH, RD), F32)

        def scores_back(n, carry):
            dmask_sum, da_f, da_b = carry
            sl = pl.ds(pl.multiple_of(n * CH, CH), CH)
            q, k, v, do = q_s[sl, :], k_s[sl, :], v_of(sl), do_s[sl, :]
            scores = _nt(q, k)
            d_att = _nt(do, v)
            d_scores = d_att * dec.mask
            d_qa = _nt(do, st_s[n])
            d_qf, d_qb = d_qa[:, 0:RD], d_qa[:, RD:2 * RD]
            dq_s[sl, :] = _nn(d_scores, k) + d_qf * dec.a[0] + d_qb * dec.a[1]
            dk_s[sl, :] = _tn(d_scores, q)
            dv_s[sl, :] = _tn(scores * dec.mask, do)
            gst_s[n] = _tn(_both(q, dec.a), do)
            return dmask_sum + d_att * scores, da_f + d_qf * q, da_b + d_qb * q

        dmask_sum, da_f, da_b = _chunk_loop(NCH, scores_back, (zeros, zeros, zeros))

        def state_back(t, carry):
            out = []
            for d, r in enumerate(carry):
                n = t if d else (NCH - 1 - t)
                rows = slice(d * RD, (d + 1) * RD)
                own = gst_s[n, rows, :]
                gst_s[n, rows, :] = r
                out.append(own + dec.g[d] * r)
            return tuple(out)

        d_states = _chunk_loop(NCH, state_back, (zeros, zeros))

        def updates_back(n, carry):
            db_f, db_b, dg_f, dg_b = carry
            sl = pl.ds(pl.multiple_of(n * CH, CH), CH)
            k, r, s = k_s[sl, :], gst_s[n], st_s[n]
            d_kw = _nt(v_of(sl), r)
            d_kf, d_kb = d_kw[:, 0:RD], d_kw[:, RD:2 * RD]
            dk_s[sl, :] += d_kf * dec.b[0] + d_kb * dec.b[1]
            dv_s[sl, :] += _nn(_both(k, dec.b), r)
            return (db_f + d_kf * k, db_b + d_kb * k, dg_f + r[0:RD, :] * s[0:RD, :],
                    dg_b + r[RD:2 * RD, :] * s[RD:2 * RD, :])

        db_dg = _chunk_loop(NCH, updates_back, (zeros, zeros, zeros, zeros))
        dkc = [None, None]
        dvc = [None, None]
        for d, ((rev, order_c, lg, rdv, ctx_states), ds) in enumerate(zip(dirs, d_states)):
            dlg = (_total(dmask_sum * dec.dmask[d]) + _total((da_f, da_b)[d] * dec.da[d])
                   + _total(db_dg[d] * dec.db[d]) + CH * dec.g[d] * _total(db_dg[2 + d]))
            for idx in (1, 0):
                n = order_c[idx]
                _, vjp = jax.vjp(functools.partial(_ret_state, reverse=rev), kcs[n], vcs[n], ctx_states[idx], lg)
                dk_c, dv_c, ds, dl = vjp(ds)
                dlg = dlg + dl
                dkc[n] = dk_c if dkc[n] is None else dkc[n] + dk_c
                dvc[n] = dv_c if dvc[n] is None else dvc[n] + dv_c
            drd_ref[int(rev):int(rev) + 1, :] += dlg * jax.nn.sigmoid(-rdv)
        dp_ref[:, 0:128] = _rope_t(dq_s[...] * (RD ** -0.5), cos_v, sin_v).astype(BF16)
        dp_ref[:, 128:256] = _rope_t(dk_s[...], cos_v, sin_v).astype(BF16)
        dp_ref[:, 256:384] = dv_s[...].astype(BF16)
        zero = jnp.zeros((CH, RD), BF16)
        for n in (0, 1):
            rows = slice(n * CH, (n + 1) * CH)
            dpc_ref[rows, 0:128] = zero
            dpc_ref[rows, 128:256] = dkc[n].astype(BF16)
            dpc_ref[rows, 256:384] = dvc[n].astype(BF16)
            dpc_ref[rows, 384:512] = zero

        @pl.when(grid_step == RH * nb - 1)
        def _():
            hosted.finish(h_in, h_out, h_sems)

    h_in_specs, h_out_specs = hosted.specs()
    return pl.pallas_call(
        body, name="retention_bwd", grid=(RH, nb),
        in_specs=[sp["pret"], sp["pretc"], sp["head"], sp["head"], sp["rd"], sp["gn"], sp["tab"], sp["tab"]]
        + h_in_specs,
        out_specs=[
            pl.BlockSpec((None, SEQ, 512), lambda h, b: (b, 0, h)),
            pl.BlockSpec((None, LC, 512), lambda h, b: (b, 0, h)),
            pl.BlockSpec((None, 2, 1), lambda h, b: (h, 0, 0)),
            pl.BlockSpec((None, 1, RD), lambda h, b: (h, 0, 0)),
        ] + h_out_specs,
        out_shape=[
            jax.ShapeDtypeStruct((nb, SEQ, IN_W), BF16),
            jax.ShapeDtypeStruct((nb, LC, IN_W), BF16),
            jax.ShapeDtypeStruct((RH, 2, 1), F32),
            jax.ShapeDtypeStruct((RH, 1, RD), F32),
        ] + hosted.out_shape,
        scratch_shapes=[pltpu.VMEM((SEQ, RD), F32)] * 6 + [pltpu.VMEM((NCH, 2 * RD, RD), F32)] * 2 + hosted.scratch,
        compiler_params=_cp(("arbitrary", "arbitrary")),
    )(pret, pretc, o_all, dmixin, rd, gn, cos, sin, *hosted.args)


def _rpb_flat(rpb):
    return jnp.pad(rpb, ((0, 0), (0, 1), (0, 33))).reshape(NPAIR, 2, 1, 1024)


def _rpb_flat_t(dflat):
    return dflat.reshape(8, 16, 64)[:, :15, :31]


def _barrel(x, left):
    row = lax.broadcasted_iota(jnp.int32, x.shape, 0)
    n = x.shape[1]
    for bit in range(6):
        s = 1 << bit
        x = jnp.where(((row >> bit) & 1) == 1, pltpu.roll(x, (n - s) if left else s, 1), x)
    return x


NA_TILE_ROWS, NA_BAND_ROWS = 4, 12
NA_Q, NA_K = NA_TILE_ROWS * GW, NA_BAND_ROWS * GW
NA_TILES = SEQ // NA_Q


def _band_start(r0):
    return min(max(r0 - 4, 0), 32 - NA_BAND_ROWS)


def _tile_layout(t):
    rows = range(t * NA_TILE_ROWS, (t + 1) * NA_TILE_ROWS)
    return tuple((r if r < 4 else (r - 24 if r > 28 else 4), min(max(r - 4, 0), 24) - _band_start(rows[0]))
                 for r in rows)


NA_CLASSES = sorted(set(_tile_layout(t) for t in range(NA_TILES)))


def _tile_rows(cls):
    return NA_CLASSES[cls]


def _na_tile(t):
    start = jnp.clip(NA_TILE_ROWS * t - 4, 0, 32 - NA_BAND_ROWS)
    cls = 0
    for tile in range(NA_TILES):
        cls = jnp.where(t == tile, NA_CLASSES.index(_tile_layout(tile)), cls)
    return pl.ds(pl.multiple_of(t * NA_Q, NA_Q), NA_Q), pl.ds(pl.multiple_of(start * GW, NA_Q), NA_K), cls


def _na_probs(qst, kb, kc, bias):
    s_loc = _nt(qst, kb) + bias
    s_ctx = _nt(qst, kc)
    m = jnp.maximum(jnp.max(s_loc, axis=1, keepdims=True), jnp.max(s_ctx, axis=1, keepdims=True))
    e_loc, e_ctx = jnp.exp(s_loc - m), jnp.exp(s_ctx - m)
    den = jnp.sum(e_loc, axis=1, keepdims=True) + jnp.sum(e_ctx, axis=1, keepdims=True)
    return e_loc / den, e_ctx / den


def _stack_heads(t):
    lane = lax.broadcasted_iota(jnp.int32, t.shape, 1)
    zero = jnp.zeros_like(t)
    return jnp.concatenate([jnp.where(lane < 64, t, zero), jnp.where(lane >= 64, t, zero)], axis=0)


def _unstack_heads(t):
    n = t.shape[0] // 2
    lane = lax.broadcasted_iota(jnp.int32, (n, 128), 1)
    return jnp.where(lane < 64, t[:n], t[n:])


NA_BIAS_SHAPE = (len(NA_CLASSES), 2 * NA_Q, NA_K)


def _na_bias_pair(flat_ref, out_ref):
    qc = lax.broadcasted_iota(jnp.int32, (GW, 512), 0)
    kc = lax.broadcasted_iota(jnp.int32, (GW, 512), 1) & 63
    start = jnp.clip(qc - 8, 0, GW - 16)
    window = (kc >= start) & (kc < start + 16)
    fill = jnp.full((GW, NA_K - 512), NEG, F32)
    for hh in (0, 1):
        skew = _barrel(pltpu.roll(jnp.broadcast_to(flat_ref[hh], (GW, 1024)), 1024 - 15, 1), left=False)
        by_class = [jnp.where(window, (skew if rc == 7 else pltpu.roll(skew, (9 + rc) * 64, 1))[:, 0:512], NEG)
                    for rc in range(8)]
        for cls in range(len(NA_CLASSES)):
            for qr, (rc, off) in enumerate(_tile_rows(cls)):
                w = jnp.concatenate([by_class[rc], fill], axis=1)
                rows = slice(hh * NA_Q + qr * GW, hh * NA_Q + (qr + 1) * GW)
                out_ref[cls, rows, :] = pltpu.roll(w, off * GW, 1) if off else w


def na_fwd(pna, pnac, bias, mixin, hosted):
    nb = pna.shape[0]

    def body(*refs):
        (p_ref, pc_ref, bias_ref, _), h_in, (out_ref,), h_out, _, h_sems = hosted.split(refs, 4, 1)
        grid_step = pl.program_id(0) * nb + pl.program_id(1)

        @pl.when(grid_step == 0)
        def _():
            hosted.start(h_in, h_out, h_sems)

        kc, vc = pc_ref[:, 128:256], pc_ref[:, 256:384]

        def tile(t, carry):
            qsl, bsl, cls = _na_tile(t)
            kb, vb = p_ref[bsl, 128:256], p_ref[bsl, 256:384]
            p_loc, p_ctx = _na_probs(_stack_heads(p_ref[qsl, 0:128] * 0.125), kb, kc, bias_ref[cls])
            out_ref[qsl, :] = _unstack_heads(_nn(p_loc, vb) + _nn(p_ctx, vc)).astype(BF16)
            return carry

        lax.fori_loop(0, NA_TILES, tile, 0, unroll=4)

        @pl.when(grid_step == NPAIR * nb - 1)
        def _():
            hosted.finish(h_in, h_out, h_sems)

    h_in_specs, h_out_specs = hosted.specs()
    return pl.pallas_call(
        body, name="na_fwd", grid=(NPAIR, nb),
        in_specs=[
            pl.BlockSpec((None, SEQ, 384), lambda p, b: (b, 0, p)),
            pl.BlockSpec((None, LC, 384), lambda p, b: (b, 0, p)),
            pl.BlockSpec((None, len(NA_CLASSES), 2 * NA_Q, NA_K), lambda p, b: (p, 0, 0, 0)),
            pl.BlockSpec(memory_space=pl.ANY),
        ] + h_in_specs,
        out_specs=[pl.BlockSpec((None, SEQ, 128), lambda p, b: (b, 0, 4 + p))] + h_out_specs,
        out_shape=[jax.ShapeDtypeStruct((nb, SEQ, D), BF16)] + hosted.out_shape,
        input_output_aliases={3: 0},
        scratch_shapes=hosted.scratch,
        compiler_params=_cp(("arbitrary", "arbitrary")),
    )(pna, pnac, bias, mixin, *hosted.args)


def na_bwd(pna, pnac, bias, dmixin, dproj, dprojc, hosted):
    nb = pna.shape[0]

    def body(*refs):
        own_in, h_in, own_out, h_out, own_scr, h_sems = hosted.split(refs, 6, 3)
        p_ref, pc_ref, bias_ref, dmix_ref = own_in[:4]
        dp_ref, dpc_ref, dpat_ref = own_out
        dbias_s, dk_s, dv_s, dkc_s, dvc_s, res_s, resc_s = own_scr
        b, part = pl.program_id(1), pl.program_id(2)
        grid_step = (pl.program_id(0) * nb + b) * 3 + part

        @pl.when(grid_step == 0)
        def _():
            hosted.start(h_in, h_out, h_sems)

        @pl.when(grid_step == NPAIR * nb * 3 - 1)
        def _():
            hosted.finish(h_in, h_out, h_sems)

        @pl.when(part == 0)
        def _():
            @pl.when(b == 0)
            def _():
                dbias_s[...] = jnp.zeros_like(dbias_s)

            dk_s[...] = jnp.zeros_like(dk_s)
            dv_s[...] = jnp.zeros_like(dv_s)
            dkc_s[...] = jnp.zeros_like(dkc_s)
            dvc_s[...] = jnp.zeros_like(dvc_s)
            kc, vc = pc_ref[:, 128:256], pc_ref[:, 256:384]

            def tile(t, carry):
                qsl, bsl, cls = _na_tile(t)
                kb, vb = p_ref[bsl, 128:256], p_ref[bsl, 256:384]
                qst, dost = _stack_heads(p_ref[qsl, 0:128] * 0.125), _stack_heads(dmix_ref[qsl, :])
                p_loc, p_ctx = _na_probs(qst, kb, kc, bias_ref[cls])
                dp_loc, dp_ctx = _nt(dost, vb), _nt(dost, vc)
                delta = (jnp.sum(p_loc * dp_loc, axis=1, keepdims=True)
                         + jnp.sum(p_ctx * dp_ctx, axis=1, keepdims=True))
                ds_loc, ds_ctx = p_loc * (dp_loc - delta), p_ctx * (dp_ctx - delta)
                dbias_s[cls] += ds_loc
                res_s[0, qsl, :] = _unstack_heads((_nn(ds_loc, kb) + _nn(ds_ctx, kc)) * 0.125).astype(BF16)
                dk_s[bsl, :] += _tn(ds_loc, qst)
                dv_s[bsl, :] += _tn(p_loc, dost)
                dkc_s[...] += _tn(ds_ctx, qst)
                dvc_s[...] += _tn(p_ctx, dost)
                return carry

            lax.fori_loop(0, NA_TILES, tile, 0, unroll=2)
            res_s[1] = dk_s[...].astype(BF16)
            res_s[2] = dv_s[...].astype(BF16)
            resc_s[0] = jnp.zeros((LC, 128), BF16)
            resc_s[1] = dkc_s[...].astype(BF16)
            resc_s[2] = dvc_s[...].astype(BF16)

            @pl.when(b == nb - 1)
            def _():
                for hh in (0, 1):
                    by_class = [None] * 8
                    for cls in range(len(NA_CLASSES)):
                        for qr, (rc, off) in enumerate(_tile_rows(cls)):
                            w = dbias_s[cls, hh * NA_Q + qr * GW:hh * NA_Q + (qr + 1) * GW, :]
                            w = (pltpu.roll(w, NA_K - off * GW, 1) if off else w)[:, 0:512]
                            by_class[rc] = w if by_class[rc] is None else by_class[rc] + w
                    skew = jnp.zeros((GW, 1024), F32)
                    for rc in range(8):
                        w = jnp.concatenate([by_class[rc], jnp.zeros((GW, 512), F32)], axis=1)
                        skew = skew + (w if rc == 7 else pltpu.roll(w, (7 - rc) * 64, 1))
                    dpat_ref[hh] = jnp.sum(pltpu.roll(_barrel(skew, left=True), 15, 1), axis=0, keepdims=True)

        dp_ref[...] = res_s[part]
        dpc_ref[...] = resc_s[part]

    h_in_specs, h_out_specs = hosted.specs()
    return pl.pallas_call(
        body, name="na_bwd", grid=(NPAIR, nb, 3),
        in_specs=[
            pl.BlockSpec((None, SEQ, 384), lambda p, b, s: (b, 0, p)),
            pl.BlockSpec((None, LC, 384), lambda p, b, s: (b, 0, p)),
            pl.BlockSpec((None, len(NA_CLASSES), 2 * NA_Q, NA_K), lambda p, b, s: (p, 0, 0, 0)),
            pl.BlockSpec((None, SEQ, 128), lambda p, b, s: (b, 0, 4 + p)),
            pl.BlockSpec(memory_space=pl.ANY),
            pl.BlockSpec(memory_space=pl.ANY),
        ] + h_in_specs,
        out_specs=[
            pl.BlockSpec((None, SEQ, 128), lambda p, b, s: (b, 0, 16 + 3 * p + s)),
            pl.BlockSpec((None, LC, 128), lambda p, b, s: (b, 0, 16 + 3 * p + s)),
            pl.BlockSpec((None, 2, 1, 1024), lambda p, b, s: (p, 0, 0, 0)),
        ] + h_out_specs,
        out_shape=[
            jax.ShapeDtypeStruct((nb, SEQ, IN_W), BF16),
            jax.ShapeDtypeStruct((nb, LC, IN_W), BF16),
            jax.ShapeDtypeStruct((NPAIR, 2, 1, 1024), F32),
        ] + hosted.out_shape,
        input_output_aliases={4: 0, 5: 1},
        scratch_shapes=[
            pltpu.VMEM((len(NA_CLASSES), 2 * NA_Q, NA_K), F32),
            pltpu.VMEM((SEQ, 128), F32), pltpu.VMEM((SEQ, 128), F32),
            pltpu.VMEM((LC, 128), F32), pltpu.VMEM((LC, 128), F32),
            pltpu.VMEM((3, SEQ, 128), BF16), pltpu.VMEM((3, LC, 128), BF16),
        ] + hosted.scratch,
        compiler_params=_cp(("arbitrary", "arbitrary", "arbitrary")),
    )(pna, pnac, bias, dmixin, dproj, dprojc, *hosted.args)


def tail_fwd_bwd(x, mixin, tgt, mod3, g_post_mix, g_pre_mlp, g_post_mlp, wout, w1, w2):
    nb = x.shape[0]

    def body(x_ref, mi_ref, tgt_ref, mod_ref, gpm_ref, gpl_ref, gpo_ref, wo_ref, w1_ref, w2_ref,
             dx_ref, dmix_ref, h2_ref, du_ref, a_ref, dm_ref, dmi_ref, dmod_ref, dg_ref, loss_ref):
        b, t = pl.program_id(0), pl.program_id(1)
        gt1, sh2, sc2, gt2 = mod_ref[2:3, :], mod_ref[3:4, :], mod_ref[4:5, :], mod_ref[5:6, :]
        mix = jnp.dot(mi_ref[...], wo_ref[...], preferred_element_type=F32)
        (x1, h2), vjp_a = jax.vjp(_post_mix, x_ref[...], mix, gt1, sc2, sh2, gpm_ref[...], gpl_ref[...])
        h2b = h2.astype(BF16)
        h2_ref[...] = h2b
        m = jnp.zeros((TN, D), F32)
        relus = []
        for j in range(4):
            cols = slice(j * D, (j + 1) * D)
            r = jnp.maximum(jnp.dot(h2b, w1_ref[j], preferred_element_type=F32), 0.0)
            ab = (r * r).astype(BF16)
            a_ref[:, cols] = ab
            m = m + jnp.dot(ab, w2_ref[cols, :], preferred_element_type=F32)
            relus.append(r)
        loss, vjp_b = jax.vjp(_head_loss, x1, m, gt2, gpo_ref[...], tgt_ref[...])
        dx1, dm, dgt2, dgpo, _ = vjp_b(jnp.ones((1, 1), F32))
        dmb = dm.astype(BF16)
        dm_ref[...] = dmb
        dh2 = jnp.zeros((TN, D), F32)
        for j in range(4):
            cols = slice(j * D, (j + 1) * D)
            da = lax.dot_general(dmb, w2_ref[cols, :], (((1,), (1,)), ((), ())), preferred_element_type=F32)
            dub = (da * (2.0 * relus[j])).astype(BF16)
            du_ref[:, cols] = dub
            dh2 = dh2 + lax.dot_general(dub, w1_ref[j], (((1,), (1,)), ((), ())), preferred_element_type=F32)
        dx, dmix, dgt1, dsc2, dsh2, dgpm, dgpl = vjp_a((dx1, dh2))
        dx_ref[...] = dx
        dmixb = dmix.astype(BF16)
        dmix_ref[...] = dmixb
        dmi_ref[...] = lax.dot_general(dmixb, wo_ref[...], (((1,), (1,)), ((), ())),
                                       preferred_element_type=F32).astype(BF16)

        @pl.when(t == 0)
        def _():
            dmod_ref[...] = jnp.zeros_like(dmod_ref)

        @pl.when((t == 0) & (b == 0))
        def _():
            dg_ref[...] = jnp.zeros_like(dg_ref)
            loss_ref[...] = jnp.zeros_like(loss_ref)

        dmod_ref[2:3, :] += dgt1
        dmod_ref[3:4, :] += dsh2
        dmod_ref[4:5, :] += dsc2
        dmod_ref[5:6, :] += dgt2
        dg_ref[0:1, :] += dgpm
        dg_ref[1:2, :] += dgpl
        dg_ref[2:3, :] += dgpo
        loss_ref[...] += jnp.broadcast_to(loss, loss_ref.shape)

    tok = lambda b, t: (b, t, 0)
    const = lambda b, t: (0, 0)
    vec = pl.BlockSpec((1, D), const)
    return pl.pallas_call(
        body, name="tail_fwd_bwd", grid=(nb, SEQ // TN),
        in_specs=[
            pl.BlockSpec((None, TN, D), tok), pl.BlockSpec((None, TN, D), tok), pl.BlockSpec((None, TN, D), tok),
            pl.BlockSpec((None, 6, D), lambda b, t: (b, 0, 0)), vec, vec, vec,
            pl.BlockSpec((D, D), const, pipeline_mode=pl.Buffered(1)),
            pl.BlockSpec((4, D, D), lambda b, t: (0, 0, 0), pipeline_mode=pl.Buffered(1)),
            pl.BlockSpec((DFF, D), const, pipeline_mode=pl.Buffered(1)),
        ],
        out_specs=[
            pl.BlockSpec((None, TN, D), tok), pl.BlockSpec((None, TN, D), tok), pl.BlockSpec((None, TN, D), tok),
            pl.BlockSpec((None, TN, DFF), tok), pl.BlockSpec((None, TN, DFF), tok), pl.BlockSpec((None, TN, D), tok),
            pl.BlockSpec((None, TN, D), tok),
            pl.BlockSpec((None, 6, D), lambda b, t: (b, 0, 0)),
            pl.BlockSpec((8, D), const), pl.BlockSpec((8, 128), const),
        ],
        out_shape=[
            jax.ShapeDtypeStruct((nb, SEQ, D), F32), jax.ShapeDtypeStruct((nb, SEQ, D), BF16),
            jax.ShapeDtypeStruct((nb, SEQ, D), BF16), jax.ShapeDtypeStruct((nb, SEQ, DFF), BF16),
            jax.ShapeDtypeStruct((nb, SEQ, DFF), BF16), jax.ShapeDtypeStruct((nb, SEQ, D), BF16),
            jax.ShapeDtypeStruct((nb, SEQ, D), BF16),
            jax.ShapeDtypeStruct((nb, 6, D), F32), jax.ShapeDtypeStruct((8, D), F32),
            jax.ShapeDtypeStruct((8, 128), F32),
        ],
        compiler_params=_cp(("arbitrary", "arbitrary")),
    )(x, mixin, tgt, mod3, g_post_mix, g_pre_mlp, g_post_mlp, wout, w1, w2)


def weight_grad(pairs, name, hosted, out_dtype=F32, col_blocks=False, tm=1024, tn=1024, tk=2048):
    m, n = pairs[0][0].shape[1], pairs[0][1].shape[1]
    tn = min(tn, n)
    tks = [min(tk, xa.shape[0]) for xa, _ in pairs]
    steps = [xa.shape[0] // t for (xa, _), t in zip(pairs, tks)]
    total = sum(steps)
    offs = [sum(steps[:i]) for i in range(len(pairs))]
    grid = (m // tm, n // tn, total)

    def body(*all_refs):
        refs, h_in, (out_ref,), h_out, own_scr, h_sems = hosted.split(all_refs, 2 * len(pairs), 1)
        acc = own_scr[0] if own_scr else out_ref
        k = pl.program_id(2)
        grid_step = (pl.program_id(0) * grid[1] + pl.program_id(1)) * total + k

        @pl.when(grid_step == 0)
        def _():
            hosted.start(h_in, h_out, h_sems)

        @pl.when(grid_step == grid[0] * grid[1] * total - 1)
        def _():
            hosted.finish(h_in, h_out, h_sems)

        @pl.when(k == 0)
        def _():
            acc[...] = jnp.zeros_like(acc)

        for i in range(len(pairs)):
            @pl.when((k >= offs[i]) & (k < offs[i] + steps[i]))
            def _(i=i):
                acc[...] += lax.dot_general(refs[2 * i][...], refs[2 * i + 1][...], (((0,), (0,)), ((), ())),
                                            preferred_element_type=F32)

        if out_dtype != F32:
            @pl.when(k == total - 1)
            def _():
                out_ref[...] = acc[...].astype(out_dtype)

    in_specs, args = [], []
    for i, (xa, ya) in enumerate(pairs):
        clamp = lambda k, i=i: jnp.clip(k - offs[i], 0, steps[i] - 1)
        in_specs.append(pl.BlockSpec((tks[i], tm), lambda a, c, k, clamp=clamp: (clamp(k), a)))
        in_specs.append(pl.BlockSpec((tks[i], tn), lambda a, c, k, clamp=clamp: (clamp(k), c)))
        args += [xa, ya]
    if col_blocks:
        out_spec = pl.BlockSpec((None, tm, tn), lambda a, c, k: (c, a, 0))
        out_shape = jax.ShapeDtypeStruct((n // tn, m, tn), out_dtype)
    else:
        out_spec = pl.BlockSpec((tm, tn), lambda a, c, k: (a, c))
        out_shape = jax.ShapeDtypeStruct((m, n), out_dtype)
    h_in_specs, h_out_specs = hosted.specs()
    return pl.pallas_call(
        body, name=name, grid=grid, in_specs=in_specs + h_in_specs, out_specs=[out_spec] + h_out_specs,
        out_shape=[out_shape] + hosted.out_shape,
        scratch_shapes=([] if out_dtype == F32 else [pltpu.VMEM((tm, tn), F32)]) + hosted.scratch,
        compiler_params=_cp(("arbitrary", "arbitrary", "arbitrary")),
    )(*args, *hosted.args)


def _perm_block(t):
    return 4 * (t % 4) + t // 4 if t < 16 else 16 + 3 * ((t - 16) % 4) + (t - 16) // 4


def _is_rope_block(p):
    return p < 16 and p % 4 < 2


def unpack_w_in(blocks):
    def body(i_ref, o_ref):
        for t in range(28):
            p = _perm_block(t)
            blk = i_ref[t // 7, :, (t % 7) * 128:(t % 7 + 1) * 128]
            if _is_rope_block(p):
                blk = _pair_order(blk.astype(F32)).astype(BF16)
            o_ref[:, p * 128:(p + 1) * 128] = blk

    return pl.pallas_call(
        body, name="unpack_w_in", grid=(2,),
        in_specs=[pl.BlockSpec((4, D // 2, 896), lambda i: (0, i, 0))],
        out_specs=pl.BlockSpec((D // 2, IN_W), lambda i: (i, 0)),
        out_shape=jax.ShapeDtypeStruct((D, IN_W), BF16),
    )(blocks)


def pack_w_in(dw):
    def body(i_ref, o_ref):
        for t in range(28):
            p = _perm_block(t)
            blk = i_ref[:, p * 128:(p + 1) * 128]
            if _is_rope_block(p):
                blk = _pair_order(blk)
            o_ref[t // 7, :, (t % 7) * 128:(t % 7 + 1) * 128] = blk.astype(BF16)

    return pl.pallas_call(
        body, name="pack_w_in", grid=(4,),
        in_specs=[pl.BlockSpec((D // 4, IN_W), lambda i: (i, 0))],
        out_specs=pl.BlockSpec((4, D // 4, 896), lambda i: (0, i, 0)),
        out_shape=jax.ShapeDtypeStruct((4, D, 896), BF16),
    )(dw)


def _place():
    return lax.axis_index("x"), lax.axis_index("y"), lax.axis_index("c")


class Hosted:
    def __init__(self, args, out_shape, scratch, start, finish):
        self.args, self.out_shape, self.scratch, self.start, self.finish = args, out_shape, scratch, start, finish

    def specs(self):
        hbm = pl.BlockSpec(memory_space=pl.ANY)
        return [hbm] * len(self.args), [hbm] * len(self.out_shape)

    def split(self, refs, n_in, n_out):
        a, b = len(self.args), len(self.out_shape)
        cuts = [n_in, n_in + a, n_in + a + n_out, n_in + a + n_out + b, len(refs) - len(self.scratch)]
        parts = [refs[i:j] for i, j in zip([0] + cuts, cuts + [len(refs)])]
        return parts[0], parts[1], parts[2], parts[3], parts[4], parts[5]


def no_exchange():
    return Hosted([], [], [], lambda *a: None, lambda *a: None)


def run_hosted(hosted, name):
    def body(*refs):
        _, ins, _, outs, _, sems = hosted.split(refs, 0, 0)
        hosted.start(ins, outs, sems)
        hosted.finish(ins, outs, sems)

    in_specs, out_specs = hosted.specs()
    return pl.pallas_call(body, name=name, in_specs=in_specs, out_specs=out_specs, out_shape=hosted.out_shape,
                          scratch_shapes=hosted.scratch)(*hosted.args)


def gather8(blocks):
    na = len(blocks)

    def copies(ins, outs, sems):
        send_sems, recv_sems, local_sem = sems
        x, y, c = _place()
        me, sibling = (x, y, c), (x, y, 1 - c)
        chips = [(1 - x, y), (x, 1 - y), (1 - x, 1 - y)]

        def slot(o_ref, px, py, pc):
            return o_ref.at[4 * px + 2 * py + pc]

        def copy(a, k, block, to, src=None):
            return pltpu.make_async_remote_copy(
                src_ref=slot(outs[a], *block) if src is None else src, dst_ref=slot(outs[a], *block),
                send_sem=send_sems.at[a, k], recv_sem=recv_sems.at[a, k], device_id=to, device_id_type=MESH)

        mine = [pltpu.make_async_copy(ins[a], slot(outs[a], *me), local_sem.at[a]) for a in range(na)]
        first = []
        for a in range(na):
            first.append(copy(a, 0, me, sibling, src=ins[a]))
            first += [copy(a, 1 + j, me, (*chip, c), src=ins[a]) for j, chip in enumerate(chips)]
        return copy, mine, first, me, sibling, chips, c

    def start(ins, outs, sems):
        _, mine, first, *_ = copies(ins, outs, sems)
        for cp in mine + first:
            cp.start()

    def finish(ins, outs, sems):
        copy, mine, first, me, sibling, chips, c = copies(ins, outs, sems)
        passed = []
        for j, chip in enumerate(chips):
            for a in range(na):
                copy(a, 1 + j, (*chip, c), me).wait_recv()
                cp = copy(a, 4 + j, (*chip, c), sibling)
                cp.start()
                passed.append(cp)
        for a in range(na):
            copy(a, 0, sibling, me).wait_recv()
            for j, chip in enumerate(chips):
                copy(a, 4 + j, (*chip, 1 - c), me).wait_recv()
        for cp in first + passed:
            cp.wait_send()
        for cp in mine:
            cp.wait()

    return Hosted(list(blocks), [jax.ShapeDtypeStruct((8,) + b.shape, b.dtype) for b in blocks],
                  [pltpu.SemaphoreType.DMA((na, 7)), pltpu.SemaphoreType.DMA((na, 7)), pltpu.SemaphoreType.DMA((na,))],
                  start, finish)


def chips3(arrays):
    na = len(arrays)

    def copies(ins, outs, sems):
        send_sems, recv_sems = sems
        x, y, c = _place()
        return [pltpu.make_async_remote_copy(
            src_ref=ins[a].at[2 * px + py], dst_ref=outs[a].at[k], send_sem=send_sems.at[a, k],
            recv_sem=recv_sems.at[a, k], device_id=(px, py, c), device_id_type=MESH)
            for a in range(na) for k, (px, py) in enumerate([(1 - x, y), (x, 1 - y), (1 - x, 1 - y)])]

    def start(ins, outs, sems):
        for cp in copies(ins, outs, sems):
            cp.start()

    def finish(ins, outs, sems):
        for cp in copies(ins, outs, sems):
            cp.wait()

    return Hosted(list(arrays), [jax.ShapeDtypeStruct((3,) + a.shape[1:], a.dtype) for a in arrays],
                  [pltpu.SemaphoreType.DMA((na, 3)), pltpu.SemaphoreType.DMA((na, 3))], start, finish)


def siblings(arrays):
    na = len(arrays)

    def copies(ins, outs, sems):
        send_sems, recv_sems = sems
        x, y, c = _place()
        return [pltpu.make_async_remote_copy(
            src_ref=ins[a], dst_ref=outs[a], send_sem=send_sems.at[a], recv_sem=recv_sems.at[a],
            device_id=(x, y, 1 - c), device_id_type=MESH) for a in range(na)]

    def start(ins, outs, sems):
        for cp in copies(ins, outs, sems):
            cp.start()

    def finish(ins, outs, sems):
        for cp in copies(ins, outs, sems):
            cp.wait()

    return Hosted(list(arrays), [jax.ShapeDtypeStruct(a.shape, a.dtype) for a in arrays],
                  [pltpu.SemaphoreType.DMA((na,)), pltpu.SemaphoreType.DMA((na,))], start, finish)


def both(first, second):
    na, no, ns = len(first.args), len(first.out_shape), len(first.scratch)

    def start(ins, outs, sems):
        first.start(ins[:na], outs[:no], sems[:ns])
        second.start(ins[na:], outs[no:], sems[ns:])

    def finish(ins, outs, sems):
        first.finish(ins[:na], outs[:no], sems[:ns])
        second.finish(ins[na:], outs[no:], sems[ns:])

    return Hosted(first.args + second.args, first.out_shape + second.out_shape, first.scratch + second.scratch,
                  start, finish)


def siblings4(arrays):
    na = len(arrays)

    def copies(ins, outs, sems):
        send_sems, recv_sems = sems
        x, y, c = _place()
        return [pltpu.make_async_remote_copy(
            src_ref=ins[a].at[2 * j + 1 - c], dst_ref=outs[a].at[j],
            send_sem=send_sems.at[a, j], recv_sem=recv_sems.at[a, j],
            device_id=(x, y, 1 - c), device_id_type=MESH) for a in range(na) for j in range(4)]

    def start(ins, outs, sems):
        for cp in copies(ins, outs, sems):
            cp.start()

    def finish(ins, outs, sems):
        for cp in copies(ins, outs, sems):
            cp.wait()

    return Hosted(list(arrays), [jax.ShapeDtypeStruct((4,) + a.shape[1:], a.dtype) for a in arrays],
                  [pltpu.SemaphoreType.DMA((na, 4)), pltpu.SemaphoreType.DMA((na, 4))], start, finish)


def _row_tile(r):
    for cand in (512, 256, 128, 64, 32, 16, 8):
        if r % cand == 0:
            return cand
    return r


def chip_partial(place, g8s, landed4s, name):
    n = len(g8s)

    def body(place_ref, *refs):
        del place_ref
        for g_ref, l_ref, o_ref in zip(refs[:n], refs[n:2 * n], refs[2 * n:]):
            o_ref[...] = (g_ref[...].astype(F32) + l_ref[...].astype(F32)).astype(BF16)

    own = [pl.BlockSpec((None,) + g.shape[1:], lambda j, s: (2 * j + s[0], 0, 0)) for g in g8s]
    plain = [pl.BlockSpec((None,) + g.shape[1:], lambda j, s: (j, 0, 0)) for g in g8s]
    return pl.pallas_call(
        body, name=name,
        grid_spec=pltpu.PrefetchScalarGridSpec(num_scalar_prefetch=1, grid=(4,), in_specs=own + plain, out_specs=plain),
        out_shape=[jax.ShapeDtypeStruct((4,) + g.shape[1:], BF16) for g in g8s],
    )(place, *g8s, *landed4s)


def shard_sum(place, partial4s, landed3s, name):
    n = len(partial4s)

    def body(place_ref, *refs):
        del place_ref
        for p_ref, l_ref, o_ref in zip(refs[:n], refs[n:2 * n], refs[2 * n:]):
            acc = p_ref[...].astype(F32)
            for k in range(3):
                acc = acc + l_ref[k].astype(F32)
            o_ref[...] = acc

    def halves(p, lead):
        r, ccols = p.shape[1:]
        return (lead, r // 2, ccols)

    return pl.pallas_call(
        body, name=name,
        grid_spec=pltpu.PrefetchScalarGridSpec(
            num_scalar_prefetch=1, grid=(2,),
            in_specs=[pl.BlockSpec(halves(p, None), lambda i, s: (s[1], i, 0)) for p in partial4s]
            + [pl.BlockSpec(halves(p, 3), lambda i, s: (0, i, 0)) for p in partial4s],
            out_specs=[pl.BlockSpec(halves(p, None)[1:], lambda i, s: (i, 0)) for p in partial4s]),
        out_shape=[jax.ShapeDtypeStruct(p.shape[1:], F32) for p in partial4s],
    )(place, *partial4s, *landed3s)


def _adamw_math(w, g, m, v):
    m2 = B1 * m + (1.0 - B1) * g
    v2 = B2 * v + (1.0 - B2) * (g * g)
    m_hat = m2 / (1.0 - B1 ** STEP)
    v_hat = v2 / (1.0 - B2 ** STEP)
    return -LR * (m_hat / (jnp.sqrt(v_hat) + AEPS) + WD * w), m2, v2


def adamw_halves(place, w, mine, theirs, m, v, name):
    r, ccols = w.shape
    hr = r // 2
    tr = _row_tile(hr)
    nt = hr // tr

    def body(place_ref, w_ref, a_ref, b_ref, m_ref, v_ref, g_out, d_out, m_out, v_out):
        g = jnp.where(pl.program_id(0) == place_ref[0], a_ref[...], b_ref[...])
        d, m2, v2 = _adamw_math(w_ref[...], g, m_ref[...], v_ref[...])
        g_out[...] = g
        d_out[...] = d
        m_out[...] = m2
        v_out[...] = v2

    full = pl.BlockSpec((tr, ccols), lambda h, i, s: (h * nt + i, 0))
    part = pl.BlockSpec((tr, ccols), lambda h, i, s: (i, 0))
    return pl.pallas_call(
        body, name=name,
        grid_spec=pltpu.PrefetchScalarGridSpec(
            num_scalar_prefetch=1, grid=(2, nt), in_specs=[full, part, part, full, full], out_specs=[full] * 4),
        out_shape=[jax.ShapeDtypeStruct((r, ccols), F32)] * 4,
    )(place, w, mine, theirs, m, v)


def adamw_group(place, halved, plain, hosted, name):
    rows = halved[0][0].shape[0]
    tr = 128
    nt = rows // 2 // tr
    nh, npl = len(halved), len(plain)

    def body(place_ref, *refs):
        own_in, h_in, own_out, h_out, _, h_sems = hosted.split(refs, 5 * nh + 4 * npl, 4 * nh + 3 * npl)
        half = pl.program_id(0)
        grid_step = half * nt + pl.program_id(1)

        @pl.when(grid_step == 0)
        def _():
            hosted.start(h_in, h_out, h_sems)

        for i in range(nh):
            w_ref, a_ref, b_ref, m_ref, v_ref = own_in[5 * i:5 * i + 5]
            g = jnp.where(half == place_ref[0], a_ref[...], b_ref[...])
            res = (g,) + _adamw_math(w_ref[...], g, m_ref[...], v_ref[...])
            for o_ref, r in zip(own_out[4 * i:4 * i + 4], res):
                o_ref[...] = r
        for i in range(npl):
            w_ref, g_ref, m_ref, v_ref = own_in[5 * nh + 4 * i:5 * nh + 4 * i + 4]
            res = _adamw_math(w_ref[...], g_ref[...], m_ref[...], v_ref[...])
            for o_ref, r in zip(own_out[4 * nh + 3 * i:4 * nh + 3 * i + 3], res):
                o_ref[...] = r

        @pl.when(grid_step == 2 * nt - 1)
        def _():
            hosted.finish(h_in, h_out, h_sems)

    def full(cols):
        return pl.BlockSpec((tr, cols), lambda h, i, s: (h * nt + i, 0))

    def part(cols):
        return pl.BlockSpec((tr, cols), lambda h, i, s: (i, 0))

    in_specs, out_specs, out_shape, args = [], [], [], []
    for w, a, b, m, v in halved:
        cols = w.shape[1]
        in_specs += [full(cols), part(cols), part(cols), full(cols), full(cols)]
        out_specs += [full(cols)] * 4
        out_shape += [jax.ShapeDtypeStruct(w.shape, F32)] * 4
        args += [w, a, b, m, v]
    for w, g, m, v in plain:
        cols = w.shape[1]
        in_specs += [full(cols)] * 4
        out_specs += [full(cols)] * 3
        out_shape += [jax.ShapeDtypeStruct(w.shape, F32)] * 3
        args += [w, g, m, v]
    h_in_specs, h_out_specs = hosted.specs()
    return pl.pallas_call(
        body, name=name,
        grid_spec=pltpu.PrefetchScalarGridSpec(
            num_scalar_prefetch=1, grid=(2, nt), in_specs=in_specs + h_in_specs, out_specs=out_specs + h_out_specs,
            scratch_shapes=hosted.scratch),
        out_shape=out_shape + hosted.out_shape,
        compiler_params=_cp(("arbitrary", "arbitrary")),
    )(place, *args, *hosted.args)


def _silu(x):
    return x * jax.nn.sigmoid(x)


def prologue(c_rows, c_ctx_row, w_ada, b_shard, rpb_flat, half_w_in, late_shards):
    shape = jax.ShapeDtypeStruct
    n_late = len(late_shards)
    half_shapes = [(w.shape[0] // 2, w.shape[1]) for w in late_shards]
    g_w = gather8([half_w_in])
    g_c = gather8([shape((8, D), F32)])
    g_m = gather8([shape((32, 1536), F32)])

    def body(*refs):
        c_ref, cc_ref, w_ref, b_ref, flat_ref, hw_ref = refs[:6]
        late_refs = refs[6:6 + n_late]
        cin_ref, mg_ref, gw_ref, bias_ref, cos_ref, sin_ref = refs[6 + n_late:12 + n_late]
        rest = refs[12 + n_late:]
        half_refs, (cg_s, ms_s, bias_s) = rest[:n_late], rest[n_late:n_late + 3]
        stage, (load_sem, bias_sem), sems = rest[n_late + 3:2 * n_late + 3], rest[2 * n_late + 3:2 * n_late + 5], \
            rest[2 * n_late + 5:]
        sw, sc, sm = sems[0:3], sems[3:6], sems[6:9]
        core = lax.axis_index("c")
        g_c.start([c_ref], [cg_s], sc)
        loads = [pltpu.make_async_copy(late_refs[a].at[pl.ds(core * half_shapes[a][0], half_shapes[a][0]), :],
                                       stage[a], load_sem.at[a]) for a in range(n_late)]
        for cp in loads:
            cp.start()
        g_c.finish([c_ref], [cg_s], sc)
        cin_ref[...] = jnp.zeros_like(cin_ref)
        for dev in range(8):
            cin_ref[2 * dev:2 * dev + 2, :] = cg_s[dev, 0:2, :]
        cin_ref[16:17, :] = cc_ref[...]
        ms_s[...] = _nn(_silu(cin_ref[...]), w_ref[...]) + b_ref[...]
        g_m.start([ms_s], [mg_ref], sm)
        g_w.start([hw_ref], [gw_ref], sw)
        for a, cp in enumerate(loads):
            cp.wait()
            half_refs[a][...] = stage[a][...].astype(BF16)
        cos_ref[...], sin_ref[...] = _rope_tables()
        stores = []
        for pair in range(NPAIR):
            if pair >= 2:
                stores[pair - 2].wait()
            _na_bias_pair(flat_ref.at[pair], bias_s.at[pair % 2])
            stores.append(pltpu.make_async_copy(bias_s.at[pair % 2], bias_ref.at[pair], bias_sem.at[pair % 2]))
            stores[pair].start()
        for cp in stores[-2:]:
            cp.wait()
        g_m.finish([ms_s], [mg_ref], sm)
        g_w.finish([hw_ref], [gw_ref], sw)

    vmem = pl.BlockSpec(memory_space=pltpu.VMEM)
    hbm = pl.BlockSpec(memory_space=pl.ANY)
    return pl.pallas_call(
        body, name="prologue", in_specs=[vmem, vmem, vmem, vmem, vmem, hbm] + [hbm] * n_late,
        out_specs=[vmem, vmem, hbm, hbm, vmem, vmem] + [vmem] * n_late,
        out_shape=[shape((32, D), F32), shape((8, 32, 1536), F32)] + g_w.out_shape
        + [shape((NPAIR,) + NA_BIAS_SHAPE, F32)] + [shape((SEQ, RD), F32)] * 2 + [shape(s, BF16) for s in half_shapes],
        scratch_shapes=[pltpu.VMEM((8, 8, D), F32), pltpu.VMEM((32, 1536), F32), pltpu.VMEM((2,) + NA_BIAS_SHAPE, F32)]
        + [pltpu.VMEM(s, F32) for s in half_shapes]
        + [pltpu.SemaphoreType.DMA((n_late,)), pltpu.SemaphoreType.DMA((2,))]
        + g_w.scratch + g_c.scratch + g_m.scratch,
        compiler_params=_cp(),
    )(c_rows, c_ctx_row, w_ada, b_shard, rpb_flat, half_w_in, *late_shards)


def ada_grads(cin, gb, gc, w_ada):
    def body(c_ref, gb_ref, gc_ref, w_ref, gw_ref, pc_ref):
        ctx_tot = jnp.sum(gc_ref[...], axis=0, keepdims=True)
        rows = lax.broadcasted_iota(jnp.int32, (16, 512), 0)
        dm = jnp.concatenate([gb_ref[...], jnp.where(rows == 0, ctx_tot, 0.0)], axis=0)
        gw_ref[...] = _tn(_silu(c_ref[...]), dm)
        rows8 = lax.broadcasted_iota(jnp.int32, (8, 512), 0)
        part = _nt(jnp.where(rows8 == 0, ctx_tot, 0.0), w_ref[...])

        @pl.when(pl.program_id(0) == 0)
        def _():
            pc_ref[...] = jnp.zeros_like(pc_ref)

        pc_ref[...] += part

    return pl.pallas_call(
        body, name="ada_grads", grid=(3,),
        in_specs=[pl.BlockSpec((32, D), lambda j: (0, 0)), pl.BlockSpec((16, 512), lambda j: (0, j)),
                  pl.BlockSpec((8, 512), lambda j: (0, j)), pl.BlockSpec((D, 512), lambda j: (0, j))],
        out_specs=[pl.BlockSpec((D, 512), lambda j: (0, j)), pl.BlockSpec((8, D), lambda j: (0, 0))],
        out_shape=[jax.ShapeDtypeStruct((D, 1536), F32), jax.ShapeDtypeStruct((8, D), F32)],
    )(cin, gb, gc, w_ada)


SMALL_SUM_ROWS = 15


def small_update(gsm, gbf, gcf, pcg, params):
    n = len(params)

    def body(*refs):
        gsm_ref, gbf_ref, gcf_ref, pcg_ref = refs[:4]
        wmv, outs, loss_out = refs[4:4 + 3 * n], refs[4 + 3 * n:4 + 7 * n], refs[-1]
        acc = gsm_ref[0]
        for dev in range(1, 8):
            acc = acc + gsm_ref[dev]
        c_ctx = wmv[0][...]
        sg = jax.nn.sigmoid(c_ctx)
        dsilu = pcg_ref[0:1, :] + pcg_ref[2:3, :] + pcg_ref[4:5, :] + pcg_ref[6:7, :]
        lane = lax.broadcasted_iota(jnp.int32, (1, D), 1)
        last = acc[14:15, :]
        grads = [
            dsilu * (sg * (1.0 + c_ctx * (1.0 - sg))),
            jnp.sum(gbf_ref[...], axis=0, keepdims=True) + jnp.sum(gcf_ref[...], axis=0, keepdims=True),
            acc[0:1, :] + acc[1:2, :], acc[2:3, :], acc[3:4, :], acc[4:5, :],
            acc[5:6, 0:512], acc[6:14, :], jnp.where(lane < 8, last, 0.0),
        ]
        loss_out[...] = jnp.broadcast_to(jnp.sum(jnp.where(lane == 8, last, 0.0), axis=1, keepdims=True), (8, 128))
        for i, g in enumerate(grads):
            d, m2, v2 = _adamw_math(wmv[3 * i][...], g, wmv[3 * i + 1][...], wmv[3 * i + 2][...])
            outs[4 * i][...] = g
            outs[4 * i + 1][...] = d
            outs[4 * i + 2][...] = m2
            outs[4 * i + 3][...] = v2

    flat = [a for wmv in params for a in wmv]
    out_shape = [jax.ShapeDtypeStruct(w.shape, F32) for w, _, _ in params for _ in range(4)]
    return pl.pallas_call(
        body, name="small_update", out_shape=out_shape + [jax.ShapeDtypeStruct((8, 128), F32)],
    )(gsm, gbf, gcf, pcg, *flat)


def _pad_row(v, rows):
    flat = v.reshape(-1)
    return jnp.pad(flat, (0, rows * D - flat.shape[0])).reshape(rows, D)


def local_step(x, ctx, tgt, mod3, rope, bias, g_pre_mix, g_post_mix, g_pre_mlp, g_post_mlp, ret_decay, ret_gn,
               wperm, late_weights, early_grads):
    nb = x.shape[0]
    tokens = nb * SEQ
    cos, sin = rope
    rd = ret_decay.T.reshape(RH, 2, 1)
    gn = ret_gn.reshape(RH, 1, RD)
    h, pret, pna = premix_proj(x, mod3, g_pre_mix, wperm, False, "premix_proj")
    hc, pretc, pnac = premix_proj(ctx, mod3, g_pre_mix, wperm, True, "premix_proj_ctx")
    o_all, mixin, gw_out = retention_fwd(pret, pretc, rd, gn, cos, sin, late_weights(0))
    mixin, gw1, gw2 = na_fwd(pna, pnac, bias, mixin, late_weights(1))
    dx_tail, dmix, h2, du, act, dm, dmixin, dmod_t, dg_t, loss_t = tail_fwd_bwd(
        x, mixin, tgt, mod3, g_post_mix, g_pre_mlp, g_post_mlp, gw_out.reshape(D, D), gw1.reshape(4, D, D),
        gw2.reshape(DFF, D))
    (dw_out,) = weight_grad([(mixin.reshape(tokens, D), dmix.reshape(tokens, D))], "grad_w_out", no_exchange(), BF16)
    (dw1,) = weight_grad([(h2.reshape(tokens, D), du.reshape(tokens, DFF))], "grad_w_mlp1", no_exchange(), BF16,
                         col_blocks=True)
    (dw2,) = weight_grad([(act.reshape(tokens, DFF), dm.reshape(tokens, D))], "grad_w_mlp2", no_exchange(), BF16)
    dproj, dprojc, drd, dgn, *landed = retention_bwd(pret, pretc, o_all, dmixin, rd, gn, cos, sin,
                                                     early_grads[0](dw_out, dw1, dw2))
    dproj, dprojc, dpat, *early = na_bwd(pna, pnac, bias, dmixin, dproj, dprojc, early_grads[1](landed))
    dw_in, *early = weight_grad([(h.reshape(tokens, D), dproj.reshape(tokens, IN_W)),
                                 (hc.reshape(nb * LC, D), dprojc.reshape(nb * LC, IN_W))], "grad_w_in",
                                early_grads[4](early), tn=IN_W // 2, tk=1024)
    dmod_c, dg_c, *late = premix_bwd(ctx, mod3, g_pre_mix, wperm, dprojc, None, early_grads[2](dw_in), "premix_bwd_ctx")
    grad_x, dmod_a, dg_a, *late = premix_bwd(x, mod3, g_pre_mix, wperm, dproj, dx_tail, early_grads[3](late),
                                             "premix_bwd")
    dmod = jnp.concatenate([jnp.concatenate([dmod_a[:, 0:2], dmod_t[:, 2:6]], axis=1), dmod_c], axis=0)
    last = jnp.pad(jnp.concatenate([drd[:, :, 0].T.reshape(8), loss_t[0, 0:1]]), (0, D - 9)).reshape(1, D)
    small = jnp.concatenate([dg_a[0:1], dg_c[0:1], dg_t[0:3], _pad_row(dgn, 1), dpat.reshape(8, D), last], axis=0)
    return grad_x, late, early, dmod, small


def kernel(x, c, ctx, c_ctx, w_ada, b_ada, g_pre_mix, g_post_mix, g_pre_mlp, g_post_mlp, w_in, ret_decay, ret_gn, na_rpb, w_out, w_mlp1, w_mlp2, loss_target, m_c_ctx, m_w_ada, m_b_ada, m_g_pre_mix, m_g_post_mix, m_g_pre_mlp, m_g_post_mlp, m_w_in, m_ret_decay, m_ret_gn, m_na_rpb, m_w_out, m_w_mlp1, m_w_mlp2, v_c_ctx, v_w_ada, v_b_ada, v_g_pre_mix, v_g_post_mix, v_g_pre_mlp, v_g_post_mlp, v_w_in, v_ret_decay, v_ret_gn, v_na_rpb, v_w_out, v_w_mlp1, v_w_mlp2):
    px, py, pc = _place()
    dev = 4 * px + 2 * py + pc
    chip = 2 * px + py

    half_w_in = lax.dynamic_slice_in_dim(w_in[0], pc * (D // 2), D // 2, 0).astype(BF16)
    cin, mg, gw_in, bias, cos, sin, *late_halves = prologue(
        jnp.pad(c, ((0, 6), (0, 0))), c_ctx[None], w_ada[0], lax.dynamic_slice_in_dim(b_ada, chip * 1536, 1536, 1),
        _rpb_flat(na_rpb[0]), half_w_in, [w_out[0], w_mlp1[0], w_mlp2[0]])
    halves = [half_w_in] + late_halves
    wperm = unpack_w_in(gw_in.reshape(4, D, 896))
    mod_all = jnp.concatenate([mg[0], mg[2], mg[4], mg[6]], axis=1)
    mod3 = (jnp.pad(lax.dynamic_slice_in_dim(mod_all, 2 * dev, 2, 0), ((0, 1), (0, 0)))
            + jnp.pad(mod_all[16:17], ((2, 0), (0, 0)))).reshape(3, 6, D)

    place = jnp.stack([pc, chip]).astype(jnp.int32)

    early_names = ["w_out", "w_mlp1", "w_mlp2"]
    early_g8, early_partial = [], []

    def early_a(dw_out, dw1, dw2):
        early_g8[:] = [dw_out.reshape(8, 128, D), dw1.reshape(8, 512, D), dw2.reshape(8, 512, D)]
        return siblings4(early_g8)

    def early_b(landed):
        early_partial[:] = chip_partial(place, early_g8, landed, "rs_chip_sum_early")
        return chips3(early_partial)

    late_partial = []

    late_g8 = []

    def late_c(dw_in):
        late_g8[:] = [pack_w_in(dw_in).reshape(8, 512, 896)]
        return siblings4(late_g8)

    def late_d(landed):
        late_partial[:] = chip_partial(place, late_g8, landed, "rs_chip_sum_w_in")
        return chips3(late_partial)

    early_mine = []

    def early_e(landed3):
        early_mine[:] = shard_sum(place, early_partial, landed3, "rs_shard_sum_early")
        return siblings(early_mine)

    grad_x, (landed3_in,), early_theirs, dmod, small = local_step(
        x, ctx, loss_target, mod3, (cos, sin), bias, g_pre_mix, g_post_mix, g_pre_mlp, g_post_mlp, ret_decay[0], ret_gn,
        wperm, lambda k: gather8(halves[1:2] if k == 0 else halves[2:4]),
        (early_a, early_b, late_c, late_d, early_e))

    pay = jnp.concatenate([dmod.reshape(18, D), small, jnp.zeros((40 - 18 - SMALL_SUM_ROWS, D), F32)], axis=0)
    (gs,) = run_hosted(gather8([pay]), "gather_small")
    gbf = gs[:, 0:12].reshape(16, 6 * D)
    gcf = gs[:, 12:18].reshape(8, 6 * D)
    gw_ada, pc_part = ada_grads(cin, lax.dynamic_slice_in_dim(gbf, chip * 1536, 1536, 1),
                                lax.dynamic_slice_in_dim(gcf, chip * 1536, 1536, 1), w_ada[0])
    (mine_in,) = shard_sum(place, late_partial, [landed3_in], "rs_shard_sum_w_in")
    theirs_in, pcg = run_hosted(both(siblings([mine_in]), gather8([pc_part])), "rs_halves_w_in_gather_c_ctx")

    grouped = adamw_group(
        place,
        [(w_mlp1[0], early_mine[1], early_theirs[1], m_w_mlp1[0], v_w_mlp1[0]),
         (w_mlp2[0], early_mine[2], early_theirs[2], m_w_mlp2[0], v_w_mlp2[0])],
        [(w_ada[0], gw_ada, m_w_ada[0], v_w_ada[0])], no_exchange(), "adamw_group")
    d_ada, m_ada, v_ada = grouped[8:11]
    big = [
        [r[None] for r in adamw_halves(place, w_in[0], mine_in, theirs_in, m_w_in[0], v_w_in[0], "adamw_w_in")],
        [r[None] for r in adamw_halves(place, w_out[0], early_mine[0], early_theirs[0], m_w_out[0], v_w_out[0],
                                       "adamw_w_out")],
        [r[None] for r in grouped[0:4]], [r[None] for r in grouped[4:8]],
    ]

    def rpb_rows(t):
        return _rpb_flat(t[0]).reshape(8, D)

    def decay_row(t):
        return jnp.pad(t.reshape(1, 8), ((0, 0), (0, D - 8)))

    views = [lambda t: t.reshape(1, D), lambda t: t, lambda t: t, lambda t: t, lambda t: t, lambda t: t, lambda t: t,
             rpb_rows, decay_row]
    back = [lambda t: t.reshape(D), lambda t: t, lambda t: t, lambda t: t, lambda t: t, lambda t: t, lambda t: t,
            lambda t: _rpb_flat_t(t)[None], lambda t: t[:, 0:8].reshape(1, 2, 4)]
    small_w = (c_ctx, b_ada, g_pre_mix, g_post_mix, g_pre_mlp, g_post_mlp, ret_gn, na_rpb, ret_decay)
    small_m = (m_c_ctx, m_b_ada, m_g_pre_mix, m_g_post_mix, m_g_pre_mlp, m_g_post_mlp, m_ret_gn, m_na_rpb, m_ret_decay)
    small_v = (v_c_ctx, v_b_ada, v_g_pre_mix, v_g_post_mix, v_g_pre_mlp, v_g_post_mlp, v_ret_gn, v_na_rpb, v_ret_decay)
    *res, loss8 = small_update(gs[:, 18:18 + SMALL_SUM_ROWS], gbf, gcf, pcg[:, 0],
                               [(f(w), f(m), f(v)) for f, w, m, v in zip(views, small_w, small_m, small_v)])

    def leaves(ada, idx):
        s_c, s_b, s_g1, s_g2, s_g3, s_g4, s_gn, s_rpb, s_rd = [back[i](res[4 * i + idx]) for i in range(9)]
        return [s_c, ada[None], s_b, s_g1, s_g2, s_g3, s_g4, big[0][idx], s_rd, s_gn, s_rpb,
                big[1][idx], big[2][idx], big[3][idx]]

    return (loss8[0, 0], grad_x, *leaves(gw_ada, 0), *leaves(d_ada, 1), *leaves(m_ada, 2), *leaves(v_ada, 3))
```

```python
import functools
import math

import jax
import jax.numpy as jnp
from jax import lax
from jax.experimental import pallas as pl
from jax.experimental.pallas import tpu as pltpu

F32, BF16 = jnp.float32, jnp.bfloat16
D = 1024
SEQ = 2048
LC = 256
GW = 64
RH, RD, CH = 4, 128, 128
NPAIR = 4
IN_W = 3584
RET_W = 2048
DFF = 4096
EPS = 1e-6
NEG = -1e30
TN = 256
NCH = SEQ // CH
LR, B1, B2, AEPS, WD, STEP = 0.001, 0.9, 0.999, 1e-08, 0.01, 10
MESH = pl.DeviceIdType.MESH
VMEM_LIMIT = 56 * 1024 * 1024


def _cp(sem=None):
    return pltpu.CompilerParams(dimension_semantics=sem, vmem_limit_bytes=VMEM_LIMIT)


def _nn(a, b):
    return jnp.dot(a.astype(BF16), b.astype(BF16), preferred_element_type=F32)


def _nt(a, b):
    return lax.dot_general(a.astype(BF16), b.astype(BF16), (((1,), (1,)), ((), ())), preferred_element_type=F32)


def _tn(a, b):
    return lax.dot_general(a.astype(BF16), b.astype(BF16), (((0,), (0,)), ((), ())), preferred_element_type=F32)


@jax.custom_vjp
def mm_tn(a, b):
    return _tn(a, b)


mm_tn.defvjp(lambda a, b: (_tn(a, b), (a, b)), lambda r, g: (_nt(r[1], g), _nn(r[0], g)))


def _rms(x):
    return x * lax.rsqrt(jnp.mean(x * x, axis=-1, keepdims=True) + EPS)


def _rms_mod(x, g, sc, sh):
    return (_rms(x) * g) * (1.0 + sc) + sh


def _post_mix(x, mix, gt1, sc2, sh2, g_post_mix, g_pre_mlp):
    x1 = x + gt1 * (_rms(mix) * g_post_mix)
    return x1, _rms_mod(x1, g_pre_mlp, sc2, sh2)


def _head_loss(x1, m, gt2, g_post_mlp, tgt):
    err = x1 + gt2 * (_rms(m) * g_post_mlp) - tgt
    return 0.5 * jnp.sum(jnp.mean(err * err, axis=-1, keepdims=True), axis=0, keepdims=True)


def _ln_gate(o, g, w):
    mu = jnp.mean(o, axis=-1, keepdims=True)
    var = jnp.mean(jnp.square(o - mu), axis=-1, keepdims=True)
    y = (o - mu) * lax.rsqrt(var + EPS)
    return (y * w) * (g * jax.nn.sigmoid(g))


def _pair_order(x):
    lane = lax.broadcasted_iota(jnp.int32, x.shape, 1)
    return jnp.where((lane >= 32) & (lane < 64), pltpu.roll(x, 96, 1),
                     jnp.where((lane >= 64) & (lane < 96), pltpu.roll(x, 32, 1), x))


def _rope(x, cos, sin):
    return x * cos + pltpu.roll(x, 64, 1) * sin


def _rope_t(g, cos, sin):
    return g * cos + pltpu.roll(g * sin, 64, 1)


def _rope_tables():
    tok = lax.broadcasted_iota(jnp.int32, (SEQ, RD), 0)
    lane = lax.broadcasted_iota(jnp.int32, (SEQ, RD), 1)
    pos = jnp.where((lane & 32) == 0, tok >> 6, tok & (GW - 1)).astype(F32)
    ang = pos * jnp.exp((lane & 31).astype(F32) * (-math.log(10000.0) / 32))
    return jnp.cos(ang), jnp.where(lane < 64, -jnp.sin(ang), jnp.sin(ang))


def _chunk_loop(n, body, init, k=4):
    def several(t, carry):
        for i in range(k):
            carry = body(k * t + i, carry)
        return carry

    return lax.fori_loop(0, n // k, several, init)


def _fiota(shape, dim):
    return lax.broadcasted_iota(jnp.int32, shape, dim).astype(F32)


def _ret_state(k, v, s, lg, reverse):
    pos = _fiota((CH, 1), 0)
    b_exp = pos if reverse else (CH - 1.0 - pos)
    return jnp.exp(lg * CH) * s + mm_tn(k * jnp.exp(lg * b_exp), v)


class _Decays:
    def __init__(self, lgs):
        i, j, pos = _fiota((CH, CH), 0), _fiota((CH, CH), 1), _fiota((CH, 1), 0)
        diffs = (i - j, j - i)
        keep = (diffs[0] >= 0, diffs[1] > 0)
        mats = [jnp.where(m, jnp.exp(lg * jnp.where(m, d, 0.0)), 0.0) for lg, d, m in zip(lgs, diffs, keep)]
        self.mask = mats[0] + mats[1]
        self.dmask = [mats[0] * diffs[0], mats[1] * diffs[1]]
        a_exp, b_exp = (pos + 1.0, CH - pos), (CH - 1.0 - pos, pos)
        self.a = [jnp.exp(lg * e) for lg, e in zip(lgs, a_exp)]
        self.b = [jnp.exp(lg * e) for lg, e in zip(lgs, b_exp)]
        self.da = [a * e for a, e in zip(self.a, a_exp)]
        self.db = [b * e for b, e in zip(self.b, b_exp)]
        self.g = [jnp.exp(lg * CH) for lg in lgs]


def _both(x, w):
    return jnp.concatenate([x * w[0], x * w[1]], axis=1)


def _total(x):
    return jnp.sum(jnp.sum(x, axis=1, keepdims=True), axis=0, keepdims=True)


def _state_pass(dec, init, k_s, v_of, st_s):
    def step(t, carry):
        out = []
        for d, s in enumerate(carry):
            n = (NCH - 1 - t) if d else t
            sl = pl.ds(pl.multiple_of(n * CH, CH), CH)
            st_s[n, d * RD:(d + 1) * RD, :] = s
            out.append(dec.g[d] * s + _tn(k_s[sl, :] * dec.b[d], v_of(sl)))
        return tuple(out)

    _chunk_loop(NCH, step, tuple(init))


def premix_proj(xin, mod3, g_pre, wperm, is_ctx, name):
    nb, length, _ = xin.shape
    tn = min(2 * TN, length)

    def body(x_ref, mod_ref, g_ref, w_ref, h_ref, pret_ref, pna_ref):
        h = _rms_mod(x_ref[...], g_ref[...], mod_ref[1:2, :], mod_ref[0:1, :])
        hb = h.astype(BF16)
        h_ref[...] = hb
        pret_ref[...] = jnp.dot(hb, w_ref[:, :RET_W], preferred_element_type=F32)
        pna_ref[...] = jnp.dot(hb, w_ref[:, RET_W:], preferred_element_type=F32).astype(BF16)

    return pl.pallas_call(
        body, name=name, grid=(nb, length // tn),
        in_specs=[
            pl.BlockSpec((None, tn, D), lambda b, t: (b, t, 0)),
            pl.BlockSpec((None, 6, D), (lambda b, t: (2, 0, 0)) if is_ctx else (lambda b, t: (b, 0, 0))),
            pl.BlockSpec((1, D), lambda b, t: (0, 0)),
            pl.BlockSpec((D, IN_W), lambda b, t: (0, 0), pipeline_mode=pl.Buffered(1)),
        ],
        out_specs=[
            pl.BlockSpec((None, tn, D), lambda b, t: (b, t, 0)),
            pl.BlockSpec((None, tn, RET_W), lambda b, t: (b, t, 0)),
            pl.BlockSpec((None, tn, IN_W - RET_W), lambda b, t: (b, t, 0)),
        ],
        out_shape=[
            jax.ShapeDtypeStruct((nb, length, D), BF16),
            jax.ShapeDtypeStruct((nb, length, RET_W), F32),
            jax.ShapeDtypeStruct((nb, length, IN_W - RET_W), BF16),
        ],
        compiler_params=_cp(("arbitrary", "arbitrary")),
    )(xin, mod3, g_pre, wperm)


def premix_bwd(xin, mod3, g_pre, wperm, dproj, dx_tail, hosted, name):
    nb, length, _ = xin.shape
    tn = min(TN, length)
    is_ctx = dx_tail is None

    def body(*refs):
        own_in, h_in, own_out, h_out, _, h_sems = hosted.split(refs, 5 if is_ctx else 6, 2 if is_ctx else 3)
        if is_ctx:
            (x_ref, mod_ref, g_ref, w_ref, dp_ref), (dmod_ref, dg_ref) = own_in, own_out
        else:
            (x_ref, mod_ref, g_ref, w_ref, dp_ref, dxt_ref), (dx_ref, dmod_ref, dg_ref) = own_in, own_out
        b, t = pl.program_id(0), pl.program_id(1)
        grid_step = b * (length // tn) + t

        @pl.when(grid_step == 0)
        def _():
            hosted.start(h_in, h_out, h_sems)

        @pl.when(grid_step == nb * (length // tn) - 1)
        def _():
            hosted.finish(h_in, h_out, h_sems)

        dh = lax.dot_general(dp_ref[...], w_ref[...], (((1,), (1,)), ((), ())), preferred_element_type=F32)
        _, vjp = jax.vjp(_rms_mod, x_ref[...], g_ref[...], mod_ref[1:2, :], mod_ref[0:1, :])
        dx, dg, dsc, dsh = vjp(dh)
        if not is_ctx:
            dx_ref[...] = dx + dxt_ref[...]

        @pl.when((t == 0) & ((b == 0) if is_ctx else True))
        def _():
            dmod_ref[...] = jnp.zeros_like(dmod_ref)

        @pl.when((t == 0) & (b == 0))
        def _():
            dg_ref[...] = jnp.zeros_like(dg_ref)

        dmod_ref[0:1, :] += dsh
        dmod_ref[1:2, :] += dsc
        dg_ref[0:1, :] += dg

    tok = lambda b, t: (b, t, 0)
    in_specs = [
        pl.BlockSpec((None, tn, D), tok),
        pl.BlockSpec((None, 6, D), (lambda b, t: (2, 0, 0)) if is_ctx else (lambda b, t: (b, 0, 0))),
        pl.BlockSpec((1, D), lambda b, t: (0, 0)),
        pl.BlockSpec((D, IN_W), lambda b, t: (0, 0), pipeline_mode=pl.Buffered(1)),
        pl.BlockSpec((None, tn, IN_W), tok),
    ]
    args = [xin, mod3, g_pre, wperm, dproj]
    out_specs = [
        pl.BlockSpec((None, 6, D), (lambda b, t: (0, 0, 0)) if is_ctx else (lambda b, t: (b, 0, 0))),
        pl.BlockSpec((8, D), lambda b, t: (0, 0)),
    ]
    out_shape = [jax.ShapeDtypeStruct((1 if is_ctx else nb, 6, D), F32), jax.ShapeDtypeStruct((8, D), F32)]
    if not is_ctx:
        in_specs.append(pl.BlockSpec((None, tn, D), tok))
        args.append(dx_tail)
        out_specs.insert(0, pl.BlockSpec((None, tn, D), tok))
        out_shape.insert(0, jax.ShapeDtypeStruct((nb, length, D), F32))
    h_in_specs, h_out_specs = hosted.specs()
    return pl.pallas_call(
        body, name=name, grid=(nb, length // tn), in_specs=in_specs + h_in_specs, out_specs=out_specs + h_out_specs,
        out_shape=out_shape + hosted.out_shape, scratch_shapes=hosted.scratch,
        compiler_params=_cp(("arbitrary", "arbitrary")),
    )(*args, *hosted.args)


def _ret_specs(order):
    def im(f):
        return lambda *g: f(*order(*g))
    return dict(
        pret=pl.BlockSpec((None, SEQ, 512), im(lambda b, h: (b, 0, h))),
        pretc=pl.BlockSpec((None, LC, 512), im(lambda b, h: (b, 0, h))),
        rd=pl.BlockSpec((None, 2, 1), im(lambda b, h: (h, 0, 0))),
        gn=pl.BlockSpec((None, 1, RD), im(lambda b, h: (h, 0, 0))),
        tab=pl.BlockSpec((SEQ, RD), im(lambda b, h: (0, 0))),
        head=pl.BlockSpec((None, SEQ, RD), im(lambda b, h: (b, 0, h))),
    )


def retention_fwd(pret, pretc, rd, gn, cos, sin, hosted):
    nb = pret.shape[0]
    sp = _ret_specs(lambda b, h: (b, h))

    def body(*refs):
        own_in, h_in, own_out, h_out, own_scr, h_sems = hosted.split(refs, 6, 2)
        p_ref, pc_ref, rd_ref, gn_ref, cos_ref, sin_ref = own_in
        (o_ref, mix_ref), (q_s, k_s, o_s, st_s) = own_out, own_scr
        grid_step = pl.program_id(0) * RH + pl.program_id(1)

        @pl.when(grid_step == 0)
        def _():
            hosted.start(h_in, h_out, h_sems)

        cos_v, sin_v = cos_ref[...], sin_ref[...]
        q_s[...] = _rope(p_ref[:, 0:128], cos_v, sin_v) * (RD ** -0.5)
        k_s[...] = _rope(p_ref[:, 128:256], cos_v, sin_v)
        lgs, init = [], []
        for rev in (False, True):
            lg = jax.nn.log_sigmoid(rd_ref[int(rev):int(rev) + 1, :])
            s = jnp.zeros((RD, RD), F32)
            for n in ((1, 0) if rev else (0, 1)):
                s = _ret_state(pc_ref[n * CH:(n + 1) * CH, 128:256], pc_ref[n * CH:(n + 1) * CH, 256:384], s, lg, rev)
            lgs.append(lg)
            init.append(s)

        dec = _Decays(lgs)
        _state_pass(dec, init, k_s, lambda sl: p_ref[sl, 256:384], st_s)

        def chunk(n, carry):
            sl = pl.ds(pl.multiple_of(n * CH, CH), CH)
            q = q_s[sl, :]
            o_s[sl, :] = (_nn(_nt(q, k_s[sl, :]) * dec.mask, p_ref[sl, 256:384]) + _nn(_both(q, dec.a), st_s[n]))
            return carry

        _chunk_loop(NCH, chunk, 0)
        o = o_s[...]
        o_ref[...] = o
        mix_ref[...] = _ln_gate(o, p_ref[:, 384:512], gn_ref[...]).astype(BF16)

        @pl.when(grid_step == nb * RH - 1)
        def _():
            hosted.finish(h_in, h_out, h_sems)

    h_in_specs, h_out_specs = hosted.specs()
    return pl.pallas_call(
        body, name="retention_fwd", grid=(nb, RH),
        in_specs=[sp["pret"], sp["pretc"], sp["rd"], sp["gn"], sp["tab"], sp["tab"]] + h_in_specs,
        out_specs=[sp["head"], sp["head"]] + h_out_specs,
        out_shape=[jax.ShapeDtypeStruct((nb, SEQ, RH * RD), F32), jax.ShapeDtypeStruct((nb, SEQ, D), BF16)]
        + hosted.out_shape,
        scratch_shapes=[pltpu.VMEM((SEQ, RD), F32)] * 3 + [pltpu.VMEM((NCH, 2 * RD, RD), F32)] + hosted.scratch,
        compiler_params=_cp(("arbitrary", "arbitrary")),
    )(pret, pretc, rd, gn, cos, sin, *hosted.args)


def retention_bwd(pret, pretc, o_all, dmixin, rd, gn, cos, sin, hosted):
    nb = pret.shape[0]
    sp = _ret_specs(lambda h, b: (b, h))

    def body(*refs):
        own_in, h_in, own_out, h_out, own_scr, h_sems = hosted.split(refs, 8, 4)
        p_ref, pc_ref, o_ref, dmix_ref, rd_ref, gn_ref, cos_ref, sin_ref = own_in
        dp_ref, dpc_ref, drd_ref, dgn_ref = own_out
        q_s, k_s, do_s, dq_s, dk_s, dv_s, st_s, gst_s = own_scr
        b = pl.program_id(1)
        grid_step = pl.program_id(0) * nb + b

        @pl.when(grid_step == 0)
        def _():
            hosted.start(h_in, h_out, h_sems)

        cos_v, sin_v = cos_ref[...], sin_ref[...]
        q_s[...] = _rope(p_ref[:, 0:128], cos_v, sin_v) * (RD ** -0.5)
        k_s[...] = _rope(p_ref[:, 128:256], cos_v, sin_v)
        _, gate_vjp = jax.vjp(_ln_gate, o_ref[...], p_ref[:, 384:512], gn_ref[...])
        do, dg, dgn = gate_vjp(dmix_ref[...].astype(F32))
        do_s[...] = do
        dp_ref[:, 384:512] = dg.astype(BF16)

        @pl.when(b == 0)
        def _():
            drd_ref[...] = jnp.zeros_like(drd_ref)
            dgn_ref[...] = jnp.zeros_like(dgn_ref)

        dgn_ref[...] += dgn
        kcs = [pc_ref[n * CH:(n + 1) * CH, 128:256] for n in (0, 1)]
        vcs = [pc_ref[n * CH:(n + 1) * CH, 256:384] for n in (0, 1)]
        dirs = []
        init = []
        for rev in (False, True):
            rdv = rd_ref[int(rev):int(rev) + 1, :]
            lg = jax.nn.log_sigmoid(rdv)
            order_c = (1, 0) if rev else (0, 1)
            s = jnp.zeros((RD, RD), F32)
            ctx_states = []
            for n in order_c:
                ctx_states.append(s)
                s = _ret_state(kcs[n], vcs[n], s, lg, rev)
            dirs.append((rev, order_c, lg, rdv, ctx_states))
            init.append(s)
        dec = _Decays([lg for _, _, lg, _, _ in dirs])

        def v_of(sl):
            return p_ref[sl, 256:384]

        _state_pass(dec, init, k_s, v_of, st_s)
        zeros = jnp.zeros((CH, RD), F32)

        def scores_back(n, carry):
            dmask_sum, da_f, da_b = carry
            sl = pl.ds(pl.multiple_of(n * CH, CH), CH)
            q, k, v, do = q_s[sl, :], k_s[sl, :], v_of(sl), do_s[sl, :]
            scores = _nt(q, k)
            d_att = _nt(do, v)
            d_scores = d_att * dec.mask
            d_qa = _nt(do, st_s[n])
            d_qf, d_qb = d_qa[:, 0:RD], d_qa[:, RD:2 * RD]
            dq_s[sl, :] = _nn(d_scores, k) + d_qf * dec.a[0] + d_qb * dec.a[1]
            dk_s[sl, :] = _tn(d_scores, q)
            dv_s[sl, :] = _tn(scores * dec.mask, do)
            gst_s[n] = _tn(_both(q, dec.a), do)
            return dmask_sum + d_att * scores, da_f + d_qf * q, da_b + d_qb * q

        dmask_sum, da_f, da_b = _chunk_loop(NCH, scores_back, (zeros, zeros, zeros))

        def state_back(t, carry):
            out = []
            for d, r in enumerate(carry):
                n = t if d else (NCH - 1 - t)
                rows = slice(d * RD, (d + 1) * RD)
                own = gst_s[n, rows, :]
                gst_s[n, rows, :] = r
                out.append(own + dec.g[d] * r)
            return tuple(out)

        d_states = _chunk_loop(NCH, state_back, (zeros, zeros))

        def updates_back(n, carry):
            db_f, db_b, dg_f, dg_b = carry
            sl = pl.ds(pl.multiple_of(n * CH, CH), CH)
            k, r, s = k_s[sl, :], gst_s[n], st_s[n]
            d_kw = _nt(v_of(sl), r)
            d_kf, d_kb = d_kw[:, 0:RD], d_kw[:, RD:2 * RD]
            dk_s[sl, :] += d_kf * dec.b[0] + d_kb * dec.b[1]
            dv_s[sl, :] += _nn(_both(k, dec.b), r)
            return (db_f + d_kf * k, db_b + d_kb * k, dg_f + r[0:RD, :] * s[0:RD, :],
                    dg_b + r[RD:2 * RD, :] * s[RD:2 * RD, :])

        db_dg = _chunk_loop(NCH, updates_back, (zeros, zeros, zeros, zeros))
        dkc = [None, None]
        dvc = [None, None]
        for d, ((rev, order_c, lg, rdv, ctx_states), ds) in enumerate(zip(dirs, d_states)):
            dlg = (_total(dmask_sum * dec.dmask[d]) + _total((da_f, da_b)[d] * dec.da[d])
                   + _total(db_dg[d] * dec.db[d]) + CH * dec.g[d] * _total(db_dg[2 + d]))
            for idx in (1, 0):
                n = order_c[idx]
                _, vjp = jax.vjp(functools.partial(_ret_state, reverse=rev), kcs[n], vcs[n], ctx_states[idx], lg)
                dk_c, dv_c, ds, dl = vjp(ds)
                dlg = dlg + dl
                dkc[n] = dk_c if dkc[n] is None else dkc[n] + dk_c
                dvc[n] = dv_c if dvc[n] is None else dvc[n] + dv_c
            drd_ref[int(rev):int(rev) + 1, :] += dlg * jax.nn.sigmoid(-rdv)
        dp_ref[:, 0:128] = _rope_t(dq_s[...] * (RD ** -0.5), cos_v, sin_v).astype(BF16)
        dp_ref[:, 128:256] = _rope_t(dk_s[...], cos_v, sin_v).astype(BF16)
        dp_ref[:, 256:384] = dv_s[...].astype(BF16)
        zero = jnp.zeros((CH, RD), BF16)
        for n in (0, 1):
            rows = slice(n * CH, (n + 1) * CH)
            dpc_ref[rows, 0:128] = zero
            dpc_ref[rows, 128:256] = dkc[n].astype(BF16)
            dpc_ref[rows, 256:384] = dvc[n].astype(BF16)
            dpc_ref[rows, 384:512] = zero

        @pl.when(grid_step == RH * nb - 1)
        def _():
            hosted.finish(h_in, h_out, h_sems)

    h_in_specs, h_out_specs = hosted.specs()
    return pl.pallas_call(
        body, name="retention_bwd", grid=(RH, nb),
        in_specs=[sp["pret"], sp["pretc"], sp["head"], sp["head"], sp["rd"], sp["gn"], sp["tab"], sp["tab"]]
        + h_in_specs,
        out_specs=[
            pl.BlockSpec((None, SEQ, 512), lambda h, b: (b, 0, h)),
            pl.BlockSpec((None, LC, 512), lambda h, b: (b, 0, h)),
            pl.BlockSpec((None, 2, 1), lambda h, b: (h, 0, 0)),
            pl.BlockSpec((None, 1, RD), lambda h, b: (h, 0, 0)),
        ] + h_out_specs,
        out_shape=[
            jax.ShapeDtypeStruct((nb, SEQ, IN_W), BF16),
            jax.ShapeDtypeStruct((nb, LC, IN_W), BF16),
            jax.ShapeDtypeStruct((RH, 2, 1), F32),
            jax.ShapeDtypeStruct((RH, 1, RD), F32),
        ] + hosted.out_shape,
        scratch_shapes=[pltpu.VMEM((SEQ, RD), F32)] * 6 + [pltpu.VMEM((NCH, 2 * RD, RD), F32)] * 2 + hosted.scratch,
        compiler_params=_cp(("arbitrary", "arbitrary")),
    )(pret, pretc, o_all, dmixin, rd, gn, cos, sin, *hosted.args)


def _rpb_flat(rpb):
    return jnp.pad(rpb, ((0, 0), (0, 1), (0, 33))).reshape(NPAIR, 2, 1, 1024)


def _rpb_flat_t(dflat):
    return dflat.reshape(8, 16, 64)[:, :15, :31]


def _barrel(x, left):
    row = lax.broadcasted_iota(jnp.int32, x.shape, 0)
    n = x.shape[1]
    for bit in range(6):
        s = 1 << bit
        x = jnp.where(((row >> bit) & 1) == 1, pltpu.roll(x, (n - s) if left else s, 1), x)
    return x


NA_TILE_ROWS, NA_BAND_ROWS = 4, 12
NA_Q, NA_K = NA_TILE_ROWS * GW, NA_BAND_ROWS * GW
NA_TILES = SEQ // NA_Q


def _band_start(r0):
    return min(max(r0 - 4, 0), 32 - NA_BAND_ROWS)


def _tile_layout(t):
    rows = range(t * NA_TILE_ROWS, (t + 1) * NA_TILE_ROWS)
    return tuple((r if r < 4 else (r - 24 if r > 28 else 4), min(max(r - 4, 0), 24) - _band_start(rows[0]))
                 for r in rows)


NA_CLASSES = sorted(set(_tile_layout(t) for t in range(NA_TILES)))


def _tile_rows(cls):
    return NA_CLASSES[cls]


def _na_tile(t):
    start = jnp.clip(NA_TILE_ROWS * t - 4, 0, 32 - NA_BAND_ROWS)
    cls = 0
    for tile in range(NA_TILES):
        cls = jnp.where(t == tile, NA_CLASSES.index(_tile_layout(tile)), cls)
    return pl.ds(pl.multiple_of(t * NA_Q, NA_Q), NA_Q), pl.ds(pl.multiple_of(start * GW, NA_Q), NA_K), cls


def _na_probs(qst, kb, kc, bias):
    s_loc = _nt(qst, kb) + bias
    s_ctx = _nt(qst, kc)
    m = jnp.maximum(jnp.max(s_loc, axis=1, keepdims=True), jnp.max(s_ctx, axis=1, keepdims=True))
    e_loc, e_ctx = jnp.exp(s_loc - m), jnp.exp(s_ctx - m)
    den = jnp.sum(e_loc, axis=1, keepdims=True) + jnp.sum(e_ctx, axis=1, keepdims=True)
    return e_loc / den, e_ctx / den


def _stack_heads(t):
    lane = lax.broadcasted_iota(jnp.int32, t.shape, 1)
    zero = jnp.zeros_like(t)
    return jnp.concatenate([jnp.where(lane < 64, t, zero), jnp.where(lane >= 64, t, zero)], axis=0)


def _unstack_heads(t):
    n = t.shape[0] // 2
    lane = lax.broadcasted_iota(jnp.int32, (n, 128), 1)
    return jnp.where(lane < 64, t[:n], t[n:])


NA_BIAS_SHAPE = (len(NA_CLASSES), 2 * NA_Q, NA_K)


def _na_bias_pair(flat_ref, out_ref):
    qc = lax.broadcasted_iota(jnp.int32, (GW, 512), 0)
    kc = lax.broadcasted_iota(jnp.int32, (GW, 512), 1) & 63
    start = jnp.clip(qc - 8, 0, GW - 16)
    window = (kc >= start) & (kc < start + 16)
    fill = jnp.full((GW, NA_K - 512), NEG, F32)
    for hh in (0, 1):
        skew = _barrel(pltpu.roll(jnp.broadcast_to(flat_ref[hh], (GW, 1024)), 1024 - 15, 1), left=False)
        by_class = [jnp.where(window, (skew if rc == 7 else pltpu.roll(skew, (9 + rc) * 64, 1))[:, 0:512], NEG)
                    for rc in range(8)]
        for cls in range(len(NA_CLASSES)):
            for qr, (rc, off) in enumerate(_tile_rows(cls)):
                w = jnp.concatenate([by_class[rc], fill], axis=1)
                rows = slice(hh * NA_Q + qr * GW, hh * NA_Q + (qr + 1) * GW)
                out_ref[cls, rows, :] = pltpu.roll(w, off * GW, 1) if off else w


def na_fwd(pna, pnac, bias, mixin, hosted):
    nb = pna.shape[0]

    def body(*refs):
        (p_ref, pc_ref, bias_ref, _), h_in, (out_ref,), h_out, _, h_sems = hosted.split(refs, 4, 1)
        grid_step = pl.program_id(0) * nb + pl.program_id(1)

        @pl.when(grid_step == 0)
        def _():
            hosted.start(h_in, h_out, h_sems)

        kc, vc = pc_ref[:, 128:256], pc_ref[:, 256:384]

        def tile(t, carry):
            qsl, bsl, cls = _na_tile(t)
            kb, vb = p_ref[bsl, 128:256], p_ref[bsl, 256:384]
            p_loc, p_ctx = _na_probs(_stack_heads(p_ref[qsl, 0:128] * 0.125), kb, kc, bias_ref[cls])
            out_ref[qsl, :] = _unstack_heads(_nn(p_loc, vb) + _nn(p_ctx, vc)).astype(BF16)
            return carry

        lax.fori_loop(0, NA_TILES, tile, 0, unroll=4)

        @pl.when(grid_step == NPAIR * nb - 1)
        def _():
            hosted.finish(h_in, h_out, h_sems)

    h_in_specs, h_out_specs = hosted.specs()
    return pl.pallas_call(
        body, name="na_fwd", grid=(NPAIR, nb),
        in_specs=[
            pl.BlockSpec((None, SEQ, 384), lambda p, b: (b, 0, p)),
            pl.BlockSpec((None, LC, 384), lambda p, b: (b, 0, p)),
            pl.BlockSpec((None, len(NA_CLASSES), 2 * NA_Q, NA_K), lambda p, b: (p, 0, 0, 0)),
            pl.BlockSpec(memory_space=pl.ANY),
        ] + h_in_specs,
        out_specs=[pl.BlockSpec((None, SEQ, 128), lambda p, b: (b, 0, 4 + p))] + h_out_specs,
        out_shape=[jax.ShapeDtypeStruct((nb, SEQ, D), BF16)] + hosted.out_shape,
        input_output_aliases={3: 0},
        scratch_shapes=hosted.scratch,
        compiler_params=_cp(("arbitrary", "arbitrary")),
    )(pna, pnac, bias, mixin, *hosted.args)


def na_bwd(pna, pnac, bias, dmixin, dproj, dprojc, hosted):
    nb = pna.shape[0]

    def body(*refs):
        own_in, h_in, own_out, h_out, own_scr, h_sems = hosted.split(refs, 6, 3)
        p_ref, pc_ref, bias_ref, dmix_ref = own_in[:4]
        dp_ref, dpc_ref, dpat_ref = own_out
        dbias_s, dk_s, dv_s, dkc_s, dvc_s, res_s, resc_s = own_scr
        b, part = pl.program_id(1), pl.program_id(2)
        grid_step = (pl.program_id(0) * nb + b) * 3 + part

        @pl.when(grid_step == 0)
        def _():
            hosted.start(h_in, h_out, h_sems)

        @pl.when(grid_step == NPAIR * nb * 3 - 1)
        def _():
            hosted.finish(h_in, h_out, h_sems)

        @pl.when(part == 0)
        def _():
            @pl.when(b == 0)
            def _():
                dbias_s[...] = jnp.zeros_like(dbias_s)

            dk_s[...] = jnp.zeros_like(dk_s)
            dv_s[...] = jnp.zeros_like(dv_s)
            dkc_s[...] = jnp.zeros_like(dkc_s)
            dvc_s[...] = jnp.zeros_like(dvc_s)
            kc, vc = pc_ref[:, 128:256], pc_ref[:, 256:384]

            def tile(t, carry):
                qsl, bsl, cls = _na_tile(t)
                kb, vb = p_ref[bsl, 128:256], p_ref[bsl, 256:384]
                qst, dost = _stack_heads(p_ref[qsl, 0:128] * 0.125), _stack_heads(dmix_ref[qsl, :])
                p_loc, p_ctx = _na_probs(qst, kb, kc, bias_ref[cls])
                dp_loc, dp_ctx = _nt(dost, vb), _nt(dost, vc)
                delta = (jnp.sum(p_loc * dp_loc, axis=1, keepdims=True)
                         + jnp.sum(p_ctx * dp_ctx, axis=1, keepdims=True))
                ds_loc, ds_ctx = p_loc * (dp_loc - delta), p_ctx * (dp_ctx - delta)
                dbias_s[cls] += ds_loc
                res_s[0, qsl, :] = _unstack_heads((_nn(ds_loc, kb) + _nn(ds_ctx, kc)) * 0.125).astype(BF16)
                dk_s[bsl, :] += _tn(ds_loc, qst)
                dv_s[bsl, :] += _tn(p_loc, dost)
                dkc_s[...] += _tn(ds_ctx, qst)
                dvc_s[...] += _tn(p_ctx, dost)
                return carry

            lax.fori_loop(0, NA_TILES, tile, 0, unroll=2)
            res_s[1] = dk_s[...].astype(BF16)
            res_s[2] = dv_s[...].astype(BF16)
            resc_s[0] = jnp.zeros((LC, 128), BF16)
            resc_s[1] = dkc_s[...].astype(BF16)
            resc_s[2] = dvc_s[...].astype(BF16)

            @pl.when(b == nb - 1)
            def _():
                for hh in (0, 1):
                    by_class = [None] * 8
                    for cls in range(len(NA_CLASSES)):
                        for qr, (rc, off) in enumerate(_tile_rows(cls)):
                            w = dbias_s[cls, hh * NA_Q + qr * GW:hh * NA_Q + (qr + 1) * GW, :]
                            w = (pltpu.roll(w, NA_K - off * GW, 1) if off else w)[:, 0:512]
                            by_class[rc] = w if by_class[rc] is None else by_class[rc] + w
                    skew = jnp.zeros((GW, 1024), F32)
                    for rc in range(8):
                        w = jnp.concatenate([by_class[rc], jnp.zeros((GW, 512), F32)], axis=1)
                        skew = skew + (w if rc == 7 else pltpu.roll(w, (7 - rc) * 64, 1))
                    dpat_ref[hh] = jnp.sum(pltpu.roll(_barrel(skew, left=True), 15, 1), axis=0, keepdims=True)

        dp_ref[...] = res_s[part]
        dpc_ref[...] = resc_s[part]

    h_in_specs, h_out_specs = hosted.specs()
    return pl.pallas_call(
        body, name="na_bwd", grid=(NPAIR, nb, 3),
        in_specs=[
            pl.BlockSpec((None, SEQ, 384), lambda p, b, s: (b, 0, p)),
            pl.BlockSpec((None, LC, 384), lambda p, b, s: (b, 0, p)),
            pl.BlockSpec((None, len(NA_CLASSES), 2 * NA_Q, NA_K), lambda p, b, s: (p, 0, 0, 0)),
            pl.BlockSpec((None, SEQ, 128), lambda p, b, s: (b, 0, 4 + p)),
            pl.BlockSpec(memory_space=pl.ANY),
            pl.BlockSpec(memory_space=pl.ANY),
        ] + h_in_specs,
        out_specs=[
            pl.BlockSpec((None, SEQ, 128), lambda p, b, s: (b, 0, 16 + 3 * p + s)),
            pl.BlockSpec((None, LC, 128), lambda p, b, s: (b, 0, 16 + 3 * p + s)),
            pl.BlockSpec((None, 2, 1, 1024), lambda p, b, s: (p, 0, 0, 0)),
        ] + h_out_specs,
        out_shape=[
            jax.ShapeDtypeStruct((nb, SEQ, IN_W), BF16),
            jax.ShapeDtypeStruct((nb, LC, IN_W), BF16),
            jax.ShapeDtypeStruct((NPAIR, 2, 1, 1024), F32),
        ] + hosted.out_shape,
        input_output_aliases={4: 0, 5: 1},
        scratch_shapes=[
            pltpu.VMEM((len(NA_CLASSES), 2 * NA_Q, NA_K), F32),
            pltpu.VMEM((SEQ, 128), F32), pltpu.VMEM((SEQ, 128), F32),
            pltpu.VMEM((LC, 128), F32), pltpu.VMEM((LC, 128), F32),
            pltpu.VMEM((3, SEQ, 128), BF16), pltpu.VMEM((3, LC, 128), BF16),
        ] + hosted.scratch,
        compiler_params=_cp(("arbitrary", "arbitrary", "arbitrary")),
    )(pna, pnac, bias, dmixin, dproj, dprojc, *hosted.args)


def tail_fwd_bwd(x, mixin, tgt, mod3, g_post_mix, g_pre_mlp, g_post_mlp, wout, w1, w2):
    nb = x.shape[0]

    def body(x_ref, mi_ref, tgt_ref, mod_ref, gpm_ref, gpl_ref, gpo_ref, wo_ref, w1_ref, w2_ref,
             dx_ref, dmix_ref, h2_ref, du_ref, a_ref, dm_ref, dmi_ref, dmod_ref, dg_ref, loss_ref):
        b, t = pl.program_id(0), pl.program_id(1)
        gt1, sh2, sc2, gt2 = mod_ref[2:3, :], mod_ref[3:4, :], mod_ref[4:5, :], mod_ref[5:6, :]
        mix = jnp.dot(mi_ref[...], wo_ref[...], preferred_element_type=F32)
        (x1, h2), vjp_a = jax.vjp(_post_mix, x_ref[...], mix, gt1, sc2, sh2, gpm_ref[...], gpl_ref[...])
        h2b = h2.astype(BF16)
        h2_ref[...] = h2b
        m = jnp.zeros((TN, D), F32)
        relus = []
        for j in range(4):
            cols = slice(j * D, (j + 1) * D)
            r = jnp.maximum(jnp.dot(h2b, w1_ref[j], preferred_element_type=F32), 0.0)
            ab = (r * r).astype(BF16)
            a_ref[:, cols] = ab
            m = m + jnp.dot(ab, w2_ref[cols, :], preferred_element_type=F32)
            relus.append(r)
        loss, vjp_b = jax.vjp(_head_loss, x1, m, gt2, gpo_ref[...], tgt_ref[...])
        dx1, dm, dgt2, dgpo, _ = vjp_b(jnp.ones((1, 1), F32))
        dmb = dm.astype(BF16)
        dm_ref[...] = dmb
        dh2 = jnp.zeros((TN, D), F32)
        for j in range(4):
            cols = slice(j * D, (j + 1) * D)
            da = lax.dot_general(dmb, w2_ref[cols, :], (((1,), (1,)), ((), ())), preferred_element_type=F32)
            dub = (da * (2.0 * relus[j])).astype(BF16)
            du_ref[:, cols] = dub
            dh2 = dh2 + lax.dot_general(dub, w1_ref[j], (((1,), (1,)), ((), ())), preferred_element_type=F32)
        dx, dmix, dgt1, dsc2, dsh2, dgpm, dgpl = vjp_a((dx1, dh2))
        dx_ref[...] = dx
        dmixb = dmix.astype(BF16)
        dmix_ref[...] = dmixb
        dmi_ref[...] = lax.dot_general(dmixb, wo_ref[...], (((1,), (1,)), ((), ())),
                                       preferred_element_type=F32).astype(BF16)

        @pl.when(t == 0)
        def _():
            dmod_ref[...] = jnp.zeros_like(dmod_ref)

        @pl.when((t == 0) & (b == 0))
        def _():
            dg_ref[...] = jnp.zeros_like(dg_ref)
            loss_ref[...] = jnp.zeros_like(loss_ref)

        dmod_ref[2:3, :] += dgt1
        dmod_ref[3:4, :] += dsh2
        dmod_ref[4:5, :] += dsc2
        dmod_ref[5:6, :] += dgt2
        dg_ref[0:1, :] += dgpm
        dg_ref[1:2, :] += dgpl
        dg_ref[2:3, :] += dgpo
        loss_ref[...] += jnp.broadcast_to(loss, loss_ref.shape)

    tok = lambda b, t: (b, t, 0)
    const = lambda b, t: (0, 0)
    vec = pl.BlockSpec((1, D), const)
    return pl.pallas_call(
        body, name="tail_fwd_bwd", grid=(nb, SEQ // TN),
        in_specs=[
            pl.BlockSpec((None, TN, D), tok), pl.BlockSpec((None, TN, D), tok), pl.BlockSpec((None, TN, D), tok),
            pl.BlockSpec((None, 6, D), lambda b, t: (b, 0, 0)), vec, vec, vec,
            pl.BlockSpec((D, D), const, pipeline_mode=pl.Buffered(1)),
            pl.BlockSpec((4, D, D), lambda b, t: (0, 0, 0), pipeline_mode=pl.Buffered(1)),
            pl.BlockSpec((DFF, D), const, pipeline_mode=pl.Buffered(1)),
        ],
        out_specs=[
            pl.BlockSpec((None, TN, D), tok), pl.BlockSpec((None, TN, D), tok), pl.BlockSpec((None, TN, D), tok),
            pl.BlockSpec((None, TN, DFF), tok), pl.BlockSpec((None, TN, DFF), tok), pl.BlockSpec((None, TN, D), tok),
            pl.BlockSpec((None, TN, D), tok),
            pl.BlockSpec((None, 6, D), lambda b, t: (b, 0, 0)),
            pl.BlockSpec((8, D), const), pl.BlockSpec((8, 128), const),
        ],
        out_shape=[
            jax.ShapeDtypeStruct((nb, SEQ, D), F32), jax.ShapeDtypeStruct((nb, SEQ, D), BF16),
            jax.ShapeDtypeStruct((nb, SEQ, D), BF16), jax.ShapeDtypeStruct((nb, SEQ, DFF), BF16),
            jax.ShapeDtypeStruct((nb, SEQ, DFF), BF16), jax.ShapeDtypeStruct((nb, SEQ, D), BF16),
            jax.ShapeDtypeStruct((nb, SEQ, D), BF16),
            jax.ShapeDtypeStruct((nb, 6, D), F32), jax.ShapeDtypeStruct((8, D), F32),
            jax.ShapeDtypeStruct((8, 128), F32),
        ],
        compiler_params=_cp(("arbitrary", "arbitrary")),
    )(x, mixin, tgt, mod3, g_post_mix, g_pre_mlp, g_post_mlp, wout, w1, w2)


def weight_grad(pairs, name, out_dtype=F32, col_blocks=False, tm=1024, tn=1024, tk=2048):
    m, n = pairs[0][0].shape[1], pairs[0][1].shape[1]
    tn = min(tn, n)
    tks = [min(tk, xa.shape[0]) for xa, _ in pairs]
    steps = [xa.shape[0] // t for (xa, _), t in zip(pairs, tks)]
    total = sum(steps)
    offs = [sum(steps[:i]) for i in range(len(pairs))]

    def body(*refs):
        out_ref, acc = refs[2 * len(pairs)], refs[-1]
        k = pl.program_id(2)

        @pl.when(k == 0)
        def _():
            acc[...] = jnp.zeros_like(acc)

        for i in range(len(pairs)):
            @pl.when((k >= offs[i]) & (k < offs[i] + steps[i]))
            def _(i=i):
                acc[...] += lax.dot_general(refs[2 * i][...], refs[2 * i + 1][...], (((0,), (0,)), ((), ())),
                                            preferred_element_type=F32)

        if out_dtype != F32:
            @pl.when(k == total - 1)
            def _():
                out_ref[...] = acc[...].astype(out_dtype)

    in_specs, args = [], []
    for i, (xa, ya) in enumerate(pairs):
        clamp = lambda k, i=i: jnp.clip(k - offs[i], 0, steps[i] - 1)
        in_specs.append(pl.BlockSpec((tks[i], tm), lambda a, c, k, clamp=clamp: (clamp(k), a)))
        in_specs.append(pl.BlockSpec((tks[i], tn), lambda a, c, k, clamp=clamp: (clamp(k), c)))
        args += [xa, ya]
    if col_blocks:
        out_spec = pl.BlockSpec((None, tm, tn), lambda a, c, k: (c, a, 0))
        out_shape = jax.ShapeDtypeStruct((n // tn, m, tn), out_dtype)
    else:
        out_spec = pl.BlockSpec((tm, tn), lambda a, c, k: (a, c))
        out_shape = jax.ShapeDtypeStruct((m, n), out_dtype)
    return pl.pallas_call(
        body, name=name, grid=(m // tm, n // tn, total), in_specs=in_specs, out_specs=out_spec, out_shape=out_shape,
        scratch_shapes=[] if out_dtype == F32 else [pltpu.VMEM((tm, tn), F32)],
        compiler_params=_cp(("arbitrary", "arbitrary", "arbitrary")),
    )(*args)


def _perm_block(t):
    return 4 * (t % 4) + t // 4 if t < 16 else 16 + 3 * ((t - 16) % 4) + (t - 16) // 4


def _is_rope_block(p):
    return p < 16 and p % 4 < 2


def unpack_w_in(blocks):
    def body(i_ref, o_ref):
        for t in range(28):
            p = _perm_block(t)
            blk = i_ref[t // 7, :, (t % 7) * 128:(t % 7 + 1) * 128]
            if _is_rope_block(p):
                blk = _pair_order(blk.astype(F32)).astype(BF16)
            o_ref[:, p * 128:(p + 1) * 128] = blk

    return pl.pallas_call(
        body, name="unpack_w_in", grid=(2,),
        in_specs=[pl.BlockSpec((4, D // 2, 896), lambda i: (0, i, 0))],
        out_specs=pl.BlockSpec((D // 2, IN_W), lambda i: (i, 0)),
        out_shape=jax.ShapeDtypeStruct((D, IN_W), BF16),
    )(blocks)


def pack_w_in(dw):
    def body(i_ref, o_ref):
        for t in range(28):
            p = _perm_block(t)
            blk = i_ref[:, p * 128:(p + 1) * 128]
            if _is_rope_block(p):
                blk = _pair_order(blk)
            o_ref[t // 7, :, (t % 7) * 128:(t % 7 + 1) * 128] = blk.astype(BF16)

    return pl.pallas_call(
        body, name="pack_w_in", grid=(4,),
        in_specs=[pl.BlockSpec((D // 4, IN_W), lambda i: (i, 0))],
        out_specs=pl.BlockSpec((4, D // 4, 896), lambda i: (0, i, 0)),
        out_shape=jax.ShapeDtypeStruct((4, D, 896), BF16),
    )(dw)


def _place():
    return lax.axis_index("x"), lax.axis_index("y"), lax.axis_index("c")


class Hosted:
    def __init__(self, args, out_shape, scratch, start, finish):
        self.args, self.out_shape, self.scratch, self.start, self.finish = args, out_shape, scratch, start, finish

    def specs(self):
        hbm = pl.BlockSpec(memory_space=pl.ANY)
        return [hbm] * len(self.args), [hbm] * len(self.out_shape)

    def split(self, refs, n_in, n_out):
        a, b = len(self.args), len(self.out_shape)
        cuts = [n_in, n_in + a, n_in + a + n_out, n_in + a + n_out + b, len(refs) - len(self.scratch)]
        parts = [refs[i:j] for i, j in zip([0] + cuts, cuts + [len(refs)])]
        return parts[0], parts[1], parts[2], parts[3], parts[4], parts[5]


def no_exchange():
    return Hosted([], [], [], lambda *a: None, lambda *a: None)


def run_hosted(hosted, name):
    def body(*refs):
        _, ins, _, outs, _, sems = hosted.split(refs, 0, 0)
        hosted.start(ins, outs, sems)
        hosted.finish(ins, outs, sems)

    in_specs, out_specs = hosted.specs()
    return pl.pallas_call(body, name=name, in_specs=in_specs, out_specs=out_specs, out_shape=hosted.out_shape,
                          scratch_shapes=hosted.scratch)(*hosted.args)


def gather8(blocks):
    na = len(blocks)

    def copies(ins, outs, sems):
        send_sems, recv_sems, local_sem = sems
        x, y, c = _place()
        me, sibling = (x, y, c), (x, y, 1 - c)
        chips = [(1 - x, y), (x, 1 - y), (1 - x, 1 - y)]

        def slot(o_ref, px, py, pc):
            return o_ref.at[4 * px + 2 * py + pc]

        def copy(a, k, block, to, src=None):
            return pltpu.make_async_remote_copy(
                src_ref=slot(outs[a], *block) if src is None else src, dst_ref=slot(outs[a], *block),
                send_sem=send_sems.at[a, k], recv_sem=recv_sems.at[a, k], device_id=to, device_id_type=MESH)

        mine = [pltpu.make_async_copy(ins[a], slot(outs[a], *me), local_sem.at[a]) for a in range(na)]
        first = []
        for a in range(na):
            first.append(copy(a, 0, me, sibling, src=ins[a]))
            first += [copy(a, 1 + j, me, (*chip, c), src=ins[a]) for j, chip in enumerate(chips)]
        return copy, mine, first, me, sibling, chips, c

    def start(ins, outs, sems):
        _, mine, first, *_ = copies(ins, outs, sems)
        for cp in mine + first:
            cp.start()

    def finish(ins, outs, sems):
        copy, mine, first, me, sibling, chips, c = copies(ins, outs, sems)
        passed = []
        for j, chip in enumerate(chips):
            for a in range(na):
                copy(a, 1 + j, (*chip, c), me).wait_recv()
                cp = copy(a, 4 + j, (*chip, c), sibling)
                cp.start()
                passed.append(cp)
        for a in range(na):
            copy(a, 0, sibling, me).wait_recv()
            for j, chip in enumerate(chips):
                copy(a, 4 + j, (*chip, 1 - c), me).wait_recv()
        for cp in first + passed:
            cp.wait_send()
        for cp in mine:
            cp.wait()

    return Hosted(list(blocks), [jax.ShapeDtypeStruct((8,) + b.shape, b.dtype) for b in blocks],
                  [pltpu.SemaphoreType.DMA((na, 7)), pltpu.SemaphoreType.DMA((na, 7)), pltpu.SemaphoreType.DMA((na,))],
                  start, finish)


def chips3(arrays):
    na = len(arrays)

    def copies(ins, outs, sems):
        send_sems, recv_sems = sems
        x, y, c = _place()
        return [pltpu.make_async_remote_copy(
            src_ref=ins[a].at[2 * px + py], dst_ref=outs[a].at[k], send_sem=send_sems.at[a, k],
            recv_sem=recv_sems.at[a, k], device_id=(px, py, c), device_id_type=MESH)
            for a in range(na) for k, (px, py) in enumerate([(1 - x, y), (x, 1 - y), (1 - x, 1 - y)])]

    def start(ins, outs, sems):
        for cp in copies(ins, outs, sems):
            cp.start()

    def finish(ins, outs, sems):
        for cp in copies(ins, outs, sems):
            cp.wait()

    return Hosted(list(arrays), [jax.ShapeDtypeStruct((3,) + a.shape[1:], a.dtype) for a in arrays],
                  [pltpu.SemaphoreType.DMA((na, 3)), pltpu.SemaphoreType.DMA((na, 3))], start, finish)


def siblings(arrays):
    na = len(arrays)

    def copies(ins, outs, sems):
        send_sems, recv_sems = sems
        x, y, c = _place()
        return [pltpu.make_async_remote_copy(
            src_ref=ins[a], dst_ref=outs[a], send_sem=send_sems.at[a], recv_sem=recv_sems.at[a],
            device_id=(x, y, 1 - c), device_id_type=MESH) for a in range(na)]

    def start(ins, outs, sems):
        for cp in copies(ins, outs, sems):
            cp.start()

    def finish(ins, outs, sems):
        for cp in copies(ins, outs, sems):
            cp.wait()

    return Hosted(list(arrays), [jax.ShapeDtypeStruct(a.shape, a.dtype) for a in arrays],
                  [pltpu.SemaphoreType.DMA((na,)), pltpu.SemaphoreType.DMA((na,))], start, finish)


def both(first, second):
    na, no, ns = len(first.args), len(first.out_shape), len(first.scratch)

    def start(ins, outs, sems):
        first.start(ins[:na], outs[:no], sems[:ns])
        second.start(ins[na:], outs[no:], sems[ns:])

    def finish(ins, outs, sems):
        first.finish(ins[:na], outs[:no], sems[:ns])
        second.finish(ins[na:], outs[no:], sems[ns:])

    return Hosted(first.args + second.args, first.out_shape + second.out_shape, first.scratch + second.scratch,
                  start, finish)


def siblings4(arrays):
    na = len(arrays)

    def copies(ins, outs, sems):
        send_sems, recv_sems = sems
        x, y, c = _place()
        return [pltpu.make_async_remote_copy(
            src_ref=ins[a].at[2 * j + 1 - c], dst_ref=outs[a].at[j],
            send_sem=send_sems.at[a, j], recv_sem=recv_sems.at[a, j],
            device_id=(x, y, 1 - c), device_id_type=MESH) for a in range(na) for j in range(4)]

    def start(ins, outs, sems):
        for cp in copies(ins, outs, sems):
            cp.start()

    def finish(ins, outs, sems):
        for cp in copies(ins, outs, sems):
            cp.wait()

    return Hosted(list(arrays), [jax.ShapeDtypeStruct((4,) + a.shape[1:], a.dtype) for a in arrays],
                  [pltpu.SemaphoreType.DMA((na, 4)), pltpu.SemaphoreType.DMA((na, 4))], start, finish)


def _row_tile(r):
    for cand in (512, 256, 128, 64, 32, 16, 8):
        if r % cand == 0:
            return cand
    return r


def chip_partial(place, g8s, landed4s, name):
    n = len(g8s)

    def body(place_ref, *refs):
        del place_ref
        for g_ref, l_ref, o_ref in zip(refs[:n], refs[n:2 * n], refs[2 * n:]):
            o_ref[...] = (g_ref[...].astype(F32) + l_ref[...].astype(F32)).astype(BF16)

    own = [pl.BlockSpec((None,) + g.shape[1:], lambda j, s: (2 * j + s[0], 0, 0)) for g in g8s]
    plain = [pl.BlockSpec((None,) + g.shape[1:], lambda j, s: (j, 0, 0)) for g in g8s]
    return pl.pallas_call(
        body, name=name,
        grid_spec=pltpu.PrefetchScalarGridSpec(num_scalar_prefetch=1, grid=(4,), in_specs=own + plain, out_specs=plain),
        out_shape=[jax.ShapeDtypeStruct((4,) + g.shape[1:], BF16) for g in g8s],
    )(place, *g8s, *landed4s)


def shard_sum(place, partial4s, landed3s, name):
    n = len(partial4s)

    def body(place_ref, *refs):
        del place_ref
        for p_ref, l_ref, o_ref in zip(refs[:n], refs[n:2 * n], refs[2 * n:]):
            acc = p_ref[...].astype(F32)
            for k in range(3):
                acc = acc + l_ref[k].astype(F32)
            o_ref[...] = acc

    def halves(p, lead):
        r, ccols = p.shape[1:]
        return (lead, r // 2, ccols)

    return pl.pallas_call(
        body, name=name,
        grid_spec=pltpu.PrefetchScalarGridSpec(
            num_scalar_prefetch=1, grid=(2,),
            in_specs=[pl.BlockSpec(halves(p, None), lambda i, s: (s[1], i, 0)) for p in partial4s]
            + [pl.BlockSpec(halves(p, 3), lambda i, s: (0, i, 0)) for p in partial4s],
            out_specs=[pl.BlockSpec(halves(p, None)[1:], lambda i, s: (i, 0)) for p in partial4s]),
        out_shape=[jax.ShapeDtypeStruct(p.shape[1:], F32) for p in partial4s],
    )(place, *partial4s, *landed3s)


def _adamw_math(w, g, m, v):
    m2 = B1 * m + (1.0 - B1) * g
    v2 = B2 * v + (1.0 - B2) * (g * g)
    m_hat = m2 / (1.0 - B1 ** STEP)
    v_hat = v2 / (1.0 - B2 ** STEP)
    return -LR * (m_hat / (jnp.sqrt(v_hat) + AEPS) + WD * w), m2, v2


def adamw_halves(place, w, mine, theirs, m, v, name):
    r, ccols = w.shape
    hr = r // 2
    tr = _row_tile(hr)
    nt = hr // tr

    def body(place_ref, w_ref, a_ref, b_ref, m_ref, v_ref, g_out, d_out, m_out, v_out):
        g = jnp.where(pl.program_id(0) == place_ref[0], a_ref[...], b_ref[...])
        d, m2, v2 = _adamw_math(w_ref[...], g, m_ref[...], v_ref[...])
        g_out[...] = g
        d_out[...] = d
        m_out[...] = m2
        v_out[...] = v2

    full = pl.BlockSpec((tr, ccols), lambda h, i, s: (h * nt + i, 0))
    part = pl.BlockSpec((tr, ccols), lambda h, i, s: (i, 0))
    return pl.pallas_call(
        body, name=name,
        grid_spec=pltpu.PrefetchScalarGridSpec(
            num_scalar_prefetch=1, grid=(2, nt), in_specs=[full, part, part, full, full], out_specs=[full] * 4),
        out_shape=[jax.ShapeDtypeStruct((r, ccols), F32)] * 4,
    )(place, w, mine, theirs, m, v)


def adamw_group(place, halved, plain, hosted, name):
    rows = halved[0][0].shape[0]
    tr = 128
    nt = rows // 2 // tr
    nh, npl = len(halved), len(plain)

    def body(place_ref, *refs):
        own_in, h_in, own_out, h_out, _, h_sems = hosted.split(refs, 5 * nh + 4 * npl, 4 * nh + 3 * npl)
        half = pl.program_id(0)
        grid_step = half * nt + pl.program_id(1)

        @pl.when(grid_step == 0)
        def _():
            hosted.start(h_in, h_out, h_sems)

        for i in range(nh):
            w_ref, a_ref, b_ref, m_ref, v_ref = own_in[5 * i:5 * i + 5]
            g = jnp.where(half == place_ref[0], a_ref[...], b_ref[...])
            res = (g,) + _adamw_math(w_ref[...], g, m_ref[...], v_ref[...])
            for o_ref, r in zip(own_out[4 * i:4 * i + 4], res):
                o_ref[...] = r
        for i in range(npl):
            w_ref, g_ref, m_ref, v_ref = own_in[5 * nh + 4 * i:5 * nh + 4 * i + 4]
            res = _adamw_math(w_ref[...], g_ref[...], m_ref[...], v_ref[...])
            for o_ref, r in zip(own_out[4 * nh + 3 * i:4 * nh + 3 * i + 3], res):
                o_ref[...] = r

        @pl.when(grid_step == 2 * nt - 1)
        def _():
            hosted.finish(h_in, h_out, h_sems)

    def full(cols):
        return pl.BlockSpec((tr, cols), lambda h, i, s: (h * nt + i, 0))

    def part(cols):
        return pl.BlockSpec((tr, cols), lambda h, i, s: (i, 0))

    in_specs, out_specs, out_shape, args = [], [], [], []
    for w, a, b, m, v in halved:
        cols = w.shape[1]
        in_specs += [full(cols), part(cols), part(cols), full(cols), full(cols)]
        out_specs += [full(cols)] * 4
        out_shape += [jax.ShapeDtypeStruct(w.shape, F32)] * 4
        args += [w, a, b, m, v]
    for w, g, m, v in plain:
        cols = w.shape[1]
        in_specs += [full(cols)] * 4
        out_specs += [full(cols)] * 3
        out_shape += [jax.ShapeDtypeStruct(w.shape, F32)] * 3
        args += [w, g, m, v]
    h_in_specs, h_out_specs = hosted.specs()
    return pl.pallas_call(
        body, name=name,
        grid_spec=pltpu.PrefetchScalarGridSpec(
            num_scalar_prefetch=1, grid=(2, nt), in_specs=in_specs + h_in_specs, out_specs=out_specs + h_out_specs,
            scratch_shapes=hosted.scratch),
        out_shape=out_shape + hosted.out_shape,
        compiler_params=_cp(("arbitrary", "arbitrary")),
    )(place, *args, *hosted.args)


def _silu(x):
    return x * jax.nn.sigmoid(x)


def prologue(c_rows, c_ctx_row, w_ada, b_shard, rpb_flat, half_w_in, late_shards):
    shape = jax.ShapeDtypeStruct
    n_late = len(late_shards)
    half_shapes = [(w.shape[0] // 2, w.shape[1]) for w in late_shards]
    g_w = gather8([half_w_in])
    g_c = gather8([shape((8, D), F32)])
    g_m = gather8([shape((32, 1536), F32)])

    def body(*refs):
        c_ref, cc_ref, w_ref, b_ref, flat_ref, hw_ref = refs[:6]
        late_refs = refs[6:6 + n_late]
        cin_ref, mg_ref, gw_ref, bias_ref, cos_ref, sin_ref = refs[6 + n_late:12 + n_late]
        rest = refs[12 + n_late:]
        half_refs, (cg_s, ms_s, bias_s) = rest[:n_late], rest[n_late:n_late + 3]
        stage, (load_sem, bias_sem), sems = rest[n_late + 3:2 * n_late + 3], rest[2 * n_late + 3:2 * n_late + 5], \
            rest[2 * n_late + 5:]
        sw, sc, sm = sems[0:3], sems[3:6], sems[6:9]
        core = lax.axis_index("c")
        g_c.start([c_ref], [cg_s], sc)
        g_w.start([hw_ref], [gw_ref], sw)
        loads = [pltpu.make_async_copy(late_refs[a].at[pl.ds(core * half_shapes[a][0], half_shapes[a][0]), :],
                                       stage[a], load_sem.at[a]) for a in range(n_late)]
        for cp in loads:
            cp.start()
        g_c.finish([c_ref], [cg_s], sc)
        cin_ref[...] = jnp.zeros_like(cin_ref)
        for dev in range(8):
            cin_ref[2 * dev:2 * dev + 2, :] = cg_s[dev, 0:2, :]
        cin_ref[16:17, :] = cc_ref[...]
        ms_s[...] = _nn(_silu(cin_ref[...]), w_ref[...]) + b_ref[...]
        g_m.start([ms_s], [mg_ref], sm)
        for a, cp in enumerate(loads):
            cp.wait()
            half_refs[a][...] = stage[a][...].astype(BF16)
        cos_ref[...], sin_ref[...] = _rope_tables()
        stores = []
        for pair in range(NPAIR):
            if pair >= 2:
                stores[pair - 2].wait()
            _na_bias_pair(flat_ref.at[pair], bias_s.at[pair % 2])
            stores.append(pltpu.make_async_copy(bias_s.at[pair % 2], bias_ref.at[pair], bias_sem.at[pair % 2]))
            stores[pair].start()
        for cp in stores[-2:]:
            cp.wait()
        g_m.finish([ms_s], [mg_ref], sm)
        g_w.finish([hw_ref], [gw_ref], sw)

    vmem = pl.BlockSpec(memory_space=pltpu.VMEM)
    hbm = pl.BlockSpec(memory_space=pl.ANY)
    return pl.pallas_call(
        body, name="prologue", in_specs=[vmem, vmem, vmem, vmem, vmem, hbm] + [hbm] * n_late,
        out_specs=[vmem, vmem, hbm, hbm, vmem, vmem] + [vmem] * n_late,
        out_shape=[shape((32, D), F32), shape((8, 32, 1536), F32)] + g_w.out_shape
        + [shape((NPAIR,) + NA_BIAS_SHAPE, F32)] + [shape((SEQ, RD), F32)] * 2 + [shape(s, BF16) for s in half_shapes],
        scratch_shapes=[pltpu.VMEM((8, 8, D), F32), pltpu.VMEM((32, 1536), F32), pltpu.VMEM((2,) + NA_BIAS_SHAPE, F32)]
        + [pltpu.VMEM(s, F32) for s in half_shapes]
        + [pltpu.SemaphoreType.DMA((n_late,)), pltpu.SemaphoreType.DMA((2,))]
        + g_w.scratch + g_c.scratch + g_m.scratch,
        compiler_params=_cp(),
    )(c_rows, c_ctx_row, w_ada, b_shard, rpb_flat, half_w_in, *late_shards)


def ada_grads(cin, gb, gc, w_ada):
    def body(c_ref, gb_ref, gc_ref, w_ref, gw_ref, pc_ref):
        ctx_tot = jnp.sum(gc_ref[...], axis=0, keepdims=True)
        rows = lax.broadcasted_iota(jnp.int32, (16, 512), 0)
        dm = jnp.concatenate([gb_ref[...], jnp.where(rows == 0, ctx_tot, 0.0)], axis=0)
        gw_ref[...] = _tn(_silu(c_ref[...]), dm)
        rows8 = lax.broadcasted_iota(jnp.int32, (8, 512), 0)
        part = _nt(jnp.where(rows8 == 0, ctx_tot, 0.0), w_ref[...])

        @pl.when(pl.program_id(0) == 0)
        def _():
            pc_ref[...] = jnp.zeros_like(pc_ref)

        pc_ref[...] += part

    return pl.pallas_call(
        body, name="ada_grads", grid=(3,),
        in_specs=[pl.BlockSpec((32, D), lambda j: (0, 0)), pl.BlockSpec((16, 512), lambda j: (0, j)),
                  pl.BlockSpec((8, 512), lambda j: (0, j)), pl.BlockSpec((D, 512), lambda j: (0, j))],
        out_specs=[pl.BlockSpec((D, 512), lambda j: (0, j)), pl.BlockSpec((8, D), lambda j: (0, 0))],
        out_shape=[jax.ShapeDtypeStruct((D, 1536), F32), jax.ShapeDtypeStruct((8, D), F32)],
    )(cin, gb, gc, w_ada)


SMALL_SUM_ROWS = 15


def small_update(gsm, gbf, gcf, pcg, params):
    n = len(params)

    def body(*refs):
        gsm_ref, gbf_ref, gcf_ref, pcg_ref = refs[:4]
        wmv, outs, loss_out = refs[4:4 + 3 * n], refs[4 + 3 * n:4 + 7 * n], refs[-1]
        acc = gsm_ref[0]
        for dev in range(1, 8):
            acc = acc + gsm_ref[dev]
        c_ctx = wmv[0][...]
        sg = jax.nn.sigmoid(c_ctx)
        dsilu = pcg_ref[0:1, :] + pcg_ref[2:3, :] + pcg_ref[4:5, :] + pcg_ref[6:7, :]
        lane = lax.broadcasted_iota(jnp.int32, (1, D), 1)
        last = acc[14:15, :]
        grads = [
            dsilu * (sg * (1.0 + c_ctx * (1.0 - sg))),
            jnp.sum(gbf_ref[...], axis=0, keepdims=True) + jnp.sum(gcf_ref[...], axis=0, keepdims=True),
            acc[0:1, :] + acc[1:2, :], acc[2:3, :], acc[3:4, :], acc[4:5, :],
            acc[5:6, 0:512], acc[6:14, :], jnp.where(lane < 8, last, 0.0),
        ]
        loss_out[...] = jnp.broadcast_to(jnp.sum(jnp.where(lane == 8, last, 0.0), axis=1, keepdims=True), (8, 128))
        for i, g in enumerate(grads):
            d, m2, v2 = _adamw_math(wmv[3 * i][...], g, wmv[3 * i + 1][...], wmv[3 * i + 2][...])
            outs[4 * i][...] = g
            outs[4 * i + 1][...] = d
            outs[4 * i + 2][...] = m2
            outs[4 * i + 3][...] = v2

    flat = [a for wmv in params for a in wmv]
    out_shape = [jax.ShapeDtypeStruct(w.shape, F32) for w, _, _ in params for _ in range(4)]
    return pl.pallas_call(
        body, name="small_update", out_shape=out_shape + [jax.ShapeDtypeStruct((8, 128), F32)],
    )(gsm, gbf, gcf, pcg, *flat)


def _pad_row(v, rows):
    flat = v.reshape(-1)
    return jnp.pad(flat, (0, rows * D - flat.shape[0])).reshape(rows, D)


def local_step(x, ctx, tgt, mod3, rope, bias, g_pre_mix, g_post_mix, g_pre_mlp, g_post_mlp, ret_decay, ret_gn,
               wperm, late_weights, early_grads):
    nb = x.shape[0]
    tokens = nb * SEQ
    cos, sin = rope
    rd = ret_decay.T.reshape(RH, 2, 1)
    gn = ret_gn.reshape(RH, 1, RD)
    h, pret, pna = premix_proj(x, mod3, g_pre_mix, wperm, False, "premix_proj")
    hc, pretc, pnac = premix_proj(ctx, mod3, g_pre_mix, wperm, True, "premix_proj_ctx")
    o_all, mixin, gw_out = retention_fwd(pret, pretc, rd, gn, cos, sin, late_weights(0))
    mixin, gw1, gw2 = na_fwd(pna, pnac, bias, mixin, late_weights(1))
    dx_tail, dmix, h2, du, act, dm, dmixin, dmod_t, dg_t, loss_t = tail_fwd_bwd(
        x, mixin, tgt, mod3, g_post_mix, g_pre_mlp, g_post_mlp, gw_out.reshape(D, D), gw1.reshape(4, D, D),
        gw2.reshape(DFF, D))
    dw_out = weight_grad([(mixin.reshape(tokens, D), dmix.reshape(tokens, D))], "grad_w_out", BF16)
    dw1 = weight_grad([(h2.reshape(tokens, D), du.reshape(tokens, DFF))], "grad_w_mlp1", BF16, col_blocks=True)
    dw2 = weight_grad([(act.reshape(tokens, DFF), dm.reshape(tokens, D))], "grad_w_mlp2", BF16)
    dproj, dprojc, drd, dgn, *landed = retention_bwd(pret, pretc, o_all, dmixin, rd, gn, cos, sin,
                                                     early_grads[0](dw_out, dw1, dw2))
    dproj, dprojc, dpat, *early = na_bwd(pna, pnac, bias, dmixin, dproj, dprojc, early_grads[1](landed))
    dw_in = weight_grad([(h.reshape(tokens, D), dproj.reshape(tokens, IN_W)),
                         (hc.reshape(nb * LC, D), dprojc.reshape(nb * LC, IN_W))], "grad_w_in", tn=IN_W // 2, tk=1024)
    dmod_c, dg_c, *late = premix_bwd(ctx, mod3, g_pre_mix, wperm, dprojc, None, early_grads[2](dw_in), "premix_bwd_ctx")
    grad_x, dmod_a, dg_a, *late = premix_bwd(x, mod3, g_pre_mix, wperm, dproj, dx_tail, early_grads[3](late),
                                             "premix_bwd")
    dmod = jnp.concatenate([jnp.concatenate([dmod_a[:, 0:2], dmod_t[:, 2:6]], axis=1), dmod_c], axis=0)
    last = jnp.pad(jnp.concatenate([drd[:, :, 0].T.reshape(8), loss_t[0, 0:1]]), (0, D - 9)).reshape(1, D)
    small = jnp.concatenate([dg_a[0:1], dg_c[0:1], dg_t[0:3], _pad_row(dgn, 1), dpat.reshape(8, D), last], axis=0)
    return grad_x, late, early, dmod, small


def kernel(x, c, ctx, c_ctx, w_ada, b_ada, g_pre_mix, g_post_mix, g_pre_mlp, g_post_mlp, w_in, ret_decay, ret_gn, na_rpb, w_out, w_mlp1, w_mlp2, loss_target, m_c_ctx, m_w_ada, m_b_ada, m_g_pre_mix, m_g_post_mix, m_g_pre_mlp, m_g_post_mlp, m_w_in, m_ret_decay, m_ret_gn, m_na_rpb, m_w_out, m_w_mlp1, m_w_mlp2, v_c_ctx, v_w_ada, v_b_ada, v_g_pre_mix, v_g_post_mix, v_g_pre_mlp, v_g_post_mlp, v_w_in, v_ret_decay, v_ret_gn, v_na_rpb, v_w_out, v_w_mlp1, v_w_mlp2):
    px, py, pc = _place()
    dev = 4 * px + 2 * py + pc
    chip = 2 * px + py

    half_w_in = lax.dynamic_slice_in_dim(w_in[0], pc * (D // 2), D // 2, 0).astype(BF16)
    cin, mg, gw_in, bias, cos, sin, *late_halves = prologue(
        jnp.pad(c, ((0, 6), (0, 0))), c_ctx[None], w_ada[0], lax.dynamic_slice_in_dim(b_ada, chip * 1536, 1536, 1),
        _rpb_flat(na_rpb[0]), half_w_in, [w_out[0], w_mlp1[0], w_mlp2[0]])
    halves = [half_w_in] + late_halves
    wperm = unpack_w_in(gw_in.reshape(4, D, 896))
    mod_all = jnp.concatenate([mg[0], mg[2], mg[4], mg[6]], axis=1)
    mod3 = (jnp.pad(lax.dynamic_slice_in_dim(mod_all, 2 * dev, 2, 0), ((0, 1), (0, 0)))
            + jnp.pad(mod_all[16:17], ((2, 0), (0, 0)))).reshape(3, 6, D)

    place = jnp.stack([pc, chip]).astype(jnp.int32)

    early_names = ["w_out", "w_mlp1", "w_mlp2"]
    early_g8, early_partial = [], []

    def early_a(dw_out, dw1, dw2):
        early_g8[:] = [dw_out.reshape(8, 128, D), dw1.reshape(8, 512, D), dw2.reshape(8, 512, D)]
        return siblings4(early_g8)

    def early_b(landed):
        early_partial[:] = chip_partial(place, early_g8, landed, "rs_chip_sum_early")
        return chips3(early_partial)

    late_partial = []

    late_g8 = []

    def late_c(dw_in):
        late_g8[:] = [pack_w_in(dw_in).reshape(8, 512, 896)]
        return siblings4(late_g8)

    def late_d(landed):
        late_partial[:] = chip_partial(place, late_g8, landed, "rs_chip_sum_w_in")
        return chips3(late_partial)

    grad_x, (landed3_in,), early_landed, dmod, small = local_step(
        x, ctx, loss_target, mod3, (cos, sin), bias, g_pre_mix, g_post_mix, g_pre_mlp, g_post_mlp, ret_decay[0], ret_gn,
        wperm, lambda k: gather8(halves[1:2] if k == 0 else halves[2:4]), (early_a, early_b, late_c, late_d))
    early_mine = shard_sum(place, early_partial, early_landed, "rs_shard_sum_early")

    pay = jnp.concatenate([dmod.reshape(18, D), small, jnp.zeros((40 - 18 - SMALL_SUM_ROWS, D), F32)], axis=0)
    *early_theirs, gs = run_hosted(both(siblings(early_mine), gather8([pay])), "rs_halves_early_gather_small")
    gbf = gs[:, 0:12].reshape(16, 6 * D)
    gcf = gs[:, 12:18].reshape(8, 6 * D)
    gw_ada, pc_part = ada_grads(cin, lax.dynamic_slice_in_dim(gbf, chip * 1536, 1536, 1),
                                lax.dynamic_slice_in_dim(gcf, chip * 1536, 1536, 1), w_ada[0])
    (mine_in,) = shard_sum(place, late_partial, [landed3_in], "rs_shard_sum_w_in")
    theirs_in, pcg = run_hosted(both(siblings([mine_in]), gather8([pc_part])), "rs_halves_w_in_gather_c_ctx")

    grouped = adamw_group(
        place,
        [(w_mlp1[0], early_mine[1], early_theirs[1], m_w_mlp1[0], v_w_mlp1[0]),
         (w_mlp2[0], early_mine[2], early_theirs[2], m_w_mlp2[0], v_w_mlp2[0])],
        [(w_ada[0], gw_ada, m_w_ada[0], v_w_ada[0])], no_exchange(), "adamw_group")
    d_ada, m_ada, v_ada = grouped[8:11]
    big = [
        [r[None] for r in adamw_halves(place, w_in[0], mine_in, theirs_in, m_w_in[0], v_w_in[0], "adamw_w_in")],
        [r[None] for r in adamw_halves(place, w_out[0], early_mine[0], early_theirs[0], m_w_out[0], v_w_out[0],
                                       "adamw_w_out")],
        [r[None] for r in grouped[0:4]], [r[None] for r in grouped[4:8]],
    ]

    def rpb_rows(t):
        return _rpb_flat(t[0]).reshape(8, D)

    def decay_row(t):
        return jnp.pad(t.reshape(1, 8), ((0, 0), (0, D - 8)))

    views = [lambda t: t.reshape(1, D), lambda t: t, lambda t: t, lambda t: t, lambda t: t, lambda t: t, lambda t: t,
             rpb_rows, decay_row]
    back = [lambda t: t.reshape(D), lambda t: t, lambda t: t, lambda t: t, lambda t: t, lambda t: t, lambda t: t,
            lambda t: _rpb_flat_t(t)[None], lambda t: t[:, 0:8].reshape(1, 2, 4)]
    small_w = (c_ctx, b_ada, g_pre_mix, g_post_mix, g_pre_mlp, g_post_mlp, ret_gn, na_rpb, ret_decay)
    small_m = (m_c_ctx, m_b_ada, m_g_pre_mix, m_g_post_mix, m_g_pre_mlp, m_g_post_mlp, m_ret_gn, m_na_rpb, m_ret_decay)
    small_v = (v_c_ctx, v_b_ada, v_g_pre_mix, v_g_post_mix, v_g_pre_mlp, v_g_post_mlp, v_ret_gn, v_na_rpb, v_ret_decay)
    *res, loss8 = small_update(gs[:, 18:18 + SMALL_SUM_ROWS], gbf, gcf, pcg[:, 0],
                               [(f(w), f(m), f(v)) for f, w, m, v in zip(views, small_w, small_m, small_v)])

    def leaves(ada, idx):
        s_c, s_b, s_g1, s_g2, s_g3, s_g4, s_gn, s_rpb, s_rd = [back[i](res[4 * i + idx]) for i in range(9)]
        return [s_c, ada[None], s_b, s_g1, s_g2, s_g3, s_g4, big[0][idx], s_rd, s_gn, s_rpb,
                big[1][idx], big[2][idx], big[3][idx]]

    return (loss8[0, 0], grad_x, *leaves(gw_ada, 0), *leaves(d_ada, 1), *leaves(m_ada, 2), *leaves(v_ada, 3))
```

```python
import functools
import math

import jax
import jax.numpy as jnp
from jax import lax
from jax.experimental import pallas as pl
from jax.experimental.pallas import tpu as pltpu

F32, BF16 = jnp.float32, jnp.bfloat16
D = 1024
SEQ = 2048
LC = 256
GW = 64
RH, RD, CH = 4, 128, 128
NPAIR = 4
IN_W = 3584
RET_W = 2048
DFF = 4096
EPS = 1e-6
NEG = -1e30
TN = 256
NCH = SEQ // CH
LR, B1, B2, AEPS, WD, STEP = 0.001, 0.9, 0.999, 1e-08, 0.01, 10
MESH = pl.DeviceIdType.MESH
VMEM_LIMIT = 56 * 1024 * 1024


def _cp(sem=None):
    return pltpu.CompilerParams(dimension_semantics=sem, vmem_limit_bytes=VMEM_LIMIT)


def _nn(a, b):
    return jnp.dot(a.astype(BF16), b.astype(BF16), preferred_element_type=F32)


def _nt(a, b):
    return lax.dot_general(a.astype(BF16), b.astype(BF16), (((1,), (1,)), ((), ())), preferred_element_type=F32)


def _tn(a, b):
    return lax.dot_general(a.astype(BF16), b.astype(BF16), (((0,), (0,)), ((), ())), preferred_element_type=F32)


@jax.custom_vjp
def mm_tn(a, b):
    return _tn(a, b)


mm_tn.defvjp(lambda a, b: (_tn(a, b), (a, b)), lambda r, g: (_nt(r[1], g), _nn(r[0], g)))


def _rms(x):
    return x * lax.rsqrt(jnp.mean(x * x, axis=-1, keepdims=True) + EPS)


def _rms_mod(x, g, sc, sh):
    return (_rms(x) * g) * (1.0 + sc) + sh


def _post_mix(x, mix, gt1, sc2, sh2, g_post_mix, g_pre_mlp):
    x1 = x + gt1 * (_rms(mix) * g_post_mix)
    return x1, _rms_mod(x1, g_pre_mlp, sc2, sh2)


def _head_loss(x1, m, gt2, g_post_mlp, tgt):
    err = x1 + gt2 * (_rms(m) * g_post_mlp) - tgt
    return 0.5 * jnp.sum(jnp.mean(err * err, axis=-1, keepdims=True), axis=0, keepdims=True)


def _ln_gate(o, g, w):
    mu = jnp.mean(o, axis=-1, keepdims=True)
    var = jnp.mean(jnp.square(o - mu), axis=-1, keepdims=True)
    y = (o - mu) * lax.rsqrt(var + EPS)
    return (y * w) * (g * jax.nn.sigmoid(g))


def _pair_order(x):
    lane = lax.broadcasted_iota(jnp.int32, x.shape, 1)
    return jnp.where((lane >= 32) & (lane < 64), pltpu.roll(x, 96, 1),
                     jnp.where((lane >= 64) & (lane < 96), pltpu.roll(x, 32, 1), x))


def _rope(x, cos, sin):
    return x * cos + pltpu.roll(x, 64, 1) * sin


def _rope_t(g, cos, sin):
    return g * cos + pltpu.roll(g * sin, 64, 1)


def _rope_tables():
    tok = lax.broadcasted_iota(jnp.int32, (SEQ, RD), 0)
    lane = lax.broadcasted_iota(jnp.int32, (SEQ, RD), 1)
    pos = jnp.where((lane & 32) == 0, tok >> 6, tok & (GW - 1)).astype(F32)
    ang = pos * jnp.exp((lane & 31).astype(F32) * (-math.log(10000.0) / 32))
    return jnp.cos(ang), jnp.where(lane < 64, -jnp.sin(ang), jnp.sin(ang))


def _chunk_loop(n, body, init, k=4):
    def several(t, carry):
        for i in range(k):
            carry = body(k * t + i, carry)
        return carry

    return lax.fori_loop(0, n // k, several, init)


def _fiota(shape, dim):
    return lax.broadcasted_iota(jnp.int32, shape, dim).astype(F32)


def _ret_state(k, v, s, lg, reverse):
    pos = _fiota((CH, 1), 0)
    b_exp = pos if reverse else (CH - 1.0 - pos)
    return jnp.exp(lg * CH) * s + mm_tn(k * jnp.exp(lg * b_exp), v)


class _Decays:
    def __init__(self, lgs):
        i, j, pos = _fiota((CH, CH), 0), _fiota((CH, CH), 1), _fiota((CH, 1), 0)
        diffs = (i - j, j - i)
        keep = (diffs[0] >= 0, diffs[1] > 0)
        mats = [jnp.where(m, jnp.exp(lg * jnp.where(m, d, 0.0)), 0.0) for lg, d, m in zip(lgs, diffs, keep)]
        self.mask = mats[0] + mats[1]
        self.dmask = [mats[0] * diffs[0], mats[1] * diffs[1]]
        a_exp, b_exp = (pos + 1.0, CH - pos), (CH - 1.0 - pos, pos)
        self.a = [jnp.exp(lg * e) for lg, e in zip(lgs, a_exp)]
        self.b = [jnp.exp(lg * e) for lg, e in zip(lgs, b_exp)]
        self.da = [a * e for a, e in zip(self.a, a_exp)]
        self.db = [b * e for b, e in zip(self.b, b_exp)]
        self.g = [jnp.exp(lg * CH) for lg in lgs]


def _both(x, w):
    return jnp.concatenate([x * w[0], x * w[1]], axis=1)


def _total(x):
    return jnp.sum(jnp.sum(x, axis=1, keepdims=True), axis=0, keepdims=True)


def _state_pass(dec, init, k_s, v_of, st_s):
    def step(t, carry):
        out = []
        for d, s in enumerate(carry):
            n = (NCH - 1 - t) if d else t
            sl = pl.ds(pl.multiple_of(n * CH, CH), CH)
            st_s[n, d * RD:(d + 1) * RD, :] = s
            out.append(dec.g[d] * s + _tn(k_s[sl, :] * dec.b[d], v_of(sl)))
        return tuple(out)

    _chunk_loop(NCH, step, tuple(init))


def premix_proj(xin, mod3, g_pre, wperm, is_ctx, name):
    nb, length, _ = xin.shape
    tn = min(2 * TN, length)

    def body(x_ref, mod_ref, g_ref, w_ref, h_ref, pret_ref, pna_ref):
        h = _rms_mod(x_ref[...], g_ref[...], mod_ref[1:2, :], mod_ref[0:1, :])
        hb = h.astype(BF16)
        h_ref[...] = hb
        pret_ref[...] = jnp.dot(hb, w_ref[:, :RET_W], preferred_element_type=F32)
        pna_ref[...] = jnp.dot(hb, w_ref[:, RET_W:], preferred_element_type=F32).astype(BF16)

    return pl.pallas_call(
        body, name=name, grid=(nb, length // tn),
        in_specs=[
            pl.BlockSpec((None, tn, D), lambda b, t: (b, t, 0)),
            pl.BlockSpec((None, 6, D), (lambda b, t: (2, 0, 0)) if is_ctx else (lambda b, t: (b, 0, 0))),
            pl.BlockSpec((1, D), lambda b, t: (0, 0)),
            pl.BlockSpec((D, IN_W), lambda b, t: (0, 0), pipeline_mode=pl.Buffered(1)),
        ],
        out_specs=[
            pl.BlockSpec((None, tn, D), lambda b, t: (b, t, 0)),
            pl.BlockSpec((None, tn, RET_W), lambda b, t: (b, t, 0)),
            pl.BlockSpec((None, tn, IN_W - RET_W), lambda b, t: (b, t, 0)),
        ],
        out_shape=[
            jax.ShapeDtypeStruct((nb, length, D), BF16),
            jax.ShapeDtypeStruct((nb, length, RET_W), F32),
            jax.ShapeDtypeStruct((nb, length, IN_W - RET_W), BF16),
        ],
        compiler_params=_cp(("arbitrary", "arbitrary")),
    )(xin, mod3, g_pre, wperm)


def premix_bwd(xin, mod3, g_pre, wperm, dproj, dx_tail, hosted, name):
    nb, length, _ = xin.shape
    tn = min(TN, length)
    is_ctx = dx_tail is None

    def body(*refs):
        own_in, h_in, own_out, h_out, _, h_sems = hosted.split(refs, 5 if is_ctx else 6, 2 if is_ctx else 3)
        if is_ctx:
            (x_ref, mod_ref, g_ref, w_ref, dp_ref), (dmod_ref, dg_ref) = own_in, own_out
        else:
            (x_ref, mod_ref, g_ref, w_ref, dp_ref, dxt_ref), (dx_ref, dmod_ref, dg_ref) = own_in, own_out
        b, t = pl.program_id(0), pl.program_id(1)
        grid_step = b * (length // tn) + t

        @pl.when(grid_step == 0)
        def _():
            hosted.start(h_in, h_out, h_sems)

        @pl.when(grid_step == nb * (length // tn) - 1)
        def _():
            hosted.finish(h_in, h_out, h_sems)

        dh = lax.dot_general(dp_ref[...], w_ref[...], (((1,), (1,)), ((), ())), preferred_element_type=F32)
        _, vjp = jax.vjp(_rms_mod, x_ref[...], g_ref[...], mod_ref[1:2, :], mod_ref[0:1, :])
        dx, dg, dsc, dsh = vjp(dh)
        if not is_ctx:
            dx_ref[...] = dx + dxt_ref[...]

        @pl.when((t == 0) & ((b == 0) if is_ctx else True))
        def _():
            dmod_ref[...] = jnp.zeros_like(dmod_ref)

        @pl.when((t == 0) & (b == 0))
        def _():
            dg_ref[...] = jnp.zeros_like(dg_ref)

        dmod_ref[0:1, :] += dsh
        dmod_ref[1:2, :] += dsc
        dg_ref[0:1, :] += dg

    tok = lambda b, t: (b, t, 0)
    in_specs = [
        pl.BlockSpec((None, tn, D), tok),
        pl.BlockSpec((None, 6, D), (lambda b, t: (2, 0, 0)) if is_ctx else (lambda b, t: (b, 0, 0))),
        pl.BlockSpec((1, D), lambda b, t: (0, 0)),
        pl.BlockSpec((D, IN_W), lambda b, t: (0, 0), pipeline_mode=pl.Buffered(1)),
        pl.BlockSpec((None, tn, IN_W), tok),
    ]
    args = [xin, mod3, g_pre, wperm, dproj]
    out_specs = [
        pl.BlockSpec((None, 6, D), (lambda b, t: (0, 0, 0)) if is_ctx else (lambda b, t: (b, 0, 0))),
        pl.BlockSpec((8, D), lambda b, t: (0, 0)),
    ]
    out_shape = [jax.ShapeDtypeStruct((1 if is_ctx else nb, 6, D), F32), jax.ShapeDtypeStruct((8, D), F32)]
    if not is_ctx:
        in_specs.append(pl.BlockSpec((None, tn, D), tok))
        args.append(dx_tail)
        out_specs.insert(0, pl.BlockSpec((None, tn, D), tok))
        out_shape.insert(0, jax.ShapeDtypeStruct((nb, length, D), F32))
    h_in_specs, h_out_specs = hosted.specs()
    return pl.pallas_call(
        body, name=name, grid=(nb, length // tn), in_specs=in_specs + h_in_specs, out_specs=out_specs + h_out_specs,
        out_shape=out_shape + hosted.out_shape, scratch_shapes=hosted.scratch,
        compiler_params=_cp(("arbitrary", "arbitrary")),
    )(*args, *hosted.args)


def _ret_specs(order):
    def im(f):
        return lambda *g: f(*order(*g))
    return dict(
        pret=pl.BlockSpec((None, SEQ, 512), im(lambda b, h: (b, 0, h))),
        pretc=pl.BlockSpec((None, LC, 512), im(lambda b, h: (b, 0, h))),
        rd=pl.BlockSpec((None, 2, 1), im(lambda b, h: (h, 0, 0))),
        gn=pl.BlockSpec((None, 1, RD), im(lambda b, h: (h, 0, 0))),
        tab=pl.BlockSpec((SEQ, RD), im(lambda b, h: (0, 0))),
        head=pl.BlockSpec((None, SEQ, RD), im(lambda b, h: (b, 0, h))),
    )


def retention_fwd(pret, pretc, rd, gn, cos, sin, hosted):
    nb = pret.shape[0]
    sp = _ret_specs(lambda b, h: (b, h))

    def body(*refs):
        own_in, h_in, own_out, h_out, own_scr, h_sems = hosted.split(refs, 6, 2)
        p_ref, pc_ref, rd_ref, gn_ref, cos_ref, sin_ref = own_in
        (o_ref, mix_ref), (q_s, k_s, o_s, st_s) = own_out, own_scr
        grid_step = pl.program_id(0) * RH + pl.program_id(1)

        @pl.when(grid_step == 0)
        def _():
            hosted.start(h_in, h_out, h_sems)

        cos_v, sin_v = cos_ref[...], sin_ref[...]
        q_s[...] = _rope(p_ref[:, 0:128], cos_v, sin_v) * (RD ** -0.5)
        k_s[...] = _rope(p_ref[:, 128:256], cos_v, sin_v)
        lgs, init = [], []
        for rev in (False, True):
            lg = jax.nn.log_sigmoid(rd_ref[int(rev):int(rev) + 1, :])
            s = jnp.zeros((RD, RD), F32)
            for n in ((1, 0) if rev else (0, 1)):
                s = _ret_state(pc_ref[n * CH:(n + 1) * CH, 128:256], pc_ref[n * CH:(n + 1) * CH, 256:384], s, lg, rev)
            lgs.append(lg)
            init.append(s)

        dec = _Decays(lgs)
        _state_pass(dec, init, k_s, lambda sl: p_ref[sl, 256:384], st_s)

        def chunk(n, carry):
            sl = pl.ds(pl.multiple_of(n * CH, CH), CH)
            q = q_s[sl, :]
            o_s[sl, :] = (_nn(_nt(q, k_s[sl, :]) * dec.mask, p_ref[sl, 256:384]) + _nn(_both(q, dec.a), st_s[n]))
            return carry

        _chunk_loop(NCH, chunk, 0)
        o = o_s[...]
        o_ref[...] = o
        mix_ref[...] = _ln_gate(o, p_ref[:, 384:512], gn_ref[...]).astype(BF16)

        @pl.when(grid_step == nb * RH - 1)
        def _():
            hosted.finish(h_in, h_out, h_sems)

    h_in_specs, h_out_specs = hosted.specs()
    return pl.pallas_call(
        body, name="retention_fwd", grid=(nb, RH),
        in_specs=[sp["pret"], sp["pretc"], sp["rd"], sp["gn"], sp["tab"], sp["tab"]] + h_in_specs,
        out_specs=[sp["head"], sp["head"]] + h_out_specs,
        out_shape=[jax.ShapeDtypeStruct((nb, SEQ, RH * RD), F32), jax.ShapeDtypeStruct((nb, SEQ, D), BF16)]
        + hosted.out_shape,
        scratch_shapes=[pltpu.VMEM((SEQ, RD), F32)] * 3 + [pltpu.VMEM((NCH, 2 * RD, RD), F32)] + hosted.scratch,
        compiler_params=_cp(("arbitrary", "arbitrary")),
    )(pret, pretc, rd, gn, cos, sin, *hosted.args)


def retention_bwd(pret, pretc, o_all, dmixin, rd, gn, cos, sin, hosted):
    nb = pret.shape[0]
    sp = _ret_specs(lambda h, b: (b, h))

    def body(*refs):
        own_in, h_in, own_out, h_out, own_scr, h_sems = hosted.split(refs, 8, 4)
        p_ref, pc_ref, o_ref, dmix_ref, rd_ref, gn_ref, cos_ref, sin_ref = own_in
        dp_ref, dpc_ref, drd_ref, dgn_ref = own_out
        q_s, k_s, do_s, dq_s, dk_s, dv_s, st_s, gst_s = own_scr
        b = pl.program_id(1)
        grid_step = pl.program_id(0) * nb + b

        @pl.when(grid_step == 0)
        def _():
            hosted.start(h_in, h_out, h_sems)

        cos_v, sin_v = cos_ref[...], sin_ref[...]
        q_s[...] = _rope(p_ref[:, 0:128], cos_v, sin_v) * (RD ** -0.5)
        k_s[...] = _rope(p_ref[:, 128:256], cos_v, sin_v)
        _, gate_vjp = jax.vjp(_ln_gate, o_ref[...], p_ref[:, 384:512], gn_ref[...])
        do, dg, dgn = gate_vjp(dmix_ref[...].astype(F32))
        do_s[...] = do
        dp_ref[:, 384:512] = dg.astype(BF16)

        @pl.when(b == 0)
        def _():
            drd_ref[...] = jnp.zeros_like(drd_ref)
            dgn_ref[...] = jnp.zeros_like(dgn_ref)

        dgn_ref[...] += dgn
        kcs = [pc_ref[n * CH:(n + 1) * CH, 128:256] for n in (0, 1)]
        vcs = [pc_ref[n * CH:(n + 1) * CH, 256:384] for n in (0, 1)]
        dirs = []
        init = []
        for rev in (False, True):
            rdv = rd_ref[int(rev):int(rev) + 1, :]
            lg = jax.nn.log_sigmoid(rdv)
            order_c = (1, 0) if rev else (0, 1)
            s = jnp.zeros((RD, RD), F32)
            ctx_states = []
            for n in order_c:
                ctx_states.append(s)
                s = _ret_state(kcs[n], vcs[n], s, lg, rev)
            dirs.append((rev, order_c, lg, rdv, ctx_states))
            init.append(s)
        dec = _Decays([lg for _, _, lg, _, _ in dirs])

        def v_of(sl):
            return p_ref[sl, 256:384]

        _state_pass(dec, init, k_s, v_of, st_s)
        zeros = jnp.zeros((CH, RD), F32)

        def scores_back(n, carry):
            dmask_sum, da_f, da_b = carry
            sl = pl.ds(pl.multiple_of(n * CH, CH), CH)
            q, k, v, do = q_s[sl, :], k_s[sl, :], v_of(sl), do_s[sl, :]
            scores = _nt(q, k)
            d_att = _nt(do, v)
            d_scores = d_att * dec.mask
            d_qa = _nt(do, st_s[n])
            d_qf, d_qb = d_qa[:, 0:RD], d_qa[:, RD:2 * RD]
            dq_s[sl, :] = _nn(d_scores, k) + d_qf * dec.a[0] + d_qb * dec.a[1]
            dk_s[sl, :] = _tn(d_scores, q)
            dv_s[sl, :] = _tn(scores * dec.mask, do)
            gst_s[n] = _tn(_both(q, dec.a), do)
            return dmask_sum + d_att * scores, da_f + d_qf * q, da_b + d_qb * q

        dmask_sum, da_f, da_b = _chunk_loop(NCH, scores_back, (zeros, zeros, zeros))

        def state_back(t, carry):
            out = []
            for d, r in enumerate(carry):
                n = t if d else (NCH - 1 - t)
                rows = slice(d * RD, (d + 1) * RD)
                own = gst_s[n, rows, :]
                gst_s[n, rows, :] = r
                out.append(own + dec.g[d] * r)
            return tuple(out)

        d_states = _chunk_loop(NCH, state_back, (zeros, zeros))

        def updates_back(n, carry):
            db_f, db_b, dg_f, dg_b = carry
            sl = pl.ds(pl.multiple_of(n * CH, CH), CH)
            k, r, s = k_s[sl, :], gst_s[n], st_s[n]
            d_kw = _nt(v_of(sl), r)
            d_kf, d_kb = d_kw[:, 0:RD], d_kw[:, RD:2 * RD]
            dk_s[sl, :] += d_kf * dec.b[0] + d_kb * dec.b[1]
            dv_s[sl, :] += _nn(_both(k, dec.b), r)
            return (db_f + d_kf * k, db_b + d_kb * k, dg_f + r[0:RD, :] * s[0:RD, :],
                    dg_b + r[RD:2 * RD, :] * s[RD:2 * RD, :])

        db_dg = _chunk_loop(NCH, updates_back, (zeros, zeros, zeros, zeros))
        dkc = [None, None]
        dvc = [None, None]
        for d, ((rev, order_c, lg, rdv, ctx_states), ds) in enumerate(zip(dirs, d_states)):
            dlg = (_total(dmask_sum * dec.dmask[d]) + _total((da_f, da_b)[d] * dec.da[d])
                   + _total(db_dg[d] * dec.db[d]) + CH * dec.g[d] * _total(db_dg[2 + d]))
            for idx in (1, 0):
                n = order_c[idx]
                _, vjp = jax.vjp(functools.partial(_ret_state, reverse=rev), kcs[n], vcs[n], ctx_states[idx], lg)
                dk_c, dv_c, ds, dl = vjp(ds)
                dlg = dlg + dl
                dkc[n] = dk_c if dkc[n] is None else dkc[n] + dk_c
                dvc[n] = dv_c if dvc[n] is None else dvc[n] + dv_c
            drd_ref[int(rev):int(rev) + 1, :] += dlg * jax.nn.sigmoid(-rdv)
        dp_ref[:, 0:128] = _rope_t(dq_s[...] * (RD ** -0.5), cos_v, sin_v).astype(BF16)
        dp_ref[:, 128:256] = _rope_t(dk_s[...], cos_v, sin_v).astype(BF16)
        dp_ref[:, 256:384] = dv_s[...].astype(BF16)
        zero = jnp.zeros((CH, RD), BF16)
        for n in (0, 1):
            rows = slice(n * CH, (n + 1) * CH)
            dpc_ref[rows, 0:128] = zero
            dpc_ref[rows, 128:256] = dkc[n].astype(BF16)
            dpc_ref[rows, 256:384] = dvc[n].astype(BF16)
            dpc_ref[rows, 384:512] = zero

        @pl.when(grid_step == RH * nb - 1)
        def _():
            hosted.finish(h_in, h_out, h_sems)

    h_in_specs, h_out_specs = hosted.specs()
    return pl.pallas_call(
        body, name="retention_bwd", grid=(RH, nb),
        in_specs=[sp["pret"], sp["pretc"], sp["head"], sp["head"], sp["rd"], sp["gn"], sp["tab"], sp["tab"]]
        + h_in_specs,
        out_specs=[
            pl.BlockSpec((None, SEQ, 512), lambda h, b: (b, 0, h)),
            pl.BlockSpec((None, LC, 512), lambda h, b: (b, 0, h)),
            pl.BlockSpec((None, 2, 1), lambda h, b: (h, 0, 0)),
            pl.BlockSpec((None, 1, RD), lambda h, b: (h, 0, 0)),
        ] + h_out_specs,
        out_shape=[
            jax.ShapeDtypeStruct((nb, SEQ, IN_W), BF16),
            jax.ShapeDtypeStruct((nb, LC, IN_W), BF16),
            jax.ShapeDtypeStruct((RH, 2, 1), F32),
            jax.ShapeDtypeStruct((RH, 1, RD), F32),
        ] + hosted.out_shape,
        scratch_shapes=[pltpu.VMEM((SEQ, RD), F32)] * 6 + [pltpu.VMEM((NCH, 2 * RD, RD), F32)] * 2 + hosted.scratch,
        compiler_params=_cp(("arbitrary", "arbitrary")),
    )(pret, pretc, o_all, dmixin, rd, gn, cos, sin, *hosted.args)


def _rpb_flat(rpb):
    return jnp.pad(rpb, ((0, 0), (0, 1), (0, 33))).reshape(NPAIR, 2, 1, 1024)


def _rpb_flat_t(dflat):
    return dflat.reshape(8, 16, 64)[:, :15, :31]


def _barrel(x, left):
    row = lax.broadcasted_iota(jnp.int32, x.shape, 0)
    n = x.shape[1]
    for bit in range(6):
        s = 1 << bit
        x = jnp.where(((row >> bit) & 1) == 1, pltpu.roll(x, (n - s) if left else s, 1), x)
    return x


NA_TILE_ROWS, NA_BAND_ROWS = 4, 12
NA_Q, NA_K = NA_TILE_ROWS * GW, NA_BAND_ROWS * GW
NA_TILES = SEQ // NA_Q


def _band_start(r0):
    return min(max(r0 - 4, 0), 32 - NA_BAND_ROWS)


def _tile_layout(t):
    rows = range(t * NA_TILE_ROWS, (t + 1) * NA_TILE_ROWS)
    return tuple((r if r < 4 else (r - 24 if r > 28 else 4), min(max(r - 4, 0), 24) - _band_start(rows[0]))
                 for r in rows)


NA_CLASSES = sorted(set(_tile_layout(t) for t in range(NA_TILES)))


def _tile_rows(cls):
    return NA_CLASSES[cls]


def _na_tile(t):
    start = jnp.clip(NA_TILE_ROWS * t - 4, 0, 32 - NA_BAND_ROWS)
    cls = 0
    for tile in range(NA_TILES):
        cls = jnp.where(t == tile, NA_CLASSES.index(_tile_layout(tile)), cls)
    return pl.ds(pl.multiple_of(t * NA_Q, NA_Q), NA_Q), pl.ds(pl.multiple_of(start * GW, NA_Q), NA_K), cls


def _na_probs(qst, kb, kc, bias):
    s_loc = _nt(qst, kb) + bias
    s_ctx = _nt(qst, kc)
    m = jnp.maximum(jnp.max(s_loc, axis=1, keepdims=True), jnp.max(s_ctx, axis=1, keepdims=True))
    e_loc, e_ctx = jnp.exp(s_loc - m), jnp.exp(s_ctx - m)
    den = jnp.sum(e_loc, axis=1, keepdims=True) + jnp.sum(e_ctx, axis=1, keepdims=True)
    return e_loc / den, e_ctx / den


def _stack_heads(t):
    lane = lax.broadcasted_iota(jnp.int32, t.shape, 1)
    zero = jnp.zeros_like(t)
    return jnp.concatenate([jnp.where(lane < 64, t, zero), jnp.where(lane >= 64, t, zero)], axis=0)


def _unstack_heads(t):
    n = t.shape[0] // 2
    lane = lax.broadcasted_iota(jnp.int32, (n, 128), 1)
    return jnp.where(lane < 64, t[:n], t[n:])


NA_BIAS_SHAPE = (len(NA_CLASSES), 2 * NA_Q, NA_K)


def _na_bias_pair(flat_ref, out_ref):
    qc = lax.broadcasted_iota(jnp.int32, (GW, 512), 0)
    kc = lax.broadcasted_iota(jnp.int32, (GW, 512), 1) & 63
    start = jnp.clip(qc - 8, 0, GW - 16)
    window = (kc >= start) & (kc < start + 16)
    fill = jnp.full((GW, NA_K - 512), NEG, F32)
    for hh in (0, 1):
        skew = _barrel(pltpu.roll(jnp.broadcast_to(flat_ref[hh], (GW, 1024)), 1024 - 15, 1), left=False)
        by_class = [jnp.where(window, (skew if rc == 7 else pltpu.roll(skew, (9 + rc) * 64, 1))[:, 0:512], NEG)
                    for rc in range(8)]
        for cls in range(len(NA_CLASSES)):
            for qr, (rc, off) in enumerate(_tile_rows(cls)):
                w = jnp.concatenate([by_class[rc], fill], axis=1)
                rows = slice(hh * NA_Q + qr * GW, hh * NA_Q + (qr + 1) * GW)
                out_ref[cls, rows, :] = pltpu.roll(w, off * GW, 1) if off else w


def na_fwd(pna, pnac, bias, mixin, hosted):
    nb = pna.shape[0]

    def body(*refs):
        (p_ref, pc_ref, bias_ref, _), h_in, (out_ref,), h_out, _, h_sems = hosted.split(refs, 4, 1)
        grid_step = pl.program_id(0) * nb + pl.program_id(1)

        @pl.when(grid_step == 0)
        def _():
            hosted.start(h_in, h_out, h_sems)

        kc, vc = pc_ref[:, 128:256], pc_ref[:, 256:384]

        def tile(t, carry):
            qsl, bsl, cls = _na_tile(t)
            kb, vb = p_ref[bsl, 128:256], p_ref[bsl, 256:384]
            p_loc, p_ctx = _na_probs(_stack_heads(p_ref[qsl, 0:128] * 0.125), kb, kc, bias_ref[cls])
            out_ref[qsl, :] = _unstack_heads(_nn(p_loc, vb) + _nn(p_ctx, vc)).astype(BF16)
            return carry

        lax.fori_loop(0, NA_TILES, tile, 0, unroll=4)

        @pl.when(grid_step == NPAIR * nb - 1)
        def _():
            hosted.finish(h_in, h_out, h_sems)

    h_in_specs, h_out_specs = hosted.specs()
    return pl.pallas_call(
        body, name="na_fwd", grid=(NPAIR, nb),
        in_specs=[
            pl.BlockSpec((None, SEQ, 384), lambda p, b: (b, 0, p)),
            pl.BlockSpec((None, LC, 384), lambda p, b: (b, 0, p)),
            pl.BlockSpec((None, len(NA_CLASSES), 2 * NA_Q, NA_K), lambda p, b: (p, 0, 0, 0)),
            pl.BlockSpec(memory_space=pl.ANY),
        ] + h_in_specs,
        out_specs=[pl.BlockSpec((None, SEQ, 128), lambda p, b: (b, 0, 4 + p))] + h_out_specs,
        out_shape=[jax.ShapeDtypeStruct((nb, SEQ, D), BF16)] + hosted.out_shape,
        input_output_aliases={3: 0},
        scratch_shapes=hosted.scratch,
        compiler_params=_cp(("arbitrary", "arbitrary")),
    )(pna, pnac, bias, mixin, *hosted.args)


def na_bwd(pna, pnac, bias, dmixin, dproj, dprojc, hosted):
    nb = pna.shape[0]

    def body(*refs):
        own_in, h_in, own_out, h_out, own_scr, h_sems = hosted.split(refs, 6, 3)
        p_ref, pc_ref, bias_ref, dmix_ref = own_in[:4]
        dp_ref, dpc_ref, dpat_ref = own_out
        dbias_s, dk_s, dv_s, dkc_s, dvc_s, res_s, resc_s = own_scr
        b, part = pl.program_id(1), pl.program_id(2)
        grid_step = (pl.program_id(0) * nb + b) * 3 + part

        @pl.when(grid_step == 0)
        def _():
            hosted.start(h_in, h_out, h_sems)

        @pl.when(grid_step == NPAIR * nb * 3 - 1)
        def _():
            hosted.finish(h_in, h_out, h_sems)

        @pl.when(part == 0)
        def _():
            @pl.when(b == 0)
            def _():
                dbias_s[...] = jnp.zeros_like(dbias_s)

            dk_s[...] = jnp.zeros_like(dk_s)
            dv_s[...] = jnp.zeros_like(dv_s)
            dkc_s[...] = jnp.zeros_like(dkc_s)
            dvc_s[...] = jnp.zeros_like(dvc_s)
            kc, vc = pc_ref[:, 128:256], pc_ref[:, 256:384]

            def tile(t, carry):
                qsl, bsl, cls = _na_tile(t)
                kb, vb = p_ref[bsl, 128:256], p_ref[bsl, 256:384]
                qst, dost = _stack_heads(p_ref[qsl, 0:128] * 0.125), _stack_heads(dmix_ref[qsl, :])
                p_loc, p_ctx = _na_probs(qst, kb, kc, bias_ref[cls])
                dp_loc, dp_ctx = _nt(dost, vb), _nt(dost, vc)
                delta = (jnp.sum(p_loc * dp_loc, axis=1, keepdims=True)
                         + jnp.sum(p_ctx * dp_ctx, axis=1, keepdims=True))
                ds_loc, ds_ctx = p_loc * (dp_loc - delta), p_ctx * (dp_ctx - delta)
                dbias_s[cls] += ds_loc
                res_s[0, qsl, :] = _unstack_heads((_nn(ds_loc, kb) + _nn(ds_ctx, kc)) * 0.125).astype(BF16)
                dk_s[bsl, :] += _tn(ds_loc, qst)
                dv_s[bsl, :] += _tn(p_loc, dost)
                dkc_s[...] += _tn(ds_ctx, qst)
                dvc_s[...] += _tn(p_ctx, dost)
                return carry

            lax.fori_loop(0, NA_TILES, tile, 0, unroll=2)
            res_s[1] = dk_s[...].astype(BF16)
            res_s[2] = dv_s[...].astype(BF16)
            resc_s[0] = jnp.zeros((LC, 128), BF16)
            resc_s[1] = dkc_s[...].astype(BF16)
            resc_s[2] = dvc_s[...].astype(BF16)

            @pl.when(b == nb - 1)
            def _():
                for hh in (0, 1):
                    by_class = [None] * 8
                    for cls in range(len(NA_CLASSES)):
                        for qr, (rc, off) in enumerate(_tile_rows(cls)):
                            w = dbias_s[cls, hh * NA_Q + qr * GW:hh * NA_Q + (qr + 1) * GW, :]
                            w = (pltpu.roll(w, NA_K - off * GW, 1) if off else w)[:, 0:512]
                            by_class[rc] = w if by_class[rc] is None else by_class[rc] + w
                    skew = jnp.zeros((GW, 1024), F32)
                    for rc in range(8):
                        w = jnp.concatenate([by_class[rc], jnp.zeros((GW, 512), F32)], axis=1)
                        skew = skew + (w if rc == 7 else pltpu.roll(w, (7 - rc) * 64, 1))
                    dpat_ref[hh] = jnp.sum(pltpu.roll(_barrel(skew, left=True), 15, 1), axis=0, keepdims=True)

        dp_ref[...] = res_s[part]
        dpc_ref[...] = resc_s[part]

    h_in_specs, h_out_specs = hosted.specs()
    return pl.pallas_call(
        body, name="na_bwd", grid=(NPAIR, nb, 3),
        in_specs=[
            pl.BlockSpec((None, SEQ, 384), lambda p, b, s: (b, 0, p)),
            pl.BlockSpec((None, LC, 384), lambda p, b, s: (b, 0, p)),
            pl.BlockSpec((None, len(NA_CLASSES), 2 * NA_Q, NA_K), lambda p, b, s: (p, 0, 0, 0)),
            pl.BlockSpec((None, SEQ, 128), lambda p, b, s: (b, 0, 4 + p)),
            pl.BlockSpec(memory_space=pl.ANY),
            pl.BlockSpec(memory_space=pl.ANY),
        ] + h_in_specs,
        out_specs=[
            pl.BlockSpec((None, SEQ, 128), lambda p, b, s: (b, 0, 16 + 3 * p + s)),
            pl.BlockSpec((None, LC, 128), lambda p, b, s: (b, 0, 16 + 3 * p + s)),
            pl.BlockSpec((None, 2, 1, 1024), lambda p, b, s: (p, 0, 0, 0)),
        ] + h_out_specs,
        out_shape=[
            jax.ShapeDtypeStruct((nb, SEQ, IN_W), BF16),
            jax.ShapeDtypeStruct((nb, LC, IN_W), BF16),
            jax.ShapeDtypeStruct((NPAIR, 2, 1, 1024), F32),
        ] + hosted.out_shape,
        input_output_aliases={4: 0, 5: 1},
        scratch_shapes=[
            pltpu.VMEM((len(NA_CLASSES), 2 * NA_Q, NA_K), F32),
            pltpu.VMEM((SEQ, 128), F32), pltpu.VMEM((SEQ, 128), F32),
            pltpu.VMEM((LC, 128), F32), pltpu.VMEM((LC, 128), F32),
            pltpu.VMEM((3, SEQ, 128), BF16), pltpu.VMEM((3, LC, 128), BF16),
        ] + hosted.scratch,
        compiler_params=_cp(("arbitrary", "arbitrary", "arbitrary")),
    )(pna, pnac, bias, dmixin, dproj, dprojc, *hosted.args)


def tail_fwd_bwd(x, mixin, tgt, mod3, g_post_mix, g_pre_mlp, g_post_mlp, wout, w1, w2):
    nb = x.shape[0]

    def body(x_ref, mi_ref, tgt_ref, mod_ref, gpm_ref, gpl_ref, gpo_ref, wo_ref, w1_ref, w2_ref,
             dx_ref, dmix_ref, h2_ref, du_ref, a_ref, dm_ref, dmi_ref, dmod_ref, dg_ref, loss_ref):
        b, t = pl.program_id(0), pl.program_id(1)
        gt1, sh2, sc2, gt2 = mod_ref[2:3, :], mod_ref[3:4, :], mod_ref[4:5, :], mod_ref[5:6, :]
        mix = jnp.dot(mi_ref[...], wo_ref[...], preferred_element_type=F32)
        (x1, h2), vjp_a = jax.vjp(_post_mix, x_ref[...], mix, gt1, sc2, sh2, gpm_ref[...], gpl_ref[...])
        h2b = h2.astype(BF16)
        h2_ref[...] = h2b
        m = jnp.zeros((TN, D), F32)
        relus = []
        for j in range(4):
            cols = slice(j * D, (j + 1) * D)
            r = jnp.maximum(jnp.dot(h2b, w1_ref[j], preferred_element_type=F32), 0.0)
            ab = (r * r).astype(BF16)
            a_ref[:, cols] = ab
            m = m + jnp.dot(ab, w2_ref[cols, :], preferred_element_type=F32)
            relus.append(r)
        loss, vjp_b = jax.vjp(_head_loss, x1, m, gt2, gpo_ref[...], tgt_ref[...])
        dx1, dm, dgt2, dgpo, _ = vjp_b(jnp.ones((1, 1), F32))
        dmb = dm.astype(BF16)
        dm_ref[...] = dmb
        dh2 = jnp.zeros((TN, D), F32)
        for j in range(4):
            cols = slice(j * D, (j + 1) * D)
            da = lax.dot_general(dmb, w2_ref[cols, :], (((1,), (1,)), ((), ())), preferred_element_type=F32)
            dub = (da * (2.0 * relus[j])).astype(BF16)
            du_ref[:, cols] = dub
            dh2 = dh2 + lax.dot_general(dub, w1_ref[j], (((1,), (1,)), ((), ())), preferred_element_type=F32)
        dx, dmix, dgt1, dsc2, dsh2, dgpm, dgpl = vjp_a((dx1, dh2))
        dx_ref[...] = dx
        dmixb = dmix.astype(BF16)
        dmix_ref[...] = dmixb
        dmi_ref[...] = lax.dot_general(dmixb, wo_ref[...], (((1,), (1,)), ((), ())),
                                       preferred_element_type=F32).astype(BF16)

        @pl.when(t == 0)
        def _():
            dmod_ref[...] = jnp.zeros_like(dmod_ref)

        @pl.when((t == 0) & (b == 0))
        def _():
            dg_ref[...] = jnp.zeros_like(dg_ref)
            loss_ref[...] = jnp.zeros_like(loss_ref)

        dmod_ref[2:3, :] += dgt1
        dmod_ref[3:4, :] += dsh2
        dmod_ref[4:5, :] += dsc2
        dmod_ref[5:6, :] += dgt2
        dg_ref[0:1, :] += dgpm
        dg_ref[1:2, :] += dgpl
        dg_ref[2:3, :] += dgpo
        loss_ref[...] += jnp.broadcast_to(loss, loss_ref.shape)

    tok = lambda b, t: (b, t, 0)
    const = lambda b, t: (0, 0)
    vec = pl.BlockSpec((1, D), const)
    return pl.pallas_call(
        body, name="tail_fwd_bwd", grid=(nb, SEQ // TN),
        in_specs=[
            pl.BlockSpec((None, TN, D), tok), pl.BlockSpec((None, TN, D), tok), pl.BlockSpec((None, TN, D), tok),
            pl.BlockSpec((None, 6, D), lambda b, t: (b, 0, 0)), vec, vec, vec,
            pl.BlockSpec((D, D), const, pipeline_mode=pl.Buffered(1)),
            pl.BlockSpec((4, D, D), lambda b, t: (0, 0, 0), pipeline_mode=pl.Buffered(1)),
            pl.BlockSpec((DFF, D), const, pipeline_mode=pl.Buffered(1)),
        ],
        out_specs=[
            pl.BlockSpec((None, TN, D), tok), pl.BlockSpec((None, TN, D), tok), pl.BlockSpec((None, TN, D), tok),
            pl.BlockSpec((None, TN, DFF), tok), pl.BlockSpec((None, TN, DFF), tok), pl.BlockSpec((None, TN, D), tok),
            pl.BlockSpec((None, TN, D), tok),
            pl.BlockSpec((None, 6, D), lambda b, t: (b, 0, 0)),
            pl.BlockSpec((8, D), const), pl.BlockSpec((8, 128), const),
        ],
        out_shape=[
            jax.ShapeDtypeStruct((nb, SEQ, D), F32), jax.ShapeDtypeStruct((nb, SEQ, D), BF16),
            jax.ShapeDtypeStruct((nb, SEQ, D), BF16), jax.ShapeDtypeStruct((nb, SEQ, DFF), BF16),
            jax.ShapeDtypeStruct((nb, SEQ, DFF), BF16), jax.ShapeDtypeStruct((nb, SEQ, D), BF16),
            jax.ShapeDtypeStruct((nb, SEQ, D), BF16),
            jax.ShapeDtypeStruct((nb, 6, D), F32), jax.ShapeDtypeStruct((8, D), F32),
            jax.ShapeDtypeStruct((8, 128), F32),
        ],
        compiler_params=_cp(("arbitrary", "arbitrary")),
    )(x, mixin, tgt, mod3, g_post_mix, g_pre_mlp, g_post_mlp, wout, w1, w2)


def weight_grad(pairs, name, out_dtype=F32, col_blocks=False, tm=1024, tn=1024, tk=2048):
    m, n = pairs[0][0].shape[1], pairs[0][1].shape[1]
    tn = min(tn, n)
    tks = [min(tk, xa.shape[0]) for xa, _ in pairs]
    steps = [xa.shape[0] // t for (xa, _), t in zip(pairs, tks)]
    total = sum(steps)
    offs = [sum(steps[:i]) for i in range(len(pairs))]

    def body(*refs):
        out_ref, acc = refs[2 * len(pairs)], refs[-1]
        k = pl.program_id(2)

        @pl.when(k == 0)
        def _():
            acc[...] = jnp.zeros_like(acc)

        for i in range(len(pairs)):
            @pl.when((k >= offs[i]) & (k < offs[i] + steps[i]))
            def _(i=i):
                acc[...] += lax.dot_general(refs[2 * i][...], refs[2 * i + 1][...], (((0,), (0,)), ((), ())),
                                            preferred_element_type=F32)

        if out_dtype != F32:
            @pl.when(k == total - 1)
            def _():
                out_ref[...] = acc[...].astype(out_dtype)

    in_specs, args = [], []
    for i, (xa, ya) in enumerate(pairs):
        clamp = lambda k, i=i: jnp.clip(k - offs[i], 0, steps[i] - 1)
        in_specs.append(pl.BlockSpec((tks[i], tm), lambda a, c, k, clamp=clamp: (clamp(k), a)))
        in_specs.append(pl.BlockSpec((tks[i], tn), lambda a, c, k, clamp=clamp: (clamp(k), c)))
        args += [xa, ya]
    if col_blocks:
        out_spec = pl.BlockSpec((None, tm, tn), lambda a, c, k: (c, a, 0))
        out_shape = jax.ShapeDtypeStruct((n // tn, m, tn), out_dtype)
    else:
        out_spec = pl.BlockSpec((tm, tn), lambda a, c, k: (a, c))
        out_shape = jax.ShapeDtypeStruct((m, n), out_dtype)
    return pl.pallas_call(
        body, name=name, grid=(m // tm, n // tn, total), in_specs=in_specs, out_specs=out_spec, out_shape=out_shape,
        scratch_shapes=[] if out_dtype == F32 else [pltpu.VMEM((tm, tn), F32)],
        compiler_params=_cp(("arbitrary", "arbitrary", "arbitrary")),
    )(*args)


def _perm_block(t):
    return 4 * (t % 4) + t // 4 if t < 16 else 16 + 3 * ((t - 16) % 4) + (t - 16) // 4


def _is_rope_block(p):
    return p < 16 and p % 4 < 2


def unpack_w_in(blocks):
    def body(i_ref, o_ref):
        for t in range(28):
            p = _perm_block(t)
            blk = i_ref[t // 7, :, (t % 7) * 128:(t % 7 + 1) * 128]
            if _is_rope_block(p):
                blk = _pair_order(blk.astype(F32)).astype(BF16)
            o_ref[:, p * 128:(p + 1) * 128] = blk

    return pl.pallas_call(
        body, name="unpack_w_in", grid=(2,),
        in_specs=[pl.BlockSpec((4, D // 2, 896), lambda i: (0, i, 0))],
        out_specs=pl.BlockSpec((D // 2, IN_W), lambda i: (i, 0)),
        out_shape=jax.ShapeDtypeStruct((D, IN_W), BF16),
    )(blocks)


def pack_w_in(dw):
    def body(i_ref, o_ref):
        for t in range(28):
            p = _perm_block(t)
            blk = i_ref[:, p * 128:(p + 1) * 128]
            if _is_rope_block(p):
                blk = _pair_order(blk)
            o_ref[t // 7, :, (t % 7) * 128:(t % 7 + 1) * 128] = blk.astype(BF16)

    return pl.pallas_call(
        body, name="pack_w_in", grid=(4,),
        in_specs=[pl.BlockSpec((D // 4, IN_W), lambda i: (i, 0))],
        out_specs=pl.BlockSpec((4, D // 4, 896), lambda i: (0, i, 0)),
        out_shape=jax.ShapeDtypeStruct((4, D, 896), BF16),
    )(dw)


def _place():
    return lax.axis_index("x"), lax.axis_index("y"), lax.axis_index("c")


class Hosted:
    def __init__(self, args, out_shape, scratch, start, finish):
        self.args, self.out_shape, self.scratch, self.start, self.finish = args, out_shape, scratch, start, finish

    def specs(self):
        hbm = pl.BlockSpec(memory_space=pl.ANY)
        return [hbm] * len(self.args), [hbm] * len(self.out_shape)

    def split(self, refs, n_in, n_out):
        a, b = len(self.args), len(self.out_shape)
        cuts = [n_in, n_in + a, n_in + a + n_out, n_in + a + n_out + b, len(refs) - len(self.scratch)]
        parts = [refs[i:j] for i, j in zip([0] + cuts, cuts + [len(refs)])]
        return parts[0], parts[1], parts[2], parts[3], parts[4], parts[5]


def no_exchange():
    return Hosted([], [], [], lambda *a: None, lambda *a: None)


def run_hosted(hosted, name):
    def body(*refs):
        _, ins, _, outs, _, sems = hosted.split(refs, 0, 0)
        hosted.start(ins, outs, sems)
        hosted.finish(ins, outs, sems)

    in_specs, out_specs = hosted.specs()
    return pl.pallas_call(body, name=name, in_specs=in_specs, out_specs=out_specs, out_shape=hosted.out_shape,
                          scratch_shapes=hosted.scratch)(*hosted.args)


def gather8(blocks):
    na = len(blocks)

    def copies(ins, outs, sems):
        send_sems, recv_sems, local_sem = sems
        x, y, c = _place()
        me, sibling = (x, y, c), (x, y, 1 - c)
        chips = [(1 - x, y), (x, 1 - y), (1 - x, 1 - y)]

        def slot(o_ref, px, py, pc):
            return o_ref.at[4 * px + 2 * py + pc]

        def copy(a, k, block, to, src=None):
            return pltpu.make_async_remote_copy(
                src_ref=slot(outs[a], *block) if src is None else src, dst_ref=slot(outs[a], *block),
                send_sem=send_sems.at[a, k], recv_sem=recv_sems.at[a, k], device_id=to, device_id_type=MESH)

        mine = [pltpu.make_async_copy(ins[a], slot(outs[a], *me), local_sem.at[a]) for a in range(na)]
        first = []
        for a in range(na):
            first.append(copy(a, 0, me, sibling, src=ins[a]))
            first += [copy(a, 1 + j, me, (*chip, c), src=ins[a]) for j, chip in enumerate(chips)]
        return copy, mine, first, me, sibling, chips, c

    def start(ins, outs, sems):
        _, mine, first, *_ = copies(ins, outs, sems)
        for cp in mine + first:
            cp.start()

    def finish(ins, outs, sems):
        copy, mine, first, me, sibling, chips, c = copies(ins, outs, sems)
        passed = []
        for j, chip in enumerate(chips):
            for a in range(na):
                copy(a, 1 + j, (*chip, c), me).wait_recv()
                cp = copy(a, 4 + j, (*chip, c), sibling)
                cp.start()
                passed.append(cp)
        for a in range(na):
            copy(a, 0, sibling, me).wait_recv()
            for j, chip in enumerate(chips):
                copy(a, 4 + j, (*chip, 1 - c), me).wait_recv()
        for cp in first + passed:
            cp.wait_send()
        for cp in mine:
            cp.wait()

    return Hosted(list(blocks), [jax.ShapeDtypeStruct((8,) + b.shape, b.dtype) for b in blocks],
                  [pltpu.SemaphoreType.DMA((na, 7)), pltpu.SemaphoreType.DMA((na, 7)), pltpu.SemaphoreType.DMA((na,))],
                  start, finish)


def chips3(arrays):
    na = len(arrays)

    def copies(ins, outs, sems):
        send_sems, recv_sems = sems
        x, y, c = _place()
        return [pltpu.make_async_remote_copy(
            src_ref=ins[a].at[2 * px + py], dst_ref=outs[a].at[k], send_sem=send_sems.at[a, k],
            recv_sem=recv_sems.at[a, k], device_id=(px, py, c), device_id_type=MESH)
            for a in range(na) for k, (px, py) in enumerate([(1 - x, y), (x, 1 - y), (1 - x, 1 - y)])]

    def start(ins, outs, sems):
        for cp in copies(ins, outs, sems):
            cp.start()

    def finish(ins, outs, sems):
        for cp in copies(ins, outs, sems):
            cp.wait()

    return Hosted(list(arrays), [jax.ShapeDtypeStruct((3,) + a.shape[1:], a.dtype) for a in arrays],
                  [pltpu.SemaphoreType.DMA((na, 3)), pltpu.SemaphoreType.DMA((na, 3))], start, finish)


def siblings(arrays):
    na = len(arrays)

    def copies(ins, outs, sems):
        send_sems, recv_sems = sems
        x, y, c = _place()
        return [pltpu.make_async_remote_copy(
            src_ref=ins[a], dst_ref=outs[a], send_sem=send_sems.at[a], recv_sem=recv_sems.at[a],
            device_id=(x, y, 1 - c), device_id_type=MESH) for a in range(na)]

    def start(ins, outs, sems):
        for cp in copies(ins, outs, sems):
            cp.start()

    def finish(ins, outs, sems):
        for cp in copies(ins, outs, sems):
            cp.wait()

    return Hosted(list(arrays), [jax.ShapeDtypeStruct(a.shape, a.dtype) for a in arrays],
                  [pltpu.SemaphoreType.DMA((na,)), pltpu.SemaphoreType.DMA((na,))], start, finish)


def both(first, second):
    na, no, ns = len(first.args), len(first.out_shape), len(first.scratch)

    def start(ins, outs, sems):
        first.start(ins[:na], outs[:no], sems[:ns])
        second.start(ins[na:], outs[no:], sems[ns:])

    def finish(ins, outs, sems):
        first.finish(ins[:na], outs[:no], sems[:ns])
        second.finish(ins[na:], outs[no:], sems[ns:])

    return Hosted(first.args + second.args, first.out_shape + second.out_shape, first.scratch + second.scratch,
                  start, finish)


def siblings4(arrays):
    na = len(arrays)

    def copies(ins, outs, sems):
        send_sems, recv_sems = sems
        x, y, c = _place()
        return [pltpu.make_async_remote_copy(
            src_ref=ins[a].at[2 * j + 1 - c], dst_ref=outs[a].at[j],
            send_sem=send_sems.at[a, j], recv_sem=recv_sems.at[a, j],
            device_id=(x, y, 1 - c), device_id_type=MESH) for a in range(na) for j in range(4)]

    def start(ins, outs, sems):
        for cp in copies(ins, outs, sems):
            cp.start()

    def finish(ins, outs, sems):
        for cp in copies(ins, outs, sems):
            cp.wait()

    return Hosted(list(arrays), [jax.ShapeDtypeStruct((4,) + a.shape[1:], a.dtype) for a in arrays],
                  [pltpu.SemaphoreType.DMA((na, 4)), pltpu.SemaphoreType.DMA((na, 4))], start, finish)


def _row_tile(r):
    for cand in (512, 256, 128, 64, 32, 16, 8):
        if r % cand == 0:
            return cand
    return r


def chip_partial(place, g8s, landed4s, name):
    n = len(g8s)

    def body(place_ref, *refs):
        del place_ref
        for g_ref, l_ref, o_ref in zip(refs[:n], refs[n:2 * n], refs[2 * n:]):
            o_ref[...] = (g_ref[...].astype(F32) + l_ref[...].astype(F32)).astype(BF16)

    own = [pl.BlockSpec((None,) + g.shape[1:], lambda j, s: (2 * j + s[0], 0, 0)) for g in g8s]
    plain = [pl.BlockSpec((None,) + g.shape[1:], lambda j, s: (j, 0, 0)) for g in g8s]
    return pl.pallas_call(
        body, name=name,
        grid_spec=pltpu.PrefetchScalarGridSpec(num_scalar_prefetch=1, grid=(4,), in_specs=own + plain, out_specs=plain),
        out_shape=[jax.ShapeDtypeStruct((4,) + g.shape[1:], BF16) for g in g8s],
    )(place, *g8s, *landed4s)


def shard_sum(place, partial4s, landed3s, name):
    n = len(partial4s)

    def body(place_ref, *refs):
        del place_ref
        for p_ref, l_ref, o_ref in zip(refs[:n], refs[n:2 * n], refs[2 * n:]):
            acc = p_ref[...].astype(F32)
            for k in range(3):
                acc = acc + l_ref[k].astype(F32)
            o_ref[...] = acc

    def halves(p, lead):
        r, ccols = p.shape[1:]
        return (lead, r // 2, ccols)

    return pl.pallas_call(
        body, name=name,
        grid_spec=pltpu.PrefetchScalarGridSpec(
            num_scalar_prefetch=1, grid=(2,),
            in_specs=[pl.BlockSpec(halves(p, None), lambda i, s: (s[1], i, 0)) for p in partial4s]
            + [pl.BlockSpec(halves(p, 3), lambda i, s: (0, i, 0)) for p in partial4s],
            out_specs=[pl.BlockSpec(halves(p, None)[1:], lambda i, s: (i, 0)) for p in partial4s]),
        out_shape=[jax.ShapeDtypeStruct(p.shape[1:], F32) for p in partial4s],
    )(place, *partial4s, *landed3s)


def _adamw_math(w, g, m, v):
    m2 = B1 * m + (1.0 - B1) * g
    v2 = B2 * v + (1.0 - B2) * (g * g)
    m_hat = m2 / (1.0 - B1 ** STEP)
    v_hat = v2 / (1.0 - B2 ** STEP)
    return -LR * (m_hat / (jnp.sqrt(v_hat) + AEPS) + WD * w), m2, v2


def adamw_halves(place, w, mine, theirs, m, v, name):
    r, ccols = w.shape
    hr = r // 2
    tr = _row_tile(hr)
    nt = hr // tr

    def body(place_ref, w_ref, a_ref, b_ref, m_ref, v_ref, g_out, d_out, m_out, v_out):
        g = jnp.where(pl.program_id(0) == place_ref[0], a_ref[...], b_ref[...])
        d, m2, v2 = _adamw_math(w_ref[...], g, m_ref[...], v_ref[...])
        g_out[...] = g
        d_out[...] = d
        m_out[...] = m2
        v_out[...] = v2

    full = pl.BlockSpec((tr, ccols), lambda h, i, s: (h * nt + i, 0))
    part = pl.BlockSpec((tr, ccols), lambda h, i, s: (i, 0))
    return pl.pallas_call(
        body, name=name,
        grid_spec=pltpu.PrefetchScalarGridSpec(
            num_scalar_prefetch=1, grid=(2, nt), in_specs=[full, part, part, full, full], out_specs=[full] * 4),
        out_shape=[jax.ShapeDtypeStruct((r, ccols), F32)] * 4,
    )(place, w, mine, theirs, m, v)


def adamw_group(place, halved, plain, hosted, name):
    rows = halved[0][0].shape[0]
    tr = 128
    nt = rows // 2 // tr
    nh, npl = len(halved), len(plain)

    def body(place_ref, *refs):
        own_in, h_in, own_out, h_out, _, h_sems = hosted.split(refs, 5 * nh + 4 * npl, 4 * nh + 3 * npl)
        half = pl.program_id(0)
        grid_step = half * nt + pl.program_id(1)

        @pl.when(grid_step == 0)
        def _():
            hosted.start(h_in, h_out, h_sems)

        for i in range(nh):
            w_ref, a_ref, b_ref, m_ref, v_ref = own_in[5 * i:5 * i + 5]
            g = jnp.where(half == place_ref[0], a_ref[...], b_ref[...])
            res = (g,) + _adamw_math(w_ref[...], g, m_ref[...], v_ref[...])
            for o_ref, r in zip(own_out[4 * i:4 * i + 4], res):
                o_ref[...] = r
        for i in range(npl):
            w_ref, g_ref, m_ref, v_ref = own_in[5 * nh + 4 * i:5 * nh + 4 * i + 4]
            res = _adamw_math(w_ref[...], g_ref[...], m_ref[...], v_ref[...])
            for o_ref, r in zip(own_out[4 * nh + 3 * i:4 * nh + 3 * i + 3], res):
                o_ref[...] = r

        @pl.when(grid_step == 2 * nt - 1)
        def _():
            hosted.finish(h_in, h_out, h_sems)

    def full(cols):
        return pl.BlockSpec((tr, cols), lambda h, i, s: (h * nt + i, 0))

    def part(cols):
        return pl.BlockSpec((tr, cols), lambda h, i, s: (i, 0))

    in_specs, out_specs, out_shape, args = [], [], [], []
    for w, a, b, m, v in halved:
        cols = w.shape[1]
        in_specs += [full(cols), part(cols), part(cols), full(cols), full(cols)]
        out_specs += [full(cols)] * 4
        out_shape += [jax.ShapeDtypeStruct(w.shape, F32)] * 4
        args += [w, a, b, m, v]
    for w, g, m, v in plain:
        cols = w.shape[1]
        in_specs += [full(cols)] * 4
        out_specs += [full(cols)] * 3
        out_shape += [jax.ShapeDtypeStruct(w.shape, F32)] * 3
        args += [w, g, m, v]
    h_in_specs, h_out_specs = hosted.specs()
    return pl.pallas_call(
        body, name=name,
        grid_spec=pltpu.PrefetchScalarGridSpec(
            num_scalar_prefetch=1, grid=(2, nt), in_specs=in_specs + h_in_specs, out_specs=out_specs + h_out_specs,
            scratch_shapes=hosted.scratch),
        out_shape=out_shape + hosted.out_shape,
        compiler_params=_cp(("arbitrary", "arbitrary")),
    )(place, *args, *hosted.args)


def _silu(x):
    return x * jax.nn.sigmoid(x)


def prologue(c_rows, c_ctx_row, w_ada, b_shard, rpb_flat, half_w_in, late_shards):
    shape = jax.ShapeDtypeStruct
    n_late = len(late_shards)
    half_shapes = [(w.shape[0] // 2, w.shape[1]) for w in late_shards]
    g_w = gather8([half_w_in])
    g_c = gather8([shape((8, D), F32)])
    g_m = gather8([shape((32, 1536), F32)])

    def body(*refs):
        c_ref, cc_ref, w_ref, b_ref, flat_ref, hw_ref = refs[:6]
        late_refs = refs[6:6 + n_late]
        cin_ref, mg_ref, gw_ref, bias_ref, cos_ref, sin_ref = refs[6 + n_late:12 + n_late]
        rest = refs[12 + n_late:]
        half_refs, (cg_s, ms_s, bias_s) = rest[:n_late], rest[n_late:n_late + 3]
        stage, (load_sem, bias_sem), sems = rest[n_late + 3:2 * n_late + 3], rest[2 * n_late + 3:2 * n_late + 5], \
            rest[2 * n_late + 5:]
        sw, sc, sm = sems[0:3], sems[3:6], sems[6:9]
        core = lax.axis_index("c")
        g_c.start([c_ref], [cg_s], sc)
        g_w.start([hw_ref], [gw_ref], sw)
        loads = [pltpu.make_async_copy(late_refs[a].at[pl.ds(core * half_shapes[a][0], half_shapes[a][0]), :],
                                       stage[a], load_sem.at[a]) for a in range(n_late)]
        for cp in loads:
            cp.start()
        g_c.finish([c_ref], [cg_s], sc)
        cin_ref[...] = jnp.zeros_like(cin_ref)
        for dev in range(8):
            cin_ref[2 * dev:2 * dev + 2, :] = cg_s[dev, 0:2, :]
        cin_ref[16:17, :] = cc_ref[...]
        ms_s[...] = _nn(_silu(cin_ref[...]), w_ref[...]) + b_ref[...]
        g_m.start([ms_s], [mg_ref], sm)
        for a, cp in enumerate(loads):
            cp.wait()
            half_refs[a][...] = stage[a][...].astype(BF16)
        cos_ref[...], sin_ref[...] = _rope_tables()
        stores = []
        for pair in range(NPAIR):
            if pair >= 2:
                stores[pair - 2].wait()
            _na_bias_pair(flat_ref.at[pair], bias_s.at[pair % 2])
            stores.append(pltpu.make_async_copy(bias_s.at[pair % 2], bias_ref.at[pair], bias_sem.at[pair % 2]))
            stores[pair].start()
        for cp in stores[-2:]:
            cp.wait()
        g_w.finish([hw_ref], [gw_ref], sw)
        g_m.finish([ms_s], [mg_ref], sm)

    vmem = pl.BlockSpec(memory_space=pltpu.VMEM)
    hbm = pl.BlockSpec(memory_space=pl.ANY)
    return pl.pallas_call(
        body, name="prologue", in_specs=[vmem, vmem, vmem, vmem, vmem, hbm] + [hbm] * n_late,
        out_specs=[vmem, vmem, hbm, hbm, vmem, vmem] + [vmem] * n_late,
        out_shape=[shape((32, D), F32), shape((8, 32, 1536), F32)] + g_w.out_shape
        + [shape((NPAIR,) + NA_BIAS_SHAPE, F32)] + [shape((SEQ, RD), F32)] * 2 + [shape(s, BF16) for s in half_shapes],
        scratch_shapes=[pltpu.VMEM((8, 8, D), F32), pltpu.VMEM((32, 1536), F32), pltpu.VMEM((2,) + NA_BIAS_SHAPE, F32)]
        + [pltpu.VMEM(s, F32) for s in half_shapes]
        + [pltpu.SemaphoreType.DMA((n_late,)), pltpu.SemaphoreType.DMA((2,))]
        + g_w.scratch + g_c.scratch + g_m.scratch,
        compiler_params=_cp(),
    )(c_rows, c_ctx_row, w_ada, b_shard, rpb_flat, half_w_in, *late_shards)


def ada_grads(cin, gb, gc, w_ada):
    def body(c_ref, gb_ref, gc_ref, w_ref, gw_ref, pc_ref):
        ctx_tot = jnp.sum(gc_ref[...], axis=0, keepdims=True)
        rows = lax.broadcasted_iota(jnp.int32, (16, 512), 0)
        dm = jnp.concatenate([gb_ref[...], jnp.where(rows == 0, ctx_tot, 0.0)], axis=0)
        gw_ref[...] = _tn(_silu(c_ref[...]), dm)
        rows8 = lax.broadcasted_iota(jnp.int32, (8, 512), 0)
        part = _nt(jnp.where(rows8 == 0, ctx_tot, 0.0), w_ref[...])

        @pl.when(pl.program_id(0) == 0)
        def _():
            pc_ref[...] = jnp.zeros_like(pc_ref)

        pc_ref[...] += part

    return pl.pallas_call(
        body, name="ada_grads", grid=(3,),
        in_specs=[pl.BlockSpec((32, D), lambda j: (0, 0)), pl.BlockSpec((16, 512), lambda j: (0, j)),
                  pl.BlockSpec((8, 512), lambda j: (0, j)), pl.BlockSpec((D, 512), lambda j: (0, j))],
        out_specs=[pl.BlockSpec((D, 512), lambda j: (0, j)), pl.BlockSpec((8, D), lambda j: (0, 0))],
        out_shape=[jax.ShapeDtypeStruct((D, 1536), F32), jax.ShapeDtypeStruct((8, D), F32)],
    )(cin, gb, gc, w_ada)


SMALL_SUM_ROWS = 15


def small_update(gsm, gbf, gcf, pcg, params):
    n = len(params)

    def body(*refs):
        gsm_ref, gbf_ref, gcf_ref, pcg_ref = refs[:4]
        wmv, outs, loss_out = refs[4:4 + 3 * n], refs[4 + 3 * n:4 + 7 * n], refs[-1]
        acc = gsm_ref[0]
        for dev in range(1, 8):
            acc = acc + gsm_ref[dev]
        c_ctx = wmv[0][...]
        sg = jax.nn.sigmoid(c_ctx)
        dsilu = pcg_ref[0:1, :] + pcg_ref[2:3, :] + pcg_ref[4:5, :] + pcg_ref[6:7, :]
        lane = lax.broadcasted_iota(jnp.int32, (1, D), 1)
        last = acc[14:15, :]
        grads = [
            dsilu * (sg * (1.0 + c_ctx * (1.0 - sg))),
            jnp.sum(gbf_ref[...], axis=0, keepdims=True) + jnp.sum(gcf_ref[...], axis=0, keepdims=True),
            acc[0:1, :] + acc[1:2, :], acc[2:3, :], acc[3:4, :], acc[4:5, :],
            acc[5:6, 0:512], acc[6:14, :], jnp.where(lane < 8, last, 0.0),
        ]
        loss_out[...] = jnp.broadcast_to(jnp.sum(jnp.where(lane == 8, last, 0.0), axis=1, keepdims=True), (8, 128))
        for i, g in enumerate(grads):
            d, m2, v2 = _adamw_math(wmv[3 * i][...], g, wmv[3 * i + 1][...], wmv[3 * i + 2][...])
            outs[4 * i][...] = g
            outs[4 * i + 1][...] = d
            outs[4 * i + 2][...] = m2
            outs[4 * i + 3][...] = v2

    flat = [a for wmv in params for a in wmv]
    out_shape = [jax.ShapeDtypeStruct(w.shape, F32) for w, _, _ in params for _ in range(4)]
    return pl.pallas_call(
        body, name="small_update", out_shape=out_shape + [jax.ShapeDtypeStruct((8, 128), F32)],
    )(gsm, gbf, gcf, pcg, *flat)


def _pad_row(v, rows):
    flat = v.reshape(-1)
    return jnp.pad(flat, (0, rows * D - flat.shape[0])).reshape(rows, D)


def local_step(x, ctx, tgt, mod3, rope, bias, g_pre_mix, g_post_mix, g_pre_mlp, g_post_mlp, ret_decay, ret_gn,
               wperm, late_weights, early_grads):
    nb = x.shape[0]
    tokens = nb * SEQ
    cos, sin = rope
    rd = ret_decay.T.reshape(RH, 2, 1)
    gn = ret_gn.reshape(RH, 1, RD)
    h, pret, pna = premix_proj(x, mod3, g_pre_mix, wperm, False, "premix_proj")
    hc, pretc, pnac = premix_proj(ctx, mod3, g_pre_mix, wperm, True, "premix_proj_ctx")
    o_all, mixin, gw_out = retention_fwd(pret, pretc, rd, gn, cos, sin, late_weights(0))
    mixin, gw1, gw2 = na_fwd(pna, pnac, bias, mixin, late_weights(1))
    dx_tail, dmix, h2, du, act, dm, dmixin, dmod_t, dg_t, loss_t = tail_fwd_bwd(
        x, mixin, tgt, mod3, g_post_mix, g_pre_mlp, g_post_mlp, gw_out.reshape(D, D), gw1.reshape(4, D, D),
        gw2.reshape(DFF, D))
    dw_out = weight_grad([(mixin.reshape(tokens, D), dmix.reshape(tokens, D))], "grad_w_out", BF16)
    dw1 = weight_grad([(h2.reshape(tokens, D), du.reshape(tokens, DFF))], "grad_w_mlp1", BF16, col_blocks=True)
    dw2 = weight_grad([(act.reshape(tokens, DFF), dm.reshape(tokens, D))], "grad_w_mlp2", BF16)
    dproj, dprojc, drd, dgn, *landed = retention_bwd(pret, pretc, o_all, dmixin, rd, gn, cos, sin,
                                                     early_grads[0](dw_out, dw1, dw2))
    dproj, dprojc, dpat, *early = na_bwd(pna, pnac, bias, dmixin, dproj, dprojc, early_grads[1](landed))
    dw_in = weight_grad([(h.reshape(tokens, D), dproj.reshape(tokens, IN_W)),
                         (hc.reshape(nb * LC, D), dprojc.reshape(nb * LC, IN_W))], "grad_w_in", tn=IN_W // 2, tk=1024)
    dmod_c, dg_c, *late = premix_bwd(ctx, mod3, g_pre_mix, wperm, dprojc, None, early_grads[2](dw_in), "premix_bwd_ctx")
    grad_x, dmod_a, dg_a, *late = premix_bwd(x, mod3, g_pre_mix, wperm, dproj, dx_tail, early_grads[3](late),
                                             "premix_bwd")
    dmod = jnp.concatenate([jnp.concatenate([dmod_a[:, 0:2], dmod_t[:, 2:6]], axis=1), dmod_c], axis=0)
    last = jnp.pad(jnp.concatenate([drd[:, :, 0].T.reshape(8), loss_t[0, 0:1]]), (0, D - 9)).reshape(1, D)
    small = jnp.concatenate([dg_a[0:1], dg_c[0:1], dg_t[0:3], _pad_row(dgn, 1), dpat.reshape(8, D), last], axis=0)
    return grad_x, late, early, dmod, small


def kernel(x, c, ctx, c_ctx, w_ada, b_ada, g_pre_mix, g_post_mix, g_pre_mlp, g_post_mlp, w_in, ret_decay, ret_gn, na_rpb, w_out, w_mlp1, w_mlp2, loss_target, m_c_ctx, m_w_ada, m_b_ada, m_g_pre_mix, m_g_post_mix, m_g_pre_mlp, m_g_post_mlp, m_w_in, m_ret_decay, m_ret_gn, m_na_rpb, m_w_out, m_w_mlp1, m_w_mlp2, v_c_ctx, v_w_ada, v_b_ada, v_g_pre_mix, v_g_post_mix, v_g_pre_mlp, v_g_post_mlp, v_w_in, v_ret_decay, v_ret_gn, v_na_rpb, v_w_out, v_w_mlp1, v_w_mlp2):
    px, py, pc = _place()
    dev = 4 * px + 2 * py + pc
    chip = 2 * px + py

    half_w_in = lax.dynamic_slice_in_dim(w_in[0], pc * (D // 2), D // 2, 0).astype(BF16)
    cin, mg, gw_in, bias, cos, sin, *late_halves = prologue(
        jnp.pad(c, ((0, 6), (0, 0))), c_ctx[None], w_ada[0], lax.dynamic_slice_in_dim(b_ada, chip * 1536, 1536, 1),
        _rpb_flat(na_rpb[0]), half_w_in, [w_out[0], w_mlp1[0], w_mlp2[0]])
    halves = [half_w_in] + late_halves
    wperm = unpack_w_in(gw_in.reshape(4, D, 896))
    mod_all = jnp.concatenate([mg[0], mg[2], mg[4], mg[6]], axis=1)
    mod3 = (jnp.pad(lax.dynamic_slice_in_dim(mod_all, 2 * dev, 2, 0), ((0, 1), (0, 0)))
            + jnp.pad(mod_all[16:17], ((2, 0), (0, 0)))).reshape(3, 6, D)

    place = jnp.stack([pc, chip]).astype(jnp.int32)

    early_names = ["w_out", "w_mlp1", "w_mlp2"]
    early_g8, early_partial = [], []

    def early_a(dw_out, dw1, dw2):
        early_g8[:] = [dw_out.reshape(8, 128, D), dw1.reshape(8, 512, D), dw2.reshape(8, 512, D)]
        return siblings4(early_g8)

    def early_b(landed):
        early_partial[:] = chip_partial(place, early_g8, landed, "rs_chip_sum_early")
        return chips3(early_partial)

    late_partial = []

    late_g8 = []

    def late_c(dw_in):
        late_g8[:] = [pack_w_in(dw_in).reshape(8, 512, 896)]
        return siblings4(late_g8)

    def late_d(landed):
        late_partial[:] = chip_partial(place, late_g8, landed, "rs_chip_sum_w_in")
        return chips3(late_partial)

    grad_x, (landed3_in,), early_landed, dmod, small = local_step(
        x, ctx, loss_target, mod3, (cos, sin), bias, g_pre_mix, g_post_mix, g_pre_mlp, g_post_mlp, ret_decay[0], ret_gn,
        wperm, lambda k: gather8(halves[1:2] if k == 0 else halves[2:4]), (early_a, early_b, late_c, late_d))
    early_mine = shard_sum(place, early_partial, early_landed, "rs_shard_sum_early")

    pay = jnp.concatenate([dmod.reshape(18, D), small, jnp.zeros((40 - 18 - SMALL_SUM_ROWS, D), F32)], axis=0)
    *early_theirs, gs = run_hosted(both(siblings(early_mine), gather8([pay])), "rs_halves_early_gather_small")
    gbf = gs[:, 0:12].reshape(16, 6 * D)
    gcf = gs[:, 12:18].reshape(8, 6 * D)
    gw_ada, pc_part = ada_grads(cin, lax.dynamic_slice_in_dim(gbf, chip * 1536, 1536, 1),
                                lax.dynamic_slice_in_dim(gcf, chip * 1536, 1536, 1), w_ada[0])
    (mine_in,) = shard_sum(place, late_partial, [landed3_in], "rs_shard_sum_w_in")
    theirs_in, pcg = run_hosted(both(siblings([mine_in]), gather8([pc_part])), "rs_halves_w_in_gather_c_ctx")

    grouped = adamw_group(
        place,
        [(w_mlp1[0], early_mine[1], early_theirs[1], m_w_mlp1[0], v_w_mlp1[0]),
         (w_mlp2[0], early_mine[2], early_theirs[2], m_w_mlp2[0], v_w_mlp2[0])],
        [(w_ada[0], gw_ada, m_w_ada[0], v_w_ada[0])], no_exchange(), "adamw_group")
    d_ada, m_ada, v_ada = grouped[8:11]
    big = [
        [r[None] for r in adamw_halves(place, w_in[0], mine_in, theirs_in, m_w_in[0], v_w_in[0], "adamw_w_in")],
        [r[None] for r in adamw_halves(place, w_out[0], early_mine[0], early_theirs[0], m_w_out[0], v_w_out[0],
                                       "adamw_w_out")],
        [r[None] for r in grouped[0:4]], [r[None] for r in grouped[4:8]],
    ]

    def rpb_rows(t):
        return _rpb_flat(t[0]).reshape(8, D)

    def decay_row(t):
        return jnp.pad(t.reshape(1, 8), ((0, 0), (0, D - 8)))

    views = [lambda t: t.reshape(1, D), lambda t: t, lambda t: t, lambda t: t, lambda t: t, lambda t: t, lambda t: t,
             rpb_rows, decay_row]
    back = [lambda t: t.reshape(D), lambda t: t, lambda t: t, lambda t: t, lambda t: t, lambda t: t, lambda t: t,
            lambda t: _rpb_flat_t(t)[None], lambda t: t[:, 0:8].reshape(1, 2, 4)]
    small_w = (c_ctx, b_ada, g_pre_mix, g_post_mix, g_pre_mlp, g_post_mlp, ret_gn, na_rpb, ret_decay)
    small_m = (m_c_ctx, m_b_ada, m_g_pre_mix, m_g_post_mix, m_g_pre_mlp, m_g_post_mlp, m_ret_gn, m_na_rpb, m_ret_decay)
    small_v = (v_c_ctx, v_b_ada, v_g_pre_mix, v_g_post_mix, v_g_pre_mlp, v_g_post_mlp, v_ret_gn, v_na_rpb, v_ret_decay)
    *res, loss8 = small_update(gs[:, 18:18 + SMALL_SUM_ROWS], gbf, gcf, pcg[:, 0],
                               [(f(w), f(m), f(v)) for f, w, m, v in zip(views, small_w, small_m, small_v)])

    def leaves(ada, idx):
        s_c, s_b, s_g1, s_g2, s_g3, s_g4, s_gn, s_rpb, s_rd = [back[i](res[4 * i + idx]) for i in range(9)]
        return [s_c, ada[None], s_b, s_g1, s_g2, s_g3, s_g4, big[0][idx], s_rd, s_gn, s_rpb,
                big[1][idx], big[2][idx], big[3][idx]]

    return (loss8[0, 0], grad_x, *leaves(gw_ada, 0), *leaves(d_ada, 1), *leaves(m_ada, 2), *leaves(v_ada, 3))
```

```python
import functools
import math

import jax
import jax.numpy as jnp
from jax import lax
from jax.experimental import pallas as pl
from jax.experimental.pallas import tpu as pltpu

F32, BF16 = jnp.float32, jnp.bfloat16
D = 1024
SEQ = 2048
LC = 256
GW = 64
RH, RD, CH = 4, 128, 128
NPAIR = 4
IN_W = 3584
RET_W = 2048
DFF = 4096
EPS = 1e-6
NEG = -1e30
TN = 256
NCH = SEQ // CH
LR, B1, B2, AEPS, WD, STEP = 0.001, 0.9, 0.999, 1e-08, 0.01, 10
MESH = pl.DeviceIdType.MESH
VMEM_LIMIT = 56 * 1024 * 1024


def _cp(sem=None):
    return pltpu.CompilerParams(dimension_semantics=sem, vmem_limit_bytes=VMEM_LIMIT)


def _nn(a, b):
    return jnp.dot(a.astype(BF16), b.astype(BF16), preferred_element_type=F32)


def _nt(a, b):
    return lax.dot_general(a.astype(BF16), b.astype(BF16), (((1,), (1,)), ((), ())), preferred_element_type=F32)


def _tn(a, b):
    return lax.dot_general(a.astype(BF16), b.astype(BF16), (((0,), (0,)), ((), ())), preferred_element_type=F32)


@jax.custom_vjp
def mm_tn(a, b):
    return _tn(a, b)


mm_tn.defvjp(lambda a, b: (_tn(a, b), (a, b)), lambda r, g: (_nt(r[1], g), _nn(r[0], g)))


def _rms(x):
    return x * lax.rsqrt(jnp.mean(x * x, axis=-1, keepdims=True) + EPS)


def _rms_mod(x, g, sc, sh):
    return (_rms(x) * g) * (1.0 + sc) + sh


def _post_mix(x, mix, gt1, sc2, sh2, g_post_mix, g_pre_mlp):
    x1 = x + gt1 * (_rms(mix) * g_post_mix)
    return x1, _rms_mod(x1, g_pre_mlp, sc2, sh2)


def _head_loss(x1, m, gt2, g_post_mlp, tgt):
    err = x1 + gt2 * (_rms(m) * g_post_mlp) - tgt
    return 0.5 * jnp.sum(jnp.mean(err * err, axis=-1, keepdims=True), axis=0, keepdims=True)


def _ln_gate(o, g, w):
    mu = jnp.mean(o, axis=-1, keepdims=True)
    var = jnp.mean(jnp.square(o - mu), axis=-1, keepdims=True)
    y = (o - mu) * lax.rsqrt(var + EPS)
    return (y * w) * (g * jax.nn.sigmoid(g))


def _pair_order(x):
    lane = lax.broadcasted_iota(jnp.int32, x.shape, 1)
    return jnp.where((lane >= 32) & (lane < 64), pltpu.roll(x, 96, 1),
                     jnp.where((lane >= 64) & (lane < 96), pltpu.roll(x, 32, 1), x))


def _rope(x, cos, sin):
    return x * cos + pltpu.roll(x, 64, 1) * sin


def _rope_t(g, cos, sin):
    return g * cos + pltpu.roll(g * sin, 64, 1)


def _rope_tables():
    tok = lax.broadcasted_iota(jnp.int32, (SEQ, RD), 0)
    lane = lax.broadcasted_iota(jnp.int32, (SEQ, RD), 1)
    pos = jnp.where((lane & 32) == 0, tok >> 6, tok & (GW - 1)).astype(F32)
    ang = pos * jnp.exp((lane & 31).astype(F32) * (-math.log(10000.0) / 32))
    return jnp.cos(ang), jnp.where(lane < 64, -jnp.sin(ang), jnp.sin(ang))


def _chunk_loop(n, body, init, k=4):
    def several(t, carry):
        for i in range(k):
            carry = body(k * t + i, carry)
        return carry

    return lax.fori_loop(0, n // k, several, init)


def _fiota(shape, dim):
    return lax.broadcasted_iota(jnp.int32, shape, dim).astype(F32)


def _ret_state(k, v, s, lg, reverse):
    pos = _fiota((CH, 1), 0)
    b_exp = pos if reverse else (CH - 1.0 - pos)
    return jnp.exp(lg * CH) * s + mm_tn(k * jnp.exp(lg * b_exp), v)


class _Decays:
    def __init__(self, lgs):
        i, j, pos = _fiota((CH, CH), 0), _fiota((CH, CH), 1), _fiota((CH, 1), 0)
        diffs = (i - j, j - i)
        keep = (diffs[0] >= 0, diffs[1] > 0)
        mats = [jnp.where(m, jnp.exp(lg * jnp.where(m, d, 0.0)), 0.0) for lg, d, m in zip(lgs, diffs, keep)]
        self.mask = mats[0] + mats[1]
        self.dmask = [mats[0] * diffs[0], mats[1] * diffs[1]]
        a_exp, b_exp = (pos + 1.0, CH - pos), (CH - 1.0 - pos, pos)
        self.a = [jnp.exp(lg * e) for lg, e in zip(lgs, a_exp)]
        self.b = [jnp.exp(lg * e) for lg, e in zip(lgs, b_exp)]
        self.da = [a * e for a, e in zip(self.a, a_exp)]
        self.db = [b * e for b, e in zip(self.b, b_exp)]
        self.g = [jnp.exp(lg * CH) for lg in lgs]


def _both(x, w):
    return jnp.concatenate([x * w[0], x * w[1]], axis=1)


def _total(x):
    return jnp.sum(jnp.sum(x, axis=1, keepdims=True), axis=0, keepdims=True)


def _state_pass(dec, init, k_s, v_of, st_s):
    def step(t, carry):
        out = []
        for d, s in enumerate(carry):
            n = (NCH - 1 - t) if d else t
            sl = pl.ds(pl.multiple_of(n * CH, CH), CH)
            st_s[n, d * RD:(d + 1) * RD, :] = s
            out.append(dec.g[d] * s + _tn(k_s[sl, :] * dec.b[d], v_of(sl)))
        return tuple(out)

    _chunk_loop(NCH, step, tuple(init))


def premix_proj(xin, mod3, g_pre, wperm, is_ctx, name):
    nb, length, _ = xin.shape
    tn = min(2 * TN, length)

    def body(x_ref, mod_ref, g_ref, w_ref, h_ref, pret_ref, pna_ref):
        h = _rms_mod(x_ref[...], g_ref[...], mod_ref[1:2, :], mod_ref[0:1, :])
        hb = h.astype(BF16)
        h_ref[...] = hb
        pret_ref[...] = jnp.dot(hb, w_ref[:, :RET_W], preferred_element_type=F32)
        pna_ref[...] = jnp.dot(hb, w_ref[:, RET_W:], preferred_element_type=F32).astype(BF16)

    return pl.pallas_call(
        body, name=name, grid=(nb, length // tn),
        in_specs=[
            pl.BlockSpec((None, tn, D), lambda b, t: (b, t, 0)),
            pl.BlockSpec((None, 6, D), (lambda b, t: (2, 0, 0)) if is_ctx else (lambda b, t: (b, 0, 0))),
            pl.BlockSpec((1, D), lambda b, t: (0, 0)),
            pl.BlockSpec((D, IN_W), lambda b, t: (0, 0), pipeline_mode=pl.Buffered(1)),
        ],
        out_specs=[
            pl.BlockSpec((None, tn, D), lambda b, t: (b, t, 0)),
            pl.BlockSpec((None, tn, RET_W), lambda b, t: (b, t, 0)),
            pl.BlockSpec((None, tn, IN_W - RET_W), lambda b, t: (b, t, 0)),
        ],
        out_shape=[
            jax.ShapeDtypeStruct((nb, length, D), BF16),
            jax.ShapeDtypeStruct((nb, length, RET_W), F32),
            jax.ShapeDtypeStruct((nb, length, IN_W - RET_W), BF16),
        ],
        compiler_params=_cp(("arbitrary", "arbitrary")),
    )(xin, mod3, g_pre, wperm)


def premix_bwd(xin, mod3, g_pre, wperm, dproj, dx_tail, hosted, name):
    nb, length, _ = xin.shape
    tn = min(TN, length)
    is_ctx = dx_tail is None

    def body(*refs):
        own_in, h_in, own_out, h_out, _, h_sems = hosted.split(refs, 5 if is_ctx else 6, 2 if is_ctx else 3)
        if is_ctx:
            (x_ref, mod_ref, g_ref, w_ref, dp_ref), (dmod_ref, dg_ref) = own_in, own_out
        else:
            (x_ref, mod_ref, g_ref, w_ref, dp_ref, dxt_ref), (dx_ref, dmod_ref, dg_ref) = own_in, own_out
        b, t = pl.program_id(0), pl.program_id(1)
        grid_step = b * (length // tn) + t

        @pl.when(grid_step == 0)
        def _():
            hosted.start(h_in, h_out, h_sems)

        @pl.when(grid_step == nb * (length // tn) - 1)
        def _():
            hosted.relay(h_in, h_out, h_sems)
            hosted.finish(h_in, h_out, h_sems)

        dh = lax.dot_general(dp_ref[...], w_ref[...], (((1,), (1,)), ((), ())), preferred_element_type=F32)
        _, vjp = jax.vjp(_rms_mod, x_ref[...], g_ref[...], mod_ref[1:2, :], mod_ref[0:1, :])
        dx, dg, dsc, dsh = vjp(dh)
        if not is_ctx:
            dx_ref[...] = dx + dxt_ref[...]

        @pl.when((t == 0) & ((b == 0) if is_ctx else True))
        def _():
            dmod_ref[...] = jnp.zeros_like(dmod_ref)

        @pl.when((t == 0) & (b == 0))
        def _():
            dg_ref[...] = jnp.zeros_like(dg_ref)

        dmod_ref[0:1, :] += dsh
        dmod_ref[1:2, :] += dsc
        dg_ref[0:1, :] += dg

    tok = lambda b, t: (b, t, 0)
    in_specs = [
        pl.BlockSpec((None, tn, D), tok),
        pl.BlockSpec((None, 6, D), (lambda b, t: (2, 0, 0)) if is_ctx else (lambda b, t: (b, 0, 0))),
        pl.BlockSpec((1, D), lambda b, t: (0, 0)),
        pl.BlockSpec((D, IN_W), lambda b, t: (0, 0), pipeline_mode=pl.Buffered(1)),
        pl.BlockSpec((None, tn, IN_W), tok),
    ]
    args = [xin, mod3, g_pre, wperm, dproj]
    out_specs = [
        pl.BlockSpec((None, 6, D), (lambda b, t: (0, 0, 0)) if is_ctx else (lambda b, t: (b, 0, 0))),
        pl.BlockSpec((8, D), lambda b, t: (0, 0)),
    ]
    out_shape = [jax.ShapeDtypeStruct((1 if is_ctx else nb, 6, D), F32), jax.ShapeDtypeStruct((8, D), F32)]
    if not is_ctx:
        in_specs.append(pl.BlockSpec((None, tn, D), tok))
        args.append(dx_tail)
        out_specs.insert(0, pl.BlockSpec((None, tn, D), tok))
        out_shape.insert(0, jax.ShapeDtypeStruct((nb, length, D), F32))
    h_in_specs, h_out_specs = hosted.specs()
    return pl.pallas_call(
        body, name=name, grid=(nb, length // tn), in_specs=in_specs + h_in_specs, out_specs=out_specs + h_out_specs,
        out_shape=out_shape + hosted.out_shape, scratch_shapes=hosted.scratch,
        compiler_params=_cp(("arbitrary", "arbitrary")),
    )(*args, *hosted.args)


def _ret_specs(order):
    def im(f):
        return lambda *g: f(*order(*g))
    return dict(
        pret=pl.BlockSpec((None, SEQ, 512), im(lambda b, h: (b, 0, h))),
        pretc=pl.BlockSpec((None, LC, 512), im(lambda b, h: (b, 0, h))),
        rd=pl.BlockSpec((None, 2, 1), im(lambda b, h: (h, 0, 0))),
        gn=pl.BlockSpec((None, 1, RD), im(lambda b, h: (h, 0, 0))),
        tab=pl.BlockSpec((SEQ, RD), im(lambda b, h: (0, 0))),
        head=pl.BlockSpec((None, SEQ, RD), im(lambda b, h: (b, 0, h))),
    )


def retention_fwd(pret, pretc, rd, gn, cos, sin, hosted):
    nb = pret.shape[0]
    sp = _ret_specs(lambda b, h: (b, h))

    def body(*refs):
        own_in, h_in, own_out, h_out, own_scr, h_sems = hosted.split(refs, 6, 2)
        p_ref, pc_ref, rd_ref, gn_ref, cos_ref, sin_ref = own_in
        (o_ref, mix_ref), (q_s, k_s, o_s, st_s) = own_out, own_scr
        grid_step = pl.program_id(0) * RH + pl.program_id(1)

        @pl.when(grid_step == 0)
        def _():
            hosted.start(h_in, h_out, h_sems)

        @pl.when(grid_step == nb * RH - 1)
        def _():
            hosted.relay(h_in, h_out, h_sems)

        cos_v, sin_v = cos_ref[...], sin_ref[...]
        q_s[...] = _rope(p_ref[:, 0:128], cos_v, sin_v) * (RD ** -0.5)
        k_s[...] = _rope(p_ref[:, 128:256], cos_v, sin_v)
        lgs, init = [], []
        for rev in (False, True):
            lg = jax.nn.log_sigmoid(rd_ref[int(rev):int(rev) + 1, :])
            s = jnp.zeros((RD, RD), F32)
            for n in ((1, 0) if rev else (0, 1)):
                s = _ret_state(pc_ref[n * CH:(n + 1) * CH, 128:256], pc_ref[n * CH:(n + 1) * CH, 256:384], s, lg, rev)
            lgs.append(lg)
            init.append(s)

        dec = _Decays(lgs)
        _state_pass(dec, init, k_s, lambda sl: p_ref[sl, 256:384], st_s)

        def chunk(n, carry):
            sl = pl.ds(pl.multiple_of(n * CH, CH), CH)
            q = q_s[sl, :]
            o_s[sl, :] = (_nn(_nt(q, k_s[sl, :]) * dec.mask, p_ref[sl, 256:384]) + _nn(_both(q, dec.a), st_s[n]))
            return carry

        _chunk_loop(NCH, chunk, 0)
        o = o_s[...]
        o_ref[...] = o
        mix_ref[...] = _ln_gate(o, p_ref[:, 384:512], gn_ref[...]).astype(BF16)

        @pl.when(grid_step == nb * RH - 1)
        def _():
            hosted.finish(h_in, h_out, h_sems)

    h_in_specs, h_out_specs = hosted.specs()
    return pl.pallas_call(
        body, name="retention_fwd", grid=(nb, RH),
        in_specs=[sp["pret"], sp["pretc"], sp["rd"], sp["gn"], sp["tab"], sp["tab"]] + h_in_specs,
        out_specs=[sp["head"], sp["head"]] + h_out_specs,
        out_shape=[jax.ShapeDtypeStruct((nb, SEQ, RH * RD), F32), jax.ShapeDtypeStruct((nb, SEQ, D), BF16)]
        + hosted.out_shape,
        scratch_shapes=[pltpu.VMEM((SEQ, RD), F32)] * 3 + [pltpu.VMEM((NCH, 2 * RD, RD), F32)] + hosted.scratch,
        compiler_params=_cp(("arbitrary", "arbitrary")),
    )(pret, pretc, rd, gn, cos, sin, *hosted.args)


def retention_bwd(pret, pretc, o_all, dmixin, rd, gn, cos, sin, hosted):
    nb = pret.shape[0]
    sp = _ret_specs(lambda h, b: (b, h))

    def body(*refs):
        own_in, h_in, own_out, h_out, own_scr, h_sems = hosted.split(refs, 8, 4)
        p_ref, pc_ref, o_ref, dmix_ref, rd_ref, gn_ref, cos_ref, sin_ref = own_in
        dp_ref, dpc_ref, drd_ref, dgn_ref = own_out
        q_s, k_s, do_s, dq_s, dk_s, dv_s, st_s, gst_s = own_scr
        b = pl.program_id(1)
        grid_step = pl.program_id(0) * nb + b

        @pl.when(grid_step == 0)
        def _():
            hosted.start(h_in, h_out, h_sems)

        cos_v, sin_v = cos_ref[...], sin_ref[...]
        q_s[...] = _rope(p_ref[:, 0:128], cos_v, sin_v) * (RD ** -0.5)
        k_s[...] = _rope(p_ref[:, 128:256], cos_v, sin_v)
        _, gate_vjp = jax.vjp(_ln_gate, o_ref[...], p_ref[:, 384:512], gn_ref[...])
        do, dg, dgn = gate_vjp(dmix_ref[...].astype(F32))
        do_s[...] = do
        dp_ref[:, 384:512] = dg.astype(BF16)

        @pl.when(b == 0)
        def _():
            drd_ref[...] = jnp.zeros_like(drd_ref)
            dgn_ref[...] = jnp.zeros_like(dgn_ref)

        dgn_ref[...] += dgn
        kcs = [pc_ref[n * CH:(n + 1) * CH, 128:256] for n in (0, 1)]
        vcs = [pc_ref[n * CH:(n + 1) * CH, 256:384] for n in (0, 1)]
        dirs = []
        init = []
        for rev in (False, True):
            rdv = rd_ref[int(rev):int(rev) + 1, :]
            lg = jax.nn.log_sigmoid(rdv)
            order_c = (1, 0) if rev else (0, 1)
            s = jnp.zeros((RD, RD), F32)
            ctx_states = []
            for n in order_c:
                ctx_states.append(s)
                s = _ret_state(kcs[n], vcs[n], s, lg, rev)
            dirs.append((rev, order_c, lg, rdv, ctx_states))
            init.append(s)
        dec = _Decays([lg for _, _, lg, _, _ in dirs])

        def v_of(sl):
            return p_ref[sl, 256:384]

        _state_pass(dec, init, k_s, v_of, st_s)
        zeros = jnp.zeros((CH, RD), F32)

        def scores_back(n, carry):
            dmask_sum, da_f, da_b = carry
            sl = pl.ds(pl.multiple_of(n * CH, CH), CH)
            q, k, v, do = q_s[sl, :], k_s[sl, :], v_of(sl), do_s[sl, :]
            scores = _nt(q, k)
            d_att = _nt(do, v)
            d_scores = d_att * dec.mask
            d_qa = _nt(do, st_s[n])
            d_qf, d_qb = d_qa[:, 0:RD], d_qa[:, RD:2 * RD]
            dq_s[sl, :] = _nn(d_scores, k) + d_qf * dec.a[0] + d_qb * dec.a[1]
            dk_s[sl, :] = _tn(d_scores, q)
            dv_s[sl, :] = _tn(scores * dec.mask, do)
            gst_s[n] = _tn(_both(q, dec.a), do)
            return dmask_sum + d_att * scores, da_f + d_qf * q, da_b + d_qb * q

        dmask_sum, da_f, da_b = _chunk_loop(NCH, scores_back, (zeros, zeros, zeros))

        def state_back(t, carry):
            out = []
            for d, r in enumerate(carry):
                n = t if d else (NCH - 1 - t)
                rows = slice(d * RD, (d + 1) * RD)
                own = gst_s[n, rows, :]
                gst_s[n, rows, :] = r
                out.append(own + dec.g[d] * r)
            return tuple(out)

        d_states = _chunk_loop(NCH, state_back, (zeros, zeros))

        def updates_back(n, carry):
            db_f, db_b, dg_f, dg_b = carry
            sl = pl.ds(pl.multiple_of(n * CH, CH), CH)
            k, r, s = k_s[sl, :], gst_s[n], st_s[n]
            d_kw = _nt(v_of(sl), r)
            d_kf, d_kb = d_kw[:, 0:RD], d_kw[:, RD:2 * RD]
            dk_s[sl, :] += d_kf * dec.b[0] + d_kb * dec.b[1]
            dv_s[sl, :] += _nn(_both(k, dec.b), r)
            return (db_f + d_kf * k, db_b + d_kb * k, dg_f + r[0:RD, :] * s[0:RD, :],
                    dg_b + r[RD:2 * RD, :] * s[RD:2 * RD, :])

        db_dg = _chunk_loop(NCH, updates_back, (zeros, zeros, zeros, zeros))
        dkc = [None, None]
        dvc = [None, None]
        for d, ((rev, order_c, lg, rdv, ctx_states), ds) in enumerate(zip(dirs, d_states)):
            dlg = (_total(dmask_sum * dec.dmask[d]) + _total((da_f, da_b)[d] * dec.da[d])
                   + _total(db_dg[d] * dec.db[d]) + CH * dec.g[d] * _total(db_dg[2 + d]))
            for idx in (1, 0):
                n = order_c[idx]
                _, vjp = jax.vjp(functools.partial(_ret_state, reverse=rev), kcs[n], vcs[n], ctx_states[idx], lg)
                dk_c, dv_c, ds, dl = vjp(ds)
                dlg = dlg + dl
                dkc[n] = dk_c if dkc[n] is None else dkc[n] + dk_c
                dvc[n] = dv_c if dvc[n] is None else dvc[n] + dv_c
            drd_ref[int(rev):int(rev) + 1, :] += dlg * jax.nn.sigmoid(-rdv)
        dp_ref[:, 0:128] = _rope_t(dq_s[...] * (RD ** -0.5), cos_v, sin_v).astype(BF16)
        dp_ref[:, 128:256] = _rope_t(dk_s[...], cos_v, sin_v).astype(BF16)
        dp_ref[:, 256:384] = dv_s[...].astype(BF16)
        zero = jnp.zeros((CH, RD), BF16)
        for n in (0, 1):
            rows = slice(n * CH, (n + 1) * CH)
            dpc_ref[rows, 0:128] = zero
            dpc_ref[rows, 128:256] = dkc[n].astype(BF16)
            dpc_ref[rows, 256:384] = dvc[n].astype(BF16)
            dpc_ref[rows, 384:512] = zero

        @pl.when(grid_step == RH * nb - 1)
        def _():
            hosted.relay(h_in, h_out, h_sems)
            hosted.finish(h_in, h_out, h_sems)

    h_in_specs, h_out_specs = hosted.specs()
    return pl.pallas_call(
        body, name="retention_bwd", grid=(RH, nb),
        in_specs=[sp["pret"], sp["pretc"], sp["head"], sp["head"], sp["rd"], sp["gn"], sp["tab"], sp["tab"]]
        + h_in_specs,
        out_specs=[
            pl.BlockSpec((None, SEQ, 512), lambda h, b: (b, 0, h)),
            pl.BlockSpec((None, LC, 512), lambda h, b: (b, 0, h)),
            pl.BlockSpec((None, 2, 1), lambda h, b: (h, 0, 0)),
            pl.BlockSpec((None, 1, RD), lambda h, b: (h, 0, 0)),
        ] + h_out_specs,
        out_shape=[
            jax.ShapeDtypeStruct((nb, SEQ, IN_W), BF16),
            jax.ShapeDtypeStruct((nb, LC, IN_W), BF16),
            jax.ShapeDtypeStruct((RH, 2, 1), F32),
            jax.ShapeDtypeStruct((RH, 1, RD), F32),
        ] + hosted.out_shape,
        scratch_shapes=[pltpu.VMEM((SEQ, RD), F32)] * 6 + [pltpu.VMEM((NCH, 2 * RD, RD), F32)] * 2 + hosted.scratch,
        compiler_params=_cp(("arbitrary", "arbitrary")),
    )(pret, pretc, o_all, dmixin, rd, gn, cos, sin, *hosted.args)


def _rpb_flat(rpb):
    return jnp.pad(rpb, ((0, 0), (0, 1), (0, 33))).reshape(NPAIR, 2, 1, 1024)


def _rpb_flat_t(dflat):
    return dflat.reshape(8, 16, 64)[:, :15, :31]


def _barrel(x, left):
    row = lax.broadcasted_iota(jnp.int32, x.shape, 0)
    n = x.shape[1]
    for bit in range(6):
        s = 1 << bit
        x = jnp.where(((row >> bit) & 1) == 1, pltpu.roll(x, (n - s) if left else s, 1), x)
    return x


NA_TILE_ROWS, NA_BAND_ROWS = 4, 12
NA_Q, NA_K = NA_TILE_ROWS * GW, NA_BAND_ROWS * GW
NA_TILES = SEQ // NA_Q


def _band_start(r0):
    return min(max(r0 - 4, 0), 32 - NA_BAND_ROWS)


def _tile_layout(t):
    rows = range(t * NA_TILE_ROWS, (t + 1) * NA_TILE_ROWS)
    return tuple((r if r < 4 else (r - 24 if r > 28 else 4), min(max(r - 4, 0), 24) - _band_start(rows[0]))
                 for r in rows)


NA_CLASSES = sorted(set(_tile_layout(t) for t in range(NA_TILES)))


def _tile_rows(cls):
    return NA_CLASSES[cls]


def _na_tile(t):
    start = jnp.clip(NA_TILE_ROWS * t - 4, 0, 32 - NA_BAND_ROWS)
    cls = 0
    for tile in range(NA_TILES):
        cls = jnp.where(t == tile, NA_CLASSES.index(_tile_layout(tile)), cls)
    return pl.ds(pl.multiple_of(t * NA_Q, NA_Q), NA_Q), pl.ds(pl.multiple_of(start * GW, NA_Q), NA_K), cls


def _na_probs(qst, kb, kc, bias):
    s_loc = _nt(qst, kb) + bias
    s_ctx = _nt(qst, kc)
    m = jnp.maximum(jnp.max(s_loc, axis=1, keepdims=True), jnp.max(s_ctx, axis=1, keepdims=True))
    e_loc, e_ctx = jnp.exp(s_loc - m), jnp.exp(s_ctx - m)
    den = jnp.sum(e_loc, axis=1, keepdims=True) + jnp.sum(e_ctx, axis=1, keepdims=True)
    return e_loc / den, e_ctx / den


def _stack_heads(t):
    lane = lax.broadcasted_iota(jnp.int32, t.shape, 1)
    zero = jnp.zeros_like(t)
    return jnp.concatenate([jnp.where(lane < 64, t, zero), jnp.where(lane >= 64, t, zero)], axis=0)


def _unstack_heads(t):
    n = t.shape[0] // 2
    lane = lax.broadcasted_iota(jnp.int32, (n, 128), 1)
    return jnp.where(lane < 64, t[:n], t[n:])


NA_BIAS_SHAPE = (len(NA_CLASSES), 2 * NA_Q, NA_K)


def _na_bias_pair(flat_ref, out_ref):
    qc = lax.broadcasted_iota(jnp.int32, (GW, 512), 0)
    kc = lax.broadcasted_iota(jnp.int32, (GW, 512), 1) & 63
    start = jnp.clip(qc - 8, 0, GW - 16)
    window = (kc >= start) & (kc < start + 16)
    fill = jnp.full((GW, NA_K - 512), NEG, F32)
    for hh in (0, 1):
        skew = _barrel(pltpu.roll(jnp.broadcast_to(flat_ref[hh], (GW, 1024)), 1024 - 15, 1), left=False)
        by_class = [jnp.where(window, (skew if rc == 7 else pltpu.roll(skew, (9 + rc) * 64, 1))[:, 0:512], NEG)
                    for rc in range(8)]
        for cls in range(len(NA_CLASSES)):
            for qr, (rc, off) in enumerate(_tile_rows(cls)):
                w = jnp.concatenate([by_class[rc], fill], axis=1)
                rows = slice(hh * NA_Q + qr * GW, hh * NA_Q + (qr + 1) * GW)
                out_ref[cls, rows, :] = pltpu.roll(w, off * GW, 1) if off else w


def na_fwd(pna, pnac, bias, mixin, hosted):
    nb = pna.shape[0]

    def body(*refs):
        (p_ref, pc_ref, bias_ref, _), h_in, (out_ref,), h_out, _, h_sems = hosted.split(refs, 4, 1)
        grid_step = pl.program_id(0) * nb + pl.program_id(1)

        @pl.when(grid_step == 0)
        def _():
            hosted.start(h_in, h_out, h_sems)

        @pl.when(grid_step == NPAIR * nb - 1)
        def _():
            hosted.relay(h_in, h_out, h_sems)

        kc, vc = pc_ref[:, 128:256], pc_ref[:, 256:384]

        def tile(t, carry):
            qsl, bsl, cls = _na_tile(t)
            kb, vb = p_ref[bsl, 128:256], p_ref[bsl, 256:384]
            p_loc, p_ctx = _na_probs(_stack_heads(p_ref[qsl, 0:128] * 0.125), kb, kc, bias_ref[cls])
            out_ref[qsl, :] = _unstack_heads(_nn(p_loc, vb) + _nn(p_ctx, vc)).astype(BF16)
            return carry

        lax.fori_loop(0, NA_TILES, tile, 0, unroll=4)

        @pl.when(grid_step == NPAIR * nb - 1)
        def _():
            hosted.finish(h_in, h_out, h_sems)

    h_in_specs, h_out_specs = hosted.specs()
    return pl.pallas_call(
        body, name="na_fwd", grid=(NPAIR, nb),
        in_specs=[
            pl.BlockSpec((None, SEQ, 384), lambda p, b: (b, 0, p)),
            pl.BlockSpec((None, LC, 384), lambda p, b: (b, 0, p)),
            pl.BlockSpec((None, len(NA_CLASSES), 2 * NA_Q, NA_K), lambda p, b: (p, 0, 0, 0)),
            pl.BlockSpec(memory_space=pl.ANY),
        ] + h_in_specs,
        out_specs=[pl.BlockSpec((None, SEQ, 128), lambda p, b: (b, 0, 4 + p))] + h_out_specs,
        out_shape=[jax.ShapeDtypeStruct((nb, SEQ, D), BF16)] + hosted.out_shape,
        input_output_aliases={3: 0},
        scratch_shapes=hosted.scratch,
        compiler_params=_cp(("arbitrary", "arbitrary")),
    )(pna, pnac, bias, mixin, *hosted.args)


def na_bwd(pna, pnac, bias, dmixin, dproj, dprojc, hosted):
    nb = pna.shape[0]

    def body(*refs):
        own_in, h_in, own_out, h_out, own_scr, h_sems = hosted.split(refs, 6, 3)
        p_ref, pc_ref, bias_ref, dmix_ref = own_in[:4]
        dp_ref, dpc_ref, dpat_ref = own_out
        dbias_s, dk_s, dv_s, dkc_s, dvc_s, res_s, resc_s = own_scr
        b, part = pl.program_id(1), pl.program_id(2)
        grid_step = (pl.program_id(0) * nb + b) * 3 + part

        @pl.when(grid_step == 0)
        def _():
            hosted.start(h_in, h_out, h_sems)

        @pl.when(grid_step == NPAIR * nb * 3 - 1)
        def _():
            hosted.relay(h_in, h_out, h_sems)
            hosted.finish(h_in, h_out, h_sems)

        @pl.when(part == 0)
        def _():
            @pl.when(b == 0)
            def _():
                dbias_s[...] = jnp.zeros_like(dbias_s)

            dk_s[...] = jnp.zeros_like(dk_s)
            dv_s[...] = jnp.zeros_like(dv_s)
            dkc_s[...] = jnp.zeros_like(dkc_s)
            dvc_s[...] = jnp.zeros_like(dvc_s)
            kc, vc = pc_ref[:, 128:256], pc_ref[:, 256:384]

            def tile(t, carry):
                qsl, bsl, cls = _na_tile(t)
                kb, vb = p_ref[bsl, 128:256], p_ref[bsl, 256:384]
                qst, dost = _stack_heads(p_ref[qsl, 0:128] * 0.125), _stack_heads(dmix_ref[qsl, :])
                p_loc, p_ctx = _na_probs(qst, kb, kc, bias_ref[cls])
                dp_loc, dp_ctx = _nt(dost, vb), _nt(dost, vc)
                delta = (jnp.sum(p_loc * dp_loc, axis=1, keepdims=True)
                         + jnp.sum(p_ctx * dp_ctx, axis=1, keepdims=True))
                ds_loc, ds_ctx = p_loc * (dp_loc - delta), p_ctx * (dp_ctx - delta)
                dbias_s[cls] += ds_loc
                res_s[0, qsl, :] = _unstack_heads((_nn(ds_loc, kb) + _nn(ds_ctx, kc)) * 0.125).astype(BF16)
                dk_s[bsl, :] += _tn(ds_loc, qst)
                dv_s[bsl, :] += _tn(p_loc, dost)
                dkc_s[...] += _tn(ds_ctx, qst)
                dvc_s[...] += _tn(p_ctx, dost)
                return carry

            lax.fori_loop(0, NA_TILES, tile, 0, unroll=2)
            res_s[1] = dk_s[...].astype(BF16)
            res_s[2] = dv_s[...].astype(BF16)
            resc_s[0] = jnp.zeros((LC, 128), BF16)
            resc_s[1] = dkc_s[...].astype(BF16)
            resc_s[2] = dvc_s[...].astype(BF16)

            @pl.when(b == nb - 1)
            def _():
                for hh in (0, 1):
                    by_class = [None] * 8
                    for cls in range(len(NA_CLASSES)):
                        for qr, (rc, off) in enumerate(_tile_rows(cls)):
                            w = dbias_s[cls, hh * NA_Q + qr * GW:hh * NA_Q + (qr + 1) * GW, :]
                            w = (pltpu.roll(w, NA_K - off * GW, 1) if off else w)[:, 0:512]
                            by_class[rc] = w if by_class[rc] is None else by_class[rc] + w
                    skew = jnp.zeros((GW, 1024), F32)
                    for rc in range(8):
                        w = jnp.concatenate([by_class[rc], jnp.zeros((GW, 512), F32)], axis=1)
                        skew = skew + (w if rc == 7 else pltpu.roll(w, (7 - rc) * 64, 1))
                    dpat_ref[hh] = jnp.sum(pltpu.roll(_barrel(skew, left=True), 15, 1), axis=0, keepdims=True)

        dp_ref[...] = res_s[part]
        dpc_ref[...] = resc_s[part]

    h_in_specs, h_out_specs = hosted.specs()
    return pl.pallas_call(
        body, name="na_bwd", grid=(NPAIR, nb, 3),
        in_specs=[
            pl.BlockSpec((None, SEQ, 384), lambda p, b, s: (b, 0, p)),
            pl.BlockSpec((None, LC, 384), lambda p, b, s: (b, 0, p)),
            pl.BlockSpec((None, len(NA_CLASSES), 2 * NA_Q, NA_K), lambda p, b, s: (p, 0, 0, 0)),
            pl.BlockSpec((None, SEQ, 128), lambda p, b, s: (b, 0, 4 + p)),
            pl.BlockSpec(memory_space=pl.ANY),
            pl.BlockSpec(memory_space=pl.ANY),
        ] + h_in_specs,
        out_specs=[
            pl.BlockSpec((None, SEQ, 128), lambda p, b, s: (b, 0, 16 + 3 * p + s)),
            pl.BlockSpec((None, LC, 128), lambda p, b, s: (b, 0, 16 + 3 * p + s)),
            pl.BlockSpec((None, 2, 1, 1024), lambda p, b, s: (p, 0, 0, 0)),
        ] + h_out_specs,
        out_shape=[
            jax.ShapeDtypeStruct((nb, SEQ, IN_W), BF16),
            jax.ShapeDtypeStruct((nb, LC, IN_W), BF16),
            jax.ShapeDtypeStruct((NPAIR, 2, 1, 1024), F32),
        ] + hosted.out_shape,
        input_output_aliases={4: 0, 5: 1},
        scratch_shapes=[
            pltpu.VMEM((len(NA_CLASSES), 2 * NA_Q, NA_K), F32),
            pltpu.VMEM((SEQ, 128), F32), pltpu.VMEM((SEQ, 128), F32),
            pltpu.VMEM((LC, 128), F32), pltpu.VMEM((LC, 128), F32),
            pltpu.VMEM((3, SEQ, 128), BF16), pltpu.VMEM((3, LC, 128), BF16),
        ] + hosted.scratch,
        compiler_params=_cp(("arbitrary", "arbitrary", "arbitrary")),
    )(pna, pnac, bias, dmixin, dproj, dprojc, *hosted.args)


def tail_fwd_bwd(x, mixin, tgt, mod3, g_post_mix, g_pre_mlp, g_post_mlp, wout, w1, w2):
    nb = x.shape[0]

    def body(x_ref, mi_ref, tgt_ref, mod_ref, gpm_ref, gpl_ref, gpo_ref, wo_ref, w1_ref, w2_ref,
             dx_ref, dmix_ref, h2_ref, du_ref, a_ref, dm_ref, dmi_ref, dmod_ref, dg_ref, loss_ref):
        b, t = pl.program_id(0), pl.program_id(1)
        gt1, sh2, sc2, gt2 = mod_ref[2:3, :], mod_ref[3:4, :], mod_ref[4:5, :], mod_ref[5:6, :]
        mix = jnp.dot(mi_ref[...], wo_ref[...], preferred_element_type=F32)
        (x1, h2), vjp_a = jax.vjp(_post_mix, x_ref[...], mix, gt1, sc2, sh2, gpm_ref[...], gpl_ref[...])
        h2b = h2.astype(BF16)
        h2_ref[...] = h2b
        m = jnp.zeros((TN, D), F32)
        relus = []
        for j in range(4):
            cols = slice(j * D, (j + 1) * D)
            r = jnp.maximum(jnp.dot(h2b, w1_ref[j], preferred_element_type=F32), 0.0)
            ab = (r * r).astype(BF16)
            a_ref[:, cols] = ab
            m = m + jnp.dot(ab, w2_ref[cols, :], preferred_element_type=F32)
            relus.append(r)
        loss, vjp_b = jax.vjp(_head_loss, x1, m, gt2, gpo_ref[...], tgt_ref[...])
        dx1, dm, dgt2, dgpo, _ = vjp_b(jnp.ones((1, 1), F32))
        dmb = dm.astype(BF16)
        dm_ref[...] = dmb
        dh2 = jnp.zeros((TN, D), F32)
        for j in range(4):
            cols = slice(j * D, (j + 1) * D)
            da = lax.dot_general(dmb, w2_ref[cols, :], (((1,), (1,)), ((), ())), preferred_element_type=F32)
            dub = (da * (2.0 * relus[j])).astype(BF16)
            du_ref[:, cols] = dub
            dh2 = dh2 + lax.dot_general(dub, w1_ref[j], (((1,), (1,)), ((), ())), preferred_element_type=F32)
        dx, dmix, dgt1, dsc2, dsh2, dgpm, dgpl = vjp_a((dx1, dh2))
        dx_ref[...] = dx
        dmixb = dmix.astype(BF16)
        dmix_ref[...] = dmixb
        dmi_ref[...] = lax.dot_general(dmixb, wo_ref[...], (((1,), (1,)), ((), ())),
                                       preferred_element_type=F32).astype(BF16)

        @pl.when(t == 0)
        def _():
            dmod_ref[...] = jnp.zeros_like(dmod_ref)

        @pl.when((t == 0) & (b == 0))
        def _():
            dg_ref[...] = jnp.zeros_like(dg_ref)
            loss_ref[...] = jnp.zeros_like(loss_ref)

        dmod_ref[2:3, :] += dgt1
        dmod_ref[3:4, :] += dsh2
        dmod_ref[4:5, :] += dsc2
        dmod_ref[5:6, :] += dgt2
        dg_ref[0:1, :] += dgpm
        dg_ref[1:2, :] += dgpl
        dg_ref[2:3, :] += dgpo
        loss_ref[...] += jnp.broadcast_to(loss, loss_ref.shape)

    tok = lambda b, t: (b, t, 0)
    const = lambda b, t: (0, 0)
    vec = pl.BlockSpec((1, D), const)
    return pl.pallas_call(
        body, name="tail_fwd_bwd", grid=(nb, SEQ // TN),
        in_specs=[
            pl.BlockSpec((None, TN, D), tok), pl.BlockSpec((None, TN, D), tok), pl.BlockSpec((None, TN, D), tok),
            pl.BlockSpec((None, 6, D), lambda b, t: (b, 0, 0)), vec, vec, vec,
            pl.BlockSpec((D, D), const, pipeline_mode=pl.Buffered(1)),
            pl.BlockSpec((4, D, D), lambda b, t: (0, 0, 0), pipeline_mode=pl.Buffered(1)),
            pl.BlockSpec((DFF, D), const, pipeline_mode=pl.Buffered(1)),
        ],
        out_specs=[
            pl.BlockSpec((None, TN, D), tok), pl.BlockSpec((None, TN, D), tok), pl.BlockSpec((None, TN, D), tok),
            pl.BlockSpec((None, TN, DFF), tok), pl.BlockSpec((None, TN, DFF), tok), pl.BlockSpec((None, TN, D), tok),
            pl.BlockSpec((None, TN, D), tok),
            pl.BlockSpec((None, 6, D), lambda b, t: (b, 0, 0)),
            pl.BlockSpec((8, D), const), pl.BlockSpec((8, 128), const),
        ],
        out_shape=[
            jax.ShapeDtypeStruct((nb, SEQ, D), F32), jax.ShapeDtypeStruct((nb, SEQ, D), BF16),
            jax.ShapeDtypeStruct((nb, SEQ, D), BF16), jax.ShapeDtypeStruct((nb, SEQ, DFF), BF16),
            jax.ShapeDtypeStruct((nb, SEQ, DFF), BF16), jax.ShapeDtypeStruct((nb, SEQ, D), BF16),
            jax.ShapeDtypeStruct((nb, SEQ, D), BF16),
            jax.ShapeDtypeStruct((nb, 6, D), F32), jax.ShapeDtypeStruct((8, D), F32),
            jax.ShapeDtypeStruct((8, 128), F32),
        ],
        compiler_params=_cp(("arbitrary", "arbitrary")),
    )(x, mixin, tgt, mod3, g_post_mix, g_pre_mlp, g_post_mlp, wout, w1, w2)


def weight_grad(pairs, name, out_dtype=F32, col_blocks=False, tm=1024, tn=1024, tk=2048):
    m, n = pairs[0][0].shape[1], pairs[0][1].shape[1]
    tn = min(tn, n)
    tks = [min(tk, xa.shape[0]) for xa, _ in pairs]
    steps = [xa.shape[0] // t for (xa, _), t in zip(pairs, tks)]
    total = sum(steps)
    offs = [sum(steps[:i]) for i in range(len(pairs))]

    def body(*refs):
        out_ref, acc = refs[2 * len(pairs)], refs[-1]
        k = pl.program_id(2)

        @pl.when(k == 0)
        def _():
            acc[...] = jnp.zeros_like(acc)

        for i in range(len(pairs)):
            @pl.when((k >= offs[i]) & (k < offs[i] + steps[i]))
            def _(i=i):
                acc[...] += lax.dot_general(refs[2 * i][...], refs[2 * i + 1][...], (((0,), (0,)), ((), ())),
                                            preferred_element_type=F32)

        if out_dtype != F32:
            @pl.when(k == total - 1)
            def _():
                out_ref[...] = acc[...].astype(out_dtype)

    in_specs, args = [], []
    for i, (xa, ya) in enumerate(pairs):
        clamp = lambda k, i=i: jnp.clip(k - offs[i], 0, steps[i] - 1)
        in_specs.append(pl.BlockSpec((tks[i], tm), lambda a, c, k, clamp=clamp: (clamp(k), a)))
        in_specs.append(pl.BlockSpec((tks[i], tn), lambda a, c, k, clamp=clamp: (clamp(k), c)))
        args += [xa, ya]
    if col_blocks:
        out_spec = pl.BlockSpec((None, tm, tn), lambda a, c, k: (c, a, 0))
        out_shape = jax.ShapeDtypeStruct((n // tn, m, tn), out_dtype)
    else:
        out_spec = pl.BlockSpec((tm, tn), lambda a, c, k: (a, c))
        out_shape = jax.ShapeDtypeStruct((m, n), out_dtype)
    return pl.pallas_call(
        body, name=name, grid=(m // tm, n // tn, total), in_specs=in_specs, out_specs=out_spec, out_shape=out_shape,
        scratch_shapes=[] if out_dtype == F32 else [pltpu.VMEM((tm, tn), F32)],
        compiler_params=_cp(("arbitrary", "arbitrary", "arbitrary")),
    )(*args)


def _perm_block(t):
    return 4 * (t % 4) + t // 4 if t < 16 else 16 + 3 * ((t - 16) % 4) + (t - 16) // 4


def _is_rope_block(p):
    return p < 16 and p % 4 < 2


def unpack_w_in(blocks):
    def body(i_ref, o_ref):
        for t in range(28):
            p = _perm_block(t)
            blk = i_ref[t // 7, :, (t % 7) * 128:(t % 7 + 1) * 128]
            if _is_rope_block(p):
                blk = _pair_order(blk.astype(F32)).astype(BF16)
            o_ref[:, p * 128:(p + 1) * 128] = blk

    return pl.pallas_call(
        body, name="unpack_w_in", grid=(2,),
        in_specs=[pl.BlockSpec((4, D // 2, 896), lambda i: (0, i, 0))],
        out_specs=pl.BlockSpec((D // 2, IN_W), lambda i: (i, 0)),
        out_shape=jax.ShapeDtypeStruct((D, IN_W), BF16),
    )(blocks)


def pack_w_in(dw):
    def body(i_ref, o_ref):
        for t in range(28):
            p = _perm_block(t)
            blk = i_ref[:, p * 128:(p + 1) * 128]
            if _is_rope_block(p):
                blk = _pair_order(blk)
            o_ref[t // 7, :, (t % 7) * 128:(t % 7 + 1) * 128] = blk.astype(BF16)

    return pl.pallas_call(
        body, name="pack_w_in", grid=(4,),
        in_specs=[pl.BlockSpec((D // 4, IN_W), lambda i: (i, 0))],
        out_specs=pl.BlockSpec((4, D // 4, 896), lambda i: (0, i, 0)),
        out_shape=jax.ShapeDtypeStruct((4, D, 896), BF16),
    )(dw)


def _place():
    return lax.axis_index("x"), lax.axis_index("y"), lax.axis_index("c")


class Hosted:
    def __init__(self, args, out_shape, scratch, start, finish, relay=None):
        self.args, self.out_shape, self.scratch, self.start, self.finish = args, out_shape, scratch, start, finish
        self.relay = relay or (lambda ins, outs, sems: None)

    def specs(self):
        hbm = pl.BlockSpec(memory_space=pl.ANY)
        return [hbm] * len(self.args), [hbm] * len(self.out_shape)

    def split(self, refs, n_in, n_out):
        a, b = len(self.args), len(self.out_shape)
        cuts = [n_in, n_in + a, n_in + a + n_out, n_in + a + n_out + b, len(refs) - len(self.scratch)]
        parts = [refs[i:j] for i, j in zip([0] + cuts, cuts + [len(refs)])]
        return parts[0], parts[1], parts[2], parts[3], parts[4], parts[5]


def no_exchange():
    return Hosted([], [], [], lambda *a: None, lambda *a: None)


def run_hosted(hosted, name):
    def body(*refs):
        _, ins, _, outs, _, sems = hosted.split(refs, 0, 0)
        hosted.start(ins, outs, sems)
        hosted.relay(ins, outs, sems)
        hosted.finish(ins, outs, sems)

    in_specs, out_specs = hosted.specs()
    return pl.pallas_call(body, name=name, in_specs=in_specs, out_specs=out_specs, out_shape=hosted.out_shape,
                          scratch_shapes=hosted.scratch)(*hosted.args)


def gather8(blocks):
    na = len(blocks)

    def copies(ins, outs, sems):
        send_sems, recv_sems, local_sem = sems
        x, y, c = _place()
        me, sibling = (x, y, c), (x, y, 1 - c)
        chips = [(1 - x, y), (x, 1 - y), (1 - x, 1 - y)]

        def slot(o_ref, px, py, pc):
            return o_ref.at[4 * px + 2 * py + pc]

        def copy(a, k, block, to, src=None):
            return pltpu.make_async_remote_copy(
                src_ref=slot(outs[a], *block) if src is None else src, dst_ref=slot(outs[a], *block),
                send_sem=send_sems.at[a, k], recv_sem=recv_sems.at[a, k], device_id=to, device_id_type=MESH)

        mine = [pltpu.make_async_copy(ins[a], slot(outs[a], *me), local_sem.at[a]) for a in range(na)]
        first = []
        for a in range(na):
            first.append(copy(a, 0, me, sibling, src=ins[a]))
            first += [copy(a, 1 + j, me, (*chip, c), src=ins[a]) for j, chip in enumerate(chips)]
        return copy, mine, first, me, sibling, chips, c

    def start(ins, outs, sems):
        _, mine, first, *_ = copies(ins, outs, sems)
        for cp in mine + first:
            cp.start()

    def relay(ins, outs, sems):
        copy, _, _, me, sibling, chips, c = copies(ins, outs, sems)
        for j, chip in enumerate(chips):
            for a in range(na):
                copy(a, 1 + j, (*chip, c), me).wait_recv()
                copy(a, 4 + j, (*chip, c), sibling).start()

    def finish(ins, outs, sems):
        copy, mine, first, me, sibling, chips, c = copies(ins, outs, sems)
        passed = [copy(a, 4 + j, (*chip, c), sibling) for j, chip in enumerate(chips) for a in range(na)]
        for a in range(na):
            copy(a, 0, sibling, me).wait_recv()
            for j, chip in enumerate(chips):
                copy(a, 4 + j, (*chip, 1 - c), me).wait_recv()
        for cp in first + passed:
            cp.wait_send()
        for cp in mine:
            cp.wait()

    return Hosted(list(blocks), [jax.ShapeDtypeStruct((8,) + b.shape, b.dtype) for b in blocks],
                  [pltpu.SemaphoreType.DMA((na, 7)), pltpu.SemaphoreType.DMA((na, 7)), pltpu.SemaphoreType.DMA((na,))],
                  start, finish, relay)


def chips3(arrays):
    na = len(arrays)

    def copies(ins, outs, sems):
        send_sems, recv_sems = sems
        x, y, c = _place()
        return [pltpu.make_async_remote_copy(
            src_ref=ins[a].at[2 * px + py], dst_ref=outs[a].at[k], send_sem=send_sems.at[a, k],
            recv_sem=recv_sems.at[a, k], device_id=(px, py, c), device_id_type=MESH)
            for a in range(na) for k, (px, py) in enumerate([(1 - x, y), (x, 1 - y), (1 - x, 1 - y)])]

    def start(ins, outs, sems):
        for cp in copies(ins, outs, sems):
            cp.start()

    def finish(ins, outs, sems):
        for cp in copies(ins, outs, sems):
            cp.wait()

    return Hosted(list(arrays), [jax.ShapeDtypeStruct((3,) + a.shape[1:], a.dtype) for a in arrays],
                  [pltpu.SemaphoreType.DMA((na, 3)), pltpu.SemaphoreType.DMA((na, 3))], start, finish)


def siblings(arrays):
    na = len(arrays)

    def copies(ins, outs, sems):
        send_sems, recv_sems = sems
        x, y, c = _place()
        return [pltpu.make_async_remote_copy(
            src_ref=ins[a], dst_ref=outs[a], send_sem=send_sems.at[a], recv_sem=recv_sems.at[a],
            device_id=(x, y, 1 - c), device_id_type=MESH) for a in range(na)]

    def start(ins, outs, sems):
        for cp in copies(ins, outs, sems):
            cp.start()

    def finish(ins, outs, sems):
        for cp in copies(ins, outs, sems):
            cp.wait()

    return Hosted(list(arrays), [jax.ShapeDtypeStruct(a.shape, a.dtype) for a in arrays],
                  [pltpu.SemaphoreType.DMA((na,)), pltpu.SemaphoreType.DMA((na,))], start, finish)


def both(first, second):
    na, no, ns = len(first.args), len(first.out_shape), len(first.scratch)

    def start(ins, outs, sems):
        first.start(ins[:na], outs[:no], sems[:ns])
        second.start(ins[na:], outs[no:], sems[ns:])

    def relay(ins, outs, sems):
        first.relay(ins[:na], outs[:no], sems[:ns])
        second.relay(ins[na:], outs[no:], sems[ns:])

    def finish(ins, outs, sems):
        first.finish(ins[:na], outs[:no], sems[:ns])
        second.finish(ins[na:], outs[no:], sems[ns:])

    return Hosted(first.args + second.args, first.out_shape + second.out_shape, first.scratch + second.scratch,
                  start, finish, relay)


def siblings4(arrays):
    na = len(arrays)

    def copies(ins, outs, sems):
        send_sems, recv_sems = sems
        x, y, c = _place()
        return [pltpu.make_async_remote_copy(
            src_ref=ins[a].at[2 * j + 1 - c], dst_ref=outs[a].at[j],
            send_sem=send_sems.at[a, j], recv_sem=recv_sems.at[a, j],
            device_id=(x, y, 1 - c), device_id_type=MESH) for a in range(na) for j in range(4)]

    def start(ins, outs, sems):
        for cp in copies(ins, outs, sems):
            cp.start()

    def finish(ins, outs, sems):
        for cp in copies(ins, outs, sems):
            cp.wait()

    return Hosted(list(arrays), [jax.ShapeDtypeStruct((4,) + a.shape[1:], a.dtype) for a in arrays],
                  [pltpu.SemaphoreType.DMA((na, 4)), pltpu.SemaphoreType.DMA((na, 4))], start, finish)


def _row_tile(r):
    for cand in (512, 256, 128, 64, 32, 16, 8):
        if r % cand == 0:
            return cand
    return r


def chip_partial(place, g8s, landed4s, name):
    n = len(g8s)

    def body(place_ref, *refs):
        del place_ref
        for g_ref, l_ref, o_ref in zip(refs[:n], refs[n:2 * n], refs[2 * n:]):
            o_ref[...] = (g_ref[...].astype(F32) + l_ref[...].astype(F32)).astype(BF16)

    own = [pl.BlockSpec((None,) + g.shape[1:], lambda j, s: (2 * j + s[0], 0, 0)) for g in g8s]
    plain = [pl.BlockSpec((None,) + g.shape[1:], lambda j, s: (j, 0, 0)) for g in g8s]
    return pl.pallas_call(
        body, name=name,
        grid_spec=pltpu.PrefetchScalarGridSpec(num_scalar_prefetch=1, grid=(4,), in_specs=own + plain, out_specs=plain),
        out_shape=[jax.ShapeDtypeStruct((4,) + g.shape[1:], BF16) for g in g8s],
    )(place, *g8s, *landed4s)


def shard_sum(place, partial4s, landed3s, name):
    n = len(partial4s)

    def body(place_ref, *refs):
        del place_ref
        for p_ref, l_ref, o_ref in zip(refs[:n], refs[n:2 * n], refs[2 * n:]):
            acc = p_ref[...].astype(F32)
            for k in range(3):
                acc = acc + l_ref[k].astype(F32)
            o_ref[...] = acc

    def halves(p, lead):
        r, ccols = p.shape[1:]
        return (lead, r // 2, ccols)

    return pl.pallas_call(
        body, name=name,
        grid_spec=pltpu.PrefetchScalarGridSpec(
            num_scalar_prefetch=1, grid=(2,),
            in_specs=[pl.BlockSpec(halves(p, None), lambda i, s: (s[1], i, 0)) for p in partial4s]
            + [pl.BlockSpec(halves(p, 3), lambda i, s: (0, i, 0)) for p in partial4s],
            out_specs=[pl.BlockSpec(halves(p, None)[1:], lambda i, s: (i, 0)) for p in partial4s]),
        out_shape=[jax.ShapeDtypeStruct(p.shape[1:], F32) for p in partial4s],
    )(place, *partial4s, *landed3s)


def _adamw_math(w, g, m, v):
    m2 = B1 * m + (1.0 - B1) * g
    v2 = B2 * v + (1.0 - B2) * (g * g)
    m_hat = m2 / (1.0 - B1 ** STEP)
    v_hat = v2 / (1.0 - B2 ** STEP)
    return -LR * (m_hat / (jnp.sqrt(v_hat) + AEPS) + WD * w), m2, v2


def adamw_halves(place, w, mine, theirs, m, v, name):
    r, ccols = w.shape
    hr = r // 2
    tr = _row_tile(hr)
    nt = hr // tr

    def body(place_ref, w_ref, a_ref, b_ref, m_ref, v_ref, g_out, d_out, m_out, v_out):
        g = jnp.where(pl.program_id(0) == place_ref[0], a_ref[...], b_ref[...])
        d, m2, v2 = _adamw_math(w_ref[...], g, m_ref[...], v_ref[...])
        g_out[...] = g
        d_out[...] = d
        m_out[...] = m2
        v_out[...] = v2

    full = pl.BlockSpec((tr, ccols), lambda h, i, s: (h * nt + i, 0))
    part = pl.BlockSpec((tr, ccols), lambda h, i, s: (i, 0))
    return pl.pallas_call(
        body, name=name,
        grid_spec=pltpu.PrefetchScalarGridSpec(
            num_scalar_prefetch=1, grid=(2, nt), in_specs=[full, part, part, full, full], out_specs=[full] * 4),
        out_shape=[jax.ShapeDtypeStruct((r, ccols), F32)] * 4,
    )(place, w, mine, theirs, m, v)


def adamw_group(place, halved, plain, hosted, name):
    rows = halved[0][0].shape[0]
    tr = 128
    nt = rows // 2 // tr
    nh, npl = len(halved), len(plain)

    def body(place_ref, *refs):
        own_in, h_in, own_out, h_out, _, h_sems = hosted.split(refs, 5 * nh + 4 * npl, 4 * nh + 3 * npl)
        half = pl.program_id(0)
        grid_step = half * nt + pl.program_id(1)

        @pl.when(grid_step == 0)
        def _():
            hosted.start(h_in, h_out, h_sems)

        for i in range(nh):
            w_ref, a_ref, b_ref, m_ref, v_ref = own_in[5 * i:5 * i + 5]
            g = jnp.where(half == place_ref[0], a_ref[...], b_ref[...])
            res = (g,) + _adamw_math(w_ref[...], g, m_ref[...], v_ref[...])
            for o_ref, r in zip(own_out[4 * i:4 * i + 4], res):
                o_ref[...] = r
        for i in range(npl):
            w_ref, g_ref, m_ref, v_ref = own_in[5 * nh + 4 * i:5 * nh + 4 * i + 4]
            res = _adamw_math(w_ref[...], g_ref[...], m_ref[...], v_ref[...])
            for o_ref, r in zip(own_out[4 * nh + 3 * i:4 * nh + 3 * i + 3], res):
                o_ref[...] = r

        @pl.when(grid_step == 2 * nt - 1)
        def _():
            hosted.relay(h_in, h_out, h_sems)
            hosted.finish(h_in, h_out, h_sems)

    def full(cols):
        return pl.BlockSpec((tr, cols), lambda h, i, s: (h * nt + i, 0))

    def part(cols):
        return pl.BlockSpec((tr, cols), lambda h, i, s: (i, 0))

    in_specs, out_specs, out_shape, args = [], [], [], []
    for w, a, b, m, v in halved:
        cols = w.shape[1]
        in_specs += [full(cols), part(cols), part(cols), full(cols), full(cols)]
        out_specs += [full(cols)] * 4
        out_shape += [jax.ShapeDtypeStruct(w.shape, F32)] * 4
        args += [w, a, b, m, v]
    for w, g, m, v in plain:
        cols = w.shape[1]
        in_specs += [full(cols)] * 4
        out_specs += [full(cols)] * 3
        out_shape += [jax.ShapeDtypeStruct(w.shape, F32)] * 3
        args += [w, g, m, v]
    h_in_specs, h_out_specs = hosted.specs()
    return pl.pallas_call(
        body, name=name,
        grid_spec=pltpu.PrefetchScalarGridSpec(
            num_scalar_prefetch=1, grid=(2, nt), in_specs=in_specs + h_in_specs, out_specs=out_specs + h_out_specs,
            scratch_shapes=hosted.scratch),
        out_shape=out_shape + hosted.out_shape,
        compiler_params=_cp(("arbitrary", "arbitrary")),
    )(place, *args, *hosted.args)


def _silu(x):
    return x * jax.nn.sigmoid(x)


def prologue(c_rows, c_ctx_row, w_ada, b_shard, rpb_flat, half_w_in, late_shards):
    shape = jax.ShapeDtypeStruct
    n_late = len(late_shards)
    half_shapes = [(w.shape[0] // 2, w.shape[1]) for w in late_shards]
    g_w = gather8([half_w_in])
    g_c = gather8([shape((8, D), F32)])
    g_m = gather8([shape((32, 1536), F32)])

    def body(*refs):
        c_ref, cc_ref, w_ref, b_ref, flat_ref, hw_ref = refs[:6]
        late_refs = refs[6:6 + n_late]
        cin_ref, mg_ref, gw_ref, bias_ref, cos_ref, sin_ref = refs[6 + n_late:12 + n_late]
        rest = refs[12 + n_late:]
        half_refs, (cg_s, ms_s, bias_s) = rest[:n_late], rest[n_late:n_late + 3]
        stage, (load_sem, bias_sem), sems = rest[n_late + 3:2 * n_late + 3], rest[2 * n_late + 3:2 * n_late + 5], \
            rest[2 * n_late + 5:]
        sw, sc, sm = sems[0:3], sems[3:6], sems[6:9]
        core = lax.axis_index("c")
        g_c.start([c_ref], [cg_s], sc)
        g_w.start([hw_ref], [gw_ref], sw)
        loads = [pltpu.make_async_copy(late_refs[a].at[pl.ds(core * half_shapes[a][0], half_shapes[a][0]), :],
                                       stage[a], load_sem.at[a]) for a in range(n_late)]
        for cp in loads:
            cp.start()
        g_c.relay([c_ref], [cg_s], sc)
        g_c.finish([c_ref], [cg_s], sc)
        cin_ref[...] = jnp.zeros_like(cin_ref)
        for dev in range(8):
            cin_ref[2 * dev:2 * dev + 2, :] = cg_s[dev, 0:2, :]
        cin_ref[16:17, :] = cc_ref[...]
        ms_s[...] = _nn(_silu(cin_ref[...]), w_ref[...]) + b_ref[...]
        g_m.start([ms_s], [mg_ref], sm)
        for a, cp in enumerate(loads):
            cp.wait()
            half_refs[a][...] = stage[a][...].astype(BF16)
        cos_ref[...], sin_ref[...] = _rope_tables()
        stores = []
        for pair in range(NPAIR):
            if pair >= 2:
                stores[pair - 2].wait()
            _na_bias_pair(flat_ref.at[pair], bias_s.at[pair % 2])
            stores.append(pltpu.make_async_copy(bias_s.at[pair % 2], bias_ref.at[pair], bias_sem.at[pair % 2]))
            stores[pair].start()
        for cp in stores[-2:]:
            cp.wait()
        g_w.relay([hw_ref], [gw_ref], sw)
        g_m.relay([ms_s], [mg_ref], sm)
        g_w.finish([hw_ref], [gw_ref], sw)
        g_m.finish([ms_s], [mg_ref], sm)

    vmem = pl.BlockSpec(memory_space=pltpu.VMEM)
    hbm = pl.BlockSpec(memory_space=pl.ANY)
    return pl.pallas_call(
        body, name="prologue", in_specs=[vmem, vmem, vmem, vmem, vmem, hbm] + [hbm] * n_late,
        out_specs=[vmem, vmem, hbm, hbm, vmem, vmem] + [vmem] * n_late,
        out_shape=[shape((32, D), F32), shape((8, 32, 1536), F32)] + g_w.out_shape
        + [shape((NPAIR,) + NA_BIAS_SHAPE, F32)] + [shape((SEQ, RD), F32)] * 2 + [shape(s, BF16) for s in half_shapes],
        scratch_shapes=[pltpu.VMEM((8, 8, D), F32), pltpu.VMEM((32, 1536), F32), pltpu.VMEM((2,) + NA_BIAS_SHAPE, F32)]
        + [pltpu.VMEM(s, F32) for s in half_shapes]
        + [pltpu.SemaphoreType.DMA((n_late,)), pltpu.SemaphoreType.DMA((2,))]
        + g_w.scratch + g_c.scratch + g_m.scratch,
        compiler_params=_cp(),
    )(c_rows, c_ctx_row, w_ada, b_shard, rpb_flat, half_w_in, *late_shards)


def ada_grads(cin, gb, gc, w_ada):
    def body(c_ref, gb_ref, gc_ref, w_ref, gw_ref, pc_ref):
        ctx_tot = jnp.sum(gc_ref[...], axis=0, keepdims=True)
        rows = lax.broadcasted_iota(jnp.int32, (16, 512), 0)
        dm = jnp.concatenate([gb_ref[...], jnp.where(rows == 0, ctx_tot, 0.0)], axis=0)
        gw_ref[...] = _tn(_silu(c_ref[...]), dm)
        rows8 = lax.broadcasted_iota(jnp.int32, (8, 512), 0)
        part = _nt(jnp.where(rows8 == 0, ctx_tot, 0.0), w_ref[...])

        @pl.when(pl.program_id(0) == 0)
        def _():
            pc_ref[...] = jnp.zeros_like(pc_ref)

        pc_ref[...] += part

    return pl.pallas_call(
        body, name="ada_grads", grid=(3,),
        in_specs=[pl.BlockSpec((32, D), lambda j: (0, 0)), pl.BlockSpec((16, 512), lambda j: (0, j)),
                  pl.BlockSpec((8, 512), lambda j: (0, j)), pl.BlockSpec((D, 512), lambda j: (0, j))],
        out_specs=[pl.BlockSpec((D, 512), lambda j: (0, j)), pl.BlockSpec((8, D), lambda j: (0, 0))],
        out_shape=[jax.ShapeDtypeStruct((D, 1536), F32), jax.ShapeDtypeStruct((8, D), F32)],
    )(cin, gb, gc, w_ada)


SMALL_SUM_ROWS = 15


def small_update(gsm, gbf, gcf, pcg, params):
    n = len(params)

    def body(*refs):
        gsm_ref, gbf_ref, gcf_ref, pcg_ref = refs[:4]
        wmv, outs, loss_out = refs[4:4 + 3 * n], refs[4 + 3 * n:4 + 7 * n], refs[-1]
        acc = gsm_ref[0]
        for dev in range(1, 8):
            acc = acc + gsm_ref[dev]
        c_ctx = wmv[0][...]
        sg = jax.nn.sigmoid(c_ctx)
        dsilu = pcg_ref[0:1, :] + pcg_ref[2:3, :] + pcg_ref[4:5, :] + pcg_ref[6:7, :]
        lane = lax.broadcasted_iota(jnp.int32, (1, D), 1)
        last = acc[14:15, :]
        grads = [
            dsilu * (sg * (1.0 + c_ctx * (1.0 - sg))),
            jnp.sum(gbf_ref[...], axis=0, keepdims=True) + jnp.sum(gcf_ref[...], axis=0, keepdims=True),
            acc[0:1, :] + acc[1:2, :], acc[2:3, :], acc[3:4, :], acc[4:5, :],
            acc[5:6, 0:512], acc[6:14, :], jnp.where(lane < 8, last, 0.0),
        ]
        loss_out[...] = jnp.broadcast_to(jnp.sum(jnp.where(lane == 8, last, 0.0), axis=1, keepdims=True), (8, 128))
        for i, g in enumerate(grads):
            d, m2, v2 = _adamw_math(wmv[3 * i][...], g, wmv[3 * i + 1][...], wmv[3 * i + 2][...])
            outs[4 * i][...] = g
            outs[4 * i + 1][...] = d
            outs[4 * i + 2][...] = m2
            outs[4 * i + 3][...] = v2

    flat = [a for wmv in params for a in wmv]
    out_shape = [jax.ShapeDtypeStruct(w.shape, F32) for w, _, _ in params for _ in range(4)]
    return pl.pallas_call(
        body, name="small_update", out_shape=out_shape + [jax.ShapeDtypeStruct((8, 128), F32)],
    )(gsm, gbf, gcf, pcg, *flat)


def _pad_row(v, rows):
    flat = v.reshape(-1)
    return jnp.pad(flat, (0, rows * D - flat.shape[0])).reshape(rows, D)


def local_step(x, ctx, tgt, mod3, rope, bias, g_pre_mix, g_post_mix, g_pre_mlp, g_post_mlp, ret_decay, ret_gn,
               wperm, late_weights, early_grads):
    nb = x.shape[0]
    tokens = nb * SEQ
    cos, sin = rope
    rd = ret_decay.T.reshape(RH, 2, 1)
    gn = ret_gn.reshape(RH, 1, RD)
    h, pret, pna = premix_proj(x, mod3, g_pre_mix, wperm, False, "premix_proj")
    hc, pretc, pnac = premix_proj(ctx, mod3, g_pre_mix, wperm, True, "premix_proj_ctx")
    o_all, mixin, gw_out = retention_fwd(pret, pretc, rd, gn, cos, sin, late_weights(0))
    mixin, gw1, gw2 = na_fwd(pna, pnac, bias, mixin, late_weights(1))
    dx_tail, dmix, h2, du, act, dm, dmixin, dmod_t, dg_t, loss_t = tail_fwd_bwd(
        x, mixin, tgt, mod3, g_post_mix, g_pre_mlp, g_post_mlp, gw_out.reshape(D, D), gw1.reshape(4, D, D),
        gw2.reshape(DFF, D))
    dw_out = weight_grad([(mixin.reshape(tokens, D), dmix.reshape(tokens, D))], "grad_w_out", BF16)
    dw1 = weight_grad([(h2.reshape(tokens, D), du.reshape(tokens, DFF))], "grad_w_mlp1", BF16, col_blocks=True)
    dw2 = weight_grad([(act.reshape(tokens, DFF), dm.reshape(tokens, D))], "grad_w_mlp2", BF16)
    dproj, dprojc, drd, dgn, *landed = retention_bwd(pret, pretc, o_all, dmixin, rd, gn, cos, sin,
                                                     early_grads[0](dw_out, dw1, dw2))
    dproj, dprojc, dpat, *early = na_bwd(pna, pnac, bias, dmixin, dproj, dprojc, early_grads[1](landed))
    dw_in = weight_grad([(h.reshape(tokens, D), dproj.reshape(tokens, IN_W)),
                         (hc.reshape(nb * LC, D), dprojc.reshape(nb * LC, IN_W))], "grad_w_in", tn=IN_W // 2, tk=1024)
    dmod_c, dg_c, *late = premix_bwd(ctx, mod3, g_pre_mix, wperm, dprojc, None, early_grads[2](dw_in), "premix_bwd_ctx")
    grad_x, dmod_a, dg_a, *late = premix_bwd(x, mod3, g_pre_mix, wperm, dproj, dx_tail, early_grads[3](late),
                                             "premix_bwd")
    dmod = jnp.concatenate([jnp.concatenate([dmod_a[:, 0:2], dmod_t[:, 2:6]], axis=1), dmod_c], axis=0)
    last = jnp.pad(jnp.concatenate([drd[:, :, 0].T.reshape(8), loss_t[0, 0:1]]), (0, D - 9)).reshape(1, D)
    small = jnp.concatenate([dg_a[0:1], dg_c[0:1], dg_t[0:3], _pad_row(dgn, 1), dpat.reshape(8, D), last], axis=0)
    return grad_x, late, early, dmod, small


def kernel(x, c, ctx, c_ctx, w_ada, b_ada, g_pre_mix, g_post_mix, g_pre_mlp, g_post_mlp, w_in, ret_decay, ret_gn, na_rpb, w_out, w_mlp1, w_mlp2, loss_target, m_c_ctx, m_w_ada, m_b_ada, m_g_pre_mix, m_g_post_mix, m_g_pre_mlp, m_g_post_mlp, m_w_in, m_ret_decay, m_ret_gn, m_na_rpb, m_w_out, m_w_mlp1, m_w_mlp2, v_c_ctx, v_w_ada, v_b_ada, v_g_pre_mix, v_g_post_mix, v_g_pre_mlp, v_g_post_mlp, v_w_in, v_ret_decay, v_ret_gn, v_na_rpb, v_w_out, v_w_mlp1, v_w_mlp2):
    px, py, pc = _place()
    dev = 4 * px + 2 * py + pc
    chip = 2 * px + py

    half_w_in = lax.dynamic_slice_in_dim(w_in[0], pc * (D // 2), D // 2, 0).astype(BF16)
    cin, mg, gw_in, bias, cos, sin, *late_halves = prologue(
        jnp.pad(c, ((0, 6), (0, 0))), c_ctx[None], w_ada[0], lax.dynamic_slice_in_dim(b_ada, chip * 1536, 1536, 1),
        _rpb_flat(na_rpb[0]), half_w_in, [w_out[0], w_mlp1[0], w_mlp2[0]])
    halves = [half_w_in] + late_halves
    wperm = unpack_w_in(gw_in.reshape(4, D, 896))
    mod_all = jnp.concatenate([mg[0], mg[2], mg[4], mg[6]], axis=1)
    mod3 = (jnp.pad(lax.dynamic_slice_in_dim(mod_all, 2 * dev, 2, 0), ((0, 1), (0, 0)))
            + jnp.pad(mod_all[16:17], ((2, 0), (0, 0)))).reshape(3, 6, D)

    place = jnp.stack([pc, chip]).astype(jnp.int32)

    early_names = ["w_out", "w_mlp1", "w_mlp2"]
    early_g8, early_partial = [], []

    def early_a(dw_out, dw1, dw2):
        early_g8[:] = [dw_out.reshape(8, 128, D), dw1.reshape(8, 512, D), dw2.reshape(8, 512, D)]
        return siblings4(early_g8)

    def early_b(landed):
        early_partial[:] = chip_partial(place, early_g8, landed, "rs_chip_sum_early")
        return chips3(early_partial)

    late_partial = []

    late_g8 = []

    def late_c(dw_in):
        late_g8[:] = [pack_w_in(dw_in).reshape(8, 512, 896)]
        return siblings4(late_g8)

    def late_d(landed):
        late_partial[:] = chip_partial(place, late_g8, landed, "rs_chip_sum_w_in")
        return chips3(late_partial)

    grad_x, (landed3_in,), early_landed, dmod, small = local_step(
        x, ctx, loss_target, mod3, (cos, sin), bias, g_pre_mix, g_post_mix, g_pre_mlp, g_post_mlp, ret_decay[0], ret_gn,
        wperm, lambda k: gather8(halves[1:2] if k == 0 else halves[2:4]), (early_a, early_b, late_c, late_d))
    early_mine = shard_sum(place, early_partial, early_landed, "rs_shard_sum_early")

    pay = jnp.concatenate([dmod.reshape(18, D), small, jnp.zeros((40 - 18 - SMALL_SUM_ROWS, D), F32)], axis=0)
    *early_theirs, gs = run_hosted(both(siblings(early_mine), gather8([pay])), "rs_halves_early_gather_small")
    gbf = gs[:, 0:12].reshape(16, 6 * D)
    gcf = gs[:, 12:18].reshape(8, 6 * D)
    gw_ada, pc_part = ada_grads(cin, lax.dynamic_slice_in_dim(gbf, chip * 1536, 1536, 1),
                                lax.dynamic_slice_in_dim(gcf, chip * 1536, 1536, 1), w_ada[0])
    (mine_in,) = shard_sum(place, late_partial, [landed3_in], "rs_shard_sum_w_in")
    theirs_in, pcg = run_hosted(both(siblings([mine_in]), gather8([pc_part])), "rs_halves_w_in_gather_c_ctx")

    grouped = adamw_group(
        place,
        [(w_mlp1[0], early_mine[1], early_theirs[1], m_w_mlp1[0], v_w_mlp1[0]),
         (w_mlp2[0], early_mine[2], early_theirs[2], m_w_mlp2[0], v_w_mlp2[0])],
        [(w_ada[0], gw_ada, m_w_ada[0], v_w_ada[0])], no_exchange(), "adamw_group")
    d_ada, m_ada, v_ada = grouped[8:11]
    big = [
        [r[None] for r in adamw_halves(place, w_in[0], mine_in, theirs_in, m_w_in[0], v_w_in[0], "adamw_w_in")],
        [r[None] for r in adamw_halves(place, w_out[0], early_mine[0], early_theirs[0], m_w_out[0], v_w_out[0],
                                       "adamw_w_out")],
        [r[None] for r in grouped[0:4]], [r[None] for r in grouped[4:8]],
    ]

    def rpb_rows(t):
        return _rpb_flat(t[0]).reshape(8, D)

    def decay_row(t):
        return jnp.pad(t.reshape(1, 8), ((0, 0), (0, D - 8)))

    views = [lambda t: t.reshape(1, D), lambda t: t, lambda t: t, lambda t: t, lambda t: t, lambda t: t, lambda t: t,
             rpb_rows, decay_row]
    back = [lambda t: t.reshape(D), lambda t: t, lambda t: t, lambda t: t, lambda t: t, lambda t: t, lambda t: t,
            lambda t: _rpb_flat_t(t)[None], lambda t: t[:, 0:8].reshape(1, 2, 4)]
    small_w = (c_ctx, b_ada, g_pre_mix, g_post_mix, g_pre_mlp, g_post_mlp, ret_gn, na_rpb, ret_decay)
    small_m = (m_c_ctx, m_b_ada, m_g_pre_mix, m_g_post_mix, m_g_pre_mlp, m_g_post_mlp, m_ret_gn, m_na_rpb, m_ret_decay)
    small_v = (v_c_ctx, v_b_ada, v_g_pre_mix, v_g_post_mix, v_g_pre_mlp, v_g_post_mlp, v_ret_gn, v_na_rpb, v_ret_decay)
    *res, loss8 = small_update(gs[:, 18:18 + SMALL_SUM_ROWS], gbf, gcf, pcg[:, 0],
                               [(f(w), f(m), f(v)) for f, w, m, v in zip(views, small_w, small_m, small_v)])

    def leaves(ada, idx):
        s_c, s_b, s_g1, s_g2, s_g3, s_g4, s_gn, s_rpb, s_rd = [back[i](res[4 * i + idx]) for i in range(9)]
        return [s_c, ada[None], s_b, s_g1, s_g2, s_g3, s_g4, big[0][idx], s_rd, s_gn, s_rpb,
                big[1][idx], big[2][idx], big[3][idx]]

    return (loss8[0, 0], grad_x, *leaves(gw_ada, 0), *leaves(d_ada, 1), *leaves(m_ada, 2), *leaves(v_ada, 3))
```

```python
import functools
import math

import jax
import jax.numpy as jnp
from jax import lax
from jax.experimental import pallas as pl
from jax.experimental.pallas import tpu as pltpu

F32, BF16 = jnp.float32, jnp.bfloat16
D = 1024
SEQ = 2048
LC = 256
GW = 64
RH, RD, CH = 4, 128, 128
NPAIR = 4
IN_W = 3584
RET_W = 2048
DFF = 4096
EPS = 1e-6
NEG = -1e30
TN = 256
NCH = SEQ // CH
LR, B1, B2, AEPS, WD, STEP = 0.001, 0.9, 0.999, 1e-08, 0.01, 10
MESH = pl.DeviceIdType.MESH
VMEM_LIMIT = 56 * 1024 * 1024


def _cp(sem=None):
    return pltpu.CompilerParams(dimension_semantics=sem, vmem_limit_bytes=VMEM_LIMIT)


def _nn(a, b):
    return jnp.dot(a.astype(BF16), b.astype(BF16), preferred_element_type=F32)


def _nt(a, b):
    return lax.dot_general(a.astype(BF16), b.astype(BF16), (((1,), (1,)), ((), ())), preferred_element_type=F32)


def _tn(a, b):
    return lax.dot_general(a.astype(BF16), b.astype(BF16), (((0,), (0,)), ((), ())), preferred_element_type=F32)


@jax.custom_vjp
def mm_tn(a, b):
    return _tn(a, b)


mm_tn.defvjp(lambda a, b: (_tn(a, b), (a, b)), lambda r, g: (_nt(r[1], g), _nn(r[0], g)))


def _rms(x):
    return x * lax.rsqrt(jnp.mean(x * x, axis=-1, keepdims=True) + EPS)


def _rms_mod(x, g, sc, sh):
    return (_rms(x) * g) * (1.0 + sc) + sh


def _post_mix(x, mix, gt1, sc2, sh2, g_post_mix, g_pre_mlp):
    x1 = x + gt1 * (_rms(mix) * g_post_mix)
    return x1, _rms_mod(x1, g_pre_mlp, sc2, sh2)


def _head_loss(x1, m, gt2, g_post_mlp, tgt):
    err = x1 + gt2 * (_rms(m) * g_post_mlp) - tgt
    return 0.5 * jnp.sum(jnp.mean(err * err, axis=-1, keepdims=True), axis=0, keepdims=True)


def _ln_gate(o, g, w):
    mu = jnp.mean(o, axis=-1, keepdims=True)
    var = jnp.mean(jnp.square(o - mu), axis=-1, keepdims=True)
    y = (o - mu) * lax.rsqrt(var + EPS)
    return (y * w) * (g * jax.nn.sigmoid(g))


def _pair_order(x):
    lane = lax.broadcasted_iota(jnp.int32, x.shape, 1)
    return jnp.where((lane >= 32) & (lane < 64), pltpu.roll(x, 96, 1),
                     jnp.where((lane >= 64) & (lane < 96), pltpu.roll(x, 32, 1), x))


def _rope(x, cos, sin):
    return x * cos + pltpu.roll(x, 64, 1) * sin


def _rope_t(g, cos, sin):
    return g * cos + pltpu.roll(g * sin, 64, 1)


def _rope_tables():
    tok = lax.broadcasted_iota(jnp.int32, (SEQ, RD), 0)
    lane = lax.broadcasted_iota(jnp.int32, (SEQ, RD), 1)
    pos = jnp.where((lane & 32) == 0, tok >> 6, tok & (GW - 1)).astype(F32)
    ang = pos * jnp.exp((lane & 31).astype(F32) * (-math.log(10000.0) / 32))
    return jnp.cos(ang), jnp.where(lane < 64, -jnp.sin(ang), jnp.sin(ang))


def _chunk_loop(n, body, init, k=4):
    def several(t, carry):
        for i in range(k):
            carry = body(k * t + i, carry)
        return carry

    return lax.fori_loop(0, n // k, several, init)


def _fiota(shape, dim):
    return lax.broadcasted_iota(jnp.int32, shape, dim).astype(F32)


def _ret_state(k, v, s, lg, reverse):
    pos = _fiota((CH, 1), 0)
    b_exp = pos if reverse else (CH - 1.0 - pos)
    return jnp.exp(lg * CH) * s + mm_tn(k * jnp.exp(lg * b_exp), v)


class _Decays:
    def __init__(self, lgs):
        i, j, pos = _fiota((CH, CH), 0), _fiota((CH, CH), 1), _fiota((CH, 1), 0)
        diffs = (i - j, j - i)
        keep = (diffs[0] >= 0, diffs[1] > 0)
        mats = [jnp.where(m, jnp.exp(lg * jnp.where(m, d, 0.0)), 0.0) for lg, d, m in zip(lgs, diffs, keep)]
        self.mask = mats[0] + mats[1]
        self.dmask = [mats[0] * diffs[0], mats[1] * diffs[1]]
        a_exp, b_exp = (pos + 1.0, CH - pos), (CH - 1.0 - pos, pos)
        self.a = [jnp.exp(lg * e) for lg, e in zip(lgs, a_exp)]
        self.b = [jnp.exp(lg * e) for lg, e in zip(lgs, b_exp)]
        self.da = [a * e for a, e in zip(self.a, a_exp)]
        self.db = [b * e for b, e in zip(self.b, b_exp)]
        self.g = [jnp.exp(lg * CH) for lg in lgs]


def _both(x, w):
    return jnp.concatenate([x * w[0], x * w[1]], axis=1)


def _total(x):
    return jnp.sum(jnp.sum(x, axis=1, keepdims=True), axis=0, keepdims=True)


def _state_pass(dec, init, k_s, v_of, st_s):
    def step(t, carry):
        out = []
        for d, s in enumerate(carry):
            n = (NCH - 1 - t) if d else t
            sl = pl.ds(pl.multiple_of(n * CH, CH), CH)
            st_s[n, d * RD:(d + 1) * RD, :] = s
            out.append(dec.g[d] * s + _tn(k_s[sl, :] * dec.b[d], v_of(sl)))
        return tuple(out)

    _chunk_loop(NCH, step, tuple(init))


def premix_proj(xin, mod3, g_pre, wperm, is_ctx, name):
    nb, length, _ = xin.shape
    tn = min(2 * TN, length)

    def body(x_ref, mod_ref, g_ref, w_ref, h_ref, pret_ref, pna_ref):
        h = _rms_mod(x_ref[...], g_ref[...], mod_ref[1:2, :], mod_ref[0:1, :])
        hb = h.astype(BF16)
        h_ref[...] = hb
        pret_ref[...] = jnp.dot(hb, w_ref[:, :RET_W], preferred_element_type=F32)
        pna_ref[...] = jnp.dot(hb, w_ref[:, RET_W:], preferred_element_type=F32).astype(BF16)

    return pl.pallas_call(
        body, name=name, grid=(nb, length // tn),
        in_specs=[
            pl.BlockSpec((None, tn, D), lambda b, t: (b, t, 0)),
            pl.BlockSpec((None, 6, D), (lambda b, t: (2, 0, 0)) if is_ctx else (lambda b, t: (b, 0, 0))),
            pl.BlockSpec((1, D), lambda b, t: (0, 0)),
            pl.BlockSpec((D, IN_W), lambda b, t: (0, 0), pipeline_mode=pl.Buffered(1)),
        ],
        out_specs=[
            pl.BlockSpec((None, tn, D), lambda b, t: (b, t, 0)),
            pl.BlockSpec((None, tn, RET_W), lambda b, t: (b, t, 0)),
            pl.BlockSpec((None, tn, IN_W - RET_W), lambda b, t: (b, t, 0)),
        ],
        out_shape=[
            jax.ShapeDtypeStruct((nb, length, D), BF16),
            jax.ShapeDtypeStruct((nb, length, RET_W), F32),
            jax.ShapeDtypeStruct((nb, length, IN_W - RET_W), BF16),
        ],
        compiler_params=_cp(("arbitrary", "arbitrary")),
    )(xin, mod3, g_pre, wperm)


def premix_bwd(xin, mod3, g_pre, wperm, dproj, dx_tail, hosted, name, steps=None, carry=None):
    nb, length, _ = xin.shape
    tn = min(TN, length)
    nt = length // tn
    is_ctx = dx_tail is None
    first, count = steps or (0, nb * nt)
    if carry is not None:
        assert not is_ctx and first % nt > 0 and first % nt + count <= nt
    n_in = (5 if is_ctx else 6) + (0 if carry is None else 3)

    def body(*refs):
        own_in, h_in, own_out, h_out, _, h_sems = hosted.split(refs, n_in, 2 if is_ctx else 3)
        if is_ctx:
            (x_ref, mod_ref, g_ref, w_ref, dp_ref), (dmod_ref, dg_ref) = own_in, own_out
        else:
            (x_ref, mod_ref, g_ref, w_ref, dp_ref, dxt_ref, *carried), (dx_ref, dmod_ref, dg_ref) = own_in, own_out
        grid_step = pl.program_id(0)
        b, t = (first + grid_step) // nt, (first + grid_step) % nt

        @pl.when(grid_step == 0)
        def _():
            hosted.start(h_in, h_out, h_sems)

        @pl.when(grid_step == count - 1)
        def _():
            hosted.finish(h_in, h_out, h_sems)

        dh = lax.dot_general(dp_ref[...], w_ref[...], (((1,), (1,)), ((), ())), preferred_element_type=F32)
        _, vjp = jax.vjp(_rms_mod, x_ref[...], g_ref[...], mod_ref[1:2, :], mod_ref[0:1, :])
        dx, dg, dsc, dsh = vjp(dh)
        if not is_ctx:
            dx_ref[...] = dx + dxt_ref[...]

        if carry is None:
            @pl.when((t == 0) & ((b == 0) if is_ctx else True))
            def _():
                dmod_ref[...] = jnp.zeros_like(dmod_ref)

            @pl.when((t == 0) & (b == 0))
            def _():
                dg_ref[...] = jnp.zeros_like(dg_ref)
        else:
            @pl.when(grid_step == 0)
            def _():
                dmod_ref[...] = carried[1][...]
                dg_ref[...] = carried[2][...]

        dmod_ref[0:1, :] += dsh
        dmod_ref[1:2, :] += dsc
        dg_ref[0:1, :] += dg

    tok = lambda i: ((first + i) // nt, (first + i) % nt, 0)
    row = (lambda i: (2, 0, 0)) if is_ctx else (lambda i: ((first + i) // nt, 0, 0))
    in_specs = [
        pl.BlockSpec((None, tn, D), tok),
        pl.BlockSpec((None, 6, D), row),
        pl.BlockSpec((1, D), lambda i: (0, 0)),
        pl.BlockSpec((D, IN_W), lambda i: (0, 0), pipeline_mode=pl.Buffered(1)),
        pl.BlockSpec((None, tn, IN_W), tok),
    ]
    args = [xin, mod3, g_pre, wperm, dproj]
    out_specs = [
        pl.BlockSpec((None, 6, D), (lambda i: (0, 0, 0)) if is_ctx else row),
        pl.BlockSpec((8, D), lambda i: (0, 0)),
    ]
    out_shape = [jax.ShapeDtypeStruct((1 if is_ctx else nb, 6, D), F32), jax.ShapeDtypeStruct((8, D), F32)]
    aliases = {}
    if not is_ctx:
        in_specs.append(pl.BlockSpec((None, tn, D), tok))
        args.append(dx_tail)
        out_specs.insert(0, pl.BlockSpec((None, tn, D), tok))
        out_shape.insert(0, jax.ShapeDtypeStruct((nb, length, D), F32))
    if carry is not None:
        in_specs += [pl.BlockSpec(memory_space=pl.ANY), pl.BlockSpec((None, 6, D), row),
                     pl.BlockSpec((8, D), lambda i: (0, 0))]
        args += list(carry)
        aliases = {6: 0, 7: 1}
    h_in_specs, h_out_specs = hosted.specs()
    return pl.pallas_call(
        body, name=name, grid=(count,), in_specs=in_specs + h_in_specs, out_specs=out_specs + h_out_specs,
        out_shape=out_shape + hosted.out_shape, scratch_shapes=hosted.scratch, input_output_aliases=aliases,
        compiler_params=_cp(("arbitrary",)),
    )(*args, *hosted.args)


def _ret_specs(order):
    def im(f):
        return lambda *g: f(*order(*g))
    return dict(
        pret=pl.BlockSpec((None, SEQ, 512), im(lambda b, h: (b, 0, h))),
        pretc=pl.BlockSpec((None, LC, 512), im(lambda b, h: (b, 0, h))),
        rd=pl.BlockSpec((None, 2, 1), im(lambda b, h: (h, 0, 0))),
        gn=pl.BlockSpec((None, 1, RD), im(lambda b, h: (h, 0, 0))),
        tab=pl.BlockSpec((SEQ, RD), im(lambda b, h: (0, 0))),
        head=pl.BlockSpec((None, SEQ, RD), im(lambda b, h: (b, 0, h))),
    )


def retention_fwd(pret, pretc, rd, gn, cos, sin, hosted):
    nb = pret.shape[0]
    sp = _ret_specs(lambda b, h: (b, h))

    def body(*refs):
        own_in, h_in, own_out, h_out, own_scr, h_sems = hosted.split(refs, 6, 2)
        p_ref, pc_ref, rd_ref, gn_ref, cos_ref, sin_ref = own_in
        (o_ref, mix_ref), (q_s, k_s, o_s, st_s) = own_out, own_scr
        grid_step = pl.program_id(0) * RH + pl.program_id(1)

        @pl.when(grid_step == 0)
        def _():
            hosted.start(h_in, h_out, h_sems)

        cos_v, sin_v = cos_ref[...], sin_ref[...]
        q_s[...] = _rope(p_ref[:, 0:128], cos_v, sin_v) * (RD ** -0.5)
        k_s[...] = _rope(p_ref[:, 128:256], cos_v, sin_v)
        lgs, init = [], []
        for rev in (False, True):
            lg = jax.nn.log_sigmoid(rd_ref[int(rev):int(rev) + 1, :])
            s = jnp.zeros((RD, RD), F32)
            for n in ((1, 0) if rev else (0, 1)):
                s = _ret_state(pc_ref[n * CH:(n + 1) * CH, 128:256], pc_ref[n * CH:(n + 1) * CH, 256:384], s, lg, rev)
            lgs.append(lg)
            init.append(s)

        dec = _Decays(lgs)
        _state_pass(dec, init, k_s, lambda sl: p_ref[sl, 256:384], st_s)

        def chunk(n, carry):
            sl = pl.ds(pl.multiple_of(n * CH, CH), CH)
            q = q_s[sl, :]
            o_s[sl, :] = (_nn(_nt(q, k_s[sl, :]) * dec.mask, p_ref[sl, 256:384]) + _nn(_both(q, dec.a), st_s[n]))
            return carry

        _chunk_loop(NCH, chunk, 0)
        o = o_s[...]
        o_ref[...] = o
        mix_ref[...] = _ln_gate(o, p_ref[:, 384:512], gn_ref[...]).astype(BF16)

        @pl.when(grid_step == nb * RH - 1)
        def _():
            hosted.finish(h_in, h_out, h_sems)

    h_in_specs, h_out_specs = hosted.specs()
    return pl.pallas_call(
        body, name="retention_fwd", grid=(nb, RH),
        in_specs=[sp["pret"], sp["pretc"], sp["rd"], sp["gn"], sp["tab"], sp["tab"]] + h_in_specs,
        out_specs=[sp["head"], sp["head"]] + h_out_specs,
        out_shape=[jax.ShapeDtypeStruct((nb, SEQ, RH * RD), F32), jax.ShapeDtypeStruct((nb, SEQ, D), BF16)]
        + hosted.out_shape,
        scratch_shapes=[pltpu.VMEM((SEQ, RD), F32)] * 3 + [pltpu.VMEM((NCH, 2 * RD, RD), F32)] + hosted.scratch,
        compiler_params=_cp(("arbitrary", "arbitrary")),
    )(pret, pretc, rd, gn, cos, sin, *hosted.args)


def retention_bwd(pret, pretc, o_all, dmixin, rd, gn, cos, sin, hosted):
    nb = pret.shape[0]
    sp = _ret_specs(lambda h, b: (b, h))

    def body(*refs):
        own_in, h_in, own_out, h_out, own_scr, h_sems = hosted.split(refs, 8, 4)
        p_ref, pc_ref, o_ref, dmix_ref, rd_ref, gn_ref, cos_ref, sin_ref = own_in
        dp_ref, dpc_ref, drd_ref, dgn_ref = own_out
        q_s, k_s, do_s, dq_s, dk_s, dv_s, st_s, gst_s = own_scr
        b = pl.program_id(1)
        grid_step = pl.program_id(0) * nb + b

        @pl.when(grid_step == 0)
        def _():
            hosted.start(h_in, h_out, h_sems)

        cos_v, sin_v = cos_ref[...], sin_ref[...]
        q_s[...] = _rope(p_ref[:, 0:128], cos_v, sin_v) * (RD ** -0.5)
        k_s[...] = _rope(p_ref[:, 128:256], cos_v, sin_v)
        _, gate_vjp = jax.vjp(_ln_gate, o_ref[...], p_ref[:, 384:512], gn_ref[...])
        do, dg, dgn = gate_vjp(dmix_ref[...].astype(F32))
        do_s[...] = do
        dp_ref[:, 384:512] = dg.astype(BF16)

        @pl.when(b == 0)
        def _():
            drd_ref[...] = jnp.zeros_like(drd_ref)
            dgn_ref[...] = jnp.zeros_like(dgn_ref)

        dgn_ref[...] += dgn
        kcs = [pc_ref[n * CH:(n + 1) * CH, 128:256] for n in (0, 1)]
        vcs = [pc_ref[n * CH:(n + 1) * CH, 256:384] for n in (0, 1)]
        dirs = []
        init = []
        for rev in (False, True):
            rdv = rd_ref[int(rev):int(rev) + 1, :]
            lg = jax.nn.log_sigmoid(rdv)
            order_c = (1, 0) if rev else (0, 1)
            s = jnp.zeros((RD, RD), F32)
            ctx_states = []
            for n in order_c:
                ctx_states.append(s)
                s = _ret_state(kcs[n], vcs[n], s, lg, rev)
            dirs.append((rev, order_c, lg, rdv, ctx_states))
            init.append(s)
        dec = _Decays([lg for _, _, lg, _, _ in dirs])

        def v_of(sl):
            return p_ref[sl, 256:384]

        _state_pass(dec, init, k_s, v_of, st_s)
        zeros = jnp.zeros((CH, RD), F32)

        def scores_back(n, carry):
            dmask_sum, da_f, da_b = carry
            sl = pl.ds(pl.multiple_of(n * CH, CH), CH)
            q, k, v, do = q_s[sl, :], k_s[sl, :], v_of(sl), do_s[sl, :]
            scores = _nt(q, k)
            d_att = _nt(do, v)
            d_scores = d_att * dec.mask
            d_qa = _nt(do, st_s[n])
            d_qf, d_qb = d_qa[:, 0:RD], d_qa[:, RD:2 * RD]
            dq_s[sl, :] = _nn(d_scores, k) + d_qf * dec.a[0] + d_qb * dec.a[1]
            dk_s[sl, :] = _tn(d_scores, q)
            dv_s[sl, :] = _tn(scores * dec.mask, do)
            gst_s[n] = _tn(_both(q, dec.a), do)
            return dmask_sum + d_att * scores, da_f + d_qf * q, da_b + d_qb * q

        dmask_sum, da_f, da_b = _chunk_loop(NCH, scores_back, (zeros, zeros, zeros))

        def state_back(t, carry):
            out = []
            for d, r in enumerate(carry):
                n = t if d else (NCH - 1 - t)
                rows = slice(d * RD, (d + 1) * RD)
                own = gst_s[n, rows, :]
                gst_s[n, rows, :] = r
                out.append(own + dec.g[d] * r)
            return tuple(out)

        d_states = _chunk_loop(NCH, state_back, (zeros, zeros))

        def updates_back(n, carry):
            db_f, db_b, dg_f, dg_b = carry
            sl = pl.ds(pl.multiple_of(n * CH, CH), CH)
            k, r, s = k_s[sl, :], gst_s[n], st_s[n]
            d_kw = _nt(v_of(sl), r)
            d_kf, d_kb = d_kw[:, 0:RD], d_kw[:, RD:2 * RD]
            dk_s[sl, :] += d_kf * dec.b[0] + d_kb * dec.b[1]
            dv_s[sl, :] += _nn(_both(k, dec.b), r)
            return (db_f + d_kf * k, db_b + d_kb * k, dg_f + r[0:RD, :] * s[0:RD, :],
                    dg_b + r[RD:2 * RD, :] * s[RD:2 * RD, :])

        db_dg = _chunk_loop(NCH, updates_back, (zeros, zeros, zeros, zeros))
        dkc = [None, None]
        dvc = [None, None]
        for d, ((rev, order_c, lg, rdv, ctx_states), ds) in enumerate(zip(dirs, d_states)):
            dlg = (_total(dmask_sum * dec.dmask[d]) + _total((da_f, da_b)[d] * dec.da[d])
                   + _total(db_dg[d] * dec.db[d]) + CH * dec.g[d] * _total(db_dg[2 + d]))
            for idx in (1, 0):
                n = order_c[idx]
                _, vjp = jax.vjp(functools.partial(_ret_state, reverse=rev), kcs[n], vcs[n], ctx_states[idx], lg)
                dk_c, dv_c, ds, dl = vjp(ds)
                dlg = dlg + dl
                dkc[n] = dk_c if dkc[n] is None else dkc[n] + dk_c
                dvc[n] = dv_c if dvc[n] is None else dvc[n] + dv_c
            drd_ref[int(rev):int(rev) + 1, :] += dlg * jax.nn.sigmoid(-rdv)
        dp_ref[:, 0:128] = _rope_t(dq_s[...] * (RD ** -0.5), cos_v, sin_v).astype(BF16)
        dp_ref[:, 128:256] = _rope_t(dk_s[...], cos_v, sin_v).astype(BF16)
        dp_ref[:, 256:384] = dv_s[...].astype(BF16)
        zero = jnp.zeros((CH, RD), BF16)
        for n in (0, 1):
            rows = slice(n * CH, (n + 1) * CH)
            dpc_ref[rows, 0:128] = zero
            dpc_ref[rows, 128:256] = dkc[n].astype(BF16)
            dpc_ref[rows, 256:384] = dvc[n].astype(BF16)
            dpc_ref[rows, 384:512] = zero

        @pl.when(grid_step == RH * nb - 1)
        def _():
            hosted.finish(h_in, h_out, h_sems)

    h_in_specs, h_out_specs = hosted.specs()
    return pl.pallas_call(
        body, name="retention_bwd", grid=(RH, nb),
        in_specs=[sp["pret"], sp["pretc"], sp["head"], sp["head"], sp["rd"], sp["gn"], sp["tab"], sp["tab"]]
        + h_in_specs,
        out_specs=[
            pl.BlockSpec((None, SEQ, 512), lambda h, b: (b, 0, h)),
            pl.BlockSpec((None, LC, 512), lambda h, b: (b, 0, h)),
            pl.BlockSpec((None, 2, 1), lambda h, b: (h, 0, 0)),
            pl.BlockSpec((None, 1, RD), lambda h, b: (h, 0, 0)),
        ] + h_out_specs,
        out_shape=[
            jax.ShapeDtypeStruct((nb, SEQ, IN_W), BF16),
            jax.ShapeDtypeStruct((nb, LC, IN_W), BF16),
            jax.ShapeDtypeStruct((RH, 2, 1), F32),
            jax.ShapeDtypeStruct((RH, 1, RD), F32),
        ] + hosted.out_shape,
        scratch_shapes=[pltpu.VMEM((SEQ, RD), F32)] * 6 + [pltpu.VMEM((NCH, 2 * RD, RD), F32)] * 2 + hosted.scratch,
        compiler_params=_cp(("arbitrary", "arbitrary")),
    )(pret, pretc, o_all, dmixin, rd, gn, cos, sin, *hosted.args)


def _rpb_flat(rpb):
    return jnp.pad(rpb, ((0, 0), (0, 1), (0, 33))).reshape(NPAIR, 2, 1, 1024)


def _rpb_flat_t(dflat):
    return dflat.reshape(8, 16, 64)[:, :15, :31]


def _barrel(x, left):
    row = lax.broadcasted_iota(jnp.int32, x.shape, 0)
    n = x.shape[1]
    for bit in range(6):
        s = 1 << bit
        x = jnp.where(((row >> bit) & 1) == 1, pltpu.roll(x, (n - s) if left else s, 1), x)
    return x


NA_TILE_ROWS, NA_BAND_ROWS = 4, 12
NA_Q, NA_K = NA_TILE_ROWS * GW, NA_BAND_ROWS * GW
NA_TILES = SEQ // NA_Q


def _band_start(r0):
    return min(max(r0 - 4, 0), 32 - NA_BAND_ROWS)


def _tile_layout(t):
    rows = range(t * NA_TILE_ROWS, (t + 1) * NA_TILE_ROWS)
    return tuple((r if r < 4 else (r - 24 if r > 28 else 4), min(max(r - 4, 0), 24) - _band_start(rows[0]))
                 for r in rows)


NA_CLASSES = sorted(set(_tile_layout(t) for t in range(NA_TILES)))


def _tile_rows(cls):
    return NA_CLASSES[cls]


def _na_tile(t):
    start = jnp.clip(NA_TILE_ROWS * t - 4, 0, 32 - NA_BAND_ROWS)
    cls = 0
    for tile in range(NA_TILES):
        cls = jnp.where(t == tile, NA_CLASSES.index(_tile_layout(tile)), cls)
    return pl.ds(pl.multiple_of(t * NA_Q, NA_Q), NA_Q), pl.ds(pl.multiple_of(start * GW, NA_Q), NA_K), cls


def _na_probs(qst, kb, kc, bias):
    s_loc = _nt(qst, kb) + bias
    s_ctx = _nt(qst, kc)
    m = jnp.maximum(jnp.max(s_loc, axis=1, keepdims=True), jnp.max(s_ctx, axis=1, keepdims=True))
    e_loc, e_ctx = jnp.exp(s_loc - m), jnp.exp(s_ctx - m)
    den = jnp.sum(e_loc, axis=1, keepdims=True) + jnp.sum(e_ctx, axis=1, keepdims=True)
    return e_loc / den, e_ctx / den


def _stack_heads(t):
    lane = lax.broadcasted_iota(jnp.int32, t.shape, 1)
    zero = jnp.zeros_like(t)
    return jnp.concatenate([jnp.where(lane < 64, t, zero), jnp.where(lane >= 64, t, zero)], axis=0)


def _unstack_heads(t):
    n = t.shape[0] // 2
    lane = lax.broadcasted_iota(jnp.int32, (n, 128), 1)
    return jnp.where(lane < 64, t[:n], t[n:])


NA_BIAS_SHAPE = (len(NA_CLASSES), 2 * NA_Q, NA_K)


def _na_bias_pair(flat_ref, out_ref):
    qc = lax.broadcasted_iota(jnp.int32, (GW, 512), 0)
    kc = lax.broadcasted_iota(jnp.int32, (GW, 512), 1) & 63
    start = jnp.clip(qc - 8, 0, GW - 16)
    window = (kc >= start) & (kc < start + 16)
    fill = jnp.full((GW, NA_K - 512), NEG, F32)
    for hh in (0, 1):
        skew = _barrel(pltpu.roll(jnp.broadcast_to(flat_ref[hh], (GW, 1024)), 1024 - 15, 1), left=False)
        by_class = [jnp.where(window, (skew if rc == 7 else pltpu.roll(skew, (9 + rc) * 64, 1))[:, 0:512], NEG)
                    for rc in range(8)]
        for cls in range(len(NA_CLASSES)):
            for qr, (rc, off) in enumerate(_tile_rows(cls)):
                w = jnp.concatenate([by_class[rc], fill], axis=1)
                rows = slice(hh * NA_Q + qr * GW, hh * NA_Q + (qr + 1) * GW)
                out_ref[cls, rows, :] = pltpu.roll(w, off * GW, 1) if off else w


def na_fwd(pna, pnac, bias, mixin, hosted):
    nb = pna.shape[0]

    def body(*refs):
        (p_ref, pc_ref, bias_ref, _), h_in, (out_ref,), h_out, _, h_sems = hosted.split(refs, 4, 1)
        grid_step = pl.program_id(0) * nb + pl.program_id(1)

        @pl.when(grid_step == 0)
        def _():
            hosted.start(h_in, h_out, h_sems)

        kc, vc = pc_ref[:, 128:256], pc_ref[:, 256:384]

        def tile(t, carry):
            qsl, bsl, cls = _na_tile(t)
            kb, vb = p_ref[bsl, 128:256], p_ref[bsl, 256:384]
            p_loc, p_ctx = _na_probs(_stack_heads(p_ref[qsl, 0:128] * 0.125), kb, kc, bias_ref[cls])
            out_ref[qsl, :] = _unstack_heads(_nn(p_loc, vb) + _nn(p_ctx, vc)).astype(BF16)
            return carry

        lax.fori_loop(0, NA_TILES, tile, 0, unroll=4)

        @pl.when(grid_step == NPAIR * nb - 1)
        def _():
            hosted.finish(h_in, h_out, h_sems)

    h_in_specs, h_out_specs = hosted.specs()
    return pl.pallas_call(
        body, name="na_fwd", grid=(NPAIR, nb),
        in_specs=[
            pl.BlockSpec((None, SEQ, 384), lambda p, b: (b, 0, p)),
            pl.BlockSpec((None, LC, 384), lambda p, b: (b, 0, p)),
            pl.BlockSpec((None, len(NA_CLASSES), 2 * NA_Q, NA_K), lambda p, b: (p, 0, 0, 0)),
            pl.BlockSpec(memory_space=pl.ANY),
        ] + h_in_specs,
        out_specs=[pl.BlockSpec((None, SEQ, 128), lambda p, b: (b, 0, 4 + p))] + h_out_specs,
        out_shape=[jax.ShapeDtypeStruct((nb, SEQ, D), BF16)] + hosted.out_shape,
        input_output_aliases={3: 0},
        scratch_shapes=hosted.scratch,
        compiler_params=_cp(("arbitrary", "arbitrary")),
    )(pna, pnac, bias, mixin, *hosted.args)


def na_bwd(pna, pnac, bias, dmixin, dproj, dprojc, hosted):
    nb = pna.shape[0]

    def body(*refs):
        own_in, h_in, own_out, h_out, own_scr, h_sems = hosted.split(refs, 6, 3)
        p_ref, pc_ref, bias_ref, dmix_ref = own_in[:4]
        dp_ref, dpc_ref, dpat_ref = own_out
        dbias_s, dk_s, dv_s, dkc_s, dvc_s, res_s, resc_s = own_scr
        b, part = pl.program_id(1), pl.program_id(2)
        grid_step = (pl.program_id(0) * nb + b) * 3 + part

        @pl.when(grid_step == 0)
        def _():
            hosted.start(h_in, h_out, h_sems)

        @pl.when(grid_step == NPAIR * nb * 3 - 1)
        def _():
            hosted.finish(h_in, h_out, h_sems)

        @pl.when(part == 0)
        def _():
            @pl.when(b == 0)
            def _():
                dbias_s[...] = jnp.zeros_like(dbias_s)

            dk_s[...] = jnp.zeros_like(dk_s)
            dv_s[...] = jnp.zeros_like(dv_s)
            dkc_s[...] = jnp.zeros_like(dkc_s)
            dvc_s[...] = jnp.zeros_like(dvc_s)
            kc, vc = pc_ref[:, 128:256], pc_ref[:, 256:384]

            def tile(t, carry):
                qsl, bsl, cls = _na_tile(t)
                kb, vb = p_ref[bsl, 128:256], p_ref[bsl, 256:384]
                qst, dost = _stack_heads(p_ref[qsl, 0:128] * 0.125), _stack_heads(dmix_ref[qsl, :])
                p_loc, p_ctx = _na_probs(qst, kb, kc, bias_ref[cls])
                dp_loc, dp_ctx = _nt(dost, vb), _nt(dost, vc)
                delta = (jnp.sum(p_loc * dp_loc, axis=1, keepdims=True)
                         + jnp.sum(p_ctx * dp_ctx, axis=1, keepdims=True))
                ds_loc, ds_ctx = p_loc * (dp_loc - delta), p_ctx * (dp_ctx - delta)
                dbias_s[cls] += ds_loc
                res_s[0, qsl, :] = _unstack_heads((_nn(ds_loc, kb) + _nn(ds_ctx, kc)) * 0.125).astype(BF16)
                dk_s[bsl, :] += _tn(ds_loc, qst)
                dv_s[bsl, :] += _tn(p_loc, dost)
                dkc_s[...] += _tn(ds_ctx, qst)
                dvc_s[...] += _tn(p_ctx, dost)
                return carry

            lax.fori_loop(0, NA_TILES, tile, 0, unroll=2)
            res_s[1] = dk_s[...].astype(BF16)
            res_s[2] = dv_s[...].astype(BF16)
            resc_s[0] = jnp.zeros((LC, 128), BF16)
            resc_s[1] = dkc_s[...].astype(BF16)
            resc_s[2] = dvc_s[...].astype(BF16)

            @pl.when(b == nb - 1)
            def _():
                for hh in (0, 1):
                    by_class = [None] * 8
                    for cls in range(len(NA_CLASSES)):
                        for qr, (rc, off) in enumerate(_tile_rows(cls)):
                            w = dbias_s[cls, hh * NA_Q + qr * GW:hh * NA_Q + (qr + 1) * GW, :]
                            w = (pltpu.roll(w, NA_K - off * GW, 1) if off else w)[:, 0:512]
                            by_class[rc] = w if by_class[rc] is None else by_class[rc] + w
                    skew = jnp.zeros((GW, 1024), F32)
                    for rc in range(8):
                        w = jnp.concatenate([by_class[rc], jnp.zeros((GW, 512), F32)], axis=1)
                        skew = skew + (w if rc == 7 else pltpu.roll(w, (7 - rc) * 64, 1))
                    dpat_ref[hh] = jnp.sum(pltpu.roll(_barrel(skew, left=True), 15, 1), axis=0, keepdims=True)

        dp_ref[...] = res_s[part]
        dpc_ref[...] = resc_s[part]

    h_in_specs, h_out_specs = hosted.specs()
    return pl.pallas_call(
        body, name="na_bwd", grid=(NPAIR, nb, 3),
        in_specs=[
            pl.BlockSpec((None, SEQ, 384), lambda p, b, s: (b, 0, p)),
            pl.BlockSpec((None, LC, 384), lambda p, b, s: (b, 0, p)),
            pl.BlockSpec((None, len(NA_CLASSES), 2 * NA_Q, NA_K), lambda p, b, s: (p, 0, 0, 0)),
            pl.BlockSpec((None, SEQ, 128), lambda p, b, s: (b, 0, 4 + p)),
            pl.BlockSpec(memory_space=pl.ANY),
            pl.BlockSpec(memory_space=pl.ANY),
        ] + h_in_specs,
        out_specs=[
            pl.BlockSpec((None, SEQ, 128), lambda p, b, s: (b, 0, 16 + 3 * p + s)),
            pl.BlockSpec((None, LC, 128), lambda p, b, s: (b, 0, 16 + 3 * p + s)),
            pl.BlockSpec((None, 2, 1, 1024), lambda p, b, s: (p, 0, 0, 0)),
        ] + h_out_specs,
        out_shape=[
            jax.ShapeDtypeStruct((nb, SEQ, IN_W), BF16),
            jax.ShapeDtypeStruct((nb, LC, IN_W), BF16),
            jax.ShapeDtypeStruct((NPAIR, 2, 1, 1024), F32),
        ] + hosted.out_shape,
        input_output_aliases={4: 0, 5: 1},
        scratch_shapes=[
            pltpu.VMEM((len(NA_CLASSES), 2 * NA_Q, NA_K), F32),
            pltpu.VMEM((SEQ, 128), F32), pltpu.VMEM((SEQ, 128), F32),
            pltpu.VMEM((LC, 128), F32), pltpu.VMEM((LC, 128), F32),
            pltpu.VMEM((3, SEQ, 128), BF16), pltpu.VMEM((3, LC, 128), BF16),
        ] + hosted.scratch,
        compiler_params=_cp(("arbitrary", "arbitrary", "arbitrary")),
    )(pna, pnac, bias, dmixin, dproj, dprojc, *hosted.args)


def tail_fwd_bwd(x, mixin, tgt, mod3, g_post_mix, g_pre_mlp, g_post_mlp, wout, w1, w2):
    nb = x.shape[0]

    def body(x_ref, mi_ref, tgt_ref, mod_ref, gpm_ref, gpl_ref, gpo_ref, wo_ref, w1_ref, w2_ref,
             dx_ref, dmix_ref, h2_ref, du_ref, a_ref, dm_ref, dmi_ref, dmod_ref, dg_ref, loss_ref):
        b, t = pl.program_id(0), pl.program_id(1)
        gt1, sh2, sc2, gt2 = mod_ref[2:3, :], mod_ref[3:4, :], mod_ref[4:5, :], mod_ref[5:6, :]
        mix = jnp.dot(mi_ref[...], wo_ref[...], preferred_element_type=F32)
        (x1, h2), vjp_a = jax.vjp(_post_mix, x_ref[...], mix, gt1, sc2, sh2, gpm_ref[...], gpl_ref[...])
        h2b = h2.astype(BF16)
        h2_ref[...] = h2b
        m = jnp.zeros((TN, D), F32)
        relus = []
        for j in range(4):
            cols = slice(j * D, (j + 1) * D)
            r = jnp.maximum(jnp.dot(h2b, w1_ref[j], preferred_element_type=F32), 0.0)
            ab = (r * r).astype(BF16)
            a_ref[:, cols] = ab
            m = m + jnp.dot(ab, w2_ref[cols, :], preferred_element_type=F32)
            relus.append(r)
        loss, vjp_b = jax.vjp(_head_loss, x1, m, gt2, gpo_ref[...], tgt_ref[...])
        dx1, dm, dgt2, dgpo, _ = vjp_b(jnp.ones((1, 1), F32))
        dmb = dm.astype(BF16)
        dm_ref[...] = dmb
        dh2 = jnp.zeros((TN, D), F32)
        for j in range(4):
            cols = slice(j * D, (j + 1) * D)
            da = lax.dot_general(dmb, w2_ref[cols, :], (((1,), (1,)), ((), ())), preferred_element_type=F32)
            dub = (da * (2.0 * relus[j])).astype(BF16)
            du_ref[:, cols] = dub
            dh2 = dh2 + lax.dot_general(dub, w1_ref[j], (((1,), (1,)), ((), ())), preferred_element_type=F32)
        dx, dmix, dgt1, dsc2, dsh2, dgpm, dgpl = vjp_a((dx1, dh2))
        dx_ref[...] = dx
        dmixb = dmix.astype(BF16)
        dmix_ref[...] = dmixb
        dmi_ref[...] = lax.dot_general(dmixb, wo_ref[...], (((1,), (1,)), ((), ())),
                                       preferred_element_type=F32).astype(BF16)

        @pl.when(t == 0)
        def _():
            dmod_ref[...] = jnp.zeros_like(dmod_ref)

        @pl.when((t == 0) & (b == 0))
        def _():
            dg_ref[...] = jnp.zeros_like(dg_ref)
            loss_ref[...] = jnp.zeros_like(loss_ref)

        dmod_ref[2:3, :] += dgt1
        dmod_ref[3:4, :] += dsh2
        dmod_ref[4:5, :] += dsc2
        dmod_ref[5:6, :] += dgt2
        dg_ref[0:1, :] += dgpm
        dg_ref[1:2, :] += dgpl
        dg_ref[2:3, :] += dgpo
        loss_ref[...] += jnp.broadcast_to(loss, loss_ref.shape)

    tok = lambda b, t: (b, t, 0)
    const = lambda b, t: (0, 0)
    vec = pl.BlockSpec((1, D), const)
    return pl.pallas_call(
        body, name="tail_fwd_bwd", grid=(nb, SEQ // TN),
        in_specs=[
            pl.BlockSpec((None, TN, D), tok), pl.BlockSpec((None, TN, D), tok), pl.BlockSpec((None, TN, D), tok),
            pl.BlockSpec((None, 6, D), lambda b, t: (b, 0, 0)), vec, vec, vec,
            pl.BlockSpec((D, D), const, pipeline_mode=pl.Buffered(1)),
            pl.BlockSpec((4, D, D), lambda b, t: (0, 0, 0), pipeline_mode=pl.Buffered(1)),
            pl.BlockSpec((DFF, D), const, pipeline_mode=pl.Buffered(1)),
        ],
        out_specs=[
            pl.BlockSpec((None, TN, D), tok), pl.BlockSpec((None, TN, D), tok), pl.BlockSpec((None, TN, D), tok),
            pl.BlockSpec((None, TN, DFF), tok), pl.BlockSpec((None, TN, DFF), tok), pl.BlockSpec((None, TN, D), tok),
            pl.BlockSpec((None, TN, D), tok),
            pl.BlockSpec((None, 6, D), lambda b, t: (b, 0, 0)),
            pl.BlockSpec((8, D), const), pl.BlockSpec((8, 128), const),
        ],
        out_shape=[
            jax.ShapeDtypeStruct((nb, SEQ, D), F32), jax.ShapeDtypeStruct((nb, SEQ, D), BF16),
            jax.ShapeDtypeStruct((nb, SEQ, D), BF16), jax.ShapeDtypeStruct((nb, SEQ, DFF), BF16),
            jax.ShapeDtypeStruct((nb, SEQ, DFF), BF16), jax.ShapeDtypeStruct((nb, SEQ, D), BF16),
            jax.ShapeDtypeStruct((nb, SEQ, D), BF16),
            jax.ShapeDtypeStruct((nb, 6, D), F32), jax.ShapeDtypeStruct((8, D), F32),
            jax.ShapeDtypeStruct((8, 128), F32),
        ],
        compiler_params=_cp(("arbitrary", "arbitrary")),
    )(x, mixin, tgt, mod3, g_post_mix, g_pre_mlp, g_post_mlp, wout, w1, w2)


def weight_grad(pairs, name, out_dtype=F32, col_blocks=False, tm=1024, tn=1024, tk=2048):
    m, n = pairs[0][0].shape[1], pairs[0][1].shape[1]
    tn = min(tn, n)
    tks = [min(tk, xa.shape[0]) for xa, _ in pairs]
    steps = [xa.shape[0] // t for (xa, _), t in zip(pairs, tks)]
    total = sum(steps)
    offs = [sum(steps[:i]) for i in range(len(pairs))]

    def body(*refs):
        out_ref, acc = refs[2 * len(pairs)], refs[-1]
        k = pl.program_id(2)

        @pl.when(k == 0)
        def _():
            acc[...] = jnp.zeros_like(acc)

        for i in range(len(pairs)):
            @pl.when((k >= offs[i]) & (k < offs[i] + steps[i]))
            def _(i=i):
                acc[...] += lax.dot_general(refs[2 * i][...], refs[2 * i + 1][...], (((0,), (0,)), ((), ())),
                                            preferred_element_type=F32)

        if out_dtype != F32:
            @pl.when(k == total - 1)
            def _():
                out_ref[...] = acc[...].astype(out_dtype)

    in_specs, args = [], []
    for i, (xa, ya) in enumerate(pairs):
        clamp = lambda k, i=i: jnp.clip(k - offs[i], 0, steps[i] - 1)
        in_specs.append(pl.BlockSpec((tks[i], tm), lambda a, c, k, clamp=clamp: (clamp(k), a)))
        in_specs.append(pl.BlockSpec((tks[i], tn), lambda a, c, k, clamp=clamp: (clamp(k), c)))
        args += [xa, ya]
    if col_blocks:
        out_spec = pl.BlockSpec((None, tm, tn), lambda a, c, k: (c, a, 0))
        out_shape = jax.ShapeDtypeStruct((n // tn, m, tn), out_dtype)
    else:
        out_spec = pl.BlockSpec((tm, tn), lambda a, c, k: (a, c))
        out_shape = jax.ShapeDtypeStruct((m, n), out_dtype)
    return pl.pallas_call(
        body, name=name, grid=(m // tm, n // tn, total), in_specs=in_specs, out_specs=out_spec, out_shape=out_shape,
        scratch_shapes=[] if out_dtype == F32 else [pltpu.VMEM((tm, tn), F32)],
        compiler_params=_cp(("arbitrary", "arbitrary", "arbitrary")),
    )(*args)


def _perm_block(t):
    return 4 * (t % 4) + t // 4 if t < 16 else 16 + 3 * ((t - 16) % 4) + (t - 16) // 4


def _is_rope_block(p):
    return p < 16 and p % 4 < 2


def unpack_w_in(blocks):
    def body(i_ref, o_ref):
        for t in range(28):
            p = _perm_block(t)
            blk = i_ref[t // 7, :, (t % 7) * 128:(t % 7 + 1) * 128]
            if _is_rope_block(p):
                blk = _pair_order(blk.astype(F32)).astype(BF16)
            o_ref[:, p * 128:(p + 1) * 128] = blk

    return pl.pallas_call(
        body, name="unpack_w_in", grid=(2,),
        in_specs=[pl.BlockSpec((4, D // 2, 896), lambda i: (0, i, 0))],
        out_specs=pl.BlockSpec((D // 2, IN_W), lambda i: (i, 0)),
        out_shape=jax.ShapeDtypeStruct((D, IN_W), BF16),
    )(blocks)


def pack_w_in(dw):
    def body(i_ref, o_ref):
        for t in range(28):
            p = _perm_block(t)
            blk = i_ref[:, p * 128:(p + 1) * 128]
            if _is_rope_block(p):
                blk = _pair_order(blk)
            o_ref[t // 7, :, (t % 7) * 128:(t % 7 + 1) * 128] = blk.astype(BF16)

    return pl.pallas_call(
        body, name="pack_w_in", grid=(4,),
        in_specs=[pl.BlockSpec((D // 4, IN_W), lambda i: (i, 0))],
        out_specs=pl.BlockSpec((4, D // 4, 896), lambda i: (0, i, 0)),
        out_shape=jax.ShapeDtypeStruct((4, D, 896), BF16),
    )(dw)


def _place():
    return lax.axis_index("x"), lax.axis_index("y"), lax.axis_index("c")


class Hosted:
    def __init__(self, args, out_shape, scratch, start, finish):
        self.args, self.out_shape, self.scratch, self.start, self.finish = args, out_shape, scratch, start, finish

    def specs(self):
        hbm = pl.BlockSpec(memory_space=pl.ANY)
        return [hbm] * len(self.args), [hbm] * len(self.out_shape)

    def split(self, refs, n_in, n_out):
        a, b = len(self.args), len(self.out_shape)
        cuts = [n_in, n_in + a, n_in + a + n_out, n_in + a + n_out + b, len(refs) - len(self.scratch)]
        parts = [refs[i:j] for i, j in zip([0] + cuts, cuts + [len(refs)])]
        return parts[0], parts[1], parts[2], parts[3], parts[4], parts[5]


def no_exchange():
    return Hosted([], [], [], lambda *a: None, lambda *a: None)


def run_hosted(hosted, name):
    def body(*refs):
        _, ins, _, outs, _, sems = hosted.split(refs, 0, 0)
        hosted.start(ins, outs, sems)
        hosted.finish(ins, outs, sems)

    in_specs, out_specs = hosted.specs()
    return pl.pallas_call(body, name=name, in_specs=in_specs, out_specs=out_specs, out_shape=hosted.out_shape,
                          scratch_shapes=hosted.scratch)(*hosted.args)


def gather8(blocks):
    na = len(blocks)

    def copies(ins, outs, sems):
        send_sems, recv_sems, local_sem = sems
        x, y, c = _place()
        me, sibling = (x, y, c), (x, y, 1 - c)
        chips = [(1 - x, y), (x, 1 - y), (1 - x, 1 - y)]

        def slot(o_ref, px, py, pc):
            return o_ref.at[4 * px + 2 * py + pc]

        def copy(a, k, block, to, src=None):
            return pltpu.make_async_remote_copy(
                src_ref=slot(outs[a], *block) if src is None else src, dst_ref=slot(outs[a], *block),
                send_sem=send_sems.at[a, k], recv_sem=recv_sems.at[a, k], device_id=to, device_id_type=MESH)

        mine = [pltpu.make_async_copy(ins[a], slot(outs[a], *me), local_sem.at[a]) for a in range(na)]
        first = []
        for a in range(na):
            first.append(copy(a, 0, me, sibling, src=ins[a]))
            first += [copy(a, 1 + j, me, (*chip, c), src=ins[a]) for j, chip in enumerate(chips)]
        return copy, mine, first, me, sibling, chips, c

    def start(ins, outs, sems):
        _, mine, first, *_ = copies(ins, outs, sems)
        for cp in mine + first:
            cp.start()

    def finish(ins, outs, sems):
        copy, mine, first, me, sibling, chips, c = copies(ins, outs, sems)
        passed = []
        for j, chip in enumerate(chips):
            for a in range(na):
                copy(a, 1 + j, (*chip, c), me).wait_recv()
                cp = copy(a, 4 + j, (*chip, c), sibling)
                cp.start()
                passed.append(cp)
        for a in range(na):
            copy(a, 0, sibling, me).wait_recv()
            for j, chip in enumerate(chips):
                copy(a, 4 + j, (*chip, 1 - c), me).wait_recv()
        for cp in first + passed:
            cp.wait_send()
        for cp in mine:
            cp.wait()

    return Hosted(list(blocks), [jax.ShapeDtypeStruct((8,) + b.shape, b.dtype) for b in blocks],
                  [pltpu.SemaphoreType.DMA((na, 7)), pltpu.SemaphoreType.DMA((na, 7)), pltpu.SemaphoreType.DMA((na,))],
                  start, finish)


def chips3(arrays):
    na = len(arrays)

    def copies(ins, outs, sems):
        send_sems, recv_sems = sems
        x, y, c = _place()
        return [pltpu.make_async_remote_copy(
            src_ref=ins[a].at[2 * px + py], dst_ref=outs[a].at[k], send_sem=send_sems.at[a, k],
            recv_sem=recv_sems.at[a, k], device_id=(px, py, c), device_id_type=MESH)
            for a in range(na) for k, (px, py) in enumerate([(1 - x, y), (x, 1 - y), (1 - x, 1 - y)])]

    def start(ins, outs, sems):
        for cp in copies(ins, outs, sems):
            cp.start()

    def finish(ins, outs, sems):
        for cp in copies(ins, outs, sems):
            cp.wait()

    return Hosted(list(arrays), [jax.ShapeDtypeStruct((3,) + a.shape[1:], a.dtype) for a in arrays],
                  [pltpu.SemaphoreType.DMA((na, 3)), pltpu.SemaphoreType.DMA((na, 3))], start, finish)


def siblings(arrays):
    na = len(arrays)

    def copies(ins, outs, sems):
        send_sems, recv_sems = sems
        x, y, c = _place()
        return [pltpu.make_async_remote_copy(
            src_ref=ins[a], dst_ref=outs[a], send_sem=send_sems.at[a], recv_sem=recv_sems.at[a],
            device_id=(x, y, 1 - c), device_id_type=MESH) for a in range(na)]

    def start(ins, outs, sems):
        for cp in copies(ins, outs, sems):
            cp.start()

    def finish(ins, outs, sems):
        for cp in copies(ins, outs, sems):
            cp.wait()

    return Hosted(list(arrays), [jax.ShapeDtypeStruct(a.shape, a.dtype) for a in arrays],
                  [pltpu.SemaphoreType.DMA((na,)), pltpu.SemaphoreType.DMA((na,))], start, finish)


def both(first, second):
    na, no, ns = len(first.args), len(first.out_shape), len(first.scratch)

    def start(ins, outs, sems):
        first.start(ins[:na], outs[:no], sems[:ns])
        second.start(ins[na:], outs[no:], sems[ns:])

    def finish(ins, outs, sems):
        first.finish(ins[:na], outs[:no], sems[:ns])
        second.finish(ins[na:], outs[no:], sems[ns:])

    return Hosted(first.args + second.args, first.out_shape + second.out_shape, first.scratch + second.scratch,
                  start, finish)


def siblings4(arrays):
    na = len(arrays)

    def copies(ins, outs, sems):
        send_sems, recv_sems = sems
        x, y, c = _place()
        return [pltpu.make_async_remote_copy(
            src_ref=ins[a].at[2 * j + 1 - c], dst_ref=outs[a].at[j],
            send_sem=send_sems.at[a, j], recv_sem=recv_sems.at[a, j],
            device_id=(x, y, 1 - c), device_id_type=MESH) for a in range(na) for j in range(4)]

    def start(ins, outs, sems):
        for cp in copies(ins, outs, sems):
            cp.start()

    def finish(ins, outs, sems):
        for cp in copies(ins, outs, sems):
            cp.wait()

    return Hosted(list(arrays), [jax.ShapeDtypeStruct((4,) + a.shape[1:], a.dtype) for a in arrays],
                  [pltpu.SemaphoreType.DMA((na, 4)), pltpu.SemaphoreType.DMA((na, 4))], start, finish)


def _row_tile(r):
    for cand in (512, 256, 128, 64, 32, 16, 8):
        if r % cand == 0:
            return cand
    return r


def chip_partial(place, g8s, landed4s, name):
    n = len(g8s)

    def body(place_ref, *refs):
        del place_ref
        for g_ref, l_ref, o_ref in zip(refs[:n], refs[n:2 * n], refs[2 * n:]):
            o_ref[...] = (g_ref[...].astype(F32) + l_ref[...].astype(F32)).astype(BF16)

    own = [pl.BlockSpec((None,) + g.shape[1:], lambda j, s: (2 * j + s[0], 0, 0)) for g in g8s]
    plain = [pl.BlockSpec((None,) + g.shape[1:], lambda j, s: (j, 0, 0)) for g in g8s]
    return pl.pallas_call(
        body, name=name,
        grid_spec=pltpu.PrefetchScalarGridSpec(num_scalar_prefetch=1, grid=(4,), in_specs=own + plain, out_specs=plain),
        out_shape=[jax.ShapeDtypeStruct((4,) + g.shape[1:], BF16) for g in g8s],
    )(place, *g8s, *landed4s)


def shard_sum(place, partial4s, landed3s, name):
    n = len(partial4s)

    def body(place_ref, *refs):
        del place_ref
        for p_ref, l_ref, o_ref in zip(refs[:n], refs[n:2 * n], refs[2 * n:]):
            acc = p_ref[...].astype(F32)
            for k in range(3):
                acc = acc + l_ref[k].astype(F32)
            o_ref[...] = acc

    def halves(p, lead):
        r, ccols = p.shape[1:]
        return (lead, r // 2, ccols)

    return pl.pallas_call(
        body, name=name,
        grid_spec=pltpu.PrefetchScalarGridSpec(
            num_scalar_prefetch=1, grid=(2,),
            in_specs=[pl.BlockSpec(halves(p, None), lambda i, s: (s[1], i, 0)) for p in partial4s]
            + [pl.BlockSpec(halves(p, 3), lambda i, s: (0, i, 0)) for p in partial4s],
            out_specs=[pl.BlockSpec(halves(p, None)[1:], lambda i, s: (i, 0)) for p in partial4s]),
        out_shape=[jax.ShapeDtypeStruct(p.shape[1:], F32) for p in partial4s],
    )(place, *partial4s, *landed3s)


def _adamw_math(w, g, m, v):
    m2 = B1 * m + (1.0 - B1) * g
    v2 = B2 * v + (1.0 - B2) * (g * g)
    m_hat = m2 / (1.0 - B1 ** STEP)
    v_hat = v2 / (1.0 - B2 ** STEP)
    return -LR * (m_hat / (jnp.sqrt(v_hat) + AEPS) + WD * w), m2, v2


def adamw_halves(place, w, mine, theirs, m, v, name):
    r, ccols = w.shape
    hr = r // 2
    tr = _row_tile(hr)
    nt = hr // tr

    def body(place_ref, w_ref, a_ref, b_ref, m_ref, v_ref, g_out, d_out, m_out, v_out):
        g = jnp.where(pl.program_id(0) == place_ref[0], a_ref[...], b_ref[...])
        d, m2, v2 = _adamw_math(w_ref[...], g, m_ref[...], v_ref[...])
        g_out[...] = g
        d_out[...] = d
        m_out[...] = m2
        v_out[...] = v2

    full = pl.BlockSpec((tr, ccols), lambda h, i, s: (h * nt + i, 0))
    part = pl.BlockSpec((tr, ccols), lambda h, i, s: (i, 0))
    return pl.pallas_call(
        body, name=name,
        grid_spec=pltpu.PrefetchScalarGridSpec(
            num_scalar_prefetch=1, grid=(2, nt), in_specs=[full, part, part, full, full], out_specs=[full] * 4),
        out_shape=[jax.ShapeDtypeStruct((r, ccols), F32)] * 4,
    )(place, w, mine, theirs, m, v)


def adamw_group(place, halved, plain, hosted, name):
    rows = halved[0][0].shape[0]
    tr = 128
    nt = rows // 2 // tr
    nh, npl = len(halved), len(plain)

    def body(place_ref, *refs):
        own_in, h_in, own_out, h_out, _, h_sems = hosted.split(refs, 5 * nh + 4 * npl, 4 * nh + 3 * npl)
        half = pl.program_id(0)
        grid_step = half * nt + pl.program_id(1)

        @pl.when(grid_step == 0)
        def _():
            hosted.start(h_in, h_out, h_sems)

        for i in range(nh):
            w_ref, a_ref, b_ref, m_ref, v_ref = own_in[5 * i:5 * i + 5]
            g = jnp.where(half == place_ref[0], a_ref[...], b_ref[...])
            res = (g,) + _adamw_math(w_ref[...], g, m_ref[...], v_ref[...])
            for o_ref, r in zip(own_out[4 * i:4 * i + 4], res):
                o_ref[...] = r
        for i in range(npl):
            w_ref, g_ref, m_ref, v_ref = own_in[5 * nh + 4 * i:5 * nh + 4 * i + 4]
            res = _adamw_math(w_ref[...], g_ref[...], m_ref[...], v_ref[...])
            for o_ref, r in zip(own_out[4 * nh + 3 * i:4 * nh + 3 * i + 3], res):
                o_ref[...] = r

        @pl.when(grid_step == 2 * nt - 1)
        def _():
            hosted.finish(h_in, h_out, h_sems)

    def full(cols):
        return pl.BlockSpec((tr, cols), lambda h, i, s: (h * nt + i, 0))

    def part(cols):
        return pl.BlockSpec((tr, cols), lambda h, i, s: (i, 0))

    in_specs, out_specs, out_shape, args = [], [], [], []
    for w, a, b, m, v in halved:
        cols = w.shape[1]
        in_specs += [full(cols), part(cols), part(cols), full(cols), full(cols)]
        out_specs += [full(cols)] * 4
        out_shape += [jax.ShapeDtypeStruct(w.shape, F32)] * 4
        args += [w, a, b, m, v]
    for w, g, m, v in plain:
        cols = w.shape[1]
        in_specs += [full(cols)] * 4
        out_specs += [full(cols)] * 3
        out_shape += [jax.ShapeDtypeStruct(w.shape, F32)] * 3
        args += [w, g, m, v]
    h_in_specs, h_out_specs = hosted.specs()
    return pl.pallas_call(
        body, name=name,
        grid_spec=pltpu.PrefetchScalarGridSpec(
            num_scalar_prefetch=1, grid=(2, nt), in_specs=in_specs + h_in_specs, out_specs=out_specs + h_out_specs,
            scratch_shapes=hosted.scratch),
        out_shape=out_shape + hosted.out_shape,
        compiler_params=_cp(("arbitrary", "arbitrary")),
    )(place, *args, *hosted.args)


def _silu(x):
    return x * jax.nn.sigmoid(x)


def prologue(c_rows, c_ctx_row, w_ada, b_shard, rpb_flat, half_w_in, late_shards):
    shape = jax.ShapeDtypeStruct
    n_late = len(late_shards)
    half_shapes = [(w.shape[0] // 2, w.shape[1]) for w in late_shards]
    g_w = gather8([half_w_in])
    g_c = gather8([shape((8, D), F32)])
    g_m = gather8([shape((32, 1536), F32)])

    def body(*refs):
        c_ref, cc_ref, w_ref, b_ref, flat_ref, hw_ref = refs[:6]
        late_refs = refs[6:6 + n_late]
        cin_ref, mg_ref, gw_ref, bias_ref, cos_ref, sin_ref = refs[6 + n_late:12 + n_late]
        rest = refs[12 + n_late:]
        half_refs, (cg_s, ms_s, bias_s) = rest[:n_late], rest[n_late:n_late + 3]
        stage, (load_sem, bias_sem), sems = rest[n_late + 3:2 * n_late + 3], rest[2 * n_late + 3:2 * n_late + 5], \
            rest[2 * n_late + 5:]
        sw, sc, sm = sems[0:3], sems[3:6], sems[6:9]
        core = lax.axis_index("c")
        g_c.start([c_ref], [cg_s], sc)
        g_w.start([hw_ref], [gw_ref], sw)
        loads = [pltpu.make_async_copy(late_refs[a].at[pl.ds(core * half_shapes[a][0], half_shapes[a][0]), :],
                                       stage[a], load_sem.at[a]) for a in range(n_late)]
        for cp in loads:
            cp.start()
        g_c.finish([c_ref], [cg_s], sc)
        cin_ref[...] = jnp.zeros_like(cin_ref)
        for dev in range(8):
            cin_ref[2 * dev:2 * dev + 2, :] = cg_s[dev, 0:2, :]
        cin_ref[16:17, :] = cc_ref[...]
        ms_s[...] = _nn(_silu(cin_ref[...]), w_ref[...]) + b_ref[...]
        g_m.start([ms_s], [mg_ref], sm)
        for a, cp in enumerate(loads):
            cp.wait()
            half_refs[a][...] = stage[a][...].astype(BF16)
        cos_ref[...], sin_ref[...] = _rope_tables()
        stores = []
        for pair in range(NPAIR):
            if pair >= 2:
                stores[pair - 2].wait()
            _na_bias_pair(flat_ref.at[pair], bias_s.at[pair % 2])
            stores.append(pltpu.make_async_copy(bias_s.at[pair % 2], bias_ref.at[pair], bias_sem.at[pair % 2]))
            stores[pair].start()
        for cp in stores[-2:]:
            cp.wait()
        g_w.finish([hw_ref], [gw_ref], sw)
        g_m.finish([ms_s], [mg_ref], sm)

    vmem = pl.BlockSpec(memory_space=pltpu.VMEM)
    hbm = pl.BlockSpec(memory_space=pl.ANY)
    return pl.pallas_call(
        body, name="prologue", in_specs=[vmem, vmem, vmem, vmem, vmem, hbm] + [hbm] * n_late,
        out_specs=[vmem, vmem, hbm, hbm, vmem, vmem] + [vmem] * n_late,
        out_shape=[shape((32, D), F32), shape((8, 32, 1536), F32)] + g_w.out_shape
        + [shape((NPAIR,) + NA_BIAS_SHAPE, F32)] + [shape((SEQ, RD), F32)] * 2 + [shape(s, BF16) for s in half_shapes],
        scratch_shapes=[pltpu.VMEM((8, 8, D), F32), pltpu.VMEM((32, 1536), F32), pltpu.VMEM((2,) + NA_BIAS_SHAPE, F32)]
        + [pltpu.VMEM(s, F32) for s in half_shapes]
        + [pltpu.SemaphoreType.DMA((n_late,)), pltpu.SemaphoreType.DMA((2,))]
        + g_w.scratch + g_c.scratch + g_m.scratch,
        compiler_params=_cp(),
    )(c_rows, c_ctx_row, w_ada, b_shard, rpb_flat, half_w_in, *late_shards)


def ada_grads(cin, gb, gc, w_ada):
    def body(c_ref, gb_ref, gc_ref, w_ref, gw_ref, pc_ref):
        ctx_tot = jnp.sum(gc_ref[...], axis=0, keepdims=True)
        rows = lax.broadcasted_iota(jnp.int32, (16, 512), 0)
        dm = jnp.concatenate([gb_ref[...], jnp.where(rows == 0, ctx_tot, 0.0)], axis=0)
        gw_ref[...] = _tn(_silu(c_ref[...]), dm)
        rows8 = lax.broadcasted_iota(jnp.int32, (8, 512), 0)
        part = _nt(jnp.where(rows8 == 0, ctx_tot, 0.0), w_ref[...])

        @pl.when(pl.program_id(0) == 0)
        def _():
            pc_ref[...] = jnp.zeros_like(pc_ref)

        pc_ref[...] += part

    return pl.pallas_call(
        body, name="ada_grads", grid=(3,),
        in_specs=[pl.BlockSpec((32, D), lambda j: (0, 0)), pl.BlockSpec((16, 512), lambda j: (0, j)),
                  pl.BlockSpec((8, 512), lambda j: (0, j)), pl.BlockSpec((D, 512), lambda j: (0, j))],
        out_specs=[pl.BlockSpec((D, 512), lambda j: (0, j)), pl.BlockSpec((8, D), lambda j: (0, 0))],
        out_shape=[jax.ShapeDtypeStruct((D, 1536), F32), jax.ShapeDtypeStruct((8, D), F32)],
    )(cin, gb, gc, w_ada)


SMALL_SUM_ROWS = 15


def small_update(gsm, gbf, gcf, pcg, params):
    n = len(params)

    def body(*refs):
        gsm_ref, gbf_ref, gcf_ref, pcg_ref = refs[:4]
        wmv, outs, loss_out = refs[4:4 + 3 * n], refs[4 + 3 * n:4 + 7 * n], refs[-1]
        acc = gsm_ref[0]
        for dev in range(1, 8):
            acc = acc + gsm_ref[dev]
        c_ctx = wmv[0][...]
        sg = jax.nn.sigmoid(c_ctx)
        dsilu = pcg_ref[0:1, :] + pcg_ref[2:3, :] + pcg_ref[4:5, :] + pcg_ref[6:7, :]
        lane = lax.broadcasted_iota(jnp.int32, (1, D), 1)
        last = acc[14:15, :]
        grads = [
            dsilu * (sg * (1.0 + c_ctx * (1.0 - sg))),
            jnp.sum(gbf_ref[...], axis=0, keepdims=True) + jnp.sum(gcf_ref[...], axis=0, keepdims=True),
            acc[0:1, :] + acc[1:2, :], acc[2:3, :], acc[3:4, :], acc[4:5, :],
            acc[5:6, 0:512], acc[6:14, :], jnp.where(lane < 8, last, 0.0),
        ]
        loss_out[...] = jnp.broadcast_to(jnp.sum(jnp.where(lane == 8, last, 0.0), axis=1, keepdims=True), (8, 128))
        for i, g in enumerate(grads):
            d, m2, v2 = _adamw_math(wmv[3 * i][...], g, wmv[3 * i + 1][...], wmv[3 * i + 2][...])
            outs[4 * i][...] = g
            outs[4 * i + 1][...] = d
            outs[4 * i + 2][...] = m2
            outs[4 * i + 3][...] = v2

    flat = [a for wmv in params for a in wmv]
    out_shape = [jax.ShapeDtypeStruct(w.shape, F32) for w, _, _ in params for _ in range(4)]
    return pl.pallas_call(
        body, name="small_update", out_shape=out_shape + [jax.ShapeDtypeStruct((8, 128), F32)],
    )(gsm, gbf, gcf, pcg, *flat)


def _pad_row(v, rows):
    flat = v.reshape(-1)
    return jnp.pad(flat, (0, rows * D - flat.shape[0])).reshape(rows, D)


def local_step(x, ctx, tgt, mod3, rope, bias, g_pre_mix, g_post_mix, g_pre_mlp, g_post_mlp, ret_decay, ret_gn,
               wperm, late_weights, early_grads):
    nb = x.shape[0]
    tokens = nb * SEQ
    cos, sin = rope
    rd = ret_decay.T.reshape(RH, 2, 1)
    gn = ret_gn.reshape(RH, 1, RD)
    h, pret, pna = premix_proj(x, mod3, g_pre_mix, wperm, False, "premix_proj")
    hc, pretc, pnac = premix_proj(ctx, mod3, g_pre_mix, wperm, True, "premix_proj_ctx")
    o_all, mixin, gw_out = retention_fwd(pret, pretc, rd, gn, cos, sin, late_weights(0))
    mixin, gw1, gw2 = na_fwd(pna, pnac, bias, mixin, late_weights(1))
    dx_tail, dmix, h2, du, act, dm, dmixin, dmod_t, dg_t, loss_t = tail_fwd_bwd(
        x, mixin, tgt, mod3, g_post_mix, g_pre_mlp, g_post_mlp, gw_out.reshape(D, D), gw1.reshape(4, D, D),
        gw2.reshape(DFF, D))
    dw_out = weight_grad([(mixin.reshape(tokens, D), dmix.reshape(tokens, D))], "grad_w_out", BF16)
    dw1 = weight_grad([(h2.reshape(tokens, D), du.reshape(tokens, DFF))], "grad_w_mlp1", BF16, col_blocks=True)
    dw2 = weight_grad([(act.reshape(tokens, DFF), dm.reshape(tokens, D))], "grad_w_mlp2", BF16)
    dproj, dprojc, drd, dgn, *landed = retention_bwd(pret, pretc, o_all, dmixin, rd, gn, cos, sin,
                                                     early_grads[0](dw_out, dw1, dw2))
    dproj, dprojc, dpat, *early = na_bwd(pna, pnac, bias, dmixin, dproj, dprojc, early_grads[1](landed))
    dw_in = weight_grad([(h.reshape(tokens, D), dproj.reshape(tokens, IN_W)),
                         (hc.reshape(nb * LC, D), dprojc.reshape(nb * LC, IN_W))], "grad_w_in", tn=IN_W // 2, tk=1024)
    dmod_c, dg_c, *late = premix_bwd(ctx, mod3, g_pre_mix, wperm, dprojc, None, early_grads[2](dw_in), "premix_bwd_ctx")
    last = nb * (SEQ // TN) - 1
    dx, dmod_a, dg_a, *late = premix_bwd(x, mod3, g_pre_mix, wperm, dproj, dx_tail, early_grads[3](late), "premix_bwd",
                                         steps=(0, last))
    grad_x, dmod_a, dg_a = premix_bwd(x, mod3, g_pre_mix, wperm, dproj, dx_tail, no_exchange(), "premix_bwd_last",
                                      steps=(last, 1), carry=(dx, dmod_a, dg_a))
    dmod = jnp.concatenate([jnp.concatenate([dmod_a[:, 0:2], dmod_t[:, 2:6]], axis=1), dmod_c], axis=0)
    last = jnp.pad(jnp.concatenate([drd[:, :, 0].T.reshape(8), loss_t[0, 0:1]]), (0, D - 9)).reshape(1, D)
    small = jnp.concatenate([dg_a[0:1], dg_c[0:1], dg_t[0:3], _pad_row(dgn, 1), dpat.reshape(8, D), last], axis=0)
    return grad_x, late, early, dmod, small


def kernel(x, c, ctx, c_ctx, w_ada, b_ada, g_pre_mix, g_post_mix, g_pre_mlp, g_post_mlp, w_in, ret_decay, ret_gn, na_rpb, w_out, w_mlp1, w_mlp2, loss_target, m_c_ctx, m_w_ada, m_b_ada, m_g_pre_mix, m_g_post_mix, m_g_pre_mlp, m_g_post_mlp, m_w_in, m_ret_decay, m_ret_gn, m_na_rpb, m_w_out, m_w_mlp1, m_w_mlp2, v_c_ctx, v_w_ada, v_b_ada, v_g_pre_mix, v_g_post_mix, v_g_pre_mlp, v_g_post_mlp, v_w_in, v_ret_decay, v_ret_gn, v_na_rpb, v_w_out, v_w_mlp1, v_w_mlp2):
    px, py, pc = _place()
    dev = 4 * px + 2 * py + pc
    chip = 2 * px + py

    half_w_in = lax.dynamic_slice_in_dim(w_in[0], pc * (D // 2), D // 2, 0).astype(BF16)
    cin, mg, gw_in, bias, cos, sin, *late_halves = prologue(
        jnp.pad(c, ((0, 6), (0, 0))), c_ctx[None], w_ada[0], lax.dynamic_slice_in_dim(b_ada, chip * 1536, 1536, 1),
        _rpb_flat(na_rpb[0]), half_w_in, [w_out[0], w_mlp1[0], w_mlp2[0]])
    halves = [half_w_in] + late_halves
    wperm = unpack_w_in(gw_in.reshape(4, D, 896))
    mod_all = jnp.concatenate([mg[0], mg[2], mg[4], mg[6]], axis=1)
    mod3 = (jnp.pad(lax.dynamic_slice_in_dim(mod_all, 2 * dev, 2, 0), ((0, 1), (0, 0)))
            + jnp.pad(mod_all[16:17], ((2, 0), (0, 0)))).reshape(3, 6, D)

    place = jnp.stack([pc, chip]).astype(jnp.int32)

    early_names = ["w_out", "w_mlp1", "w_mlp2"]
    early_g8, early_partial = [], []

    def early_a(dw_out, dw1, dw2):
        early_g8[:] = [dw_out.reshape(8, 128, D), dw1.reshape(8, 512, D), dw2.reshape(8, 512, D)]
        return siblings4(early_g8)

    def early_b(landed):
        early_partial[:] = chip_partial(place, early_g8, landed, "rs_chip_sum_early")
        return chips3(early_partial)

    late_partial = []

    late_g8 = []

    def late_c(dw_in):
        late_g8[:] = [pack_w_in(dw_in).reshape(8, 512, 896)]
        return siblings4(late_g8)

    def late_d(landed):
        late_partial[:] = chip_partial(place, late_g8, landed, "rs_chip_sum_w_in")
        return chips3(late_partial)

    grad_x, (landed3_in,), early_landed, dmod, small = local_step(
        x, ctx, loss_target, mod3, (cos, sin), bias, g_pre_mix, g_post_mix, g_pre_mlp, g_post_mlp, ret_decay[0], ret_gn,
        wperm, lambda k: gather8(halves[1:2] if k == 0 else halves[2:4]), (early_a, early_b, late_c, late_d))
    early_mine = shard_sum(place, early_partial, early_landed, "rs_shard_sum_early")

    pay = jnp.concatenate([dmod.reshape(18, D), small, jnp.zeros((40 - 18 - SMALL_SUM_ROWS, D), F32)], axis=0)
    *early_theirs, gs = run_hosted(both(siblings(early_mine), gather8([pay])), "rs_halves_early_gather_small")
    gbf = gs[:, 0:12].reshape(16, 6 * D)
    gcf = gs[:, 12:18].reshape(8, 6 * D)
    gw_ada, pc_part = ada_grads(cin, lax.dynamic_slice_in_dim(gbf, chip * 1536, 1536, 1),
                                lax.dynamic_slice_in_dim(gcf, chip * 1536, 1536, 1), w_ada[0])
    (mine_in,) = shard_sum(place, late_partial, [landed3_in], "rs_shard_sum_w_in")
    theirs_in, pcg = run_hosted(both(siblings([mine_in]), gather8([pc_part])), "rs_halves_w_in_gather_c_ctx")

    grouped = adamw_group(
        place,
        [(w_mlp1[0], early_mine[1], early_theirs[1], m_w_mlp1[0], v_w_mlp1[0]),
         (w_mlp2[0], early_mine[2], early_theirs[2], m_w_mlp2[0], v_w_mlp2[0])],
        [(w_ada[0], gw_ada, m_w_ada[0], v_w_ada[0])], no_exchange(), "adamw_group")
    d_ada, m_ada, v_ada = grouped[8:11]
    big = [
        [r[None] for r in adamw_halves(place, w_in[0], mine_in, theirs_in, m_w_in[0], v_w_in[0], "adamw_w_in")],
        [r[None] for r in adamw_halves(place, w_out[0], early_mine[0], early_theirs[0], m_w_out[0], v_w_out[0],
                                       "adamw_w_out")],
        [r[None] for r in grouped[0:4]], [r[None] for r in grouped[4:8]],
    ]

    def rpb_rows(t):
        return _rpb_flat(t[0]).reshape(8, D)

    def decay_row(t):
        return jnp.pad(t.reshape(1, 8), ((0, 0), (0, D - 8)))

    views = [lambda t: t.reshape(1, D), lambda t: t, lambda t: t, lambda t: t, lambda t: t, lambda t: t, lambda t: t,
             rpb_rows, decay_row]
    back = [lambda t: t.reshape(D), lambda t: t, lambda t: t, lambda t: t, lambda t: t, lambda t: t, lambda t: t,
            lambda t: _rpb_flat_t(t)[None], lambda t: t[:, 0:8].reshape(1, 2, 4)]
    small_w = (c_ctx, b_ada, g_pre_mix, g_post_mix, g_pre_mlp, g_post_mlp, ret_gn, na_rpb, ret_decay)
    small_m = (m_c_ctx, m_b_ada, m_g_pre_mix, m_g_post_mix, m_g_pre_mlp, m_g_post_mlp, m_ret_gn, m_na_rpb, m_ret_decay)
    small_v = (v_c_ctx, v_b_ada, v_g_pre_mix, v_g_post_mix, v_g_pre_mlp, v_g_post_mlp, v_ret_gn, v_na_rpb, v_ret_decay)
    *res, loss8 = small_update(gs[:, 18:18 + SMALL_SUM_ROWS], gbf, gcf, pcg[:, 0],
                               [(f(w), f(m), f(v)) for f, w, m, v in zip(views, small_w, small_m, small_v)])

    def leaves(ada, idx):
        s_c, s_b, s_g1, s_g2, s_g3, s_g4, s_gn, s_rpb, s_rd = [back[i](res[4 * i + idx]) for i in range(9)]
        return [s_c, ada[None], s_b, s_g1, s_g2, s_g3, s_g4, big[0][idx], s_rd, s_gn, s_rpb,
                big[1][idx], big[2][idx], big[3][idx]]

    return (loss8[0, 0], grad_x, *leaves(gw_ada, 0), *leaves(d_ada, 1), *leaves(m_ada, 2), *leaves(v_ada, 3))
```

```python
import functools
import math

import jax
import jax.numpy as jnp
from jax import lax
from jax.experimental import pallas as pl
from jax.experimental.pallas import tpu as pltpu

F32, BF16 = jnp.float32, jnp.bfloat16
D = 1024
SEQ = 2048
LC = 256
GW = 64
RH, RD, CH = 4, 128, 128
NPAIR = 4
IN_W = 3584
RET_W = 2048
DFF = 4096
EPS = 1e-6
NEG = -1e30
TN = 256
NCH = SEQ // CH
LR, B1, B2, AEPS, WD, STEP = 0.001, 0.9, 0.999, 1e-08, 0.01, 10
MESH = pl.DeviceIdType.MESH
VMEM_LIMIT = 56 * 1024 * 1024


def _cp(sem=None):
    return pltpu.CompilerParams(dimension_semantics=sem, vmem_limit_bytes=VMEM_LIMIT)


def _nn(a, b):
    return jnp.dot(a.astype(BF16), b.astype(BF16), preferred_element_type=F32)


def _nt(a, b):
    return lax.dot_general(a.astype(BF16), b.astype(BF16), (((1,), (1,)), ((), ())), preferred_element_type=F32)


def _tn(a, b):
    return lax.dot_general(a.astype(BF16), b.astype(BF16), (((0,), (0,)), ((), ())), preferred_element_type=F32)


@jax.custom_vjp
def mm_tn(a, b):
    return _tn(a, b)


mm_tn.defvjp(lambda a, b: (_tn(a, b), (a, b)), lambda r, g: (_nt(r[1], g), _nn(r[0], g)))


def _rms(x):
    return x * lax.rsqrt(jnp.mean(x * x, axis=-1, keepdims=True) + EPS)


def _rms_mod(x, g, sc, sh):
    return (_rms(x) * g) * (1.0 + sc) + sh


def _post_mix(x, mix, gt1, sc2, sh2, g_post_mix, g_pre_mlp):
    x1 = x + gt1 * (_rms(mix) * g_post_mix)
    return x1, _rms_mod(x1, g_pre_mlp, sc2, sh2)


def _head_loss(x1, m, gt2, g_post_mlp, tgt):
    err = x1 + gt2 * (_rms(m) * g_post_mlp) - tgt
    return 0.5 * jnp.sum(jnp.mean(err * err, axis=-1, keepdims=True), axis=0, keepdims=True)


def _ln_gate(o, g, w):
    mu = jnp.mean(o, axis=-1, keepdims=True)
    var = jnp.mean(jnp.square(o - mu), axis=-1, keepdims=True)
    y = (o - mu) * lax.rsqrt(var + EPS)
    return (y * w) * (g * jax.nn.sigmoid(g))


def _pair_order(x):
    lane = lax.broadcasted_iota(jnp.int32, x.shape, 1)
    return jnp.where((lane >= 32) & (lane < 64), pltpu.roll(x, 96, 1),
                     jnp.where((lane >= 64) & (lane < 96), pltpu.roll(x, 32, 1), x))


def _rope(x, cos, sin):
    return x * cos + pltpu.roll(x, 64, 1) * sin


def _rope_t(g, cos, sin):
    return g * cos + pltpu.roll(g * sin, 64, 1)


def _rope_tables():
    tok = lax.broadcasted_iota(jnp.int32, (SEQ, RD), 0)
    lane = lax.broadcasted_iota(jnp.int32, (SEQ, RD), 1)
    pos = jnp.where((lane & 32) == 0, tok >> 6, tok & (GW - 1)).astype(F32)
    ang = pos * jnp.exp((lane & 31).astype(F32) * (-math.log(10000.0) / 32))
    return jnp.cos(ang), jnp.where(lane < 64, -jnp.sin(ang), jnp.sin(ang))


def _chunk_loop(n, body, init, k=4):
    def several(t, carry):
        for i in range(k):
            carry = body(k * t + i, carry)
        return carry

    return lax.fori_loop(0, n // k, several, init)


def _fiota(shape, dim):
    return lax.broadcasted_iota(jnp.int32, shape, dim).astype(F32)


def _ret_state(k, v, s, lg, reverse):
    pos = _fiota((CH, 1), 0)
    b_exp = pos if reverse else (CH - 1.0 - pos)
    return jnp.exp(lg * CH) * s + mm_tn(k * jnp.exp(lg * b_exp), v)


class _Decays:
    def __init__(self, lgs):
        i, j, pos = _fiota((CH, CH), 0), _fiota((CH, CH), 1), _fiota((CH, 1), 0)
        diffs = (i - j, j - i)
        keep = (diffs[0] >= 0, diffs[1] > 0)
        mats = [jnp.where(m, jnp.exp(lg * jnp.where(m, d, 0.0)), 0.0) for lg, d, m in zip(lgs, diffs, keep)]
        self.mask = mats[0] + mats[1]
        self.dmask = [mats[0] * diffs[0], mats[1] * diffs[1]]
        a_exp, b_exp = (pos + 1.0, CH - pos), (CH - 1.0 - pos, pos)
        self.a = [jnp.exp(lg * e) for lg, e in zip(lgs, a_exp)]
        self.b = [jnp.exp(lg * e) for lg, e in zip(lgs, b_exp)]
        self.da = [a * e for a, e in zip(self.a, a_exp)]
        self.db = [b * e for b, e in zip(self.b, b_exp)]
        self.g = [jnp.exp(lg * CH) for lg in lgs]


def _both(x, w):
    return jnp.concatenate([x * w[0], x * w[1]], axis=1)


def _total(x):
    return jnp.sum(jnp.sum(x, axis=1, keepdims=True), axis=0, keepdims=True)


def _state_pass(dec, init, k_s, v_of, st_s):
    def step(t, carry):
        out = []
        for d, s in enumerate(carry):
            n = (NCH - 1 - t) if d else t
            sl = pl.ds(pl.multiple_of(n * CH, CH), CH)
            st_s[n, d * RD:(d + 1) * RD, :] = s
            out.append(dec.g[d] * s + _tn(k_s[sl, :] * dec.b[d], v_of(sl)))
        return tuple(out)

    _chunk_loop(NCH, step, tuple(init))


def premix_proj(xin, mod3, g_pre, wperm, is_ctx, name):
    nb, length, _ = xin.shape
    tn = min(2 * TN, length)

    def body(x_ref, mod_ref, g_ref, w_ref, h_ref, pret_ref, pna_ref):
        h = _rms_mod(x_ref[...], g_ref[...], mod_ref[1:2, :], mod_ref[0:1, :])
        hb = h.astype(BF16)
        h_ref[...] = hb
        pret_ref[...] = jnp.dot(hb, w_ref[:, :RET_W], preferred_element_type=F32)
        pna_ref[...] = jnp.dot(hb, w_ref[:, RET_W:], preferred_element_type=F32).astype(BF16)

    return pl.pallas_call(
        body, name=name, grid=(nb, length // tn),
        in_specs=[
            pl.BlockSpec((None, tn, D), lambda b, t: (b, t, 0)),
            pl.BlockSpec((None, 6, D), (lambda b, t: (2, 0, 0)) if is_ctx else (lambda b, t: (b, 0, 0))),
            pl.BlockSpec((1, D), lambda b, t: (0, 0)),
            pl.BlockSpec((D, IN_W), lambda b, t: (0, 0), pipeline_mode=pl.Buffered(1)),
        ],
        out_specs=[
            pl.BlockSpec((None, tn, D), lambda b, t: (b, t, 0)),
            pl.BlockSpec((None, tn, RET_W), lambda b, t: (b, t, 0)),
            pl.BlockSpec((None, tn, IN_W - RET_W), lambda b, t: (b, t, 0)),
        ],
        out_shape=[
            jax.ShapeDtypeStruct((nb, length, D), BF16),
            jax.ShapeDtypeStruct((nb, length, RET_W), F32),
            jax.ShapeDtypeStruct((nb, length, IN_W - RET_W), BF16),
        ],
        compiler_params=_cp(("arbitrary", "arbitrary")),
    )(xin, mod3, g_pre, wperm)


def premix_bwd(xin, mod3, g_pre, wperm, dproj, dx_tail, hosted, name):
    nb, length, _ = xin.shape
    tn = min(TN, length)
    is_ctx = dx_tail is None

    def body(*refs):
        own_in, h_in, own_out, h_out, _, h_sems = hosted.split(refs, 5 if is_ctx else 6, 2 if is_ctx else 3)
        if is_ctx:
            (x_ref, mod_ref, g_ref, w_ref, dp_ref), (dmod_ref, dg_ref) = own_in, own_out
        else:
            (x_ref, mod_ref, g_ref, w_ref, dp_ref, dxt_ref), (dx_ref, dmod_ref, dg_ref) = own_in, own_out
        b, t = pl.program_id(0), pl.program_id(1)
        grid_step = b * (length // tn) + t

        @pl.when(grid_step == 0)
        def _():
            hosted.start(h_in, h_out, h_sems)

        @pl.when(grid_step == nb * (length // tn) - 1)
        def _():
            hosted.finish(h_in, h_out, h_sems)

        dh = lax.dot_general(dp_ref[...], w_ref[...], (((1,), (1,)), ((), ())), preferred_element_type=F32)
        _, vjp = jax.vjp(_rms_mod, x_ref[...], g_ref[...], mod_ref[1:2, :], mod_ref[0:1, :])
        dx, dg, dsc, dsh = vjp(dh)
        if not is_ctx:
            dx_ref[...] = dx + dxt_ref[...]

        @pl.when((t == 0) & ((b == 0) if is_ctx else True))
        def _():
            dmod_ref[...] = jnp.zeros_like(dmod_ref)

        @pl.when((t == 0) & (b == 0))
        def _():
            dg_ref[...] = jnp.zeros_like(dg_ref)

        dmod_ref[0:1, :] += dsh
        dmod_ref[1:2, :] += dsc
        dg_ref[0:1, :] += dg

    tok = lambda b, t: (b, t, 0)
    in_specs = [
        pl.BlockSpec((None, tn, D), tok),
        pl.BlockSpec((None, 6, D), (lambda b, t: (2, 0, 0)) if is_ctx else (lambda b, t: (b, 0, 0))),
        pl.BlockSpec((1, D), lambda b, t: (0, 0)),
        pl.BlockSpec((D, IN_W), lambda b, t: (0, 0), pipeline_mode=pl.Buffered(1)),
        pl.BlockSpec((None, tn, IN_W), tok),
    ]
    args = [xin, mod3, g_pre, wperm, dproj]
    out_specs = [
        pl.BlockSpec((None, 6, D), (lambda b, t: (0, 0, 0)) if is_ctx else (lambda b, t: (b, 0, 0))),
        pl.BlockSpec((8, D), lambda b, t: (0, 0)),
    ]
    out_shape = [jax.ShapeDtypeStruct((1 if is_ctx else nb, 6, D), F32), jax.ShapeDtypeStruct((8, D), F32)]
    if not is_ctx:
        in_specs.append(pl.BlockSpec((None, tn, D), tok))
        args.append(dx_tail)
        out_specs.insert(0, pl.BlockSpec((None, tn, D), tok))
        out_shape.insert(0, jax.ShapeDtypeStruct((nb, length, D), F32))
    h_in_specs, h_out_specs = hosted.specs()
    return pl.pallas_call(
        body, name=name, grid=(nb, length // tn), in_specs=in_specs + h_in_specs, out_specs=out_specs + h_out_specs,
        out_shape=out_shape + hosted.out_shape, scratch_shapes=hosted.scratch,
        compiler_params=_cp(("arbitrary", "arbitrary")),
    )(*args, *hosted.args)


def _ret_specs(order):
    def im(f):
        return lambda *g: f(*order(*g))
    return dict(
        pret=pl.BlockSpec((None, SEQ, 512), im(lambda b, h: (b, 0, h))),
        pretc=pl.BlockSpec((None, LC, 512), im(lambda b, h: (b, 0, h))),
        rd=pl.BlockSpec((None, 2, 1), im(lambda b, h: (h, 0, 0))),
        gn=pl.BlockSpec((None, 1, RD), im(lambda b, h: (h, 0, 0))),
        tab=pl.BlockSpec((SEQ, RD), im(lambda b, h: (0, 0))),
        head=pl.BlockSpec((None, SEQ, RD), im(lambda b, h: (b, 0, h))),
    )


def retention_fwd(pret, pretc, rd, gn, cos, sin, hosted):
    nb = pret.shape[0]
    sp = _ret_specs(lambda b, h: (b, h))

    def body(*refs):
        own_in, h_in, own_out, h_out, own_scr, h_sems = hosted.split(refs, 6, 2)
        p_ref, pc_ref, rd_ref, gn_ref, cos_ref, sin_ref = own_in
        (o_ref, mix_ref), (q_s, k_s, o_s, st_s) = own_out, own_scr
        grid_step = pl.program_id(0) * RH + pl.program_id(1)

        @pl.when(grid_step == 0)
        def _():
            hosted.start(h_in, h_out, h_sems)

        cos_v, sin_v = cos_ref[...], sin_ref[...]
        q_s[...] = _rope(p_ref[:, 0:128], cos_v, sin_v) * (RD ** -0.5)
        k_s[...] = _rope(p_ref[:, 128:256], cos_v, sin_v)
        lgs, init = [], []
        for rev in (False, True):
            lg = jax.nn.log_sigmoid(rd_ref[int(rev):int(rev) + 1, :])
            s = jnp.zeros((RD, RD), F32)
            for n in ((1, 0) if rev else (0, 1)):
                s = _ret_state(pc_ref[n * CH:(n + 1) * CH, 128:256], pc_ref[n * CH:(n + 1) * CH, 256:384], s, lg, rev)
            lgs.append(lg)
            init.append(s)

        dec = _Decays(lgs)
        _state_pass(dec, init, k_s, lambda sl: p_ref[sl, 256:384], st_s)

        def chunk(n, carry):
            sl = pl.ds(pl.multiple_of(n * CH, CH), CH)
            q = q_s[sl, :]
            o_s[sl, :] = (_nn(_nt(q, k_s[sl, :]) * dec.mask, p_ref[sl, 256:384]) + _nn(_both(q, dec.a), st_s[n]))
            return carry

        _chunk_loop(NCH, chunk, 0)
        o = o_s[...]
        o_ref[...] = o
        mix_ref[...] = _ln_gate(o, p_ref[:, 384:512], gn_ref[...]).astype(BF16)

        @pl.when(grid_step == nb * RH - 1)
        def _():
            hosted.finish(h_in, h_out, h_sems)

    h_in_specs, h_out_specs = hosted.specs()
    return pl.pallas_call(
        body, name="retention_fwd", grid=(nb, RH),
        in_specs=[sp["pret"], sp["pretc"], sp["rd"], sp["gn"], sp["tab"], sp["tab"]] + h_in_specs,
        out_specs=[sp["head"], sp["head"]] + h_out_specs,
        out_shape=[jax.ShapeDtypeStruct((nb, SEQ, RH * RD), F32), jax.ShapeDtypeStruct((nb, SEQ, D), BF16)]
        + hosted.out_shape,
        scratch_shapes=[pltpu.VMEM((SEQ, RD), F32)] * 3 + [pltpu.VMEM((NCH, 2 * RD, RD), F32)] + hosted.scratch,
        compiler_params=_cp(("arbitrary", "arbitrary")),
    )(pret, pretc, rd, gn, cos, sin, *hosted.args)


def retention_bwd(pret, pretc, o_all, dmixin, rd, gn, cos, sin, hosted):
    nb = pret.shape[0]
    sp = _ret_specs(lambda h, b: (b, h))

    def body(*refs):
        own_in, h_in, own_out, h_out, own_scr, h_sems = hosted.split(refs, 8, 4)
        p_ref, pc_ref, o_ref, dmix_ref, rd_ref, gn_ref, cos_ref, sin_ref = own_in
        dp_ref, dpc_ref, drd_ref, dgn_ref = own_out
        q_s, k_s, do_s, dq_s, dk_s, dv_s, st_s, gst_s = own_scr
        b = pl.program_id(1)
        grid_step = pl.program_id(0) * nb + b

        @pl.when(grid_step == 0)
        def _():
            hosted.start(h_in, h_out, h_sems)

        cos_v, sin_v = cos_ref[...], sin_ref[...]
        q_s[...] = _rope(p_ref[:, 0:128], cos_v, sin_v) * (RD ** -0.5)
        k_s[...] = _rope(p_ref[:, 128:256], cos_v, sin_v)
        _, gate_vjp = jax.vjp(_ln_gate, o_ref[...], p_ref[:, 384:512], gn_ref[...])
        do, dg, dgn = gate_vjp(dmix_ref[...].astype(F32))
        do_s[...] = do
        dp_ref[:, 384:512] = dg.astype(BF16)

        @pl.when(b == 0)
        def _():
            drd_ref[...] = jnp.zeros_like(drd_ref)
            dgn_ref[...] = jnp.zeros_like(dgn_ref)

        dgn_ref[...] += dgn
        kcs = [pc_ref[n * CH:(n + 1) * CH, 128:256] for n in (0, 1)]
        vcs = [pc_ref[n * CH:(n + 1) * CH, 256:384] for n in (0, 1)]
        dirs = []
        init = []
        for rev in (False, True):
            rdv = rd_ref[int(rev):int(rev) + 1, :]
            lg = jax.nn.log_sigmoid(rdv)
            order_c = (1, 0) if rev else (0, 1)
            s = jnp.zeros((RD, RD), F32)
            ctx_states = []
            for n in order_c:
                ctx_states.append(s)
                s = _ret_state(kcs[n], vcs[n], s, lg, rev)
            dirs.append((rev, order_c, lg, rdv, ctx_states))
            init.append(s)
        dec = _Decays([lg for _, _, lg, _, _ in dirs])

        def v_of(sl):
            return p_ref[sl, 256:384]

        _state_pass(dec, init, k_s, v_of, st_s)
        zeros = jnp.zeros((CH, RD), F32)

        def scores_back(n, carry):
            dmask_sum, da_f, da_b = carry
            sl = pl.ds(pl.multiple_of(n * CH, CH), CH)
            q, k, v, do = q_s[sl, :], k_s[sl, :], v_of(sl), do_s[sl, :]
            scores = _nt(q, k)
            d_att = _nt(do, v)
            d_scores = d_att * dec.mask
            d_qa = _nt(do, st_s[n])
            d_qf, d_qb = d_qa[:, 0:RD], d_qa[:, RD:2 * RD]
            dq_s[sl, :] = _nn(d_scores, k) + d_qf * dec.a[0] + d_qb * dec.a[1]
            dk_s[sl, :] = _tn(d_scores, q)
            dv_s[sl, :] = _tn(scores * dec.mask, do)
            gst_s[n] = _tn(_both(q, dec.a), do)
            return dmask_sum + d_att * scores, da_f + d_qf * q, da_b + d_qb * q

        dmask_sum, da_f, da_b = _chunk_loop(NCH, scores_back, (zeros, zeros, zeros))

        def state_back(t, carry):
            out = []
            for d, r in enumerate(carry):
                n = t if d else (NCH - 1 - t)
                rows = slice(d * RD, (d + 1) * RD)
                own = gst_s[n, rows, :]
                gst_s[n, rows, :] = r
                out.append(own + dec.g[d] * r)
            return tuple(out)

        d_states = _chunk_loop(NCH, state_back, (zeros, zeros))

        def updates_back(n, carry):
            db_f, db_b, dg_f, dg_b = carry
            sl = pl.ds(pl.multiple_of(n * CH, CH), CH)
            k, r, s = k_s[sl, :], gst_s[n], st_s[n]
            d_kw = _nt(v_of(sl), r)
            d_kf, d_kb = d_kw[:, 0:RD], d_kw[:, RD:2 * RD]
            dk_s[sl, :] += d_kf * dec.b[0] + d_kb * dec.b[1]
            dv_s[sl, :] += _nn(_both(k, dec.b), r)
            return (db_f + d_kf * k, db_b + d_kb * k, dg_f + r[0:RD, :] * s[0:RD, :],
                    dg_b + r[RD:2 * RD, :] * s[RD:2 * RD, :])

        db_dg = _chunk_loop(NCH, updates_back, (zeros, zeros, zeros, zeros))
        dkc = [None, None]
        dvc = [None, None]
        for d, ((rev, order_c, lg, rdv, ctx_states), ds) in enumerate(zip(dirs, d_states)):
            dlg = (_total(dmask_sum * dec.dmask[d]) + _total((da_f, da_b)[d] * dec.da[d])
                   + _total(db_dg[d] * dec.db[d]) + CH * dec.g[d] * _total(db_dg[2 + d]))
            for idx in (1, 0):
                n = order_c[idx]
                _, vjp = jax.vjp(functools.partial(_ret_state, reverse=rev), kcs[n], vcs[n], ctx_states[idx], lg)
                dk_c, dv_c, ds, dl = vjp(ds)
                dlg = dlg + dl
                dkc[n] = dk_c if dkc[n] is None else dkc[n] + dk_c
                dvc[n] = dv_c if dvc[n] is None else dvc[n] + dv_c
            drd_ref[int(rev):int(rev) + 1, :] += dlg * jax.nn.sigmoid(-rdv)
        dp_ref[:, 0:128] = _rope_t(dq_s[...] * (RD ** -0.5), cos_v, sin_v).astype(BF16)
        dp_ref[:, 128:256] = _rope_t(dk_s[...], cos_v, sin_v).astype(BF16)
        dp_ref[:, 256:384] = dv_s[...].astype(BF16)
        zero = jnp.zeros((CH, RD), BF16)
        for n in (0, 1):
            rows = slice(n * CH, (n + 1) * CH)
            dpc_ref[rows, 0:128] = zero
            dpc_ref[rows, 128:256] = dkc[n].astype(BF16)
            dpc_ref[rows, 256:384] = dvc[n].astype(BF16)
            dpc_ref[rows, 384:512] = zero

        @pl.when(grid_step == RH * nb - 1)
        def _():
            hosted.finish(h_in, h_out, h_sems)

    h_in_specs, h_out_specs = hosted.specs()
    return pl.pallas_call(
        body, name="retention_bwd", grid=(RH, nb),
        in_specs=[sp["pret"], sp["pretc"], sp["head"], sp["head"], sp["rd"], sp["gn"], sp["tab"], sp["tab"]]
        + h_in_specs,
        out_specs=[
            pl.BlockSpec((None, SEQ, 512), lambda h, b: (b, 0, h)),
            pl.BlockSpec((None, LC, 512), lambda h, b: (b, 0, h)),
            pl.BlockSpec((None, 2, 1), lambda h, b: (h, 0, 0)),
            pl.BlockSpec((None, 1, RD), lambda h, b: (h, 0, 0)),
        ] + h_out_specs,
        out_shape=[
            jax.ShapeDtypeStruct((nb, SEQ, IN_W), BF16),
            jax.ShapeDtypeStruct((nb, LC, IN_W), BF16),
            jax.ShapeDtypeStruct((RH, 2, 1), F32),
            jax.ShapeDtypeStruct((RH, 1, RD), F32),
        ] + hosted.out_shape,
        scratch_shapes=[pltpu.VMEM((SEQ, RD), F32)] * 6 + [pltpu.VMEM((NCH, 2 * RD, RD), F32)] * 2 + hosted.scratch,
        compiler_params=_cp(("arbitrary", "arbitrary")),
    )(pret, pretc, o_all, dmixin, rd, gn, cos, sin, *hosted.args)


def _rpb_flat(rpb):
    return jnp.pad(rpb, ((0, 0), (0, 1), (0, 33))).reshape(NPAIR, 2, 1, 1024)


def _rpb_flat_t(dflat):
    return dflat.reshape(8, 16, 64)[:, :15, :31]


def _barrel(x, left):
    row = lax.broadcasted_iota(jnp.int32, x.shape, 0)
    n = x.shape[1]
    for bit in range(6):
        s = 1 << bit
        x = jnp.where(((row >> bit) & 1) == 1, pltpu.roll(x, (n - s) if left else s, 1), x)
    return x


NA_TILE_ROWS, NA_BAND_ROWS = 4, 12
NA_Q, NA_K = NA_TILE_ROWS * GW, NA_BAND_ROWS * GW
NA_TILES = SEQ // NA_Q


def _band_start(r0):
    return min(max(r0 - 4, 0), 32 - NA_BAND_ROWS)


def _tile_layout(t):
    rows = range(t * NA_TILE_ROWS, (t + 1) * NA_TILE_ROWS)
    return tuple((r if r < 4 else (r - 24 if r > 28 else 4), min(max(r - 4, 0), 24) - _band_start(rows[0]))
                 for r in rows)


NA_CLASSES = sorted(set(_tile_layout(t) for t in range(NA_TILES)))


def _tile_rows(cls):
    return NA_CLASSES[cls]


def _na_tile(t):
    start = jnp.clip(NA_TILE_ROWS * t - 4, 0, 32 - NA_BAND_ROWS)
    cls = 0
    for tile in range(NA_TILES):
        cls = jnp.where(t == tile, NA_CLASSES.index(_tile_layout(tile)), cls)
    return pl.ds(pl.multiple_of(t * NA_Q, NA_Q), NA_Q), pl.ds(pl.multiple_of(start * GW, NA_Q), NA_K), cls


def _na_probs(qst, kb, kc, bias):
    s_loc = _nt(qst, kb) + bias
    s_ctx = _nt(qst, kc)
    m = jnp.maximum(jnp.max(s_loc, axis=1, keepdims=True), jnp.max(s_ctx, axis=1, keepdims=True))
    e_loc, e_ctx = jnp.exp(s_loc - m), jnp.exp(s_ctx - m)
    den = jnp.sum(e_loc, axis=1, keepdims=True) + jnp.sum(e_ctx, axis=1, keepdims=True)
    return e_loc / den, e_ctx / den


def _stack_heads(t):
    lane = lax.broadcasted_iota(jnp.int32, t.shape, 1)
    zero = jnp.zeros_like(t)
    return jnp.concatenate([jnp.where(lane < 64, t, zero), jnp.where(lane >= 64, t, zero)], axis=0)


def _unstack_heads(t):
    n = t.shape[0] // 2
    lane = lax.broadcasted_iota(jnp.int32, (n, 128), 1)
    return jnp.where(lane < 64, t[:n], t[n:])


NA_BIAS_SHAPE = (len(NA_CLASSES), 2 * NA_Q, NA_K)


def _na_bias_pair(flat_ref, out_ref):
    qc = lax.broadcasted_iota(jnp.int32, (GW, 512), 0)
    kc = lax.broadcasted_iota(jnp.int32, (GW, 512), 1) & 63
    start = jnp.clip(qc - 8, 0, GW - 16)
    window = (kc >= start) & (kc < start + 16)
    fill = jnp.full((GW, NA_K - 512), NEG, F32)
    for hh in (0, 1):
        skew = _barrel(pltpu.roll(jnp.broadcast_to(flat_ref[hh], (GW, 1024)), 1024 - 15, 1), left=False)
        by_class = [jnp.where(window, (skew if rc == 7 else pltpu.roll(skew, (9 + rc) * 64, 1))[:, 0:512], NEG)
                    for rc in range(8)]
        for cls in range(len(NA_CLASSES)):
            for qr, (rc, off) in enumerate(_tile_rows(cls)):
                w = jnp.concatenate([by_class[rc], fill], axis=1)
                rows = slice(hh * NA_Q + qr * GW, hh * NA_Q + (qr + 1) * GW)
                out_ref[cls, rows, :] = pltpu.roll(w, off * GW, 1) if off else w


def na_fwd(pna, pnac, bias, mixin, hosted):
    nb = pna.shape[0]

    def body(*refs):
        (p_ref, pc_ref, bias_ref, _), h_in, (out_ref,), h_out, _, h_sems = hosted.split(refs, 4, 1)
        grid_step = pl.program_id(0) * nb + pl.program_id(1)

        @pl.when(grid_step == 0)
        def _():
            hosted.start(h_in, h_out, h_sems)

        kc, vc = pc_ref[:, 128:256], pc_ref[:, 256:384]

        def tile(t, carry):
            qsl, bsl, cls = _na_tile(t)
            kb, vb = p_ref[bsl, 128:256], p_ref[bsl, 256:384]
            p_loc, p_ctx = _na_probs(_stack_heads(p_ref[qsl, 0:128] * 0.125), kb, kc, bias_ref[cls])
            out_ref[qsl, :] = _unstack_heads(_nn(p_loc, vb) + _nn(p_ctx, vc)).astype(BF16)
            return carry

        lax.fori_loop(0, NA_TILES, tile, 0, unroll=4)

        @pl.when(grid_step == NPAIR * nb - 1)
        def _():
            hosted.finish(h_in, h_out, h_sems)

    h_in_specs, h_out_specs = hosted.specs()
    return pl.pallas_call(
        body, name="na_fwd", grid=(NPAIR, nb),
        in_specs=[
            pl.BlockSpec((None, SEQ, 384), lambda p, b: (b, 0, p)),
            pl.BlockSpec((None, LC, 384), lambda p, b: (b, 0, p)),
            pl.BlockSpec((None, len(NA_CLASSES), 2 * NA_Q, NA_K), lambda p, b: (p, 0, 0, 0)),
            pl.BlockSpec(memory_space=pl.ANY),
        ] + h_in_specs,
        out_specs=[pl.BlockSpec((None, SEQ, 128), lambda p, b: (b, 0, 4 + p))] + h_out_specs,
        out_shape=[jax.ShapeDtypeStruct((nb, SEQ, D), BF16)] + hosted.out_shape,
        input_output_aliases={3: 0},
        scratch_shapes=hosted.scratch,
        compiler_params=_cp(("arbitrary", "arbitrary")),
    )(pna, pnac, bias, mixin, *hosted.args)


def na_bwd(pna, pnac, bias, dmixin, dproj, dprojc, hosted):
    nb = pna.shape[0]

    def body(*refs):
        own_in, h_in, own_out, h_out, own_scr, h_sems = hosted.split(refs, 6, 3)
        p_ref, pc_ref, bias_ref, dmix_ref = own_in[:4]
        dp_ref, dpc_ref, dpat_ref = own_out
        dbias_s, dk_s, dv_s, dkc_s, dvc_s, res_s, resc_s = own_scr
        b, part = pl.program_id(1), pl.program_id(2)
        grid_step = (pl.program_id(0) * nb + b) * 3 + part

        @pl.when(grid_step == 0)
        def _():
            hosted.start(h_in, h_out, h_sems)

        @pl.when(grid_step == NPAIR * nb * 3 - 1)
        def _():
            hosted.finish(h_in, h_out, h_sems)

        @pl.when(part == 0)
        def _():
            @pl.when(b == 0)
            def _():
                dbias_s[...] = jnp.zeros_like(dbias_s)

            dk_s[...] = jnp.zeros_like(dk_s)
            dv_s[...] = jnp.zeros_like(dv_s)
            dkc_s[...] = jnp.zeros_like(dkc_s)
            dvc_s[...] = jnp.zeros_like(dvc_s)
            kc, vc = pc_ref[:, 128:256], pc_ref[:, 256:384]

            def tile(t, carry):
                qsl, bsl, cls = _na_tile(t)
                kb, vb = p_ref[bsl, 128:256], p_ref[bsl, 256:384]
                qst, dost = _stack_heads(p_ref[qsl, 0:128] * 0.125), _stack_heads(dmix_ref[qsl, :])
                p_loc, p_ctx = _na_probs(qst, kb, kc, bias_ref[cls])
                dp_loc, dp_ctx = _nt(dost, vb), _nt(dost, vc)
                delta = (jnp.sum(p_loc * dp_loc, axis=1, keepdims=True)
                         + jnp.sum(p_ctx * dp_ctx, axis=1, keepdims=True))
                ds_loc, ds_ctx = p_loc * (dp_loc - delta), p_ctx * (dp_ctx - delta)
                dbias_s[cls] += ds_loc
                res_s[0, qsl, :] = _unstack_heads((_nn(ds_loc, kb) + _nn(ds_ctx, kc)) * 0.125).astype(BF16)
                dk_s[bsl, :] += _tn(ds_loc, qst)
                dv_s[bsl, :] += _tn(p_loc, dost)
                dkc_s[...] += _tn(ds_ctx, qst)
                dvc_s[...] += _tn(p_ctx, dost)
                return carry

            lax.fori_loop(0, NA_TILES, tile, 0, unroll=2)
            res_s[1] = dk_s[...].astype(BF16)
            res_s[2] = dv_s[...].astype(BF16)
            resc_s[0] = jnp.zeros((LC, 128), BF16)
            resc_s[1] = dkc_s[...].astype(BF16)
            resc_s[2] = dvc_s[...].astype(BF16)

            @pl.when(b == nb - 1)
            def _():
                for hh in (0, 1):
                    by_class = [None] * 8
                    for cls in range(len(NA_CLASSES)):
                        for qr, (rc, off) in enumerate(_tile_rows(cls)):
                            w = dbias_s[cls, hh * NA_Q + qr * GW:hh * NA_Q + (qr + 1) * GW, :]
                            w = (pltpu.roll(w, NA_K - off * GW, 1) if off else w)[:, 0:512]
                            by_class[rc] = w if by_class[rc] is None else by_class[rc] + w
                    skew = jnp.zeros((GW, 1024), F32)
                    for rc in range(8):
                        w = jnp.concatenate([by_class[rc], jnp.zeros((GW, 512), F32)], axis=1)
                        skew = skew + (w if rc == 7 else pltpu.roll(w, (7 - rc) * 64, 1))
                    dpat_ref[hh] = jnp.sum(pltpu.roll(_barrel(skew, left=True), 15, 1), axis=0, keepdims=True)

        dp_ref[...] = res_s[part]
        dpc_ref[...] = resc_s[part]

    h_in_specs, h_out_specs = hosted.specs()
    return pl.pallas_call(
        body, name="na_bwd", grid=(NPAIR, nb, 3),
        in_specs=[
            pl.BlockSpec((None, SEQ, 384), lambda p, b, s: (b, 0, p)),
            pl.BlockSpec((None, LC, 384), lambda p, b, s: (b, 0, p)),
            pl.BlockSpec((None, len(NA_CLASSES), 2 * NA_Q, NA_K), lambda p, b, s: (p, 0, 0, 0)),
            pl.BlockSpec((None, SEQ, 128), lambda p, b, s: (b, 0, 4 + p)),
            pl.BlockSpec(memory_space=pl.ANY),
            pl.BlockSpec(memory_space=pl.ANY),
        ] + h_in_specs,
        out_specs=[
            pl.BlockSpec((None, SEQ, 128), lambda p, b, s: (b, 0, 16 + 3 * p + s)),
            pl.BlockSpec((None, LC, 128), lambda p, b, s: (b, 0, 16 + 3 * p + s)),
            pl.BlockSpec((None, 2, 1, 1024), lambda p, b, s: (p, 0, 0, 0)),
        ] + h_out_specs,
        out_shape=[
            jax.ShapeDtypeStruct((nb, SEQ, IN_W), BF16),
            jax.ShapeDtypeStruct((nb, LC, IN_W), BF16),
            jax.ShapeDtypeStruct((NPAIR, 2, 1, 1024), F32),
        ] + hosted.out_shape,
        input_output_aliases={4: 0, 5: 1},
        scratch_shapes=[
            pltpu.VMEM((len(NA_CLASSES), 2 * NA_Q, NA_K), F32),
            pltpu.VMEM((SEQ, 128), F32), pltpu.VMEM((SEQ, 128), F32),
            pltpu.VMEM((LC, 128), F32), pltpu.VMEM((LC, 128), F32),
            pltpu.VMEM((3, SEQ, 128), BF16), pltpu.VMEM((3, LC, 128), BF16),
        ] + hosted.scratch,
        compiler_params=_cp(("arbitrary", "arbitrary", "arbitrary")),
    )(pna, pnac, bias, dmixin, dproj, dprojc, *hosted.args)


def tail_fwd_bwd(x, mixin, tgt, mod3, g_post_mix, g_pre_mlp, g_post_mlp, wout, w1, w2):
    nb = x.shape[0]

    def body(x_ref, mi_ref, tgt_ref, mod_ref, gpm_ref, gpl_ref, gpo_ref, wo_ref, w1_ref, w2_ref,
             dx_ref, dmix_ref, h2_ref, du_ref, a_ref, dm_ref, dmi_ref, dmod_ref, dg_ref, loss_ref):
        b, t = pl.program_id(0), pl.program_id(1)
        gt1, sh2, sc2, gt2 = mod_ref[2:3, :], mod_ref[3:4, :], mod_ref[4:5, :], mod_ref[5:6, :]
        mix = jnp.dot(mi_ref[...], wo_ref[...], preferred_element_type=F32)
        (x1, h2), vjp_a = jax.vjp(_post_mix, x_ref[...], mix, gt1, sc2, sh2, gpm_ref[...], gpl_ref[...])
        h2b = h2.astype(BF16)
        h2_ref[...] = h2b
        m = jnp.zeros((TN, D), F32)
        relus = []
        for j in range(4):
            cols = slice(j * D, (j + 1) * D)
            r = jnp.maximum(jnp.dot(h2b, w1_ref[j], preferred_element_type=F32), 0.0)
            ab = (r * r).astype(BF16)
            a_ref[:, cols] = ab
            m = m + jnp.dot(ab, w2_ref[cols, :], preferred_element_type=F32)
            relus.append(r)
        loss, vjp_b = jax.vjp(_head_loss, x1, m, gt2, gpo_ref[...], tgt_ref[...])
        dx1, dm, dgt2, dgpo, _ = vjp_b(jnp.ones((1, 1), F32))
        dmb = dm.astype(BF16)
        dm_ref[...] = dmb
        dh2 = jnp.zeros((TN, D), F32)
        for j in range(4):
            cols = slice(j * D, (j + 1) * D)
            da = lax.dot_general(dmb, w2_ref[cols, :], (((1,), (1,)), ((), ())), preferred_element_type=F32)
            dub = (da * (2.0 * relus[j])).astype(BF16)
            du_ref[:, cols] = dub
            dh2 = dh2 + lax.dot_general(dub, w1_ref[j], (((1,), (1,)), ((), ())), preferred_element_type=F32)
        dx, dmix, dgt1, dsc2, dsh2, dgpm, dgpl = vjp_a((dx1, dh2))
        dx_ref[...] = dx
        dmixb = dmix.astype(BF16)
        dmix_ref[...] = dmixb
        dmi_ref[...] = lax.dot_general(dmixb, wo_ref[...], (((1,), (1,)), ((), ())),
                                       preferred_element_type=F32).astype(BF16)

        @pl.when(t == 0)
        def _():
            dmod_ref[...] = jnp.zeros_like(dmod_ref)

        @pl.when((t == 0) & (b == 0))
        def _():
            dg_ref[...] = jnp.zeros_like(dg_ref)
            loss_ref[...] = jnp.zeros_like(loss_ref)

        dmod_ref[2:3, :] += dgt1
        dmod_ref[3:4, :] += dsh2
        dmod_ref[4:5, :] += dsc2
        dmod_ref[5:6, :] += dgt2
        dg_ref[0:1, :] += dgpm
        dg_ref[1:2, :] += dgpl
        dg_ref[2:3, :] += dgpo
        loss_ref[...] += jnp.broadcast_to(loss, loss_ref.shape)

    tok = lambda b, t: (b, t, 0)
    const = lambda b, t: (0, 0)
    vec = pl.BlockSpec((1, D), const)
    return pl.pallas_call(
        body, name="tail_fwd_bwd", grid=(nb, SEQ // TN),
        in_specs=[
            pl.BlockSpec((None, TN, D), tok), pl.BlockSpec((None, TN, D), tok), pl.BlockSpec((None, TN, D), tok),
            pl.BlockSpec((None, 6, D), lambda b, t: (b, 0, 0)), vec, vec, vec,
            pl.BlockSpec((D, D), const, pipeline_mode=pl.Buffered(1)),
            pl.BlockSpec((4, D, D), lambda b, t: (0, 0, 0), pipeline_mode=pl.Buffered(1)),
            pl.BlockSpec((DFF, D), const, pipeline_mode=pl.Buffered(1)),
        ],
        out_specs=[
            pl.BlockSpec((None, TN, D), tok), pl.BlockSpec((None, TN, D), tok), pl.BlockSpec((None, TN, D), tok),
            pl.BlockSpec((None, TN, DFF), tok), pl.BlockSpec((None, TN, DFF), tok), pl.BlockSpec((None, TN, D), tok),
            pl.BlockSpec((None, TN, D), tok),
            pl.BlockSpec((None, 6, D), lambda b, t: (b, 0, 0)),
            pl.BlockSpec((8, D), const), pl.BlockSpec((8, 128), const),
        ],
        out_shape=[
            jax.ShapeDtypeStruct((nb, SEQ, D), F32), jax.ShapeDtypeStruct((nb, SEQ, D), BF16),
            jax.ShapeDtypeStruct((nb, SEQ, D), BF16), jax.ShapeDtypeStruct((nb, SEQ, DFF), BF16),
            jax.ShapeDtypeStruct((nb, SEQ, DFF), BF16), jax.ShapeDtypeStruct((nb, SEQ, D), BF16),
            jax.ShapeDtypeStruct((nb, SEQ, D), BF16),
            jax.ShapeDtypeStruct((nb, 6, D), F32), jax.ShapeDtypeStruct((8, D), F32),
            jax.ShapeDtypeStruct((8, 128), F32),
        ],
        compiler_params=_cp(("arbitrary", "arbitrary")),
    )(x, mixin, tgt, mod3, g_post_mix, g_pre_mlp, g_post_mlp, wout, w1, w2)


def weight_grad(pairs, name, out_dtype=F32, col_blocks=False, tm=1024, tn=1024, tk=2048):
    m, n = pairs[0][0].shape[1], pairs[0][1].shape[1]
    tn = min(tn, n)
    tks = [min(tk, xa.shape[0]) for xa, _ in pairs]
    steps = [xa.shape[0] // t for (xa, _), t in zip(pairs, tks)]
    total = sum(steps)
    offs = [sum(steps[:i]) for i in range(len(pairs))]

    def body(*refs):
        out_ref, acc = refs[2 * len(pairs)], refs[-1]
        k = pl.program_id(2)

        @pl.when(k == 0)
        def _():
            acc[...] = jnp.zeros_like(acc)

        for i in range(len(pairs)):
            @pl.when((k >= offs[i]) & (k < offs[i] + steps[i]))
            def _(i=i):
                acc[...] += lax.dot_general(refs[2 * i][...], refs[2 * i + 1][...], (((0,), (0,)), ((), ())),
                                            preferred_element_type=F32)

        if out_dtype != F32:
            @pl.when(k == total - 1)
            def _():
                out_ref[...] = acc[...].astype(out_dtype)

    in_specs, args = [], []
    for i, (xa, ya) in enumerate(pairs):
        clamp = lambda k, i=i: jnp.clip(k - offs[i], 0, steps[i] - 1)
        in_specs.append(pl.BlockSpec((tks[i], tm), lambda a, c, k, clamp=clamp: (clamp(k), a)))
        in_specs.append(pl.BlockSpec((tks[i], tn), lambda a, c, k, clamp=clamp: (clamp(k), c)))
        args += [xa, ya]
    if col_blocks:
        out_spec = pl.BlockSpec((None, tm, tn), lambda a, c, k: (c, a, 0))
        out_shape = jax.ShapeDtypeStruct((n // tn, m, tn), out_dtype)
    else:
        out_spec = pl.BlockSpec((tm, tn), lambda a, c, k: (a, c))
        out_shape = jax.ShapeDtypeStruct((m, n), out_dtype)
    return pl.pallas_call(
        body, name=name, grid=(m // tm, n // tn, total), in_specs=in_specs, out_specs=out_spec, out_shape=out_shape,
        scratch_shapes=[] if out_dtype == F32 else [pltpu.VMEM((tm, tn), F32)],
        compiler_params=_cp(("arbitrary", "arbitrary", "arbitrary")),
    )(*args)


def _perm_block(t):
    return 4 * (t % 4) + t // 4 if t < 16 else 16 + 3 * ((t - 16) % 4) + (t - 16) // 4


def _is_rope_block(p):
    return p < 16 and p % 4 < 2


def unpack_w_in(blocks):
    def body(i_ref, o_ref):
        for t in range(28):
            p = _perm_block(t)
            blk = i_ref[t // 7, :, (t % 7) * 128:(t % 7 + 1) * 128]
            if _is_rope_block(p):
                blk = _pair_order(blk.astype(F32)).astype(BF16)
            o_ref[:, p * 128:(p + 1) * 128] = blk

    return pl.pallas_call(
        body, name="unpack_w_in", grid=(2,),
        in_specs=[pl.BlockSpec((4, D // 2, 896), lambda i: (0, i, 0))],
        out_specs=pl.BlockSpec((D // 2, IN_W), lambda i: (i, 0)),
        out_shape=jax.ShapeDtypeStruct((D, IN_W), BF16),
    )(blocks)


def pack_w_in(dw):
    def body(i_ref, o_ref):
        for t in range(28):
            p = _perm_block(t)
            blk = i_ref[:, p * 128:(p + 1) * 128]
            if _is_rope_block(p):
                blk = _pair_order(blk)
            o_ref[t // 7, :, (t % 7) * 128:(t % 7 + 1) * 128] = blk.astype(BF16)

    return pl.pallas_call(
        body, name="pack_w_in", grid=(4,),
        in_specs=[pl.BlockSpec((D // 4, IN_W), lambda i: (i, 0))],
        out_specs=pl.BlockSpec((4, D // 4, 896), lambda i: (0, i, 0)),
        out_shape=jax.ShapeDtypeStruct((4, D, 896), BF16),
    )(dw)


def _place():
    return lax.axis_index("x"), lax.axis_index("y"), lax.axis_index("c")


class Hosted:
    def __init__(self, args, out_shape, scratch, start, finish):
        self.args, self.out_shape, self.scratch, self.start, self.finish = args, out_shape, scratch, start, finish

    def specs(self):
        hbm = pl.BlockSpec(memory_space=pl.ANY)
        return [hbm] * len(self.args), [hbm] * len(self.out_shape)

    def split(self, refs, n_in, n_out):
        a, b = len(self.args), len(self.out_shape)
        cuts = [n_in, n_in + a, n_in + a + n_out, n_in + a + n_out + b, len(refs) - len(self.scratch)]
        parts = [refs[i:j] for i, j in zip([0] + cuts, cuts + [len(refs)])]
        return parts[0], parts[1], parts[2], parts[3], parts[4], parts[5]


def no_exchange():
    return Hosted([], [], [], lambda *a: None, lambda *a: None)


def run_hosted(hosted, name):
    def body(*refs):
        _, ins, _, outs, _, sems = hosted.split(refs, 0, 0)
        hosted.start(ins, outs, sems)
        hosted.finish(ins, outs, sems)

    in_specs, out_specs = hosted.specs()
    return pl.pallas_call(body, name=name, in_specs=in_specs, out_specs=out_specs, out_shape=hosted.out_shape,
                          scratch_shapes=hosted.scratch)(*hosted.args)


def gather8(blocks, relay_diagonal=False):
    na = len(blocks)

    def copies(ins, outs, sems):
        send_sems, recv_sems, local_sem = sems
        x, y, c = _place()
        me, sibling = (x, y, c), (x, y, 1 - c)
        chips = [(1 - x, y), (x, 1 - y), (1 - x, 1 - y)]

        def slot(o_ref, px, py, pc, half=None):
            ref = o_ref.at[4 * px + 2 * py + pc]
            if half is None:
                return ref
            rows = ref.shape[0] // 2
            return ref.at[pl.ds(half * rows, rows)]

        def copy(a, k, block, to, src=None, half=None):
            return pltpu.make_async_remote_copy(
                src_ref=slot(outs[a], *block, half) if src is None else src, dst_ref=slot(outs[a], *block, half),
                send_sem=send_sems.at[a, k], recv_sem=recv_sems.at[a, k], device_id=to, device_id_type=MESH)

        mine = [pltpu.make_async_copy(ins[a], slot(outs[a], *me), local_sem.at[a]) for a in range(na)]
        first = []
        for a in range(na):
            first.append(copy(a, 0, me, sibling, src=ins[a]))
            first += [copy(a, 1 + j, me, (*chip, c), src=ins[a])
                      for j, chip in enumerate(chips[:2] if relay_diagonal else chips)]
        return copy, mine, first, me, sibling, chips, c

    def start(ins, outs, sems):
        _, mine, first, *_ = copies(ins, outs, sems)
        for cp in mine + first:
            cp.start()

    def finish(ins, outs, sems):
        copy, mine, first, me, sibling, chips, c = copies(ins, outs, sems)
        passed = []
        for j, chip in enumerate(chips[:2] if relay_diagonal else chips):
            for a in range(na):
                copy(a, 1 + j, (*chip, c), me).wait_recv()
                onward = [copy(a, 4 + j, (*chip, c), sibling)]
                if relay_diagonal:
                    onward.insert(0, copy(a, (3, 7)[j], (*chip, c), (*chips[1 - j], c), half=j))
                for cp in onward:
                    cp.start()
                passed += onward
        if relay_diagonal:
            for a in range(na):
                copy(a, 3, (*chips[2], c), me, half=0).wait_recv()
                copy(a, 7, (*chips[2], c), me, half=1).wait_recv()
                cp = copy(a, 6, (*chips[2], c), sibling)
                cp.start()
                passed.append(cp)
        for a in range(na):
            copy(a, 0, sibling, me).wait_recv()
            for j, chip in enumerate(chips):
                copy(a, 4 + j, (*chip, 1 - c), me).wait_recv()
        for cp in first + passed:
            cp.wait_send()
        for cp in mine:
            cp.wait()

    return Hosted(list(blocks), [jax.ShapeDtypeStruct((8,) + b.shape, b.dtype) for b in blocks],
                  [pltpu.SemaphoreType.DMA((na, 8)), pltpu.SemaphoreType.DMA((na, 8)), pltpu.SemaphoreType.DMA((na,))],
                  start, finish)


def chips3(arrays):
    na = len(arrays)

    def copies(ins, outs, sems):
        send_sems, recv_sems = sems
        x, y, c = _place()
        return [pltpu.make_async_remote_copy(
            src_ref=ins[a].at[2 * px + py], dst_ref=outs[a].at[k], send_sem=send_sems.at[a, k],
            recv_sem=recv_sems.at[a, k], device_id=(px, py, c), device_id_type=MESH)
            for a in range(na) for k, (px, py) in enumerate([(1 - x, y), (x, 1 - y), (1 - x, 1 - y)])]

    def start(ins, outs, sems):
        for cp in copies(ins, outs, sems):
            cp.start()

    def finish(ins, outs, sems):
        for cp in copies(ins, outs, sems):
            cp.wait()

    return Hosted(list(arrays), [jax.ShapeDtypeStruct((3,) + a.shape[1:], a.dtype) for a in arrays],
                  [pltpu.SemaphoreType.DMA((na, 3)), pltpu.SemaphoreType.DMA((na, 3))], start, finish)


def siblings(arrays):
    na = len(arrays)

    def copies(ins, outs, sems):
        send_sems, recv_sems = sems
        x, y, c = _place()
        return [pltpu.make_async_remote_copy(
            src_ref=ins[a], dst_ref=outs[a], send_sem=send_sems.at[a], recv_sem=recv_sems.at[a],
            device_id=(x, y, 1 - c), device_id_type=MESH) for a in range(na)]

    def start(ins, outs, sems):
        for cp in copies(ins, outs, sems):
            cp.start()

    def finish(ins, outs, sems):
        for cp in copies(ins, outs, sems):
            cp.wait()

    return Hosted(list(arrays), [jax.ShapeDtypeStruct(a.shape, a.dtype) for a in arrays],
                  [pltpu.SemaphoreType.DMA((na,)), pltpu.SemaphoreType.DMA((na,))], start, finish)


def both(first, second):
    na, no, ns = len(first.args), len(first.out_shape), len(first.scratch)

    def start(ins, outs, sems):
        first.start(ins[:na], outs[:no], sems[:ns])
        second.start(ins[na:], outs[no:], sems[ns:])

    def finish(ins, outs, sems):
        first.finish(ins[:na], outs[:no], sems[:ns])
        second.finish(ins[na:], outs[no:], sems[ns:])

    return Hosted(first.args + second.args, first.out_shape + second.out_shape, first.scratch + second.scratch,
                  start, finish)


def siblings4(arrays):
    na = len(arrays)

    def copies(ins, outs, sems):
        send_sems, recv_sems = sems
        x, y, c = _place()
        return [pltpu.make_async_remote_copy(
            src_ref=ins[a].at[2 * j + 1 - c], dst_ref=outs[a].at[j],
            send_sem=send_sems.at[a, j], recv_sem=recv_sems.at[a, j],
            device_id=(x, y, 1 - c), device_id_type=MESH) for a in range(na) for j in range(4)]

    def start(ins, outs, sems):
        for cp in copies(ins, outs, sems):
            cp.start()

    def finish(ins, outs, sems):
        for cp in copies(ins, outs, sems):
            cp.wait()

    return Hosted(list(arrays), [jax.ShapeDtypeStruct((4,) + a.shape[1:], a.dtype) for a in arrays],
                  [pltpu.SemaphoreType.DMA((na, 4)), pltpu.SemaphoreType.DMA((na, 4))], start, finish)


def _row_tile(r):
    for cand in (512, 256, 128, 64, 32, 16, 8):
        if r % cand == 0:
            return cand
    return r


def chip_partial(place, g8s, landed4s, name):
    n = len(g8s)

    def body(place_ref, *refs):
        del place_ref
        for g_ref, l_ref, o_ref in zip(refs[:n], refs[n:2 * n], refs[2 * n:]):
            o_ref[...] = (g_ref[...].astype(F32) + l_ref[...].astype(F32)).astype(BF16)

    own = [pl.BlockSpec((None,) + g.shape[1:], lambda j, s: (2 * j + s[0], 0, 0)) for g in g8s]
    plain = [pl.BlockSpec((None,) + g.shape[1:], lambda j, s: (j, 0, 0)) for g in g8s]
    return pl.pallas_call(
        body, name=name,
        grid_spec=pltpu.PrefetchScalarGridSpec(num_scalar_prefetch=1, grid=(4,), in_specs=own + plain, out_specs=plain),
        out_shape=[jax.ShapeDtypeStruct((4,) + g.shape[1:], BF16) for g in g8s],
    )(place, *g8s, *landed4s)


def shard_sum(place, partial4s, landed3s, name):
    n = len(partial4s)

    def body(place_ref, *refs):
        del place_ref
        for p_ref, l_ref, o_ref in zip(refs[:n], refs[n:2 * n], refs[2 * n:]):
            acc = p_ref[...].astype(F32)
            for k in range(3):
                acc = acc + l_ref[k].astype(F32)
            o_ref[...] = acc

    def halves(p, lead):
        r, ccols = p.shape[1:]
        return (lead, r // 2, ccols)

    return pl.pallas_call(
        body, name=name,
        grid_spec=pltpu.PrefetchScalarGridSpec(
            num_scalar_prefetch=1, grid=(2,),
            in_specs=[pl.BlockSpec(halves(p, None), lambda i, s: (s[1], i, 0)) for p in partial4s]
            + [pl.BlockSpec(halves(p, 3), lambda i, s: (0, i, 0)) for p in partial4s],
            out_specs=[pl.BlockSpec(halves(p, None)[1:], lambda i, s: (i, 0)) for p in partial4s]),
        out_shape=[jax.ShapeDtypeStruct(p.shape[1:], F32) for p in partial4s],
    )(place, *partial4s, *landed3s)


def _adamw_math(w, g, m, v):
    m2 = B1 * m + (1.0 - B1) * g
    v2 = B2 * v + (1.0 - B2) * (g * g)
    m_hat = m2 / (1.0 - B1 ** STEP)
    v_hat = v2 / (1.0 - B2 ** STEP)
    return -LR * (m_hat / (jnp.sqrt(v_hat) + AEPS) + WD * w), m2, v2


def adamw_halves(place, w, mine, theirs, m, v, name):
    r, ccols = w.shape
    hr = r // 2
    tr = _row_tile(hr)
    nt = hr // tr

    def body(place_ref, w_ref, a_ref, b_ref, m_ref, v_ref, g_out, d_out, m_out, v_out):
        g = jnp.where(pl.program_id(0) == place_ref[0], a_ref[...], b_ref[...])
        d, m2, v2 = _adamw_math(w_ref[...], g, m_ref[...], v_ref[...])
        g_out[...] = g
        d_out[...] = d
        m_out[...] = m2
        v_out[...] = v2

    full = pl.BlockSpec((tr, ccols), lambda h, i, s: (h * nt + i, 0))
    part = pl.BlockSpec((tr, ccols), lambda h, i, s: (i, 0))
    return pl.pallas_call(
        body, name=name,
        grid_spec=pltpu.PrefetchScalarGridSpec(
            num_scalar_prefetch=1, grid=(2, nt), in_specs=[full, part, part, full, full], out_specs=[full] * 4),
        out_shape=[jax.ShapeDtypeStruct((r, ccols), F32)] * 4,
    )(place, w, mine, theirs, m, v)


def adamw_group(place, halved, plain, hosted, name):
    rows = halved[0][0].shape[0]
    tr = 128
    nt = rows // 2 // tr
    nh, npl = len(halved), len(plain)

    def body(place_ref, *refs):
        own_in, h_in, own_out, h_out, _, h_sems = hosted.split(refs, 5 * nh + 4 * npl, 4 * nh + 3 * npl)
        half = pl.program_id(0)
        grid_step = half * nt + pl.program_id(1)

        @pl.when(grid_step == 0)
        def _():
            hosted.start(h_in, h_out, h_sems)

        for i in range(nh):
            w_ref, a_ref, b_ref, m_ref, v_ref = own_in[5 * i:5 * i + 5]
            g = jnp.where(half == place_ref[0], a_ref[...], b_ref[...])
            res = (g,) + _adamw_math(w_ref[...], g, m_ref[...], v_ref[...])
            for o_ref, r in zip(own_out[4 * i:4 * i + 4], res):
                o_ref[...] = r
        for i in range(npl):
            w_ref, g_ref, m_ref, v_ref = own_in[5 * nh + 4 * i:5 * nh + 4 * i + 4]
            res = _adamw_math(w_ref[...], g_ref[...], m_ref[...], v_ref[...])
            for o_ref, r in zip(own_out[4 * nh + 3 * i:4 * nh + 3 * i + 3], res):
                o_ref[...] = r

        @pl.when(grid_step == 2 * nt - 1)
        def _():
            hosted.finish(h_in, h_out, h_sems)

    def full(cols):
        return pl.BlockSpec((tr, cols), lambda h, i, s: (h * nt + i, 0))

    def part(cols):
        return pl.BlockSpec((tr, cols), lambda h, i, s: (i, 0))

    in_specs, out_specs, out_shape, args = [], [], [], []
    for w, a, b, m, v in halved:
        cols = w.shape[1]
        in_specs += [full(cols), part(cols), part(cols), full(cols), full(cols)]
        out_specs += [full(cols)] * 4
        out_shape += [jax.ShapeDtypeStruct(w.shape, F32)] * 4
        args += [w, a, b, m, v]
    for w, g, m, v in plain:
        cols = w.shape[1]
        in_specs += [full(cols)] * 4
        out_specs += [full(cols)] * 3
        out_shape += [jax.ShapeDtypeStruct(w.shape, F32)] * 3
        args += [w, g, m, v]
    h_in_specs, h_out_specs = hosted.specs()
    return pl.pallas_call(
        body, name=name,
        grid_spec=pltpu.PrefetchScalarGridSpec(
            num_scalar_prefetch=1, grid=(2, nt), in_specs=in_specs + h_in_specs, out_specs=out_specs + h_out_specs,
            scratch_shapes=hosted.scratch),
        out_shape=out_shape + hosted.out_shape,
        compiler_params=_cp(("arbitrary", "arbitrary")),
    )(place, *args, *hosted.args)


def _silu(x):
    return x * jax.nn.sigmoid(x)


def prologue(c_rows, c_ctx_row, w_ada, b_shard, rpb_flat, half_w_in, late_shards):
    shape = jax.ShapeDtypeStruct
    n_late = len(late_shards)
    half_shapes = [(w.shape[0] // 2, w.shape[1]) for w in late_shards]
    g_w = gather8([half_w_in], relay_diagonal=True)
    g_c = gather8([shape((8, D), F32)])
    g_m = gather8([shape((32, 1536), F32)])

    def body(*refs):
        c_ref, cc_ref, w_ref, b_ref, flat_ref, hw_ref = refs[:6]
        late_refs = refs[6:6 + n_late]
        cin_ref, mg_ref, gw_ref, bias_ref, cos_ref, sin_ref = refs[6 + n_late:12 + n_late]
        rest = refs[12 + n_late:]
        half_refs, (cg_s, ms_s, bias_s) = rest[:n_late], rest[n_late:n_late + 3]
        stage, (load_sem, bias_sem), sems = rest[n_late + 3:2 * n_late + 3], rest[2 * n_late + 3:2 * n_late + 5], \
            rest[2 * n_late + 5:]
        sw, sc, sm = sems[0:3], sems[3:6], sems[6:9]
        core = lax.axis_index("c")
        g_c.start([c_ref], [cg_s], sc)
        g_w.start([hw_ref], [gw_ref], sw)
        loads = [pltpu.make_async_copy(late_refs[a].at[pl.ds(core * half_shapes[a][0], half_shapes[a][0]), :],
                                       stage[a], load_sem.at[a]) for a in range(n_late)]
        for cp in loads:
            cp.start()
        g_c.finish([c_ref], [cg_s], sc)
        cin_ref[...] = jnp.zeros_like(cin_ref)
        for dev in range(8):
            cin_ref[2 * dev:2 * dev + 2, :] = cg_s[dev, 0:2, :]
        cin_ref[16:17, :] = cc_ref[...]
        ms_s[...] = _nn(_silu(cin_ref[...]), w_ref[...]) + b_ref[...]
        g_m.start([ms_s], [mg_ref], sm)
        for a, cp in enumerate(loads):
            cp.wait()
            half_refs[a][...] = stage[a][...].astype(BF16)
        cos_ref[...], sin_ref[...] = _rope_tables()
        stores = []
        for pair in range(NPAIR):
            if pair >= 2:
                stores[pair - 2].wait()
            _na_bias_pair(flat_ref.at[pair], bias_s.at[pair % 2])
            stores.append(pltpu.make_async_copy(bias_s.at[pair % 2], bias_ref.at[pair], bias_sem.at[pair % 2]))
            stores[pair].start()
        for cp in stores[-2:]:
            cp.wait()
        g_w.finish([hw_ref], [gw_ref], sw)
        g_m.finish([ms_s], [mg_ref], sm)

    vmem = pl.BlockSpec(memory_space=pltpu.VMEM)
    hbm = pl.BlockSpec(memory_space=pl.ANY)
    return pl.pallas_call(
        body, name="prologue", in_specs=[vmem, vmem, vmem, vmem, vmem, hbm] + [hbm] * n_late,
        out_specs=[vmem, vmem, hbm, hbm, vmem, vmem] + [vmem] * n_late,
        out_shape=[shape((32, D), F32), shape((8, 32, 1536), F32)] + g_w.out_shape
        + [shape((NPAIR,) + NA_BIAS_SHAPE, F32)] + [shape((SEQ, RD), F32)] * 2 + [shape(s, BF16) for s in half_shapes],
        scratch_shapes=[pltpu.VMEM((8, 8, D), F32), pltpu.VMEM((32, 1536), F32), pltpu.VMEM((2,) + NA_BIAS_SHAPE, F32)]
        + [pltpu.VMEM(s, F32) for s in half_shapes]
        + [pltpu.SemaphoreType.DMA((n_late,)), pltpu.SemaphoreType.DMA((2,))]
        + g_w.scratch + g_c.scratch + g_m.scratch,
        compiler_params=_cp(),
    )(c_rows, c_ctx_row, w_ada, b_shard, rpb_flat, half_w_in, *late_shards)


def ada_grads(cin, gb, gc, w_ada):
    def body(c_ref, gb_ref, gc_ref, w_ref, gw_ref, pc_ref):
        ctx_tot = jnp.sum(gc_ref[...], axis=0, keepdims=True)
        rows = lax.broadcasted_iota(jnp.int32, (16, 512), 0)
        dm = jnp.concatenate([gb_ref[...], jnp.where(rows == 0, ctx_tot, 0.0)], axis=0)
        gw_ref[...] = _tn(_silu(c_ref[...]), dm)
        rows8 = lax.broadcasted_iota(jnp.int32, (8, 512), 0)
        part = _nt(jnp.where(rows8 == 0, ctx_tot, 0.0), w_ref[...])

        @pl.when(pl.program_id(0) == 0)
        def _():
            pc_ref[...] = jnp.zeros_like(pc_ref)

        pc_ref[...] += part

    return pl.pallas_call(
        body, name="ada_grads", grid=(3,),
        in_specs=[pl.BlockSpec((32, D), lambda j: (0, 0)), pl.BlockSpec((16, 512), lambda j: (0, j)),
                  pl.BlockSpec((8, 512), lambda j: (0, j)), pl.BlockSpec((D, 512), lambda j: (0, j))],
        out_specs=[pl.BlockSpec((D, 512), lambda j: (0, j)), pl.BlockSpec((8, D), lambda j: (0, 0))],
        out_shape=[jax.ShapeDtypeStruct((D, 1536), F32), jax.ShapeDtypeStruct((8, D), F32)],
    )(cin, gb, gc, w_ada)


SMALL_SUM_ROWS = 15


def small_update(gsm, gbf, gcf, pcg, params):
    n = len(params)

    def body(*refs):
        gsm_ref, gbf_ref, gcf_ref, pcg_ref = refs[:4]
        wmv, outs, loss_out = refs[4:4 + 3 * n], refs[4 + 3 * n:4 + 7 * n], refs[-1]
        acc = gsm_ref[0]
        for dev in range(1, 8):
            acc = acc + gsm_ref[dev]
        c_ctx = wmv[0][...]
        sg = jax.nn.sigmoid(c_ctx)
        dsilu = pcg_ref[0:1, :] + pcg_ref[2:3, :] + pcg_ref[4:5, :] + pcg_ref[6:7, :]
        lane = lax.broadcasted_iota(jnp.int32, (1, D), 1)
        last = acc[14:15, :]
        grads = [
            dsilu * (sg * (1.0 + c_ctx * (1.0 - sg))),
            jnp.sum(gbf_ref[...], axis=0, keepdims=True) + jnp.sum(gcf_ref[...], axis=0, keepdims=True),
            acc[0:1, :] + acc[1:2, :], acc[2:3, :], acc[3:4, :], acc[4:5, :],
            acc[5:6, 0:512], acc[6:14, :], jnp.where(lane < 8, last, 0.0),
        ]
        loss_out[...] = jnp.broadcast_to(jnp.sum(jnp.where(lane == 8, last, 0.0), axis=1, keepdims=True), (8, 128))
        for i, g in enumerate(grads):
            d, m2, v2 = _adamw_math(wmv[3 * i][...], g, wmv[3 * i + 1][...], wmv[3 * i + 2][...])
            outs[4 * i][...] = g
            outs[4 * i + 1][...] = d
            outs[4 * i + 2][...] = m2
            outs[4 * i + 3][...] = v2

    flat = [a for wmv in params for a in wmv]
    out_shape = [jax.ShapeDtypeStruct(w.shape, F32) for w, _, _ in params for _ in range(4)]
    return pl.pallas_call(
        body, name="small_update", out_shape=out_shape + [jax.ShapeDtypeStruct((8, 128), F32)],
    )(gsm, gbf, gcf, pcg, *flat)


def _pad_row(v, rows):
    flat = v.reshape(-1)
    return jnp.pad(flat, (0, rows * D - flat.shape[0])).reshape(rows, D)


def local_step(x, ctx, tgt, mod3, rope, bias, g_pre_mix, g_post_mix, g_pre_mlp, g_post_mlp, ret_decay, ret_gn,
               wperm, late_weights, early_grads):
    nb = x.shape[0]
    tokens = nb * SEQ
    cos, sin = rope
    rd = ret_decay.T.reshape(RH, 2, 1)
    gn = ret_gn.reshape(RH, 1, RD)
    h, pret, pna = premix_proj(x, mod3, g_pre_mix, wperm, False, "premix_proj")
    hc, pretc, pnac = premix_proj(ctx, mod3, g_pre_mix, wperm, True, "premix_proj_ctx")
    o_all, mixin, gw_out = retention_fwd(pret, pretc, rd, gn, cos, sin, late_weights(0))
    mixin, gw1, gw2 = na_fwd(pna, pnac, bias, mixin, late_weights(1))
    dx_tail, dmix, h2, du, act, dm, dmixin, dmod_t, dg_t, loss_t = tail_fwd_bwd(
        x, mixin, tgt, mod3, g_post_mix, g_pre_mlp, g_post_mlp, gw_out.reshape(D, D), gw1.reshape(4, D, D),
        gw2.reshape(DFF, D))
    dw_out = weight_grad([(mixin.reshape(tokens, D), dmix.reshape(tokens, D))], "grad_w_out", BF16)
    dw1 = weight_grad([(h2.reshape(tokens, D), du.reshape(tokens, DFF))], "grad_w_mlp1", BF16, col_blocks=True)
    dw2 = weight_grad([(act.reshape(tokens, DFF), dm.reshape(tokens, D))], "grad_w_mlp2", BF16)
    dproj, dprojc, drd, dgn, *landed = retention_bwd(pret, pretc, o_all, dmixin, rd, gn, cos, sin,
                                                     early_grads[0](dw_out, dw1, dw2))
    dproj, dprojc, dpat, *early = na_bwd(pna, pnac, bias, dmixin, dproj, dprojc, early_grads[1](landed))
    dw_in = weight_grad([(h.reshape(tokens, D), dproj.reshape(tokens, IN_W)),
                         (hc.reshape(nb * LC, D), dprojc.reshape(nb * LC, IN_W))], "grad_w_in", tn=IN_W // 2, tk=1024)
    dmod_c, dg_c, *late = premix_bwd(ctx, mod3, g_pre_mix, wperm, dprojc, None, early_grads[2](dw_in), "premix_bwd_ctx")
    grad_x, dmod_a, dg_a, *late = premix_bwd(x, mod3, g_pre_mix, wperm, dproj, dx_tail, early_grads[3](late),
                                             "premix_bwd")
    dmod = jnp.concatenate([jnp.concatenate([dmod_a[:, 0:2], dmod_t[:, 2:6]], axis=1), dmod_c], axis=0)
    last = jnp.pad(jnp.concatenate([drd[:, :, 0].T.reshape(8), loss_t[0, 0:1]]), (0, D - 9)).reshape(1, D)
    small = jnp.concatenate([dg_a[0:1], dg_c[0:1], dg_t[0:3], _pad_row(dgn, 1), dpat.reshape(8, D), last], axis=0)
    return grad_x, late, early, dmod, small


def kernel(x, c, ctx, c_ctx, w_ada, b_ada, g_pre_mix, g_post_mix, g_pre_mlp, g_post_mlp, w_in, ret_decay, ret_gn, na_rpb, w_out, w_mlp1, w_mlp2, loss_target, m_c_ctx, m_w_ada, m_b_ada, m_g_pre_mix, m_g_post_mix, m_g_pre_mlp, m_g_post_mlp, m_w_in, m_ret_decay, m_ret_gn, m_na_rpb, m_w_out, m_w_mlp1, m_w_mlp2, v_c_ctx, v_w_ada, v_b_ada, v_g_pre_mix, v_g_post_mix, v_g_pre_mlp, v_g_post_mlp, v_w_in, v_ret_decay, v_ret_gn, v_na_rpb, v_w_out, v_w_mlp1, v_w_mlp2):
    px, py, pc = _place()
    dev = 4 * px + 2 * py + pc
    chip = 2 * px + py

    half_w_in = lax.dynamic_slice_in_dim(w_in[0], pc * (D // 2), D // 2, 0).astype(BF16)
    cin, mg, gw_in, bias, cos, sin, *late_halves = prologue(
        jnp.pad(c, ((0, 6), (0, 0))), c_ctx[None], w_ada[0], lax.dynamic_slice_in_dim(b_ada, chip * 1536, 1536, 1),
        _rpb_flat(na_rpb[0]), half_w_in, [w_out[0], w_mlp1[0], w_mlp2[0]])
    halves = [half_w_in] + late_halves
    wperm = unpack_w_in(gw_in.reshape(4, D, 896))
    mod_all = jnp.concatenate([mg[0], mg[2], mg[4], mg[6]], axis=1)
    mod3 = (jnp.pad(lax.dynamic_slice_in_dim(mod_all, 2 * dev, 2, 0), ((0, 1), (0, 0)))
            + jnp.pad(mod_all[16:17], ((2, 0), (0, 0)))).reshape(3, 6, D)

    place = jnp.stack([pc, chip]).astype(jnp.int32)

    early_names = ["w_out", "w_mlp1", "w_mlp2"]
    early_g8, early_partial = [], []

    def early_a(dw_out, dw1, dw2):
        early_g8[:] = [dw_out.reshape(8, 128, D), dw1.reshape(8, 512, D), dw2.reshape(8, 512, D)]
        return siblings4(early_g8)

    def early_b(landed):
        early_partial[:] = chip_partial(place, early_g8, landed, "rs_chip_sum_early")
        return chips3(early_partial)

    late_partial = []

    late_g8 = []

    def late_c(dw_in):
        late_g8[:] = [pack_w_in(dw_in).reshape(8, 512, 896)]
        return siblings4(late_g8)

    def late_d(landed):
        late_partial[:] = chip_partial(place, late_g8, landed, "rs_chip_sum_w_in")
        return chips3(late_partial)

    grad_x, (landed3_in,), early_landed, dmod, small = local_step(
        x, ctx, loss_target, mod3, (cos, sin), bias, g_pre_mix, g_post_mix, g_pre_mlp, g_post_mlp, ret_decay[0], ret_gn,
        wperm, lambda k: gather8(halves[1:2] if k == 0 else halves[2:4]), (early_a, early_b, late_c, late_d))
    early_mine = shard_sum(place, early_partial, early_landed, "rs_shard_sum_early")

    pay = jnp.concatenate([dmod.reshape(18, D), small, jnp.zeros((40 - 18 - SMALL_SUM_ROWS, D), F32)], axis=0)
    *early_theirs, gs = run_hosted(both(siblings(early_mine), gather8([pay])), "rs_halves_early_gather_small")
    gbf = gs[:, 0:12].reshape(16, 6 * D)
    gcf = gs[:, 12:18].reshape(8, 6 * D)
    gw_ada, pc_part = ada_grads(cin, lax.dynamic_slice_in_dim(gbf, chip * 1536, 1536, 1),
                                lax.dynamic_slice_in_dim(gcf, chip * 1536, 1536, 1), w_ada[0])
    (mine_in,) = shard_sum(place, late_partial, [landed3_in], "rs_shard_sum_w_in")
    theirs_in, pcg = run_hosted(both(siblings([mine_in]), gather8([pc_part])), "rs_halves_w_in_gather_c_ctx")

    grouped = adamw_group(
        place,
        [(w_mlp1[0], early_mine[1], early_theirs[1], m_w_mlp1[0], v_w_mlp1[0]),
         (w_mlp2[0], early_mine[2], early_theirs[2], m_w_mlp2[0], v_w_mlp2[0])],
        [(w_ada[0], gw_ada, m_w_ada[0], v_w_ada[0])], no_exchange(), "adamw_group")
    d_ada, m_ada, v_ada = grouped[8:11]
    big = [
        [r[None] for r in adamw_halves(place, w_in[0], mine_in, theirs_in, m_w_in[0], v_w_in[0], "adamw_w_in")],
        [r[None] for r in adamw_halves(place, w_out[0], early_mine[0], early_theirs[0], m_w_out[0], v_w_out[0],
                                       "adamw_w_out")],
        [r[None] for r in grouped[0:4]], [r[None] for r in grouped[4:8]],
    ]

    def rpb_rows(t):
        return _rpb_flat(t[0]).reshape(8, D)

    def decay_row(t):
        return jnp.pad(t.reshape(1, 8), ((0, 0), (0, D - 8)))

    views = [lambda t: t.reshape(1, D), lambda t: t, lambda t: t, lambda t: t, lambda t: t, lambda t: t, lambda t: t,
             rpb_rows, decay_row]
    back = [lambda t: t.reshape(D), lambda t: t, lambda t: t, lambda t: t, lambda t: t, lambda t: t, lambda t: t,
            lambda t: _rpb_flat_t(t)[None], lambda t: t[:, 0:8].reshape(1, 2, 4)]
    small_w = (c_ctx, b_ada, g_pre_mix, g_post_mix, g_pre_mlp, g_post_mlp, ret_gn, na_rpb, ret_decay)
    small_m = (m_c_ctx, m_b_ada, m_g_pre_mix, m_g_post_mix, m_g_pre_mlp, m_g_post_mlp, m_ret_gn, m_na_rpb, m_ret_decay)
    small_v = (v_c_ctx, v_b_ada, v_g_pre_mix, v_g_post_mix, v_g_pre_mlp, v_g_post_mlp, v_ret_gn, v_na_rpb, v_ret_decay)
    *res, loss8 = small_update(gs[:, 18:18 + SMALL_SUM_ROWS], gbf, gcf, pcg[:, 0],
                               [(f(w), f(m), f(v)) for f, w, m, v in zip(views, small_w, small_m, small_v)])

    def leaves(ada, idx):
        s_c, s_b, s_g1, s_g2, s_g3, s_g4, s_gn, s_rpb, s_rd = [back[i](res[4 * i + idx]) for i in range(9)]
        return [s_c, ada[None], s_b, s_g1, s_g2, s_g3, s_g4, big[0][idx], s_rd, s_gn, s_rpb,
                big[1][idx], big[2][idx], big[3][idx]]

    return (loss8[0, 0], grad_x, *leaves(gw_ada, 0), *leaves(d_ada, 1), *leaves(m_ada, 2), *leaves(v_ada, 3))
```

```python
import functools
import math

import jax
import jax.numpy as jnp
from jax import lax
from jax.experimental import pallas as pl
from jax.experimental.pallas import tpu as pltpu

F32, BF16 = jnp.float32, jnp.bfloat16
D = 1024
SEQ = 2048
LC = 256
GW = 64
RH, RD, CH = 4, 128, 128
NPAIR = 4
IN_W = 3584
RET_W = 2048
DFF = 4096
EPS = 1e-6
NEG = -1e30
TN = 256
NCH = SEQ // CH
LR, B1, B2, AEPS, WD, STEP = 0.001, 0.9, 0.999, 1e-08, 0.01, 10
MESH = pl.DeviceIdType.MESH
VMEM_LIMIT = 56 * 1024 * 1024


def _cp(sem=None):
    return pltpu.CompilerParams(dimension_semantics=sem, vmem_limit_bytes=VMEM_LIMIT)


def _nn(a, b):
    return jnp.dot(a.astype(BF16), b.astype(BF16), preferred_element_type=F32)


def _nt(a, b):
    return lax.dot_general(a.astype(BF16), b.astype(BF16), (((1,), (1,)), ((), ())), preferred_element_type=F32)


def _tn(a, b):
    return lax.dot_general(a.astype(BF16), b.astype(BF16), (((0,), (0,)), ((), ())), preferred_element_type=F32)


@jax.custom_vjp
def mm_tn(a, b):
    return _tn(a, b)


mm_tn.defvjp(lambda a, b: (_tn(a, b), (a, b)), lambda r, g: (_nt(r[1], g), _nn(r[0], g)))


def _rms(x):
    return x * lax.rsqrt(jnp.mean(x * x, axis=-1, keepdims=True) + EPS)


def _rms_mod(x, g, sc, sh):
    return (_rms(x) * g) * (1.0 + sc) + sh


def _post_mix(x, mix, gt1, sc2, sh2, g_post_mix, g_pre_mlp):
    x1 = x + gt1 * (_rms(mix) * g_post_mix)
    return x1, _rms_mod(x1, g_pre_mlp, sc2, sh2)


def _head_loss(x1, m, gt2, g_post_mlp, tgt):
    err = x1 + gt2 * (_rms(m) * g_post_mlp) - tgt
    return 0.5 * jnp.sum(jnp.mean(err * err, axis=-1, keepdims=True), axis=0, keepdims=True)


def _ln_gate(o, g, w):
    mu = jnp.mean(o, axis=-1, keepdims=True)
    var = jnp.mean(jnp.square(o - mu), axis=-1, keepdims=True)
    y = (o - mu) * lax.rsqrt(var + EPS)
    return (y * w) * (g * jax.nn.sigmoid(g))


def _pair_order(x):
    lane = lax.broadcasted_iota(jnp.int32, x.shape, 1)
    return jnp.where((lane >= 32) & (lane < 64), pltpu.roll(x, 96, 1),
                     jnp.where((lane >= 64) & (lane < 96), pltpu.roll(x, 32, 1), x))


def _rope(x, cos, sin):
    return x * cos + pltpu.roll(x, 64, 1) * sin


def _rope_t(g, cos, sin):
    return g * cos + pltpu.roll(g * sin, 64, 1)


def _rope_tables():
    tok = lax.broadcasted_iota(jnp.int32, (SEQ, RD), 0)
    lane = lax.broadcasted_iota(jnp.int32, (SEQ, RD), 1)
    pos = jnp.where((lane & 32) == 0, tok >> 6, tok & (GW - 1)).astype(F32)
    ang = pos * jnp.exp((lane & 31).astype(F32) * (-math.log(10000.0) / 32))
    return jnp.cos(ang), jnp.where(lane < 64, -jnp.sin(ang), jnp.sin(ang))


def _chunk_loop(n, body, init, k=4):
    def several(t, carry):
        for i in range(k):
            carry = body(k * t + i, carry)
        return carry

    return lax.fori_loop(0, n // k, several, init)


def _fiota(shape, dim):
    return lax.broadcasted_iota(jnp.int32, shape, dim).astype(F32)


def _ret_state(k, v, s, lg, reverse):
    pos = _fiota((CH, 1), 0)
    b_exp = pos if reverse else (CH - 1.0 - pos)
    return jnp.exp(lg * CH) * s + mm_tn(k * jnp.exp(lg * b_exp), v)


class _Decays:
    def __init__(self, lgs):
        i, j, pos = _fiota((CH, CH), 0), _fiota((CH, CH), 1), _fiota((CH, 1), 0)
        diffs = (i - j, j - i)
        keep = (diffs[0] >= 0, diffs[1] > 0)
        mats = [jnp.where(m, jnp.exp(lg * jnp.where(m, d, 0.0)), 0.0) for lg, d, m in zip(lgs, diffs, keep)]
        self.mask = mats[0] + mats[1]
        self.dmask = [mats[0] * diffs[0], mats[1] * diffs[1]]
        a_exp, b_exp = (pos + 1.0, CH - pos), (CH - 1.0 - pos, pos)
        self.a = [jnp.exp(lg * e) for lg, e in zip(lgs, a_exp)]
        self.b = [jnp.exp(lg * e) for lg, e in zip(lgs, b_exp)]
        self.da = [a * e for a, e in zip(self.a, a_exp)]
        self.db = [b * e for b, e in zip(self.b, b_exp)]
        self.g = [jnp.exp(lg * CH) for lg in lgs]


def _both(x, w):
    return jnp.concatenate([x * w[0], x * w[1]], axis=1)


def _total(x):
    return jnp.sum(jnp.sum(x, axis=1, keepdims=True), axis=0, keepdims=True)


def _state_pass(dec, init, k_s, v_of, st_s):
    def step(t, carry):
        out = []
        for d, s in enumerate(carry):
            n = (NCH - 1 - t) if d else t
            sl = pl.ds(pl.multiple_of(n * CH, CH), CH)
            st_s[n, d * RD:(d + 1) * RD, :] = s
            out.append(dec.g[d] * s + _tn(k_s[sl, :] * dec.b[d], v_of(sl)))
        return tuple(out)

    _chunk_loop(NCH, step, tuple(init))


def premix_proj(xin, mod3, g_pre, wperm, is_ctx, name):
    nb, length, _ = xin.shape
    tn = min(2 * TN, length)

    def body(x_ref, mod_ref, g_ref, w_ref, h_ref, pret_ref, pna_ref):
        h = _rms_mod(x_ref[...], g_ref[...], mod_ref[1:2, :], mod_ref[0:1, :])
        hb = h.astype(BF16)
        h_ref[...] = hb
        pret_ref[...] = jnp.dot(hb, w_ref[:, :RET_W], preferred_element_type=F32)
        pna_ref[...] = jnp.dot(hb, w_ref[:, RET_W:], preferred_element_type=F32).astype(BF16)

    return pl.pallas_call(
        body, name=name, grid=(nb, length // tn),
        in_specs=[
            pl.BlockSpec((None, tn, D), lambda b, t: (b, t, 0)),
            pl.BlockSpec((None, 6, D), (lambda b, t: (2, 0, 0)) if is_ctx else (lambda b, t: (b, 0, 0))),
            pl.BlockSpec((1, D), lambda b, t: (0, 0)),
            pl.BlockSpec((D, IN_W), lambda b, t: (0, 0), pipeline_mode=pl.Buffered(1)),
        ],
        out_specs=[
            pl.BlockSpec((None, tn, D), lambda b, t: (b, t, 0)),
            pl.BlockSpec((None, tn, RET_W), lambda b, t: (b, t, 0)),
            pl.BlockSpec((None, tn, IN_W - RET_W), lambda b, t: (b, t, 0)),
        ],
        out_shape=[
            jax.ShapeDtypeStruct((nb, length, D), BF16),
            jax.ShapeDtypeStruct((nb, length, RET_W), F32),
            jax.ShapeDtypeStruct((nb, length, IN_W - RET_W), BF16),
        ],
        compiler_params=_cp(("arbitrary", "arbitrary")),
    )(xin, mod3, g_pre, wperm)


def premix_bwd(xin, mod3, g_pre, wperm, dproj, dx_tail, hosted, name):
    nb, length, _ = xin.shape
    tn = min(TN, length)
    is_ctx = dx_tail is None

    def body(*refs):
        own_in, h_in, own_out, h_out, _, h_sems = hosted.split(refs, 5 if is_ctx else 6, 2 if is_ctx else 3)
        if is_ctx:
            (x_ref, mod_ref, g_ref, w_ref, dp_ref), (dmod_ref, dg_ref) = own_in, own_out
        else:
            (x_ref, mod_ref, g_ref, w_ref, dp_ref, dxt_ref), (dx_ref, dmod_ref, dg_ref) = own_in, own_out
        b, t = pl.program_id(0), pl.program_id(1)
        grid_step = b * (length // tn) + t

        @pl.when(grid_step == 0)
        def _():
            hosted.start(h_in, h_out, h_sems)

        @pl.when(grid_step == nb * (length // tn) - 1)
        def _():
            hosted.finish(h_in, h_out, h_sems)

        dh = lax.dot_general(dp_ref[...], w_ref[...], (((1,), (1,)), ((), ())), preferred_element_type=F32)
        _, vjp = jax.vjp(_rms_mod, x_ref[...], g_ref[...], mod_ref[1:2, :], mod_ref[0:1, :])
        dx, dg, dsc, dsh = vjp(dh)
        if not is_ctx:
            dx_ref[...] = dx + dxt_ref[...]

        @pl.when((t == 0) & ((b == 0) if is_ctx else True))
        def _():
            dmod_ref[...] = jnp.zeros_like(dmod_ref)

        @pl.when((t == 0) & (b == 0))
        def _():
            dg_ref[...] = jnp.zeros_like(dg_ref)

        dmod_ref[0:1, :] += dsh
        dmod_ref[1:2, :] += dsc
        dg_ref[0:1, :] += dg

    tok = lambda b, t: (b, t, 0)
    in_specs = [
        pl.BlockSpec((None, tn, D), tok),
        pl.BlockSpec((None, 6, D), (lambda b, t: (2, 0, 0)) if is_ctx else (lambda b, t: (b, 0, 0))),
        pl.BlockSpec((1, D), lambda b, t: (0, 0)),
        pl.BlockSpec((D, IN_W), lambda b, t: (0, 0), pipeline_mode=pl.Buffered(1)),
        pl.BlockSpec((None, tn, IN_W), tok),
    ]
    args = [xin, mod3, g_pre, wperm, dproj]
    out_specs = [
        pl.BlockSpec((None, 6, D), (lambda b, t: (0, 0, 0)) if is_ctx else (lambda b, t: (b, 0, 0))),
        pl.BlockSpec((8, D), lambda b, t: (0, 0)),
    ]
    out_shape = [jax.ShapeDtypeStruct((1 if is_ctx else nb, 6, D), F32), jax.ShapeDtypeStruct((8, D), F32)]
    if not is_ctx:
        in_specs.append(pl.BlockSpec((None, tn, D), tok))
        args.append(dx_tail)
        out_specs.insert(0, pl.BlockSpec((None, tn, D), tok))
        out_shape.insert(0, jax.ShapeDtypeStruct((nb, length, D), F32))
    h_in_specs, h_out_specs = hosted.specs()
    return pl.pallas_call(
        body, name=name, grid=(nb, length // tn), in_specs=in_specs + h_in_specs, out_specs=out_specs + h_out_specs,
        out_shape=out_shape + hosted.out_shape, scratch_shapes=hosted.scratch,
        compiler_params=_cp(("arbitrary", "arbitrary")),
    )(*args, *hosted.args)


def _ret_specs(order):
    def im(f):
        return lambda *g: f(*order(*g))
    return dict(
        pret=pl.BlockSpec((None, SEQ, 512), im(lambda b, h: (b, 0, h))),
        pretc=pl.BlockSpec((None, LC, 512), im(lambda b, h: (b, 0, h))),
        rd=pl.BlockSpec((None, 2, 1), im(lambda b, h: (h, 0, 0))),
        gn=pl.BlockSpec((None, 1, RD), im(lambda b, h: (h, 0, 0))),
        tab=pl.BlockSpec((SEQ, RD), im(lambda b, h: (0, 0))),
        head=pl.BlockSpec((None, SEQ, RD), im(lambda b, h: (b, 0, h))),
    )


def retention_fwd(pret, pretc, rd, gn, cos, sin, hosted):
    nb = pret.shape[0]
    sp = _ret_specs(lambda b, h: (b, h))

    def body(*refs):
        own_in, h_in, own_out, h_out, own_scr, h_sems = hosted.split(refs, 6, 2)
        p_ref, pc_ref, rd_ref, gn_ref, cos_ref, sin_ref = own_in
        (o_ref, mix_ref), (q_s, k_s, o_s, st_s) = own_out, own_scr
        grid_step = pl.program_id(0) * RH + pl.program_id(1)

        @pl.when(grid_step == 0)
        def _():
            hosted.start(h_in, h_out, h_sems)

        cos_v, sin_v = cos_ref[...], sin_ref[...]
        q_s[...] = _rope(p_ref[:, 0:128], cos_v, sin_v) * (RD ** -0.5)
        k_s[...] = _rope(p_ref[:, 128:256], cos_v, sin_v)
        lgs, init = [], []
        for rev in (False, True):
            lg = jax.nn.log_sigmoid(rd_ref[int(rev):int(rev) + 1, :])
            s = jnp.zeros((RD, RD), F32)
            for n in ((1, 0) if rev else (0, 1)):
                s = _ret_state(pc_ref[n * CH:(n + 1) * CH, 128:256], pc_ref[n * CH:(n + 1) * CH, 256:384], s, lg, rev)
            lgs.append(lg)
            init.append(s)

        dec = _Decays(lgs)
        _state_pass(dec, init, k_s, lambda sl: p_ref[sl, 256:384], st_s)

        def chunk(n, carry):
            sl = pl.ds(pl.multiple_of(n * CH, CH), CH)
            q = q_s[sl, :]
            o_s[sl, :] = (_nn(_nt(q, k_s[sl, :]) * dec.mask, p_ref[sl, 256:384]) + _nn(_both(q, dec.a), st_s[n]))
            return carry

        _chunk_loop(NCH, chunk, 0)
        o = o_s[...]
        o_ref[...] = o
        mix_ref[...] = _ln_gate(o, p_ref[:, 384:512], gn_ref[...]).astype(BF16)

        @pl.when(grid_step == nb * RH - 1)
        def _():
            hosted.finish(h_in, h_out, h_sems)

    h_in_specs, h_out_specs = hosted.specs()
    return pl.pallas_call(
        body, name="retention_fwd", grid=(nb, RH),
        in_specs=[sp["pret"], sp["pretc"], sp["rd"], sp["gn"], sp["tab"], sp["tab"]] + h_in_specs,
        out_specs=[sp["head"], sp["head"]] + h_out_specs,
        out_shape=[jax.ShapeDtypeStruct((nb, SEQ, RH * RD), F32), jax.ShapeDtypeStruct((nb, SEQ, D), BF16)]
        + hosted.out_shape,
        scratch_shapes=[pltpu.VMEM((SEQ, RD), F32)] * 3 + [pltpu.VMEM((NCH, 2 * RD, RD), F32)] + hosted.scratch,
        compiler_params=_cp(("arbitrary", "arbitrary")),
    )(pret, pretc, rd, gn, cos, sin, *hosted.args)


def retention_bwd(pret, pretc, o_all, dmixin, rd, gn, cos, sin, hosted):
    nb = pret.shape[0]
    sp = _ret_specs(lambda h, b: (b, h))

    def body(*refs):
        own_in, h_in, own_out, h_out, own_scr, h_sems = hosted.split(refs, 8, 4)
        p_ref, pc_ref, o_ref, dmix_ref, rd_ref, gn_ref, cos_ref, sin_ref = own_in
        dp_ref, dpc_ref, drd_ref, dgn_ref = own_out
        q_s, k_s, do_s, dq_s, dk_s, dv_s, st_s, gst_s = own_scr
        b = pl.program_id(1)
        grid_step = pl.program_id(0) * nb + b

        @pl.when(grid_step == 0)
        def _():
            hosted.start(h_in, h_out, h_sems)

        cos_v, sin_v = cos_ref[...], sin_ref[...]
        q_s[...] = _rope(p_ref[:, 0:128], cos_v, sin_v) * (RD ** -0.5)
        k_s[...] = _rope(p_ref[:, 128:256], cos_v, sin_v)
        _, gate_vjp = jax.vjp(_ln_gate, o_ref[...], p_ref[:, 384:512], gn_ref[...])
        do, dg, dgn = gate_vjp(dmix_ref[...].astype(F32))
        do_s[...] = do
        dp_ref[:, 384:512] = dg.astype(BF16)

        @pl.when(b == 0)
        def _():
            drd_ref[...] = jnp.zeros_like(drd_ref)
            dgn_ref[...] = jnp.zeros_like(dgn_ref)

        dgn_ref[...] += dgn
        kcs = [pc_ref[n * CH:(n + 1) * CH, 128:256] for n in (0, 1)]
        vcs = [pc_ref[n * CH:(n + 1) * CH, 256:384] for n in (0, 1)]
        dirs = []
        init = []
        for rev in (False, True):
            rdv = rd_ref[int(rev):int(rev) + 1, :]
            lg = jax.nn.log_sigmoid(rdv)
            order_c = (1, 0) if rev else (0, 1)
            s = jnp.zeros((RD, RD), F32)
            ctx_states = []
            for n in order_c:
                ctx_states.append(s)
                s = _ret_state(kcs[n], vcs[n], s, lg, rev)
            dirs.append((rev, order_c, lg, rdv, ctx_states))
            init.append(s)
        dec = _Decays([lg for _, _, lg, _, _ in dirs])

        def v_of(sl):
            return p_ref[sl, 256:384]

        _state_pass(dec, init, k_s, v_of, st_s)
        zeros = jnp.zeros((CH, RD), F32)

        def scores_back(n, carry):
            dmask_sum, da_f, da_b = carry
            sl = pl.ds(pl.multiple_of(n * CH, CH), CH)
            q, k, v, do = q_s[sl, :], k_s[sl, :], v_of(sl), do_s[sl, :]
            scores = _nt(q, k)
            d_att = _nt(do, v)
            d_scores = d_att * dec.mask
            d_qa = _nt(do, st_s[n])
            d_qf, d_qb = d_qa[:, 0:RD], d_qa[:, RD:2 * RD]
            dq_s[sl, :] = _nn(d_scores, k) + d_qf * dec.a[0] + d_qb * dec.a[1]
            dk_s[sl, :] = _tn(d_scores, q)
            dv_s[sl, :] = _tn(scores * dec.mask, do)
            gst_s[n] = _tn(_both(q, dec.a), do)
            return dmask_sum + d_att * scores, da_f + d_qf * q, da_b + d_qb * q

        dmask_sum, da_f, da_b = _chunk_loop(NCH, scores_back, (zeros, zeros, zeros))

        def state_back(t, carry):
            out = []
            for d, r in enumerate(carry):
                n = t if d else (NCH - 1 - t)
                rows = slice(d * RD, (d + 1) * RD)
                own = gst_s[n, rows, :]
                gst_s[n, rows, :] = r
                out.append(own + dec.g[d] * r)
            return tuple(out)

        d_states = _chunk_loop(NCH, state_back, (zeros, zeros))

        def updates_back(n, carry):
            db_f, db_b, dg_f, dg_b = carry
            sl = pl.ds(pl.multiple_of(n * CH, CH), CH)
            k, r, s = k_s[sl, :], gst_s[n], st_s[n]
            d_kw = _nt(v_of(sl), r)
            d_kf, d_kb = d_kw[:, 0:RD], d_kw[:, RD:2 * RD]
            dk_s[sl, :] += d_kf * dec.b[0] + d_kb * dec.b[1]
            dv_s[sl, :] += _nn(_both(k, dec.b), r)
            return (db_f + d_kf * k, db_b + d_kb * k, dg_f + r[0:RD, :] * s[0:RD, :],
                    dg_b + r[RD:2 * RD, :] * s[RD:2 * RD, :])

        db_dg = _chunk_loop(NCH, updates_back, (zeros, zeros, zeros, zeros))
        dkc = [None, None]
        dvc = [None, None]
        for d, ((rev, order_c, lg, rdv, ctx_states), ds) in enumerate(zip(dirs, d_states)):
            dlg = (_total(dmask_sum * dec.dmask[d]) + _total((da_f, da_b)[d] * dec.da[d])
                   + _total(db_dg[d] * dec.db[d]) + CH * dec.g[d] * _total(db_dg[2 + d]))
            for idx in (1, 0):
                n = order_c[idx]
                _, vjp = jax.vjp(functools.partial(_ret_state, reverse=rev), kcs[n], vcs[n], ctx_states[idx], lg)
                dk_c, dv_c, ds, dl = vjp(ds)
                dlg = dlg + dl
                dkc[n] = dk_c if dkc[n] is None else dkc[n] + dk_c
                dvc[n] = dv_c if dvc[n] is None else dvc[n] + dv_c
            drd_ref[int(rev):int(rev) + 1, :] += dlg * jax.nn.sigmoid(-rdv)
        dp_ref[:, 0:128] = _rope_t(dq_s[...] * (RD ** -0.5), cos_v, sin_v).astype(BF16)
        dp_ref[:, 128:256] = _rope_t(dk_s[...], cos_v, sin_v).astype(BF16)
        dp_ref[:, 256:384] = dv_s[...].astype(BF16)
        zero = jnp.zeros((CH, RD), BF16)
        for n in (0, 1):
            rows = slice(n * CH, (n + 1) * CH)
            dpc_ref[rows, 0:128] = zero
            dpc_ref[rows, 128:256] = dkc[n].astype(BF16)
            dpc_ref[rows, 256:384] = dvc[n].astype(BF16)
            dpc_ref[rows, 384:512] = zero

        @pl.when(grid_step == RH * nb - 1)
        def _():
            hosted.finish(h_in, h_out, h_sems)

    h_in_specs, h_out_specs = hosted.specs()
    return pl.pallas_call(
        body, name="retention_bwd", grid=(RH, nb),
        in_specs=[sp["pret"], sp["pretc"], sp["head"], sp["head"], sp["rd"], sp["gn"], sp["tab"], sp["tab"]]
        + h_in_specs,
        out_specs=[
            pl.BlockSpec((None, SEQ, 512), lambda h, b: (b, 0, h)),
            pl.BlockSpec((None, LC, 512), lambda h, b: (b, 0, h)),
            pl.BlockSpec((None, 2, 1), lambda h, b: (h, 0, 0)),
            pl.BlockSpec((None, 1, RD), lambda h, b: (h, 0, 0)),
        ] + h_out_specs,
        out_shape=[
            jax.ShapeDtypeStruct((nb, SEQ, IN_W), BF16),
            jax.ShapeDtypeStruct((nb, LC, IN_W), BF16),
            jax.ShapeDtypeStruct((RH, 2, 1), F32),
            jax.ShapeDtypeStruct((RH, 1, RD), F32),
        ] + hosted.out_shape,
        scratch_shapes=[pltpu.VMEM((SEQ, RD), F32)] * 6 + [pltpu.VMEM((NCH, 2 * RD, RD), F32)] * 2 + hosted.scratch,
        compiler_params=_cp(("arbitrary", "arbitrary")),
    )(pret, pretc, o_all, dmixin, rd, gn, cos, sin, *hosted.args)


def _rpb_flat(rpb):
    return jnp.pad(rpb, ((0, 0), (0, 1), (0, 33))).reshape(NPAIR, 2, 1, 1024)


def _rpb_flat_t(dflat):
    return dflat.reshape(8, 16, 64)[:, :15, :31]


def _barrel(x, left):
    row = lax.broadcasted_iota(jnp.int32, x.shape, 0)
    n = x.shape[1]
    for bit in range(6):
        s = 1 << bit
        x = jnp.where(((row >> bit) & 1) == 1, pltpu.roll(x, (n - s) if left else s, 1), x)
    return x


NA_TILE_ROWS, NA_BAND_ROWS = 4, 12
NA_Q, NA_K = NA_TILE_ROWS * GW, NA_BAND_ROWS * GW
NA_TILES = SEQ // NA_Q


def _band_start(r0):
    return min(max(r0 - 4, 0), 32 - NA_BAND_ROWS)


def _tile_layout(t):
    rows = range(t * NA_TILE_ROWS, (t + 1) * NA_TILE_ROWS)
    return tuple((r if r < 4 else (r - 24 if r > 28 else 4), min(max(r - 4, 0), 24) - _band_start(rows[0]))
                 for r in rows)


NA_CLASSES = sorted(set(_tile_layout(t) for t in range(NA_TILES)))


def _tile_rows(cls):
    return NA_CLASSES[cls]


def _na_tile(t):
    start = jnp.clip(NA_TILE_ROWS * t - 4, 0, 32 - NA_BAND_ROWS)
    cls = 0
    for tile in range(NA_TILES):
        cls = jnp.where(t == tile, NA_CLASSES.index(_tile_layout(tile)), cls)
    return pl.ds(pl.multiple_of(t * NA_Q, NA_Q), NA_Q), pl.ds(pl.multiple_of(start * GW, NA_Q), NA_K), cls


def _na_probs(qst, kb, kc, bias):
    s_loc = _nt(qst, kb) + bias
    s_ctx = _nt(qst, kc)
    m = jnp.maximum(jnp.max(s_loc, axis=1, keepdims=True), jnp.max(s_ctx, axis=1, keepdims=True))
    e_loc, e_ctx = jnp.exp(s_loc - m), jnp.exp(s_ctx - m)
    den = jnp.sum(e_loc, axis=1, keepdims=True) + jnp.sum(e_ctx, axis=1, keepdims=True)
    return e_loc / den, e_ctx / den


def _stack_heads(t):
    lane = lax.broadcasted_iota(jnp.int32, t.shape, 1)
    zero = jnp.zeros_like(t)
    return jnp.concatenate([jnp.where(lane < 64, t, zero), jnp.where(lane >= 64, t, zero)], axis=0)


def _unstack_heads(t):
    n = t.shape[0] // 2
    lane = lax.broadcasted_iota(jnp.int32, (n, 128), 1)
    return jnp.where(lane < 64, t[:n], t[n:])


NA_BIAS_SHAPE = (len(NA_CLASSES), 2 * NA_Q, NA_K)


def _na_bias_pair(flat_ref, out_ref):
    qc = lax.broadcasted_iota(jnp.int32, (GW, 512), 0)
    kc = lax.broadcasted_iota(jnp.int32, (GW, 512), 1) & 63
    start = jnp.clip(qc - 8, 0, GW - 16)
    window = (kc >= start) & (kc < start + 16)
    fill = jnp.full((GW, NA_K - 512), NEG, F32)
    for hh in (0, 1):
        skew = _barrel(pltpu.roll(jnp.broadcast_to(flat_ref[hh], (GW, 1024)), 1024 - 15, 1), left=False)
        by_class = [jnp.where(window, (skew if rc == 7 else pltpu.roll(skew, (9 + rc) * 64, 1))[:, 0:512], NEG)
                    for rc in range(8)]
        for cls in range(len(NA_CLASSES)):
            for qr, (rc, off) in enumerate(_tile_rows(cls)):
                w = jnp.concatenate([by_class[rc], fill], axis=1)
                rows = slice(hh * NA_Q + qr * GW, hh * NA_Q + (qr + 1) * GW)
                out_ref[cls, rows, :] = pltpu.roll(w, off * GW, 1) if off else w


def na_fwd(pna, pnac, bias, mixin, hosted):
    nb = pna.shape[0]

    def body(*refs):
        (p_ref, pc_ref, bias_ref, _), h_in, (out_ref,), h_out, _, h_sems = hosted.split(refs, 4, 1)
        grid_step = pl.program_id(0) * nb + pl.program_id(1)

        @pl.when(grid_step == 0)
        def _():
            hosted.start(h_in, h_out, h_sems)

        kc, vc = pc_ref[:, 128:256], pc_ref[:, 256:384]

        def tile(t, carry):
            qsl, bsl, cls = _na_tile(t)
            kb, vb = p_ref[bsl, 128:256], p_ref[bsl, 256:384]
            p_loc, p_ctx = _na_probs(_stack_heads(p_ref[qsl, 0:128] * 0.125), kb, kc, bias_ref[cls])
            out_ref[qsl, :] = _unstack_heads(_nn(p_loc, vb) + _nn(p_ctx, vc)).astype(BF16)
            return carry

        lax.fori_loop(0, NA_TILES, tile, 0, unroll=4)

        @pl.when(grid_step == NPAIR * nb - 1)
        def _():
            hosted.finish(h_in, h_out, h_sems)

    h_in_specs, h_out_specs = hosted.specs()
    return pl.pallas_call(
        body, name="na_fwd", grid=(NPAIR, nb),
        in_specs=[
            pl.BlockSpec((None, SEQ, 384), lambda p, b: (b, 0, p)),
            pl.BlockSpec((None, LC, 384), lambda p, b: (b, 0, p)),
            pl.BlockSpec((None, len(NA_CLASSES), 2 * NA_Q, NA_K), lambda p, b: (p, 0, 0, 0)),
            pl.BlockSpec(memory_space=pl.ANY),
        ] + h_in_specs,
        out_specs=[pl.BlockSpec((None, SEQ, 128), lambda p, b: (b, 0, 4 + p))] + h_out_specs,
        out_shape=[jax.ShapeDtypeStruct((nb, SEQ, D), BF16)] + hosted.out_shape,
        input_output_aliases={3: 0},
        scratch_shapes=hosted.scratch,
        compiler_params=_cp(("arbitrary", "arbitrary")),
    )(pna, pnac, bias, mixin, *hosted.args)


def na_bwd(pna, pnac, bias, dmixin, dproj, dprojc, hosted):
    nb = pna.shape[0]

    def body(*refs):
        own_in, h_in, own_out, h_out, own_scr, h_sems = hosted.split(refs, 6, 3)
        p_ref, pc_ref, bias_ref, dmix_ref = own_in[:4]
        dp_ref, dpc_ref, dpat_ref = own_out
        dbias_s, dk_s, dv_s, dkc_s, dvc_s, res_s, resc_s = own_scr
        b, part = pl.program_id(1), pl.program_id(2)
        grid_step = (pl.program_id(0) * nb + b) * 3 + part

        @pl.when(grid_step == 0)
        def _():
            hosted.start(h_in, h_out, h_sems)

        @pl.when(grid_step == NPAIR * nb * 3 - 1)
        def _():
            hosted.finish(h_in, h_out, h_sems)

        @pl.when(part == 0)
        def _():
            @pl.when(b == 0)
            def _():
                dbias_s[...] = jnp.zeros_like(dbias_s)

            dk_s[...] = jnp.zeros_like(dk_s)
            dv_s[...] = jnp.zeros_like(dv_s)
            dkc_s[...] = jnp.zeros_like(dkc_s)
            dvc_s[...] = jnp.zeros_like(dvc_s)
            kc, vc = pc_ref[:, 128:256], pc_ref[:, 256:384]

            def tile(t, carry):
                qsl, bsl, cls = _na_tile(t)
                kb, vb = p_ref[bsl, 128:256], p_ref[bsl, 256:384]
                qst, dost = _stack_heads(p_ref[qsl, 0:128] * 0.125), _stack_heads(dmix_ref[qsl, :])
                p_loc, p_ctx = _na_probs(qst, kb, kc, bias_ref[cls])
                dp_loc, dp_ctx = _nt(dost, vb), _nt(dost, vc)
                delta = (jnp.sum(p_loc * dp_loc, axis=1, keepdims=True)
                         + jnp.sum(p_ctx * dp_ctx, axis=1, keepdims=True))
                ds_loc, ds_ctx = p_loc * (dp_loc - delta), p_ctx * (dp_ctx - delta)
                dbias_s[cls] += ds_loc
                res_s[0, qsl, :] = _unstack_heads((_nn(ds_loc, kb) + _nn(ds_ctx, kc)) * 0.125).astype(BF16)
                dk_s[bsl, :] += _tn(ds_loc, qst)
                dv_s[bsl, :] += _tn(p_loc, dost)
                dkc_s[...] += _tn(ds_ctx, qst)
                dvc_s[...] += _tn(p_ctx, dost)
                return carry

            lax.fori_loop(0, NA_TILES, tile, 0, unroll=2)
            res_s[1] = dk_s[...].astype(BF16)
            res_s[2] = dv_s[...].astype(BF16)
            resc_s[0] = jnp.zeros((LC, 128), BF16)
            resc_s[1] = dkc_s[...].astype(BF16)
            resc_s[2] = dvc_s[...].astype(BF16)

            @pl.when(b == nb - 1)
            def _():
                for hh in (0, 1):
                    by_class = [None] * 8
                    for cls in range(len(NA_CLASSES)):
                        for qr, (rc, off) in enumerate(_tile_rows(cls)):
                            w = dbias_s[cls, hh * NA_Q + qr * GW:hh * NA_Q + (qr + 1) * GW, :]
                            w = (pltpu.roll(w, NA_K - off * GW, 1) if off else w)[:, 0:512]
                            by_class[rc] = w if by_class[rc] is None else by_class[rc] + w
                    skew = jnp.zeros((GW, 1024), F32)
                    for rc in range(8):
                        w = jnp.concatenate([by_class[rc], jnp.zeros((GW, 512), F32)], axis=1)
                        skew = skew + (w if rc == 7 else pltpu.roll(w, (7 - rc) * 64, 1))
                    dpat_ref[hh] = jnp.sum(pltpu.roll(_barrel(skew, left=True), 15, 1), axis=0, keepdims=True)

        dp_ref[...] = res_s[part]
        dpc_ref[...] = resc_s[part]

    h_in_specs, h_out_specs = hosted.specs()
    return pl.pallas_call(
        body, name="na_bwd", grid=(NPAIR, nb, 3),
        in_specs=[
            pl.BlockSpec((None, SEQ, 384), lambda p, b, s: (b, 0, p)),
            pl.BlockSpec((None, LC, 384), lambda p, b, s: (b, 0, p)),
            pl.BlockSpec((None, len(NA_CLASSES), 2 * NA_Q, NA_K), lambda p, b, s: (p, 0, 0, 0)),
            pl.BlockSpec((None, SEQ, 128), lambda p, b, s: (b, 0, 4 + p)),
            pl.BlockSpec(memory_space=pl.ANY),
            pl.BlockSpec(memory_space=pl.ANY),
        ] + h_in_specs,
        out_specs=[
            pl.BlockSpec((None, SEQ, 128), lambda p, b, s: (b, 0, 16 + 3 * p + s)),
            pl.BlockSpec((None, LC, 128), lambda p, b, s: (b, 0, 16 + 3 * p + s)),
            pl.BlockSpec((None, 2, 1, 1024), lambda p, b, s: (p, 0, 0, 0)),
        ] + h_out_specs,
        out_shape=[
            jax.ShapeDtypeStruct((nb, SEQ, IN_W), BF16),
            jax.ShapeDtypeStruct((nb, LC, IN_W), BF16),
            jax.ShapeDtypeStruct((NPAIR, 2, 1, 1024), F32),
        ] + hosted.out_shape,
        input_output_aliases={4: 0, 5: 1},
        scratch_shapes=[
            pltpu.VMEM((len(NA_CLASSES), 2 * NA_Q, NA_K), F32),
            pltpu.VMEM((SEQ, 128), F32), pltpu.VMEM((SEQ, 128), F32),
            pltpu.VMEM((LC, 128), F32), pltpu.VMEM((LC, 128), F32),
            pltpu.VMEM((3, SEQ, 128), BF16), pltpu.VMEM((3, LC, 128), BF16),
        ] + hosted.scratch,
        compiler_params=_cp(("arbitrary", "arbitrary", "arbitrary")),
    )(pna, pnac, bias, dmixin, dproj, dprojc, *hosted.args)


def tail_fwd_bwd(x, mixin, tgt, mod3, g_post_mix, g_pre_mlp, g_post_mlp, wout, w1, w2):
    nb = x.shape[0]

    def body(x_ref, mi_ref, tgt_ref, mod_ref, gpm_ref, gpl_ref, gpo_ref, wo_ref, w1_ref, w2_ref,
             dx_ref, dmix_ref, h2_ref, du_ref, a_ref, dm_ref, dmi_ref, dmod_ref, dg_ref, loss_ref):
        b, t = pl.program_id(0), pl.program_id(1)
        gt1, sh2, sc2, gt2 = mod_ref[2:3, :], mod_ref[3:4, :], mod_ref[4:5, :], mod_ref[5:6, :]
        mix = jnp.dot(mi_ref[...], wo_ref[...], preferred_element_type=F32)
        (x1, h2), vjp_a = jax.vjp(_post_mix, x_ref[...], mix, gt1, sc2, sh2, gpm_ref[...], gpl_ref[...])
        h2b = h2.astype(BF16)
        h2_ref[...] = h2b
        m = jnp.zeros((TN, D), F32)
        relus = []
        for j in range(4):
            cols = slice(j * D, (j + 1) * D)
            r = jnp.maximum(jnp.dot(h2b, w1_ref[j], preferred_element_type=F32), 0.0)
            ab = (r * r).astype(BF16)
            a_ref[:, cols] = ab
            m = m + jnp.dot(ab, w2_ref[cols, :], preferred_element_type=F32)
            relus.append(r)
        loss, vjp_b = jax.vjp(_head_loss, x1, m, gt2, gpo_ref[...], tgt_ref[...])
        dx1, dm, dgt2, dgpo, _ = vjp_b(jnp.ones((1, 1), F32))
        dmb = dm.astype(BF16)
        dm_ref[...] = dmb
        dh2 = jnp.zeros((TN, D), F32)
        for j in range(4):
            cols = slice(j * D, (j + 1) * D)
            da = lax.dot_general(dmb, w2_ref[cols, :], (((1,), (1,)), ((), ())), preferred_element_type=F32)
            dub = (da * (2.0 * relus[j])).astype(BF16)
            du_ref[:, cols] = dub
            dh2 = dh2 + lax.dot_general(dub, w1_ref[j], (((1,), (1,)), ((), ())), preferred_element_type=F32)
        dx, dmix, dgt1, dsc2, dsh2, dgpm, dgpl = vjp_a((dx1, dh2))
        dx_ref[...] = dx
        dmixb = dmix.astype(BF16)
        dmix_ref[...] = dmixb
        dmi_ref[...] = lax.dot_general(dmixb, wo_ref[...], (((1,), (1,)), ((), ())),
                                       preferred_element_type=F32).astype(BF16)

        @pl.when(t == 0)
        def _():
            dmod_ref[...] = jnp.zeros_like(dmod_ref)

        @pl.when((t == 0) & (b == 0))
        def _():
            dg_ref[...] = jnp.zeros_like(dg_ref)
            loss_ref[...] = jnp.zeros_like(loss_ref)

        dmod_ref[2:3, :] += dgt1
        dmod_ref[3:4, :] += dsh2
        dmod_ref[4:5, :] += dsc2
        dmod_ref[5:6, :] += dgt2
        dg_ref[0:1, :] += dgpm
        dg_ref[1:2, :] += dgpl
        dg_ref[2:3, :] += dgpo
        loss_ref[...] += jnp.broadcast_to(loss, loss_ref.shape)

    tok = lambda b, t: (b, t, 0)
    const = lambda b, t: (0, 0)
    vec = pl.BlockSpec((1, D), const)
    return pl.pallas_call(
        body, name="tail_fwd_bwd", grid=(nb, SEQ // TN),
        in_specs=[
            pl.BlockSpec((None, TN, D), tok), pl.BlockSpec((None, TN, D), tok), pl.BlockSpec((None, TN, D), tok),
            pl.BlockSpec((None, 6, D), lambda b, t: (b, 0, 0)), vec, vec, vec,
            pl.BlockSpec((D, D), const, pipeline_mode=pl.Buffered(1)),
            pl.BlockSpec((4, D, D), lambda b, t: (0, 0, 0), pipeline_mode=pl.Buffered(1)),
            pl.BlockSpec((DFF, D), const, pipeline_mode=pl.Buffered(1)),
        ],
        out_specs=[
            pl.BlockSpec((None, TN, D), tok), pl.BlockSpec((None, TN, D), tok), pl.BlockSpec((None, TN, D), tok),
            pl.BlockSpec((None, TN, DFF), tok), pl.BlockSpec((None, TN, DFF), tok), pl.BlockSpec((None, TN, D), tok),
            pl.BlockSpec((None, TN, D), tok),
            pl.BlockSpec((None, 6, D), lambda b, t: (b, 0, 0)),
            pl.BlockSpec((8, D), const), pl.BlockSpec((8, 128), const),
        ],
        out_shape=[
            jax.ShapeDtypeStruct((nb, SEQ, D), F32), jax.ShapeDtypeStruct((nb, SEQ, D), BF16),
            jax.ShapeDtypeStruct((nb, SEQ, D), BF16), jax.ShapeDtypeStruct((nb, SEQ, DFF), BF16),
            jax.ShapeDtypeStruct((nb, SEQ, DFF), BF16), jax.ShapeDtypeStruct((nb, SEQ, D), BF16),
            jax.ShapeDtypeStruct((nb, SEQ, D), BF16),
            jax.ShapeDtypeStruct((nb, 6, D), F32), jax.ShapeDtypeStruct((8, D), F32),
            jax.ShapeDtypeStruct((8, 128), F32),
        ],
        compiler_params=_cp(("arbitrary", "arbitrary")),
    )(x, mixin, tgt, mod3, g_post_mix, g_pre_mlp, g_post_mlp, wout, w1, w2)


def weight_grad(pairs, name, out_dtype=F32, col_blocks=False, tm=1024, tn=1024, tk=2048):
    m, n = pairs[0][0].shape[1], pairs[0][1].shape[1]
    tn = min(tn, n)
    tks = [min(tk, xa.shape[0]) for xa, _ in pairs]
    steps = [xa.shape[0] // t for (xa, _), t in zip(pairs, tks)]
    total = sum(steps)
    offs = [sum(steps[:i]) for i in range(len(pairs))]

    def body(*refs):
        out_ref, acc = refs[2 * len(pairs)], refs[-1]
        k = pl.program_id(2)

        @pl.when(k == 0)
        def _():
            acc[...] = jnp.zeros_like(acc)

        for i in range(len(pairs)):
            @pl.when((k >= offs[i]) & (k < offs[i] + steps[i]))
            def _(i=i):
                acc[...] += lax.dot_general(refs[2 * i][...], refs[2 * i + 1][...], (((0,), (0,)), ((), ())),
                                            preferred_element_type=F32)

        if out_dtype != F32:
            @pl.when(k == total - 1)
            def _():
                out_ref[...] = acc[...].astype(out_dtype)

    in_specs, args = [], []
    for i, (xa, ya) in enumerate(pairs):
        clamp = lambda k, i=i: jnp.clip(k - offs[i], 0, steps[i] - 1)
        in_specs.append(pl.BlockSpec((tks[i], tm), lambda a, c, k, clamp=clamp: (clamp(k), a)))
        in_specs.append(pl.BlockSpec((tks[i], tn), lambda a, c, k, clamp=clamp: (clamp(k), c)))
        args += [xa, ya]
    if col_blocks:
        out_spec = pl.BlockSpec((None, tm, tn), lambda a, c, k: (c, a, 0))
        out_shape = jax.ShapeDtypeStruct((n // tn, m, tn), out_dtype)
    else:
        out_spec = pl.BlockSpec((tm, tn), lambda a, c, k: (a, c))
        out_shape = jax.ShapeDtypeStruct((m, n), out_dtype)
    return pl.pallas_call(
        body, name=name, grid=(m // tm, n // tn, total), in_specs=in_specs, out_specs=out_spec, out_shape=out_shape,
        scratch_shapes=[] if out_dtype == F32 else [pltpu.VMEM((tm, tn), F32)],
        compiler_params=_cp(("arbitrary", "arbitrary", "arbitrary")),
    )(*args)


def _perm_block(t):
    return 4 * (t % 4) + t // 4 if t < 16 else 16 + 3 * ((t - 16) % 4) + (t - 16) // 4


def _is_rope_block(p):
    return p < 16 and p % 4 < 2


def unpack_w_in(blocks):
    def body(i_ref, o_ref):
        for t in range(28):
            p = _perm_block(t)
            blk = i_ref[t // 7, :, (t % 7) * 128:(t % 7 + 1) * 128]
            if _is_rope_block(p):
                blk = _pair_order(blk.astype(F32)).astype(BF16)
            o_ref[:, p * 128:(p + 1) * 128] = blk

    return pl.pallas_call(
        body, name="unpack_w_in", grid=(2,),
        in_specs=[pl.BlockSpec((4, D // 2, 896), lambda i: (0, i, 0))],
        out_specs=pl.BlockSpec((D // 2, IN_W), lambda i: (i, 0)),
        out_shape=jax.ShapeDtypeStruct((D, IN_W), BF16),
    )(blocks)


def pack_w_in(dw):
    def body(i_ref, o_ref):
        for t in range(28):
            p = _perm_block(t)
            blk = i_ref[:, p * 128:(p + 1) * 128]
            if _is_rope_block(p):
                blk = _pair_order(blk)
            o_ref[t // 7, :, (t % 7) * 128:(t % 7 + 1) * 128] = blk.astype(BF16)

    return pl.pallas_call(
        body, name="pack_w_in", grid=(4,),
        in_specs=[pl.BlockSpec((D // 4, IN_W), lambda i: (i, 0))],
        out_specs=pl.BlockSpec((4, D // 4, 896), lambda i: (0, i, 0)),
        out_shape=jax.ShapeDtypeStruct((4, D, 896), BF16),
    )(dw)


def _place():
    return lax.axis_index("x"), lax.axis_index("y"), lax.axis_index("c")


class Hosted:
    def __init__(self, args, out_shape, scratch, start, finish):
        self.args, self.out_shape, self.scratch, self.start, self.finish = args, out_shape, scratch, start, finish

    def specs(self):
        hbm = pl.BlockSpec(memory_space=pl.ANY)
        return [hbm] * len(self.args), [hbm] * len(self.out_shape)

    def split(self, refs, n_in, n_out):
        a, b = len(self.args), len(self.out_shape)
        cuts = [n_in, n_in + a, n_in + a + n_out, n_in + a + n_out + b, len(refs) - len(self.scratch)]
        parts = [refs[i:j] for i, j in zip([0] + cuts, cuts + [len(refs)])]
        return parts[0], parts[1], parts[2], parts[3], parts[4], parts[5]


def no_exchange():
    return Hosted([], [], [], lambda *a: None, lambda *a: None)


def run_hosted(hosted, name):
    def body(*refs):
        _, ins, _, outs, _, sems = hosted.split(refs, 0, 0)
        hosted.start(ins, outs, sems)
        hosted.finish(ins, outs, sems)

    in_specs, out_specs = hosted.specs()
    return pl.pallas_call(body, name=name, in_specs=in_specs, out_specs=out_specs, out_shape=hosted.out_shape,
                          scratch_shapes=hosted.scratch)(*hosted.args)


def gather8(blocks, relay_diagonal=False):
    na = len(blocks)

    def copies(ins, outs, sems):
        send_sems, recv_sems, local_sem = sems
        x, y, c = _place()
        me, sibling = (x, y, c), (x, y, 1 - c)
        chips = [(1 - x, y), (x, 1 - y), (1 - x, 1 - y)]

        def slot(o_ref, px, py, pc, half=None):
            ref = o_ref.at[4 * px + 2 * py + pc]
            if half is None:
                return ref
            rows = ref.shape[0] // 2
            return ref.at[pl.ds(half * rows, rows)]

        def copy(a, k, block, to, src=None, half=None):
            return pltpu.make_async_remote_copy(
                src_ref=slot(outs[a], *block, half) if src is None else src, dst_ref=slot(outs[a], *block, half),
                send_sem=send_sems.at[a, k], recv_sem=recv_sems.at[a, k], device_id=to, device_id_type=MESH)

        mine = [pltpu.make_async_copy(ins[a], slot(outs[a], *me), local_sem.at[a]) for a in range(na)]
        first = []
        for a in range(na):
            first.append(copy(a, 0, me, sibling, src=ins[a]))
            first += [copy(a, 1 + j, me, (*chip, c), src=ins[a])
                      for j, chip in enumerate(chips[:2] if relay_diagonal else chips)]
        return copy, mine, first, me, sibling, chips, c

    def start(ins, outs, sems):
        _, mine, first, *_ = copies(ins, outs, sems)
        for cp in mine + first:
            cp.start()

    def hops(copy, me, sibling, chips, c):
        pairs = []
        for j, chip in enumerate(chips[:2] if relay_diagonal else chips):
            for a in range(na):
                onward = [copy(a, 4 + j, (*chip, c), sibling)]
                if relay_diagonal:
                    onward.insert(0, copy(a, (3, 7)[j], (*chip, c), (*chips[1 - j], c), half=j))
                pairs.append(((a, 1 + j, (*chip, c)), onward))
        return pairs

    def relay(ins, outs, sems):
        copy, _, _, me, sibling, chips, c = copies(ins, outs, sems)
        for arrival, onward in hops(copy, me, sibling, chips, c):
            copy(*arrival, me).wait_recv()
            for cp in onward:
                cp.start()

    def finish(ins, outs, sems):
        copy, mine, first, me, sibling, chips, c = copies(ins, outs, sems)
        if not relay_diagonal:
            relay(ins, outs, sems)
        passed = [cp for _, onward in hops(copy, me, sibling, chips, c) for cp in onward]
        if relay_diagonal:
            for a in range(na):
                copy(a, 3, (*chips[2], c), me, half=0).wait_recv()
                copy(a, 7, (*chips[2], c), me, half=1).wait_recv()
                cp = copy(a, 6, (*chips[2], c), sibling)
                cp.start()
                passed.append(cp)
        for a in range(na):
            copy(a, 0, sibling, me).wait_recv()
            for j, chip in enumerate(chips):
                copy(a, 4 + j, (*chip, 1 - c), me).wait_recv()
        for cp in first + passed:
            cp.wait_send()
        for cp in mine:
            cp.wait()

    hosted = Hosted(list(blocks), [jax.ShapeDtypeStruct((8,) + b.shape, b.dtype) for b in blocks],
                    [pltpu.SemaphoreType.DMA((na, 8)), pltpu.SemaphoreType.DMA((na, 8)), pltpu.SemaphoreType.DMA((na,))],
                    start, finish)
    hosted.relay = relay
    return hosted


def chips3(arrays):
    na = len(arrays)

    def copies(ins, outs, sems):
        send_sems, recv_sems = sems
        x, y, c = _place()
        return [pltpu.make_async_remote_copy(
            src_ref=ins[a].at[2 * px + py], dst_ref=outs[a].at[k], send_sem=send_sems.at[a, k],
            recv_sem=recv_sems.at[a, k], device_id=(px, py, c), device_id_type=MESH)
            for a in range(na) for k, (px, py) in enumerate([(1 - x, y), (x, 1 - y), (1 - x, 1 - y)])]

    def start(ins, outs, sems):
        for cp in copies(ins, outs, sems):
            cp.start()

    def finish(ins, outs, sems):
        for cp in copies(ins, outs, sems):
            cp.wait()

    return Hosted(list(arrays), [jax.ShapeDtypeStruct((3,) + a.shape[1:], a.dtype) for a in arrays],
                  [pltpu.SemaphoreType.DMA((na, 3)), pltpu.SemaphoreType.DMA((na, 3))], start, finish)


def siblings(arrays):
    na = len(arrays)

    def copies(ins, outs, sems):
        send_sems, recv_sems = sems
        x, y, c = _place()
        return [pltpu.make_async_remote_copy(
            src_ref=ins[a], dst_ref=outs[a], send_sem=send_sems.at[a], recv_sem=recv_sems.at[a],
            device_id=(x, y, 1 - c), device_id_type=MESH) for a in range(na)]

    def start(ins, outs, sems):
        for cp in copies(ins, outs, sems):
            cp.start()

    def finish(ins, outs, sems):
        for cp in copies(ins, outs, sems):
            cp.wait()

    return Hosted(list(arrays), [jax.ShapeDtypeStruct(a.shape, a.dtype) for a in arrays],
                  [pltpu.SemaphoreType.DMA((na,)), pltpu.SemaphoreType.DMA((na,))], start, finish)


def both(first, second):
    na, no, ns = len(first.args), len(first.out_shape), len(first.scratch)

    def start(ins, outs, sems):
        first.start(ins[:na], outs[:no], sems[:ns])
        second.start(ins[na:], outs[no:], sems[ns:])

    def finish(ins, outs, sems):
        first.finish(ins[:na], outs[:no], sems[:ns])
        second.finish(ins[na:], outs[no:], sems[ns:])

    return Hosted(first.args + second.args, first.out_shape + second.out_shape, first.scratch + second.scratch,
                  start, finish)


def siblings4(arrays):
    na = len(arrays)

    def copies(ins, outs, sems):
        send_sems, recv_sems = sems
        x, y, c = _place()
        return [pltpu.make_async_remote_copy(
            src_ref=ins[a].at[2 * j + 1 - c], dst_ref=outs[a].at[j],
            send_sem=send_sems.at[a, j], recv_sem=recv_sems.at[a, j],
            device_id=(x, y, 1 - c), device_id_type=MESH) for a in range(na) for j in range(4)]

    def start(ins, outs, sems):
        for cp in copies(ins, outs, sems):
            cp.start()

    def finish(ins, outs, sems):
        for cp in copies(ins, outs, sems):
            cp.wait()

    return Hosted(list(arrays), [jax.ShapeDtypeStruct((4,) + a.shape[1:], a.dtype) for a in arrays],
                  [pltpu.SemaphoreType.DMA((na, 4)), pltpu.SemaphoreType.DMA((na, 4))], start, finish)


def _row_tile(r):
    for cand in (512, 256, 128, 64, 32, 16, 8):
        if r % cand == 0:
            return cand
    return r


def chip_partial(place, g8s, landed4s, name):
    n = len(g8s)

    def body(place_ref, *refs):
        del place_ref
        for g_ref, l_ref, o_ref in zip(refs[:n], refs[n:2 * n], refs[2 * n:]):
            o_ref[...] = (g_ref[...].astype(F32) + l_ref[...].astype(F32)).astype(BF16)

    own = [pl.BlockSpec((None,) + g.shape[1:], lambda j, s: (2 * j + s[0], 0, 0)) for g in g8s]
    plain = [pl.BlockSpec((None,) + g.shape[1:], lambda j, s: (j, 0, 0)) for g in g8s]
    return pl.pallas_call(
        body, name=name,
        grid_spec=pltpu.PrefetchScalarGridSpec(num_scalar_prefetch=1, grid=(4,), in_specs=own + plain, out_specs=plain),
        out_shape=[jax.ShapeDtypeStruct((4,) + g.shape[1:], BF16) for g in g8s],
    )(place, *g8s, *landed4s)


def shard_sum(place, partial4s, landed3s, name):
    n = len(partial4s)

    def body(place_ref, *refs):
        del place_ref
        for p_ref, l_ref, o_ref in zip(refs[:n], refs[n:2 * n], refs[2 * n:]):
            acc = p_ref[...].astype(F32)
            for k in range(3):
                acc = acc + l_ref[k].astype(F32)
            o_ref[...] = acc

    def halves(p, lead):
        r, ccols = p.shape[1:]
        return (lead, r // 2, ccols)

    return pl.pallas_call(
        body, name=name,
        grid_spec=pltpu.PrefetchScalarGridSpec(
            num_scalar_prefetch=1, grid=(2,),
            in_specs=[pl.BlockSpec(halves(p, None), lambda i, s: (s[1], i, 0)) for p in partial4s]
            + [pl.BlockSpec(halves(p, 3), lambda i, s: (0, i, 0)) for p in partial4s],
            out_specs=[pl.BlockSpec(halves(p, None)[1:], lambda i, s: (i, 0)) for p in partial4s]),
        out_shape=[jax.ShapeDtypeStruct(p.shape[1:], F32) for p in partial4s],
    )(place, *partial4s, *landed3s)


def _adamw_math(w, g, m, v):
    m2 = B1 * m + (1.0 - B1) * g
    v2 = B2 * v + (1.0 - B2) * (g * g)
    m_hat = m2 / (1.0 - B1 ** STEP)
    v_hat = v2 / (1.0 - B2 ** STEP)
    return -LR * (m_hat / (jnp.sqrt(v_hat) + AEPS) + WD * w), m2, v2


def adamw_halves(place, w, mine, theirs, m, v, name):
    r, ccols = w.shape
    hr = r // 2
    tr = _row_tile(hr)
    nt = hr // tr

    def body(place_ref, w_ref, a_ref, b_ref, m_ref, v_ref, g_out, d_out, m_out, v_out):
        g = jnp.where(pl.program_id(0) == place_ref[0], a_ref[...], b_ref[...])
        d, m2, v2 = _adamw_math(w_ref[...], g, m_ref[...], v_ref[...])
        g_out[...] = g
        d_out[...] = d
        m_out[...] = m2
        v_out[...] = v2

    full = pl.BlockSpec((tr, ccols), lambda h, i, s: (h * nt + i, 0))
    part = pl.BlockSpec((tr, ccols), lambda h, i, s: (i, 0))
    return pl.pallas_call(
        body, name=name,
        grid_spec=pltpu.PrefetchScalarGridSpec(
            num_scalar_prefetch=1, grid=(2, nt), in_specs=[full, part, part, full, full], out_specs=[full] * 4),
        out_shape=[jax.ShapeDtypeStruct((r, ccols), F32)] * 4,
    )(place, w, mine, theirs, m, v)


def adamw_group(place, halved, plain, hosted, name):
    rows = halved[0][0].shape[0]
    tr = 128
    nt = rows // 2 // tr
    nh, npl = len(halved), len(plain)

    def body(place_ref, *refs):
        own_in, h_in, own_out, h_out, _, h_sems = hosted.split(refs, 5 * nh + 4 * npl, 4 * nh + 3 * npl)
        half = pl.program_id(0)
        grid_step = half * nt + pl.program_id(1)

        @pl.when(grid_step == 0)
        def _():
            hosted.start(h_in, h_out, h_sems)

        for i in range(nh):
            w_ref, a_ref, b_ref, m_ref, v_ref = own_in[5 * i:5 * i + 5]
            g = jnp.where(half == place_ref[0], a_ref[...], b_ref[...])
            res = (g,) + _adamw_math(w_ref[...], g, m_ref[...], v_ref[...])
            for o_ref, r in zip(own_out[4 * i:4 * i + 4], res):
                o_ref[...] = r
        for i in range(npl):
            w_ref, g_ref, m_ref, v_ref = own_in[5 * nh + 4 * i:5 * nh + 4 * i + 4]
            res = _adamw_math(w_ref[...], g_ref[...], m_ref[...], v_ref[...])
            for o_ref, r in zip(own_out[4 * nh + 3 * i:4 * nh + 3 * i + 3], res):
                o_ref[...] = r

        @pl.when(grid_step == 2 * nt - 1)
        def _():
            hosted.finish(h_in, h_out, h_sems)

    def full(cols):
        return pl.BlockSpec((tr, cols), lambda h, i, s: (h * nt + i, 0))

    def part(cols):
        return pl.BlockSpec((tr, cols), lambda h, i, s: (i, 0))

    in_specs, out_specs, out_shape, args = [], [], [], []
    for w, a, b, m, v in halved:
        cols = w.shape[1]
        in_specs += [full(cols), part(cols), part(cols), full(cols), full(cols)]
        out_specs += [full(cols)] * 4
        out_shape += [jax.ShapeDtypeStruct(w.shape, F32)] * 4
        args += [w, a, b, m, v]
    for w, g, m, v in plain:
        cols = w.shape[1]
        in_specs += [full(cols)] * 4
        out_specs += [full(cols)] * 3
        out_shape += [jax.ShapeDtypeStruct(w.shape, F32)] * 3
        args += [w, g, m, v]
    h_in_specs, h_out_specs = hosted.specs()
    return pl.pallas_call(
        body, name=name,
        grid_spec=pltpu.PrefetchScalarGridSpec(
            num_scalar_prefetch=1, grid=(2, nt), in_specs=in_specs + h_in_specs, out_specs=out_specs + h_out_specs,
            scratch_shapes=hosted.scratch),
        out_shape=out_shape + hosted.out_shape,
        compiler_params=_cp(("arbitrary", "arbitrary")),
    )(place, *args, *hosted.args)


def _silu(x):
    return x * jax.nn.sigmoid(x)


def prologue(c_rows, c_ctx_row, w_ada, b_shard, rpb_flat, half_w_in, late_shards):
    shape = jax.ShapeDtypeStruct
    n_late = len(late_shards)
    half_shapes = [(w.shape[0] // 2, w.shape[1]) for w in late_shards]
    g_w = gather8([half_w_in], relay_diagonal=True)
    g_c = gather8([shape((8, D), F32)])
    g_m = gather8([shape((32, 1536), F32)])

    def body(*refs):
        c_ref, cc_ref, w_ref, b_ref, flat_ref, hw_ref = refs[:6]
        late_refs = refs[6:6 + n_late]
        cin_ref, mg_ref, gw_ref, bias_ref, cos_ref, sin_ref = refs[6 + n_late:12 + n_late]
        rest = refs[12 + n_late:]
        half_refs, (cg_s, ms_s, bias_s) = rest[:n_late], rest[n_late:n_late + 3]
        stage, (load_sem, bias_sem), sems = rest[n_late + 3:2 * n_late + 3], rest[2 * n_late + 3:2 * n_late + 5], \
            rest[2 * n_late + 5:]
        sw, sc, sm = sems[0:3], sems[3:6], sems[6:9]
        core = lax.axis_index("c")
        g_c.start([c_ref], [cg_s], sc)
        g_w.start([hw_ref], [gw_ref], sw)
        loads = [pltpu.make_async_copy(late_refs[a].at[pl.ds(core * half_shapes[a][0], half_shapes[a][0]), :],
                                       stage[a], load_sem.at[a]) for a in range(n_late)]
        for cp in loads:
            cp.start()
        g_c.finish([c_ref], [cg_s], sc)
        cin_ref[...] = jnp.zeros_like(cin_ref)
        for dev in range(8):
            cin_ref[2 * dev:2 * dev + 2, :] = cg_s[dev, 0:2, :]
        cin_ref[16:17, :] = cc_ref[...]
        ms_s[...] = _nn(_silu(cin_ref[...]), w_ref[...]) + b_ref[...]
        g_m.start([ms_s], [mg_ref], sm)
        for a, cp in enumerate(loads):
            cp.wait()
            half_refs[a][...] = stage[a][...].astype(BF16)
        cos_ref[...], sin_ref[...] = _rope_tables()
        stores = []
        for pair in range(NPAIR):
            if pair >= 2:
                stores[pair - 2].wait()
            if pair == 2:
                g_w.relay([hw_ref], [gw_ref], sw)
            _na_bias_pair(flat_ref.at[pair], bias_s.at[pair % 2])
            stores.append(pltpu.make_async_copy(bias_s.at[pair % 2], bias_ref.at[pair], bias_sem.at[pair % 2]))
            stores[pair].start()
        for cp in stores[-2:]:
            cp.wait()
        g_w.finish([hw_ref], [gw_ref], sw)
        g_m.finish([ms_s], [mg_ref], sm)

    vmem = pl.BlockSpec(memory_space=pltpu.VMEM)
    hbm = pl.BlockSpec(memory_space=pl.ANY)
    return pl.pallas_call(
        body, name="prologue", in_specs=[vmem, vmem, vmem, vmem, vmem, hbm] + [hbm] * n_late,
        out_specs=[vmem, vmem, hbm, hbm, vmem, vmem] + [vmem] * n_late,
        out_shape=[shape((32, D), F32), shape((8, 32, 1536), F32)] + g_w.out_shape
        + [shape((NPAIR,) + NA_BIAS_SHAPE, F32)] + [shape((SEQ, RD), F32)] * 2 + [shape(s, BF16) for s in half_shapes],
        scratch_shapes=[pltpu.VMEM((8, 8, D), F32), pltpu.VMEM((32, 1536), F32), pltpu.VMEM((2,) + NA_BIAS_SHAPE, F32)]
        + [pltpu.VMEM(s, F32) for s in half_shapes]
        + [pltpu.SemaphoreType.DMA((n_late,)), pltpu.SemaphoreType.DMA((2,))]
        + g_w.scratch + g_c.scratch + g_m.scratch,
        compiler_params=_cp(),
    )(c_rows, c_ctx_row, w_ada, b_shard, rpb_flat, half_w_in, *late_shards)


def ada_grads(cin, gb, gc, w_ada):
    def body(c_ref, gb_ref, gc_ref, w_ref, gw_ref, pc_ref):
        ctx_tot = jnp.sum(gc_ref[...], axis=0, keepdims=True)
        rows = lax.broadcasted_iota(jnp.int32, (16, 512), 0)
        dm = jnp.concatenate([gb_ref[...], jnp.where(rows == 0, ctx_tot, 0.0)], axis=0)
        gw_ref[...] = _tn(_silu(c_ref[...]), dm)
        rows8 = lax.broadcasted_iota(jnp.int32, (8, 512), 0)
        part = _nt(jnp.where(rows8 == 0, ctx_tot, 0.0), w_ref[...])

        @pl.when(pl.program_id(0) == 0)
        def _():
            pc_ref[...] = jnp.zeros_like(pc_ref)

        pc_ref[...] += part

    return pl.pallas_call(
        body, name="ada_grads", grid=(3,),
        in_specs=[pl.BlockSpec((32, D), lambda j: (0, 0)), pl.BlockSpec((16, 512), lambda j: (0, j)),
                  pl.BlockSpec((8, 512), lambda j: (0, j)), pl.BlockSpec((D, 512), lambda j: (0, j))],
        out_specs=[pl.BlockSpec((D, 512), lambda j: (0, j)), pl.BlockSpec((8, D), lambda j: (0, 0))],
        out_shape=[jax.ShapeDtypeStruct((D, 1536), F32), jax.ShapeDtypeStruct((8, D), F32)],
    )(cin, gb, gc, w_ada)


SMALL_SUM_ROWS = 15


def small_update(gsm, gbf, gcf, pcg, params):
    n = len(params)

    def body(*refs):
        gsm_ref, gbf_ref, gcf_ref, pcg_ref = refs[:4]
        wmv, outs, loss_out = refs[4:4 + 3 * n], refs[4 + 3 * n:4 + 7 * n], refs[-1]
        acc = gsm_ref[0]
        for dev in range(1, 8):
            acc = acc + gsm_ref[dev]
        c_ctx = wmv[0][...]
        sg = jax.nn.sigmoid(c_ctx)
        dsilu = pcg_ref[0:1, :] + pcg_ref[2:3, :] + pcg_ref[4:5, :] + pcg_ref[6:7, :]
        lane = lax.broadcasted_iota(jnp.int32, (1, D), 1)
        last = acc[14:15, :]
        grads = [
            dsilu * (sg * (1.0 + c_ctx * (1.0 - sg))),
            jnp.sum(gbf_ref[...], axis=0, keepdims=True) + jnp.sum(gcf_ref[...], axis=0, keepdims=True),
            acc[0:1, :] + acc[1:2, :], acc[2:3, :], acc[3:4, :], acc[4:5, :],
            acc[5:6, 0:512], acc[6:14, :], jnp.where(lane < 8, last, 0.0),
        ]
        loss_out[...] = jnp.broadcast_to(jnp.sum(jnp.where(lane == 8, last, 0.0), axis=1, keepdims=True), (8, 128))
        for i, g in enumerate(grads):
            d, m2, v2 = _adamw_math(wmv[3 * i][...], g, wmv[3 * i + 1][...], wmv[3 * i + 2][...])
            outs[4 * i][...] = g
            outs[4 * i + 1][...] = d
            outs[4 * i + 2][...] = m2
            outs[4 * i + 3][...] = v2

    flat = [a for wmv in params for a in wmv]
    out_shape = [jax.ShapeDtypeStruct(w.shape, F32) for w, _, _ in params for _ in range(4)]
    return pl.pallas_call(
        body, name="small_update", out_shape=out_shape + [jax.ShapeDtypeStruct((8, 128), F32)],
    )(gsm, gbf, gcf, pcg, *flat)


def _pad_row(v, rows):
    flat = v.reshape(-1)
    return jnp.pad(flat, (0, rows * D - flat.shape[0])).reshape(rows, D)


def local_step(x, ctx, tgt, mod3, rope, bias, g_pre_mix, g_post_mix, g_pre_mlp, g_post_mlp, ret_decay, ret_gn,
               wperm, late_weights, early_grads):
    nb = x.shape[0]
    tokens = nb * SEQ
    cos, sin = rope
    rd = ret_decay.T.reshape(RH, 2, 1)
    gn = ret_gn.reshape(RH, 1, RD)
    h, pret, pna = premix_proj(x, mod3, g_pre_mix, wperm, False, "premix_proj")
    hc, pretc, pnac = premix_proj(ctx, mod3, g_pre_mix, wperm, True, "premix_proj_ctx")
    o_all, mixin, gw_out = retention_fwd(pret, pretc, rd, gn, cos, sin, late_weights(0))
    mixin, gw1, gw2 = na_fwd(pna, pnac, bias, mixin, late_weights(1))
    dx_tail, dmix, h2, du, act, dm, dmixin, dmod_t, dg_t, loss_t = tail_fwd_bwd(
        x, mixin, tgt, mod3, g_post_mix, g_pre_mlp, g_post_mlp, gw_out.reshape(D, D), gw1.reshape(4, D, D),
        gw2.reshape(DFF, D))
    dw_out = weight_grad([(mixin.reshape(tokens, D), dmix.reshape(tokens, D))], "grad_w_out", BF16)
    dw1 = weight_grad([(h2.reshape(tokens, D), du.reshape(tokens, DFF))], "grad_w_mlp1", BF16, col_blocks=True)
    dw2 = weight_grad([(act.reshape(tokens, DFF), dm.reshape(tokens, D))], "grad_w_mlp2", BF16)
    dproj, dprojc, drd, dgn, *landed = retention_bwd(pret, pretc, o_all, dmixin, rd, gn, cos, sin,
                                                     early_grads[0](dw_out, dw1, dw2))
    dproj, dprojc, dpat, *early = na_bwd(pna, pnac, bias, dmixin, dproj, dprojc, early_grads[1](landed))
    dw_in = weight_grad([(h.reshape(tokens, D), dproj.reshape(tokens, IN_W)),
                         (hc.reshape(nb * LC, D), dprojc.reshape(nb * LC, IN_W))], "grad_w_in", tn=IN_W // 2, tk=1024)
    dmod_c, dg_c, *late = premix_bwd(ctx, mod3, g_pre_mix, wperm, dprojc, None, early_grads[2](dw_in), "premix_bwd_ctx")
    grad_x, dmod_a, dg_a, *late = premix_bwd(x, mod3, g_pre_mix, wperm, dproj, dx_tail, early_grads[3](late),
                                             "premix_bwd")
    dmod = jnp.concatenate([jnp.concatenate([dmod_a[:, 0:2], dmod_t[:, 2:6]], axis=1), dmod_c], axis=0)
    last = jnp.pad(jnp.concatenate([drd[:, :, 0].T.reshape(8), loss_t[0, 0:1]]), (0, D - 9)).reshape(1, D)
    small = jnp.concatenate([dg_a[0:1], dg_c[0:1], dg_t[0:3], _pad_row(dgn, 1), dpat.reshape(8, D), last], axis=0)
    return grad_x, late, early, dmod, small


def kernel(x, c, ctx, c_ctx, w_ada, b_ada, g_pre_mix, g_post_mix, g_pre_mlp, g_post_mlp, w_in, ret_decay, ret_gn, na_rpb, w_out, w_mlp1, w_mlp2, loss_target, m_c_ctx, m_w_ada, m_b_ada, m_g_pre_mix, m_g_post_mix, m_g_pre_mlp, m_g_post_mlp, m_w_in, m_ret_decay, m_ret_gn, m_na_rpb, m_w_out, m_w_mlp1, m_w_mlp2, v_c_ctx, v_w_ada, v_b_ada, v_g_pre_mix, v_g_post_mix, v_g_pre_mlp, v_g_post_mlp, v_w_in, v_ret_decay, v_ret_gn, v_na_rpb, v_w_out, v_w_mlp1, v_w_mlp2):
    px, py, pc = _place()
    dev = 4 * px + 2 * py + pc
    chip = 2 * px + py

    half_w_in = lax.dynamic_slice_in_dim(w_in[0], pc * (D // 2), D // 2, 0).astype(BF16)
    cin, mg, gw_in, bias, cos, sin, *late_halves = prologue(
        jnp.pad(c, ((0, 6), (0, 0))), c_ctx[None], w_ada[0], lax.dynamic_slice_in_dim(b_ada, chip * 1536, 1536, 1),
        _rpb_flat(na_rpb[0]), half_w_in, [w_out[0], w_mlp1[0], w_mlp2[0]])
    halves = [half_w_in] + late_halves
    wperm = unpack_w_in(gw_in.reshape(4, D, 896))
    mod_all = jnp.concatenate([mg[0], mg[2], mg[4], mg[6]], axis=1)
    mod3 = (jnp.pad(lax.dynamic_slice_in_dim(mod_all, 2 * dev, 2, 0), ((0, 1), (0, 0)))
            + jnp.pad(mod_all[16:17], ((2, 0), (0, 0)))).reshape(3, 6, D)

    place = jnp.stack([pc, chip]).astype(jnp.int32)

    early_names = ["w_out", "w_mlp1", "w_mlp2"]
    early_g8, early_partial = [], []

    def early_a(dw_out, dw1, dw2):
        early_g8[:] = [dw_out.reshape(8, 128, D), dw1.reshape(8, 512, D), dw2.reshape(8, 512, D)]
        return siblings4(early_g8)

    def early_b(landed):
        early_partial[:] = chip_partial(place, early_g8, landed, "rs_chip_sum_early")
        return chips3(early_partial)

    late_partial = []

    late_g8 = []

    def late_c(dw_in):
        late_g8[:] = [pack_w_in(dw_in).reshape(8, 512, 896)]
        return siblings4(late_g8)

    def late_d(landed):
        late_partial[:] = chip_partial(place, late_g8, landed, "rs_chip_sum_w_in")
        return chips3(late_partial)

    grad_x, (landed3_in,), early_landed, dmod, small = local_step(
        x, ctx, loss_target, mod3, (cos, sin), bias, g_pre_mix, g_post_mix, g_pre_mlp, g_post_mlp, ret_decay[0], ret_gn,
        wperm, lambda k: gather8(halves[1:2] if k == 0 else halves[2:4]), (early_a, early_b, late_c, late_d))
    early_mine = shard_sum(place, early_partial, early_landed, "rs_shard_sum_early")

    pay = jnp.concatenate([dmod.reshape(18, D), small, jnp.zeros((40 - 18 - SMALL_SUM_ROWS, D), F32)], axis=0)
    *early_theirs, gs = run_hosted(both(siblings(early_mine), gather8([pay])), "rs_halves_early_gather_small")
    gbf = gs[:, 0:12].reshape(16, 6 * D)
    gcf = gs[:, 12:18].reshape(8, 6 * D)
    gw_ada, pc_part = ada_grads(cin, lax.dynamic_slice_in_dim(gbf, chip * 1536, 1536, 1),
                                lax.dynamic_slice_in_dim(gcf, chip * 1536, 1536, 1), w_ada[0])
    (mine_in,) = shard_sum(place, late_partial, [landed3_in], "rs_shard_sum_w_in")
    theirs_in, pcg = run_hosted(both(siblings([mine_in]), gather8([pc_part])), "rs_halves_w_in_gather_c_ctx")

    grouped = adamw_group(
        place,
        [(w_mlp1[0], early_mine[1], early_theirs[1], m_w_mlp1[0], v_w_mlp1[0]),
         (w_mlp2[0], early_mine[2], early_theirs[2], m_w_mlp2[0], v_w_mlp2[0])],
        [(w_ada[0], gw_ada, m_w_ada[0], v_w_ada[0])], no_exchange(), "adamw_group")
    d_ada, m_ada, v_ada = grouped[8:11]
    big = [
        [r[None] for r in adamw_halves(place, w_in[0], mine_in, theirs_in, m_w_in[0], v_w_in[0], "adamw_w_in")],
        [r[None] for r in adamw_halves(place, w_out[0], early_mine[0], early_theirs[0], m_w_out[0], v_w_out[0],
                                       "adamw_w_out")],
        [r[None] for r in grouped[0:4]], [r[None] for r in grouped[4:8]],
    ]

    def rpb_rows(t):
        return _rpb_flat(t[0]).reshape(8, D)

    def decay_row(t):
        return jnp.pad(t.reshape(1, 8), ((0, 0), (0, D - 8)))

    views = [lambda t: t.reshape(1, D), lambda t: t, lambda t: t, lambda t: t, lambda t: t, lambda t: t, lambda t: t,
             rpb_rows, decay_row]
    back = [lambda t: t.reshape(D), lambda t: t, lambda t: t, lambda t: t, lambda t: t, lambda t: t, lambda t: t,
            lambda t: _rpb_flat_t(t)[None], lambda t: t[:, 0:8].reshape(1, 2, 4)]
    small_w = (c_ctx, b_ada, g_pre_mix, g_post_mix, g_pre_mlp, g_post_mlp, ret_gn, na_rpb, ret_decay)
    small_m = (m_c_ctx, m_b_ada, m_g_pre_mix, m_g_post_mix, m_g_pre_mlp, m_g_post_mlp, m_ret_gn, m_na_rpb, m_ret_decay)
    small_v = (v_c_ctx, v_b_ada, v_g_pre_mix, v_g_post_mix, v_g_pre_mlp, v_g_post_mlp, v_ret_gn, v_na_rpb, v_ret_decay)
    *res, loss8 = small_update(gs[:, 18:18 + SMALL_SUM_ROWS], gbf, gcf, pcg[:, 0],
                               [(f(w), f(m), f(v)) for f, w, m, v in zip(views, small_w, small_m, small_v)])

    def leaves(ada, idx):
        s_c, s_b, s_g1, s_g2, s_g3, s_g4, s_gn, s_rpb, s_rd = [back[i](res[4 * i + idx]) for i in range(9)]
        return [s_c, ada[None], s_b, s_g1, s_g2, s_g3, s_g4, big[0][idx], s_rd, s_gn, s_rpb,
                big[1][idx], big[2][idx], big[3][idx]]

    return (loss8[0, 0], grad_x, *leaves(gw_ada, 0), *leaves(d_ada, 1), *leaves(m_ada, 2), *leaves(v_ada, 3))
```

```python
import functools
import math

import jax
import jax.numpy as jnp
from jax import lax
from jax.experimental import pallas as pl
from jax.experimental.pallas import tpu as pltpu

F32, BF16 = jnp.float32, jnp.bfloat16
D = 1024
SEQ = 2048
LC = 256
GW = 64
RH, RD, CH = 4, 128, 128
NPAIR = 4
IN_W = 3584
RET_W = 2048
DFF = 4096
EPS = 1e-6
NEG = -1e30
TN = 256
NCH = SEQ // CH
LR, B1, B2, AEPS, WD, STEP = 0.001, 0.9, 0.999, 1e-08, 0.01, 10
MESH = pl.DeviceIdType.MESH
VMEM_LIMIT = 56 * 1024 * 1024


def _cp(sem=None):
    return pltpu.CompilerParams(dimension_semantics=sem, vmem_limit_bytes=VMEM_LIMIT)


def _nn(a, b):
    return jnp.dot(a.astype(BF16), b.astype(BF16), preferred_element_type=F32)


def _nt(a, b):
    return lax.dot_general(a.astype(BF16), b.astype(BF16), (((1,), (1,)), ((), ())), preferred_element_type=F32)


def _tn(a, b):
    return lax.dot_general(a.astype(BF16), b.astype(BF16), (((0,), (0,)), ((), ())), preferred_element_type=F32)


@jax.custom_vjp
def mm_tn(a, b):
    return _tn(a, b)


mm_tn.defvjp(lambda a, b: (_tn(a, b), (a, b)), lambda r, g: (_nt(r[1], g), _nn(r[0], g)))


def _rms(x):
    return x * lax.rsqrt(jnp.mean(x * x, axis=-1, keepdims=True) + EPS)


def _rms_mod(x, g, sc, sh):
    return (_rms(x) * g) * (1.0 + sc) + sh


def _post_mix(x, mix, gt1, sc2, sh2, g_post_mix, g_pre_mlp):
    x1 = x + gt1 * (_rms(mix) * g_post_mix)
    return x1, _rms_mod(x1, g_pre_mlp, sc2, sh2)


def _head_loss(x1, m, gt2, g_post_mlp, tgt):
    err = x1 + gt2 * (_rms(m) * g_post_mlp) - tgt
    return 0.5 * jnp.sum(jnp.mean(err * err, axis=-1, keepdims=True), axis=0, keepdims=True)


def _ln_gate(o, g, w):
    mu = jnp.mean(o, axis=-1, keepdims=True)
    var = jnp.mean(jnp.square(o - mu), axis=-1, keepdims=True)
    y = (o - mu) * lax.rsqrt(var + EPS)
    return (y * w) * (g * jax.nn.sigmoid(g))


def _pair_order(x):
    lane = lax.broadcasted_iota(jnp.int32, x.shape, 1)
    return jnp.where((lane >= 32) & (lane < 64), pltpu.roll(x, 96, 1),
                     jnp.where((lane >= 64) & (lane < 96), pltpu.roll(x, 32, 1), x))


def _rope(x, cos, sin):
    return x * cos + pltpu.roll(x, 64, 1) * sin


def _rope_t(g, cos, sin):
    return g * cos + pltpu.roll(g * sin, 64, 1)


def _rope_tables():
    tok = lax.broadcasted_iota(jnp.int32, (SEQ, RD), 0)
    lane = lax.broadcasted_iota(jnp.int32, (SEQ, RD), 1)
    pos = jnp.where((lane & 32) == 0, tok >> 6, tok & (GW - 1)).astype(F32)
    ang = pos * jnp.exp((lane & 31).astype(F32) * (-math.log(10000.0) / 32))
    return jnp.cos(ang), jnp.where(lane < 64, -jnp.sin(ang), jnp.sin(ang))


def _chunk_loop(n, body, init, k=4):
    def several(t, carry):
        for i in range(k):
            carry = body(k * t + i, carry)
        return carry

    return lax.fori_loop(0, n // k, several, init)


def _fiota(shape, dim):
    return lax.broadcasted_iota(jnp.int32, shape, dim).astype(F32)


def _ret_state(k, v, s, lg, reverse):
    pos = _fiota((CH, 1), 0)
    b_exp = pos if reverse else (CH - 1.0 - pos)
    return jnp.exp(lg * CH) * s + mm_tn(k * jnp.exp(lg * b_exp), v)


class _Decays:
    def __init__(self, lgs):
        i, j, pos = _fiota((CH, CH), 0), _fiota((CH, CH), 1), _fiota((CH, 1), 0)
        diffs = (i - j, j - i)
        keep = (diffs[0] >= 0, diffs[1] > 0)
        mats = [jnp.where(m, jnp.exp(lg * jnp.where(m, d, 0.0)), 0.0) for lg, d, m in zip(lgs, diffs, keep)]
        self.mask = mats[0] + mats[1]
        self.dmask = [mats[0] * diffs[0], mats[1] * diffs[1]]
        a_exp, b_exp = (pos + 1.0, CH - pos), (CH - 1.0 - pos, pos)
        self.a = [jnp.exp(lg * e) for lg, e in zip(lgs, a_exp)]
        self.b = [jnp.exp(lg * e) for lg, e in zip(lgs, b_exp)]
        self.da = [a * e for a, e in zip(self.a, a_exp)]
        self.db = [b * e for b, e in zip(self.b, b_exp)]
        self.g = [jnp.exp(lg * CH) for lg in lgs]


def _both(x, w):
    return jnp.concatenate([x * w[0], x * w[1]], axis=1)


def _total(x):
    return jnp.sum(jnp.sum(x, axis=1, keepdims=True), axis=0, keepdims=True)


def _state_pass(dec, init, k_s, v_of, st_s):
    def step(t, carry):
        out = []
        for d, s in enumerate(carry):
            n = (NCH - 1 - t) if d else t
            sl = pl.ds(pl.multiple_of(n * CH, CH), CH)
            st_s[n, d * RD:(d + 1) * RD, :] = s
            out.append(dec.g[d] * s + _tn(k_s[sl, :] * dec.b[d], v_of(sl)))
        return tuple(out)

    _chunk_loop(NCH, step, tuple(init))


def premix_proj(xin, mod3, g_pre, wperm, is_ctx, name):
    nb, length, _ = xin.shape
    tn = min(2 * TN, length)

    def body(x_ref, mod_ref, g_ref, w_ref, h_ref, pret_ref, pna_ref):
        h = _rms_mod(x_ref[...], g_ref[...], mod_ref[1:2, :], mod_ref[0:1, :])
        hb = h.astype(BF16)
        h_ref[...] = hb
        pret_ref[...] = jnp.dot(hb, w_ref[:, :RET_W], preferred_element_type=F32)
        pna_ref[...] = jnp.dot(hb, w_ref[:, RET_W:], preferred_element_type=F32).astype(BF16)

    return pl.pallas_call(
        body, name=name, grid=(nb, length // tn),
        in_specs=[
            pl.BlockSpec((None, tn, D), lambda b, t: (b, t, 0)),
            pl.BlockSpec((None, 6, D), (lambda b, t: (2, 0, 0)) if is_ctx else (lambda b, t: (b, 0, 0))),
            pl.BlockSpec((1, D), lambda b, t: (0, 0)),
            pl.BlockSpec((D, IN_W), lambda b, t: (0, 0), pipeline_mode=pl.Buffered(1)),
        ],
        out_specs=[
            pl.BlockSpec((None, tn, D), lambda b, t: (b, t, 0)),
            pl.BlockSpec((None, tn, RET_W), lambda b, t: (b, t, 0)),
            pl.BlockSpec((None, tn, IN_W - RET_W), lambda b, t: (b, t, 0)),
        ],
        out_shape=[
            jax.ShapeDtypeStruct((nb, length, D), BF16),
            jax.ShapeDtypeStruct((nb, length, RET_W), F32),
            jax.ShapeDtypeStruct((nb, length, IN_W - RET_W), BF16),
        ],
        compiler_params=_cp(("arbitrary", "arbitrary")),
    )(xin, mod3, g_pre, wperm)


def premix_bwd(xin, mod3, g_pre, wperm, dproj, dx_tail, hosted, name):
    nb, length, _ = xin.shape
    tn = min(TN, length)
    is_ctx = dx_tail is None

    def body(*refs):
        own_in, h_in, own_out, h_out, _, h_sems = hosted.split(refs, 5 if is_ctx else 6, 2 if is_ctx else 3)
        if is_ctx:
            (x_ref, mod_ref, g_ref, w_ref, dp_ref), (dmod_ref, dg_ref) = own_in, own_out
        else:
            (x_ref, mod_ref, g_ref, w_ref, dp_ref, dxt_ref), (dx_ref, dmod_ref, dg_ref) = own_in, own_out
        b, t = pl.program_id(0), pl.program_id(1)
        grid_step = b * (length // tn) + t

        @pl.when(grid_step == 0)
        def _():
            hosted.start(h_in, h_out, h_sems)

        @pl.when(grid_step == nb * (length // tn) - 1)
        def _():
            hosted.finish(h_in, h_out, h_sems)

        dh = lax.dot_general(dp_ref[...], w_ref[...], (((1,), (1,)), ((), ())), preferred_element_type=F32)
        _, vjp = jax.vjp(_rms_mod, x_ref[...], g_ref[...], mod_ref[1:2, :], mod_ref[0:1, :])
        dx, dg, dsc, dsh = vjp(dh)
        if not is_ctx:
            dx_ref[...] = dx + dxt_ref[...]

        @pl.when((t == 0) & ((b == 0) if is_ctx else True))
        def _():
            dmod_ref[...] = jnp.zeros_like(dmod_ref)

        @pl.when((t == 0) & (b == 0))
        def _():
            dg_ref[...] = jnp.zeros_like(dg_ref)

        dmod_ref[0:1, :] += dsh
        dmod_ref[1:2, :] += dsc
        dg_ref[0:1, :] += dg

    tok = lambda b, t: (b, t, 0)
    in_specs = [
        pl.BlockSpec((None, tn, D), tok),
        pl.BlockSpec((None, 6, D), (lambda b, t: (2, 0, 0)) if is_ctx else (lambda b, t: (b, 0, 0))),
        pl.BlockSpec((1, D), lambda b, t: (0, 0)),
        pl.BlockSpec((D, IN_W), lambda b, t: (0, 0), pipeline_mode=pl.Buffered(1)),
        pl.BlockSpec((None, tn, IN_W), tok),
    ]
    args = [xin, mod3, g_pre, wperm, dproj]
    out_specs = [
        pl.BlockSpec((None, 6, D), (lambda b, t: (0, 0, 0)) if is_ctx else (lambda b, t: (b, 0, 0))),
        pl.BlockSpec((8, D), lambda b, t: (0, 0)),
    ]
    out_shape = [jax.ShapeDtypeStruct((1 if is_ctx else nb, 6, D), F32), jax.ShapeDtypeStruct((8, D), F32)]
    if not is_ctx:
        in_specs.append(pl.BlockSpec((None, tn, D), tok))
        args.append(dx_tail)
        out_specs.insert(0, pl.BlockSpec((None, tn, D), tok))
        out_shape.insert(0, jax.ShapeDtypeStruct((nb, length, D), F32))
    h_in_specs, h_out_specs = hosted.specs()
    return pl.pallas_call(
        body, name=name, grid=(nb, length // tn), in_specs=in_specs + h_in_specs, out_specs=out_specs + h_out_specs,
        out_shape=out_shape + hosted.out_shape, scratch_shapes=hosted.scratch,
        compiler_params=_cp(("arbitrary", "arbitrary")),
    )(*args, *hosted.args)


def _ret_specs(order):
    def im(f):
        return lambda *g: f(*order(*g))
    return dict(
        pret=pl.BlockSpec((None, SEQ, 512), im(lambda b, h: (b, 0, h))),
        pretc=pl.BlockSpec((None, LC, 512), im(lambda b, h: (b, 0, h))),
        rd=pl.BlockSpec((None, 2, 1), im(lambda b, h: (h, 0, 0))),
        gn=pl.BlockSpec((None, 1, RD), im(lambda b, h: (h, 0, 0))),
        tab=pl.BlockSpec((SEQ, RD), im(lambda b, h: (0, 0))),
        head=pl.BlockSpec((None, SEQ, RD), im(lambda b, h: (b, 0, h))),
    )


def retention_fwd(pret, pretc, rd, gn, cos, sin, hosted):
    nb = pret.shape[0]
    sp = _ret_specs(lambda b, h: (b, h))

    def body(*refs):
        own_in, h_in, own_out, h_out, own_scr, h_sems = hosted.split(refs, 6, 2)
        p_ref, pc_ref, rd_ref, gn_ref, cos_ref, sin_ref = own_in
        (o_ref, mix_ref), (q_s, k_s, o_s, st_s) = own_out, own_scr
        grid_step = pl.program_id(0) * RH + pl.program_id(1)

        @pl.when(grid_step == 0)
        def _():
            hosted.start(h_in, h_out, h_sems)

        cos_v, sin_v = cos_ref[...], sin_ref[...]
        q_s[...] = _rope(p_ref[:, 0:128], cos_v, sin_v) * (RD ** -0.5)
        k_s[...] = _rope(p_ref[:, 128:256], cos_v, sin_v)
        lgs, init = [], []
        for rev in (False, True):
            lg = jax.nn.log_sigmoid(rd_ref[int(rev):int(rev) + 1, :])
            s = jnp.zeros((RD, RD), F32)
            for n in ((1, 0) if rev else (0, 1)):
                s = _ret_state(pc_ref[n * CH:(n + 1) * CH, 128:256], pc_ref[n * CH:(n + 1) * CH, 256:384], s, lg, rev)
            lgs.append(lg)
            init.append(s)

        dec = _Decays(lgs)
        _state_pass(dec, init, k_s, lambda sl: p_ref[sl, 256:384], st_s)

        def chunk(n, carry):
            sl = pl.ds(pl.multiple_of(n * CH, CH), CH)
            q = q_s[sl, :]
            o_s[sl, :] = (_nn(_nt(q, k_s[sl, :]) * dec.mask, p_ref[sl, 256:384]) + _nn(_both(q, dec.a), st_s[n]))
            return carry

        _chunk_loop(NCH, chunk, 0)
        o = o_s[...]
        o_ref[...] = o
        mix_ref[...] = _ln_gate(o, p_ref[:, 384:512], gn_ref[...]).astype(BF16)

        @pl.when(grid_step == nb * RH - 1)
        def _():
            hosted.finish(h_in, h_out, h_sems)

    h_in_specs, h_out_specs = hosted.specs()
    return pl.pallas_call(
        body, name="retention_fwd", grid=(nb, RH),
        in_specs=[sp["pret"], sp["pretc"], sp["rd"], sp["gn"], sp["tab"], sp["tab"]] + h_in_specs,
        out_specs=[sp["head"], sp["head"]] + h_out_specs,
        out_shape=[jax.ShapeDtypeStruct((nb, SEQ, RH * RD), F32), jax.ShapeDtypeStruct((nb, SEQ, D), BF16)]
        + hosted.out_shape,
        scratch_shapes=[pltpu.VMEM((SEQ, RD), F32)] * 3 + [pltpu.VMEM((NCH, 2 * RD, RD), F32)] + hosted.scratch,
        compiler_params=_cp(("arbitrary", "arbitrary")),
    )(pret, pretc, rd, gn, cos, sin, *hosted.args)


def retention_bwd(pret, pretc, o_all, dmixin, rd, gn, cos, sin, hosted):
    nb = pret.shape[0]
    sp = _ret_specs(lambda h, b: (b, h))

    def body(*refs):
        own_in, h_in, own_out, h_out, own_scr, h_sems = hosted.split(refs, 8, 4)
        p_ref, pc_ref, o_ref, dmix_ref, rd_ref, gn_ref, cos_ref, sin_ref = own_in
        dp_ref, dpc_ref, drd_ref, dgn_ref = own_out
        q_s, k_s, do_s, dq_s, dk_s, dv_s, st_s, gst_s = own_scr
        b = pl.program_id(1)
        grid_step = pl.program_id(0) * nb + b

        @pl.when(grid_step == 0)
        def _():
            hosted.start(h_in, h_out, h_sems)

        cos_v, sin_v = cos_ref[...], sin_ref[...]
        q_s[...] = _rope(p_ref[:, 0:128], cos_v, sin_v) * (RD ** -0.5)
        k_s[...] = _rope(p_ref[:, 128:256], cos_v, sin_v)
        _, gate_vjp = jax.vjp(_ln_gate, o_ref[...], p_ref[:, 384:512], gn_ref[...])
        do, dg, dgn = gate_vjp(dmix_ref[...].astype(F32))
        do_s[...] = do
        dp_ref[:, 384:512] = dg.astype(BF16)

        @pl.when(b == 0)
        def _():
            drd_ref[...] = jnp.zeros_like(drd_ref)
            dgn_ref[...] = jnp.zeros_like(dgn_ref)

        dgn_ref[...] += dgn
        kcs = [pc_ref[n * CH:(n + 1) * CH, 128:256] for n in (0, 1)]
        vcs = [pc_ref[n * CH:(n + 1) * CH, 256:384] for n in (0, 1)]
        dirs = []
        init = []
        for rev in (False, True):
            rdv = rd_ref[int(rev):int(rev) + 1, :]
            lg = jax.nn.log_sigmoid(rdv)
            order_c = (1, 0) if rev else (0, 1)
            s = jnp.zeros((RD, RD), F32)
            ctx_states = []
            for n in order_c:
                ctx_states.append(s)
                s = _ret_state(kcs[n], vcs[n], s, lg, rev)
            dirs.append((rev, order_c, lg, rdv, ctx_states))
            init.append(s)
        dec = _Decays([lg for _, _, lg, _, _ in dirs])

        def v_of(sl):
            return p_ref[sl, 256:384]

        _state_pass(dec, init, k_s, v_of, st_s)
        zeros = jnp.zeros((CH, RD), F32)

        def scores_back(n, carry):
            dmask_sum, da_f, da_b = carry
            sl = pl.ds(pl.multiple_of(n * CH, CH), CH)
            q, k, v, do = q_s[sl, :], k_s[sl, :], v_of(sl), do_s[sl, :]
            scores = _nt(q, k)
            d_att = _nt(do, v)
            d_scores = d_att * dec.mask
            d_qa = _nt(do, st_s[n])
            d_qf, d_qb = d_qa[:, 0:RD], d_qa[:, RD:2 * RD]
            dq_s[sl, :] = _nn(d_scores, k) + d_qf * dec.a[0] + d_qb * dec.a[1]
            dk_s[sl, :] = _tn(d_scores, q)
            dv_s[sl, :] = _tn(scores * dec.mask, do)
            gst_s[n] = _tn(_both(q, dec.a), do)
            return dmask_sum + d_att * scores, da_f + d_qf * q, da_b + d_qb * q

        dmask_sum, da_f, da_b = _chunk_loop(NCH, scores_back, (zeros, zeros, zeros))

        def state_back(t, carry):
            out = []
            for d, r in enumerate(carry):
                n = t if d else (NCH - 1 - t)
                rows = slice(d * RD, (d + 1) * RD)
                own = gst_s[n, rows, :]
                gst_s[n, rows, :] = r
                out.append(own + dec.g[d] * r)
            return tuple(out)

        d_states = _chunk_loop(NCH, state_back, (zeros, zeros))

        def updates_back(n, carry):
            db_f, db_b, dg_f, dg_b = carry
            sl = pl.ds(pl.multiple_of(n * CH, CH), CH)
            k, r, s = k_s[sl, :], gst_s[n], st_s[n]
            d_kw = _nt(v_of(sl), r)
            d_kf, d_kb = d_kw[:, 0:RD], d_kw[:, RD:2 * RD]
            dk_s[sl, :] += d_kf * dec.b[0] + d_kb * dec.b[1]
            dv_s[sl, :] += _nn(_both(k, dec.b), r)
            return (db_f + d_kf * k, db_b + d_kb * k, dg_f + r[0:RD, :] * s[0:RD, :],
                    dg_b + r[RD:2 * RD, :] * s[RD:2 * RD, :])

        db_dg = _chunk_loop(NCH, updates_back, (zeros, zeros, zeros, zeros))
        dkc = [None, None]
        dvc = [None, None]
        for d, ((rev, order_c, lg, rdv, ctx_states), ds) in enumerate(zip(dirs, d_states)):
            dlg = (_total(dmask_sum * dec.dmask[d]) + _total((da_f, da_b)[d] * dec.da[d])
                   + _total(db_dg[d] * dec.db[d]) + CH * dec.g[d] * _total(db_dg[2 + d]))
            for idx in (1, 0):
                n = order_c[idx]
                _, vjp = jax.vjp(functools.partial(_ret_state, reverse=rev), kcs[n], vcs[n], ctx_states[idx], lg)
                dk_c, dv_c, ds, dl = vjp(ds)
                dlg = dlg + dl
                dkc[n] = dk_c if dkc[n] is None else dkc[n] + dk_c
                dvc[n] = dv_c if dvc[n] is None else dvc[n] + dv_c
            drd_ref[int(rev):int(rev) + 1, :] += dlg * jax.nn.sigmoid(-rdv)
        dp_ref[:, 0:128] = _rope_t(dq_s[...] * (RD ** -0.5), cos_v, sin_v).astype(BF16)
        dp_ref[:, 128:256] = _rope_t(dk_s[...], cos_v, sin_v).astype(BF16)
        dp_ref[:, 256:384] = dv_s[...].astype(BF16)
        zero = jnp.zeros((CH, RD), BF16)
        for n in (0, 1):
            rows = slice(n * CH, (n + 1) * CH)
            dpc_ref[rows, 0:128] = zero
            dpc_ref[rows, 128:256] = dkc[n].astype(BF16)
            dpc_ref[rows, 256:384] = dvc[n].astype(BF16)
            dpc_ref[rows, 384:512] = zero

        @pl.when(grid_step == RH * nb - 1)
        def _():
            hosted.finish(h_in, h_out, h_sems)

    h_in_specs, h_out_specs = hosted.specs()
    return pl.pallas_call(
        body, name="retention_bwd", grid=(RH, nb),
        in_specs=[sp["pret"], sp["pretc"], sp["head"], sp["head"], sp["rd"], sp["gn"], sp["tab"], sp["tab"]]
        + h_in_specs,
        out_specs=[
            pl.BlockSpec((None, SEQ, 512), lambda h, b: (b, 0, h)),
            pl.BlockSpec((None, LC, 512), lambda h, b: (b, 0, h)),
            pl.BlockSpec((None, 2, 1), lambda h, b: (h, 0, 0)),
            pl.BlockSpec((None, 1, RD), lambda h, b: (h, 0, 0)),
        ] + h_out_specs,
        out_shape=[
            jax.ShapeDtypeStruct((nb, SEQ, IN_W), BF16),
            jax.ShapeDtypeStruct((nb, LC, IN_W), BF16),
            jax.ShapeDtypeStruct((RH, 2, 1), F32),
            jax.ShapeDtypeStruct((RH, 1, RD), F32),
        ] + hosted.out_shape,
        scratch_shapes=[pltpu.VMEM((SEQ, RD), F32)] * 6 + [pltpu.VMEM((NCH, 2 * RD, RD), F32)] * 2 + hosted.scratch,
        compiler_params=_cp(("arbitrary", "arbitrary")),
    )(pret, pretc, o_all, dmixin, rd, gn, cos, sin, *hosted.args)


def _rpb_flat(rpb):
    return jnp.pad(rpb, ((0, 0), (0, 1), (0, 33))).reshape(NPAIR, 2, 1, 1024)


def _rpb_flat_t(dflat):
    return dflat.reshape(8, 16, 64)[:, :15, :31]


def _barrel(x, left):
    row = lax.broadcasted_iota(jnp.int32, x.shape, 0)
    n = x.shape[1]
    for bit in range(6):
        s = 1 << bit
        x = jnp.where(((row >> bit) & 1) == 1, pltpu.roll(x, (n - s) if left else s, 1), x)
    return x


NA_TILE_ROWS, NA_BAND_ROWS = 4, 12
NA_Q, NA_K = NA_TILE_ROWS * GW, NA_BAND_ROWS * GW
NA_TILES = SEQ // NA_Q


def _band_start(r0):
    return min(max(r0 - 4, 0), 32 - NA_BAND_ROWS)


def _tile_layout(t):
    rows = range(t * NA_TILE_ROWS, (t + 1) * NA_TILE_ROWS)
    return tuple((r if r < 4 else (r - 24 if r > 28 else 4), min(max(r - 4, 0), 24) - _band_start(rows[0]))
                 for r in rows)


NA_CLASSES = sorted(set(_tile_layout(t) for t in range(NA_TILES)))


def _tile_rows(cls):
    return NA_CLASSES[cls]


def _na_tile(t):
    start = jnp.clip(NA_TILE_ROWS * t - 4, 0, 32 - NA_BAND_ROWS)
    cls = 0
    for tile in range(NA_TILES):
        cls = jnp.where(t == tile, NA_CLASSES.index(_tile_layout(tile)), cls)
    return pl.ds(pl.multiple_of(t * NA_Q, NA_Q), NA_Q), pl.ds(pl.multiple_of(start * GW, NA_Q), NA_K), cls


def _na_probs(qst, kb, kc, bias):
    s_loc = _nt(qst, kb) + bias
    s_ctx = _nt(qst, kc)
    m = jnp.maximum(jnp.max(s_loc, axis=1, keepdims=True), jnp.max(s_ctx, axis=1, keepdims=True))
    e_loc, e_ctx = jnp.exp(s_loc - m), jnp.exp(s_ctx - m)
    den = jnp.sum(e_loc, axis=1, keepdims=True) + jnp.sum(e_ctx, axis=1, keepdims=True)
    return e_loc / den, e_ctx / den


def _stack_heads(t):
    lane = lax.broadcasted_iota(jnp.int32, t.shape, 1)
    zero = jnp.zeros_like(t)
    return jnp.concatenate([jnp.where(lane < 64, t, zero), jnp.where(lane >= 64, t, zero)], axis=0)


def _unstack_heads(t):
    n = t.shape[0] // 2
    lane = lax.broadcasted_iota(jnp.int32, (n, 128), 1)
    return jnp.where(lane < 64, t[:n], t[n:])


NA_BIAS_SHAPE = (len(NA_CLASSES), 2 * NA_Q, NA_K)


def _na_bias_pair(flat_ref, out_ref):
    qc = lax.broadcasted_iota(jnp.int32, (GW, 512), 0)
    kc = lax.broadcasted_iota(jnp.int32, (GW, 512), 1) & 63
    start = jnp.clip(qc - 8, 0, GW - 16)
    window = (kc >= start) & (kc < start + 16)
    fill = jnp.full((GW, NA_K - 512), NEG, F32)
    for hh in (0, 1):
        skew = _barrel(pltpu.roll(jnp.broadcast_to(flat_ref[hh], (GW, 1024)), 1024 - 15, 1), left=False)
        by_class = [jnp.where(window, (skew if rc == 7 else pltpu.roll(skew, (9 + rc) * 64, 1))[:, 0:512], NEG)
                    for rc in range(8)]
        for cls in range(len(NA_CLASSES)):
            for qr, (rc, off) in enumerate(_tile_rows(cls)):
                w = jnp.concatenate([by_class[rc], fill], axis=1)
                rows = slice(hh * NA_Q + qr * GW, hh * NA_Q + (qr + 1) * GW)
                out_ref[cls, rows, :] = pltpu.roll(w, off * GW, 1) if off else w


def na_fwd(pna, pnac, bias, mixin, hosted):
    nb = pna.shape[0]

    def body(*refs):
        (p_ref, pc_ref, bias_ref, _), h_in, (out_ref,), h_out, _, h_sems = hosted.split(refs, 4, 1)
        grid_step = pl.program_id(0) * nb + pl.program_id(1)

        @pl.when(grid_step == 0)
        def _():
            hosted.start(h_in, h_out, h_sems)

        kc, vc = pc_ref[:, 128:256], pc_ref[:, 256:384]

        def tile(t, carry):
            qsl, bsl, cls = _na_tile(t)
            kb, vb = p_ref[bsl, 128:256], p_ref[bsl, 256:384]
            p_loc, p_ctx = _na_probs(_stack_heads(p_ref[qsl, 0:128] * 0.125), kb, kc, bias_ref[cls])
            out_ref[qsl, :] = _unstack_heads(_nn(p_loc, vb) + _nn(p_ctx, vc)).astype(BF16)
            return carry

        lax.fori_loop(0, NA_TILES, tile, 0, unroll=4)

        @pl.when(grid_step == NPAIR * nb - 1)
        def _():
            hosted.finish(h_in, h_out, h_sems)

    h_in_specs, h_out_specs = hosted.specs()
    return pl.pallas_call(
        body, name="na_fwd", grid=(NPAIR, nb),
        in_specs=[
            pl.BlockSpec((None, SEQ, 384), lambda p, b: (b, 0, p)),
            pl.BlockSpec((None, LC, 384), lambda p, b: (b, 0, p)),
            pl.BlockSpec((None, len(NA_CLASSES), 2 * NA_Q, NA_K), lambda p, b: (p, 0, 0, 0)),
            pl.BlockSpec(memory_space=pl.ANY),
        ] + h_in_specs,
        out_specs=[pl.BlockSpec((None, SEQ, 128), lambda p, b: (b, 0, 4 + p))] + h_out_specs,
        out_shape=[jax.ShapeDtypeStruct((nb, SEQ, D), BF16)] + hosted.out_shape,
        input_output_aliases={3: 0},
        scratch_shapes=hosted.scratch,
        compiler_params=_cp(("arbitrary", "arbitrary")),
    )(pna, pnac, bias, mixin, *hosted.args)


def na_bwd(pna, pnac, bias, dmixin, dproj, dprojc, hosted):
    nb = pna.shape[0]

    def body(*refs):
        own_in, h_in, own_out, h_out, own_scr, h_sems = hosted.split(refs, 6, 3)
        p_ref, pc_ref, bias_ref, dmix_ref = own_in[:4]
        dp_ref, dpc_ref, dpat_ref = own_out
        dbias_s, dk_s, dv_s, dkc_s, dvc_s, res_s, resc_s = own_scr
        b, part = pl.program_id(1), pl.program_id(2)
        grid_step = (pl.program_id(0) * nb + b) * 3 + part

        @pl.when(grid_step == 0)
        def _():
            hosted.start(h_in, h_out, h_sems)

        @pl.when(grid_step == NPAIR * nb * 3 - 1)
        def _():
            hosted.finish(h_in, h_out, h_sems)

        @pl.when(part == 0)
        def _():
            @pl.when(b == 0)
            def _():
                dbias_s[...] = jnp.zeros_like(dbias_s)

            dk_s[...] = jnp.zeros_like(dk_s)
            dv_s[...] = jnp.zeros_like(dv_s)
            dkc_s[...] = jnp.zeros_like(dkc_s)
            dvc_s[...] = jnp.zeros_like(dvc_s)
            kc, vc = pc_ref[:, 128:256], pc_ref[:, 256:384]

            def tile(t, carry):
                qsl, bsl, cls = _na_tile(t)
                kb, vb = p_ref[bsl, 128:256], p_ref[bsl, 256:384]
                qst, dost = _stack_heads(p_ref[qsl, 0:128] * 0.125), _stack_heads(dmix_ref[qsl, :])
                p_loc, p_ctx = _na_probs(qst, kb, kc, bias_ref[cls])
                dp_loc, dp_ctx = _nt(dost, vb), _nt(dost, vc)
                delta = (jnp.sum(p_loc * dp_loc, axis=1, keepdims=True)
                         + jnp.sum(p_ctx * dp_ctx, axis=1, keepdims=True))
                ds_loc, ds_ctx = p_loc * (dp_loc - delta), p_ctx * (dp_ctx - delta)
                dbias_s[cls] += ds_loc
                res_s[0, qsl, :] = _unstack_heads((_nn(ds_loc, kb) + _nn(ds_ctx, kc)) * 0.125).astype(BF16)
                dk_s[bsl, :] += _tn(ds_loc, qst)
                dv_s[bsl, :] += _tn(p_loc, dost)
                dkc_s[...] += _tn(ds_ctx, qst)
                dvc_s[...] += _tn(p_ctx, dost)
                return carry

            lax.fori_loop(0, NA_TILES, tile, 0, unroll=2)
            res_s[1] = dk_s[...].astype(BF16)
            res_s[2] = dv_s[...].astype(BF16)
            resc_s[0] = jnp.zeros((LC, 128), BF16)
            resc_s[1] = dkc_s[...].astype(BF16)
            resc_s[2] = dvc_s[...].astype(BF16)

            @pl.when(b == nb - 1)
            def _():
                for hh in (0, 1):
                    by_class = [None] * 8
                    for cls in range(len(NA_CLASSES)):
                        for qr, (rc, off) in enumerate(_tile_rows(cls)):
                            w = dbias_s[cls, hh * NA_Q + qr * GW:hh * NA_Q + (qr + 1) * GW, :]
                            w = (pltpu.roll(w, NA_K - off * GW, 1) if off else w)[:, 0:512]
                            by_class[rc] = w if by_class[rc] is None else by_class[rc] + w
                    skew = jnp.zeros((GW, 1024), F32)
                    for rc in range(8):
                        w = jnp.concatenate([by_class[rc], jnp.zeros((GW, 512), F32)], axis=1)
                        skew = skew + (w if rc == 7 else pltpu.roll(w, (7 - rc) * 64, 1))
                    dpat_ref[hh] = jnp.sum(pltpu.roll(_barrel(skew, left=True), 15, 1), axis=0, keepdims=True)

        dp_ref[...] = res_s[part]
        dpc_ref[...] = resc_s[part]

    h_in_specs, h_out_specs = hosted.specs()
    return pl.pallas_call(
        body, name="na_bwd", grid=(NPAIR, nb, 3),
        in_specs=[
            pl.BlockSpec((None, SEQ, 384), lambda p, b, s: (b, 0, p)),
            pl.BlockSpec((None, LC, 384), lambda p, b, s: (b, 0, p)),
            pl.BlockSpec((None, len(NA_CLASSES), 2 * NA_Q, NA_K), lambda p, b, s: (p, 0, 0, 0)),
            pl.BlockSpec((None, SEQ, 128), lambda p, b, s: (b, 0, 4 + p)),
            pl.BlockSpec(memory_space=pl.ANY),
            pl.BlockSpec(memory_space=pl.ANY),
        ] + h_in_specs,
        out_specs=[
            pl.BlockSpec((None, SEQ, 128), lambda p, b, s: (b, 0, 16 + 3 * p + s)),
            pl.BlockSpec((None, LC, 128), lambda p, b, s: (b, 0, 16 + 3 * p + s)),
            pl.BlockSpec((None, 2, 1, 1024), lambda p, b, s: (p, 0, 0, 0)),
        ] + h_out_specs,
        out_shape=[
            jax.ShapeDtypeStruct((nb, SEQ, IN_W), BF16),
            jax.ShapeDtypeStruct((nb, LC, IN_W), BF16),
            jax.ShapeDtypeStruct((NPAIR, 2, 1, 1024), F32),
        ] + hosted.out_shape,
        input_output_aliases={4: 0, 5: 1},
        scratch_shapes=[
            pltpu.VMEM((len(NA_CLASSES), 2 * NA_Q, NA_K), F32),
            pltpu.VMEM((SEQ, 128), F32), pltpu.VMEM((SEQ, 128), F32),
            pltpu.VMEM((LC, 128), F32), pltpu.VMEM((LC, 128), F32),
            pltpu.VMEM((3, SEQ, 128), BF16), pltpu.VMEM((3, LC, 128), BF16),
        ] + hosted.scratch,
        compiler_params=_cp(("arbitrary", "arbitrary", "arbitrary")),
    )(pna, pnac, bias, dmixin, dproj, dprojc, *hosted.args)


def tail_fwd_bwd(x, mixin, tgt, mod3, g_post_mix, g_pre_mlp, g_post_mlp, wout, w1, w2):
    nb = x.shape[0]

    def body(x_ref, mi_ref, tgt_ref, mod_ref, gpm_ref, gpl_ref, gpo_ref, wo_ref, w1_ref, w2_ref,
             dx_ref, dmix_ref, h2_ref, du_ref, a_ref, dm_ref, dmi_ref, dmod_ref, dg_ref, loss_ref):
        b, t = pl.program_id(0), pl.program_id(1)
        gt1, sh2, sc2, gt2 = mod_ref[2:3, :], mod_ref[3:4, :], mod_ref[4:5, :], mod_ref[5:6, :]
        mix = jnp.dot(mi_ref[...], wo_ref[...], preferred_element_type=F32)
        (x1, h2), vjp_a = jax.vjp(_post_mix, x_ref[...], mix, gt1, sc2, sh2, gpm_ref[...], gpl_ref[...])
        h2b = h2.astype(BF16)
        h2_ref[...] = h2b
        m = jnp.zeros((TN, D), F32)
        relus = []
        for j in range(4):
            cols = slice(j * D, (j + 1) * D)
            r = jnp.maximum(jnp.dot(h2b, w1_ref[j], preferred_element_type=F32), 0.0)
            ab = (r * r).astype(BF16)
            a_ref[:, cols] = ab
            m = m + jnp.dot(ab, w2_ref[cols, :], preferred_element_type=F32)
            relus.append(r)
        loss, vjp_b = jax.vjp(_head_loss, x1, m, gt2, gpo_ref[...], tgt_ref[...])
        dx1, dm, dgt2, dgpo, _ = vjp_b(jnp.ones((1, 1), F32))
        dmb = dm.astype(BF16)
        dm_ref[...] = dmb
        dh2 = jnp.zeros((TN, D), F32)
        for j in range(4):
            cols = slice(j * D, (j + 1) * D)
            da = lax.dot_general(dmb, w2_ref[cols, :], (((1,), (1,)), ((), ())), preferred_element_type=F32)
            dub = (da * (2.0 * relus[j])).astype(BF16)
            du_ref[:, cols] = dub
            dh2 = dh2 + lax.dot_general(dub, w1_ref[j], (((1,), (1,)), ((), ())), preferred_element_type=F32)
        dx, dmix, dgt1, dsc2, dsh2, dgpm, dgpl = vjp_a((dx1, dh2))
        dx_ref[...] = dx
        dmixb = dmix.astype(BF16)
        dmix_ref[...] = dmixb
        dmi_ref[...] = lax.dot_general(dmixb, wo_ref[...], (((1,), (1,)), ((), ())),
                                       preferred_element_type=F32).astype(BF16)

        @pl.when(t == 0)
        def _():
            dmod_ref[...] = jnp.zeros_like(dmod_ref)

        @pl.when((t == 0) & (b == 0))
        def _():
            dg_ref[...] = jnp.zeros_like(dg_ref)
            loss_ref[...] = jnp.zeros_like(loss_ref)

        dmod_ref[2:3, :] += dgt1
        dmod_ref[3:4, :] += dsh2
        dmod_ref[4:5, :] += dsc2
        dmod_ref[5:6, :] += dgt2
        dg_ref[0:1, :] += dgpm
        dg_ref[1:2, :] += dgpl
        dg_ref[2:3, :] += dgpo
        loss_ref[...] += jnp.broadcast_to(loss, loss_ref.shape)

    tok = lambda b, t: (b, t, 0)
    const = lambda b, t: (0, 0)
    vec = pl.BlockSpec((1, D), const)
    return pl.pallas_call(
        body, name="tail_fwd_bwd", grid=(nb, SEQ // TN),
        in_specs=[
            pl.BlockSpec((None, TN, D), tok), pl.BlockSpec((None, TN, D), tok), pl.BlockSpec((None, TN, D), tok),
            pl.BlockSpec((None, 6, D), lambda b, t: (b, 0, 0)), vec, vec, vec,
            pl.BlockSpec((D, D), const, pipeline_mode=pl.Buffered(1)),
            pl.BlockSpec((4, D, D), lambda b, t: (0, 0, 0), pipeline_mode=pl.Buffered(1)),
            pl.BlockSpec((DFF, D), const, pipeline_mode=pl.Buffered(1)),
        ],
        out_specs=[
            pl.BlockSpec((None, TN, D), tok), pl.BlockSpec((None, TN, D), tok), pl.BlockSpec((None, TN, D), tok),
            pl.BlockSpec((None, TN, DFF), tok), pl.BlockSpec((None, TN, DFF), tok), pl.BlockSpec((None, TN, D), tok),
            pl.BlockSpec((None, TN, D), tok),
            pl.BlockSpec((None, 6, D), lambda b, t: (b, 0, 0)),
            pl.BlockSpec((8, D), const), pl.BlockSpec((8, 128), const),
        ],
        out_shape=[
            jax.ShapeDtypeStruct((nb, SEQ, D), F32), jax.ShapeDtypeStruct((nb, SEQ, D), BF16),
            jax.ShapeDtypeStruct((nb, SEQ, D), BF16), jax.ShapeDtypeStruct((nb, SEQ, DFF), BF16),
            jax.ShapeDtypeStruct((nb, SEQ, DFF), BF16), jax.ShapeDtypeStruct((nb, SEQ, D), BF16),
            jax.ShapeDtypeStruct((nb, SEQ, D), BF16),
            jax.ShapeDtypeStruct((nb, 6, D), F32), jax.ShapeDtypeStruct((8, D), F32),
            jax.ShapeDtypeStruct((8, 128), F32),
        ],
        compiler_params=_cp(("arbitrary", "arbitrary")),
    )(x, mixin, tgt, mod3, g_post_mix, g_pre_mlp, g_post_mlp, wout, w1, w2)


def weight_grad(pairs, name, out_dtype=F32, col_blocks=False, tm=1024, tn=1024, tk=2048):
    m, n = pairs[0][0].shape[1], pairs[0][1].shape[1]
    tn = min(tn, n)
    tks = [min(tk, xa.shape[0]) for xa, _ in pairs]
    steps = [xa.shape[0] // t for (xa, _), t in zip(pairs, tks)]
    total = sum(steps)
    offs = [sum(steps[:i]) for i in range(len(pairs))]

    def body(*refs):
        out_ref, acc = refs[2 * len(pairs)], refs[-1]
        k = pl.program_id(2)

        @pl.when(k == 0)
        def _():
            acc[...] = jnp.zeros_like(acc)

        for i in range(len(pairs)):
            @pl.when((k >= offs[i]) & (k < offs[i] + steps[i]))
            def _(i=i):
                acc[...] += lax.dot_general(refs[2 * i][...], refs[2 * i + 1][...], (((0,), (0,)), ((), ())),
                                            preferred_element_type=F32)

        if out_dtype != F32:
            @pl.when(k == total - 1)
            def _():
                out_ref[...] = acc[...].astype(out_dtype)

    in_specs, args = [], []
    for i, (xa, ya) in enumerate(pairs):
        clamp = lambda k, i=i: jnp.clip(k - offs[i], 0, steps[i] - 1)
        in_specs.append(pl.BlockSpec((tks[i], tm), lambda a, c, k, clamp=clamp: (clamp(k), a)))
        in_specs.append(pl.BlockSpec((tks[i], tn), lambda a, c, k, clamp=clamp: (clamp(k), c)))
        args += [xa, ya]
    if col_blocks:
        out_spec = pl.BlockSpec((None, tm, tn), lambda a, c, k: (c, a, 0))
        out_shape = jax.ShapeDtypeStruct((n // tn, m, tn), out_dtype)
    else:
        out_spec = pl.BlockSpec((tm, tn), lambda a, c, k: (a, c))
        out_shape = jax.ShapeDtypeStruct((m, n), out_dtype)
    return pl.pallas_call(
        body, name=name, grid=(m // tm, n // tn, total), in_specs=in_specs, out_specs=out_spec, out_shape=out_shape,
        scratch_shapes=[] if out_dtype == F32 else [pltpu.VMEM((tm, tn), F32)],
        compiler_params=_cp(("arbitrary", "arbitrary", "arbitrary")),
    )(*args)


def _perm_block(t):
    return 4 * (t % 4) + t // 4 if t < 16 else 16 + 3 * ((t - 16) % 4) + (t - 16) // 4


def _is_rope_block(p):
    return p < 16 and p % 4 < 2


def unpack_w_in(blocks):
    def body(i_ref, o_ref):
        for t in range(28):
            p = _perm_block(t)
            blk = i_ref[t // 7, :, (t % 7) * 128:(t % 7 + 1) * 128]
            if _is_rope_block(p):
                blk = _pair_order(blk.astype(F32)).astype(BF16)
            o_ref[:, p * 128:(p + 1) * 128] = blk

    return pl.pallas_call(
        body, name="unpack_w_in", grid=(2,),
        in_specs=[pl.BlockSpec((4, D // 2, 896), lambda i: (0, i, 0))],
        out_specs=pl.BlockSpec((D // 2, IN_W), lambda i: (i, 0)),
        out_shape=jax.ShapeDtypeStruct((D, IN_W), BF16),
    )(blocks)


def pack_w_in(dw):
    def body(i_ref, o_ref):
        for t in range(28):
            p = _perm_block(t)
            blk = i_ref[:, p * 128:(p + 1) * 128]
            if _is_rope_block(p):
                blk = _pair_order(blk)
            o_ref[t // 7, :, (t % 7) * 128:(t % 7 + 1) * 128] = blk.astype(BF16)

    return pl.pallas_call(
        body, name="pack_w_in", grid=(4,),
        in_specs=[pl.BlockSpec((D // 4, IN_W), lambda i: (i, 0))],
        out_specs=pl.BlockSpec((4, D // 4, 896), lambda i: (0, i, 0)),
        out_shape=jax.ShapeDtypeStruct((4, D, 896), BF16),
    )(dw)


def _place():
    return lax.axis_index("x"), lax.axis_index("y"), lax.axis_index("c")


class Hosted:
    def __init__(self, args, out_shape, scratch, start, finish):
        self.args, self.out_shape, self.scratch, self.start, self.finish = args, out_shape, scratch, start, finish

    def specs(self):
        hbm = pl.BlockSpec(memory_space=pl.ANY)
        return [hbm] * len(self.args), [hbm] * len(self.out_shape)

    def split(self, refs, n_in, n_out):
        a, b = len(self.args), len(self.out_shape)
        cuts = [n_in, n_in + a, n_in + a + n_out, n_in + a + n_out + b, len(refs) - len(self.scratch)]
        parts = [refs[i:j] for i, j in zip([0] + cuts, cuts + [len(refs)])]
        return parts[0], parts[1], parts[2], parts[3], parts[4], parts[5]


def no_exchange():
    return Hosted([], [], [], lambda *a: None, lambda *a: None)


def run_hosted(hosted, name):
    def body(*refs):
        _, ins, _, outs, _, sems = hosted.split(refs, 0, 0)
        hosted.start(ins, outs, sems)
        hosted.finish(ins, outs, sems)

    in_specs, out_specs = hosted.specs()
    return pl.pallas_call(body, name=name, in_specs=in_specs, out_specs=out_specs, out_shape=hosted.out_shape,
                          scratch_shapes=hosted.scratch)(*hosted.args)


def gather8(blocks, relay_diagonal=False):
    na = len(blocks)

    def copies(ins, outs, sems):
        send_sems, recv_sems, local_sem = sems
        x, y, c = _place()
        me, sibling = (x, y, c), (x, y, 1 - c)
        chips = [(1 - x, y), (x, 1 - y), (1 - x, 1 - y)]

        def slot(o_ref, px, py, pc, half=None):
            ref = o_ref.at[4 * px + 2 * py + pc]
            if half is None:
                return ref
            rows = ref.shape[0] // 2
            return ref.at[pl.ds(half * rows, rows)]

        def copy(a, k, block, to, src=None, half=None):
            return pltpu.make_async_remote_copy(
                src_ref=slot(outs[a], *block, half) if src is None else src, dst_ref=slot(outs[a], *block, half),
                send_sem=send_sems.at[a, k], recv_sem=recv_sems.at[a, k], device_id=to, device_id_type=MESH)

        mine = [pltpu.make_async_copy(ins[a], slot(outs[a], *me), local_sem.at[a]) for a in range(na)]
        first = []
        for a in range(na):
            first.append(copy(a, 0, me, sibling, src=ins[a]))
            first += [copy(a, 1 + j, me, (*chip, c), src=ins[a])
                      for j, chip in enumerate(chips[:2] if relay_diagonal else chips)]
        return copy, mine, first, me, sibling, chips, c

    def start(ins, outs, sems):
        _, mine, first, *_ = copies(ins, outs, sems)
        for cp in mine + first:
            cp.start()

    def finish(ins, outs, sems):
        copy, mine, first, me, sibling, chips, c = copies(ins, outs, sems)
        passed = []
        for j, chip in enumerate(chips[:2] if relay_diagonal else chips):
            for a in range(na):
                copy(a, 1 + j, (*chip, c), me).wait_recv()
                onward = [copy(a, 4 + j, (*chip, c), sibling)]
                if relay_diagonal:
                    onward.insert(0, copy(a, (3, 7)[j], (*chip, c), (*chips[1 - j], c), half=j))
                for cp in onward:
                    cp.start()
                passed += onward
        if relay_diagonal:
            for a in range(na):
                copy(a, 3, (*chips[2], c), me, half=0).wait_recv()
                copy(a, 7, (*chips[2], c), me, half=1).wait_recv()
                cp = copy(a, 6, (*chips[2], c), sibling)
                cp.start()
                passed.append(cp)
        for a in range(na):
            copy(a, 0, sibling, me).wait_recv()
            for j, chip in enumerate(chips):
                copy(a, 4 + j, (*chip, 1 - c), me).wait_recv()
        for cp in first + passed:
            cp.wait_send()
        for cp in mine:
            cp.wait()

    return Hosted(list(blocks), [jax.ShapeDtypeStruct((8,) + b.shape, b.dtype) for b in blocks],
                  [pltpu.SemaphoreType.DMA((na, 8)), pltpu.SemaphoreType.DMA((na, 8)), pltpu.SemaphoreType.DMA((na,))],
                  start, finish)


def chips3(arrays):
    na = len(arrays)

    def copies(ins, outs, sems):
        send_sems, recv_sems = sems
        x, y, c = _place()
        return [pltpu.make_async_remote_copy(
            src_ref=ins[a].at[2 * px + py], dst_ref=outs[a].at[k], send_sem=send_sems.at[a, k],
            recv_sem=recv_sems.at[a, k], device_id=(px, py, c), device_id_type=MESH)
            for a in range(na) for k, (px, py) in enumerate([(1 - x, y), (x, 1 - y), (1 - x, 1 - y)])]

    def start(ins, outs, sems):
        for cp in copies(ins, outs, sems):
            cp.start()

    def finish(ins, outs, sems):
        for cp in copies(ins, outs, sems):
            cp.wait()

    return Hosted(list(arrays), [jax.ShapeDtypeStruct((3,) + a.shape[1:], a.dtype) for a in arrays],
                  [pltpu.SemaphoreType.DMA((na, 3)), pltpu.SemaphoreType.DMA((na, 3))], start, finish)


def siblings(arrays):
    na = len(arrays)

    def copies(ins, outs, sems):
        send_sems, recv_sems = sems
        x, y, c = _place()
        return [pltpu.make_async_remote_copy(
            src_ref=ins[a], dst_ref=outs[a], send_sem=send_sems.at[a], recv_sem=recv_sems.at[a],
            device_id=(x, y, 1 - c), device_id_type=MESH) for a in range(na)]

    def start(ins, outs, sems):
        for cp in copies(ins, outs, sems):
            cp.start()

    def finish(ins, outs, sems):
        for cp in copies(ins, outs, sems):
            cp.wait()

    return Hosted(list(arrays), [jax.ShapeDtypeStruct(a.shape, a.dtype) for a in arrays],
                  [pltpu.SemaphoreType.DMA((na,)), pltpu.SemaphoreType.DMA((na,))], start, finish)


def both(first, second):
    na, no, ns = len(first.args), len(first.out_shape), len(first.scratch)

    def start(ins, outs, sems):
        first.start(ins[:na], outs[:no], sems[:ns])
        second.start(ins[na:], outs[no:], sems[ns:])

    def finish(ins, outs, sems):
        first.finish(ins[:na], outs[:no], sems[:ns])
        second.finish(ins[na:], outs[no:], sems[ns:])

    return Hosted(first.args + second.args, first.out_shape + second.out_shape, first.scratch + second.scratch,
                  start, finish)


def siblings4(arrays):
    na = len(arrays)

    def copies(ins, outs, sems):
        send_sems, recv_sems = sems
        x, y, c = _place()
        return [pltpu.make_async_remote_copy(
            src_ref=ins[a].at[2 * j + 1 - c], dst_ref=outs[a].at[j],
            send_sem=send_sems.at[a, j], recv_sem=recv_sems.at[a, j],
            device_id=(x, y, 1 - c), device_id_type=MESH) for a in range(na) for j in range(4)]

    def start(ins, outs, sems):
        for cp in copies(ins, outs, sems):
            cp.start()

    def finish(ins, outs, sems):
        for cp in copies(ins, outs, sems):
            cp.wait()

    return Hosted(list(arrays), [jax.ShapeDtypeStruct((4,) + a.shape[1:], a.dtype) for a in arrays],
                  [pltpu.SemaphoreType.DMA((na, 4)), pltpu.SemaphoreType.DMA((na, 4))], start, finish)


def _row_tile(r):
    for cand in (512, 256, 128, 64, 32, 16, 8):
        if r % cand == 0:
            return cand
    return r


def chip_partial(place, g8s, landed4s, name):
    n = len(g8s)

    def body(place_ref, *refs):
        del place_ref
        for g_ref, l_ref, o_ref in zip(refs[:n], refs[n:2 * n], refs[2 * n:]):
            o_ref[...] = (g_ref[...].astype(F32) + l_ref[...].astype(F32)).astype(BF16)

    own = [pl.BlockSpec((None,) + g.shape[1:], lambda j, s: (2 * j + s[0], 0, 0)) for g in g8s]
    plain = [pl.BlockSpec((None,) + g.shape[1:], lambda j, s: (j, 0, 0)) for g in g8s]
    return pl.pallas_call(
        body, name=name,
        grid_spec=pltpu.PrefetchScalarGridSpec(num_scalar_prefetch=1, grid=(4,), in_specs=own + plain, out_specs=plain),
        out_shape=[jax.ShapeDtypeStruct((4,) + g.shape[1:], BF16) for g in g8s],
    )(place, *g8s, *landed4s)


def shard_sum(place, partial4s, landed3s, name):
    n = len(partial4s)

    def body(place_ref, *refs):
        del place_ref
        for p_ref, l_ref, o_ref in zip(refs[:n], refs[n:2 * n], refs[2 * n:]):
            acc = p_ref[...].astype(F32)
            for k in range(3):
                acc = acc + l_ref[k].astype(F32)
            o_ref[...] = acc

    def halves(p, lead):
        r, ccols = p.shape[1:]
        return (lead, r // 2, ccols)

    return pl.pallas_call(
        body, name=name,
        grid_spec=pltpu.PrefetchScalarGridSpec(
            num_scalar_prefetch=1, grid=(2,),
            in_specs=[pl.BlockSpec(halves(p, None), lambda i, s: (s[1], i, 0)) for p in partial4s]
            + [pl.BlockSpec(halves(p, 3), lambda i, s: (0, i, 0)) for p in partial4s],
            out_specs=[pl.BlockSpec(halves(p, None)[1:], lambda i, s: (i, 0)) for p in partial4s]),
        out_shape=[jax.ShapeDtypeStruct(p.shape[1:], F32) for p in partial4s],
    )(place, *partial4s, *landed3s)


def _adamw_math(w, g, m, v):
    m2 = B1 * m + (1.0 - B1) * g
    v2 = B2 * v + (1.0 - B2) * (g * g)
    m_hat = m2 / (1.0 - B1 ** STEP)
    v_hat = v2 / (1.0 - B2 ** STEP)
    return -LR * (m_hat / (jnp.sqrt(v_hat) + AEPS) + WD * w), m2, v2


def adamw_halves(place, w, mine, theirs, m, v, name):
    r, ccols = w.shape
    hr = r // 2
    tr = _row_tile(hr)
    nt = hr // tr

    def body(place_ref, w_ref, a_ref, b_ref, m_ref, v_ref, g_out, d_out, m_out, v_out):
        g = jnp.where(pl.program_id(0) == place_ref[0], a_ref[...], b_ref[...])
        d, m2, v2 = _adamw_math(w_ref[...], g, m_ref[...], v_ref[...])
        g_out[...] = g
        d_out[...] = d
        m_out[...] = m2
        v_out[...] = v2

    full = pl.BlockSpec((tr, ccols), lambda h, i, s: (h * nt + i, 0))
    part = pl.BlockSpec((tr, ccols), lambda h, i, s: (i, 0))
    return pl.pallas_call(
        body, name=name,
        grid_spec=pltpu.PrefetchScalarGridSpec(
            num_scalar_prefetch=1, grid=(2, nt), in_specs=[full, part, part, full, full], out_specs=[full] * 4),
        out_shape=[jax.ShapeDtypeStruct((r, ccols), F32)] * 4,
    )(place, w, mine, theirs, m, v)


def adamw_group(place, halved, plain, hosted, name):
    rows = halved[0][0].shape[0]
    tr = 128
    nt = rows // 2 // tr
    nh, npl = len(halved), len(plain)

    def body(place_ref, *refs):
        own_in, h_in, own_out, h_out, _, h_sems = hosted.split(refs, 5 * nh + 4 * npl, 4 * nh + 3 * npl)
        half = pl.program_id(0)
        grid_step = half * nt + pl.program_id(1)

        @pl.when(grid_step == 0)
        def _():
            hosted.start(h_in, h_out, h_sems)

        for i in range(nh):
            w_ref, a_ref, b_ref, m_ref, v_ref = own_in[5 * i:5 * i + 5]
            g = jnp.where(half == place_ref[0], a_ref[...], b_ref[...])
            res = (g,) + _adamw_math(w_ref[...], g, m_ref[...], v_ref[...])
            for o_ref, r in zip(own_out[4 * i:4 * i + 4], res):
                o_ref[...] = r
        for i in range(npl):
            w_ref, g_ref, m_ref, v_ref = own_in[5 * nh + 4 * i:5 * nh + 4 * i + 4]
            res = _adamw_math(w_ref[...], g_ref[...], m_ref[...], v_ref[...])
            for o_ref, r in zip(own_out[4 * nh + 3 * i:4 * nh + 3 * i + 3], res):
                o_ref[...] = r

        @pl.when(grid_step == 2 * nt - 1)
        def _():
            hosted.finish(h_in, h_out, h_sems)

    def full(cols):
        return pl.BlockSpec((tr, cols), lambda h, i, s: (h * nt + i, 0))

    def part(cols):
        return pl.BlockSpec((tr, cols), lambda h, i, s: (i, 0))

    in_specs, out_specs, out_shape, args = [], [], [], []
    for w, a, b, m, v in halved:
        cols = w.shape[1]
        in_specs += [full(cols), part(cols), part(cols), full(cols), full(cols)]
        out_specs += [full(cols)] * 4
        out_shape += [jax.ShapeDtypeStruct(w.shape, F32)] * 4
        args += [w, a, b, m, v]
    for w, g, m, v in plain:
        cols = w.shape[1]
        in_specs += [full(cols)] * 4
        out_specs += [full(cols)] * 3
        out_shape += [jax.ShapeDtypeStruct(w.shape, F32)] * 3
        args += [w, g, m, v]
    h_in_specs, h_out_specs = hosted.specs()
    return pl.pallas_call(
        body, name=name,
        grid_spec=pltpu.PrefetchScalarGridSpec(
            num_scalar_prefetch=1, grid=(2, nt), in_specs=in_specs + h_in_specs, out_specs=out_specs + h_out_specs,
            scratch_shapes=hosted.scratch),
        out_shape=out_shape + hosted.out_shape,
        compiler_params=_cp(("arbitrary", "arbitrary")),
    )(place, *args, *hosted.args)


def _silu(x):
    return x * jax.nn.sigmoid(x)


def prologue(c_rows, c_ctx_row, w_ada, b_shard, rpb_flat, half_w_in, late_shards):
    shape = jax.ShapeDtypeStruct
    n_late = len(late_shards)
    half_shapes = [(w.shape[0] // 2, w.shape[1]) for w in late_shards]
    g_w = gather8([half_w_in], relay_diagonal=True)
    g_c = gather8([shape((8, D), F32)])
    g_m = chips3([shape((4, 8, 1536), F32)])

    def body(*refs):
        c_ref, cc_ref, w_ref, b_ref, flat_ref, hw_ref = refs[:6]
        late_refs = refs[6:6 + n_late]
        cin_ref, mg_ref, gw_ref, bias_ref, cos_ref, sin_ref = refs[6 + n_late:12 + n_late]
        rest = refs[12 + n_late:]
        half_refs, (cg_s, ms_s, bias_s, need_s, landed_s) = rest[:n_late], rest[n_late:n_late + 5]
        stage, (load_sem, bias_sem), sems = rest[n_late + 5:2 * n_late + 5], rest[2 * n_late + 5:2 * n_late + 7], \
            rest[2 * n_late + 7:]
        sw, sc, sm = sems[0:3], sems[3:6], sems[6:8]
        px, py, core = _place()
        g_c.start([c_ref], [cg_s], sc)
        g_w.start([hw_ref], [gw_ref], sw)
        loads = [pltpu.make_async_copy(late_refs[a].at[pl.ds(core * half_shapes[a][0], half_shapes[a][0]), :],
                                       stage[a], load_sem.at[a]) for a in range(n_late)]
        for cp in loads:
            cp.start()
        g_c.finish([c_ref], [cg_s], sc)
        cin_ref[...] = jnp.zeros_like(cin_ref)
        for dev in range(8):
            cin_ref[2 * dev:2 * dev + 2, :] = cg_s[dev, 0:2, :]
        cin_ref[16:17, :] = cc_ref[...]
        need_s[...] = jnp.zeros_like(need_s)
        for j in range(4):
            need_s[8 * j:8 * j + 2, :] = cg_s[2 * j + core, 0:2, :]
            need_s[8 * j + 2:8 * j + 3, :] = cc_ref[...]
        ms_s[...] = (_nn(_silu(need_s[...]), w_ref[...]) + b_ref[...]).reshape(4, 8, 1536)
        g_m.start([ms_s], [landed_s], sm)
        for a, cp in enumerate(loads):
            cp.wait()
            half_refs[a][...] = stage[a][...].astype(BF16)
        cos_ref[...], sin_ref[...] = _rope_tables()
        stores = []
        for pair in range(NPAIR):
            if pair >= 2:
                stores[pair - 2].wait()
            _na_bias_pair(flat_ref.at[pair], bias_s.at[pair % 2])
            stores.append(pltpu.make_async_copy(bias_s.at[pair % 2], bias_ref.at[pair], bias_sem.at[pair % 2]))
            stores[pair].start()
        for cp in stores[-2:]:
            cp.wait()
        g_w.finish([hw_ref], [gw_ref], sw)
        g_m.finish([ms_s], [landed_s], sm)
        mg_ref[2 * px + py] = ms_s[2 * px + py]
        for k, (qx, qy) in enumerate([(1 - px, py), (px, 1 - py), (1 - px, 1 - py)]):
            mg_ref[2 * qx + qy] = landed_s[k]

    vmem = pl.BlockSpec(memory_space=pltpu.VMEM)
    hbm = pl.BlockSpec(memory_space=pl.ANY)
    return pl.pallas_call(
        body, name="prologue", in_specs=[vmem, vmem, vmem, vmem, vmem, hbm] + [hbm] * n_late,
        out_specs=[vmem, vmem, hbm, hbm, vmem, vmem] + [vmem] * n_late,
        out_shape=[shape((32, D), F32), shape((4, 8, 1536), F32)] + g_w.out_shape
        + [shape((NPAIR,) + NA_BIAS_SHAPE, F32)] + [shape((SEQ, RD), F32)] * 2 + [shape(s, BF16) for s in half_shapes],
        scratch_shapes=[pltpu.VMEM((8, 8, D), F32), pltpu.VMEM((4, 8, 1536), F32), pltpu.VMEM((2,) + NA_BIAS_SHAPE, F32),
                        pltpu.VMEM((32, D), F32), pltpu.VMEM((3, 8, 1536), F32)]
        + [pltpu.VMEM(s, F32) for s in half_shapes]
        + [pltpu.SemaphoreType.DMA((n_late,)), pltpu.SemaphoreType.DMA((2,))]
        + g_w.scratch + g_c.scratch + g_m.scratch,
        compiler_params=_cp(),
    )(c_rows, c_ctx_row, w_ada, b_shard, rpb_flat, half_w_in, *late_shards)


def ada_grads(cin, gb, gc, w_ada):
    def body(c_ref, gb_ref, gc_ref, w_ref, gw_ref, pc_ref):
        ctx_tot = jnp.sum(gc_ref[...], axis=0, keepdims=True)
        rows = lax.broadcasted_iota(jnp.int32, (16, 512), 0)
        dm = jnp.concatenate([gb_ref[...], jnp.where(rows == 0, ctx_tot, 0.0)], axis=0)
        gw_ref[...] = _tn(_silu(c_ref[...]), dm)
        rows8 = lax.broadcasted_iota(jnp.int32, (8, 512), 0)
        part = _nt(jnp.where(rows8 == 0, ctx_tot, 0.0), w_ref[...])

        @pl.when(pl.program_id(0) == 0)
        def _():
            pc_ref[...] = jnp.zeros_like(pc_ref)

        pc_ref[...] += part

    return pl.pallas_call(
        body, name="ada_grads", grid=(3,),
        in_specs=[pl.BlockSpec((32, D), lambda j: (0, 0)), pl.BlockSpec((16, 512), lambda j: (0, j)),
                  pl.BlockSpec((8, 512), lambda j: (0, j)), pl.BlockSpec((D, 512), lambda j: (0, j))],
        out_specs=[pl.BlockSpec((D, 512), lambda j: (0, j)), pl.BlockSpec((8, D), lambda j: (0, 0))],
        out_shape=[jax.ShapeDtypeStruct((D, 1536), F32), jax.ShapeDtypeStruct((8, D), F32)],
    )(cin, gb, gc, w_ada)


SMALL_SUM_ROWS = 15


def small_update(gsm, gbf, gcf, pcg, params):
    n = len(params)

    def body(*refs):
        gsm_ref, gbf_ref, gcf_ref, pcg_ref = refs[:4]
        wmv, outs, loss_out = refs[4:4 + 3 * n], refs[4 + 3 * n:4 + 7 * n], refs[-1]
        acc = gsm_ref[0]
        for dev in range(1, 8):
            acc = acc + gsm_ref[dev]
        c_ctx = wmv[0][...]
        sg = jax.nn.sigmoid(c_ctx)
        dsilu = pcg_ref[0:1, :] + pcg_ref[2:3, :] + pcg_ref[4:5, :] + pcg_ref[6:7, :]
        lane = lax.broadcasted_iota(jnp.int32, (1, D), 1)
        last = acc[14:15, :]
        grads = [
            dsilu * (sg * (1.0 + c_ctx * (1.0 - sg))),
            jnp.sum(gbf_ref[...], axis=0, keepdims=True) + jnp.sum(gcf_ref[...], axis=0, keepdims=True),
            acc[0:1, :] + acc[1:2, :], acc[2:3, :], acc[3:4, :], acc[4:5, :],
            acc[5:6, 0:512], acc[6:14, :], jnp.where(lane < 8, last, 0.0),
        ]
        loss_out[...] = jnp.broadcast_to(jnp.sum(jnp.where(lane == 8, last, 0.0), axis=1, keepdims=True), (8, 128))
        for i, g in enumerate(grads):
            d, m2, v2 = _adamw_math(wmv[3 * i][...], g, wmv[3 * i + 1][...], wmv[3 * i + 2][...])
            outs[4 * i][...] = g
            outs[4 * i + 1][...] = d
            outs[4 * i + 2][...] = m2
            outs[4 * i + 3][...] = v2

    flat = [a for wmv in params for a in wmv]
    out_shape = [jax.ShapeDtypeStruct(w.shape, F32) for w, _, _ in params for _ in range(4)]
    return pl.pallas_call(
        body, name="small_update", out_shape=out_shape + [jax.ShapeDtypeStruct((8, 128), F32)],
    )(gsm, gbf, gcf, pcg, *flat)


def _pad_row(v, rows):
    flat = v.reshape(-1)
    return jnp.pad(flat, (0, rows * D - flat.shape[0])).reshape(rows, D)


def local_step(x, ctx, tgt, mod3, rope, bias, g_pre_mix, g_post_mix, g_pre_mlp, g_post_mlp, ret_decay, ret_gn,
               wperm, late_weights, early_grads):
    nb = x.shape[0]
    tokens = nb * SEQ
    cos, sin = rope
    rd = ret_decay.T.reshape(RH, 2, 1)
    gn = ret_gn.reshape(RH, 1, RD)
    h, pret, pna = premix_proj(x, mod3, g_pre_mix, wperm, False, "premix_proj")
    hc, pretc, pnac = premix_proj(ctx, mod3, g_pre_mix, wperm, True, "premix_proj_ctx")
    o_all, mixin, gw_out = retention_fwd(pret, pretc, rd, gn, cos, sin, late_weights(0))
    mixin, gw1, gw2 = na_fwd(pna, pnac, bias, mixin, late_weights(1))
    dx_tail, dmix, h2, du, act, dm, dmixin, dmod_t, dg_t, loss_t = tail_fwd_bwd(
        x, mixin, tgt, mod3, g_post_mix, g_pre_mlp, g_post_mlp, gw_out.reshape(D, D), gw1.reshape(4, D, D),
        gw2.reshape(DFF, D))
    dw_out = weight_grad([(mixin.reshape(tokens, D), dmix.reshape(tokens, D))], "grad_w_out", BF16)
    dw1 = weight_grad([(h2.reshape(tokens, D), du.reshape(tokens, DFF))], "grad_w_mlp1", BF16, col_blocks=True)
    dw2 = weight_grad([(act.reshape(tokens, DFF), dm.reshape(tokens, D))], "grad_w_mlp2", BF16)
    dproj, dprojc, drd, dgn, *landed = retention_bwd(pret, pretc, o_all, dmixin, rd, gn, cos, sin,
                                                     early_grads[0](dw_out, dw1, dw2))
    dproj, dprojc, dpat, *early = na_bwd(pna, pnac, bias, dmixin, dproj, dprojc, early_grads[1](landed))
    dw_in = weight_grad([(h.reshape(tokens, D), dproj.reshape(tokens, IN_W)),
                         (hc.reshape(nb * LC, D), dprojc.reshape(nb * LC, IN_W))], "grad_w_in", tn=IN_W // 2, tk=1024)
    dmod_c, dg_c, *late = premix_bwd(ctx, mod3, g_pre_mix, wperm, dprojc, None, early_grads[2](dw_in), "premix_bwd_ctx")
    grad_x, dmod_a, dg_a, *late = premix_bwd(x, mod3, g_pre_mix, wperm, dproj, dx_tail, early_grads[3](late),
                                             "premix_bwd")
    dmod = jnp.concatenate([jnp.concatenate([dmod_a[:, 0:2], dmod_t[:, 2:6]], axis=1), dmod_c], axis=0)
    last = jnp.pad(jnp.concatenate([drd[:, :, 0].T.reshape(8), loss_t[0, 0:1]]), (0, D - 9)).reshape(1, D)
    small = jnp.concatenate([dg_a[0:1], dg_c[0:1], dg_t[0:3], _pad_row(dgn, 1), dpat.reshape(8, D), last], axis=0)
    return grad_x, late, early, dmod, small


def kernel(x, c, ctx, c_ctx, w_ada, b_ada, g_pre_mix, g_post_mix, g_pre_mlp, g_post_mlp, w_in, ret_decay, ret_gn, na_rpb, w_out, w_mlp1, w_mlp2, loss_target, m_c_ctx, m_w_ada, m_b_ada, m_g_pre_mix, m_g_post_mix, m_g_pre_mlp, m_g_post_mlp, m_w_in, m_ret_decay, m_ret_gn, m_na_rpb, m_w_out, m_w_mlp1, m_w_mlp2, v_c_ctx, v_w_ada, v_b_ada, v_g_pre_mix, v_g_post_mix, v_g_pre_mlp, v_g_post_mlp, v_w_in, v_ret_decay, v_ret_gn, v_na_rpb, v_w_out, v_w_mlp1, v_w_mlp2):
    px, py, pc = _place()
    chip = 2 * px + py

    half_w_in = lax.dynamic_slice_in_dim(w_in[0], pc * (D // 2), D // 2, 0).astype(BF16)
    cin, mg, gw_in, bias, cos, sin, *late_halves = prologue(
        jnp.pad(c, ((0, 6), (0, 0))), c_ctx[None], w_ada[0], lax.dynamic_slice_in_dim(b_ada, chip * 1536, 1536, 1),
        _rpb_flat(na_rpb[0]), half_w_in, [w_out[0], w_mlp1[0], w_mlp2[0]])
    halves = [half_w_in] + late_halves
    wperm = unpack_w_in(gw_in.reshape(4, D, 896))
    mod3 = mg[:, 0:3].transpose(1, 0, 2).reshape(3, 6, D)

    place = jnp.stack([pc, chip]).astype(jnp.int32)

    early_names = ["w_out", "w_mlp1", "w_mlp2"]
    early_g8, early_partial = [], []

    def early_a(dw_out, dw1, dw2):
        early_g8[:] = [dw_out.reshape(8, 128, D), dw1.reshape(8, 512, D), dw2.reshape(8, 512, D)]
        return siblings4(early_g8)

    def early_b(landed):
        early_partial[:] = chip_partial(place, early_g8, landed, "rs_chip_sum_early")
        return chips3(early_partial)

    late_partial = []

    late_g8 = []

    def late_c(dw_in):
        late_g8[:] = [pack_w_in(dw_in).reshape(8, 512, 896)]
        return siblings4(late_g8)

    def late_d(landed):
        late_partial[:] = chip_partial(place, late_g8, landed, "rs_chip_sum_w_in")
        return chips3(late_partial)

    grad_x, (landed3_in,), early_landed, dmod, small = local_step(
        x, ctx, loss_target, mod3, (cos, sin), bias, g_pre_mix, g_post_mix, g_pre_mlp, g_post_mlp, ret_decay[0], ret_gn,
        wperm, lambda k: gather8(halves[1:2] if k == 0 else halves[2:4]), (early_a, early_b, late_c, late_d))
    early_mine = shard_sum(place, early_partial, early_landed, "rs_shard_sum_early")

    pay = jnp.concatenate([dmod.reshape(18, D), small, jnp.zeros((40 - 18 - SMALL_SUM_ROWS, D), F32)], axis=0)
    *early_theirs, gs = run_hosted(both(siblings(early_mine), gather8([pay])), "rs_halves_early_gather_small")
    gbf = gs[:, 0:12].reshape(16, 6 * D)
    gcf = gs[:, 12:18].reshape(8, 6 * D)
    gw_ada, pc_part = ada_grads(cin, lax.dynamic_slice_in_dim(gbf, chip * 1536, 1536, 1),
                                lax.dynamic_slice_in_dim(gcf, chip * 1536, 1536, 1), w_ada[0])
    (mine_in,) = shard_sum(place, late_partial, [landed3_in], "rs_shard_sum_w_in")
    theirs_in, pcg = run_hosted(both(siblings([mine_in]), gather8([pc_part])), "rs_halves_w_in_gather_c_ctx")

    grouped = adamw_group(
        place,
        [(w_mlp1[0], early_mine[1], early_theirs[1], m_w_mlp1[0], v_w_mlp1[0]),
         (w_mlp2[0], early_mine[2], early_theirs[2], m_w_mlp2[0], v_w_mlp2[0])],
        [(w_ada[0], gw_ada, m_w_ada[0], v_w_ada[0])], no_exchange(), "adamw_group")
    d_ada, m_ada, v_ada = grouped[8:11]
    big = [
        [r[None] for r in adamw_halves(place, w_in[0], mine_in, theirs_in, m_w_in[0], v_w_in[0], "adamw_w_in")],
        [r[None] for r in adamw_halves(place, w_out[0], early_mine[0], early_theirs[0], m_w_out[0], v_w_out[0],
                                       "adamw_w_out")],
        [r[None] for r in grouped[0:4]], [r[None] for r in grouped[4:8]],
    ]

    def rpb_rows(t):
        return _rpb_flat(t[0]).reshape(8, D)

    def decay_row(t):
        return jnp.pad(t.reshape(1, 8), ((0, 0), (0, D - 8)))

    views = [lambda t: t.reshape(1, D), lambda t: t, lambda t: t, lambda t: t, lambda t: t, lambda t: t, lambda t: t,
             rpb_rows, decay_row]
    back = [lambda t: t.reshape(D), lambda t: t, lambda t: t, lambda t: t, lambda t: t, lambda t: t, lambda t: t,
            lambda t: _rpb_flat_t(t)[None], lambda t: t[:, 0:8].reshape(1, 2, 4)]
    small_w = (c_ctx, b_ada, g_pre_mix, g_post_mix, g_pre_mlp, g_post_mlp, ret_gn, na_rpb, ret_decay)
    small_m = (m_c_ctx, m_b_ada, m_g_pre_mix, m_g_post_mix, m_g_pre_mlp, m_g_post_mlp, m_ret_gn, m_na_rpb, m_ret_decay)
    small_v = (v_c_ctx, v_b_ada, v_g_pre_mix, v_g_post_mix, v_g_pre_mlp, v_g_post_mlp, v_ret_gn, v_na_rpb, v_ret_decay)
    *res, loss8 = small_update(gs[:, 18:18 + SMALL_SUM_ROWS], gbf, gcf, pcg[:, 0],
                               [(f(w), f(m), f(v)) for f, w, m, v in zip(views, small_w, small_m, small_v)])

    def leaves(ada, idx):
        s_c, s_b, s_g1, s_g2, s_g3, s_g4, s_gn, s_rpb, s_rd = [back[i](res[4 * i + idx]) for i in range(9)]
        return [s_c, ada[None], s_b, s_g1, s_g2, s_g3, s_g4, big[0][idx], s_rd, s_gn, s_rpb,
                big[1][idx], big[2][idx], big[3][idx]]

    return (loss8[0, 0], grad_x, *leaves(gw_ada, 0), *leaves(d_ada, 1), *leaves(m_ada, 2), *leaves(v_ada, 3))
```

```python
import functools
import math

import jax
import jax.numpy as jnp
from jax import lax
from jax.experimental import pallas as pl
from jax.experimental.pallas import tpu as pltpu

F32, BF16 = jnp.float32, jnp.bfloat16
D = 1024
SEQ = 2048
LC = 256
GW = 64
RH, RD, CH = 4, 128, 128
NPAIR = 4
IN_W = 3584
RET_W = 2048
DFF = 4096
EPS = 1e-6
NEG = -1e30
TN = 256
NCH = SEQ // CH
LR, B1, B2, AEPS, WD, STEP = 0.001, 0.9, 0.999, 1e-08, 0.01, 10
MESH = pl.DeviceIdType.MESH
VMEM_LIMIT = 56 * 1024 * 1024


def _cp(sem=None):
    return pltpu.CompilerParams(dimension_semantics=sem, vmem_limit_bytes=VMEM_LIMIT)


def _nn(a, b):
    return jnp.dot(a.astype(BF16), b.astype(BF16), preferred_element_type=F32)


def _nt(a, b):
    return lax.dot_general(a.astype(BF16), b.astype(BF16), (((1,), (1,)), ((), ())), preferred_element_type=F32)


def _tn(a, b):
    return lax.dot_general(a.astype(BF16), b.astype(BF16), (((0,), (0,)), ((), ())), preferred_element_type=F32)


@jax.custom_vjp
def mm_tn(a, b):
    return _tn(a, b)


mm_tn.defvjp(lambda a, b: (_tn(a, b), (a, b)), lambda r, g: (_nt(r[1], g), _nn(r[0], g)))


def _rms(x):
    return x * lax.rsqrt(jnp.mean(x * x, axis=-1, keepdims=True) + EPS)


def _rms_mod(x, g, sc, sh):
    return (_rms(x) * g) * (1.0 + sc) + sh


def _post_mix(x, mix, gt1, sc2, sh2, g_post_mix, g_pre_mlp):
    x1 = x + gt1 * (_rms(mix) * g_post_mix)
    return x1, _rms_mod(x1, g_pre_mlp, sc2, sh2)


def _head_loss(x1, m, gt2, g_post_mlp, tgt):
    err = x1 + gt2 * (_rms(m) * g_post_mlp) - tgt
    return 0.5 * jnp.sum(jnp.mean(err * err, axis=-1, keepdims=True), axis=0, keepdims=True)


def _ln_gate(o, g, w):
    mu = jnp.mean(o, axis=-1, keepdims=True)
    var = jnp.mean(jnp.square(o - mu), axis=-1, keepdims=True)
    y = (o - mu) * lax.rsqrt(var + EPS)
    return (y * w) * (g * jax.nn.sigmoid(g))


def _pair_order(x):
    lane = lax.broadcasted_iota(jnp.int32, x.shape, 1)
    return jnp.where((lane >= 32) & (lane < 64), pltpu.roll(x, 96, 1),
                     jnp.where((lane >= 64) & (lane < 96), pltpu.roll(x, 32, 1), x))


def _rope(x, cos, sin):
    return x * cos + pltpu.roll(x, 64, 1) * sin


def _rope_t(g, cos, sin):
    return g * cos + pltpu.roll(g * sin, 64, 1)


def _rope_tables():
    tok = lax.broadcasted_iota(jnp.int32, (SEQ, RD), 0)
    lane = lax.broadcasted_iota(jnp.int32, (SEQ, RD), 1)
    pos = jnp.where((lane & 32) == 0, tok >> 6, tok & (GW - 1)).astype(F32)
    ang = pos * jnp.exp((lane & 31).astype(F32) * (-math.log(10000.0) / 32))
    return jnp.cos(ang), jnp.where(lane < 64, -jnp.sin(ang), jnp.sin(ang))


def _chunk_loop(n, body, init, k=4):
    def several(t, carry):
        for i in range(k):
            carry = body(k * t + i, carry)
        return carry

    return lax.fori_loop(0, n // k, several, init)


def _fiota(shape, dim):
    return lax.broadcasted_iota(jnp.int32, shape, dim).astype(F32)


def _ret_state(k, v, s, lg, reverse):
    pos = _fiota((CH, 1), 0)
    b_exp = pos if reverse else (CH - 1.0 - pos)
    return jnp.exp(lg * CH) * s + mm_tn(k * jnp.exp(lg * b_exp), v)


class _Decays:
    def __init__(self, lgs):
        i, j, pos = _fiota((CH, CH), 0), _fiota((CH, CH), 1), _fiota((CH, 1), 0)
        diffs = (i - j, j - i)
        keep = (diffs[0] >= 0, diffs[1] > 0)
        mats = [jnp.where(m, jnp.exp(lg * jnp.where(m, d, 0.0)), 0.0) for lg, d, m in zip(lgs, diffs, keep)]
        self.mask = mats[0] + mats[1]
        self.dmask = [mats[0] * diffs[0], mats[1] * diffs[1]]
        a_exp, b_exp = (pos + 1.0, CH - pos), (CH - 1.0 - pos, pos)
        self.a = [jnp.exp(lg * e) for lg, e in zip(lgs, a_exp)]
        self.b = [jnp.exp(lg * e) for lg, e in zip(lgs, b_exp)]
        self.da = [a * e for a, e in zip(self.a, a_exp)]
        self.db = [b * e for b, e in zip(self.b, b_exp)]
        self.g = [jnp.exp(lg * CH) for lg in lgs]


def _both(x, w):
    return jnp.concatenate([x * w[0], x * w[1]], axis=1)


def _total(x):
    return jnp.sum(jnp.sum(x, axis=1, keepdims=True), axis=0, keepdims=True)


def _state_pass(dec, init, k_s, v_of, st_s):
    def step(t, carry):
        out = []
        for d, s in enumerate(carry):
            n = (NCH - 1 - t) if d else t
            sl = pl.ds(pl.multiple_of(n * CH, CH), CH)
            st_s[n, d * RD:(d + 1) * RD, :] = s
            out.append(dec.g[d] * s + _tn(k_s[sl, :] * dec.b[d], v_of(sl)))
        return tuple(out)

    _chunk_loop(NCH, step, tuple(init))


def premix_proj(xin, mod3, g_pre, wperm, is_ctx, name):
    nb, length, _ = xin.shape
    tn = min(2 * TN, length)

    def body(x_ref, mod_ref, g_ref, w_ref, h_ref, pret_ref, pna_ref):
        h = _rms_mod(x_ref[...], g_ref[...], mod_ref[1:2, :], mod_ref[0:1, :])
        hb = h.astype(BF16)
        h_ref[...] = hb
        pret_ref[...] = jnp.dot(hb, w_ref[:, :RET_W], preferred_element_type=F32)
        pna_ref[...] = jnp.dot(hb, w_ref[:, RET_W:], preferred_element_type=F32).astype(BF16)

    return pl.pallas_call(
        body, name=name, grid=(nb, length // tn),
        in_specs=[
            pl.BlockSpec((None, tn, D), lambda b, t: (b, t, 0)),
            pl.BlockSpec((None, 6, D), (lambda b, t: (2, 0, 0)) if is_ctx else (lambda b, t: (b, 0, 0))),
            pl.BlockSpec((1, D), lambda b, t: (0, 0)),
            pl.BlockSpec((D, IN_W), lambda b, t: (0, 0), pipeline_mode=pl.Buffered(1)),
        ],
        out_specs=[
            pl.BlockSpec((None, tn, D), lambda b, t: (b, t, 0)),
            pl.BlockSpec((None, tn, RET_W), lambda b, t: (b, t, 0)),
            pl.BlockSpec((None, tn, IN_W - RET_W), lambda b, t: (b, t, 0)),
        ],
        out_shape=[
            jax.ShapeDtypeStruct((nb, length, D), BF16),
            jax.ShapeDtypeStruct((nb, length, RET_W), F32),
            jax.ShapeDtypeStruct((nb, length, IN_W - RET_W), BF16),
        ],
        compiler_params=_cp(("arbitrary", "arbitrary")),
    )(xin, mod3, g_pre, wperm)


def premix_bwd(xin, mod3, g_pre, wperm, dproj, dx_tail, hosted, name):
    nb, length, _ = xin.shape
    tn = min(TN, length)
    is_ctx = dx_tail is None

    def body(*refs):
        own_in, h_in, own_out, h_out, _, h_sems = hosted.split(refs, 5 if is_ctx else 6, 2 if is_ctx else 3)
        if is_ctx:
            (x_ref, mod_ref, g_ref, w_ref, dp_ref), (dmod_ref, dg_ref) = own_in, own_out
        else:
            (x_ref, mod_ref, g_ref, w_ref, dp_ref, dxt_ref), (dx_ref, dmod_ref, dg_ref) = own_in, own_out
        b, t = pl.program_id(0), pl.program_id(1)
        grid_step = b * (length // tn) + t

        @pl.when(grid_step == 0)
        def _():
            hosted.start(h_in, h_out, h_sems)

        @pl.when(grid_step == nb * (length // tn) - 1)
        def _():
            hosted.finish(h_in, h_out, h_sems)

        dh = lax.dot_general(dp_ref[...], w_ref[...], (((1,), (1,)), ((), ())), preferred_element_type=F32)
        _, vjp = jax.vjp(_rms_mod, x_ref[...], g_ref[...], mod_ref[1:2, :], mod_ref[0:1, :])
        dx, dg, dsc, dsh = vjp(dh)
        if not is_ctx:
            dx_ref[...] = dx + dxt_ref[...]

        @pl.when((t == 0) & ((b == 0) if is_ctx else True))
        def _():
            dmod_ref[...] = jnp.zeros_like(dmod_ref)

        @pl.when((t == 0) & (b == 0))
        def _():
            dg_ref[...] = jnp.zeros_like(dg_ref)

        dmod_ref[0:1, :] += dsh
        dmod_ref[1:2, :] += dsc
        dg_ref[0:1, :] += dg

    tok = lambda b, t: (b, t, 0)
    in_specs = [
        pl.BlockSpec((None, tn, D), tok),
        pl.BlockSpec((None, 6, D), (lambda b, t: (2, 0, 0)) if is_ctx else (lambda b, t: (b, 0, 0))),
        pl.BlockSpec((1, D), lambda b, t: (0, 0)),
        pl.BlockSpec((D, IN_W), lambda b, t: (0, 0), pipeline_mode=pl.Buffered(1)),
        pl.BlockSpec((None, tn, IN_W), tok),
    ]
    args = [xin, mod3, g_pre, wperm, dproj]
    out_specs = [
        pl.BlockSpec((None, 6, D), (lambda b, t: (0, 0, 0)) if is_ctx else (lambda b, t: (b, 0, 0))),
        pl.BlockSpec((8, D), lambda b, t: (0, 0)),
    ]
    out_shape = [jax.ShapeDtypeStruct((1 if is_ctx else nb, 6, D), F32), jax.ShapeDtypeStruct((8, D), F32)]
    if not is_ctx:
        in_specs.append(pl.BlockSpec((None, tn, D), tok))
        args.append(dx_tail)
        out_specs.insert(0, pl.BlockSpec((None, tn, D), tok))
        out_shape.insert(0, jax.ShapeDtypeStruct((nb, length, D), F32))
    h_in_specs, h_out_specs = hosted.specs()
    return pl.pallas_call(
        body, name=name, grid=(nb, length // tn), in_specs=in_specs + h_in_specs, out_specs=out_specs + h_out_specs,
        out_shape=out_shape + hosted.out_shape, scratch_shapes=hosted.scratch,
        compiler_params=_cp(("arbitrary", "arbitrary")),
    )(*args, *hosted.args)


def _ret_specs(order):
    def im(f):
        return lambda *g: f(*order(*g))
    return dict(
        pret=pl.BlockSpec((None, SEQ, 512), im(lambda b, h: (b, 0, h))),
        pretc=pl.BlockSpec((None, LC, 512), im(lambda b, h: (b, 0, h))),
        rd=pl.BlockSpec((None, 2, 1), im(lambda b, h: (h, 0, 0))),
        gn=pl.BlockSpec((None, 1, RD), im(lambda b, h: (h, 0, 0))),
        tab=pl.BlockSpec((SEQ, RD), im(lambda b, h: (0, 0))),
        head=pl.BlockSpec((None, SEQ, RD), im(lambda b, h: (b, 0, h))),
    )


def retention_fwd(pret, pretc, rd, gn, cos, sin, hosted):
    nb = pret.shape[0]
    sp = _ret_specs(lambda b, h: (b, h))

    def body(*refs):
        own_in, h_in, own_out, h_out, own_scr, h_sems = hosted.split(refs, 6, 2)
        p_ref, pc_ref, rd_ref, gn_ref, cos_ref, sin_ref = own_in
        (o_ref, mix_ref), (q_s, k_s, o_s, st_s) = own_out, own_scr
        grid_step = pl.program_id(0) * RH + pl.program_id(1)

        @pl.when(grid_step == 0)
        def _():
            hosted.start(h_in, h_out, h_sems)

        cos_v, sin_v = cos_ref[...], sin_ref[...]
        q_s[...] = _rope(p_ref[:, 0:128], cos_v, sin_v) * (RD ** -0.5)
        k_s[...] = _rope(p_ref[:, 128:256], cos_v, sin_v)
        lgs, init = [], []
        for rev in (False, True):
            lg = jax.nn.log_sigmoid(rd_ref[int(rev):int(rev) + 1, :])
            s = jnp.zeros((RD, RD), F32)
            for n in ((1, 0) if rev else (0, 1)):
                s = _ret_state(pc_ref[n * CH:(n + 1) * CH, 128:256], pc_ref[n * CH:(n + 1) * CH, 256:384], s, lg, rev)
            lgs.append(lg)
            init.append(s)

        dec = _Decays(lgs)
        _state_pass(dec, init, k_s, lambda sl: p_ref[sl, 256:384], st_s)

        def chunk(n, carry):
            sl = pl.ds(pl.multiple_of(n * CH, CH), CH)
            q = q_s[sl, :]
            o_s[sl, :] = (_nn(_nt(q, k_s[sl, :]) * dec.mask, p_ref[sl, 256:384]) + _nn(_both(q, dec.a), st_s[n]))
            return carry

        _chunk_loop(NCH, chunk, 0)
        o = o_s[...]
        o_ref[...] = o
        mix_ref[...] = _ln_gate(o, p_ref[:, 384:512], gn_ref[...]).astype(BF16)

        @pl.when(grid_step == nb * RH - 1)
        def _():
            hosted.finish(h_in, h_out, h_sems)

    h_in_specs, h_out_specs = hosted.specs()
    return pl.pallas_call(
        body, name="retention_fwd", grid=(nb, RH),
        in_specs=[sp["pret"], sp["pretc"], sp["rd"], sp["gn"], sp["tab"], sp["tab"]] + h_in_specs,
        out_specs=[sp["head"], sp["head"]] + h_out_specs,
        out_shape=[jax.ShapeDtypeStruct((nb, SEQ, RH * RD), F32), jax.ShapeDtypeStruct((nb, SEQ, D), BF16)]
        + hosted.out_shape,
        scratch_shapes=[pltpu.VMEM((SEQ, RD), F32)] * 3 + [pltpu.VMEM((NCH, 2 * RD, RD), F32)] + hosted.scratch,
        compiler_params=_cp(("arbitrary", "arbitrary")),
    )(pret, pretc, rd, gn, cos, sin, *hosted.args)


def retention_bwd(pret, pretc, o_all, dmixin, rd, gn, cos, sin, hosted):
    nb = pret.shape[0]
    sp = _ret_specs(lambda h, b: (b, h))

    def body(*refs):
        own_in, h_in, own_out, h_out, own_scr, h_sems = hosted.split(refs, 8, 4)
        p_ref, pc_ref, o_ref, dmix_ref, rd_ref, gn_ref, cos_ref, sin_ref = own_in
        dp_ref, dpc_ref, drd_ref, dgn_ref = own_out
        q_s, k_s, do_s, dq_s, dk_s, dv_s, st_s, gst_s = own_scr
        b = pl.program_id(1)
        grid_step = pl.program_id(0) * nb + b

        @pl.when(grid_step == 0)
        def _():
            hosted.start(h_in, h_out, h_sems)

        cos_v, sin_v = cos_ref[...], sin_ref[...]
        q_s[...] = _rope(p_ref[:, 0:128], cos_v, sin_v) * (RD ** -0.5)
        k_s[...] = _rope(p_ref[:, 128:256], cos_v, sin_v)
        _, gate_vjp = jax.vjp(_ln_gate, o_ref[...], p_ref[:, 384:512], gn_ref[...])
        do, dg, dgn = gate_vjp(dmix_ref[...].astype(F32))
        do_s[...] = do
        dp_ref[:, 384:512] = dg.astype(BF16)

        @pl.when(b == 0)
        def _():
            drd_ref[...] = jnp.zeros_like(drd_ref)
            dgn_ref[...] = jnp.zeros_like(dgn_ref)

        dgn_ref[...] += dgn
        kcs = [pc_ref[n * CH:(n + 1) * CH, 128:256] for n in (0, 1)]
        vcs = [pc_ref[n * CH:(n + 1) * CH, 256:384] for n in (0, 1)]
        dirs = []
        init = []
        for rev in (False, True):
            rdv = rd_ref[int(rev):int(rev) + 1, :]
            lg = jax.nn.log_sigmoid(rdv)
            order_c = (1, 0) if rev else (0, 1)
            s = jnp.zeros((RD, RD), F32)
            ctx_states = []
            for n in order_c:
                ctx_states.append(s)
                s = _ret_state(kcs[n], vcs[n], s, lg, rev)
            dirs.append((rev, order_c, lg, rdv, ctx_states))
            init.append(s)
        dec = _Decays([lg for _, _, lg, _, _ in dirs])

        def v_of(sl):
            return p_ref[sl, 256:384]

        _state_pass(dec, init, k_s, v_of, st_s)
        zeros = jnp.zeros((CH, RD), F32)

        def scores_back(n, carry):
            dmask_sum, da_f, da_b = carry
            sl = pl.ds(pl.multiple_of(n * CH, CH), CH)
            q, k, v, do = q_s[sl, :], k_s[sl, :], v_of(sl), do_s[sl, :]
            scores = _nt(q, k)
            d_att = _nt(do, v)
            d_scores = d_att * dec.mask
            d_qa = _nt(do, st_s[n])
            d_qf, d_qb = d_qa[:, 0:RD], d_qa[:, RD:2 * RD]
            dq_s[sl, :] = _nn(d_scores, k) + d_qf * dec.a[0] + d_qb * dec.a[1]
            dk_s[sl, :] = _tn(d_scores, q)
            dv_s[sl, :] = _tn(scores * dec.mask, do)
            gst_s[n] = _tn(_both(q, dec.a), do)
            return dmask_sum + d_att * scores, da_f + d_qf * q, da_b + d_qb * q

        dmask_sum, da_f, da_b = _chunk_loop(NCH, scores_back, (zeros, zeros, zeros))

        def state_back(t, carry):
            out = []
            for d, r in enumerate(carry):
                n = t if d else (NCH - 1 - t)
                rows = slice(d * RD, (d + 1) * RD)
                own = gst_s[n, rows, :]
                gst_s[n, rows, :] = r
                out.append(own + dec.g[d] * r)
            return tuple(out)

        d_states = _chunk_loop(NCH, state_back, (zeros, zeros))

        def updates_back(n, carry):
            db_f, db_b, dg_f, dg_b = carry
            sl = pl.ds(pl.multiple_of(n * CH, CH), CH)
            k, r, s = k_s[sl, :], gst_s[n], st_s[n]
            d_kw = _nt(v_of(sl), r)
            d_kf, d_kb = d_kw[:, 0:RD], d_kw[:, RD:2 * RD]
            dk_s[sl, :] += d_kf * dec.b[0] + d_kb * dec.b[1]
            dv_s[sl, :] += _nn(_both(k, dec.b), r)
            return (db_f + d_kf * k, db_b + d_kb * k, dg_f + r[0:RD, :] * s[0:RD, :],
                    dg_b + r[RD:2 * RD, :] * s[RD:2 * RD, :])

        db_dg = _chunk_loop(NCH, updates_back, (zeros, zeros, zeros, zeros))
        dkc = [None, None]
        dvc = [None, None]
        for d, ((rev, order_c, lg, rdv, ctx_states), ds) in enumerate(zip(dirs, d_states)):
            dlg = (_total(dmask_sum * dec.dmask[d]) + _total((da_f, da_b)[d] * dec.da[d])
                   + _total(db_dg[d] * dec.db[d]) + CH * dec.g[d] * _total(db_dg[2 + d]))
            for idx in (1, 0):
                n = order_c[idx]
                _, vjp = jax.vjp(functools.partial(_ret_state, reverse=rev), kcs[n], vcs[n], ctx_states[idx], lg)
                dk_c, dv_c, ds, dl = vjp(ds)
                dlg = dlg + dl
                dkc[n] = dk_c if dkc[n] is None else dkc[n] + dk_c
                dvc[n] = dv_c if dvc[n] is None else dvc[n] + dv_c
            drd_ref[int(rev):int(rev) + 1, :] += dlg * jax.nn.sigmoid(-rdv)
        dp_ref[:, 0:128] = _rope_t(dq_s[...] * (RD ** -0.5), cos_v, sin_v).astype(BF16)
        dp_ref[:, 128:256] = _rope_t(dk_s[...], cos_v, sin_v).astype(BF16)
        dp_ref[:, 256:384] = dv_s[...].astype(BF16)
        zero = jnp.zeros((CH, RD), BF16)
        for n in (0, 1):
            rows = slice(n * CH, (n + 1) * CH)
            dpc_ref[rows, 0:128] = zero
            dpc_ref[rows, 128:256] = dkc[n].astype(BF16)
            dpc_ref[rows, 256:384] = dvc[n].astype(BF16)
            dpc_ref[rows, 384:512] = zero

        @pl.when(grid_step == RH * nb - 1)
        def _():
            hosted.finish(h_in, h_out, h_sems)

    h_in_specs, h_out_specs = hosted.specs()
    return pl.pallas_call(
        body, name="retention_bwd", grid=(RH, nb),
        in_specs=[sp["pret"], sp["pretc"], sp["head"], sp["head"], sp["rd"], sp["gn"], sp["tab"], sp["tab"]]
        + h_in_specs,
        out_specs=[
            pl.BlockSpec((None, SEQ, 512), lambda h, b: (b, 0, h)),
            pl.BlockSpec((None, LC, 512), lambda h, b: (b, 0, h)),
            pl.BlockSpec((None, 2, 1), lambda h, b: (h, 0, 0)),
            pl.BlockSpec((None, 1, RD), lambda h, b: (h, 0, 0)),
        ] + h_out_specs,
        out_shape=[
            jax.ShapeDtypeStruct((nb, SEQ, IN_W), BF16),
            jax.ShapeDtypeStruct((nb, LC, IN_W), BF16),
            jax.ShapeDtypeStruct((RH, 2, 1), F32),
            jax.ShapeDtypeStruct((RH, 1, RD), F32),
        ] + hosted.out_shape,
        scratch_shapes=[pltpu.VMEM((SEQ, RD), F32)] * 6 + [pltpu.VMEM((NCH, 2 * RD, RD), F32)] * 2 + hosted.scratch,
        compiler_params=_cp(("arbitrary", "arbitrary")),
    )(pret, pretc, o_all, dmixin, rd, gn, cos, sin, *hosted.args)


def _rpb_flat(rpb):
    return jnp.pad(rpb, ((0, 0), (0, 1), (0, 33))).reshape(NPAIR, 2, 1, 1024)


def _rpb_flat_t(dflat):
    return dflat.reshape(8, 16, 64)[:, :15, :31]


def _barrel(x, left):
    row = lax.broadcasted_iota(jnp.int32, x.shape, 0)
    n = x.shape[1]
    for bit in range(6):
        s = 1 << bit
        x = jnp.where(((row >> bit) & 1) == 1, pltpu.roll(x, (n - s) if left else s, 1), x)
    return x


NA_TILE_ROWS, NA_BAND_ROWS = 4, 12
NA_Q, NA_K = NA_TILE_ROWS * GW, NA_BAND_ROWS * GW
NA_TILES = SEQ // NA_Q


def _band_start(r0):
    return min(max(r0 - 4, 0), 32 - NA_BAND_ROWS)


def _tile_layout(t):
    rows = range(t * NA_TILE_ROWS, (t + 1) * NA_TILE_ROWS)
    return tuple((r if r < 4 else (r - 24 if r > 28 else 4), min(max(r - 4, 0), 24) - _band_start(rows[0]))
                 for r in rows)


NA_CLASSES = sorted(set(_tile_layout(t) for t in range(NA_TILES)))


def _tile_rows(cls):
    return NA_CLASSES[cls]


def _na_tile(t):
    start = jnp.clip(NA_TILE_ROWS * t - 4, 0, 32 - NA_BAND_ROWS)
    cls = 0
    for tile in range(NA_TILES):
        cls = jnp.where(t == tile, NA_CLASSES.index(_tile_layout(tile)), cls)
    return pl.ds(pl.multiple_of(t * NA_Q, NA_Q), NA_Q), pl.ds(pl.multiple_of(start * GW, NA_Q), NA_K), cls


def _na_probs(qst, kb, kc, bias):
    s_loc = _nt(qst, kb) + bias
    s_ctx = _nt(qst, kc)
    m = jnp.maximum(jnp.max(s_loc, axis=1, keepdims=True), jnp.max(s_ctx, axis=1, keepdims=True))
    e_loc, e_ctx = jnp.exp(s_loc - m), jnp.exp(s_ctx - m)
    den = jnp.sum(e_loc, axis=1, keepdims=True) + jnp.sum(e_ctx, axis=1, keepdims=True)
    return e_loc / den, e_ctx / den


def _stack_heads(t):
    lane = lax.broadcasted_iota(jnp.int32, t.shape, 1)
    zero = jnp.zeros_like(t)
    return jnp.concatenate([jnp.where(lane < 64, t, zero), jnp.where(lane >= 64, t, zero)], axis=0)


def _unstack_heads(t):
    n = t.shape[0] // 2
    lane = lax.broadcasted_iota(jnp.int32, (n, 128), 1)
    return jnp.where(lane < 64, t[:n], t[n:])


NA_BIAS_SHAPE = (len(NA_CLASSES), 2 * NA_Q, NA_K)


def _na_bias_pair(flat_ref, out_ref):
    qc = lax.broadcasted_iota(jnp.int32, (GW, 512), 0)
    kc = lax.broadcasted_iota(jnp.int32, (GW, 512), 1) & 63
    start = jnp.clip(qc - 8, 0, GW - 16)
    window = (kc >= start) & (kc < start + 16)
    fill = jnp.full((GW, NA_K - 512), NEG, F32)
    for hh in (0, 1):
        skew = _barrel(pltpu.roll(jnp.broadcast_to(flat_ref[hh], (GW, 1024)), 1024 - 15, 1), left=False)
        by_class = [jnp.where(window, (skew if rc == 7 else pltpu.roll(skew, (9 + rc) * 64, 1))[:, 0:512], NEG)
                    for rc in range(8)]
        for cls in range(len(NA_CLASSES)):
            for qr, (rc, off) in enumerate(_tile_rows(cls)):
                w = jnp.concatenate([by_class[rc], fill], axis=1)
                rows = slice(hh * NA_Q + qr * GW, hh * NA_Q + (qr + 1) * GW)
                out_ref[cls, rows, :] = pltpu.roll(w, off * GW, 1) if off else w


def na_fwd(pna, pnac, bias, mixin, hosted):
    nb = pna.shape[0]

    def body(*refs):
        (p_ref, pc_ref, bias_ref, _), h_in, (out_ref,), h_out, _, h_sems = hosted.split(refs, 4, 1)
        grid_step = pl.program_id(0) * nb + pl.program_id(1)

        @pl.when(grid_step == 0)
        def _():
            hosted.start(h_in, h_out, h_sems)

        kc, vc = pc_ref[:, 128:256], pc_ref[:, 256:384]

        def tile(t, carry):
            qsl, bsl, cls = _na_tile(t)
            kb, vb = p_ref[bsl, 128:256], p_ref[bsl, 256:384]
            p_loc, p_ctx = _na_probs(_stack_heads(p_ref[qsl, 0:128] * 0.125), kb, kc, bias_ref[cls])
            out_ref[qsl, :] = _unstack_heads(_nn(p_loc, vb) + _nn(p_ctx, vc)).astype(BF16)
            return carry

        lax.fori_loop(0, NA_TILES, tile, 0, unroll=4)

        @pl.when(grid_step == NPAIR * nb - 1)
        def _():
            hosted.finish(h_in, h_out, h_sems)

    h_in_specs, h_out_specs = hosted.specs()
    return pl.pallas_call(
        body, name="na_fwd", grid=(NPAIR, nb),
        in_specs=[
            pl.BlockSpec((None, SEQ, 384), lambda p, b: (b, 0, p)),
            pl.BlockSpec((None, LC, 384), lambda p, b: (b, 0, p)),
            pl.BlockSpec((None, len(NA_CLASSES), 2 * NA_Q, NA_K), lambda p, b: (p, 0, 0, 0)),
            pl.BlockSpec(memory_space=pl.ANY),
        ] + h_in_specs,
        out_specs=[pl.BlockSpec((None, SEQ, 128), lambda p, b: (b, 0, 4 + p))] + h_out_specs,
        out_shape=[jax.ShapeDtypeStruct((nb, SEQ, D), BF16)] + hosted.out_shape,
        input_output_aliases={3: 0},
        scratch_shapes=hosted.scratch,
        compiler_params=_cp(("arbitrary", "arbitrary")),
    )(pna, pnac, bias, mixin, *hosted.args)


def na_bwd(pna, pnac, bias, dmixin, dproj, dprojc, hosted):
    nb = pna.shape[0]

    def body(*refs):
        own_in, h_in, own_out, h_out, own_scr, h_sems = hosted.split(refs, 6, 3)
        p_ref, pc_ref, bias_ref, dmix_ref = own_in[:4]
        dp_ref, dpc_ref, dpat_ref = own_out
        dbias_s, dk_s, dv_s, dkc_s, dvc_s, res_s, resc_s = own_scr
        b, part = pl.program_id(1), pl.program_id(2)
        grid_step = (pl.program_id(0) * nb + b) * 3 + part

        @pl.when(grid_step == 0)
        def _():
            hosted.start(h_in, h_out, h_sems)

        @pl.when(grid_step == NPAIR * nb * 3 - 1)
        def _():
            hosted.finish(h_in, h_out, h_sems)

        @pl.when(part == 0)
        def _():
            @pl.when(b == 0)
            def _():
                dbias_s[...] = jnp.zeros_like(dbias_s)

            dk_s[...] = jnp.zeros_like(dk_s)
            dv_s[...] = jnp.zeros_like(dv_s)
            dkc_s[...] = jnp.zeros_like(dkc_s)
            dvc_s[...] = jnp.zeros_like(dvc_s)
            kc, vc = pc_ref[:, 128:256], pc_ref[:, 256:384]

            def tile(t, carry):
                qsl, bsl, cls = _na_tile(t)
                kb, vb = p_ref[bsl, 128:256], p_ref[bsl, 256:384]
                qst, dost = _stack_heads(p_ref[qsl, 0:128] * 0.125), _stack_heads(dmix_ref[qsl, :])
                p_loc, p_ctx = _na_probs(qst, kb, kc, bias_ref[cls])
                dp_loc, dp_ctx = _nt(dost, vb), _nt(dost, vc)
                delta = (jnp.sum(p_loc * dp_loc, axis=1, keepdims=True)
                         + jnp.sum(p_ctx * dp_ctx, axis=1, keepdims=True))
                ds_loc, ds_ctx = p_loc * (dp_loc - delta), p_ctx * (dp_ctx - delta)
                dbias_s[cls] += ds_loc
                res_s[0, qsl, :] = _unstack_heads((_nn(ds_loc, kb) + _nn(ds_ctx, kc)) * 0.125).astype(BF16)
                dk_s[bsl, :] += _tn(ds_loc, qst)
                dv_s[bsl, :] += _tn(p_loc, dost)
                dkc_s[...] += _tn(ds_ctx, qst)
                dvc_s[...] += _tn(p_ctx, dost)
                return carry

            lax.fori_loop(0, NA_TILES, tile, 0, unroll=2)
            res_s[1] = dk_s[...].astype(BF16)
            res_s[2] = dv_s[...].astype(BF16)
            resc_s[0] = jnp.zeros((LC, 128), BF16)
            resc_s[1] = dkc_s[...].astype(BF16)
            resc_s[2] = dvc_s[...].astype(BF16)

            @pl.when(b == nb - 1)
            def _():
                for hh in (0, 1):
                    by_class = [None] * 8
                    for cls in range(len(NA_CLASSES)):
                        for qr, (rc, off) in enumerate(_tile_rows(cls)):
                            w = dbias_s[cls, hh * NA_Q + qr * GW:hh * NA_Q + (qr + 1) * GW, :]
                            w = (pltpu.roll(w, NA_K - off * GW, 1) if off else w)[:, 0:512]
                            by_class[rc] = w if by_class[rc] is None else by_class[rc] + w
                    skew = jnp.zeros((GW, 1024), F32)
                    for rc in range(8):
                        w = jnp.concatenate([by_class[rc], jnp.zeros((GW, 512), F32)], axis=1)
                        skew = skew + (w if rc == 7 else pltpu.roll(w, (7 - rc) * 64, 1))
                    dpat_ref[hh] = jnp.sum(pltpu.roll(_barrel(skew, left=True), 15, 1), axis=0, keepdims=True)

        dp_ref[...] = res_s[part]
        dpc_ref[...] = resc_s[part]

    h_in_specs, h_out_specs = hosted.specs()
    return pl.pallas_call(
        body, name="na_bwd", grid=(NPAIR, nb, 3),
        in_specs=[
            pl.BlockSpec((None, SEQ, 384), lambda p, b, s: (b, 0, p)),
            pl.BlockSpec((None, LC, 384), lambda p, b, s: (b, 0, p)),
            pl.BlockSpec((None, len(NA_CLASSES), 2 * NA_Q, NA_K), lambda p, b, s: (p, 0, 0, 0)),
            pl.BlockSpec((None, SEQ, 128), lambda p, b, s: (b, 0, 4 + p)),
            pl.BlockSpec(memory_space=pl.ANY),
            pl.BlockSpec(memory_space=pl.ANY),
        ] + h_in_specs,
        out_specs=[
            pl.BlockSpec((None, SEQ, 128), lambda p, b, s: (b, 0, 16 + 3 * p + s)),
            pl.BlockSpec((None, LC, 128), lambda p, b, s: (b, 0, 16 + 3 * p + s)),
            pl.BlockSpec((None, 2, 1, 1024), lambda p, b, s: (p, 0, 0, 0)),
        ] + h_out_specs,
        out_shape=[
            jax.ShapeDtypeStruct((nb, SEQ, IN_W), BF16),
            jax.ShapeDtypeStruct((nb, LC, IN_W), BF16),
            jax.ShapeDtypeStruct((NPAIR, 2, 1, 1024), F32),
        ] + hosted.out_shape,
        input_output_aliases={4: 0, 5: 1},
        scratch_shapes=[
            pltpu.VMEM((len(NA_CLASSES), 2 * NA_Q, NA_K), F32),
            pltpu.VMEM((SEQ, 128), F32), pltpu.VMEM((SEQ, 128), F32),
            pltpu.VMEM((LC, 128), F32), pltpu.VMEM((LC, 128), F32),
            pltpu.VMEM((3, SEQ, 128), BF16), pltpu.VMEM((3, LC, 128), BF16),
        ] + hosted.scratch,
        compiler_params=_cp(("arbitrary", "arbitrary", "arbitrary")),
    )(pna, pnac, bias, dmixin, dproj, dprojc, *hosted.args)


def tail_fwd_bwd(x, mixin, tgt, mod3, g_post_mix, g_pre_mlp, g_post_mlp, wout, w1, w2):
    nb = x.shape[0]

    def body(x_ref, mi_ref, tgt_ref, mod_ref, gpm_ref, gpl_ref, gpo_ref, wo_ref, w1_ref, w2_ref,
             dx_ref, dmix_ref, h2_ref, du_ref, a_ref, dm_ref, dmi_ref, dmod_ref, dg_ref, loss_ref):
        b, t = pl.program_id(0), pl.program_id(1)
        gt1, sh2, sc2, gt2 = mod_ref[2:3, :], mod_ref[3:4, :], mod_ref[4:5, :], mod_ref[5:6, :]
        mix = jnp.dot(mi_ref[...], wo_ref[...], preferred_element_type=F32)
        (x1, h2), vjp_a = jax.vjp(_post_mix, x_ref[...], mix, gt1, sc2, sh2, gpm_ref[...], gpl_ref[...])
        h2b = h2.astype(BF16)
        h2_ref[...] = h2b
        m = jnp.zeros((TN, D), F32)
        relus = []
        for j in range(4):
            cols = slice(j * D, (j + 1) * D)
            r = jnp.maximum(jnp.dot(h2b, w1_ref[j], preferred_element_type=F32), 0.0)
            ab = (r * r).astype(BF16)
            a_ref[:, cols] = ab
            m = m + jnp.dot(ab, w2_ref[cols, :], preferred_element_type=F32)
            relus.append(r)
        loss, vjp_b = jax.vjp(_head_loss, x1, m, gt2, gpo_ref[...], tgt_ref[...])
        dx1, dm, dgt2, dgpo, _ = vjp_b(jnp.ones((1, 1), F32))
        dmb = dm.astype(BF16)
        dm_ref[...] = dmb
        dh2 = jnp.zeros((TN, D), F32)
        for j in range(4):
            cols = slice(j * D, (j + 1) * D)
            da = lax.dot_general(dmb, w2_ref[cols, :], (((1,), (1,)), ((), ())), preferred_element_type=F32)
            dub = (da * (2.0 * relus[j])).astype(BF16)
            du_ref[:, cols] = dub
            dh2 = dh2 + lax.dot_general(dub, w1_ref[j], (((1,), (1,)), ((), ())), preferred_element_type=F32)
        dx, dmix, dgt1, dsc2, dsh2, dgpm, dgpl = vjp_a((dx1, dh2))
        dx_ref[...] = dx
        dmixb = dmix.astype(BF16)
        dmix_ref[...] = dmixb
        dmi_ref[...] = lax.dot_general(dmixb, wo_ref[...], (((1,), (1,)), ((), ())),
                                       preferred_element_type=F32).astype(BF16)

        @pl.when(t == 0)
        def _():
            dmod_ref[...] = jnp.zeros_like(dmod_ref)

        @pl.when((t == 0) & (b == 0))
        def _():
            dg_ref[...] = jnp.zeros_like(dg_ref)
            loss_ref[...] = jnp.zeros_like(loss_ref)

        dmod_ref[2:3, :] += dgt1
        dmod_ref[3:4, :] += dsh2
        dmod_ref[4:5, :] += dsc2
        dmod_ref[5:6, :] += dgt2
        dg_ref[0:1, :] += dgpm
        dg_ref[1:2, :] += dgpl
        dg_ref[2:3, :] += dgpo
        loss_ref[...] += jnp.broadcast_to(loss, loss_ref.shape)

    tok = lambda b, t: (b, t, 0)
    const = lambda b, t: (0, 0)
    vec = pl.BlockSpec((1, D), const)
    return pl.pallas_call(
        body, name="tail_fwd_bwd", grid=(nb, SEQ // TN),
        in_specs=[
            pl.BlockSpec((None, TN, D), tok), pl.BlockSpec((None, TN, D), tok), pl.BlockSpec((None, TN, D), tok),
            pl.BlockSpec((None, 6, D), lambda b, t: (b, 0, 0)), vec, vec, vec,
            pl.BlockSpec((D, D), const, pipeline_mode=pl.Buffered(1)),
            pl.BlockSpec((4, D, D), lambda b, t: (0, 0, 0), pipeline_mode=pl.Buffered(1)),
            pl.BlockSpec((DFF, D), const, pipeline_mode=pl.Buffered(1)),
        ],
        out_specs=[
            pl.BlockSpec((None, TN, D), tok), pl.BlockSpec((None, TN, D), tok), pl.BlockSpec((None, TN, D), tok),
            pl.BlockSpec((None, TN, DFF), tok), pl.BlockSpec((None, TN, DFF), tok), pl.BlockSpec((None, TN, D), tok),
            pl.BlockSpec((None, TN, D), tok),
            pl.BlockSpec((None, 6, D), lambda b, t: (b, 0, 0)),
            pl.BlockSpec((8, D), const), pl.BlockSpec((8, 128), const),
        ],
        out_shape=[
            jax.ShapeDtypeStruct((nb, SEQ, D), F32), jax.ShapeDtypeStruct((nb, SEQ, D), BF16),
            jax.ShapeDtypeStruct((nb, SEQ, D), BF16), jax.ShapeDtypeStruct((nb, SEQ, DFF), BF16),
            jax.ShapeDtypeStruct((nb, SEQ, DFF), BF16), jax.ShapeDtypeStruct((nb, SEQ, D), BF16),
            jax.ShapeDtypeStruct((nb, SEQ, D), BF16),
            jax.ShapeDtypeStruct((nb, 6, D), F32), jax.ShapeDtypeStruct((8, D), F32),
            jax.ShapeDtypeStruct((8, 128), F32),
        ],
        compiler_params=_cp(("arbitrary", "arbitrary")),
    )(x, mixin, tgt, mod3, g_post_mix, g_pre_mlp, g_post_mlp, wout, w1, w2)


def weight_grad(pairs, name, out_dtype=F32, col_blocks=False, tm=1024, tn=1024, tk=2048):
    m, n = pairs[0][0].shape[1], pairs[0][1].shape[1]
    tn = min(tn, n)
    tks = [min(tk, xa.shape[0]) for xa, _ in pairs]
    steps = [xa.shape[0] // t for (xa, _), t in zip(pairs, tks)]
    total = sum(steps)
    offs = [sum(steps[:i]) for i in range(len(pairs))]

    def body(*refs):
        out_ref, acc = refs[2 * len(pairs)], refs[-1]
        k = pl.program_id(2)

        @pl.when(k == 0)
        def _():
            acc[...] = jnp.zeros_like(acc)

        for i in range(len(pairs)):
            @pl.when((k >= offs[i]) & (k < offs[i] + steps[i]))
            def _(i=i):
                acc[...] += lax.dot_general(refs[2 * i][...], refs[2 * i + 1][...], (((0,), (0,)), ((), ())),
                                            preferred_element_type=F32)

        if out_dtype != F32:
            @pl.when(k == total - 1)
            def _():
                out_ref[...] = acc[...].astype(out_dtype)

    in_specs, args = [], []
    for i, (xa, ya) in enumerate(pairs):
        clamp = lambda k, i=i: jnp.clip(k - offs[i], 0, steps[i] - 1)
        in_specs.append(pl.BlockSpec((tks[i], tm), lambda a, c, k, clamp=clamp: (clamp(k), a)))
        in_specs.append(pl.BlockSpec((tks[i], tn), lambda a, c, k, clamp=clamp: (clamp(k), c)))
        args += [xa, ya]
    if col_blocks:
        out_spec = pl.BlockSpec((None, tm, tn), lambda a, c, k: (c, a, 0))
        out_shape = jax.ShapeDtypeStruct((n // tn, m, tn), out_dtype)
    else:
        out_spec = pl.BlockSpec((tm, tn), lambda a, c, k: (a, c))
        out_shape = jax.ShapeDtypeStruct((m, n), out_dtype)
    return pl.pallas_call(
        body, name=name, grid=(m // tm, n // tn, total), in_specs=in_specs, out_specs=out_spec, out_shape=out_shape,
        scratch_shapes=[] if out_dtype == F32 else [pltpu.VMEM((tm, tn), F32)],
        compiler_params=_cp(("arbitrary", "arbitrary", "arbitrary")),
    )(*args)


def _perm_block(t):
    return 4 * (t % 4) + t // 4 if t < 16 else 16 + 3 * ((t - 16) % 4) + (t - 16) // 4


def _is_rope_block(p):
    return p < 16 and p % 4 < 2


def unpack_w_in(blocks):
    def body(i_ref, o_ref):
        for t in range(28):
            p = _perm_block(t)
            blk = i_ref[t // 7, :, (t % 7) * 128:(t % 7 + 1) * 128]
            if _is_rope_block(p):
                blk = _pair_order(blk.astype(F32)).astype(BF16)
            o_ref[:, p * 128:(p + 1) * 128] = blk

    return pl.pallas_call(
        body, name="unpack_w_in", grid=(2,),
        in_specs=[pl.BlockSpec((4, D // 2, 896), lambda i: (0, i, 0))],
        out_specs=pl.BlockSpec((D // 2, IN_W), lambda i: (i, 0)),
        out_shape=jax.ShapeDtypeStruct((D, IN_W), BF16),
    )(blocks)


def pack_w_in(dw):
    def body(i_ref, o_ref):
        for t in range(28):
            p = _perm_block(t)
            blk = i_ref[:, p * 128:(p + 1) * 128]
            if _is_rope_block(p):
                blk = _pair_order(blk)
            o_ref[t // 7, :, (t % 7) * 128:(t % 7 + 1) * 128] = blk.astype(BF16)

    return pl.pallas_call(
        body, name="pack_w_in", grid=(4,),
        in_specs=[pl.BlockSpec((D // 4, IN_W), lambda i: (i, 0))],
        out_specs=pl.BlockSpec((4, D // 4, 896), lambda i: (0, i, 0)),
        out_shape=jax.ShapeDtypeStruct((4, D, 896), BF16),
    )(dw)


def _place():
    return lax.axis_index("x"), lax.axis_index("y"), lax.axis_index("c")


class Hosted:
    def __init__(self, args, out_shape, scratch, start, finish):
        self.args, self.out_shape, self.scratch, self.start, self.finish = args, out_shape, scratch, start, finish

    def specs(self):
        hbm = pl.BlockSpec(memory_space=pl.ANY)
        return [hbm] * len(self.args), [hbm] * len(self.out_shape)

    def split(self, refs, n_in, n_out):
        a, b = len(self.args), len(self.out_shape)
        cuts = [n_in, n_in + a, n_in + a + n_out, n_in + a + n_out + b, len(refs) - len(self.scratch)]
        parts = [refs[i:j] for i, j in zip([0] + cuts, cuts + [len(refs)])]
        return parts[0], parts[1], parts[2], parts[3], parts[4], parts[5]


def no_exchange():
    return Hosted([], [], [], lambda *a: None, lambda *a: None)


def run_hosted(hosted, name):
    def body(*refs):
        _, ins, _, outs, _, sems = hosted.split(refs, 0, 0)
        hosted.start(ins, outs, sems)
        hosted.finish(ins, outs, sems)

    in_specs, out_specs = hosted.specs()
    return pl.pallas_call(body, name=name, in_specs=in_specs, out_specs=out_specs, out_shape=hosted.out_shape,
                          scratch_shapes=hosted.scratch)(*hosted.args)


def gather8(blocks, relay_diagonal=False):
    na = len(blocks)

    def copies(ins, outs, sems):
        send_sems, recv_sems, local_sem = sems
        x, y, c = _place()
        me, sibling = (x, y, c), (x, y, 1 - c)
        chips = [(1 - x, y), (x, 1 - y), (1 - x, 1 - y)]

        def slot(o_ref, px, py, pc, half=None):
            ref = o_ref.at[4 * px + 2 * py + pc]
            if half is None:
                return ref
            rows = ref.shape[0] // 2
            return ref.at[pl.ds(half * rows, rows)]

        def copy(a, k, block, to, src=None, half=None):
            return pltpu.make_async_remote_copy(
                src_ref=slot(outs[a], *block, half) if src is None else src, dst_ref=slot(outs[a], *block, half),
                send_sem=send_sems.at[a, k], recv_sem=recv_sems.at[a, k], device_id=to, device_id_type=MESH)

        mine = [pltpu.make_async_copy(ins[a], slot(outs[a], *me), local_sem.at[a]) for a in range(na)]
        first = []
        for a in range(na):
            first.append(copy(a, 0, me, sibling, src=ins[a]))
            first += [copy(a, 1 + j, me, (*chip, c), src=ins[a])
                      for j, chip in enumerate(chips[:2] if relay_diagonal else chips)]
        return copy, mine, first, me, sibling, chips, c

    def start(ins, outs, sems):
        _, mine, first, *_ = copies(ins, outs, sems)
        for cp in mine + first:
            cp.start()

    def hops(copy, me, sibling, chips, c):
        pairs = []
        for j, chip in enumerate(chips[:2] if relay_diagonal else chips):
            for a in range(na):
                onward = [copy(a, 4 + j, (*chip, c), sibling)]
                if relay_diagonal:
                    onward.insert(0, copy(a, (3, 7)[j], (*chip, c), (*chips[1 - j], c), half=j))
                pairs.append(((a, 1 + j, (*chip, c)), onward))
        return pairs

    def relay(ins, outs, sems):
        copy, _, _, me, sibling, chips, c = copies(ins, outs, sems)
        for arrival, onward in hops(copy, me, sibling, chips, c):
            copy(*arrival, me).wait_recv()
            for cp in onward:
                cp.start()

    def finish(ins, outs, sems):
        copy, mine, first, me, sibling, chips, c = copies(ins, outs, sems)
        if not relay_diagonal:
            relay(ins, outs, sems)
        passed = [cp for _, onward in hops(copy, me, sibling, chips, c) for cp in onward]
        if relay_diagonal:
            for a in range(na):
                copy(a, 3, (*chips[2], c), me, half=0).wait_recv()
                copy(a, 7, (*chips[2], c), me, half=1).wait_recv()
                cp = copy(a, 6, (*chips[2], c), sibling)
                cp.start()
                passed.append(cp)
        for a in range(na):
            copy(a, 0, sibling, me).wait_recv()
            for j, chip in enumerate(chips):
                copy(a, 4 + j, (*chip, 1 - c), me).wait_recv()
        for cp in first + passed:
            cp.wait_send()
        for cp in mine:
            cp.wait()

    hosted = Hosted(list(blocks), [jax.ShapeDtypeStruct((8,) + b.shape, b.dtype) for b in blocks],
                    [pltpu.SemaphoreType.DMA((na, 8)), pltpu.SemaphoreType.DMA((na, 8)), pltpu.SemaphoreType.DMA((na,))],
                    start, finish)
    hosted.relay = relay
    return hosted


def chips3(arrays):
    na = len(arrays)

    def copies(ins, outs, sems):
        send_sems, recv_sems = sems
        x, y, c = _place()
        return [pltpu.make_async_remote_copy(
            src_ref=ins[a].at[2 * px + py], dst_ref=outs[a].at[k], send_sem=send_sems.at[a, k],
            recv_sem=recv_sems.at[a, k], device_id=(px, py, c), device_id_type=MESH)
            for a in range(na) for k, (px, py) in enumerate([(1 - x, y), (x, 1 - y), (1 - x, 1 - y)])]

    def start(ins, outs, sems):
        for cp in copies(ins, outs, sems):
            cp.start()

    def finish(ins, outs, sems):
        for cp in copies(ins, outs, sems):
            cp.wait()

    return Hosted(list(arrays), [jax.ShapeDtypeStruct((3,) + a.shape[1:], a.dtype) for a in arrays],
                  [pltpu.SemaphoreType.DMA((na, 3)), pltpu.SemaphoreType.DMA((na, 3))], start, finish)


def siblings(arrays):
    na = len(arrays)

    def copies(ins, outs, sems):
        send_sems, recv_sems = sems
        x, y, c = _place()
        return [pltpu.make_async_remote_copy(
            src_ref=ins[a], dst_ref=outs[a], send_sem=send_sems.at[a], recv_sem=recv_sems.at[a],
            device_id=(x, y, 1 - c), device_id_type=MESH) for a in range(na)]

    def start(ins, outs, sems):
        for cp in copies(ins, outs, sems):
            cp.start()

    def finish(ins, outs, sems):
        for cp in copies(ins, outs, sems):
            cp.wait()

    return Hosted(list(arrays), [jax.ShapeDtypeStruct(a.shape, a.dtype) for a in arrays],
                  [pltpu.SemaphoreType.DMA((na,)), pltpu.SemaphoreType.DMA((na,))], start, finish)


def both(first, second):
    na, no, ns = len(first.args), len(first.out_shape), len(first.scratch)

    def start(ins, outs, sems):
        first.start(ins[:na], outs[:no], sems[:ns])
        second.start(ins[na:], outs[no:], sems[ns:])

    def finish(ins, outs, sems):
        first.finish(ins[:na], outs[:no], sems[:ns])
        second.finish(ins[na:], outs[no:], sems[ns:])

    return Hosted(first.args + second.args, first.out_shape + second.out_shape, first.scratch + second.scratch,
                  start, finish)


def siblings4(arrays):
    na = len(arrays)

    def copies(ins, outs, sems):
        send_sems, recv_sems = sems
        x, y, c = _place()
        return [pltpu.make_async_remote_copy(
            src_ref=ins[a].at[2 * j + 1 - c], dst_ref=outs[a].at[j],
            send_sem=send_sems.at[a, j], recv_sem=recv_sems.at[a, j],
            device_id=(x, y, 1 - c), device_id_type=MESH) for a in range(na) for j in range(4)]

    def start(ins, outs, sems):
        for cp in copies(ins, outs, sems):
            cp.start()

    def finish(ins, outs, sems):
        for cp in copies(ins, outs, sems):
            cp.wait()

    return Hosted(list(arrays), [jax.ShapeDtypeStruct((4,) + a.shape[1:], a.dtype) for a in arrays],
                  [pltpu.SemaphoreType.DMA((na, 4)), pltpu.SemaphoreType.DMA((na, 4))], start, finish)


def _row_tile(r):
    for cand in (512, 256, 128, 64, 32, 16, 8):
        if r % cand == 0:
            return cand
    return r


def chip_partial(place, g8s, landed4s, name):
    n = len(g8s)

    def body(place_ref, *refs):
        del place_ref
        for g_ref, l_ref, o_ref in zip(refs[:n], refs[n:2 * n], refs[2 * n:]):
            o_ref[...] = (g_ref[...].astype(F32) + l_ref[...].astype(F32)).astype(BF16)

    own = [pl.BlockSpec((None,) + g.shape[1:], lambda j, s: (2 * j + s[0], 0, 0)) for g in g8s]
    plain = [pl.BlockSpec((None,) + g.shape[1:], lambda j, s: (j, 0, 0)) for g in g8s]
    return pl.pallas_call(
        body, name=name,
        grid_spec=pltpu.PrefetchScalarGridSpec(num_scalar_prefetch=1, grid=(4,), in_specs=own + plain, out_specs=plain),
        out_shape=[jax.ShapeDtypeStruct((4,) + g.shape[1:], BF16) for g in g8s],
    )(place, *g8s, *landed4s)


def shard_sum(place, partial4s, landed3s, name):
    n = len(partial4s)

    def body(place_ref, *refs):
        del place_ref
        for p_ref, l_ref, o_ref in zip(refs[:n], refs[n:2 * n], refs[2 * n:]):
            acc = p_ref[...].astype(F32)
            for k in range(3):
                acc = acc + l_ref[k].astype(F32)
            o_ref[...] = acc

    def halves(p, lead):
        r, ccols = p.shape[1:]
        return (lead, r // 2, ccols)

    return pl.pallas_call(
        body, name=name,
        grid_spec=pltpu.PrefetchScalarGridSpec(
            num_scalar_prefetch=1, grid=(2,),
            in_specs=[pl.BlockSpec(halves(p, None), lambda i, s: (s[1], i, 0)) for p in partial4s]
            + [pl.BlockSpec(halves(p, 3), lambda i, s: (0, i, 0)) for p in partial4s],
            out_specs=[pl.BlockSpec(halves(p, None)[1:], lambda i, s: (i, 0)) for p in partial4s]),
        out_shape=[jax.ShapeDtypeStruct(p.shape[1:], F32) for p in partial4s],
    )(place, *partial4s, *landed3s)


def _adamw_math(w, g, m, v):
    m2 = B1 * m + (1.0 - B1) * g
    v2 = B2 * v + (1.0 - B2) * (g * g)
    m_hat = m2 / (1.0 - B1 ** STEP)
    v_hat = v2 / (1.0 - B2 ** STEP)
    return -LR * (m_hat / (jnp.sqrt(v_hat) + AEPS) + WD * w), m2, v2


def adamw_halves(place, w, mine, theirs, m, v, name):
    r, ccols = w.shape
    hr = r // 2
    tr = _row_tile(hr)
    nt = hr // tr

    def body(place_ref, w_ref, a_ref, b_ref, m_ref, v_ref, g_out, d_out, m_out, v_out):
        g = jnp.where(pl.program_id(0) == place_ref[0], a_ref[...], b_ref[...])
        d, m2, v2 = _adamw_math(w_ref[...], g, m_ref[...], v_ref[...])
        g_out[...] = g
        d_out[...] = d
        m_out[...] = m2
        v_out[...] = v2

    full = pl.BlockSpec((tr, ccols), lambda h, i, s: (h * nt + i, 0))
    part = pl.BlockSpec((tr, ccols), lambda h, i, s: (i, 0))
    return pl.pallas_call(
        body, name=name,
        grid_spec=pltpu.PrefetchScalarGridSpec(
            num_scalar_prefetch=1, grid=(2, nt), in_specs=[full, part, part, full, full], out_specs=[full] * 4),
        out_shape=[jax.ShapeDtypeStruct((r, ccols), F32)] * 4,
    )(place, w, mine, theirs, m, v)


def adamw_group(place, halved, plain, hosted, name):
    rows = halved[0][0].shape[0]
    tr = 128
    nt = rows // 2 // tr
    nh, npl = len(halved), len(plain)

    def body(place_ref, *refs):
        own_in, h_in, own_out, h_out, _, h_sems = hosted.split(refs, 5 * nh + 4 * npl, 4 * nh + 3 * npl)
        half = pl.program_id(0)
        grid_step = half * nt + pl.program_id(1)

        @pl.when(grid_step == 0)
        def _():
            hosted.start(h_in, h_out, h_sems)

        for i in range(nh):
            w_ref, a_ref, b_ref, m_ref, v_ref = own_in[5 * i:5 * i + 5]
            g = jnp.where(half == place_ref[0], a_ref[...], b_ref[...])
            res = (g,) + _adamw_math(w_ref[...], g, m_ref[...], v_ref[...])
            for o_ref, r in zip(own_out[4 * i:4 * i + 4], res):
                o_ref[...] = r
        for i in range(npl):
            w_ref, g_ref, m_ref, v_ref = own_in[5 * nh + 4 * i:5 * nh + 4 * i + 4]
            res = _adamw_math(w_ref[...], g_ref[...], m_ref[...], v_ref[...])
            for o_ref, r in zip(own_out[4 * nh + 3 * i:4 * nh + 3 * i + 3], res):
                o_ref[...] = r

        @pl.when(grid_step == 2 * nt - 1)
        def _():
            hosted.finish(h_in, h_out, h_sems)

    def full(cols):
        return pl.BlockSpec((tr, cols), lambda h, i, s: (h * nt + i, 0))

    def part(cols):
        return pl.BlockSpec((tr, cols), lambda h, i, s: (i, 0))

    in_specs, out_specs, out_shape, args = [], [], [], []
    for w, a, b, m, v in halved:
        cols = w.shape[1]
        in_specs += [full(cols), part(cols), part(cols), full(cols), full(cols)]
        out_specs += [full(cols)] * 4
        out_shape += [jax.ShapeDtypeStruct(w.shape, F32)] * 4
        args += [w, a, b, m, v]
    for w, g, m, v in plain:
        cols = w.shape[1]
        in_specs += [full(cols)] * 4
        out_specs += [full(cols)] * 3
        out_shape += [jax.ShapeDtypeStruct(w.shape, F32)] * 3
        args += [w, g, m, v]
    h_in_specs, h_out_specs = hosted.specs()
    return pl.pallas_call(
        body, name=name,
        grid_spec=pltpu.PrefetchScalarGridSpec(
            num_scalar_prefetch=1, grid=(2, nt), in_specs=in_specs + h_in_specs, out_specs=out_specs + h_out_specs,
            scratch_shapes=hosted.scratch),
        out_shape=out_shape + hosted.out_shape,
        compiler_params=_cp(("arbitrary", "arbitrary")),
    )(place, *args, *hosted.args)


def _silu(x):
    return x * jax.nn.sigmoid(x)


def prologue(c_rows, c_ctx_row, w_ada, b_shard, rpb_flat, half_w_in, late_shards):
    shape = jax.ShapeDtypeStruct
    n_late = len(late_shards)
    half_shapes = [(w.shape[0] // 2, w.shape[1]) for w in late_shards]
    g_w = gather8([half_w_in], relay_diagonal=True)
    g_c = gather8([shape((8, D), F32)])
    g_m = chips3([shape((4, 8, 1536), F32)])

    def body(*refs):
        c_ref, cc_ref, w_ref, b_ref, flat_ref, hw_ref = refs[:6]
        late_refs = refs[6:6 + n_late]
        cin_ref, mg_ref, gw_ref, bias_ref, cos_ref, sin_ref = refs[6 + n_late:12 + n_late]
        rest = refs[12 + n_late:]
        half_refs, (cg_s, ms_s, bias_s, need_s, landed_s) = rest[:n_late], rest[n_late:n_late + 5]
        stage, (load_sem, bias_sem), sems = rest[n_late + 5:2 * n_late + 5], rest[2 * n_late + 5:2 * n_late + 7], \
            rest[2 * n_late + 7:]
        sw, sc, sm = sems[0:3], sems[3:6], sems[6:8]
        px, py, core = _place()
        g_c.start([c_ref], [cg_s], sc)
        g_w.start([hw_ref], [gw_ref], sw)
        loads = [pltpu.make_async_copy(late_refs[a].at[pl.ds(core * half_shapes[a][0], half_shapes[a][0]), :],
                                       stage[a], load_sem.at[a]) for a in range(n_late)]
        for cp in loads:
            cp.start()
        g_c.finish([c_ref], [cg_s], sc)
        cin_ref[...] = jnp.zeros_like(cin_ref)
        for dev in range(8):
            cin_ref[2 * dev:2 * dev + 2, :] = cg_s[dev, 0:2, :]
        cin_ref[16:17, :] = cc_ref[...]
        need_s[...] = jnp.zeros_like(need_s)
        for j in range(4):
            need_s[8 * j:8 * j + 2, :] = cg_s[2 * j + core, 0:2, :]
            need_s[8 * j + 2:8 * j + 3, :] = cc_ref[...]
        ms_s[...] = (_nn(_silu(need_s[...]), w_ref[...]) + b_ref[...]).reshape(4, 8, 1536)
        g_m.start([ms_s], [landed_s], sm)
        for a, cp in enumerate(loads):
            cp.wait()
            half_refs[a][...] = stage[a][...].astype(BF16)
        cos_ref[...], sin_ref[...] = _rope_tables()
        stores = []
        for pair in range(NPAIR):
            if pair >= 2:
                stores[pair - 2].wait()
            if pair == 2:
                g_w.relay([hw_ref], [gw_ref], sw)
            _na_bias_pair(flat_ref.at[pair], bias_s.at[pair % 2])
            stores.append(pltpu.make_async_copy(bias_s.at[pair % 2], bias_ref.at[pair], bias_sem.at[pair % 2]))
            stores[pair].start()
        for cp in stores[-2:]:
            cp.wait()
        g_w.finish([hw_ref], [gw_ref], sw)
        g_m.finish([ms_s], [landed_s], sm)
        mg_ref[2 * px + py] = ms_s[2 * px + py]
        for k, (qx, qy) in enumerate([(1 - px, py), (px, 1 - py), (1 - px, 1 - py)]):
            mg_ref[2 * qx + qy] = landed_s[k]

    vmem = pl.BlockSpec(memory_space=pltpu.VMEM)
    hbm = pl.BlockSpec(memory_space=pl.ANY)
    return pl.pallas_call(
        body, name="prologue", in_specs=[vmem, vmem, vmem, vmem, vmem, hbm] + [hbm] * n_late,
        out_specs=[vmem, vmem, hbm, hbm, vmem, vmem] + [vmem] * n_late,
        out_shape=[shape((32, D), F32), shape((4, 8, 1536), F32)] + g_w.out_shape
        + [shape((NPAIR,) + NA_BIAS_SHAPE, F32)] + [shape((SEQ, RD), F32)] * 2 + [shape(s, BF16) for s in half_shapes],
        scratch_shapes=[pltpu.VMEM((8, 8, D), F32), pltpu.VMEM((4, 8, 1536), F32), pltpu.VMEM((2,) + NA_BIAS_SHAPE, F32),
                        pltpu.VMEM((32, D), F32), pltpu.VMEM((3, 8, 1536), F32)]
        + [pltpu.VMEM(s, F32) for s in half_shapes]
        + [pltpu.SemaphoreType.DMA((n_late,)), pltpu.SemaphoreType.DMA((2,))]
        + g_w.scratch + g_c.scratch + g_m.scratch,
        compiler_params=_cp(),
    )(c_rows, c_ctx_row, w_ada, b_shard, rpb_flat, half_w_in, *late_shards)


def ada_grads(cin, gb, gc, w_ada):
    def body(c_ref, gb_ref, gc_ref, w_ref, gw_ref, pc_ref):
        ctx_tot = jnp.sum(gc_ref[...], axis=0, keepdims=True)
        rows = lax.broadcasted_iota(jnp.int32, (16, 512), 0)
        dm = jnp.concatenate([gb_ref[...], jnp.where(rows == 0, ctx_tot, 0.0)], axis=0)
        gw_ref[...] = _tn(_silu(c_ref[...]), dm)
        rows8 = lax.broadcasted_iota(jnp.int32, (8, 512), 0)
        part = _nt(jnp.where(rows8 == 0, ctx_tot, 0.0), w_ref[...])

        @pl.when(pl.program_id(0) == 0)
        def _():
            pc_ref[...] = jnp.zeros_like(pc_ref)

        pc_ref[...] += part

    return pl.pallas_call(
        body, name="ada_grads", grid=(3,),
        in_specs=[pl.BlockSpec((32, D), lambda j: (0, 0)), pl.BlockSpec((16, 512), lambda j: (0, j)),
                  pl.BlockSpec((8, 512), lambda j: (0, j)), pl.BlockSpec((D, 512), lambda j: (0, j))],
        out_specs=[pl.BlockSpec((D, 512), lambda j: (0, j)), pl.BlockSpec((8, D), lambda j: (0, 0))],
        out_shape=[jax.ShapeDtypeStruct((D, 1536), F32), jax.ShapeDtypeStruct((8, D), F32)],
    )(cin, gb, gc, w_ada)


SMALL_SUM_ROWS = 15


def small_update(gsm, gbf, gcf, pcg, params):
    n = len(params)

    def body(*refs):
        gsm_ref, gbf_ref, gcf_ref, pcg_ref = refs[:4]
        wmv, outs, loss_out = refs[4:4 + 3 * n], refs[4 + 3 * n:4 + 7 * n], refs[-1]
        acc = gsm_ref[0]
        for dev in range(1, 8):
            acc = acc + gsm_ref[dev]
        c_ctx = wmv[0][...]
        sg = jax.nn.sigmoid(c_ctx)
        dsilu = pcg_ref[0:1, :] + pcg_ref[2:3, :] + pcg_ref[4:5, :] + pcg_ref[6:7, :]
        lane = lax.broadcasted_iota(jnp.int32, (1, D), 1)
        last = acc[14:15, :]
        grads = [
            dsilu * (sg * (1.0 + c_ctx * (1.0 - sg))),
            jnp.sum(gbf_ref[...], axis=0, keepdims=True) + jnp.sum(gcf_ref[...], axis=0, keepdims=True),
            acc[0:1, :] + acc[1:2, :], acc[2:3, :], acc[3:4, :], acc[4:5, :],
            acc[5:6, 0:512], acc[6:14, :], jnp.where(lane < 8, last, 0.0),
        ]
        loss_out[...] = jnp.broadcast_to(jnp.sum(jnp.where(lane == 8, last, 0.0), axis=1, keepdims=True), (8, 128))
        for i, g in enumerate(grads):
            d, m2, v2 = _adamw_math(wmv[3 * i][...], g, wmv[3 * i + 1][...], wmv[3 * i + 2][...])
            outs[4 * i][...] = g
            outs[4 * i + 1][...] = d
            outs[4 * i + 2][...] = m2
            outs[4 * i + 3][...] = v2

    flat = [a for wmv in params for a in wmv]
    out_shape = [jax.ShapeDtypeStruct(w.shape, F32) for w, _, _ in params for _ in range(4)]
    return pl.pallas_call(
        body, name="small_update", out_shape=out_shape + [jax.ShapeDtypeStruct((8, 128), F32)],
    )(gsm, gbf, gcf, pcg, *flat)


def _pad_row(v, rows):
    flat = v.reshape(-1)
    return jnp.pad(flat, (0, rows * D - flat.shape[0])).reshape(rows, D)


def local_step(x, ctx, tgt, mod3, rope, bias, g_pre_mix, g_post_mix, g_pre_mlp, g_post_mlp, ret_decay, ret_gn,
               wperm, late_weights, early_grads):
    nb = x.shape[0]
    tokens = nb * SEQ
    cos, sin = rope
    rd = ret_decay.T.reshape(RH, 2, 1)
    gn = ret_gn.reshape(RH, 1, RD)
    h, pret, pna = premix_proj(x, mod3, g_pre_mix, wperm, False, "premix_proj")
    hc, pretc, pnac = premix_proj(ctx, mod3, g_pre_mix, wperm, True, "premix_proj_ctx")
    o_all, mixin, gw_out = retention_fwd(pret, pretc, rd, gn, cos, sin, late_weights(0))
    mixin, gw1, gw2 = na_fwd(pna, pnac, bias, mixin, late_weights(1))
    dx_tail, dmix, h2, du, act, dm, dmixin, dmod_t, dg_t, loss_t = tail_fwd_bwd(
        x, mixin, tgt, mod3, g_post_mix, g_pre_mlp, g_post_mlp, gw_out.reshape(D, D), gw1.reshape(4, D, D),
        gw2.reshape(DFF, D))
    dw_out = weight_grad([(mixin.reshape(tokens, D), dmix.reshape(tokens, D))], "grad_w_out", BF16)
    dw1 = weight_grad([(h2.reshape(tokens, D), du.reshape(tokens, DFF))], "grad_w_mlp1", BF16, col_blocks=True)
    dw2 = weight_grad([(act.reshape(tokens, DFF), dm.reshape(tokens, D))], "grad_w_mlp2", BF16)
    dproj, dprojc, drd, dgn, *landed = retention_bwd(pret, pretc, o_all, dmixin, rd, gn, cos, sin,
                                                     early_grads[0](dw_out, dw1, dw2))
    dproj, dprojc, dpat, *early = na_bwd(pna, pnac, bias, dmixin, dproj, dprojc, early_grads[1](landed))
    dw_in = weight_grad([(h.reshape(tokens, D), dproj.reshape(tokens, IN_W)),
                         (hc.reshape(nb * LC, D), dprojc.reshape(nb * LC, IN_W))], "grad_w_in", tn=IN_W // 2, tk=1024)
    dmod_c, dg_c, *late = premix_bwd(ctx, mod3, g_pre_mix, wperm, dprojc, None, early_grads[2](dw_in), "premix_bwd_ctx")
    grad_x, dmod_a, dg_a, *late = premix_bwd(x, mod3, g_pre_mix, wperm, dproj, dx_tail, early_grads[3](late),
                                             "premix_bwd")
    dmod = jnp.concatenate([jnp.concatenate([dmod_a[:, 0:2], dmod_t[:, 2:6]], axis=1), dmod_c], axis=0)
    last = jnp.pad(jnp.concatenate([drd[:, :, 0].T.reshape(8), loss_t[0, 0:1]]), (0, D - 9)).reshape(1, D)
    small = jnp.concatenate([dg_a[0:1], dg_c[0:1], dg_t[0:3], _pad_row(dgn, 1), dpat.reshape(8, D), last], axis=0)
    return grad_x, late, early, dmod, small


def kernel(x, c, ctx, c_ctx, w_ada, b_ada, g_pre_mix, g_post_mix, g_pre_mlp, g_post_mlp, w_in, ret_decay, ret_gn, na_rpb, w_out, w_mlp1, w_mlp2, loss_target, m_c_ctx, m_w_ada, m_b_ada, m_g_pre_mix, m_g_post_mix, m_g_pre_mlp, m_g_post_mlp, m_w_in, m_ret_decay, m_ret_gn, m_na_rpb, m_w_out, m_w_mlp1, m_w_mlp2, v_c_ctx, v_w_ada, v_b_ada, v_g_pre_mix, v_g_post_mix, v_g_pre_mlp, v_g_post_mlp, v_w_in, v_ret_decay, v_ret_gn, v_na_rpb, v_w_out, v_w_mlp1, v_w_mlp2):
    px, py, pc = _place()
    chip = 2 * px + py

    half_w_in = lax.dynamic_slice_in_dim(w_in[0], pc * (D // 2), D // 2, 0).astype(BF16)
    cin, mg, gw_in, bias, cos, sin, *late_halves = prologue(
        jnp.pad(c, ((0, 6), (0, 0))), c_ctx[None], w_ada[0], lax.dynamic_slice_in_dim(b_ada, chip * 1536, 1536, 1),
        _rpb_flat(na_rpb[0]), half_w_in, [w_out[0], w_mlp1[0], w_mlp2[0]])
    halves = [half_w_in] + late_halves
    wperm = unpack_w_in(gw_in.reshape(4, D, 896))
    mod3 = mg[:, 0:3].transpose(1, 0, 2).reshape(3, 6, D)

    place = jnp.stack([pc, chip]).astype(jnp.int32)

    early_names = ["w_out", "w_mlp1", "w_mlp2"]
    early_g8, early_partial = [], []

    def early_a(dw_out, dw1, dw2):
        early_g8[:] = [dw_out.reshape(8, 128, D), dw1.reshape(8, 512, D), dw2.reshape(8, 512, D)]
        return siblings4(early_g8)

    def early_b(landed):
        early_partial[:] = chip_partial(place, early_g8, landed, "rs_chip_sum_early")
        return chips3(early_partial)

    late_partial = []

    late_g8 = []

    def late_c(dw_in):
        late_g8[:] = [pack_w_in(dw_in).reshape(8, 512, 896)]
        return siblings4(late_g8)

    def late_d(landed):
        late_partial[:] = chip_partial(place, late_g8, landed, "rs_chip_sum_w_in")
        return chips3(late_partial)

    grad_x, (landed3_in,), early_landed, dmod, small = local_step(
        x, ctx, loss_target, mod3, (cos, sin), bias, g_pre_mix, g_post_mix, g_pre_mlp, g_post_mlp, ret_decay[0], ret_gn,
        wperm, lambda k: gather8(halves[1:2] if k == 0 else halves[2:4]), (early_a, early_b, late_c, late_d))
    early_mine = shard_sum(place, early_partial, early_landed, "rs_shard_sum_early")

    pay = jnp.concatenate([dmod.reshape(18, D), small, jnp.zeros((40 - 18 - SMALL_SUM_ROWS, D), F32)], axis=0)
    *early_theirs, gs = run_hosted(both(siblings(early_mine), gather8([pay])), "rs_halves_early_gather_small")
    gbf = gs[:, 0:12].reshape(16, 6 * D)
    gcf = gs[:, 12:18].reshape(8, 6 * D)
    gw_ada, pc_part = ada_grads(cin, lax.dynamic_slice_in_dim(gbf, chip * 1536, 1536, 1),
                                lax.dynamic_slice_in_dim(gcf, chip * 1536, 1536, 1), w_ada[0])
    (mine_in,) = shard_sum(place, late_partial, [landed3_in], "rs_shard_sum_w_in")
    theirs_in, pcg = run_hosted(both(siblings([mine_in]), gather8([pc_part])), "rs_halves_w_in_gather_c_ctx")

    grouped = adamw_group(
        place,
        [(w_mlp1[0], early_mine[1], early_theirs[1], m_w_mlp1[0], v_w_mlp1[0]),
         (w_mlp2[0], early_mine[2], early_theirs[2], m_w_mlp2[0], v_w_mlp2[0])],
        [(w_ada[0], gw_ada, m_w_ada[0], v_w_ada[0])], no_exchange(), "adamw_group")
    d_ada, m_ada, v_ada = grouped[8:11]
    big = [
        [r[None] for r in adamw_halves(place, w_in[0], mine_in, theirs_in, m_w_in[0], v_w_in[0], "adamw_w_in")],
        [r[None] for r in adamw_halves(place, w_out[0], early_mine[0], early_theirs[0], m_w_out[0], v_w_out[0],
                                       "adamw_w_out")],
        [r[None] for r in grouped[0:4]], [r[None] for r in grouped[4:8]],
    ]

    def rpb_rows(t):
        return _rpb_flat(t[0]).reshape(8, D)

    def decay_row(t):
        return jnp.pad(t.reshape(1, 8), ((0, 0), (0, D - 8)))

    views = [lambda t: t.reshape(1, D), lambda t: t, lambda t: t, lambda t: t, lambda t: t, lambda t: t, lambda t: t,
             rpb_rows, decay_row]
    back = [lambda t: t.reshape(D), lambda t: t, lambda t: t, lambda t: t, lambda t: t, lambda t: t, lambda t: t,
            lambda t: _rpb_flat_t(t)[None], lambda t: t[:, 0:8].reshape(1, 2, 4)]
    small_w = (c_ctx, b_ada, g_pre_mix, g_post_mix, g_pre_mlp, g_post_mlp, ret_gn, na_rpb, ret_decay)
    small_m = (m_c_ctx, m_b_ada, m_g_pre_mix, m_g_post_mix, m_g_pre_mlp, m_g_post_mlp, m_ret_gn, m_na_rpb, m_ret_decay)
    small_v = (v_c_ctx, v_b_ada, v_g_pre_mix, v_g_post_mix, v_g_pre_mlp, v_g_post_mlp, v_ret_gn, v_na_rpb, v_ret_decay)
    *res, loss8 = small_update(gs[:, 18:18 + SMALL_SUM_ROWS], gbf, gcf, pcg[:, 0],
                               [(f(w), f(m), f(v)) for f, w, m, v in zip(views, small_w, small_m, small_v)])

    def leaves(ada, idx):
        s_c, s_b, s_g1, s_g2, s_g3, s_g4, s_gn, s_rpb, s_rd = [back[i](res[4 * i + idx]) for i in range(9)]
        return [s_c, ada[None], s_b, s_g1, s_g2, s_g3, s_g4, big[0][idx], s_rd, s_gn, s_rpb,
                big[1][idx], big[2][idx], big[3][idx]]

    return (loss8[0, 0], grad_x, *leaves(gw_ada, 0), *leaves(d_ada, 1), *leaves(m_ada, 2), *leaves(v_ada, 3))
```

```python
import functools
import math

import jax
import jax.numpy as jnp
from jax import lax
from jax.experimental import pallas as pl
from jax.experimental.pallas import tpu as pltpu

F32, BF16 = jnp.float32, jnp.bfloat16
D = 1024
SEQ = 2048
LC = 256
GW = 64
RH, RD, CH = 4, 128, 128
NPAIR = 4
IN_W = 3584
RET_W = 2048
DFF = 4096
EPS = 1e-6
NEG = -1e30
TN = 256
NCH = SEQ // CH
LR, B1, B2, AEPS, WD, STEP = 0.001, 0.9, 0.999, 1e-08, 0.01, 10
MESH = pl.DeviceIdType.MESH
VMEM_LIMIT = 56 * 1024 * 1024


def _cp(sem=None):
    return pltpu.CompilerParams(dimension_semantics=sem, vmem_limit_bytes=VMEM_LIMIT)


def _nn(a, b):
    return jnp.dot(a.astype(BF16), b.astype(BF16), preferred_element_type=F32)


def _nt(a, b):
    return lax.dot_general(a.astype(BF16), b.astype(BF16), (((1,), (1,)), ((), ())), preferred_element_type=F32)


def _tn(a, b):
    return lax.dot_general(a.astype(BF16), b.astype(BF16), (((0,), (0,)), ((), ())), preferred_element_type=F32)


@jax.custom_vjp
def mm_tn(a, b):
    return _tn(a, b)


mm_tn.defvjp(lambda a, b: (_tn(a, b), (a, b)), lambda r, g: (_nt(r[1], g), _nn(r[0], g)))


def _rms(x):
    return x * lax.rsqrt(jnp.mean(x * x, axis=-1, keepdims=True) + EPS)


def _rms_mod(x, g, sc, sh):
    return (_rms(x) * g) * (1.0 + sc) + sh


def _post_mix(x, mix, gt1, sc2, sh2, g_post_mix, g_pre_mlp):
    x1 = x + gt1 * (_rms(mix) * g_post_mix)
    return x1, _rms_mod(x1, g_pre_mlp, sc2, sh2)


def _head_loss(x1, m, gt2, g_post_mlp, tgt):
    err = x1 + gt2 * (_rms(m) * g_post_mlp) - tgt
    return 0.5 * jnp.sum(jnp.mean(err * err, axis=-1, keepdims=True), axis=0, keepdims=True)


def _ln_gate(o, g, w):
    mu = jnp.mean(o, axis=-1, keepdims=True)
    var = jnp.mean(jnp.square(o - mu), axis=-1, keepdims=True)
    y = (o - mu) * lax.rsqrt(var + EPS)
    return (y * w) * (g * jax.nn.sigmoid(g))


def _pair_order(x):
    lane = lax.broadcasted_iota(jnp.int32, x.shape, 1)
    return jnp.where((lane >= 32) & (lane < 64), pltpu.roll(x, 96, 1),
                     jnp.where((lane >= 64) & (lane < 96), pltpu.roll(x, 32, 1), x))


def _rope(x, cos, sin):
    return x * cos + pltpu.roll(x, 64, 1) * sin


def _rope_t(g, cos, sin):
    return g * cos + pltpu.roll(g * sin, 64, 1)


def _rope_tables():
    tok = lax.broadcasted_iota(jnp.int32, (SEQ, RD), 0)
    lane = lax.broadcasted_iota(jnp.int32, (SEQ, RD), 1)
    pos = jnp.where((lane & 32) == 0, tok >> 6, tok & (GW - 1)).astype(F32)
    ang = pos * jnp.exp((lane & 31).astype(F32) * (-math.log(10000.0) / 32))
    return jnp.cos(ang), jnp.where(lane < 64, -jnp.sin(ang), jnp.sin(ang))


def _chunk_loop(n, body, init, k=4):
    def several(t, carry):
        for i in range(k):
            carry = body(k * t + i, carry)
        return carry

    return lax.fori_loop(0, n // k, several, init)


def _fiota(shape, dim):
    return lax.broadcasted_iota(jnp.int32, shape, dim).astype(F32)


def _ret_state(k, v, s, lg, reverse):
    pos = _fiota((CH, 1), 0)
    b_exp = pos if reverse else (CH - 1.0 - pos)
    return jnp.exp(lg * CH) * s + mm_tn(k * jnp.exp(lg * b_exp), v)


class _Decays:
    def __init__(self, lgs):
        i, j, pos = _fiota((CH, CH), 0), _fiota((CH, CH), 1), _fiota((CH, 1), 0)
        diffs = (i - j, j - i)
        keep = (diffs[0] >= 0, diffs[1] > 0)
        mats = [jnp.where(m, jnp.exp(lg * jnp.where(m, d, 0.0)), 0.0) for lg, d, m in zip(lgs, diffs, keep)]
        self.mask = mats[0] + mats[1]
        self.dmask = [mats[0] * diffs[0], mats[1] * diffs[1]]
        a_exp, b_exp = (pos + 1.0, CH - pos), (CH - 1.0 - pos, pos)
        self.a = [jnp.exp(lg * e) for lg, e in zip(lgs, a_exp)]
        self.b = [jnp.exp(lg * e) for lg, e in zip(lgs, b_exp)]
        self.da = [a * e for a, e in zip(self.a, a_exp)]
        self.db = [b * e for b, e in zip(self.b, b_exp)]
        self.g = [jnp.exp(lg * CH) for lg in lgs]


def _both(x, w):
    return jnp.concatenate([x * w[0], x * w[1]], axis=1)


def _total(x):
    return jnp.sum(jnp.sum(x, axis=1, keepdims=True), axis=0, keepdims=True)


def _state_pass(dec, init, k_s, v_of, st_s):
    def step(t, carry):
        out = []
        for d, s in enumerate(carry):
            n = (NCH - 1 - t) if d else t
            sl = pl.ds(pl.multiple_of(n * CH, CH), CH)
            st_s[n, d * RD:(d + 1) * RD, :] = s
            out.append(dec.g[d] * s + _tn(k_s[sl, :] * dec.b[d], v_of(sl)))
        return tuple(out)

    _chunk_loop(NCH, step, tuple(init))


def premix_proj(xin, mod3, g_pre, wperm, is_ctx, name):
    nb, length, _ = xin.shape
    tn = min(2 * TN, length)

    def body(x_ref, mod_ref, g_ref, w_ref, h_ref, pret_ref, pna_ref):
        h = _rms_mod(x_ref[...], g_ref[...], mod_ref[1:2, :], mod_ref[0:1, :])
        hb = h.astype(BF16)
        h_ref[...] = hb
        pret_ref[...] = jnp.dot(hb, w_ref[:, :RET_W], preferred_element_type=F32)
        pna_ref[...] = jnp.dot(hb, w_ref[:, RET_W:], preferred_element_type=F32).astype(BF16)

    return pl.pallas_call(
        body, name=name, grid=(nb, length // tn),
        in_specs=[
            pl.BlockSpec((None, tn, D), lambda b, t: (b, t, 0)),
            pl.BlockSpec((None, 6, D), (lambda b, t: (2, 0, 0)) if is_ctx else (lambda b, t: (b, 0, 0))),
            pl.BlockSpec((1, D), lambda b, t: (0, 0)),
            pl.BlockSpec((D, IN_W), lambda b, t: (0, 0), pipeline_mode=pl.Buffered(1)),
        ],
        out_specs=[
            pl.BlockSpec((None, tn, D), lambda b, t: (b, t, 0)),
            pl.BlockSpec((None, tn, RET_W), lambda b, t: (b, t, 0)),
            pl.BlockSpec((None, tn, IN_W - RET_W), lambda b, t: (b, t, 0)),
        ],
        out_shape=[
            jax.ShapeDtypeStruct((nb, length, D), BF16),
            jax.ShapeDtypeStruct((nb, length, RET_W), F32),
            jax.ShapeDtypeStruct((nb, length, IN_W - RET_W), BF16),
        ],
        compiler_params=_cp(("arbitrary", "arbitrary")),
    )(xin, mod3, g_pre, wperm)


def premix_bwd(xin, mod3, g_pre, wperm, dproj, dx_tail, hosted, name):
    nb, length, _ = xin.shape
    tn = min(TN, length)
    is_ctx = dx_tail is None

    def body(*refs):
        own_in, h_in, own_out, h_out, _, h_sems = hosted.split(refs, 5 if is_ctx else 6, 2 if is_ctx else 3)
        if is_ctx:
            (x_ref, mod_ref, g_ref, w_ref, dp_ref), (dmod_ref, dg_ref) = own_in, own_out
        else:
            (x_ref, mod_ref, g_ref, w_ref, dp_ref, dxt_ref), (dx_ref, dmod_ref, dg_ref) = own_in, own_out
        b, t = pl.program_id(0), pl.program_id(1)
        grid_step = b * (length // tn) + t

        @pl.when(grid_step == 0)
        def _():
            hosted.start(h_in, h_out, h_sems)

        @pl.when(grid_step == nb * (length // tn) - 1)
        def _():
            hosted.finish(h_in, h_out, h_sems)

        dh = lax.dot_general(dp_ref[...], w_ref[...], (((1,), (1,)), ((), ())), preferred_element_type=F32)
        _, vjp = jax.vjp(_rms_mod, x_ref[...], g_ref[...], mod_ref[1:2, :], mod_ref[0:1, :])
        dx, dg, dsc, dsh = vjp(dh)
        if not is_ctx:
            dx_ref[...] = dx + dxt_ref[...]

        @pl.when((t == 0) & ((b == 0) if is_ctx else True))
        def _():
            dmod_ref[...] = jnp.zeros_like(dmod_ref)

        @pl.when((t == 0) & (b == 0))
        def _():
            dg_ref[...] = jnp.zeros_like(dg_ref)

        dmod_ref[0:1, :] += dsh
        dmod_ref[1:2, :] += dsc
        dg_ref[0:1, :] += dg

    tok = lambda b, t: (b, t, 0)
    in_specs = [
        pl.BlockSpec((None, tn, D), tok),
        pl.BlockSpec((None, 6, D), (lambda b, t: (2, 0, 0)) if is_ctx else (lambda b, t: (b, 0, 0))),
        pl.BlockSpec((1, D), lambda b, t: (0, 0)),
        pl.BlockSpec((D, IN_W), lambda b, t: (0, 0), pipeline_mode=pl.Buffered(1)),
        pl.BlockSpec((None, tn, IN_W), tok),
    ]
    args = [xin, mod3, g_pre, wperm, dproj]
    out_specs = [
        pl.BlockSpec((None, 6, D), (lambda b, t: (0, 0, 0)) if is_ctx else (lambda b, t: (b, 0, 0))),
        pl.BlockSpec((8, D), lambda b, t: (0, 0)),
    ]
    out_shape = [jax.ShapeDtypeStruct((1 if is_ctx else nb, 6, D), F32), jax.ShapeDtypeStruct((8, D), F32)]
    if not is_ctx:
        in_specs.append(pl.BlockSpec((None, tn, D), tok))
        args.append(dx_tail)
        out_specs.insert(0, pl.BlockSpec((None, tn, D), tok))
        out_shape.insert(0, jax.ShapeDtypeStruct((nb, length, D), F32))
    h_in_specs, h_out_specs = hosted.specs()
    return pl.pallas_call(
        body, name=name, grid=(nb, length // tn), in_specs=in_specs + h_in_specs, out_specs=out_specs + h_out_specs,
        out_shape=out_shape + hosted.out_shape, scratch_shapes=hosted.scratch,
        compiler_params=_cp(("arbitrary", "arbitrary")),
    )(*args, *hosted.args)


def _ret_specs(order):
    def im(f):
        return lambda *g: f(*order(*g))
    return dict(
        pret=pl.BlockSpec((None, SEQ, 512), im(lambda b, h: (b, 0, h))),
        pretc=pl.BlockSpec((None, LC, 512), im(lambda b, h: (b, 0, h))),
        rd=pl.BlockSpec((None, 2, 1), im(lambda b, h: (h, 0, 0))),
        gn=pl.BlockSpec((None, 1, RD), im(lambda b, h: (h, 0, 0))),
        tab=pl.BlockSpec((SEQ, RD), im(lambda b, h: (0, 0))),
        head=pl.BlockSpec((None, SEQ, RD), im(lambda b, h: (b, 0, h))),
    )


def retention_fwd(pret, pretc, rd, gn, cos, sin, hosted):
    nb = pret.shape[0]
    sp = _ret_specs(lambda b, h: (b, h))

    def body(*refs):
        own_in, h_in, own_out, h_out, own_scr, h_sems = hosted.split(refs, 6, 2)
        p_ref, pc_ref, rd_ref, gn_ref, cos_ref, sin_ref = own_in
        (o_ref, mix_ref), (q_s, k_s, o_s, st_s) = own_out, own_scr
        grid_step = pl.program_id(0) * RH + pl.program_id(1)

        @pl.when(grid_step == 0)
        def _():
            hosted.start(h_in, h_out, h_sems)

        cos_v, sin_v = cos_ref[...], sin_ref[...]
        q_s[...] = _rope(p_ref[:, 0:128], cos_v, sin_v) * (RD ** -0.5)
        k_s[...] = _rope(p_ref[:, 128:256], cos_v, sin_v)
        lgs, init = [], []
        for rev in (False, True):
            lg = jax.nn.log_sigmoid(rd_ref[int(rev):int(rev) + 1, :])
            s = jnp.zeros((RD, RD), F32)
            for n in ((1, 0) if rev else (0, 1)):
                s = _ret_state(pc_ref[n * CH:(n + 1) * CH, 128:256], pc_ref[n * CH:(n + 1) * CH, 256:384], s, lg, rev)
            lgs.append(lg)
            init.append(s)

        dec = _Decays(lgs)
        _state_pass(dec, init, k_s, lambda sl: p_ref[sl, 256:384], st_s)

        def chunk(n, carry):
            sl = pl.ds(pl.multiple_of(n * CH, CH), CH)
            q = q_s[sl, :]
            o_s[sl, :] = (_nn(_nt(q, k_s[sl, :]) * dec.mask, p_ref[sl, 256:384]) + _nn(_both(q, dec.a), st_s[n]))
            return carry

        _chunk_loop(NCH, chunk, 0)
        o = o_s[...]
        o_ref[...] = o
        mix_ref[...] = _ln_gate(o, p_ref[:, 384:512], gn_ref[...]).astype(BF16)

        @pl.when(grid_step == nb * RH - 1)
        def _():
            hosted.finish(h_in, h_out, h_sems)

    h_in_specs, h_out_specs = hosted.specs()
    return pl.pallas_call(
        body, name="retention_fwd", grid=(nb, RH),
        in_specs=[sp["pret"], sp["pretc"], sp["rd"], sp["gn"], sp["tab"], sp["tab"]] + h_in_specs,
        out_specs=[sp["head"], sp["head"]] + h_out_specs,
        out_shape=[jax.ShapeDtypeStruct((nb, SEQ, RH * RD), F32), jax.ShapeDtypeStruct((nb, SEQ, D), BF16)]
        + hosted.out_shape,
        scratch_shapes=[pltpu.VMEM((SEQ, RD), F32)] * 3 + [pltpu.VMEM((NCH, 2 * RD, RD), F32)] + hosted.scratch,
        compiler_params=_cp(("arbitrary", "arbitrary")),
    )(pret, pretc, rd, gn, cos, sin, *hosted.args)


def retention_bwd(pret, pretc, o_all, dmixin, rd, gn, cos, sin, hosted):
    nb = pret.shape[0]
    sp = _ret_specs(lambda h, b: (b, h))

    def body(*refs):
        own_in, h_in, own_out, h_out, own_scr, h_sems = hosted.split(refs, 8, 4)
        p_ref, pc_ref, o_ref, dmix_ref, rd_ref, gn_ref, cos_ref, sin_ref = own_in
        dp_ref, dpc_ref, drd_ref, dgn_ref = own_out
        q_s, k_s, do_s, dq_s, dk_s, dv_s, st_s, gst_s = own_scr
        b = pl.program_id(1)
        grid_step = pl.program_id(0) * nb + b

        @pl.when(grid_step == 0)
        def _():
            hosted.start(h_in, h_out, h_sems)

        cos_v, sin_v = cos_ref[...], sin_ref[...]
        q_s[...] = _rope(p_ref[:, 0:128], cos_v, sin_v) * (RD ** -0.5)
        k_s[...] = _rope(p_ref[:, 128:256], cos_v, sin_v)
        _, gate_vjp = jax.vjp(_ln_gate, o_ref[...], p_ref[:, 384:512], gn_ref[...])
        do, dg, dgn = gate_vjp(dmix_ref[...].astype(F32))
        do_s[...] = do
        dp_ref[:, 384:512] = dg.astype(BF16)

        @pl.when(b == 0)
        def _():
            drd_ref[...] = jnp.zeros_like(drd_ref)
            dgn_ref[...] = jnp.zeros_like(dgn_ref)

        dgn_ref[...] += dgn
        kcs = [pc_ref[n * CH:(n + 1) * CH, 128:256] for n in (0, 1)]
        vcs = [pc_ref[n * CH:(n + 1) * CH, 256:384] for n in (0, 1)]
        dirs = []
        init = []
        for rev in (False, True):
            rdv = rd_ref[int(rev):int(rev) + 1, :]
            lg = jax.nn.log_sigmoid(rdv)
            order_c = (1, 0) if rev else (0, 1)
            s = jnp.zeros((RD, RD), F32)
            ctx_states = []
            for n in order_c:
                ctx_states.append(s)
                s = _ret_state(kcs[n], vcs[n], s, lg, rev)
            dirs.append((rev, order_c, lg, rdv, ctx_states))
            init.append(s)
        dec = _Decays([lg for _, _, lg, _, _ in dirs])

        def v_of(sl):
            return p_ref[sl, 256:384]

        _state_pass(dec, init, k_s, v_of, st_s)
        zeros = jnp.zeros((CH, RD), F32)

        def scores_back(n, carry):
            dmask_sum, da_f, da_b = carry
            sl = pl.ds(pl.multiple_of(n * CH, CH), CH)
            q, k, v, do = q_s[sl, :], k_s[sl, :], v_of(sl), do_s[sl, :]
            scores = _nt(q, k)
            d_att = _nt(do, v)
            d_scores = d_att * dec.mask
            d_qa = _nt(do, st_s[n])
            d_qf, d_qb = d_qa[:, 0:RD], d_qa[:, RD:2 * RD]
            dq_s[sl, :] = _nn(d_scores, k) + d_qf * dec.a[0] + d_qb * dec.a[1]
            dk_s[sl, :] = _tn(d_scores, q)
            dv_s[sl, :] = _tn(scores * dec.mask, do)
            gst_s[n] = _tn(_both(q, dec.a), do)
            return dmask_sum + d_att * scores, da_f + d_qf * q, da_b + d_qb * q

        dmask_sum, da_f, da_b = _chunk_loop(NCH, scores_back, (zeros, zeros, zeros))

        def state_back(t, carry):
            out = []
            for d, r in enumerate(carry):
                n = t if d else (NCH - 1 - t)
                rows = slice(d * RD, (d + 1) * RD)
                own = gst_s[n, rows, :]
                gst_s[n, rows, :] = r
                out.append(own + dec.g[d] * r)
            return tuple(out)

        d_states = _chunk_loop(NCH, state_back, (zeros, zeros))

        def updates_back(n, carry):
            db_f, db_b, dg_f, dg_b = carry
            sl = pl.ds(pl.multiple_of(n * CH, CH), CH)
            k, r, s = k_s[sl, :], gst_s[n], st_s[n]
            d_kw = _nt(v_of(sl), r)
            d_kf, d_kb = d_kw[:, 0:RD], d_kw[:, RD:2 * RD]
            dk_s[sl, :] += d_kf * dec.b[0] + d_kb * dec.b[1]
            dv_s[sl, :] += _nn(_both(k, dec.b), r)
            return (db_f + d_kf * k, db_b + d_kb * k, dg_f + r[0:RD, :] * s[0:RD, :],
                    dg_b + r[RD:2 * RD, :] * s[RD:2 * RD, :])

        db_dg = _chunk_loop(NCH, updates_back, (zeros, zeros, zeros, zeros))
        dkc = [None, None]
        dvc = [None, None]
        for d, ((rev, order_c, lg, rdv, ctx_states), ds) in enumerate(zip(dirs, d_states)):
            dlg = (_total(dmask_sum * dec.dmask[d]) + _total((da_f, da_b)[d] * dec.da[d])
                   + _total(db_dg[d] * dec.db[d]) + CH * dec.g[d] * _total(db_dg[2 + d]))
            for idx in (1, 0):
                n = order_c[idx]
                _, vjp = jax.vjp(functools.partial(_ret_state, reverse=rev), kcs[n], vcs[n], ctx_states[idx], lg)
                dk_c, dv_c, ds, dl = vjp(ds)
                dlg = dlg + dl
                dkc[n] = dk_c if dkc[n] is None else dkc[n] + dk_c
                dvc[n] = dv_c if dvc[n] is None else dvc[n] + dv_c
            drd_ref[int(rev):int(rev) + 1, :] += dlg * jax.nn.sigmoid(-rdv)
        dp_ref[:, 0:128] = _rope_t(dq_s[...] * (RD ** -0.5), cos_v, sin_v).astype(BF16)
        dp_ref[:, 128:256] = _rope_t(dk_s[...], cos_v, sin_v).astype(BF16)
        dp_ref[:, 256:384] = dv_s[...].astype(BF16)
        zero = jnp.zeros((CH, RD), BF16)
        for n in (0, 1):
            rows = slice(n * CH, (n + 1) * CH)
            dpc_ref[rows, 0:128] = zero
            dpc_ref[rows, 128:256] = dkc[n].astype(BF16)
            dpc_ref[rows, 256:384] = dvc[n].astype(BF16)
            dpc_ref[rows, 384:512] = zero

        @pl.when(grid_step == RH * nb - 1)
        def _():
            hosted.finish(h_in, h_out, h_sems)

    h_in_specs, h_out_specs = hosted.specs()
    return pl.pallas_call(
        body, name="retention_bwd", grid=(RH, nb),
        in_specs=[sp["pret"], sp["pretc"], sp["head"], sp["head"], sp["rd"], sp["gn"], sp["tab"], sp["tab"]]
        + h_in_specs,
        out_specs=[
            pl.BlockSpec((None, SEQ, 512), lambda h, b: (b, 0, h)),
            pl.BlockSpec((None, LC, 512), lambda h, b: (b, 0, h)),
            pl.BlockSpec((None, 2, 1), lambda h, b: (h, 0, 0)),
            pl.BlockSpec((None, 1, RD), lambda h, b: (h, 0, 0)),
        ] + h_out_specs,
        out_shape=[
            jax.ShapeDtypeStruct((nb, SEQ, IN_W), BF16),
            jax.ShapeDtypeStruct((nb, LC, IN_W), BF16),
            jax.ShapeDtypeStruct((RH, 2, 1), F32),
            jax.ShapeDtypeStruct((RH, 1, RD), F32),
        ] + hosted.out_shape,
        scratch_shapes=[pltpu.VMEM((SEQ, RD), F32)] * 6 + [pltpu.VMEM((NCH, 2 * RD, RD), F32)] * 2 + hosted.scratch,
        compiler_params=_cp(("arbitrary", "arbitrary")),
    )(pret, pretc, o_all, dmixin, rd, gn, cos, sin, *hosted.args)


def _rpb_flat(rpb):
    return jnp.pad(rpb, ((0, 0), (0, 1), (0, 33))).reshape(NPAIR, 2, 1, 1024)


def _rpb_flat_t(dflat):
    return dflat.reshape(8, 16, 64)[:, :15, :31]


def _barrel(x, left):
    row = lax.broadcasted_iota(jnp.int32, x.shape, 0)
    n = x.shape[1]
    for bit in range(6):
        s = 1 << bit
        x = jnp.where(((row >> bit) & 1) == 1, pltpu.roll(x, (n - s) if left else s, 1), x)
    return x


NA_TILE_ROWS, NA_BAND_ROWS = 4, 12
NA_Q, NA_K = NA_TILE_ROWS * GW, NA_BAND_ROWS * GW
NA_TILES = SEQ // NA_Q


def _band_start(r0):
    return min(max(r0 - 4, 0), 32 - NA_BAND_ROWS)


def _tile_layout(t):
    rows = range(t * NA_TILE_ROWS, (t + 1) * NA_TILE_ROWS)
    return tuple((r if r < 4 else (r - 24 if r > 28 else 4), min(max(r - 4, 0), 24) - _band_start(rows[0]))
                 for r in rows)


NA_CLASSES = sorted(set(_tile_layout(t) for t in range(NA_TILES)))


def _tile_rows(cls):
    return NA_CLASSES[cls]


def _na_tile(t):
    start = jnp.clip(NA_TILE_ROWS * t - 4, 0, 32 - NA_BAND_ROWS)
    cls = 0
    for tile in range(NA_TILES):
        cls = jnp.where(t == tile, NA_CLASSES.index(_tile_layout(tile)), cls)
    return pl.ds(pl.multiple_of(t * NA_Q, NA_Q), NA_Q), pl.ds(pl.multiple_of(start * GW, NA_Q), NA_K), cls


def _na_probs(qst, kb, kc, bias):
    s_loc = _nt(qst, kb) + bias
    s_ctx = _nt(qst, kc)
    m = jnp.maximum(jnp.max(s_loc, axis=1, keepdims=True), jnp.max(s_ctx, axis=1, keepdims=True))
    e_loc, e_ctx = jnp.exp(s_loc - m), jnp.exp(s_ctx - m)
    den = jnp.sum(e_loc, axis=1, keepdims=True) + jnp.sum(e_ctx, axis=1, keepdims=True)
    return e_loc / den, e_ctx / den


def _stack_heads(t):
    lane = lax.broadcasted_iota(jnp.int32, t.shape, 1)
    zero = jnp.zeros_like(t)
    return jnp.concatenate([jnp.where(lane < 64, t, zero), jnp.where(lane >= 64, t, zero)], axis=0)


def _unstack_heads(t):
    n = t.shape[0] // 2
    lane = lax.broadcasted_iota(jnp.int32, (n, 128), 1)
    return jnp.where(lane < 64, t[:n], t[n:])


NA_BIAS_SHAPE = (len(NA_CLASSES), 2 * NA_Q, NA_K)


def _na_bias_pair(flat_ref, out_ref):
    qc = lax.broadcasted_iota(jnp.int32, (GW, 512), 0)
    kc = lax.broadcasted_iota(jnp.int32, (GW, 512), 1) & 63
    start = jnp.clip(qc - 8, 0, GW - 16)
    window = (kc >= start) & (kc < start + 16)
    fill = jnp.full((GW, NA_K - 512), NEG, F32)
    for hh in (0, 1):
        skew = _barrel(pltpu.roll(jnp.broadcast_to(flat_ref[hh], (GW, 1024)), 1024 - 15, 1), left=False)
        by_class = [jnp.where(window, (skew if rc == 7 else pltpu.roll(skew, (9 + rc) * 64, 1))[:, 0:512], NEG)
                    for rc in range(8)]
        for cls in range(len(NA_CLASSES)):
            for qr, (rc, off) in enumerate(_tile_rows(cls)):
                w = jnp.concatenate([by_class[rc], fill], axis=1)
                rows = slice(hh * NA_Q + qr * GW, hh * NA_Q + (qr + 1) * GW)
                out_ref[cls, rows, :] = pltpu.roll(w, off * GW, 1) if off else w


def na_fwd(pna, pnac, bias, mixin, hosted):
    nb = pna.shape[0]

    def body(*refs):
        (p_ref, pc_ref, bias_ref, _), h_in, (out_ref,), h_out, _, h_sems = hosted.split(refs, 4, 1)
        grid_step = pl.program_id(0) * nb + pl.program_id(1)

        @pl.when(grid_step == 0)
        def _():
            hosted.start(h_in, h_out, h_sems)

        kc, vc = pc_ref[:, 128:256], pc_ref[:, 256:384]

        def tile(t, carry):
            qsl, bsl, cls = _na_tile(t)
            kb, vb = p_ref[bsl, 128:256], p_ref[bsl, 256:384]
            p_loc, p_ctx = _na_probs(_stack_heads(p_ref[qsl, 0:128] * 0.125), kb, kc, bias_ref[cls])
            out_ref[qsl, :] = _unstack_heads(_nn(p_loc, vb) + _nn(p_ctx, vc)).astype(BF16)
            return carry

        lax.fori_loop(0, NA_TILES, tile, 0, unroll=4)

        @pl.when(grid_step == NPAIR * nb - 1)
        def _():
            hosted.finish(h_in, h_out, h_sems)

    h_in_specs, h_out_specs = hosted.specs()
    return pl.pallas_call(
        body, name="na_fwd", grid=(NPAIR, nb),
        in_specs=[
            pl.BlockSpec((None, SEQ, 384), lambda p, b: (b, 0, p)),
            pl.BlockSpec((None, LC, 384), lambda p, b: (b, 0, p)),
            pl.BlockSpec((None, len(NA_CLASSES), 2 * NA_Q, NA_K), lambda p, b: (p, 0, 0, 0)),
            pl.BlockSpec(memory_space=pl.ANY),
        ] + h_in_specs,
        out_specs=[pl.BlockSpec((None, SEQ, 128), lambda p, b: (b, 0, 4 + p))] + h_out_specs,
        out_shape=[jax.ShapeDtypeStruct((nb, SEQ, D), BF16)] + hosted.out_shape,
        input_output_aliases={3: 0},
        scratch_shapes=hosted.scratch,
        compiler_params=_cp(("arbitrary", "arbitrary")),
    )(pna, pnac, bias, mixin, *hosted.args)


def na_bwd(pna, pnac, bias, dmixin, dproj, dprojc, hosted):
    nb = pna.shape[0]

    def body(*refs):
        own_in, h_in, own_out, h_out, own_scr, h_sems = hosted.split(refs, 6, 3)
        p_ref, pc_ref, bias_ref, dmix_ref = own_in[:4]
        dp_ref, dpc_ref, dpat_ref = own_out
        dbias_s, dk_s, dv_s, dkc_s, dvc_s, res_s, resc_s = own_scr
        b, part = pl.program_id(1), pl.program_id(2)
        grid_step = (pl.program_id(0) * nb + b) * 3 + part

        @pl.when(grid_step == 0)
        def _():
            hosted.start(h_in, h_out, h_sems)

        @pl.when(grid_step == NPAIR * nb * 3 - 1)
        def _():
            hosted.finish(h_in, h_out, h_sems)

        @pl.when(part == 0)
        def _():
            @pl.when(b == 0)
            def _():
                dbias_s[...] = jnp.zeros_like(dbias_s)

            dk_s[...] = jnp.zeros_like(dk_s)
            dv_s[...] = jnp.zeros_like(dv_s)
            dkc_s[...] = jnp.zeros_like(dkc_s)
            dvc_s[...] = jnp.zeros_like(dvc_s)
            kc, vc = pc_ref[:, 128:256], pc_ref[:, 256:384]

            def tile(t, carry):
                qsl, bsl, cls = _na_tile(t)
                kb, vb = p_ref[bsl, 128:256], p_ref[bsl, 256:384]
                qst, dost = _stack_heads(p_ref[qsl, 0:128] * 0.125), _stack_heads(dmix_ref[qsl, :])
                p_loc, p_ctx = _na_probs(qst, kb, kc, bias_ref[cls])
                dp_loc, dp_ctx = _nt(dost, vb), _nt(dost, vc)
                delta = (jnp.sum(p_loc * dp_loc, axis=1, keepdims=True)
                         + jnp.sum(p_ctx * dp_ctx, axis=1, keepdims=True))
                ds_loc, ds_ctx = p_loc * (dp_loc - delta), p_ctx * (dp_ctx - delta)
                dbias_s[cls] += ds_loc
                res_s[0, qsl, :] = _unstack_heads((_nn(ds_loc, kb) + _nn(ds_ctx, kc)) * 0.125).astype(BF16)
                dk_s[bsl, :] += _tn(ds_loc, qst)
                dv_s[bsl, :] += _tn(p_loc, dost)
                dkc_s[...] += _tn(ds_ctx, qst)
                dvc_s[...] += _tn(p_ctx, dost)
                return carry

            lax.fori_loop(0, NA_TILES, tile, 0, unroll=2)
            res_s[1] = dk_s[...].astype(BF16)
            res_s[2] = dv_s[...].astype(BF16)
            resc_s[0] = jnp.zeros((LC, 128), BF16)
            resc_s[1] = dkc_s[...].astype(BF16)
            resc_s[2] = dvc_s[...].astype(BF16)

            @pl.when(b == nb - 1)
            def _():
                for hh in (0, 1):
                    by_class = [None] * 8
                    for cls in range(len(NA_CLASSES)):
                        for qr, (rc, off) in enumerate(_tile_rows(cls)):
                            w = dbias_s[cls, hh * NA_Q + qr * GW:hh * NA_Q + (qr + 1) * GW, :]
                            w = (pltpu.roll(w, NA_K - off * GW, 1) if off else w)[:, 0:512]
                            by_class[rc] = w if by_class[rc] is None else by_class[rc] + w
                    skew = jnp.zeros((GW, 1024), F32)
                    for rc in range(8):
                        w = jnp.concatenate([by_class[rc], jnp.zeros((GW, 512), F32)], axis=1)
                        skew = skew + (w if rc == 7 else pltpu.roll(w, (7 - rc) * 64, 1))
                    dpat_ref[hh] = jnp.sum(pltpu.roll(_barrel(skew, left=True), 15, 1), axis=0, keepdims=True)

        dp_ref[...] = res_s[part]
        dpc_ref[...] = resc_s[part]

    h_in_specs, h_out_specs = hosted.specs()
    return pl.pallas_call(
        body, name="na_bwd", grid=(NPAIR, nb, 3),
        in_specs=[
            pl.BlockSpec((None, SEQ, 384), lambda p, b, s: (b, 0, p)),
            pl.BlockSpec((None, LC, 384), lambda p, b, s: (b, 0, p)),
            pl.BlockSpec((None, len(NA_CLASSES), 2 * NA_Q, NA_K), lambda p, b, s: (p, 0, 0, 0)),
            pl.BlockSpec((None, SEQ, 128), lambda p, b, s: (b, 0, 4 + p)),
            pl.BlockSpec(memory_space=pl.ANY),
            pl.BlockSpec(memory_space=pl.ANY),
        ] + h_in_specs,
        out_specs=[
            pl.BlockSpec((None, SEQ, 128), lambda p, b, s: (b, 0, 16 + 3 * p + s)),
            pl.BlockSpec((None, LC, 128), lambda p, b, s: (b, 0, 16 + 3 * p + s)),
            pl.BlockSpec((None, 2, 1, 1024), lambda p, b, s: (p, 0, 0, 0)),
        ] + h_out_specs,
        out_shape=[
            jax.ShapeDtypeStruct((nb, SEQ, IN_W), BF16),
            jax.ShapeDtypeStruct((nb, LC, IN_W), BF16),
            jax.ShapeDtypeStruct((NPAIR, 2, 1, 1024), F32),
        ] + hosted.out_shape,
        input_output_aliases={4: 0, 5: 1},
        scratch_shapes=[
            pltpu.VMEM((len(NA_CLASSES), 2 * NA_Q, NA_K), F32),
            pltpu.VMEM((SEQ, 128), F32), pltpu.VMEM((SEQ, 128), F32),
            pltpu.VMEM((LC, 128), F32), pltpu.VMEM((LC, 128), F32),
            pltpu.VMEM((3, SEQ, 128), BF16), pltpu.VMEM((3, LC, 128), BF16),
        ] + hosted.scratch,
        compiler_params=_cp(("arbitrary", "arbitrary", "arbitrary")),
    )(pna, pnac, bias, dmixin, dproj, dprojc, *hosted.args)


def tail_fwd_bwd(x, mixin, tgt, mod3, g_post_mix, g_pre_mlp, g_post_mlp, wout, w1, w2):
    nb = x.shape[0]

    def body(x_ref, mi_ref, tgt_ref, mod_ref, gpm_ref, gpl_ref, gpo_ref, wo_ref, w1_ref, w2_ref,
             dx_ref, dmix_ref, h2_ref, du_ref, a_ref, dm_ref, dmi_ref, dmod_ref, dg_ref, loss_ref):
        b, t = pl.program_id(0), pl.program_id(1)
        gt1, sh2, sc2, gt2 = mod_ref[2:3, :], mod_ref[3:4, :], mod_ref[4:5, :], mod_ref[5:6, :]
        mix = jnp.dot(mi_ref[...], wo_ref[...], preferred_element_type=F32)
        (x1, h2), vjp_a = jax.vjp(_post_mix, x_ref[...], mix, gt1, sc2, sh2, gpm_ref[...], gpl_ref[...])
        h2b = h2.astype(BF16)
        h2_ref[...] = h2b
        m = jnp.zeros((TN, D), F32)
        relus = []
        for j in range(4):
            cols = slice(j * D, (j + 1) * D)
            r = jnp.maximum(jnp.dot(h2b, w1_ref[j], preferred_element_type=F32), 0.0)
            ab = (r * r).astype(BF16)
            a_ref[:, cols] = ab
            m = m + jnp.dot(ab, w2_ref[cols, :], preferred_element_type=F32)
            relus.append(r)
        loss, vjp_b = jax.vjp(_head_loss, x1, m, gt2, gpo_ref[...], tgt_ref[...])
        dx1, dm, dgt2, dgpo, _ = vjp_b(jnp.ones((1, 1), F32))
        dmb = dm.astype(BF16)
        dm_ref[...] = dmb
        dh2 = jnp.zeros((TN, D), F32)
        for j in range(4):
            cols = slice(j * D, (j + 1) * D)
            da = lax.dot_general(dmb, w2_ref[cols, :], (((1,), (1,)), ((), ())), preferred_element_type=F32)
            dub = (da * (2.0 * relus[j])).astype(BF16)
            du_ref[:, cols] = dub
            dh2 = dh2 + lax.dot_general(dub, w1_ref[j], (((1,), (1,)), ((), ())), preferred_element_type=F32)
        dx, dmix, dgt1, dsc2, dsh2, dgpm, dgpl = vjp_a((dx1, dh2))
        dx_ref[...] = dx
        dmixb = dmix.astype(BF16)
        dmix_ref[...] = dmixb
        dmi_ref[...] = lax.dot_general(dmixb, wo_ref[...], (((1,), (1,)), ((), ())),
                                       preferred_element_type=F32).astype(BF16)

        @pl.when(t == 0)
        def _():
            dmod_ref[...] = jnp.zeros_like(dmod_ref)

        @pl.when((t == 0) & (b == 0))
        def _():
            dg_ref[...] = jnp.zeros_like(dg_ref)
            loss_ref[...] = jnp.zeros_like(loss_ref)

        dmod_ref[2:3, :] += dgt1
        dmod_ref[3:4, :] += dsh2
        dmod_ref[4:5, :] += dsc2
        dmod_ref[5:6, :] += dgt2
        dg_ref[0:1, :] += dgpm
        dg_ref[1:2, :] += dgpl
        dg_ref[2:3, :] += dgpo
        loss_ref[...] += jnp.broadcast_to(loss, loss_ref.shape)

    tok = lambda b, t: (b, t, 0)
    const = lambda b, t: (0, 0)
    vec = pl.BlockSpec((1, D), const)
    return pl.pallas_call(
        body, name="tail_fwd_bwd", grid=(nb, SEQ // TN),
        in_specs=[
            pl.BlockSpec((None, TN, D), tok), pl.BlockSpec((None, TN, D), tok), pl.BlockSpec((None, TN, D), tok),
            pl.BlockSpec((None, 6, D), lambda b, t: (b, 0, 0)), vec, vec, vec,
            pl.BlockSpec((D, D), const, pipeline_mode=pl.Buffered(1)),
            pl.BlockSpec((4, D, D), lambda b, t: (0, 0, 0), pipeline_mode=pl.Buffered(1)),
            pl.BlockSpec((DFF, D), const, pipeline_mode=pl.Buffered(1)),
        ],
        out_specs=[
            pl.BlockSpec((None, TN, D), tok), pl.BlockSpec((None, TN, D), tok), pl.BlockSpec((None, TN, D), tok),
            pl.BlockSpec((None, TN, DFF), tok), pl.BlockSpec((None, TN, DFF), tok), pl.BlockSpec((None, TN, D), tok),
            pl.BlockSpec((None, TN, D), tok),
            pl.BlockSpec((None, 6, D), lambda b, t: (b, 0, 0)),
            pl.BlockSpec((8, D), const), pl.BlockSpec((8, 128), const),
        ],
        out_shape=[
            jax.ShapeDtypeStruct((nb, SEQ, D), F32), jax.ShapeDtypeStruct((nb, SEQ, D), BF16),
            jax.ShapeDtypeStruct((nb, SEQ, D), BF16), jax.ShapeDtypeStruct((nb, SEQ, DFF), BF16),
            jax.ShapeDtypeStruct((nb, SEQ, DFF), BF16), jax.ShapeDtypeStruct((nb, SEQ, D), BF16),
            jax.ShapeDtypeStruct((nb, SEQ, D), BF16),
            jax.ShapeDtypeStruct((nb, 6, D), F32), jax.ShapeDtypeStruct((8, D), F32),
            jax.ShapeDtypeStruct((8, 128), F32),
        ],
        compiler_params=_cp(("arbitrary", "arbitrary")),
    )(x, mixin, tgt, mod3, g_post_mix, g_pre_mlp, g_post_mlp, wout, w1, w2)


def weight_grad(pairs, name, out_dtype=F32, col_blocks=False, tm=1024, tn=1024, tk=2048):
    m, n = pairs[0][0].shape[1], pairs[0][1].shape[1]
    tn = min(tn, n)
    tks = [min(tk, xa.shape[0]) for xa, _ in pairs]
    steps = [xa.shape[0] // t for (xa, _), t in zip(pairs, tks)]
    total = sum(steps)
    offs = [sum(steps[:i]) for i in range(len(pairs))]

    def body(*refs):
        out_ref, acc = refs[2 * len(pairs)], refs[-1]
        k = pl.program_id(2)

        @pl.when(k == 0)
        def _():
            acc[...] = jnp.zeros_like(acc)

        for i in range(len(pairs)):
            @pl.when((k >= offs[i]) & (k < offs[i] + steps[i]))
            def _(i=i):
                acc[...] += lax.dot_general(refs[2 * i][...], refs[2 * i + 1][...], (((0,), (0,)), ((), ())),
                                            preferred_element_type=F32)

        if out_dtype != F32:
            @pl.when(k == total - 1)
            def _():
                out_ref[...] = acc[...].astype(out_dtype)

    in_specs, args = [], []
    for i, (xa, ya) in enumerate(pairs):
        clamp = lambda k, i=i: jnp.clip(k - offs[i], 0, steps[i] - 1)
        in_specs.append(pl.BlockSpec((tks[i], tm), lambda a, c, k, clamp=clamp: (clamp(k), a)))
        in_specs.append(pl.BlockSpec((tks[i], tn), lambda a, c, k, clamp=clamp: (clamp(k), c)))
        args += [xa, ya]
    if col_blocks:
        out_spec = pl.BlockSpec((None, tm, tn), lambda a, c, k: (c, a, 0))
        out_shape = jax.ShapeDtypeStruct((n // tn, m, tn), out_dtype)
    else:
        out_spec = pl.BlockSpec((tm, tn), lambda a, c, k: (a, c))
        out_shape = jax.ShapeDtypeStruct((m, n), out_dtype)
    return pl.pallas_call(
        body, name=name, grid=(m // tm, n // tn, total), in_specs=in_specs, out_specs=out_spec, out_shape=out_shape,
        scratch_shapes=[] if out_dtype == F32 else [pltpu.VMEM((tm, tn), F32)],
        compiler_params=_cp(("arbitrary", "arbitrary", "arbitrary")),
    )(*args)


def _perm_block(t):
    return 4 * (t % 4) + t // 4 if t < 16 else 16 + 3 * ((t - 16) % 4) + (t - 16) // 4


def _is_rope_block(p):
    return p < 16 and p % 4 < 2


def unpack_w_in(blocks):
    def body(i_ref, o_ref):
        for t in range(28):
            p = _perm_block(t)
            blk = i_ref[t // 7, :, (t % 7) * 128:(t % 7 + 1) * 128]
            if _is_rope_block(p):
                blk = _pair_order(blk.astype(F32)).astype(BF16)
            o_ref[:, p * 128:(p + 1) * 128] = blk

    return pl.pallas_call(
        body, name="unpack_w_in", grid=(2,),
        in_specs=[pl.BlockSpec((4, D // 2, 896), lambda i: (0, i, 0))],
        out_specs=pl.BlockSpec((D // 2, IN_W), lambda i: (i, 0)),
        out_shape=jax.ShapeDtypeStruct((D, IN_W), BF16),
    )(blocks)


def pack_w_in(dw):
    def body(i_ref, o_ref):
        for t in range(28):
            p = _perm_block(t)
            blk = i_ref[:, p * 128:(p + 1) * 128]
            if _is_rope_block(p):
                blk = _pair_order(blk)
            o_ref[t // 7, :, (t % 7) * 128:(t % 7 + 1) * 128] = blk.astype(BF16)

    return pl.pallas_call(
        body, name="pack_w_in", grid=(4,),
        in_specs=[pl.BlockSpec((D // 4, IN_W), lambda i: (i, 0))],
        out_specs=pl.BlockSpec((4, D // 4, 896), lambda i: (0, i, 0)),
        out_shape=jax.ShapeDtypeStruct((4, D, 896), BF16),
    )(dw)


def _place():
    return lax.axis_index("x"), lax.axis_index("y"), lax.axis_index("c")


class Hosted:
    def __init__(self, args, out_shape, scratch, start, finish):
        self.args, self.out_shape, self.scratch, self.start, self.finish = args, out_shape, scratch, start, finish

    def specs(self):
        hbm = pl.BlockSpec(memory_space=pl.ANY)
        return [hbm] * len(self.args), [hbm] * len(self.out_shape)

    def split(self, refs, n_in, n_out):
        a, b = len(self.args), len(self.out_shape)
        cuts = [n_in, n_in + a, n_in + a + n_out, n_in + a + n_out + b, len(refs) - len(self.scratch)]
        parts = [refs[i:j] for i, j in zip([0] + cuts, cuts + [len(refs)])]
        return parts[0], parts[1], parts[2], parts[3], parts[4], parts[5]


def no_exchange():
    return Hosted([], [], [], lambda *a: None, lambda *a: None)


def run_hosted(hosted, name):
    def body(*refs):
        _, ins, _, outs, _, sems = hosted.split(refs, 0, 0)
        hosted.start(ins, outs, sems)
        hosted.finish(ins, outs, sems)

    in_specs, out_specs = hosted.specs()
    return pl.pallas_call(body, name=name, in_specs=in_specs, out_specs=out_specs, out_shape=hosted.out_shape,
                          scratch_shapes=hosted.scratch)(*hosted.args)


def gather8(blocks, relay_diagonal=False):
    na = len(blocks)

    def copies(ins, outs, sems):
        send_sems, recv_sems, local_sem = sems
        x, y, c = _place()
        me, sibling = (x, y, c), (x, y, 1 - c)
        chips = [(1 - x, y), (x, 1 - y), (1 - x, 1 - y)]

        def slot(o_ref, px, py, pc, half=None):
            ref = o_ref.at[4 * px + 2 * py + pc]
            if half is None:
                return ref
            rows = ref.shape[0] // 2
            return ref.at[pl.ds(half * rows, rows)]

        def copy(a, k, block, to, src=None, half=None):
            return pltpu.make_async_remote_copy(
                src_ref=slot(outs[a], *block, half) if src is None else src, dst_ref=slot(outs[a], *block, half),
                send_sem=send_sems.at[a, k], recv_sem=recv_sems.at[a, k], device_id=to, device_id_type=MESH)

        mine = [pltpu.make_async_copy(ins[a], slot(outs[a], *me), local_sem.at[a]) for a in range(na)]
        first = []
        for a in range(na):
            first.append(copy(a, 0, me, sibling, src=ins[a]))
            first += [copy(a, 1 + j, me, (*chip, c), src=ins[a])
                      for j, chip in enumerate(chips[:2] if relay_diagonal else chips)]
        return copy, mine, first, me, sibling, chips, c

    def start(ins, outs, sems):
        _, mine, first, *_ = copies(ins, outs, sems)
        for cp in mine + first:
            cp.start()

    def finish(ins, outs, sems):
        copy, mine, first, me, sibling, chips, c = copies(ins, outs, sems)
        passed = []
        for j, chip in enumerate(chips[:2] if relay_diagonal else chips):
            for a in range(na):
                copy(a, 1 + j, (*chip, c), me).wait_recv()
                onward = [copy(a, 4 + j, (*chip, c), sibling)]
                if relay_diagonal:
                    onward.insert(0, copy(a, (3, 7)[j], (*chip, c), (*chips[1 - j], c), half=j))
                for cp in onward:
                    cp.start()
                passed += onward
        if relay_diagonal:
            for a in range(na):
                copy(a, 3, (*chips[2], c), me, half=0).wait_recv()
                copy(a, 7, (*chips[2], c), me, half=1).wait_recv()
                cp = copy(a, 6, (*chips[2], c), sibling)
                cp.start()
                passed.append(cp)
        for a in range(na):
            copy(a, 0, sibling, me).wait_recv()
            for j, chip in enumerate(chips):
                copy(a, 4 + j, (*chip, 1 - c), me).wait_recv()
        for cp in first + passed:
            cp.wait_send()
        for cp in mine:
            cp.wait()

    return Hosted(list(blocks), [jax.ShapeDtypeStruct((8,) + b.shape, b.dtype) for b in blocks],
                  [pltpu.SemaphoreType.DMA((na, 8)), pltpu.SemaphoreType.DMA((na, 8)), pltpu.SemaphoreType.DMA((na,))],
                  start, finish)


def chips3(arrays):
    na = len(arrays)

    def copies(ins, outs, sems):
        send_sems, recv_sems = sems
        x, y, c = _place()
        return [pltpu.make_async_remote_copy(
            src_ref=ins[a].at[2 * px + py], dst_ref=outs[a].at[k], send_sem=send_sems.at[a, k],
            recv_sem=recv_sems.at[a, k], device_id=(px, py, c), device_id_type=MESH)
            for a in range(na) for k, (px, py) in enumerate([(1 - x, y), (x, 1 - y), (1 - x, 1 - y)])]

    def start(ins, outs, sems):
        for cp in copies(ins, outs, sems):
            cp.start()

    def finish(ins, outs, sems):
        for cp in copies(ins, outs, sems):
            cp.wait()

    return Hosted(list(arrays), [jax.ShapeDtypeStruct((3,) + a.shape[1:], a.dtype) for a in arrays],
                  [pltpu.SemaphoreType.DMA((na, 3)), pltpu.SemaphoreType.DMA((na, 3))], start, finish)


def siblings(arrays):
    na = len(arrays)

    def copies(ins, outs, sems):
        send_sems, recv_sems = sems
        x, y, c = _place()
        return [pltpu.make_async_remote_copy(
            src_ref=ins[a], dst_ref=outs[a], send_sem=send_sems.at[a], recv_sem=recv_sems.at[a],
            device_id=(x, y, 1 - c), device_id_type=MESH) for a in range(na)]

    def start(ins, outs, sems):
        for cp in copies(ins, outs, sems):
            cp.start()

    def finish(ins, outs, sems):
        for cp in copies(ins, outs, sems):
            cp.wait()

    return Hosted(list(arrays), [jax.ShapeDtypeStruct(a.shape, a.dtype) for a in arrays],
                  [pltpu.SemaphoreType.DMA((na,)), pltpu.SemaphoreType.DMA((na,))], start, finish)


def both(first, second):
    na, no, ns = len(first.args), len(first.out_shape), len(first.scratch)

    def start(ins, outs, sems):
        first.start(ins[:na], outs[:no], sems[:ns])
        second.start(ins[na:], outs[no:], sems[ns:])

    def finish(ins, outs, sems):
        first.finish(ins[:na], outs[:no], sems[:ns])
        second.finish(ins[na:], outs[no:], sems[ns:])

    return Hosted(first.args + second.args, first.out_shape + second.out_shape, first.scratch + second.scratch,
                  start, finish)


def siblings4(arrays):
    na = len(arrays)

    def copies(ins, outs, sems):
        send_sems, recv_sems = sems
        x, y, c = _place()
        return [pltpu.make_async_remote_copy(
            src_ref=ins[a].at[2 * j + 1 - c], dst_ref=outs[a].at[j],
            send_sem=send_sems.at[a, j], recv_sem=recv_sems.at[a, j],
            device_id=(x, y, 1 - c), device_id_type=MESH) for a in range(na) for j in range(4)]

    def start(ins, outs, sems):
        for cp in copies(ins, outs, sems):
            cp.start()

    def finish(ins, outs, sems):
        for cp in copies(ins, outs, sems):
            cp.wait()

    return Hosted(list(arrays), [jax.ShapeDtypeStruct((4,) + a.shape[1:], a.dtype) for a in arrays],
                  [pltpu.SemaphoreType.DMA((na, 4)), pltpu.SemaphoreType.DMA((na, 4))], start, finish)


def _row_tile(r):
    for cand in (512, 256, 128, 64, 32, 16, 8):
        if r % cand == 0:
            return cand
    return r


def chip_partial(place, g8s, landed4s, name):
    n = len(g8s)

    def body(place_ref, *refs):
        del place_ref
        for g_ref, l_ref, o_ref in zip(refs[:n], refs[n:2 * n], refs[2 * n:]):
            o_ref[...] = (g_ref[...].astype(F32) + l_ref[...].astype(F32)).astype(BF16)

    own = [pl.BlockSpec((None,) + g.shape[1:], lambda j, s: (2 * j + s[0], 0, 0)) for g in g8s]
    plain = [pl.BlockSpec((None,) + g.shape[1:], lambda j, s: (j, 0, 0)) for g in g8s]
    return pl.pallas_call(
        body, name=name,
        grid_spec=pltpu.PrefetchScalarGridSpec(num_scalar_prefetch=1, grid=(4,), in_specs=own + plain, out_specs=plain),
        out_shape=[jax.ShapeDtypeStruct((4,) + g.shape[1:], BF16) for g in g8s],
    )(place, *g8s, *landed4s)


def shard_sum(place, partial4s, landed3s, name):
    n = len(partial4s)

    def body(place_ref, *refs):
        del place_ref
        for p_ref, l_ref, o_ref in zip(refs[:n], refs[n:2 * n], refs[2 * n:]):
            acc = p_ref[...].astype(F32)
            for k in range(3):
                acc = acc + l_ref[k].astype(F32)
            o_ref[...] = acc

    def halves(p, lead):
        r, ccols = p.shape[1:]
        return (lead, r // 2, ccols)

    return pl.pallas_call(
        body, name=name,
        grid_spec=pltpu.PrefetchScalarGridSpec(
            num_scalar_prefetch=1, grid=(2,),
            in_specs=[pl.BlockSpec(halves(p, None), lambda i, s: (s[1], i, 0)) for p in partial4s]
            + [pl.BlockSpec(halves(p, 3), lambda i, s: (0, i, 0)) for p in partial4s],
            out_specs=[pl.BlockSpec(halves(p, None)[1:], lambda i, s: (i, 0)) for p in partial4s]),
        out_shape=[jax.ShapeDtypeStruct(p.shape[1:], F32) for p in partial4s],
    )(place, *partial4s, *landed3s)


def _adamw_math(w, g, m, v):
    m2 = B1 * m + (1.0 - B1) * g
    v2 = B2 * v + (1.0 - B2) * (g * g)
    m_hat = m2 / (1.0 - B1 ** STEP)
    v_hat = v2 / (1.0 - B2 ** STEP)
    return -LR * (m_hat / (jnp.sqrt(v_hat) + AEPS) + WD * w), m2, v2


def adamw_halves(place, w, mine, theirs, m, v, name):
    r, ccols = w.shape
    hr = r // 2
    tr = _row_tile(hr)
    nt = hr // tr

    def body(place_ref, w_ref, a_ref, b_ref, m_ref, v_ref, g_out, d_out, m_out, v_out):
        g = jnp.where(pl.program_id(0) == place_ref[0], a_ref[...], b_ref[...])
        d, m2, v2 = _adamw_math(w_ref[...], g, m_ref[...], v_ref[...])
        g_out[...] = g
        d_out[...] = d
        m_out[...] = m2
        v_out[...] = v2

    full = pl.BlockSpec((tr, ccols), lambda h, i, s: (h * nt + i, 0))
    part = pl.BlockSpec((tr, ccols), lambda h, i, s: (i, 0))
    return pl.pallas_call(
        body, name=name,
        grid_spec=pltpu.PrefetchScalarGridSpec(
            num_scalar_prefetch=1, grid=(2, nt), in_specs=[full, part, part, full, full], out_specs=[full] * 4),
        out_shape=[jax.ShapeDtypeStruct((r, ccols), F32)] * 4,
    )(place, w, mine, theirs, m, v)


def adamw_group(place, halved, plain, hosted, name):
    rows = halved[0][0].shape[0]
    tr = 128
    nt = rows // 2 // tr
    nh, npl = len(halved), len(plain)

    def body(place_ref, *refs):
        own_in, h_in, own_out, h_out, _, h_sems = hosted.split(refs, 5 * nh + 4 * npl, 4 * nh + 3 * npl)
        half = pl.program_id(0)
        grid_step = half * nt + pl.program_id(1)

        @pl.when(grid_step == 0)
        def _():
            hosted.start(h_in, h_out, h_sems)

        for i in range(nh):
            w_ref, a_ref, b_ref, m_ref, v_ref = own_in[5 * i:5 * i + 5]
            g = jnp.where(half == place_ref[0], a_ref[...], b_ref[...])
            res = (g,) + _adamw_math(w_ref[...], g, m_ref[...], v_ref[...])
            for o_ref, r in zip(own_out[4 * i:4 * i + 4], res):
                o_ref[...] = r
        for i in range(npl):
            w_ref, g_ref, m_ref, v_ref = own_in[5 * nh + 4 * i:5 * nh + 4 * i + 4]
            res = _adamw_math(w_ref[...], g_ref[...], m_ref[...], v_ref[...])
            for o_ref, r in zip(own_out[4 * nh + 3 * i:4 * nh + 3 * i + 3], res):
                o_ref[...] = r

        @pl.when(grid_step == 2 * nt - 1)
        def _():
            hosted.finish(h_in, h_out, h_sems)

    def full(cols):
        return pl.BlockSpec((tr, cols), lambda h, i, s: (h * nt + i, 0))

    def part(cols):
        return pl.BlockSpec((tr, cols), lambda h, i, s: (i, 0))

    in_specs, out_specs, out_shape, args = [], [], [], []
    for w, a, b, m, v in halved:
        cols = w.shape[1]
        in_specs += [full(cols), part(cols), part(cols), full(cols), full(cols)]
        out_specs += [full(cols)] * 4
        out_shape += [jax.ShapeDtypeStruct(w.shape, F32)] * 4
        args += [w, a, b, m, v]
    for w, g, m, v in plain:
        cols = w.shape[1]
        in_specs += [full(cols)] * 4
        out_specs += [full(cols)] * 3
        out_shape += [jax.ShapeDtypeStruct(w.shape, F32)] * 3
        args += [w, g, m, v]
    h_in_specs, h_out_specs = hosted.specs()
    return pl.pallas_call(
        body, name=name,
        grid_spec=pltpu.PrefetchScalarGridSpec(
            num_scalar_prefetch=1, grid=(2, nt), in_specs=in_specs + h_in_specs, out_specs=out_specs + h_out_specs,
            scratch_shapes=hosted.scratch),
        out_shape=out_shape + hosted.out_shape,
        compiler_params=_cp(("arbitrary", "arbitrary")),
    )(place, *args, *hosted.args)


def _silu(x):
    return x * jax.nn.sigmoid(x)


def prologue(c_rows, c_ctx_row, w_ada, b_shard, rpb_flat, half_w_in, late_shards):
    shape = jax.ShapeDtypeStruct
    n_late = len(late_shards)
    half_shapes = [(w.shape[0] // 2, w.shape[1]) for w in late_shards]
    g_w = gather8([half_w_in], relay_diagonal=True)
    g_c = gather8([shape((8, D), F32)])
    g_m = chips3([shape((4, 8, 1536), F32)])

    def body(*refs):
        c_ref, cc_ref, w_ref, b_ref, flat_ref, hw_ref = refs[:6]
        late_refs = refs[6:6 + n_late]
        cin_ref, mg_ref, gw_ref, bias_ref, cos_ref, sin_ref = refs[6 + n_late:12 + n_late]
        rest = refs[12 + n_late:]
        half_refs, (cg_s, ms_s, bias_s, need_s, landed_s) = rest[:n_late], rest[n_late:n_late + 5]
        stage, (load_sem, bias_sem), sems = rest[n_late + 5:2 * n_late + 5], rest[2 * n_late + 5:2 * n_late + 7], \
            rest[2 * n_late + 7:]
        sw, sc, sm = sems[0:3], sems[3:6], sems[6:8]
        px, py, core = _place()
        g_c.start([c_ref], [cg_s], sc)
        g_w.start([hw_ref], [gw_ref], sw)
        loads = [pltpu.make_async_copy(late_refs[a].at[pl.ds(core * half_shapes[a][0], half_shapes[a][0]), :],
                                       stage[a], load_sem.at[a]) for a in range(n_late)]
        for cp in loads:
            cp.start()
        cos_ref[...], sin_ref[...] = _rope_tables()
        for a, cp in enumerate(loads):
            cp.wait()
            half_refs[a][...] = stage[a][...].astype(BF16)
        g_c.finish([c_ref], [cg_s], sc)
        cin_ref[...] = jnp.zeros_like(cin_ref)
        for dev in range(8):
            cin_ref[2 * dev:2 * dev + 2, :] = cg_s[dev, 0:2, :]
        cin_ref[16:17, :] = cc_ref[...]
        need_s[...] = jnp.zeros_like(need_s)
        for j in range(4):
            need_s[8 * j:8 * j + 2, :] = cg_s[2 * j + core, 0:2, :]
            need_s[8 * j + 2:8 * j + 3, :] = cc_ref[...]
        ms_s[...] = (_nn(_silu(need_s[...]), w_ref[...]) + b_ref[...]).reshape(4, 8, 1536)
        g_m.start([ms_s], [landed_s], sm)
        stores = []
        for pair in range(NPAIR):
            if pair >= 2:
                stores[pair - 2].wait()
            _na_bias_pair(flat_ref.at[pair], bias_s.at[pair % 2])
            stores.append(pltpu.make_async_copy(bias_s.at[pair % 2], bias_ref.at[pair], bias_sem.at[pair % 2]))
            stores[pair].start()
        for cp in stores[-2:]:
            cp.wait()
        g_w.finish([hw_ref], [gw_ref], sw)
        g_m.finish([ms_s], [landed_s], sm)
        mg_ref[2 * px + py] = ms_s[2 * px + py]
        for k, (qx, qy) in enumerate([(1 - px, py), (px, 1 - py), (1 - px, 1 - py)]):
            mg_ref[2 * qx + qy] = landed_s[k]

    vmem = pl.BlockSpec(memory_space=pltpu.VMEM)
    hbm = pl.BlockSpec(memory_space=pl.ANY)
    return pl.pallas_call(
        body, name="prologue", in_specs=[vmem, vmem, vmem, vmem, vmem, hbm] + [hbm] * n_late,
        out_specs=[vmem, vmem, hbm, hbm, vmem, vmem] + [vmem] * n_late,
        out_shape=[shape((32, D), F32), shape((4, 8, 1536), F32)] + g_w.out_shape
        + [shape((NPAIR,) + NA_BIAS_SHAPE, F32)] + [shape((SEQ, RD), F32)] * 2 + [shape(s, BF16) for s in half_shapes],
        scratch_shapes=[pltpu.VMEM((8, 8, D), F32), pltpu.VMEM((4, 8, 1536), F32), pltpu.VMEM((2,) + NA_BIAS_SHAPE, F32),
                        pltpu.VMEM((32, D), F32), pltpu.VMEM((3, 8, 1536), F32)]
        + [pltpu.VMEM(s, F32) for s in half_shapes]
        + [pltpu.SemaphoreType.DMA((n_late,)), pltpu.SemaphoreType.DMA((2,))]
        + g_w.scratch + g_c.scratch + g_m.scratch,
        compiler_params=_cp(),
    )(c_rows, c_ctx_row, w_ada, b_shard, rpb_flat, half_w_in, *late_shards)


def ada_grads(cin, gb, gc, w_ada):
    def body(c_ref, gb_ref, gc_ref, w_ref, gw_ref, pc_ref):
        ctx_tot = jnp.sum(gc_ref[...], axis=0, keepdims=True)
        rows = lax.broadcasted_iota(jnp.int32, (16, 512), 0)
        dm = jnp.concatenate([gb_ref[...], jnp.where(rows == 0, ctx_tot, 0.0)], axis=0)
        gw_ref[...] = _tn(_silu(c_ref[...]), dm)
        rows8 = lax.broadcasted_iota(jnp.int32, (8, 512), 0)
        part = _nt(jnp.where(rows8 == 0, ctx_tot, 0.0), w_ref[...])

        @pl.when(pl.program_id(0) == 0)
        def _():
            pc_ref[...] = jnp.zeros_like(pc_ref)

        pc_ref[...] += part

    return pl.pallas_call(
        body, name="ada_grads", grid=(3,),
        in_specs=[pl.BlockSpec((32, D), lambda j: (0, 0)), pl.BlockSpec((16, 512), lambda j: (0, j)),
                  pl.BlockSpec((8, 512), lambda j: (0, j)), pl.BlockSpec((D, 512), lambda j: (0, j))],
        out_specs=[pl.BlockSpec((D, 512), lambda j: (0, j)), pl.BlockSpec((8, D), lambda j: (0, 0))],
        out_shape=[jax.ShapeDtypeStruct((D, 1536), F32), jax.ShapeDtypeStruct((8, D), F32)],
    )(cin, gb, gc, w_ada)


SMALL_SUM_ROWS = 15


def small_update(gsm, gbf, gcf, pcg, params):
    n = len(params)

    def body(*refs):
        gsm_ref, gbf_ref, gcf_ref, pcg_ref = refs[:4]
        wmv, outs, loss_out = refs[4:4 + 3 * n], refs[4 + 3 * n:4 + 7 * n], refs[-1]
        acc = gsm_ref[0]
        for dev in range(1, 8):
            acc = acc + gsm_ref[dev]
        c_ctx = wmv[0][...]
        sg = jax.nn.sigmoid(c_ctx)
        dsilu = pcg_ref[0:1, :] + pcg_ref[2:3, :] + pcg_ref[4:5, :] + pcg_ref[6:7, :]
        lane = lax.broadcasted_iota(jnp.int32, (1, D), 1)
        last = acc[14:15, :]
        grads = [
            dsilu * (sg * (1.0 + c_ctx * (1.0 - sg))),
            jnp.sum(gbf_ref[...], axis=0, keepdims=True) + jnp.sum(gcf_ref[...], axis=0, keepdims=True),
            acc[0:1, :] + acc[1:2, :], acc[2:3, :], acc[3:4, :], acc[4:5, :],
            acc[5:6, 0:512], acc[6:14, :], jnp.where(lane < 8, last, 0.0),
        ]
        loss_out[...] = jnp.broadcast_to(jnp.sum(jnp.where(lane == 8, last, 0.0), axis=1, keepdims=True), (8, 128))
        for i, g in enumerate(grads):
            d, m2, v2 = _adamw_math(wmv[3 * i][...], g, wmv[3 * i + 1][...], wmv[3 * i + 2][...])
            outs[4 * i][...] = g
            outs[4 * i + 1][...] = d
            outs[4 * i + 2][...] = m2
            outs[4 * i + 3][...] = v2

    flat = [a for wmv in params for a in wmv]
    out_shape = [jax.ShapeDtypeStruct(w.shape, F32) for w, _, _ in params for _ in range(4)]
    return pl.pallas_call(
        body, name="small_update", out_shape=out_shape + [jax.ShapeDtypeStruct((8, 128), F32)],
    )(gsm, gbf, gcf, pcg, *flat)


def _pad_row(v, rows):
    flat = v.reshape(-1)
    return jnp.pad(flat, (0, rows * D - flat.shape[0])).reshape(rows, D)


def local_step(x, ctx, tgt, mod3, rope, bias, g_pre_mix, g_post_mix, g_pre_mlp, g_post_mlp, ret_decay, ret_gn,
               wperm, late_weights, early_grads):
    nb = x.shape[0]
    tokens = nb * SEQ
    cos, sin = rope
    rd = ret_decay.T.reshape(RH, 2, 1)
    gn = ret_gn.reshape(RH, 1, RD)
    h, pret, pna = premix_proj(x, mod3, g_pre_mix, wperm, False, "premix_proj")
    hc, pretc, pnac = premix_proj(ctx, mod3, g_pre_mix, wperm, True, "premix_proj_ctx")
    o_all, mixin, gw_out = retention_fwd(pret, pretc, rd, gn, cos, sin, late_weights(0))
    mixin, gw1, gw2 = na_fwd(pna, pnac, bias, mixin, late_weights(1))
    dx_tail, dmix, h2, du, act, dm, dmixin, dmod_t, dg_t, loss_t = tail_fwd_bwd(
        x, mixin, tgt, mod3, g_post_mix, g_pre_mlp, g_post_mlp, gw_out.reshape(D, D), gw1.reshape(4, D, D),
        gw2.reshape(DFF, D))
    dw_out = weight_grad([(mixin.reshape(tokens, D), dmix.reshape(tokens, D))], "grad_w_out", BF16)
    dw1 = weight_grad([(h2.reshape(tokens, D), du.reshape(tokens, DFF))], "grad_w_mlp1", BF16, col_blocks=True)
    dw2 = weight_grad([(act.reshape(tokens, DFF), dm.reshape(tokens, D))], "grad_w_mlp2", BF16)
    dproj, dprojc, drd, dgn, *landed = retention_bwd(pret, pretc, o_all, dmixin, rd, gn, cos, sin,
                                                     early_grads[0](dw_out, dw1, dw2))
    dproj, dprojc, dpat, *early = na_bwd(pna, pnac, bias, dmixin, dproj, dprojc, early_grads[1](landed))
    dw_in = weight_grad([(h.reshape(tokens, D), dproj.reshape(tokens, IN_W)),
                         (hc.reshape(nb * LC, D), dprojc.reshape(nb * LC, IN_W))], "grad_w_in", tn=IN_W // 2, tk=1024)
    dmod_c, dg_c, *late = premix_bwd(ctx, mod3, g_pre_mix, wperm, dprojc, None, early_grads[2](dw_in), "premix_bwd_ctx")
    grad_x, dmod_a, dg_a, *late = premix_bwd(x, mod3, g_pre_mix, wperm, dproj, dx_tail, early_grads[3](late),
                                             "premix_bwd")
    dmod = jnp.concatenate([jnp.concatenate([dmod_a[:, 0:2], dmod_t[:, 2:6]], axis=1), dmod_c], axis=0)
    last = jnp.pad(jnp.concatenate([drd[:, :, 0].T.reshape(8), loss_t[0, 0:1]]), (0, D - 9)).reshape(1, D)
    small = jnp.concatenate([dg_a[0:1], dg_c[0:1], dg_t[0:3], _pad_row(dgn, 1), dpat.reshape(8, D), last], axis=0)
    return grad_x, late, early, dmod, small


def kernel(x, c, ctx, c_ctx, w_ada, b_ada, g_pre_mix, g_post_mix, g_pre_mlp, g_post_mlp, w_in, ret_decay, ret_gn, na_rpb, w_out, w_mlp1, w_mlp2, loss_target, m_c_ctx, m_w_ada, m_b_ada, m_g_pre_mix, m_g_post_mix, m_g_pre_mlp, m_g_post_mlp, m_w_in, m_ret_decay, m_ret_gn, m_na_rpb, m_w_out, m_w_mlp1, m_w_mlp2, v_c_ctx, v_w_ada, v_b_ada, v_g_pre_mix, v_g_post_mix, v_g_pre_mlp, v_g_post_mlp, v_w_in, v_ret_decay, v_ret_gn, v_na_rpb, v_w_out, v_w_mlp1, v_w_mlp2):
    px, py, pc = _place()
    chip = 2 * px + py

    half_w_in = lax.dynamic_slice_in_dim(w_in[0], pc * (D // 2), D // 2, 0).astype(BF16)
    cin, mg, gw_in, bias, cos, sin, *late_halves = prologue(
        jnp.pad(c, ((0, 6), (0, 0))), c_ctx[None], w_ada[0], lax.dynamic_slice_in_dim(b_ada, chip * 1536, 1536, 1),
        _rpb_flat(na_rpb[0]), half_w_in, [w_out[0], w_mlp1[0], w_mlp2[0]])
    halves = [half_w_in] + late_halves
    wperm = unpack_w_in(gw_in.reshape(4, D, 896))
    mod3 = mg[:, 0:3].transpose(1, 0, 2).reshape(3, 6, D)

    place = jnp.stack([pc, chip]).astype(jnp.int32)

    early_names = ["w_out", "w_mlp1", "w_mlp2"]
    early_g8, early_partial = [], []

    def early_a(dw_out, dw1, dw2):
        early_g8[:] = [dw_out.reshape(8, 128, D), dw1.reshape(8, 512, D), dw2.reshape(8, 512, D)]
        return siblings4(early_g8)

    def early_b(landed):
        early_partial[:] = chip_partial(place, early_g8, landed, "rs_chip_sum_early")
        return chips3(early_partial)

    late_partial = []

    late_g8 = []

    def late_c(dw_in):
        late_g8[:] = [pack_w_in(dw_in).reshape(8, 512, 896)]
        return siblings4(late_g8)

    def late_d(landed):
        late_partial[:] = chip_partial(place, late_g8, landed, "rs_chip_sum_w_in")
        return chips3(late_partial)

    grad_x, (landed3_in,), early_landed, dmod, small = local_step(
        x, ctx, loss_target, mod3, (cos, sin), bias, g_pre_mix, g_post_mix, g_pre_mlp, g_post_mlp, ret_decay[0], ret_gn,
        wperm, lambda k: gather8(halves[1:2] if k == 0 else halves[2:4]), (early_a, early_b, late_c, late_d))
    early_mine = shard_sum(place, early_partial, early_landed, "rs_shard_sum_early")

    pay = jnp.concatenate([dmod.reshape(18, D), small, jnp.zeros((40 - 18 - SMALL_SUM_ROWS, D), F32)], axis=0)
    *early_theirs, gs = run_hosted(both(siblings(early_mine), gather8([pay])), "rs_halves_early_gather_small")
    gbf = gs[:, 0:12].reshape(16, 6 * D)
    gcf = gs[:, 12:18].reshape(8, 6 * D)
    gw_ada, pc_part = ada_grads(cin, lax.dynamic_slice_in_dim(gbf, chip * 1536, 1536, 1),
                                lax.dynamic_slice_in_dim(gcf, chip * 1536, 1536, 1), w_ada[0])
    (mine_in,) = shard_sum(place, late_partial, [landed3_in], "rs_shard_sum_w_in")
    theirs_in, pcg = run_hosted(both(siblings([mine_in]), gather8([pc_part])), "rs_halves_w_in_gather_c_ctx")

    grouped = adamw_group(
        place,
        [(w_mlp1[0], early_mine[1], early_theirs[1], m_w_mlp1[0], v_w_mlp1[0]),
         (w_mlp2[0], early_mine[2], early_theirs[2], m_w_mlp2[0], v_w_mlp2[0])],
        [(w_ada[0], gw_ada, m_w_ada[0], v_w_ada[0])], no_exchange(), "adamw_group")
    d_ada, m_ada, v_ada = grouped[8:11]
    big = [
        [r[None] for r in adamw_halves(place, w_in[0], mine_in, theirs_in, m_w_in[0], v_w_in[0], "adamw_w_in")],
        [r[None] for r in adamw_halves(place, w_out[0], early_mine[0], early_theirs[0], m_w_out[0], v_w_out[0],
                                       "adamw_w_out")],
        [r[None] for r in grouped[0:4]], [r[None] for r in grouped[4:8]],
    ]

    def rpb_rows(t):
        return _rpb_flat(t[0]).reshape(8, D)

    def decay_row(t):
        return jnp.pad(t.reshape(1, 8), ((0, 0), (0, D - 8)))

    views = [lambda t: t.reshape(1, D), lambda t: t, lambda t: t, lambda t: t, lambda t: t, lambda t: t, lambda t: t,
             rpb_rows, decay_row]
    back = [lambda t: t.reshape(D), lambda t: t, lambda t: t, lambda t: t, lambda t: t, lambda t: t, lambda t: t,
            lambda t: _rpb_flat_t(t)[None], lambda t: t[:, 0:8].reshape(1, 2, 4)]
    small_w = (c_ctx, b_ada, g_pre_mix, g_post_mix, g_pre_mlp, g_post_mlp, ret_gn, na_rpb, ret_decay)
    small_m = (m_c_ctx, m_b_ada, m_g_pre_mix, m_g_post_mix, m_g_pre_mlp, m_g_post_mlp, m_ret_gn, m_na_rpb, m_ret_decay)
    small_v = (v_c_ctx, v_b_ada, v_g_pre_mix, v_g_post_mix, v_g_pre_mlp, v_g_post_mlp, v_ret_gn, v_na_rpb, v_ret_decay)
    *res, loss8 = small_update(gs[:, 18:18 + SMALL_SUM_ROWS], gbf, gcf, pcg[:, 0],
                               [(f(w), f(m), f(v)) for f, w, m, v in zip(views, small_w, small_m, small_v)])

    def leaves(ada, idx):
        s_c, s_b, s_g1, s_g2, s_g3, s_g4, s_gn, s_rpb, s_rd = [back[i](res[4 * i + idx]) for i in range(9)]
        return [s_c, ada[None], s_b, s_g1, s_g2, s_g3, s_g4, big[0][idx], s_rd, s_gn, s_rpb,
                big[1][idx], big[2][idx], big[3][idx]]

    return (loss8[0, 0], grad_x, *leaves(gw_ada, 0), *leaves(d_ada, 1), *leaves(m_ada, 2), *leaves(v_ada, 3))
```

```python
import functools
import math

import jax
import jax.numpy as jnp
from jax import lax
from jax.experimental import pallas as pl
from jax.experimental.pallas import tpu as pltpu

F32, BF16 = jnp.float32, jnp.bfloat16
D = 1024
SEQ = 2048
LC = 256
GW = 64
RH, RD, CH = 4, 128, 128
NPAIR = 4
IN_W = 3584
RET_W = 2048
DFF = 4096
EPS = 1e-6
NEG = -1e30
TN = 256
NCH = SEQ // CH
LR, B1, B2, AEPS, WD, STEP = 0.001, 0.9, 0.999, 1e-08, 0.01, 10
MESH = pl.DeviceIdType.MESH
VMEM_LIMIT = 56 * 1024 * 1024


def _cp(sem=None):
    return pltpu.CompilerParams(dimension_semantics=sem, vmem_limit_bytes=VMEM_LIMIT)


def _nn(a, b):
    return jnp.dot(a.astype(BF16), b.astype(BF16), preferred_element_type=F32)


def _nt(a, b):
    return lax.dot_general(a.astype(BF16), b.astype(BF16), (((1,), (1,)), ((), ())), preferred_element_type=F32)


def _tn(a, b):
    return lax.dot_general(a.astype(BF16), b.astype(BF16), (((0,), (0,)), ((), ())), preferred_element_type=F32)


@jax.custom_vjp
def mm_tn(a, b):
    return _tn(a, b)


mm_tn.defvjp(lambda a, b: (_tn(a, b), (a, b)), lambda r, g: (_nt(r[1], g), _nn(r[0], g)))


def _rms(x):
    return x * lax.rsqrt(jnp.mean(x * x, axis=-1, keepdims=True) + EPS)


def _rms_mod(x, g, sc, sh):
    return (_rms(x) * g) * (1.0 + sc) + sh


def _post_mix(x, mix, gt1, sc2, sh2, g_post_mix, g_pre_mlp):
    x1 = x + gt1 * (_rms(mix) * g_post_mix)
    return x1, _rms_mod(x1, g_pre_mlp, sc2, sh2)


def _head_loss(x1, m, gt2, g_post_mlp, tgt):
    err = x1 + gt2 * (_rms(m) * g_post_mlp) - tgt
    return 0.5 * jnp.sum(jnp.mean(err * err, axis=-1, keepdims=True), axis=0, keepdims=True)


def _ln_gate(o, g, w):
    mu = jnp.mean(o, axis=-1, keepdims=True)
    var = jnp.mean(jnp.square(o - mu), axis=-1, keepdims=True)
    y = (o - mu) * lax.rsqrt(var + EPS)
    return (y * w) * (g * jax.nn.sigmoid(g))


def _pair_order(x):
    lane = lax.broadcasted_iota(jnp.int32, x.shape, 1)
    return jnp.where((lane >= 32) & (lane < 64), pltpu.roll(x, 96, 1),
                     jnp.where((lane >= 64) & (lane < 96), pltpu.roll(x, 32, 1), x))


def _rope(x, cos, sin):
    return x * cos + pltpu.roll(x, 64, 1) * sin


def _rope_t(g, cos, sin):
    return g * cos + pltpu.roll(g * sin, 64, 1)


def _rope_tables():
    tok = lax.broadcasted_iota(jnp.int32, (SEQ, RD), 0)
    lane = lax.broadcasted_iota(jnp.int32, (SEQ, RD), 1)
    pos = jnp.where((lane & 32) == 0, tok >> 6, tok & (GW - 1)).astype(F32)
    ang = pos * jnp.exp((lane & 31).astype(F32) * (-math.log(10000.0) / 32))
    return jnp.cos(ang), jnp.where(lane < 64, -jnp.sin(ang), jnp.sin(ang))


def _chunk_loop(n, body, init, k=4):
    def several(t, carry):
        for i in range(k):
            carry = body(k * t + i, carry)
        return carry

    return lax.fori_loop(0, n // k, several, init)


def _fiota(shape, dim):
    return lax.broadcasted_iota(jnp.int32, shape, dim).astype(F32)


def _ret_state(k, v, s, lg, reverse):
    pos = _fiota((CH, 1), 0)
    b_exp = pos if reverse else (CH - 1.0 - pos)
    return jnp.exp(lg * CH) * s + mm_tn(k * jnp.exp(lg * b_exp), v)


class _Decays:
    def __init__(self, lgs):
        i, j, pos = _fiota((CH, CH), 0), _fiota((CH, CH), 1), _fiota((CH, 1), 0)
        diffs = (i - j, j - i)
        keep = (diffs[0] >= 0, diffs[1] > 0)
        mats = [jnp.where(m, jnp.exp(lg * jnp.where(m, d, 0.0)), 0.0) for lg, d, m in zip(lgs, diffs, keep)]
        self.mask = mats[0] + mats[1]
        self.dmask = [mats[0] * diffs[0], mats[1] * diffs[1]]
        a_exp, b_exp = (pos + 1.0, CH - pos), (CH - 1.0 - pos, pos)
        self.a = [jnp.exp(lg * e) for lg, e in zip(lgs, a_exp)]
        self.b = [jnp.exp(lg * e) for lg, e in zip(lgs, b_exp)]
        self.da = [a * e for a, e in zip(self.a, a_exp)]
        self.db = [b * e for b, e in zip(self.b, b_exp)]
        self.g = [jnp.exp(lg * CH) for lg in lgs]


def _both(x, w):
    return jnp.concatenate([x * w[0], x * w[1]], axis=1)


def _total(x):
    return jnp.sum(jnp.sum(x, axis=1, keepdims=True), axis=0, keepdims=True)


def _state_pass(dec, init, k_s, v_of, st_s):
    def step(t, carry):
        out = []
        for d, s in enumerate(carry):
            n = (NCH - 1 - t) if d else t
            sl = pl.ds(pl.multiple_of(n * CH, CH), CH)
            st_s[n, d * RD:(d + 1) * RD, :] = s
            out.append(dec.g[d] * s + _tn(k_s[sl, :] * dec.b[d], v_of(sl)))
        return tuple(out)

    _chunk_loop(NCH, step, tuple(init))


def premix_proj(xin, mod3, g_pre, wperm, is_ctx, name):
    nb, length, _ = xin.shape
    tn = min(2 * TN, length)

    def body(x_ref, mod_ref, g_ref, w_ref, h_ref, pret_ref, pna_ref):
        h = _rms_mod(x_ref[...], g_ref[...], mod_ref[1:2, :], mod_ref[0:1, :])
        hb = h.astype(BF16)
        h_ref[...] = hb
        pret_ref[...] = jnp.dot(hb, w_ref[:, :RET_W], preferred_element_type=F32)
        pna_ref[...] = jnp.dot(hb, w_ref[:, RET_W:], preferred_element_type=F32).astype(BF16)

    return pl.pallas_call(
        body, name=name, grid=(nb, length // tn),
        in_specs=[
            pl.BlockSpec((None, tn, D), lambda b, t: (b, t, 0)),
            pl.BlockSpec((None, 6, D), (lambda b, t: (2, 0, 0)) if is_ctx else (lambda b, t: (b, 0, 0))),
            pl.BlockSpec((1, D), lambda b, t: (0, 0)),
            pl.BlockSpec((D, IN_W), lambda b, t: (0, 0), pipeline_mode=pl.Buffered(1)),
        ],
        out_specs=[
            pl.BlockSpec((None, tn, D), lambda b, t: (b, t, 0)),
            pl.BlockSpec((None, tn, RET_W), lambda b, t: (b, t, 0)),
            pl.BlockSpec((None, tn, IN_W - RET_W), lambda b, t: (b, t, 0)),
        ],
        out_shape=[
            jax.ShapeDtypeStruct((nb, length, D), BF16),
            jax.ShapeDtypeStruct((nb, length, RET_W), F32),
            jax.ShapeDtypeStruct((nb, length, IN_W - RET_W), BF16),
        ],
        compiler_params=_cp(("arbitrary", "arbitrary")),
    )(xin, mod3, g_pre, wperm)


def premix_bwd(xin, mod3, g_pre, wperm, dproj, dx_tail, hosted, name):
    nb, length, _ = xin.shape
    tn = min(TN, length)
    is_ctx = dx_tail is None

    def body(*refs):
        own_in, h_in, own_out, h_out, _, h_sems = hosted.split(refs, 5 if is_ctx else 6, 2 if is_ctx else 3)
        if is_ctx:
            (x_ref, mod_ref, g_ref, w_ref, dp_ref), (dmod_ref, dg_ref) = own_in, own_out
        else:
            (x_ref, mod_ref, g_ref, w_ref, dp_ref, dxt_ref), (dx_ref, dmod_ref, dg_ref) = own_in, own_out
        b, t = pl.program_id(0), pl.program_id(1)
        grid_step = b * (length // tn) + t

        @pl.when(grid_step == 0)
        def _():
            hosted.start(h_in, h_out, h_sems)

        @pl.when(grid_step == nb * (length // tn) - 1)
        def _():
            hosted.finish(h_in, h_out, h_sems)

        dh = lax.dot_general(dp_ref[...], w_ref[...], (((1,), (1,)), ((), ())), preferred_element_type=F32)
        _, vjp = jax.vjp(_rms_mod, x_ref[...], g_ref[...], mod_ref[1:2, :], mod_ref[0:1, :])
        dx, dg, dsc, dsh = vjp(dh)
        if not is_ctx:
            dx_ref[...] = dx + dxt_ref[...]

        @pl.when((t == 0) & ((b == 0) if is_ctx else True))
        def _():
            dmod_ref[...] = jnp.zeros_like(dmod_ref)

        @pl.when((t == 0) & (b == 0))
        def _():
            dg_ref[...] = jnp.zeros_like(dg_ref)

        dmod_ref[0:1, :] += dsh
        dmod_ref[1:2, :] += dsc
        dg_ref[0:1, :] += dg

    tok = lambda b, t: (b, t, 0)
    in_specs = [
        pl.BlockSpec((None, tn, D), tok),
        pl.BlockSpec((None, 6, D), (lambda b, t: (2, 0, 0)) if is_ctx else (lambda b, t: (b, 0, 0))),
        pl.BlockSpec((1, D), lambda b, t: (0, 0)),
        pl.BlockSpec((D, IN_W), lambda b, t: (0, 0), pipeline_mode=pl.Buffered(1)),
        pl.BlockSpec((None, tn, IN_W), tok),
    ]
    args = [xin, mod3, g_pre, wperm, dproj]
    out_specs = [
        pl.BlockSpec((None, 6, D), (lambda b, t: (0, 0, 0)) if is_ctx else (lambda b, t: (b, 0, 0))),
        pl.BlockSpec((8, D), lambda b, t: (0, 0)),
    ]
    out_shape = [jax.ShapeDtypeStruct((1 if is_ctx else nb, 6, D), F32), jax.ShapeDtypeStruct((8, D), F32)]
    if not is_ctx:
        in_specs.append(pl.BlockSpec((None, tn, D), tok))
        args.append(dx_tail)
        out_specs.insert(0, pl.BlockSpec((None, tn, D), tok))
        out_shape.insert(0, jax.ShapeDtypeStruct((nb, length, D), F32))
    h_in_specs, h_out_specs = hosted.specs()
    return pl.pallas_call(
        body, name=name, grid=(nb, length // tn), in_specs=in_specs + h_in_specs, out_specs=out_specs + h_out_specs,
        out_shape=out_shape + hosted.out_shape, scratch_shapes=hosted.scratch,
        compiler_params=_cp(("arbitrary", "arbitrary")),
    )(*args, *hosted.args)


def _ret_specs(order):
    def im(f):
        return lambda *g: f(*order(*g))
    return dict(
        pret=pl.BlockSpec((None, SEQ, 512), im(lambda b, h: (b, 0, h))),
        pretc=pl.BlockSpec((None, LC, 512), im(lambda b, h: (b, 0, h))),
        rd=pl.BlockSpec((None, 2, 1), im(lambda b, h: (h, 0, 0))),
        gn=pl.BlockSpec((None, 1, RD), im(lambda b, h: (h, 0, 0))),
        tab=pl.BlockSpec((SEQ, RD), im(lambda b, h: (0, 0))),
        head=pl.BlockSpec((None, SEQ, RD), im(lambda b, h: (b, 0, h))),
    )


def retention_fwd(pret, pretc, rd, gn, cos, sin, hosted):
    nb = pret.shape[0]
    sp = _ret_specs(lambda b, h: (b, h))

    def body(*refs):
        own_in, h_in, own_out, h_out, own_scr, h_sems = hosted.split(refs, 6, 2)
        p_ref, pc_ref, rd_ref, gn_ref, cos_ref, sin_ref = own_in
        (o_ref, mix_ref), (q_s, k_s, o_s, st_s) = own_out, own_scr
        grid_step = pl.program_id(0) * RH + pl.program_id(1)

        @pl.when(grid_step == 0)
        def _():
            hosted.start(h_in, h_out, h_sems)

        cos_v, sin_v = cos_ref[...], sin_ref[...]
        q_s[...] = _rope(p_ref[:, 0:128], cos_v, sin_v) * (RD ** -0.5)
        k_s[...] = _rope(p_ref[:, 128:256], cos_v, sin_v)
        lgs, init = [], []
        for rev in (False, True):
            lg = jax.nn.log_sigmoid(rd_ref[int(rev):int(rev) + 1, :])
            s = jnp.zeros((RD, RD), F32)
            for n in ((1, 0) if rev else (0, 1)):
                s = _ret_state(pc_ref[n * CH:(n + 1) * CH, 128:256], pc_ref[n * CH:(n + 1) * CH, 256:384], s, lg, rev)
            lgs.append(lg)
            init.append(s)

        dec = _Decays(lgs)
        _state_pass(dec, init, k_s, lambda sl: p_ref[sl, 256:384], st_s)

        def chunk(n, carry):
            sl = pl.ds(pl.multiple_of(n * CH, CH), CH)
            q = q_s[sl, :]
            o_s[sl, :] = (_nn(_nt(q, k_s[sl, :]) * dec.mask, p_ref[sl, 256:384]) + _nn(_both(q, dec.a), st_s[n]))
            return carry

        _chunk_loop(NCH, chunk, 0)
        o = o_s[...]
        o_ref[...] = o
        mix_ref[...] = _ln_gate(o, p_ref[:, 384:512], gn_ref[...]).astype(BF16)

        @pl.when(grid_step == nb * RH - 1)
        def _():
            hosted.finish(h_in, h_out, h_sems)

    h_in_specs, h_out_specs = hosted.specs()
    return pl.pallas_call(
        body, name="retention_fwd", grid=(nb, RH),
        in_specs=[sp["pret"], sp["pretc"], sp["rd"], sp["gn"], sp["tab"], sp["tab"]] + h_in_specs,
        out_specs=[sp["head"], sp["head"]] + h_out_specs,
        out_shape=[jax.ShapeDtypeStruct((nb, SEQ, RH * RD), F32), jax.ShapeDtypeStruct((nb, SEQ, D), BF16)]
        + hosted.out_shape,
        scratch_shapes=[pltpu.VMEM((SEQ, RD), F32)] * 3 + [pltpu.VMEM((NCH, 2 * RD, RD), F32)] + hosted.scratch,
        compiler_params=_cp(("arbitrary", "arbitrary")),
    )(pret, pretc, rd, gn, cos, sin, *hosted.args)


def retention_bwd(pret, pretc, o_all, dmixin, rd, gn, cos, sin, hosted):
    nb = pret.shape[0]
    sp = _ret_specs(lambda h, b: (b, h))

    def body(*refs):
        own_in, h_in, own_out, h_out, own_scr, h_sems = hosted.split(refs, 8, 4)
        p_ref, pc_ref, o_ref, dmix_ref, rd_ref, gn_ref, cos_ref, sin_ref = own_in
        dp_ref, dpc_ref, drd_ref, dgn_ref = own_out
        q_s, k_s, do_s, dq_s, dk_s, dv_s, st_s, gst_s = own_scr
        b = pl.program_id(1)
        grid_step = pl.program_id(0) * nb + b

        @pl.when(grid_step == 0)
        def _():
            hosted.start(h_in, h_out, h_sems)

        cos_v, sin_v = cos_ref[...], sin_ref[...]
        q_s[...] = _rope(p_ref[:, 0:128], cos_v, sin_v) * (RD ** -0.5)
        k_s[...] = _rope(p_ref[:, 128:256], cos_v, sin_v)
        _, gate_vjp = jax.vjp(_ln_gate, o_ref[...], p_ref[:, 384:512], gn_ref[...])
        do, dg, dgn = gate_vjp(dmix_ref[...].astype(F32))
        do_s[...] = do
        dp_ref[:, 384:512] = dg.astype(BF16)

        @pl.when(b == 0)
        def _():
            drd_ref[...] = jnp.zeros_like(drd_ref)
            dgn_ref[...] = jnp.zeros_like(dgn_ref)

        dgn_ref[...] += dgn
        kcs = [pc_ref[n * CH:(n + 1) * CH, 128:256] for n in (0, 1)]
        vcs = [pc_ref[n * CH:(n + 1) * CH, 256:384] for n in (0, 1)]
        dirs = []
        init = []
        for rev in (False, True):
            rdv = rd_ref[int(rev):int(rev) + 1, :]
            lg = jax.nn.log_sigmoid(rdv)
            order_c = (1, 0) if rev else (0, 1)
            s = jnp.zeros((RD, RD), F32)
            ctx_states = []
            for n in order_c:
                ctx_states.append(s)
                s = _ret_state(kcs[n], vcs[n], s, lg, rev)
            dirs.append((rev, order_c, lg, rdv, ctx_states))
            init.append(s)
        dec = _Decays([lg for _, _, lg, _, _ in dirs])

        def v_of(sl):
            return p_ref[sl, 256:384]

        _state_pass(dec, init, k_s, v_of, st_s)
        zeros = jnp.zeros((CH, RD), F32)

        def scores_back(n, carry):
            dmask_sum, da_f, da_b = carry
            sl = pl.ds(pl.multiple_of(n * CH, CH), CH)
            q, k, v, do = q_s[sl, :], k_s[sl, :], v_of(sl), do_s[sl, :]
            scores = _nt(q, k)
            d_att = _nt(do, v)
            d_scores = d_att * dec.mask
            d_qa = _nt(do, st_s[n])
            d_qf, d_qb = d_qa[:, 0:RD], d_qa[:, RD:2 * RD]
            dq_s[sl, :] = _nn(d_scores, k) + d_qf * dec.a[0] + d_qb * dec.a[1]
            dk_s[sl, :] = _tn(d_scores, q)
            dv_s[sl, :] = _tn(scores * dec.mask, do)
            gst_s[n] = _tn(_both(q, dec.a), do)
            return dmask_sum + d_att * scores, da_f + d_qf * q, da_b + d_qb * q

        dmask_sum, da_f, da_b = _chunk_loop(NCH, scores_back, (zeros, zeros, zeros))

        def state_back(t, carry):
            out = []
            for d, r in enumerate(carry):
                n = t if d else (NCH - 1 - t)
                rows = slice(d * RD, (d + 1) * RD)
                own = gst_s[n, rows, :]
                gst_s[n, rows, :] = r
                out.append(own + dec.g[d] * r)
            return tuple(out)

        d_states = _chunk_loop(NCH, state_back, (zeros, zeros))

        def updates_back(n, carry):
            db_f, db_b, dg_f, dg_b = carry
            sl = pl.ds(pl.multiple_of(n * CH, CH), CH)
            k, r, s = k_s[sl, :], gst_s[n], st_s[n]
            d_kw = _nt(v_of(sl), r)
            d_kf, d_kb = d_kw[:, 0:RD], d_kw[:, RD:2 * RD]
            dk_s[sl, :] += d_kf * dec.b[0] + d_kb * dec.b[1]
            dv_s[sl, :] += _nn(_both(k, dec.b), r)
            return (db_f + d_kf * k, db_b + d_kb * k, dg_f + r[0:RD, :] * s[0:RD, :],
                    dg_b + r[RD:2 * RD, :] * s[RD:2 * RD, :])

        db_dg = _chunk_loop(NCH, updates_back, (zeros, zeros, zeros, zeros))
        dkc = [None, None]
        dvc = [None, None]
        for d, ((rev, order_c, lg, rdv, ctx_states), ds) in enumerate(zip(dirs, d_states)):
            dlg = (_total(dmask_sum * dec.dmask[d]) + _total((da_f, da_b)[d] * dec.da[d])
                   + _total(db_dg[d] * dec.db[d]) + CH * dec.g[d] * _total(db_dg[2 + d]))
            for idx in (1, 0):
                n = order_c[idx]
                _, vjp = jax.vjp(functools.partial(_ret_state, reverse=rev), kcs[n], vcs[n], ctx_states[idx], lg)
                dk_c, dv_c, ds, dl = vjp(ds)
                dlg = dlg + dl
                dkc[n] = dk_c if dkc[n] is None else dkc[n] + dk_c
                dvc[n] = dv_c if dvc[n] is None else dvc[n] + dv_c
            drd_ref[int(rev):int(rev) + 1, :] += dlg * jax.nn.sigmoid(-rdv)
        dp_ref[:, 0:128] = _rope_t(dq_s[...] * (RD ** -0.5), cos_v, sin_v).astype(BF16)
        dp_ref[:, 128:256] = _rope_t(dk_s[...], cos_v, sin_v).astype(BF16)
        dp_ref[:, 256:384] = dv_s[...].astype(BF16)
        zero = jnp.zeros((CH, RD), BF16)
        for n in (0, 1):
            rows = slice(n * CH, (n + 1) * CH)
            dpc_ref[rows, 0:128] = zero
            dpc_ref[rows, 128:256] = dkc[n].astype(BF16)
            dpc_ref[rows, 256:384] = dvc[n].astype(BF16)
            dpc_ref[rows, 384:512] = zero

        @pl.when(grid_step == RH * nb - 1)
        def _():
            hosted.finish(h_in, h_out, h_sems)

    h_in_specs, h_out_specs = hosted.specs()
    return pl.pallas_call(
        body, name="retention_bwd", grid=(RH, nb),
        in_specs=[sp["pret"], sp["pretc"], sp["head"], sp["head"], sp["rd"], sp["gn"], sp["tab"], sp["tab"]]
        + h_in_specs,
        out_specs=[
            pl.BlockSpec((None, SEQ, 512), lambda h, b: (b, 0, h)),
            pl.BlockSpec((None, LC, 512), lambda h, b: (b, 0, h)),
            pl.BlockSpec((None, 2, 1), lambda h, b: (h, 0, 0)),
            pl.BlockSpec((None, 1, RD), lambda h, b: (h, 0, 0)),
        ] + h_out_specs,
        out_shape=[
            jax.ShapeDtypeStruct((nb, SEQ, IN_W), BF16),
            jax.ShapeDtypeStruct((nb, LC, IN_W), BF16),
            jax.ShapeDtypeStruct((RH, 2, 1), F32),
            jax.ShapeDtypeStruct((RH, 1, RD), F32),
        ] + hosted.out_shape,
        scratch_shapes=[pltpu.VMEM((SEQ, RD), F32)] * 6 + [pltpu.VMEM((NCH, 2 * RD, RD), F32)] * 2 + hosted.scratch,
        compiler_params=_cp(("arbitrary", "arbitrary")),
    )(pret, pretc, o_all, dmixin, rd, gn, cos, sin, *hosted.args)


def _rpb_flat(rpb):
    return jnp.pad(rpb, ((0, 0), (0, 1), (0, 33))).reshape(NPAIR, 2, 1, 1024)


def _rpb_flat_t(dflat):
    return dflat.reshape(8, 16, 64)[:, :15, :31]


def _barrel(x, left):
    row = lax.broadcasted_iota(jnp.int32, x.shape, 0)
    n = x.shape[1]
    for bit in range(6):
        s = 1 << bit
        x = jnp.where(((row >> bit) & 1) == 1, pltpu.roll(x, (n - s) if left else s, 1), x)
    return x


NA_TILE_ROWS, NA_BAND_ROWS = 4, 12
NA_Q, NA_K = NA_TILE_ROWS * GW, NA_BAND_ROWS * GW
NA_TILES = SEQ // NA_Q


def _band_start(r0):
    return min(max(r0 - 4, 0), 32 - NA_BAND_ROWS)


def _tile_layout(t):
    rows = range(t * NA_TILE_ROWS, (t + 1) * NA_TILE_ROWS)
    return tuple((r if r < 4 else (r - 24 if r > 28 else 4), min(max(r - 4, 0), 24) - _band_start(rows[0]))
                 for r in rows)


NA_CLASSES = sorted(set(_tile_layout(t) for t in range(NA_TILES)))


def _tile_rows(cls):
    return NA_CLASSES[cls]


def _na_tile(t):
    start = jnp.clip(NA_TILE_ROWS * t - 4, 0, 32 - NA_BAND_ROWS)
    cls = 0
    for tile in range(NA_TILES):
        cls = jnp.where(t == tile, NA_CLASSES.index(_tile_layout(tile)), cls)
    return pl.ds(pl.multiple_of(t * NA_Q, NA_Q), NA_Q), pl.ds(pl.multiple_of(start * GW, NA_Q), NA_K), cls


def _na_probs(qst, kb, kc, bias):
    s_loc = _nt(qst, kb) + bias
    s_ctx = _nt(qst, kc)
    m = jnp.maximum(jnp.max(s_loc, axis=1, keepdims=True), jnp.max(s_ctx, axis=1, keepdims=True))
    e_loc, e_ctx = jnp.exp(s_loc - m), jnp.exp(s_ctx - m)
    den = jnp.sum(e_loc, axis=1, keepdims=True) + jnp.sum(e_ctx, axis=1, keepdims=True)
    return e_loc / den, e_ctx / den


def _stack_heads(t):
    lane = lax.broadcasted_iota(jnp.int32, t.shape, 1)
    zero = jnp.zeros_like(t)
    return jnp.concatenate([jnp.where(lane < 64, t, zero), jnp.where(lane >= 64, t, zero)], axis=0)


def _unstack_heads(t):
    n = t.shape[0] // 2
    lane = lax.broadcasted_iota(jnp.int32, (n, 128), 1)
    return jnp.where(lane < 64, t[:n], t[n:])


NA_BIAS_SHAPE = (len(NA_CLASSES), 2 * NA_Q, NA_K)


def _na_bias_pair(flat_ref, out_ref):
    qc = lax.broadcasted_iota(jnp.int32, (GW, 512), 0)
    kc = lax.broadcasted_iota(jnp.int32, (GW, 512), 1) & 63
    start = jnp.clip(qc - 8, 0, GW - 16)
    window = (kc >= start) & (kc < start + 16)
    fill = jnp.full((GW, NA_K - 512), NEG, F32)
    for hh in (0, 1):
        skew = _barrel(pltpu.roll(jnp.broadcast_to(flat_ref[hh], (GW, 1024)), 1024 - 15, 1), left=False)
        by_class = [jnp.where(window, (skew if rc == 7 else pltpu.roll(skew, (9 + rc) * 64, 1))[:, 0:512], NEG)
                    for rc in range(8)]
        for cls in range(len(NA_CLASSES)):
            for qr, (rc, off) in enumerate(_tile_rows(cls)):
                w = jnp.concatenate([by_class[rc], fill], axis=1)
                rows = slice(hh * NA_Q + qr * GW, hh * NA_Q + (qr + 1) * GW)
                out_ref[cls, rows, :] = pltpu.roll(w, off * GW, 1) if off else w


def na_fwd(pna, pnac, bias, mixin, hosted):
    nb = pna.shape[0]

    def body(*refs):
        (p_ref, pc_ref, bias_ref, _), h_in, (out_ref,), h_out, _, h_sems = hosted.split(refs, 4, 1)
        grid_step = pl.program_id(0) * nb + pl.program_id(1)

        @pl.when(grid_step == 0)
        def _():
            hosted.start(h_in, h_out, h_sems)

        kc, vc = pc_ref[:, 128:256], pc_ref[:, 256:384]

        def tile(t, carry):
            qsl, bsl, cls = _na_tile(t)
            kb, vb = p_ref[bsl, 128:256], p_ref[bsl, 256:384]
            p_loc, p_ctx = _na_probs(_stack_heads(p_ref[qsl, 0:128] * 0.125), kb, kc, bias_ref[cls])
            out_ref[qsl, :] = _unstack_heads(_nn(p_loc, vb) + _nn(p_ctx, vc)).astype(BF16)
            return carry

        lax.fori_loop(0, NA_TILES, tile, 0, unroll=4)

        @pl.when(grid_step == NPAIR * nb - 1)
        def _():
            hosted.finish(h_in, h_out, h_sems)

    h_in_specs, h_out_specs = hosted.specs()
    return pl.pallas_call(
        body, name="na_fwd", grid=(NPAIR, nb),
        in_specs=[
            pl.BlockSpec((None, SEQ, 384), lambda p, b: (b, 0, p)),
            pl.BlockSpec((None, LC, 384), lambda p, b: (b, 0, p)),
            pl.BlockSpec((None, len(NA_CLASSES), 2 * NA_Q, NA_K), lambda p, b: (p, 0, 0, 0)),
            pl.BlockSpec(memory_space=pl.ANY),
        ] + h_in_specs,
        out_specs=[pl.BlockSpec((None, SEQ, 128), lambda p, b: (b, 0, 4 + p))] + h_out_specs,
        out_shape=[jax.ShapeDtypeStruct((nb, SEQ, D), BF16)] + hosted.out_shape,
        input_output_aliases={3: 0},
        scratch_shapes=hosted.scratch,
        compiler_params=_cp(("arbitrary", "arbitrary")),
    )(pna, pnac, bias, mixin, *hosted.args)


def na_bwd(pna, pnac, bias, dmixin, dproj, dprojc, hosted):
    nb = pna.shape[0]

    def body(*refs):
        own_in, h_in, own_out, h_out, own_scr, h_sems = hosted.split(refs, 6, 3)
        p_ref, pc_ref, bias_ref, dmix_ref = own_in[:4]
        dp_ref, dpc_ref, dpat_ref = own_out
        dbias_s, dk_s, dv_s, dkc_s, dvc_s, res_s, resc_s = own_scr
        b, part = pl.program_id(1), pl.program_id(2)
        grid_step = (pl.program_id(0) * nb + b) * 3 + part

        @pl.when(grid_step == 0)
        def _():
            hosted.start(h_in, h_out, h_sems)

        @pl.when(grid_step == NPAIR * nb * 3 - 1)
        def _():
            hosted.finish(h_in, h_out, h_sems)

        @pl.when(part == 0)
        def _():
            @pl.when(b == 0)
            def _():
                dbias_s[...] = jnp.zeros_like(dbias_s)

            dk_s[...] = jnp.zeros_like(dk_s)
            dv_s[...] = jnp.zeros_like(dv_s)
            dkc_s[...] = jnp.zeros_like(dkc_s)
            dvc_s[...] = jnp.zeros_like(dvc_s)
            kc, vc = pc_ref[:, 128:256], pc_ref[:, 256:384]

            def tile(t, carry):
                qsl, bsl, cls = _na_tile(t)
                kb, vb = p_ref[bsl, 128:256], p_ref[bsl, 256:384]
                qst, dost = _stack_heads(p_ref[qsl, 0:128] * 0.125), _stack_heads(dmix_ref[qsl, :])
                p_loc, p_ctx = _na_probs(qst, kb, kc, bias_ref[cls])
                dp_loc, dp_ctx = _nt(dost, vb), _nt(dost, vc)
                delta = (jnp.sum(p_loc * dp_loc, axis=1, keepdims=True)
                         + jnp.sum(p_ctx * dp_ctx, axis=1, keepdims=True))
                ds_loc, ds_ctx = p_loc * (dp_loc - delta), p_ctx * (dp_ctx - delta)
                dbias_s[cls] += ds_loc
                res_s[0, qsl, :] = _unstack_heads((_nn(ds_loc, kb) + _nn(ds_ctx, kc)) * 0.125).astype(BF16)
                dk_s[bsl, :] += _tn(ds_loc, qst)
                dv_s[bsl, :] += _tn(p_loc, dost)
                dkc_s[...] += _tn(ds_ctx, qst)
                dvc_s[...] += _tn(p_ctx, dost)
                return carry

            lax.fori_loop(0, NA_TILES, tile, 0, unroll=2)
            res_s[1] = dk_s[...].astype(BF16)
            res_s[2] = dv_s[...].astype(BF16)
            resc_s[0] = jnp.zeros((LC, 128), BF16)
            resc_s[1] = dkc_s[...].astype(BF16)
            resc_s[2] = dvc_s[...].astype(BF16)

            @pl.when(b == nb - 1)
            def _():
                for hh in (0, 1):
                    by_class = [None] * 8
                    for cls in range(len(NA_CLASSES)):
                        for qr, (rc, off) in enumerate(_tile_rows(cls)):
                            w = dbias_s[cls, hh * NA_Q + qr * GW:hh * NA_Q + (qr + 1) * GW, :]
                            w = (pltpu.roll(w, NA_K - off * GW, 1) if off else w)[:, 0:512]
                            by_class[rc] = w if by_class[rc] is None else by_class[rc] + w
                    skew = jnp.zeros((GW, 1024), F32)
                    for rc in range(8):
                        w = jnp.concatenate([by_class[rc], jnp.zeros((GW, 512), F32)], axis=1)
                        skew = skew + (w if rc == 7 else pltpu.roll(w, (7 - rc) * 64, 1))
                    dpat_ref[hh] = jnp.sum(pltpu.roll(_barrel(skew, left=True), 15, 1), axis=0, keepdims=True)

        dp_ref[...] = res_s[part]
        dpc_ref[...] = resc_s[part]

    h_in_specs, h_out_specs = hosted.specs()
    return pl.pallas_call(
        body, name="na_bwd", grid=(NPAIR, nb, 3),
        in_specs=[
            pl.BlockSpec((None, SEQ, 384), lambda p, b, s: (b, 0, p)),
            pl.BlockSpec((None, LC, 384), lambda p, b, s: (b, 0, p)),
            pl.BlockSpec((None, len(NA_CLASSES), 2 * NA_Q, NA_K), lambda p, b, s: (p, 0, 0, 0)),
            pl.BlockSpec((None, SEQ, 128), lambda p, b, s: (b, 0, 4 + p)),
            pl.BlockSpec(memory_space=pl.ANY),
            pl.BlockSpec(memory_space=pl.ANY),
        ] + h_in_specs,
        out_specs=[
            pl.BlockSpec((None, SEQ, 128), lambda p, b, s: (b, 0, 16 + 3 * p + s)),
            pl.BlockSpec((None, LC, 128), lambda p, b, s: (b, 0, 16 + 3 * p + s)),
            pl.BlockSpec((None, 2, 1, 1024), lambda p, b, s: (p, 0, 0, 0)),
        ] + h_out_specs,
        out_shape=[
            jax.ShapeDtypeStruct((nb, SEQ, IN_W), BF16),
            jax.ShapeDtypeStruct((nb, LC, IN_W), BF16),
            jax.ShapeDtypeStruct((NPAIR, 2, 1, 1024), F32),
        ] + hosted.out_shape,
        input_output_aliases={4: 0, 5: 1},
        scratch_shapes=[
            pltpu.VMEM((len(NA_CLASSES), 2 * NA_Q, NA_K), F32),
            pltpu.VMEM((SEQ, 128), F32), pltpu.VMEM((SEQ, 128), F32),
            pltpu.VMEM((LC, 128), F32), pltpu.VMEM((LC, 128), F32),
            pltpu.VMEM((3, SEQ, 128), BF16), pltpu.VMEM((3, LC, 128), BF16),
        ] + hosted.scratch,
        compiler_params=_cp(("arbitrary", "arbitrary", "arbitrary")),
    )(pna, pnac, bias, dmixin, dproj, dprojc, *hosted.args)


def tail_fwd_bwd(x, mixin, tgt, mod3, g_post_mix, g_pre_mlp, g_post_mlp, wout, w1, w2):
    nb = x.shape[0]

    def body(x_ref, mi_ref, tgt_ref, mod_ref, gpm_ref, gpl_ref, gpo_ref, wo_ref, w1_ref, w2_ref,
             dx_ref, dmix_ref, h2_ref, du_ref, a_ref, dm_ref, dmi_ref, dmod_ref, dg_ref, loss_ref):
        b, t = pl.program_id(0), pl.program_id(1)
        gt1, sh2, sc2, gt2 = mod_ref[2:3, :], mod_ref[3:4, :], mod_ref[4:5, :], mod_ref[5:6, :]
        mix = jnp.dot(mi_ref[...], wo_ref[...], preferred_element_type=F32)
        (x1, h2), vjp_a = jax.vjp(_post_mix, x_ref[...], mix, gt1, sc2, sh2, gpm_ref[...], gpl_ref[...])
        h2b = h2.astype(BF16)
        h2_ref[...] = h2b
        m = jnp.zeros((TN, D), F32)
        relus = []
        for j in range(4):
            cols = slice(j * D, (j + 1) * D)
            r = jnp.maximum(jnp.dot(h2b, w1_ref[j], preferred_element_type=F32), 0.0)
            ab = (r * r).astype(BF16)
            a_ref[:, cols] = ab
            m = m + jnp.dot(ab, w2_ref[cols, :], preferred_element_type=F32)
            relus.append(r)
        loss, vjp_b = jax.vjp(_head_loss, x1, m, gt2, gpo_ref[...], tgt_ref[...])
        dx1, dm, dgt2, dgpo, _ = vjp_b(jnp.ones((1, 1), F32))
        dmb = dm.astype(BF16)
        dm_ref[...] = dmb
        dh2 = jnp.zeros((TN, D), F32)
        for j in range(4):
            cols = slice(j * D, (j + 1) * D)
            da = lax.dot_general(dmb, w2_ref[cols, :], (((1,), (1,)), ((), ())), preferred_element_type=F32)
            dub = (da * (2.0 * relus[j])).astype(BF16)
            du_ref[:, cols] = dub
            dh2 = dh2 + lax.dot_general(dub, w1_ref[j], (((1,), (1,)), ((), ())), preferred_element_type=F32)
        dx, dmix, dgt1, dsc2, dsh2, dgpm, dgpl = vjp_a((dx1, dh2))
        dx_ref[...] = dx
        dmixb = dmix.astype(BF16)
        dmix_ref[...] = dmixb
        dmi_ref[...] = lax.dot_general(dmixb, wo_ref[...], (((1,), (1,)), ((), ())),
                                       preferred_element_type=F32).astype(BF16)

        @pl.when(t == 0)
        def _():
            dmod_ref[...] = jnp.zeros_like(dmod_ref)

        @pl.when((t == 0) & (b == 0))
        def _():
            dg_ref[...] = jnp.zeros_like(dg_ref)
            loss_ref[...] = jnp.zeros_like(loss_ref)

        dmod_ref[2:3, :] += dgt1
        dmod_ref[3:4, :] += dsh2
        dmod_ref[4:5, :] += dsc2
        dmod_ref[5:6, :] += dgt2
        dg_ref[0:1, :] += dgpm
        dg_ref[1:2, :] += dgpl
        dg_ref[2:3, :] += dgpo
        loss_ref[...] += jnp.broadcast_to(loss, loss_ref.shape)

    tok = lambda b, t: (b, t, 0)
    const = lambda b, t: (0, 0)
    vec = pl.BlockSpec((1, D), const)
    return pl.pallas_call(
        body, name="tail_fwd_bwd", grid=(nb, SEQ // TN),
        in_specs=[
            pl.BlockSpec((None, TN, D), tok), pl.BlockSpec((None, TN, D), tok), pl.BlockSpec((None, TN, D), tok),
            pl.BlockSpec((None, 6, D), lambda b, t: (b, 0, 0)), vec, vec, vec,
            pl.BlockSpec((D, D), const, pipeline_mode=pl.Buffered(1)),
            pl.BlockSpec((4, D, D), lambda b, t: (0, 0, 0), pipeline_mode=pl.Buffered(1)),
            pl.BlockSpec((DFF, D), const, pipeline_mode=pl.Buffered(1)),
        ],
        out_specs=[
            pl.BlockSpec((None, TN, D), tok), pl.BlockSpec((None, TN, D), tok), pl.BlockSpec((None, TN, D), tok),
            pl.BlockSpec((None, TN, DFF), tok), pl.BlockSpec((None, TN, DFF), tok), pl.BlockSpec((None, TN, D), tok),
            pl.BlockSpec((None, TN, D), tok),
            pl.BlockSpec((None, 6, D), lambda b, t: (b, 0, 0)),
            pl.BlockSpec((8, D), const), pl.BlockSpec((8, 128), const),
        ],
        out_shape=[
            jax.ShapeDtypeStruct((nb, SEQ, D), F32), jax.ShapeDtypeStruct((nb, SEQ, D), BF16),
            jax.ShapeDtypeStruct((nb, SEQ, D), BF16), jax.ShapeDtypeStruct((nb, SEQ, DFF), BF16),
            jax.ShapeDtypeStruct((nb, SEQ, DFF), BF16), jax.ShapeDtypeStruct((nb, SEQ, D), BF16),
            jax.ShapeDtypeStruct((nb, SEQ, D), BF16),
            jax.ShapeDtypeStruct((nb, 6, D), F32), jax.ShapeDtypeStruct((8, D), F32),
            jax.ShapeDtypeStruct((8, 128), F32),
        ],
        compiler_params=_cp(("arbitrary", "arbitrary")),
    )(x, mixin, tgt, mod3, g_post_mix, g_pre_mlp, g_post_mlp, wout, w1, w2)


def weight_grad(pairs, name, out_dtype=F32, col_blocks=False, tm=1024, tn=1024, tk=2048, w_in_blocks=False):
    m, n = pairs[0][0].shape[1], pairs[0][1].shape[1]
    tn = n if w_in_blocks else min(tn, n)
    tks = [min(tk, xa.shape[0]) for xa, _ in pairs]
    steps = [xa.shape[0] // t for (xa, _), t in zip(pairs, tks)]
    total = sum(steps)
    offs = [sum(steps[:i]) for i in range(len(pairs))]

    def body(*refs):
        out_ref, acc = refs[2 * len(pairs)], refs[-1]
        k = pl.program_id(2)

        @pl.when(k == 0)
        def _():
            acc[...] = jnp.zeros_like(acc)

        for i in range(len(pairs)):
            @pl.when((k >= offs[i]) & (k < offs[i] + steps[i]))
            def _(i=i):
                acc[...] += lax.dot_general(refs[2 * i][...], refs[2 * i + 1][...], (((0,), (0,)), ((), ())),
                                            preferred_element_type=F32)

        if out_dtype != F32:
            @pl.when(k == total - 1)
            def _():
                if w_in_blocks:
                    _to_w_in_blocks(acc, out_ref)
                else:
                    out_ref[...] = acc[...].astype(out_dtype)

    in_specs, args = [], []
    for i, (xa, ya) in enumerate(pairs):
        clamp = lambda k, i=i: jnp.clip(k - offs[i], 0, steps[i] - 1)
        in_specs.append(pl.BlockSpec((tks[i], tm), lambda a, c, k, clamp=clamp: (clamp(k), a)))
        in_specs.append(pl.BlockSpec((tks[i], tn), lambda a, c, k, clamp=clamp: (clamp(k), c)))
        args += [xa, ya]
    if w_in_blocks:
        assert n == IN_W and out_dtype == BF16
        out_spec = pl.BlockSpec((4, tm, 896), lambda a, c, k: (0, a, 0))
        out_shape = jax.ShapeDtypeStruct((4, m, 896), BF16)
    elif col_blocks:
        out_spec = pl.BlockSpec((None, tm, tn), lambda a, c, k: (c, a, 0))
        out_shape = jax.ShapeDtypeStruct((n // tn, m, tn), out_dtype)
    else:
        out_spec = pl.BlockSpec((tm, tn), lambda a, c, k: (a, c))
        out_shape = jax.ShapeDtypeStruct((m, n), out_dtype)
    return pl.pallas_call(
        body, name=name, grid=(m // tm, n // tn, total), in_specs=in_specs, out_specs=out_spec, out_shape=out_shape,
        scratch_shapes=[] if out_dtype == F32 else [pltpu.VMEM((tm, tn), F32)],
        compiler_params=_cp(("arbitrary", "arbitrary", "arbitrary")),
    )(*args)


def _perm_block(t):
    return 4 * (t % 4) + t // 4 if t < 16 else 16 + 3 * ((t - 16) % 4) + (t - 16) // 4


def _is_rope_block(p):
    return p < 16 and p % 4 < 2


def unpack_w_in(blocks):
    def body(i_ref, o_ref):
        for t in range(28):
            p = _perm_block(t)
            blk = i_ref[t // 7, :, (t % 7) * 128:(t % 7 + 1) * 128]
            if _is_rope_block(p):
                blk = _pair_order(blk.astype(F32)).astype(BF16)
            o_ref[:, p * 128:(p + 1) * 128] = blk

    return pl.pallas_call(
        body, name="unpack_w_in", grid=(2,),
        in_specs=[pl.BlockSpec((4, D // 2, 896), lambda i: (0, i, 0))],
        out_specs=pl.BlockSpec((D // 2, IN_W), lambda i: (i, 0)),
        out_shape=jax.ShapeDtypeStruct((D, IN_W), BF16),
    )(blocks)


def _to_w_in_blocks(i_ref, o_ref):
    for t in range(28):
        p = _perm_block(t)
        blk = i_ref[:, p * 128:(p + 1) * 128]
        if _is_rope_block(p):
            blk = _pair_order(blk)
        o_ref[t // 7, :, (t % 7) * 128:(t % 7 + 1) * 128] = blk.astype(BF16)


def _place():
    return lax.axis_index("x"), lax.axis_index("y"), lax.axis_index("c")


class Hosted:
    def __init__(self, args, out_shape, scratch, start, finish):
        self.args, self.out_shape, self.scratch, self.start, self.finish = args, out_shape, scratch, start, finish

    def specs(self):
        hbm = pl.BlockSpec(memory_space=pl.ANY)
        return [hbm] * len(self.args), [hbm] * len(self.out_shape)

    def split(self, refs, n_in, n_out):
        a, b = len(self.args), len(self.out_shape)
        cuts = [n_in, n_in + a, n_in + a + n_out, n_in + a + n_out + b, len(refs) - len(self.scratch)]
        parts = [refs[i:j] for i, j in zip([0] + cuts, cuts + [len(refs)])]
        return parts[0], parts[1], parts[2], parts[3], parts[4], parts[5]


def no_exchange():
    return Hosted([], [], [], lambda *a: None, lambda *a: None)


def run_hosted(hosted, name):
    def body(*refs):
        _, ins, _, outs, _, sems = hosted.split(refs, 0, 0)
        hosted.start(ins, outs, sems)
        hosted.finish(ins, outs, sems)

    in_specs, out_specs = hosted.specs()
    return pl.pallas_call(body, name=name, in_specs=in_specs, out_specs=out_specs, out_shape=hosted.out_shape,
                          scratch_shapes=hosted.scratch)(*hosted.args)


def gather8(blocks, relay_diagonal=False):
    na = len(blocks)

    def copies(ins, outs, sems):
        send_sems, recv_sems, local_sem = sems
        x, y, c = _place()
        me, sibling = (x, y, c), (x, y, 1 - c)
        chips = [(1 - x, y), (x, 1 - y), (1 - x, 1 - y)]

        def slot(o_ref, px, py, pc, half=None):
            ref = o_ref.at[4 * px + 2 * py + pc]
            if half is None:
                return ref
            rows = ref.shape[0] // 2
            return ref.at[pl.ds(half * rows, rows)]

        def copy(a, k, block, to, src=None, half=None):
            return pltpu.make_async_remote_copy(
                src_ref=slot(outs[a], *block, half) if src is None else src, dst_ref=slot(outs[a], *block, half),
                send_sem=send_sems.at[a, k], recv_sem=recv_sems.at[a, k], device_id=to, device_id_type=MESH)

        mine = [pltpu.make_async_copy(ins[a], slot(outs[a], *me), local_sem.at[a]) for a in range(na)]
        first = []
        for a in range(na):
            first.append(copy(a, 0, me, sibling, src=ins[a]))
            first += [copy(a, 1 + j, me, (*chip, c), src=ins[a])
                      for j, chip in enumerate(chips[:2] if relay_diagonal else chips)]
        return copy, mine, first, me, sibling, chips, c

    def start(ins, outs, sems):
        _, mine, first, *_ = copies(ins, outs, sems)
        for cp in mine + first:
            cp.start()

    def finish(ins, outs, sems):
        copy, mine, first, me, sibling, chips, c = copies(ins, outs, sems)
        passed = []
        for j, chip in enumerate(chips[:2] if relay_diagonal else chips):
            for a in range(na):
                copy(a, 1 + j, (*chip, c), me).wait_recv()
                onward = [copy(a, 4 + j, (*chip, c), sibling)]
                if relay_diagonal:
                    onward.insert(0, copy(a, (3, 7)[j], (*chip, c), (*chips[1 - j], c), half=j))
                for cp in onward:
                    cp.start()
                passed += onward
        if relay_diagonal:
            for a in range(na):
                copy(a, 3, (*chips[2], c), me, half=0).wait_recv()
                copy(a, 7, (*chips[2], c), me, half=1).wait_recv()
                cp = copy(a, 6, (*chips[2], c), sibling)
                cp.start()
                passed.append(cp)
        for a in range(na):
            copy(a, 0, sibling, me).wait_recv()
            for j, chip in enumerate(chips):
                copy(a, 4 + j, (*chip, 1 - c), me).wait_recv()
        for cp in first + passed:
            cp.wait_send()
        for cp in mine:
            cp.wait()

    return Hosted(list(blocks), [jax.ShapeDtypeStruct((8,) + b.shape, b.dtype) for b in blocks],
                  [pltpu.SemaphoreType.DMA((na, 8)), pltpu.SemaphoreType.DMA((na, 8)), pltpu.SemaphoreType.DMA((na,))],
                  start, finish)


def chips3(arrays):
    na = len(arrays)

    def copies(ins, outs, sems):
        send_sems, recv_sems = sems
        x, y, c = _place()
        return [pltpu.make_async_remote_copy(
            src_ref=ins[a].at[2 * px + py], dst_ref=outs[a].at[k], send_sem=send_sems.at[a, k],
            recv_sem=recv_sems.at[a, k], device_id=(px, py, c), device_id_type=MESH)
            for a in range(na) for k, (px, py) in enumerate([(1 - x, y), (x, 1 - y), (1 - x, 1 - y)])]

    def start(ins, outs, sems):
        for cp in copies(ins, outs, sems):
            cp.start()

    def finish(ins, outs, sems):
        for cp in copies(ins, outs, sems):
            cp.wait()

    return Hosted(list(arrays), [jax.ShapeDtypeStruct((3,) + a.shape[1:], a.dtype) for a in arrays],
                  [pltpu.SemaphoreType.DMA((na, 3)), pltpu.SemaphoreType.DMA((na, 3))], start, finish)


def siblings(arrays):
    na = len(arrays)

    def copies(ins, outs, sems):
        send_sems, recv_sems = sems
        x, y, c = _place()
        return [pltpu.make_async_remote_copy(
            src_ref=ins[a], dst_ref=outs[a], send_sem=send_sems.at[a], recv_sem=recv_sems.at[a],
            device_id=(x, y, 1 - c), device_id_type=MESH) for a in range(na)]

    def start(ins, outs, sems):
        for cp in copies(ins, outs, sems):
            cp.start()

    def finish(ins, outs, sems):
        for cp in copies(ins, outs, sems):
            cp.wait()

    return Hosted(list(arrays), [jax.ShapeDtypeStruct(a.shape, a.dtype) for a in arrays],
                  [pltpu.SemaphoreType.DMA((na,)), pltpu.SemaphoreType.DMA((na,))], start, finish)


def both(first, second):
    na, no, ns = len(first.args), len(first.out_shape), len(first.scratch)

    def start(ins, outs, sems):
        first.start(ins[:na], outs[:no], sems[:ns])
        second.start(ins[na:], outs[no:], sems[ns:])

    def finish(ins, outs, sems):
        first.finish(ins[:na], outs[:no], sems[:ns])
        second.finish(ins[na:], outs[no:], sems[ns:])

    return Hosted(first.args + second.args, first.out_shape + second.out_shape, first.scratch + second.scratch,
                  start, finish)


def siblings4(arrays):
    na = len(arrays)

    def copies(ins, outs, sems):
        send_sems, recv_sems = sems
        x, y, c = _place()
        return [pltpu.make_async_remote_copy(
            src_ref=ins[a].at[2 * j + 1 - c], dst_ref=outs[a].at[j],
            send_sem=send_sems.at[a, j], recv_sem=recv_sems.at[a, j],
            device_id=(x, y, 1 - c), device_id_type=MESH) for a in range(na) for j in range(4)]

    def start(ins, outs, sems):
        for cp in copies(ins, outs, sems):
            cp.start()

    def finish(ins, outs, sems):
        for cp in copies(ins, outs, sems):
            cp.wait()

    return Hosted(list(arrays), [jax.ShapeDtypeStruct((4,) + a.shape[1:], a.dtype) for a in arrays],
                  [pltpu.SemaphoreType.DMA((na, 4)), pltpu.SemaphoreType.DMA((na, 4))], start, finish)


def _row_tile(r):
    for cand in (512, 256, 128, 64, 32, 16, 8):
        if r % cand == 0:
            return cand
    return r


def chip_partial(place, g8s, landed4s, name):
    n = len(g8s)

    def body(place_ref, *refs):
        del place_ref
        for g_ref, l_ref, o_ref in zip(refs[:n], refs[n:2 * n], refs[2 * n:]):
            o_ref[...] = (g_ref[...].astype(F32) + l_ref[...].astype(F32)).astype(BF16)

    own = [pl.BlockSpec((None,) + g.shape[1:], lambda j, s: (2 * j + s[0], 0, 0)) for g in g8s]
    plain = [pl.BlockSpec((None,) + g.shape[1:], lambda j, s: (j, 0, 0)) for g in g8s]
    return pl.pallas_call(
        body, name=name,
        grid_spec=pltpu.PrefetchScalarGridSpec(num_scalar_prefetch=1, grid=(4,), in_specs=own + plain, out_specs=plain),
        out_shape=[jax.ShapeDtypeStruct((4,) + g.shape[1:], BF16) for g in g8s],
    )(place, *g8s, *landed4s)


def shard_sum(place, partial4s, landed3s, name):
    n = len(partial4s)

    def body(place_ref, *refs):
        del place_ref
        for p_ref, l_ref, o_ref in zip(refs[:n], refs[n:2 * n], refs[2 * n:]):
            acc = p_ref[...].astype(F32)
            for k in range(3):
                acc = acc + l_ref[k].astype(F32)
            o_ref[...] = acc

    def halves(p, lead):
        r, ccols = p.shape[1:]
        return (lead, r // 2, ccols)

    return pl.pallas_call(
        body, name=name,
        grid_spec=pltpu.PrefetchScalarGridSpec(
            num_scalar_prefetch=1, grid=(2,),
            in_specs=[pl.BlockSpec(halves(p, None), lambda i, s: (s[1], i, 0)) for p in partial4s]
            + [pl.BlockSpec(halves(p, 3), lambda i, s: (0, i, 0)) for p in partial4s],
            out_specs=[pl.BlockSpec(halves(p, None)[1:], lambda i, s: (i, 0)) for p in partial4s]),
        out_shape=[jax.ShapeDtypeStruct(p.shape[1:], F32) for p in partial4s],
    )(place, *partial4s, *landed3s)


def _adamw_math(w, g, m, v):
    m2 = B1 * m + (1.0 - B1) * g
    v2 = B2 * v + (1.0 - B2) * (g * g)
    m_hat = m2 / (1.0 - B1 ** STEP)
    v_hat = v2 / (1.0 - B2 ** STEP)
    return -LR * (m_hat / (jnp.sqrt(v_hat) + AEPS) + WD * w), m2, v2


def adamw_halves(place, w, mine, theirs, m, v, name):
    r, ccols = w.shape
    hr = r // 2
    tr = _row_tile(hr)
    nt = hr // tr

    def body(place_ref, w_ref, a_ref, b_ref, m_ref, v_ref, g_out, d_out, m_out, v_out):
        g = jnp.where(pl.program_id(0) == place_ref[0], a_ref[...], b_ref[...])
        d, m2, v2 = _adamw_math(w_ref[...], g, m_ref[...], v_ref[...])
        g_out[...] = g
        d_out[...] = d
        m_out[...] = m2
        v_out[...] = v2

    full = pl.BlockSpec((tr, ccols), lambda h, i, s: (h * nt + i, 0))
    part = pl.BlockSpec((tr, ccols), lambda h, i, s: (i, 0))
    return pl.pallas_call(
        body, name=name,
        grid_spec=pltpu.PrefetchScalarGridSpec(
            num_scalar_prefetch=1, grid=(2, nt), in_specs=[full, part, part, full, full], out_specs=[full] * 4),
        out_shape=[jax.ShapeDtypeStruct((r, ccols), F32)] * 4,
    )(place, w, mine, theirs, m, v)


def adamw_group(place, halved, plain, hosted, name):
    rows = halved[0][0].shape[0]
    tr = 128
    nt = rows // 2 // tr
    nh, npl = len(halved), len(plain)

    def body(place_ref, *refs):
        own_in, h_in, own_out, h_out, _, h_sems = hosted.split(refs, 5 * nh + 4 * npl, 4 * nh + 3 * npl)
        half = pl.program_id(0)
        grid_step = half * nt + pl.program_id(1)

        @pl.when(grid_step == 0)
        def _():
            hosted.start(h_in, h_out, h_sems)

        for i in range(nh):
            w_ref, a_ref, b_ref, m_ref, v_ref = own_in[5 * i:5 * i + 5]
            g = jnp.where(half == place_ref[0], a_ref[...], b_ref[...])
            res = (g,) + _adamw_math(w_ref[...], g, m_ref[...], v_ref[...])
            for o_ref, r in zip(own_out[4 * i:4 * i + 4], res):
                o_ref[...] = r
        for i in range(npl):
            w_ref, g_ref, m_ref, v_ref = own_in[5 * nh + 4 * i:5 * nh + 4 * i + 4]
            res = _adamw_math(w_ref[...], g_ref[...], m_ref[...], v_ref[...])
            for o_ref, r in zip(own_out[4 * nh + 3 * i:4 * nh + 3 * i + 3], res):
                o_ref[...] = r

        @pl.when(grid_step == 2 * nt - 1)
        def _():
            hosted.finish(h_in, h_out, h_sems)

    def full(cols):
        return pl.BlockSpec((tr, cols), lambda h, i, s: (h * nt + i, 0))

    def part(cols):
        return pl.BlockSpec((tr, cols), lambda h, i, s: (i, 0))

    in_specs, out_specs, out_shape, args = [], [], [], []
    for w, a, b, m, v in halved:
        cols = w.shape[1]
        in_specs += [full(cols), part(cols), part(cols), full(cols), full(cols)]
        out_specs += [full(cols)] * 4
        out_shape += [jax.ShapeDtypeStruct(w.shape, F32)] * 4
        args += [w, a, b, m, v]
    for w, g, m, v in plain:
        cols = w.shape[1]
        in_specs += [full(cols)] * 4
        out_specs += [full(cols)] * 3
        out_shape += [jax.ShapeDtypeStruct(w.shape, F32)] * 3
        args += [w, g, m, v]
    h_in_specs, h_out_specs = hosted.specs()
    return pl.pallas_call(
        body, name=name,
        grid_spec=pltpu.PrefetchScalarGridSpec(
            num_scalar_prefetch=1, grid=(2, nt), in_specs=in_specs + h_in_specs, out_specs=out_specs + h_out_specs,
            scratch_shapes=hosted.scratch),
        out_shape=out_shape + hosted.out_shape,
        compiler_params=_cp(("arbitrary", "arbitrary")),
    )(place, *args, *hosted.args)


def _silu(x):
    return x * jax.nn.sigmoid(x)


def prologue(c_rows, c_ctx_row, w_ada, b_shard, rpb_flat, half_w_in, late_shards):
    shape = jax.ShapeDtypeStruct
    n_late = len(late_shards)
    half_shapes = [(w.shape[0] // 2, w.shape[1]) for w in late_shards]
    g_w = gather8([half_w_in], relay_diagonal=True)
    g_c = gather8([shape((8, D), F32)])
    g_m = chips3([shape((4, 8, 1536), F32)])

    def body(*refs):
        c_ref, cc_ref, w_ref, b_ref, flat_ref, hw_ref = refs[:6]
        late_refs = refs[6:6 + n_late]
        cin_ref, mg_ref, gw_ref, bias_ref, cos_ref, sin_ref = refs[6 + n_late:12 + n_late]
        rest = refs[12 + n_late:]
        half_refs, (cg_s, ms_s, bias_s, need_s, landed_s) = rest[:n_late], rest[n_late:n_late + 5]
        stage, (load_sem, bias_sem), sems = rest[n_late + 5:2 * n_late + 5], rest[2 * n_late + 5:2 * n_late + 7], \
            rest[2 * n_late + 7:]
        sw, sc, sm = sems[0:3], sems[3:6], sems[6:8]
        px, py, core = _place()
        g_c.start([c_ref], [cg_s], sc)
        g_w.start([hw_ref], [gw_ref], sw)
        loads = [pltpu.make_async_copy(late_refs[a].at[pl.ds(core * half_shapes[a][0], half_shapes[a][0]), :],
                                       stage[a], load_sem.at[a]) for a in range(n_late)]
        for cp in loads:
            cp.start()
        g_c.finish([c_ref], [cg_s], sc)
        cin_ref[...] = jnp.zeros_like(cin_ref)
        for dev in range(8):
            cin_ref[2 * dev:2 * dev + 2, :] = cg_s[dev, 0:2, :]
        cin_ref[16:17, :] = cc_ref[...]
        need_s[...] = jnp.zeros_like(need_s)
        for j in range(4):
            need_s[8 * j:8 * j + 2, :] = cg_s[2 * j + core, 0:2, :]
            need_s[8 * j + 2:8 * j + 3, :] = cc_ref[...]
        ms_s[...] = (_nn(_silu(need_s[...]), w_ref[...]) + b_ref[...]).reshape(4, 8, 1536)
        g_m.start([ms_s], [landed_s], sm)
        for a, cp in enumerate(loads):
            cp.wait()
            half_refs[a][...] = stage[a][...].astype(BF16)
        cos_ref[...], sin_ref[...] = _rope_tables()
        stores = []
        for pair in range(NPAIR):
            if pair >= 2:
                stores[pair - 2].wait()
            _na_bias_pair(flat_ref.at[pair], bias_s.at[pair % 2])
            stores.append(pltpu.make_async_copy(bias_s.at[pair % 2], bias_ref.at[pair], bias_sem.at[pair % 2]))
            stores[pair].start()
        for cp in stores[-2:]:
            cp.wait()
        g_w.finish([hw_ref], [gw_ref], sw)
        g_m.finish([ms_s], [landed_s], sm)
        mg_ref[2 * px + py] = ms_s[2 * px + py]
        for k, (qx, qy) in enumerate([(1 - px, py), (px, 1 - py), (1 - px, 1 - py)]):
            mg_ref[2 * qx + qy] = landed_s[k]

    vmem = pl.BlockSpec(memory_space=pltpu.VMEM)
    hbm = pl.BlockSpec(memory_space=pl.ANY)
    return pl.pallas_call(
        body, name="prologue", in_specs=[vmem, vmem, vmem, vmem, vmem, hbm] + [hbm] * n_late,
        out_specs=[vmem, vmem, hbm, hbm, vmem, vmem] + [vmem] * n_late,
        out_shape=[shape((32, D), F32), shape((4, 8, 1536), F32)] + g_w.out_shape
        + [shape((NPAIR,) + NA_BIAS_SHAPE, F32)] + [shape((SEQ, RD), F32)] * 2 + [shape(s, BF16) for s in half_shapes],
        scratch_shapes=[pltpu.VMEM((8, 8, D), F32), pltpu.VMEM((4, 8, 1536), F32), pltpu.VMEM((2,) + NA_BIAS_SHAPE, F32),
                        pltpu.VMEM((32, D), F32), pltpu.VMEM((3, 8, 1536), F32)]
        + [pltpu.VMEM(s, F32) for s in half_shapes]
        + [pltpu.SemaphoreType.DMA((n_late,)), pltpu.SemaphoreType.DMA((2,))]
        + g_w.scratch + g_c.scratch + g_m.scratch,
        compiler_params=_cp(),
    )(c_rows, c_ctx_row, w_ada, b_shard, rpb_flat, half_w_in, *late_shards)


def ada_grads(cin, gb, gc, w_ada):
    def body(c_ref, gb_ref, gc_ref, w_ref, gw_ref, pc_ref):
        ctx_tot = jnp.sum(gc_ref[...], axis=0, keepdims=True)
        rows = lax.broadcasted_iota(jnp.int32, (16, 512), 0)
        dm = jnp.concatenate([gb_ref[...], jnp.where(rows == 0, ctx_tot, 0.0)], axis=0)
        gw_ref[...] = _tn(_silu(c_ref[...]), dm)
        rows8 = lax.broadcasted_iota(jnp.int32, (8, 512), 0)
        part = _nt(jnp.where(rows8 == 0, ctx_tot, 0.0), w_ref[...])

        @pl.when(pl.program_id(0) == 0)
        def _():
            pc_ref[...] = jnp.zeros_like(pc_ref)

        pc_ref[...] += part

    return pl.pallas_call(
        body, name="ada_grads", grid=(3,),
        in_specs=[pl.BlockSpec((32, D), lambda j: (0, 0)), pl.BlockSpec((16, 512), lambda j: (0, j)),
                  pl.BlockSpec((8, 512), lambda j: (0, j)), pl.BlockSpec((D, 512), lambda j: (0, j))],
        out_specs=[pl.BlockSpec((D, 512), lambda j: (0, j)), pl.BlockSpec((8, D), lambda j: (0, 0))],
        out_shape=[jax.ShapeDtypeStruct((D, 1536), F32), jax.ShapeDtypeStruct((8, D), F32)],
    )(cin, gb, gc, w_ada)


SMALL_SUM_ROWS = 15


def small_update(gsm, gbf, gcf, pcg, params):
    n = len(params)

    def body(*refs):
        gsm_ref, gbf_ref, gcf_ref, pcg_ref = refs[:4]
        wmv, outs, loss_out = refs[4:4 + 3 * n], refs[4 + 3 * n:4 + 7 * n], refs[-1]
        acc = gsm_ref[0]
        for dev in range(1, 8):
            acc = acc + gsm_ref[dev]
        c_ctx = wmv[0][...]
        sg = jax.nn.sigmoid(c_ctx)
        dsilu = pcg_ref[0:1, :] + pcg_ref[2:3, :] + pcg_ref[4:5, :] + pcg_ref[6:7, :]
        lane = lax.broadcasted_iota(jnp.int32, (1, D), 1)
        last = acc[14:15, :]
        grads = [
            dsilu * (sg * (1.0 + c_ctx * (1.0 - sg))),
            jnp.sum(gbf_ref[...], axis=0, keepdims=True) + jnp.sum(gcf_ref[...], axis=0, keepdims=True),
            acc[0:1, :] + acc[1:2, :], acc[2:3, :], acc[3:4, :], acc[4:5, :],
            acc[5:6, 0:512], acc[6:14, :], jnp.where(lane < 8, last, 0.0),
        ]
        loss_out[...] = jnp.broadcast_to(jnp.sum(jnp.where(lane == 8, last, 0.0), axis=1, keepdims=True), (8, 128))
        for i, g in enumerate(grads):
            d, m2, v2 = _adamw_math(wmv[3 * i][...], g, wmv[3 * i + 1][...], wmv[3 * i + 2][...])
            outs[4 * i][...] = g
            outs[4 * i + 1][...] = d
            outs[4 * i + 2][...] = m2
            outs[4 * i + 3][...] = v2

    flat = [a for wmv in params for a in wmv]
    out_shape = [jax.ShapeDtypeStruct(w.shape, F32) for w, _, _ in params for _ in range(4)]
    return pl.pallas_call(
        body, name="small_update", out_shape=out_shape + [jax.ShapeDtypeStruct((8, 128), F32)],
    )(gsm, gbf, gcf, pcg, *flat)


def _pad_row(v, rows):
    flat = v.reshape(-1)
    return jnp.pad(flat, (0, rows * D - flat.shape[0])).reshape(rows, D)


def local_step(x, ctx, tgt, mod3, rope, bias, g_pre_mix, g_post_mix, g_pre_mlp, g_post_mlp, ret_decay, ret_gn,
               wperm, late_weights, early_grads):
    nb = x.shape[0]
    tokens = nb * SEQ
    cos, sin = rope
    rd = ret_decay.T.reshape(RH, 2, 1)
    gn = ret_gn.reshape(RH, 1, RD)
    h, pret, pna = premix_proj(x, mod3, g_pre_mix, wperm, False, "premix_proj")
    hc, pretc, pnac = premix_proj(ctx, mod3, g_pre_mix, wperm, True, "premix_proj_ctx")
    o_all, mixin, gw_out = retention_fwd(pret, pretc, rd, gn, cos, sin, late_weights(0))
    mixin, gw1, gw2 = na_fwd(pna, pnac, bias, mixin, late_weights(1))
    dx_tail, dmix, h2, du, act, dm, dmixin, dmod_t, dg_t, loss_t = tail_fwd_bwd(
        x, mixin, tgt, mod3, g_post_mix, g_pre_mlp, g_post_mlp, gw_out.reshape(D, D), gw1.reshape(4, D, D),
        gw2.reshape(DFF, D))
    dw_out = weight_grad([(mixin.reshape(tokens, D), dmix.reshape(tokens, D))], "grad_w_out", BF16)
    dw1 = weight_grad([(h2.reshape(tokens, D), du.reshape(tokens, DFF))], "grad_w_mlp1", BF16, col_blocks=True)
    dw2 = weight_grad([(act.reshape(tokens, DFF), dm.reshape(tokens, D))], "grad_w_mlp2", BF16)
    dproj, dprojc, drd, dgn, *landed = retention_bwd(pret, pretc, o_all, dmixin, rd, gn, cos, sin,
                                                     early_grads[0](dw_out, dw1, dw2))
    dproj, dprojc, dpat, *early = na_bwd(pna, pnac, bias, dmixin, dproj, dprojc, early_grads[1](landed))
    dw_in = weight_grad([(h.reshape(tokens, D), dproj.reshape(tokens, IN_W)),
                         (hc.reshape(nb * LC, D), dprojc.reshape(nb * LC, IN_W))], "grad_w_in", BF16, tk=512,
                        w_in_blocks=True)
    dmod_c, dg_c, *late = premix_bwd(ctx, mod3, g_pre_mix, wperm, dprojc, None, early_grads[2](dw_in), "premix_bwd_ctx")
    grad_x, dmod_a, dg_a, *late = premix_bwd(x, mod3, g_pre_mix, wperm, dproj, dx_tail, early_grads[3](late),
                                             "premix_bwd")
    dmod = jnp.concatenate([jnp.concatenate([dmod_a[:, 0:2], dmod_t[:, 2:6]], axis=1), dmod_c], axis=0)
    last = jnp.pad(jnp.concatenate([drd[:, :, 0].T.reshape(8), loss_t[0, 0:1]]), (0, D - 9)).reshape(1, D)
    small = jnp.concatenate([dg_a[0:1], dg_c[0:1], dg_t[0:3], _pad_row(dgn, 1), dpat.reshape(8, D), last], axis=0)
    return grad_x, late, early, dmod, small


def kernel(x, c, ctx, c_ctx, w_ada, b_ada, g_pre_mix, g_post_mix, g_pre_mlp, g_post_mlp, w_in, ret_decay, ret_gn, na_rpb, w_out, w_mlp1, w_mlp2, loss_target, m_c_ctx, m_w_ada, m_b_ada, m_g_pre_mix, m_g_post_mix, m_g_pre_mlp, m_g_post_mlp, m_w_in, m_ret_decay, m_ret_gn, m_na_rpb, m_w_out, m_w_mlp1, m_w_mlp2, v_c_ctx, v_w_ada, v_b_ada, v_g_pre_mix, v_g_post_mix, v_g_pre_mlp, v_g_post_mlp, v_w_in, v_ret_decay, v_ret_gn, v_na_rpb, v_w_out, v_w_mlp1, v_w_mlp2):
    px, py, pc = _place()
    chip = 2 * px + py

    half_w_in = lax.dynamic_slice_in_dim(w_in[0], pc * (D // 2), D // 2, 0).astype(BF16)
    cin, mg, gw_in, bias, cos, sin, *late_halves = prologue(
        jnp.pad(c, ((0, 6), (0, 0))), c_ctx[None], w_ada[0], lax.dynamic_slice_in_dim(b_ada, chip * 1536, 1536, 1),
        _rpb_flat(na_rpb[0]), half_w_in, [w_out[0], w_mlp1[0], w_mlp2[0]])
    halves = [half_w_in] + late_halves
    wperm = unpack_w_in(gw_in.reshape(4, D, 896))
    mod3 = mg[:, 0:3].transpose(1, 0, 2).reshape(3, 6, D)

    place = jnp.stack([pc, chip]).astype(jnp.int32)

    early_names = ["w_out", "w_mlp1", "w_mlp2"]
    early_g8, early_partial = [], []

    def early_a(dw_out, dw1, dw2):
        early_g8[:] = [dw_out.reshape(8, 128, D), dw1.reshape(8, 512, D), dw2.reshape(8, 512, D)]
        return siblings4(early_g8)

    def early_b(landed):
        early_partial[:] = chip_partial(place, early_g8, landed, "rs_chip_sum_early")
        return chips3(early_partial)

    late_partial = []

    late_g8 = []

    def late_c(dw_in):
        late_g8[:] = [dw_in.reshape(8, 512, 896)]
        return siblings4(late_g8)

    def late_d(landed):
        late_partial[:] = chip_partial(place, late_g8, landed, "rs_chip_sum_w_in")
        return chips3(late_partial)

    grad_x, (landed3_in,), early_landed, dmod, small = local_step(
        x, ctx, loss_target, mod3, (cos, sin), bias, g_pre_mix, g_post_mix, g_pre_mlp, g_post_mlp, ret_decay[0], ret_gn,
        wperm, lambda k: gather8(halves[1:2] if k == 0 else halves[2:4]), (early_a, early_b, late_c, late_d))
    early_mine = shard_sum(place, early_partial, early_landed, "rs_shard_sum_early")

    pay = jnp.concatenate([dmod.reshape(18, D), small, jnp.zeros((40 - 18 - SMALL_SUM_ROWS, D), F32)], axis=0)
    *early_theirs, gs = run_hosted(both(siblings(early_mine), gather8([pay])), "rs_halves_early_gather_small")
    gbf = gs[:, 0:12].reshape(16, 6 * D)
    gcf = gs[:, 12:18].reshape(8, 6 * D)
    gw_ada, pc_part = ada_grads(cin, lax.dynamic_slice_in_dim(gbf, chip * 1536, 1536, 1),
                                lax.dynamic_slice_in_dim(gcf, chip * 1536, 1536, 1), w_ada[0])
    (mine_in,) = shard_sum(place, late_partial, [landed3_in], "rs_shard_sum_w_in")
    theirs_in, pcg = run_hosted(both(siblings([mine_in]), gather8([pc_part])), "rs_halves_w_in_gather_c_ctx")

    grouped = adamw_group(
        place,
        [(w_mlp1[0], early_mine[1], early_theirs[1], m_w_mlp1[0], v_w_mlp1[0]),
         (w_mlp2[0], early_mine[2], early_theirs[2], m_w_mlp2[0], v_w_mlp2[0])],
        [(w_ada[0], gw_ada, m_w_ada[0], v_w_ada[0])], no_exchange(), "adamw_group")
    d_ada, m_ada, v_ada = grouped[8:11]
    big = [
        [r[None] for r in adamw_halves(place, w_in[0], mine_in, theirs_in, m_w_in[0], v_w_in[0], "adamw_w_in")],
        [r[None] for r in adamw_halves(place, w_out[0], early_mine[0], early_theirs[0], m_w_out[0], v_w_out[0],
                                       "adamw_w_out")],
        [r[None] for r in grouped[0:4]], [r[None] for r in grouped[4:8]],
    ]

    def rpb_rows(t):
        return _rpb_flat(t[0]).reshape(8, D)

    def decay_row(t):
        return jnp.pad(t.reshape(1, 8), ((0, 0), (0, D - 8)))

    views = [lambda t: t.reshape(1, D), lambda t: t, lambda t: t, lambda t: t, lambda t: t, lambda t: t, lambda t: t,
             rpb_rows, decay_row]
    back = [lambda t: t.reshape(D), lambda t: t, lambda t: t, lambda t: t, lambda t: t, lambda t: t, lambda t: t,
            lambda t: _rpb_flat_t(t)[None], lambda t: t[:, 0:8].reshape(1, 2, 4)]
    small_w = (c_ctx, b_ada, g_pre_mix, g_post_mix, g_pre_mlp, g_post_mlp, ret_gn, na_rpb, ret_decay)
    small_m = (m_c_ctx, m_b_ada, m_g_pre_mix, m_g_post_mix, m_g_pre_mlp, m_g_post_mlp, m_ret_gn, m_na_rpb, m_ret_decay)
    small_v = (v_c_ctx, v_b_ada, v_g_pre_mix, v_g_post_mix, v_g_pre_mlp, v_g_post_mlp, v_ret_gn, v_na_rpb, v_ret_decay)
    *res, loss8 = small_update(gs[:, 18:18 + SMALL_SUM_ROWS], gbf, gcf, pcg[:, 0],
                               [(f(w), f(m), f(v)) for f, w, m, v in zip(views, small_w, small_m, small_v)])

    def leaves(ada, idx):
        s_c, s_b, s_g1, s_g2, s_g3, s_g4, s_gn, s_rpb, s_rd = [back[i](res[4 * i + idx]) for i in range(9)]
        return [s_c, ada[None], s_b, s_g1, s_g2, s_g3, s_g4, big[0][idx], s_rd, s_gn, s_rpb,
                big[1][idx], big[2][idx], big[3][idx]]

    return (loss8[0, 0], grad_x, *leaves(gw_ada, 0), *leaves(d_ada, 1), *leaves(m_ada, 2), *leaves(v_ada, 3))
```

```python
import functools
import math

import jax
import jax.numpy as jnp
from jax import lax
from jax.experimental import pallas as pl
from jax.experimental.pallas import tpu as pltpu

F32, BF16 = jnp.float32, jnp.bfloat16
D = 1024
SEQ = 2048
LC = 256
GW = 64
RH, RD, CH = 4, 128, 128
NPAIR = 4
IN_W = 3584
RET_W = 2048
DFF = 4096
EPS = 1e-6
NEG = -1e30
TN = 256
NCH = SEQ // CH
LR, B1, B2, AEPS, WD, STEP = 0.001, 0.9, 0.999, 1e-08, 0.01, 10
MESH = pl.DeviceIdType.MESH
VMEM_LIMIT = 56 * 1024 * 1024


def _cp(sem=None):
    return pltpu.CompilerParams(dimension_semantics=sem, vmem_limit_bytes=VMEM_LIMIT)


def _nn(a, b):
    return jnp.dot(a.astype(BF16), b.astype(BF16), preferred_element_type=F32)


def _nt(a, b):
    return lax.dot_general(a.astype(BF16), b.astype(BF16), (((1,), (1,)), ((), ())), preferred_element_type=F32)


def _tn(a, b):
    return lax.dot_general(a.astype(BF16), b.astype(BF16), (((0,), (0,)), ((), ())), preferred_element_type=F32)


@jax.custom_vjp
def mm_tn(a, b):
    return _tn(a, b)


mm_tn.defvjp(lambda a, b: (_tn(a, b), (a, b)), lambda r, g: (_nt(r[1], g), _nn(r[0], g)))


def _rms(x):
    return x * lax.rsqrt(jnp.mean(x * x, axis=-1, keepdims=True) + EPS)


def _rms_mod(x, g, sc, sh):
    return (_rms(x) * g) * (1.0 + sc) + sh


def _post_mix(x, mix, gt1, sc2, sh2, g_post_mix, g_pre_mlp):
    x1 = x + gt1 * (_rms(mix) * g_post_mix)
    return x1, _rms_mod(x1, g_pre_mlp, sc2, sh2)


def _head_loss(x1, m, gt2, g_post_mlp, tgt):
    err = x1 + gt2 * (_rms(m) * g_post_mlp) - tgt
    return 0.5 * jnp.sum(jnp.mean(err * err, axis=-1, keepdims=True), axis=0, keepdims=True)


def _ln_gate(o, g, w):
    mu = jnp.mean(o, axis=-1, keepdims=True)
    var = jnp.mean(jnp.square(o - mu), axis=-1, keepdims=True)
    y = (o - mu) * lax.rsqrt(var + EPS)
    return (y * w) * (g * jax.nn.sigmoid(g))


def _pair_order(x):
    lane = lax.broadcasted_iota(jnp.int32, x.shape, 1)
    return jnp.where((lane >= 32) & (lane < 64), pltpu.roll(x, 96, 1),
                     jnp.where((lane >= 64) & (lane < 96), pltpu.roll(x, 32, 1), x))


def _rope(x, cos, sin):
    return x * cos + pltpu.roll(x, 64, 1) * sin


def _rope_t(g, cos, sin):
    return g * cos + pltpu.roll(g * sin, 64, 1)


def _rope_tables():
    tok = lax.broadcasted_iota(jnp.int32, (SEQ, RD), 0)
    lane = lax.broadcasted_iota(jnp.int32, (SEQ, RD), 1)
    pos = jnp.where((lane & 32) == 0, tok >> 6, tok & (GW - 1)).astype(F32)
    ang = pos * jnp.exp((lane & 31).astype(F32) * (-math.log(10000.0) / 32))
    return jnp.cos(ang), jnp.where(lane < 64, -jnp.sin(ang), jnp.sin(ang))


def _chunk_loop(n, body, init, k=4):
    def several(t, carry):
        for i in range(k):
            carry = body(k * t + i, carry)
        return carry

    return lax.fori_loop(0, n // k, several, init)


def _fiota(shape, dim):
    return lax.broadcasted_iota(jnp.int32, shape, dim).astype(F32)


def _ret_state(k, v, s, lg, reverse):
    pos = _fiota((CH, 1), 0)
    b_exp = pos if reverse else (CH - 1.0 - pos)
    return jnp.exp(lg * CH) * s + mm_tn(k * jnp.exp(lg * b_exp), v)


class _Decays:
    def __init__(self, lgs):
        i, j, pos = _fiota((CH, CH), 0), _fiota((CH, CH), 1), _fiota((CH, 1), 0)
        diffs = (i - j, j - i)
        keep = (diffs[0] >= 0, diffs[1] > 0)
        mats = [jnp.where(m, jnp.exp(lg * jnp.where(m, d, 0.0)), 0.0) for lg, d, m in zip(lgs, diffs, keep)]
        self.mask = mats[0] + mats[1]
        self.dmask = [mats[0] * diffs[0], mats[1] * diffs[1]]
        a_exp, b_exp = (pos + 1.0, CH - pos), (CH - 1.0 - pos, pos)
        self.a = [jnp.exp(lg * e) for lg, e in zip(lgs, a_exp)]
        self.b = [jnp.exp(lg * e) for lg, e in zip(lgs, b_exp)]
        self.da = [a * e for a, e in zip(self.a, a_exp)]
        self.db = [b * e for b, e in zip(self.b, b_exp)]
        self.g = [jnp.exp(lg * CH) for lg in lgs]


def _both(x, w):
    return jnp.concatenate([x * w[0], x * w[1]], axis=1)


def _total(x):
    return jnp.sum(jnp.sum(x, axis=1, keepdims=True), axis=0, keepdims=True)


def _state_pass(dec, init, k_s, v_of, st_s):
    def step(t, carry):
        out = []
        for d, s in enumerate(carry):
            n = (NCH - 1 - t) if d else t
            sl = pl.ds(pl.multiple_of(n * CH, CH), CH)
            st_s[n, d * RD:(d + 1) * RD, :] = s
            out.append(dec.g[d] * s + _tn(k_s[sl, :] * dec.b[d], v_of(sl)))
        return tuple(out)

    _chunk_loop(NCH, step, tuple(init))


def premix_proj(xin, mod3, g_pre, w, is_ctx, name, unpack=False):
    nb, length, _ = xin.shape
    tn = min(2 * TN, length)

    def body(x_ref, mod_ref, g_ref, w_ref, h_ref, pret_ref, pna_ref, *unpacked):
        if unpack:
            @pl.when((pl.program_id(0) == 0) & (pl.program_id(1) == 0))
            def _():
                _from_w_in_blocks(w_ref, unpacked[0])

        wp_ref = unpacked[0] if unpack else w_ref
        h = _rms_mod(x_ref[...], g_ref[...], mod_ref[1:2, :], mod_ref[0:1, :])
        hb = h.astype(BF16)
        h_ref[...] = hb
        pret_ref[...] = jnp.dot(hb, wp_ref[:, :RET_W], preferred_element_type=F32)
        pna_ref[...] = jnp.dot(hb, wp_ref[:, RET_W:], preferred_element_type=F32).astype(BF16)

    return pl.pallas_call(
        body, name=name, grid=(nb, length // tn),
        in_specs=[
            pl.BlockSpec((None, tn, D), lambda b, t: (b, t, 0)),
            pl.BlockSpec((None, 6, D), (lambda b, t: (2, 0, 0)) if is_ctx else (lambda b, t: (b, 0, 0))),
            pl.BlockSpec((1, D), lambda b, t: (0, 0)),
            pl.BlockSpec((4, D, 896), lambda b, t: (0, 0, 0), pipeline_mode=pl.Buffered(1)) if unpack else
            pl.BlockSpec((D, IN_W), lambda b, t: (0, 0), pipeline_mode=pl.Buffered(1)),
        ],
        out_specs=[
            pl.BlockSpec((None, tn, D), lambda b, t: (b, t, 0)),
            pl.BlockSpec((None, tn, RET_W), lambda b, t: (b, t, 0)),
            pl.BlockSpec((None, tn, IN_W - RET_W), lambda b, t: (b, t, 0)),
        ] + ([pl.BlockSpec((D, IN_W), lambda b, t: (0, 0))] if unpack else []),
        out_shape=[
            jax.ShapeDtypeStruct((nb, length, D), BF16),
            jax.ShapeDtypeStruct((nb, length, RET_W), F32),
            jax.ShapeDtypeStruct((nb, length, IN_W - RET_W), BF16),
        ] + ([jax.ShapeDtypeStruct((D, IN_W), BF16)] if unpack else []),
        compiler_params=_cp(("arbitrary", "arbitrary")),
    )(xin, mod3, g_pre, w)


def premix_bwd(xin, mod3, g_pre, wperm, dproj, dx_tail, hosted, name):
    nb, length, _ = xin.shape
    tn = min(TN, length)
    is_ctx = dx_tail is None

    def body(*refs):
        own_in, h_in, own_out, h_out, _, h_sems = hosted.split(refs, 5 if is_ctx else 6, 2 if is_ctx else 3)
        if is_ctx:
            (x_ref, mod_ref, g_ref, w_ref, dp_ref), (dmod_ref, dg_ref) = own_in, own_out
        else:
            (x_ref, mod_ref, g_ref, w_ref, dp_ref, dxt_ref), (dx_ref, dmod_ref, dg_ref) = own_in, own_out
        b, t = pl.program_id(0), pl.program_id(1)
        grid_step = b * (length // tn) + t

        @pl.when(grid_step == 0)
        def _():
            hosted.start(h_in, h_out, h_sems)

        @pl.when(grid_step == nb * (length // tn) - 1)
        def _():
            hosted.finish(h_in, h_out, h_sems)

        dh = lax.dot_general(dp_ref[...], w_ref[...], (((1,), (1,)), ((), ())), preferred_element_type=F32)
        _, vjp = jax.vjp(_rms_mod, x_ref[...], g_ref[...], mod_ref[1:2, :], mod_ref[0:1, :])
        dx, dg, dsc, dsh = vjp(dh)
        if not is_ctx:
            dx_ref[...] = dx + dxt_ref[...]

        @pl.when((t == 0) & ((b == 0) if is_ctx else True))
        def _():
            dmod_ref[...] = jnp.zeros_like(dmod_ref)

        @pl.when((t == 0) & (b == 0))
        def _():
            dg_ref[...] = jnp.zeros_like(dg_ref)

        dmod_ref[0:1, :] += dsh
        dmod_ref[1:2, :] += dsc
        dg_ref[0:1, :] += dg

    tok = lambda b, t: (b, t, 0)
    in_specs = [
        pl.BlockSpec((None, tn, D), tok),
        pl.BlockSpec((None, 6, D), (lambda b, t: (2, 0, 0)) if is_ctx else (lambda b, t: (b, 0, 0))),
        pl.BlockSpec((1, D), lambda b, t: (0, 0)),
        pl.BlockSpec((D, IN_W), lambda b, t: (0, 0), pipeline_mode=pl.Buffered(1)),
        pl.BlockSpec((None, tn, IN_W), tok),
    ]
    args = [xin, mod3, g_pre, wperm, dproj]
    out_specs = [
        pl.BlockSpec((None, 6, D), (lambda b, t: (0, 0, 0)) if is_ctx else (lambda b, t: (b, 0, 0))),
        pl.BlockSpec((8, D), lambda b, t: (0, 0)),
    ]
    out_shape = [jax.ShapeDtypeStruct((1 if is_ctx else nb, 6, D), F32), jax.ShapeDtypeStruct((8, D), F32)]
    if not is_ctx:
        in_specs.append(pl.BlockSpec((None, tn, D), tok))
        args.append(dx_tail)
        out_specs.insert(0, pl.BlockSpec((None, tn, D), tok))
        out_shape.insert(0, jax.ShapeDtypeStruct((nb, length, D), F32))
    h_in_specs, h_out_specs = hosted.specs()
    return pl.pallas_call(
        body, name=name, grid=(nb, length // tn), in_specs=in_specs + h_in_specs, out_specs=out_specs + h_out_specs,
        out_shape=out_shape + hosted.out_shape, scratch_shapes=hosted.scratch,
        compiler_params=_cp(("arbitrary", "arbitrary")),
    )(*args, *hosted.args)


def _ret_specs(order):
    def im(f):
        return lambda *g: f(*order(*g))
    return dict(
        pret=pl.BlockSpec((None, SEQ, 512), im(lambda b, h: (b, 0, h))),
        pretc=pl.BlockSpec((None, LC, 512), im(lambda b, h: (b, 0, h))),
        rd=pl.BlockSpec((None, 2, 1), im(lambda b, h: (h, 0, 0))),
        gn=pl.BlockSpec((None, 1, RD), im(lambda b, h: (h, 0, 0))),
        tab=pl.BlockSpec((SEQ, RD), im(lambda b, h: (0, 0))),
        head=pl.BlockSpec((None, SEQ, RD), im(lambda b, h: (b, 0, h))),
    )


def retention_fwd(pret, pretc, rd, gn, cos, sin, hosted):
    nb = pret.shape[0]
    sp = _ret_specs(lambda b, h: (b, h))

    def body(*refs):
        own_in, h_in, own_out, h_out, own_scr, h_sems = hosted.split(refs, 6, 2)
        p_ref, pc_ref, rd_ref, gn_ref, cos_ref, sin_ref = own_in
        (o_ref, mix_ref), (q_s, k_s, o_s, st_s) = own_out, own_scr
        grid_step = pl.program_id(0) * RH + pl.program_id(1)

        @pl.when(grid_step == 0)
        def _():
            hosted.start(h_in, h_out, h_sems)

        cos_v, sin_v = cos_ref[...], sin_ref[...]
        q_s[...] = _rope(p_ref[:, 0:128], cos_v, sin_v) * (RD ** -0.5)
        k_s[...] = _rope(p_ref[:, 128:256], cos_v, sin_v)
        lgs, init = [], []
        for rev in (False, True):
            lg = jax.nn.log_sigmoid(rd_ref[int(rev):int(rev) + 1, :])
            s = jnp.zeros((RD, RD), F32)
            for n in ((1, 0) if rev else (0, 1)):
                s = _ret_state(pc_ref[n * CH:(n + 1) * CH, 128:256], pc_ref[n * CH:(n + 1) * CH, 256:384], s, lg, rev)
            lgs.append(lg)
            init.append(s)

        dec = _Decays(lgs)
        _state_pass(dec, init, k_s, lambda sl: p_ref[sl, 256:384], st_s)

        def chunk(n, carry):
            sl = pl.ds(pl.multiple_of(n * CH, CH), CH)
            q = q_s[sl, :]
            o_s[sl, :] = (_nn(_nt(q, k_s[sl, :]) * dec.mask, p_ref[sl, 256:384]) + _nn(_both(q, dec.a), st_s[n]))
            return carry

        _chunk_loop(NCH, chunk, 0)
        o = o_s[...]
        o_ref[...] = o
        mix_ref[...] = _ln_gate(o, p_ref[:, 384:512], gn_ref[...]).astype(BF16)

        @pl.when(grid_step == nb * RH - 1)
        def _():
            hosted.finish(h_in, h_out, h_sems)

    h_in_specs, h_out_specs = hosted.specs()
    return pl.pallas_call(
        body, name="retention_fwd", grid=(nb, RH),
        in_specs=[sp["pret"], sp["pretc"], sp["rd"], sp["gn"], sp["tab"], sp["tab"]] + h_in_specs,
        out_specs=[sp["head"], sp["head"]] + h_out_specs,
        out_shape=[jax.ShapeDtypeStruct((nb, SEQ, RH * RD), F32), jax.ShapeDtypeStruct((nb, SEQ, D), BF16)]
        + hosted.out_shape,
        scratch_shapes=[pltpu.VMEM((SEQ, RD), F32)] * 3 + [pltpu.VMEM((NCH, 2 * RD, RD), F32)] + hosted.scratch,
        compiler_params=_cp(("arbitrary", "arbitrary")),
    )(pret, pretc, rd, gn, cos, sin, *hosted.args)


def retention_bwd(pret, pretc, o_all, dmixin, rd, gn, cos, sin, hosted):
    nb = pret.shape[0]
    sp = _ret_specs(lambda h, b: (b, h))

    def body(*refs):
        own_in, h_in, own_out, h_out, own_scr, h_sems = hosted.split(refs, 8, 4)
        p_ref, pc_ref, o_ref, dmix_ref, rd_ref, gn_ref, cos_ref, sin_ref = own_in
        dp_ref, dpc_ref, drd_ref, dgn_ref = own_out
        q_s, k_s, do_s, dq_s, dk_s, dv_s, st_s, gst_s = own_scr
        b = pl.program_id(1)
        grid_step = pl.program_id(0) * nb + b

        @pl.when(grid_step == 0)
        def _():
            hosted.start(h_in, h_out, h_sems)

        cos_v, sin_v = cos_ref[...], sin_ref[...]
        q_s[...] = _rope(p_ref[:, 0:128], cos_v, sin_v) * (RD ** -0.5)
        k_s[...] = _rope(p_ref[:, 128:256], cos_v, sin_v)
        _, gate_vjp = jax.vjp(_ln_gate, o_ref[...], p_ref[:, 384:512], gn_ref[...])
        do, dg, dgn = gate_vjp(dmix_ref[...].astype(F32))
        do_s[...] = do
        dp_ref[:, 384:512] = dg.astype(BF16)

        @pl.when(b == 0)
        def _():
            drd_ref[...] = jnp.zeros_like(drd_ref)
            dgn_ref[...] = jnp.zeros_like(dgn_ref)

        dgn_ref[...] += dgn
        kcs = [pc_ref[n * CH:(n + 1) * CH, 128:256] for n in (0, 1)]
        vcs = [pc_ref[n * CH:(n + 1) * CH, 256:384] for n in (0, 1)]
        dirs = []
        init = []
        for rev in (False, True):
            rdv = rd_ref[int(rev):int(rev) + 1, :]
            lg = jax.nn.log_sigmoid(rdv)
            order_c = (1, 0) if rev else (0, 1)
            s = jnp.zeros((RD, RD), F32)
            ctx_states = []
            for n in order_c:
                ctx_states.append(s)
                s = _ret_state(kcs[n], vcs[n], s, lg, rev)
            dirs.append((rev, order_c, lg, rdv, ctx_states))
            init.append(s)
        dec = _Decays([lg for _, _, lg, _, _ in dirs])

        def v_of(sl):
            return p_ref[sl, 256:384]

        _state_pass(dec, init, k_s, v_of, st_s)
        zeros = jnp.zeros((CH, RD), F32)

        def scores_back(n, carry):
            dmask_sum, da_f, da_b = carry
            sl = pl.ds(pl.multiple_of(n * CH, CH), CH)
            q, k, v, do = q_s[sl, :], k_s[sl, :], v_of(sl), do_s[sl, :]
            scores = _nt(q, k)
            d_att = _nt(do, v)
            d_scores = d_att * dec.mask
            d_qa = _nt(do, st_s[n])
            d_qf, d_qb = d_qa[:, 0:RD], d_qa[:, RD:2 * RD]
            dq_s[sl, :] = _nn(d_scores, k) + d_qf * dec.a[0] + d_qb * dec.a[1]
            dk_s[sl, :] = _tn(d_scores, q)
            dv_s[sl, :] = _tn(scores * dec.mask, do)
            gst_s[n] = _tn(_both(q, dec.a), do)
            return dmask_sum + d_att * scores, da_f + d_qf * q, da_b + d_qb * q

        dmask_sum, da_f, da_b = _chunk_loop(NCH, scores_back, (zeros, zeros, zeros))

        def state_back(t, carry):
            out = []
            for d, r in enumerate(carry):
                n = t if d else (NCH - 1 - t)
                rows = slice(d * RD, (d + 1) * RD)
                own = gst_s[n, rows, :]
                gst_s[n, rows, :] = r
                out.append(own + dec.g[d] * r)
            return tuple(out)

        d_states = _chunk_loop(NCH, state_back, (zeros, zeros))

        def updates_back(n, carry):
            db_f, db_b, dg_f, dg_b = carry
            sl = pl.ds(pl.multiple_of(n * CH, CH), CH)
            k, r, s = k_s[sl, :], gst_s[n], st_s[n]
            d_kw = _nt(v_of(sl), r)
            d_kf, d_kb = d_kw[:, 0:RD], d_kw[:, RD:2 * RD]
            dk_s[sl, :] += d_kf * dec.b[0] + d_kb * dec.b[1]
            dv_s[sl, :] += _nn(_both(k, dec.b), r)
            return (db_f + d_kf * k, db_b + d_kb * k, dg_f + r[0:RD, :] * s[0:RD, :],
                    dg_b + r[RD:2 * RD, :] * s[RD:2 * RD, :])

        db_dg = _chunk_loop(NCH, updates_back, (zeros, zeros, zeros, zeros))
        dkc = [None, None]
        dvc = [None, None]
        for d, ((rev, order_c, lg, rdv, ctx_states), ds) in enumerate(zip(dirs, d_states)):
            dlg = (_total(dmask_sum * dec.dmask[d]) + _total((da_f, da_b)[d] * dec.da[d])
                   + _total(db_dg[d] * dec.db[d]) + CH * dec.g[d] * _total(db_dg[2 + d]))
            for idx in (1, 0):
                n = order_c[idx]
                _, vjp = jax.vjp(functools.partial(_ret_state, reverse=rev), kcs[n], vcs[n], ctx_states[idx], lg)
                dk_c, dv_c, ds, dl = vjp(ds)
                dlg = dlg + dl
                dkc[n] = dk_c if dkc[n] is None else dkc[n] + dk_c
                dvc[n] = dv_c if dvc[n] is None else dvc[n] + dv_c
            drd_ref[int(rev):int(rev) + 1, :] += dlg * jax.nn.sigmoid(-rdv)
        dp_ref[:, 0:128] = _rope_t(dq_s[...] * (RD ** -0.5), cos_v, sin_v).astype(BF16)
        dp_ref[:, 128:256] = _rope_t(dk_s[...], cos_v, sin_v).astype(BF16)
        dp_ref[:, 256:384] = dv_s[...].astype(BF16)
        zero = jnp.zeros((CH, RD), BF16)
        for n in (0, 1):
            rows = slice(n * CH, (n + 1) * CH)
            dpc_ref[rows, 0:128] = zero
            dpc_ref[rows, 128:256] = dkc[n].astype(BF16)
            dpc_ref[rows, 256:384] = dvc[n].astype(BF16)
            dpc_ref[rows, 384:512] = zero

        @pl.when(grid_step == RH * nb - 1)
        def _():
            hosted.finish(h_in, h_out, h_sems)

    h_in_specs, h_out_specs = hosted.specs()
    return pl.pallas_call(
        body, name="retention_bwd", grid=(RH, nb),
        in_specs=[sp["pret"], sp["pretc"], sp["head"], sp["head"], sp["rd"], sp["gn"], sp["tab"], sp["tab"]]
        + h_in_specs,
        out_specs=[
            pl.BlockSpec((None, SEQ, 512), lambda h, b: (b, 0, h)),
            pl.BlockSpec((None, LC, 512), lambda h, b: (b, 0, h)),
            pl.BlockSpec((None, 2, 1), lambda h, b: (h, 0, 0)),
            pl.BlockSpec((None, 1, RD), lambda h, b: (h, 0, 0)),
        ] + h_out_specs,
        out_shape=[
            jax.ShapeDtypeStruct((nb, SEQ, IN_W), BF16),
            jax.ShapeDtypeStruct((nb, LC, IN_W), BF16),
            jax.ShapeDtypeStruct((RH, 2, 1), F32),
            jax.ShapeDtypeStruct((RH, 1, RD), F32),
        ] + hosted.out_shape,
        scratch_shapes=[pltpu.VMEM((SEQ, RD), F32)] * 6 + [pltpu.VMEM((NCH, 2 * RD, RD), F32)] * 2 + hosted.scratch,
        compiler_params=_cp(("arbitrary", "arbitrary")),
    )(pret, pretc, o_all, dmixin, rd, gn, cos, sin, *hosted.args)


def _rpb_flat(rpb):
    return jnp.pad(rpb, ((0, 0), (0, 1), (0, 33))).reshape(NPAIR, 2, 1, 1024)


def _rpb_flat_t(dflat):
    return dflat.reshape(8, 16, 64)[:, :15, :31]


def _barrel(x, left):
    row = lax.broadcasted_iota(jnp.int32, x.shape, 0)
    n = x.shape[1]
    for bit in range(6):
        s = 1 << bit
        x = jnp.where(((row >> bit) & 1) == 1, pltpu.roll(x, (n - s) if left else s, 1), x)
    return x


NA_TILE_ROWS, NA_BAND_ROWS = 4, 12
NA_Q, NA_K = NA_TILE_ROWS * GW, NA_BAND_ROWS * GW
NA_TILES = SEQ // NA_Q


def _band_start(r0):
    return min(max(r0 - 4, 0), 32 - NA_BAND_ROWS)


def _tile_layout(t):
    rows = range(t * NA_TILE_ROWS, (t + 1) * NA_TILE_ROWS)
    return tuple((r if r < 4 else (r - 24 if r > 28 else 4), min(max(r - 4, 0), 24) - _band_start(rows[0]))
                 for r in rows)


NA_CLASSES = sorted(set(_tile_layout(t) for t in range(NA_TILES)))


def _tile_rows(cls):
    return NA_CLASSES[cls]


def _na_tile(t):
    start = jnp.clip(NA_TILE_ROWS * t - 4, 0, 32 - NA_BAND_ROWS)
    cls = 0
    for tile in range(NA_TILES):
        cls = jnp.where(t == tile, NA_CLASSES.index(_tile_layout(tile)), cls)
    return pl.ds(pl.multiple_of(t * NA_Q, NA_Q), NA_Q), pl.ds(pl.multiple_of(start * GW, NA_Q), NA_K), cls


def _na_probs(qst, kb, kc, bias):
    s_loc = _nt(qst, kb) + bias
    s_ctx = _nt(qst, kc)
    m = jnp.maximum(jnp.max(s_loc, axis=1, keepdims=True), jnp.max(s_ctx, axis=1, keepdims=True))
    e_loc, e_ctx = jnp.exp(s_loc - m), jnp.exp(s_ctx - m)
    den = jnp.sum(e_loc, axis=1, keepdims=True) + jnp.sum(e_ctx, axis=1, keepdims=True)
    return e_loc / den, e_ctx / den


def _stack_heads(t):
    lane = lax.broadcasted_iota(jnp.int32, t.shape, 1)
    zero = jnp.zeros_like(t)
    return jnp.concatenate([jnp.where(lane < 64, t, zero), jnp.where(lane >= 64, t, zero)], axis=0)


def _unstack_heads(t):
    n = t.shape[0] // 2
    lane = lax.broadcasted_iota(jnp.int32, (n, 128), 1)
    return jnp.where(lane < 64, t[:n], t[n:])


NA_BIAS_SHAPE = (len(NA_CLASSES), 2 * NA_Q, NA_K)


def _na_bias_pair(flat_ref, out_ref):
    qc = lax.broadcasted_iota(jnp.int32, (GW, 512), 0)
    kc = lax.broadcasted_iota(jnp.int32, (GW, 512), 1) & 63
    start = jnp.clip(qc - 8, 0, GW - 16)
    window = (kc >= start) & (kc < start + 16)
    fill = jnp.full((GW, NA_K - 512), NEG, F32)
    for hh in (0, 1):
        skew = _barrel(pltpu.roll(jnp.broadcast_to(flat_ref[hh], (GW, 1024)), 1024 - 15, 1), left=False)
        by_class = [jnp.where(window, (skew if rc == 7 else pltpu.roll(skew, (9 + rc) * 64, 1))[:, 0:512], NEG)
                    for rc in range(8)]
        for cls in range(len(NA_CLASSES)):
            for qr, (rc, off) in enumerate(_tile_rows(cls)):
                w = jnp.concatenate([by_class[rc], fill], axis=1)
                rows = slice(hh * NA_Q + qr * GW, hh * NA_Q + (qr + 1) * GW)
                out_ref[cls, rows, :] = pltpu.roll(w, off * GW, 1) if off else w


def na_fwd(pna, pnac, bias, mixin, hosted):
    nb = pna.shape[0]

    def body(*refs):
        (p_ref, pc_ref, bias_ref, _), h_in, (out_ref,), h_out, _, h_sems = hosted.split(refs, 4, 1)
        grid_step = pl.program_id(0) * nb + pl.program_id(1)

        @pl.when(grid_step == 0)
        def _():
            hosted.start(h_in, h_out, h_sems)

        kc, vc = pc_ref[:, 128:256], pc_ref[:, 256:384]

        def tile(t, carry):
            qsl, bsl, cls = _na_tile(t)
            kb, vb = p_ref[bsl, 128:256], p_ref[bsl, 256:384]
            p_loc, p_ctx = _na_probs(_stack_heads(p_ref[qsl, 0:128] * 0.125), kb, kc, bias_ref[cls])
            out_ref[qsl, :] = _unstack_heads(_nn(p_loc, vb) + _nn(p_ctx, vc)).astype(BF16)
            return carry

        lax.fori_loop(0, NA_TILES, tile, 0, unroll=4)

        @pl.when(grid_step == NPAIR * nb - 1)
        def _():
            hosted.finish(h_in, h_out, h_sems)

    h_in_specs, h_out_specs = hosted.specs()
    return pl.pallas_call(
        body, name="na_fwd", grid=(NPAIR, nb),
        in_specs=[
            pl.BlockSpec((None, SEQ, 384), lambda p, b: (b, 0, p)),
            pl.BlockSpec((None, LC, 384), lambda p, b: (b, 0, p)),
            pl.BlockSpec((None, len(NA_CLASSES), 2 * NA_Q, NA_K), lambda p, b: (p, 0, 0, 0)),
            pl.BlockSpec(memory_space=pl.ANY),
        ] + h_in_specs,
        out_specs=[pl.BlockSpec((None, SEQ, 128), lambda p, b: (b, 0, 4 + p))] + h_out_specs,
        out_shape=[jax.ShapeDtypeStruct((nb, SEQ, D), BF16)] + hosted.out_shape,
        input_output_aliases={3: 0},
        scratch_shapes=hosted.scratch,
        compiler_params=_cp(("arbitrary", "arbitrary")),
    )(pna, pnac, bias, mixin, *hosted.args)


def na_bwd(pna, pnac, bias, dmixin, dproj, dprojc, hosted):
    nb = pna.shape[0]

    def body(*refs):
        own_in, h_in, own_out, h_out, own_scr, h_sems = hosted.split(refs, 6, 3)
        p_ref, pc_ref, bias_ref, dmix_ref = own_in[:4]
        dp_ref, dpc_ref, dpat_ref = own_out
        dbias_s, dk_s, dv_s, dkc_s, dvc_s, res_s, resc_s = own_scr
        b, part = pl.program_id(1), pl.program_id(2)
        grid_step = (pl.program_id(0) * nb + b) * 3 + part

        @pl.when(grid_step == 0)
        def _():
            hosted.start(h_in, h_out, h_sems)

        @pl.when(grid_step == NPAIR * nb * 3 - 1)
        def _():
            hosted.finish(h_in, h_out, h_sems)

        @pl.when(part == 0)
        def _():
            @pl.when(b == 0)
            def _():
                dbias_s[...] = jnp.zeros_like(dbias_s)

            dk_s[...] = jnp.zeros_like(dk_s)
            dv_s[...] = jnp.zeros_like(dv_s)
            dkc_s[...] = jnp.zeros_like(dkc_s)
            dvc_s[...] = jnp.zeros_like(dvc_s)
            kc, vc = pc_ref[:, 128:256], pc_ref[:, 256:384]

            def tile(t, carry):
                qsl, bsl, cls = _na_tile(t)
                kb, vb = p_ref[bsl, 128:256], p_ref[bsl, 256:384]
                qst, dost = _stack_heads(p_ref[qsl, 0:128] * 0.125), _stack_heads(dmix_ref[qsl, :])
                p_loc, p_ctx = _na_probs(qst, kb, kc, bias_ref[cls])
                dp_loc, dp_ctx = _nt(dost, vb), _nt(dost, vc)
                delta = (jnp.sum(p_loc * dp_loc, axis=1, keepdims=True)
                         + jnp.sum(p_ctx * dp_ctx, axis=1, keepdims=True))
                ds_loc, ds_ctx = p_loc * (dp_loc - delta), p_ctx * (dp_ctx - delta)
                dbias_s[cls] += ds_loc
                res_s[0, qsl, :] = _unstack_heads((_nn(ds_loc, kb) + _nn(ds_ctx, kc)) * 0.125).astype(BF16)
                dk_s[bsl, :] += _tn(ds_loc, qst)
                dv_s[bsl, :] += _tn(p_loc, dost)
                dkc_s[...] += _tn(ds_ctx, qst)
                dvc_s[...] += _tn(p_ctx, dost)
                return carry

            lax.fori_loop(0, NA_TILES, tile, 0, unroll=2)
            res_s[1] = dk_s[...].astype(BF16)
            res_s[2] = dv_s[...].astype(BF16)
            resc_s[0] = jnp.zeros((LC, 128), BF16)
            resc_s[1] = dkc_s[...].astype(BF16)
            resc_s[2] = dvc_s[...].astype(BF16)

            @pl.when(b == nb - 1)
            def _():
                for hh in (0, 1):
                    by_class = [None] * 8
                    for cls in range(len(NA_CLASSES)):
                        for qr, (rc, off) in enumerate(_tile_rows(cls)):
                            w = dbias_s[cls, hh * NA_Q + qr * GW:hh * NA_Q + (qr + 1) * GW, :]
                            w = (pltpu.roll(w, NA_K - off * GW, 1) if off else w)[:, 0:512]
                            by_class[rc] = w if by_class[rc] is None else by_class[rc] + w
                    skew = jnp.zeros((GW, 1024), F32)
                    for rc in range(8):
                        w = jnp.concatenate([by_class[rc], jnp.zeros((GW, 512), F32)], axis=1)
                        skew = skew + (w if rc == 7 else pltpu.roll(w, (7 - rc) * 64, 1))
                    dpat_ref[hh] = jnp.sum(pltpu.roll(_barrel(skew, left=True), 15, 1), axis=0, keepdims=True)

        dp_ref[...] = res_s[part]
        dpc_ref[...] = resc_s[part]

    h_in_specs, h_out_specs = hosted.specs()
    return pl.pallas_call(
        body, name="na_bwd", grid=(NPAIR, nb, 3),
        in_specs=[
            pl.BlockSpec((None, SEQ, 384), lambda p, b, s: (b, 0, p)),
            pl.BlockSpec((None, LC, 384), lambda p, b, s: (b, 0, p)),
            pl.BlockSpec((None, len(NA_CLASSES), 2 * NA_Q, NA_K), lambda p, b, s: (p, 0, 0, 0)),
            pl.BlockSpec((None, SEQ, 128), lambda p, b, s: (b, 0, 4 + p)),
            pl.BlockSpec(memory_space=pl.ANY),
            pl.BlockSpec(memory_space=pl.ANY),
        ] + h_in_specs,
        out_specs=[
            pl.BlockSpec((None, SEQ, 128), lambda p, b, s: (b, 0, 16 + 3 * p + s)),
            pl.BlockSpec((None, LC, 128), lambda p, b, s: (b, 0, 16 + 3 * p + s)),
            pl.BlockSpec((None, 2, 1, 1024), lambda p, b, s: (p, 0, 0, 0)),
        ] + h_out_specs,
        out_shape=[
            jax.ShapeDtypeStruct((nb, SEQ, IN_W), BF16),
            jax.ShapeDtypeStruct((nb, LC, IN_W), BF16),
            jax.ShapeDtypeStruct((NPAIR, 2, 1, 1024), F32),
        ] + hosted.out_shape,
        input_output_aliases={4: 0, 5: 1},
        scratch_shapes=[
            pltpu.VMEM((len(NA_CLASSES), 2 * NA_Q, NA_K), F32),
            pltpu.VMEM((SEQ, 128), F32), pltpu.VMEM((SEQ, 128), F32),
            pltpu.VMEM((LC, 128), F32), pltpu.VMEM((LC, 128), F32),
            pltpu.VMEM((3, SEQ, 128), BF16), pltpu.VMEM((3, LC, 128), BF16),
        ] + hosted.scratch,
        compiler_params=_cp(("arbitrary", "arbitrary", "arbitrary")),
    )(pna, pnac, bias, dmixin, dproj, dprojc, *hosted.args)


def tail_fwd_bwd(x, mixin, tgt, mod3, g_post_mix, g_pre_mlp, g_post_mlp, wout, w1, w2):
    nb = x.shape[0]

    def body(x_ref, mi_ref, tgt_ref, mod_ref, gpm_ref, gpl_ref, gpo_ref, wo_ref, w1_ref, w2_ref,
             dx_ref, dmix_ref, h2_ref, du_ref, a_ref, dm_ref, dmi_ref, dmod_ref, dg_ref, loss_ref):
        b, t = pl.program_id(0), pl.program_id(1)
        gt1, sh2, sc2, gt2 = mod_ref[2:3, :], mod_ref[3:4, :], mod_ref[4:5, :], mod_ref[5:6, :]
        mix = jnp.dot(mi_ref[...], wo_ref[...], preferred_element_type=F32)
        (x1, h2), vjp_a = jax.vjp(_post_mix, x_ref[...], mix, gt1, sc2, sh2, gpm_ref[...], gpl_ref[...])
        h2b = h2.astype(BF16)
        h2_ref[...] = h2b
        m = jnp.zeros((TN, D), F32)
        relus = []
        for j in range(4):
            cols = slice(j * D, (j + 1) * D)
            r = jnp.maximum(jnp.dot(h2b, w1_ref[j], preferred_element_type=F32), 0.0)
            ab = (r * r).astype(BF16)
            a_ref[:, cols] = ab
            m = m + jnp.dot(ab, w2_ref[cols, :], preferred_element_type=F32)
            relus.append(r)
        loss, vjp_b = jax.vjp(_head_loss, x1, m, gt2, gpo_ref[...], tgt_ref[...])
        dx1, dm, dgt2, dgpo, _ = vjp_b(jnp.ones((1, 1), F32))
        dmb = dm.astype(BF16)
        dm_ref[...] = dmb
        dh2 = jnp.zeros((TN, D), F32)
        for j in range(4):
            cols = slice(j * D, (j + 1) * D)
            da = lax.dot_general(dmb, w2_ref[cols, :], (((1,), (1,)), ((), ())), preferred_element_type=F32)
            dub = (da * (2.0 * relus[j])).astype(BF16)
            du_ref[:, cols] = dub
            dh2 = dh2 + lax.dot_general(dub, w1_ref[j], (((1,), (1,)), ((), ())), preferred_element_type=F32)
        dx, dmix, dgt1, dsc2, dsh2, dgpm, dgpl = vjp_a((dx1, dh2))
        dx_ref[...] = dx
        dmixb = dmix.astype(BF16)
        dmix_ref[...] = dmixb
        dmi_ref[...] = lax.dot_general(dmixb, wo_ref[...], (((1,), (1,)), ((), ())),
                                       preferred_element_type=F32).astype(BF16)

        @pl.when(t == 0)
        def _():
            dmod_ref[...] = jnp.zeros_like(dmod_ref)

        @pl.when((t == 0) & (b == 0))
        def _():
            dg_ref[...] = jnp.zeros_like(dg_ref)
            loss_ref[...] = jnp.zeros_like(loss_ref)

        dmod_ref[2:3, :] += dgt1
        dmod_ref[3:4, :] += dsh2
        dmod_ref[4:5, :] += dsc2
        dmod_ref[5:6, :] += dgt2
        dg_ref[0:1, :] += dgpm
        dg_ref[1:2, :] += dgpl
        dg_ref[2:3, :] += dgpo
        loss_ref[...] += jnp.broadcast_to(loss, loss_ref.shape)

    tok = lambda b, t: (b, t, 0)
    const = lambda b, t: (0, 0)
    vec = pl.BlockSpec((1, D), const)
    return pl.pallas_call(
        body, name="tail_fwd_bwd", grid=(nb, SEQ // TN),
        in_specs=[
            pl.BlockSpec((None, TN, D), tok), pl.BlockSpec((None, TN, D), tok), pl.BlockSpec((None, TN, D), tok),
            pl.BlockSpec((None, 6, D), lambda b, t: (b, 0, 0)), vec, vec, vec,
            pl.BlockSpec((D, D), const, pipeline_mode=pl.Buffered(1)),
            pl.BlockSpec((4, D, D), lambda b, t: (0, 0, 0), pipeline_mode=pl.Buffered(1)),
            pl.BlockSpec((DFF, D), const, pipeline_mode=pl.Buffered(1)),
        ],
        out_specs=[
            pl.BlockSpec((None, TN, D), tok), pl.BlockSpec((None, TN, D), tok), pl.BlockSpec((None, TN, D), tok),
            pl.BlockSpec((None, TN, DFF), tok), pl.BlockSpec((None, TN, DFF), tok), pl.BlockSpec((None, TN, D), tok),
            pl.BlockSpec((None, TN, D), tok),
            pl.BlockSpec((None, 6, D), lambda b, t: (b, 0, 0)),
            pl.BlockSpec((8, D), const), pl.BlockSpec((8, 128), const),
        ],
        out_shape=[
            jax.ShapeDtypeStruct((nb, SEQ, D), F32), jax.ShapeDtypeStruct((nb, SEQ, D), BF16),
            jax.ShapeDtypeStruct((nb, SEQ, D), BF16), jax.ShapeDtypeStruct((nb, SEQ, DFF), BF16),
            jax.ShapeDtypeStruct((nb, SEQ, DFF), BF16), jax.ShapeDtypeStruct((nb, SEQ, D), BF16),
            jax.ShapeDtypeStruct((nb, SEQ, D), BF16),
            jax.ShapeDtypeStruct((nb, 6, D), F32), jax.ShapeDtypeStruct((8, D), F32),
            jax.ShapeDtypeStruct((8, 128), F32),
        ],
        compiler_params=_cp(("arbitrary", "arbitrary")),
    )(x, mixin, tgt, mod3, g_post_mix, g_pre_mlp, g_post_mlp, wout, w1, w2)


def weight_grad(pairs, name, out_dtype=F32, col_blocks=False, tm=1024, tn=1024, tk=2048, w_in_blocks=False):
    m, n = pairs[0][0].shape[1], pairs[0][1].shape[1]
    tn = n if w_in_blocks else min(tn, n)
    tks = [min(tk, xa.shape[0]) for xa, _ in pairs]
    steps = [xa.shape[0] // t for (xa, _), t in zip(pairs, tks)]
    total = sum(steps)
    offs = [sum(steps[:i]) for i in range(len(pairs))]

    def body(*refs):
        out_ref, acc = refs[2 * len(pairs)], refs[-1]
        k = pl.program_id(2)

        @pl.when(k == 0)
        def _():
            acc[...] = jnp.zeros_like(acc)

        for i in range(len(pairs)):
            @pl.when((k >= offs[i]) & (k < offs[i] + steps[i]))
            def _(i=i):
                acc[...] += lax.dot_general(refs[2 * i][...], refs[2 * i + 1][...], (((0,), (0,)), ((), ())),
                                            preferred_element_type=F32)

        if out_dtype != F32:
            @pl.when(k == total - 1)
            def _():
                if w_in_blocks:
                    _to_w_in_blocks(acc, out_ref)
                else:
                    out_ref[...] = acc[...].astype(out_dtype)

    in_specs, args = [], []
    for i, (xa, ya) in enumerate(pairs):
        clamp = lambda k, i=i: jnp.clip(k - offs[i], 0, steps[i] - 1)
        in_specs.append(pl.BlockSpec((tks[i], tm), lambda a, c, k, clamp=clamp: (clamp(k), a)))
        in_specs.append(pl.BlockSpec((tks[i], tn), lambda a, c, k, clamp=clamp: (clamp(k), c)))
        args += [xa, ya]
    if w_in_blocks:
        assert n == IN_W and out_dtype == BF16
        out_spec = pl.BlockSpec((4, tm, 896), lambda a, c, k: (0, a, 0))
        out_shape = jax.ShapeDtypeStruct((4, m, 896), BF16)
    elif col_blocks:
        out_spec = pl.BlockSpec((None, tm, tn), lambda a, c, k: (c, a, 0))
        out_shape = jax.ShapeDtypeStruct((n // tn, m, tn), out_dtype)
    else:
        out_spec = pl.BlockSpec((tm, tn), lambda a, c, k: (a, c))
        out_shape = jax.ShapeDtypeStruct((m, n), out_dtype)
    return pl.pallas_call(
        body, name=name, grid=(m // tm, n // tn, total), in_specs=in_specs, out_specs=out_spec, out_shape=out_shape,
        scratch_shapes=[] if out_dtype == F32 else [pltpu.VMEM((tm, tn), F32)],
        compiler_params=_cp(("arbitrary", "arbitrary", "arbitrary")),
    )(*args)


def _perm_block(t):
    return 4 * (t % 4) + t // 4 if t < 16 else 16 + 3 * ((t - 16) % 4) + (t - 16) // 4


def _is_rope_block(p):
    return p < 16 and p % 4 < 2


def _from_w_in_blocks(i_ref, o_ref):
    for t in range(28):
        p = _perm_block(t)
        blk = i_ref[t // 7, :, (t % 7) * 128:(t % 7 + 1) * 128]
        if _is_rope_block(p):
            blk = _pair_order(blk.astype(F32)).astype(BF16)
        o_ref[:, p * 128:(p + 1) * 128] = blk


def _to_w_in_blocks(i_ref, o_ref):
    for t in range(28):
        p = _perm_block(t)
        blk = i_ref[:, p * 128:(p + 1) * 128]
        if _is_rope_block(p):
            blk = _pair_order(blk)
        o_ref[t // 7, :, (t % 7) * 128:(t % 7 + 1) * 128] = blk.astype(BF16)


def _place():
    return lax.axis_index("x"), lax.axis_index("y"), lax.axis_index("c")


class Hosted:
    def __init__(self, args, out_shape, scratch, start, finish):
        self.args, self.out_shape, self.scratch, self.start, self.finish = args, out_shape, scratch, start, finish

    def specs(self):
        hbm = pl.BlockSpec(memory_space=pl.ANY)
        return [hbm] * len(self.args), [hbm] * len(self.out_shape)

    def split(self, refs, n_in, n_out):
        a, b = len(self.args), len(self.out_shape)
        cuts = [n_in, n_in + a, n_in + a + n_out, n_in + a + n_out + b, len(refs) - len(self.scratch)]
        parts = [refs[i:j] for i, j in zip([0] + cuts, cuts + [len(refs)])]
        return parts[0], parts[1], parts[2], parts[3], parts[4], parts[5]


def no_exchange():
    return Hosted([], [], [], lambda *a: None, lambda *a: None)


def run_hosted(hosted, name):
    def body(*refs):
        _, ins, _, outs, _, sems = hosted.split(refs, 0, 0)
        hosted.start(ins, outs, sems)
        hosted.finish(ins, outs, sems)

    in_specs, out_specs = hosted.specs()
    return pl.pallas_call(body, name=name, in_specs=in_specs, out_specs=out_specs, out_shape=hosted.out_shape,
                          scratch_shapes=hosted.scratch)(*hosted.args)


def gather8(blocks, relay_diagonal=False):
    na = len(blocks)

    def copies(ins, outs, sems):
        send_sems, recv_sems, local_sem = sems
        x, y, c = _place()
        me, sibling = (x, y, c), (x, y, 1 - c)
        chips = [(1 - x, y), (x, 1 - y), (1 - x, 1 - y)]

        def slot(o_ref, px, py, pc, half=None):
            ref = o_ref.at[4 * px + 2 * py + pc]
            if half is None:
                return ref
            rows = ref.shape[0] // 2
            return ref.at[pl.ds(half * rows, rows)]

        def copy(a, k, block, to, src=None, half=None):
            return pltpu.make_async_remote_copy(
                src_ref=slot(outs[a], *block, half) if src is None else src, dst_ref=slot(outs[a], *block, half),
                send_sem=send_sems.at[a, k], recv_sem=recv_sems.at[a, k], device_id=to, device_id_type=MESH)

        mine = [pltpu.make_async_copy(ins[a], slot(outs[a], *me), local_sem.at[a]) for a in range(na)]
        first = []
        for a in range(na):
            first.append(copy(a, 0, me, sibling, src=ins[a]))
            first += [copy(a, 1 + j, me, (*chip, c), src=ins[a])
                      for j, chip in enumerate(chips[:2] if relay_diagonal else chips)]
        return copy, mine, first, me, sibling, chips, c

    def start(ins, outs, sems):
        _, mine, first, *_ = copies(ins, outs, sems)
        for cp in mine + first:
            cp.start()

    def finish(ins, outs, sems):
        copy, mine, first, me, sibling, chips, c = copies(ins, outs, sems)
        passed = []
        for j, chip in enumerate(chips[:2] if relay_diagonal else chips):
            for a in range(na):
                copy(a, 1 + j, (*chip, c), me).wait_recv()
                onward = [copy(a, 4 + j, (*chip, c), sibling)]
                if relay_diagonal:
                    onward.insert(0, copy(a, (3, 7)[j], (*chip, c), (*chips[1 - j], c), half=j))
                for cp in onward:
                    cp.start()
                passed += onward
        if relay_diagonal:
            for a in range(na):
                copy(a, 3, (*chips[2], c), me, half=0).wait_recv()
                copy(a, 7, (*chips[2], c), me, half=1).wait_recv()
                cp = copy(a, 6, (*chips[2], c), sibling)
                cp.start()
                passed.append(cp)
        for a in range(na):
            copy(a, 0, sibling, me).wait_recv()
            for j, chip in enumerate(chips):
                copy(a, 4 + j, (*chip, 1 - c), me).wait_recv()
        for cp in first + passed:
            cp.wait_send()
        for cp in mine:
            cp.wait()

    return Hosted(list(blocks), [jax.ShapeDtypeStruct((8,) + b.shape, b.dtype) for b in blocks],
                  [pltpu.SemaphoreType.DMA((na, 8)), pltpu.SemaphoreType.DMA((na, 8)), pltpu.SemaphoreType.DMA((na,))],
                  start, finish)


def chips3(arrays):
    na = len(arrays)

    def copies(ins, outs, sems):
        send_sems, recv_sems = sems
        x, y, c = _place()
        return [pltpu.make_async_remote_copy(
            src_ref=ins[a].at[2 * px + py], dst_ref=outs[a].at[k], send_sem=send_sems.at[a, k],
            recv_sem=recv_sems.at[a, k], device_id=(px, py, c), device_id_type=MESH)
            for a in range(na) for k, (px, py) in enumerate([(1 - x, y), (x, 1 - y), (1 - x, 1 - y)])]

    def start(ins, outs, sems):
        for cp in copies(ins, outs, sems):
            cp.start()

    def finish(ins, outs, sems):
        for cp in copies(ins, outs, sems):
            cp.wait()

    return Hosted(list(arrays), [jax.ShapeDtypeStruct((3,) + a.shape[1:], a.dtype) for a in arrays],
                  [pltpu.SemaphoreType.DMA((na, 3)), pltpu.SemaphoreType.DMA((na, 3))], start, finish)


def siblings(arrays):
    na = len(arrays)

    def copies(ins, outs, sems):
        send_sems, recv_sems = sems
        x, y, c = _place()
        return [pltpu.make_async_remote_copy(
            src_ref=ins[a], dst_ref=outs[a], send_sem=send_sems.at[a], recv_sem=recv_sems.at[a],
            device_id=(x, y, 1 - c), device_id_type=MESH) for a in range(na)]

    def start(ins, outs, sems):
        for cp in copies(ins, outs, sems):
            cp.start()

    def finish(ins, outs, sems):
        for cp in copies(ins, outs, sems):
            cp.wait()

    return Hosted(list(arrays), [jax.ShapeDtypeStruct(a.shape, a.dtype) for a in arrays],
                  [pltpu.SemaphoreType.DMA((na,)), pltpu.SemaphoreType.DMA((na,))], start, finish)


def both(first, second):
    na, no, ns = len(first.args), len(first.out_shape), len(first.scratch)

    def start(ins, outs, sems):
        first.start(ins[:na], outs[:no], sems[:ns])
        second.start(ins[na:], outs[no:], sems[ns:])

    def finish(ins, outs, sems):
        first.finish(ins[:na], outs[:no], sems[:ns])
        second.finish(ins[na:], outs[no:], sems[ns:])

    return Hosted(first.args + second.args, first.out_shape + second.out_shape, first.scratch + second.scratch,
                  start, finish)


def siblings4(arrays):
    na = len(arrays)

    def copies(ins, outs, sems):
        send_sems, recv_sems = sems
        x, y, c = _place()
        return [pltpu.make_async_remote_copy(
            src_ref=ins[a].at[2 * j + 1 - c], dst_ref=outs[a].at[j],
            send_sem=send_sems.at[a, j], recv_sem=recv_sems.at[a, j],
            device_id=(x, y, 1 - c), device_id_type=MESH) for a in range(na) for j in range(4)]

    def start(ins, outs, sems):
        for cp in copies(ins, outs, sems):
            cp.start()

    def finish(ins, outs, sems):
        for cp in copies(ins, outs, sems):
            cp.wait()

    return Hosted(list(arrays), [jax.ShapeDtypeStruct((4,) + a.shape[1:], a.dtype) for a in arrays],
                  [pltpu.SemaphoreType.DMA((na, 4)), pltpu.SemaphoreType.DMA((na, 4))], start, finish)


def _row_tile(r):
    for cand in (512, 256, 128, 64, 32, 16, 8):
        if r % cand == 0:
            return cand
    return r


def chip_partial(place, g8s, landed4s, name):
    n = len(g8s)

    def body(place_ref, *refs):
        del place_ref
        for g_ref, l_ref, o_ref in zip(refs[:n], refs[n:2 * n], refs[2 * n:]):
            o_ref[...] = (g_ref[...].astype(F32) + l_ref[...].astype(F32)).astype(BF16)

    own = [pl.BlockSpec((None,) + g.shape[1:], lambda j, s: (2 * j + s[0], 0, 0)) for g in g8s]
    plain = [pl.BlockSpec((None,) + g.shape[1:], lambda j, s: (j, 0, 0)) for g in g8s]
    return pl.pallas_call(
        body, name=name,
        grid_spec=pltpu.PrefetchScalarGridSpec(num_scalar_prefetch=1, grid=(4,), in_specs=own + plain, out_specs=plain),
        out_shape=[jax.ShapeDtypeStruct((4,) + g.shape[1:], BF16) for g in g8s],
    )(place, *g8s, *landed4s)


def shard_sum(place, partial4s, landed3s, name):
    n = len(partial4s)

    def body(place_ref, *refs):
        del place_ref
        for p_ref, l_ref, o_ref in zip(refs[:n], refs[n:2 * n], refs[2 * n:]):
            acc = p_ref[...].astype(F32)
            for k in range(3):
                acc = acc + l_ref[k].astype(F32)
            o_ref[...] = acc

    def halves(p, lead):
        r, ccols = p.shape[1:]
        return (lead, r // 2, ccols)

    return pl.pallas_call(
        body, name=name,
        grid_spec=pltpu.PrefetchScalarGridSpec(
            num_scalar_prefetch=1, grid=(2,),
            in_specs=[pl.BlockSpec(halves(p, None), lambda i, s: (s[1], i, 0)) for p in partial4s]
            + [pl.BlockSpec(halves(p, 3), lambda i, s: (0, i, 0)) for p in partial4s],
            out_specs=[pl.BlockSpec(halves(p, None)[1:], lambda i, s: (i, 0)) for p in partial4s]),
        out_shape=[jax.ShapeDtypeStruct(p.shape[1:], F32) for p in partial4s],
    )(place, *partial4s, *landed3s)


def _adamw_math(w, g, m, v):
    m2 = B1 * m + (1.0 - B1) * g
    v2 = B2 * v + (1.0 - B2) * (g * g)
    m_hat = m2 / (1.0 - B1 ** STEP)
    v_hat = v2 / (1.0 - B2 ** STEP)
    return -LR * (m_hat / (jnp.sqrt(v_hat) + AEPS) + WD * w), m2, v2


def adamw_halves(place, w, mine, theirs, m, v, name):
    r, ccols = w.shape
    hr = r // 2
    tr = _row_tile(hr)
    nt = hr // tr

    def body(place_ref, w_ref, a_ref, b_ref, m_ref, v_ref, g_out, d_out, m_out, v_out):
        g = jnp.where(pl.program_id(0) == place_ref[0], a_ref[...], b_ref[...])
        d, m2, v2 = _adamw_math(w_ref[...], g, m_ref[...], v_ref[...])
        g_out[...] = g
        d_out[...] = d
        m_out[...] = m2
        v_out[...] = v2

    full = pl.BlockSpec((tr, ccols), lambda h, i, s: (h * nt + i, 0))
    part = pl.BlockSpec((tr, ccols), lambda h, i, s: (i, 0))
    return pl.pallas_call(
        body, name=name,
        grid_spec=pltpu.PrefetchScalarGridSpec(
            num_scalar_prefetch=1, grid=(2, nt), in_specs=[full, part, part, full, full], out_specs=[full] * 4),
        out_shape=[jax.ShapeDtypeStruct((r, ccols), F32)] * 4,
    )(place, w, mine, theirs, m, v)


def adamw_group(place, halved, plain, hosted, name):
    rows = halved[0][0].shape[0]
    tr = 128
    nt = rows // 2 // tr
    nh, npl = len(halved), len(plain)

    def body(place_ref, *refs):
        own_in, h_in, own_out, h_out, _, h_sems = hosted.split(refs, 5 * nh + 4 * npl, 4 * nh + 3 * npl)
        half = pl.program_id(0)
        grid_step = half * nt + pl.program_id(1)

        @pl.when(grid_step == 0)
        def _():
            hosted.start(h_in, h_out, h_sems)

        for i in range(nh):
            w_ref, a_ref, b_ref, m_ref, v_ref = own_in[5 * i:5 * i + 5]
            g = jnp.where(half == place_ref[0], a_ref[...], b_ref[...])
            res = (g,) + _adamw_math(w_ref[...], g, m_ref[...], v_ref[...])
            for o_ref, r in zip(own_out[4 * i:4 * i + 4], res):
                o_ref[...] = r
        for i in range(npl):
            w_ref, g_ref, m_ref, v_ref = own_in[5 * nh + 4 * i:5 * nh + 4 * i + 4]
            res = _adamw_math(w_ref[...], g_ref[...], m_ref[...], v_ref[...])
            for o_ref, r in zip(own_out[4 * nh + 3 * i:4 * nh + 3 * i + 3], res):
                o_ref[...] = r

        @pl.when(grid_step == 2 * nt - 1)
        def _():
            hosted.finish(h_in, h_out, h_sems)

    def full(cols):
        return pl.BlockSpec((tr, cols), lambda h, i, s: (h * nt + i, 0))

    def part(cols):
        return pl.BlockSpec((tr, cols), lambda h, i, s: (i, 0))

    in_specs, out_specs, out_shape, args = [], [], [], []
    for w, a, b, m, v in halved:
        cols = w.shape[1]
        in_specs += [full(cols), part(cols), part(cols), full(cols), full(cols)]
        out_specs += [full(cols)] * 4
        out_shape += [jax.ShapeDtypeStruct(w.shape, F32)] * 4
        args += [w, a, b, m, v]
    for w, g, m, v in plain:
        cols = w.shape[1]
        in_specs += [full(cols)] * 4
        out_specs += [full(cols)] * 3
        out_shape += [jax.ShapeDtypeStruct(w.shape, F32)] * 3
        args += [w, g, m, v]
    h_in_specs, h_out_specs = hosted.specs()
    return pl.pallas_call(
        body, name=name,
        grid_spec=pltpu.PrefetchScalarGridSpec(
            num_scalar_prefetch=1, grid=(2, nt), in_specs=in_specs + h_in_specs, out_specs=out_specs + h_out_specs,
            scratch_shapes=hosted.scratch),
        out_shape=out_shape + hosted.out_shape,
        compiler_params=_cp(("arbitrary", "arbitrary")),
    )(place, *args, *hosted.args)


def _silu(x):
    return x * jax.nn.sigmoid(x)


def prologue(c_rows, c_ctx_row, w_ada, b_shard, rpb_flat, half_w_in, late_shards):
    shape = jax.ShapeDtypeStruct
    n_late = len(late_shards)
    half_shapes = [(w.shape[0] // 2, w.shape[1]) for w in late_shards]
    g_w = gather8([half_w_in], relay_diagonal=True)
    g_c = gather8([shape((8, D), F32)])
    g_m = chips3([shape((4, 8, 1536), F32)])

    def body(*refs):
        c_ref, cc_ref, w_ref, b_ref, flat_ref, hw_ref = refs[:6]
        late_refs = refs[6:6 + n_late]
        cin_ref, mg_ref, gw_ref, bias_ref, cos_ref, sin_ref = refs[6 + n_late:12 + n_late]
        rest = refs[12 + n_late:]
        half_refs, (cg_s, ms_s, bias_s, need_s, landed_s) = rest[:n_late], rest[n_late:n_late + 5]
        stage, (load_sem, bias_sem), sems = rest[n_late + 5:2 * n_late + 5], rest[2 * n_late + 5:2 * n_late + 7], \
            rest[2 * n_late + 7:]
        sw, sc, sm = sems[0:3], sems[3:6], sems[6:8]
        px, py, core = _place()
        g_c.start([c_ref], [cg_s], sc)
        g_w.start([hw_ref], [gw_ref], sw)
        loads = [pltpu.make_async_copy(late_refs[a].at[pl.ds(core * half_shapes[a][0], half_shapes[a][0]), :],
                                       stage[a], load_sem.at[a]) for a in range(n_late)]
        for cp in loads:
            cp.start()
        g_c.finish([c_ref], [cg_s], sc)
        cin_ref[...] = jnp.zeros_like(cin_ref)
        for dev in range(8):
            cin_ref[2 * dev:2 * dev + 2, :] = cg_s[dev, 0:2, :]
        cin_ref[16:17, :] = cc_ref[...]
        need_s[...] = jnp.zeros_like(need_s)
        for j in range(4):
            need_s[8 * j:8 * j + 2, :] = cg_s[2 * j + core, 0:2, :]
            need_s[8 * j + 2:8 * j + 3, :] = cc_ref[...]
        ms_s[...] = (_nn(_silu(need_s[...]), w_ref[...]) + b_ref[...]).reshape(4, 8, 1536)
        g_m.start([ms_s], [landed_s], sm)
        for a, cp in enumerate(loads):
            cp.wait()
            half_refs[a][...] = stage[a][...].astype(BF16)
        cos_ref[...], sin_ref[...] = _rope_tables()
        stores = []
        for pair in range(NPAIR):
            if pair >= 2:
                stores[pair - 2].wait()
            _na_bias_pair(flat_ref.at[pair], bias_s.at[pair % 2])
            stores.append(pltpu.make_async_copy(bias_s.at[pair % 2], bias_ref.at[pair], bias_sem.at[pair % 2]))
            stores[pair].start()
        for cp in stores[-2:]:
            cp.wait()
        g_w.finish([hw_ref], [gw_ref], sw)
        g_m.finish([ms_s], [landed_s], sm)
        mg_ref[2 * px + py] = ms_s[2 * px + py]
        for k, (qx, qy) in enumerate([(1 - px, py), (px, 1 - py), (1 - px, 1 - py)]):
            mg_ref[2 * qx + qy] = landed_s[k]

    vmem = pl.BlockSpec(memory_space=pltpu.VMEM)
    hbm = pl.BlockSpec(memory_space=pl.ANY)
    return pl.pallas_call(
        body, name="prologue", in_specs=[vmem, vmem, vmem, vmem, vmem, hbm] + [hbm] * n_late,
        out_specs=[vmem, vmem, hbm, hbm, vmem, vmem] + [vmem] * n_late,
        out_shape=[shape((32, D), F32), shape((4, 8, 1536), F32)] + g_w.out_shape
        + [shape((NPAIR,) + NA_BIAS_SHAPE, F32)] + [shape((SEQ, RD), F32)] * 2 + [shape(s, BF16) for s in half_shapes],
        scratch_shapes=[pltpu.VMEM((8, 8, D), F32), pltpu.VMEM((4, 8, 1536), F32), pltpu.VMEM((2,) + NA_BIAS_SHAPE, F32),
                        pltpu.VMEM((32, D), F32), pltpu.VMEM((3, 8, 1536), F32)]
        + [pltpu.VMEM(s, F32) for s in half_shapes]
        + [pltpu.SemaphoreType.DMA((n_late,)), pltpu.SemaphoreType.DMA((2,))]
        + g_w.scratch + g_c.scratch + g_m.scratch,
        compiler_params=_cp(),
    )(c_rows, c_ctx_row, w_ada, b_shard, rpb_flat, half_w_in, *late_shards)


def ada_grads(cin, gb, gc, w_ada):
    def body(c_ref, gb_ref, gc_ref, w_ref, gw_ref, pc_ref):
        ctx_tot = jnp.sum(gc_ref[...], axis=0, keepdims=True)
        rows = lax.broadcasted_iota(jnp.int32, (16, 512), 0)
        dm = jnp.concatenate([gb_ref[...], jnp.where(rows == 0, ctx_tot, 0.0)], axis=0)
        gw_ref[...] = _tn(_silu(c_ref[...]), dm)
        rows8 = lax.broadcasted_iota(jnp.int32, (8, 512), 0)
        part = _nt(jnp.where(rows8 == 0, ctx_tot, 0.0), w_ref[...])

        @pl.when(pl.program_id(0) == 0)
        def _():
            pc_ref[...] = jnp.zeros_like(pc_ref)

        pc_ref[...] += part

    return pl.pallas_call(
        body, name="ada_grads", grid=(3,),
        in_specs=[pl.BlockSpec((32, D), lambda j: (0, 0)), pl.BlockSpec((16, 512), lambda j: (0, j)),
                  pl.BlockSpec((8, 512), lambda j: (0, j)), pl.BlockSpec((D, 512), lambda j: (0, j))],
        out_specs=[pl.BlockSpec((D, 512), lambda j: (0, j)), pl.BlockSpec((8, D), lambda j: (0, 0))],
        out_shape=[jax.ShapeDtypeStruct((D, 1536), F32), jax.ShapeDtypeStruct((8, D), F32)],
    )(cin, gb, gc, w_ada)


SMALL_SUM_ROWS = 15


def small_update(gsm, gbf, gcf, pcg, params):
    n = len(params)

    def body(*refs):
        gsm_ref, gbf_ref, gcf_ref, pcg_ref = refs[:4]
        wmv, outs, loss_out = refs[4:4 + 3 * n], refs[4 + 3 * n:4 + 7 * n], refs[-1]
        acc = gsm_ref[0]
        for dev in range(1, 8):
            acc = acc + gsm_ref[dev]
        c_ctx = wmv[0][...]
        sg = jax.nn.sigmoid(c_ctx)
        dsilu = pcg_ref[0:1, :] + pcg_ref[2:3, :] + pcg_ref[4:5, :] + pcg_ref[6:7, :]
        lane = lax.broadcasted_iota(jnp.int32, (1, D), 1)
        last = acc[14:15, :]
        grads = [
            dsilu * (sg * (1.0 + c_ctx * (1.0 - sg))),
            jnp.sum(gbf_ref[...], axis=0, keepdims=True) + jnp.sum(gcf_ref[...], axis=0, keepdims=True),
            acc[0:1, :] + acc[1:2, :], acc[2:3, :], acc[3:4, :], acc[4:5, :],
            acc[5:6, 0:512], acc[6:14, :], jnp.where(lane < 8, last, 0.0),
        ]
        loss_out[...] = jnp.broadcast_to(jnp.sum(jnp.where(lane == 8, last, 0.0), axis=1, keepdims=True), (8, 128))
        for i, g in enumerate(grads):
            d, m2, v2 = _adamw_math(wmv[3 * i][...], g, wmv[3 * i + 1][...], wmv[3 * i + 2][...])
            outs[4 * i][...] = g
            outs[4 * i + 1][...] = d
            outs[4 * i + 2][...] = m2
            outs[4 * i + 3][...] = v2

    flat = [a for wmv in params for a in wmv]
    out_shape = [jax.ShapeDtypeStruct(w.shape, F32) for w, _, _ in params for _ in range(4)]
    return pl.pallas_call(
        body, name="small_update", out_shape=out_shape + [jax.ShapeDtypeStruct((8, 128), F32)],
    )(gsm, gbf, gcf, pcg, *flat)


def _pad_row(v, rows):
    flat = v.reshape(-1)
    return jnp.pad(flat, (0, rows * D - flat.shape[0])).reshape(rows, D)


def local_step(x, ctx, tgt, mod3, rope, bias, g_pre_mix, g_post_mix, g_pre_mlp, g_post_mlp, ret_decay, ret_gn,
               w_in_blocks, late_weights, early_grads):
    nb = x.shape[0]
    tokens = nb * SEQ
    cos, sin = rope
    rd = ret_decay.T.reshape(RH, 2, 1)
    gn = ret_gn.reshape(RH, 1, RD)
    h, pret, pna, wperm = premix_proj(x, mod3, g_pre_mix, w_in_blocks, False, "premix_proj", unpack=True)
    hc, pretc, pnac = premix_proj(ctx, mod3, g_pre_mix, wperm, True, "premix_proj_ctx")
    o_all, mixin, gw_out = retention_fwd(pret, pretc, rd, gn, cos, sin, late_weights(0))
    mixin, gw1, gw2 = na_fwd(pna, pnac, bias, mixin, late_weights(1))
    dx_tail, dmix, h2, du, act, dm, dmixin, dmod_t, dg_t, loss_t = tail_fwd_bwd(
        x, mixin, tgt, mod3, g_post_mix, g_pre_mlp, g_post_mlp, gw_out.reshape(D, D), gw1.reshape(4, D, D),
        gw2.reshape(DFF, D))
    dw_out = weight_grad([(mixin.reshape(tokens, D), dmix.reshape(tokens, D))], "grad_w_out", BF16)
    dw1 = weight_grad([(h2.reshape(tokens, D), du.reshape(tokens, DFF))], "grad_w_mlp1", BF16, col_blocks=True)
    dw2 = weight_grad([(act.reshape(tokens, DFF), dm.reshape(tokens, D))], "grad_w_mlp2", BF16)
    dproj, dprojc, drd, dgn, *landed = retention_bwd(pret, pretc, o_all, dmixin, rd, gn, cos, sin,
                                                     early_grads[0](dw_out, dw1, dw2))
    dproj, dprojc, dpat, *early = na_bwd(pna, pnac, bias, dmixin, dproj, dprojc, early_grads[1](landed))
    dw_in = weight_grad([(h.reshape(tokens, D), dproj.reshape(tokens, IN_W)),
                         (hc.reshape(nb * LC, D), dprojc.reshape(nb * LC, IN_W))], "grad_w_in", BF16, tk=512,
                        w_in_blocks=True)
    dmod_c, dg_c, *late = premix_bwd(ctx, mod3, g_pre_mix, wperm, dprojc, None, early_grads[2](dw_in), "premix_bwd_ctx")
    grad_x, dmod_a, dg_a, *late = premix_bwd(x, mod3, g_pre_mix, wperm, dproj, dx_tail, early_grads[3](late),
                                             "premix_bwd")
    dmod = jnp.concatenate([jnp.concatenate([dmod_a[:, 0:2], dmod_t[:, 2:6]], axis=1), dmod_c], axis=0)
    last = jnp.pad(jnp.concatenate([drd[:, :, 0].T.reshape(8), loss_t[0, 0:1]]), (0, D - 9)).reshape(1, D)
    small = jnp.concatenate([dg_a[0:1], dg_c[0:1], dg_t[0:3], _pad_row(dgn, 1), dpat.reshape(8, D), last], axis=0)
    return grad_x, late, early, dmod, small


def kernel(x, c, ctx, c_ctx, w_ada, b_ada, g_pre_mix, g_post_mix, g_pre_mlp, g_post_mlp, w_in, ret_decay, ret_gn, na_rpb, w_out, w_mlp1, w_mlp2, loss_target, m_c_ctx, m_w_ada, m_b_ada, m_g_pre_mix, m_g_post_mix, m_g_pre_mlp, m_g_post_mlp, m_w_in, m_ret_decay, m_ret_gn, m_na_rpb, m_w_out, m_w_mlp1, m_w_mlp2, v_c_ctx, v_w_ada, v_b_ada, v_g_pre_mix, v_g_post_mix, v_g_pre_mlp, v_g_post_mlp, v_w_in, v_ret_decay, v_ret_gn, v_na_rpb, v_w_out, v_w_mlp1, v_w_mlp2):
    px, py, pc = _place()
    chip = 2 * px + py

    half_w_in = lax.dynamic_slice_in_dim(w_in[0], pc * (D // 2), D // 2, 0).astype(BF16)
    cin, mg, gw_in, bias, cos, sin, *late_halves = prologue(
        jnp.pad(c, ((0, 6), (0, 0))), c_ctx[None], w_ada[0], lax.dynamic_slice_in_dim(b_ada, chip * 1536, 1536, 1),
        _rpb_flat(na_rpb[0]), half_w_in, [w_out[0], w_mlp1[0], w_mlp2[0]])
    halves = [half_w_in] + late_halves
    mod3 = mg[:, 0:3].transpose(1, 0, 2).reshape(3, 6, D)

    place = jnp.stack([pc, chip]).astype(jnp.int32)

    early_names = ["w_out", "w_mlp1", "w_mlp2"]
    early_g8, early_partial = [], []

    def early_a(dw_out, dw1, dw2):
        early_g8[:] = [dw_out.reshape(8, 128, D), dw1.reshape(8, 512, D), dw2.reshape(8, 512, D)]
        return siblings4(early_g8)

    def early_b(landed):
        early_partial[:] = chip_partial(place, early_g8, landed, "rs_chip_sum_early")
        return chips3(early_partial)

    late_partial = []

    late_g8 = []

    def late_c(dw_in):
        late_g8[:] = [dw_in.reshape(8, 512, 896)]
        return siblings4(late_g8)

    def late_d(landed):
        late_partial[:] = chip_partial(place, late_g8, landed, "rs_chip_sum_w_in")
        return chips3(late_partial)

    grad_x, (landed3_in,), early_landed, dmod, small = local_step(
        x, ctx, loss_target, mod3, (cos, sin), bias, g_pre_mix, g_post_mix, g_pre_mlp, g_post_mlp, ret_decay[0], ret_gn,
        gw_in.reshape(4, D, 896), lambda k: gather8(halves[1:2] if k == 0 else halves[2:4]), (early_a, early_b, late_c, late_d))
    early_mine = shard_sum(place, early_partial, early_landed, "rs_shard_sum_early")

    pay = jnp.concatenate([dmod.reshape(18, D), small, jnp.zeros((40 - 18 - SMALL_SUM_ROWS, D), F32)], axis=0)
    *early_theirs, gs = run_hosted(both(siblings(early_mine), gather8([pay])), "rs_halves_early_gather_small")
    gbf = gs[:, 0:12].reshape(16, 6 * D)
    gcf = gs[:, 12:18].reshape(8, 6 * D)
    gw_ada, pc_part = ada_grads(cin, lax.dynamic_slice_in_dim(gbf, chip * 1536, 1536, 1),
                                lax.dynamic_slice_in_dim(gcf, chip * 1536, 1536, 1), w_ada[0])
    (mine_in,) = shard_sum(place, late_partial, [landed3_in], "rs_shard_sum_w_in")
    theirs_in, pcg = run_hosted(both(siblings([mine_in]), gather8([pc_part])), "rs_halves_w_in_gather_c_ctx")

    grouped = adamw_group(
        place,
        [(w_mlp1[0], early_mine[1], early_theirs[1], m_w_mlp1[0], v_w_mlp1[0]),
         (w_mlp2[0], early_mine[2], early_theirs[2], m_w_mlp2[0], v_w_mlp2[0])],
        [(w_ada[0], gw_ada, m_w_ada[0], v_w_ada[0])], no_exchange(), "adamw_group")
    d_ada, m_ada, v_ada = grouped[8:11]
    big = [
        [r[None] for r in adamw_halves(place, w_in[0], mine_in, theirs_in, m_w_in[0], v_w_in[0], "adamw_w_in")],
        [r[None] for r in adamw_halves(place, w_out[0], early_mine[0], early_theirs[0], m_w_out[0], v_w_out[0],
                                       "adamw_w_out")],
        [r[None] for r in grouped[0:4]], [r[None] for r in grouped[4:8]],
    ]

    def rpb_rows(t):
        return _rpb_flat(t[0]).reshape(8, D)

    def decay_row(t):
        return jnp.pad(t.reshape(1, 8), ((0, 0), (0, D - 8)))

    views = [lambda t: t.reshape(1, D), lambda t: t, lambda t: t, lambda t: t, lambda t: t, lambda t: t, lambda t: t,
             rpb_rows, decay_row]
    back = [lambda t: t.reshape(D), lambda t: t, lambda t: t, lambda t: t, lambda t: t, lambda t: t, lambda t: t,
            lambda t: _rpb_flat_t(t)[None], lambda t: t[:, 0:8].reshape(1, 2, 4)]
    small_w = (c_ctx, b_ada, g_pre_mix, g_post_mix, g_pre_mlp, g_post_mlp, ret_gn, na_rpb, ret_decay)
    small_m = (m_c_ctx, m_b_ada, m_g_pre_mix, m_g_post_mix, m_g_pre_mlp, m_g_post_mlp, m_ret_gn, m_na_rpb, m_ret_decay)
    small_v = (v_c_ctx, v_b_ada, v_g_pre_mix, v_g_post_mix, v_g_pre_mlp, v_g_post_mlp, v_ret_gn, v_na_rpb, v_ret_decay)
    *res, loss8 = small_update(gs[:, 18:18 + SMALL_SUM_ROWS], gbf, gcf, pcg[:, 0],
                               [(f(w), f(m), f(v)) for f, w, m, v in zip(views, small_w, small_m, small_v)])

    def leaves(ada, idx):
        s_c, s_b, s_g1, s_g2, s_g3, s_g4, s_gn, s_rpb, s_rd = [back[i](res[4 * i + idx]) for i in range(9)]
        return [s_c, ada[None], s_b, s_g1, s_g2, s_g3, s_g4, big[0][idx], s_rd, s_gn, s_rpb,
                big[1][idx], big[2][idx], big[3][idx]]

    return (loss8[0, 0], grad_x, *leaves(gw_ada, 0), *leaves(d_ada, 1), *leaves(m_ada, 2), *leaves(v_ada, 3))
```

```python
import functools
import math

import jax
import jax.numpy as jnp
from jax import lax
from jax.experimental import pallas as pl
from jax.experimental.pallas import tpu as pltpu

F32, BF16 = jnp.float32, jnp.bfloat16
D = 1024
SEQ = 2048
LC = 256
GW = 64
RH, RD, CH = 4, 128, 128
NPAIR = 4
IN_W = 3584
RET_W = 2048
DFF = 4096
EPS = 1e-6
NEG = -1e30
TN = 256
NCH = SEQ // CH
LR, B1, B2, AEPS, WD, STEP = 0.001, 0.9, 0.999, 1e-08, 0.01, 10
MESH = pl.DeviceIdType.MESH
VMEM_LIMIT = 56 * 1024 * 1024


def _cp(sem=None):
    return pltpu.CompilerParams(dimension_semantics=sem, vmem_limit_bytes=VMEM_LIMIT)


def _nn(a, b):
    return jnp.dot(a.astype(BF16), b.astype(BF16), preferred_element_type=F32)


def _nt(a, b):
    return lax.dot_general(a.astype(BF16), b.astype(BF16), (((1,), (1,)), ((), ())), preferred_element_type=F32)


def _tn(a, b):
    return lax.dot_general(a.astype(BF16), b.astype(BF16), (((0,), (0,)), ((), ())), preferred_element_type=F32)


@jax.custom_vjp
def mm_tn(a, b):
    return _tn(a, b)


mm_tn.defvjp(lambda a, b: (_tn(a, b), (a, b)), lambda r, g: (_nt(r[1], g), _nn(r[0], g)))


def _rms(x):
    return x * lax.rsqrt(jnp.mean(x * x, axis=-1, keepdims=True) + EPS)


def _rms_mod(x, g, sc, sh):
    return (_rms(x) * g) * (1.0 + sc) + sh


def _post_mix(x, mix, gt1, sc2, sh2, g_post_mix, g_pre_mlp):
    x1 = x + gt1 * (_rms(mix) * g_post_mix)
    return x1, _rms_mod(x1, g_pre_mlp, sc2, sh2)


def _head_loss(x1, m, gt2, g_post_mlp, tgt):
    err = x1 + gt2 * (_rms(m) * g_post_mlp) - tgt
    return 0.5 * jnp.sum(jnp.mean(err * err, axis=-1, keepdims=True), axis=0, keepdims=True)


def _ln_gate(o, g, w):
    mu = jnp.mean(o, axis=-1, keepdims=True)
    var = jnp.mean(jnp.square(o - mu), axis=-1, keepdims=True)
    y = (o - mu) * lax.rsqrt(var + EPS)
    return (y * w) * (g * jax.nn.sigmoid(g))


def _pair_order(x):
    lane = lax.broadcasted_iota(jnp.int32, x.shape, 1)
    return jnp.where((lane >= 32) & (lane < 64), pltpu.roll(x, 96, 1),
                     jnp.where((lane >= 64) & (lane < 96), pltpu.roll(x, 32, 1), x))


def _rope(x, cos, sin):
    return x * cos + pltpu.roll(x, 64, 1) * sin


def _rope_t(g, cos, sin):
    return g * cos + pltpu.roll(g * sin, 64, 1)


def _rope_tables():
    tok = lax.broadcasted_iota(jnp.int32, (SEQ, RD), 0)
    lane = lax.broadcasted_iota(jnp.int32, (SEQ, RD), 1)
    pos = jnp.where((lane & 32) == 0, tok >> 6, tok & (GW - 1)).astype(F32)
    ang = pos * jnp.exp((lane & 31).astype(F32) * (-math.log(10000.0) / 32))
    return jnp.cos(ang), jnp.where(lane < 64, -jnp.sin(ang), jnp.sin(ang))


def _chunk_loop(n, body, init, k=4):
    def several(t, carry):
        for i in range(k):
            carry = body(k * t + i, carry)
        return carry

    return lax.fori_loop(0, n // k, several, init)


def _fiota(shape, dim):
    return lax.broadcasted_iota(jnp.int32, shape, dim).astype(F32)


def _ret_state(k, v, s, lg, reverse):
    pos = _fiota((CH, 1), 0)
    b_exp = pos if reverse else (CH - 1.0 - pos)
    return jnp.exp(lg * CH) * s + mm_tn(k * jnp.exp(lg * b_exp), v)


class _Decays:
    def __init__(self, lgs):
        i, j, pos = _fiota((CH, CH), 0), _fiota((CH, CH), 1), _fiota((CH, 1), 0)
        diffs = (i - j, j - i)
        keep = (diffs[0] >= 0, diffs[1] > 0)
        mats = [jnp.where(m, jnp.exp(lg * jnp.where(m, d, 0.0)), 0.0) for lg, d, m in zip(lgs, diffs, keep)]
        self.mask = mats[0] + mats[1]
        self.dmask = [mats[0] * diffs[0], mats[1] * diffs[1]]
        a_exp, b_exp = (pos + 1.0, CH - pos), (CH - 1.0 - pos, pos)
        self.a = [jnp.exp(lg * e) for lg, e in zip(lgs, a_exp)]
        self.b = [jnp.exp(lg * e) for lg, e in zip(lgs, b_exp)]
        self.da = [a * e for a, e in zip(self.a, a_exp)]
        self.db = [b * e for b, e in zip(self.b, b_exp)]
        self.g = [jnp.exp(lg * CH) for lg in lgs]


def _both(x, w):
    return jnp.concatenate([x * w[0], x * w[1]], axis=1)


def _total(x):
    return jnp.sum(jnp.sum(x, axis=1, keepdims=True), axis=0, keepdims=True)


def _state_pass(dec, init, k_s, v_of, st_s):
    def step(t, carry):
        out = []
        for d, s in enumerate(carry):
            n = (NCH - 1 - t) if d else t
            sl = pl.ds(pl.multiple_of(n * CH, CH), CH)
            st_s[n, d * RD:(d + 1) * RD, :] = s
            out.append(dec.g[d] * s + _tn(k_s[sl, :] * dec.b[d], v_of(sl)))
        return tuple(out)

    _chunk_loop(NCH, step, tuple(init))


def premix_proj(xin, mod3, g_pre, w, is_ctx, name, unpack=False):
    nb, length, _ = xin.shape
    tn = min(2 * TN, length)

    def body(x_ref, mod_ref, g_ref, w_ref, h_ref, pret_ref, pna_ref, *unpacked):
        if unpack:
            @pl.when((pl.program_id(0) == 0) & (pl.program_id(1) == 0))
            def _():
                _from_w_in_blocks(w_ref, unpacked[0])

        wp_ref = unpacked[0] if unpack else w_ref
        h = _rms_mod(x_ref[...], g_ref[...], mod_ref[1:2, :], mod_ref[0:1, :])
        hb = h.astype(BF16)
        h_ref[...] = hb
        pret_ref[...] = jnp.dot(hb, wp_ref[:, :RET_W], preferred_element_type=F32)
        pna_ref[...] = jnp.dot(hb, wp_ref[:, RET_W:], preferred_element_type=F32).astype(BF16)

    return pl.pallas_call(
        body, name=name, grid=(nb, length // tn),
        in_specs=[
            pl.BlockSpec((None, tn, D), lambda b, t: (b, t, 0)),
            pl.BlockSpec((None, 6, D), (lambda b, t: (2, 0, 0)) if is_ctx else (lambda b, t: (b, 0, 0))),
            pl.BlockSpec((1, D), lambda b, t: (0, 0)),
            pl.BlockSpec((4, D, 896), lambda b, t: (0, 0, 0), pipeline_mode=pl.Buffered(1)) if unpack else
            pl.BlockSpec((D, IN_W), lambda b, t: (0, 0), pipeline_mode=pl.Buffered(1)),
        ],
        out_specs=[
            pl.BlockSpec((None, tn, D), lambda b, t: (b, t, 0)),
            pl.BlockSpec((None, tn, RET_W), lambda b, t: (b, t, 0)),
            pl.BlockSpec((None, tn, IN_W - RET_W), lambda b, t: (b, t, 0)),
        ] + ([pl.BlockSpec((D, IN_W), lambda b, t: (0, 0))] if unpack else []),
        out_shape=[
            jax.ShapeDtypeStruct((nb, length, D), BF16),
            jax.ShapeDtypeStruct((nb, length, RET_W), F32),
            jax.ShapeDtypeStruct((nb, length, IN_W - RET_W), BF16),
        ] + ([jax.ShapeDtypeStruct((D, IN_W), BF16)] if unpack else []),
        compiler_params=_cp(("arbitrary", "arbitrary")),
    )(xin, mod3, g_pre, w)


def premix_bwd(xin, mod3, g_pre, wperm, dproj, dx_tail, hosted, name):
    nb, length, _ = xin.shape
    tn = min(TN, length)
    nt = length // tn
    is_ctx = dx_tail is None

    def body(*refs):
        own_in, h_in, own_out, h_out, (dp_buf, dp_sem), h_sems = hosted.split(
            refs, 5 if is_ctx else 6, 2 if is_ctx else 3)
        if is_ctx:
            (x_ref, mod_ref, g_ref, w_ref, dp_hbm), (dmod_ref, dg_ref) = own_in, own_out
        else:
            (x_ref, mod_ref, g_ref, w_ref, dp_hbm, dxt_ref), (dx_ref, dmod_ref, dg_ref) = own_in, own_out
        b, t = pl.program_id(0), pl.program_id(1)
        grid_step = b * nt + t

        def fetch(step):
            rows = pl.ds((step % nt) * tn, tn)
            return pltpu.make_async_copy(dp_hbm.at[step // nt, rows, :], dp_buf.at[step % 3], dp_sem.at[step % 3])

        @pl.when(grid_step == 0)
        def _():
            hosted.start(h_in, h_out, h_sems)
            for step in range(min(2, nb * nt)):
                fetch(step).start()

        @pl.when(grid_step + 2 < nb * nt)
        def _():
            fetch(grid_step + 2).start()

        @pl.when(grid_step == nb * nt - 1)
        def _():
            hosted.finish(h_in, h_out, h_sems)

        fetch(grid_step).wait()
        dh = lax.dot_general(dp_buf[grid_step % 3], w_ref[...], (((1,), (1,)), ((), ())), preferred_element_type=F32)
        _, vjp = jax.vjp(_rms_mod, x_ref[...], g_ref[...], mod_ref[1:2, :], mod_ref[0:1, :])
        dx, dg, dsc, dsh = vjp(dh)
        if not is_ctx:
            dx_ref[...] = dx + dxt_ref[...]

        @pl.when((t == 0) & ((b == 0) if is_ctx else True))
        def _():
            dmod_ref[...] = jnp.zeros_like(dmod_ref)

        @pl.when((t == 0) & (b == 0))
        def _():
            dg_ref[...] = jnp.zeros_like(dg_ref)

        dmod_ref[0:1, :] += dsh
        dmod_ref[1:2, :] += dsc
        dg_ref[0:1, :] += dg

    tok = lambda b, t: (b, t, 0)
    in_specs = [
        pl.BlockSpec((None, tn, D), tok),
        pl.BlockSpec((None, 6, D), (lambda b, t: (2, 0, 0)) if is_ctx else (lambda b, t: (b, 0, 0))),
        pl.BlockSpec((1, D), lambda b, t: (0, 0)),
        pl.BlockSpec((D, IN_W), lambda b, t: (0, 0), pipeline_mode=pl.Buffered(1)),
        pl.BlockSpec(memory_space=pl.ANY),
    ]
    args = [xin, mod3, g_pre, wperm, dproj]
    out_specs = [
        pl.BlockSpec((None, 6, D), (lambda b, t: (0, 0, 0)) if is_ctx else (lambda b, t: (b, 0, 0))),
        pl.BlockSpec((8, D), lambda b, t: (0, 0)),
    ]
    out_shape = [jax.ShapeDtypeStruct((1 if is_ctx else nb, 6, D), F32), jax.ShapeDtypeStruct((8, D), F32)]
    if not is_ctx:
        in_specs.append(pl.BlockSpec((None, tn, D), tok))
        args.append(dx_tail)
        out_specs.insert(0, pl.BlockSpec((None, tn, D), tok))
        out_shape.insert(0, jax.ShapeDtypeStruct((nb, length, D), F32))
    h_in_specs, h_out_specs = hosted.specs()
    return pl.pallas_call(
        body, name=name, grid=(nb, length // tn), in_specs=in_specs + h_in_specs, out_specs=out_specs + h_out_specs,
        out_shape=out_shape + hosted.out_shape,
        scratch_shapes=[pltpu.VMEM((3, tn, IN_W), BF16), pltpu.SemaphoreType.DMA((3,))] + hosted.scratch,
        compiler_params=_cp(("arbitrary", "arbitrary")),
    )(*args, *hosted.args)


def _ret_specs(order):
    def im(f):
        return lambda *g: f(*order(*g))
    return dict(
        pret=pl.BlockSpec((None, SEQ, 512), im(lambda b, h: (b, 0, h))),
        pretc=pl.BlockSpec((None, LC, 512), im(lambda b, h: (b, 0, h))),
        rd=pl.BlockSpec((None, 2, 1), im(lambda b, h: (h, 0, 0))),
        gn=pl.BlockSpec((None, 1, RD), im(lambda b, h: (h, 0, 0))),
        tab=pl.BlockSpec((SEQ, RD), im(lambda b, h: (0, 0))),
        head=pl.BlockSpec((None, SEQ, RD), im(lambda b, h: (b, 0, h))),
    )


def retention_fwd(pret, pretc, rd, gn, cos, sin, hosted):
    nb = pret.shape[0]
    sp = _ret_specs(lambda b, h: (b, h))

    def body(*refs):
        own_in, h_in, own_out, h_out, own_scr, h_sems = hosted.split(refs, 6, 2)
        p_ref, pc_ref, rd_ref, gn_ref, cos_ref, sin_ref = own_in
        (o_ref, mix_ref), (q_s, k_s, o_s, st_s) = own_out, own_scr
        grid_step = pl.program_id(0) * RH + pl.program_id(1)

        @pl.when(grid_step == 0)
        def _():
            hosted.start(h_in, h_out, h_sems)

        cos_v, sin_v = cos_ref[...], sin_ref[...]
        q_s[...] = _rope(p_ref[:, 0:128], cos_v, sin_v) * (RD ** -0.5)
        k_s[...] = _rope(p_ref[:, 128:256], cos_v, sin_v)
        lgs, init = [], []
        for rev in (False, True):
            lg = jax.nn.log_sigmoid(rd_ref[int(rev):int(rev) + 1, :])
            s = jnp.zeros((RD, RD), F32)
            for n in ((1, 0) if rev else (0, 1)):
                s = _ret_state(pc_ref[n * CH:(n + 1) * CH, 128:256], pc_ref[n * CH:(n + 1) * CH, 256:384], s, lg, rev)
            lgs.append(lg)
            init.append(s)

        dec = _Decays(lgs)
        _state_pass(dec, init, k_s, lambda sl: p_ref[sl, 256:384], st_s)

        def chunk(n, carry):
            sl = pl.ds(pl.multiple_of(n * CH, CH), CH)
            q = q_s[sl, :]
            o_s[sl, :] = (_nn(_nt(q, k_s[sl, :]) * dec.mask, p_ref[sl, 256:384]) + _nn(_both(q, dec.a), st_s[n]))
            return carry

        _chunk_loop(NCH, chunk, 0)
        o = o_s[...]
        o_ref[...] = o
        mix_ref[...] = _ln_gate(o, p_ref[:, 384:512], gn_ref[...]).astype(BF16)

        @pl.when(grid_step == nb * RH - 1)
        def _():
            hosted.finish(h_in, h_out, h_sems)

    h_in_specs, h_out_specs = hosted.specs()
    return pl.pallas_call(
        body, name="retention_fwd", grid=(nb, RH),
        in_specs=[sp["pret"], sp["pretc"], sp["rd"], sp["gn"], sp["tab"], sp["tab"]] + h_in_specs,
        out_specs=[sp["head"], sp["head"]] + h_out_specs,
        out_shape=[jax.ShapeDtypeStruct((nb, SEQ, RH * RD), F32), jax.ShapeDtypeStruct((nb, SEQ, D), BF16)]
        + hosted.out_shape,
        scratch_shapes=[pltpu.VMEM((SEQ, RD), F32)] * 3 + [pltpu.VMEM((NCH, 2 * RD, RD), F32)] + hosted.scratch,
        compiler_params=_cp(("arbitrary", "arbitrary")),
    )(pret, pretc, rd, gn, cos, sin, *hosted.args)


def retention_bwd(pret, pretc, o_all, dmixin, rd, gn, cos, sin, hosted):
    nb = pret.shape[0]
    sp = _ret_specs(lambda h, b: (b, h))

    def body(*refs):
        own_in, h_in, own_out, h_out, own_scr, h_sems = hosted.split(refs, 8, 4)
        p_ref, pc_ref, o_ref, dmix_ref, rd_ref, gn_ref, cos_ref, sin_ref = own_in
        dp_ref, dpc_ref, drd_ref, dgn_ref = own_out
        q_s, k_s, do_s, dq_s, dk_s, dv_s, st_s, gst_s = own_scr
        b = pl.program_id(1)
        grid_step = pl.program_id(0) * nb + b

        @pl.when(grid_step == 0)
        def _():
            hosted.start(h_in, h_out, h_sems)

        cos_v, sin_v = cos_ref[...], sin_ref[...]
        q_s[...] = _rope(p_ref[:, 0:128], cos_v, sin_v) * (RD ** -0.5)
        k_s[...] = _rope(p_ref[:, 128:256], cos_v, sin_v)
        _, gate_vjp = jax.vjp(_ln_gate, o_ref[...], p_ref[:, 384:512], gn_ref[...])
        do, dg, dgn = gate_vjp(dmix_ref[...].astype(F32))
        do_s[...] = do
        dp_ref[:, 384:512] = dg.astype(BF16)

        @pl.when(b == 0)
        def _():
            drd_ref[...] = jnp.zeros_like(drd_ref)
            dgn_ref[...] = jnp.zeros_like(dgn_ref)

        dgn_ref[...] += dgn
        kcs = [pc_ref[n * CH:(n + 1) * CH, 128:256] for n in (0, 1)]
        vcs = [pc_ref[n * CH:(n + 1) * CH, 256:384] for n in (0, 1)]
        dirs = []
        init = []
        for rev in (False, True):
            rdv = rd_ref[int(rev):int(rev) + 1, :]
            lg = jax.nn.log_sigmoid(rdv)
            order_c = (1, 0) if rev else (0, 1)
            s = jnp.zeros((RD, RD), F32)
            ctx_states = []
            for n in order_c:
                ctx_states.append(s)
                s = _ret_state(kcs[n], vcs[n], s, lg, rev)
            dirs.append((rev, order_c, lg, rdv, ctx_states))
            init.append(s)
        dec = _Decays([lg for _, _, lg, _, _ in dirs])

        def v_of(sl):
            return p_ref[sl, 256:384]

        _state_pass(dec, init, k_s, v_of, st_s)
        zeros = jnp.zeros((CH, RD), F32)

        def scores_back(n, carry):
            dmask_sum, da_f, da_b = carry
            sl = pl.ds(pl.multiple_of(n * CH, CH), CH)
            q, k, v, do = q_s[sl, :], k_s[sl, :], v_of(sl), do_s[sl, :]
            scores = _nt(q, k)
            d_att = _nt(do, v)
            d_scores = d_att * dec.mask
            d_qa = _nt(do, st_s[n])
            d_qf, d_qb = d_qa[:, 0:RD], d_qa[:, RD:2 * RD]
            dq_s[sl, :] = _nn(d_scores, k) + d_qf * dec.a[0] + d_qb * dec.a[1]
            dk_s[sl, :] = _tn(d_scores, q)
            dv_s[sl, :] = _tn(scores * dec.mask, do)
            gst_s[n] = _tn(_both(q, dec.a), do)
            return dmask_sum + d_att * scores, da_f + d_qf * q, da_b + d_qb * q

        dmask_sum, da_f, da_b = _chunk_loop(NCH, scores_back, (zeros, zeros, zeros))

        def state_back(t, carry):
            out = []
            for d, r in enumerate(carry):
                n = t if d else (NCH - 1 - t)
                rows = slice(d * RD, (d + 1) * RD)
                own = gst_s[n, rows, :]
                gst_s[n, rows, :] = r
                out.append(own + dec.g[d] * r)
            return tuple(out)

        d_states = _chunk_loop(NCH, state_back, (zeros, zeros))

        def updates_back(n, carry):
            db_f, db_b, dg_f, dg_b = carry
            sl = pl.ds(pl.multiple_of(n * CH, CH), CH)
            k, r, s = k_s[sl, :], gst_s[n], st_s[n]
            d_kw = _nt(v_of(sl), r)
            d_kf, d_kb = d_kw[:, 0:RD], d_kw[:, RD:2 * RD]
            dk_s[sl, :] += d_kf * dec.b[0] + d_kb * dec.b[1]
            dv_s[sl, :] += _nn(_both(k, dec.b), r)
            return (db_f + d_kf * k, db_b + d_kb * k, dg_f + r[0:RD, :] * s[0:RD, :],
                    dg_b + r[RD:2 * RD, :] * s[RD:2 * RD, :])

        db_dg = _chunk_loop(NCH, updates_back, (zeros, zeros, zeros, zeros))
        dkc = [None, None]
        dvc = [None, None]
        for d, ((rev, order_c, lg, rdv, ctx_states), ds) in enumerate(zip(dirs, d_states)):
            dlg = (_total(dmask_sum * dec.dmask[d]) + _total((da_f, da_b)[d] * dec.da[d])
                   + _total(db_dg[d] * dec.db[d]) + CH * dec.g[d] * _total(db_dg[2 + d]))
            for idx in (1, 0):
                n = order_c[idx]
                _, vjp = jax.vjp(functools.partial(_ret_state, reverse=rev), kcs[n], vcs[n], ctx_states[idx], lg)
                dk_c, dv_c, ds, dl = vjp(ds)
                dlg = dlg + dl
                dkc[n] = dk_c if dkc[n] is None else dkc[n] + dk_c
                dvc[n] = dv_c if dvc[n] is None else dvc[n] + dv_c
            drd_ref[int(rev):int(rev) + 1, :] += dlg * jax.nn.sigmoid(-rdv)
        dp_ref[:, 0:128] = _rope_t(dq_s[...] * (RD ** -0.5), cos_v, sin_v).astype(BF16)
        dp_ref[:, 128:256] = _rope_t(dk_s[...], cos_v, sin_v).astype(BF16)
        dp_ref[:, 256:384] = dv_s[...].astype(BF16)
        zero = jnp.zeros((CH, RD), BF16)
        for n in (0, 1):
            rows = slice(n * CH, (n + 1) * CH)
            dpc_ref[rows, 0:128] = zero
            dpc_ref[rows, 128:256] = dkc[n].astype(BF16)
            dpc_ref[rows, 256:384] = dvc[n].astype(BF16)
            dpc_ref[rows, 384:512] = zero

        @pl.when(grid_step == RH * nb - 1)
        def _():
            hosted.finish(h_in, h_out, h_sems)

    h_in_specs, h_out_specs = hosted.specs()
    return pl.pallas_call(
        body, name="retention_bwd", grid=(RH, nb),
        in_specs=[sp["pret"], sp["pretc"], sp["head"], sp["head"], sp["rd"], sp["gn"], sp["tab"], sp["tab"]]
        + h_in_specs,
        out_specs=[
            pl.BlockSpec((None, SEQ, 512), lambda h, b: (b, 0, h)),
            pl.BlockSpec((None, LC, 512), lambda h, b: (b, 0, h)),
            pl.BlockSpec((None, 2, 1), lambda h, b: (h, 0, 0)),
            pl.BlockSpec((None, 1, RD), lambda h, b: (h, 0, 0)),
        ] + h_out_specs,
        out_shape=[
            jax.ShapeDtypeStruct((nb, SEQ, IN_W), BF16),
            jax.ShapeDtypeStruct((nb, LC, IN_W), BF16),
            jax.ShapeDtypeStruct((RH, 2, 1), F32),
            jax.ShapeDtypeStruct((RH, 1, RD), F32),
        ] + hosted.out_shape,
        scratch_shapes=[pltpu.VMEM((SEQ, RD), F32)] * 6 + [pltpu.VMEM((NCH, 2 * RD, RD), F32)] * 2 + hosted.scratch,
        compiler_params=_cp(("arbitrary", "arbitrary")),
    )(pret, pretc, o_all, dmixin, rd, gn, cos, sin, *hosted.args)


def _rpb_flat(rpb):
    return jnp.pad(rpb, ((0, 0), (0, 1), (0, 33))).reshape(NPAIR, 2, 1, 1024)


def _rpb_flat_t(dflat):
    return dflat.reshape(8, 16, 64)[:, :15, :31]


def _barrel(x, left):
    row = lax.broadcasted_iota(jnp.int32, x.shape, 0)
    n = x.shape[1]
    for bit in range(6):
        s = 1 << bit
        x = jnp.where(((row >> bit) & 1) == 1, pltpu.roll(x, (n - s) if left else s, 1), x)
    return x


NA_TILE_ROWS, NA_BAND_ROWS = 4, 12
NA_Q, NA_K = NA_TILE_ROWS * GW, NA_BAND_ROWS * GW
NA_TILES = SEQ // NA_Q


def _band_start(r0):
    return min(max(r0 - 4, 0), 32 - NA_BAND_ROWS)


def _tile_layout(t):
    rows = range(t * NA_TILE_ROWS, (t + 1) * NA_TILE_ROWS)
    return tuple((r if r < 4 else (r - 24 if r > 28 else 4), min(max(r - 4, 0), 24) - _band_start(rows[0]))
                 for r in rows)


NA_CLASSES = sorted(set(_tile_layout(t) for t in range(NA_TILES)))


def _tile_rows(cls):
    return NA_CLASSES[cls]


def _na_tile(t):
    start = jnp.clip(NA_TILE_ROWS * t - 4, 0, 32 - NA_BAND_ROWS)
    cls = 0
    for tile in range(NA_TILES):
        cls = jnp.where(t == tile, NA_CLASSES.index(_tile_layout(tile)), cls)
    return pl.ds(pl.multiple_of(t * NA_Q, NA_Q), NA_Q), pl.ds(pl.multiple_of(start * GW, NA_Q), NA_K), cls


def _na_probs(qst, kb, kc, bias):
    s_loc = _nt(qst, kb) + bias
    s_ctx = _nt(qst, kc)
    m = jnp.maximum(jnp.max(s_loc, axis=1, keepdims=True), jnp.max(s_ctx, axis=1, keepdims=True))
    e_loc, e_ctx = jnp.exp(s_loc - m), jnp.exp(s_ctx - m)
    den = jnp.sum(e_loc, axis=1, keepdims=True) + jnp.sum(e_ctx, axis=1, keepdims=True)
    return e_loc / den, e_ctx / den


def _stack_heads(t):
    lane = lax.broadcasted_iota(jnp.int32, t.shape, 1)
    zero = jnp.zeros_like(t)
    return jnp.concatenate([jnp.where(lane < 64, t, zero), jnp.where(lane >= 64, t, zero)], axis=0)


def _unstack_heads(t):
    n = t.shape[0] // 2
    lane = lax.broadcasted_iota(jnp.int32, (n, 128), 1)
    return jnp.where(lane < 64, t[:n], t[n:])


NA_BIAS_SHAPE = (len(NA_CLASSES), 2 * NA_Q, NA_K)


def _na_bias_pair(flat_ref, out_ref):
    qc = lax.broadcasted_iota(jnp.int32, (GW, 512), 0)
    kc = lax.broadcasted_iota(jnp.int32, (GW, 512), 1) & 63
    start = jnp.clip(qc - 8, 0, GW - 16)
    window = (kc >= start) & (kc < start + 16)
    fill = jnp.full((GW, NA_K - 512), NEG, F32)
    for hh in (0, 1):
        skew = _barrel(pltpu.roll(jnp.broadcast_to(flat_ref[hh], (GW, 1024)), 1024 - 15, 1), left=False)
        by_class = [jnp.where(window, (skew if rc == 7 else pltpu.roll(skew, (9 + rc) * 64, 1))[:, 0:512], NEG)
                    for rc in range(8)]
        for cls in range(len(NA_CLASSES)):
            for qr, (rc, off) in enumerate(_tile_rows(cls)):
                w = jnp.concatenate([by_class[rc], fill], axis=1)
                rows = slice(hh * NA_Q + qr * GW, hh * NA_Q + (qr + 1) * GW)
                out_ref[cls, rows, :] = pltpu.roll(w, off * GW, 1) if off else w


def na_fwd(pna, pnac, bias, mixin, hosted):
    nb = pna.shape[0]

    def body(*refs):
        (p_ref, pc_ref, bias_ref, _), h_in, (out_ref,), h_out, _, h_sems = hosted.split(refs, 4, 1)
        grid_step = pl.program_id(0) * nb + pl.program_id(1)

        @pl.when(grid_step == 0)
        def _():
            hosted.start(h_in, h_out, h_sems)

        kc, vc = pc_ref[:, 128:256], pc_ref[:, 256:384]

        def tile(t, carry):
            qsl, bsl, cls = _na_tile(t)
            kb, vb = p_ref[bsl, 128:256], p_ref[bsl, 256:384]
            p_loc, p_ctx = _na_probs(_stack_heads(p_ref[qsl, 0:128] * 0.125), kb, kc, bias_ref[cls])
            out_ref[qsl, :] = _unstack_heads(_nn(p_loc, vb) + _nn(p_ctx, vc)).astype(BF16)
            return carry

        lax.fori_loop(0, NA_TILES, tile, 0, unroll=4)

        @pl.when(grid_step == NPAIR * nb - 1)
        def _():
            hosted.finish(h_in, h_out, h_sems)

    h_in_specs, h_out_specs = hosted.specs()
    return pl.pallas_call(
        body, name="na_fwd", grid=(NPAIR, nb),
        in_specs=[
            pl.BlockSpec((None, SEQ, 384), lambda p, b: (b, 0, p)),
            pl.BlockSpec((None, LC, 384), lambda p, b: (b, 0, p)),
            pl.BlockSpec((None, len(NA_CLASSES), 2 * NA_Q, NA_K), lambda p, b: (p, 0, 0, 0)),
            pl.BlockSpec(memory_space=pl.ANY),
        ] + h_in_specs,
        out_specs=[pl.BlockSpec((None, SEQ, 128), lambda p, b: (b, 0, 4 + p))] + h_out_specs,
        out_shape=[jax.ShapeDtypeStruct((nb, SEQ, D), BF16)] + hosted.out_shape,
        input_output_aliases={3: 0},
        scratch_shapes=hosted.scratch,
        compiler_params=_cp(("arbitrary", "arbitrary")),
    )(pna, pnac, bias, mixin, *hosted.args)


def na_bwd(pna, pnac, bias, dmixin, dproj, dprojc, hosted):
    nb = pna.shape[0]

    def body(*refs):
        own_in, h_in, own_out, h_out, own_scr, h_sems = hosted.split(refs, 6, 3)
        p_ref, pc_ref, bias_ref, dmix_ref = own_in[:4]
        dp_ref, dpc_ref, dpat_ref = own_out
        dbias_s, dk_s, dv_s, dkc_s, dvc_s, res_s, resc_s = own_scr
        b, part = pl.program_id(1), pl.program_id(2)
        grid_step = (pl.program_id(0) * nb + b) * 3 + part

        @pl.when(grid_step == 0)
        def _():
            hosted.start(h_in, h_out, h_sems)

        @pl.when(grid_step == NPAIR * nb * 3 - 1)
        def _():
            hosted.finish(h_in, h_out, h_sems)

        @pl.when(part == 0)
        def _():
            @pl.when(b == 0)
            def _():
                dbias_s[...] = jnp.zeros_like(dbias_s)

            dk_s[...] = jnp.zeros_like(dk_s)
            dv_s[...] = jnp.zeros_like(dv_s)
            dkc_s[...] = jnp.zeros_like(dkc_s)
            dvc_s[...] = jnp.zeros_like(dvc_s)
            kc, vc = pc_ref[:, 128:256], pc_ref[:, 256:384]

            def tile(t, carry):
                qsl, bsl, cls = _na_tile(t)
                kb, vb = p_ref[bsl, 128:256], p_ref[bsl, 256:384]
                qst, dost = _stack_heads(p_ref[qsl, 0:128] * 0.125), _stack_heads(dmix_ref[qsl, :])
                p_loc, p_ctx = _na_probs(qst, kb, kc, bias_ref[cls])
                dp_loc, dp_ctx = _nt(dost, vb), _nt(dost, vc)
                delta = (jnp.sum(p_loc * dp_loc, axis=1, keepdims=True)
                         + jnp.sum(p_ctx * dp_ctx, axis=1, keepdims=True))
                ds_loc, ds_ctx = p_loc * (dp_loc - delta), p_ctx * (dp_ctx - delta)
                dbias_s[cls] += ds_loc
                res_s[0, qsl, :] = _unstack_heads((_nn(ds_loc, kb) + _nn(ds_ctx, kc)) * 0.125).astype(BF16)
                dk_s[bsl, :] += _tn(ds_loc, qst)
                dv_s[bsl, :] += _tn(p_loc, dost)
                dkc_s[...] += _tn(ds_ctx, qst)
                dvc_s[...] += _tn(p_ctx, dost)
                return carry

            lax.fori_loop(0, NA_TILES, tile, 0, unroll=2)
            res_s[1] = dk_s[...].astype(BF16)
            res_s[2] = dv_s[...].astype(BF16)
            resc_s[0] = jnp.zeros((LC, 128), BF16)
            resc_s[1] = dkc_s[...].astype(BF16)
            resc_s[2] = dvc_s[...].astype(BF16)

            @pl.when(b == nb - 1)
            def _():
                for hh in (0, 1):
                    by_class = [None] * 8
                    for cls in range(len(NA_CLASSES)):
                        for qr, (rc, off) in enumerate(_tile_rows(cls)):
                            w = dbias_s[cls, hh * NA_Q + qr * GW:hh * NA_Q + (qr + 1) * GW, :]
                            w = (pltpu.roll(w, NA_K - off * GW, 1) if off else w)[:, 0:512]
                            by_class[rc] = w if by_class[rc] is None else by_class[rc] + w
                    skew = jnp.zeros((GW, 1024), F32)
                    for rc in range(8):
                        w = jnp.concatenate([by_class[rc], jnp.zeros((GW, 512), F32)], axis=1)
                        skew = skew + (w if rc == 7 else pltpu.roll(w, (7 - rc) * 64, 1))
                    dpat_ref[hh] = jnp.sum(pltpu.roll(_barrel(skew, left=True), 15, 1), axis=0, keepdims=True)

        dp_ref[...] = res_s[part]
        dpc_ref[...] = resc_s[part]

    h_in_specs, h_out_specs = hosted.specs()
    return pl.pallas_call(
        body, name="na_bwd", grid=(NPAIR, nb, 3),
        in_specs=[
            pl.BlockSpec((None, SEQ, 384), lambda p, b, s: (b, 0, p)),
            pl.BlockSpec((None, LC, 384), lambda p, b, s: (b, 0, p)),
            pl.BlockSpec((None, len(NA_CLASSES), 2 * NA_Q, NA_K), lambda p, b, s: (p, 0, 0, 0)),
            pl.BlockSpec((None, SEQ, 128), lambda p, b, s: (b, 0, 4 + p)),
            pl.BlockSpec(memory_space=pl.ANY),
            pl.BlockSpec(memory_space=pl.ANY),
        ] + h_in_specs,
        out_specs=[
            pl.BlockSpec((None, SEQ, 128), lambda p, b, s: (b, 0, 16 + 3 * p + s)),
            pl.BlockSpec((None, LC, 128), lambda p, b, s: (b, 0, 16 + 3 * p + s)),
            pl.BlockSpec((None, 2, 1, 1024), lambda p, b, s: (p, 0, 0, 0)),
        ] + h_out_specs,
        out_shape=[
            jax.ShapeDtypeStruct((nb, SEQ, IN_W), BF16),
            jax.ShapeDtypeStruct((nb, LC, IN_W), BF16),
            jax.ShapeDtypeStruct((NPAIR, 2, 1, 1024), F32),
        ] + hosted.out_shape,
        input_output_aliases={4: 0, 5: 1},
        scratch_shapes=[
            pltpu.VMEM((len(NA_CLASSES), 2 * NA_Q, NA_K), F32),
            pltpu.VMEM((SEQ, 128), F32), pltpu.VMEM((SEQ, 128), F32),
            pltpu.VMEM((LC, 128), F32), pltpu.VMEM((LC, 128), F32),
            pltpu.VMEM((3, SEQ, 128), BF16), pltpu.VMEM((3, LC, 128), BF16),
        ] + hosted.scratch,
        compiler_params=_cp(("arbitrary", "arbitrary", "arbitrary")),
    )(pna, pnac, bias, dmixin, dproj, dprojc, *hosted.args)


def tail_fwd_bwd(x, mixin, tgt, mod3, g_post_mix, g_pre_mlp, g_post_mlp, wout, w1, w2):
    nb = x.shape[0]

    def body(x_ref, mi_ref, tgt_ref, mod_ref, gpm_ref, gpl_ref, gpo_ref, wo_ref, w1_ref, w2_ref,
             dx_ref, dmix_ref, h2_ref, du_ref, a_ref, dm_ref, dmi_ref, dmod_ref, dg_ref, loss_ref):
        b, t = pl.program_id(0), pl.program_id(1)
        gt1, sh2, sc2, gt2 = mod_ref[2:3, :], mod_ref[3:4, :], mod_ref[4:5, :], mod_ref[5:6, :]
        mix = jnp.dot(mi_ref[...], wo_ref[...], preferred_element_type=F32)
        (x1, h2), vjp_a = jax.vjp(_post_mix, x_ref[...], mix, gt1, sc2, sh2, gpm_ref[...], gpl_ref[...])
        h2b = h2.astype(BF16)
        h2_ref[...] = h2b
        m = jnp.zeros((TN, D), F32)
        relus = []
        for j in range(4):
            cols = slice(j * D, (j + 1) * D)
            r = jnp.maximum(jnp.dot(h2b, w1_ref[j], preferred_element_type=F32), 0.0)
            ab = (r * r).astype(BF16)
            a_ref[:, cols] = ab
            m = m + jnp.dot(ab, w2_ref[cols, :], preferred_element_type=F32)
            relus.append(r)
        loss, vjp_b = jax.vjp(_head_loss, x1, m, gt2, gpo_ref[...], tgt_ref[...])
        dx1, dm, dgt2, dgpo, _ = vjp_b(jnp.ones((1, 1), F32))
        dmb = dm.astype(BF16)
        dm_ref[...] = dmb
        dh2 = jnp.zeros((TN, D), F32)
        for j in range(4):
            cols = slice(j * D, (j + 1) * D)
            da = lax.dot_general(dmb, w2_ref[cols, :], (((1,), (1,)), ((), ())), preferred_element_type=F32)
            dub = (da * (2.0 * relus[j])).astype(BF16)
            du_ref[:, cols] = dub
            dh2 = dh2 + lax.dot_general(dub, w1_ref[j], (((1,), (1,)), ((), ())), preferred_element_type=F32)
        dx, dmix, dgt1, dsc2, dsh2, dgpm, dgpl = vjp_a((dx1, dh2))
        dx_ref[...] = dx
        dmixb = dmix.astype(BF16)
        dmix_ref[...] = dmixb
        dmi_ref[...] = lax.dot_general(dmixb, wo_ref[...], (((1,), (1,)), ((), ())),
                                       preferred_element_type=F32).astype(BF16)

        @pl.when(t == 0)
        def _():
            dmod_ref[...] = jnp.zeros_like(dmod_ref)

        @pl.when((t == 0) & (b == 0))
        def _():
            dg_ref[...] = jnp.zeros_like(dg_ref)
            loss_ref[...] = jnp.zeros_like(loss_ref)

        dmod_ref[2:3, :] += dgt1
        dmod_ref[3:4, :] += dsh2
        dmod_ref[4:5, :] += dsc2
        dmod_ref[5:6, :] += dgt2
        dg_ref[0:1, :] += dgpm
        dg_ref[1:2, :] += dgpl
        dg_ref[2:3, :] += dgpo
        loss_ref[...] += jnp.broadcast_to(loss, loss_ref.shape)

    tok = lambda b, t: (b, t, 0)
    const = lambda b, t: (0, 0)
    vec = pl.BlockSpec((1, D), const)
    return pl.pallas_call(
        body, name="tail_fwd_bwd", grid=(nb, SEQ // TN),
        in_specs=[
            pl.BlockSpec((None, TN, D), tok), pl.BlockSpec((None, TN, D), tok), pl.BlockSpec((None, TN, D), tok),
            pl.BlockSpec((None, 6, D), lambda b, t: (b, 0, 0)), vec, vec, vec,
            pl.BlockSpec((D, D), const, pipeline_mode=pl.Buffered(1)),
            pl.BlockSpec((4, D, D), lambda b, t: (0, 0, 0), pipeline_mode=pl.Buffered(1)),
            pl.BlockSpec((DFF, D), const, pipeline_mode=pl.Buffered(1)),
        ],
        out_specs=[
            pl.BlockSpec((None, TN, D), tok), pl.BlockSpec((None, TN, D), tok), pl.BlockSpec((None, TN, D), tok),
            pl.BlockSpec((None, TN, DFF), tok), pl.BlockSpec((None, TN, DFF), tok), pl.BlockSpec((None, TN, D), tok),
            pl.BlockSpec((None, TN, D), tok),
            pl.BlockSpec((None, 6, D), lambda b, t: (b, 0, 0)),
            pl.BlockSpec((8, D), const), pl.BlockSpec((8, 128), const),
        ],
        out_shape=[
            jax.ShapeDtypeStruct((nb, SEQ, D), F32), jax.ShapeDtypeStruct((nb, SEQ, D), BF16),
            jax.ShapeDtypeStruct((nb, SEQ, D), BF16), jax.ShapeDtypeStruct((nb, SEQ, DFF), BF16),
            jax.ShapeDtypeStruct((nb, SEQ, DFF), BF16), jax.ShapeDtypeStruct((nb, SEQ, D), BF16),
            jax.ShapeDtypeStruct((nb, SEQ, D), BF16),
            jax.ShapeDtypeStruct((nb, 6, D), F32), jax.ShapeDtypeStruct((8, D), F32),
            jax.ShapeDtypeStruct((8, 128), F32),
        ],
        compiler_params=_cp(("arbitrary", "arbitrary")),
    )(x, mixin, tgt, mod3, g_post_mix, g_pre_mlp, g_post_mlp, wout, w1, w2)


def weight_grad(pairs, name, out_dtype=F32, col_blocks=False, tm=1024, tn=1024, tk=2048, w_in_blocks=False):
    m, n = pairs[0][0].shape[1], pairs[0][1].shape[1]
    tn = n if w_in_blocks else min(tn, n)
    tks = [min(tk, xa.shape[0]) for xa, _ in pairs]
    steps = [xa.shape[0] // t for (xa, _), t in zip(pairs, tks)]
    total = sum(steps)
    offs = [sum(steps[:i]) for i in range(len(pairs))]

    def body(*refs):
        out_ref, acc = refs[2 * len(pairs)], refs[-1]
        k = pl.program_id(2)

        @pl.when(k == 0)
        def _():
            acc[...] = jnp.zeros_like(acc)

        for i in range(len(pairs)):
            @pl.when((k >= offs[i]) & (k < offs[i] + steps[i]))
            def _(i=i):
                acc[...] += lax.dot_general(refs[2 * i][...], refs[2 * i + 1][...], (((0,), (0,)), ((), ())),
                                            preferred_element_type=F32)

        if out_dtype != F32:
            @pl.when(k == total - 1)
            def _():
                if w_in_blocks:
                    _to_w_in_blocks(acc, out_ref)
                else:
                    out_ref[...] = acc[...].astype(out_dtype)

    in_specs, args = [], []
    for i, (xa, ya) in enumerate(pairs):
        clamp = lambda k, i=i: jnp.clip(k - offs[i], 0, steps[i] - 1)
        in_specs.append(pl.BlockSpec((tks[i], tm), lambda a, c, k, clamp=clamp: (clamp(k), a)))
        in_specs.append(pl.BlockSpec((tks[i], tn), lambda a, c, k, clamp=clamp: (clamp(k), c)))
        args += [xa, ya]
    if w_in_blocks:
        assert n == IN_W and out_dtype == BF16
        out_spec = pl.BlockSpec((4, tm, 896), lambda a, c, k: (0, a, 0))
        out_shape = jax.ShapeDtypeStruct((4, m, 896), BF16)
    elif col_blocks:
        out_spec = pl.BlockSpec((None, tm, tn), lambda a, c, k: (c, a, 0))
        out_shape = jax.ShapeDtypeStruct((n // tn, m, tn), out_dtype)
    else:
        out_spec = pl.BlockSpec((tm, tn), lambda a, c, k: (a, c))
        out_shape = jax.ShapeDtypeStruct((m, n), out_dtype)
    return pl.pallas_call(
        body, name=name, grid=(m // tm, n // tn, total), in_specs=in_specs, out_specs=out_spec, out_shape=out_shape,
        scratch_shapes=[] if out_dtype == F32 else [pltpu.VMEM((tm, tn), F32)],
        compiler_params=_cp(("arbitrary", "arbitrary", "arbitrary")),
    )(*args)


def _perm_block(t):
    return 4 * (t % 4) + t // 4 if t < 16 else 16 + 3 * ((t - 16) % 4) + (t - 16) // 4


def _is_rope_block(p):
    return p < 16 and p % 4 < 2


def _from_w_in_blocks(i_ref, o_ref):
    for t in range(28):
        p = _perm_block(t)
        blk = i_ref[t // 7, :, (t % 7) * 128:(t % 7 + 1) * 128]
        if _is_rope_block(p):
            blk = _pair_order(blk.astype(F32)).astype(BF16)
        o_ref[:, p * 128:(p + 1) * 128] = blk


def _to_w_in_blocks(i_ref, o_ref):
    for t in range(28):
        p = _perm_block(t)
        blk = i_ref[:, p * 128:(p + 1) * 128]
        if _is_rope_block(p):
            blk = _pair_order(blk)
        o_ref[t // 7, :, (t % 7) * 128:(t % 7 + 1) * 128] = blk.astype(BF16)


def _place():
    return lax.axis_index("x"), lax.axis_index("y"), lax.axis_index("c")


class Hosted:
    def __init__(self, args, out_shape, scratch, start, finish):
        self.args, self.out_shape, self.scratch, self.start, self.finish = args, out_shape, scratch, start, finish

    def specs(self):
        hbm = pl.BlockSpec(memory_space=pl.ANY)
        return [hbm] * len(self.args), [hbm] * len(self.out_shape)

    def split(self, refs, n_in, n_out):
        a, b = len(self.args), len(self.out_shape)
        cuts = [n_in, n_in + a, n_in + a + n_out, n_in + a + n_out + b, len(refs) - len(self.scratch)]
        parts = [refs[i:j] for i, j in zip([0] + cuts, cuts + [len(refs)])]
        return parts[0], parts[1], parts[2], parts[3], parts[4], parts[5]


def no_exchange():
    return Hosted([], [], [], lambda *a: None, lambda *a: None)


def run_hosted(hosted, name):
    def body(*refs):
        _, ins, _, outs, _, sems = hosted.split(refs, 0, 0)
        hosted.start(ins, outs, sems)
        hosted.finish(ins, outs, sems)

    in_specs, out_specs = hosted.specs()
    return pl.pallas_call(body, name=name, in_specs=in_specs, out_specs=out_specs, out_shape=hosted.out_shape,
                          scratch_shapes=hosted.scratch)(*hosted.args)


def gather8(blocks, relay_diagonal=False):
    na = len(blocks)

    def copies(ins, outs, sems):
        send_sems, recv_sems, local_sem = sems
        x, y, c = _place()
        me, sibling = (x, y, c), (x, y, 1 - c)
        chips = [(1 - x, y), (x, 1 - y), (1 - x, 1 - y)]

        def slot(o_ref, px, py, pc, half=None):
            ref = o_ref.at[4 * px + 2 * py + pc]
            if half is None:
                return ref
            rows = ref.shape[0] // 2
            return ref.at[pl.ds(half * rows, rows)]

        def copy(a, k, block, to, src=None, half=None):
            return pltpu.make_async_remote_copy(
                src_ref=slot(outs[a], *block, half) if src is None else src, dst_ref=slot(outs[a], *block, half),
                send_sem=send_sems.at[a, k], recv_sem=recv_sems.at[a, k], device_id=to, device_id_type=MESH)

        mine = [pltpu.make_async_copy(ins[a], slot(outs[a], *me), local_sem.at[a]) for a in range(na)]
        first = []
        for a in range(na):
            first.append(copy(a, 0, me, sibling, src=ins[a]))
            first += [copy(a, 1 + j, me, (*chip, c), src=ins[a])
                      for j, chip in enumerate(chips[:2] if relay_diagonal else chips)]
        return copy, mine, first, me, sibling, chips, c

    def start(ins, outs, sems):
        _, mine, first, *_ = copies(ins, outs, sems)
        for cp in mine + first:
            cp.start()

    def finish(ins, outs, sems):
        copy, mine, first, me, sibling, chips, c = copies(ins, outs, sems)
        passed = []
        for j, chip in enumerate(chips[:2] if relay_diagonal else chips):
            for a in range(na):
                copy(a, 1 + j, (*chip, c), me).wait_recv()
                onward = [copy(a, 4 + j, (*chip, c), sibling)]
                if relay_diagonal:
                    onward.insert(0, copy(a, (3, 7)[j], (*chip, c), (*chips[1 - j], c), half=j))
                for cp in onward:
                    cp.start()
                passed += onward
        if relay_diagonal:
            for a in range(na):
                copy(a, 3, (*chips[2], c), me, half=0).wait_recv()
                copy(a, 7, (*chips[2], c), me, half=1).wait_recv()
                cp = copy(a, 6, (*chips[2], c), sibling)
                cp.start()
                passed.append(cp)
        for a in range(na):
            copy(a, 0, sibling, me).wait_recv()
            for j, chip in enumerate(chips):
                copy(a, 4 + j, (*chip, 1 - c), me).wait_recv()
        for cp in first + passed:
            cp.wait_send()
        for cp in mine:
            cp.wait()

    return Hosted(list(blocks), [jax.ShapeDtypeStruct((8,) + b.shape, b.dtype) for b in blocks],
                  [pltpu.SemaphoreType.DMA((na, 8)), pltpu.SemaphoreType.DMA((na, 8)), pltpu.SemaphoreType.DMA((na,))],
                  start, finish)


def chips3(arrays):
    na = len(arrays)

    def copies(ins, outs, sems):
        send_sems, recv_sems = sems
        x, y, c = _place()
        return [pltpu.make_async_remote_copy(
            src_ref=ins[a].at[2 * px + py], dst_ref=outs[a].at[k], send_sem=send_sems.at[a, k],
            recv_sem=recv_sems.at[a, k], device_id=(px, py, c), device_id_type=MESH)
            for a in range(na) for k, (px, py) in enumerate([(1 - x, y), (x, 1 - y), (1 - x, 1 - y)])]

    def start(ins, outs, sems):
        for cp in copies(ins, outs, sems):
            cp.start()

    def finish(ins, outs, sems):
        for cp in copies(ins, outs, sems):
            cp.wait()

    return Hosted(list(arrays), [jax.ShapeDtypeStruct((3,) + a.shape[1:], a.dtype) for a in arrays],
                  [pltpu.SemaphoreType.DMA((na, 3)), pltpu.SemaphoreType.DMA((na, 3))], start, finish)


def siblings(arrays):
    na = len(arrays)

    def copies(ins, outs, sems):
        send_sems, recv_sems = sems
        x, y, c = _place()
        return [pltpu.make_async_remote_copy(
            src_ref=ins[a], dst_ref=outs[a], send_sem=send_sems.at[a], recv_sem=recv_sems.at[a],
            device_id=(x, y, 1 - c), device_id_type=MESH) for a in range(na)]

    def start(ins, outs, sems):
        for cp in copies(ins, outs, sems):
            cp.start()

    def finish(ins, outs, sems):
        for cp in copies(ins, outs, sems):
            cp.wait()

    return Hosted(list(arrays), [jax.ShapeDtypeStruct(a.shape, a.dtype) for a in arrays],
                  [pltpu.SemaphoreType.DMA((na,)), pltpu.SemaphoreType.DMA((na,))], start, finish)


def both(first, second):
    na, no, ns = len(first.args), len(first.out_shape), len(first.scratch)

    def start(ins, outs, sems):
        first.start(ins[:na], outs[:no], sems[:ns])
        second.start(ins[na:], outs[no:], sems[ns:])

    def finish(ins, outs, sems):
        first.finish(ins[:na], outs[:no], sems[:ns])
        second.finish(ins[na:], outs[no:], sems[ns:])

    return Hosted(first.args + second.args, first.out_shape + second.out_shape, first.scratch + second.scratch,
                  start, finish)


def siblings4(arrays):
    na = len(arrays)

    def copies(ins, outs, sems):
        send_sems, recv_sems = sems
        x, y, c = _place()
        return [pltpu.make_async_remote_copy(
            src_ref=ins[a].at[2 * j + 1 - c], dst_ref=outs[a].at[j],
            send_sem=send_sems.at[a, j], recv_sem=recv_sems.at[a, j],
            device_id=(x, y, 1 - c), device_id_type=MESH) for a in range(na) for j in range(4)]

    def start(ins, outs, sems):
        for cp in copies(ins, outs, sems):
            cp.start()

    def finish(ins, outs, sems):
        for cp in copies(ins, outs, sems):
            cp.wait()

    return Hosted(list(arrays), [jax.ShapeDtypeStruct((4,) + a.shape[1:], a.dtype) for a in arrays],
                  [pltpu.SemaphoreType.DMA((na, 4)), pltpu.SemaphoreType.DMA((na, 4))], start, finish)


def _row_tile(r):
    for cand in (512, 256, 128, 64, 32, 16, 8):
        if r % cand == 0:
            return cand
    return r


def chip_partial(place, g8s, landed4s, name):
    n = len(g8s)

    def body(place_ref, *refs):
        del place_ref
        for g_ref, l_ref, o_ref in zip(refs[:n], refs[n:2 * n], refs[2 * n:]):
            o_ref[...] = (g_ref[...].astype(F32) + l_ref[...].astype(F32)).astype(BF16)

    own = [pl.BlockSpec((None,) + g.shape[1:], lambda j, s: (2 * j + s[0], 0, 0)) for g in g8s]
    plain = [pl.BlockSpec((None,) + g.shape[1:], lambda j, s: (j, 0, 0)) for g in g8s]
    return pl.pallas_call(
        body, name=name,
        grid_spec=pltpu.PrefetchScalarGridSpec(num_scalar_prefetch=1, grid=(4,), in_specs=own + plain, out_specs=plain),
        out_shape=[jax.ShapeDtypeStruct((4,) + g.shape[1:], BF16) for g in g8s],
    )(place, *g8s, *landed4s)


def shard_sum(place, partial4s, landed3s, name):
    n = len(partial4s)

    def body(place_ref, *refs):
        del place_ref
        for p_ref, l_ref, o_ref in zip(refs[:n], refs[n:2 * n], refs[2 * n:]):
            acc = p_ref[...].astype(F32)
            for k in range(3):
                acc = acc + l_ref[k].astype(F32)
            o_ref[...] = acc

    def halves(p, lead):
        r, ccols = p.shape[1:]
        return (lead, r // 2, ccols)

    return pl.pallas_call(
        body, name=name,
        grid_spec=pltpu.PrefetchScalarGridSpec(
            num_scalar_prefetch=1, grid=(2,),
            in_specs=[pl.BlockSpec(halves(p, None), lambda i, s: (s[1], i, 0)) for p in partial4s]
            + [pl.BlockSpec(halves(p, 3), lambda i, s: (0, i, 0)) for p in partial4s],
            out_specs=[pl.BlockSpec(halves(p, None)[1:], lambda i, s: (i, 0)) for p in partial4s]),
        out_shape=[jax.ShapeDtypeStruct(p.shape[1:], F32) for p in partial4s],
    )(place, *partial4s, *landed3s)


def _adamw_math(w, g, m, v):
    m2 = B1 * m + (1.0 - B1) * g
    v2 = B2 * v + (1.0 - B2) * (g * g)
    m_hat = m2 / (1.0 - B1 ** STEP)
    v_hat = v2 / (1.0 - B2 ** STEP)
    return -LR * (m_hat / (jnp.sqrt(v_hat) + AEPS) + WD * w), m2, v2


def adamw_halves(place, w, mine, theirs, m, v, name):
    r, ccols = w.shape
    hr = r // 2
    tr = _row_tile(hr)
    nt = hr // tr

    def body(place_ref, w_ref, a_ref, b_ref, m_ref, v_ref, g_out, d_out, m_out, v_out):
        g = jnp.where(pl.program_id(0) == place_ref[0], a_ref[...], b_ref[...])
        d, m2, v2 = _adamw_math(w_ref[...], g, m_ref[...], v_ref[...])
        g_out[...] = g
        d_out[...] = d
        m_out[...] = m2
        v_out[...] = v2

    full = pl.BlockSpec((tr, ccols), lambda h, i, s: (h * nt + i, 0))
    part = pl.BlockSpec((tr, ccols), lambda h, i, s: (i, 0))
    return pl.pallas_call(
        body, name=name,
        grid_spec=pltpu.PrefetchScalarGridSpec(
            num_scalar_prefetch=1, grid=(2, nt), in_specs=[full, part, part, full, full], out_specs=[full] * 4),
        out_shape=[jax.ShapeDtypeStruct((r, ccols), F32)] * 4,
    )(place, w, mine, theirs, m, v)


def adamw_group(place, halved, plain, hosted, name):
    rows = halved[0][0].shape[0]
    tr = 128
    nt = rows // 2 // tr
    nh, npl = len(halved), len(plain)

    def body(place_ref, *refs):
        own_in, h_in, own_out, h_out, _, h_sems = hosted.split(refs, 5 * nh + 4 * npl, 4 * nh + 3 * npl)
        half = pl.program_id(0)
        grid_step = half * nt + pl.program_id(1)

        @pl.when(grid_step == 0)
        def _():
            hosted.start(h_in, h_out, h_sems)

        for i in range(nh):
            w_ref, a_ref, b_ref, m_ref, v_ref = own_in[5 * i:5 * i + 5]
            g = jnp.where(half == place_ref[0], a_ref[...], b_ref[...])
            res = (g,) + _adamw_math(w_ref[...], g, m_ref[...], v_ref[...])
            for o_ref, r in zip(own_out[4 * i:4 * i + 4], res):
                o_ref[...] = r
        for i in range(npl):
            w_ref, g_ref, m_ref, v_ref = own_in[5 * nh + 4 * i:5 * nh + 4 * i + 4]
            res = _adamw_math(w_ref[...], g_ref[...], m_ref[...], v_ref[...])
            for o_ref, r in zip(own_out[4 * nh + 3 * i:4 * nh + 3 * i + 3], res):
                o_ref[...] = r

        @pl.when(grid_step == 2 * nt - 1)
        def _():
            hosted.finish(h_in, h_out, h_sems)

    def full(cols):
        return pl.BlockSpec((tr, cols), lambda h, i, s: (h * nt + i, 0))

    def part(cols):
        return pl.BlockSpec((tr, cols), lambda h, i, s: (i, 0))

    in_specs, out_specs, out_shape, args = [], [], [], []
    for w, a, b, m, v in halved:
        cols = w.shape[1]
        in_specs += [full(cols), part(cols), part(cols), full(cols), full(cols)]
        out_specs += [full(cols)] * 4
        out_shape += [jax.ShapeDtypeStruct(w.shape, F32)] * 4
        args += [w, a, b, m, v]
    for w, g, m, v in plain:
        cols = w.shape[1]
        in_specs += [full(cols)] * 4
        out_specs += [full(cols)] * 3
        out_shape += [jax.ShapeDtypeStruct(w.shape, F32)] * 3
        args += [w, g, m, v]
    h_in_specs, h_out_specs = hosted.specs()
    return pl.pallas_call(
        body, name=name,
        grid_spec=pltpu.PrefetchScalarGridSpec(
            num_scalar_prefetch=1, grid=(2, nt), in_specs=in_specs + h_in_specs, out_specs=out_specs + h_out_specs,
            scratch_shapes=hosted.scratch),
        out_shape=out_shape + hosted.out_shape,
        compiler_params=_cp(("arbitrary", "arbitrary")),
    )(place, *args, *hosted.args)


def _silu(x):
    return x * jax.nn.sigmoid(x)


def prologue(c_rows, c_ctx_row, w_ada, b_shard, rpb_flat, half_w_in, late_shards):
    shape = jax.ShapeDtypeStruct
    n_late = len(late_shards)
    half_shapes = [(w.shape[0] // 2, w.shape[1]) for w in late_shards]
    g_w = gather8([half_w_in], relay_diagonal=True)
    g_c = gather8([shape((8, D), F32)])
    g_m = chips3([shape((4, 8, 1536), F32)])

    def body(*refs):
        c_ref, cc_ref, w_ref, b_ref, flat_ref, hw_ref = refs[:6]
        late_refs = refs[6:6 + n_late]
        cin_ref, mg_ref, gw_ref, bias_ref, cos_ref, sin_ref = refs[6 + n_late:12 + n_late]
        rest = refs[12 + n_late:]
        half_refs, (cg_s, ms_s, bias_s, need_s, landed_s) = rest[:n_late], rest[n_late:n_late + 5]
        stage, (load_sem, bias_sem), sems = rest[n_late + 5:2 * n_late + 5], rest[2 * n_late + 5:2 * n_late + 7], \
            rest[2 * n_late + 7:]
        sw, sc, sm = sems[0:3], sems[3:6], sems[6:8]
        px, py, core = _place()
        g_c.start([c_ref], [cg_s], sc)
        g_w.start([hw_ref], [gw_ref], sw)
        loads = [pltpu.make_async_copy(late_refs[a].at[pl.ds(core * half_shapes[a][0], half_shapes[a][0]), :],
                                       stage[a], load_sem.at[a]) for a in range(n_late)]
        for cp in loads:
            cp.start()
        g_c.finish([c_ref], [cg_s], sc)
        cin_ref[...] = jnp.zeros_like(cin_ref)
        for dev in range(8):
            cin_ref[2 * dev:2 * dev + 2, :] = cg_s[dev, 0:2, :]
        cin_ref[16:17, :] = cc_ref[...]
        need_s[...] = jnp.zeros_like(need_s)
        for j in range(4):
            need_s[8 * j:8 * j + 2, :] = cg_s[2 * j + core, 0:2, :]
            need_s[8 * j + 2:8 * j + 3, :] = cc_ref[...]
        ms_s[...] = (_nn(_silu(need_s[...]), w_ref[...]) + b_ref[...]).reshape(4, 8, 1536)
        g_m.start([ms_s], [landed_s], sm)
        for a, cp in enumerate(loads):
            cp.wait()
            half_refs[a][...] = stage[a][...].astype(BF16)
        cos_ref[...], sin_ref[...] = _rope_tables()
        stores = []
        for pair in range(NPAIR):
            if pair >= 2:
                stores[pair - 2].wait()
            _na_bias_pair(flat_ref.at[pair], bias_s.at[pair % 2])
            stores.append(pltpu.make_async_copy(bias_s.at[pair % 2], bias_ref.at[pair], bias_sem.at[pair % 2]))
            stores[pair].start()
        for cp in stores[-2:]:
            cp.wait()
        g_w.finish([hw_ref], [gw_ref], sw)
        g_m.finish([ms_s], [landed_s], sm)
        mg_ref[2 * px + py] = ms_s[2 * px + py]
        for k, (qx, qy) in enumerate([(1 - px, py), (px, 1 - py), (1 - px, 1 - py)]):
            mg_ref[2 * qx + qy] = landed_s[k]

    vmem = pl.BlockSpec(memory_space=pltpu.VMEM)
    hbm = pl.BlockSpec(memory_space=pl.ANY)
    return pl.pallas_call(
        body, name="prologue", in_specs=[vmem, vmem, vmem, vmem, vmem, hbm] + [hbm] * n_late,
        out_specs=[vmem, vmem, hbm, hbm, vmem, vmem] + [vmem] * n_late,
        out_shape=[shape((32, D), F32), shape((4, 8, 1536), F32)] + g_w.out_shape
        + [shape((NPAIR,) + NA_BIAS_SHAPE, F32)] + [shape((SEQ, RD), F32)] * 2 + [shape(s, BF16) for s in half_shapes],
        scratch_shapes=[pltpu.VMEM((8, 8, D), F32), pltpu.VMEM((4, 8, 1536), F32), pltpu.VMEM((2,) + NA_BIAS_SHAPE, F32),
                        pltpu.VMEM((32, D), F32), pltpu.VMEM((3, 8, 1536), F32)]
        + [pltpu.VMEM(s, F32) for s in half_shapes]
        + [pltpu.SemaphoreType.DMA((n_late,)), pltpu.SemaphoreType.DMA((2,))]
        + g_w.scratch + g_c.scratch + g_m.scratch,
        compiler_params=_cp(),
    )(c_rows, c_ctx_row, w_ada, b_shard, rpb_flat, half_w_in, *late_shards)


def ada_grads(cin, gb, gc, w_ada):
    def body(c_ref, gb_ref, gc_ref, w_ref, gw_ref, pc_ref):
        ctx_tot = jnp.sum(gc_ref[...], axis=0, keepdims=True)
        rows = lax.broadcasted_iota(jnp.int32, (16, 512), 0)
        dm = jnp.concatenate([gb_ref[...], jnp.where(rows == 0, ctx_tot, 0.0)], axis=0)
        gw_ref[...] = _tn(_silu(c_ref[...]), dm)
        rows8 = lax.broadcasted_iota(jnp.int32, (8, 512), 0)
        part = _nt(jnp.where(rows8 == 0, ctx_tot, 0.0), w_ref[...])

        @pl.when(pl.program_id(0) == 0)
        def _():
            pc_ref[...] = jnp.zeros_like(pc_ref)

        pc_ref[...] += part

    return pl.pallas_call(
        body, name="ada_grads", grid=(3,),
        in_specs=[pl.BlockSpec((32, D), lambda j: (0, 0)), pl.BlockSpec((16, 512), lambda j: (0, j)),
                  pl.BlockSpec((8, 512), lambda j: (0, j)), pl.BlockSpec((D, 512), lambda j: (0, j))],
        out_specs=[pl.BlockSpec((D, 512), lambda j: (0, j)), pl.BlockSpec((8, D), lambda j: (0, 0))],
        out_shape=[jax.ShapeDtypeStruct((D, 1536), F32), jax.ShapeDtypeStruct((8, D), F32)],
    )(cin, gb, gc, w_ada)


SMALL_SUM_ROWS = 15


def small_update(gsm, gbf, gcf, pcg, params):
    n = len(params)

    def body(*refs):
        gsm_ref, gbf_ref, gcf_ref, pcg_ref = refs[:4]
        wmv, outs, loss_out = refs[4:4 + 3 * n], refs[4 + 3 * n:4 + 7 * n], refs[-1]
        acc = gsm_ref[0]
        for dev in range(1, 8):
            acc = acc + gsm_ref[dev]
        c_ctx = wmv[0][...]
        sg = jax.nn.sigmoid(c_ctx)
        dsilu = pcg_ref[0:1, :] + pcg_ref[2:3, :] + pcg_ref[4:5, :] + pcg_ref[6:7, :]
        lane = lax.broadcasted_iota(jnp.int32, (1, D), 1)
        last = acc[14:15, :]
        grads = [
            dsilu * (sg * (1.0 + c_ctx * (1.0 - sg))),
            jnp.sum(gbf_ref[...], axis=0, keepdims=True) + jnp.sum(gcf_ref[...], axis=0, keepdims=True),
            acc[0:1, :] + acc[1:2, :], acc[2:3, :], acc[3:4, :], acc[4:5, :],
            acc[5:6, 0:512], acc[6:14, :], jnp.where(lane < 8, last, 0.0),
        ]
        loss_out[...] = jnp.broadcast_to(jnp.sum(jnp.where(lane == 8, last, 0.0), axis=1, keepdims=True), (8, 128))
        for i, g in enumerate(grads):
            d, m2, v2 = _adamw_math(wmv[3 * i][...], g, wmv[3 * i + 1][...], wmv[3 * i + 2][...])
            outs[4 * i][...] = g
            outs[4 * i + 1][...] = d
            outs[4 * i + 2][...] = m2
            outs[4 * i + 3][...] = v2

    flat = [a for wmv in params for a in wmv]
    out_shape = [jax.ShapeDtypeStruct(w.shape, F32) for w, _, _ in params for _ in range(4)]
    return pl.pallas_call(
        body, name="small_update", out_shape=out_shape + [jax.ShapeDtypeStruct((8, 128), F32)],
    )(gsm, gbf, gcf, pcg, *flat)


def _pad_row(v, rows):
    flat = v.reshape(-1)
    return jnp.pad(flat, (0, rows * D - flat.shape[0])).reshape(rows, D)


def local_step(x, ctx, tgt, mod3, rope, bias, g_pre_mix, g_post_mix, g_pre_mlp, g_post_mlp, ret_decay, ret_gn,
               w_in_blocks, late_weights, early_grads):
    nb = x.shape[0]
    tokens = nb * SEQ
    cos, sin = rope
    rd = ret_decay.T.reshape(RH, 2, 1)
    gn = ret_gn.reshape(RH, 1, RD)
    h, pret, pna, wperm = premix_proj(x, mod3, g_pre_mix, w_in_blocks, False, "premix_proj", unpack=True)
    hc, pretc, pnac = premix_proj(ctx, mod3, g_pre_mix, wperm, True, "premix_proj_ctx")
    o_all, mixin, gw_out = retention_fwd(pret, pretc, rd, gn, cos, sin, late_weights(0))
    mixin, gw1, gw2 = na_fwd(pna, pnac, bias, mixin, late_weights(1))
    dx_tail, dmix, h2, du, act, dm, dmixin, dmod_t, dg_t, loss_t = tail_fwd_bwd(
        x, mixin, tgt, mod3, g_post_mix, g_pre_mlp, g_post_mlp, gw_out.reshape(D, D), gw1.reshape(4, D, D),
        gw2.reshape(DFF, D))
    dw_out = weight_grad([(mixin.reshape(tokens, D), dmix.reshape(tokens, D))], "grad_w_out", BF16)
    dw1 = weight_grad([(h2.reshape(tokens, D), du.reshape(tokens, DFF))], "grad_w_mlp1", BF16, col_blocks=True)
    dw2 = weight_grad([(act.reshape(tokens, DFF), dm.reshape(tokens, D))], "grad_w_mlp2", BF16)
    dproj, dprojc, drd, dgn, *landed = retention_bwd(pret, pretc, o_all, dmixin, rd, gn, cos, sin,
                                                     early_grads[0](dw_out, dw1, dw2))
    dproj, dprojc, dpat, *early = na_bwd(pna, pnac, bias, dmixin, dproj, dprojc, early_grads[1](landed))
    dw_in = weight_grad([(h.reshape(tokens, D), dproj.reshape(tokens, IN_W)),
                         (hc.reshape(nb * LC, D), dprojc.reshape(nb * LC, IN_W))], "grad_w_in", BF16, tk=512,
                        w_in_blocks=True)
    dmod_c, dg_c, *late = premix_bwd(ctx, mod3, g_pre_mix, wperm, dprojc, None, early_grads[2](dw_in), "premix_bwd_ctx")
    grad_x, dmod_a, dg_a, *late = premix_bwd(x, mod3, g_pre_mix, wperm, dproj, dx_tail, early_grads[3](late),
                                             "premix_bwd")
    dmod = jnp.concatenate([jnp.concatenate([dmod_a[:, 0:2], dmod_t[:, 2:6]], axis=1), dmod_c], axis=0)
    last = jnp.pad(jnp.concatenate([drd[:, :, 0].T.reshape(8), loss_t[0, 0:1]]), (0, D - 9)).reshape(1, D)
    small = jnp.concatenate([dg_a[0:1], dg_c[0:1], dg_t[0:3], _pad_row(dgn, 1), dpat.reshape(8, D), last], axis=0)
    return grad_x, late, early, dmod, small


def kernel(x, c, ctx, c_ctx, w_ada, b_ada, g_pre_mix, g_post_mix, g_pre_mlp, g_post_mlp, w_in, ret_decay, ret_gn, na_rpb, w_out, w_mlp1, w_mlp2, loss_target, m_c_ctx, m_w_ada, m_b_ada, m_g_pre_mix, m_g_post_mix, m_g_pre_mlp, m_g_post_mlp, m_w_in, m_ret_decay, m_ret_gn, m_na_rpb, m_w_out, m_w_mlp1, m_w_mlp2, v_c_ctx, v_w_ada, v_b_ada, v_g_pre_mix, v_g_post_mix, v_g_pre_mlp, v_g_post_mlp, v_w_in, v_ret_decay, v_ret_gn, v_na_rpb, v_w_out, v_w_mlp1, v_w_mlp2):
    px, py, pc = _place()
    chip = 2 * px + py

    half_w_in = lax.dynamic_slice_in_dim(w_in[0], pc * (D // 2), D // 2, 0).astype(BF16)
    cin, mg, gw_in, bias, cos, sin, *late_halves = prologue(
        jnp.pad(c, ((0, 6), (0, 0))), c_ctx[None], w_ada[0], lax.dynamic_slice_in_dim(b_ada, chip * 1536, 1536, 1),
        _rpb_flat(na_rpb[0]), half_w_in, [w_out[0], w_mlp1[0], w_mlp2[0]])
    halves = [half_w_in] + late_halves
    mod3 = mg[:, 0:3].transpose(1, 0, 2).reshape(3, 6, D)

    place = jnp.stack([pc, chip]).astype(jnp.int32)

    early_names = ["w_out", "w_mlp1", "w_mlp2"]
    early_g8, early_partial = [], []

    def early_a(dw_out, dw1, dw2):
        early_g8[:] = [dw_out.reshape(8, 128, D), dw1.reshape(8, 512, D), dw2.reshape(8, 512, D)]
        return siblings4(early_g8)

    def early_b(landed):
        early_partial[:] = chip_partial(place, early_g8, landed, "rs_chip_sum_early")
        return chips3(early_partial)

    late_partial = []

    late_g8 = []

    def late_c(dw_in):
        late_g8[:] = [dw_in.reshape(8, 512, 896)]
        return siblings4(late_g8)

    def late_d(landed):
        late_partial[:] = chip_partial(place, late_g8, landed, "rs_chip_sum_w_in")
        return chips3(late_partial)

    grad_x, (landed3_in,), early_landed, dmod, small = local_step(
        x, ctx, loss_target, mod3, (cos, sin), bias, g_pre_mix, g_post_mix, g_pre_mlp, g_post_mlp, ret_decay[0], ret_gn,
        gw_in.reshape(4, D, 896), lambda k: gather8(halves[1:2] if k == 0 else halves[2:4]), (early_a, early_b, late_c, late_d))
    early_mine = shard_sum(place, early_partial, early_landed, "rs_shard_sum_early")

    pay = jnp.concatenate([dmod.reshape(18, D), small, jnp.zeros((40 - 18 - SMALL_SUM_ROWS, D), F32)], axis=0)
    *early_theirs, gs = run_hosted(both(siblings(early_mine), gather8([pay])), "rs_halves_early_gather_small")
    gbf = gs[:, 0:12].reshape(16, 6 * D)
    gcf = gs[:, 12:18].reshape(8, 6 * D)
    gw_ada, pc_part = ada_grads(cin, lax.dynamic_slice_in_dim(gbf, chip * 1536, 1536, 1),
                                lax.dynamic_slice_in_dim(gcf, chip * 1536, 1536, 1), w_ada[0])
    (mine_in,) = shard_sum(place, late_partial, [landed3_in], "rs_shard_sum_w_in")
    theirs_in, pcg = run_hosted(both(siblings([mine_in]), gather8([pc_part])), "rs_halves_w_in_gather_c_ctx")

    grouped = adamw_group(
        place,
        [(w_mlp1[0], early_mine[1], early_theirs[1], m_w_mlp1[0], v_w_mlp1[0]),
         (w_mlp2[0], early_mine[2], early_theirs[2], m_w_mlp2[0], v_w_mlp2[0])],
        [(w_ada[0], gw_ada, m_w_ada[0], v_w_ada[0])], no_exchange(), "adamw_group")
    d_ada, m_ada, v_ada = grouped[8:11]
    big = [
        [r[None] for r in adamw_halves(place, w_in[0], mine_in, theirs_in, m_w_in[0], v_w_in[0], "adamw_w_in")],
        [r[None] for r in adamw_halves(place, w_out[0], early_mine[0], early_theirs[0], m_w_out[0], v_w_out[0],
                                       "adamw_w_out")],
        [r[None] for r in grouped[0:4]], [r[None] for r in grouped[4:8]],
    ]

    def rpb_rows(t):
        return _rpb_flat(t[0]).reshape(8, D)

    def decay_row(t):
        return jnp.pad(t.reshape(1, 8), ((0, 0), (0, D - 8)))

    views = [lambda t: t.reshape(1, D), lambda t: t, lambda t: t, lambda t: t, lambda t: t, lambda t: t, lambda t: t,
             rpb_rows, decay_row]
    back = [lambda t: t.reshape(D), lambda t: t, lambda t: t, lambda t: t, lambda t: t, lambda t: t, lambda t: t,
            lambda t: _rpb_flat_t(t)[None], lambda t: t[:, 0:8].reshape(1, 2, 4)]
    small_w = (c_ctx, b_ada, g_pre_mix, g_post_mix, g_pre_mlp, g_post_mlp, ret_gn, na_rpb, ret_decay)
    small_m = (m_c_ctx, m_b_ada, m_g_pre_mix, m_g_post_mix, m_g_pre_mlp, m_g_post_mlp, m_ret_gn, m_na_rpb, m_ret_decay)
    small_v = (v_c_ctx, v_b_ada, v_g_pre_mix, v_g_post_mix, v_g_pre_mlp, v_g_post_mlp, v_ret_gn, v_na_rpb, v_ret_decay)
    *res, loss8 = small_update(gs[:, 18:18 + SMALL_SUM_ROWS], gbf, gcf, pcg[:, 0],
                               [(f(w), f(m), f(v)) for f, w, m, v in zip(views, small_w, small_m, small_v)])

    def leaves(ada, idx):
        s_c, s_b, s_g1, s_g2, s_g3, s_g4, s_gn, s_rpb, s_rd = [back[i](res[4 * i + idx]) for i in range(9)]
        return [s_c, ada[None], s_b, s_g1, s_g2, s_g3, s_g4, big[0][idx], s_rd, s_gn, s_rpb,
                big[1][idx], big[2][idx], big[3][idx]]

    return (loss8[0, 0], grad_x, *leaves(gw_ada, 0), *leaves(d_ada, 1), *leaves(m_ada, 2), *leaves(v_ada, 3))
```
